```python
import jax, jax.numpy as jnp
from jax import lax
import numpy as np

D_MODEL = 1024
BATCH = 8
SEQ = 8192
DEPTH = 2

D_MIX = D_MODEL
GROUP_WIDTH = D_MIX // 4
POOL_WINDOWS = (2, 4, 8, 16)
POOL_GROUP = GROUP_WIDTH // len(POOL_WINDOWS)
SCONV_K = 3
SSD_D_INNER = GROUP_WIDTH
SSD_HEAD_DIM = 64
SSD_HEADS = SSD_D_INNER // SSD_HEAD_DIM
SSD_GROUPS = 2
SSD_STATE = 128
SSD_CONV_K = 4
SSD_CHUNK = 128
SSD_CONV_DIM = SSD_D_INNER + 2 * SSD_GROUPS * SSD_STATE
S5_WIDTH = GROUP_WIDTH
S5_GROUP = 16
S5_GROUPS = S5_WIDTH // S5_GROUP
S5_STATE = 64
MLP_HIDDEN = 4 * D_MODEL
EPS = 1e-6

IN_SPLITS = (GROUP_WIDTH,
             GROUP_WIDTH, GROUP_WIDTH, GROUP_WIDTH,
             SSD_D_INNER, SSD_CONV_DIM, SSD_HEADS,
             S5_WIDTH)
D_IN_PROJ = sum(IN_SPLITS)
IN_SPLIT_IDX = tuple(int(v) for v in np.cumsum(IN_SPLITS)[:-1])

kernel_name = "hymba_style_pool_conv_ssd_s5_hybrid"

f32 = jnp.float32


def rmsnorm(x, w):
    xf = x.astype(f32)
    y = xf * lax.rsqrt(jnp.mean(xf * xf, axis=-1, keepdims=True) + EPS)
    return (y * w.astype(f32)).astype(x.dtype)


def causal_dwconv(x, w):
    k, ch = w.shape
    return lax.conv_general_dilated(
        x, w[:, None, :].astype(x.dtype), window_strides=(1,),
        padding=[(k - 1, 0)], dimension_numbers=("NWC", "WIO", "NWC"),
        feature_group_count=ch)


def pool_mixer(v, w_grp, scale):
    b, s, _ = v.shape
    vf = v.astype(f32)
    cs = jnp.pad(jnp.cumsum(vf, axis=1), ((0, 0), (1, 0), (0, 0)))
    t = jnp.arange(s)
    outs = []
    for g, win in enumerate(POOL_WINDOWS):
        csg = cs[..., g * POOL_GROUP:(g + 1) * POOL_GROUP]
        start = jnp.maximum(t + 1 - win, 0)
        wsum = csg[:, 1:] - csg[:, start]
        count = jnp.minimum(t + 1, win).astype(f32)[None, :, None]
        outs.append(wsum / count - vf[..., g * POOL_GROUP:(g + 1) * POOL_GROUP])
    p = jnp.stack(outs, axis=2)
    y = jnp.einsum('bsgc,gcd->bsgd', p, w_grp.astype(f32)).reshape(b, s, GROUP_WIDTH)
    return y * scale.astype(f32)


def short_conv_mixer(gate_b, gate_c, h, w):
    return gate_b * causal_dwconv(gate_c * h, w)


def ssd_mixer(z, xbc, dt_raw, conv_w, conv_b, dt_bias, a_log, d_skip):
    b, s, _ = z.shape
    nc = s // SSD_CHUNK
    rep = SSD_HEADS // SSD_GROUPS
    xbc = jax.nn.silu((causal_dwconv(xbc, conv_w) + conv_b).astype(f32))
    xs, bm, cm = jnp.split(xbc, [SSD_D_INNER, SSD_D_INNER + SSD_GROUPS * SSD_STATE], axis=-1)
    xs = xs.reshape(b, nc, SSD_CHUNK, SSD_HEADS, SSD_HEAD_DIM)
    bm = jnp.repeat(bm.reshape(b, nc, SSD_CHUNK, SSD_GROUPS, SSD_STATE), rep, axis=3)
    cm = jnp.repeat(cm.reshape(b, nc, SSD_CHUNK, SSD_GROUPS, SSD_STATE), rep, axis=3)
    dt = jax.nn.softplus(dt_raw.astype(f32) + dt_bias.astype(f32))
    dt = dt.reshape(b, nc, SSD_CHUNK, SSD_HEADS)
    a = -jnp.exp(a_log.astype(f32))
    a_dt = (dt * a).transpose(0, 3, 1, 2)
    x_dt = xs * dt[..., None]
    a_cs = jnp.cumsum(a_dt, axis=-1)
    diff = a_cs[..., :, None] - a_cs[..., None, :]
    causal = jnp.tril(jnp.ones((SSD_CHUNK, SSD_CHUNK), dtype=bool))
    decay = jnp.exp(jnp.where(causal, diff, -jnp.inf))
    scores = jnp.einsum('bclhn,bcshn->bhcls', cm, bm) * decay
    y_diag = jnp.einsum('bhcls,bcshp->bclhp', scores, x_dt)
    decay_to_end = jnp.exp(a_cs[..., -1:] - a_cs)
    states = jnp.einsum('bclhn,bhcl,bclhp->bchpn', bm, decay_to_end, x_dt)
    chunk_decay = jnp.exp(a_cs[..., -1])

    def step(carry, inp):
        st, dec = inp
        return carry * dec[..., None, None] + st, carry

    init = jnp.zeros((b, SSD_HEADS, SSD_HEAD_DIM, SSD_STATE), f32)
    _, prev = lax.scan(step, init, (states.transpose(1, 0, 2, 3, 4), chunk_decay.transpose(2, 0, 1)))
    prev = prev.transpose(1, 0, 2, 3, 4)
    y_off = jnp.einsum('bclhn,bchpn,bhcl->bclhp', cm, prev, jnp.exp(a_cs))
    y = y_diag + y_off + xs * d_skip.astype(f32)[:, None]
    y = y.reshape(b, s, SSD_D_INNER)
    return y * jax.nn.silu(z.astype(f32))


def _complex_affine_combine(e1, e2):
    a1r, a1i, b1r, b1i = e1
    a2r, a2i, b2r, b2i = e2
    return (a2r * a1r - a2i * a1i,
            a2r * a1i + a2i * a1r,
            a2r * b1r - a2i * b1i + b2r,
            a2r * b1i + a2i * b1r + b2i)


def s5_mixer(u, a_re, a_im, log_step, b_re, b_im, c_re, c_im, d_skip, glu_w, glu_b):
    bsz, s, _ = u.shape
    uf = u.astype(f32)
    ug = uf.reshape(bsz, s, S5_GROUPS, S5_GROUP)
    a_re = a_re.astype(f32); a_im = a_im.astype(f32)
    step = jnp.exp(log_step.astype(f32))[:, None]
    mag = jnp.exp(a_re * step)
    lam_re = mag * jnp.cos(a_im * step)
    lam_im = mag * jnp.sin(a_im * step)
    den = a_re * a_re + a_im * a_im
    nr = lam_re - 1.0
    f_re = (nr * a_re + lam_im * a_im) / den
    f_im = (lam_im * a_re - nr * a_im) / den
    b_re = b_re.astype(f32); b_im = b_im.astype(f32)
    bb_re = f_re[..., None] * b_re - f_im[..., None] * b_im
    bb_im = f_re[..., None] * b_im + f_im[..., None] * b_re
    bu_re = jnp.einsum('bsgh,gph->bsgp', ug, bb_re)
    bu_im = jnp.einsum('bsgh,gph->bsgp', ug, bb_im)
    lr = jnp.broadcast_to(lam_re, bu_re.shape)
    li = jnp.broadcast_to(lam_im, bu_re.shape)
    _, _, st_re, st_im = lax.associative_scan(_complex_affine_combine, (lr, li, bu_re, bu_im), axis=1)
    y = (jnp.einsum('bsgp,ghp->bsgh', st_re, c_re.astype(f32))
         - jnp.einsum('bsgp,ghp->bsgh', st_im, c_im.astype(f32)))
    y = y.reshape(bsz, s, S5_WIDTH) + d_skip.astype(f32) * uf
    g = jax.nn.gelu(y)
    return g * jax.nn.sigmoid(g @ glu_w.astype(f32) + glu_b.astype(f32))


def _fwd_setup_inputs(seed: int = 0) -> dict:
    key = jax.random.key(seed)
    ks = jax.random.split(key, 32)
    L, D = DEPTH, D_MODEL
    nrm = lambda k, shape, sc: jax.random.normal(k, shape, f32) * sc
    gain = lambda k, shape: 1.0 + 0.01 * jax.random.normal(k, shape, f32)
    ssd_dt = jnp.exp(jax.random.uniform(ks[10], (L, SSD_HEADS), f32, np.log(1e-3), np.log(1e-1)))
    s5_a_im = (jnp.pi * jnp.arange(S5_STATE, dtype=f32))[None, None, :] + 0.01 * jax.random.normal(ks[14], (L, S5_GROUPS, S5_STATE), f32)
    return {
        "x": jax.random.normal(ks[0], (BATCH, SEQ, D), f32),
        "c": jax.random.normal(ks[1], (BATCH, D), f32),
        "norm_mix_w": gain(ks[2], (L, D)),
        "norm_mlp_w": gain(ks[3], (L, D)),
        "ada_w": nrm(ks[4], (L, D, 6 * D), 0.5 * D ** -0.5),
        "ada_b": nrm(ks[5], (L, 6 * D), 0.01),
        "w_in": nrm(ks[6], (L, D, D_IN_PROJ), D ** -0.5),
        "pool_w": nrm(ks[7], (L, len(POOL_WINDOWS), POOL_GROUP, POOL_GROUP), POOL_GROUP ** -0.5),
        "pool_scale": 1.0 + 0.1 * jax.random.normal(ks[8], (L, GROUP_WIDTH), f32),
        "sconv_w": nrm(ks[9], (L, SCONV_K, GROUP_WIDTH), SCONV_K ** -0.5),
        "ssd_conv_w": nrm(ks[11], (L, SSD_CONV_K, SSD_CONV_DIM), SSD_CONV_K ** -0.5),
        "ssd_conv_b": nrm(ks[12], (L, SSD_CONV_DIM), 0.01),
        "ssd_dt_bias": ssd_dt + jnp.log(-jnp.expm1(-ssd_dt)),
        "ssd_a_log": jnp.log(jax.random.uniform(ks[13], (L, SSD_HEADS), f32, 1.0, 16.0)),
        "ssd_d": gain(ks[15], (L, SSD_HEADS)),
        "s5_a_re": -0.5 + 0.01 * jax.random.normal(ks[16], (L, S5_GROUPS, S5_STATE), f32),
        "s5_a_im": s5_a_im,
        "s5_log_step": jax.random.uniform(ks[17], (L, S5_GROUPS), f32, np.log(1e-3), np.log(1e-1)),
        "s5_b_re": nrm(ks[18], (L, S5_GROUPS, S5_STATE, S5_GROUP), (2 * S5_GROUP) ** -0.5),
        "s5_b_im": nrm(ks[19], (L, S5_GROUPS, S5_STATE, S5_GROUP), (2 * S5_GROUP) ** -0.5),
        "s5_c_re": nrm(ks[20], (L, S5_GROUPS, S5_GROUP, S5_STATE), S5_STATE ** -0.5),
        "s5_c_im": nrm(ks[21], (L, S5_GROUPS, S5_GROUP, S5_STATE), S5_STATE ** -0.5),
        "s5_d": nrm(ks[22], (L, S5_WIDTH), 1.0),
        "s5_glu_w": nrm(ks[23], (L, S5_WIDTH, S5_WIDTH), S5_WIDTH ** -0.5),
        "s5_glu_b": nrm(ks[24], (L, S5_WIDTH), 0.01),
        "branch_norm_w": gain(ks[25], (L, D_MIX)),
        "w_out": nrm(ks[26], (L, D_MIX, D), D_MIX ** -0.5),
        "mlp_w1": nrm(ks[27], (L, D, MLP_HIDDEN), D ** -0.5),
        "mlp_w2": nrm(ks[28], (L, MLP_HIDDEN, D), MLP_HIDDEN ** -0.5),
        "final_norm_w": gain(ks[29], (D,)),
    }


def _fwd_reference(x, c, norm_mix_w, norm_mlp_w, ada_w, ada_b, w_in, pool_w, pool_scale,
              sconv_w, ssd_conv_w, ssd_conv_b, ssd_dt_bias, ssd_a_log, ssd_d,
              s5_a_re, s5_a_im, s5_log_step, s5_b_re, s5_b_im, s5_c_re, s5_c_im,
              s5_d, s5_glu_w, s5_glu_b, branch_norm_w, w_out, mlp_w1, mlp_w2,
              final_norm_w):
    dtype = x.dtype
    b, s, _ = x.shape
    cond = jax.nn.silu(c)
    h = x
    for l in range(DEPTH):
        mod = (cond @ ada_w[l] + ada_b[l])[:, None, :]
        sh1, sc1, g1, sh2, sc2, g2 = jnp.split(mod, 6, axis=-1)
        u = rmsnorm(h, norm_mix_w[l]) * (1.0 + sc1) + sh1
        proj = u @ w_in[l]
        (p_pool, p_gb, p_gc, p_h, p_z, p_xbc, p_dt, p_s5) = jnp.split(proj, IN_SPLIT_IDX, axis=-1)
        y_a = pool_mixer(p_pool, pool_w[l], pool_scale[l])
        y_b = short_conv_mixer(p_gb, p_gc, p_h, sconv_w[l])
        y_c = ssd_mixer(p_z, p_xbc, p_dt, ssd_conv_w[l], ssd_conv_b[l],
                        ssd_dt_bias[l], ssd_a_log[l], ssd_d[l])
        y_d = s5_mixer(p_s5, s5_a_re[l], s5_a_im[l], s5_log_step[l], s5_b_re[l],
                       s5_b_im[l], s5_c_re[l], s5_c_im[l], s5_d[l], s5_glu_w[l], s5_glu_b[l])
        groups = jnp.stack([y_a.astype(dtype), y_b.astype(dtype),
                            y_c.astype(dtype), y_d.astype(dtype)], axis=2)
        groups = rmsnorm(groups, branch_norm_w[l].reshape(4, GROUP_WIDTH)).reshape(b, s, D_MIX)
        h = h + g1 * (groups @ w_out[l])
        v = rmsnorm(h, norm_mlp_w[l]) * (1.0 + sc2) + sh2
        h = h + g2 * (jnp.square(jax.nn.relu(v @ mlp_w1[l])) @ mlp_w2[l])
    return rmsnorm(h, final_norm_w)


import jax as _jax
import jax.numpy as _jnp

TWIN_FORMAT = 'train_step'
FWD_PARAMS = ['x', 'c', 'norm_mix_w', 'norm_mlp_w', 'ada_w', 'ada_b', 'w_in', 'pool_w', 'pool_scale', 'sconv_w', 'ssd_conv_w', 'ssd_conv_b', 'ssd_dt_bias', 'ssd_a_log', 'ssd_d', 's5_a_re', 's5_a_im', 's5_log_step', 's5_b_re', 's5_b_im', 's5_c_re', 's5_c_im', 's5_d', 's5_glu_w', 's5_glu_b', 'branch_norm_w', 'w_out', 'mlp_w1', 'mlp_w2', 'final_norm_w']
TWIN_WEIGHTS = ['norm_mix_w', 'norm_mlp_w', 'ada_w', 'ada_b', 'w_in', 'pool_w', 'pool_scale', 'sconv_w', 'ssd_conv_w', 'ssd_conv_b', 'ssd_dt_bias', 'ssd_a_log', 'ssd_d', 's5_a_re', 's5_a_im', 's5_log_step', 's5_b_re', 's5_b_im', 's5_c_re', 's5_c_im', 's5_d', 's5_glu_w', 's5_glu_b', 'branch_norm_w', 'w_out', 'mlp_w1', 'mlp_w2', 'final_norm_w']
TWIN_DIFF_INPUT = 'x'
TWIN_INPUTS = ['x', 'c', 'norm_mix_w', 'norm_mlp_w', 'ada_w', 'ada_b', 'w_in', 'pool_w', 'pool_scale', 'sconv_w', 'ssd_conv_w', 'ssd_conv_b', 'ssd_dt_bias', 'ssd_a_log', 'ssd_d', 's5_a_re', 's5_a_im', 's5_log_step', 's5_b_re', 's5_b_im', 's5_c_re', 's5_c_im', 's5_d', 's5_glu_w', 's5_glu_b', 'branch_norm_w', 'w_out', 'mlp_w1', 'mlp_w2', 'final_norm_w', 'loss_target', 'm_norm_mix_w', 'm_norm_mlp_w', 'm_ada_w', 'm_ada_b', 'm_w_in', 'm_pool_w', 'm_pool_scale', 'm_sconv_w', 'm_ssd_conv_w', 'm_ssd_conv_b', 'm_ssd_dt_bias', 'm_ssd_a_log', 'm_ssd_d', 'm_s5_a_re', 'm_s5_a_im', 'm_s5_log_step', 'm_s5_b_re', 'm_s5_b_im', 'm_s5_c_re', 'm_s5_c_im', 'm_s5_d', 'm_s5_glu_w', 'm_s5_glu_b', 'm_branch_norm_w', 'm_w_out', 'm_mlp_w1', 'm_mlp_w2', 'm_final_norm_w', 'v_norm_mix_w', 'v_norm_mlp_w', 'v_ada_w', 'v_ada_b', 'v_w_in', 'v_pool_w', 'v_pool_scale', 'v_sconv_w', 'v_ssd_conv_w', 'v_ssd_conv_b', 'v_ssd_dt_bias', 'v_ssd_a_log', 'v_ssd_d', 'v_s5_a_re', 'v_s5_a_im', 'v_s5_log_step', 'v_s5_b_re', 'v_s5_b_im', 'v_s5_c_re', 'v_s5_c_im', 'v_s5_d', 'v_s5_glu_w', 'v_s5_glu_b', 'v_branch_norm_w', 'v_w_out', 'v_mlp_w1', 'v_mlp_w2', 'v_final_norm_w']
TWIN_OUTPUTS = ['loss', 'grad_x', 'grad_norm_mix_w', 'grad_norm_mlp_w', 'grad_ada_w', 'grad_ada_b', 'grad_w_in', 'grad_pool_w', 'grad_pool_scale', 'grad_sconv_w', 'grad_ssd_conv_w', 'grad_ssd_conv_b', 'grad_ssd_dt_bias', 'grad_ssd_a_log', 'grad_ssd_d', 'grad_s5_a_re', 'grad_s5_a_im', 'grad_s5_log_step', 'grad_s5_b_re', 'grad_s5_b_im', 'grad_s5_c_re', 'grad_s5_c_im', 'grad_s5_d', 'grad_s5_glu_w', 'grad_s5_glu_b', 'grad_branch_norm_w', 'grad_w_out', 'grad_mlp_w1', 'grad_mlp_w2', 'grad_final_norm_w', 'delta_norm_mix_w', 'delta_norm_mlp_w', 'delta_ada_w', 'delta_ada_b', 'delta_w_in', 'delta_pool_w', 'delta_pool_scale', 'delta_sconv_w', 'delta_ssd_conv_w', 'delta_ssd_conv_b', 'delta_ssd_dt_bias', 'delta_ssd_a_log', 'delta_ssd_d', 'delta_s5_a_re', 'delta_s5_a_im', 'delta_s5_log_step', 'delta_s5_b_re', 'delta_s5_b_im', 'delta_s5_c_re', 'delta_s5_c_im', 'delta_s5_d', 'delta_s5_glu_w', 'delta_s5_glu_b', 'delta_branch_norm_w', 'delta_w_out', 'delta_mlp_w1', 'delta_mlp_w2', 'delta_final_norm_w', 'new_m_norm_mix_w', 'new_m_norm_mlp_w', 'new_m_ada_w', 'new_m_ada_b', 'new_m_w_in', 'new_m_pool_w', 'new_m_pool_scale', 'new_m_sconv_w', 'new_m_ssd_conv_w', 'new_m_ssd_conv_b', 'new_m_ssd_dt_bias', 'new_m_ssd_a_log', 'new_m_ssd_d', 'new_m_s5_a_re', 'new_m_s5_a_im', 'new_m_s5_log_step', 'new_m_s5_b_re', 'new_m_s5_b_im', 'new_m_s5_c_re', 'new_m_s5_c_im', 'new_m_s5_d', 'new_m_s5_glu_w', 'new_m_s5_glu_b', 'new_m_branch_norm_w', 'new_m_w_out', 'new_m_mlp_w1', 'new_m_mlp_w2', 'new_m_final_norm_w', 'new_v_norm_mix_w', 'new_v_norm_mlp_w', 'new_v_ada_w', 'new_v_ada_b', 'new_v_w_in', 'new_v_pool_w', 'new_v_pool_scale', 'new_v_sconv_w', 'new_v_ssd_conv_w', 'new_v_ssd_conv_b', 'new_v_ssd_dt_bias', 'new_v_ssd_a_log', 'new_v_ssd_d', 'new_v_s5_a_re', 'new_v_s5_a_im', 'new_v_s5_log_step', 'new_v_s5_b_re', 'new_v_s5_b_im', 'new_v_s5_c_re', 'new_v_s5_c_im', 'new_v_s5_d', 'new_v_s5_glu_w', 'new_v_s5_glu_b', 'new_v_branch_norm_w', 'new_v_w_out', 'new_v_mlp_w1', 'new_v_mlp_w2', 'new_v_final_norm_w']
TWIN_LEAF_KINDS = {'loss': 'loss', 'grad_x': 'grad_x', 'grad_norm_mix_w': 'grad_w', 'grad_norm_mlp_w': 'grad_w', 'grad_ada_w': 'grad_w', 'grad_ada_b': 'grad_w', 'grad_w_in': 'grad_w', 'grad_pool_w': 'grad_w', 'grad_pool_scale': 'grad_w', 'grad_sconv_w': 'grad_w', 'grad_ssd_conv_w': 'grad_w', 'grad_ssd_conv_b': 'grad_w', 'grad_ssd_dt_bias': 'grad_w', 'grad_ssd_a_log': 'grad_w', 'grad_ssd_d': 'grad_w', 'grad_s5_a_re': 'grad_w', 'grad_s5_a_im': 'grad_w', 'grad_s5_log_step': 'grad_w', 'grad_s5_b_re': 'grad_w', 'grad_s5_b_im': 'grad_w', 'grad_s5_c_re': 'grad_w', 'grad_s5_c_im': 'grad_w', 'grad_s5_d': 'grad_w', 'grad_s5_glu_w': 'grad_w', 'grad_s5_glu_b': 'grad_w', 'grad_branch_norm_w': 'grad_w', 'grad_w_out': 'grad_w', 'grad_mlp_w1': 'grad_w', 'grad_mlp_w2': 'grad_w', 'grad_final_norm_w': 'grad_w', 'delta_norm_mix_w': 'delta_w', 'delta_norm_mlp_w': 'delta_w', 'delta_ada_w': 'delta_w', 'delta_ada_b': 'delta_w', 'delta_w_in': 'delta_w', 'delta_pool_w': 'delta_w', 'delta_pool_scale': 'delta_w', 'delta_sconv_w': 'delta_w', 'delta_ssd_conv_w': 'delta_w', 'delta_ssd_conv_b': 'delta_w', 'delta_ssd_dt_bias': 'delta_w', 'delta_ssd_a_log': 'delta_w', 'delta_ssd_d': 'delta_w', 'delta_s5_a_re': 'delta_w', 'delta_s5_a_im': 'delta_w', 'delta_s5_log_step': 'delta_w', 'delta_s5_b_re': 'delta_w', 'delta_s5_b_im': 'delta_w', 'delta_s5_c_re': 'delta_w', 'delta_s5_c_im': 'delta_w', 'delta_s5_d': 'delta_w', 'delta_s5_glu_w': 'delta_w', 'delta_s5_glu_b': 'delta_w', 'delta_branch_norm_w': 'delta_w', 'delta_w_out': 'delta_w', 'delta_mlp_w1': 'delta_w', 'delta_mlp_w2': 'delta_w', 'delta_final_norm_w': 'delta_w', 'new_m_norm_mix_w': 'new_m', 'new_m_norm_mlp_w': 'new_m', 'new_m_ada_w': 'new_m', 'new_m_ada_b': 'new_m', 'new_m_w_in': 'new_m', 'new_m_pool_w': 'new_m', 'new_m_pool_scale': 'new_m', 'new_m_sconv_w': 'new_m', 'new_m_ssd_conv_w': 'new_m', 'new_m_ssd_conv_b': 'new_m', 'new_m_ssd_dt_bias': 'new_m', 'new_m_ssd_a_log': 'new_m', 'new_m_ssd_d': 'new_m', 'new_m_s5_a_re': 'new_m', 'new_m_s5_a_im': 'new_m', 'new_m_s5_log_step': 'new_m', 'new_m_s5_b_re': 'new_m', 'new_m_s5_b_im': 'new_m', 'new_m_s5_c_re': 'new_m', 'new_m_s5_c_im': 'new_m', 'new_m_s5_d': 'new_m', 'new_m_s5_glu_w': 'new_m', 'new_m_s5_glu_b': 'new_m', 'new_m_branch_norm_w': 'new_m', 'new_m_w_out': 'new_m', 'new_m_mlp_w1': 'new_m', 'new_m_mlp_w2': 'new_m', 'new_m_final_norm_w': 'new_m', 'new_v_norm_mix_w': 'new_v', 'new_v_norm_mlp_w': 'new_v', 'new_v_ada_w': 'new_v', 'new_v_ada_b': 'new_v', 'new_v_w_in': 'new_v', 'new_v_pool_w': 'new_v', 'new_v_pool_scale': 'new_v', 'new_v_sconv_w': 'new_v', 'new_v_ssd_conv_w': 'new_v', 'new_v_ssd_conv_b': 'new_v', 'new_v_ssd_dt_bias': 'new_v', 'new_v_ssd_a_log': 'new_v', 'new_v_ssd_d': 'new_v', 'new_v_s5_a_re': 'new_v', 'new_v_s5_a_im': 'new_v', 'new_v_s5_log_step': 'new_v', 'new_v_s5_b_re': 'new_v', 'new_v_s5_b_im': 'new_v', 'new_v_s5_c_re': 'new_v', 'new_v_s5_c_im': 'new_v', 'new_v_s5_d': 'new_v', 'new_v_s5_glu_w': 'new_v', 'new_v_s5_glu_b': 'new_v', 'new_v_branch_norm_w': 'new_v', 'new_v_w_out': 'new_v', 'new_v_mlp_w1': 'new_v', 'new_v_mlp_w2': 'new_v', 'new_v_final_norm_w': 'new_v'}


def _forward(args):
    return _fwd_reference(*[args[k] for k in FWD_PARAMS])


def _output_shape():
    def fwd():
        inp = _fwd_setup_inputs(0)
        return _fwd_reference(*[inp[k] for k in FWD_PARAMS])
    out = _jax.eval_shape(fwd)
    return out.shape, out.dtype

N_MICROBATCH = 1
ADAM_LR = 0.001
ADAM_B1 = 0.9
ADAM_B2 = 0.999
ADAM_EPS = 1e-08
ADAM_WD = 0.01
ADAM_STEP = 10
PER_EXAMPLE_BATCH_AXIS = {'x': 0, 'c': 0, 'loss_target': 0}
SHARED_INPUTS = []
_WEIGHT_DTYPES = {'norm_mix_w': _jnp.float32, 'norm_mlp_w': _jnp.float32, 'ada_w': _jnp.float32, 'ada_b': _jnp.float32, 'w_in': _jnp.float32, 'pool_w': _jnp.float32, 'pool_scale': _jnp.float32, 'sconv_w': _jnp.float32, 'ssd_conv_w': _jnp.float32, 'ssd_conv_b': _jnp.float32, 'ssd_dt_bias': _jnp.float32, 'ssd_a_log': _jnp.float32, 'ssd_d': _jnp.float32, 's5_a_re': _jnp.float32, 's5_a_im': _jnp.float32, 's5_log_step': _jnp.float32, 's5_b_re': _jnp.float32, 's5_b_im': _jnp.float32, 's5_c_re': _jnp.float32, 's5_c_im': _jnp.float32, 's5_d': _jnp.float32, 's5_glu_w': _jnp.float32, 's5_glu_b': _jnp.float32, 'branch_norm_w': _jnp.float32, 'w_out': _jnp.float32, 'mlp_w1': _jnp.float32, 'mlp_w2': _jnp.float32, 'final_norm_w': _jnp.float32}
MOMENT_SCALE = {'norm_mix_w': 9.892958e-02, 'norm_mlp_w': 1.049211e-01, 'ada_w': 1.134815e-01, 'ada_b': 1.941519e-01, 'w_in': 6.602352e-02, 'pool_w': 7.186557e-02, 'pool_scale': 6.962486e-02, 'sconv_w': 7.680708e-02, 'ssd_conv_w': 4.617025e-02, 'ssd_conv_b': 5.330350e-02, 'ssd_dt_bias': 9.120788e-02, 'ssd_a_log': 2.556207e-01, 'ssd_d': 3.513355e-01, 's5_a_re': 1.107628e-02, 's5_a_im': 9.511095e-03, 's5_log_step': 6.918892e+00, 's5_b_re': 4.524812e-03, 's5_b_im': 4.907572e-03, 's5_c_re': 6.886493e-03, 's5_c_im': 7.723942e-03, 's5_d': 7.958255e-02, 's5_glu_w': 2.075885e-02, 's5_glu_b': 2.826588e-02, 'branch_norm_w': 7.307725e-02, 'w_out': 7.114409e-02, 'mlp_w1': 5.468236e-02, 'mlp_w2': 9.461389e-02, 'final_norm_w': 6.441890e+01}


def _to_microbatches(a, axis):
    t = _jnp.moveaxis(a, axis, 0)
    t = t.reshape((N_MICROBATCH, t.shape[0] // N_MICROBATCH) + t.shape[1:])
    return _jnp.moveaxis(t, 1, axis + 1)


def setup_inputs(seed: int = 0) -> dict:
    inp = _fwd_setup_inputs(seed)
    key = _jax.random.fold_in(_jax.random.key(seed), 7919)
    shape, _ = _output_shape()
    out = dict(inp)
    out["loss_target"] = _jax.random.normal(_jax.random.fold_in(key, 0), shape, _jnp.float32)
    for i, name in enumerate(TWIN_WEIGHTS):
        w = inp[name].astype(_jnp.float32)
        if MOMENT_SCALE is None:
            s = _jnp.sqrt(_jnp.mean(_jnp.square(w)) + 1e-30)
        else:
            s = MOMENT_SCALE[name]
        km, kv = _jax.random.split(_jax.random.fold_in(key, i + 1))
        out[name] = w
        out["m_" + name] = s * _jax.random.normal(km, w.shape, _jnp.float32)
        out["v_" + name] = (s * s) * _jax.random.uniform(kv, w.shape, _jnp.float32, 0.5, 1.5)
    if N_MICROBATCH > 1:
        for name, axis in PER_EXAMPLE_BATCH_AXIS.items():
            out[name] = _to_microbatches(out[name], axis)
    return {'x': out['x'], 'c': out['c'], 'norm_mix_w': out['norm_mix_w'], 'norm_mlp_w': out['norm_mlp_w'], 'ada_w': out['ada_w'], 'ada_b': out['ada_b'], 'w_in': out['w_in'], 'pool_w': out['pool_w'], 'pool_scale': out['pool_scale'], 'sconv_w': out['sconv_w'], 'ssd_conv_w': out['ssd_conv_w'], 'ssd_conv_b': out['ssd_conv_b'], 'ssd_dt_bias': out['ssd_dt_bias'], 'ssd_a_log': out['ssd_a_log'], 'ssd_d': out['ssd_d'], 's5_a_re': out['s5_a_re'], 's5_a_im': out['s5_a_im'], 's5_log_step': out['s5_log_step'], 's5_b_re': out['s5_b_re'], 's5_b_im': out['s5_b_im'], 's5_c_re': out['s5_c_re'], 's5_c_im': out['s5_c_im'], 's5_d': out['s5_d'], 's5_glu_w': out['s5_glu_w'], 's5_glu_b': out['s5_glu_b'], 'branch_norm_w': out['branch_norm_w'], 'w_out': out['w_out'], 'mlp_w1': out['mlp_w1'], 'mlp_w2': out['mlp_w2'], 'final_norm_w': out['final_norm_w'], 'loss_target': out['loss_target'], 'm_norm_mix_w': out['m_norm_mix_w'], 'm_norm_mlp_w': out['m_norm_mlp_w'], 'm_ada_w': out['m_ada_w'], 'm_ada_b': out['m_ada_b'], 'm_w_in': out['m_w_in'], 'm_pool_w': out['m_pool_w'], 'm_pool_scale': out['m_pool_scale'], 'm_sconv_w': out['m_sconv_w'], 'm_ssd_conv_w': out['m_ssd_conv_w'], 'm_ssd_conv_b': out['m_ssd_conv_b'], 'm_ssd_dt_bias': out['m_ssd_dt_bias'], 'm_ssd_a_log': out['m_ssd_a_log'], 'm_ssd_d': out['m_ssd_d'], 'm_s5_a_re': out['m_s5_a_re'], 'm_s5_a_im': out['m_s5_a_im'], 'm_s5_log_step': out['m_s5_log_step'], 'm_s5_b_re': out['m_s5_b_re'], 'm_s5_b_im': out['m_s5_b_im'], 'm_s5_c_re': out['m_s5_c_re'], 'm_s5_c_im': out['m_s5_c_im'], 'm_s5_d': out['m_s5_d'], 'm_s5_glu_w': out['m_s5_glu_w'], 'm_s5_glu_b': out['m_s5_glu_b'], 'm_branch_norm_w': out['m_branch_norm_w'], 'm_w_out': out['m_w_out'], 'm_mlp_w1': out['m_mlp_w1'], 'm_mlp_w2': out['m_mlp_w2'], 'm_final_norm_w': out['m_final_norm_w'], 'v_norm_mix_w': out['v_norm_mix_w'], 'v_norm_mlp_w': out['v_norm_mlp_w'], 'v_ada_w': out['v_ada_w'], 'v_ada_b': out['v_ada_b'], 'v_w_in': out['v_w_in'], 'v_pool_w': out['v_pool_w'], 'v_pool_scale': out['v_pool_scale'], 'v_sconv_w': out['v_sconv_w'], 'v_ssd_conv_w': out['v_ssd_conv_w'], 'v_ssd_conv_b': out['v_ssd_conv_b'], 'v_ssd_dt_bias': out['v_ssd_dt_bias'], 'v_ssd_a_log': out['v_ssd_a_log'], 'v_ssd_d': out['v_ssd_d'], 'v_s5_a_re': out['v_s5_a_re'], 'v_s5_a_im': out['v_s5_a_im'], 'v_s5_log_step': out['v_s5_log_step'], 'v_s5_b_re': out['v_s5_b_re'], 'v_s5_b_im': out['v_s5_b_im'], 'v_s5_c_re': out['v_s5_c_re'], 'v_s5_c_im': out['v_s5_c_im'], 'v_s5_d': out['v_s5_d'], 'v_s5_glu_w': out['v_s5_glu_w'], 'v_s5_glu_b': out['v_s5_glu_b'], 'v_branch_norm_w': out['v_branch_norm_w'], 'v_w_out': out['v_w_out'], 'v_mlp_w1': out['v_mlp_w1'], 'v_mlp_w2': out['v_mlp_w2'], 'v_final_norm_w': out['v_final_norm_w']}


def _loss(weights, diff, rest, loss_target):
    with _jax.named_scope("forward"):
        args = {**rest, TWIN_DIFF_INPUT: diff, **{k: w.astype(_WEIGHT_DTYPES[k]) for k, w in weights.items()}}
        y = _forward(args)
    with _jax.named_scope("loss_head"):
        err = _jnp.square(y.astype(_jnp.float32) - loss_target)
        return 0.5 * _jnp.sum(_jnp.mean(err, axis=-1)) if err.ndim else 0.5 * err


def _adamw(w, g, m, v):
    m = ADAM_B1 * m + (1.0 - ADAM_B1) * g
    v = ADAM_B2 * v + (1.0 - ADAM_B2) * _jnp.square(g)
    m_hat = m / (1.0 - ADAM_B1 ** ADAM_STEP)
    v_hat = v / (1.0 - ADAM_B2 ** ADAM_STEP)
    delta = -ADAM_LR * (m_hat / (_jnp.sqrt(v_hat) + ADAM_EPS) + ADAM_WD * w)
    return delta, m, v


def reference(x, c, norm_mix_w, norm_mlp_w, ada_w, ada_b, w_in, pool_w, pool_scale, sconv_w, ssd_conv_w, ssd_conv_b, ssd_dt_bias, ssd_a_log, ssd_d, s5_a_re, s5_a_im, s5_log_step, s5_b_re, s5_b_im, s5_c_re, s5_c_im, s5_d, s5_glu_w, s5_glu_b, branch_norm_w, w_out, mlp_w1, mlp_w2, final_norm_w, loss_target, m_norm_mix_w, m_norm_mlp_w, m_ada_w, m_ada_b, m_w_in, m_pool_w, m_pool_scale, m_sconv_w, m_ssd_conv_w, m_ssd_conv_b, m_ssd_dt_bias, m_ssd_a_log, m_ssd_d, m_s5_a_re, m_s5_a_im, m_s5_log_step, m_s5_b_re, m_s5_b_im, m_s5_c_re, m_s5_c_im, m_s5_d, m_s5_glu_w, m_s5_glu_b, m_branch_norm_w, m_w_out, m_mlp_w1, m_mlp_w2, m_final_norm_w, v_norm_mix_w, v_norm_mlp_w, v_ada_w, v_ada_b, v_w_in, v_pool_w, v_pool_scale, v_sconv_w, v_ssd_conv_w, v_ssd_conv_b, v_ssd_dt_bias, v_ssd_a_log, v_ssd_d, v_s5_a_re, v_s5_a_im, v_s5_log_step, v_s5_b_re, v_s5_b_im, v_s5_c_re, v_s5_c_im, v_s5_d, v_s5_glu_w, v_s5_glu_b, v_branch_norm_w, v_w_out, v_mlp_w1, v_mlp_w2, v_final_norm_w):
    given = dict(x=x, c=c, norm_mix_w=norm_mix_w, norm_mlp_w=norm_mlp_w, ada_w=ada_w, ada_b=ada_b, w_in=w_in, pool_w=pool_w, pool_scale=pool_scale, sconv_w=sconv_w, ssd_conv_w=ssd_conv_w, ssd_conv_b=ssd_conv_b, ssd_dt_bias=ssd_dt_bias, ssd_a_log=ssd_a_log, ssd_d=ssd_d, s5_a_re=s5_a_re, s5_a_im=s5_a_im, s5_log_step=s5_log_step, s5_b_re=s5_b_re, s5_b_im=s5_b_im, s5_c_re=s5_c_re, s5_c_im=s5_c_im, s5_d=s5_d, s5_glu_w=s5_glu_w, s5_glu_b=s5_glu_b, branch_norm_w=branch_norm_w, w_out=w_out, mlp_w1=mlp_w1, mlp_w2=mlp_w2, final_norm_w=final_norm_w, loss_target=loss_target, m_norm_mix_w=m_norm_mix_w, m_norm_mlp_w=m_norm_mlp_w, m_ada_w=m_ada_w, m_ada_b=m_ada_b, m_w_in=m_w_in, m_pool_w=m_pool_w, m_pool_scale=m_pool_scale, m_sconv_w=m_sconv_w, m_ssd_conv_w=m_ssd_conv_w, m_ssd_conv_b=m_ssd_conv_b, m_ssd_dt_bias=m_ssd_dt_bias, m_ssd_a_log=m_ssd_a_log, m_ssd_d=m_ssd_d, m_s5_a_re=m_s5_a_re, m_s5_a_im=m_s5_a_im, m_s5_log_step=m_s5_log_step, m_s5_b_re=m_s5_b_re, m_s5_b_im=m_s5_b_im, m_s5_c_re=m_s5_c_re, m_s5_c_im=m_s5_c_im, m_s5_d=m_s5_d, m_s5_glu_w=m_s5_glu_w, m_s5_glu_b=m_s5_glu_b, m_branch_norm_w=m_branch_norm_w, m_w_out=m_w_out, m_mlp_w1=m_mlp_w1, m_mlp_w2=m_mlp_w2, m_final_norm_w=m_final_norm_w, v_norm_mix_w=v_norm_mix_w, v_norm_mlp_w=v_norm_mlp_w, v_ada_w=v_ada_w, v_ada_b=v_ada_b, v_w_in=v_w_in, v_pool_w=v_pool_w, v_pool_scale=v_pool_scale, v_sconv_w=v_sconv_w, v_ssd_conv_w=v_ssd_conv_w, v_ssd_conv_b=v_ssd_conv_b, v_ssd_dt_bias=v_ssd_dt_bias, v_ssd_a_log=v_ssd_a_log, v_ssd_d=v_ssd_d, v_s5_a_re=v_s5_a_re, v_s5_a_im=v_s5_a_im, v_s5_log_step=v_s5_log_step, v_s5_b_re=v_s5_b_re, v_s5_b_im=v_s5_b_im, v_s5_c_re=v_s5_c_re, v_s5_c_im=v_s5_c_im, v_s5_d=v_s5_d, v_s5_glu_w=v_s5_glu_w, v_s5_glu_b=v_s5_glu_b, v_branch_norm_w=v_branch_norm_w, v_w_out=v_w_out, v_mlp_w1=v_mlp_w1, v_mlp_w2=v_mlp_w2, v_final_norm_w=v_final_norm_w)
    weights = {n: given[n] for n in TWIN_WEIGHTS}
    shared = {n: given[n] for n in SHARED_INPUTS}
    per_example = {n: given[n] for n in ['x', 'c']}
    grad_fn = _jax.value_and_grad(_loss, argnums=(0, 1))

    def one_microbatch(ex, loss_target):
        ex = dict(ex)
        diff = ex.pop(TWIN_DIFF_INPUT)
        return grad_fn(weights, diff, {**shared, **ex}, loss_target)

    if N_MICROBATCH == 1:
        loss, (grad_w, grad_x) = one_microbatch(per_example, given["loss_target"])
    else:
        def body(carry, xs):
            loss_sum, grad_sum = carry
            l_k, (gw_k, gx_k) = one_microbatch(xs[0], xs[1])
            with _jax.named_scope("update"):
                return (loss_sum + l_k, _jax.tree.map(_jnp.add, grad_sum, gw_k)), gx_k

        init = (_jnp.zeros((), _jnp.float32), _jax.tree.map(_jnp.zeros_like, weights))
        (loss, grad_w), grad_x = _jax.lax.scan(body, init, (per_example, given["loss_target"]))
    with _jax.named_scope("update"):
        delta_w, new_m, new_v = {}, {}, {}
        for n in TWIN_WEIGHTS:
            delta_w[n], new_m[n], new_v[n] = _adamw(weights[n], grad_w[n], given["m_" + n], given["v_" + n])
    return (loss, grad_x, *[grad_w[n] for n in TWIN_WEIGHTS], *[delta_w[n] for n in TWIN_WEIGHTS],
            *[new_m[n] for n in TWIN_WEIGHTS], *[new_v[n] for n in TWIN_WEIGHTS])
```

```python
import functools
import math

import jax
import jax.numpy as jnp
from jax import lax
from jax.experimental import pallas as pl
from jax.experimental.pallas import tpu as pltpu

F32 = jnp.float32
BF16 = jnp.bfloat16
HI = lax.Precision.HIGHEST

D = 1024
GW = 256
HID = 4096
EPS = 1e-6
PW = 2304
DTW = 128
SSD_L = 128
NH, HP, NS = 4, 64, 128
S5_P = 1024
MESH = pl.DeviceIdType.MESH

ADAM_LR, ADAM_B1, ADAM_B2, ADAM_EPS, ADAM_WD, ADAM_STEP = 0.001, 0.9, 0.999, 1e-08, 0.01, 10

NT = (((1,), (1,)), ((), ()))
TN = (((0,), (0,)), ((), ()))

WEIGHTS = ['norm_mix_w', 'norm_mlp_w', 'ada_w', 'ada_b', 'w_in', 'pool_w', 'pool_scale', 'sconv_w', 'ssd_conv_w',
           'ssd_conv_b', 'ssd_dt_bias', 'ssd_a_log', 'ssd_d', 's5_a_re', 's5_a_im', 's5_log_step', 's5_b_re', 's5_b_im',
           's5_c_re', 's5_c_im', 's5_d', 's5_glu_w', 's5_glu_b', 'branch_norm_w', 'w_out', 'mlp_w1', 'mlp_w2',
           'final_norm_w']
BIG = ('ada_w', 'w_in', 'w_out', 'mlp_w1', 'mlp_w2')
SMALL_SHARDED = {'sconv_w': 2, 'ssd_conv_w': 2, 's5_glu_w': 1}


def _cparams(n_axes, vmem_mb=48):
    return pltpu.CompilerParams(dimension_semantics=("arbitrary",) * n_axes, vmem_limit_bytes=vmem_mb * 1024 * 1024)


def _row(n):
    return pl.BlockSpec((1, n), lambda *_: (0, 0))


def _full(shape):
    nd = len(shape)
    return pl.BlockSpec(tuple(shape), lambda *_: (0,) * nd)


def _dot(a, b, dims=None, prec=None):
    if dims is None:
        dims = (((a.ndim - 1,), (0,)), ((), ()))
    return lax.dot_general(a, b, dims, preferred_element_type=F32, precision=prec)


def _bdot(a, b, dims=None):
    return _dot(a.astype(BF16), b.astype(BF16), dims)


def _sig(x):
    return jax.nn.sigmoid(x)


def _silu(x):
    return x * _sig(x)


def _dsilu(x):
    s = _sig(x)
    return s * (1.0 + x * (1.0 - s))


def _softplus(x):
    return jnp.maximum(x, 0.0) + jnp.log(1.0 + jnp.exp(-jnp.abs(x)))


_GK = math.sqrt(2.0 / math.pi)


def _gelu(x):
    return 0.5 * x * (1.0 + jnp.tanh(_GK * (x + 0.044715 * x * x * x)))


def _dgelu(x):
    th = jnp.tanh(_GK * (x + 0.044715 * x * x * x))
    return 0.5 * (1.0 + th) + 0.5 * x * (1.0 - th * th) * _GK * (1.0 + 3.0 * 0.044715 * x * x)


def _colsum(x):
    return jnp.sum(x, axis=0, keepdims=True)


def _rms(x):
    r = lax.rsqrt(jnp.mean(x * x, axis=-1, keepdims=True) + EPS)
    return r, x * r


def _rms_bwd(r, n, dn):
    return r * (dn - n * jnp.mean(dn * n, axis=-1, keepdims=True))


def _roll(x, k):
    n = x.shape[0]
    k = k % n
    return x if k == 0 else pltpu.roll(x, k, axis=0)


def _tblock(t, want=512):
    return min(t, want)


def _peer(mask):
    x, y, c = lax.axis_index("x"), lax.axis_index("y"), lax.axis_index("c")
    return (x ^ ((mask >> 2) & 1), y ^ ((mask >> 1) & 1), c ^ (mask & 1))


def _group_index(masks):
    x, y, c = lax.axis_index("x"), lax.axis_index("y"), lax.axis_index("c")
    full = 0
    for m in masks:
        full |= m
    bits = [b for b in (4, 2, 1) if full & b]

    def idx(px, py, pc):
        v = {4: px, 2: py, 1: pc}
        out = 0
        for b in bits:
            out = out * 2 + v[b]
        return out

    return idx(x, y, c), [idx(*_peer(m)) for m in masks]


def _exchange(arrs, masks, scatter, name):
    n_arr, n_peer, n_grp = len(arrs), len(masks), len(masks) + 1

    def body(*refs):
        ins, outs = refs[:n_arr], refs[n_arr:2 * n_arr]
        send_sems, recv_sems, local_sems = refs[2 * n_arr:]
        me, peer_idx = _group_index(masks)
        copies = []
        for t in range(n_arr):
            src_me = ins[t].at[me] if scatter else ins[t]
            loc = pltpu.make_async_copy(src_me, outs[t].at[me], local_sems.at[t])
            loc.start()
            copies.append(loc)
            for j, m in enumerate(masks):
                src = ins[t].at[peer_idx[j]] if scatter else ins[t]
                cp = pltpu.make_async_remote_copy(src_ref=src, dst_ref=outs[t].at[me], send_sem=send_sems.at[t, j],
                                                  recv_sem=recv_sems.at[t, j], device_id=_peer(m), device_id_type=MESH)
                cp.start()
                copies.append(cp)
        for cp in copies:
            cp.wait()

    hbm = pl.BlockSpec(memory_space=pl.ANY)
    out_shape = [jax.ShapeDtypeStruct((n_grp,) + (a.shape[1:] if scatter else a.shape), a.dtype) for a in arrs]
    outs = pl.pallas_call(
        body, name=name, in_specs=[hbm] * n_arr, out_specs=[hbm] * n_arr, out_shape=out_shape,
        scratch_shapes=[pltpu.SemaphoreType.DMA((n_arr, n_peer)), pltpu.SemaphoreType.DMA((n_arr, n_peer)),
                        pltpu.SemaphoreType.DMA((n_arr,))],
    )(*arrs)
    return list(outs)


CHIPS = (4, 2, 6)
EVERYONE = (1, 2, 3, 4, 5, 6, 7)
SIBLING = (1,)


def _sum_lead(a, name):
    n = a.shape[0]
    shape = a.shape[1:]
    cols = shape[-1]
    rows = int(math.prod(shape[:-1])) if len(shape) > 1 else 1
    a3 = a.reshape(n, rows, cols)
    rb = rows
    for cand in (512, 256, 128, 64, 32, 16, 8):
        if rows % cand == 0 and rows > cand:
            rb = cand
            break

    def body(a_ref, o_ref):
        acc = a_ref[0]
        for k in range(1, n):
            acc = acc + a_ref[k]
        o_ref[...] = acc

    out = pl.pallas_call(
        body, name=name, grid=(rows // rb,), in_specs=[pl.BlockSpec((n, rb, cols), lambda i: (0, i, 0))],
        out_specs=pl.BlockSpec((rb, cols), lambda i: (i, 0)), out_shape=jax.ShapeDtypeStruct((rows, cols), a.dtype),
        compiler_params=_cparams(1),
    )(a3)
    return out.reshape(shape)


def _tn_matmul(a, b, name, col_major=False):
    t, k = a.shape
    n = b.shape[1]
    tb = _tblock(t, 1024)
    kb = min(k, 1024)
    nb = min(n, 1024)
    grid = (k // kb, n // nb, t // tb)

    def body(a_ref, b_ref, o_ref):
        @pl.when(pl.program_id(2) == 0)
        def _():
            o_ref[...] = jnp.zeros_like(o_ref)

        acc = _bdot(a_ref[...], b_ref[...], TN)
        if col_major:
            o_ref[0] += acc
        else:
            o_ref[...] += acc

    if col_major:
        out_spec = pl.BlockSpec((1, kb, nb), lambda ki, ni, ti: (ni, ki, 0))
        out_shape = jax.ShapeDtypeStruct((n // nb, k, nb), F32)
    else:
        out_spec = pl.BlockSpec((kb, nb), lambda ki, ni, ti: (ki, ni))
        out_shape = jax.ShapeDtypeStruct((k, n), F32)
    return pl.pallas_call(
        body, name=name, grid=grid,
        in_specs=[pl.BlockSpec((tb, kb), lambda ki, ni, ti: (ti, ki)), pl.BlockSpec((tb, nb), lambda ki, ni, ti: (ti, ni))],
        out_specs=out_spec, out_shape=out_shape, compiler_params=_cparams(3),
    )(a, b)


def _adamw(w, g, m, v, name):
    shape = w.shape
    cols = shape[-1]
    rows = int(math.prod(shape[:-1]))
    rb = rows
    for cand in (256, 128, 64, 32, 16, 8):
        if rows % cand == 0 and rows > cand:
            rb = cand
            break
    bc1 = 1.0 - ADAM_B1 ** ADAM_STEP
    bc2 = 1.0 - ADAM_B2 ** ADAM_STEP

    def body(w_ref, g_ref, m_ref, v_ref, d_ref, nm_ref, nv_ref):
        gg = g_ref[...]
        m2 = ADAM_B1 * m_ref[...] + (1.0 - ADAM_B1) * gg
        v2 = ADAM_B2 * v_ref[...] + (1.0 - ADAM_B2) * (gg * gg)
        m_hat = m2 / bc1
        v_hat = v2 / bc2
        d_ref[...] = -ADAM_LR * (m_hat / (jnp.sqrt(v_hat) + ADAM_EPS) + ADAM_WD * w_ref[...])
        nm_ref[...] = m2
        nv_ref[...] = v2

    spec = pl.BlockSpec((rb, cols), lambda i: (i, 0))
    sds = jax.ShapeDtypeStruct((rows, cols), F32)
    outs = pl.pallas_call(
        body, name=name, grid=(rows // rb,), in_specs=[spec] * 4, out_specs=[spec] * 3, out_shape=[sds] * 3,
        compiler_params=_cparams(1),
    )(*(z.reshape(rows, cols) for z in (w, g, m, v)))
    return tuple(o.reshape(shape) for o in outs)


def _ada_fwd(c_all, ada_w_sh, ada_b_sh):
    s = ada_w_sh.shape[2]
    sb = 512

    def body(c_ref, w_ref, b_ref, o_ref):
        cond = _silu(c_ref[...])
        o_ref[0] = _bdot(cond, w_ref[0]) + b_ref[0]

    return pl.pallas_call(
        body, name="ada_fwd", grid=(2, s // sb),
        in_specs=[_full((8, D)), pl.BlockSpec((1, D, sb), lambda l, j: (l, 0, j)), pl.BlockSpec((1, 1, sb), lambda l, j: (l, 0, j))],
        out_specs=pl.BlockSpec((1, 8, sb), lambda l, j: (l, 0, j)), out_shape=jax.ShapeDtypeStruct((2, 8, s), F32),
        compiler_params=_cparams(2),
    )(c_all, ada_w_sh, ada_b_sh)


def _ada_bwd(c_all, dmod_sh, dmod_all):
    s = dmod_sh.shape[2]
    sb = 512

    def body(c_ref, d_ref, o_ref):
        cond = _silu(c_ref[...])
        o_ref[0] = _bdot(cond, d_ref[0], TN)

    gw = pl.pallas_call(
        body, name="ada_bwd_w", grid=(2, s // sb),
        in_specs=[_full((8, D)), pl.BlockSpec((1, 8, sb), lambda l, j: (l, 0, j))],
        out_specs=pl.BlockSpec((1, D, sb), lambda l, j: (l, 0, j)), out_shape=jax.ShapeDtypeStruct((2, D, s), F32),
        compiler_params=_cparams(2),
    )(c_all, dmod_sh)

    def body_b(d_ref, o_ref):
        acc = d_ref[0, 0:1, :]
        for k in range(1, 8):
            acc = acc + d_ref[0, k:k + 1, :]
        o_ref[0] = acc

    gb = pl.pallas_call(
        body_b, name="ada_bwd_b", grid=(2,), in_specs=[pl.BlockSpec((1, 8, 6 * D), lambda l: (l, 0, 0))],
        out_specs=pl.BlockSpec((1, 1, 6 * D), lambda l: (l, 0, 0)), out_shape=jax.ShapeDtypeStruct((2, 1, 6 * D), F32),
        compiler_params=_cparams(1),
    )(dmod_all)
    return gw, gb.reshape(2, 6 * D)


def _f_in(h, nw, sc, sh, w_main, w_dt):
    t = h.shape[0]
    tb = _tblock(t)

    def body(h_ref, nw_ref, sc_ref, sh_ref, w_ref, wd_ref, p_ref, dt_ref, u_ref):
        _, n = _rms(h_ref[...])
        u = ((n * nw_ref[...]) * (1.0 + sc_ref[...]) + sh_ref[...]).astype(BF16)
        u_ref[...] = u
        p_ref[...] = _dot(u, w_ref[...])
        dt_ref[...] = _dot(u, wd_ref[...])

    return pl.pallas_call(
        body, name="f_in", grid=(t // tb,),
        in_specs=[pl.BlockSpec((tb, D), lambda i: (i, 0)), _row(D), _row(D), _row(D), _full((D, PW)), _full((D, DTW))],
        out_specs=[pl.BlockSpec((tb, PW), lambda i: (i, 0)), pl.BlockSpec((tb, DTW), lambda i: (i, 0)),
                   pl.BlockSpec((tb, D), lambda i: (i, 0))],
        out_shape=[jax.ShapeDtypeStruct((t, PW), F32), jax.ShapeDtypeStruct((t, DTW), F32), jax.ShapeDtypeStruct((t, D), BF16)],
        compiler_params=_cparams(1),
    )(h, nw, sc, sh, w_main, w_dt)


def _b_in_du(dab, dz, dxbc, ds5, ddt, w_main, w_dt):
    t = dab.shape[0]
    tb = _tblock(t)

    def body(a_ref, z_ref, x_ref, s_ref, d_ref, w_ref, wd_ref, o_ref):
        acc = _bdot(a_ref[...], w_ref[:, 0:1024], NT)
        acc += _bdot(z_ref[...], w_ref[:, 1024:1280], NT)
        acc += _bdot(s_ref[...], w_ref[:, 1280:1536], NT)
        acc += _bdot(x_ref[...], w_ref[:, 1536:2304], NT)
        acc += _bdot(d_ref[...], wd_ref[...], NT)
        o_ref[...] = acc

    blk = lambda n: pl.BlockSpec((tb, n), lambda i: (i, 0))
    return pl.pallas_call(
        body, name="b_in_du", grid=(t // tb,),
        in_specs=[blk(1024), blk(256), blk(768), blk(256), blk(DTW), _full((D, PW)), _full((D, DTW))],
        out_specs=blk(D), out_shape=jax.ShapeDtypeStruct((t, D), F32), compiler_params=_cparams(1),
    )(dab, dz, dxbc, ds5, ddt, w_main, w_dt)


def _b_normmod(du, x, dres, gated, nw, sc, name):
    t = x.shape[0]
    tb = _tblock(t)

    def body(du_ref, x_ref, dr_ref, g_ref, nw_ref, sc_ref, dx_ref, dsc_ref, dsh_ref, dnw_ref, dg_ref):
        @pl.when(pl.program_id(0) == 0)
        def _():
            for r in (dsc_ref, dsh_ref, dnw_ref, dg_ref):
                r[...] = jnp.zeros_like(r)

        du_v = du_ref[...]
        r, n = _rms(x_ref[...])
        nwv = nw_ref[...]
        scale = 1.0 + sc_ref[...]
        dsc_ref[...] += _colsum(du_v * (n * nwv))
        dsh_ref[...] += _colsum(du_v)
        dnw_ref[...] += _colsum(du_v * scale * n)
        dres_v = dr_ref[...]
        dg_ref[...] += _colsum(dres_v * g_ref[...])
        dx_ref[...] = dres_v + _rms_bwd(r, n, du_v * scale * nwv)

    blk = pl.BlockSpec((tb, D), lambda i: (i, 0))
    row = jax.ShapeDtypeStruct((1, D), F32)
    return pl.pallas_call(
        body, name=name, grid=(t // tb,), in_specs=[blk, blk, blk, blk, _row(D), _row(D)],
        out_specs=[blk, _row(D), _row(D), _row(D), _row(D)], out_shape=[jax.ShapeDtypeStruct((t, D), F32), row, row, row, row],
        compiler_params=_cparams(1),
    )(du, x, dres, gated, nw, sc)


HALO = 16


def _lane_group(shape):
    return lax.broadcasted_iota(jnp.int32, shape, 1) // 64


def _window_select(g, s2, s4, s8, s16):
    return jnp.where(g == 0, s2, jnp.where(g == 1, s4, jnp.where(g == 2, s8, s16)))


def _pool_count(t0, rows):
    g = _lane_group((rows, GW))
    win = _window_select(g, 2, 4, 8, 16)
    tt = t0 + lax.broadcasted_iota(jnp.int32, (rows, GW), 0)
    return jnp.minimum(tt + 1, win).astype(F32)


def _pool_p(v_ext, t0, tb):
    s2 = v_ext + _roll(v_ext, 1)
    s4 = s2 + _roll(s2, 2)
    s8 = s4 + _roll(s4, 4)
    s16 = s8 + _roll(s8, 8)
    ws = _window_select(_lane_group(v_ext.shape), s2, s4, s8, s16)[HALO:]
    return ws / _pool_count(t0, tb) - v_ext[HALO:]


def _sconv(q_ext, w):
    return (_roll(q_ext, 2) * w[0:1] + _roll(q_ext, 1) * w[1:2] + q_ext * w[2:3])[HALO:]


def _halo_specs(t, tb, cols, col_block):
    per = tb // HALO
    last = t // HALO - 1
    prev = pl.BlockSpec((HALO, cols), lambda i: (jnp.maximum(i * per - 1, 0), col_block))
    nxt = pl.BlockSpec((HALO, cols), lambda i: (jnp.minimum((i + 1) * per, last), col_block))
    return prev, nxt


def _f_ab(proj, pool_mat, pool_scale, sconv_w):
    t = proj.shape[0]
    tb = _tblock(t)
    prev, _ = _halo_specs(t, tb, 1024, 0)

    def body(p_ref, h_ref, pm_ref, ps_ref, sw_ref, ya_ref, yb_ref):
        i = pl.program_id(0)
        halo = jnp.where(i > 0, h_ref[...], 0.0)
        ext = jnp.concatenate([halo, p_ref[...]], axis=0)
        p = _pool_p(ext[:, 0:256], i * tb, tb)
        ya_ref[...] = _dot(p, pm_ref[...]) * ps_ref[...]
        q_ext = ext[:, 512:768] * ext[:, 768:1024]
        yb_ref[...] = p_ref[:, 256:512] * _sconv(q_ext, sw_ref[...])

    blk = pl.BlockSpec((tb, GW), lambda i: (i, 0))
    sds = jax.ShapeDtypeStruct((t, GW), F32)
    return pl.pallas_call(
        body, name="f_ab", grid=(t // tb,),
        in_specs=[pl.BlockSpec((tb, 1024), lambda i: (i, 0)), prev, _full((GW, GW)), _row(GW), _full((3, GW))],
        out_specs=[blk, blk], out_shape=[sds, sds], compiler_params=_cparams(1),
    )(proj, proj, pool_mat, pool_scale, sconv_w)


def _b_ab(proj, dya, dyb, pool_mat, pool_scale, sconv_w):
    t = proj.shape[0]
    tb = _tblock(t)
    nb = t // tb
    prev, nxt = _halo_specs(t, tb, 1024, 0)
    _, nxt_g = _halo_specs(t, tb, GW, 0)
    n_ext = tb + HALO

    def body(p_ref, hp_ref, hn_ref, da_ref, dan_ref, db_ref, dbn_ref, pm_ref, ps_ref, sw_ref,
             o_ref, dpm_ref, dps_ref, dsw_ref):
        i = pl.program_id(0)

        @pl.when(i == 0)
        def _():
            for r in (dpm_ref, dps_ref, dsw_ref):
                r[...] = jnp.zeros_like(r)

        last = i == nb - 1
        halo = jnp.where(i > 0, hp_ref[...], 0.0)
        main = p_ref[...]
        ext = jnp.concatenate([halo, main], axis=0)
        scale = ps_ref[...]
        pm = pm_ref[...]
        p = _pool_p(ext[:, 0:256], i * tb, tb)
        da = da_ref[...]
        dps_ref[...] += _colsum(da * _dot(p, pm))
        da_ext = jnp.concatenate([da, jnp.where(last, 0.0, dan_ref[...])], axis=0)
        dys = da_ext * scale
        dpm_ref[...] += _dot(p, dys[:tb], TN)
        dp = _dot(dys, pm, NT)
        dpc = dp / _pool_count(i * tb, n_ext)
        a2 = dpc + _roll(dpc, n_ext - 1)
        a4 = a2 + _roll(a2, n_ext - 2)
        a8 = a4 + _roll(a4, n_ext - 4)
        a16 = a8 + _roll(a8, n_ext - 8)
        o_ref[:, 0:256] = (_window_select(_lane_group(dpc.shape), a2, a4, a8, a16) - dp)[:tb]
        w = sw_ref[...]
        gb, gc, hh = main[:, 256:512], main[:, 512:768], main[:, 768:1024]
        q_ext = ext[:, 512:768] * ext[:, 768:1024]
        db = db_ref[...]
        o_ref[:, 256:512] = db * _sconv(q_ext, w)
        gb_next = hn_ref[:, 256:512]
        dconv = jnp.concatenate([db * gb, jnp.where(last, 0.0, dbn_ref[...] * gb_next)], axis=0)
        dq = (dconv * w[2:3] + _roll(dconv, n_ext - 1) * w[1:2] + _roll(dconv, n_ext - 2) * w[0:1])[:tb]
        o_ref[:, 512:768] = dq * hh
        o_ref[:, 768:1024] = dq * gc
        dc = dconv[:tb]
        dsw_ref[0:1, :] += _colsum(dc * _roll(q_ext, 2)[HALO:])
        dsw_ref[1:2, :] += _colsum(dc * _roll(q_ext, 1)[HALO:])
        dsw_ref[2:3, :] += _colsum(dc * q_ext[HALO:])

    blk = pl.BlockSpec((tb, GW), lambda i: (i, 0))
    return pl.pallas_call(
        body, name="b_ab", grid=(nb,),
        in_specs=[pl.BlockSpec((tb, 1024), lambda i: (i, 0)), prev, nxt, blk, nxt_g, blk, nxt_g,
                  _full((GW, GW)), _row(GW), _full((3, GW))],
        out_specs=[pl.BlockSpec((tb, 1024), lambda i: (i, 0)), _full((GW, GW)), _row(GW), _full((3, GW))],
        out_shape=[jax.ShapeDtypeStruct((t, 1024), F32), jax.ShapeDtypeStruct((GW, GW), F32),
                   jax.ShapeDtypeStruct((1, GW), F32), jax.ShapeDtypeStruct((3, GW), F32)],
        compiler_params=_cparams(1),
    )(proj, proj, proj, dya, dya, dyb, dyb, pool_mat, pool_scale, sconv_w)


CH = 8


def _ssd_conv(x, halo, w, b):
    ext = jnp.concatenate([halo, x], axis=0)
    pre = ext * w[3:4] + _roll(ext, 1) * w[2:3] + _roll(ext, 2) * w[1:2] + _roll(ext, 3) * w[0:1] + b
    return pre[CH:], ext


def _ssd_common(dt_raw, dtb, alog):
    ll = dt_raw.shape[0]
    dtv = _softplus(dt_raw + dtb)
    a_row = -jnp.exp(alog)
    r = lax.broadcasted_iota(jnp.int32, (ll, ll), 0)
    c = lax.broadcasted_iota(jnp.int32, (ll, ll), 1)
    tril = (r >= c).astype(F32)
    cs = _dot(tril, dtv * a_row, prec=HI)
    return dtv, a_row, cs, cs.T, r >= c


def _ssd_head(h, act, dtv, cs, cs_t, causal, gmat):
    g = h // 2
    xs = act[:, HP * h:HP * (h + 1)]
    bm = act[:, 256 + NS * g:256 + NS * (g + 1)]
    cm = act[:, 512 + NS * g:512 + NS * (g + 1)]
    cs_c = cs[:, h:h + 1]
    cs_r = cs_t[h:h + 1, :]
    mdec = jnp.where(causal, jnp.exp(jnp.minimum(cs_c - cs_r, 0.0)), 0.0)
    sc = gmat[g] * mdec
    dt_c = dtv[:, h:h + 1]
    xdt = xs * dt_c
    e = jnp.exp(cs_c)
    cs_last = cs[SSD_L - 1:SSD_L, h:h + 1]
    wdec = jnp.exp(cs_last - cs_c)
    return xs, bm, cm, cs_c, mdec, sc, dt_c, xdt, e, cs_last, wdec


def _f_ssd(proj, dtp, conv_w, conv_b, dt_bias, a_log, d_skip):
    t = proj.shape[0]
    nc = t // SSD_L
    per = SSD_L // CH

    def body(x_ref, hx_ref, dt_ref, z_ref, cw_ref, cb_ref, dtb_ref, al_ref, dk_ref, y_ref, yp_ref, sp_ref, s_ref):
        i = pl.program_id(0)

        @pl.when(i == 0)
        def _():
            s_ref[...] = jnp.zeros_like(s_ref)

        halo = jnp.where(i > 0, hx_ref[...], 0.0)
        pre, _ = _ssd_conv(x_ref[...], halo, cw_ref[...], cb_ref[...])
        act = _silu(pre)
        dtv, _, cs, cs_t, causal = _ssd_common(dt_ref[...], dtb_ref[...], al_ref[...])
        gmat = [_dot(act[:, 512 + NS * g:512 + NS * (g + 1)], act[:, 256 + NS * g:256 + NS * (g + 1)], NT) for g in range(2)]
        for h in range(NH):
            xs, bm, cm, _, _, sc, _, xdt, e, cs_last, wdec = _ssd_head(h, act, dtv, cs, cs_t, causal, gmat)
            prev = s_ref[h]
            sp_ref[0, h] = prev
            y = _dot(sc, xdt) + e * _dot(cm, prev, NT) + xs * dk_ref[0:1, h:h + 1]
            yp_ref[:, HP * h:HP * (h + 1)] = y
            s_ref[h] = prev * jnp.exp(cs_last) + _dot(xdt * wdec, bm, TN)
        y_ref[...] = yp_ref[...] * _silu(z_ref[...])

    blk = pl.BlockSpec((SSD_L, GW), lambda i: (i, 0))
    sds = jax.ShapeDtypeStruct((t, GW), F32)
    return pl.pallas_call(
        body, name="f_ssd", grid=(nc,),
        in_specs=[pl.BlockSpec((SSD_L, 768), lambda i: (i, 2)),
                  pl.BlockSpec((CH, 768), lambda i: (jnp.maximum(i * per - 1, 0), 2)),
                  pl.BlockSpec((SSD_L, DTW), lambda i: (i, 0)),
                  pl.BlockSpec((SSD_L, GW), lambda i: (i, 4)),
                  _full((4, 768)), _row(768), _row(DTW), _row(DTW), _row(DTW)],
        out_specs=[blk, blk, pl.BlockSpec((1, NH, HP, NS), lambda i: (i, 0, 0, 0))],
        out_shape=[sds, sds, jax.ShapeDtypeStruct((nc, NH, HP, NS), F32)],
        scratch_shapes=[pltpu.VMEM((NH, HP, NS), F32)], compiler_params=_cparams(1),
    )(proj, proj, dtp, proj, conv_w, conv_b, dt_bias, a_log, d_skip)


def _b_ssd(proj, dtp, ypre, dyc, sprev, conv_w, conv_b, dt_bias, a_log, d_skip):
    t = proj.shape[0]
    nc = t // SSD_L
    per = SSD_L // CH
    n_ext = SSD_L + CH

    def body(x_ref, hx_ref, dt_ref, z_ref, yp_ref, dy_ref, sp_ref, cw_ref, cb_ref, dtb_ref, al_ref, dk_ref,
             dz_ref, dx_ref, ddt_ref, dcw_ref, dcb_ref, ddtb_ref, dal_ref, ddk_ref, ds_ref, dnext_ref, dact_ref):
        i = pl.program_id(0)

        @pl.when(i == 0)
        def _():
            ds_ref[...] = jnp.zeros_like(ds_ref)
            dnext_ref[...] = jnp.zeros_like(dnext_ref)
            for r in (dcw_ref, dcb_ref, ddtb_ref, dal_ref, ddk_ref):
                r[...] = jnp.zeros_like(r)

        first_chunk = i == nc - 1
        halo = jnp.where(first_chunk, 0.0, hx_ref[...])
        w = cw_ref[...]
        pre, ext = _ssd_conv(x_ref[...], halo, w, cb_ref[...])
        act = _silu(pre)
        dt_raw = dt_ref[...]
        dtv, a_row, cs, cs_t, causal = _ssd_common(dt_raw, dtb_ref[...], al_ref[...])
        gmat = [_dot(act[:, 512 + NS * g:512 + NS * (g + 1)], act[:, 256 + NS * g:256 + NS * (g + 1)], NT) for g in range(2)]
        z = z_ref[...]
        dyc_v = dy_ref[...]
        dz_ref[...] = dyc_v * yp_ref[...] * _dsilu(z)
        dy_all = dyc_v * _silu(z)
        lane = lax.broadcasted_iota(jnp.int32, (SSD_L, DTW), 1)
        rowi = lax.broadcasted_iota(jnp.int32, (SSD_L, 1), 0)
        dcs_mat = jnp.zeros((SSD_L, DTW), F32)
        ddtx_mat = jnp.zeros((SSD_L, DTW), F32)
        ddk_row = jnp.zeros((1, DTW), F32)
        lane1 = lax.broadcasted_iota(jnp.int32, (1, DTW), 1)
        dbm = [None, None]
        dcm = [None, None]
        for h in range(NH):
            g = h // 2
            xs, bm, cm, _, mdec, sc, dt_c, xdt, e, cs_last, wdec = _ssd_head(h, act, dtv, cs, cs_t, causal, gmat)
            dy = dy_all[:, HP * h:HP * (h + 1)]
            prev = sp_ref[0, h]
            ds = ds_ref[h]
            dsc = _dot(dy, xdt, NT)
            q = dsc * sc
            dg = dsc * mdec
            dxdt = _dot(sc, dy, TN)
            dcs = jnp.sum(q, axis=1, keepdims=True) - jnp.sum(q.T, axis=1, keepdims=True)
            dc_h = _dot(dg, bm)
            db_h = _dot(dg, cm, TN)
            cp = _dot(cm, prev, NT)
            dcs += jnp.sum(dy * cp, axis=1, keepdims=True) * e
            ey = e * dy
            dc_h += _dot(ey, prev)
            dprev = _dot(ey, cm, TN)
            elast = jnp.exp(cs_last)
            dprev += ds * elast
            dcs_last = jnp.sum(ds * prev, keepdims=True) * elast
            bds = _dot(bm, ds, NT)
            dxdt += wdec * bds
            db_h += wdec * _dot(xdt, ds)
            dw = jnp.sum(xdt * bds, axis=1, keepdims=True) * wdec
            dcs -= dw
            dcs_last += jnp.sum(dw, keepdims=True)
            dcs += jnp.where(rowi == SSD_L - 1, dcs_last, 0.0)
            ds_ref[h] = dprev
            dact_ref[:, HP * h:HP * (h + 1)] = dxdt * dt_c + dy * dk_ref[0:1, h:h + 1]
            dcs_mat = jnp.where(lane == h, dcs, dcs_mat)
            ddtx_mat = jnp.where(lane == h, jnp.sum(dxdt * xs, axis=1, keepdims=True), ddtx_mat)
            ddk_row = jnp.where(lane1 == h, jnp.sum(dy * xs, keepdims=True), ddk_row)
            dbm[g] = db_h if dbm[g] is None else dbm[g] + db_h
            dcm[g] = dc_h if dcm[g] is None else dcm[g] + dc_h
        for g in range(2):
            dact_ref[:, 256 + NS * g:256 + NS * (g + 1)] = dbm[g]
            dact_ref[:, 512 + NS * g:512 + NS * (g + 1)] = dcm[g]
        ddk_ref[...] += ddk_row
        r2 = lax.broadcasted_iota(jnp.int32, (SSD_L, SSD_L), 0)
        c2 = lax.broadcasted_iota(jnp.int32, (SSD_L, SSD_L), 1)
        dadt = _dot((c2 >= r2).astype(F32), dcs_mat, prec=HI)
        dal_ref[...] += _colsum(dadt * dtv) * a_row
        ddt = jnp.where(lane < NH, (dadt * a_row + ddtx_mat) * _sig(dt_raw + dtb_ref[...]), 0.0)
        ddt_ref[...] = ddt
        ddtb_ref[...] += _colsum(ddt)
        dpre = dact_ref[...] * _dsilu(pre)
        dcb_ref[...] += _colsum(dpre)
        for k in range(4):
            dcw_ref[k:k + 1, :] += _colsum(dpre * _roll(ext, 3 - k)[CH:])
        dext = jnp.concatenate([dpre, dnext_ref[...]], axis=0)
        dx_ref[...] = (dext * w[3:4] + _roll(dext, n_ext - 1) * w[2:3] + _roll(dext, n_ext - 2) * w[1:2]
                       + _roll(dext, n_ext - 3) * w[0:1])[:SSD_L]
        dnext_ref[...] = dpre[0:CH]

    rev = lambda i: nc - 1 - i
    blk = lambda n, cb=0: pl.BlockSpec((SSD_L, n), lambda i: (rev(i), cb))
    row = lambda n: jax.ShapeDtypeStruct((1, n), F32)
    return pl.pallas_call(
        body, name="b_ssd", grid=(nc,),
        in_specs=[blk(768, 2), pl.BlockSpec((CH, 768), lambda i: (jnp.maximum(rev(i) * per - 1, 0), 2)),
                  blk(DTW), blk(GW, 4), blk(GW), blk(GW), pl.BlockSpec((1, NH, HP, NS), lambda i: (rev(i), 0, 0, 0)),
                  _full((4, 768)), _row(768), _row(DTW), _row(DTW), _row(DTW)],
        out_specs=[blk(GW), blk(768), blk(DTW), _full((4, 768)), _row(768), _row(DTW), _row(DTW), _row(DTW)],
        out_shape=[jax.ShapeDtypeStruct((t, GW), F32), jax.ShapeDtypeStruct((t, 768), F32), jax.ShapeDtypeStruct((t, DTW), F32),
                   jax.ShapeDtypeStruct((4, 768), F32), row(768), row(DTW), row(DTW), row(DTW)],
        scratch_shapes=[pltpu.VMEM((NH, HP, NS), F32), pltpu.VMEM((CH, 768), F32), pltpu.VMEM((SSD_L, 768), F32)],
        compiler_params=_cparams(1),
    )(proj, proj, dtp, proj, ypre, dyc, sprev, conv_w, conv_b, dt_bias, a_log, d_skip)


def _s5_block(t):
    return min(t, 256)


def _seg_t():
    r = lax.broadcasted_iota(jnp.int32, (64, 1024), 0)
    c = lax.broadcasted_iota(jnp.int32, (64, 1024), 1)
    return (c // 16 == r).astype(F32)


def _s5_prep_math(a_re, a_im, lstep, b_re, b_im):
    step = jnp.exp(lstep)
    ars = a_re * step
    ais = a_im * step
    mag = jnp.exp(ars)
    lr = mag * jnp.cos(ais)
    li = mag * jnp.sin(ais)
    den = a_re * a_re + a_im * a_im
    nr = lr - 1.0
    f_re = (nr * a_re + li * a_im) / den
    f_im = (li * a_re - nr * a_im) / den
    seg = _seg_t()
    fr = _dot(f_re, seg, prec=HI)
    fi = _dot(f_im, seg, prec=HI)
    return lr, li, fr * b_re - fi * b_im, fr * b_im + fi * b_re, ars, ais


def _s5_prep(a_re, a_im, lstep, b_re, b_im):
    def body(ar, ai, ls, br, bi, lr_o, li_o, bbr_o, bbi_o, ars_o, ais_o):
        outs = _s5_prep_math(ar[...], ai[...], ls[...], br[...], bi[...])
        for o, v in zip((lr_o, li_o, bbr_o, bbi_o, ars_o, ais_o), outs):
            o[...] = v

    s64 = jax.ShapeDtypeStruct((16, 64), F32)
    s1k = jax.ShapeDtypeStruct((16, 1024), F32)
    return pl.pallas_call(body, name="s5_prep", out_shape=[s64, s64, s1k, s1k, s64, s64])(a_re, a_im, lstep, b_re, b_im)


def _s5_prep_bwd(a_re, a_im, lstep, b_re, b_im, dlr, dli, dbbr, dbbi):
    def body(ar, ai, ls, br, bi, g0, g1, g2, g3, o0, o1, o2, o3, o4):
        f = lambda *a: _s5_prep_math(*a)[:4]
        _, vjp = jax.vjp(f, ar[...], ai[...], ls[...], br[...], bi[...])
        for o, v in zip((o0, o1, o2, o3, o4), vjp((g0[...], g1[...], g2[...], g3[...]))):
            o[...] = v

    s64 = jax.ShapeDtypeStruct((16, 64), F32)
    s1k = jax.ShapeDtypeStruct((16, 1024), F32)
    return pl.pallas_call(body, name="s5_prep_bwd", out_shape=[s64, s64, jax.ShapeDtypeStruct((16, 1), F32), s1k, s1k])(
        a_re, a_im, lstep, b_re, b_im, dlr, dli, dbbr, dbbi)


def _s5_tables(ars, ais, lb):
    def body(ar, ai, pr, pi, qr, qi):
        row = lax.broadcasted_iota(jnp.int32, (lb, S5_P), 0).astype(F32)
        for n, o_r, o_i in ((row + 1.0, pr, pi), (float(lb) - row, qr, qi)):
            mag = jnp.exp(n * ar[...])
            o_r[...] = mag * jnp.cos(n * ai[...])
            o_i[...] = mag * jnp.sin(n * ai[...])

    sds = jax.ShapeDtypeStruct((lb, S5_P), F32)
    return pl.pallas_call(body, name="s5_tables", out_shape=[sds] * 4)(ars, ais)


def _s5_scan(bu_r, bu_i, p_r, p_i, c_r, c_i, lb):
    row = lax.broadcasted_iota(jnp.int32, (lb, S5_P), 0)
    sr, si = bu_r, bu_i
    k = 1
    while k < lb:
        lr, li = p_r[k - 1:k, :], p_i[k - 1:k, :]
        tr = jnp.where(row >= k, _roll(sr, k), 0.0)
        ti = jnp.where(row >= k, _roll(si, k), 0.0)
        sr, si = sr + lr * tr - li * ti, si + lr * ti + li * tr
        k *= 2
    pr, pi = p_r[...], p_i[...]
    return sr + pr * c_r - pi * c_i, si + pr * c_i + pi * c_r


def _s5_y(u, sr, si, cre, cim, dsk):
    return _dot(sr, cre) + _dot(si, cim) + dsk * u


def _f_s5(proj, bmat, cre, cim, p_r, p_i, dsk, glu_w, glu_b):
    t = proj.shape[0]
    lb = _s5_block(t)
    nb = t // lb

    def body(u_ref, bm_ref, cr_ref, ci_ref, pr_ref, pi_ref, dk_ref, gw_ref, gb_ref, y_ref, car_ref, st_ref):
        @pl.when(pl.program_id(0) == 0)
        def _():
            st_ref[...] = jnp.zeros_like(st_ref)

        u = u_ref[...]
        bu = _dot(u, bm_ref[...])
        c_r, c_i = st_ref[0:1, 0:S5_P], st_ref[0:1, S5_P:]
        car_ref[0] = st_ref[0:1, :]
        sr, si = _s5_scan(bu[:, :S5_P], bu[:, S5_P:], pr_ref, pi_ref, c_r, c_i, lb)
        st_ref[0:1, 0:S5_P] = sr[lb - 1:lb]
        st_ref[0:1, S5_P:] = si[lb - 1:lb]
        gel = _gelu(_s5_y(u, sr, si, cr_ref[...], ci_ref[...], dk_ref[...]))
        y_ref[...] = gel * _sig(_dot(gel, gw_ref[...]) + gb_ref[...])

    return pl.pallas_call(
        body, name="f_s5", grid=(nb,),
        in_specs=[pl.BlockSpec((lb, GW), lambda i: (i, 5)),
                  _full((GW, 2 * S5_P)), _full((S5_P, GW)), _full((S5_P, GW)), _full((lb, S5_P)), _full((lb, S5_P)),
                  _row(GW), _full((GW, GW)), _row(GW)],
        out_specs=[pl.BlockSpec((lb, GW), lambda i: (i, 0)), pl.BlockSpec((1, 1, 2 * S5_P), lambda i: (i, 0, 0))],
        out_shape=[jax.ShapeDtypeStruct((t, GW), F32), jax.ShapeDtypeStruct((nb, 1, 2 * S5_P), F32)],
        scratch_shapes=[pltpu.VMEM((8, 2 * S5_P), F32)], compiler_params=_cparams(1),
    )(proj, bmat, cre, cim, p_r, p_i, dsk, glu_w, glu_b)


def _b_s5(proj, dyd, carries, bmat, cre, cim, p_r, p_i, q_r, q_i, dsk, glu_w, glu_b):
    t = proj.shape[0]
    lb = _s5_block(t)
    nb = t // lb

    def body(u_ref, dy_ref, car_ref, bm_ref, cr_ref, ci_ref, pr_ref, pi_ref, qr_ref, qi_ref, dk_ref, gw_ref, gb_ref,
             du_ref, dbm_ref, dcr_ref, dci_ref, dlam_ref, ddk_ref, dgw_ref, dgb_ref, gc_ref):
        @pl.when(pl.program_id(0) == 0)
        def _():
            gc_ref[...] = jnp.zeros_like(gc_ref)
            for r in (dbm_ref, dcr_ref, dci_ref, dlam_ref, ddk_ref, dgw_ref, dgb_ref):
                r[...] = jnp.zeros_like(r)

        u = u_ref[...]
        bm = bm_ref[...]
        bu = _dot(u, bm)
        c_r, c_i = car_ref[0, 0:1, 0:S5_P], car_ref[0, 0:1, S5_P:]
        sr, si = _s5_scan(bu[:, :S5_P], bu[:, S5_P:], pr_ref, pi_ref, c_r, c_i, lb)
        cre_v, cim_v, dk, gw = cr_ref[...], ci_ref[...], dk_ref[...], gw_ref[...]
        y = _s5_y(u, sr, si, cre_v, cim_v, dk)
        gel = _gelu(y)
        gate = _sig(_dot(gel, gw) + gb_ref[...])
        dout = dy_ref[...]
        t1 = dout * gel * gate * (1.0 - gate)
        dgw_ref[...] += _dot(gel, t1, TN)
        dgb_ref[...] += _colsum(t1)
        dyv = (dout * gate + _dot(t1, gw, NT)) * _dgelu(y)
        ddk_ref[...] += _colsum(dyv * u)
        dcr_ref[...] += _dot(sr, dyv, TN)
        dci_ref[...] += _dot(si, dyv, TN)
        gr = _dot(dyv, cre_v, NT)
        gi = _dot(dyv, cim_v, NT)
        row = lax.broadcasted_iota(jnp.int32, (lb, S5_P), 0)
        k = 1
        while k < lb:
            lr, li = pr_ref[k - 1:k, :], pi_ref[k - 1:k, :]
            tr = jnp.where(row < lb - k, _roll(gr, lb - k), 0.0)
            ti = jnp.where(row < lb - k, _roll(gi, lb - k), 0.0)
            gr, gi = gr + lr * tr + li * ti, gi + lr * ti - li * tr
            k *= 2
        qr, qi = qr_ref[...], qi_ref[...]
        n_r, n_i = gc_ref[0:1, 0:S5_P], gc_ref[0:1, S5_P:]
        gr, gi = gr + qr * n_r + qi * n_i, gi + qr * n_i - qi * n_r
        gc_ref[0:1, 0:S5_P] = gr[0:1]
        gc_ref[0:1, S5_P:] = gi[0:1]
        gcat = jnp.concatenate([gr, gi], axis=1)
        dbm_ref[...] += _dot(u, gcat, TN)
        du_ref[...] = dyv * dk + _dot(gcat, bm, NT)
        spr = jnp.where(row >= 1, _roll(sr, 1), c_r)
        spi = jnp.where(row >= 1, _roll(si, 1), c_i)
        dlam_ref[0:1, :] += _colsum(gr * spr + gi * spi)
        dlam_ref[1:2, :] += _colsum(gi * spr - gr * spi)

    rev = lambda i: nb - 1 - i
    return pl.pallas_call(
        body, name="b_s5", grid=(nb,),
        in_specs=[pl.BlockSpec((lb, GW), lambda i: (rev(i), 5)), pl.BlockSpec((lb, GW), lambda i: (rev(i), 0)),
                  pl.BlockSpec((1, 1, 2 * S5_P), lambda i: (rev(i), 0, 0)),
                  _full((GW, 2 * S5_P)), _full((S5_P, GW)), _full((S5_P, GW)), _full((lb, S5_P)), _full((lb, S5_P)),
                  _full((lb, S5_P)), _full((lb, S5_P)), _row(GW), _full((GW, GW)), _row(GW)],
        out_specs=[pl.BlockSpec((lb, GW), lambda i: (rev(i), 0)), _full((GW, 2 * S5_P)), _full((S5_P, GW)), _full((S5_P, GW)),
                   _full((2, S5_P)), _row(GW), _full((GW, GW)), _row(GW)],
        out_shape=[jax.ShapeDtypeStruct((t, GW), F32), jax.ShapeDtypeStruct((GW, 2 * S5_P), F32),
                   jax.ShapeDtypeStruct((S5_P, GW), F32), jax.ShapeDtypeStruct((S5_P, GW), F32),
                   jax.ShapeDtypeStruct((2, S5_P), F32), jax.ShapeDtypeStruct((1, GW), F32),
                   jax.ShapeDtypeStruct((GW, GW), F32), jax.ShapeDtypeStruct((1, GW), F32)],
        scratch_shapes=[pltpu.VMEM((8, 2 * S5_P), F32)], compiler_params=_cparams(1),
    )(proj, dyd, carries, bmat, cre, cim, p_r, p_i, q_r, q_i, dsk, glu_w, glu_b)


def _group_norm(ys, bw):
    outs, stats = [], []
    for g, y in enumerate(ys):
        r, n = _rms(y)
        stats.append((r, n))
        outs.append(n * bw[:, GW * g:GW * (g + 1)])
    return jnp.concatenate(outs, axis=1), stats


def _f_out(ya, yb, yc, yd, bw, w_out, h, g1):
    t = h.shape[0]
    tb = _tblock(t)

    def body(a_ref, b_ref, c_ref, d_ref, bw_ref, w_ref, h_ref, g_ref, h2_ref, o_ref, cat_ref):
        cat, _ = _group_norm([a_ref[...], b_ref[...], c_ref[...], d_ref[...]], bw_ref[...])
        catb = cat.astype(BF16)
        cat_ref[...] = catb
        o = _dot(catb, w_ref[...])
        o_ref[...] = o
        h2_ref[...] = h_ref[...] + g_ref[...] * o

    yblk = pl.BlockSpec((tb, GW), lambda i: (i, 0))
    blk = pl.BlockSpec((tb, D), lambda i: (i, 0))
    return pl.pallas_call(
        body, name="f_out", grid=(t // tb,), in_specs=[yblk] * 4 + [_row(D), _full((D, D)), blk, _row(D)],
        out_specs=[blk, blk, blk],
        out_shape=[jax.ShapeDtypeStruct((t, D), F32), jax.ShapeDtypeStruct((t, D), F32), jax.ShapeDtypeStruct((t, D), BF16)],
        compiler_params=_cparams(1),
    )(ya, yb, yc, yd, bw, w_out, h, g1)


def _b_out(dh2, ya, yb, yc, yd, bw, w_out, g1):
    t = dh2.shape[0]
    tb = _tblock(t)

    def body(dh_ref, a_ref, b_ref, c_ref, d_ref, bw_ref, w_ref, g_ref, da_ref, db_ref, dc_ref, dd_ref, do_ref, dbw_ref):
        @pl.when(pl.program_id(0) == 0)
        def _():
            dbw_ref[...] = jnp.zeros_like(dbw_ref)

        do = (dh_ref[...] * g_ref[...]).astype(BF16)
        do_ref[...] = do
        dcat = _dot(do, w_ref[...], NT)
        bw_v = bw_ref[...]
        for g, (y_ref, dy_ref) in enumerate(((a_ref, da_ref), (b_ref, db_ref), (c_ref, dc_ref), (d_ref, dd_ref))):
            r, n = _rms(y_ref[...])
            dc = dcat[:, GW * g:GW * (g + 1)]
            dbw_ref[:, GW * g:GW * (g + 1)] += _colsum(dc * n)
            dy_ref[...] = _rms_bwd(r, n, dc * bw_v[:, GW * g:GW * (g + 1)])

    yblk = pl.BlockSpec((tb, GW), lambda i: (i, 0))
    blk = pl.BlockSpec((tb, D), lambda i: (i, 0))
    ysd = jax.ShapeDtypeStruct((t, GW), F32)
    return pl.pallas_call(
        body, name="b_out", grid=(t // tb,), in_specs=[blk] + [yblk] * 4 + [_row(D), _full((D, D)), _row(D)],
        out_specs=[yblk] * 4 + [blk, _row(D)],
        out_shape=[ysd] * 4 + [jax.ShapeDtypeStruct((t, D), BF16), jax.ShapeDtypeStruct((1, D), F32)],
        compiler_params=_cparams(1),
    )(dh2, ya, yb, yc, yd, bw, w_out, g1)


HB = 1024


def _f_mlp(h2, nw, sc, sh, g2, w1, w2):
    t = h2.shape[0]
    tb = _tblock(t)
    nk = HID // HB

    def body(h_ref, nw_ref, sc_ref, sh_ref, g_ref, w1_ref, w2_ref, h3_ref, m_ref, a_ref, v_ref):
        k = pl.program_id(1)

        @pl.when(k == 0)
        def _():
            _, n = _rms(h_ref[...])
            v_ref[...] = ((n * nw_ref[...]) * (1.0 + sc_ref[...]) + sh_ref[...]).astype(BF16)
            m_ref[...] = jnp.zeros_like(m_ref)

        a = _dot(v_ref[...], w1_ref[...])
        a_ref[...] = a
        ra = jnp.maximum(a, 0.0)
        m_ref[...] += _dot((ra * ra).astype(BF16), w2_ref[...])

        @pl.when(k == nk - 1)
        def _():
            h3_ref[...] = h_ref[...] + g_ref[...] * m_ref[...]

    blk = pl.BlockSpec((tb, D), lambda i, k: (i, 0))
    return pl.pallas_call(
        body, name="f_mlp", grid=(t // tb, nk),
        in_specs=[blk, _row(D), _row(D), _row(D), _row(D), pl.BlockSpec((D, HB), lambda i, k: (0, k)),
                  pl.BlockSpec((HB, D), lambda i, k: (k, 0))],
        out_specs=[blk, blk, pl.BlockSpec((tb, HB), lambda i, k: (i, k)), blk],
        out_shape=[jax.ShapeDtypeStruct((t, D), F32), jax.ShapeDtypeStruct((t, D), F32), jax.ShapeDtypeStruct((t, HID), F32),
                   jax.ShapeDtypeStruct((t, D), BF16)],
        compiler_params=_cparams(2),
    )(h2, nw, sc, sh, g2, w1, w2)


def _b_mlp(dh3, a, g2, w1, w2):
    t = dh3.shape[0]
    tb = _tblock(t)
    nk = HID // HB

    def body(dh_ref, a_ref, g_ref, w1_ref, w2_ref, dv_ref, da_ref, act_ref, dm_ref):
        k = pl.program_id(1)
        dm = (dh_ref[...] * g_ref[...]).astype(BF16)

        @pl.when(k == 0)
        def _():
            dm_ref[...] = dm
            dv_ref[...] = jnp.zeros_like(dv_ref)

        ra = jnp.maximum(a_ref[...], 0.0)
        act_ref[...] = (ra * ra).astype(BF16)
        da = (_dot(dm, w2_ref[...], NT) * (2.0 * ra)).astype(BF16)
        da_ref[...] = da
        dv_ref[...] += _dot(da, w1_ref[...], NT)

    blk = pl.BlockSpec((tb, D), lambda i, k: (i, 0))
    hblk = pl.BlockSpec((tb, HB), lambda i, k: (i, k))
    return pl.pallas_call(
        body, name="b_mlp", grid=(t // tb, nk),
        in_specs=[blk, hblk, _row(D), pl.BlockSpec((D, HB), lambda i, k: (0, k)), pl.BlockSpec((HB, D), lambda i, k: (k, 0))],
        out_specs=[blk, hblk, hblk, blk],
        out_shape=[jax.ShapeDtypeStruct((t, D), F32), jax.ShapeDtypeStruct((t, HID), BF16), jax.ShapeDtypeStruct((t, HID), BF16),
                   jax.ShapeDtypeStruct((t, D), BF16)],
        compiler_params=_cparams(2),
    )(dh3, a, g2, w1, w2)


def _b_final(h, tgt, fw):
    t = h.shape[0]
    tb = _tblock(t)

    def body(h_ref, t_ref, w_ref, dh_ref, loss_ref, dfw_ref):
        @pl.when(pl.program_id(0) == 0)
        def _():
            loss_ref[...] = jnp.zeros_like(loss_ref)
            dfw_ref[...] = jnp.zeros_like(dfw_ref)

        r, n = _rms(h_ref[...])
        wv = w_ref[...]
        err = n * wv - t_ref[...]
        loss_ref[...] += jnp.sum(err * err, keepdims=True) * (0.5 / D)
        dy = err * (1.0 / D)
        dfw_ref[...] += _colsum(dy * n)
        dh_ref[...] = _rms_bwd(r, n, dy * wv)

    blk = pl.BlockSpec((tb, D), lambda i: (i, 0))
    return pl.pallas_call(
        body, name="b_final", grid=(t // tb,), in_specs=[blk, blk, _row(D)], out_specs=[blk, _row(1), _row(D)],
        out_shape=[jax.ShapeDtypeStruct((t, D), F32), jax.ShapeDtypeStruct((1, 1), F32), jax.ShapeDtypeStruct((1, D), F32)],
        compiler_params=_cparams(1),
    )(h, tgt, fw)


_EYE16 = None


def _eye(n):
    return jnp.eye(n, dtype=F32)


def _pool_embed(pool_w):
    return jnp.einsum('gcd,gk->gckd', pool_w, _eye(4)).reshape(GW, GW)


def _pool_extract(m):
    return jnp.einsum('gcgd->gcd', m.reshape(4, 64, 4, 64))


def _bmat_embed(bb):
    return jnp.einsum('gph,gk->ghkp', bb, _eye(16)).reshape(GW, S5_P)


def _bmat_extract(m):
    return jnp.einsum('ghgp->gph', m.reshape(16, 16, 16, 64))


def _cmat_embed(cc):
    return jnp.einsum('ghp,gk->kpgh', cc, _eye(16)).reshape(S5_P, GW)


def _cmat_extract(m):
    return jnp.einsum('gpgh->ghp', m.reshape(16, 64, 16, 16))


def _pad_lanes(v, n=DTW):
    return jnp.pad(v.reshape(1, -1), ((0, 0), (0, n - v.shape[-1])))


def _layer_params(p, l, mod):
    q = {}
    q['mod'] = [mod[k:k + 1] for k in range(6)]
    q['nw1'] = p['norm_mix_w'][l:l + 1]
    q['nw2'] = p['norm_mlp_w'][l:l + 1]
    w_in = p['w_in'][l]
    q['w_main'] = jnp.concatenate([w_in[:, :1280], w_in[:, 2052:2308], w_in[:, 1280:2048]], axis=1)
    q['w_dt'] = jnp.pad(w_in[:, 2048:2052], ((0, 0), (0, DTW - 4)))
    q['pool_mat'] = _pool_embed(p['pool_w'][l])
    q['pool_scale'] = p['pool_scale'][l:l + 1]
    q['sconv_w'] = p['sconv_w'][l]
    q['conv_w'] = p['ssd_conv_w'][l]
    q['conv_b'] = p['ssd_conv_b'][l:l + 1]
    q['dt_bias'] = _pad_lanes(p['ssd_dt_bias'][l])
    q['a_log'] = _pad_lanes(p['ssd_a_log'][l])
    q['ssd_d'] = _pad_lanes(p['ssd_d'][l])
    q['s5_raw'] = (p['s5_a_re'][l], p['s5_a_im'][l], p['s5_log_step'][l].reshape(16, 1),
                   p['s5_b_re'][l].reshape(16, 1024), p['s5_b_im'][l].reshape(16, 1024))
    q['cre'] = _cmat_embed(p['s5_c_re'][l])
    q['cim'] = -_cmat_embed(p['s5_c_im'][l])
    q['s5_d'] = p['s5_d'][l:l + 1]
    q['glu_w'] = p['s5_glu_w'][l]
    q['glu_b'] = p['s5_glu_b'][l:l + 1]
    q['bw'] = p['branch_norm_w'][l:l + 1]
    q['w_out'] = p['w_out'][l]
    q['w1'] = p['mlp_w1'][l]
    q['w2'] = p['mlp_w2'][l]
    return q


def _layer_fwd(h, q):
    sh1, sc1, g1, sh2, sc2, g2 = q['mod']
    t = h.shape[0]
    s = {'h': h}
    s['proj'], s['dtp'], s['u'] = _f_in(h, q['nw1'], sc1, sh1, q['w_main'], q['w_dt'])
    s['ya'], s['yb'] = _f_ab(s['proj'], q['pool_mat'], q['pool_scale'], q['sconv_w'])
    s['yc'], s['ypre'], s['sprev'] = _f_ssd(s['proj'], s['dtp'], q['conv_w'], q['conv_b'], q['dt_bias'], q['a_log'], q['ssd_d'])
    lr, li, bbr, bbi, ars, ais = _s5_prep(*q['s5_raw'])
    s['bmat'] = jnp.concatenate([_bmat_embed(bbr.reshape(16, 64, 16)), _bmat_embed(bbi.reshape(16, 64, 16))], axis=1)
    s['tables'] = _s5_tables(ars.reshape(1, S5_P), ais.reshape(1, S5_P), _s5_block(t))
    s['yd'], s['carries'] = _f_s5(s['proj'], s['bmat'], q['cre'], q['cim'], s['tables'][0], s['tables'][1],
                                  q['s5_d'], q['glu_w'], q['glu_b'])
    s['h2'], s['o'], s['cat'] = _f_out(s['ya'], s['yb'], s['yc'], s['yd'], q['bw'], q['w_out'], h, g1)
    h3, s['m'], s['a'], s['v'] = _f_mlp(s['h2'], q['nw2'], sc2, sh2, g2, q['w1'], q['w2'])
    return h3, s


def _layer_bwd(dh3, q, s):
    sh1, sc1, g1, sh2, sc2, g2 = q['mod']
    g = {}
    dv, da, act, dm = _b_mlp(dh3, s['a'], g2, q['w1'], q['w2'])
    g['mlp_w1'] = _tn_matmul(s['v'], da, "dw1", col_major=True)
    g['mlp_w2'] = _tn_matmul(act, dm, "dw2")
    dh2, dsc2, dsh2, dnw2, dg2 = _b_normmod(dv, s['h2'], dh3, s['m'], q['nw2'], sc2, "b_norm_mlp")
    dya, dyb, dyc, dyd, do, dbw = _b_out(dh2, s['ya'], s['yb'], s['yc'], s['yd'], q['bw'], q['w_out'], g1)
    g['w_out'] = _tn_matmul(s['cat'], do, "dwout")
    g['branch_norm_w'] = dbw[0]
    dab, dpm, dps, dsw = _b_ab(s['proj'], dya, dyb, q['pool_mat'], q['pool_scale'], q['sconv_w'])
    g['pool_w'] = _pool_extract(dpm)
    g['pool_scale'] = dps[0]
    g['sconv_w'] = dsw
    dz, dxbc, ddt, dcw, dcb, ddtb, dal, ddk = _b_ssd(s['proj'], s['dtp'], s['ypre'], dyc, s['sprev'], q['conv_w'],
                                                     q['conv_b'], q['dt_bias'], q['a_log'], q['ssd_d'])
    g['ssd_conv_w'] = dcw
    g['ssd_conv_b'] = dcb[0]
    g['ssd_dt_bias'] = ddtb[0, :4]
    g['ssd_a_log'] = dal[0, :4]
    g['ssd_d'] = ddk[0, :4]
    tb = s['tables']
    ds5, dbmat, dcre, dcim, dlam, dd5, dgw, dgb = _b_s5(s['proj'], dyd, s['carries'], s['bmat'], q['cre'], q['cim'],
                                                        tb[0], tb[1], tb[2], tb[3], q['s5_d'], q['glu_w'], q['glu_b'])
    g['s5_c_re'] = _cmat_extract(dcre)
    g['s5_c_im'] = -_cmat_extract(dcim)
    g['s5_d'] = dd5[0]
    g['s5_glu_w'] = dgw
    g['s5_glu_b'] = dgb[0]
    dbbr = _bmat_extract(dbmat[:, :S5_P]).reshape(16, 1024)
    dbbi = _bmat_extract(dbmat[:, S5_P:]).reshape(16, 1024)
    dar, dai, dls, dbr, dbi = _s5_prep_bwd(*q['s5_raw'], dlam[0].reshape(16, 64), dlam[1].reshape(16, 64), dbbr, dbbi)
    g['s5_a_re'], g['s5_a_im'], g['s5_log_step'] = dar, dai, dls[:, 0]
    g['s5_b_re'], g['s5_b_im'] = dbr.reshape(16, 64, 16), dbi.reshape(16, 64, 16)
    du = _b_in_du(dab, dz, dxbc, ds5, ddt, q['w_main'], q['w_dt'])
    u = s['u']
    pieces = [_tn_matmul(u, dab, "dwin_ab"), _tn_matmul(u, dz, "dwin_z"), _tn_matmul(u, dxbc, "dwin_xbc"),
              _tn_matmul(u, ddt, "dwin_dt")[:, :4], _tn_matmul(u, ds5, "dwin_s5")]
    g['w_in'] = jnp.concatenate(pieces, axis=1)
    dh, dsc1, dsh1, dnw1, dg1 = _b_normmod(du, s['h'], dh2, s['o'], q['nw1'], sc1, "b_norm_mix")
    g['norm_mix_w'] = dnw1[0]
    g['norm_mlp_w'] = dnw2[0]
    dmod = jnp.concatenate([dsh1, dsc1, dg1, dsh2, dsc2, dg2], axis=1)
    return dh, g, dmod


def _local_step(x, tgt, p, mod):
    qs = [_layer_params(p, l, mod[l]) for l in range(2)]
    h = x
    saved = []
    for l in range(2):
        h, s = _layer_fwd(h, qs[l])
        saved.append(s)
    dh, loss, dfw = _b_final(h, tgt, p['final_norm_w'].reshape(1, D))
    grads = [None, None]
    dmods = [None, None]
    for l in (1, 0):
        dh, grads[l], dmods[l] = _layer_bwd(dh, qs[l], saved[l])
    out = {k: jnp.stack([grads[0][k], grads[1][k]]) for k in grads[0]}
    out['final_norm_w'] = dfw[0]
    return loss, dh, out, jnp.concatenate(dmods, axis=0)


def _pack(arrs):
    flat = []
    for a in arrs:
        f = a.reshape(-1).astype(F32)
        pad = (-f.shape[0]) % 1024
        flat.append(jnp.pad(f, (0, pad)) if pad else f)
    buf = jnp.concatenate(flat)
    return jnp.pad(buf, (0, (-buf.shape[0]) % (256 * 128))).reshape(-1, 128)


def _unpack(buf, shapes):
    flat = buf.reshape(-1)
    out, off = [], 0
    for shp in shapes:
        n = int(math.prod(shp)) if len(shp) else 1
        out.append(flat[off:off + n].reshape(shp))
        off += n + ((-n) % 1024)
    return out


def _shard_of(a, axis, k):
    n = a.shape[axis] // 4
    return lax.dynamic_slice_in_dim(a, k * n, n, axis)


def kernel(x, c, norm_mix_w, norm_mlp_w, ada_w, ada_b, w_in, pool_w, pool_scale, sconv_w, ssd_conv_w, ssd_conv_b, ssd_dt_bias, ssd_a_log, ssd_d, s5_a_re, s5_a_im, s5_log_step, s5_b_re, s5_b_im, s5_c_re, s5_c_im, s5_d, s5_glu_w, s5_glu_b, branch_norm_w, w_out, mlp_w1, mlp_w2, final_norm_w, loss_target, m_norm_mix_w, m_norm_mlp_w, m_ada_w, m_ada_b, m_w_in, m_pool_w, m_pool_scale, m_sconv_w, m_ssd_conv_w, m_ssd_conv_b, m_ssd_dt_bias, m_ssd_a_log, m_ssd_d, m_s5_a_re, m_s5_a_im, m_s5_log_step, m_s5_b_re, m_s5_b_im, m_s5_c_re, m_s5_c_im, m_s5_d, m_s5_glu_w, m_s5_glu_b, m_branch_norm_w, m_w_out, m_mlp_w1, m_mlp_w2, m_final_norm_w, v_norm_mix_w, v_norm_mlp_w, v_ada_w, v_ada_b, v_w_in, v_pool_w, v_pool_scale, v_sconv_w, v_ssd_conv_w, v_ssd_conv_b, v_ssd_dt_bias, v_ssd_a_log, v_ssd_d, v_s5_a_re, v_s5_a_im, v_s5_log_step, v_s5_b_re, v_s5_b_im, v_s5_c_re, v_s5_c_im, v_s5_d, v_s5_glu_w, v_s5_glu_b, v_branch_norm_w, v_w_out, v_mlp_w1, v_mlp_w2, v_final_norm_w):
    loc = locals()
    w = {n: loc[n] for n in WEIGHTS}
    mom = {n: loc['m_' + n] for n in WEIGHTS}
    var = {n: loc['v_' + n] for n in WEIGHTS}
    ix, iy, ic = lax.axis_index("x"), lax.axis_index("y"), lax.axis_index("c")
    chip = 2 * ix + iy
    dev = 4 * ix + 2 * iy + ic

    (c_all,) = _exchange([c], EVERYONE, False, "ag_cond")
    c_all = c_all.reshape(8, D)
    small_sh = _exchange([w[n] for n in SMALL_SHARDED], CHIPS, False, "ag_small")
    big_sh = _exchange([w['w_in'].astype(BF16), w['w_out'].astype(BF16), w['mlp_w1'].astype(BF16), w['mlp_w2'].astype(BF16)],
                       CHIPS, False, "ag_big")
    p = dict(w)
    for n, g in zip(SMALL_SHARDED, small_sh):
        ax = SMALL_SHARDED[n]
        p[n] = jnp.concatenate([g[k] for k in range(4)], axis=ax)
    p['w_in'] = jnp.concatenate([big_sh[0][k] for k in range(4)], axis=2)
    p['w_out'] = jnp.concatenate([big_sh[1][k] for k in range(4)], axis=1)
    p['mlp_w1'] = jnp.concatenate([big_sh[2][k] for k in range(4)], axis=2)
    p['mlp_w2'] = jnp.concatenate([big_sh[3][k] for k in range(4)], axis=1)

    ada_b_sh = _shard_of(w['ada_b'], 1, chip).reshape(2, 1, 6 * D // 4)
    mod_sh = _ada_fwd(c_all, w['ada_w'], ada_b_sh)
    (mod_all,) = _exchange([mod_sh], CHIPS, False, "ag_mod")
    mine = lax.dynamic_index_in_dim(mod_all, dev, axis=2, keepdims=False)
    mod = jnp.transpose(mine, (1, 0, 2)).reshape(2, 6, D)

    loss, grad_x, g, dmod = _local_step(x[0], loss_target[0], p, mod)

    (dmod_all,) = _exchange([dmod], EVERYONE, False, "ag_dmod")
    dmod_all = jnp.transpose(dmod_all, (1, 0, 2))
    g_ada_w, g_ada_b = _ada_bwd(c_all, _shard_of(dmod_all, 2, chip), dmod_all)

    gw_in = jnp.transpose(g['w_in'].reshape(2, D, 4, 577), (0, 2, 1, 3))
    gw_out = g['w_out'].reshape(2, 4, 256, D)
    gw1 = g['mlp_w1']
    gw2 = g['mlp_w2'].reshape(2, 4, 1024, D)
    pair = _exchange([gw_in, gw_out, gw1, gw2], SIBLING, True, "rs_pair")
    pair = [_sum_lead(a, "rs_pair_sum%d" % k) for k, a in enumerate(pair)]
    quad = _exchange(pair, CHIPS, True, "rs_chips")
    quad = [_sum_lead(a, "rs_chip_sum%d" % k) for k, a in enumerate(quad)]
    both = _exchange(quad, SIBLING, False, "ag_pair")
    red = dict(zip(('w_in', 'w_out', 'mlp_w1', 'mlp_w2'), both))
    red['ada_w'] = g_ada_w

    small_names = [n for n in WEIGHTS if n not in BIG and n != 'ada_b']
    small_shapes = [g[n].shape for n in small_names] + [(1, 1)]
    packed = _pack([g[n] for n in small_names] + [loss])
    (packed_all,) = _exchange([packed], EVERYONE, False, "ag_smallgrad")
    summed = _unpack(_sum_lead(packed_all, "smallgrad_sum"), small_shapes)
    for n, a in zip(small_names, summed[:-1]):
        red[n] = _shard_of(a, SMALL_SHARDED[n], chip) if n in SMALL_SHARDED else a
    red['ada_b'] = g_ada_b
    loss_out = summed[-1].reshape(())

    delta, new_m, new_v = {}, {}, {}
    for n in BIG:
        delta[n], new_m[n], new_v[n] = _adamw(w[n], red[n], mom[n], var[n], "adamw_" + n)
    rest = [n for n in WEIGHTS if n not in BIG]
    shapes = [w[n].shape for n in rest]
    d_p, m_p, v_p = _adamw(_pack([w[n] for n in rest]), _pack([red[n] for n in rest]), _pack([mom[n] for n in rest]),
                           _pack([var[n] for n in rest]), "adamw_small")
    for n, a, b, cc in zip(rest, _unpack(d_p, shapes), _unpack(m_p, shapes), _unpack(v_p, shapes)):
        delta[n], new_m[n], new_v[n] = a, b, cc

    return (loss_out, grad_x[None], *[red[n] for n in WEIGHTS], *[delta[n] for n in WEIGHTS],
            *[new_m[n] for n in WEIGHTS], *[new_v[n] for n in WEIGHTS])
```

```python
import functools
import math

import jax
import jax.numpy as jnp
from jax import lax
from jax.experimental import pallas as pl
from jax.experimental.pallas import tpu as pltpu

F32 = jnp.float32
BF16 = jnp.bfloat16
HI = lax.Precision.HIGHEST

D = 1024
GW = 256
HID = 4096
EPS = 1e-6
PW = 2304
DTW = 128
SSD_L = 128
NH, HP, NS = 4, 64, 128
S5_P = 1024
MESH = pl.DeviceIdType.MESH

ADAM_LR, ADAM_B1, ADAM_B2, ADAM_EPS, ADAM_WD, ADAM_STEP = 0.001, 0.9, 0.999, 1e-08, 0.01, 10

NT = (((1,), (1,)), ((), ()))
TN = (((0,), (0,)), ((), ()))

WEIGHTS = ['norm_mix_w', 'norm_mlp_w', 'ada_w', 'ada_b', 'w_in', 'pool_w', 'pool_scale', 'sconv_w', 'ssd_conv_w',
           'ssd_conv_b', 'ssd_dt_bias', 'ssd_a_log', 'ssd_d', 's5_a_re', 's5_a_im', 's5_log_step', 's5_b_re', 's5_b_im',
           's5_c_re', 's5_c_im', 's5_d', 's5_glu_w', 's5_glu_b', 'branch_norm_w', 'w_out', 'mlp_w1', 'mlp_w2',
           'final_norm_w']
BIG = ('ada_w', 'w_in', 'w_out', 'mlp_w1', 'mlp_w2')
SMALL_SHARDED = {'sconv_w': 2, 'ssd_conv_w': 2, 's5_glu_w': 1}


def _cparams(n_axes, vmem_mb=48):
    return pltpu.CompilerParams(dimension_semantics=("arbitrary",) * n_axes, vmem_limit_bytes=vmem_mb * 1024 * 1024)


def _row(n):
    return pl.BlockSpec((1, n), lambda *_: (0, 0))


def _full(shape):
    nd = len(shape)
    return pl.BlockSpec(tuple(shape), lambda *_: (0,) * nd)


def _dot(a, b, dims=None, prec=None):
    if dims is None:
        dims = (((a.ndim - 1,), (0,)), ((), ()))
    return lax.dot_general(a, b, dims, preferred_element_type=F32, precision=prec)


def _bdot(a, b, dims=None):
    return _dot(a.astype(BF16), b.astype(BF16), dims)


def _sig(x):
    return jax.nn.sigmoid(x)


def _silu(x):
    return x * _sig(x)


def _dsilu(x):
    s = _sig(x)
    return s * (1.0 + x * (1.0 - s))


def _softplus(x):
    return jnp.maximum(x, 0.0) + jnp.log(1.0 + jnp.exp(-jnp.abs(x)))


_GK = math.sqrt(2.0 / math.pi)


def _gelu(x):
    return 0.5 * x * (1.0 + jnp.tanh(_GK * (x + 0.044715 * x * x * x)))


def _dgelu(x):
    th = jnp.tanh(_GK * (x + 0.044715 * x * x * x))
    return 0.5 * (1.0 + th) + 0.5 * x * (1.0 - th * th) * _GK * (1.0 + 3.0 * 0.044715 * x * x)


def _colsum(x):
    return jnp.sum(x, axis=0, keepdims=True)


def _rms(x):
    r = lax.rsqrt(jnp.mean(x * x, axis=-1, keepdims=True) + EPS)
    return r, x * r


def _rms_bwd(r, n, dn):
    return r * (dn - n * jnp.mean(dn * n, axis=-1, keepdims=True))


def _roll(x, k):
    n = x.shape[0]
    k = k % n
    return x if k == 0 else pltpu.roll(x, k, axis=0)


def _tblock(t, want=512):
    return min(t, want)


def _peer(mask):
    x, y, c = lax.axis_index("x"), lax.axis_index("y"), lax.axis_index("c")
    return (x ^ ((mask >> 2) & 1), y ^ ((mask >> 1) & 1), c ^ (mask & 1))


def _group_index(masks):
    x, y, c = lax.axis_index("x"), lax.axis_index("y"), lax.axis_index("c")
    full = 0
    for m in masks:
        full |= m
    bits = [b for b in (4, 2, 1) if full & b]

    def idx(px, py, pc):
        v = {4: px, 2: py, 1: pc}
        out = 0
        for b in bits:
            out = out * 2 + v[b]
        return out

    return idx(x, y, c), [idx(*_peer(m)) for m in masks]


def _exchange(arrs, masks, scatter, name):
    n_arr, n_peer, n_grp = len(arrs), len(masks), len(masks) + 1

    def body(*refs):
        ins, outs = refs[:n_arr], refs[n_arr:2 * n_arr]
        send_sems, recv_sems, local_sems = refs[2 * n_arr:]
        me, peer_idx = _group_index(masks)
        copies = []
        for t in range(n_arr):
            src_me = ins[t].at[me] if scatter else ins[t]
            loc = pltpu.make_async_copy(src_me, outs[t].at[me], local_sems.at[t])
            loc.start()
            copies.append(loc)
            for j, m in enumerate(masks):
                src = ins[t].at[peer_idx[j]] if scatter else ins[t]
                cp = pltpu.make_async_remote_copy(src_ref=src, dst_ref=outs[t].at[me], send_sem=send_sems.at[t, j],
                                                  recv_sem=recv_sems.at[t, j], device_id=_peer(m), device_id_type=MESH)
                cp.start()
                copies.append(cp)
        for cp in copies:
            cp.wait()

    hbm = pl.BlockSpec(memory_space=pl.ANY)
    out_shape = [jax.ShapeDtypeStruct((n_grp,) + (a.shape[1:] if scatter else a.shape), a.dtype) for a in arrs]
    outs = pl.pallas_call(
        body, name=name, in_specs=[hbm] * n_arr, out_specs=[hbm] * n_arr, out_shape=out_shape,
        scratch_shapes=[pltpu.SemaphoreType.DMA((n_arr, n_peer)), pltpu.SemaphoreType.DMA((n_arr, n_peer)),
                        pltpu.SemaphoreType.DMA((n_arr,))],
    )(*arrs)
    return list(outs)


CHIPS = (4, 2, 6)
EVERYONE = (1, 2, 3, 4, 5, 6, 7)
SIBLING = (1,)


def _sum_lead(a, name, out_dtype):
    n = a.shape[0]
    shape = a.shape[1:]

    def body(a_ref, o_ref):
        acc = a_ref[0].astype(F32)
        for k in range(1, n):
            acc = acc + a_ref[k].astype(F32)
        o_ref[...] = acc.astype(out_dtype)

    if len(shape) == 3:
        blk = (1,) + shape[1:]
        return pl.pallas_call(
            body, name=name, grid=(shape[0],), in_specs=[pl.BlockSpec((n,) + blk, lambda i: (0, i, 0, 0))],
            out_specs=pl.BlockSpec(blk, lambda i: (i, 0, 0)), out_shape=jax.ShapeDtypeStruct(shape, out_dtype),
            compiler_params=_cparams(1),
        )(a)
    rows, cols = shape
    rb = rows
    for cand in (512, 256, 128, 64, 32, 16):
        if rows % cand == 0 and rows > cand:
            rb = cand
            break
    return pl.pallas_call(
        body, name=name, grid=(rows // rb,), in_specs=[pl.BlockSpec((n, rb, cols), lambda i: (0, i, 0))],
        out_specs=pl.BlockSpec((rb, cols), lambda i: (i, 0)), out_shape=jax.ShapeDtypeStruct((rows, cols), out_dtype),
        compiler_params=_cparams(1),
    )(a)


def _tn_matmul(a, b, name, col_major=False):
    t, k = a.shape
    n = b.shape[1]
    tb = _tblock(t, 1024)
    kb = min(k, 1024)
    nb = min(n, 1024)
    grid = (k // kb, n // nb, t // tb)

    def body(a_ref, b_ref, o_ref):
        @pl.when(pl.program_id(2) == 0)
        def _():
            o_ref[...] = jnp.zeros_like(o_ref)

        acc = _bdot(a_ref[...], b_ref[...], TN)
        if col_major:
            o_ref[0] += acc
        else:
            o_ref[...] += acc

    if col_major:
        out_spec = pl.BlockSpec((1, kb, nb), lambda ki, ni, ti: (ni, ki, 0))
        out_shape = jax.ShapeDtypeStruct((n // nb, k, nb), F32)
    else:
        out_spec = pl.BlockSpec((kb, nb), lambda ki, ni, ti: (ki, ni))
        out_shape = jax.ShapeDtypeStruct((k, n), F32)
    return pl.pallas_call(
        body, name=name, grid=grid,
        in_specs=[pl.BlockSpec((tb, kb), lambda ki, ni, ti: (ti, ki)), pl.BlockSpec((tb, nb), lambda ki, ni, ti: (ti, ni))],
        out_specs=out_spec, out_shape=out_shape, compiler_params=_cparams(3),
    )(a, b)


def _adamw(w, g, m, v, name):
    shape = w.shape
    cols = shape[-1]
    rows = int(math.prod(shape[:-1]))
    rb = rows
    for cand in (256, 128, 64, 32, 16, 8):
        if rows % cand == 0 and rows > cand:
            rb = cand
            break
    bc1 = 1.0 - ADAM_B1 ** ADAM_STEP
    bc2 = 1.0 - ADAM_B2 ** ADAM_STEP

    def body(w_ref, g_ref, m_ref, v_ref, d_ref, nm_ref, nv_ref):
        gg = g_ref[...]
        m2 = ADAM_B1 * m_ref[...] + (1.0 - ADAM_B1) * gg
        v2 = ADAM_B2 * v_ref[...] + (1.0 - ADAM_B2) * (gg * gg)
        m_hat = m2 / bc1
        v_hat = v2 / bc2
        d_ref[...] = -ADAM_LR * (m_hat / (jnp.sqrt(v_hat) + ADAM_EPS) + ADAM_WD * w_ref[...])
        nm_ref[...] = m2
        nv_ref[...] = v2

    spec = pl.BlockSpec((rb, cols), lambda i: (i, 0))
    sds = jax.ShapeDtypeStruct((rows, cols), F32)
    outs = pl.pallas_call(
        body, name=name, grid=(rows // rb,), in_specs=[spec] * 4, out_specs=[spec] * 3, out_shape=[sds] * 3,
        compiler_params=_cparams(1),
    )(*(z.reshape(rows, cols) for z in (w, g, m, v)))
    return tuple(o.reshape(shape) for o in outs)


def _ada_fwd(c_all, ada_w_sh, ada_b_sh):
    s = ada_w_sh.shape[2]
    sb = 512

    def body(c_ref, w_ref, b_ref, o_ref):
        cond = _silu(c_ref[...])
        o_ref[0] = _bdot(cond, w_ref[0]) + b_ref[0]

    return pl.pallas_call(
        body, name="ada_fwd", grid=(2, s // sb),
        in_specs=[_full((8, D)), pl.BlockSpec((1, D, sb), lambda l, j: (l, 0, j)), pl.BlockSpec((1, 1, sb), lambda l, j: (l, 0, j))],
        out_specs=pl.BlockSpec((1, 8, sb), lambda l, j: (l, 0, j)), out_shape=jax.ShapeDtypeStruct((2, 8, s), F32),
        compiler_params=_cparams(2),
    )(c_all, ada_w_sh, ada_b_sh)


def _ada_bwd(c_all, dmod_sh, dmod_all):
    s = dmod_sh.shape[2]
    sb = 512

    def body(c_ref, d_ref, o_ref):
        cond = _silu(c_ref[...])
        o_ref[0] = _bdot(cond, d_ref[0], TN)

    gw = pl.pallas_call(
        body, name="ada_bwd_w", grid=(2, s // sb),
        in_specs=[_full((8, D)), pl.BlockSpec((1, 8, sb), lambda l, j: (l, 0, j))],
        out_specs=pl.BlockSpec((1, D, sb), lambda l, j: (l, 0, j)), out_shape=jax.ShapeDtypeStruct((2, D, s), F32),
        compiler_params=_cparams(2),
    )(c_all, dmod_sh)

    def body_b(d_ref, o_ref):
        acc = d_ref[0, 0:1, :]
        for k in range(1, 8):
            acc = acc + d_ref[0, k:k + 1, :]
        o_ref[0] = acc

    gb = pl.pallas_call(
        body_b, name="ada_bwd_b", grid=(2,), in_specs=[pl.BlockSpec((1, 8, 6 * D), lambda l: (l, 0, 0))],
        out_specs=pl.BlockSpec((1, 1, 6 * D), lambda l: (l, 0, 0)), out_shape=jax.ShapeDtypeStruct((2, 1, 6 * D), F32),
        compiler_params=_cparams(1),
    )(dmod_all)
    return gw, gb.reshape(2, 6 * D)


def _f_in(h, nw, sc, sh, w_main, w_dt):
    t = h.shape[0]
    tb = _tblock(t)

    def body(h_ref, nw_ref, sc_ref, sh_ref, w_ref, wd_ref, p_ref, dt_ref, u_ref):
        _, n = _rms(h_ref[...])
        u = ((n * nw_ref[...]) * (1.0 + sc_ref[...]) + sh_ref[...]).astype(BF16)
        u_ref[...] = u
        p_ref[...] = _dot(u, w_ref[...])
        dt_ref[...] = _dot(u, wd_ref[...])

    return pl.pallas_call(
        body, name="f_in", grid=(t // tb,),
        in_specs=[pl.BlockSpec((tb, D), lambda i: (i, 0)), _row(D), _row(D), _row(D), _full((D, PW)), _full((D, DTW))],
        out_specs=[pl.BlockSpec((tb, PW), lambda i: (i, 0)), pl.BlockSpec((tb, DTW), lambda i: (i, 0)),
                   pl.BlockSpec((tb, D), lambda i: (i, 0))],
        out_shape=[jax.ShapeDtypeStruct((t, PW), F32), jax.ShapeDtypeStruct((t, DTW), F32), jax.ShapeDtypeStruct((t, D), BF16)],
        compiler_params=_cparams(1),
    )(h, nw, sc, sh, w_main, w_dt)


def _b_in_du(dab, dz, dxbc, ds5, ddt, w_main, w_dt):
    t = dab.shape[0]
    tb = _tblock(t)

    def body(a_ref, z_ref, x_ref, s_ref, d_ref, w_ref, wd_ref, o_ref):
        acc = _bdot(a_ref[...], w_ref[:, 0:1024], NT)
        acc += _bdot(z_ref[...], w_ref[:, 1024:1280], NT)
        acc += _bdot(s_ref[...], w_ref[:, 1280:1536], NT)
        acc += _bdot(x_ref[...], w_ref[:, 1536:2304], NT)
        acc += _bdot(d_ref[...], wd_ref[...], NT)
        o_ref[...] = acc

    blk = lambda n: pl.BlockSpec((tb, n), lambda i: (i, 0))
    return pl.pallas_call(
        body, name="b_in_du", grid=(t // tb,),
        in_specs=[blk(1024), blk(256), blk(768), blk(256), blk(DTW), _full((D, PW)), _full((D, DTW))],
        out_specs=blk(D), out_shape=jax.ShapeDtypeStruct((t, D), F32), compiler_params=_cparams(1),
    )(dab, dz, dxbc, ds5, ddt, w_main, w_dt)


def _b_normmod(du, x, dres, gated, nw, sc, name):
    t = x.shape[0]
    tb = _tblock(t)

    def body(du_ref, x_ref, dr_ref, g_ref, nw_ref, sc_ref, dx_ref, dsc_ref, dsh_ref, dnw_ref, dg_ref):
        @pl.when(pl.program_id(0) == 0)
        def _():
            for r in (dsc_ref, dsh_ref, dnw_ref, dg_ref):
                r[...] = jnp.zeros_like(r)

        du_v = du_ref[...]
        r, n = _rms(x_ref[...])
        nwv = nw_ref[...]
        scale = 1.0 + sc_ref[...]
        dsc_ref[...] += _colsum(du_v * (n * nwv))
        dsh_ref[...] += _colsum(du_v)
        dnw_ref[...] += _colsum(du_v * scale * n)
        dres_v = dr_ref[...]
        dg_ref[...] += _colsum(dres_v * g_ref[...])
        dx_ref[...] = dres_v + _rms_bwd(r, n, du_v * scale * nwv)

    blk = pl.BlockSpec((tb, D), lambda i: (i, 0))
    row = jax.ShapeDtypeStruct((1, D), F32)
    return pl.pallas_call(
        body, name=name, grid=(t // tb,), in_specs=[blk, blk, blk, blk, _row(D), _row(D)],
        out_specs=[blk, _row(D), _row(D), _row(D), _row(D)], out_shape=[jax.ShapeDtypeStruct((t, D), F32), row, row, row, row],
        compiler_params=_cparams(1),
    )(du, x, dres, gated, nw, sc)


HALO = 16


def _lane_group(shape):
    return lax.broadcasted_iota(jnp.int32, shape, 1) // 64


def _window_select(g, s2, s4, s8, s16):
    return jnp.where(g == 0, s2, jnp.where(g == 1, s4, jnp.where(g == 2, s8, s16)))


def _pool_count(t0, rows):
    g = _lane_group((rows, GW))
    win = _window_select(g, 2, 4, 8, 16)
    tt = t0 + lax.broadcasted_iota(jnp.int32, (rows, GW), 0)
    return jnp.minimum(tt + 1, win).astype(F32)


def _pool_p(v_ext, t0, tb):
    s2 = v_ext + _roll(v_ext, 1)
    s4 = s2 + _roll(s2, 2)
    s8 = s4 + _roll(s4, 4)
    s16 = s8 + _roll(s8, 8)
    ws = _window_select(_lane_group(v_ext.shape), s2, s4, s8, s16)[HALO:]
    return ws / _pool_count(t0, tb) - v_ext[HALO:]


def _sconv(q_ext, w):
    return (_roll(q_ext, 2) * w[0:1] + _roll(q_ext, 1) * w[1:2] + q_ext * w[2:3])[HALO:]


def _halo_specs(t, tb, cols, col_block):
    per = tb // HALO
    last = t // HALO - 1
    prev = pl.BlockSpec((HALO, cols), lambda i: (jnp.maximum(i * per - 1, 0), col_block))
    nxt = pl.BlockSpec((HALO, cols), lambda i: (jnp.minimum((i + 1) * per, last), col_block))
    return prev, nxt


def _f_ab(proj, pool_mat, pool_scale, sconv_w):
    t = proj.shape[0]
    tb = _tblock(t)
    prev, _ = _halo_specs(t, tb, 1024, 0)

    def body(p_ref, h_ref, pm_ref, ps_ref, sw_ref, ya_ref, yb_ref):
        i = pl.program_id(0)
        halo = jnp.where(i > 0, h_ref[...], 0.0)
        ext = jnp.concatenate([halo, p_ref[...]], axis=0)
        p = _pool_p(ext[:, 0:256], i * tb, tb)
        ya_ref[...] = _dot(p, pm_ref[...]) * ps_ref[...]
        q_ext = ext[:, 512:768] * ext[:, 768:1024]
        yb_ref[...] = p_ref[:, 256:512] * _sconv(q_ext, sw_ref[...])

    blk = pl.BlockSpec((tb, GW), lambda i: (i, 0))
    sds = jax.ShapeDtypeStruct((t, GW), F32)
    return pl.pallas_call(
        body, name="f_ab", grid=(t // tb,),
        in_specs=[pl.BlockSpec((tb, 1024), lambda i: (i, 0)), prev, _full((GW, GW)), _row(GW), _full((3, GW))],
        out_specs=[blk, blk], out_shape=[sds, sds], compiler_params=_cparams(1),
    )(proj, proj, pool_mat, pool_scale, sconv_w)


def _b_ab(proj, dya, dyb, pool_mat, pool_scale, sconv_w):
    t = proj.shape[0]
    tb = _tblock(t)
    nb = t // tb
    prev, nxt = _halo_specs(t, tb, 1024, 0)
    _, nxt_g = _halo_specs(t, tb, GW, 0)
    n_ext = tb + HALO

    def body(p_ref, hp_ref, hn_ref, da_ref, dan_ref, db_ref, dbn_ref, pm_ref, ps_ref, sw_ref,
             o_ref, dpm_ref, dps_ref, dsw_ref):
        i = pl.program_id(0)

        @pl.when(i == 0)
        def _():
            for r in (dpm_ref, dps_ref, dsw_ref):
                r[...] = jnp.zeros_like(r)

        last = i == nb - 1
        halo = jnp.where(i > 0, hp_ref[...], 0.0)
        main = p_ref[...]
        ext = jnp.concatenate([halo, main], axis=0)
        scale = ps_ref[...]
        pm = pm_ref[...]
        p = _pool_p(ext[:, 0:256], i * tb, tb)
        da = da_ref[...]
        dps_ref[...] += _colsum(da * _dot(p, pm))
        da_ext = jnp.concatenate([da, jnp.where(last, 0.0, dan_ref[...])], axis=0)
        dys = da_ext * scale
        dpm_ref[...] += _dot(p, dys[:tb], TN)
        dp = _dot(dys, pm, NT)
        dpc = dp / _pool_count(i * tb, n_ext)
        a2 = dpc + _roll(dpc, n_ext - 1)
        a4 = a2 + _roll(a2, n_ext - 2)
        a8 = a4 + _roll(a4, n_ext - 4)
        a16 = a8 + _roll(a8, n_ext - 8)
        o_ref[:, 0:256] = (_window_select(_lane_group(dpc.shape), a2, a4, a8, a16) - dp)[:tb]
        w = sw_ref[...]
        gb, gc, hh = main[:, 256:512], main[:, 512:768], main[:, 768:1024]
        q_ext = ext[:, 512:768] * ext[:, 768:1024]
        db = db_ref[...]
        o_ref[:, 256:512] = db * _sconv(q_ext, w)
        gb_next = hn_ref[:, 256:512]
        dconv = jnp.concatenate([db * gb, jnp.where(last, 0.0, dbn_ref[...] * gb_next)], axis=0)
        dq = (dconv * w[2:3] + _roll(dconv, n_ext - 1) * w[1:2] + _roll(dconv, n_ext - 2) * w[0:1])[:tb]
        o_ref[:, 512:768] = dq * hh
        o_ref[:, 768:1024] = dq * gc
        dc = dconv[:tb]
        dsw_ref[0:1, :] += _colsum(dc * _roll(q_ext, 2)[HALO:])
        dsw_ref[1:2, :] += _colsum(dc * _roll(q_ext, 1)[HALO:])
        dsw_ref[2:3, :] += _colsum(dc * q_ext[HALO:])

    blk = pl.BlockSpec((tb, GW), lambda i: (i, 0))
    return pl.pallas_call(
        body, name="b_ab", grid=(nb,),
        in_specs=[pl.BlockSpec((tb, 1024), lambda i: (i, 0)), prev, nxt, blk, nxt_g, blk, nxt_g,
                  _full((GW, GW)), _row(GW), _full((3, GW))],
        out_specs=[pl.BlockSpec((tb, 1024), lambda i: (i, 0)), _full((GW, GW)), _row(GW), _full((3, GW))],
        out_shape=[jax.ShapeDtypeStruct((t, 1024), F32), jax.ShapeDtypeStruct((GW, GW), F32),
                   jax.ShapeDtypeStruct((1, GW), F32), jax.ShapeDtypeStruct((3, GW), F32)],
        compiler_params=_cparams(1),
    )(proj, proj, proj, dya, dya, dyb, dyb, pool_mat, pool_scale, sconv_w)


CH = 8


def _ssd_conv(x, halo, w, b):
    ext = jnp.concatenate([halo, x], axis=0)
    pre = ext * w[3:4] + _roll(ext, 1) * w[2:3] + _roll(ext, 2) * w[1:2] + _roll(ext, 3) * w[0:1] + b
    return pre[CH:], ext


def _ssd_common(dt_raw, dtb, alog):
    ll = dt_raw.shape[0]
    dtv = _softplus(dt_raw + dtb)
    a_row = -jnp.exp(alog)
    r = lax.broadcasted_iota(jnp.int32, (ll, ll), 0)
    c = lax.broadcasted_iota(jnp.int32, (ll, ll), 1)
    tril = (r >= c).astype(F32)
    cs = _dot(tril, dtv * a_row, prec=HI)
    return dtv, a_row, cs, cs.T, r >= c


def _ssd_head(h, act, dtv, cs, cs_t, causal, gmat):
    g = h // 2
    xs = act[:, HP * h:HP * (h + 1)]
    bm = act[:, 256 + NS * g:256 + NS * (g + 1)]
    cm = act[:, 512 + NS * g:512 + NS * (g + 1)]
    cs_c = cs[:, h:h + 1]
    cs_r = cs_t[h:h + 1, :]
    mdec = jnp.where(causal, jnp.exp(jnp.minimum(cs_c - cs_r, 0.0)), 0.0)
    sc = gmat[g] * mdec
    dt_c = dtv[:, h:h + 1]
    xdt = xs * dt_c
    e = jnp.exp(cs_c)
    cs_last = cs[SSD_L - 1:SSD_L, h:h + 1]
    wdec = jnp.exp(cs_last - cs_c)
    return xs, bm, cm, cs_c, mdec, sc, dt_c, xdt, e, cs_last, wdec


def _f_ssd(proj, dtp, conv_w, conv_b, dt_bias, a_log, d_skip):
    t = proj.shape[0]
    nc = t // SSD_L
    per = SSD_L // CH

    def body(x_ref, hx_ref, dt_ref, z_ref, cw_ref, cb_ref, dtb_ref, al_ref, dk_ref, y_ref, yp_ref, sp_ref, s_ref):
        i = pl.program_id(0)

        @pl.when(i == 0)
        def _():
            s_ref[...] = jnp.zeros_like(s_ref)

        halo = jnp.where(i > 0, hx_ref[...], 0.0)
        pre, _ = _ssd_conv(x_ref[...], halo, cw_ref[...], cb_ref[...])
        act = _silu(pre)
        dtv, _, cs, cs_t, causal = _ssd_common(dt_ref[...], dtb_ref[...], al_ref[...])
        gmat = [_dot(act[:, 512 + NS * g:512 + NS * (g + 1)], act[:, 256 + NS * g:256 + NS * (g + 1)], NT) for g in range(2)]
        for h in range(NH):
            xs, bm, cm, _, _, sc, _, xdt, e, cs_last, wdec = _ssd_head(h, act, dtv, cs, cs_t, causal, gmat)
            prev = s_ref[h]
            sp_ref[0, h] = prev
            y = _dot(sc, xdt) + e * _dot(cm, prev, NT) + xs * dk_ref[0:1, h:h + 1]
            yp_ref[:, HP * h:HP * (h + 1)] = y
            s_ref[h] = prev * jnp.exp(cs_last) + _dot(xdt * wdec, bm, TN)
        y_ref[...] = yp_ref[...] * _silu(z_ref[...])

    blk = pl.BlockSpec((SSD_L, GW), lambda i: (i, 0))
    sds = jax.ShapeDtypeStruct((t, GW), F32)
    return pl.pallas_call(
        body, name="f_ssd", grid=(nc,),
        in_specs=[pl.BlockSpec((SSD_L, 768), lambda i: (i, 2)),
                  pl.BlockSpec((CH, 768), lambda i: (jnp.maximum(i * per - 1, 0), 2)),
                  pl.BlockSpec((SSD_L, DTW), lambda i: (i, 0)),
                  pl.BlockSpec((SSD_L, GW), lambda i: (i, 4)),
                  _full((4, 768)), _row(768), _row(DTW), _row(DTW), _row(DTW)],
        out_specs=[blk, blk, pl.BlockSpec((1, NH, HP, NS), lambda i: (i, 0, 0, 0))],
        out_shape=[sds, sds, jax.ShapeDtypeStruct((nc, NH, HP, NS), F32)],
        scratch_shapes=[pltpu.VMEM((NH, HP, NS), F32)], compiler_params=_cparams(1),
    )(proj, proj, dtp, proj, conv_w, conv_b, dt_bias, a_log, d_skip)


def _b_ssd(proj, dtp, ypre, dyc, sprev, conv_w, conv_b, dt_bias, a_log, d_skip):
    t = proj.shape[0]
    nc = t // SSD_L
    per = SSD_L // CH
    n_ext = SSD_L + CH

    def body(x_ref, hx_ref, dt_ref, z_ref, yp_ref, dy_ref, sp_ref, cw_ref, cb_ref, dtb_ref, al_ref, dk_ref,
             dz_ref, dx_ref, ddt_ref, dcw_ref, dcb_ref, ddtb_ref, dal_ref, ddk_ref, ds_ref, dnext_ref, dact_ref):
        i = pl.program_id(0)

        @pl.when(i == 0)
        def _():
            ds_ref[...] = jnp.zeros_like(ds_ref)
            dnext_ref[...] = jnp.zeros_like(dnext_ref)
            for r in (dcw_ref, dcb_ref, ddtb_ref, dal_ref, ddk_ref):
                r[...] = jnp.zeros_like(r)

        first_chunk = i == nc - 1
        halo = jnp.where(first_chunk, 0.0, hx_ref[...])
        w = cw_ref[...]
        pre, ext = _ssd_conv(x_ref[...], halo, w, cb_ref[...])
        act = _silu(pre)
        dt_raw = dt_ref[...]
        dtv, a_row, cs, cs_t, causal = _ssd_common(dt_raw, dtb_ref[...], al_ref[...])
        gmat = [_dot(act[:, 512 + NS * g:512 + NS * (g + 1)], act[:, 256 + NS * g:256 + NS * (g + 1)], NT) for g in range(2)]
        z = z_ref[...]
        dyc_v = dy_ref[...]
        dz_ref[...] = dyc_v * yp_ref[...] * _dsilu(z)
        dy_all = dyc_v * _silu(z)
        lane = lax.broadcasted_iota(jnp.int32, (SSD_L, DTW), 1)
        rowi = lax.broadcasted_iota(jnp.int32, (SSD_L, 1), 0)
        dcs_mat = jnp.zeros((SSD_L, DTW), F32)
        ddtx_mat = jnp.zeros((SSD_L, DTW), F32)
        ddk_row = jnp.zeros((1, DTW), F32)
        lane1 = lax.broadcasted_iota(jnp.int32, (1, DTW), 1)
        dbm = [None, None]
        dcm = [None, None]
        for h in range(NH):
            g = h // 2
            xs, bm, cm, _, mdec, sc, dt_c, xdt, e, cs_last, wdec = _ssd_head(h, act, dtv, cs, cs_t, causal, gmat)
            dy = dy_all[:, HP * h:HP * (h + 1)]
            prev = sp_ref[0, h]
            ds = ds_ref[h]
            dsc = _dot(dy, xdt, NT)
            q = dsc * sc
            dg = dsc * mdec
            dxdt = _dot(sc, dy, TN)
            dcs = jnp.sum(q, axis=1, keepdims=True) - jnp.sum(q.T, axis=1, keepdims=True)
            dc_h = _dot(dg, bm)
            db_h = _dot(dg, cm, TN)
            cp = _dot(cm, prev, NT)
            dcs += jnp.sum(dy * cp, axis=1, keepdims=True) * e
            ey = e * dy
            dc_h += _dot(ey, prev)
            dprev = _dot(ey, cm, TN)
            elast = jnp.exp(cs_last)
            dprev += ds * elast
            dcs_last = jnp.sum(ds * prev, keepdims=True) * elast
            bds = _dot(bm, ds, NT)
            dxdt += wdec * bds
            db_h += wdec * _dot(xdt, ds)
            dw = jnp.sum(xdt * bds, axis=1, keepdims=True) * wdec
            dcs -= dw
            dcs_last += jnp.sum(dw, keepdims=True)
            dcs += jnp.where(rowi == SSD_L - 1, dcs_last, 0.0)
            ds_ref[h] = dprev
            dact_ref[:, HP * h:HP * (h + 1)] = dxdt * dt_c + dy * dk_ref[0:1, h:h + 1]
            dcs_mat = jnp.where(lane == h, dcs, dcs_mat)
            ddtx_mat = jnp.where(lane == h, jnp.sum(dxdt * xs, axis=1, keepdims=True), ddtx_mat)
            ddk_row = jnp.where(lane1 == h, jnp.sum(dy * xs, keepdims=True), ddk_row)
            dbm[g] = db_h if dbm[g] is None else dbm[g] + db_h
            dcm[g] = dc_h if dcm[g] is None else dcm[g] + dc_h
        for g in range(2):
            dact_ref[:, 256 + NS * g:256 + NS * (g + 1)] = dbm[g]
            dact_ref[:, 512 + NS * g:512 + NS * (g + 1)] = dcm[g]
        ddk_ref[...] += ddk_row
        r2 = lax.broadcasted_iota(jnp.int32, (SSD_L, SSD_L), 0)
        c2 = lax.broadcasted_iota(jnp.int32, (SSD_L, SSD_L), 1)
        dadt = _dot((c2 >= r2).astype(F32), dcs_mat, prec=HI)
        dal_ref[...] += _colsum(dadt * dtv) * a_row
        ddt = jnp.where(lane < NH, (dadt * a_row + ddtx_mat) * _sig(dt_raw + dtb_ref[...]), 0.0)
        ddt_ref[...] = ddt
        ddtb_ref[...] += _colsum(ddt)
        dpre = dact_ref[...] * _dsilu(pre)
        dcb_ref[...] += _colsum(dpre)
        for k in range(4):
            dcw_ref[k:k + 1, :] += _colsum(dpre * _roll(ext, 3 - k)[CH:])
        dext = jnp.concatenate([dpre, dnext_ref[...]], axis=0)
        dx_ref[...] = (dext * w[3:4] + _roll(dext, n_ext - 1) * w[2:3] + _roll(dext, n_ext - 2) * w[1:2]
                       + _roll(dext, n_ext - 3) * w[0:1])[:SSD_L]
        dnext_ref[...] = dpre[0:CH]

    rev = lambda i: nc - 1 - i
    blk = lambda n, cb=0: pl.BlockSpec((SSD_L, n), lambda i: (rev(i), cb))
    row = lambda n: jax.ShapeDtypeStruct((1, n), F32)
    return pl.pallas_call(
        body, name="b_ssd", grid=(nc,),
        in_specs=[blk(768, 2), pl.BlockSpec((CH, 768), lambda i: (jnp.maximum(rev(i) * per - 1, 0), 2)),
                  blk(DTW), blk(GW, 4), blk(GW), blk(GW), pl.BlockSpec((1, NH, HP, NS), lambda i: (rev(i), 0, 0, 0)),
                  _full((4, 768)), _row(768), _row(DTW), _row(DTW), _row(DTW)],
        out_specs=[blk(GW), blk(768), blk(DTW), _full((4, 768)), _row(768), _row(DTW), _row(DTW), _row(DTW)],
        out_shape=[jax.ShapeDtypeStruct((t, GW), F32), jax.ShapeDtypeStruct((t, 768), F32), jax.ShapeDtypeStruct((t, DTW), F32),
                   jax.ShapeDtypeStruct((4, 768), F32), row(768), row(DTW), row(DTW), row(DTW)],
        scratch_shapes=[pltpu.VMEM((NH, HP, NS), F32), pltpu.VMEM((CH, 768), F32), pltpu.VMEM((SSD_L, 768), F32)],
        compiler_params=_cparams(1),
    )(proj, proj, dtp, proj, ypre, dyc, sprev, conv_w, conv_b, dt_bias, a_log, d_skip)


def _s5_block(t):
    return min(t, 256)


def _seg_t():
    r = lax.broadcasted_iota(jnp.int32, (64, 1024), 0)
    c = lax.broadcasted_iota(jnp.int32, (64, 1024), 1)
    return (c // 16 == r).astype(F32)


def _s5_prep_math(a_re, a_im, lstep, b_re, b_im):
    step = jnp.exp(lstep)
    ars = a_re * step
    ais = a_im * step
    mag = jnp.exp(ars)
    lr = mag * jnp.cos(ais)
    li = mag * jnp.sin(ais)
    den = a_re * a_re + a_im * a_im
    nr = lr - 1.0
    f_re = (nr * a_re + li * a_im) / den
    f_im = (li * a_re - nr * a_im) / den
    seg = _seg_t()
    fr = _dot(f_re, seg, prec=HI)
    fi = _dot(f_im, seg, prec=HI)
    return lr, li, fr * b_re - fi * b_im, fr * b_im + fi * b_re, ars, ais


def _s5_prep(a_re, a_im, lstep, b_re, b_im):
    def body(ar, ai, ls, br, bi, lr_o, li_o, bbr_o, bbi_o, ars_o, ais_o):
        outs = _s5_prep_math(ar[...], ai[...], ls[...], br[...], bi[...])
        for o, v in zip((lr_o, li_o, bbr_o, bbi_o, ars_o, ais_o), outs):
            o[...] = v

    s64 = jax.ShapeDtypeStruct((16, 64), F32)
    s1k = jax.ShapeDtypeStruct((16, 1024), F32)
    return pl.pallas_call(body, name="s5_prep", out_shape=[s64, s64, s1k, s1k, s64, s64])(a_re, a_im, lstep, b_re, b_im)


def _s5_prep_bwd(a_re, a_im, lstep, b_re, b_im, dlr, dli, dbbr, dbbi):
    def body(ar, ai, ls, br, bi, g0, g1, g2, g3, o0, o1, o2, o3, o4):
        f = lambda *a: _s5_prep_math(*a)[:4]
        _, vjp = jax.vjp(f, ar[...], ai[...], ls[...], br[...], bi[...])
        for o, v in zip((o0, o1, o2, o3, o4), vjp((g0[...], g1[...], g2[...], g3[...]))):
            o[...] = v

    s64 = jax.ShapeDtypeStruct((16, 64), F32)
    s1k = jax.ShapeDtypeStruct((16, 1024), F32)
    return pl.pallas_call(body, name="s5_prep_bwd", out_shape=[s64, s64, jax.ShapeDtypeStruct((16, 1), F32), s1k, s1k])(
        a_re, a_im, lstep, b_re, b_im, dlr, dli, dbbr, dbbi)


def _s5_tables(ars, ais, lb):
    def body(ar, ai, pr, pi, qr, qi):
        row = lax.broadcasted_iota(jnp.int32, (lb, S5_P), 0).astype(F32)
        for n, o_r, o_i in ((row + 1.0, pr, pi), (float(lb) - row, qr, qi)):
            mag = jnp.exp(n * ar[...])
            o_r[...] = mag * jnp.cos(n * ai[...])
            o_i[...] = mag * jnp.sin(n * ai[...])

    sds = jax.ShapeDtypeStruct((lb, S5_P), F32)
    return pl.pallas_call(body, name="s5_tables", out_shape=[sds] * 4)(ars, ais)


def _s5_scan(bu_r, bu_i, p_r, p_i, c_r, c_i, lb):
    row = lax.broadcasted_iota(jnp.int32, (lb, S5_P), 0)
    sr, si = bu_r, bu_i
    k = 1
    while k < lb:
        lr, li = p_r[k - 1:k, :], p_i[k - 1:k, :]
        tr = jnp.where(row >= k, _roll(sr, k), 0.0)
        ti = jnp.where(row >= k, _roll(si, k), 0.0)
        sr, si = sr + lr * tr - li * ti, si + lr * ti + li * tr
        k *= 2
    pr, pi = p_r[...], p_i[...]
    return sr + pr * c_r - pi * c_i, si + pr * c_i + pi * c_r


def _s5_y(u, sr, si, cre, cim, dsk):
    return _dot(sr, cre) + _dot(si, cim) + dsk * u


def _f_s5(proj, bmat, cre, cim, p_r, p_i, dsk, glu_w, glu_b):
    t = proj.shape[0]
    lb = _s5_block(t)
    nb = t // lb

    def body(u_ref, bm_ref, cr_ref, ci_ref, pr_ref, pi_ref, dk_ref, gw_ref, gb_ref, y_ref, car_ref, st_ref):
        @pl.when(pl.program_id(0) == 0)
        def _():
            st_ref[...] = jnp.zeros_like(st_ref)

        u = u_ref[...]
        bu = _dot(u, bm_ref[...])
        c_r, c_i = st_ref[0:1, 0:S5_P], st_ref[0:1, S5_P:]
        car_ref[0] = st_ref[0:1, :]
        sr, si = _s5_scan(bu[:, :S5_P], bu[:, S5_P:], pr_ref, pi_ref, c_r, c_i, lb)
        st_ref[0:1, 0:S5_P] = sr[lb - 1:lb]
        st_ref[0:1, S5_P:] = si[lb - 1:lb]
        gel = _gelu(_s5_y(u, sr, si, cr_ref[...], ci_ref[...], dk_ref[...]))
        y_ref[...] = gel * _sig(_dot(gel, gw_ref[...]) + gb_ref[...])

    return pl.pallas_call(
        body, name="f_s5", grid=(nb,),
        in_specs=[pl.BlockSpec((lb, GW), lambda i: (i, 5)),
                  _full((GW, 2 * S5_P)), _full((S5_P, GW)), _full((S5_P, GW)), _full((lb, S5_P)), _full((lb, S5_P)),
                  _row(GW), _full((GW, GW)), _row(GW)],
        out_specs=[pl.BlockSpec((lb, GW), lambda i: (i, 0)), pl.BlockSpec((1, 1, 2 * S5_P), lambda i: (i, 0, 0))],
        out_shape=[jax.ShapeDtypeStruct((t, GW), F32), jax.ShapeDtypeStruct((nb, 1, 2 * S5_P), F32)],
        scratch_shapes=[pltpu.VMEM((8, 2 * S5_P), F32)], compiler_params=_cparams(1),
    )(proj, bmat, cre, cim, p_r, p_i, dsk, glu_w, glu_b)


def _b_s5(proj, dyd, carries, bmat, cre, cim, p_r, p_i, q_r, q_i, dsk, glu_w, glu_b):
    t = proj.shape[0]
    lb = _s5_block(t)
    nb = t // lb

    def body(u_ref, dy_ref, car_ref, bm_ref, cr_ref, ci_ref, pr_ref, pi_ref, qr_ref, qi_ref, dk_ref, gw_ref, gb_ref,
             du_ref, dbm_ref, dcr_ref, dci_ref, dlam_ref, ddk_ref, dgw_ref, dgb_ref, gc_ref):
        @pl.when(pl.program_id(0) == 0)
        def _():
            gc_ref[...] = jnp.zeros_like(gc_ref)
            for r in (dbm_ref, dcr_ref, dci_ref, dlam_ref, ddk_ref, dgw_ref, dgb_ref):
                r[...] = jnp.zeros_like(r)

        u = u_ref[...]
        bm = bm_ref[...]
        bu = _dot(u, bm)
        c_r, c_i = car_ref[0, 0:1, 0:S5_P], car_ref[0, 0:1, S5_P:]
        sr, si = _s5_scan(bu[:, :S5_P], bu[:, S5_P:], pr_ref, pi_ref, c_r, c_i, lb)
        cre_v, cim_v, dk, gw = cr_ref[...], ci_ref[...], dk_ref[...], gw_ref[...]
        y = _s5_y(u, sr, si, cre_v, cim_v, dk)
        gel = _gelu(y)
        gate = _sig(_dot(gel, gw) + gb_ref[...])
        dout = dy_ref[...]
        t1 = dout * gel * gate * (1.0 - gate)
        dgw_ref[...] += _dot(gel, t1, TN)
        dgb_ref[...] += _colsum(t1)
        dyv = (dout * gate + _dot(t1, gw, NT)) * _dgelu(y)
        ddk_ref[...] += _colsum(dyv * u)
        dcr_ref[...] += _dot(sr, dyv, TN)
        dci_ref[...] += _dot(si, dyv, TN)
        gr = _dot(dyv, cre_v, NT)
        gi = _dot(dyv, cim_v, NT)
        row = lax.broadcasted_iota(jnp.int32, (lb, S5_P), 0)
        k = 1
        while k < lb:
            lr, li = pr_ref[k - 1:k, :], pi_ref[k - 1:k, :]
            tr = jnp.where(row < lb - k, _roll(gr, lb - k), 0.0)
            ti = jnp.where(row < lb - k, _roll(gi, lb - k), 0.0)
            gr, gi = gr + lr * tr + li * ti, gi + lr * ti - li * tr
            k *= 2
        qr, qi = qr_ref[...], qi_ref[...]
        n_r, n_i = gc_ref[0:1, 0:S5_P], gc_ref[0:1, S5_P:]
        gr, gi = gr + qr * n_r + qi * n_i, gi + qr * n_i - qi * n_r
        gc_ref[0:1, 0:S5_P] = gr[0:1]
        gc_ref[0:1, S5_P:] = gi[0:1]
        gcat = jnp.concatenate([gr, gi], axis=1)
        dbm_ref[...] += _dot(u, gcat, TN)
        du_ref[...] = dyv * dk + _dot(gcat, bm, NT)
        spr = jnp.where(row >= 1, _roll(sr, 1), c_r)
        spi = jnp.where(row >= 1, _roll(si, 1), c_i)
        dlam_ref[0:1, :] += _colsum(gr * spr + gi * spi)
        dlam_ref[1:2, :] += _colsum(gi * spr - gr * spi)

    rev = lambda i: nb - 1 - i
    return pl.pallas_call(
        body, name="b_s5", grid=(nb,),
        in_specs=[pl.BlockSpec((lb, GW), lambda i: (rev(i), 5)), pl.BlockSpec((lb, GW), lambda i: (rev(i), 0)),
                  pl.BlockSpec((1, 1, 2 * S5_P), lambda i: (rev(i), 0, 0)),
                  _full((GW, 2 * S5_P)), _full((S5_P, GW)), _full((S5_P, GW)), _full((lb, S5_P)), _full((lb, S5_P)),
                  _full((lb, S5_P)), _full((lb, S5_P)), _row(GW), _full((GW, GW)), _row(GW)],
        out_specs=[pl.BlockSpec((lb, GW), lambda i: (rev(i), 0)), _full((GW, 2 * S5_P)), _full((S5_P, GW)), _full((S5_P, GW)),
                   _full((2, S5_P)), _row(GW), _full((GW, GW)), _row(GW)],
        out_shape=[jax.ShapeDtypeStruct((t, GW), F32), jax.ShapeDtypeStruct((GW, 2 * S5_P), F32),
                   jax.ShapeDtypeStruct((S5_P, GW), F32), jax.ShapeDtypeStruct((S5_P, GW), F32),
                   jax.ShapeDtypeStruct((2, S5_P), F32), jax.ShapeDtypeStruct((1, GW), F32),
                   jax.ShapeDtypeStruct((GW, GW), F32), jax.ShapeDtypeStruct((1, GW), F32)],
        scratch_shapes=[pltpu.VMEM((8, 2 * S5_P), F32)], compiler_params=_cparams(1),
    )(proj, dyd, carries, bmat, cre, cim, p_r, p_i, q_r, q_i, dsk, glu_w, glu_b)


def _group_norm(ys, bw):
    outs, stats = [], []
    for g, y in enumerate(ys):
        r, n = _rms(y)
        stats.append((r, n))
        outs.append(n * bw[:, GW * g:GW * (g + 1)])
    return jnp.concatenate(outs, axis=1), stats


def _f_out(ya, yb, yc, yd, bw, w_out, h, g1):
    t = h.shape[0]
    tb = _tblock(t)

    def body(a_ref, b_ref, c_ref, d_ref, bw_ref, w_ref, h_ref, g_ref, h2_ref, o_ref, cat_ref):
        cat, _ = _group_norm([a_ref[...], b_ref[...], c_ref[...], d_ref[...]], bw_ref[...])
        catb = cat.astype(BF16)
        cat_ref[...] = catb
        o = _dot(catb, w_ref[...])
        o_ref[...] = o
        h2_ref[...] = h_ref[...] + g_ref[...] * o

    yblk = pl.BlockSpec((tb, GW), lambda i: (i, 0))
    blk = pl.BlockSpec((tb, D), lambda i: (i, 0))
    return pl.pallas_call(
        body, name="f_out", grid=(t // tb,), in_specs=[yblk] * 4 + [_row(D), _full((D, D)), blk, _row(D)],
        out_specs=[blk, blk, blk],
        out_shape=[jax.ShapeDtypeStruct((t, D), F32), jax.ShapeDtypeStruct((t, D), F32), jax.ShapeDtypeStruct((t, D), BF16)],
        compiler_params=_cparams(1),
    )(ya, yb, yc, yd, bw, w_out, h, g1)


def _b_out(dh2, ya, yb, yc, yd, bw, w_out, g1):
    t = dh2.shape[0]
    tb = _tblock(t)

    def body(dh_ref, a_ref, b_ref, c_ref, d_ref, bw_ref, w_ref, g_ref, da_ref, db_ref, dc_ref, dd_ref, do_ref, dbw_ref):
        @pl.when(pl.program_id(0) == 0)
        def _():
            dbw_ref[...] = jnp.zeros_like(dbw_ref)

        do = (dh_ref[...] * g_ref[...]).astype(BF16)
        do_ref[...] = do
        dcat = _dot(do, w_ref[...], NT)
        bw_v = bw_ref[...]
        for g, (y_ref, dy_ref) in enumerate(((a_ref, da_ref), (b_ref, db_ref), (c_ref, dc_ref), (d_ref, dd_ref))):
            r, n = _rms(y_ref[...])
            dc = dcat[:, GW * g:GW * (g + 1)]
            dbw_ref[:, GW * g:GW * (g + 1)] += _colsum(dc * n)
            dy_ref[...] = _rms_bwd(r, n, dc * bw_v[:, GW * g:GW * (g + 1)])

    yblk = pl.BlockSpec((tb, GW), lambda i: (i, 0))
    blk = pl.BlockSpec((tb, D), lambda i: (i, 0))
    ysd = jax.ShapeDtypeStruct((t, GW), F32)
    return pl.pallas_call(
        body, name="b_out", grid=(t // tb,), in_specs=[blk] + [yblk] * 4 + [_row(D), _full((D, D)), _row(D)],
        out_specs=[yblk] * 4 + [blk, _row(D)],
        out_shape=[ysd] * 4 + [jax.ShapeDtypeStruct((t, D), BF16), jax.ShapeDtypeStruct((1, D), F32)],
        compiler_params=_cparams(1),
    )(dh2, ya, yb, yc, yd, bw, w_out, g1)


HB = 1024


def _f_mlp(h2, nw, sc, sh, g2, w1, w2):
    t = h2.shape[0]
    tb = _tblock(t)
    nk = HID // HB

    def body(h_ref, nw_ref, sc_ref, sh_ref, g_ref, w1_ref, w2_ref, h3_ref, m_ref, a_ref, v_ref):
        k = pl.program_id(1)

        @pl.when(k == 0)
        def _():
            _, n = _rms(h_ref[...])
            v_ref[...] = ((n * nw_ref[...]) * (1.0 + sc_ref[...]) + sh_ref[...]).astype(BF16)
            m_ref[...] = jnp.zeros_like(m_ref)

        a = _dot(v_ref[...], w1_ref[...])
        a_ref[...] = a
        ra = jnp.maximum(a, 0.0)
        m_ref[...] += _dot((ra * ra).astype(BF16), w2_ref[...])

        @pl.when(k == nk - 1)
        def _():
            h3_ref[...] = h_ref[...] + g_ref[...] * m_ref[...]

    blk = pl.BlockSpec((tb, D), lambda i, k: (i, 0))
    return pl.pallas_call(
        body, name="f_mlp", grid=(t // tb, nk),
        in_specs=[blk, _row(D), _row(D), _row(D), _row(D), pl.BlockSpec((D, HB), lambda i, k: (0, k)),
                  pl.BlockSpec((HB, D), lambda i, k: (k, 0))],
        out_specs=[blk, blk, pl.BlockSpec((tb, HB), lambda i, k: (i, k)), blk],
        out_shape=[jax.ShapeDtypeStruct((t, D), F32), jax.ShapeDtypeStruct((t, D), F32), jax.ShapeDtypeStruct((t, HID), F32),
                   jax.ShapeDtypeStruct((t, D), BF16)],
        compiler_params=_cparams(2),
    )(h2, nw, sc, sh, g2, w1, w2)


def _b_mlp(dh3, a, g2, w1, w2):
    t = dh3.shape[0]
    tb = _tblock(t)
    nk = HID // HB

    def body(dh_ref, a_ref, g_ref, w1_ref, w2_ref, dv_ref, da_ref, act_ref, dm_ref):
        k = pl.program_id(1)
        dm = (dh_ref[...] * g_ref[...]).astype(BF16)

        @pl.when(k == 0)
        def _():
            dm_ref[...] = dm
            dv_ref[...] = jnp.zeros_like(dv_ref)

        ra = jnp.maximum(a_ref[...], 0.0)
        act_ref[...] = (ra * ra).astype(BF16)
        da = (_dot(dm, w2_ref[...], NT) * (2.0 * ra)).astype(BF16)
        da_ref[...] = da
        dv_ref[...] += _dot(da, w1_ref[...], NT)

    blk = pl.BlockSpec((tb, D), lambda i, k: (i, 0))
    hblk = pl.BlockSpec((tb, HB), lambda i, k: (i, k))
    return pl.pallas_call(
        body, name="b_mlp", grid=(t // tb, nk),
        in_specs=[blk, hblk, _row(D), pl.BlockSpec((D, HB), lambda i, k: (0, k)), pl.BlockSpec((HB, D), lambda i, k: (k, 0))],
        out_specs=[blk, hblk, hblk, blk],
        out_shape=[jax.ShapeDtypeStruct((t, D), F32), jax.ShapeDtypeStruct((t, HID), BF16), jax.ShapeDtypeStruct((t, HID), BF16),
                   jax.ShapeDtypeStruct((t, D), BF16)],
        compiler_params=_cparams(2),
    )(dh3, a, g2, w1, w2)


def _b_final(h, tgt, fw):
    t = h.shape[0]
    tb = _tblock(t)

    def body(h_ref, t_ref, w_ref, dh_ref, loss_ref, dfw_ref):
        @pl.when(pl.program_id(0) == 0)
        def _():
            loss_ref[...] = jnp.zeros_like(loss_ref)
            dfw_ref[...] = jnp.zeros_like(dfw_ref)

        r, n = _rms(h_ref[...])
        wv = w_ref[...]
        err = n * wv - t_ref[...]
        loss_ref[...] += jnp.sum(err * err, keepdims=True) * (0.5 / D)
        dy = err * (1.0 / D)
        dfw_ref[...] += _colsum(dy * n)
        dh_ref[...] = _rms_bwd(r, n, dy * wv)

    blk = pl.BlockSpec((tb, D), lambda i: (i, 0))
    return pl.pallas_call(
        body, name="b_final", grid=(t // tb,), in_specs=[blk, blk, _row(D)], out_specs=[blk, _row(1), _row(D)],
        out_shape=[jax.ShapeDtypeStruct((t, D), F32), jax.ShapeDtypeStruct((1, 1), F32), jax.ShapeDtypeStruct((1, D), F32)],
        compiler_params=_cparams(1),
    )(h, tgt, fw)


_EYE16 = None


def _eye(n):
    return jnp.eye(n, dtype=F32)


def _pool_embed(pool_w):
    return jnp.einsum('gcd,gk->gckd', pool_w, _eye(4)).reshape(GW, GW)


def _pool_extract(m):
    return jnp.einsum('gcgd->gcd', m.reshape(4, 64, 4, 64))


def _bmat_embed(bb):
    return jnp.einsum('gph,gk->ghkp', bb, _eye(16)).reshape(GW, S5_P)


def _bmat_extract(m):
    return jnp.einsum('ghgp->gph', m.reshape(16, 16, 16, 64))


def _cmat_embed(cc):
    return jnp.einsum('ghp,gk->kpgh', cc, _eye(16)).reshape(S5_P, GW)


def _cmat_extract(m):
    return jnp.einsum('gpgh->ghp', m.reshape(16, 64, 16, 16))


def _pad_lanes(v, n=DTW):
    return jnp.pad(v.reshape(1, -1), ((0, 0), (0, n - v.shape[-1])))


def _layer_params(p, l, mod):
    q = {}
    q['mod'] = [mod[k:k + 1] for k in range(6)]
    q['nw1'] = p['norm_mix_w'][l:l + 1]
    q['nw2'] = p['norm_mlp_w'][l:l + 1]
    w_in = p['w_in'][l]
    q['w_main'] = jnp.concatenate([w_in[:, :1280], w_in[:, 2052:2308], w_in[:, 1280:2048]], axis=1)
    q['w_dt'] = jnp.pad(w_in[:, 2048:2052], ((0, 0), (0, DTW - 4)))
    q['pool_mat'] = _pool_embed(p['pool_w'][l])
    q['pool_scale'] = p['pool_scale'][l:l + 1]
    q['sconv_w'] = p['sconv_w'][l]
    q['conv_w'] = p['ssd_conv_w'][l]
    q['conv_b'] = p['ssd_conv_b'][l:l + 1]
    q['dt_bias'] = _pad_lanes(p['ssd_dt_bias'][l])
    q['a_log'] = _pad_lanes(p['ssd_a_log'][l])
    q['ssd_d'] = _pad_lanes(p['ssd_d'][l])
    q['s5_raw'] = (p['s5_a_re'][l], p['s5_a_im'][l], p['s5_log_step'][l].reshape(16, 1),
                   p['s5_b_re'][l].reshape(16, 1024), p['s5_b_im'][l].reshape(16, 1024))
    q['cre'] = _cmat_embed(p['s5_c_re'][l])
    q['cim'] = -_cmat_embed(p['s5_c_im'][l])
    q['s5_d'] = p['s5_d'][l:l + 1]
    q['glu_w'] = p['s5_glu_w'][l]
    q['glu_b'] = p['s5_glu_b'][l:l + 1]
    q['bw'] = p['branch_norm_w'][l:l + 1]
    q['w_out'] = p['w_out'][l]
    q['w1'] = p['mlp_w1'][l]
    q['w2'] = p['mlp_w2'][l]
    return q


def _layer_fwd(h, q):
    sh1, sc1, g1, sh2, sc2, g2 = q['mod']
    t = h.shape[0]
    s = {'h': h}
    s['proj'], s['dtp'], s['u'] = _f_in(h, q['nw1'], sc1, sh1, q['w_main'], q['w_dt'])
    s['ya'], s['yb'] = _f_ab(s['proj'], q['pool_mat'], q['pool_scale'], q['sconv_w'])
    s['yc'], s['ypre'], s['sprev'] = _f_ssd(s['proj'], s['dtp'], q['conv_w'], q['conv_b'], q['dt_bias'], q['a_log'], q['ssd_d'])
    lr, li, bbr, bbi, ars, ais = _s5_prep(*q['s5_raw'])
    s['bmat'] = jnp.concatenate([_bmat_embed(bbr.reshape(16, 64, 16)), _bmat_embed(bbi.reshape(16, 64, 16))], axis=1)
    s['tables'] = _s5_tables(ars.reshape(1, S5_P), ais.reshape(1, S5_P), _s5_block(t))
    s['yd'], s['carries'] = _f_s5(s['proj'], s['bmat'], q['cre'], q['cim'], s['tables'][0], s['tables'][1],
                                  q['s5_d'], q['glu_w'], q['glu_b'])
    s['h2'], s['o'], s['cat'] = _f_out(s['ya'], s['yb'], s['yc'], s['yd'], q['bw'], q['w_out'], h, g1)
    h3, s['m'], s['a'], s['v'] = _f_mlp(s['h2'], q['nw2'], sc2, sh2, g2, q['w1'], q['w2'])
    return h3, s


def _layer_bwd(dh3, q, s):
    sh1, sc1, g1, sh2, sc2, g2 = q['mod']
    g = {}
    dv, da, act, dm = _b_mlp(dh3, s['a'], g2, q['w1'], q['w2'])
    g['mlp_w1'] = _tn_matmul(s['v'], da, "dw1", col_major=True)
    g['mlp_w2'] = _tn_matmul(act, dm, "dw2")
    dh2, dsc2, dsh2, dnw2, dg2 = _b_normmod(dv, s['h2'], dh3, s['m'], q['nw2'], sc2, "b_norm_mlp")
    dya, dyb, dyc, dyd, do, dbw = _b_out(dh2, s['ya'], s['yb'], s['yc'], s['yd'], q['bw'], q['w_out'], g1)
    g['w_out'] = _tn_matmul(s['cat'], do, "dwout")
    g['branch_norm_w'] = dbw[0]
    dab, dpm, dps, dsw = _b_ab(s['proj'], dya, dyb, q['pool_mat'], q['pool_scale'], q['sconv_w'])
    g['pool_w'] = _pool_extract(dpm)
    g['pool_scale'] = dps[0]
    g['sconv_w'] = dsw
    dz, dxbc, ddt, dcw, dcb, ddtb, dal, ddk = _b_ssd(s['proj'], s['dtp'], s['ypre'], dyc, s['sprev'], q['conv_w'],
                                                     q['conv_b'], q['dt_bias'], q['a_log'], q['ssd_d'])
    g['ssd_conv_w'] = dcw
    g['ssd_conv_b'] = dcb[0]
    g['ssd_dt_bias'] = ddtb[0, :4]
    g['ssd_a_log'] = dal[0, :4]
    g['ssd_d'] = ddk[0, :4]
    tb = s['tables']
    ds5, dbmat, dcre, dcim, dlam, dd5, dgw, dgb = _b_s5(s['proj'], dyd, s['carries'], s['bmat'], q['cre'], q['cim'],
                                                        tb[0], tb[1], tb[2], tb[3], q['s5_d'], q['glu_w'], q['glu_b'])
    g['s5_c_re'] = _cmat_extract(dcre)
    g['s5_c_im'] = -_cmat_extract(dcim)
    g['s5_d'] = dd5[0]
    g['s5_glu_w'] = dgw
    g['s5_glu_b'] = dgb[0]
    dbbr = _bmat_extract(dbmat[:, :S5_P]).reshape(16, 1024)
    dbbi = _bmat_extract(dbmat[:, S5_P:]).reshape(16, 1024)
    dar, dai, dls, dbr, dbi = _s5_prep_bwd(*q['s5_raw'], dlam[0].reshape(16, 64), dlam[1].reshape(16, 64), dbbr, dbbi)
    g['s5_a_re'], g['s5_a_im'], g['s5_log_step'] = dar, dai, dls[:, 0]
    g['s5_b_re'], g['s5_b_im'] = dbr.reshape(16, 64, 16), dbi.reshape(16, 64, 16)
    du = _b_in_du(dab, dz, dxbc, ds5, ddt, q['w_main'], q['w_dt'])
    u = s['u']
    pieces = [_tn_matmul(u, dab, "dwin_ab"), _tn_matmul(u, dz, "dwin_z"), _tn_matmul(u, dxbc, "dwin_xbc"),
              _tn_matmul(u, ddt, "dwin_dt")[:, :4], _tn_matmul(u, ds5, "dwin_s5")]
    g['w_in'] = jnp.concatenate(pieces, axis=1)
    dh, dsc1, dsh1, dnw1, dg1 = _b_normmod(du, s['h'], dh2, s['o'], q['nw1'], sc1, "b_norm_mix")
    g['norm_mix_w'] = dnw1[0]
    g['norm_mlp_w'] = dnw2[0]
    dmod = jnp.concatenate([dsh1, dsc1, dg1, dsh2, dsc2, dg2], axis=1)
    return dh, g, dmod


def _local_step(x, tgt, p, mod):
    qs = [_layer_params(p, l, mod[l]) for l in range(2)]
    h = x
    saved = []
    for l in range(2):
        h, s = _layer_fwd(h, qs[l])
        saved.append(s)
    dh, loss, dfw = _b_final(h, tgt, p['final_norm_w'].reshape(1, D))
    grads = [None, None]
    dmods = [None, None]
    for l in (1, 0):
        dh, grads[l], dmods[l] = _layer_bwd(dh, qs[l], saved[l])
    out = {k: jnp.stack([grads[0][k], grads[1][k]]) for k in grads[0]}
    out['final_norm_w'] = dfw[0]
    return loss, dh, out, jnp.concatenate(dmods, axis=0)


def _pack(arrs):
    parts, rows = [], 0
    for a in arrs:
        f = a.reshape(-1).astype(F32)
        pad = (-f.shape[0]) % 1024
        f = jnp.pad(f, (0, pad)) if pad else f
        parts.append(f.reshape(-1, 128))
        rows += parts[-1].shape[0]
    if rows % 256:
        parts.append(jnp.zeros((256 - rows % 256, 128), F32))
    return jnp.concatenate(parts, axis=0)


def _unpack(buf, shapes):
    out, row = [], 0
    for shp in shapes:
        n = int(math.prod(shp)) if len(shp) else 1
        rows = (n + 1023) // 1024 * 8
        out.append(buf[row:row + rows].reshape(-1)[:n].reshape(shp))
        row += rows
    return out


def _shard_of(a, axis, k):
    n = a.shape[axis] // 4
    return lax.dynamic_slice_in_dim(a, k * n, n, axis)


def kernel(x, c, norm_mix_w, norm_mlp_w, ada_w, ada_b, w_in, pool_w, pool_scale, sconv_w, ssd_conv_w, ssd_conv_b, ssd_dt_bias, ssd_a_log, ssd_d, s5_a_re, s5_a_im, s5_log_step, s5_b_re, s5_b_im, s5_c_re, s5_c_im, s5_d, s5_glu_w, s5_glu_b, branch_norm_w, w_out, mlp_w1, mlp_w2, final_norm_w, loss_target, m_norm_mix_w, m_norm_mlp_w, m_ada_w, m_ada_b, m_w_in, m_pool_w, m_pool_scale, m_sconv_w, m_ssd_conv_w, m_ssd_conv_b, m_ssd_dt_bias, m_ssd_a_log, m_ssd_d, m_s5_a_re, m_s5_a_im, m_s5_log_step, m_s5_b_re, m_s5_b_im, m_s5_c_re, m_s5_c_im, m_s5_d, m_s5_glu_w, m_s5_glu_b, m_branch_norm_w, m_w_out, m_mlp_w1, m_mlp_w2, m_final_norm_w, v_norm_mix_w, v_norm_mlp_w, v_ada_w, v_ada_b, v_w_in, v_pool_w, v_pool_scale, v_sconv_w, v_ssd_conv_w, v_ssd_conv_b, v_ssd_dt_bias, v_ssd_a_log, v_ssd_d, v_s5_a_re, v_s5_a_im, v_s5_log_step, v_s5_b_re, v_s5_b_im, v_s5_c_re, v_s5_c_im, v_s5_d, v_s5_glu_w, v_s5_glu_b, v_branch_norm_w, v_w_out, v_mlp_w1, v_mlp_w2, v_final_norm_w):
    loc = locals()
    w = {n: loc[n] for n in WEIGHTS}
    mom = {n: loc['m_' + n] for n in WEIGHTS}
    var = {n: loc['v_' + n] for n in WEIGHTS}
    ix, iy, ic = lax.axis_index("x"), lax.axis_index("y"), lax.axis_index("c")
    chip = 2 * ix + iy
    dev = 4 * ix + 2 * iy + ic

    (c_all,) = _exchange([c], EVERYONE, False, "ag_cond")
    c_all = c_all.reshape(8, D)
    small_sh = _exchange([w[n] for n in SMALL_SHARDED], CHIPS, False, "ag_small")
    mine_of = lambda a: lax.dynamic_index_in_dim(a.astype(BF16), ic, axis=0, keepdims=False)
    big_l = _exchange([mine_of(w['w_in']).reshape(577, D), mine_of(w['w_out']), mine_of(w['mlp_w1']), mine_of(w['mlp_w2'])],
                      CHIPS, False, "ag_big")
    big_sh = _exchange(big_l, SIBLING, False, "ag_bigpair")
    p = dict(w)
    for n, g in zip(SMALL_SHARDED, small_sh):
        ax = SMALL_SHARDED[n]
        p[n] = jnp.concatenate([g[k] for k in range(4)], axis=ax)
    w_in_sh = big_sh[0].reshape(2, 4, D, 577)
    p['w_in'] = jnp.concatenate([w_in_sh[:, k] for k in range(4)], axis=2)
    p['w_out'] = big_sh[1].reshape(2, D, D)
    p['mlp_w1'] = jnp.concatenate([big_sh[2][:, k] for k in range(4)], axis=2)
    p['mlp_w2'] = big_sh[3].reshape(2, HID, D)

    ada_b_sh = _shard_of(w['ada_b'], 1, chip).reshape(2, 1, 6 * D // 4)
    mod_sh = _ada_fwd(c_all, w['ada_w'], ada_b_sh)
    (mod_all,) = _exchange([mod_sh], CHIPS, False, "ag_mod")
    mine = lax.dynamic_index_in_dim(mod_all, dev, axis=2, keepdims=False)
    mod = jnp.transpose(mine, (1, 0, 2)).reshape(2, 6, D)

    loss, grad_x, g, dmod = _local_step(x[0], loss_target[0], p, mod)

    (dmod_all,) = _exchange([dmod], EVERYONE, False, "ag_dmod")
    dmod_all = jnp.transpose(dmod_all, (1, 0, 2))
    g_ada_w, g_ada_b = _ada_bwd(c_all, _shard_of(dmod_all, 2, chip), dmod_all)

    gw_in = jnp.transpose(g['w_in'].reshape(2, D, 4, 577), (0, 2, 1, 3)).reshape(2, 4, 577, D)
    gw_out = g['w_out'].reshape(2, 4, 256, D)
    gw1 = g['mlp_w1']
    gw2 = g['mlp_w2'].reshape(2, 4, 1024, D)
    pair = _exchange([gw_in, gw_out, gw1, gw2], SIBLING, True, "rs_pair")
    pair = [_sum_lead(a, "rs_pair_sum%d" % k, BF16) for k, a in enumerate(pair)]
    quad = _exchange(pair, CHIPS, True, "rs_chips")
    quad = [_sum_lead(a, "rs_chip_sum%d" % k, F32) for k, a in enumerate(quad)]
    both = _exchange(quad, SIBLING, False, "ag_pair")
    both[0] = both[0].reshape(2, D, 577)
    red = dict(zip(('w_in', 'w_out', 'mlp_w1', 'mlp_w2'), both))
    red['ada_w'] = g_ada_w

    small_names = [n for n in WEIGHTS if n not in BIG and n != 'ada_b']
    small_shapes = [g[n].shape for n in small_names] + [(1, 1)]
    packed = _pack([g[n] for n in small_names] + [loss])
    (packed_all,) = _exchange([packed], EVERYONE, False, "ag_smallgrad")
    summed = _unpack(_sum_lead(packed_all, "smallgrad_sum", F32), small_shapes)
    for n, a in zip(small_names, summed[:-1]):
        red[n] = _shard_of(a, SMALL_SHARDED[n], chip) if n in SMALL_SHARDED else a
    red['ada_b'] = g_ada_b
    loss_out = summed[-1].reshape(())

    delta, new_m, new_v = {}, {}, {}
    for n in BIG:
        delta[n], new_m[n], new_v[n] = _adamw(w[n], red[n], mom[n], var[n], "adamw_" + n)
    rest = [n for n in WEIGHTS if n not in BIG]
    shapes = [w[n].shape for n in rest]
    d_p, m_p, v_p = _adamw(_pack([w[n] for n in rest]), _pack([red[n] for n in rest]), _pack([mom[n] for n in rest]),
                           _pack([var[n] for n in rest]), "adamw_small")
    for n, a, b, cc in zip(rest, _unpack(d_p, shapes), _unpack(m_p, shapes), _unpack(v_p, shapes)):
        delta[n], new_m[n], new_v[n] = a, b, cc

    return (loss_out, grad_x[None], *[red[n] for n in WEIGHTS], *[delta[n] for n in WEIGHTS],
            *[new_m[n] for n in WEIGHTS], *[new_v[n] for n in WEIGHTS])
```

```python
import functools
import math

import jax
import jax.numpy as jnp
from jax import lax
from jax.experimental import pallas as pl
from jax.experimental.pallas import tpu as pltpu

F32 = jnp.float32
BF16 = jnp.bfloat16
HI = lax.Precision.HIGHEST

D = 1024
GW = 256
HID = 4096
EPS = 1e-6
PW = 2304
DTW = 128
SSD_L = 128
NH, HP, NS = 4, 64, 128
S5_P = 1024
MESH = pl.DeviceIdType.MESH

ADAM_LR, ADAM_B1, ADAM_B2, ADAM_EPS, ADAM_WD, ADAM_STEP = 0.001, 0.9, 0.999, 1e-08, 0.01, 10

NT = (((1,), (1,)), ((), ()))
TN = (((0,), (0,)), ((), ()))

WEIGHTS = ['norm_mix_w', 'norm_mlp_w', 'ada_w', 'ada_b', 'w_in', 'pool_w', 'pool_scale', 'sconv_w', 'ssd_conv_w',
           'ssd_conv_b', 'ssd_dt_bias', 'ssd_a_log', 'ssd_d', 's5_a_re', 's5_a_im', 's5_log_step', 's5_b_re', 's5_b_im',
           's5_c_re', 's5_c_im', 's5_d', 's5_glu_w', 's5_glu_b', 'branch_norm_w', 'w_out', 'mlp_w1', 'mlp_w2',
           'final_norm_w']
BIG = ('ada_w', 'w_in', 'w_out', 'mlp_w1', 'mlp_w2')
SMALL_SHARDED = {'sconv_w': 2, 'ssd_conv_w': 2, 's5_glu_w': 1}


def _cparams(n_axes, vmem_mb=48):
    return pltpu.CompilerParams(dimension_semantics=("arbitrary",) * n_axes, vmem_limit_bytes=vmem_mb * 1024 * 1024)


def _row(n):
    return pl.BlockSpec((1, n), lambda *_: (0, 0))


def _full(shape):
    nd = len(shape)
    return pl.BlockSpec(tuple(shape), lambda *_: (0,) * nd)


def _dot(a, b, dims=None, prec=None):
    if dims is None:
        dims = (((a.ndim - 1,), (0,)), ((), ()))
    return lax.dot_general(a, b, dims, preferred_element_type=F32, precision=prec)


def _bdot(a, b, dims=None):
    return _dot(a.astype(BF16), b.astype(BF16), dims)


def _sig(x):
    return jax.nn.sigmoid(x)


def _silu(x):
    return x * _sig(x)


def _dsilu(x):
    s = _sig(x)
    return s * (1.0 + x * (1.0 - s))


def _softplus(x):
    return jnp.maximum(x, 0.0) + jnp.log(1.0 + jnp.exp(-jnp.abs(x)))


_GK = math.sqrt(2.0 / math.pi)


def _gelu(x):
    return 0.5 * x * (1.0 + jnp.tanh(_GK * (x + 0.044715 * x * x * x)))


def _dgelu(x):
    th = jnp.tanh(_GK * (x + 0.044715 * x * x * x))
    return 0.5 * (1.0 + th) + 0.5 * x * (1.0 - th * th) * _GK * (1.0 + 3.0 * 0.044715 * x * x)


def _colsum(x):
    return jnp.sum(x, axis=0, keepdims=True)


def _rms(x):
    r = lax.rsqrt(jnp.mean(x * x, axis=-1, keepdims=True) + EPS)
    return r, x * r


def _rms_bwd(r, n, dn):
    return r * (dn - n * jnp.mean(dn * n, axis=-1, keepdims=True))


def _roll(x, k):
    n = x.shape[0]
    k = k % n
    return x if k == 0 else pltpu.roll(x, k, axis=0)


def _tblock(t, want=512):
    return min(t, want)


def _peer(mask):
    x, y, c = lax.axis_index("x"), lax.axis_index("y"), lax.axis_index("c")
    return (x ^ ((mask >> 2) & 1), y ^ ((mask >> 1) & 1), c ^ (mask & 1))


def _group_index(masks):
    x, y, c = lax.axis_index("x"), lax.axis_index("y"), lax.axis_index("c")
    full = 0
    for m in masks:
        full |= m
    bits = [b for b in (4, 2, 1) if full & b]

    def idx(px, py, pc):
        v = {4: px, 2: py, 1: pc}
        out = 0
        for b in bits:
            out = out * 2 + v[b]
        return out

    return idx(x, y, c), [idx(*_peer(m)) for m in masks]


def _exchange(arrs, masks, scatter, name):
    n_arr, n_peer, n_grp = len(arrs), len(masks), len(masks) + 1

    def body(*refs):
        ins, outs = refs[:n_arr], refs[n_arr:2 * n_arr]
        send_sems, recv_sems, local_sems = refs[2 * n_arr:]
        me, peer_idx = _group_index(masks)
        copies = []
        for t in range(n_arr):
            src_me = ins[t].at[me] if scatter else ins[t]
            loc = pltpu.make_async_copy(src_me, outs[t].at[me], local_sems.at[t])
            loc.start()
            copies.append(loc)
            for j, m in enumerate(masks):
                src = ins[t].at[peer_idx[j]] if scatter else ins[t]
                cp = pltpu.make_async_remote_copy(src_ref=src, dst_ref=outs[t].at[me], send_sem=send_sems.at[t, j],
                                                  recv_sem=recv_sems.at[t, j], device_id=_peer(m), device_id_type=MESH)
                cp.start()
                copies.append(cp)
        for cp in copies:
            cp.wait()

    hbm = pl.BlockSpec(memory_space=pl.ANY)
    out_shape = [jax.ShapeDtypeStruct((n_grp,) + (a.shape[1:] if scatter else a.shape), a.dtype) for a in arrs]
    outs = pl.pallas_call(
        body, name=name, in_specs=[hbm] * n_arr, out_specs=[hbm] * n_arr, out_shape=out_shape,
        scratch_shapes=[pltpu.SemaphoreType.DMA((n_arr, n_peer)), pltpu.SemaphoreType.DMA((n_arr, n_peer)),
                        pltpu.SemaphoreType.DMA((n_arr,))],
    )(*arrs)
    return list(outs)


CHIPS = (4, 2, 6)
EVERYONE = (1, 2, 3, 4, 5, 6, 7)
SIBLING = (1,)
SWAP_ROWS = 512
WIN_ROWS = 592


def _pair_swap(arrs, other_layer, name):
    n_arr = len(arrs)
    shapes = [a.shape[-2:] for a in arrs]
    chunks = []
    for t, (rows, _) in enumerate(shapes):
        assert rows % 16 == 0
        for j, r0 in enumerate(range(0, rows, SWAP_ROWS)):
            chunks.append((t, r0, min(SWAP_ROWS, rows - r0), j % 2))

    def body(*refs):
        ins, outs = refs[:n_arr], refs[n_arr:2 * n_arr]
        bufs = refs[2 * n_arr:3 * n_arr]
        load_sems, send_sems, recv_sems = refs[3 * n_arr:]
        sibling = _peer(1)
        c = lax.axis_index("c")

        def load(k):
            t, r0, n, slot = chunks[k]
            src = ins[t].at[1 - c] if other_layer else ins[t]
            return pltpu.make_async_copy(src.at[pl.ds(r0, n)], bufs[t].at[slot, pl.ds(0, n)], load_sems.at[t, slot])

        def send(k):
            t, r0, n, slot = chunks[k]
            return pltpu.make_async_remote_copy(src_ref=bufs[t].at[slot, pl.ds(0, n)], dst_ref=outs[t].at[pl.ds(r0, n)],
                                                send_sem=send_sems.at[t, slot], recv_sem=recv_sems.at[t],
                                                device_id=sibling, device_id_type=MESH)

        in_flight = {}

        def start_load(k):
            key = (chunks[k][0], chunks[k][3])
            if key in in_flight:
                send(in_flight.pop(key)).wait_send()
            load(k).start()

        start_load(0)
        for k in range(len(chunks)):
            load(k).wait()
            if k + 1 < len(chunks):
                start_load(k + 1)
            send(k).start()
            in_flight[(chunks[k][0], chunks[k][3])] = k
        for k in in_flight.values():
            send(k).wait_send()
        for t in range(n_arr):
            pltpu.make_async_remote_copy(src_ref=outs[t], dst_ref=outs[t], send_sem=send_sems.at[t, 0],
                                         recv_sem=recv_sems.at[t], device_id=sibling, device_id_type=MESH).wait_recv()

    hbm = pl.BlockSpec(memory_space=pl.ANY)
    outs = pl.pallas_call(
        body, name=name, in_specs=[hbm] * n_arr, out_specs=[hbm] * n_arr,
        out_shape=[jax.ShapeDtypeStruct(s, a.dtype) for s, a in zip(shapes, arrs)],
        scratch_shapes=[pltpu.VMEM((2, min(SWAP_ROWS, s[0]), s[1]), a.dtype) for s, a in zip(shapes, arrs)]
        + [pltpu.SemaphoreType.DMA((n_arr, 2)), pltpu.SemaphoreType.DMA((n_arr, 2)), pltpu.SemaphoreType.DMA((n_arr,))],
        compiler_params=pltpu.CompilerParams(vmem_limit_bytes=48 * 1024 * 1024),
    )(*arrs)
    return list(outs)


def _sum_lead(a, name, out_dtype):
    n = a.shape[0]
    shape = a.shape[1:]

    def body(a_ref, o_ref):
        acc = a_ref[0].astype(F32)
        for k in range(1, n):
            acc = acc + a_ref[k].astype(F32)
        o_ref[...] = acc.astype(out_dtype)

    if len(shape) == 3:
        blk = (1,) + shape[1:]
        return pl.pallas_call(
            body, name=name, grid=(shape[0],), in_specs=[pl.BlockSpec((n,) + blk, lambda i: (0, i, 0, 0))],
            out_specs=pl.BlockSpec(blk, lambda i: (i, 0, 0)), out_shape=jax.ShapeDtypeStruct(shape, out_dtype),
            compiler_params=_cparams(1),
        )(a)
    rows, cols = shape
    rb = rows
    for cand in (512, 256, 128):
        if rows % cand == 0 and rows > cand:
            rb = cand
            break
    return pl.pallas_call(
        body, name=name, grid=(rows // rb,), in_specs=[pl.BlockSpec((n, rb, cols), lambda i: (0, i, 0))],
        out_specs=pl.BlockSpec((rb, cols), lambda i: (i, 0)), out_shape=jax.ShapeDtypeStruct((rows, cols), out_dtype),
        compiler_params=_cparams(1),
    )(a)


def _pair_sum(g, recv, layer, name, out_dtype):
    _, n, r, c = g.shape

    def body(l_ref, g_ref, r_ref, o_ref):
        o_ref[...] = (g_ref[0].astype(F32) + r_ref[...].astype(F32)).astype(out_dtype)

    return pl.pallas_call(
        body, name=name,
        grid_spec=pltpu.PrefetchScalarGridSpec(
            num_scalar_prefetch=1, grid=(n,),
            in_specs=[pl.BlockSpec((1, 1, r, c), lambda i, l: (l[0], i, 0, 0)), pl.BlockSpec((1, r, c), lambda i, l: (i, 0, 0))],
            out_specs=pl.BlockSpec((1, r, c), lambda i, l: (i, 0, 0))),
        out_shape=jax.ShapeDtypeStruct((n, r, c), out_dtype), compiler_params=_cparams(1),
    )(layer, g, recv)


def _tn_matmul(a, b, name, col_major=False):
    t, k = a.shape
    n = b.shape[1]
    tb = _tblock(t, 1024)
    kb = min(k, 1024)
    nb = min(n, 1024)
    grid = (k // kb, n // nb, t // tb)

    def body(a_ref, b_ref, o_ref):
        @pl.when(pl.program_id(2) == 0)
        def _():
            o_ref[...] = jnp.zeros_like(o_ref)

        acc = _bdot(a_ref[...], b_ref[...], TN)
        if col_major:
            o_ref[0] += acc
        else:
            o_ref[...] += acc

    if col_major:
        out_spec = pl.BlockSpec((1, kb, nb), lambda ki, ni, ti: (ni, ki, 0))
        out_shape = jax.ShapeDtypeStruct((n // nb, k, nb), F32)
    else:
        out_spec = pl.BlockSpec((kb, nb), lambda ki, ni, ti: (ki, ni))
        out_shape = jax.ShapeDtypeStruct((k, n), F32)
    return pl.pallas_call(
        body, name=name, grid=grid,
        in_specs=[pl.BlockSpec((tb, kb), lambda ki, ni, ti: (ti, ki)), pl.BlockSpec((tb, nb), lambda ki, ni, ti: (ti, ni))],
        out_specs=out_spec, out_shape=out_shape, compiler_params=_cparams(3),
    )(a, b)


def _adamw(w, g, m, v, name):
    shape = w.shape
    cols = shape[-1]
    rows = int(math.prod(shape[:-1]))
    rb = rows
    for cand in (256, 128, 64, 32, 16, 8):
        if rows % cand == 0 and rows > cand:
            rb = cand
            break
    bc1 = 1.0 - ADAM_B1 ** ADAM_STEP
    bc2 = 1.0 - ADAM_B2 ** ADAM_STEP

    def body(w_ref, g_ref, m_ref, v_ref, d_ref, nm_ref, nv_ref):
        gg = g_ref[...]
        m2 = ADAM_B1 * m_ref[...] + (1.0 - ADAM_B1) * gg
        v2 = ADAM_B2 * v_ref[...] + (1.0 - ADAM_B2) * (gg * gg)
        m_hat = m2 / bc1
        v_hat = v2 / bc2
        d_ref[...] = -ADAM_LR * (m_hat / (jnp.sqrt(v_hat) + ADAM_EPS) + ADAM_WD * w_ref[...])
        nm_ref[...] = m2
        nv_ref[...] = v2

    spec = pl.BlockSpec((rb, cols), lambda i: (i, 0))
    sds = jax.ShapeDtypeStruct((rows, cols), F32)
    outs = pl.pallas_call(
        body, name=name, grid=(rows // rb,), in_specs=[spec] * 4, out_specs=[spec] * 3, out_shape=[sds] * 3,
        compiler_params=_cparams(1),
    )(*(z.reshape(rows, cols) for z in (w, g, m, v)))
    return tuple(o.reshape(shape) for o in outs)


def _ada_fwd(c_all, ada_w_sh, ada_b_sh):
    s = ada_w_sh.shape[2]
    sb = 512

    def body(c_ref, w_ref, b_ref, o_ref):
        cond = _silu(c_ref[...])
        o_ref[0] = _bdot(cond, w_ref[0]) + b_ref[0]

    return pl.pallas_call(
        body, name="ada_fwd", grid=(2, s // sb),
        in_specs=[_full((8, D)), pl.BlockSpec((1, D, sb), lambda l, j: (l, 0, j)), pl.BlockSpec((1, 1, sb), lambda l, j: (l, 0, j))],
        out_specs=pl.BlockSpec((1, 8, sb), lambda l, j: (l, 0, j)), out_shape=jax.ShapeDtypeStruct((2, 8, s), F32),
        compiler_params=_cparams(2),
    )(c_all, ada_w_sh, ada_b_sh)


def _ada_bwd(c_all, dmod_sh, dmod_all):
    s = dmod_sh.shape[2]
    sb = 512

    def body(c_ref, d_ref, o_ref):
        cond = _silu(c_ref[...])
        o_ref[0] = _bdot(cond, d_ref[0], TN)

    gw = pl.pallas_call(
        body, name="ada_bwd_w", grid=(2, s // sb),
        in_specs=[_full((8, D)), pl.BlockSpec((1, 8, sb), lambda l, j: (l, 0, j))],
        out_specs=pl.BlockSpec((1, D, sb), lambda l, j: (l, 0, j)), out_shape=jax.ShapeDtypeStruct((2, D, s), F32),
        compiler_params=_cparams(2),
    )(c_all, dmod_sh)

    def body_b(d_ref, o_ref):
        acc = d_ref[0, 0:1, :]
        for k in range(1, 8):
            acc = acc + d_ref[0, k:k + 1, :]
        o_ref[0] = acc

    gb = pl.pallas_call(
        body_b, name="ada_bwd_b", grid=(2,), in_specs=[pl.BlockSpec((1, 8, 6 * D), lambda l: (l, 0, 0))],
        out_specs=pl.BlockSpec((1, 1, 6 * D), lambda l: (l, 0, 0)), out_shape=jax.ShapeDtypeStruct((2, 1, 6 * D), F32),
        compiler_params=_cparams(1),
    )(dmod_all)
    return gw, gb.reshape(2, 6 * D)


def _f_in(h, nw, sc, sh, w_main, w_dt):
    t = h.shape[0]
    tb = _tblock(t)

    def body(h_ref, nw_ref, sc_ref, sh_ref, w_ref, wd_ref, p_ref, dt_ref, u_ref):
        _, n = _rms(h_ref[...])
        u = ((n * nw_ref[...]) * (1.0 + sc_ref[...]) + sh_ref[...]).astype(BF16)
        u_ref[...] = u
        p_ref[...] = _dot(u, w_ref[...])
        dt_ref[...] = _dot(u, wd_ref[...])

    return pl.pallas_call(
        body, name="f_in", grid=(t // tb,),
        in_specs=[pl.BlockSpec((tb, D), lambda i: (i, 0)), _row(D), _row(D), _row(D), _full((D, PW)), _full((D, DTW))],
        out_specs=[pl.BlockSpec((tb, PW), lambda i: (i, 0)), pl.BlockSpec((tb, DTW), lambda i: (i, 0)),
                   pl.BlockSpec((tb, D), lambda i: (i, 0))],
        out_shape=[jax.ShapeDtypeStruct((t, PW), F32), jax.ShapeDtypeStruct((t, DTW), F32), jax.ShapeDtypeStruct((t, D), BF16)],
        compiler_params=_cparams(1),
    )(h, nw, sc, sh, w_main, w_dt)


def _b_in_du(dab, dz, dxbc, ds5, ddt, w_main, w_dt):
    t = dab.shape[0]
    tb = _tblock(t)

    def body(a_ref, z_ref, x_ref, s_ref, d_ref, w_ref, wd_ref, o_ref):
        acc = _bdot(a_ref[...], w_ref[:, 0:1024], NT)
        acc += _bdot(z_ref[...], w_ref[:, 1024:1280], NT)
        acc += _bdot(s_ref[...], w_ref[:, 1280:1536], NT)
        acc += _bdot(x_ref[...], w_ref[:, 1536:2304], NT)
        acc += _bdot(d_ref[...], wd_ref[...], NT)
        o_ref[...] = acc

    blk = lambda n: pl.BlockSpec((tb, n), lambda i: (i, 0))
    return pl.pallas_call(
        body, name="b_in_du", grid=(t // tb,),
        in_specs=[blk(1024), blk(256), blk(768), blk(256), blk(DTW), _full((D, PW)), _full((D, DTW))],
        out_specs=blk(D), out_shape=jax.ShapeDtypeStruct((t, D), F32), compiler_params=_cparams(1),
    )(dab, dz, dxbc, ds5, ddt, w_main, w_dt)


def _b_normmod(du, x, dres, gated, nw, sc, name):
    t = x.shape[0]
    tb = _tblock(t)

    def body(du_ref, x_ref, dr_ref, g_ref, nw_ref, sc_ref, dx_ref, dsc_ref, dsh_ref, dnw_ref, dg_ref):
        @pl.when(pl.program_id(0) == 0)
        def _():
            for r in (dsc_ref, dsh_ref, dnw_ref, dg_ref):
                r[...] = jnp.zeros_like(r)

        du_v = du_ref[...]
        r, n = _rms(x_ref[...])
        nwv = nw_ref[...]
        scale = 1.0 + sc_ref[...]
        dsc_ref[...] += _colsum(du_v * (n * nwv))
        dsh_ref[...] += _colsum(du_v)
        dnw_ref[...] += _colsum(du_v * scale * n)
        dres_v = dr_ref[...]
        dg_ref[...] += _colsum(dres_v * g_ref[...])
        dx_ref[...] = dres_v + _rms_bwd(r, n, du_v * scale * nwv)

    blk = pl.BlockSpec((tb, D), lambda i: (i, 0))
    row = jax.ShapeDtypeStruct((1, D), F32)
    return pl.pallas_call(
        body, name=name, grid=(t // tb,), in_specs=[blk, blk, blk, blk, _row(D), _row(D)],
        out_specs=[blk, _row(D), _row(D), _row(D), _row(D)], out_shape=[jax.ShapeDtypeStruct((t, D), F32), row, row, row, row],
        compiler_params=_cparams(1),
    )(du, x, dres, gated, nw, sc)


HALO = 16


def _lane_group(shape):
    return lax.broadcasted_iota(jnp.int32, shape, 1) // 64


def _window_select(g, s2, s4, s8, s16):
    return jnp.where(g == 0, s2, jnp.where(g == 1, s4, jnp.where(g == 2, s8, s16)))


def _pool_count(t0, rows):
    g = _lane_group((rows, GW))
    win = _window_select(g, 2, 4, 8, 16)
    tt = t0 + lax.broadcasted_iota(jnp.int32, (rows, GW), 0)
    return jnp.minimum(tt + 1, win).astype(F32)


def _pool_p(v_ext, t0, tb):
    s2 = v_ext + _roll(v_ext, 1)
    s4 = s2 + _roll(s2, 2)
    s8 = s4 + _roll(s4, 4)
    s16 = s8 + _roll(s8, 8)
    ws = _window_select(_lane_group(v_ext.shape), s2, s4, s8, s16)[HALO:]
    return ws / _pool_count(t0, tb) - v_ext[HALO:]


def _sconv(q_ext, w):
    return (_roll(q_ext, 2) * w[0:1] + _roll(q_ext, 1) * w[1:2] + q_ext * w[2:3])[HALO:]


def _halo_specs(t, tb, cols, col_block):
    per = tb // HALO
    last = t // HALO - 1
    prev = pl.BlockSpec((HALO, cols), lambda i: (jnp.maximum(i * per - 1, 0), col_block))
    nxt = pl.BlockSpec((HALO, cols), lambda i: (jnp.minimum((i + 1) * per, last), col_block))
    return prev, nxt


def _f_ab(proj, pool_mat, pool_scale, sconv_w):
    t = proj.shape[0]
    tb = _tblock(t)
    prev, _ = _halo_specs(t, tb, 1024, 0)

    def body(p_ref, h_ref, pm_ref, ps_ref, sw_ref, ya_ref, yb_ref):
        i = pl.program_id(0)
        halo = jnp.where(i > 0, h_ref[...], 0.0)
        ext = jnp.concatenate([halo, p_ref[...]], axis=0)
        p = _pool_p(ext[:, 0:256], i * tb, tb)
        ya_ref[...] = _dot(p, pm_ref[...]) * ps_ref[...]
        q_ext = ext[:, 512:768] * ext[:, 768:1024]
        yb_ref[...] = p_ref[:, 256:512] * _sconv(q_ext, sw_ref[...])

    blk = pl.BlockSpec((tb, GW), lambda i: (i, 0))
    sds = jax.ShapeDtypeStruct((t, GW), F32)
    return pl.pallas_call(
        body, name="f_ab", grid=(t // tb,),
        in_specs=[pl.BlockSpec((tb, 1024), lambda i: (i, 0)), prev, _full((GW, GW)), _row(GW), _full((3, GW))],
        out_specs=[blk, blk], out_shape=[sds, sds], compiler_params=_cparams(1),
    )(proj, proj, pool_mat, pool_scale, sconv_w)


def _b_ab(proj, dya, dyb, pool_mat, pool_scale, sconv_w):
    t = proj.shape[0]
    tb = _tblock(t)
    nb = t // tb
    prev, nxt = _halo_specs(t, tb, 1024, 0)
    _, nxt_g = _halo_specs(t, tb, GW, 0)
    n_ext = tb + HALO

    def body(p_ref, hp_ref, hn_ref, da_ref, dan_ref, db_ref, dbn_ref, pm_ref, ps_ref, sw_ref,
             o_ref, dpm_ref, dps_ref, dsw_ref):
        i = pl.program_id(0)

        @pl.when(i == 0)
        def _():
            for r in (dpm_ref, dps_ref, dsw_ref):
                r[...] = jnp.zeros_like(r)

        last = i == nb - 1
        halo = jnp.where(i > 0, hp_ref[...], 0.0)
        main = p_ref[...]
        ext = jnp.concatenate([halo, main], axis=0)
        scale = ps_ref[...]
        pm = pm_ref[...]
        p = _pool_p(ext[:, 0:256], i * tb, tb)
        da = da_ref[...]
        dps_ref[...] += _colsum(da * _dot(p, pm))
        da_ext = jnp.concatenate([da, jnp.where(last, 0.0, dan_ref[...])], axis=0)
        dys = da_ext * scale
        dpm_ref[...] += _dot(p, dys[:tb], TN)
        dp = _dot(dys, pm, NT)
        dpc = dp / _pool_count(i * tb, n_ext)
        a2 = dpc + _roll(dpc, n_ext - 1)
        a4 = a2 + _roll(a2, n_ext - 2)
        a8 = a4 + _roll(a4, n_ext - 4)
        a16 = a8 + _roll(a8, n_ext - 8)
        o_ref[:, 0:256] = (_window_select(_lane_group(dpc.shape), a2, a4, a8, a16) - dp)[:tb]
        w = sw_ref[...]
        gb, gc, hh = main[:, 256:512], main[:, 512:768], main[:, 768:1024]
        q_ext = ext[:, 512:768] * ext[:, 768:1024]
        db = db_ref[...]
        o_ref[:, 256:512] = db * _sconv(q_ext, w)
        gb_next = hn_ref[:, 256:512]
        dconv = jnp.concatenate([db * gb, jnp.where(last, 0.0, dbn_ref[...] * gb_next)], axis=0)
        dq = (dconv * w[2:3] + _roll(dconv, n_ext - 1) * w[1:2] + _roll(dconv, n_ext - 2) * w[0:1])[:tb]
        o_ref[:, 512:768] = dq * hh
        o_ref[:, 768:1024] = dq * gc
        dc = dconv[:tb]
        dsw_ref[0:1, :] += _colsum(dc * _roll(q_ext, 2)[HALO:])
        dsw_ref[1:2, :] += _colsum(dc * _roll(q_ext, 1)[HALO:])
        dsw_ref[2:3, :] += _colsum(dc * q_ext[HALO:])

    blk = pl.BlockSpec((tb, GW), lambda i: (i, 0))
    return pl.pallas_call(
        body, name="b_ab", grid=(nb,),
        in_specs=[pl.BlockSpec((tb, 1024), lambda i: (i, 0)), prev, nxt, blk, nxt_g, blk, nxt_g,
                  _full((GW, GW)), _row(GW), _full((3, GW))],
        out_specs=[pl.BlockSpec((tb, 1024), lambda i: (i, 0)), _full((GW, GW)), _row(GW), _full((3, GW))],
        out_shape=[jax.ShapeDtypeStruct((t, 1024), F32), jax.ShapeDtypeStruct((GW, GW), F32),
                   jax.ShapeDtypeStruct((1, GW), F32), jax.ShapeDtypeStruct((3, GW), F32)],
        compiler_params=_cparams(1),
    )(proj, proj, proj, dya, dya, dyb, dyb, pool_mat, pool_scale, sconv_w)


CH = 8


def _ssd_conv(x, halo, w, b):
    ext = jnp.concatenate([halo, x], axis=0)
    pre = ext * w[3:4] + _roll(ext, 1) * w[2:3] + _roll(ext, 2) * w[1:2] + _roll(ext, 3) * w[0:1] + b
    return pre[CH:], ext


def _ssd_common(dt_raw, dtb, alog):
    ll = dt_raw.shape[0]
    dtv = _softplus(dt_raw + dtb)
    a_row = -jnp.exp(alog)
    r = lax.broadcasted_iota(jnp.int32, (ll, ll), 0)
    c = lax.broadcasted_iota(jnp.int32, (ll, ll), 1)
    tril = (r >= c).astype(F32)
    cs = _dot(tril, dtv * a_row, prec=HI)
    return dtv, a_row, cs, cs.T, r >= c


def _ssd_head(h, act, dtv, cs, cs_t, causal, gmat):
    g = h // 2
    xs = act[:, HP * h:HP * (h + 1)]
    bm = act[:, 256 + NS * g:256 + NS * (g + 1)]
    cm = act[:, 512 + NS * g:512 + NS * (g + 1)]
    cs_c = cs[:, h:h + 1]
    cs_r = cs_t[h:h + 1, :]
    mdec = jnp.where(causal, jnp.exp(jnp.minimum(cs_c - cs_r, 0.0)), 0.0)
    sc = gmat[g] * mdec
    dt_c = dtv[:, h:h + 1]
    xdt = xs * dt_c
    e = jnp.exp(cs_c)
    cs_last = cs[SSD_L - 1:SSD_L, h:h + 1]
    wdec = jnp.exp(cs_last - cs_c)
    return xs, bm, cm, cs_c, mdec, sc, dt_c, xdt, e, cs_last, wdec


def _f_ssd(proj, dtp, conv_w, conv_b, dt_bias, a_log, d_skip):
    t = proj.shape[0]
    nc = t // SSD_L
    per = SSD_L // CH

    def body(x_ref, hx_ref, dt_ref, z_ref, cw_ref, cb_ref, dtb_ref, al_ref, dk_ref, y_ref, yp_ref, sp_ref, s_ref):
        i = pl.program_id(0)

        @pl.when(i == 0)
        def _():
            s_ref[...] = jnp.zeros_like(s_ref)

        halo = jnp.where(i > 0, hx_ref[...], 0.0)
        pre, _ = _ssd_conv(x_ref[...], halo, cw_ref[...], cb_ref[...])
        act = _silu(pre)
        dtv, _, cs, cs_t, causal = _ssd_common(dt_ref[...], dtb_ref[...], al_ref[...])
        gmat = [_dot(act[:, 512 + NS * g:512 + NS * (g + 1)], act[:, 256 + NS * g:256 + NS * (g + 1)], NT) for g in range(2)]
        for h in range(NH):
            xs, bm, cm, _, _, sc, _, xdt, e, cs_last, wdec = _ssd_head(h, act, dtv, cs, cs_t, causal, gmat)
            prev = s_ref[h]
            sp_ref[0, h] = prev
            y = _dot(sc, xdt) + e * _dot(cm, prev, NT) + xs * dk_ref[0:1, h:h + 1]
            yp_ref[:, HP * h:HP * (h + 1)] = y
            s_ref[h] = prev * jnp.exp(cs_last) + _dot(xdt * wdec, bm, TN)
        y_ref[...] = yp_ref[...] * _silu(z_ref[...])

    blk = pl.BlockSpec((SSD_L, GW), lambda i: (i, 0))
    sds = jax.ShapeDtypeStruct((t, GW), F32)
    return pl.pallas_call(
        body, name="f_ssd", grid=(nc,),
        in_specs=[pl.BlockSpec((SSD_L, 768), lambda i: (i, 2)),
                  pl.BlockSpec((CH, 768), lambda i: (jnp.maximum(i * per - 1, 0), 2)),
                  pl.BlockSpec((SSD_L, DTW), lambda i: (i, 0)),
                  pl.BlockSpec((SSD_L, GW), lambda i: (i, 4)),
                  _full((4, 768)), _row(768), _row(DTW), _row(DTW), _row(DTW)],
        out_specs=[blk, blk, pl.BlockSpec((1, NH, HP, NS), lambda i: (i, 0, 0, 0))],
        out_shape=[sds, sds, jax.ShapeDtypeStruct((nc, NH, HP, NS), F32)],
        scratch_shapes=[pltpu.VMEM((NH, HP, NS), F32)], compiler_params=_cparams(1),
    )(proj, proj, dtp, proj, conv_w, conv_b, dt_bias, a_log, d_skip)


def _b_ssd(proj, dtp, ypre, dyc, sprev, conv_w, conv_b, dt_bias, a_log, d_skip):
    t = proj.shape[0]
    nc = t // SSD_L
    per = SSD_L // CH
    n_ext = SSD_L + CH

    def body(x_ref, hx_ref, dt_ref, z_ref, yp_ref, dy_ref, sp_ref, cw_ref, cb_ref, dtb_ref, al_ref, dk_ref,
             dz_ref, dx_ref, ddt_ref, dcw_ref, dcb_ref, ddtb_ref, dal_ref, ddk_ref, ds_ref, dnext_ref, dact_ref):
        i = pl.program_id(0)

        @pl.when(i == 0)
        def _():
            ds_ref[...] = jnp.zeros_like(ds_ref)
            dnext_ref[...] = jnp.zeros_like(dnext_ref)
            for r in (dcw_ref, dcb_ref, ddtb_ref, dal_ref, ddk_ref):
                r[...] = jnp.zeros_like(r)

        first_chunk = i == nc - 1
        halo = jnp.where(first_chunk, 0.0, hx_ref[...])
        w = cw_ref[...]
        pre, ext = _ssd_conv(x_ref[...], halo, w, cb_ref[...])
        act = _silu(pre)
        dt_raw = dt_ref[...]
        dtv, a_row, cs, cs_t, causal = _ssd_common(dt_raw, dtb_ref[...], al_ref[...])
        gmat = [_dot(act[:, 512 + NS * g:512 + NS * (g + 1)], act[:, 256 + NS * g:256 + NS * (g + 1)], NT) for g in range(2)]
        z = z_ref[...]
        dyc_v = dy_ref[...]
        dz_ref[...] = dyc_v * yp_ref[...] * _dsilu(z)
        dy_all = dyc_v * _silu(z)
        lane = lax.broadcasted_iota(jnp.int32, (SSD_L, DTW), 1)
        rowi = lax.broadcasted_iota(jnp.int32, (SSD_L, 1), 0)
        dcs_mat = jnp.zeros((SSD_L, DTW), F32)
        ddtx_mat = jnp.zeros((SSD_L, DTW), F32)
        ddk_row = jnp.zeros((1, DTW), F32)
        lane1 = lax.broadcasted_iota(jnp.int32, (1, DTW), 1)
        dbm = [None, None]
        dcm = [None, None]
        for h in range(NH):
            g = h // 2
            xs, bm, cm, _, mdec, sc, dt_c, xdt, e, cs_last, wdec = _ssd_head(h, act, dtv, cs, cs_t, causal, gmat)
            dy = dy_all[:, HP * h:HP * (h + 1)]
            prev = sp_ref[0, h]
            ds = ds_ref[h]
            dsc = _dot(dy, xdt, NT)
            q = dsc * sc
            dg = dsc * mdec
            dxdt = _dot(sc, dy, TN)
            dcs = jnp.sum(q, axis=1, keepdims=True) - jnp.sum(q.T, axis=1, keepdims=True)
            dc_h = _dot(dg, bm)
            db_h = _dot(dg, cm, TN)
            cp = _dot(cm, prev, NT)
            dcs += jnp.sum(dy * cp, axis=1, keepdims=True) * e
            ey = e * dy
            dc_h += _dot(ey, prev)
            dprev = _dot(ey, cm, TN)
            elast = jnp.exp(cs_last)
            dprev += ds * elast
            dcs_last = jnp.sum(ds * prev, keepdims=True) * elast
            bds = _dot(bm, ds, NT)
            dxdt += wdec * bds
            db_h += wdec * _dot(xdt, ds)
            dw = jnp.sum(xdt * bds, axis=1, keepdims=True) * wdec
            dcs -= dw
            dcs_last += jnp.sum(dw, keepdims=True)
            dcs += jnp.where(rowi == SSD_L - 1, dcs_last, 0.0)
            ds_ref[h] = dprev
            dact_ref[:, HP * h:HP * (h + 1)] = dxdt * dt_c + dy * dk_ref[0:1, h:h + 1]
            dcs_mat = jnp.where(lane == h, dcs, dcs_mat)
            ddtx_mat = jnp.where(lane == h, jnp.sum(dxdt * xs, axis=1, keepdims=True), ddtx_mat)
            ddk_row = jnp.where(lane1 == h, jnp.sum(dy * xs, keepdims=True), ddk_row)
            dbm[g] = db_h if dbm[g] is None else dbm[g] + db_h
            dcm[g] = dc_h if dcm[g] is None else dcm[g] + dc_h
        for g in range(2):
            dact_ref[:, 256 + NS * g:256 + NS * (g + 1)] = dbm[g]
            dact_ref[:, 512 + NS * g:512 + NS * (g + 1)] = dcm[g]
        ddk_ref[...] += ddk_row
        r2 = lax.broadcasted_iota(jnp.int32, (SSD_L, SSD_L), 0)
        c2 = lax.broadcasted_iota(jnp.int32, (SSD_L, SSD_L), 1)
        dadt = _dot((c2 >= r2).astype(F32), dcs_mat, prec=HI)
        dal_ref[...] += _colsum(dadt * dtv) * a_row
        ddt = jnp.where(lane < NH, (dadt * a_row + ddtx_mat) * _sig(dt_raw + dtb_ref[...]), 0.0)
        ddt_ref[...] = ddt
        ddtb_ref[...] += _colsum(ddt)
        dpre = dact_ref[...] * _dsilu(pre)
        dcb_ref[...] += _colsum(dpre)
        for k in range(4):
            dcw_ref[k:k + 1, :] += _colsum(dpre * _roll(ext, 3 - k)[CH:])
        dext = jnp.concatenate([dpre, dnext_ref[...]], axis=0)
        dx_ref[...] = (dext * w[3:4] + _roll(dext, n_ext - 1) * w[2:3] + _roll(dext, n_ext - 2) * w[1:2]
                       + _roll(dext, n_ext - 3) * w[0:1])[:SSD_L]
        dnext_ref[...] = dpre[0:CH]

    rev = lambda i: nc - 1 - i
    blk = lambda n, cb=0: pl.BlockSpec((SSD_L, n), lambda i: (rev(i), cb))
    row = lambda n: jax.ShapeDtypeStruct((1, n), F32)
    return pl.pallas_call(
        body, name="b_ssd", grid=(nc,),
        in_specs=[blk(768, 2), pl.BlockSpec((CH, 768), lambda i: (jnp.maximum(rev(i) * per - 1, 0), 2)),
                  blk(DTW), blk(GW, 4), blk(GW), blk(GW), pl.BlockSpec((1, NH, HP, NS), lambda i: (rev(i), 0, 0, 0)),
                  _full((4, 768)), _row(768), _row(DTW), _row(DTW), _row(DTW)],
        out_specs=[blk(GW), blk(768), blk(DTW), _full((4, 768)), _row(768), _row(DTW), _row(DTW), _row(DTW)],
        out_shape=[jax.ShapeDtypeStruct((t, GW), F32), jax.ShapeDtypeStruct((t, 768), F32), jax.ShapeDtypeStruct((t, DTW), F32),
                   jax.ShapeDtypeStruct((4, 768), F32), row(768), row(DTW), row(DTW), row(DTW)],
        scratch_shapes=[pltpu.VMEM((NH, HP, NS), F32), pltpu.VMEM((CH, 768), F32), pltpu.VMEM((SSD_L, 768), F32)],
        compiler_params=_cparams(1),
    )(proj, proj, dtp, proj, ypre, dyc, sprev, conv_w, conv_b, dt_bias, a_log, d_skip)


def _s5_block(t):
    return min(t, 256)


def _seg_t():
    r = lax.broadcasted_iota(jnp.int32, (64, 1024), 0)
    c = lax.broadcasted_iota(jnp.int32, (64, 1024), 1)
    return (c // 16 == r).astype(F32)


def _s5_prep_math(a_re, a_im, lstep, b_re, b_im):
    step = jnp.exp(lstep)
    ars = a_re * step
    ais = a_im * step
    mag = jnp.exp(ars)
    lr = mag * jnp.cos(ais)
    li = mag * jnp.sin(ais)
    den = a_re * a_re + a_im * a_im
    nr = lr - 1.0
    f_re = (nr * a_re + li * a_im) / den
    f_im = (li * a_re - nr * a_im) / den
    seg = _seg_t()
    fr = _dot(f_re, seg, prec=HI)
    fi = _dot(f_im, seg, prec=HI)
    return lr, li, fr * b_re - fi * b_im, fr * b_im + fi * b_re, ars, ais


def _s5_prep(a_re, a_im, lstep, b_re, b_im):
    def body(ar, ai, ls, br, bi, lr_o, li_o, bbr_o, bbi_o, ars_o, ais_o):
        outs = _s5_prep_math(ar[...], ai[...], ls[...], br[...], bi[...])
        for o, v in zip((lr_o, li_o, bbr_o, bbi_o, ars_o, ais_o), outs):
            o[...] = v

    s64 = jax.ShapeDtypeStruct((16, 64), F32)
    s1k = jax.ShapeDtypeStruct((16, 1024), F32)
    return pl.pallas_call(body, name="s5_prep", out_shape=[s64, s64, s1k, s1k, s64, s64])(a_re, a_im, lstep, b_re, b_im)


def _s5_prep_bwd(a_re, a_im, lstep, b_re, b_im, dlr, dli, dbbr, dbbi):
    def body(ar, ai, ls, br, bi, g0, g1, g2, g3, o0, o1, o2, o3, o4):
        f = lambda *a: _s5_prep_math(*a)[:4]
        _, vjp = jax.vjp(f, ar[...], ai[...], ls[...], br[...], bi[...])
        for o, v in zip((o0, o1, o2, o3, o4), vjp((g0[...], g1[...], g2[...], g3[...]))):
            o[...] = v

    s64 = jax.ShapeDtypeStruct((16, 64), F32)
    s1k = jax.ShapeDtypeStruct((16, 1024), F32)
    return pl.pallas_call(body, name="s5_prep_bwd", out_shape=[s64, s64, jax.ShapeDtypeStruct((16, 1), F32), s1k, s1k])(
        a_re, a_im, lstep, b_re, b_im, dlr, dli, dbbr, dbbi)


def _s5_tables(ars, ais, lb):
    def body(ar, ai, pr, pi, qr, qi):
        row = lax.broadcasted_iota(jnp.int32, (lb, S5_P), 0).astype(F32)
        for n, o_r, o_i in ((row + 1.0, pr, pi), (float(lb) - row, qr, qi)):
            mag = jnp.exp(n * ar[...])
            o_r[...] = mag * jnp.cos(n * ai[...])
            o_i[...] = mag * jnp.sin(n * ai[...])

    sds = jax.ShapeDtypeStruct((lb, S5_P), F32)
    return pl.pallas_call(body, name="s5_tables", out_shape=[sds] * 4)(ars, ais)


def _s5_scan(bu_r, bu_i, p_r, p_i, c_r, c_i, lb):
    row = lax.broadcasted_iota(jnp.int32, (lb, S5_P), 0)
    sr, si = bu_r, bu_i
    k = 1
    while k < lb:
        lr, li = p_r[k - 1:k, :], p_i[k - 1:k, :]
        tr = jnp.where(row >= k, _roll(sr, k), 0.0)
        ti = jnp.where(row >= k, _roll(si, k), 0.0)
        sr, si = sr + lr * tr - li * ti, si + lr * ti + li * tr
        k *= 2
    pr, pi = p_r[...], p_i[...]
    return sr + pr * c_r - pi * c_i, si + pr * c_i + pi * c_r


def _s5_y(u, sr, si, cre, cim, dsk):
    return _dot(sr, cre) + _dot(si, cim) + dsk * u


def _f_s5(proj, bmat, cre, cim, p_r, p_i, dsk, glu_w, glu_b):
    t = proj.shape[0]
    lb = _s5_block(t)
    nb = t // lb

    def body(u_ref, bm_ref, cr_ref, ci_ref, pr_ref, pi_ref, dk_ref, gw_ref, gb_ref, y_ref, car_ref, st_ref):
        @pl.when(pl.program_id(0) == 0)
        def _():
            st_ref[...] = jnp.zeros_like(st_ref)

        u = u_ref[...]
        bu = _dot(u, bm_ref[...])
        c_r, c_i = st_ref[0:1, 0:S5_P], st_ref[0:1, S5_P:]
        car_ref[0] = st_ref[0:1, :]
        sr, si = _s5_scan(bu[:, :S5_P], bu[:, S5_P:], pr_ref, pi_ref, c_r, c_i, lb)
        st_ref[0:1, 0:S5_P] = sr[lb - 1:lb]
        st_ref[0:1, S5_P:] = si[lb - 1:lb]
        gel = _gelu(_s5_y(u, sr, si, cr_ref[...], ci_ref[...], dk_ref[...]))
        y_ref[...] = gel * _sig(_dot(gel, gw_ref[...]) + gb_ref[...])

    return pl.pallas_call(
        body, name="f_s5", grid=(nb,),
        in_specs=[pl.BlockSpec((lb, GW), lambda i: (i, 5)),
                  _full((GW, 2 * S5_P)), _full((S5_P, GW)), _full((S5_P, GW)), _full((lb, S5_P)), _full((lb, S5_P)),
                  _row(GW), _full((GW, GW)), _row(GW)],
        out_specs=[pl.BlockSpec((lb, GW), lambda i: (i, 0)), pl.BlockSpec((1, 1, 2 * S5_P), lambda i: (i, 0, 0))],
        out_shape=[jax.ShapeDtypeStruct((t, GW), F32), jax.ShapeDtypeStruct((nb, 1, 2 * S5_P), F32)],
        scratch_shapes=[pltpu.VMEM((8, 2 * S5_P), F32)], compiler_params=_cparams(1),
    )(proj, bmat, cre, cim, p_r, p_i, dsk, glu_w, glu_b)


def _b_s5(proj, dyd, carries, bmat, cre, cim, p_r, p_i, q_r, q_i, dsk, glu_w, glu_b):
    t = proj.shape[0]
    lb = _s5_block(t)
    nb = t // lb

    def body(u_ref, dy_ref, car_ref, bm_ref, cr_ref, ci_ref, pr_ref, pi_ref, qr_ref, qi_ref, dk_ref, gw_ref, gb_ref,
             du_ref, dbm_ref, dcr_ref, dci_ref, dlam_ref, ddk_ref, dgw_ref, dgb_ref, gc_ref):
        @pl.when(pl.program_id(0) == 0)
        def _():
            gc_ref[...] = jnp.zeros_like(gc_ref)
            for r in (dbm_ref, dcr_ref, dci_ref, dlam_ref, ddk_ref, dgw_ref, dgb_ref):
                r[...] = jnp.zeros_like(r)

        u = u_ref[...]
        bm = bm_ref[...]
        bu = _dot(u, bm)
        c_r, c_i = car_ref[0, 0:1, 0:S5_P], car_ref[0, 0:1, S5_P:]
        sr, si = _s5_scan(bu[:, :S5_P], bu[:, S5_P:], pr_ref, pi_ref, c_r, c_i, lb)
        cre_v, cim_v, dk, gw = cr_ref[...], ci_ref[...], dk_ref[...], gw_ref[...]
        y = _s5_y(u, sr, si, cre_v, cim_v, dk)
        gel = _gelu(y)
        gate = _sig(_dot(gel, gw) + gb_ref[...])
        dout = dy_ref[...]
        t1 = dout * gel * gate * (1.0 - gate)
        dgw_ref[...] += _dot(gel, t1, TN)
        dgb_ref[...] += _colsum(t1)
        dyv = (dout * gate + _dot(t1, gw, NT)) * _dgelu(y)
        ddk_ref[...] += _colsum(dyv * u)
        dcr_ref[...] += _dot(sr, dyv, TN)
        dci_ref[...] += _dot(si, dyv, TN)
        gr = _dot(dyv, cre_v, NT)
        gi = _dot(dyv, cim_v, NT)
        row = lax.broadcasted_iota(jnp.int32, (lb, S5_P), 0)
        k = 1
        while k < lb:
            lr, li = pr_ref[k - 1:k, :], pi_ref[k - 1:k, :]
            tr = jnp.where(row < lb - k, _roll(gr, lb - k), 0.0)
            ti = jnp.where(row < lb - k, _roll(gi, lb - k), 0.0)
            gr, gi = gr + lr * tr + li * ti, gi + lr * ti - li * tr
            k *= 2
        qr, qi = qr_ref[...], qi_ref[...]
        n_r, n_i = gc_ref[0:1, 0:S5_P], gc_ref[0:1, S5_P:]
        gr, gi = gr + qr * n_r + qi * n_i, gi + qr * n_i - qi * n_r
        gc_ref[0:1, 0:S5_P] = gr[0:1]
        gc_ref[0:1, S5_P:] = gi[0:1]
        gcat = jnp.concatenate([gr, gi], axis=1)
        dbm_ref[...] += _dot(u, gcat, TN)
        du_ref[...] = dyv * dk + _dot(gcat, bm, NT)
        spr = jnp.where(row >= 1, _roll(sr, 1), c_r)
        spi = jnp.where(row >= 1, _roll(si, 1), c_i)
        dlam_ref[0:1, :] += _colsum(gr * spr + gi * spi)
        dlam_ref[1:2, :] += _colsum(gi * spr - gr * spi)

    rev = lambda i: nb - 1 - i
    return pl.pallas_call(
        body, name="b_s5", grid=(nb,),
        in_specs=[pl.BlockSpec((lb, GW), lambda i: (rev(i), 5)), pl.BlockSpec((lb, GW), lambda i: (rev(i), 0)),
                  pl.BlockSpec((1, 1, 2 * S5_P), lambda i: (rev(i), 0, 0)),
                  _full((GW, 2 * S5_P)), _full((S5_P, GW)), _full((S5_P, GW)), _full((lb, S5_P)), _full((lb, S5_P)),
                  _full((lb, S5_P)), _full((lb, S5_P)), _row(GW), _full((GW, GW)), _row(GW)],
        out_specs=[pl.BlockSpec((lb, GW), lambda i: (rev(i), 0)), _full((GW, 2 * S5_P)), _full((S5_P, GW)), _full((S5_P, GW)),
                   _full((2, S5_P)), _row(GW), _full((GW, GW)), _row(GW)],
        out_shape=[jax.ShapeDtypeStruct((t, GW), F32), jax.ShapeDtypeStruct((GW, 2 * S5_P), F32),
                   jax.ShapeDtypeStruct((S5_P, GW), F32), jax.ShapeDtypeStruct((S5_P, GW), F32),
                   jax.ShapeDtypeStruct((2, S5_P), F32), jax.ShapeDtypeStruct((1, GW), F32),
                   jax.ShapeDtypeStruct((GW, GW), F32), jax.ShapeDtypeStruct((1, GW), F32)],
        scratch_shapes=[pltpu.VMEM((8, 2 * S5_P), F32)], compiler_params=_cparams(1),
    )(proj, dyd, carries, bmat, cre, cim, p_r, p_i, q_r, q_i, dsk, glu_w, glu_b)


def _group_norm(ys, bw):
    outs, stats = [], []
    for g, y in enumerate(ys):
        r, n = _rms(y)
        stats.append((r, n))
        outs.append(n * bw[:, GW * g:GW * (g + 1)])
    return jnp.concatenate(outs, axis=1), stats


def _f_out(ya, yb, yc, yd, bw, w_out, h, g1):
    t = h.shape[0]
    tb = _tblock(t)

    def body(a_ref, b_ref, c_ref, d_ref, bw_ref, w_ref, h_ref, g_ref, h2_ref, o_ref, cat_ref):
        cat, _ = _group_norm([a_ref[...], b_ref[...], c_ref[...], d_ref[...]], bw_ref[...])
        catb = cat.astype(BF16)
        cat_ref[...] = catb
        o = _dot(catb, w_ref[...])
        o_ref[...] = o
        h2_ref[...] = h_ref[...] + g_ref[...] * o

    yblk = pl.BlockSpec((tb, GW), lambda i: (i, 0))
    blk = pl.BlockSpec((tb, D), lambda i: (i, 0))
    return pl.pallas_call(
        body, name="f_out", grid=(t // tb,), in_specs=[yblk] * 4 + [_row(D), _full((D, D)), blk, _row(D)],
        out_specs=[blk, blk, blk],
        out_shape=[jax.ShapeDtypeStruct((t, D), F32), jax.ShapeDtypeStruct((t, D), F32), jax.ShapeDtypeStruct((t, D), BF16)],
        compiler_params=_cparams(1),
    )(ya, yb, yc, yd, bw, w_out, h, g1)


def _b_out(dh2, ya, yb, yc, yd, bw, w_out, g1):
    t = dh2.shape[0]
    tb = _tblock(t)

    def body(dh_ref, a_ref, b_ref, c_ref, d_ref, bw_ref, w_ref, g_ref, da_ref, db_ref, dc_ref, dd_ref, do_ref, dbw_ref):
        @pl.when(pl.program_id(0) == 0)
        def _():
            dbw_ref[...] = jnp.zeros_like(dbw_ref)

        do = (dh_ref[...] * g_ref[...]).astype(BF16)
        do_ref[...] = do
        dcat = _dot(do, w_ref[...], NT)
        bw_v = bw_ref[...]
        for g, (y_ref, dy_ref) in enumerate(((a_ref, da_ref), (b_ref, db_ref), (c_ref, dc_ref), (d_ref, dd_ref))):
            r, n = _rms(y_ref[...])
            dc = dcat[:, GW * g:GW * (g + 1)]
            dbw_ref[:, GW * g:GW * (g + 1)] += _colsum(dc * n)
            dy_ref[...] = _rms_bwd(r, n, dc * bw_v[:, GW * g:GW * (g + 1)])

    yblk = pl.BlockSpec((tb, GW), lambda i: (i, 0))
    blk = pl.BlockSpec((tb, D), lambda i: (i, 0))
    ysd = jax.ShapeDtypeStruct((t, GW), F32)
    return pl.pallas_call(
        body, name="b_out", grid=(t // tb,), in_specs=[blk] + [yblk] * 4 + [_row(D), _full((D, D)), _row(D)],
        out_specs=[yblk] * 4 + [blk, _row(D)],
        out_shape=[ysd] * 4 + [jax.ShapeDtypeStruct((t, D), BF16), jax.ShapeDtypeStruct((1, D), F32)],
        compiler_params=_cparams(1),
    )(dh2, ya, yb, yc, yd, bw, w_out, g1)


HB = 1024


def _f_mlp(h2, nw, sc, sh, g2, w1, w2):
    t = h2.shape[0]
    tb = _tblock(t)
    nk = HID // HB

    def body(h_ref, nw_ref, sc_ref, sh_ref, g_ref, w1_ref, w2_ref, h3_ref, m_ref, a_ref, v_ref):
        k = pl.program_id(1)

        @pl.when(k == 0)
        def _():
            _, n = _rms(h_ref[...])
            v_ref[...] = ((n * nw_ref[...]) * (1.0 + sc_ref[...]) + sh_ref[...]).astype(BF16)
            m_ref[...] = jnp.zeros_like(m_ref)

        a = _dot(v_ref[...], w1_ref[...])
        a_ref[...] = a
        ra = jnp.maximum(a, 0.0)
        m_ref[...] += _dot((ra * ra).astype(BF16), w2_ref[...])

        @pl.when(k == nk - 1)
        def _():
            h3_ref[...] = h_ref[...] + g_ref[...] * m_ref[...]

    blk = pl.BlockSpec((tb, D), lambda i, k: (i, 0))
    return pl.pallas_call(
        body, name="f_mlp", grid=(t // tb, nk),
        in_specs=[blk, _row(D), _row(D), _row(D), _row(D), pl.BlockSpec((D, HB), lambda i, k: (0, k)),
                  pl.BlockSpec((HB, D), lambda i, k: (k, 0))],
        out_specs=[blk, blk, pl.BlockSpec((tb, HB), lambda i, k: (i, k)), blk],
        out_shape=[jax.ShapeDtypeStruct((t, D), F32), jax.ShapeDtypeStruct((t, D), F32), jax.ShapeDtypeStruct((t, HID), F32),
                   jax.ShapeDtypeStruct((t, D), BF16)],
        compiler_params=_cparams(2),
    )(h2, nw, sc, sh, g2, w1, w2)


def _b_mlp(dh3, a, g2, w1, w2):
    t = dh3.shape[0]
    tb = _tblock(t)
    nk = HID // HB

    def body(dh_ref, a_ref, g_ref, w1_ref, w2_ref, dv_ref, da_ref, act_ref, dm_ref):
        k = pl.program_id(1)
        dm = (dh_ref[...] * g_ref[...]).astype(BF16)

        @pl.when(k == 0)
        def _():
            dm_ref[...] = dm
            dv_ref[...] = jnp.zeros_like(dv_ref)

        ra = jnp.maximum(a_ref[...], 0.0)
        act_ref[...] = (ra * ra).astype(BF16)
        da = (_dot(dm, w2_ref[...], NT) * (2.0 * ra)).astype(BF16)
        da_ref[...] = da
        dv_ref[...] += _dot(da, w1_ref[...], NT)

    blk = pl.BlockSpec((tb, D), lambda i, k: (i, 0))
    hblk = pl.BlockSpec((tb, HB), lambda i, k: (i, k))
    return pl.pallas_call(
        body, name="b_mlp", grid=(t // tb, nk),
        in_specs=[blk, hblk, _row(D), pl.BlockSpec((D, HB), lambda i, k: (0, k)), pl.BlockSpec((HB, D), lambda i, k: (k, 0))],
        out_specs=[blk, hblk, hblk, blk],
        out_shape=[jax.ShapeDtypeStruct((t, D), F32), jax.ShapeDtypeStruct((t, HID), BF16), jax.ShapeDtypeStruct((t, HID), BF16),
                   jax.ShapeDtypeStruct((t, D), BF16)],
        compiler_params=_cparams(2),
    )(dh3, a, g2, w1, w2)


def _b_final(h, tgt, fw):
    t = h.shape[0]
    tb = _tblock(t)

    def body(h_ref, t_ref, w_ref, dh_ref, loss_ref, dfw_ref):
        @pl.when(pl.program_id(0) == 0)
        def _():
            loss_ref[...] = jnp.zeros_like(loss_ref)
            dfw_ref[...] = jnp.zeros_like(dfw_ref)

        r, n = _rms(h_ref[...])
        wv = w_ref[...]
        err = n * wv - t_ref[...]
        loss_ref[...] += jnp.sum(err * err, keepdims=True) * (0.5 / D)
        dy = err * (1.0 / D)
        dfw_ref[...] += _colsum(dy * n)
        dh_ref[...] = _rms_bwd(r, n, dy * wv)

    blk = pl.BlockSpec((tb, D), lambda i: (i, 0))
    return pl.pallas_call(
        body, name="b_final", grid=(t // tb,), in_specs=[blk, blk, _row(D)], out_specs=[blk, _row(1), _row(D)],
        out_shape=[jax.ShapeDtypeStruct((t, D), F32), jax.ShapeDtypeStruct((1, 1), F32), jax.ShapeDtypeStruct((1, D), F32)],
        compiler_params=_cparams(1),
    )(h, tgt, fw)


_EYE16 = None


def _eye(n):
    return jnp.eye(n, dtype=F32)


def _pool_embed(pool_w):
    return jnp.einsum('gcd,gk->gckd', pool_w, _eye(4)).reshape(GW, GW)


def _pool_extract(m):
    return jnp.einsum('gcgd->gcd', m.reshape(4, 64, 4, 64))


def _bmat_embed(bb):
    return jnp.einsum('gph,gk->ghkp', bb, _eye(16)).reshape(GW, S5_P)


def _bmat_extract(m):
    return jnp.einsum('ghgp->gph', m.reshape(16, 16, 16, 64))


def _cmat_embed(cc):
    return jnp.einsum('ghp,gk->kpgh', cc, _eye(16)).reshape(S5_P, GW)


def _cmat_extract(m):
    return jnp.einsum('gpgh->ghp', m.reshape(16, 64, 16, 16))


def _pad_lanes(v, n=DTW):
    return jnp.pad(v.reshape(1, -1), ((0, 0), (0, n - v.shape[-1])))


def _layer_params(p, l, mod):
    q = {}
    q['mod'] = [mod[k:k + 1] for k in range(6)]
    q['nw1'] = p['norm_mix_w'][l:l + 1]
    q['nw2'] = p['norm_mlp_w'][l:l + 1]
    w_in = p['w_in'][l]
    q['w_main'] = jnp.concatenate([w_in[:, :1280], w_in[:, 2052:2308], w_in[:, 1280:2048]], axis=1)
    q['w_dt'] = jnp.pad(w_in[:, 2048:2052], ((0, 0), (0, DTW - 4)))
    q['pool_mat'] = _pool_embed(p['pool_w'][l])
    q['pool_scale'] = p['pool_scale'][l:l + 1]
    q['sconv_w'] = p['sconv_w'][l]
    q['conv_w'] = p['ssd_conv_w'][l]
    q['conv_b'] = p['ssd_conv_b'][l:l + 1]
    q['dt_bias'] = _pad_lanes(p['ssd_dt_bias'][l])
    q['a_log'] = _pad_lanes(p['ssd_a_log'][l])
    q['ssd_d'] = _pad_lanes(p['ssd_d'][l])
    q['s5_raw'] = (p['s5_a_re'][l], p['s5_a_im'][l], p['s5_log_step'][l].reshape(16, 1),
                   p['s5_b_re'][l].reshape(16, 1024), p['s5_b_im'][l].reshape(16, 1024))
    q['cre'] = _cmat_embed(p['s5_c_re'][l])
    q['cim'] = -_cmat_embed(p['s5_c_im'][l])
    q['s5_d'] = p['s5_d'][l:l + 1]
    q['glu_w'] = p['s5_glu_w'][l]
    q['glu_b'] = p['s5_glu_b'][l:l + 1]
    q['bw'] = p['branch_norm_w'][l:l + 1]
    q['w_out'] = p['w_out'][l]
    q['w1'] = p['mlp_w1'][l]
    q['w2'] = p['mlp_w2'][l]
    return q


def _layer_fwd(h, q):
    sh1, sc1, g1, sh2, sc2, g2 = q['mod']
    t = h.shape[0]
    s = {'h': h}
    s['proj'], s['dtp'], s['u'] = _f_in(h, q['nw1'], sc1, sh1, q['w_main'], q['w_dt'])
    s['ya'], s['yb'] = _f_ab(s['proj'], q['pool_mat'], q['pool_scale'], q['sconv_w'])
    s['yc'], s['ypre'], s['sprev'] = _f_ssd(s['proj'], s['dtp'], q['conv_w'], q['conv_b'], q['dt_bias'], q['a_log'], q['ssd_d'])
    lr, li, bbr, bbi, ars, ais = _s5_prep(*q['s5_raw'])
    s['bmat'] = jnp.concatenate([_bmat_embed(bbr.reshape(16, 64, 16)), _bmat_embed(bbi.reshape(16, 64, 16))], axis=1)
    s['tables'] = _s5_tables(ars.reshape(1, S5_P), ais.reshape(1, S5_P), _s5_block(t))
    s['yd'], s['carries'] = _f_s5(s['proj'], s['bmat'], q['cre'], q['cim'], s['tables'][0], s['tables'][1],
                                  q['s5_d'], q['glu_w'], q['glu_b'])
    s['h2'], s['o'], s['cat'] = _f_out(s['ya'], s['yb'], s['yc'], s['yd'], q['bw'], q['w_out'], h, g1)
    h3, s['m'], s['a'], s['v'] = _f_mlp(s['h2'], q['nw2'], sc2, sh2, g2, q['w1'], q['w2'])
    return h3, s


def _layer_bwd(dh3, q, s):
    sh1, sc1, g1, sh2, sc2, g2 = q['mod']
    g = {}
    dv, da, act, dm = _b_mlp(dh3, s['a'], g2, q['w1'], q['w2'])
    g['mlp_w1'] = _tn_matmul(s['v'], da, "dw1", col_major=True)
    g['mlp_w2'] = _tn_matmul(act, dm, "dw2")
    dh2, dsc2, dsh2, dnw2, dg2 = _b_normmod(dv, s['h2'], dh3, s['m'], q['nw2'], sc2, "b_norm_mlp")
    dya, dyb, dyc, dyd, do, dbw = _b_out(dh2, s['ya'], s['yb'], s['yc'], s['yd'], q['bw'], q['w_out'], g1)
    g['w_out'] = _tn_matmul(s['cat'], do, "dwout")
    g['branch_norm_w'] = dbw[0]
    dab, dpm, dps, dsw = _b_ab(s['proj'], dya, dyb, q['pool_mat'], q['pool_scale'], q['sconv_w'])
    g['pool_w'] = _pool_extract(dpm)
    g['pool_scale'] = dps[0]
    g['sconv_w'] = dsw
    dz, dxbc, ddt, dcw, dcb, ddtb, dal, ddk = _b_ssd(s['proj'], s['dtp'], s['ypre'], dyc, s['sprev'], q['conv_w'],
                                                     q['conv_b'], q['dt_bias'], q['a_log'], q['ssd_d'])
    g['ssd_conv_w'] = dcw
    g['ssd_conv_b'] = dcb[0]
    g['ssd_dt_bias'] = ddtb[0, :4]
    g['ssd_a_log'] = dal[0, :4]
    g['ssd_d'] = ddk[0, :4]
    tb = s['tables']
    ds5, dbmat, dcre, dcim, dlam, dd5, dgw, dgb = _b_s5(s['proj'], dyd, s['carries'], s['bmat'], q['cre'], q['cim'],
                                                        tb[0], tb[1], tb[2], tb[3], q['s5_d'], q['glu_w'], q['glu_b'])
    g['s5_c_re'] = _cmat_extract(dcre)
    g['s5_c_im'] = -_cmat_extract(dcim)
    g['s5_d'] = dd5[0]
    g['s5_glu_w'] = dgw
    g['s5_glu_b'] = dgb[0]
    dbbr = _bmat_extract(dbmat[:, :S5_P]).reshape(16, 1024)
    dbbi = _bmat_extract(dbmat[:, S5_P:]).reshape(16, 1024)
    dar, dai, dls, dbr, dbi = _s5_prep_bwd(*q['s5_raw'], dlam[0].reshape(16, 64), dlam[1].reshape(16, 64), dbbr, dbbi)
    g['s5_a_re'], g['s5_a_im'], g['s5_log_step'] = dar, dai, dls[:, 0]
    g['s5_b_re'], g['s5_b_im'] = dbr.reshape(16, 64, 16), dbi.reshape(16, 64, 16)
    du = _b_in_du(dab, dz, dxbc, ds5, ddt, q['w_main'], q['w_dt'])
    u = s['u']
    pieces = [_tn_matmul(u, dab, "dwin_ab"), _tn_matmul(u, dz, "dwin_z"), _tn_matmul(u, dxbc, "dwin_xbc"),
              _tn_matmul(u, ddt, "dwin_dt")[:, :4], _tn_matmul(u, ds5, "dwin_s5")]
    g['w_in'] = jnp.concatenate(pieces, axis=1)
    dh, dsc1, dsh1, dnw1, dg1 = _b_normmod(du, s['h'], dh2, s['o'], q['nw1'], sc1, "b_norm_mix")
    g['norm_mix_w'] = dnw1[0]
    g['norm_mlp_w'] = dnw2[0]
    dmod = jnp.concatenate([dsh1, dsc1, dg1, dsh2, dsc2, dg2], axis=1)
    return dh, g, dmod


def _local_step(x, tgt, p, mod):
    qs = [_layer_params(p, l, mod[l]) for l in range(2)]
    h = x
    saved = []
    for l in range(2):
        h, s = _layer_fwd(h, qs[l])
        saved.append(s)
    dh, loss, dfw = _b_final(h, tgt, p['final_norm_w'].reshape(1, D))
    grads = [None, None]
    dmods = [None, None]
    for l in (1, 0):
        dh, grads[l], dmods[l] = _layer_bwd(dh, qs[l], saved[l])
    out = {k: jnp.stack([grads[0][k], grads[1][k]]) for k in grads[0]}
    out['final_norm_w'] = dfw[0]
    return loss, dh, out, jnp.concatenate(dmods, axis=0)


def _pack(arrs):
    parts, rows = [], 0
    for a in arrs:
        f = a.reshape(-1).astype(F32)
        pad = (-f.shape[0]) % 1024
        f = jnp.pad(f, (0, pad)) if pad else f
        parts.append(f.reshape(-1, 128))
        rows += parts[-1].shape[0]
    if rows % 256:
        parts.append(jnp.zeros((256 - rows % 256, 128), F32))
    return jnp.concatenate(parts, axis=0)


def _unpack(buf, shapes):
    out, row = [], 0
    for shp in shapes:
        n = int(math.prod(shp)) if len(shp) else 1
        rows = (n + 1023) // 1024 * 8
        out.append(buf[row:row + rows].reshape(-1)[:n].reshape(shp))
        row += rows
    return out


def _shard_of(a, axis, k):
    n = a.shape[axis] // 4
    return lax.dynamic_slice_in_dim(a, k * n, n, axis)


def kernel(x, c, norm_mix_w, norm_mlp_w, ada_w, ada_b, w_in, pool_w, pool_scale, sconv_w, ssd_conv_w, ssd_conv_b, ssd_dt_bias, ssd_a_log, ssd_d, s5_a_re, s5_a_im, s5_log_step, s5_b_re, s5_b_im, s5_c_re, s5_c_im, s5_d, s5_glu_w, s5_glu_b, branch_norm_w, w_out, mlp_w1, mlp_w2, final_norm_w, loss_target, m_norm_mix_w, m_norm_mlp_w, m_ada_w, m_ada_b, m_w_in, m_pool_w, m_pool_scale, m_sconv_w, m_ssd_conv_w, m_ssd_conv_b, m_ssd_dt_bias, m_ssd_a_log, m_ssd_d, m_s5_a_re, m_s5_a_im, m_s5_log_step, m_s5_b_re, m_s5_b_im, m_s5_c_re, m_s5_c_im, m_s5_d, m_s5_glu_w, m_s5_glu_b, m_branch_norm_w, m_w_out, m_mlp_w1, m_mlp_w2, m_final_norm_w, v_norm_mix_w, v_norm_mlp_w, v_ada_w, v_ada_b, v_w_in, v_pool_w, v_pool_scale, v_sconv_w, v_ssd_conv_w, v_ssd_conv_b, v_ssd_dt_bias, v_ssd_a_log, v_ssd_d, v_s5_a_re, v_s5_a_im, v_s5_log_step, v_s5_b_re, v_s5_b_im, v_s5_c_re, v_s5_c_im, v_s5_d, v_s5_glu_w, v_s5_glu_b, v_branch_norm_w, v_w_out, v_mlp_w1, v_mlp_w2, v_final_norm_w):
    loc = locals()
    w = {n: loc[n] for n in WEIGHTS}
    mom = {n: loc['m_' + n] for n in WEIGHTS}
    var = {n: loc['v_' + n] for n in WEIGHTS}
    ix, iy, ic = lax.axis_index("x"), lax.axis_index("y"), lax.axis_index("c")
    chip = 2 * ix + iy
    dev = 4 * ix + 2 * iy + ic

    (c_all,) = _exchange([c], EVERYONE, False, "ag_cond")
    c_all = c_all.reshape(8, D)
    small_sh = _exchange([w[n] for n in SMALL_SHARDED], CHIPS, False, "ag_small")
    mine_of = lambda a: lax.dynamic_index_in_dim(a.astype(BF16), ic, axis=0, keepdims=False)
    pad_in = lambda a: jnp.pad(a.reshape(577, D), ((0, WIN_ROWS - 577), (0, 0)))
    big_l = _exchange([pad_in(mine_of(w['w_in'])), mine_of(w['w_out']), mine_of(w['mlp_w1']), mine_of(w['mlp_w2'])],
                      CHIPS, False, "ag_big")
    big_o = _pair_swap([a.reshape(-1, D) for a in big_l], False, "swap_big")
    big_sh = [jnp.stack([jnp.where(ic == l, a, b.reshape(a.shape)) for l in range(2)]) for a, b in zip(big_l, big_o)]
    p = dict(w)
    for n, g in zip(SMALL_SHARDED, small_sh):
        ax = SMALL_SHARDED[n]
        p[n] = jnp.concatenate([g[k] for k in range(4)], axis=ax)
    w_in_sh = big_sh[0][:, :, :577].reshape(2, 4, D, 577)
    p['w_in'] = jnp.concatenate([w_in_sh[:, k] for k in range(4)], axis=2)
    p['w_out'] = big_sh[1].reshape(2, D, D)
    p['mlp_w1'] = jnp.concatenate([big_sh[2][:, k] for k in range(4)], axis=2)
    p['mlp_w2'] = big_sh[3].reshape(2, HID, D)

    ada_b_sh = _shard_of(w['ada_b'], 1, chip).reshape(2, 1, 6 * D // 4)
    mod_sh = _ada_fwd(c_all, w['ada_w'], ada_b_sh)
    (mod_all,) = _exchange([mod_sh], CHIPS, False, "ag_mod")
    mine = lax.dynamic_index_in_dim(mod_all, dev, axis=2, keepdims=False)
    mod = jnp.transpose(mine, (1, 0, 2)).reshape(2, 6, D)

    loss, grad_x, g, dmod = _local_step(x[0], loss_target[0], p, mod)

    (dmod_all,) = _exchange([dmod], EVERYONE, False, "ag_dmod")
    dmod_all = jnp.transpose(dmod_all, (1, 0, 2))
    g_ada_w, g_ada_b = _ada_bwd(c_all, _shard_of(dmod_all, 2, chip), dmod_all)

    gw_in = jnp.transpose(g['w_in'].reshape(2, D, 4, 577), (0, 2, 1, 3)).reshape(2, 4, 577, D)
    gw_in = jnp.pad(gw_in, ((0, 0), (0, 0), (0, WIN_ROWS - 577), (0, 0)))
    gw_out = g['w_out'].reshape(2, 4, 256, D)
    gw1 = g['mlp_w1']
    gw2 = g['mlp_w2'].reshape(2, 4, 1024, D)
    gws = [gw_in, gw_out, gw1, gw2]
    got = _pair_swap([a.reshape(2, -1, D) for a in gws], True, "swap_grad")
    layer = ic.astype(jnp.int32).reshape(1)
    pair = [_pair_sum(a, b.reshape(a.shape[1:]), layer, "pair_sum%d" % k, BF16) for k, (a, b) in enumerate(zip(gws, got))]
    quad = _exchange(pair, CHIPS, True, "rs_chips")
    quad = [_sum_lead(a, "rs_chip_sum%d" % k, F32) for k, a in enumerate(quad)]
    other = _pair_swap(quad, False, "swap_red")
    both = [jnp.stack([jnp.where(ic == l, a, b) for l in range(2)]) for a, b in zip(quad, other)]
    both[0] = both[0][:, :577].reshape(2, D, 577)
    red = dict(zip(('w_in', 'w_out', 'mlp_w1', 'mlp_w2'), both))
    red['ada_w'] = g_ada_w

    small_names = [n for n in WEIGHTS if n not in BIG and n != 'ada_b']
    small_shapes = [g[n].shape for n in small_names] + [(1, 1)]
    packed = _pack([g[n] for n in small_names] + [loss])
    (packed_all,) = _exchange([packed], EVERYONE, False, "ag_smallgrad")
    summed = _unpack(_sum_lead(packed_all, "smallgrad_sum", F32), small_shapes)
    for n, a in zip(small_names, summed[:-1]):
        red[n] = _shard_of(a, SMALL_SHARDED[n], chip) if n in SMALL_SHARDED else a
    red['ada_b'] = g_ada_b
    loss_out = summed[-1].reshape(())

    delta, new_m, new_v = {}, {}, {}
    for n in BIG:
        delta[n], new_m[n], new_v[n] = _adamw(w[n], red[n], mom[n], var[n], "adamw_" + n)
    rest = [n for n in WEIGHTS if n not in BIG]
    shapes = [w[n].shape for n in rest]
    d_p, m_p, v_p = _adamw(_pack([w[n] for n in rest]), _pack([red[n] for n in rest]), _pack([mom[n] for n in rest]),
                           _pack([var[n] for n in rest]), "adamw_small")
    for n, a, b, cc in zip(rest, _unpack(d_p, shapes), _unpack(m_p, shapes), _unpack(v_p, shapes)):
        delta[n], new_m[n], new_v[n] = a, b, cc

    return (loss_out, grad_x[None], *[red[n] for n in WEIGHTS], *[delta[n] for n in WEIGHTS],
            *[new_m[n] for n in WEIGHTS], *[new_v[n] for n in WEIGHTS])
```

```python
import functools
import math

import jax
import jax.numpy as jnp
from jax import lax
from jax.experimental import pallas as pl
from jax.experimental.pallas import tpu as pltpu

F32 = jnp.float32
BF16 = jnp.bfloat16
HI = lax.Precision.HIGHEST

D = 1024
GW = 256
HID = 4096
EPS = 1e-6
PW = 2304
DTW = 128
SSD_L = 128
NH, HP, NS = 4, 64, 128
S5_P = 1024
MESH = pl.DeviceIdType.MESH

ADAM_LR, ADAM_B1, ADAM_B2, ADAM_EPS, ADAM_WD, ADAM_STEP = 0.001, 0.9, 0.999, 1e-08, 0.01, 10

NT = (((1,), (1,)), ((), ()))
TN = (((0,), (0,)), ((), ()))

WEIGHTS = ['norm_mix_w', 'norm_mlp_w', 'ada_w', 'ada_b', 'w_in', 'pool_w', 'pool_scale', 'sconv_w', 'ssd_conv_w',
           'ssd_conv_b', 'ssd_dt_bias', 'ssd_a_log', 'ssd_d', 's5_a_re', 's5_a_im', 's5_log_step', 's5_b_re', 's5_b_im',
           's5_c_re', 's5_c_im', 's5_d', 's5_glu_w', 's5_glu_b', 'branch_norm_w', 'w_out', 'mlp_w1', 'mlp_w2',
           'final_norm_w']
BIG = ('ada_w', 'w_in', 'w_out', 'mlp_w1', 'mlp_w2')
SMALL_SHARDED = {'sconv_w': 2, 'ssd_conv_w': 2, 's5_glu_w': 1}


def _cparams(n_axes, vmem_mb=48):
    return pltpu.CompilerParams(dimension_semantics=("arbitrary",) * n_axes, vmem_limit_bytes=vmem_mb * 1024 * 1024)


def _row(n):
    return pl.BlockSpec((1, n), lambda *_: (0, 0))


def _full(shape):
    nd = len(shape)
    return pl.BlockSpec(tuple(shape), lambda *_: (0,) * nd)


def _dot(a, b, dims=None, prec=None):
    if dims is None:
        dims = (((a.ndim - 1,), (0,)), ((), ()))
    return lax.dot_general(a, b, dims, preferred_element_type=F32, precision=prec)


def _bdot(a, b, dims=None):
    return _dot(a.astype(BF16), b.astype(BF16), dims)


def _sig(x):
    return jax.nn.sigmoid(x)


def _silu(x):
    return x * _sig(x)


def _dsilu(x):
    s = _sig(x)
    return s * (1.0 + x * (1.0 - s))


def _softplus(x):
    return jnp.maximum(x, 0.0) + jnp.log(1.0 + jnp.exp(-jnp.abs(x)))


_GK = math.sqrt(2.0 / math.pi)


def _gelu(x):
    return 0.5 * x * (1.0 + jnp.tanh(_GK * (x + 0.044715 * x * x * x)))


def _dgelu(x):
    th = jnp.tanh(_GK * (x + 0.044715 * x * x * x))
    return 0.5 * (1.0 + th) + 0.5 * x * (1.0 - th * th) * _GK * (1.0 + 3.0 * 0.044715 * x * x)


def _colsum(x):
    return jnp.sum(x, axis=0, keepdims=True)


def _rms(x):
    r = lax.rsqrt(jnp.mean(x * x, axis=-1, keepdims=True) + EPS)
    return r, x * r


def _rms_bwd(r, n, dn):
    return r * (dn - n * jnp.mean(dn * n, axis=-1, keepdims=True))


def _roll(x, k):
    n = x.shape[0]
    k = k % n
    return x if k == 0 else pltpu.roll(x, k, axis=0)


def _tblock(t, want=512):
    return min(t, want)


def _peer(mask):
    x, y, c = lax.axis_index("x"), lax.axis_index("y"), lax.axis_index("c")
    return (x ^ ((mask >> 2) & 1), y ^ ((mask >> 1) & 1), c ^ (mask & 1))


def _group_index(masks):
    x, y, c = lax.axis_index("x"), lax.axis_index("y"), lax.axis_index("c")
    full = 0
    for m in masks:
        full |= m
    bits = [b for b in (4, 2, 1) if full & b]

    def idx(px, py, pc):
        v = {4: px, 2: py, 1: pc}
        out = 0
        for b in bits:
            out = out * 2 + v[b]
        return out

    return idx(x, y, c), [idx(*_peer(m)) for m in masks]


def _exchange(arrs, masks, scatter, name, stage=False):
    n_arr, n_peer, n_grp = len(arrs), len(masks), len(masks) + 1

    def body(*refs):
        ins, outs = refs[:n_arr], refs[n_arr:2 * n_arr]
        send_sems, recv_sems, local_sems = refs[2 * n_arr:]
        me, peer_idx = _group_index(masks)
        copies = []
        for t in range(n_arr):
            src_me = ins[t].at[me] if scatter else ins[t]
            loc = pltpu.make_async_copy(src_me, outs[t].at[me], local_sems.at[t])
            loc.start()
            copies.append(loc)
            for j, m in enumerate(masks):
                src = ins[t].at[peer_idx[j]] if scatter else ins[t]
                cp = pltpu.make_async_remote_copy(src_ref=src, dst_ref=outs[t].at[me], send_sem=send_sems.at[t, j],
                                                  recv_sem=recv_sems.at[t, j], device_id=_peer(m), device_id_type=MESH)
                cp.start()
                copies.append(cp)
        for cp in copies:
            cp.wait()

    hbm = pl.BlockSpec(memory_space=pl.ANY)
    out_shape = [jax.ShapeDtypeStruct((n_grp,) + (a.shape[1:] if scatter else a.shape), a.dtype) for a in arrs]
    src_spec = pl.BlockSpec(memory_space=pltpu.VMEM) if stage else hbm
    outs = pl.pallas_call(
        body, name=name, in_specs=[src_spec] * n_arr, out_specs=[hbm] * n_arr, out_shape=out_shape,
        scratch_shapes=[pltpu.SemaphoreType.DMA((n_arr, n_peer)), pltpu.SemaphoreType.DMA((n_arr, n_peer)),
                        pltpu.SemaphoreType.DMA((n_arr,))],
    )(*arrs)
    return list(outs)


CHIPS = (4, 2, 6)
EVERYONE = (1, 2, 3, 4, 5, 6, 7)
SIBLING = (1,)
SWAP_ROWS = 512
WIN_ROWS = 592


def _pair_swap(arrs, other_layer, name):
    n_arr = len(arrs)
    shapes = [a.shape[-2:] for a in arrs]
    chunks = []
    for t, (rows, _) in enumerate(shapes):
        assert rows % 16 == 0
        for j, r0 in enumerate(range(0, rows, SWAP_ROWS)):
            chunks.append((t, r0, min(SWAP_ROWS, rows - r0), j % 2))

    def body(*refs):
        ins, outs = refs[:n_arr], refs[n_arr:2 * n_arr]
        bufs = refs[2 * n_arr:3 * n_arr]
        load_sems, send_sems, recv_sems = refs[3 * n_arr:]
        sibling = _peer(1)
        c = lax.axis_index("c")

        def load(k):
            t, r0, n, slot = chunks[k]
            src = ins[t].at[1 - c] if other_layer else ins[t]
            return pltpu.make_async_copy(src.at[pl.ds(r0, n)], bufs[t].at[slot, pl.ds(0, n)], load_sems.at[t, slot])

        def send(k):
            t, r0, n, slot = chunks[k]
            return pltpu.make_async_remote_copy(src_ref=bufs[t].at[slot, pl.ds(0, n)], dst_ref=outs[t].at[pl.ds(r0, n)],
                                                send_sem=send_sems.at[t, slot], recv_sem=recv_sems.at[t],
                                                device_id=sibling, device_id_type=MESH)

        in_flight = {}

        def start_load(k):
            key = (chunks[k][0], chunks[k][3])
            if key in in_flight:
                send(in_flight.pop(key)).wait_send()
            load(k).start()

        start_load(0)
        for k in range(len(chunks)):
            load(k).wait()
            if k + 1 < len(chunks):
                start_load(k + 1)
            send(k).start()
            in_flight[(chunks[k][0], chunks[k][3])] = k
        for k in in_flight.values():
            send(k).wait_send()
        for t in range(n_arr):
            pltpu.make_async_remote_copy(src_ref=outs[t], dst_ref=outs[t], send_sem=send_sems.at[t, 0],
                                         recv_sem=recv_sems.at[t], device_id=sibling, device_id_type=MESH).wait_recv()

    hbm = pl.BlockSpec(memory_space=pl.ANY)
    outs = pl.pallas_call(
        body, name=name, in_specs=[hbm] * n_arr, out_specs=[hbm] * n_arr,
        out_shape=[jax.ShapeDtypeStruct(s, a.dtype) for s, a in zip(shapes, arrs)],
        scratch_shapes=[pltpu.VMEM((2, min(SWAP_ROWS, s[0]), s[1]), a.dtype) for s, a in zip(shapes, arrs)]
        + [pltpu.SemaphoreType.DMA((n_arr, 2)), pltpu.SemaphoreType.DMA((n_arr, 2)), pltpu.SemaphoreType.DMA((n_arr,))],
        compiler_params=pltpu.CompilerParams(vmem_limit_bytes=48 * 1024 * 1024),
    )(*arrs)
    return list(outs)


def _sum_lead(a, name, out_dtype):
    n = a.shape[0]
    shape = a.shape[1:]

    def body(a_ref, o_ref):
        acc = a_ref[0].astype(F32)
        for k in range(1, n):
            acc = acc + a_ref[k].astype(F32)
        o_ref[...] = acc.astype(out_dtype)

    if len(shape) == 3:
        blk = (1,) + shape[1:]
        return pl.pallas_call(
            body, name=name, grid=(shape[0],), in_specs=[pl.BlockSpec((n,) + blk, lambda i: (0, i, 0, 0))],
            out_specs=pl.BlockSpec(blk, lambda i: (i, 0, 0)), out_shape=jax.ShapeDtypeStruct(shape, out_dtype),
            compiler_params=_cparams(1),
        )(a)
    rows, cols = shape
    rb = rows
    for cand in (512, 256, 128):
        if rows % cand == 0 and rows > cand:
            rb = cand
            break
    return pl.pallas_call(
        body, name=name, grid=(rows // rb,), in_specs=[pl.BlockSpec((n, rb, cols), lambda i: (0, i, 0))],
        out_specs=pl.BlockSpec((rb, cols), lambda i: (i, 0)), out_shape=jax.ShapeDtypeStruct((rows, cols), out_dtype),
        compiler_params=_cparams(1),
    )(a)


def _pair_sum(g, recv, layer, name, out_dtype):
    _, n, r, c = g.shape

    def body(l_ref, g_ref, r_ref, o_ref):
        o_ref[...] = (g_ref[0].astype(F32) + r_ref[...].astype(F32)).astype(out_dtype)

    return pl.pallas_call(
        body, name=name,
        grid_spec=pltpu.PrefetchScalarGridSpec(
            num_scalar_prefetch=1, grid=(n,),
            in_specs=[pl.BlockSpec((1, 1, r, c), lambda i, l: (l[0], i, 0, 0)), pl.BlockSpec((1, r, c), lambda i, l: (i, 0, 0))],
            out_specs=pl.BlockSpec((1, r, c), lambda i, l: (i, 0, 0))),
        out_shape=jax.ShapeDtypeStruct((n, r, c), out_dtype), compiler_params=_cparams(1),
    )(layer, g, recv)


def _tn_matmul(a, b, name, col_major=False):
    t, k = a.shape
    n = b.shape[1]
    tb = _tblock(t, 1024)
    kb = min(k, 1024)
    nb = min(n, 1024)
    grid = (k // kb, n // nb, t // tb)

    def body(a_ref, b_ref, o_ref):
        @pl.when(pl.program_id(2) == 0)
        def _():
            o_ref[...] = jnp.zeros_like(o_ref)

        acc = _bdot(a_ref[...], b_ref[...], TN)
        if col_major:
            o_ref[0] += acc
        else:
            o_ref[...] += acc

    if col_major:
        out_spec = pl.BlockSpec((1, kb, nb), lambda ki, ni, ti: (ni, ki, 0))
        out_shape = jax.ShapeDtypeStruct((n // nb, k, nb), F32)
    else:
        out_spec = pl.BlockSpec((kb, nb), lambda ki, ni, ti: (ki, ni))
        out_shape = jax.ShapeDtypeStruct((k, n), F32)
    return pl.pallas_call(
        body, name=name, grid=grid,
        in_specs=[pl.BlockSpec((tb, kb), lambda ki, ni, ti: (ti, ki)), pl.BlockSpec((tb, nb), lambda ki, ni, ti: (ti, ni))],
        out_specs=out_spec, out_shape=out_shape, compiler_params=_cparams(3),
    )(a, b)


def _adamw(w, g, m, v, name):
    shape = w.shape
    cols = shape[-1]
    rows = int(math.prod(shape[:-1]))
    rb = rows
    for cand in (256, 128, 64, 32, 16, 8):
        if rows % cand == 0 and rows > cand:
            rb = cand
            break
    bc1 = 1.0 - ADAM_B1 ** ADAM_STEP
    bc2 = 1.0 - ADAM_B2 ** ADAM_STEP

    def body(w_ref, g_ref, m_ref, v_ref, d_ref, nm_ref, nv_ref):
        gg = g_ref[...]
        m2 = ADAM_B1 * m_ref[...] + (1.0 - ADAM_B1) * gg
        v2 = ADAM_B2 * v_ref[...] + (1.0 - ADAM_B2) * (gg * gg)
        m_hat = m2 / bc1
        v_hat = v2 / bc2
        d_ref[...] = -ADAM_LR * (m_hat / (jnp.sqrt(v_hat) + ADAM_EPS) + ADAM_WD * w_ref[...])
        nm_ref[...] = m2
        nv_ref[...] = v2

    spec = pl.BlockSpec((rb, cols), lambda i: (i, 0))
    sds = jax.ShapeDtypeStruct((rows, cols), F32)
    outs = pl.pallas_call(
        body, name=name, grid=(rows // rb,), in_specs=[spec] * 4, out_specs=[spec] * 3, out_shape=[sds] * 3,
        compiler_params=_cparams(1),
    )(*(z.reshape(rows, cols) for z in (w, g, m, v)))
    return tuple(o.reshape(shape) for o in outs)


def _ada_fwd(c_all, ada_w_sh, ada_b_sh):
    s = ada_w_sh.shape[2]
    sb = 512

    def body(c_ref, w_ref, b_ref, o_ref):
        cond = _silu(c_ref[...])
        o_ref[0] = _bdot(cond, w_ref[0]) + b_ref[0]

    return pl.pallas_call(
        body, name="ada_fwd", grid=(2, s // sb),
        in_specs=[_full((8, D)), pl.BlockSpec((1, D, sb), lambda l, j: (l, 0, j)), pl.BlockSpec((1, 1, sb), lambda l, j: (l, 0, j))],
        out_specs=pl.BlockSpec((1, 8, sb), lambda l, j: (l, 0, j)), out_shape=jax.ShapeDtypeStruct((2, 8, s), F32),
        compiler_params=_cparams(2),
    )(c_all, ada_w_sh, ada_b_sh)


def _ada_bwd(c_all, dmod_sh, dmod_all):
    s = dmod_sh.shape[2]
    sb = 512

    def body(c_ref, d_ref, o_ref):
        cond = _silu(c_ref[...])
        o_ref[0] = _bdot(cond, d_ref[0], TN)

    gw = pl.pallas_call(
        body, name="ada_bwd_w", grid=(2, s // sb),
        in_specs=[_full((8, D)), pl.BlockSpec((1, 8, sb), lambda l, j: (l, 0, j))],
        out_specs=pl.BlockSpec((1, D, sb), lambda l, j: (l, 0, j)), out_shape=jax.ShapeDtypeStruct((2, D, s), F32),
        compiler_params=_cparams(2),
    )(c_all, dmod_sh)

    def body_b(d_ref, o_ref):
        acc = d_ref[0, 0:1, :]
        for k in range(1, 8):
            acc = acc + d_ref[0, k:k + 1, :]
        o_ref[0] = acc

    gb = pl.pallas_call(
        body_b, name="ada_bwd_b", grid=(2,), in_specs=[pl.BlockSpec((1, 8, 6 * D), lambda l: (l, 0, 0))],
        out_specs=pl.BlockSpec((1, 1, 6 * D), lambda l: (l, 0, 0)), out_shape=jax.ShapeDtypeStruct((2, 1, 6 * D), F32),
        compiler_params=_cparams(1),
    )(dmod_all)
    return gw, gb.reshape(2, 6 * D)


def _f_in(h, nw, sc, sh, w_main, w_dt):
    t = h.shape[0]
    tb = _tblock(t)

    def body(h_ref, nw_ref, sc_ref, sh_ref, w_ref, wd_ref, p_ref, dt_ref, u_ref):
        _, n = _rms(h_ref[...])
        u = ((n * nw_ref[...]) * (1.0 + sc_ref[...]) + sh_ref[...]).astype(BF16)
        u_ref[...] = u
        p_ref[...] = _dot(u, w_ref[...])
        dt_ref[...] = _dot(u, wd_ref[...])

    return pl.pallas_call(
        body, name="f_in", grid=(t // tb,),
        in_specs=[pl.BlockSpec((tb, D), lambda i: (i, 0)), _row(D), _row(D), _row(D), _full((D, PW)), _full((D, DTW))],
        out_specs=[pl.BlockSpec((tb, PW), lambda i: (i, 0)), pl.BlockSpec((tb, DTW), lambda i: (i, 0)),
                   pl.BlockSpec((tb, D), lambda i: (i, 0))],
        out_shape=[jax.ShapeDtypeStruct((t, PW), F32), jax.ShapeDtypeStruct((t, DTW), F32), jax.ShapeDtypeStruct((t, D), BF16)],
        compiler_params=_cparams(1),
    )(h, nw, sc, sh, w_main, w_dt)


def _b_in_du(dab, dz, dxbc, ds5, ddt, w_main, w_dt):
    t = dab.shape[0]
    tb = _tblock(t)

    def body(a_ref, z_ref, x_ref, s_ref, d_ref, w_ref, wd_ref, o_ref):
        acc = _bdot(a_ref[...], w_ref[:, 0:1024], NT)
        acc += _bdot(z_ref[...], w_ref[:, 1024:1280], NT)
        acc += _bdot(s_ref[...], w_ref[:, 1280:1536], NT)
        acc += _bdot(x_ref[...], w_ref[:, 1536:2304], NT)
        acc += _bdot(d_ref[...], wd_ref[...], NT)
        o_ref[...] = acc

    blk = lambda n: pl.BlockSpec((tb, n), lambda i: (i, 0))
    return pl.pallas_call(
        body, name="b_in_du", grid=(t // tb,),
        in_specs=[blk(1024), blk(256), blk(768), blk(256), blk(DTW), _full((D, PW)), _full((D, DTW))],
        out_specs=blk(D), out_shape=jax.ShapeDtypeStruct((t, D), F32), compiler_params=_cparams(1),
    )(dab, dz, dxbc, ds5, ddt, w_main, w_dt)


def _b_normmod(du, x, dres, gated, nw, sc, name):
    t = x.shape[0]
    tb = _tblock(t)

    def body(du_ref, x_ref, dr_ref, g_ref, nw_ref, sc_ref, dx_ref, dsc_ref, dsh_ref, dnw_ref, dg_ref):
        @pl.when(pl.program_id(0) == 0)
        def _():
            for r in (dsc_ref, dsh_ref, dnw_ref, dg_ref):
                r[...] = jnp.zeros_like(r)

        du_v = du_ref[...]
        r, n = _rms(x_ref[...])
        nwv = nw_ref[...]
        scale = 1.0 + sc_ref[...]
        dsc_ref[...] += _colsum(du_v * (n * nwv))
        dsh_ref[...] += _colsum(du_v)
        dnw_ref[...] += _colsum(du_v * scale * n)
        dres_v = dr_ref[...]
        dg_ref[...] += _colsum(dres_v * g_ref[...])
        dx_ref[...] = dres_v + _rms_bwd(r, n, du_v * scale * nwv)

    blk = pl.BlockSpec((tb, D), lambda i: (i, 0))
    row = jax.ShapeDtypeStruct((1, D), F32)
    return pl.pallas_call(
        body, name=name, grid=(t // tb,), in_specs=[blk, blk, blk, blk, _row(D), _row(D)],
        out_specs=[blk, _row(D), _row(D), _row(D), _row(D)], out_shape=[jax.ShapeDtypeStruct((t, D), F32), row, row, row, row],
        compiler_params=_cparams(1),
    )(du, x, dres, gated, nw, sc)


HALO = 16


def _lane_group(shape):
    return lax.broadcasted_iota(jnp.int32, shape, 1) // 64


def _window_select(g, s2, s4, s8, s16):
    return jnp.where(g == 0, s2, jnp.where(g == 1, s4, jnp.where(g == 2, s8, s16)))


def _pool_count(t0, rows):
    g = _lane_group((rows, GW))
    win = _window_select(g, 2, 4, 8, 16)
    tt = t0 + lax.broadcasted_iota(jnp.int32, (rows, GW), 0)
    return jnp.minimum(tt + 1, win).astype(F32)


def _pool_p(v_ext, t0, tb):
    s2 = v_ext + _roll(v_ext, 1)
    s4 = s2 + _roll(s2, 2)
    s8 = s4 + _roll(s4, 4)
    s16 = s8 + _roll(s8, 8)
    ws = _window_select(_lane_group(v_ext.shape), s2, s4, s8, s16)[HALO:]
    return ws / _pool_count(t0, tb) - v_ext[HALO:]


def _sconv(q_ext, w):
    return (_roll(q_ext, 2) * w[0:1] + _roll(q_ext, 1) * w[1:2] + q_ext * w[2:3])[HALO:]


def _halo_specs(t, tb, cols, col_block):
    per = tb // HALO
    last = t // HALO - 1
    prev = pl.BlockSpec((HALO, cols), lambda i: (jnp.maximum(i * per - 1, 0), col_block))
    nxt = pl.BlockSpec((HALO, cols), lambda i: (jnp.minimum((i + 1) * per, last), col_block))
    return prev, nxt


def _f_ab(proj, pool_mat, pool_scale, sconv_w):
    t = proj.shape[0]
    tb = _tblock(t)
    prev, _ = _halo_specs(t, tb, 1024, 0)

    def body(p_ref, h_ref, pm_ref, ps_ref, sw_ref, ya_ref, yb_ref):
        i = pl.program_id(0)
        halo = jnp.where(i > 0, h_ref[...], 0.0)
        ext = jnp.concatenate([halo, p_ref[...]], axis=0)
        p = _pool_p(ext[:, 0:256], i * tb, tb)
        ya_ref[...] = _bdot(p, pm_ref[...]) * ps_ref[...]
        q_ext = ext[:, 512:768] * ext[:, 768:1024]
        yb_ref[...] = p_ref[:, 256:512] * _sconv(q_ext, sw_ref[...])

    blk = pl.BlockSpec((tb, GW), lambda i: (i, 0))
    sds = jax.ShapeDtypeStruct((t, GW), F32)
    return pl.pallas_call(
        body, name="f_ab", grid=(t // tb,),
        in_specs=[pl.BlockSpec((tb, 1024), lambda i: (i, 0)), prev, _full((GW, GW)), _row(GW), _full((3, GW))],
        out_specs=[blk, blk], out_shape=[sds, sds], compiler_params=_cparams(1),
    )(proj, proj, pool_mat, pool_scale, sconv_w)


def _b_ab(proj, dya, dyb, pool_mat, pool_scale, sconv_w):
    t = proj.shape[0]
    tb = _tblock(t)
    nb = t // tb
    prev, nxt = _halo_specs(t, tb, 1024, 0)
    _, nxt_g = _halo_specs(t, tb, GW, 0)
    n_ext = tb + HALO

    def body(p_ref, hp_ref, hn_ref, da_ref, dan_ref, db_ref, dbn_ref, pm_ref, ps_ref, sw_ref,
             o_ref, dpm_ref, dps_ref, dsw_ref):
        i = pl.program_id(0)

        @pl.when(i == 0)
        def _():
            for r in (dpm_ref, dps_ref, dsw_ref):
                r[...] = jnp.zeros_like(r)

        last = i == nb - 1
        halo = jnp.where(i > 0, hp_ref[...], 0.0)
        main = p_ref[...]
        ext = jnp.concatenate([halo, main], axis=0)
        scale = ps_ref[...]
        pm = pm_ref[...]
        p = _pool_p(ext[:, 0:256], i * tb, tb)
        da = da_ref[...]
        dps_ref[...] += _colsum(da * _bdot(p, pm))
        da_ext = jnp.concatenate([da, jnp.where(last, 0.0, dan_ref[...])], axis=0)
        dys = da_ext * scale
        dpm_ref[...] += _bdot(p, dys[:tb], TN)
        dp = _bdot(dys, pm, NT)
        dpc = dp / _pool_count(i * tb, n_ext)
        a2 = dpc + _roll(dpc, n_ext - 1)
        a4 = a2 + _roll(a2, n_ext - 2)
        a8 = a4 + _roll(a4, n_ext - 4)
        a16 = a8 + _roll(a8, n_ext - 8)
        o_ref[:, 0:256] = (_window_select(_lane_group(dpc.shape), a2, a4, a8, a16) - dp)[:tb]
        w = sw_ref[...]
        gb, gc, hh = main[:, 256:512], main[:, 512:768], main[:, 768:1024]
        q_ext = ext[:, 512:768] * ext[:, 768:1024]
        db = db_ref[...]
        o_ref[:, 256:512] = db * _sconv(q_ext, w)
        gb_next = hn_ref[:, 256:512]
        dconv = jnp.concatenate([db * gb, jnp.where(last, 0.0, dbn_ref[...] * gb_next)], axis=0)
        dq = (dconv * w[2:3] + _roll(dconv, n_ext - 1) * w[1:2] + _roll(dconv, n_ext - 2) * w[0:1])[:tb]
        o_ref[:, 512:768] = dq * hh
        o_ref[:, 768:1024] = dq * gc
        dc = dconv[:tb]
        dsw_ref[0:1, :] += _colsum(dc * _roll(q_ext, 2)[HALO:])
        dsw_ref[1:2, :] += _colsum(dc * _roll(q_ext, 1)[HALO:])
        dsw_ref[2:3, :] += _colsum(dc * q_ext[HALO:])

    blk = pl.BlockSpec((tb, GW), lambda i: (i, 0))
    return pl.pallas_call(
        body, name="b_ab", grid=(nb,),
        in_specs=[pl.BlockSpec((tb, 1024), lambda i: (i, 0)), prev, nxt, blk, nxt_g, blk, nxt_g,
                  _full((GW, GW)), _row(GW), _full((3, GW))],
        out_specs=[pl.BlockSpec((tb, 1024), lambda i: (i, 0)), _full((GW, GW)), _row(GW), _full((3, GW))],
        out_shape=[jax.ShapeDtypeStruct((t, 1024), F32), jax.ShapeDtypeStruct((GW, GW), F32),
                   jax.ShapeDtypeStruct((1, GW), F32), jax.ShapeDtypeStruct((3, GW), F32)],
        compiler_params=_cparams(1),
    )(proj, proj, proj, dya, dya, dyb, dyb, pool_mat, pool_scale, sconv_w)


CH = 8


def _ssd_conv(x, halo, w, b):
    ext = jnp.concatenate([halo, x], axis=0)
    pre = ext * w[3:4] + _roll(ext, 1) * w[2:3] + _roll(ext, 2) * w[1:2] + _roll(ext, 3) * w[0:1] + b
    return pre[CH:], ext


def _ssd_common(dt_raw, dtb, alog):
    ll = dt_raw.shape[0]
    dtv = _softplus(dt_raw + dtb)
    a_row = -jnp.exp(alog)
    r = lax.broadcasted_iota(jnp.int32, (ll, ll), 0)
    c = lax.broadcasted_iota(jnp.int32, (ll, ll), 1)
    tril = (r >= c).astype(F32)
    cs = _dot(tril, dtv * a_row, prec=HI)
    return dtv, a_row, cs, cs.T, r >= c


def _ssd_bc(act_b, g):
    return act_b[:, 256 + NS * g:256 + NS * (g + 1)], act_b[:, 512 + NS * g:512 + NS * (g + 1)]


def _ssd_gmat(act_b):
    return [_dot(_ssd_bc(act_b, g)[1], _ssd_bc(act_b, g)[0], NT) for g in range(2)]


def _ssd_head(h, act, act_b, dtv, cs, cs_t, causal, gmat):
    g = h // 2
    xs = act[:, HP * h:HP * (h + 1)]
    bm, cm = _ssd_bc(act_b, g)
    cs_c = cs[:, h:h + 1]
    cs_r = cs_t[h:h + 1, :]
    mdec = jnp.where(causal, jnp.exp(jnp.minimum(cs_c - cs_r, 0.0)), 0.0)
    sc = gmat[g] * mdec
    dt_c = dtv[:, h:h + 1]
    xdt = xs * dt_c
    e = jnp.exp(cs_c)
    cs_last = cs[SSD_L - 1:SSD_L, h:h + 1]
    wdec = jnp.exp(cs_last - cs_c)
    return xs, bm, cm, cs_c, mdec, sc, dt_c, xdt, e, cs_last, wdec


def _f_ssd(proj, dtp, conv_w, conv_b, dt_bias, a_log, d_skip):
    t = proj.shape[0]
    nc = t // SSD_L
    per = SSD_L // CH

    def body(x_ref, hx_ref, dt_ref, z_ref, cw_ref, cb_ref, dtb_ref, al_ref, dk_ref, y_ref, yp_ref, sp_ref, s_ref):
        i = pl.program_id(0)

        @pl.when(i == 0)
        def _():
            s_ref[...] = jnp.zeros_like(s_ref)

        halo = jnp.where(i > 0, hx_ref[...], 0.0)
        pre, _ = _ssd_conv(x_ref[...], halo, cw_ref[...], cb_ref[...])
        act = _silu(pre)
        dtv, _, cs, cs_t, causal = _ssd_common(dt_ref[...], dtb_ref[...], al_ref[...])
        gmat = _ssd_gmat(act)
        for h in range(NH):
            xs, bm, cm, _, _, sc, _, xdt, e, cs_last, wdec = _ssd_head(h, act, act, dtv, cs, cs_t, causal, gmat)
            prev = s_ref[h]
            sp_ref[0, h] = prev
            y = _dot(sc, xdt) + e * _dot(cm, prev, NT) + xs * dk_ref[0:1, h:h + 1]
            yp_ref[:, HP * h:HP * (h + 1)] = y
            s_ref[h] = prev * jnp.exp(cs_last) + _dot(xdt * wdec, bm, TN)
        y_ref[...] = yp_ref[...] * _silu(z_ref[...])

    blk = pl.BlockSpec((SSD_L, GW), lambda i: (i, 0))
    sds = jax.ShapeDtypeStruct((t, GW), F32)
    return pl.pallas_call(
        body, name="f_ssd", grid=(nc,),
        in_specs=[pl.BlockSpec((SSD_L, 768), lambda i: (i, 2)),
                  pl.BlockSpec((CH, 768), lambda i: (jnp.maximum(i * per - 1, 0), 2)),
                  pl.BlockSpec((SSD_L, DTW), lambda i: (i, 0)),
                  pl.BlockSpec((SSD_L, GW), lambda i: (i, 4)),
                  _full((4, 768)), _row(768), _row(DTW), _row(DTW), _row(DTW)],
        out_specs=[blk, blk, pl.BlockSpec((1, NH, HP, NS), lambda i: (i, 0, 0, 0))],
        out_shape=[sds, sds, jax.ShapeDtypeStruct((nc, NH, HP, NS), F32)],
        scratch_shapes=[pltpu.VMEM((NH, HP, NS), F32)], compiler_params=_cparams(1),
    )(proj, proj, dtp, proj, conv_w, conv_b, dt_bias, a_log, d_skip)


def _b_ssd(proj, dtp, ypre, dyc, sprev, conv_w, conv_b, dt_bias, a_log, d_skip):
    t = proj.shape[0]
    nc = t // SSD_L
    per = SSD_L // CH
    n_ext = SSD_L + CH

    def body(x_ref, hx_ref, dt_ref, z_ref, yp_ref, dy_ref, sp_ref, cw_ref, cb_ref, dtb_ref, al_ref, dk_ref,
             dz_ref, dx_ref, ddt_ref, dcw_ref, dcb_ref, ddtb_ref, dal_ref, ddk_ref, ds_ref, dnext_ref, dact_ref):
        i = pl.program_id(0)

        @pl.when(i == 0)
        def _():
            ds_ref[...] = jnp.zeros_like(ds_ref)
            dnext_ref[...] = jnp.zeros_like(dnext_ref)
            for r in (dcw_ref, dcb_ref, ddtb_ref, dal_ref, ddk_ref):
                r[...] = jnp.zeros_like(r)

        first_chunk = i == nc - 1
        halo = jnp.where(first_chunk, 0.0, hx_ref[...])
        w = cw_ref[...]
        pre, ext = _ssd_conv(x_ref[...], halo, w, cb_ref[...])
        act = _silu(pre)
        dt_raw = dt_ref[...]
        dtv, a_row, cs, cs_t, causal = _ssd_common(dt_raw, dtb_ref[...], al_ref[...])
        act_b = act.astype(BF16)
        gmat = _ssd_gmat(act)
        z = z_ref[...]
        dyc_v = dy_ref[...]
        dz_ref[...] = dyc_v * yp_ref[...] * _dsilu(z)
        dy_all = dyc_v * _silu(z)
        lane = lax.broadcasted_iota(jnp.int32, (SSD_L, DTW), 1)
        rowi = lax.broadcasted_iota(jnp.int32, (SSD_L, 1), 0)
        dcs_mat = jnp.zeros((SSD_L, DTW), F32)
        ddtx_mat = jnp.zeros((SSD_L, DTW), F32)
        ddk_row = jnp.zeros((1, DTW), F32)
        lane1 = lax.broadcasted_iota(jnp.int32, (1, DTW), 1)
        dbm = [None, None]
        dcm = [None, None]
        for h in range(NH):
            g = h // 2
            xs, bm, cm, _, mdec, sc, dt_c, xdt, e, cs_last, wdec = _ssd_head(h, act, act_b, dtv, cs, cs_t, causal, gmat)
            dy = dy_all[:, HP * h:HP * (h + 1)]
            prev = sp_ref[0, h]
            ds = ds_ref[h]
            dy_b, xdt_b, prev_b, ds_b = dy.astype(BF16), xdt.astype(BF16), prev.astype(BF16), ds.astype(BF16)
            dsc = _dot(dy, xdt, NT)
            q = dsc * sc
            dg = (dsc * mdec).astype(BF16)
            dxdt = _dot(sc.astype(BF16), dy_b, TN)
            dcs = jnp.sum(q, axis=1, keepdims=True) - jnp.sum(q.T, axis=1, keepdims=True)
            dc_h = _dot(dg, bm)
            db_h = _dot(dg, cm, TN)
            bm32, cm32 = _ssd_bc(act, g)
            cp = _dot(cm32, prev, NT)
            dcs += jnp.sum(dy * cp, axis=1, keepdims=True) * e
            ey = (e * dy).astype(BF16)
            dc_h += _dot(ey, prev_b)
            dprev = _dot(ey, cm, TN)
            elast = jnp.exp(cs_last)
            dprev += ds * elast
            dcs_last = jnp.sum(ds * prev, keepdims=True) * elast
            bds = _dot(bm32, ds, NT)
            dxdt += wdec * bds
            db_h += wdec * _dot(xdt_b, ds_b)
            dw = jnp.sum(xdt * bds, axis=1, keepdims=True) * wdec
            dcs -= dw
            dcs_last += jnp.sum(dw, keepdims=True)
            dcs += jnp.where(rowi == SSD_L - 1, dcs_last, 0.0)
            ds_ref[h] = dprev
            dact_ref[:, HP * h:HP * (h + 1)] = dxdt * dt_c + dy * dk_ref[0:1, h:h + 1]
            dcs_mat = jnp.where(lane == h, dcs, dcs_mat)
            ddtx_mat = jnp.where(lane == h, jnp.sum(dxdt * xs, axis=1, keepdims=True), ddtx_mat)
            ddk_row = jnp.where(lane1 == h, jnp.sum(dy * xs, keepdims=True), ddk_row)
            dbm[g] = db_h if dbm[g] is None else dbm[g] + db_h
            dcm[g] = dc_h if dcm[g] is None else dcm[g] + dc_h
        for g in range(2):
            dact_ref[:, 256 + NS * g:256 + NS * (g + 1)] = dbm[g]
            dact_ref[:, 512 + NS * g:512 + NS * (g + 1)] = dcm[g]
        ddk_ref[...] += ddk_row
        r2 = lax.broadcasted_iota(jnp.int32, (SSD_L, SSD_L), 0)
        c2 = lax.broadcasted_iota(jnp.int32, (SSD_L, SSD_L), 1)
        dadt = _dot((c2 >= r2).astype(F32), dcs_mat, prec=HI)
        dal_ref[...] += _colsum(dadt * dtv) * a_row
        ddt = jnp.where(lane < NH, (dadt * a_row + ddtx_mat) * _sig(dt_raw + dtb_ref[...]), 0.0)
        ddt_ref[...] = ddt
        ddtb_ref[...] += _colsum(ddt)
        dpre = dact_ref[...] * _dsilu(pre)
        dcb_ref[...] += _colsum(dpre)
        for k in range(4):
            dcw_ref[k:k + 1, :] += _colsum(dpre * _roll(ext, 3 - k)[CH:])
        dext = jnp.concatenate([dpre, dnext_ref[...]], axis=0)
        dx_ref[...] = (dext * w[3:4] + _roll(dext, n_ext - 1) * w[2:3] + _roll(dext, n_ext - 2) * w[1:2]
                       + _roll(dext, n_ext - 3) * w[0:1])[:SSD_L]
        dnext_ref[...] = dpre[0:CH]

    rev = lambda i: nc - 1 - i
    blk = lambda n, cb=0: pl.BlockSpec((SSD_L, n), lambda i: (rev(i), cb))
    row = lambda n: jax.ShapeDtypeStruct((1, n), F32)
    return pl.pallas_call(
        body, name="b_ssd", grid=(nc,),
        in_specs=[blk(768, 2), pl.BlockSpec((CH, 768), lambda i: (jnp.maximum(rev(i) * per - 1, 0), 2)),
                  blk(DTW), blk(GW, 4), blk(GW), blk(GW), pl.BlockSpec((1, NH, HP, NS), lambda i: (rev(i), 0, 0, 0)),
                  _full((4, 768)), _row(768), _row(DTW), _row(DTW), _row(DTW)],
        out_specs=[blk(GW), blk(768), blk(DTW), _full((4, 768)), _row(768), _row(DTW), _row(DTW), _row(DTW)],
        out_shape=[jax.ShapeDtypeStruct((t, GW), F32), jax.ShapeDtypeStruct((t, 768), F32), jax.ShapeDtypeStruct((t, DTW), F32),
                   jax.ShapeDtypeStruct((4, 768), F32), row(768), row(DTW), row(DTW), row(DTW)],
        scratch_shapes=[pltpu.VMEM((NH, HP, NS), F32), pltpu.VMEM((CH, 768), F32), pltpu.VMEM((SSD_L, 768), F32)],
        compiler_params=_cparams(1),
    )(proj, proj, dtp, proj, ypre, dyc, sprev, conv_w, conv_b, dt_bias, a_log, d_skip)


def _s5_block(t):
    return min(t, 256)


def _seg_t():
    r = lax.broadcasted_iota(jnp.int32, (64, 1024), 0)
    c = lax.broadcasted_iota(jnp.int32, (64, 1024), 1)
    return (c // 16 == r).astype(F32)


def _s5_prep_math(a_re, a_im, lstep, b_re, b_im):
    step = jnp.exp(lstep)
    ars = a_re * step
    ais = a_im * step
    mag = jnp.exp(ars)
    lr = mag * jnp.cos(ais)
    li = mag * jnp.sin(ais)
    den = a_re * a_re + a_im * a_im
    nr = lr - 1.0
    f_re = (nr * a_re + li * a_im) / den
    f_im = (li * a_re - nr * a_im) / den
    seg = _seg_t()
    fr = _dot(f_re, seg, prec=HI)
    fi = _dot(f_im, seg, prec=HI)
    return lr, li, fr * b_re - fi * b_im, fr * b_im + fi * b_re, ars, ais


def _s5_prep(a_re, a_im, lstep, b_re, b_im):
    def body(ar, ai, ls, br, bi, lr_o, li_o, bbr_o, bbi_o, ars_o, ais_o):
        outs = _s5_prep_math(ar[...], ai[...], ls[...], br[...], bi[...])
        for o, v in zip((lr_o, li_o, bbr_o, bbi_o, ars_o, ais_o), outs):
            o[...] = v

    s64 = jax.ShapeDtypeStruct((16, 64), F32)
    s1k = jax.ShapeDtypeStruct((16, 1024), F32)
    return pl.pallas_call(body, name="s5_prep", out_shape=[s64, s64, s1k, s1k, s64, s64])(a_re, a_im, lstep, b_re, b_im)


def _s5_prep_bwd(a_re, a_im, lstep, b_re, b_im, dlr, dli, dbbr, dbbi):
    def body(ar, ai, ls, br, bi, g0, g1, g2, g3, o0, o1, o2, o3, o4):
        f = lambda *a: _s5_prep_math(*a)[:4]
        _, vjp = jax.vjp(f, ar[...], ai[...], ls[...], br[...], bi[...])
        for o, v in zip((o0, o1, o2, o3, o4), vjp((g0[...], g1[...], g2[...], g3[...]))):
            o[...] = v

    s64 = jax.ShapeDtypeStruct((16, 64), F32)
    s1k = jax.ShapeDtypeStruct((16, 1024), F32)
    return pl.pallas_call(body, name="s5_prep_bwd", out_shape=[s64, s64, jax.ShapeDtypeStruct((16, 1), F32), s1k, s1k])(
        a_re, a_im, lstep, b_re, b_im, dlr, dli, dbbr, dbbi)


def _s5_tables(ars, ais, lb):
    def body(ar, ai, pr, pi, qr, qi):
        row = lax.broadcasted_iota(jnp.int32, (lb, S5_P), 0).astype(F32)
        for n, o_r, o_i in ((row + 1.0, pr, pi), (float(lb) - row, qr, qi)):
            mag = jnp.exp(n * ar[...])
            o_r[...] = mag * jnp.cos(n * ai[...])
            o_i[...] = mag * jnp.sin(n * ai[...])

    sds = jax.ShapeDtypeStruct((lb, S5_P), F32)
    return pl.pallas_call(body, name="s5_tables", out_shape=[sds] * 4)(ars, ais)


def _s5_scan(bu_r, bu_i, p_r, p_i, c_r, c_i, lb):
    row = lax.broadcasted_iota(jnp.int32, (lb, S5_P), 0)
    sr, si = bu_r, bu_i
    k = 1
    while k < lb:
        lr, li = p_r[k - 1:k, :], p_i[k - 1:k, :]
        tr = jnp.where(row >= k, _roll(sr, k), 0.0)
        ti = jnp.where(row >= k, _roll(si, k), 0.0)
        sr, si = sr + lr * tr - li * ti, si + lr * ti + li * tr
        k *= 2
    pr, pi = p_r[...], p_i[...]
    return sr + pr * c_r - pi * c_i, si + pr * c_i + pi * c_r


def _s5_y(u, sr, si, cre, cim, dsk):
    return _bdot(sr, cre) + _bdot(si, cim) + dsk * u


def _f_s5(proj, bmat, cre, cim, p_r, p_i, dsk, glu_w, glu_b):
    t = proj.shape[0]
    lb = _s5_block(t)
    nb = t // lb

    def body(u_ref, bm_ref, cr_ref, ci_ref, pr_ref, pi_ref, dk_ref, gw_ref, gb_ref, y_ref, car_ref, st_ref):
        @pl.when(pl.program_id(0) == 0)
        def _():
            st_ref[...] = jnp.zeros_like(st_ref)

        u = u_ref[...]
        bu = _bdot(u, bm_ref[...])
        c_r, c_i = st_ref[0:1, 0:S5_P], st_ref[0:1, S5_P:]
        car_ref[0] = st_ref[0:1, :]
        sr, si = _s5_scan(bu[:, :S5_P], bu[:, S5_P:], pr_ref, pi_ref, c_r, c_i, lb)
        st_ref[0:1, 0:S5_P] = sr[lb - 1:lb]
        st_ref[0:1, S5_P:] = si[lb - 1:lb]
        gel = _gelu(_s5_y(u, sr, si, cr_ref[...], ci_ref[...], dk_ref[...]))
        y_ref[...] = gel * _sig(_bdot(gel, gw_ref[...]) + gb_ref[...])

    return pl.pallas_call(
        body, name="f_s5", grid=(nb,),
        in_specs=[pl.BlockSpec((lb, GW), lambda i: (i, 5)),
                  _full((GW, 2 * S5_P)), _full((S5_P, GW)), _full((S5_P, GW)), _full((lb, S5_P)), _full((lb, S5_P)),
                  _row(GW), _full((GW, GW)), _row(GW)],
        out_specs=[pl.BlockSpec((lb, GW), lambda i: (i, 0)), pl.BlockSpec((1, 1, 2 * S5_P), lambda i: (i, 0, 0))],
        out_shape=[jax.ShapeDtypeStruct((t, GW), F32), jax.ShapeDtypeStruct((nb, 1, 2 * S5_P), F32)],
        scratch_shapes=[pltpu.VMEM((8, 2 * S5_P), F32)], compiler_params=_cparams(1),
    )(proj, bmat, cre, cim, p_r, p_i, dsk, glu_w, glu_b)


def _b_s5(proj, dyd, carries, bmat, cre, cim, p_r, p_i, q_r, q_i, dsk, glu_w, glu_b):
    t = proj.shape[0]
    lb = _s5_block(t)
    nb = t // lb

    def body(u_ref, dy_ref, car_ref, bm_ref, cr_ref, ci_ref, pr_ref, pi_ref, qr_ref, qi_ref, dk_ref, gw_ref, gb_ref,
             du_ref, dbm_ref, dcr_ref, dci_ref, dlam_ref, ddk_ref, dgw_ref, dgb_ref, gc_ref):
        @pl.when(pl.program_id(0) == 0)
        def _():
            gc_ref[...] = jnp.zeros_like(gc_ref)
            for r in (dbm_ref, dcr_ref, dci_ref, dlam_ref, ddk_ref, dgw_ref, dgb_ref):
                r[...] = jnp.zeros_like(r)

        u = u_ref[...]
        bm = bm_ref[...]
        u_b = u.astype(BF16)
        bu = _dot(u_b, bm)
        c_r, c_i = car_ref[0, 0:1, 0:S5_P], car_ref[0, 0:1, S5_P:]
        sr, si = _s5_scan(bu[:, :S5_P], bu[:, S5_P:], pr_ref, pi_ref, c_r, c_i, lb)
        cre_v, cim_v, dk, gw = cr_ref[...], ci_ref[...], dk_ref[...], gw_ref[...]
        sr_b, si_b = sr.astype(BF16), si.astype(BF16)
        y = _dot(sr_b, cre_v) + _dot(si_b, cim_v) + dk * u
        gel = _gelu(y)
        gel_b = gel.astype(BF16)
        gate = _sig(_dot(gel_b, gw) + gb_ref[...])
        dout = dy_ref[...]
        t1 = dout * gel * gate * (1.0 - gate)
        t1_b = t1.astype(BF16)
        dgw_ref[...] += _dot(gel_b, t1_b, TN)
        dgb_ref[...] += _colsum(t1)
        dyv = (dout * gate + _dot(t1_b, gw, NT)) * _dgelu(y)
        dyv_b = dyv.astype(BF16)
        ddk_ref[...] += _colsum(dyv * u)
        dcr_ref[...] += _dot(sr_b, dyv_b, TN)
        dci_ref[...] += _dot(si_b, dyv_b, TN)
        gr = _dot(dyv_b, cre_v, NT)
        gi = _dot(dyv_b, cim_v, NT)
        row = lax.broadcasted_iota(jnp.int32, (lb, S5_P), 0)
        k = 1
        while k < lb:
            lr, li = pr_ref[k - 1:k, :], pi_ref[k - 1:k, :]
            tr = jnp.where(row < lb - k, _roll(gr, lb - k), 0.0)
            ti = jnp.where(row < lb - k, _roll(gi, lb - k), 0.0)
            gr, gi = gr + lr * tr + li * ti, gi + lr * ti - li * tr
            k *= 2
        qr, qi = qr_ref[...], qi_ref[...]
        n_r, n_i = gc_ref[0:1, 0:S5_P], gc_ref[0:1, S5_P:]
        gr, gi = gr + qr * n_r + qi * n_i, gi + qr * n_i - qi * n_r
        gc_ref[0:1, 0:S5_P] = gr[0:1]
        gc_ref[0:1, S5_P:] = gi[0:1]
        gcat = jnp.concatenate([gr, gi], axis=1).astype(BF16)
        dbm_ref[...] += _dot(u_b, gcat, TN)
        du_ref[...] = dyv * dk + _dot(gcat, bm, NT)
        spr = jnp.where(row >= 1, _roll(sr, 1), c_r)
        spi = jnp.where(row >= 1, _roll(si, 1), c_i)
        dlam_ref[0:1, :] += _colsum(gr * spr + gi * spi)
        dlam_ref[1:2, :] += _colsum(gi * spr - gr * spi)

    rev = lambda i: nb - 1 - i
    return pl.pallas_call(
        body, name="b_s5", grid=(nb,),
        in_specs=[pl.BlockSpec((lb, GW), lambda i: (rev(i), 5)), pl.BlockSpec((lb, GW), lambda i: (rev(i), 0)),
                  pl.BlockSpec((1, 1, 2 * S5_P), lambda i: (rev(i), 0, 0)),
                  _full((GW, 2 * S5_P)), _full((S5_P, GW)), _full((S5_P, GW)), _full((lb, S5_P)), _full((lb, S5_P)),
                  _full((lb, S5_P)), _full((lb, S5_P)), _row(GW), _full((GW, GW)), _row(GW)],
        out_specs=[pl.BlockSpec((lb, GW), lambda i: (rev(i), 0)), _full((GW, 2 * S5_P)), _full((S5_P, GW)), _full((S5_P, GW)),
                   _full((2, S5_P)), _row(GW), _full((GW, GW)), _row(GW)],
        out_shape=[jax.ShapeDtypeStruct((t, GW), F32), jax.ShapeDtypeStruct((GW, 2 * S5_P), F32),
                   jax.ShapeDtypeStruct((S5_P, GW), F32), jax.ShapeDtypeStruct((S5_P, GW), F32),
                   jax.ShapeDtypeStruct((2, S5_P), F32), jax.ShapeDtypeStruct((1, GW), F32),
                   jax.ShapeDtypeStruct((GW, GW), F32), jax.ShapeDtypeStruct((1, GW), F32)],
        scratch_shapes=[pltpu.VMEM((8, 2 * S5_P), F32)], compiler_params=_cparams(1),
    )(proj, dyd, carries, bmat, cre, cim, p_r, p_i, q_r, q_i, dsk, glu_w, glu_b)


def _group_norm(ys, bw):
    outs, stats = [], []
    for g, y in enumerate(ys):
        r, n = _rms(y)
        stats.append((r, n))
        outs.append(n * bw[:, GW * g:GW * (g + 1)])
    return jnp.concatenate(outs, axis=1), stats


def _f_out(ya, yb, yc, yd, bw, w_out, h, g1):
    t = h.shape[0]
    tb = _tblock(t)

    def body(a_ref, b_ref, c_ref, d_ref, bw_ref, w_ref, h_ref, g_ref, h2_ref, o_ref, cat_ref):
        cat, _ = _group_norm([a_ref[...], b_ref[...], c_ref[...], d_ref[...]], bw_ref[...])
        catb = cat.astype(BF16)
        cat_ref[...] = catb
        o = _dot(catb, w_ref[...])
        o_ref[...] = o
        h2_ref[...] = h_ref[...] + g_ref[...] * o

    yblk = pl.BlockSpec((tb, GW), lambda i: (i, 0))
    blk = pl.BlockSpec((tb, D), lambda i: (i, 0))
    return pl.pallas_call(
        body, name="f_out", grid=(t // tb,), in_specs=[yblk] * 4 + [_row(D), _full((D, D)), blk, _row(D)],
        out_specs=[blk, blk, blk],
        out_shape=[jax.ShapeDtypeStruct((t, D), F32), jax.ShapeDtypeStruct((t, D), F32), jax.ShapeDtypeStruct((t, D), BF16)],
        compiler_params=_cparams(1),
    )(ya, yb, yc, yd, bw, w_out, h, g1)


def _b_out(dh2, ya, yb, yc, yd, bw, w_out, g1):
    t = dh2.shape[0]
    tb = _tblock(t)

    def body(dh_ref, a_ref, b_ref, c_ref, d_ref, bw_ref, w_ref, g_ref, da_ref, db_ref, dc_ref, dd_ref, do_ref, dbw_ref):
        @pl.when(pl.program_id(0) == 0)
        def _():
            dbw_ref[...] = jnp.zeros_like(dbw_ref)

        do = (dh_ref[...] * g_ref[...]).astype(BF16)
        do_ref[...] = do
        dcat = _dot(do, w_ref[...], NT)
        bw_v = bw_ref[...]
        for g, (y_ref, dy_ref) in enumerate(((a_ref, da_ref), (b_ref, db_ref), (c_ref, dc_ref), (d_ref, dd_ref))):
            r, n = _rms(y_ref[...])
            dc = dcat[:, GW * g:GW * (g + 1)]
            dbw_ref[:, GW * g:GW * (g + 1)] += _colsum(dc * n)
            dy_ref[...] = _rms_bwd(r, n, dc * bw_v[:, GW * g:GW * (g + 1)])

    yblk = pl.BlockSpec((tb, GW), lambda i: (i, 0))
    blk = pl.BlockSpec((tb, D), lambda i: (i, 0))
    ysd = jax.ShapeDtypeStruct((t, GW), F32)
    return pl.pallas_call(
        body, name="b_out", grid=(t // tb,), in_specs=[blk] + [yblk] * 4 + [_row(D), _full((D, D)), _row(D)],
        out_specs=[yblk] * 4 + [blk, _row(D)],
        out_shape=[ysd] * 4 + [jax.ShapeDtypeStruct((t, D), BF16), jax.ShapeDtypeStruct((1, D), F32)],
        compiler_params=_cparams(1),
    )(dh2, ya, yb, yc, yd, bw, w_out, g1)


HB = 1024


def _f_mlp(h2, nw, sc, sh, g2, w1, w2):
    t = h2.shape[0]
    tb = _tblock(t)
    nk = HID // HB

    def body(h_ref, nw_ref, sc_ref, sh_ref, g_ref, w1_ref, w2_ref, h3_ref, m_ref, a_ref, v_ref):
        k = pl.program_id(1)

        @pl.when(k == 0)
        def _():
            _, n = _rms(h_ref[...])
            v_ref[...] = ((n * nw_ref[...]) * (1.0 + sc_ref[...]) + sh_ref[...]).astype(BF16)
            m_ref[...] = jnp.zeros_like(m_ref)

        a = _dot(v_ref[...], w1_ref[...])
        a_ref[...] = a
        ra = jnp.maximum(a, 0.0)
        m_ref[...] += _dot((ra * ra).astype(BF16), w2_ref[...])

        @pl.when(k == nk - 1)
        def _():
            h3_ref[...] = h_ref[...] + g_ref[...] * m_ref[...]

    blk = pl.BlockSpec((tb, D), lambda i, k: (i, 0))
    return pl.pallas_call(
        body, name="f_mlp", grid=(t // tb, nk),
        in_specs=[blk, _row(D), _row(D), _row(D), _row(D), pl.BlockSpec((D, HB), lambda i, k: (0, k)),
                  pl.BlockSpec((HB, D), lambda i, k: (k, 0))],
        out_specs=[blk, blk, pl.BlockSpec((tb, HB), lambda i, k: (i, k)), blk],
        out_shape=[jax.ShapeDtypeStruct((t, D), F32), jax.ShapeDtypeStruct((t, D), F32), jax.ShapeDtypeStruct((t, HID), F32),
                   jax.ShapeDtypeStruct((t, D), BF16)],
        compiler_params=_cparams(2),
    )(h2, nw, sc, sh, g2, w1, w2)


def _b_mlp(dh3, a, g2, w1, w2):
    t = dh3.shape[0]
    tb = _tblock(t)
    nk = HID // HB

    def body(dh_ref, a_ref, g_ref, w1_ref, w2_ref, dv_ref, da_ref, act_ref, dm_ref):
        k = pl.program_id(1)
        dm = (dh_ref[...] * g_ref[...]).astype(BF16)

        @pl.when(k == 0)
        def _():
            dm_ref[...] = dm
            dv_ref[...] = jnp.zeros_like(dv_ref)

        ra = jnp.maximum(a_ref[...], 0.0)
        act_ref[...] = (ra * ra).astype(BF16)
        da = (_dot(dm, w2_ref[...], NT) * (2.0 * ra)).astype(BF16)
        da_ref[...] = da
        dv_ref[...] += _dot(da, w1_ref[...], NT)

    blk = pl.BlockSpec((tb, D), lambda i, k: (i, 0))
    hblk = pl.BlockSpec((tb, HB), lambda i, k: (i, k))
    return pl.pallas_call(
        body, name="b_mlp", grid=(t // tb, nk),
        in_specs=[blk, hblk, _row(D), pl.BlockSpec((D, HB), lambda i, k: (0, k)), pl.BlockSpec((HB, D), lambda i, k: (k, 0))],
        out_specs=[blk, hblk, hblk, blk],
        out_shape=[jax.ShapeDtypeStruct((t, D), F32), jax.ShapeDtypeStruct((t, HID), BF16), jax.ShapeDtypeStruct((t, HID), BF16),
                   jax.ShapeDtypeStruct((t, D), BF16)],
        compiler_params=_cparams(2),
    )(dh3, a, g2, w1, w2)


def _b_final(h, tgt, fw):
    t = h.shape[0]
    tb = _tblock(t)

    def body(h_ref, t_ref, w_ref, dh_ref, loss_ref, dfw_ref):
        @pl.when(pl.program_id(0) == 0)
        def _():
            loss_ref[...] = jnp.zeros_like(loss_ref)
            dfw_ref[...] = jnp.zeros_like(dfw_ref)

        r, n = _rms(h_ref[...])
        wv = w_ref[...]
        err = n * wv - t_ref[...]
        loss_ref[...] += jnp.sum(err * err, keepdims=True) * (0.5 / D)
        dy = err * (1.0 / D)
        dfw_ref[...] += _colsum(dy * n)
        dh_ref[...] = _rms_bwd(r, n, dy * wv)

    blk = pl.BlockSpec((tb, D), lambda i: (i, 0))
    return pl.pallas_call(
        body, name="b_final", grid=(t // tb,), in_specs=[blk, blk, _row(D)], out_specs=[blk, _row(1), _row(D)],
        out_shape=[jax.ShapeDtypeStruct((t, D), F32), jax.ShapeDtypeStruct((1, 1), F32), jax.ShapeDtypeStruct((1, D), F32)],
        compiler_params=_cparams(1),
    )(h, tgt, fw)


_EYE16 = None


def _eye(n):
    return jnp.eye(n, dtype=F32)


def _pool_embed(pool_w):
    return jnp.einsum('gcd,gk->gckd', pool_w, _eye(4)).reshape(GW, GW)


def _pool_extract(m):
    return jnp.einsum('gcgd->gcd', m.reshape(4, 64, 4, 64))


def _bmat_embed(bb):
    return jnp.einsum('gph,gk->ghkp', bb, _eye(16)).reshape(GW, S5_P)


def _bmat_extract(m):
    return jnp.einsum('ghgp->gph', m.reshape(16, 16, 16, 64))


def _cmat_embed(cc):
    return jnp.einsum('ghp,gk->kpgh', cc, _eye(16)).reshape(S5_P, GW)


def _cmat_extract(m):
    return jnp.einsum('gpgh->ghp', m.reshape(16, 64, 16, 16))


def _pad_lanes(v, n=DTW):
    return jnp.pad(v.reshape(1, -1), ((0, 0), (0, n - v.shape[-1])))


def _layer_params(p, l, mod):
    q = {}
    q['mod'] = [mod[k:k + 1] for k in range(6)]
    q['nw1'] = p['norm_mix_w'][l:l + 1]
    q['nw2'] = p['norm_mlp_w'][l:l + 1]
    w_in = p['w_in'][l]
    q['w_main'] = jnp.concatenate([w_in[:, :1280], w_in[:, 2052:2308], w_in[:, 1280:2048]], axis=1)
    q['w_dt'] = jnp.pad(w_in[:, 2048:2052], ((0, 0), (0, DTW - 4)))
    q['pool_mat'] = _pool_embed(p['pool_w'][l]).astype(BF16)
    q['pool_scale'] = p['pool_scale'][l:l + 1]
    q['sconv_w'] = p['sconv_w'][l]
    q['conv_w'] = p['ssd_conv_w'][l]
    q['conv_b'] = p['ssd_conv_b'][l:l + 1]
    q['dt_bias'] = _pad_lanes(p['ssd_dt_bias'][l])
    q['a_log'] = _pad_lanes(p['ssd_a_log'][l])
    q['ssd_d'] = _pad_lanes(p['ssd_d'][l])
    q['s5_raw'] = (p['s5_a_re'][l], p['s5_a_im'][l], p['s5_log_step'][l].reshape(16, 1),
                   p['s5_b_re'][l].reshape(16, 1024), p['s5_b_im'][l].reshape(16, 1024))
    q['cre'] = _cmat_embed(p['s5_c_re'][l]).astype(BF16)
    q['cim'] = (-_cmat_embed(p['s5_c_im'][l])).astype(BF16)
    q['s5_d'] = p['s5_d'][l:l + 1]
    q['glu_w'] = p['s5_glu_w'][l].astype(BF16)
    q['glu_b'] = p['s5_glu_b'][l:l + 1]
    q['bw'] = p['branch_norm_w'][l:l + 1]
    q['w_out'] = p['w_out'][l]
    q['w1'] = p['mlp_w1'][l]
    q['w2'] = p['mlp_w2'][l]
    return q


def _layer_fwd(h, q):
    sh1, sc1, g1, sh2, sc2, g2 = q['mod']
    t = h.shape[0]
    s = {'h': h}
    s['proj'], s['dtp'], s['u'] = _f_in(h, q['nw1'], sc1, sh1, q['w_main'], q['w_dt'])
    s['ya'], s['yb'] = _f_ab(s['proj'], q['pool_mat'], q['pool_scale'], q['sconv_w'])
    s['yc'], s['ypre'], s['sprev'] = _f_ssd(s['proj'], s['dtp'], q['conv_w'], q['conv_b'], q['dt_bias'], q['a_log'], q['ssd_d'])
    lr, li, bbr, bbi, ars, ais = _s5_prep(*q['s5_raw'])
    s['bmat'] = jnp.concatenate([_bmat_embed(bbr.reshape(16, 64, 16)), _bmat_embed(bbi.reshape(16, 64, 16))],
                                axis=1).astype(BF16)
    s['tables'] = _s5_tables(ars.reshape(1, S5_P), ais.reshape(1, S5_P), _s5_block(t))
    s['yd'], s['carries'] = _f_s5(s['proj'], s['bmat'], q['cre'], q['cim'], s['tables'][0], s['tables'][1],
                                  q['s5_d'], q['glu_w'], q['glu_b'])
    s['h2'], s['o'], s['cat'] = _f_out(s['ya'], s['yb'], s['yc'], s['yd'], q['bw'], q['w_out'], h, g1)
    h3, s['m'], s['a'], s['v'] = _f_mlp(s['h2'], q['nw2'], sc2, sh2, g2, q['w1'], q['w2'])
    return h3, s


def _layer_bwd(dh3, q, s):
    sh1, sc1, g1, sh2, sc2, g2 = q['mod']
    g = {}
    dv, da, act, dm = _b_mlp(dh3, s['a'], g2, q['w1'], q['w2'])
    g['mlp_w1'] = _tn_matmul(s['v'], da, "dw1", col_major=True)
    g['mlp_w2'] = _tn_matmul(act, dm, "dw2")
    dh2, dsc2, dsh2, dnw2, dg2 = _b_normmod(dv, s['h2'], dh3, s['m'], q['nw2'], sc2, "b_norm_mlp")
    dya, dyb, dyc, dyd, do, dbw = _b_out(dh2, s['ya'], s['yb'], s['yc'], s['yd'], q['bw'], q['w_out'], g1)
    g['w_out'] = _tn_matmul(s['cat'], do, "dwout")
    g['branch_norm_w'] = dbw[0]
    dab, dpm, dps, dsw = _b_ab(s['proj'], dya, dyb, q['pool_mat'], q['pool_scale'], q['sconv_w'])
    g['pool_w'] = _pool_extract(dpm)
    g['pool_scale'] = dps[0]
    g['sconv_w'] = dsw
    dz, dxbc, ddt, dcw, dcb, ddtb, dal, ddk = _b_ssd(s['proj'], s['dtp'], s['ypre'], dyc, s['sprev'], q['conv_w'],
                                                     q['conv_b'], q['dt_bias'], q['a_log'], q['ssd_d'])
    g['ssd_conv_w'] = dcw
    g['ssd_conv_b'] = dcb[0]
    g['ssd_dt_bias'] = ddtb[0, :4]
    g['ssd_a_log'] = dal[0, :4]
    g['ssd_d'] = ddk[0, :4]
    tb = s['tables']
    ds5, dbmat, dcre, dcim, dlam, dd5, dgw, dgb = _b_s5(s['proj'], dyd, s['carries'], s['bmat'], q['cre'], q['cim'],
                                                        tb[0], tb[1], tb[2], tb[3], q['s5_d'], q['glu_w'], q['glu_b'])
    g['s5_c_re'] = _cmat_extract(dcre)
    g['s5_c_im'] = -_cmat_extract(dcim)
    g['s5_d'] = dd5[0]
    g['s5_glu_w'] = dgw
    g['s5_glu_b'] = dgb[0]
    dbbr = _bmat_extract(dbmat[:, :S5_P]).reshape(16, 1024)
    dbbi = _bmat_extract(dbmat[:, S5_P:]).reshape(16, 1024)
    dar, dai, dls, dbr, dbi = _s5_prep_bwd(*q['s5_raw'], dlam[0].reshape(16, 64), dlam[1].reshape(16, 64), dbbr, dbbi)
    g['s5_a_re'], g['s5_a_im'], g['s5_log_step'] = dar, dai, dls[:, 0]
    g['s5_b_re'], g['s5_b_im'] = dbr.reshape(16, 64, 16), dbi.reshape(16, 64, 16)
    du = _b_in_du(dab, dz, dxbc, ds5, ddt, q['w_main'], q['w_dt'])
    u = s['u']
    pieces = [_tn_matmul(u, dab, "dwin_ab"), _tn_matmul(u, dz, "dwin_z"), _tn_matmul(u, dxbc, "dwin_xbc"),
              _tn_matmul(u, ddt, "dwin_dt")[:, :4], _tn_matmul(u, ds5, "dwin_s5")]
    g['w_in'] = jnp.concatenate(pieces, axis=1)
    dh, dsc1, dsh1, dnw1, dg1 = _b_normmod(du, s['h'], dh2, s['o'], q['nw1'], sc1, "b_norm_mix")
    g['norm_mix_w'] = dnw1[0]
    g['norm_mlp_w'] = dnw2[0]
    dmod = jnp.concatenate([dsh1, dsc1, dg1, dsh2, dsc2, dg2], axis=1)
    return dh, g, dmod


def _local_step(x, tgt, p, mod):
    qs = [_layer_params(p, l, mod[l]) for l in range(2)]
    h = x
    saved = []
    for l in range(2):
        h, s = _layer_fwd(h, qs[l])
        saved.append(s)
    dh, loss, dfw = _b_final(h, tgt, p['final_norm_w'].reshape(1, D))
    grads = [None, None]
    dmods = [None, None]
    for l in (1, 0):
        dh, grads[l], dmods[l] = _layer_bwd(dh, qs[l], saved[l])
    out = {k: jnp.stack([grads[0][k], grads[1][k]]) for k in grads[0]}
    out['final_norm_w'] = dfw[0]
    return loss, dh, out, jnp.concatenate(dmods, axis=0)


def _pack(arrs):
    parts, rows = [], 0
    for a in arrs:
        f = a.reshape(-1).astype(F32)
        pad = (-f.shape[0]) % 1024
        f = jnp.pad(f, (0, pad)) if pad else f
        parts.append(f.reshape(-1, 128))
        rows += parts[-1].shape[0]
    if rows % 256:
        parts.append(jnp.zeros((256 - rows % 256, 128), F32))
    return jnp.concatenate(parts, axis=0)


def _unpack(buf, shapes):
    out, row = [], 0
    for shp in shapes:
        n = int(math.prod(shp)) if len(shp) else 1
        rows = (n + 1023) // 1024 * 8
        out.append(buf[row:row + rows].reshape(-1)[:n].reshape(shp))
        row += rows
    return out


def _shard_of(a, axis, k):
    n = a.shape[axis] // 4
    return lax.dynamic_slice_in_dim(a, k * n, n, axis)


def kernel(x, c, norm_mix_w, norm_mlp_w, ada_w, ada_b, w_in, pool_w, pool_scale, sconv_w, ssd_conv_w, ssd_conv_b, ssd_dt_bias, ssd_a_log, ssd_d, s5_a_re, s5_a_im, s5_log_step, s5_b_re, s5_b_im, s5_c_re, s5_c_im, s5_d, s5_glu_w, s5_glu_b, branch_norm_w, w_out, mlp_w1, mlp_w2, final_norm_w, loss_target, m_norm_mix_w, m_norm_mlp_w, m_ada_w, m_ada_b, m_w_in, m_pool_w, m_pool_scale, m_sconv_w, m_ssd_conv_w, m_ssd_conv_b, m_ssd_dt_bias, m_ssd_a_log, m_ssd_d, m_s5_a_re, m_s5_a_im, m_s5_log_step, m_s5_b_re, m_s5_b_im, m_s5_c_re, m_s5_c_im, m_s5_d, m_s5_glu_w, m_s5_glu_b, m_branch_norm_w, m_w_out, m_mlp_w1, m_mlp_w2, m_final_norm_w, v_norm_mix_w, v_norm_mlp_w, v_ada_w, v_ada_b, v_w_in, v_pool_w, v_pool_scale, v_sconv_w, v_ssd_conv_w, v_ssd_conv_b, v_ssd_dt_bias, v_ssd_a_log, v_ssd_d, v_s5_a_re, v_s5_a_im, v_s5_log_step, v_s5_b_re, v_s5_b_im, v_s5_c_re, v_s5_c_im, v_s5_d, v_s5_glu_w, v_s5_glu_b, v_branch_norm_w, v_w_out, v_mlp_w1, v_mlp_w2, v_final_norm_w):
    loc = locals()
    w = {n: loc[n] for n in WEIGHTS}
    mom = {n: loc['m_' + n] for n in WEIGHTS}
    var = {n: loc['v_' + n] for n in WEIGHTS}
    ix, iy, ic = lax.axis_index("x"), lax.axis_index("y"), lax.axis_index("c")
    chip = 2 * ix + iy
    dev = 4 * ix + 2 * iy + ic

    (c_all,) = _exchange([c], EVERYONE, False, "ag_cond", stage=True)
    c_all = c_all.reshape(8, D)
    small_sh = _exchange([w[n] for n in SMALL_SHARDED], CHIPS, False, "ag_small")
    mine_of = lambda a: lax.dynamic_index_in_dim(a.astype(BF16), ic, axis=0, keepdims=False)
    pad_in = lambda a: jnp.pad(a.reshape(577, D), ((0, WIN_ROWS - 577), (0, 0)))
    big_l = _exchange([pad_in(mine_of(w['w_in'])), mine_of(w['w_out']), mine_of(w['mlp_w1']), mine_of(w['mlp_w2'])],
                      CHIPS, False, "ag_big")
    big_o = _pair_swap([a.reshape(-1, D) for a in big_l], False, "swap_big")
    big_sh = [jnp.stack([jnp.where(ic == l, a, b.reshape(a.shape)) for l in range(2)]) for a, b in zip(big_l, big_o)]
    p = dict(w)
    for n, g in zip(SMALL_SHARDED, small_sh):
        ax = SMALL_SHARDED[n]
        p[n] = jnp.concatenate([g[k] for k in range(4)], axis=ax)
    w_in_sh = big_sh[0][:, :, :577].reshape(2, 4, D, 577)
    p['w_in'] = jnp.concatenate([w_in_sh[:, k] for k in range(4)], axis=2)
    p['w_out'] = big_sh[1].reshape(2, D, D)
    p['mlp_w1'] = jnp.concatenate([big_sh[2][:, k] for k in range(4)], axis=2)
    p['mlp_w2'] = big_sh[3].reshape(2, HID, D)

    ada_b_sh = _shard_of(w['ada_b'], 1, chip).reshape(2, 1, 6 * D // 4)
    mod_sh = _ada_fwd(c_all, w['ada_w'], ada_b_sh)
    (mod_all,) = _exchange([mod_sh], CHIPS, False, "ag_mod", stage=True)
    mine = lax.dynamic_index_in_dim(mod_all, dev, axis=2, keepdims=False)
    mod = jnp.transpose(mine, (1, 0, 2)).reshape(2, 6, D)

    loss, grad_x, g, dmod = _local_step(x[0], loss_target[0], p, mod)

    (dmod_all,) = _exchange([dmod], EVERYONE, False, "ag_dmod", stage=True)
    dmod_all = jnp.transpose(dmod_all, (1, 0, 2))
    g_ada_w, g_ada_b = _ada_bwd(c_all, _shard_of(dmod_all, 2, chip), dmod_all)

    gw_in = jnp.transpose(g['w_in'].reshape(2, D, 4, 577), (0, 2, 1, 3)).reshape(2, 4, 577, D)
    gw_in = jnp.pad(gw_in, ((0, 0), (0, 0), (0, WIN_ROWS - 577), (0, 0)))
    gw_out = g['w_out'].reshape(2, 4, 256, D)
    gw1 = g['mlp_w1']
    gw2 = g['mlp_w2'].reshape(2, 4, 1024, D)
    gws = [gw_in, gw_out, gw1, gw2]
    got = _pair_swap([a.reshape(2, -1, D) for a in gws], True, "swap_grad")
    layer = ic.astype(jnp.int32).reshape(1)
    pair = [_pair_sum(a, b.reshape(a.shape[1:]), layer, "pair_sum%d" % k, BF16) for k, (a, b) in enumerate(zip(gws, got))]
    quad = _exchange(pair, CHIPS, True, "rs_chips")
    quad = [_sum_lead(a, "rs_chip_sum%d" % k, F32) for k, a in enumerate(quad)]
    other = _pair_swap(quad, False, "swap_red")
    both = [jnp.stack([jnp.where(ic == l, a, b) for l in range(2)]) for a, b in zip(quad, other)]
    both[0] = both[0][:, :577].reshape(2, D, 577)
    red = dict(zip(('w_in', 'w_out', 'mlp_w1', 'mlp_w2'), both))
    red['ada_w'] = g_ada_w

    small_names = [n for n in WEIGHTS if n not in BIG and n != 'ada_b']
    small_shapes = [g[n].shape for n in small_names] + [(1, 1)]
    packed = _pack([g[n] for n in small_names] + [loss])
    (packed_all,) = _exchange([packed], EVERYONE, False, "ag_smallgrad", stage=True)
    summed = _unpack(_sum_lead(packed_all, "smallgrad_sum", F32), small_shapes)
    for n, a in zip(small_names, summed[:-1]):
        red[n] = _shard_of(a, SMALL_SHARDED[n], chip) if n in SMALL_SHARDED else a
    red['ada_b'] = g_ada_b
    loss_out = summed[-1].reshape(())

    delta, new_m, new_v = {}, {}, {}
    for n in BIG:
        delta[n], new_m[n], new_v[n] = _adamw(w[n], red[n], mom[n], var[n], "adamw_" + n)
    rest = [n for n in WEIGHTS if n not in BIG]
    shapes = [w[n].shape for n in rest]
    d_p, m_p, v_p = _adamw(_pack([w[n] for n in rest]), _pack([red[n] for n in rest]), _pack([mom[n] for n in rest]),
                           _pack([var[n] for n in rest]), "adamw_small")
    for n, a, b, cc in zip(rest, _unpack(d_p, shapes), _unpack(m_p, shapes), _unpack(v_p, shapes)):
        delta[n], new_m[n], new_v[n] = a, b, cc

    return (loss_out, grad_x[None], *[red[n] for n in WEIGHTS], *[delta[n] for n in WEIGHTS],
            *[new_m[n] for n in WEIGHTS], *[new_v[n] for n in WEIGHTS])
```

```python
import functools
import math

import jax
import jax.numpy as jnp
from jax import lax
from jax.experimental import pallas as pl
from jax.experimental.pallas import tpu as pltpu

F32 = jnp.float32
BF16 = jnp.bfloat16
HI = lax.Precision.HIGHEST

D = 1024
GW = 256
HID = 4096
EPS = 1e-6
PW = 2304
DTW = 128
SSD_L = 128
SSD_SUB = 2
NH, HP, NS = 4, 64, 128
S5_P = 1024
MESH = pl.DeviceIdType.MESH

ADAM_LR, ADAM_B1, ADAM_B2, ADAM_EPS, ADAM_WD, ADAM_STEP = 0.001, 0.9, 0.999, 1e-08, 0.01, 10

NT = (((1,), (1,)), ((), ()))
TN = (((0,), (0,)), ((), ()))

WEIGHTS = ['norm_mix_w', 'norm_mlp_w', 'ada_w', 'ada_b', 'w_in', 'pool_w', 'pool_scale', 'sconv_w', 'ssd_conv_w',
           'ssd_conv_b', 'ssd_dt_bias', 'ssd_a_log', 'ssd_d', 's5_a_re', 's5_a_im', 's5_log_step', 's5_b_re', 's5_b_im',
           's5_c_re', 's5_c_im', 's5_d', 's5_glu_w', 's5_glu_b', 'branch_norm_w', 'w_out', 'mlp_w1', 'mlp_w2',
           'final_norm_w']
BIG = ('ada_w', 'w_in', 'w_out', 'mlp_w1', 'mlp_w2')
SMALL_SHARDED = {'sconv_w': 2, 'ssd_conv_w': 2, 's5_glu_w': 1}


def _cparams(n_axes, vmem_mb=48):
    return pltpu.CompilerParams(dimension_semantics=("arbitrary",) * n_axes, vmem_limit_bytes=vmem_mb * 1024 * 1024)


def _row(n):
    return pl.BlockSpec((1, n), lambda *_: (0, 0))


def _full(shape):
    nd = len(shape)
    return pl.BlockSpec(tuple(shape), lambda *_: (0,) * nd)


def _dot(a, b, dims=None, prec=None):
    if dims is None:
        dims = (((a.ndim - 1,), (0,)), ((), ()))
    return lax.dot_general(a, b, dims, preferred_element_type=F32, precision=prec)


def _bdot(a, b, dims=None):
    return _dot(a.astype(BF16), b.astype(BF16), dims)


def _sig(x):
    return jax.nn.sigmoid(x)


def _silu(x):
    return x * _sig(x)


def _dsilu(x):
    s = _sig(x)
    return s * (1.0 + x * (1.0 - s))


def _softplus(x):
    return jnp.maximum(x, 0.0) + jnp.log(1.0 + jnp.exp(-jnp.abs(x)))


_GK = math.sqrt(2.0 / math.pi)


def _gelu(x):
    return 0.5 * x * (1.0 + jnp.tanh(_GK * (x + 0.044715 * x * x * x)))


def _dgelu(x):
    th = jnp.tanh(_GK * (x + 0.044715 * x * x * x))
    return 0.5 * (1.0 + th) + 0.5 * x * (1.0 - th * th) * _GK * (1.0 + 3.0 * 0.044715 * x * x)


def _colsum(x):
    return jnp.sum(x, axis=0, keepdims=True)


def _rms(x):
    r = lax.rsqrt(jnp.mean(x * x, axis=-1, keepdims=True) + EPS)
    return r, x * r


def _rms_bwd(r, n, dn):
    return r * (dn - n * jnp.mean(dn * n, axis=-1, keepdims=True))


def _roll(x, k):
    n = x.shape[0]
    k = k % n
    return x if k == 0 else pltpu.roll(x, k, axis=0)


def _tblock(t, want=512):
    return min(t, want)


def _peer(mask):
    x, y, c = lax.axis_index("x"), lax.axis_index("y"), lax.axis_index("c")
    return (x ^ ((mask >> 2) & 1), y ^ ((mask >> 1) & 1), c ^ (mask & 1))


def _group_index(masks):
    x, y, c = lax.axis_index("x"), lax.axis_index("y"), lax.axis_index("c")
    full = 0
    for m in masks:
        full |= m
    bits = [b for b in (4, 2, 1) if full & b]

    def idx(px, py, pc):
        v = {4: px, 2: py, 1: pc}
        out = 0
        for b in bits:
            out = out * 2 + v[b]
        return out

    return idx(x, y, c), [idx(*_peer(m)) for m in masks]


def _exchange(arrs, masks, scatter, name, stage=False):
    n_arr, n_peer, n_grp = len(arrs), len(masks), len(masks) + 1

    def body(*refs):
        ins, outs = refs[:n_arr], refs[n_arr:2 * n_arr]
        send_sems, recv_sems, local_sems = refs[2 * n_arr:]
        me, peer_idx = _group_index(masks)
        copies = []
        for t in range(n_arr):
            src_me = ins[t].at[me] if scatter else ins[t]
            loc = pltpu.make_async_copy(src_me, outs[t].at[me], local_sems.at[t])
            loc.start()
            copies.append(loc)
            for j, m in enumerate(masks):
                src = ins[t].at[peer_idx[j]] if scatter else ins[t]
                cp = pltpu.make_async_remote_copy(src_ref=src, dst_ref=outs[t].at[me], send_sem=send_sems.at[t, j],
                                                  recv_sem=recv_sems.at[t, j], device_id=_peer(m), device_id_type=MESH)
                cp.start()
                copies.append(cp)
        for cp in copies:
            cp.wait()

    hbm = pl.BlockSpec(memory_space=pl.ANY)
    out_shape = [jax.ShapeDtypeStruct((n_grp,) + (a.shape[1:] if scatter else a.shape), a.dtype) for a in arrs]
    src_spec = pl.BlockSpec(memory_space=pltpu.VMEM) if stage else hbm
    outs = pl.pallas_call(
        body, name=name, in_specs=[src_spec] * n_arr, out_specs=[hbm] * n_arr, out_shape=out_shape,
        scratch_shapes=[pltpu.SemaphoreType.DMA((n_arr, n_peer)), pltpu.SemaphoreType.DMA((n_arr, n_peer)),
                        pltpu.SemaphoreType.DMA((n_arr,))],
    )(*arrs)
    return list(outs)


CHIPS = (4, 2, 6)
EVERYONE = (1, 2, 3, 4, 5, 6, 7)
SIBLING = (1,)
SWAP_ROWS = 512
WIN_ROWS = 592


def _pair_swap(arrs, other_layer, name):
    n_arr = len(arrs)
    shapes = [a.shape[-2:] for a in arrs]
    chunks = []
    for t, (rows, _) in enumerate(shapes):
        assert rows % 16 == 0
        for j, r0 in enumerate(range(0, rows, SWAP_ROWS)):
            chunks.append((t, r0, min(SWAP_ROWS, rows - r0), j % 2))

    def body(*refs):
        ins, outs = refs[:n_arr], refs[n_arr:2 * n_arr]
        bufs = refs[2 * n_arr:3 * n_arr]
        load_sems, send_sems, recv_sems = refs[3 * n_arr:]
        sibling = _peer(1)
        c = lax.axis_index("c")

        def load(k):
            t, r0, n, slot = chunks[k]
            src = ins[t].at[1 - c] if other_layer else ins[t]
            return pltpu.make_async_copy(src.at[pl.ds(r0, n)], bufs[t].at[slot, pl.ds(0, n)], load_sems.at[t, slot])

        def send(k):
            t, r0, n, slot = chunks[k]
            return pltpu.make_async_remote_copy(src_ref=bufs[t].at[slot, pl.ds(0, n)], dst_ref=outs[t].at[pl.ds(r0, n)],
                                                send_sem=send_sems.at[t, slot], recv_sem=recv_sems.at[t],
                                                device_id=sibling, device_id_type=MESH)

        in_flight = {}

        def start_load(k):
            key = (chunks[k][0], chunks[k][3])
            if key in in_flight:
                send(in_flight.pop(key)).wait_send()
            load(k).start()

        start_load(0)
        for k in range(len(chunks)):
            load(k).wait()
            if k + 1 < len(chunks):
                start_load(k + 1)
            send(k).start()
            in_flight[(chunks[k][0], chunks[k][3])] = k
        for k in in_flight.values():
            send(k).wait_send()
        for t in range(n_arr):
            pltpu.make_async_remote_copy(src_ref=outs[t], dst_ref=outs[t], send_sem=send_sems.at[t, 0],
                                         recv_sem=recv_sems.at[t], device_id=sibling, device_id_type=MESH).wait_recv()

    hbm = pl.BlockSpec(memory_space=pl.ANY)
    outs = pl.pallas_call(
        body, name=name, in_specs=[hbm] * n_arr, out_specs=[hbm] * n_arr,
        out_shape=[jax.ShapeDtypeStruct(s, a.dtype) for s, a in zip(shapes, arrs)],
        scratch_shapes=[pltpu.VMEM((2, min(SWAP_ROWS, s[0]), s[1]), a.dtype) for s, a in zip(shapes, arrs)]
        + [pltpu.SemaphoreType.DMA((n_arr, 2)), pltpu.SemaphoreType.DMA((n_arr, 2)), pltpu.SemaphoreType.DMA((n_arr,))],
        compiler_params=pltpu.CompilerParams(vmem_limit_bytes=48 * 1024 * 1024),
    )(*arrs)
    return list(outs)


def _sum_lead(a, name, out_dtype):
    n = a.shape[0]
    shape = a.shape[1:]

    def body(a_ref, o_ref):
        acc = a_ref[0].astype(F32)
        for k in range(1, n):
            acc = acc + a_ref[k].astype(F32)
        o_ref[...] = acc.astype(out_dtype)

    if len(shape) == 3:
        blk = (1,) + shape[1:]
        return pl.pallas_call(
            body, name=name, grid=(shape[0],), in_specs=[pl.BlockSpec((n,) + blk, lambda i: (0, i, 0, 0))],
            out_specs=pl.BlockSpec(blk, lambda i: (i, 0, 0)), out_shape=jax.ShapeDtypeStruct(shape, out_dtype),
            compiler_params=_cparams(1),
        )(a)
    rows, cols = shape
    rb = rows
    for cand in (512, 256, 128):
        if rows % cand == 0 and rows > cand:
            rb = cand
            break
    return pl.pallas_call(
        body, name=name, grid=(rows // rb,), in_specs=[pl.BlockSpec((n, rb, cols), lambda i: (0, i, 0))],
        out_specs=pl.BlockSpec((rb, cols), lambda i: (i, 0)), out_shape=jax.ShapeDtypeStruct((rows, cols), out_dtype),
        compiler_params=_cparams(1),
    )(a)


def _pair_sum(g, recv, layer, name, out_dtype):
    _, n, r, c = g.shape

    def body(l_ref, g_ref, r_ref, o_ref):
        o_ref[...] = (g_ref[0].astype(F32) + r_ref[...].astype(F32)).astype(out_dtype)

    return pl.pallas_call(
        body, name=name,
        grid_spec=pltpu.PrefetchScalarGridSpec(
            num_scalar_prefetch=1, grid=(n,),
            in_specs=[pl.BlockSpec((1, 1, r, c), lambda i, l: (l[0], i, 0, 0)), pl.BlockSpec((1, r, c), lambda i, l: (i, 0, 0))],
            out_specs=pl.BlockSpec((1, r, c), lambda i, l: (i, 0, 0))),
        out_shape=jax.ShapeDtypeStruct((n, r, c), out_dtype), compiler_params=_cparams(1),
    )(layer, g, recv)


def _tn_matmul(a, b, name, col_major=False):
    t, k = a.shape
    n = b.shape[1]
    tb = _tblock(t, 1024)
    kb = min(k, 1024)
    nb = min(n, 1024)
    grid = (k // kb, n // nb, t // tb)

    def body(a_ref, b_ref, o_ref):
        @pl.when(pl.program_id(2) == 0)
        def _():
            o_ref[...] = jnp.zeros_like(o_ref)

        acc = _bdot(a_ref[...], b_ref[...], TN)
        if col_major:
            o_ref[0] += acc
        else:
            o_ref[...] += acc

    if col_major:
        out_spec = pl.BlockSpec((1, kb, nb), lambda ki, ni, ti: (ni, ki, 0))
        out_shape = jax.ShapeDtypeStruct((n // nb, k, nb), F32)
    else:
        out_spec = pl.BlockSpec((kb, nb), lambda ki, ni, ti: (ki, ni))
        out_shape = jax.ShapeDtypeStruct((k, n), F32)
    return pl.pallas_call(
        body, name=name, grid=grid,
        in_specs=[pl.BlockSpec((tb, kb), lambda ki, ni, ti: (ti, ki)), pl.BlockSpec((tb, nb), lambda ki, ni, ti: (ti, ni))],
        out_specs=out_spec, out_shape=out_shape, compiler_params=_cparams(3),
    )(a, b)


def _adamw(w, g, m, v, name):
    shape = w.shape
    cols = shape[-1]
    rows = int(math.prod(shape[:-1]))
    rb = rows
    for cand in (256, 128, 64, 32, 16, 8):
        if rows % cand == 0 and rows > cand:
            rb = cand
            break
    bc1 = 1.0 - ADAM_B1 ** ADAM_STEP
    bc2 = 1.0 - ADAM_B2 ** ADAM_STEP

    def body(w_ref, g_ref, m_ref, v_ref, d_ref, nm_ref, nv_ref):
        gg = g_ref[...]
        m2 = ADAM_B1 * m_ref[...] + (1.0 - ADAM_B1) * gg
        v2 = ADAM_B2 * v_ref[...] + (1.0 - ADAM_B2) * (gg * gg)
        m_hat = m2 / bc1
        v_hat = v2 / bc2
        d_ref[...] = -ADAM_LR * (m_hat / (jnp.sqrt(v_hat) + ADAM_EPS) + ADAM_WD * w_ref[...])
        nm_ref[...] = m2
        nv_ref[...] = v2

    spec = pl.BlockSpec((rb, cols), lambda i: (i, 0))
    sds = jax.ShapeDtypeStruct((rows, cols), F32)
    outs = pl.pallas_call(
        body, name=name, grid=(rows // rb,), in_specs=[spec] * 4, out_specs=[spec] * 3, out_shape=[sds] * 3,
        compiler_params=_cparams(1),
    )(*(z.reshape(rows, cols) for z in (w, g, m, v)))
    return tuple(o.reshape(shape) for o in outs)


def _ada_fwd(c_all, ada_w_sh, ada_b_sh):
    s = ada_w_sh.shape[2]
    sb = 512

    def body(c_ref, w_ref, b_ref, o_ref):
        cond = _silu(c_ref[...])
        o_ref[0] = _bdot(cond, w_ref[0]) + b_ref[0]

    return pl.pallas_call(
        body, name="ada_fwd", grid=(2, s // sb),
        in_specs=[_full((8, D)), pl.BlockSpec((1, D, sb), lambda l, j: (l, 0, j)), pl.BlockSpec((1, 1, sb), lambda l, j: (l, 0, j))],
        out_specs=pl.BlockSpec((1, 8, sb), lambda l, j: (l, 0, j)), out_shape=jax.ShapeDtypeStruct((2, 8, s), F32),
        compiler_params=_cparams(2),
    )(c_all, ada_w_sh, ada_b_sh)


def _ada_bwd(c_all, dmod_sh, dmod_all):
    s = dmod_sh.shape[2]
    sb = 512

    def body(c_ref, d_ref, o_ref):
        cond = _silu(c_ref[...])
        o_ref[0] = _bdot(cond, d_ref[0], TN)

    gw = pl.pallas_call(
        body, name="ada_bwd_w", grid=(2, s // sb),
        in_specs=[_full((8, D)), pl.BlockSpec((1, 8, sb), lambda l, j: (l, 0, j))],
        out_specs=pl.BlockSpec((1, D, sb), lambda l, j: (l, 0, j)), out_shape=jax.ShapeDtypeStruct((2, D, s), F32),
        compiler_params=_cparams(2),
    )(c_all, dmod_sh)

    def body_b(d_ref, o_ref):
        acc = d_ref[0, 0:1, :]
        for k in range(1, 8):
            acc = acc + d_ref[0, k:k + 1, :]
        o_ref[0] = acc

    gb = pl.pallas_call(
        body_b, name="ada_bwd_b", grid=(2,), in_specs=[pl.BlockSpec((1, 8, 6 * D), lambda l: (l, 0, 0))],
        out_specs=pl.BlockSpec((1, 1, 6 * D), lambda l: (l, 0, 0)), out_shape=jax.ShapeDtypeStruct((2, 1, 6 * D), F32),
        compiler_params=_cparams(1),
    )(dmod_all)
    return gw, gb.reshape(2, 6 * D)


def _f_in(h, nw, sc, sh, w_main, w_dt):
    t = h.shape[0]
    tb = _tblock(t)

    def body(h_ref, nw_ref, sc_ref, sh_ref, w_ref, wd_ref, p_ref, dt_ref, u_ref):
        _, n = _rms(h_ref[...])
        u = ((n * nw_ref[...]) * (1.0 + sc_ref[...]) + sh_ref[...]).astype(BF16)
        u_ref[...] = u
        p_ref[...] = _dot(u, w_ref[...])
        dt_ref[...] = _dot(u, wd_ref[...])

    return pl.pallas_call(
        body, name="f_in", grid=(t // tb,),
        in_specs=[pl.BlockSpec((tb, D), lambda i: (i, 0)), _row(D), _row(D), _row(D), _full((D, PW)), _full((D, DTW))],
        out_specs=[pl.BlockSpec((tb, PW), lambda i: (i, 0)), pl.BlockSpec((tb, DTW), lambda i: (i, 0)),
                   pl.BlockSpec((tb, D), lambda i: (i, 0))],
        out_shape=[jax.ShapeDtypeStruct((t, PW), F32), jax.ShapeDtypeStruct((t, DTW), F32), jax.ShapeDtypeStruct((t, D), BF16)],
        compiler_params=_cparams(1),
    )(h, nw, sc, sh, w_main, w_dt)


def _b_in_du(dab, dz, dxbc, ds5, ddt, w_main, w_dt):
    t = dab.shape[0]
    tb = _tblock(t)

    def body(a_ref, z_ref, x_ref, s_ref, d_ref, w_ref, wd_ref, o_ref):
        acc = _bdot(a_ref[...], w_ref[:, 0:1024], NT)
        acc += _bdot(z_ref[...], w_ref[:, 1024:1280], NT)
        acc += _bdot(s_ref[...], w_ref[:, 1280:1536], NT)
        acc += _bdot(x_ref[...], w_ref[:, 1536:2304], NT)
        acc += _bdot(d_ref[...], wd_ref[...], NT)
        o_ref[...] = acc

    blk = lambda n: pl.BlockSpec((tb, n), lambda i: (i, 0))
    return pl.pallas_call(
        body, name="b_in_du", grid=(t // tb,),
        in_specs=[blk(1024), blk(256), blk(768), blk(256), blk(DTW), _full((D, PW)), _full((D, DTW))],
        out_specs=blk(D), out_shape=jax.ShapeDtypeStruct((t, D), F32), compiler_params=_cparams(1),
    )(dab, dz, dxbc, ds5, ddt, w_main, w_dt)


def _b_normmod(du, x, dres, gated, nw, sc, name):
    t = x.shape[0]
    tb = _tblock(t)

    def body(du_ref, x_ref, dr_ref, g_ref, nw_ref, sc_ref, dx_ref, dsc_ref, dsh_ref, dnw_ref, dg_ref):
        @pl.when(pl.program_id(0) == 0)
        def _():
            for r in (dsc_ref, dsh_ref, dnw_ref, dg_ref):
                r[...] = jnp.zeros_like(r)

        du_v = du_ref[...]
        r, n = _rms(x_ref[...])
        nwv = nw_ref[...]
        scale = 1.0 + sc_ref[...]
        dsc_ref[...] += _colsum(du_v * (n * nwv))
        dsh_ref[...] += _colsum(du_v)
        dnw_ref[...] += _colsum(du_v * scale * n)
        dres_v = dr_ref[...]
        dg_ref[...] += _colsum(dres_v * g_ref[...])
        dx_ref[...] = dres_v + _rms_bwd(r, n, du_v * scale * nwv)

    blk = pl.BlockSpec((tb, D), lambda i: (i, 0))
    row = jax.ShapeDtypeStruct((1, D), F32)
    return pl.pallas_call(
        body, name=name, grid=(t // tb,), in_specs=[blk, blk, blk, blk, _row(D), _row(D)],
        out_specs=[blk, _row(D), _row(D), _row(D), _row(D)], out_shape=[jax.ShapeDtypeStruct((t, D), F32), row, row, row, row],
        compiler_params=_cparams(1),
    )(du, x, dres, gated, nw, sc)


HALO = 16


def _lane_group(shape):
    return lax.broadcasted_iota(jnp.int32, shape, 1) // 64


def _window_select(g, s2, s4, s8, s16):
    return jnp.where(g == 0, s2, jnp.where(g == 1, s4, jnp.where(g == 2, s8, s16)))


def _pool_count(t0, rows):
    g = _lane_group((rows, GW))
    win = _window_select(g, 2, 4, 8, 16)
    tt = t0 + lax.broadcasted_iota(jnp.int32, (rows, GW), 0)
    return jnp.minimum(tt + 1, win).astype(F32)


def _pool_p(v_ext, t0, tb):
    s2 = v_ext + _roll(v_ext, 1)
    s4 = s2 + _roll(s2, 2)
    s8 = s4 + _roll(s4, 4)
    s16 = s8 + _roll(s8, 8)
    ws = _window_select(_lane_group(v_ext.shape), s2, s4, s8, s16)[HALO:]
    return ws / _pool_count(t0, tb) - v_ext[HALO:]


def _sconv(q_ext, w):
    return (_roll(q_ext, 2) * w[0:1] + _roll(q_ext, 1) * w[1:2] + q_ext * w[2:3])[HALO:]


def _halo_specs(t, tb, cols, col_block):
    per = tb // HALO
    last = t // HALO - 1
    prev = pl.BlockSpec((HALO, cols), lambda i: (jnp.maximum(i * per - 1, 0), col_block))
    nxt = pl.BlockSpec((HALO, cols), lambda i: (jnp.minimum((i + 1) * per, last), col_block))
    return prev, nxt


def _f_ab(proj, pool_mat, pool_scale, sconv_w):
    t = proj.shape[0]
    tb = _tblock(t)
    prev, _ = _halo_specs(t, tb, 1024, 0)

    def body(p_ref, h_ref, pm_ref, ps_ref, sw_ref, ya_ref, yb_ref):
        i = pl.program_id(0)
        halo = jnp.where(i > 0, h_ref[...], 0.0)
        ext = jnp.concatenate([halo, p_ref[...]], axis=0)
        p = _pool_p(ext[:, 0:256], i * tb, tb)
        ya_ref[...] = _bdot(p, pm_ref[...]) * ps_ref[...]
        q_ext = ext[:, 512:768] * ext[:, 768:1024]
        yb_ref[...] = p_ref[:, 256:512] * _sconv(q_ext, sw_ref[...])

    blk = pl.BlockSpec((tb, GW), lambda i: (i, 0))
    sds = jax.ShapeDtypeStruct((t, GW), F32)
    return pl.pallas_call(
        body, name="f_ab", grid=(t // tb,),
        in_specs=[pl.BlockSpec((tb, 1024), lambda i: (i, 0)), prev, _full((GW, GW)), _row(GW), _full((3, GW))],
        out_specs=[blk, blk], out_shape=[sds, sds], compiler_params=_cparams(1),
    )(proj, proj, pool_mat, pool_scale, sconv_w)


def _b_ab(proj, dya, dyb, pool_mat, pool_scale, sconv_w):
    t = proj.shape[0]
    tb = _tblock(t)
    nb = t // tb
    prev, nxt = _halo_specs(t, tb, 1024, 0)
    _, nxt_g = _halo_specs(t, tb, GW, 0)
    n_ext = tb + HALO

    def body(p_ref, hp_ref, hn_ref, da_ref, dan_ref, db_ref, dbn_ref, pm_ref, ps_ref, sw_ref,
             o_ref, dpm_ref, dps_ref, dsw_ref):
        i = pl.program_id(0)

        @pl.when(i == 0)
        def _():
            for r in (dpm_ref, dps_ref, dsw_ref):
                r[...] = jnp.zeros_like(r)

        last = i == nb - 1
        halo = jnp.where(i > 0, hp_ref[...], 0.0)
        main = p_ref[...]
        ext = jnp.concatenate([halo, main], axis=0)
        scale = ps_ref[...]
        pm = pm_ref[...]
        p = _pool_p(ext[:, 0:256], i * tb, tb)
        da = da_ref[...]
        dps_ref[...] += _colsum(da * _bdot(p, pm))
        da_ext = jnp.concatenate([da, jnp.where(last, 0.0, dan_ref[...])], axis=0)
        dys = da_ext * scale
        dpm_ref[...] += _bdot(p, dys[:tb], TN)
        dp = _bdot(dys, pm, NT)
        dpc = dp / _pool_count(i * tb, n_ext)
        a2 = dpc + _roll(dpc, n_ext - 1)
        a4 = a2 + _roll(a2, n_ext - 2)
        a8 = a4 + _roll(a4, n_ext - 4)
        a16 = a8 + _roll(a8, n_ext - 8)
        o_ref[:, 0:256] = (_window_select(_lane_group(dpc.shape), a2, a4, a8, a16) - dp)[:tb]
        w = sw_ref[...]
        gb, gc, hh = main[:, 256:512], main[:, 512:768], main[:, 768:1024]
        q_ext = ext[:, 512:768] * ext[:, 768:1024]
        db = db_ref[...]
        o_ref[:, 256:512] = db * _sconv(q_ext, w)
        gb_next = hn_ref[:, 256:512]
        dconv = jnp.concatenate([db * gb, jnp.where(last, 0.0, dbn_ref[...] * gb_next)], axis=0)
        dq = (dconv * w[2:3] + _roll(dconv, n_ext - 1) * w[1:2] + _roll(dconv, n_ext - 2) * w[0:1])[:tb]
        o_ref[:, 512:768] = dq * hh
        o_ref[:, 768:1024] = dq * gc
        dc = dconv[:tb]
        dsw_ref[0:1, :] += _colsum(dc * _roll(q_ext, 2)[HALO:])
        dsw_ref[1:2, :] += _colsum(dc * _roll(q_ext, 1)[HALO:])
        dsw_ref[2:3, :] += _colsum(dc * q_ext[HALO:])

    blk = pl.BlockSpec((tb, GW), lambda i: (i, 0))
    return pl.pallas_call(
        body, name="b_ab", grid=(nb,),
        in_specs=[pl.BlockSpec((tb, 1024), lambda i: (i, 0)), prev, nxt, blk, nxt_g, blk, nxt_g,
                  _full((GW, GW)), _row(GW), _full((3, GW))],
        out_specs=[pl.BlockSpec((tb, 1024), lambda i: (i, 0)), _full((GW, GW)), _row(GW), _full((3, GW))],
        out_shape=[jax.ShapeDtypeStruct((t, 1024), F32), jax.ShapeDtypeStruct((GW, GW), F32),
                   jax.ShapeDtypeStruct((1, GW), F32), jax.ShapeDtypeStruct((3, GW), F32)],
        compiler_params=_cparams(1),
    )(proj, proj, proj, dya, dya, dyb, dyb, pool_mat, pool_scale, sconv_w)


CH = 8


def _ssd_conv(x, halo, w, b):
    ext = jnp.concatenate([halo, x], axis=0)
    pre = ext * w[3:4] + _roll(ext, 1) * w[2:3] + _roll(ext, 2) * w[1:2] + _roll(ext, 3) * w[0:1] + b
    return pre[CH:], ext


def _ssd_common(dt_raw, dtb, alog):
    ll = dt_raw.shape[0]
    dtv = _softplus(dt_raw + dtb)
    a_row = -jnp.exp(alog)
    r = lax.broadcasted_iota(jnp.int32, (ll, ll), 0)
    c = lax.broadcasted_iota(jnp.int32, (ll, ll), 1)
    tril = (r >= c).astype(F32)
    cs = _dot(tril, dtv * a_row, prec=HI)
    return dtv, a_row, cs, cs.T, r >= c


def _ssd_bc(act_b, g):
    return act_b[:, 256 + NS * g:256 + NS * (g + 1)], act_b[:, 512 + NS * g:512 + NS * (g + 1)]


def _ssd_gmat(act_b):
    return [_dot(_ssd_bc(act_b, g)[1], _ssd_bc(act_b, g)[0], NT) for g in range(2)]


def _ssd_head(h, act, act_b, dtv, cs, cs_t, causal, gmat):
    g = h // 2
    xs = act[:, HP * h:HP * (h + 1)]
    bm, cm = _ssd_bc(act_b, g)
    cs_c = cs[:, h:h + 1]
    cs_r = cs_t[h:h + 1, :]
    mdec = jnp.where(causal, jnp.exp(jnp.minimum(cs_c - cs_r, 0.0)), 0.0)
    sc = gmat[g] * mdec
    dt_c = dtv[:, h:h + 1]
    xdt = xs * dt_c
    e = jnp.exp(cs_c)
    cs_last = cs[SSD_L - 1:SSD_L, h:h + 1]
    wdec = jnp.exp(cs_last - cs_c)
    return xs, bm, cm, cs_c, mdec, sc, dt_c, xdt, e, cs_last, wdec


def _f_ssd(proj, dtp, conv_w, conv_b, dt_bias, a_log, d_skip):
    t = proj.shape[0]
    nc = t // SSD_L
    rows = SSD_SUB * SSD_L
    per = rows // CH

    def body(x_ref, hx_ref, dt_ref, z_ref, cw_ref, cb_ref, dtb_ref, al_ref, dk_ref, y_ref, yp_ref, sp_ref, s_ref):
        i = pl.program_id(0)

        @pl.when(i == 0)
        def _():
            s_ref[...] = jnp.zeros_like(s_ref)

        state = [s_ref[h] for h in range(NH)]
        for sub in range(SSD_SUB):
            r0 = sub * SSD_L
            rs = slice(r0, r0 + SSD_L)
            halo = jnp.where(i > 0, hx_ref[...], 0.0) if sub == 0 else x_ref[r0 - CH:r0, :]
            pre, _ = _ssd_conv(x_ref[rs, :], halo, cw_ref[...], cb_ref[...])
            act = _silu(pre)
            dtv, _, cs, cs_t, causal = _ssd_common(dt_ref[rs, :], dtb_ref[...], al_ref[...])
            gmat = _ssd_gmat(act)
            for h in range(NH):
                xs, bm, cm, _, _, sc, _, xdt, e, cs_last, wdec = _ssd_head(h, act, act, dtv, cs, cs_t, causal, gmat)
                prev = state[h]
                sp_ref[sub, h] = prev
                y = _dot(sc, xdt) + e * _dot(cm, prev, NT) + xs * dk_ref[0:1, h:h + 1]
                yp_ref[rs, HP * h:HP * (h + 1)] = y
                state[h] = prev * jnp.exp(cs_last) + _dot(xdt * wdec, bm, TN)
            y_ref[rs, :] = yp_ref[rs, :] * _silu(z_ref[rs, :])
        for h in range(NH):
            s_ref[h] = state[h]

    blk = pl.BlockSpec((rows, GW), lambda i: (i, 0))
    sds = jax.ShapeDtypeStruct((t, GW), F32)
    return pl.pallas_call(
        body, name="f_ssd", grid=(nc // SSD_SUB,),
        in_specs=[pl.BlockSpec((rows, 768), lambda i: (i, 2)),
                  pl.BlockSpec((CH, 768), lambda i: (jnp.maximum(i * per - 1, 0), 2)),
                  pl.BlockSpec((rows, DTW), lambda i: (i, 0)),
                  pl.BlockSpec((rows, GW), lambda i: (i, 4)),
                  _full((4, 768)), _row(768), _row(DTW), _row(DTW), _row(DTW)],
        out_specs=[blk, blk, pl.BlockSpec((SSD_SUB, NH, HP, NS), lambda i: (i, 0, 0, 0))],
        out_shape=[sds, sds, jax.ShapeDtypeStruct((nc, NH, HP, NS), F32)],
        scratch_shapes=[pltpu.VMEM((NH, HP, NS), F32)], compiler_params=_cparams(1),
    )(proj, proj, dtp, proj, conv_w, conv_b, dt_bias, a_log, d_skip)


def _b_ssd(proj, dtp, ypre, dyc, sprev, conv_w, conv_b, dt_bias, a_log, d_skip):
    t = proj.shape[0]
    nc = t // SSD_L
    steps = nc // SSD_SUB
    rows = SSD_SUB * SSD_L
    per = rows // CH
    n_ext = SSD_L + CH

    def chunk(sub, halo, dnext, ds_in, refs):
        (x_ref, dt_ref, z_ref, yp_ref, dy_ref, sp_ref, cw_ref, cb_ref, dtb_ref, al_ref, dk_ref,
         dz_ref, dx_ref, ddt_ref, dact_ref) = refs
        rs = slice(sub * SSD_L, (sub + 1) * SSD_L)
        dact = dact_ref.at[sub]
        w = cw_ref[...]
        pre, ext = _ssd_conv(x_ref[rs, :], halo, w, cb_ref[...])
        act = _silu(pre)
        dt_raw = dt_ref[rs, :]
        dtv, a_row, cs, cs_t, causal = _ssd_common(dt_raw, dtb_ref[...], al_ref[...])
        act_b = act.astype(BF16)
        gmat = _ssd_gmat(act)
        z = z_ref[rs, :]
        dyc_v = dy_ref[rs, :]
        dz_ref[rs, :] = dyc_v * yp_ref[rs, :] * _dsilu(z)
        dy_all = dyc_v * _silu(z)
        lane = lax.broadcasted_iota(jnp.int32, (SSD_L, DTW), 1)
        rowi = lax.broadcasted_iota(jnp.int32, (SSD_L, 1), 0)
        dcs_mat = jnp.zeros((SSD_L, DTW), F32)
        ddtx_mat = jnp.zeros((SSD_L, DTW), F32)
        ddk_row = jnp.zeros((1, DTW), F32)
        lane1 = lax.broadcasted_iota(jnp.int32, (1, DTW), 1)
        dbm = [None, None]
        dcm = [None, None]
        ds_out = []
        for h in range(NH):
            g = h // 2
            xs, bm, cm, _, mdec, sc, dt_c, xdt, e, cs_last, wdec = _ssd_head(h, act, act_b, dtv, cs, cs_t, causal, gmat)
            dy = dy_all[:, HP * h:HP * (h + 1)]
            prev = sp_ref[sub, h]
            ds = ds_in[h]
            dy_b, xdt_b, prev_b, ds_b = dy.astype(BF16), xdt.astype(BF16), prev.astype(BF16), ds.astype(BF16)
            dsc = _dot(dy, xdt, NT)
            q = dsc * sc
            dg = (dsc * mdec).astype(BF16)
            dxdt = _dot(sc.astype(BF16), dy_b, TN)
            dcs = jnp.sum(q, axis=1, keepdims=True) - jnp.sum(q.T, axis=1, keepdims=True)
            dc_h = _dot(dg, bm)
            db_h = _dot(dg, cm, TN)
            bm32, cm32 = _ssd_bc(act, g)
            cp = _dot(cm32, prev, NT)
            dcs += jnp.sum(dy * cp, axis=1, keepdims=True) * e
            ey = (e * dy).astype(BF16)
            dc_h += _dot(ey, prev_b)
            dprev = _dot(ey, cm, TN)
            elast = jnp.exp(cs_last)
            dprev += ds * elast
            dcs_last = jnp.sum(ds * prev, keepdims=True) * elast
            bds = _dot(bm32, ds, NT)
            dxdt += wdec * bds
            db_h += wdec * _dot(xdt_b, ds_b)
            dw = jnp.sum(xdt * bds, axis=1, keepdims=True) * wdec
            dcs -= dw
            dcs_last += jnp.sum(dw, keepdims=True)
            dcs += jnp.where(rowi == SSD_L - 1, dcs_last, 0.0)
            ds_out.append(dprev)
            dact[:, HP * h:HP * (h + 1)] = dxdt * dt_c + dy * dk_ref[0:1, h:h + 1]
            dcs_mat = jnp.where(lane == h, dcs, dcs_mat)
            ddtx_mat = jnp.where(lane == h, jnp.sum(dxdt * xs, axis=1, keepdims=True), ddtx_mat)
            ddk_row = jnp.where(lane1 == h, jnp.sum(dy * xs, keepdims=True), ddk_row)
            dbm[g] = db_h if dbm[g] is None else dbm[g] + db_h
            dcm[g] = dc_h if dcm[g] is None else dcm[g] + dc_h
        for g in range(2):
            dact[:, 256 + NS * g:256 + NS * (g + 1)] = dbm[g]
            dact[:, 512 + NS * g:512 + NS * (g + 1)] = dcm[g]
        r2 = lax.broadcasted_iota(jnp.int32, (SSD_L, SSD_L), 0)
        c2 = lax.broadcasted_iota(jnp.int32, (SSD_L, SSD_L), 1)
        dadt = _dot((c2 >= r2).astype(F32), dcs_mat, prec=HI)
        ddt = jnp.where(lane < NH, (dadt * a_row + ddtx_mat) * _sig(dt_raw + dtb_ref[...]), 0.0)
        ddt_ref[rs, :] = ddt
        dpre = dact[...] * _dsilu(pre)
        dcw = jnp.concatenate([_colsum(dpre * _roll(ext, 3 - k)[CH:]) for k in range(4)], axis=0)
        dext = jnp.concatenate([dpre, dnext], axis=0)
        dx_ref[rs, :] = (dext * w[3:4] + _roll(dext, n_ext - 1) * w[2:3] + _roll(dext, n_ext - 2) * w[1:2]
                         + _roll(dext, n_ext - 3) * w[0:1])[:SSD_L]
        acc = (dcw, _colsum(dpre), _colsum(ddt), _colsum(dadt * dtv) * a_row, ddk_row)
        return dpre[0:CH], ds_out, acc

    def body(x_ref, hx_ref, dt_ref, z_ref, yp_ref, dy_ref, sp_ref, cw_ref, cb_ref, dtb_ref, al_ref, dk_ref,
             dz_ref, dx_ref, ddt_ref, dcw_ref, dcb_ref, ddtb_ref, dal_ref, ddk_ref, ds_ref, dnext_ref, dact_ref):
        i = pl.program_id(0)
        acc_refs = (dcw_ref, dcb_ref, ddtb_ref, dal_ref, ddk_ref)

        @pl.when(i == 0)
        def _():
            ds_ref[...] = jnp.zeros_like(ds_ref)
            dnext_ref[...] = jnp.zeros_like(dnext_ref)
            for r in acc_refs:
                r[...] = jnp.zeros_like(r)

        refs = (x_ref, dt_ref, z_ref, yp_ref, dy_ref, sp_ref, cw_ref, cb_ref, dtb_ref, al_ref, dk_ref, dz_ref, dx_ref, ddt_ref,
                dact_ref)
        ds = [ds_ref[h] for h in range(NH)]
        dnext = dnext_ref[...]
        total = None
        for sub in reversed(range(SSD_SUB)):
            if sub == 0:
                halo = jnp.where(i == steps - 1, 0.0, hx_ref[...])
            else:
                halo = x_ref[sub * SSD_L - CH:sub * SSD_L, :]
            dnext, ds, acc = chunk(sub, halo, dnext, ds, refs)
            total = acc if total is None else tuple(a + b for a, b in zip(total, acc))
        for h in range(NH):
            ds_ref[h] = ds[h]
        dnext_ref[...] = dnext
        for r, v in zip(acc_refs, total):
            r[...] += v

    rev = lambda i: steps - 1 - i
    blk = lambda n, cb=0: pl.BlockSpec((rows, n), lambda i: (rev(i), cb))
    row = lambda n: jax.ShapeDtypeStruct((1, n), F32)
    return pl.pallas_call(
        body, name="b_ssd", grid=(steps,),
        in_specs=[blk(768, 2), pl.BlockSpec((CH, 768), lambda i: (jnp.maximum(rev(i) * per - 1, 0), 2)),
                  blk(DTW), blk(GW, 4), blk(GW), blk(GW), pl.BlockSpec((SSD_SUB, NH, HP, NS), lambda i: (rev(i), 0, 0, 0)),
                  _full((4, 768)), _row(768), _row(DTW), _row(DTW), _row(DTW)],
        out_specs=[blk(GW), blk(768), blk(DTW), _full((4, 768)), _row(768), _row(DTW), _row(DTW), _row(DTW)],
        out_shape=[jax.ShapeDtypeStruct((t, GW), F32), jax.ShapeDtypeStruct((t, 768), F32), jax.ShapeDtypeStruct((t, DTW), F32),
                   jax.ShapeDtypeStruct((4, 768), F32), row(768), row(DTW), row(DTW), row(DTW)],
        scratch_shapes=[pltpu.VMEM((NH, HP, NS), F32), pltpu.VMEM((CH, 768), F32), pltpu.VMEM((SSD_SUB, SSD_L, 768), F32)],
        compiler_params=_cparams(1),
    )(proj, proj, dtp, proj, ypre, dyc, sprev, conv_w, conv_b, dt_bias, a_log, d_skip)


def _s5_block(t):
    return min(t, 256)


def _seg_t():
    r = lax.broadcasted_iota(jnp.int32, (64, 1024), 0)
    c = lax.broadcasted_iota(jnp.int32, (64, 1024), 1)
    return (c // 16 == r).astype(F32)


def _s5_prep_math(a_re, a_im, lstep, b_re, b_im):
    step = jnp.exp(lstep)
    ars = a_re * step
    ais = a_im * step
    mag = jnp.exp(ars)
    lr = mag * jnp.cos(ais)
    li = mag * jnp.sin(ais)
    den = a_re * a_re + a_im * a_im
    nr = lr - 1.0
    f_re = (nr * a_re + li * a_im) / den
    f_im = (li * a_re - nr * a_im) / den
    seg = _seg_t()
    fr = _dot(f_re, seg, prec=HI)
    fi = _dot(f_im, seg, prec=HI)
    return lr, li, fr * b_re - fi * b_im, fr * b_im + fi * b_re, ars, ais


def _s5_prep(a_re, a_im, lstep, b_re, b_im):
    def body(ar, ai, ls, br, bi, lr_o, li_o, bbr_o, bbi_o, ars_o, ais_o):
        outs = _s5_prep_math(ar[...], ai[...], ls[...], br[...], bi[...])
        for o, v in zip((lr_o, li_o, bbr_o, bbi_o, ars_o, ais_o), outs):
            o[...] = v

    s64 = jax.ShapeDtypeStruct((16, 64), F32)
    s1k = jax.ShapeDtypeStruct((16, 1024), F32)
    return pl.pallas_call(body, name="s5_prep", out_shape=[s64, s64, s1k, s1k, s64, s64])(a_re, a_im, lstep, b_re, b_im)


def _s5_prep_bwd(a_re, a_im, lstep, b_re, b_im, dlr, dli, dbbr, dbbi):
    def body(ar, ai, ls, br, bi, g0, g1, g2, g3, o0, o1, o2, o3, o4):
        f = lambda *a: _s5_prep_math(*a)[:4]
        _, vjp = jax.vjp(f, ar[...], ai[...], ls[...], br[...], bi[...])
        for o, v in zip((o0, o1, o2, o3, o4), vjp((g0[...], g1[...], g2[...], g3[...]))):
            o[...] = v

    s64 = jax.ShapeDtypeStruct((16, 64), F32)
    s1k = jax.ShapeDtypeStruct((16, 1024), F32)
    return pl.pallas_call(body, name="s5_prep_bwd", out_shape=[s64, s64, jax.ShapeDtypeStruct((16, 1), F32), s1k, s1k])(
        a_re, a_im, lstep, b_re, b_im, dlr, dli, dbbr, dbbi)


SUB = 8


def _s5_tables(ars, ais):
    def body(ar, ai, tr, ti):
        rr = lax.broadcasted_iota(jnp.int32, (8 * SUB, S5_P), 0)
        seg, r = rr // SUB, rr % SUB
        step = jnp.where((seg == 1) | (seg == 4), 1, jnp.where((seg == 2) | (seg == 5), 2, 4))
        n = jnp.where(seg == 0, r + 1, jnp.where(seg == 7, SUB - r, step))
        fwd_gap = jnp.where(seg <= 3, r - step, SUB - step - 1 - r)
        gap = jnp.where((seg == 0) | (seg == 7), 0, fwd_gap)
        nf = n.astype(F32)
        mag = jnp.where(gap >= 0, jnp.exp(nf * ar[...]), 0.0)
        tr[...] = mag * jnp.cos(nf * ai[...])
        ti[...] = mag * jnp.sin(nf * ai[...])

    sds = jax.ShapeDtypeStruct((8 * SUB, S5_P), F32)
    return pl.pallas_call(body, name="s5_tables", out_shape=[sds] * 2)(ars, ais)


def _s5_table(tb_r, tb_i, k):
    return tb_r[SUB * k:SUB * (k + 1), :], tb_i[SUB * k:SUB * (k + 1), :]


def _s5_scan(bu_r, bu_i, tb_r, tb_i, c_r, c_i, lb):
    nt = lb // SUB
    sr, si = bu_r.reshape(nt, SUB, S5_P), bu_i.reshape(nt, SUB, S5_P)
    for j, k in enumerate((1, 2, 4)):
        mr, mi = _s5_table(tb_r, tb_i, 1 + j)
        tr, ti = pltpu.roll(sr, k, axis=1), pltpu.roll(si, k, axis=1)
        sr, si = sr + mr * tr - mi * ti, si + mr * ti + mi * tr
    pr, pi = _s5_table(tb_r, tb_i, 0)
    out_r, out_i = [], []
    for j in range(nt):
        a_r = sr[j] + pr * c_r - pi * c_i
        a_i = si[j] + pr * c_i + pi * c_r
        out_r.append(a_r)
        out_i.append(a_i)
        c_r, c_i = a_r[SUB - 1:SUB], a_i[SUB - 1:SUB]
    return jnp.concatenate(out_r, axis=0), jnp.concatenate(out_i, axis=0)


def _s5_rscan(g_r, g_i, tb_r, tb_i, n_r, n_i, lb):
    nt = lb // SUB
    gr, gi = g_r.reshape(nt, SUB, S5_P), g_i.reshape(nt, SUB, S5_P)
    for j, k in enumerate((1, 2, 4)):
        mr, mi = _s5_table(tb_r, tb_i, 4 + j)
        tr, ti = pltpu.roll(gr, SUB - k, axis=1), pltpu.roll(gi, SUB - k, axis=1)
        gr, gi = gr + mr * tr + mi * ti, gi + mr * ti - mi * tr
    qr, qi = _s5_table(tb_r, tb_i, 7)
    out_r, out_i = [None] * nt, [None] * nt
    for j in reversed(range(nt)):
        a_r = gr[j] + qr * n_r + qi * n_i
        a_i = gi[j] + qr * n_i - qi * n_r
        out_r[j], out_i[j] = a_r, a_i
        n_r, n_i = a_r[0:1], a_i[0:1]
    return jnp.concatenate(out_r, axis=0), jnp.concatenate(out_i, axis=0)


def _s5_y(u, sr, si, cre, cim, dsk):
    return _bdot(sr, cre) + _bdot(si, cim) + dsk * u


def _f_s5(proj, bmat, cre, cim, p_r, p_i, dsk, glu_w, glu_b):
    t = proj.shape[0]
    lb = _s5_block(t)
    nb = t // lb

    def body(u_ref, bm_ref, cr_ref, ci_ref, pr_ref, pi_ref, dk_ref, gw_ref, gb_ref, y_ref, car_ref, s_ref, st_ref):
        @pl.when(pl.program_id(0) == 0)
        def _():
            st_ref[...] = jnp.zeros_like(st_ref)

        u = u_ref[...]
        bu = _bdot(u, bm_ref[...])
        c_r, c_i = st_ref[0:1, 0:S5_P], st_ref[0:1, S5_P:]
        car_ref[0] = st_ref[0:1, :]
        sr, si = _s5_scan(bu[:, :S5_P], bu[:, S5_P:], pr_ref, pi_ref, c_r, c_i, lb)
        st_ref[0:1, 0:S5_P] = sr[lb - 1:lb]
        st_ref[0:1, S5_P:] = si[lb - 1:lb]
        sr_b, si_b = sr.astype(BF16), si.astype(BF16)
        s_ref[:, 0:S5_P] = sr_b
        s_ref[:, S5_P:] = si_b
        gel = _gelu(_s5_y(u, sr_b, si_b, cr_ref[...], ci_ref[...], dk_ref[...]))
        y_ref[...] = gel * _sig(_bdot(gel, gw_ref[...]) + gb_ref[...])

    return pl.pallas_call(
        body, name="f_s5", grid=(nb,),
        in_specs=[pl.BlockSpec((lb, GW), lambda i: (i, 5)),
                  _full((GW, 2 * S5_P)), _full((S5_P, GW)), _full((S5_P, GW)), _full((8 * SUB, S5_P)), _full((8 * SUB, S5_P)),
                  _row(GW), _full((GW, GW)), _row(GW)],
        out_specs=[pl.BlockSpec((lb, GW), lambda i: (i, 0)), pl.BlockSpec((1, 1, 2 * S5_P), lambda i: (i, 0, 0)),
                   pl.BlockSpec((lb, 2 * S5_P), lambda i: (i, 0))],
        out_shape=[jax.ShapeDtypeStruct((t, GW), F32), jax.ShapeDtypeStruct((nb, 1, 2 * S5_P), F32),
                   jax.ShapeDtypeStruct((t, 2 * S5_P), BF16)],
        scratch_shapes=[pltpu.VMEM((8, 2 * S5_P), F32)], compiler_params=_cparams(1),
    )(proj, bmat, cre, cim, p_r, p_i, dsk, glu_w, glu_b)


def _b_s5(proj, dyd, carries, states, bmat, cre, cim, p_r, p_i, dsk, glu_w, glu_b):
    t = proj.shape[0]
    lb = _s5_block(t)
    nb = t // lb

    def body(u_ref, dy_ref, car_ref, s_ref, bm_ref, cr_ref, ci_ref, pr_ref, pi_ref, dk_ref, gw_ref, gb_ref,
             du_ref, dbm_ref, dcr_ref, dci_ref, dlam_ref, ddk_ref, dgw_ref, dgb_ref, gc_ref):
        @pl.when(pl.program_id(0) == 0)
        def _():
            gc_ref[...] = jnp.zeros_like(gc_ref)
            for r in (dbm_ref, dcr_ref, dci_ref, dlam_ref, ddk_ref, dgw_ref, dgb_ref):
                r[...] = jnp.zeros_like(r)

        u = u_ref[...]
        bm = bm_ref[...]
        u_b = u.astype(BF16)
        c_r, c_i = car_ref[0, 0:1, 0:S5_P], car_ref[0, 0:1, S5_P:]
        cre_v, cim_v, dk, gw = cr_ref[...], ci_ref[...], dk_ref[...], gw_ref[...]
        sr_b, si_b = s_ref[:, 0:S5_P], s_ref[:, S5_P:]
        sr, si = sr_b.astype(F32), si_b.astype(F32)
        y = _dot(sr_b, cre_v) + _dot(si_b, cim_v) + dk * u
        gel = _gelu(y)
        gel_b = gel.astype(BF16)
        gate = _sig(_dot(gel_b, gw) + gb_ref[...])
        dout = dy_ref[...]
        t1 = dout * gel * gate * (1.0 - gate)
        t1_b = t1.astype(BF16)
        dgw_ref[...] += _dot(gel_b, t1_b, TN)
        dgb_ref[...] += _colsum(t1)
        dyv = (dout * gate + _dot(t1_b, gw, NT)) * _dgelu(y)
        dyv_b = dyv.astype(BF16)
        ddk_ref[...] += _colsum(dyv * u)
        dcr_ref[...] += _dot(sr_b, dyv_b, TN)
        dci_ref[...] += _dot(si_b, dyv_b, TN)
        gr = _dot(dyv_b, cre_v, NT)
        gi = _dot(dyv_b, cim_v, NT)
        row = lax.broadcasted_iota(jnp.int32, (lb, S5_P), 0)
        n_r, n_i = gc_ref[0:1, 0:S5_P], gc_ref[0:1, S5_P:]
        gr, gi = _s5_rscan(gr, gi, pr_ref, pi_ref, n_r, n_i, lb)
        gc_ref[0:1, 0:S5_P] = gr[0:1]
        gc_ref[0:1, S5_P:] = gi[0:1]
        gcat = jnp.concatenate([gr, gi], axis=1).astype(BF16)
        dbm_ref[...] += _dot(u_b, gcat, TN)
        du_ref[...] = dyv * dk + _dot(gcat, bm, NT)
        spr = jnp.where(row >= 1, _roll(sr, 1), c_r)
        spi = jnp.where(row >= 1, _roll(si, 1), c_i)
        dlam_ref[0:1, :] += _colsum(gr * spr + gi * spi)
        dlam_ref[1:2, :] += _colsum(gi * spr - gr * spi)

    rev = lambda i: nb - 1 - i
    return pl.pallas_call(
        body, name="b_s5", grid=(nb,),
        in_specs=[pl.BlockSpec((lb, GW), lambda i: (rev(i), 5)), pl.BlockSpec((lb, GW), lambda i: (rev(i), 0)),
                  pl.BlockSpec((1, 1, 2 * S5_P), lambda i: (rev(i), 0, 0)), pl.BlockSpec((lb, 2 * S5_P), lambda i: (rev(i), 0)),
                  _full((GW, 2 * S5_P)), _full((S5_P, GW)), _full((S5_P, GW)), _full((8 * SUB, S5_P)), _full((8 * SUB, S5_P)),
                  _row(GW), _full((GW, GW)), _row(GW)],
        out_specs=[pl.BlockSpec((lb, GW), lambda i: (rev(i), 0)), _full((GW, 2 * S5_P)), _full((S5_P, GW)), _full((S5_P, GW)),
                   _full((2, S5_P)), _row(GW), _full((GW, GW)), _row(GW)],
        out_shape=[jax.ShapeDtypeStruct((t, GW), F32), jax.ShapeDtypeStruct((GW, 2 * S5_P), F32),
                   jax.ShapeDtypeStruct((S5_P, GW), F32), jax.ShapeDtypeStruct((S5_P, GW), F32),
                   jax.ShapeDtypeStruct((2, S5_P), F32), jax.ShapeDtypeStruct((1, GW), F32),
                   jax.ShapeDtypeStruct((GW, GW), F32), jax.ShapeDtypeStruct((1, GW), F32)],
        scratch_shapes=[pltpu.VMEM((8, 2 * S5_P), F32)], compiler_params=_cparams(1),
    )(proj, dyd, carries, states, bmat, cre, cim, p_r, p_i, dsk, glu_w, glu_b)


def _group_norm(ys, bw):
    outs, stats = [], []
    for g, y in enumerate(ys):
        r, n = _rms(y)
        stats.append((r, n))
        outs.append(n * bw[:, GW * g:GW * (g + 1)])
    return jnp.concatenate(outs, axis=1), stats


def _f_out(ya, yb, yc, yd, bw, w_out, h, g1):
    t = h.shape[0]
    tb = _tblock(t)

    def body(a_ref, b_ref, c_ref, d_ref, bw_ref, w_ref, h_ref, g_ref, h2_ref, o_ref, cat_ref):
        cat, _ = _group_norm([a_ref[...], b_ref[...], c_ref[...], d_ref[...]], bw_ref[...])
        catb = cat.astype(BF16)
        cat_ref[...] = catb
        o = _dot(catb, w_ref[...])
        o_ref[...] = o
        h2_ref[...] = h_ref[...] + g_ref[...] * o

    yblk = pl.BlockSpec((tb, GW), lambda i: (i, 0))
    blk = pl.BlockSpec((tb, D), lambda i: (i, 0))
    return pl.pallas_call(
        body, name="f_out", grid=(t // tb,), in_specs=[yblk] * 4 + [_row(D), _full((D, D)), blk, _row(D)],
        out_specs=[blk, blk, blk],
        out_shape=[jax.ShapeDtypeStruct((t, D), F32), jax.ShapeDtypeStruct((t, D), F32), jax.ShapeDtypeStruct((t, D), BF16)],
        compiler_params=_cparams(1),
    )(ya, yb, yc, yd, bw, w_out, h, g1)


def _b_out(dh2, ya, yb, yc, yd, bw, w_out, g1):
    t = dh2.shape[0]
    tb = _tblock(t)

    def body(dh_ref, a_ref, b_ref, c_ref, d_ref, bw_ref, w_ref, g_ref, da_ref, db_ref, dc_ref, dd_ref, do_ref, dbw_ref):
        @pl.when(pl.program_id(0) == 0)
        def _():
            dbw_ref[...] = jnp.zeros_like(dbw_ref)

        do = (dh_ref[...] * g_ref[...]).astype(BF16)
        do_ref[...] = do
        dcat = _dot(do, w_ref[...], NT)
        bw_v = bw_ref[...]
        for g, (y_ref, dy_ref) in enumerate(((a_ref, da_ref), (b_ref, db_ref), (c_ref, dc_ref), (d_ref, dd_ref))):
            r, n = _rms(y_ref[...])
            dc = dcat[:, GW * g:GW * (g + 1)]
            dbw_ref[:, GW * g:GW * (g + 1)] += _colsum(dc * n)
            dy_ref[...] = _rms_bwd(r, n, dc * bw_v[:, GW * g:GW * (g + 1)])

    yblk = pl.BlockSpec((tb, GW), lambda i: (i, 0))
    blk = pl.BlockSpec((tb, D), lambda i: (i, 0))
    ysd = jax.ShapeDtypeStruct((t, GW), F32)
    return pl.pallas_call(
        body, name="b_out", grid=(t // tb,), in_specs=[blk] + [yblk] * 4 + [_row(D), _full((D, D)), _row(D)],
        out_specs=[yblk] * 4 + [blk, _row(D)],
        out_shape=[ysd] * 4 + [jax.ShapeDtypeStruct((t, D), BF16), jax.ShapeDtypeStruct((1, D), F32)],
        compiler_params=_cparams(1),
    )(dh2, ya, yb, yc, yd, bw, w_out, g1)


HB = 1024


def _f_mlp(h2, nw, sc, sh, g2, w1, w2):
    t = h2.shape[0]
    tb = _tblock(t)
    nk = HID // HB

    def body(h_ref, nw_ref, sc_ref, sh_ref, g_ref, w1_ref, w2_ref, h3_ref, m_ref, a_ref, v_ref):
        k = pl.program_id(1)

        @pl.when(k == 0)
        def _():
            _, n = _rms(h_ref[...])
            v_ref[...] = ((n * nw_ref[...]) * (1.0 + sc_ref[...]) + sh_ref[...]).astype(BF16)
            m_ref[...] = jnp.zeros_like(m_ref)

        a = _dot(v_ref[...], w1_ref[...])
        a_ref[...] = a
        ra = jnp.maximum(a, 0.0)
        m_ref[...] += _dot((ra * ra).astype(BF16), w2_ref[...])

        @pl.when(k == nk - 1)
        def _():
            h3_ref[...] = h_ref[...] + g_ref[...] * m_ref[...]

    blk = pl.BlockSpec((tb, D), lambda i, k: (i, 0))
    return pl.pallas_call(
        body, name="f_mlp", grid=(t // tb, nk),
        in_specs=[blk, _row(D), _row(D), _row(D), _row(D), pl.BlockSpec((D, HB), lambda i, k: (0, k)),
                  pl.BlockSpec((HB, D), lambda i, k: (k, 0))],
        out_specs=[blk, blk, pl.BlockSpec((tb, HB), lambda i, k: (i, k)), blk],
        out_shape=[jax.ShapeDtypeStruct((t, D), F32), jax.ShapeDtypeStruct((t, D), F32), jax.ShapeDtypeStruct((t, HID), F32),
                   jax.ShapeDtypeStruct((t, D), BF16)],
        compiler_params=_cparams(2),
    )(h2, nw, sc, sh, g2, w1, w2)


def _b_mlp(dh3, a, g2, w1, w2):
    t = dh3.shape[0]
    tb = _tblock(t)
    nk = HID // HB

    def body(dh_ref, a_ref, g_ref, w1_ref, w2_ref, dv_ref, da_ref, act_ref, dm_ref):
        k = pl.program_id(1)
        dm = (dh_ref[...] * g_ref[...]).astype(BF16)

        @pl.when(k == 0)
        def _():
            dm_ref[...] = dm
            dv_ref[...] = jnp.zeros_like(dv_ref)

        ra = jnp.maximum(a_ref[...], 0.0)
        act_ref[...] = (ra * ra).astype(BF16)
        da = (_dot(dm, w2_ref[...], NT) * (2.0 * ra)).astype(BF16)
        da_ref[...] = da
        dv_ref[...] += _dot(da, w1_ref[...], NT)

    blk = pl.BlockSpec((tb, D), lambda i, k: (i, 0))
    hblk = pl.BlockSpec((tb, HB), lambda i, k: (i, k))
    return pl.pallas_call(
        body, name="b_mlp", grid=(t // tb, nk),
        in_specs=[blk, hblk, _row(D), pl.BlockSpec((D, HB), lambda i, k: (0, k)), pl.BlockSpec((HB, D), lambda i, k: (k, 0))],
        out_specs=[blk, hblk, hblk, blk],
        out_shape=[jax.ShapeDtypeStruct((t, D), F32), jax.ShapeDtypeStruct((t, HID), BF16), jax.ShapeDtypeStruct((t, HID), BF16),
                   jax.ShapeDtypeStruct((t, D), BF16)],
        compiler_params=_cparams(2),
    )(dh3, a, g2, w1, w2)


def _b_final(h, tgt, fw):
    t = h.shape[0]
    tb = _tblock(t)

    def body(h_ref, t_ref, w_ref, dh_ref, loss_ref, dfw_ref):
        @pl.when(pl.program_id(0) == 0)
        def _():
            loss_ref[...] = jnp.zeros_like(loss_ref)
            dfw_ref[...] = jnp.zeros_like(dfw_ref)

        r, n = _rms(h_ref[...])
        wv = w_ref[...]
        err = n * wv - t_ref[...]
        loss_ref[...] += jnp.sum(err * err, keepdims=True) * (0.5 / D)
        dy = err * (1.0 / D)
        dfw_ref[...] += _colsum(dy * n)
        dh_ref[...] = _rms_bwd(r, n, dy * wv)

    blk = pl.BlockSpec((tb, D), lambda i: (i, 0))
    return pl.pallas_call(
        body, name="b_final", grid=(t // tb,), in_specs=[blk, blk, _row(D)], out_specs=[blk, _row(1), _row(D)],
        out_shape=[jax.ShapeDtypeStruct((t, D), F32), jax.ShapeDtypeStruct((1, 1), F32), jax.ShapeDtypeStruct((1, D), F32)],
        compiler_params=_cparams(1),
    )(h, tgt, fw)


_EYE16 = None


def _eye(n):
    return jnp.eye(n, dtype=F32)


def _pool_embed(pool_w):
    return jnp.einsum('gcd,gk->gckd', pool_w, _eye(4)).reshape(GW, GW)


def _pool_extract(m):
    return jnp.einsum('gcgd->gcd', m.reshape(4, 64, 4, 64))


def _bmat_embed(bb):
    return jnp.einsum('gph,gk->ghkp', bb, _eye(16)).reshape(GW, S5_P)


def _bmat_extract(m):
    return jnp.einsum('ghgp->gph', m.reshape(16, 16, 16, 64))


def _cmat_embed(cc):
    return jnp.einsum('ghp,gk->kpgh', cc, _eye(16)).reshape(S5_P, GW)


def _cmat_extract(m):
    return jnp.einsum('gpgh->ghp', m.reshape(16, 64, 16, 16))


def _pad_lanes(v, n=DTW):
    return jnp.pad(v.reshape(1, -1), ((0, 0), (0, n - v.shape[-1])))


def _layer_params(p, l, mod):
    q = {}
    q['mod'] = [mod[k:k + 1] for k in range(6)]
    q['nw1'] = p['norm_mix_w'][l:l + 1]
    q['nw2'] = p['norm_mlp_w'][l:l + 1]
    w_in = p['w_in'][l]
    q['w_main'] = jnp.concatenate([w_in[:, :1280], w_in[:, 2052:2308], w_in[:, 1280:2048]], axis=1)
    q['w_dt'] = jnp.pad(w_in[:, 2048:2052], ((0, 0), (0, DTW - 4)))
    q['pool_mat'] = _pool_embed(p['pool_w'][l]).astype(BF16)
    q['pool_scale'] = p['pool_scale'][l:l + 1]
    q['sconv_w'] = p['sconv_w'][l]
    q['conv_w'] = p['ssd_conv_w'][l]
    q['conv_b'] = p['ssd_conv_b'][l:l + 1]
    q['dt_bias'] = _pad_lanes(p['ssd_dt_bias'][l])
    q['a_log'] = _pad_lanes(p['ssd_a_log'][l])
    q['ssd_d'] = _pad_lanes(p['ssd_d'][l])
    q['s5_raw'] = (p['s5_a_re'][l], p['s5_a_im'][l], p['s5_log_step'][l].reshape(16, 1),
                   p['s5_b_re'][l].reshape(16, 1024), p['s5_b_im'][l].reshape(16, 1024))
    q['cre'] = _cmat_embed(p['s5_c_re'][l]).astype(BF16)
    q['cim'] = (-_cmat_embed(p['s5_c_im'][l])).astype(BF16)
    q['s5_d'] = p['s5_d'][l:l + 1]
    q['glu_w'] = p['s5_glu_w'][l].astype(BF16)
    q['glu_b'] = p['s5_glu_b'][l:l + 1]
    q['bw'] = p['branch_norm_w'][l:l + 1]
    q['w_out'] = p['w_out'][l]
    q['w1'] = p['mlp_w1'][l]
    q['w2'] = p['mlp_w2'][l]
    return q


def _layer_fwd(h, q):
    sh1, sc1, g1, sh2, sc2, g2 = q['mod']
    t = h.shape[0]
    s = {'h': h}
    s['proj'], s['dtp'], s['u'] = _f_in(h, q['nw1'], sc1, sh1, q['w_main'], q['w_dt'])
    s['ya'], s['yb'] = _f_ab(s['proj'], q['pool_mat'], q['pool_scale'], q['sconv_w'])
    s['yc'], s['ypre'], s['sprev'] = _f_ssd(s['proj'], s['dtp'], q['conv_w'], q['conv_b'], q['dt_bias'], q['a_log'], q['ssd_d'])
    lr, li, bbr, bbi, ars, ais = _s5_prep(*q['s5_raw'])
    s['bmat'] = jnp.concatenate([_bmat_embed(bbr.reshape(16, 64, 16)), _bmat_embed(bbi.reshape(16, 64, 16))],
                                axis=1).astype(BF16)
    s['tables'] = _s5_tables(ars.reshape(1, S5_P), ais.reshape(1, S5_P))
    s['yd'], s['carries'], s['states'] = _f_s5(s['proj'], s['bmat'], q['cre'], q['cim'], s['tables'][0], s['tables'][1],
                                  q['s5_d'], q['glu_w'], q['glu_b'])
    s['h2'], s['o'], s['cat'] = _f_out(s['ya'], s['yb'], s['yc'], s['yd'], q['bw'], q['w_out'], h, g1)
    h3, s['m'], s['a'], s['v'] = _f_mlp(s['h2'], q['nw2'], sc2, sh2, g2, q['w1'], q['w2'])
    return h3, s


def _layer_bwd(dh3, q, s):
    sh1, sc1, g1, sh2, sc2, g2 = q['mod']
    g = {}
    dv, da, act, dm = _b_mlp(dh3, s['a'], g2, q['w1'], q['w2'])
    g['mlp_w1'] = _tn_matmul(s['v'], da, "dw1", col_major=True)
    g['mlp_w2'] = _tn_matmul(act, dm, "dw2")
    dh2, dsc2, dsh2, dnw2, dg2 = _b_normmod(dv, s['h2'], dh3, s['m'], q['nw2'], sc2, "b_norm_mlp")
    dya, dyb, dyc, dyd, do, dbw = _b_out(dh2, s['ya'], s['yb'], s['yc'], s['yd'], q['bw'], q['w_out'], g1)
    g['w_out'] = _tn_matmul(s['cat'], do, "dwout")
    g['branch_norm_w'] = dbw[0]
    dab, dpm, dps, dsw = _b_ab(s['proj'], dya, dyb, q['pool_mat'], q['pool_scale'], q['sconv_w'])
    g['pool_w'] = _pool_extract(dpm)
    g['pool_scale'] = dps[0]
    g['sconv_w'] = dsw
    dz, dxbc, ddt, dcw, dcb, ddtb, dal, ddk = _b_ssd(s['proj'], s['dtp'], s['ypre'], dyc, s['sprev'], q['conv_w'],
                                                     q['conv_b'], q['dt_bias'], q['a_log'], q['ssd_d'])
    g['ssd_conv_w'] = dcw
    g['ssd_conv_b'] = dcb[0]
    g['ssd_dt_bias'] = ddtb[0, :4]
    g['ssd_a_log'] = dal[0, :4]
    g['ssd_d'] = ddk[0, :4]
    tb = s['tables']
    ds5, dbmat, dcre, dcim, dlam, dd5, dgw, dgb = _b_s5(s['proj'], dyd, s['carries'], s['states'], s['bmat'], q['cre'], q['cim'],
                                                        tb[0], tb[1], q['s5_d'], q['glu_w'], q['glu_b'])
    g['s5_c_re'] = _cmat_extract(dcre)
    g['s5_c_im'] = -_cmat_extract(dcim)
    g['s5_d'] = dd5[0]
    g['s5_glu_w'] = dgw
    g['s5_glu_b'] = dgb[0]
    dbbr = _bmat_extract(dbmat[:, :S5_P]).reshape(16, 1024)
    dbbi = _bmat_extract(dbmat[:, S5_P:]).reshape(16, 1024)
    dar, dai, dls, dbr, dbi = _s5_prep_bwd(*q['s5_raw'], dlam[0].reshape(16, 64), dlam[1].reshape(16, 64), dbbr, dbbi)
    g['s5_a_re'], g['s5_a_im'], g['s5_log_step'] = dar, dai, dls[:, 0]
    g['s5_b_re'], g['s5_b_im'] = dbr.reshape(16, 64, 16), dbi.reshape(16, 64, 16)
    du = _b_in_du(dab, dz, dxbc, ds5, ddt, q['w_main'], q['w_dt'])
    u = s['u']
    pieces = [_tn_matmul(u, dab, "dwin_ab"), _tn_matmul(u, dz, "dwin_z"), _tn_matmul(u, dxbc, "dwin_xbc"),
              _tn_matmul(u, ddt, "dwin_dt")[:, :4], _tn_matmul(u, ds5, "dwin_s5")]
    g['w_in'] = jnp.concatenate(pieces, axis=1)
    dh, dsc1, dsh1, dnw1, dg1 = _b_normmod(du, s['h'], dh2, s['o'], q['nw1'], sc1, "b_norm_mix")
    g['norm_mix_w'] = dnw1[0]
    g['norm_mlp_w'] = dnw2[0]
    dmod = jnp.concatenate([dsh1, dsc1, dg1, dsh2, dsc2, dg2], axis=1)
    return dh, g, dmod


def _local_step(x, tgt, p, mod):
    qs = [_layer_params(p, l, mod[l]) for l in range(2)]
    h = x
    saved = []
    for l in range(2):
        h, s = _layer_fwd(h, qs[l])
        saved.append(s)
    dh, loss, dfw = _b_final(h, tgt, p['final_norm_w'].reshape(1, D))
    grads = [None, None]
    dmods = [None, None]
    for l in (1, 0):
        dh, grads[l], dmods[l] = _layer_bwd(dh, qs[l], saved[l])
    out = {k: jnp.stack([grads[0][k], grads[1][k]]) for k in grads[0]}
    out['final_norm_w'] = dfw[0]
    return loss, dh, out, jnp.concatenate(dmods, axis=0)


def _pack(arrs):
    parts, rows = [], 0
    for a in arrs:
        f = a.reshape(-1).astype(F32)
        pad = (-f.shape[0]) % 1024
        f = jnp.pad(f, (0, pad)) if pad else f
        parts.append(f.reshape(-1, 128))
        rows += parts[-1].shape[0]
    if rows % 256:
        parts.append(jnp.zeros((256 - rows % 256, 128), F32))
    return jnp.concatenate(parts, axis=0)


def _unpack(buf, shapes):
    out, row = [], 0
    for shp in shapes:
        n = int(math.prod(shp)) if len(shp) else 1
        rows = (n + 1023) // 1024 * 8
        out.append(buf[row:row + rows].reshape(-1)[:n].reshape(shp))
        row += rows
    return out


def _shard_of(a, axis, k):
    n = a.shape[axis] // 4
    return lax.dynamic_slice_in_dim(a, k * n, n, axis)


def kernel(x, c, norm_mix_w, norm_mlp_w, ada_w, ada_b, w_in, pool_w, pool_scale, sconv_w, ssd_conv_w, ssd_conv_b, ssd_dt_bias, ssd_a_log, ssd_d, s5_a_re, s5_a_im, s5_log_step, s5_b_re, s5_b_im, s5_c_re, s5_c_im, s5_d, s5_glu_w, s5_glu_b, branch_norm_w, w_out, mlp_w1, mlp_w2, final_norm_w, loss_target, m_norm_mix_w, m_norm_mlp_w, m_ada_w, m_ada_b, m_w_in, m_pool_w, m_pool_scale, m_sconv_w, m_ssd_conv_w, m_ssd_conv_b, m_ssd_dt_bias, m_ssd_a_log, m_ssd_d, m_s5_a_re, m_s5_a_im, m_s5_log_step, m_s5_b_re, m_s5_b_im, m_s5_c_re, m_s5_c_im, m_s5_d, m_s5_glu_w, m_s5_glu_b, m_branch_norm_w, m_w_out, m_mlp_w1, m_mlp_w2, m_final_norm_w, v_norm_mix_w, v_norm_mlp_w, v_ada_w, v_ada_b, v_w_in, v_pool_w, v_pool_scale, v_sconv_w, v_ssd_conv_w, v_ssd_conv_b, v_ssd_dt_bias, v_ssd_a_log, v_ssd_d, v_s5_a_re, v_s5_a_im, v_s5_log_step, v_s5_b_re, v_s5_b_im, v_s5_c_re, v_s5_c_im, v_s5_d, v_s5_glu_w, v_s5_glu_b, v_branch_norm_w, v_w_out, v_mlp_w1, v_mlp_w2, v_final_norm_w):
    loc = locals()
    w = {n: loc[n] for n in WEIGHTS}
    mom = {n: loc['m_' + n] for n in WEIGHTS}
    var = {n: loc['v_' + n] for n in WEIGHTS}
    ix, iy, ic = lax.axis_index("x"), lax.axis_index("y"), lax.axis_index("c")
    chip = 2 * ix + iy
    dev = 4 * ix + 2 * iy + ic

    (c_all,) = _exchange([c], EVERYONE, False, "ag_cond", stage=True)
    c_all = c_all.reshape(8, D)
    small_sh = _exchange([w[n] for n in SMALL_SHARDED], CHIPS, False, "ag_small")
    mine_of = lambda a: lax.dynamic_index_in_dim(a.astype(BF16), ic, axis=0, keepdims=False)
    pad_in = lambda a: jnp.pad(a.reshape(577, D), ((0, WIN_ROWS - 577), (0, 0)))
    big_l = _exchange([pad_in(mine_of(w['w_in'])), mine_of(w['w_out']), mine_of(w['mlp_w1']), mine_of(w['mlp_w2'])],
                      CHIPS, False, "ag_big")
    big_o = _pair_swap([a.reshape(-1, D) for a in big_l], False, "swap_big")
    big_sh = [jnp.stack([jnp.where(ic == l, a, b.reshape(a.shape)) for l in range(2)]) for a, b in zip(big_l, big_o)]
    p = dict(w)
    for n, g in zip(SMALL_SHARDED, small_sh):
        ax = SMALL_SHARDED[n]
        p[n] = jnp.concatenate([g[k] for k in range(4)], axis=ax)
    w_in_sh = big_sh[0][:, :, :577].reshape(2, 4, D, 577)
    p['w_in'] = jnp.concatenate([w_in_sh[:, k] for k in range(4)], axis=2)
    p['w_out'] = big_sh[1].reshape(2, D, D)
    p['mlp_w1'] = jnp.concatenate([big_sh[2][:, k] for k in range(4)], axis=2)
    p['mlp_w2'] = big_sh[3].reshape(2, HID, D)

    ada_b_sh = _shard_of(w['ada_b'], 1, chip).reshape(2, 1, 6 * D // 4)
    mod_sh = _ada_fwd(c_all, w['ada_w'], ada_b_sh)
    (mod_all,) = _exchange([mod_sh], CHIPS, False, "ag_mod", stage=True)
    mine = lax.dynamic_index_in_dim(mod_all, dev, axis=2, keepdims=False)
    mod = jnp.transpose(mine, (1, 0, 2)).reshape(2, 6, D)

    loss, grad_x, g, dmod = _local_step(x[0], loss_target[0], p, mod)

    (dmod_all,) = _exchange([dmod], EVERYONE, False, "ag_dmod", stage=True)
    dmod_all = jnp.transpose(dmod_all, (1, 0, 2))
    g_ada_w, g_ada_b = _ada_bwd(c_all, _shard_of(dmod_all, 2, chip), dmod_all)

    gw_in = jnp.transpose(g['w_in'].reshape(2, D, 4, 577), (0, 2, 1, 3)).reshape(2, 4, 577, D)
    gw_in = jnp.pad(gw_in, ((0, 0), (0, 0), (0, WIN_ROWS - 577), (0, 0)))
    gw_out = g['w_out'].reshape(2, 4, 256, D)
    gw1 = g['mlp_w1']
    gw2 = g['mlp_w2'].reshape(2, 4, 1024, D)
    gws = [gw_in, gw_out, gw1, gw2]
    got = _pair_swap([a.reshape(2, -1, D) for a in gws], True, "swap_grad")
    layer = ic.astype(jnp.int32).reshape(1)
    pair = [_pair_sum(a, b.reshape(a.shape[1:]), layer, "pair_sum%d" % k, BF16) for k, (a, b) in enumerate(zip(gws, got))]
    quad = _exchange(pair, CHIPS, True, "rs_chips")
    quad = [_sum_lead(a, "rs_chip_sum%d" % k, F32) for k, a in enumerate(quad)]
    other = _pair_swap(quad, False, "swap_red")
    both = [jnp.stack([jnp.where(ic == l, a, b) for l in range(2)]) for a, b in zip(quad, other)]
    both[0] = both[0][:, :577].reshape(2, D, 577)
    red = dict(zip(('w_in', 'w_out', 'mlp_w1', 'mlp_w2'), both))
    red['ada_w'] = g_ada_w

    small_names = [n for n in WEIGHTS if n not in BIG and n != 'ada_b']
    small_shapes = [g[n].shape for n in small_names] + [(1, 1)]
    packed = _pack([g[n] for n in small_names] + [loss])
    (packed_all,) = _exchange([packed], EVERYONE, False, "ag_smallgrad", stage=True)
    summed = _unpack(_sum_lead(packed_all, "smallgrad_sum", F32), small_shapes)
    for n, a in zip(small_names, summed[:-1]):
        red[n] = _shard_of(a, SMALL_SHARDED[n], chip) if n in SMALL_SHARDED else a
    red['ada_b'] = g_ada_b
    loss_out = summed[-1].reshape(())

    delta, new_m, new_v = {}, {}, {}
    for n in BIG:
        delta[n], new_m[n], new_v[n] = _adamw(w[n], red[n], mom[n], var[n], "adamw_" + n)
    rest = [n for n in WEIGHTS if n not in BIG]
    shapes = [w[n].shape for n in rest]
    d_p, m_p, v_p = _adamw(_pack([w[n] for n in rest]), _pack([red[n] for n in rest]), _pack([mom[n] for n in rest]),
                           _pack([var[n] for n in rest]), "adamw_small")
    for n, a, b, cc in zip(rest, _unpack(d_p, shapes), _unpack(m_p, shapes), _unpack(v_p, shapes)):
        delta[n], new_m[n], new_v[n] = a, b, cc

    return (loss_out, grad_x[None], *[red[n] for n in WEIGHTS], *[delta[n] for n in WEIGHTS],
            *[new_m[n] for n in WEIGHTS], *[new_v[n] for n in WEIGHTS])
```

```python
import functools
import math

import jax
import jax.numpy as jnp
from jax import lax
from jax.experimental import pallas as pl
from jax.experimental.pallas import tpu as pltpu

F32 = jnp.float32
BF16 = jnp.bfloat16
HI = lax.Precision.HIGHEST

D = 1024
GW = 256
HID = 4096
EPS = 1e-6
PW = 2304
DTW = 128
SSD_L = 128
SSD_SUB = 2
SSD_SUB_BWD = 1
NH, HP, NS = 4, 64, 128
S5_P = 1024
MESH = pl.DeviceIdType.MESH

ADAM_LR, ADAM_B1, ADAM_B2, ADAM_EPS, ADAM_WD, ADAM_STEP = 0.001, 0.9, 0.999, 1e-08, 0.01, 10

NT = (((1,), (1,)), ((), ()))
TN = (((0,), (0,)), ((), ()))

WEIGHTS = ['norm_mix_w', 'norm_mlp_w', 'ada_w', 'ada_b', 'w_in', 'pool_w', 'pool_scale', 'sconv_w', 'ssd_conv_w',
           'ssd_conv_b', 'ssd_dt_bias', 'ssd_a_log', 'ssd_d', 's5_a_re', 's5_a_im', 's5_log_step', 's5_b_re', 's5_b_im',
           's5_c_re', 's5_c_im', 's5_d', 's5_glu_w', 's5_glu_b', 'branch_norm_w', 'w_out', 'mlp_w1', 'mlp_w2',
           'final_norm_w']
BIG = ('ada_w', 'w_in', 'w_out', 'mlp_w1', 'mlp_w2')
SMALL_SHARDED = {'sconv_w': 2, 'ssd_conv_w': 2, 's5_glu_w': 1}


def _cparams(n_axes, vmem_mb=48):
    return pltpu.CompilerParams(dimension_semantics=("arbitrary",) * n_axes, vmem_limit_bytes=vmem_mb * 1024 * 1024)


def _row(n):
    return pl.BlockSpec((1, n), lambda *_: (0, 0))


def _full(shape):
    nd = len(shape)
    return pl.BlockSpec(tuple(shape), lambda *_: (0,) * nd)


def _dot(a, b, dims=None, prec=None):
    if dims is None:
        dims = (((a.ndim - 1,), (0,)), ((), ()))
    return lax.dot_general(a, b, dims, preferred_element_type=F32, precision=prec)


def _bdot(a, b, dims=None):
    return _dot(a.astype(BF16), b.astype(BF16), dims)


def _sig(x):
    return jax.nn.sigmoid(x)


def _silu(x):
    return x * _sig(x)


def _dsilu(x):
    s = _sig(x)
    return s * (1.0 + x * (1.0 - s))


def _softplus(x):
    return jnp.maximum(x, 0.0) + jnp.log(1.0 + jnp.exp(-jnp.abs(x)))


_GK = math.sqrt(2.0 / math.pi)


def _gelu(x):
    return 0.5 * x * (1.0 + jnp.tanh(_GK * (x + 0.044715 * x * x * x)))


def _dgelu(x):
    th = jnp.tanh(_GK * (x + 0.044715 * x * x * x))
    return 0.5 * (1.0 + th) + 0.5 * x * (1.0 - th * th) * _GK * (1.0 + 3.0 * 0.044715 * x * x)


def _colsum(x):
    return jnp.sum(x, axis=0, keepdims=True)


def _rms(x):
    r = lax.rsqrt(jnp.mean(x * x, axis=-1, keepdims=True) + EPS)
    return r, x * r


def _rms_bwd(r, n, dn):
    return r * (dn - n * jnp.mean(dn * n, axis=-1, keepdims=True))


def _roll(x, k):
    n = x.shape[0]
    k = k % n
    return x if k == 0 else pltpu.roll(x, k, axis=0)


def _tblock(t, want=512):
    return min(t, want)


def _peer(mask):
    x, y, c = lax.axis_index("x"), lax.axis_index("y"), lax.axis_index("c")
    return (x ^ ((mask >> 2) & 1), y ^ ((mask >> 1) & 1), c ^ (mask & 1))


def _group_index(masks):
    x, y, c = lax.axis_index("x"), lax.axis_index("y"), lax.axis_index("c")
    full = 0
    for m in masks:
        full |= m
    bits = [b for b in (4, 2, 1) if full & b]

    def idx(px, py, pc):
        v = {4: px, 2: py, 1: pc}
        out = 0
        for b in bits:
            out = out * 2 + v[b]
        return out

    return idx(x, y, c), [idx(*_peer(m)) for m in masks]


def _exchange(arrs, masks, scatter, name, stage=False):
    n_arr, n_peer, n_grp = len(arrs), len(masks), len(masks) + 1

    def body(*refs):
        ins, outs = refs[:n_arr], refs[n_arr:2 * n_arr]
        send_sems, recv_sems, local_sems = refs[2 * n_arr:]
        me, peer_idx = _group_index(masks)
        copies = []
        for t in range(n_arr):
            src_me = ins[t].at[me] if scatter else ins[t]
            loc = pltpu.make_async_copy(src_me, outs[t].at[me], local_sems.at[t])
            loc.start()
            copies.append(loc)
            for j, m in enumerate(masks):
                src = ins[t].at[peer_idx[j]] if scatter else ins[t]
                cp = pltpu.make_async_remote_copy(src_ref=src, dst_ref=outs[t].at[me], send_sem=send_sems.at[t, j],
                                                  recv_sem=recv_sems.at[t, j], device_id=_peer(m), device_id_type=MESH)
                cp.start()
                copies.append(cp)
        for cp in copies:
            cp.wait()

    hbm = pl.BlockSpec(memory_space=pl.ANY)
    out_shape = [jax.ShapeDtypeStruct((n_grp,) + (a.shape[1:] if scatter else a.shape), a.dtype) for a in arrs]
    src_spec = pl.BlockSpec(memory_space=pltpu.VMEM) if stage else hbm
    outs = pl.pallas_call(
        body, name=name, in_specs=[src_spec] * n_arr, out_specs=[hbm] * n_arr, out_shape=out_shape,
        scratch_shapes=[pltpu.SemaphoreType.DMA((n_arr, n_peer)), pltpu.SemaphoreType.DMA((n_arr, n_peer)),
                        pltpu.SemaphoreType.DMA((n_arr,))],
    )(*arrs)
    return list(outs)


def _gather_copies(src_ref, land_ref, send_sems, recv_sems):
    me, _ = _group_index(CHIPS)
    return [pltpu.make_async_remote_copy(src_ref=src_ref, dst_ref=land_ref.at[me], send_sem=send_sems[j], recv_sem=recv_sems[j],
                                         device_id=_peer(m), device_id_type=MESH) for j, m in enumerate(CHIPS)]


def _gather_start(src, name):
    n = len(CHIPS)

    def body(src_ref, land_ref, *rest):
        sems, token = rest[:2 * n], rest[-1]
        for cp in _gather_copies(src_ref, land_ref, sems[:n], sems[n:]):
            cp.start()
        token[...] = jnp.zeros_like(token)

    hbm = pl.BlockSpec(memory_space=pltpu.HBM)
    sem = pl.BlockSpec(memory_space=pltpu.SEMAPHORE)
    land = lax.empty((n + 1,) + src.shape, src.dtype)
    outs = pl.pallas_call(
        body, name=name,
        out_shape=(pltpu.SemaphoreType.DMA(()),) * (2 * n) + (pltpu.HBM(src.shape, src.dtype), pltpu.HBM(land.shape, land.dtype),
                                                              jax.ShapeDtypeStruct((8, 128), F32)),
        in_specs=(hbm, hbm), out_specs=(sem,) * (2 * n) + (hbm, hbm, pl.BlockSpec(memory_space=pltpu.VMEM)),
        input_output_aliases={0: 2 * n, 1: 2 * n + 1},
        compiler_params=pltpu.CompilerParams(has_side_effects=pltpu.SideEffectType.DATAFLOW_SIDE_EFFECTING),
    )(pltpu.with_memory_space_constraint(src, pltpu.HBM), pltpu.with_memory_space_constraint(land, pltpu.HBM))
    return outs[:2 * n], outs[2 * n], outs[2 * n + 1], outs[2 * n + 2]


def _gather_wait(sems, src, land, after, name):
    n = len(CHIPS)

    def body(src_ref, land_ref, *rest):
        for cp in _gather_copies(src_ref, land_ref, rest[:n], rest[n:2 * n]):
            cp.wait_send()
            cp.wait_recv()

    hbm = pl.BlockSpec(memory_space=pltpu.HBM)
    sem = pl.BlockSpec(memory_space=pltpu.SEMAPHORE)
    return pl.pallas_call(
        body, name=name, out_shape=(pltpu.HBM(src.shape, src.dtype), pltpu.HBM(land.shape, land.dtype)),
        in_specs=(hbm, hbm) + (sem,) * (2 * n) + (pl.BlockSpec(memory_space=pl.ANY),) * len(after), out_specs=(hbm, hbm),
        input_output_aliases={0: 0, 1: 1},
        compiler_params=pltpu.CompilerParams(has_side_effects=pltpu.SideEffectType.DATAFLOW_SIDE_EFFECTING),
    )(src, land, *sems, *after)[1]


CHIPS = (4, 2, 6)
EVERYONE = (1, 2, 3, 4, 5, 6, 7)
SIBLING = (1,)
SWAP_ROWS = 512
WIN_ROWS = 592


def _pair_swap(arrs, other_layer, name):
    n_arr = len(arrs)
    shapes = [a.shape[-2:] for a in arrs]
    chunks = []
    for t, (rows, _) in enumerate(shapes):
        assert rows % 16 == 0
        for j, r0 in enumerate(range(0, rows, SWAP_ROWS)):
            chunks.append((t, r0, min(SWAP_ROWS, rows - r0), j % 2))

    def body(*refs):
        ins, outs = refs[:n_arr], refs[n_arr:2 * n_arr]
        bufs = refs[2 * n_arr:3 * n_arr]
        load_sems, send_sems, recv_sems = refs[3 * n_arr:]
        sibling = _peer(1)
        c = lax.axis_index("c")

        def load(k):
            t, r0, n, slot = chunks[k]
            src = ins[t].at[1 - c] if other_layer else ins[t]
            return pltpu.make_async_copy(src.at[pl.ds(r0, n)], bufs[t].at[slot, pl.ds(0, n)], load_sems.at[t, slot])

        def send(k):
            t, r0, n, slot = chunks[k]
            return pltpu.make_async_remote_copy(src_ref=bufs[t].at[slot, pl.ds(0, n)], dst_ref=outs[t].at[pl.ds(r0, n)],
                                                send_sem=send_sems.at[t, slot], recv_sem=recv_sems.at[t],
                                                device_id=sibling, device_id_type=MESH)

        in_flight = {}

        def start_load(k):
            key = (chunks[k][0], chunks[k][3])
            if key in in_flight:
                send(in_flight.pop(key)).wait_send()
            load(k).start()

        start_load(0)
        for k in range(len(chunks)):
            load(k).wait()
            if k + 1 < len(chunks):
                start_load(k + 1)
            send(k).start()
            in_flight[(chunks[k][0], chunks[k][3])] = k
        for k in in_flight.values():
            send(k).wait_send()
        for t in range(n_arr):
            pltpu.make_async_remote_copy(src_ref=outs[t], dst_ref=outs[t], send_sem=send_sems.at[t, 0],
                                         recv_sem=recv_sems.at[t], device_id=sibling, device_id_type=MESH).wait_recv()

    hbm = pl.BlockSpec(memory_space=pl.ANY)
    outs = pl.pallas_call(
        body, name=name, in_specs=[hbm] * n_arr, out_specs=[hbm] * n_arr,
        out_shape=[jax.ShapeDtypeStruct(s, a.dtype) for s, a in zip(shapes, arrs)],
        scratch_shapes=[pltpu.VMEM((2, min(SWAP_ROWS, s[0]), s[1]), a.dtype) for s, a in zip(shapes, arrs)]
        + [pltpu.SemaphoreType.DMA((n_arr, 2)), pltpu.SemaphoreType.DMA((n_arr, 2)), pltpu.SemaphoreType.DMA((n_arr,))],
        compiler_params=pltpu.CompilerParams(vmem_limit_bytes=48 * 1024 * 1024),
    )(*arrs)
    return list(outs)


def _sum_lead(a, name, out_dtype):
    n = a.shape[0]
    shape = a.shape[1:]

    def body(a_ref, o_ref):
        acc = a_ref[0].astype(F32)
        for k in range(1, n):
            acc = acc + a_ref[k].astype(F32)
        o_ref[...] = acc.astype(out_dtype)

    if len(shape) == 3:
        blk = (1,) + shape[1:]
        return pl.pallas_call(
            body, name=name, grid=(shape[0],), in_specs=[pl.BlockSpec((n,) + blk, lambda i: (0, i, 0, 0))],
            out_specs=pl.BlockSpec(blk, lambda i: (i, 0, 0)), out_shape=jax.ShapeDtypeStruct(shape, out_dtype),
            compiler_params=_cparams(1),
        )(a)
    rows, cols = shape
    rb = rows
    for cand in (512, 256, 128):
        if rows % cand == 0 and rows > cand:
            rb = cand
            break
    return pl.pallas_call(
        body, name=name, grid=(rows // rb,), in_specs=[pl.BlockSpec((n, rb, cols), lambda i: (0, i, 0))],
        out_specs=pl.BlockSpec((rb, cols), lambda i: (i, 0)), out_shape=jax.ShapeDtypeStruct((rows, cols), out_dtype),
        compiler_params=_cparams(1),
    )(a)


def _pair_sum(g, recv, layer, name, out_dtype):
    _, n, r, c = g.shape

    def body(l_ref, g_ref, r_ref, o_ref):
        o_ref[...] = (g_ref[0].astype(F32) + r_ref[...].astype(F32)).astype(out_dtype)

    return pl.pallas_call(
        body, name=name,
        grid_spec=pltpu.PrefetchScalarGridSpec(
            num_scalar_prefetch=1, grid=(n,),
            in_specs=[pl.BlockSpec((1, 1, r, c), lambda i, l: (l[0], i, 0, 0)), pl.BlockSpec((1, r, c), lambda i, l: (i, 0, 0))],
            out_specs=pl.BlockSpec((1, r, c), lambda i, l: (i, 0, 0))),
        out_shape=jax.ShapeDtypeStruct((n, r, c), out_dtype), compiler_params=_cparams(1),
    )(layer, g, recv)


def _tn_matmul(a, b, name, col_major=False):
    t, k = a.shape
    n = b.shape[1]
    tb = _tblock(t, 1024)
    kb = min(k, 1024)
    nb = min(n, 1024)
    grid = (k // kb, n // nb, t // tb)

    def body(a_ref, b_ref, o_ref):
        @pl.when(pl.program_id(2) == 0)
        def _():
            o_ref[...] = jnp.zeros_like(o_ref)

        acc = _bdot(a_ref[...], b_ref[...], TN)
        if col_major:
            o_ref[0] += acc
        else:
            o_ref[...] += acc

    if col_major:
        out_spec = pl.BlockSpec((1, kb, nb), lambda ki, ni, ti: (ni, ki, 0))
        out_shape = jax.ShapeDtypeStruct((n // nb, k, nb), F32)
    else:
        out_spec = pl.BlockSpec((kb, nb), lambda ki, ni, ti: (ki, ni))
        out_shape = jax.ShapeDtypeStruct((k, n), F32)
    return pl.pallas_call(
        body, name=name, grid=grid,
        in_specs=[pl.BlockSpec((tb, kb), lambda ki, ni, ti: (ti, ki)), pl.BlockSpec((tb, nb), lambda ki, ni, ti: (ti, ni))],
        out_specs=out_spec, out_shape=out_shape, compiler_params=_cparams(3),
    )(a, b)


def _adamw(w, g, m, v, name):
    shape = w.shape
    cols = shape[-1]
    rows = int(math.prod(shape[:-1]))
    rb = rows
    for cand in (256, 128, 64, 32, 16, 8):
        if rows % cand == 0 and rows > cand:
            rb = cand
            break
    bc1 = 1.0 - ADAM_B1 ** ADAM_STEP
    bc2 = 1.0 - ADAM_B2 ** ADAM_STEP

    def body(w_ref, g_ref, m_ref, v_ref, d_ref, nm_ref, nv_ref):
        gg = g_ref[...]
        m2 = ADAM_B1 * m_ref[...] + (1.0 - ADAM_B1) * gg
        v2 = ADAM_B2 * v_ref[...] + (1.0 - ADAM_B2) * (gg * gg)
        m_hat = m2 / bc1
        v_hat = v2 / bc2
        d_ref[...] = -ADAM_LR * (m_hat / (jnp.sqrt(v_hat) + ADAM_EPS) + ADAM_WD * w_ref[...])
        nm_ref[...] = m2
        nv_ref[...] = v2

    spec = pl.BlockSpec((rb, cols), lambda i: (i, 0))
    sds = jax.ShapeDtypeStruct((rows, cols), F32)
    outs = pl.pallas_call(
        body, name=name, grid=(rows // rb,), in_specs=[spec] * 4, out_specs=[spec] * 3, out_shape=[sds] * 3,
        compiler_params=_cparams(1),
    )(*(z.reshape(rows, cols) for z in (w, g, m, v)))
    return tuple(o.reshape(shape) for o in outs)


def _ada_fwd(c_all, ada_w_sh, ada_b_sh):
    s = ada_w_sh.shape[2]
    sb = 512

    def body(c_ref, w_ref, b_ref, o_ref):
        cond = _silu(c_ref[...])
        o_ref[0] = _bdot(cond, w_ref[0]) + b_ref[0]

    return pl.pallas_call(
        body, name="ada_fwd", grid=(2, s // sb),
        in_specs=[_full((8, D)), pl.BlockSpec((1, D, sb), lambda l, j: (l, 0, j)), pl.BlockSpec((1, 1, sb), lambda l, j: (l, 0, j))],
        out_specs=pl.BlockSpec((1, 8, sb), lambda l, j: (l, 0, j)), out_shape=jax.ShapeDtypeStruct((2, 8, s), F32),
        compiler_params=_cparams(2),
    )(c_all, ada_w_sh, ada_b_sh)


def _ada_bwd(c_all, dmod_sh, dmod_all):
    s = dmod_sh.shape[2]
    sb = 512

    def body(c_ref, d_ref, o_ref):
        cond = _silu(c_ref[...])
        o_ref[0] = _bdot(cond, d_ref[0], TN)

    gw = pl.pallas_call(
        body, name="ada_bwd_w", grid=(2, s // sb),
        in_specs=[_full((8, D)), pl.BlockSpec((1, 8, sb), lambda l, j: (l, 0, j))],
        out_specs=pl.BlockSpec((1, D, sb), lambda l, j: (l, 0, j)), out_shape=jax.ShapeDtypeStruct((2, D, s), F32),
        compiler_params=_cparams(2),
    )(c_all, dmod_sh)

    def body_b(d_ref, o_ref):
        acc = d_ref[0, 0:1, :]
        for k in range(1, 8):
            acc = acc + d_ref[0, k:k + 1, :]
        o_ref[0] = acc

    gb = pl.pallas_call(
        body_b, name="ada_bwd_b", grid=(2,), in_specs=[pl.BlockSpec((1, 8, 6 * D), lambda l: (l, 0, 0))],
        out_specs=pl.BlockSpec((1, 1, 6 * D), lambda l: (l, 0, 0)), out_shape=jax.ShapeDtypeStruct((2, 1, 6 * D), F32),
        compiler_params=_cparams(1),
    )(dmod_all)
    return gw, gb.reshape(2, 6 * D)


def _f_in(h, nw, sc, sh, w_main, w_dt):
    t = h.shape[0]
    tb = _tblock(t)

    def body(h_ref, nw_ref, sc_ref, sh_ref, w_ref, wd_ref, p_ref, dt_ref, u_ref):
        _, n = _rms(h_ref[...])
        u = ((n * nw_ref[...]) * (1.0 + sc_ref[...]) + sh_ref[...]).astype(BF16)
        u_ref[...] = u
        p_ref[...] = _dot(u, w_ref[...])
        dt_ref[...] = _dot(u, wd_ref[...])

    return pl.pallas_call(
        body, name="f_in", grid=(t // tb,),
        in_specs=[pl.BlockSpec((tb, D), lambda i: (i, 0)), _row(D), _row(D), _row(D), _full((D, PW)), _full((D, DTW))],
        out_specs=[pl.BlockSpec((tb, PW), lambda i: (i, 0)), pl.BlockSpec((tb, DTW), lambda i: (i, 0)),
                   pl.BlockSpec((tb, D), lambda i: (i, 0))],
        out_shape=[jax.ShapeDtypeStruct((t, PW), F32), jax.ShapeDtypeStruct((t, DTW), F32), jax.ShapeDtypeStruct((t, D), BF16)],
        compiler_params=_cparams(1),
    )(h, nw, sc, sh, w_main, w_dt)


def _b_in_du(dab, dz, dxbc, ds5, ddt, w_main, w_dt):
    t = dab.shape[0]
    tb = _tblock(t)

    def body(a_ref, z_ref, x_ref, s_ref, d_ref, w_ref, wd_ref, o_ref):
        acc = _bdot(a_ref[...], w_ref[:, 0:1024], NT)
        acc += _bdot(z_ref[...], w_ref[:, 1024:1280], NT)
        acc += _bdot(s_ref[...], w_ref[:, 1280:1536], NT)
        acc += _bdot(x_ref[...], w_ref[:, 1536:2304], NT)
        acc += _bdot(d_ref[...], wd_ref[...], NT)
        o_ref[...] = acc

    blk = lambda n: pl.BlockSpec((tb, n), lambda i: (i, 0))
    return pl.pallas_call(
        body, name="b_in_du", grid=(t // tb,),
        in_specs=[blk(1024), blk(256), blk(768), blk(256), blk(DTW), _full((D, PW)), _full((D, DTW))],
        out_specs=blk(D), out_shape=jax.ShapeDtypeStruct((t, D), F32), compiler_params=_cparams(1),
    )(dab, dz, dxbc, ds5, ddt, w_main, w_dt)


def _b_normmod(du, x, dres, gated, nw, sc, name):
    t = x.shape[0]
    tb = _tblock(t)

    def body(du_ref, x_ref, dr_ref, g_ref, nw_ref, sc_ref, dx_ref, dsc_ref, dsh_ref, dnw_ref, dg_ref):
        @pl.when(pl.program_id(0) == 0)
        def _():
            for r in (dsc_ref, dsh_ref, dnw_ref, dg_ref):
                r[...] = jnp.zeros_like(r)

        du_v = du_ref[...]
        r, n = _rms(x_ref[...])
        nwv = nw_ref[...]
        scale = 1.0 + sc_ref[...]
        dsc_ref[...] += _colsum(du_v * (n * nwv))
        dsh_ref[...] += _colsum(du_v)
        dnw_ref[...] += _colsum(du_v * scale * n)
        dres_v = dr_ref[...]
        dg_ref[...] += _colsum(dres_v * g_ref[...])
        dx_ref[...] = dres_v + _rms_bwd(r, n, du_v * scale * nwv)

    blk = pl.BlockSpec((tb, D), lambda i: (i, 0))
    row = jax.ShapeDtypeStruct((1, D), F32)
    return pl.pallas_call(
        body, name=name, grid=(t // tb,), in_specs=[blk, blk, blk, blk, _row(D), _row(D)],
        out_specs=[blk, _row(D), _row(D), _row(D), _row(D)], out_shape=[jax.ShapeDtypeStruct((t, D), F32), row, row, row, row],
        compiler_params=_cparams(1),
    )(du, x, dres, gated, nw, sc)


HALO = 16


def _lane_group(shape):
    return lax.broadcasted_iota(jnp.int32, shape, 1) // 64


def _window_select(g, s2, s4, s8, s16):
    return jnp.where(g == 0, s2, jnp.where(g == 1, s4, jnp.where(g == 2, s8, s16)))


def _pool_count(t0, rows):
    g = _lane_group((rows, GW))
    win = _window_select(g, 2, 4, 8, 16)
    tt = t0 + lax.broadcasted_iota(jnp.int32, (rows, GW), 0)
    return jnp.minimum(tt + 1, win).astype(F32)


def _pool_p(v_ext, t0, tb):
    s2 = v_ext + _roll(v_ext, 1)
    s4 = s2 + _roll(s2, 2)
    s8 = s4 + _roll(s4, 4)
    s16 = s8 + _roll(s8, 8)
    ws = _window_select(_lane_group(v_ext.shape), s2, s4, s8, s16)[HALO:]
    return ws / _pool_count(t0, tb) - v_ext[HALO:]


def _sconv(q_ext, w):
    return (_roll(q_ext, 2) * w[0:1] + _roll(q_ext, 1) * w[1:2] + q_ext * w[2:3])[HALO:]


def _halo_specs(t, tb, cols, col_block):
    per = tb // HALO
    last = t // HALO - 1
    prev = pl.BlockSpec((HALO, cols), lambda i: (jnp.maximum(i * per - 1, 0), col_block))
    nxt = pl.BlockSpec((HALO, cols), lambda i: (jnp.minimum((i + 1) * per, last), col_block))
    return prev, nxt


def _f_ab(proj, pool_mat, pool_scale, sconv_w):
    t = proj.shape[0]
    tb = _tblock(t)
    prev, _ = _halo_specs(t, tb, 1024, 0)

    def body(p_ref, h_ref, pm_ref, ps_ref, sw_ref, ya_ref, yb_ref):
        i = pl.program_id(0)
        halo = jnp.where(i > 0, h_ref[...], 0.0)
        ext = jnp.concatenate([halo, p_ref[...]], axis=0)
        p = _pool_p(ext[:, 0:256], i * tb, tb)
        ya_ref[...] = _bdot(p, pm_ref[...]) * ps_ref[...]
        q_ext = ext[:, 512:768] * ext[:, 768:1024]
        yb_ref[...] = p_ref[:, 256:512] * _sconv(q_ext, sw_ref[...])

    blk = pl.BlockSpec((tb, GW), lambda i: (i, 0))
    sds = jax.ShapeDtypeStruct((t, GW), F32)
    return pl.pallas_call(
        body, name="f_ab", grid=(t // tb,),
        in_specs=[pl.BlockSpec((tb, 1024), lambda i: (i, 0)), prev, _full((GW, GW)), _row(GW), _full((3, GW))],
        out_specs=[blk, blk], out_shape=[sds, sds], compiler_params=_cparams(1),
    )(proj, proj, pool_mat, pool_scale, sconv_w)


def _b_ab(proj, dya, dyb, pool_mat, pool_scale, sconv_w):
    t = proj.shape[0]
    tb = _tblock(t)
    nb = t // tb
    prev, nxt = _halo_specs(t, tb, 1024, 0)
    _, nxt_g = _halo_specs(t, tb, GW, 0)
    n_ext = tb + HALO

    def body(p_ref, hp_ref, hn_ref, da_ref, dan_ref, db_ref, dbn_ref, pm_ref, ps_ref, sw_ref,
             o_ref, dpm_ref, dps_ref, dsw_ref):
        i = pl.program_id(0)

        @pl.when(i == 0)
        def _():
            for r in (dpm_ref, dps_ref, dsw_ref):
                r[...] = jnp.zeros_like(r)

        last = i == nb - 1
        halo = jnp.where(i > 0, hp_ref[...], 0.0)
        main = p_ref[...]
        ext = jnp.concatenate([halo, main], axis=0)
        scale = ps_ref[...]
        pm = pm_ref[...]
        p = _pool_p(ext[:, 0:256], i * tb, tb)
        da = da_ref[...]
        dps_ref[...] += _colsum(da * _bdot(p, pm))
        da_ext = jnp.concatenate([da, jnp.where(last, 0.0, dan_ref[...])], axis=0)
        dys = da_ext * scale
        dpm_ref[...] += _bdot(p, dys[:tb], TN)
        dp = _bdot(dys, pm, NT)
        dpc = dp / _pool_count(i * tb, n_ext)
        a2 = dpc + _roll(dpc, n_ext - 1)
        a4 = a2 + _roll(a2, n_ext - 2)
        a8 = a4 + _roll(a4, n_ext - 4)
        a16 = a8 + _roll(a8, n_ext - 8)
        o_ref[:, 0:256] = (_window_select(_lane_group(dpc.shape), a2, a4, a8, a16) - dp)[:tb]
        w = sw_ref[...]
        gb, gc, hh = main[:, 256:512], main[:, 512:768], main[:, 768:1024]
        q_ext = ext[:, 512:768] * ext[:, 768:1024]
        db = db_ref[...]
        o_ref[:, 256:512] = db * _sconv(q_ext, w)
        gb_next = hn_ref[:, 256:512]
        dconv = jnp.concatenate([db * gb, jnp.where(last, 0.0, dbn_ref[...] * gb_next)], axis=0)
        dq = (dconv * w[2:3] + _roll(dconv, n_ext - 1) * w[1:2] + _roll(dconv, n_ext - 2) * w[0:1])[:tb]
        o_ref[:, 512:768] = dq * hh
        o_ref[:, 768:1024] = dq * gc
        dc = dconv[:tb]
        dsw_ref[0:1, :] += _colsum(dc * _roll(q_ext, 2)[HALO:])
        dsw_ref[1:2, :] += _colsum(dc * _roll(q_ext, 1)[HALO:])
        dsw_ref[2:3, :] += _colsum(dc * q_ext[HALO:])

    blk = pl.BlockSpec((tb, GW), lambda i: (i, 0))
    return pl.pallas_call(
        body, name="b_ab", grid=(nb,),
        in_specs=[pl.BlockSpec((tb, 1024), lambda i: (i, 0)), prev, nxt, blk, nxt_g, blk, nxt_g,
                  _full((GW, GW)), _row(GW), _full((3, GW))],
        out_specs=[pl.BlockSpec((tb, 1024), lambda i: (i, 0)), _full((GW, GW)), _row(GW), _full((3, GW))],
        out_shape=[jax.ShapeDtypeStruct((t, 1024), F32), jax.ShapeDtypeStruct((GW, GW), F32),
                   jax.ShapeDtypeStruct((1, GW), F32), jax.ShapeDtypeStruct((3, GW), F32)],
        compiler_params=_cparams(1),
    )(proj, proj, proj, dya, dya, dyb, dyb, pool_mat, pool_scale, sconv_w)


CH = 8


def _ssd_conv(x, halo, w, b):
    ext = jnp.concatenate([halo, x], axis=0)
    pre = ext * w[3:4] + _roll(ext, 1) * w[2:3] + _roll(ext, 2) * w[1:2] + _roll(ext, 3) * w[0:1] + b
    return pre[CH:], ext


def _ssd_common(dt_raw, dtb, alog):
    ll = dt_raw.shape[0]
    dtv = _softplus(dt_raw + dtb)
    a_row = -jnp.exp(alog)
    r = lax.broadcasted_iota(jnp.int32, (ll, ll), 0)
    c = lax.broadcasted_iota(jnp.int32, (ll, ll), 1)
    tril = (r >= c).astype(F32)
    cs = _dot(tril, dtv * a_row, prec=HI)
    return dtv, a_row, cs, cs.T, r >= c


def _ssd_bc(act_b, g):
    return act_b[:, 256 + NS * g:256 + NS * (g + 1)], act_b[:, 512 + NS * g:512 + NS * (g + 1)]


def _ssd_gmat(act_b):
    return [_dot(_ssd_bc(act_b, g)[1], _ssd_bc(act_b, g)[0], NT) for g in range(2)]


def _ssd_head(h, act, act_b, dtv, cs, cs_t, causal, gmat):
    g = h // 2
    xs = act[:, HP * h:HP * (h + 1)]
    bm, cm = _ssd_bc(act_b, g)
    cs_c = cs[:, h:h + 1]
    cs_r = cs_t[h:h + 1, :]
    mdec = jnp.where(causal, jnp.exp(jnp.minimum(cs_c - cs_r, 0.0)), 0.0)
    sc = gmat[g] * mdec
    dt_c = dtv[:, h:h + 1]
    xdt = xs * dt_c
    e = jnp.exp(cs_c)
    cs_last = cs[SSD_L - 1:SSD_L, h:h + 1]
    wdec = jnp.exp(cs_last - cs_c)
    return xs, bm, cm, cs_c, mdec, sc, dt_c, xdt, e, cs_last, wdec


def _f_ssd(proj, dtp, conv_w, conv_b, dt_bias, a_log, d_skip):
    t = proj.shape[0]
    nc = t // SSD_L
    rows = SSD_SUB * SSD_L
    per = rows // CH

    def body(x_ref, hx_ref, dt_ref, z_ref, cw_ref, cb_ref, dtb_ref, al_ref, dk_ref, y_ref, yp_ref, sp_ref, s_ref):
        i = pl.program_id(0)

        @pl.when(i == 0)
        def _():
            s_ref[...] = jnp.zeros_like(s_ref)

        state = [s_ref[h] for h in range(NH)]
        for sub in range(SSD_SUB):
            r0 = sub * SSD_L
            rs = slice(r0, r0 + SSD_L)
            halo = jnp.where(i > 0, hx_ref[...], 0.0) if sub == 0 else x_ref[r0 - CH:r0, :]
            pre, _ = _ssd_conv(x_ref[rs, :], halo, cw_ref[...], cb_ref[...])
            act = _silu(pre)
            dtv, _, cs, cs_t, causal = _ssd_common(dt_ref[rs, :], dtb_ref[...], al_ref[...])
            gmat = _ssd_gmat(act)
            for h in range(NH):
                xs, bm, cm, _, _, sc, _, xdt, e, cs_last, wdec = _ssd_head(h, act, act, dtv, cs, cs_t, causal, gmat)
                prev = state[h]
                sp_ref[sub, h] = prev
                y = _dot(sc, xdt) + e * _dot(cm, prev, NT) + xs * dk_ref[0:1, h:h + 1]
                yp_ref[rs, HP * h:HP * (h + 1)] = y
                state[h] = prev * jnp.exp(cs_last) + _dot(xdt * wdec, bm, TN)
            y_ref[rs, :] = yp_ref[rs, :] * _silu(z_ref[rs, :])
        for h in range(NH):
            s_ref[h] = state[h]

    blk = pl.BlockSpec((rows, GW), lambda i: (i, 0))
    sds = jax.ShapeDtypeStruct((t, GW), F32)
    return pl.pallas_call(
        body, name="f_ssd", grid=(nc // SSD_SUB,),
        in_specs=[pl.BlockSpec((rows, 768), lambda i: (i, 2)),
                  pl.BlockSpec((CH, 768), lambda i: (jnp.maximum(i * per - 1, 0), 2)),
                  pl.BlockSpec((rows, DTW), lambda i: (i, 0)),
                  pl.BlockSpec((rows, GW), lambda i: (i, 4)),
                  _full((4, 768)), _row(768), _row(DTW), _row(DTW), _row(DTW)],
        out_specs=[blk, blk, pl.BlockSpec((SSD_SUB, NH, HP, NS), lambda i: (i, 0, 0, 0))],
        out_shape=[sds, sds, jax.ShapeDtypeStruct((nc, NH, HP, NS), F32)],
        scratch_shapes=[pltpu.VMEM((NH, HP, NS), F32)], compiler_params=_cparams(1),
    )(proj, proj, dtp, proj, conv_w, conv_b, dt_bias, a_log, d_skip)


def _b_ssd(proj, dtp, ypre, dyc, sprev, conv_w, conv_b, dt_bias, a_log, d_skip):
    t = proj.shape[0]
    nc = t // SSD_L
    steps = nc // SSD_SUB_BWD
    rows = SSD_SUB_BWD * SSD_L
    per = rows // CH
    n_ext = SSD_L + CH

    def chunk(sub, halo, dnext, ds_in, refs):
        (x_ref, dt_ref, z_ref, yp_ref, dy_ref, sp_ref, cw_ref, cb_ref, dtb_ref, al_ref, dk_ref,
         dz_ref, dx_ref, ddt_ref, dact_ref) = refs
        rs = slice(sub * SSD_L, (sub + 1) * SSD_L)
        dact = dact_ref.at[sub]
        w = cw_ref[...]
        pre, ext = _ssd_conv(x_ref[rs, :], halo, w, cb_ref[...])
        act = _silu(pre)
        dt_raw = dt_ref[rs, :]
        dtv, a_row, cs, cs_t, causal = _ssd_common(dt_raw, dtb_ref[...], al_ref[...])
        gmat = _ssd_gmat(act)
        z = z_ref[rs, :]
        dyc_v = dy_ref[rs, :]
        dz_ref[rs, :] = dyc_v * yp_ref[rs, :] * _dsilu(z)
        dy_all = dyc_v * _silu(z)
        lane = lax.broadcasted_iota(jnp.int32, (SSD_L, DTW), 1)
        rowi = lax.broadcasted_iota(jnp.int32, (SSD_L, 1), 0)
        dcs_mat = jnp.zeros((SSD_L, DTW), F32)
        ddtx_mat = jnp.zeros((SSD_L, DTW), F32)
        ddk_row = jnp.zeros((1, DTW), F32)
        lane1 = lax.broadcasted_iota(jnp.int32, (1, DTW), 1)
        dbm = [None, None]
        dcm = [None, None]
        ds_out = []
        for h in range(NH):
            g = h // 2
            xs, bm, cm, _, mdec, sc, dt_c, xdt, e, cs_last, wdec = _ssd_head(h, act, act, dtv, cs, cs_t, causal, gmat)
            dy = dy_all[:, HP * h:HP * (h + 1)]
            prev = sp_ref[sub, h]
            ds = ds_in[h]
            dsc = _dot(dy, xdt, NT)
            q = dsc * sc
            dg = dsc * mdec
            dxdt = _dot(sc, dy, TN)
            dcs = jnp.sum(q, axis=1, keepdims=True) - jnp.sum(q.T, axis=1, keepdims=True)
            dc_h = _dot(dg, bm)
            db_h = _dot(dg, cm, TN)
            cp = _dot(cm, prev, NT)
            dcs += jnp.sum(dy * cp, axis=1, keepdims=True) * e
            ey = e * dy
            dc_h += _dot(ey, prev)
            dprev = _dot(ey, cm, TN)
            elast = jnp.exp(cs_last)
            dprev += ds * elast
            dcs_last = jnp.sum(ds * prev, keepdims=True) * elast
            bds = _dot(bm, ds, NT)
            dxdt += wdec * bds
            db_h += wdec * _dot(xdt, ds)
            dw = jnp.sum(xdt * bds, axis=1, keepdims=True) * wdec
            dcs -= dw
            dcs_last += jnp.sum(dw, keepdims=True)
            dcs += jnp.where(rowi == SSD_L - 1, dcs_last, 0.0)
            ds_out.append(dprev)
            dact[:, HP * h:HP * (h + 1)] = dxdt * dt_c + dy * dk_ref[0:1, h:h + 1]
            dcs_mat = jnp.where(lane == h, dcs, dcs_mat)
            ddtx_mat = jnp.where(lane == h, jnp.sum(dxdt * xs, axis=1, keepdims=True), ddtx_mat)
            ddk_row = jnp.where(lane1 == h, jnp.sum(dy * xs, keepdims=True), ddk_row)
            dbm[g] = db_h if dbm[g] is None else dbm[g] + db_h
            dcm[g] = dc_h if dcm[g] is None else dcm[g] + dc_h
        for g in range(2):
            dact[:, 256 + NS * g:256 + NS * (g + 1)] = dbm[g]
            dact[:, 512 + NS * g:512 + NS * (g + 1)] = dcm[g]
        r2 = lax.broadcasted_iota(jnp.int32, (SSD_L, SSD_L), 0)
        c2 = lax.broadcasted_iota(jnp.int32, (SSD_L, SSD_L), 1)
        dadt = _dot((c2 >= r2).astype(F32), dcs_mat, prec=HI)
        ddt = jnp.where(lane < NH, (dadt * a_row + ddtx_mat) * _sig(dt_raw + dtb_ref[...]), 0.0)
        ddt_ref[rs, :] = ddt
        dpre = dact[...] * _dsilu(pre)
        dcw = jnp.concatenate([_colsum(dpre * _roll(ext, 3 - k)[CH:]) for k in range(4)], axis=0)
        dext = jnp.concatenate([dpre, dnext], axis=0)
        dx_ref[rs, :] = (dext * w[3:4] + _roll(dext, n_ext - 1) * w[2:3] + _roll(dext, n_ext - 2) * w[1:2]
                         + _roll(dext, n_ext - 3) * w[0:1])[:SSD_L]
        acc = (dcw, _colsum(dpre), _colsum(ddt), _colsum(dadt * dtv) * a_row, ddk_row)
        return dpre[0:CH], ds_out, acc

    def body(x_ref, hx_ref, dt_ref, z_ref, yp_ref, dy_ref, sp_ref, cw_ref, cb_ref, dtb_ref, al_ref, dk_ref,
             dz_ref, dx_ref, ddt_ref, dcw_ref, dcb_ref, ddtb_ref, dal_ref, ddk_ref, ds_ref, dnext_ref, dact_ref):
        i = pl.program_id(0)
        acc_refs = (dcw_ref, dcb_ref, ddtb_ref, dal_ref, ddk_ref)

        @pl.when(i == 0)
        def _():
            ds_ref[...] = jnp.zeros_like(ds_ref)
            dnext_ref[...] = jnp.zeros_like(dnext_ref)
            for r in acc_refs:
                r[...] = jnp.zeros_like(r)

        refs = (x_ref, dt_ref, z_ref, yp_ref, dy_ref, sp_ref, cw_ref, cb_ref, dtb_ref, al_ref, dk_ref, dz_ref, dx_ref, ddt_ref,
                dact_ref)
        ds = [ds_ref[h] for h in range(NH)]
        dnext = dnext_ref[...]
        total = None
        for sub in reversed(range(SSD_SUB_BWD)):
            if sub == 0:
                halo = jnp.where(i == steps - 1, 0.0, hx_ref[...])
            else:
                halo = x_ref[sub * SSD_L - CH:sub * SSD_L, :]
            dnext, ds, acc = chunk(sub, halo, dnext, ds, refs)
            total = acc if total is None else tuple(a + b for a, b in zip(total, acc))
        for h in range(NH):
            ds_ref[h] = ds[h]
        dnext_ref[...] = dnext
        for r, v in zip(acc_refs, total):
            r[...] += v

    rev = lambda i: steps - 1 - i
    blk = lambda n, cb=0: pl.BlockSpec((rows, n), lambda i: (rev(i), cb))
    row = lambda n: jax.ShapeDtypeStruct((1, n), F32)
    return pl.pallas_call(
        body, name="b_ssd", grid=(steps,),
        in_specs=[blk(768, 2), pl.BlockSpec((CH, 768), lambda i: (jnp.maximum(rev(i) * per - 1, 0), 2)),
                  blk(DTW), blk(GW, 4), blk(GW), blk(GW), pl.BlockSpec((SSD_SUB_BWD, NH, HP, NS), lambda i: (rev(i), 0, 0, 0)),
                  _full((4, 768)), _row(768), _row(DTW), _row(DTW), _row(DTW)],
        out_specs=[blk(GW), blk(768), blk(DTW), _full((4, 768)), _row(768), _row(DTW), _row(DTW), _row(DTW)],
        out_shape=[jax.ShapeDtypeStruct((t, GW), F32), jax.ShapeDtypeStruct((t, 768), F32), jax.ShapeDtypeStruct((t, DTW), F32),
                   jax.ShapeDtypeStruct((4, 768), F32), row(768), row(DTW), row(DTW), row(DTW)],
        scratch_shapes=[pltpu.VMEM((NH, HP, NS), F32), pltpu.VMEM((CH, 768), F32), pltpu.VMEM((SSD_SUB_BWD, SSD_L, 768), F32)],
        compiler_params=_cparams(1),
    )(proj, proj, dtp, proj, ypre, dyc, sprev, conv_w, conv_b, dt_bias, a_log, d_skip)


def _s5_block(t):
    return min(t, 256)


def _seg_t():
    r = lax.broadcasted_iota(jnp.int32, (64, 1024), 0)
    c = lax.broadcasted_iota(jnp.int32, (64, 1024), 1)
    return (c // 16 == r).astype(F32)


def _s5_prep_math(a_re, a_im, lstep, b_re, b_im):
    step = jnp.exp(lstep)
    ars = a_re * step
    ais = a_im * step
    mag = jnp.exp(ars)
    lr = mag * jnp.cos(ais)
    li = mag * jnp.sin(ais)
    den = a_re * a_re + a_im * a_im
    nr = lr - 1.0
    f_re = (nr * a_re + li * a_im) / den
    f_im = (li * a_re - nr * a_im) / den
    seg = _seg_t()
    fr = _dot(f_re, seg, prec=HI)
    fi = _dot(f_im, seg, prec=HI)
    return lr, li, fr * b_re - fi * b_im, fr * b_im + fi * b_re, ars, ais


def _s5_prep(a_re, a_im, lstep, b_re, b_im):
    def body(ar, ai, ls, br, bi, lr_o, li_o, bbr_o, bbi_o, ars_o, ais_o):
        outs = _s5_prep_math(ar[...], ai[...], ls[...], br[...], bi[...])
        for o, v in zip((lr_o, li_o, bbr_o, bbi_o, ars_o, ais_o), outs):
            o[...] = v

    s64 = jax.ShapeDtypeStruct((16, 64), F32)
    s1k = jax.ShapeDtypeStruct((16, 1024), F32)
    return pl.pallas_call(body, name="s5_prep", out_shape=[s64, s64, s1k, s1k, s64, s64])(a_re, a_im, lstep, b_re, b_im)


def _s5_prep_bwd(a_re, a_im, lstep, b_re, b_im, dlr, dli, dbbr, dbbi):
    def body(ar, ai, ls, br, bi, g0, g1, g2, g3, o0, o1, o2, o3, o4):
        f = lambda *a: _s5_prep_math(*a)[:4]
        _, vjp = jax.vjp(f, ar[...], ai[...], ls[...], br[...], bi[...])
        for o, v in zip((o0, o1, o2, o3, o4), vjp((g0[...], g1[...], g2[...], g3[...]))):
            o[...] = v

    s64 = jax.ShapeDtypeStruct((16, 64), F32)
    s1k = jax.ShapeDtypeStruct((16, 1024), F32)
    return pl.pallas_call(body, name="s5_prep_bwd", out_shape=[s64, s64, jax.ShapeDtypeStruct((16, 1), F32), s1k, s1k])(
        a_re, a_im, lstep, b_re, b_im, dlr, dli, dbbr, dbbi)


SUB = 8


def _s5_tables(ars, ais):
    def body(ar, ai, tr, ti):
        rr = lax.broadcasted_iota(jnp.int32, (8 * SUB, S5_P), 0)
        seg, r = rr // SUB, rr % SUB
        step = jnp.where((seg == 1) | (seg == 4), 1, jnp.where((seg == 2) | (seg == 5), 2, 4))
        n = jnp.where(seg == 0, r + 1, jnp.where(seg == 7, SUB - r, step))
        fwd_gap = jnp.where(seg <= 3, r - step, SUB - step - 1 - r)
        gap = jnp.where((seg == 0) | (seg == 7), 0, fwd_gap)
        nf = n.astype(F32)
        mag = jnp.where(gap >= 0, jnp.exp(nf * ar[...]), 0.0)
        tr[...] = mag * jnp.cos(nf * ai[...])
        ti[...] = mag * jnp.sin(nf * ai[...])

    sds = jax.ShapeDtypeStruct((8 * SUB, S5_P), F32)
    return pl.pallas_call(body, name="s5_tables", out_shape=[sds] * 2)(ars, ais)


def _s5_table(tb_r, tb_i, k):
    return tb_r[SUB * k:SUB * (k + 1), :], tb_i[SUB * k:SUB * (k + 1), :]


def _s5_scan(bu_r, bu_i, tb_r, tb_i, c_r, c_i, lb):
    nt = lb // SUB
    sr, si = bu_r.reshape(nt, SUB, S5_P), bu_i.reshape(nt, SUB, S5_P)
    for j, k in enumerate((1, 2, 4)):
        mr, mi = _s5_table(tb_r, tb_i, 1 + j)
        tr, ti = pltpu.roll(sr, k, axis=1), pltpu.roll(si, k, axis=1)
        sr, si = sr + mr * tr - mi * ti, si + mr * ti + mi * tr
    pr, pi = _s5_table(tb_r, tb_i, 0)
    out_r, out_i = [], []
    for j in range(nt):
        a_r = sr[j] + pr * c_r - pi * c_i
        a_i = si[j] + pr * c_i + pi * c_r
        out_r.append(a_r)
        out_i.append(a_i)
        c_r, c_i = a_r[SUB - 1:SUB], a_i[SUB - 1:SUB]
    return jnp.concatenate(out_r, axis=0), jnp.concatenate(out_i, axis=0)


def _s5_rscan(g_r, g_i, tb_r, tb_i, n_r, n_i, lb):
    nt = lb // SUB
    gr, gi = g_r.reshape(nt, SUB, S5_P), g_i.reshape(nt, SUB, S5_P)
    for j, k in enumerate((1, 2, 4)):
        mr, mi = _s5_table(tb_r, tb_i, 4 + j)
        tr, ti = pltpu.roll(gr, SUB - k, axis=1), pltpu.roll(gi, SUB - k, axis=1)
        gr, gi = gr + mr * tr + mi * ti, gi + mr * ti - mi * tr
    qr, qi = _s5_table(tb_r, tb_i, 7)
    out_r, out_i = [None] * nt, [None] * nt
    for j in reversed(range(nt)):
        a_r = gr[j] + qr * n_r + qi * n_i
        a_i = gi[j] + qr * n_i - qi * n_r
        out_r[j], out_i[j] = a_r, a_i
        n_r, n_i = a_r[0:1], a_i[0:1]
    return jnp.concatenate(out_r, axis=0), jnp.concatenate(out_i, axis=0)


def _s5_y(u, sr, si, cre, cim, dsk):
    return _bdot(sr, cre) + _bdot(si, cim) + dsk * u


def _f_s5(proj, bmat, cre, cim, p_r, p_i, dsk, glu_w, glu_b):
    t = proj.shape[0]
    lb = _s5_block(t)
    nb = t // lb

    def body(u_ref, bm_ref, cr_ref, ci_ref, pr_ref, pi_ref, dk_ref, gw_ref, gb_ref, y_ref, car_ref, s_ref, st_ref):
        @pl.when(pl.program_id(0) == 0)
        def _():
            st_ref[...] = jnp.zeros_like(st_ref)

        u = u_ref[...]
        bu = _bdot(u, bm_ref[...])
        c_r, c_i = st_ref[0:1, 0:S5_P], st_ref[0:1, S5_P:]
        car_ref[0] = st_ref[0:1, :]
        sr, si = _s5_scan(bu[:, :S5_P], bu[:, S5_P:], pr_ref, pi_ref, c_r, c_i, lb)
        st_ref[0:1, 0:S5_P] = sr[lb - 1:lb]
        st_ref[0:1, S5_P:] = si[lb - 1:lb]
        sr_b, si_b = sr.astype(BF16), si.astype(BF16)
        s_ref[:, 0:S5_P] = sr_b
        s_ref[:, S5_P:] = si_b
        gel = _gelu(_s5_y(u, sr_b, si_b, cr_ref[...], ci_ref[...], dk_ref[...]))
        y_ref[...] = gel * _sig(_bdot(gel, gw_ref[...]) + gb_ref[...])

    return pl.pallas_call(
        body, name="f_s5", grid=(nb,),
        in_specs=[pl.BlockSpec((lb, GW), lambda i: (i, 5)),
                  _full((GW, 2 * S5_P)), _full((S5_P, GW)), _full((S5_P, GW)), _full((8 * SUB, S5_P)), _full((8 * SUB, S5_P)),
                  _row(GW), _full((GW, GW)), _row(GW)],
        out_specs=[pl.BlockSpec((lb, GW), lambda i: (i, 0)), pl.BlockSpec((1, 1, 2 * S5_P), lambda i: (i, 0, 0)),
                   pl.BlockSpec((lb, 2 * S5_P), lambda i: (i, 0))],
        out_shape=[jax.ShapeDtypeStruct((t, GW), F32), jax.ShapeDtypeStruct((nb, 1, 2 * S5_P), F32),
                   jax.ShapeDtypeStruct((t, 2 * S5_P), BF16)],
        scratch_shapes=[pltpu.VMEM((8, 2 * S5_P), F32)], compiler_params=_cparams(1),
    )(proj, bmat, cre, cim, p_r, p_i, dsk, glu_w, glu_b)


def _b_s5(proj, dyd, carries, states, bmat, cre, cim, p_r, p_i, dsk, glu_w, glu_b):
    t = proj.shape[0]
    lb = _s5_block(t)
    nb = t // lb

    def body(u_ref, dy_ref, car_ref, s_ref, bm_ref, cr_ref, ci_ref, pr_ref, pi_ref, dk_ref, gw_ref, gb_ref,
             du_ref, dbm_ref, dcr_ref, dci_ref, dlam_ref, ddk_ref, dgw_ref, dgb_ref, gc_ref):
        @pl.when(pl.program_id(0) == 0)
        def _():
            gc_ref[...] = jnp.zeros_like(gc_ref)
            for r in (dbm_ref, dcr_ref, dci_ref, dlam_ref, ddk_ref, dgw_ref, dgb_ref):
                r[...] = jnp.zeros_like(r)

        u = u_ref[...]
        bm = bm_ref[...]
        u_b = u.astype(BF16)
        c_r, c_i = car_ref[0, 0:1, 0:S5_P], car_ref[0, 0:1, S5_P:]
        cre_v, cim_v, dk, gw = cr_ref[...], ci_ref[...], dk_ref[...], gw_ref[...]
        sr_b, si_b = s_ref[:, 0:S5_P], s_ref[:, S5_P:]
        sr, si = sr_b.astype(F32), si_b.astype(F32)
        y = _dot(sr_b, cre_v) + _dot(si_b, cim_v) + dk * u
        gel = _gelu(y)
        gel_b = gel.astype(BF16)
        gate = _sig(_dot(gel_b, gw) + gb_ref[...])
        dout = dy_ref[...]
        t1 = dout * gel * gate * (1.0 - gate)
        t1_b = t1.astype(BF16)
        dgw_ref[...] += _dot(gel_b, t1_b, TN)
        dgb_ref[...] += _colsum(t1)
        dyv = (dout * gate + _dot(t1_b, gw, NT)) * _dgelu(y)
        dyv_b = dyv.astype(BF16)
        ddk_ref[...] += _colsum(dyv * u)
        dcr_ref[...] += _dot(sr_b, dyv_b, TN)
        dci_ref[...] += _dot(si_b, dyv_b, TN)
        gr = _dot(dyv_b, cre_v, NT)
        gi = _dot(dyv_b, cim_v, NT)
        row = lax.broadcasted_iota(jnp.int32, (lb, S5_P), 0)
        n_r, n_i = gc_ref[0:1, 0:S5_P], gc_ref[0:1, S5_P:]
        gr, gi = _s5_rscan(gr, gi, pr_ref, pi_ref, n_r, n_i, lb)
        gc_ref[0:1, 0:S5_P] = gr[0:1]
        gc_ref[0:1, S5_P:] = gi[0:1]
        gcat = jnp.concatenate([gr, gi], axis=1).astype(BF16)
        dbm_ref[...] += _dot(u_b, gcat, TN)
        du_ref[...] = dyv * dk + _dot(gcat, bm, NT)
        spr = jnp.where(row >= 1, _roll(sr, 1), c_r)
        spi = jnp.where(row >= 1, _roll(si, 1), c_i)
        dlam_ref[0:1, :] += _colsum(gr * spr + gi * spi)
        dlam_ref[1:2, :] += _colsum(gi * spr - gr * spi)

    rev = lambda i: nb - 1 - i
    return pl.pallas_call(
        body, name="b_s5", grid=(nb,),
        in_specs=[pl.BlockSpec((lb, GW), lambda i: (rev(i), 5)), pl.BlockSpec((lb, GW), lambda i: (rev(i), 0)),
                  pl.BlockSpec((1, 1, 2 * S5_P), lambda i: (rev(i), 0, 0)), pl.BlockSpec((lb, 2 * S5_P), lambda i: (rev(i), 0)),
                  _full((GW, 2 * S5_P)), _full((S5_P, GW)), _full((S5_P, GW)), _full((8 * SUB, S5_P)), _full((8 * SUB, S5_P)),
                  _row(GW), _full((GW, GW)), _row(GW)],
        out_specs=[pl.BlockSpec((lb, GW), lambda i: (rev(i), 0)), _full((GW, 2 * S5_P)), _full((S5_P, GW)), _full((S5_P, GW)),
                   _full((2, S5_P)), _row(GW), _full((GW, GW)), _row(GW)],
        out_shape=[jax.ShapeDtypeStruct((t, GW), F32), jax.ShapeDtypeStruct((GW, 2 * S5_P), F32),
                   jax.ShapeDtypeStruct((S5_P, GW), F32), jax.ShapeDtypeStruct((S5_P, GW), F32),
                   jax.ShapeDtypeStruct((2, S5_P), F32), jax.ShapeDtypeStruct((1, GW), F32),
                   jax.ShapeDtypeStruct((GW, GW), F32), jax.ShapeDtypeStruct((1, GW), F32)],
        scratch_shapes=[pltpu.VMEM((8, 2 * S5_P), F32)], compiler_params=_cparams(1),
    )(proj, dyd, carries, states, bmat, cre, cim, p_r, p_i, dsk, glu_w, glu_b)


def _group_norm(ys, bw):
    outs, stats = [], []
    for g, y in enumerate(ys):
        r, n = _rms(y)
        stats.append((r, n))
        outs.append(n * bw[:, GW * g:GW * (g + 1)])
    return jnp.concatenate(outs, axis=1), stats


def _f_out(ya, yb, yc, yd, bw, w_out, h, g1):
    t = h.shape[0]
    tb = _tblock(t)

    def body(a_ref, b_ref, c_ref, d_ref, bw_ref, w_ref, h_ref, g_ref, h2_ref, o_ref, cat_ref):
        cat, _ = _group_norm([a_ref[...], b_ref[...], c_ref[...], d_ref[...]], bw_ref[...])
        catb = cat.astype(BF16)
        cat_ref[...] = catb
        o = _dot(catb, w_ref[...])
        o_ref[...] = o
        h2_ref[...] = h_ref[...] + g_ref[...] * o

    yblk = pl.BlockSpec((tb, GW), lambda i: (i, 0))
    blk = pl.BlockSpec((tb, D), lambda i: (i, 0))
    return pl.pallas_call(
        body, name="f_out", grid=(t // tb,), in_specs=[yblk] * 4 + [_row(D), _full((D, D)), blk, _row(D)],
        out_specs=[blk, blk, blk],
        out_shape=[jax.ShapeDtypeStruct((t, D), F32), jax.ShapeDtypeStruct((t, D), F32), jax.ShapeDtypeStruct((t, D), BF16)],
        compiler_params=_cparams(1),
    )(ya, yb, yc, yd, bw, w_out, h, g1)


def _b_out(dh2, ya, yb, yc, yd, bw, w_out, g1):
    t = dh2.shape[0]
    tb = _tblock(t)

    def body(dh_ref, a_ref, b_ref, c_ref, d_ref, bw_ref, w_ref, g_ref, da_ref, db_ref, dc_ref, dd_ref, do_ref, dbw_ref):
        @pl.when(pl.program_id(0) == 0)
        def _():
            dbw_ref[...] = jnp.zeros_like(dbw_ref)

        do = (dh_ref[...] * g_ref[...]).astype(BF16)
        do_ref[...] = do
        dcat = _dot(do, w_ref[...], NT)
        bw_v = bw_ref[...]
        for g, (y_ref, dy_ref) in enumerate(((a_ref, da_ref), (b_ref, db_ref), (c_ref, dc_ref), (d_ref, dd_ref))):
            r, n = _rms(y_ref[...])
            dc = dcat[:, GW * g:GW * (g + 1)]
            dbw_ref[:, GW * g:GW * (g + 1)] += _colsum(dc * n)
            dy_ref[...] = _rms_bwd(r, n, dc * bw_v[:, GW * g:GW * (g + 1)])

    yblk = pl.BlockSpec((tb, GW), lambda i: (i, 0))
    blk = pl.BlockSpec((tb, D), lambda i: (i, 0))
    ysd = jax.ShapeDtypeStruct((t, GW), F32)
    return pl.pallas_call(
        body, name="b_out", grid=(t // tb,), in_specs=[blk] + [yblk] * 4 + [_row(D), _full((D, D)), _row(D)],
        out_specs=[yblk] * 4 + [blk, _row(D)],
        out_shape=[ysd] * 4 + [jax.ShapeDtypeStruct((t, D), BF16), jax.ShapeDtypeStruct((1, D), F32)],
        compiler_params=_cparams(1),
    )(dh2, ya, yb, yc, yd, bw, w_out, g1)


HB = 1024


def _f_mlp(h2, nw, sc, sh, g2, w1, w2):
    t = h2.shape[0]
    tb = _tblock(t)
    nk = HID // HB

    def body(h_ref, nw_ref, sc_ref, sh_ref, g_ref, w1_ref, w2_ref, h3_ref, m_ref, a_ref, v_ref):
        k = pl.program_id(1)

        @pl.when(k == 0)
        def _():
            _, n = _rms(h_ref[...])
            v_ref[...] = ((n * nw_ref[...]) * (1.0 + sc_ref[...]) + sh_ref[...]).astype(BF16)
            m_ref[...] = jnp.zeros_like(m_ref)

        a = _dot(v_ref[...], w1_ref[...])
        a_ref[...] = a
        ra = jnp.maximum(a, 0.0)
        m_ref[...] += _dot((ra * ra).astype(BF16), w2_ref[...])

        @pl.when(k == nk - 1)
        def _():
            h3_ref[...] = h_ref[...] + g_ref[...] * m_ref[...]

    blk = pl.BlockSpec((tb, D), lambda i, k: (i, 0))
    return pl.pallas_call(
        body, name="f_mlp", grid=(t // tb, nk),
        in_specs=[blk, _row(D), _row(D), _row(D), _row(D), pl.BlockSpec((D, HB), lambda i, k: (0, k)),
                  pl.BlockSpec((HB, D), lambda i, k: (k, 0))],
        out_specs=[blk, blk, pl.BlockSpec((tb, HB), lambda i, k: (i, k)), blk],
        out_shape=[jax.ShapeDtypeStruct((t, D), F32), jax.ShapeDtypeStruct((t, D), F32), jax.ShapeDtypeStruct((t, HID), F32),
                   jax.ShapeDtypeStruct((t, D), BF16)],
        compiler_params=_cparams(2),
    )(h2, nw, sc, sh, g2, w1, w2)


def _b_mlp(dh3, a, g2, w1, w2):
    t = dh3.shape[0]
    tb = _tblock(t)
    nk = HID // HB

    def body(dh_ref, a_ref, g_ref, w1_ref, w2_ref, dv_ref, da_ref, act_ref, dm_ref):
        k = pl.program_id(1)
        dm = (dh_ref[...] * g_ref[...]).astype(BF16)

        @pl.when(k == 0)
        def _():
            dm_ref[...] = dm
            dv_ref[...] = jnp.zeros_like(dv_ref)

        ra = jnp.maximum(a_ref[...], 0.0)
        act_ref[...] = (ra * ra).astype(BF16)
        da = (_dot(dm, w2_ref[...], NT) * (2.0 * ra)).astype(BF16)
        da_ref[...] = da
        dv_ref[...] += _dot(da, w1_ref[...], NT)

    blk = pl.BlockSpec((tb, D), lambda i, k: (i, 0))
    hblk = pl.BlockSpec((tb, HB), lambda i, k: (i, k))
    return pl.pallas_call(
        body, name="b_mlp", grid=(t // tb, nk),
        in_specs=[blk, hblk, _row(D), pl.BlockSpec((D, HB), lambda i, k: (0, k)), pl.BlockSpec((HB, D), lambda i, k: (k, 0))],
        out_specs=[blk, hblk, hblk, blk],
        out_shape=[jax.ShapeDtypeStruct((t, D), F32), jax.ShapeDtypeStruct((t, HID), BF16), jax.ShapeDtypeStruct((t, HID), BF16),
                   jax.ShapeDtypeStruct((t, D), BF16)],
        compiler_params=_cparams(2),
    )(dh3, a, g2, w1, w2)


def _b_final(h, tgt, fw):
    t = h.shape[0]
    tb = _tblock(t)

    def body(h_ref, t_ref, w_ref, dh_ref, loss_ref, dfw_ref):
        @pl.when(pl.program_id(0) == 0)
        def _():
            loss_ref[...] = jnp.zeros_like(loss_ref)
            dfw_ref[...] = jnp.zeros_like(dfw_ref)

        r, n = _rms(h_ref[...])
        wv = w_ref[...]
        err = n * wv - t_ref[...]
        loss_ref[...] += jnp.sum(err * err, keepdims=True) * (0.5 / D)
        dy = err * (1.0 / D)
        dfw_ref[...] += _colsum(dy * n)
        dh_ref[...] = _rms_bwd(r, n, dy * wv)

    blk = pl.BlockSpec((tb, D), lambda i: (i, 0))
    return pl.pallas_call(
        body, name="b_final", grid=(t // tb,), in_specs=[blk, blk, _row(D)], out_specs=[blk, _row(1), _row(D)],
        out_shape=[jax.ShapeDtypeStruct((t, D), F32), jax.ShapeDtypeStruct((1, 1), F32), jax.ShapeDtypeStruct((1, D), F32)],
        compiler_params=_cparams(1),
    )(h, tgt, fw)


_EYE16 = None


def _eye(n):
    return jnp.eye(n, dtype=F32)


def _pool_embed(pool_w):
    return jnp.einsum('gcd,gk->gckd', pool_w, _eye(4)).reshape(GW, GW)


def _pool_extract(m):
    return jnp.einsum('gcgd->gcd', m.reshape(4, 64, 4, 64))


def _bmat_embed(bb):
    return jnp.einsum('gph,gk->ghkp', bb, _eye(16)).reshape(GW, S5_P)


def _bmat_extract(m):
    return jnp.einsum('ghgp->gph', m.reshape(16, 16, 16, 64))


def _cmat_embed(cc):
    return jnp.einsum('ghp,gk->kpgh', cc, _eye(16)).reshape(S5_P, GW)


def _cmat_extract(m):
    return jnp.einsum('gpgh->ghp', m.reshape(16, 64, 16, 16))


def _pad_lanes(v, n=DTW):
    return jnp.pad(v.reshape(1, -1), ((0, 0), (0, n - v.shape[-1])))


def _w_in_layout(w_in):
    w_main = jnp.concatenate([w_in[:, :1280], w_in[:, 2052:2308], w_in[:, 1280:2048]], axis=1)
    return w_main, jnp.pad(w_in[:, 2048:2052], ((0, 0), (0, DTW - 4)))


def _layer_params(p, l, mod, w_in, rest):
    q = {'rest': rest}
    q['mod'] = [mod[k:k + 1] for k in range(6)]
    q['nw1'] = p['norm_mix_w'][l:l + 1]
    q['nw2'] = p['norm_mlp_w'][l:l + 1]
    q['w_main'], q['w_dt'] = _w_in_layout(w_in)
    q['pool_mat'] = _pool_embed(p['pool_w'][l]).astype(BF16)
    q['pool_scale'] = p['pool_scale'][l:l + 1]
    q['sconv_w'] = p['sconv_w'][l]
    q['conv_w'] = p['ssd_conv_w'][l]
    q['conv_b'] = p['ssd_conv_b'][l:l + 1]
    q['dt_bias'] = _pad_lanes(p['ssd_dt_bias'][l])
    q['a_log'] = _pad_lanes(p['ssd_a_log'][l])
    q['ssd_d'] = _pad_lanes(p['ssd_d'][l])
    q['s5_raw'] = (p['s5_a_re'][l], p['s5_a_im'][l], p['s5_log_step'][l].reshape(16, 1),
                   p['s5_b_re'][l].reshape(16, 1024), p['s5_b_im'][l].reshape(16, 1024))
    q['cre'] = _cmat_embed(p['s5_c_re'][l]).astype(BF16)
    q['cim'] = (-_cmat_embed(p['s5_c_im'][l])).astype(BF16)
    q['s5_d'] = p['s5_d'][l:l + 1]
    q['glu_w'] = p['s5_glu_w'][l].astype(BF16)
    q['glu_b'] = p['s5_glu_b'][l:l + 1]
    q['bw'] = p['branch_norm_w'][l:l + 1]
    return q


def _layer_fwd(h, q):
    sh1, sc1, g1, sh2, sc2, g2 = q['mod']
    t = h.shape[0]
    s = {'h': h}
    s['proj'], s['dtp'], s['u'] = _f_in(h, q['nw1'], sc1, sh1, q['w_main'], q['w_dt'])
    s['ya'], s['yb'] = _f_ab(s['proj'], q['pool_mat'], q['pool_scale'], q['sconv_w'])
    s['yc'], s['ypre'], s['sprev'] = _f_ssd(s['proj'], s['dtp'], q['conv_w'], q['conv_b'], q['dt_bias'], q['a_log'], q['ssd_d'])
    lr, li, bbr, bbi, ars, ais = _s5_prep(*q['s5_raw'])
    s['bmat'] = jnp.concatenate([_bmat_embed(bbr.reshape(16, 64, 16)), _bmat_embed(bbi.reshape(16, 64, 16))],
                                axis=1).astype(BF16)
    s['tables'] = _s5_tables(ars.reshape(1, S5_P), ais.reshape(1, S5_P))
    s['yd'], s['carries'], s['states'] = _f_s5(s['proj'], s['bmat'], q['cre'], q['cim'], s['tables'][0], s['tables'][1],
                                  q['s5_d'], q['glu_w'], q['glu_b'])
    q['w_out'], q['w1'], q['w2'] = q['rest']((s['ya'], s['yc'], s['yd']))
    s['h2'], s['o'], s['cat'] = _f_out(s['ya'], s['yb'], s['yc'], s['yd'], q['bw'], q['w_out'], h, g1)
    h3, s['m'], s['a'], s['v'] = _f_mlp(s['h2'], q['nw2'], sc2, sh2, g2, q['w1'], q['w2'])
    return h3, s


def _layer_bwd(dh3, q, s):
    sh1, sc1, g1, sh2, sc2, g2 = q['mod']
    g = {}
    dv, da, act, dm = _b_mlp(dh3, s['a'], g2, q['w1'], q['w2'])
    g['mlp_w1'] = _tn_matmul(s['v'], da, "dw1", col_major=True)
    g['mlp_w2'] = _tn_matmul(act, dm, "dw2")
    dh2, dsc2, dsh2, dnw2, dg2 = _b_normmod(dv, s['h2'], dh3, s['m'], q['nw2'], sc2, "b_norm_mlp")
    dya, dyb, dyc, dyd, do, dbw = _b_out(dh2, s['ya'], s['yb'], s['yc'], s['yd'], q['bw'], q['w_out'], g1)
    g['w_out'] = _tn_matmul(s['cat'], do, "dwout")
    g['branch_norm_w'] = dbw[0]
    dab, dpm, dps, dsw = _b_ab(s['proj'], dya, dyb, q['pool_mat'], q['pool_scale'], q['sconv_w'])
    g['pool_w'] = _pool_extract(dpm)
    g['pool_scale'] = dps[0]
    g['sconv_w'] = dsw
    dz, dxbc, ddt, dcw, dcb, ddtb, dal, ddk = _b_ssd(s['proj'], s['dtp'], s['ypre'], dyc, s['sprev'], q['conv_w'],
                                                     q['conv_b'], q['dt_bias'], q['a_log'], q['ssd_d'])
    g['ssd_conv_w'] = dcw
    g['ssd_conv_b'] = dcb[0]
    g['ssd_dt_bias'] = ddtb[0, :4]
    g['ssd_a_log'] = dal[0, :4]
    g['ssd_d'] = ddk[0, :4]
    tb = s['tables']
    ds5, dbmat, dcre, dcim, dlam, dd5, dgw, dgb = _b_s5(s['proj'], dyd, s['carries'], s['states'], s['bmat'], q['cre'], q['cim'],
                                                        tb[0], tb[1], q['s5_d'], q['glu_w'], q['glu_b'])
    g['s5_c_re'] = _cmat_extract(dcre)
    g['s5_c_im'] = -_cmat_extract(dcim)
    g['s5_d'] = dd5[0]
    g['s5_glu_w'] = dgw
    g['s5_glu_b'] = dgb[0]
    dbbr = _bmat_extract(dbmat[:, :S5_P]).reshape(16, 1024)
    dbbi = _bmat_extract(dbmat[:, S5_P:]).reshape(16, 1024)
    dar, dai, dls, dbr, dbi = _s5_prep_bwd(*q['s5_raw'], dlam[0].reshape(16, 64), dlam[1].reshape(16, 64), dbbr, dbbi)
    g['s5_a_re'], g['s5_a_im'], g['s5_log_step'] = dar, dai, dls[:, 0]
    g['s5_b_re'], g['s5_b_im'] = dbr.reshape(16, 64, 16), dbi.reshape(16, 64, 16)
    du = _b_in_du(dab, dz, dxbc, ds5, ddt, q['w_main'], q['w_dt'])
    u = s['u']
    pieces = [_tn_matmul(u, dab, "dwin_ab"), _tn_matmul(u, dz, "dwin_z"), _tn_matmul(u, dxbc, "dwin_xbc"),
              _tn_matmul(u, ddt, "dwin_dt")[:, :4], _tn_matmul(u, ds5, "dwin_s5")]
    g['w_in'] = jnp.concatenate(pieces, axis=1)
    dh, dsc1, dsh1, dnw1, dg1 = _b_normmod(du, s['h'], dh2, s['o'], q['nw1'], sc1, "b_norm_mix")
    g['norm_mix_w'] = dnw1[0]
    g['norm_mlp_w'] = dnw2[0]
    dmod = jnp.concatenate([dsh1, dsc1, dg1, dsh2, dsc2, dg2], axis=1)
    return dh, g, dmod


def _local_step(x, tgt, p, mod, w_in_of, rest_of):
    h = x
    qs, saved = [], []
    for l in range(2):
        qs.append(_layer_params(p, l, mod[l], w_in_of(l), functools.partial(rest_of, l)))
        h, s = _layer_fwd(h, qs[l])
        saved.append(s)
    dh, loss, dfw = _b_final(h, tgt, p['final_norm_w'].reshape(1, D))
    grads = [None, None]
    dmods = [None, None]
    for l in (1, 0):
        dh, grads[l], dmods[l] = _layer_bwd(dh, qs[l], saved[l])
    out = {k: jnp.stack([grads[0][k], grads[1][k]]) for k in grads[0]}
    out['final_norm_w'] = dfw[0]
    return loss, dh, out, jnp.concatenate(dmods, axis=0)


def _pack(arrs):
    parts, rows = [], 0
    for a in arrs:
        f = a.reshape(-1).astype(F32)
        pad = (-f.shape[0]) % 1024
        f = jnp.pad(f, (0, pad)) if pad else f
        parts.append(f.reshape(-1, 128))
        rows += parts[-1].shape[0]
    if rows % 256:
        parts.append(jnp.zeros((256 - rows % 256, 128), F32))
    return jnp.concatenate(parts, axis=0)


def _unpack(buf, shapes):
    out, row = [], 0
    for shp in shapes:
        n = int(math.prod(shp)) if len(shp) else 1
        rows = (n + 1023) // 1024 * 8
        out.append(buf[row:row + rows].reshape(-1)[:n].reshape(shp))
        row += rows
    return out


def _shard_of(a, axis, k):
    n = a.shape[axis] // 4
    return lax.dynamic_slice_in_dim(a, k * n, n, axis)


def kernel(x, c, norm_mix_w, norm_mlp_w, ada_w, ada_b, w_in, pool_w, pool_scale, sconv_w, ssd_conv_w, ssd_conv_b, ssd_dt_bias, ssd_a_log, ssd_d, s5_a_re, s5_a_im, s5_log_step, s5_b_re, s5_b_im, s5_c_re, s5_c_im, s5_d, s5_glu_w, s5_glu_b, branch_norm_w, w_out, mlp_w1, mlp_w2, final_norm_w, loss_target, m_norm_mix_w, m_norm_mlp_w, m_ada_w, m_ada_b, m_w_in, m_pool_w, m_pool_scale, m_sconv_w, m_ssd_conv_w, m_ssd_conv_b, m_ssd_dt_bias, m_ssd_a_log, m_ssd_d, m_s5_a_re, m_s5_a_im, m_s5_log_step, m_s5_b_re, m_s5_b_im, m_s5_c_re, m_s5_c_im, m_s5_d, m_s5_glu_w, m_s5_glu_b, m_branch_norm_w, m_w_out, m_mlp_w1, m_mlp_w2, m_final_norm_w, v_norm_mix_w, v_norm_mlp_w, v_ada_w, v_ada_b, v_w_in, v_pool_w, v_pool_scale, v_sconv_w, v_ssd_conv_w, v_ssd_conv_b, v_ssd_dt_bias, v_ssd_a_log, v_ssd_d, v_s5_a_re, v_s5_a_im, v_s5_log_step, v_s5_b_re, v_s5_b_im, v_s5_c_re, v_s5_c_im, v_s5_d, v_s5_glu_w, v_s5_glu_b, v_branch_norm_w, v_w_out, v_mlp_w1, v_mlp_w2, v_final_norm_w):
    loc = locals()
    w = {n: loc[n] for n in WEIGHTS}
    mom = {n: loc['m_' + n] for n in WEIGHTS}
    var = {n: loc['v_' + n] for n in WEIGHTS}
    ix, iy, ic = lax.axis_index("x"), lax.axis_index("y"), lax.axis_index("c")
    chip = 2 * ix + iy
    dev = 4 * ix + 2 * iy + ic

    mine_of = lambda a: lax.dynamic_index_in_dim(a.astype(BF16), ic, axis=0, keepdims=False)
    pad_in = lambda a: jnp.pad(a.reshape(577, D), ((0, WIN_ROWS - 577), (0, 0)))
    shard = jnp.concatenate([pad_in(mine_of(w['w_in'])), mine_of(w['w_out']), mine_of(w['mlp_w1']), mine_of(w['mlp_w2'])], axis=0)
    sems, shard_thru, land, token = _gather_start(shard, "ag_big_start")

    (c_all,) = _exchange([c], EVERYONE, False, "ag_cond", stage=True)
    c_all = c_all.reshape(8, D)
    small_sh = _exchange([w[n] for n in SMALL_SHARDED], CHIPS, False, "ag_small")
    (w_in0,) = _exchange([pad_in(w['w_in'][0].astype(BF16))], CHIPS, False, "ag_win0")
    p = {n: w[n] for n in WEIGHTS if n not in BIG}
    for n, g in zip(SMALL_SHARDED, small_sh):
        ax = SMALL_SHARDED[n]
        p[n] = jnp.concatenate([g[k] for k in range(4)], axis=ax)

    def w_in_full(sh):
        cols = sh[:, :577].reshape(4, D, 577)
        return jnp.concatenate([cols[k] for k in range(4)], axis=1)

    big = {}

    def fetch(after):
        if not big:
            got = _gather_wait(sems, shard_thru, land, after, "ag_big_wait")
            got = lax.dynamic_update_slice(got, shard[None], (chip, 0, 0))
            other = _pair_swap([got.reshape(-1, D)], False, "swap_big")[0].reshape(got.shape)
            big['both'] = [jnp.where(ic == l, got, other) for l in range(2)]
        return big['both']

    def w_in_of(l):
        return w_in_full(w_in0) if l == 0 else w_in_full(fetch(None)[1])

    def rest_of(l, after):
        blk = fetch(after)[l]
        r0 = WIN_ROWS
        w_out_l = blk[:, r0:r0 + 256].reshape(D, D)
        w1_l = jnp.concatenate([blk[k, r0 + 256:r0 + 1280] for k in range(4)], axis=1)
        w2_l = blk[:, r0 + 1280:r0 + 2304].reshape(HID, D)
        return w_out_l, w1_l, w2_l

    ada_b_sh = _shard_of(w['ada_b'], 1, chip).reshape(2, 1, 6 * D // 4)
    mod_sh = _ada_fwd(c_all, w['ada_w'], ada_b_sh)
    (mod_all,) = _exchange([mod_sh], CHIPS, False, "ag_mod", stage=True)
    mine = lax.dynamic_index_in_dim(mod_all, dev, axis=2, keepdims=False)
    mod = jnp.transpose(mine, (1, 0, 2)).reshape(2, 6, D) + token[0, 0]

    loss, grad_x, g, dmod = _local_step(x[0], loss_target[0], p, mod, w_in_of, rest_of)

    (dmod_all,) = _exchange([dmod], EVERYONE, False, "ag_dmod", stage=True)
    dmod_all = jnp.transpose(dmod_all, (1, 0, 2))
    g_ada_w, g_ada_b = _ada_bwd(c_all, _shard_of(dmod_all, 2, chip), dmod_all)

    gw_in = jnp.transpose(g['w_in'].reshape(2, D, 4, 577), (0, 2, 1, 3)).reshape(2, 4, 577, D)
    gw_in = jnp.pad(gw_in, ((0, 0), (0, 0), (0, WIN_ROWS - 577), (0, 0)))
    gw_out = g['w_out'].reshape(2, 4, 256, D)
    gw1 = g['mlp_w1']
    gw2 = g['mlp_w2'].reshape(2, 4, 1024, D)
    gws = [gw_in, gw_out, gw1, gw2]
    got = _pair_swap([a.reshape(2, -1, D) for a in gws], True, "swap_grad")
    layer = ic.astype(jnp.int32).reshape(1)
    pair = [_pair_sum(a, b.reshape(a.shape[1:]), layer, "pair_sum%d" % k, BF16) for k, (a, b) in enumerate(zip(gws, got))]
    quad = _exchange(pair, CHIPS, True, "rs_chips")
    quad = [_sum_lead(a, "rs_chip_sum%d" % k, F32) for k, a in enumerate(quad)]
    other = _pair_swap(quad, False, "swap_red")
    both = [jnp.stack([jnp.where(ic == l, a, b) for l in range(2)]) for a, b in zip(quad, other)]
    both[0] = both[0][:, :577].reshape(2, D, 577)
    red = dict(zip(('w_in', 'w_out', 'mlp_w1', 'mlp_w2'), both))
    red['ada_w'] = g_ada_w

    small_names = [n for n in WEIGHTS if n not in BIG and n != 'ada_b']
    small_shapes = [g[n].shape for n in small_names] + [(1, 1)]
    packed = _pack([g[n] for n in small_names] + [loss])
    (packed_all,) = _exchange([packed], EVERYONE, False, "ag_smallgrad", stage=True)
    summed = _unpack(_sum_lead(packed_all, "smallgrad_sum", F32), small_shapes)
    for n, a in zip(small_names, summed[:-1]):
        red[n] = _shard_of(a, SMALL_SHARDED[n], chip) if n in SMALL_SHARDED else a
    red['ada_b'] = g_ada_b
    loss_out = summed[-1].reshape(())

    delta, new_m, new_v = {}, {}, {}
    for n in BIG:
        delta[n], new_m[n], new_v[n] = _adamw(w[n], red[n], mom[n], var[n], "adamw_" + n)
    rest = [n for n in WEIGHTS if n not in BIG]
    shapes = [w[n].shape for n in rest]
    d_p, m_p, v_p = _adamw(_pack([w[n] for n in rest]), _pack([red[n] for n in rest]), _pack([mom[n] for n in rest]),
                           _pack([var[n] for n in rest]), "adamw_small")
    for n, a, b, cc in zip(rest, _unpack(d_p, shapes), _unpack(m_p, shapes), _unpack(v_p, shapes)):
        delta[n], new_m[n], new_v[n] = a, b, cc

    return (loss_out, grad_x[None], *[red[n] for n in WEIGHTS], *[delta[n] for n in WEIGHTS],
            *[new_m[n] for n in WEIGHTS], *[new_v[n] for n in WEIGHTS])
```

```python
import functools
import math

import jax
import jax.numpy as jnp
from jax import lax
from jax.experimental import pallas as pl
from jax.experimental.pallas import tpu as pltpu

F32 = jnp.float32
BF16 = jnp.bfloat16
HI = lax.Precision.HIGHEST

D = 1024
GW = 256
HID = 4096
EPS = 1e-6
PW = 2304
DTW = 128
SSD_L = 128
SSD_SUB = 2
SSD_SUB_BWD = 1
NH, HP, NS = 4, 64, 128
S5_P = 1024
MESH = pl.DeviceIdType.MESH

ADAM_LR, ADAM_B1, ADAM_B2, ADAM_EPS, ADAM_WD, ADAM_STEP = 0.001, 0.9, 0.999, 1e-08, 0.01, 10

NT = (((1,), (1,)), ((), ()))
TN = (((0,), (0,)), ((), ()))

WEIGHTS = ['norm_mix_w', 'norm_mlp_w', 'ada_w', 'ada_b', 'w_in', 'pool_w', 'pool_scale', 'sconv_w', 'ssd_conv_w',
           'ssd_conv_b', 'ssd_dt_bias', 'ssd_a_log', 'ssd_d', 's5_a_re', 's5_a_im', 's5_log_step', 's5_b_re', 's5_b_im',
           's5_c_re', 's5_c_im', 's5_d', 's5_glu_w', 's5_glu_b', 'branch_norm_w', 'w_out', 'mlp_w1', 'mlp_w2',
           'final_norm_w']
BIG = ('ada_w', 'w_in', 'w_out', 'mlp_w1', 'mlp_w2')
SMALL_SHARDED = {'sconv_w': 2, 'ssd_conv_w': 2, 's5_glu_w': 1}


def _cparams(n_axes, vmem_mb=48):
    return pltpu.CompilerParams(dimension_semantics=("arbitrary",) * n_axes, vmem_limit_bytes=vmem_mb * 1024 * 1024)


def _row(n):
    return pl.BlockSpec((1, n), lambda *_: (0, 0))


def _full(shape):
    nd = len(shape)
    return pl.BlockSpec(tuple(shape), lambda *_: (0,) * nd)


def _dot(a, b, dims=None, prec=None):
    if dims is None:
        dims = (((a.ndim - 1,), (0,)), ((), ()))
    return lax.dot_general(a, b, dims, preferred_element_type=F32, precision=prec)


def _bdot(a, b, dims=None):
    return _dot(a.astype(BF16), b.astype(BF16), dims)


def _sig(x):
    return jax.nn.sigmoid(x)


def _silu(x):
    return x * _sig(x)


def _dsilu(x):
    s = _sig(x)
    return s * (1.0 + x * (1.0 - s))


def _softplus(x):
    return jnp.maximum(x, 0.0) + jnp.log(1.0 + jnp.exp(-jnp.abs(x)))


_GK = math.sqrt(2.0 / math.pi)


def _gelu(x):
    return 0.5 * x * (1.0 + jnp.tanh(_GK * (x + 0.044715 * x * x * x)))


def _dgelu(x):
    th = jnp.tanh(_GK * (x + 0.044715 * x * x * x))
    return 0.5 * (1.0 + th) + 0.5 * x * (1.0 - th * th) * _GK * (1.0 + 3.0 * 0.044715 * x * x)


def _colsum(x):
    return jnp.sum(x, axis=0, keepdims=True)


def _rms(x):
    r = lax.rsqrt(jnp.mean(x * x, axis=-1, keepdims=True) + EPS)
    return r, x * r


def _rms_bwd(r, n, dn):
    return r * (dn - n * jnp.mean(dn * n, axis=-1, keepdims=True))


def _roll(x, k):
    n = x.shape[0]
    k = k % n
    return x if k == 0 else pltpu.roll(x, k, axis=0)


def _tblock(t, want=512):
    return min(t, want)


def _peer(mask):
    x, y, c = lax.axis_index("x"), lax.axis_index("y"), lax.axis_index("c")
    return (x ^ ((mask >> 2) & 1), y ^ ((mask >> 1) & 1), c ^ (mask & 1))


def _group_index(masks):
    x, y, c = lax.axis_index("x"), lax.axis_index("y"), lax.axis_index("c")
    full = 0
    for m in masks:
        full |= m
    bits = [b for b in (4, 2, 1) if full & b]

    def idx(px, py, pc):
        v = {4: px, 2: py, 1: pc}
        out = 0
        for b in bits:
            out = out * 2 + v[b]
        return out

    return idx(x, y, c), [idx(*_peer(m)) for m in masks]


def _exchange(arrs, masks, scatter, name, stage=False):
    n_arr, n_peer, n_grp = len(arrs), len(masks), len(masks) + 1

    def body(*refs):
        ins, outs = refs[:n_arr], refs[n_arr:2 * n_arr]
        send_sems, recv_sems, local_sems = refs[2 * n_arr:]
        me, peer_idx = _group_index(masks)
        copies = []
        for t in range(n_arr):
            src_me = ins[t].at[me] if scatter else ins[t]
            loc = pltpu.make_async_copy(src_me, outs[t].at[me], local_sems.at[t])
            loc.start()
            copies.append(loc)
            for j, m in enumerate(masks):
                src = ins[t].at[peer_idx[j]] if scatter else ins[t]
                cp = pltpu.make_async_remote_copy(src_ref=src, dst_ref=outs[t].at[me], send_sem=send_sems.at[t, j],
                                                  recv_sem=recv_sems.at[t, j], device_id=_peer(m), device_id_type=MESH)
                cp.start()
                copies.append(cp)
        for cp in copies:
            cp.wait()

    hbm = pl.BlockSpec(memory_space=pl.ANY)
    out_shape = [jax.ShapeDtypeStruct((n_grp,) + (a.shape[1:] if scatter else a.shape), a.dtype) for a in arrs]
    src_spec = pl.BlockSpec(memory_space=pltpu.VMEM) if stage else hbm
    outs = pl.pallas_call(
        body, name=name, in_specs=[src_spec] * n_arr, out_specs=[hbm] * n_arr, out_shape=out_shape,
        scratch_shapes=[pltpu.SemaphoreType.DMA((n_arr, n_peer)), pltpu.SemaphoreType.DMA((n_arr, n_peer)),
                        pltpu.SemaphoreType.DMA((n_arr,))],
    )(*arrs)
    return list(outs)


def _gather_copies(src_ref, land_ref, send_sems, recv_sems):
    me, _ = _group_index(CHIPS)
    return [pltpu.make_async_remote_copy(src_ref=src_ref, dst_ref=land_ref.at[me], send_sem=send_sems[j], recv_sem=recv_sems[j],
                                         device_id=_peer(m), device_id_type=MESH) for j, m in enumerate(CHIPS)]


def _gather_start(src, after, name):
    n = len(CHIPS)

    def body(src_ref, land_ref, *rest):
        sems, token = rest[len(after):len(after) + 2 * n], rest[-1]
        for cp in _gather_copies(src_ref, land_ref, sems[:n], sems[n:]):
            cp.start()
        token[...] = jnp.zeros_like(token)

    hbm = pl.BlockSpec(memory_space=pltpu.HBM)
    sem = pl.BlockSpec(memory_space=pltpu.SEMAPHORE)
    land = lax.empty((n + 1,) + src.shape, src.dtype)
    outs = pl.pallas_call(
        body, name=name,
        out_shape=(pltpu.SemaphoreType.DMA(()),) * (2 * n) + (pltpu.HBM(src.shape, src.dtype), pltpu.HBM(land.shape, land.dtype),
                                                              jax.ShapeDtypeStruct((8, 128), F32)),
        in_specs=(hbm, hbm) + (pl.BlockSpec(memory_space=pl.ANY),) * len(after),
        out_specs=(sem,) * (2 * n) + (hbm, hbm, pl.BlockSpec(memory_space=pltpu.VMEM)),
        input_output_aliases={0: 2 * n, 1: 2 * n + 1},
        compiler_params=pltpu.CompilerParams(has_side_effects=pltpu.SideEffectType.DATAFLOW_SIDE_EFFECTING),
    )(pltpu.with_memory_space_constraint(src, pltpu.HBM), pltpu.with_memory_space_constraint(land, pltpu.HBM), *after)
    return outs[:2 * n], outs[2 * n], outs[2 * n + 1], outs[2 * n + 2]


def _gather_wait(sems, src, land, after, name):
    n = len(CHIPS)

    def body(src_ref, land_ref, *rest):
        for cp in _gather_copies(src_ref, land_ref, rest[:n], rest[n:2 * n]):
            cp.wait_send()
            cp.wait_recv()

    hbm = pl.BlockSpec(memory_space=pltpu.HBM)
    sem = pl.BlockSpec(memory_space=pltpu.SEMAPHORE)
    return pl.pallas_call(
        body, name=name, out_shape=(pltpu.HBM(src.shape, src.dtype), pltpu.HBM(land.shape, land.dtype)),
        in_specs=(hbm, hbm) + (sem,) * (2 * n) + (pl.BlockSpec(memory_space=pl.ANY),) * len(after), out_specs=(hbm, hbm),
        input_output_aliases={0: 0, 1: 1},
        compiler_params=pltpu.CompilerParams(has_side_effects=pltpu.SideEffectType.DATAFLOW_SIDE_EFFECTING),
    )(src, land, *sems, *after)[1]


CHIPS = (4, 2, 6)
EVERYONE = (1, 2, 3, 4, 5, 6, 7)
SIBLING = (1,)
SWAP_ROWS = 512
WIN_ROWS = 592


def _pair_swap(arrs, other_layer, name):
    n_arr = len(arrs)
    shapes = [a.shape[-2:] for a in arrs]
    chunks = []
    for t, (rows, _) in enumerate(shapes):
        assert rows % 16 == 0
        for j, r0 in enumerate(range(0, rows, SWAP_ROWS)):
            chunks.append((t, r0, min(SWAP_ROWS, rows - r0), j % 2))

    def body(*refs):
        ins, outs = refs[:n_arr], refs[n_arr:2 * n_arr]
        bufs = refs[2 * n_arr:3 * n_arr]
        load_sems, send_sems, recv_sems = refs[3 * n_arr:]
        sibling = _peer(1)
        c = lax.axis_index("c")

        def load(k):
            t, r0, n, slot = chunks[k]
            src = ins[t].at[1 - c] if other_layer else ins[t]
            return pltpu.make_async_copy(src.at[pl.ds(r0, n)], bufs[t].at[slot, pl.ds(0, n)], load_sems.at[t, slot])

        def send(k):
            t, r0, n, slot = chunks[k]
            return pltpu.make_async_remote_copy(src_ref=bufs[t].at[slot, pl.ds(0, n)], dst_ref=outs[t].at[pl.ds(r0, n)],
                                                send_sem=send_sems.at[t, slot], recv_sem=recv_sems.at[t],
                                                device_id=sibling, device_id_type=MESH)

        in_flight = {}

        def start_load(k):
            key = (chunks[k][0], chunks[k][3])
            if key in in_flight:
                send(in_flight.pop(key)).wait_send()
            load(k).start()

        start_load(0)
        for k in range(len(chunks)):
            load(k).wait()
            if k + 1 < len(chunks):
                start_load(k + 1)
            send(k).start()
            in_flight[(chunks[k][0], chunks[k][3])] = k
        for k in in_flight.values():
            send(k).wait_send()
        for t in range(n_arr):
            pltpu.make_async_remote_copy(src_ref=outs[t], dst_ref=outs[t], send_sem=send_sems.at[t, 0],
                                         recv_sem=recv_sems.at[t], device_id=sibling, device_id_type=MESH).wait_recv()

    hbm = pl.BlockSpec(memory_space=pl.ANY)
    outs = pl.pallas_call(
        body, name=name, in_specs=[hbm] * n_arr, out_specs=[hbm] * n_arr,
        out_shape=[jax.ShapeDtypeStruct(s, a.dtype) for s, a in zip(shapes, arrs)],
        scratch_shapes=[pltpu.VMEM((2, min(SWAP_ROWS, s[0]), s[1]), a.dtype) for s, a in zip(shapes, arrs)]
        + [pltpu.SemaphoreType.DMA((n_arr, 2)), pltpu.SemaphoreType.DMA((n_arr, 2)), pltpu.SemaphoreType.DMA((n_arr,))],
        compiler_params=pltpu.CompilerParams(vmem_limit_bytes=48 * 1024 * 1024),
    )(*arrs)
    return list(outs)


def _sum_lead(a, name, out_dtype):
    n = a.shape[0]
    shape = a.shape[1:]

    def body(a_ref, o_ref):
        acc = a_ref[0].astype(F32)
        for k in range(1, n):
            acc = acc + a_ref[k].astype(F32)
        o_ref[...] = acc.astype(out_dtype)

    if len(shape) == 3:
        blk = (1,) + shape[1:]
        return pl.pallas_call(
            body, name=name, grid=(shape[0],), in_specs=[pl.BlockSpec((n,) + blk, lambda i: (0, i, 0, 0))],
            out_specs=pl.BlockSpec(blk, lambda i: (i, 0, 0)), out_shape=jax.ShapeDtypeStruct(shape, out_dtype),
            compiler_params=_cparams(1),
        )(a)
    rows, cols = shape
    rb = rows
    for cand in (512, 256, 128):
        if rows % cand == 0 and rows > cand:
            rb = cand
            break
    return pl.pallas_call(
        body, name=name, grid=(rows // rb,), in_specs=[pl.BlockSpec((n, rb, cols), lambda i: (0, i, 0))],
        out_specs=pl.BlockSpec((rb, cols), lambda i: (i, 0)), out_shape=jax.ShapeDtypeStruct((rows, cols), out_dtype),
        compiler_params=_cparams(1),
    )(a)


def _pair_sum(g, recv, layer, name, out_dtype):
    _, n, r, c = g.shape

    def body(l_ref, g_ref, r_ref, o_ref):
        o_ref[...] = (g_ref[0].astype(F32) + r_ref[...].astype(F32)).astype(out_dtype)

    return pl.pallas_call(
        body, name=name,
        grid_spec=pltpu.PrefetchScalarGridSpec(
            num_scalar_prefetch=1, grid=(n,),
            in_specs=[pl.BlockSpec((1, 1, r, c), lambda i, l: (l[0], i, 0, 0)), pl.BlockSpec((1, r, c), lambda i, l: (i, 0, 0))],
            out_specs=pl.BlockSpec((1, r, c), lambda i, l: (i, 0, 0))),
        out_shape=jax.ShapeDtypeStruct((n, r, c), out_dtype), compiler_params=_cparams(1),
    )(layer, g, recv)


def _tn_matmul(a, b, name, col_major=False):
    t, k = a.shape
    n = b.shape[1]
    tb = _tblock(t, 1024)
    kb = min(k, 1024)
    nb = min(n, 1024)
    grid = (k // kb, n // nb, t // tb)

    def body(a_ref, b_ref, o_ref):
        @pl.when(pl.program_id(2) == 0)
        def _():
            o_ref[...] = jnp.zeros_like(o_ref)

        acc = _bdot(a_ref[...], b_ref[...], TN)
        if col_major:
            o_ref[0] += acc
        else:
            o_ref[...] += acc

    if col_major:
        out_spec = pl.BlockSpec((1, kb, nb), lambda ki, ni, ti: (ni, ki, 0))
        out_shape = jax.ShapeDtypeStruct((n // nb, k, nb), F32)
    else:
        out_spec = pl.BlockSpec((kb, nb), lambda ki, ni, ti: (ki, ni))
        out_shape = jax.ShapeDtypeStruct((k, n), F32)
    return pl.pallas_call(
        body, name=name, grid=grid,
        in_specs=[pl.BlockSpec((tb, kb), lambda ki, ni, ti: (ti, ki)), pl.BlockSpec((tb, nb), lambda ki, ni, ti: (ti, ni))],
        out_specs=out_spec, out_shape=out_shape, compiler_params=_cparams(3),
    )(a, b)


def _sum_many(arrs, name):
    k = len(arrs)

    def body(*refs):
        for a_ref, o_ref in zip(refs[:k], refs[k:]):
            acc = a_ref[0]
            for j in range(1, a_ref.shape[0]):
                acc = acc + a_ref[j]
            o_ref[...] = acc

    return pl.pallas_call(body, name=name, out_shape=[jax.ShapeDtypeStruct(a.shape[1:], F32) for a in arrs],
                          compiler_params=pltpu.CompilerParams(vmem_limit_bytes=48 * 1024 * 1024))(*arrs)


def _adamw_math(w, g, m, v):
    m2 = ADAM_B1 * m + (1.0 - ADAM_B1) * g
    v2 = ADAM_B2 * v + (1.0 - ADAM_B2) * (g * g)
    m_hat = m2 / (1.0 - ADAM_B1 ** ADAM_STEP)
    v_hat = v2 / (1.0 - ADAM_B2 ** ADAM_STEP)
    return -ADAM_LR * (m_hat / (jnp.sqrt(v_hat) + ADAM_EPS) + ADAM_WD * w), m2, v2


def _adamw_many(ws, gs, ms, vs, name):
    n = len(ws)

    def body(*refs):
        ins, outs = refs[:4 * n], refs[4 * n:]
        for k in range(n):
            res = _adamw_math(ins[k][...], ins[n + k][...], ins[2 * n + k][...], ins[3 * n + k][...])
            for j in range(3):
                outs[3 * k + j][...] = res[j]

    out_shape = []
    for a in ws:
        out_shape += [jax.ShapeDtypeStruct(a.shape, F32)] * 3
    return pl.pallas_call(body, name=name, out_shape=out_shape,
                          compiler_params=pltpu.CompilerParams(vmem_limit_bytes=48 * 1024 * 1024))(*ws, *gs, *ms, *vs)


def _adamw(w, g, m, v, name):
    shape = w.shape
    cols = shape[-1]
    rows = int(math.prod(shape[:-1]))
    rb = rows
    for cand in (256, 128, 64, 32, 16, 8):
        if rows % cand == 0 and rows > cand:
            rb = cand
            break
    bc1 = 1.0 - ADAM_B1 ** ADAM_STEP
    bc2 = 1.0 - ADAM_B2 ** ADAM_STEP

    def body(w_ref, g_ref, m_ref, v_ref, d_ref, nm_ref, nv_ref):
        gg = g_ref[...]
        m2 = ADAM_B1 * m_ref[...] + (1.0 - ADAM_B1) * gg
        v2 = ADAM_B2 * v_ref[...] + (1.0 - ADAM_B2) * (gg * gg)
        m_hat = m2 / bc1
        v_hat = v2 / bc2
        d_ref[...] = -ADAM_LR * (m_hat / (jnp.sqrt(v_hat) + ADAM_EPS) + ADAM_WD * w_ref[...])
        nm_ref[...] = m2
        nv_ref[...] = v2

    spec = pl.BlockSpec((rb, cols), lambda i: (i, 0))
    sds = jax.ShapeDtypeStruct((rows, cols), F32)
    outs = pl.pallas_call(
        body, name=name, grid=(rows // rb,), in_specs=[spec] * 4, out_specs=[spec] * 3, out_shape=[sds] * 3,
        compiler_params=_cparams(1),
    )(*(z.reshape(rows, cols) for z in (w, g, m, v)))
    return tuple(o.reshape(shape) for o in outs)


def _ada_fwd(c_all, ada_w_sh, ada_b_sh):
    s = ada_w_sh.shape[2]
    sb = 512

    def body(c_ref, w_ref, b_ref, o_ref):
        cond = _silu(c_ref[...])
        o_ref[0] = _bdot(cond, w_ref[0]) + b_ref[0]

    return pl.pallas_call(
        body, name="ada_fwd", grid=(2, s // sb),
        in_specs=[_full((8, D)), pl.BlockSpec((1, D, sb), lambda l, j: (l, 0, j)), pl.BlockSpec((1, 1, sb), lambda l, j: (l, 0, j))],
        out_specs=pl.BlockSpec((1, 8, sb), lambda l, j: (l, 0, j)), out_shape=jax.ShapeDtypeStruct((2, 8, s), F32),
        compiler_params=_cparams(2),
    )(c_all, ada_w_sh, ada_b_sh)


def _ada_bwd(c_all, dmod_sh, dmod_all):
    s = dmod_sh.shape[2]
    sb = 512

    def body(c_ref, d_ref, o_ref):
        cond = _silu(c_ref[...])
        o_ref[0] = _bdot(cond, d_ref[0], TN)

    gw = pl.pallas_call(
        body, name="ada_bwd_w", grid=(2, s // sb),
        in_specs=[_full((8, D)), pl.BlockSpec((1, 8, sb), lambda l, j: (l, 0, j))],
        out_specs=pl.BlockSpec((1, D, sb), lambda l, j: (l, 0, j)), out_shape=jax.ShapeDtypeStruct((2, D, s), F32),
        compiler_params=_cparams(2),
    )(c_all, dmod_sh)

    def body_b(d_ref, o_ref):
        acc = d_ref[0, 0:1, :]
        for k in range(1, 8):
            acc = acc + d_ref[0, k:k + 1, :]
        o_ref[0] = acc

    gb = pl.pallas_call(
        body_b, name="ada_bwd_b", grid=(2,), in_specs=[pl.BlockSpec((1, 8, 6 * D), lambda l: (l, 0, 0))],
        out_specs=pl.BlockSpec((1, 1, 6 * D), lambda l: (l, 0, 0)), out_shape=jax.ShapeDtypeStruct((2, 1, 6 * D), F32),
        compiler_params=_cparams(1),
    )(dmod_all)
    return gw, gb.reshape(2, 6 * D)


def _f_in(h, nw, sc, sh, w_main, w_dt):
    t = h.shape[0]
    tb = _tblock(t)

    def body(h_ref, nw_ref, sc_ref, sh_ref, w_ref, wd_ref, p_ref, dt_ref, u_ref):
        _, n = _rms(h_ref[...])
        u = ((n * nw_ref[...]) * (1.0 + sc_ref[...]) + sh_ref[...]).astype(BF16)
        u_ref[...] = u
        p_ref[...] = _dot(u, w_ref[...])
        dt_ref[...] = _dot(u, wd_ref[...])

    return pl.pallas_call(
        body, name="f_in", grid=(t // tb,),
        in_specs=[pl.BlockSpec((tb, D), lambda i: (i, 0)), _row(D), _row(D), _row(D), _full((D, PW)), _full((D, DTW))],
        out_specs=[pl.BlockSpec((tb, PW), lambda i: (i, 0)), pl.BlockSpec((tb, DTW), lambda i: (i, 0)),
                   pl.BlockSpec((tb, D), lambda i: (i, 0))],
        out_shape=[jax.ShapeDtypeStruct((t, PW), F32), jax.ShapeDtypeStruct((t, DTW), F32), jax.ShapeDtypeStruct((t, D), BF16)],
        compiler_params=_cparams(1),
    )(h, nw, sc, sh, w_main, w_dt)


def _b_in_du(dab, dz, dxbc, ds5, ddt, w_main, w_dt):
    t = dab.shape[0]
    tb = _tblock(t)

    def body(a_ref, z_ref, x_ref, s_ref, d_ref, w_ref, wd_ref, o_ref):
        acc = _bdot(a_ref[...], w_ref[:, 0:1024], NT)
        acc += _bdot(z_ref[...], w_ref[:, 1024:1280], NT)
        acc += _bdot(s_ref[...], w_ref[:, 1280:1536], NT)
        acc += _bdot(x_ref[...], w_ref[:, 1536:2304], NT)
        acc += _bdot(d_ref[...], wd_ref[...], NT)
        o_ref[...] = acc

    blk = lambda n: pl.BlockSpec((tb, n), lambda i: (i, 0))
    return pl.pallas_call(
        body, name="b_in_du", grid=(t // tb,),
        in_specs=[blk(1024), blk(256), blk(768), blk(256), blk(DTW), _full((D, PW)), _full((D, DTW))],
        out_specs=blk(D), out_shape=jax.ShapeDtypeStruct((t, D), F32), compiler_params=_cparams(1),
    )(dab, dz, dxbc, ds5, ddt, w_main, w_dt)


def _b_normmod(du, x, dres, gated, nw, sc, name):
    t = x.shape[0]
    tb = _tblock(t)

    def body(du_ref, x_ref, dr_ref, g_ref, nw_ref, sc_ref, dx_ref, dsc_ref, dsh_ref, dnw_ref, dg_ref):
        @pl.when(pl.program_id(0) == 0)
        def _():
            for r in (dsc_ref, dsh_ref, dnw_ref, dg_ref):
                r[...] = jnp.zeros_like(r)

        du_v = du_ref[...]
        r, n = _rms(x_ref[...])
        nwv = nw_ref[...]
        scale = 1.0 + sc_ref[...]
        dsc_ref[...] += _colsum(du_v * (n * nwv))
        dsh_ref[...] += _colsum(du_v)
        dnw_ref[...] += _colsum(du_v * scale * n)
        dres_v = dr_ref[...]
        dg_ref[...] += _colsum(dres_v * g_ref[...])
        dx_ref[...] = dres_v + _rms_bwd(r, n, du_v * scale * nwv)

    blk = pl.BlockSpec((tb, D), lambda i: (i, 0))
    row = jax.ShapeDtypeStruct((1, D), F32)
    return pl.pallas_call(
        body, name=name, grid=(t // tb,), in_specs=[blk, blk, blk, blk, _row(D), _row(D)],
        out_specs=[blk, _row(D), _row(D), _row(D), _row(D)], out_shape=[jax.ShapeDtypeStruct((t, D), F32), row, row, row, row],
        compiler_params=_cparams(1),
    )(du, x, dres, gated, nw, sc)


HALO = 16


def _lane_group(shape):
    return lax.broadcasted_iota(jnp.int32, shape, 1) // 64


def _window_select(g, s2, s4, s8, s16):
    return jnp.where(g == 0, s2, jnp.where(g == 1, s4, jnp.where(g == 2, s8, s16)))


def _pool_count(t0, rows):
    g = _lane_group((rows, GW))
    win = _window_select(g, 2, 4, 8, 16)
    tt = t0 + lax.broadcasted_iota(jnp.int32, (rows, GW), 0)
    return jnp.minimum(tt + 1, win).astype(F32)


def _pool_p(v_ext, t0, tb):
    s2 = v_ext + _roll(v_ext, 1)
    s4 = s2 + _roll(s2, 2)
    s8 = s4 + _roll(s4, 4)
    s16 = s8 + _roll(s8, 8)
    ws = _window_select(_lane_group(v_ext.shape), s2, s4, s8, s16)[HALO:]
    return ws / _pool_count(t0, tb) - v_ext[HALO:]


def _sconv(q_ext, w):
    return (_roll(q_ext, 2) * w[0:1] + _roll(q_ext, 1) * w[1:2] + q_ext * w[2:3])[HALO:]


def _halo_specs(t, tb, cols, col_block):
    per = tb // HALO
    last = t // HALO - 1
    prev = pl.BlockSpec((HALO, cols), lambda i: (jnp.maximum(i * per - 1, 0), col_block))
    nxt = pl.BlockSpec((HALO, cols), lambda i: (jnp.minimum((i + 1) * per, last), col_block))
    return prev, nxt


def _f_ab(proj, pool_mat, pool_scale, sconv_w):
    t = proj.shape[0]
    tb = _tblock(t)
    prev, _ = _halo_specs(t, tb, 1024, 0)

    def body(p_ref, h_ref, pm_ref, ps_ref, sw_ref, ya_ref, yb_ref):
        i = pl.program_id(0)
        halo = jnp.where(i > 0, h_ref[...], 0.0)
        ext = jnp.concatenate([halo, p_ref[...]], axis=0)
        p = _pool_p(ext[:, 0:256], i * tb, tb)
        ya_ref[...] = _bdot(p, pm_ref[...]) * ps_ref[...]
        q_ext = ext[:, 512:768] * ext[:, 768:1024]
        yb_ref[...] = p_ref[:, 256:512] * _sconv(q_ext, sw_ref[...])

    blk = pl.BlockSpec((tb, GW), lambda i: (i, 0))
    sds = jax.ShapeDtypeStruct((t, GW), F32)
    return pl.pallas_call(
        body, name="f_ab", grid=(t // tb,),
        in_specs=[pl.BlockSpec((tb, 1024), lambda i: (i, 0)), prev, _full((GW, GW)), _row(GW), _full((3, GW))],
        out_specs=[blk, blk], out_shape=[sds, sds], compiler_params=_cparams(1),
    )(proj, proj, pool_mat, pool_scale, sconv_w)


def _b_ab(proj, dya, dyb, pool_mat, pool_scale, sconv_w):
    t = proj.shape[0]
    tb = _tblock(t)
    nb = t // tb
    prev, nxt = _halo_specs(t, tb, 1024, 0)
    _, nxt_g = _halo_specs(t, tb, GW, 0)
    n_ext = tb + HALO

    def body(p_ref, hp_ref, hn_ref, da_ref, dan_ref, db_ref, dbn_ref, pm_ref, ps_ref, sw_ref,
             o_ref, dpm_ref, dps_ref, dsw_ref):
        i = pl.program_id(0)

        @pl.when(i == 0)
        def _():
            for r in (dpm_ref, dps_ref, dsw_ref):
                r[...] = jnp.zeros_like(r)

        last = i == nb - 1
        halo = jnp.where(i > 0, hp_ref[...], 0.0)
        main = p_ref[...]
        ext = jnp.concatenate([halo, main], axis=0)
        scale = ps_ref[...]
        pm = pm_ref[...]
        p = _pool_p(ext[:, 0:256], i * tb, tb)
        da = da_ref[...]
        dps_ref[...] += _colsum(da * _bdot(p, pm))
        da_ext = jnp.concatenate([da, jnp.where(last, 0.0, dan_ref[...])], axis=0)
        dys = da_ext * scale
        dpm_ref[...] += _bdot(p, dys[:tb], TN)
        dp = _bdot(dys, pm, NT)
        dpc = dp / _pool_count(i * tb, n_ext)
        a2 = dpc + _roll(dpc, n_ext - 1)
        a4 = a2 + _roll(a2, n_ext - 2)
        a8 = a4 + _roll(a4, n_ext - 4)
        a16 = a8 + _roll(a8, n_ext - 8)
        o_ref[:, 0:256] = (_window_select(_lane_group(dpc.shape), a2, a4, a8, a16) - dp)[:tb]
        w = sw_ref[...]
        gb, gc, hh = main[:, 256:512], main[:, 512:768], main[:, 768:1024]
        q_ext = ext[:, 512:768] * ext[:, 768:1024]
        db = db_ref[...]
        o_ref[:, 256:512] = db * _sconv(q_ext, w)
        gb_next = hn_ref[:, 256:512]
        dconv = jnp.concatenate([db * gb, jnp.where(last, 0.0, dbn_ref[...] * gb_next)], axis=0)
        dq = (dconv * w[2:3] + _roll(dconv, n_ext - 1) * w[1:2] + _roll(dconv, n_ext - 2) * w[0:1])[:tb]
        o_ref[:, 512:768] = dq * hh
        o_ref[:, 768:1024] = dq * gc
        dc = dconv[:tb]
        dsw_ref[0:1, :] += _colsum(dc * _roll(q_ext, 2)[HALO:])
        dsw_ref[1:2, :] += _colsum(dc * _roll(q_ext, 1)[HALO:])
        dsw_ref[2:3, :] += _colsum(dc * q_ext[HALO:])

    blk = pl.BlockSpec((tb, GW), lambda i: (i, 0))
    return pl.pallas_call(
        body, name="b_ab", grid=(nb,),
        in_specs=[pl.BlockSpec((tb, 1024), lambda i: (i, 0)), prev, nxt, blk, nxt_g, blk, nxt_g,
                  _full((GW, GW)), _row(GW), _full((3, GW))],
        out_specs=[pl.BlockSpec((tb, 1024), lambda i: (i, 0)), _full((GW, GW)), _row(GW), _full((3, GW))],
        out_shape=[jax.ShapeDtypeStruct((t, 1024), F32), jax.ShapeDtypeStruct((GW, GW), F32),
                   jax.ShapeDtypeStruct((1, GW), F32), jax.ShapeDtypeStruct((3, GW), F32)],
        compiler_params=_cparams(1),
    )(proj, proj, proj, dya, dya, dyb, dyb, pool_mat, pool_scale, sconv_w)


CH = 8


def _ssd_conv(x, halo, w, b):
    ext = jnp.concatenate([halo, x], axis=0)
    pre = ext * w[3:4] + _roll(ext, 1) * w[2:3] + _roll(ext, 2) * w[1:2] + _roll(ext, 3) * w[0:1] + b
    return pre[CH:], ext


def _ssd_common(dt_raw, dtb, alog):
    ll = dt_raw.shape[0]
    dtv = _softplus(dt_raw + dtb)
    a_row = -jnp.exp(alog)
    r = lax.broadcasted_iota(jnp.int32, (ll, ll), 0)
    c = lax.broadcasted_iota(jnp.int32, (ll, ll), 1)
    tril = (r >= c).astype(F32)
    cs = _dot(tril, dtv * a_row, prec=HI)
    return dtv, a_row, cs, cs.T, r >= c


def _ssd_bc(act_b, g):
    return act_b[:, 256 + NS * g:256 + NS * (g + 1)], act_b[:, 512 + NS * g:512 + NS * (g + 1)]


def _ssd_gmat(act_b):
    return [_dot(_ssd_bc(act_b, g)[1], _ssd_bc(act_b, g)[0], NT) for g in range(2)]


def _ssd_head(h, act, act_b, dtv, cs, cs_t, causal, gmat):
    g = h // 2
    xs = act[:, HP * h:HP * (h + 1)]
    bm, cm = _ssd_bc(act_b, g)
    cs_c = cs[:, h:h + 1]
    cs_r = cs_t[h:h + 1, :]
    mdec = jnp.where(causal, jnp.exp(jnp.minimum(cs_c - cs_r, 0.0)), 0.0)
    sc = gmat[g] * mdec
    dt_c = dtv[:, h:h + 1]
    xdt = xs * dt_c
    e = jnp.exp(cs_c)
    cs_last = cs[SSD_L - 1:SSD_L, h:h + 1]
    wdec = jnp.exp(cs_last - cs_c)
    return xs, bm, cm, cs_c, mdec, sc, dt_c, xdt, e, cs_last, wdec


def _f_ssd(proj, dtp, conv_w, conv_b, dt_bias, a_log, d_skip):
    t = proj.shape[0]
    nc = t // SSD_L
    rows = SSD_SUB * SSD_L
    per = rows // CH

    def body(x_ref, hx_ref, dt_ref, z_ref, cw_ref, cb_ref, dtb_ref, al_ref, dk_ref, y_ref, yp_ref, sp_ref, s_ref):
        i = pl.program_id(0)

        @pl.when(i == 0)
        def _():
            s_ref[...] = jnp.zeros_like(s_ref)

        state = [s_ref[h] for h in range(NH)]
        for sub in range(SSD_SUB):
            r0 = sub * SSD_L
            rs = slice(r0, r0 + SSD_L)
            halo = jnp.where(i > 0, hx_ref[...], 0.0) if sub == 0 else x_ref[r0 - CH:r0, :]
            pre, _ = _ssd_conv(x_ref[rs, :], halo, cw_ref[...], cb_ref[...])
            act = _silu(pre)
            dtv, _, cs, cs_t, causal = _ssd_common(dt_ref[rs, :], dtb_ref[...], al_ref[...])
            gmat = _ssd_gmat(act)
            for h in range(NH):
                xs, bm, cm, _, _, sc, _, xdt, e, cs_last, wdec = _ssd_head(h, act, act, dtv, cs, cs_t, causal, gmat)
                prev = state[h]
                sp_ref[sub, h] = prev
                y = _dot(sc, xdt) + e * _dot(cm, prev, NT) + xs * dk_ref[0:1, h:h + 1]
                yp_ref[rs, HP * h:HP * (h + 1)] = y
                state[h] = prev * jnp.exp(cs_last) + _dot(xdt * wdec, bm, TN)
            y_ref[rs, :] = yp_ref[rs, :] * _silu(z_ref[rs, :])
        for h in range(NH):
            s_ref[h] = state[h]

    blk = pl.BlockSpec((rows, GW), lambda i: (i, 0))
    sds = jax.ShapeDtypeStruct((t, GW), F32)
    return pl.pallas_call(
        body, name="f_ssd", grid=(nc // SSD_SUB,),
        in_specs=[pl.BlockSpec((rows, 768), lambda i: (i, 2)),
                  pl.BlockSpec((CH, 768), lambda i: (jnp.maximum(i * per - 1, 0), 2)),
                  pl.BlockSpec((rows, DTW), lambda i: (i, 0)),
                  pl.BlockSpec((rows, GW), lambda i: (i, 4)),
                  _full((4, 768)), _row(768), _row(DTW), _row(DTW), _row(DTW)],
        out_specs=[blk, blk, pl.BlockSpec((SSD_SUB, NH, HP, NS), lambda i: (i, 0, 0, 0))],
        out_shape=[sds, sds, jax.ShapeDtypeStruct((nc, NH, HP, NS), F32)],
        scratch_shapes=[pltpu.VMEM((NH, HP, NS), F32)], compiler_params=_cparams(1),
    )(proj, proj, dtp, proj, conv_w, conv_b, dt_bias, a_log, d_skip)


def _b_ssd(proj, dtp, ypre, dyc, sprev, conv_w, conv_b, dt_bias, a_log, d_skip):
    t = proj.shape[0]
    nc = t // SSD_L
    steps = nc // SSD_SUB_BWD
    rows = SSD_SUB_BWD * SSD_L
    per = rows // CH
    n_ext = SSD_L + CH

    def chunk(sub, halo, dnext, ds_in, refs):
        (x_ref, dt_ref, z_ref, yp_ref, dy_ref, sp_ref, cw_ref, cb_ref, dtb_ref, al_ref, dk_ref,
         dz_ref, dx_ref, ddt_ref, dact_ref) = refs
        rs = slice(sub * SSD_L, (sub + 1) * SSD_L)
        dact = dact_ref.at[sub]
        w = cw_ref[...]
        pre, ext = _ssd_conv(x_ref[rs, :], halo, w, cb_ref[...])
        act = _silu(pre)
        dt_raw = dt_ref[rs, :]
        dtv, a_row, cs, cs_t, causal = _ssd_common(dt_raw, dtb_ref[...], al_ref[...])
        gmat = _ssd_gmat(act)
        z = z_ref[rs, :]
        dyc_v = dy_ref[rs, :]
        dz_ref[rs, :] = dyc_v * yp_ref[rs, :] * _dsilu(z)
        dy_all = dyc_v * _silu(z)
        lane = lax.broadcasted_iota(jnp.int32, (SSD_L, DTW), 1)
        rowi = lax.broadcasted_iota(jnp.int32, (SSD_L, 1), 0)
        dcs_mat = jnp.zeros((SSD_L, DTW), F32)
        ddtx_mat = jnp.zeros((SSD_L, DTW), F32)
        ddk_row = jnp.zeros((1, DTW), F32)
        lane1 = lax.broadcasted_iota(jnp.int32, (1, DTW), 1)
        dbm = [None, None]
        dcm = [None, None]
        ds_out = []
        for h in range(NH):
            g = h // 2
            xs, bm, cm, _, mdec, sc, dt_c, xdt, e, cs_last, wdec = _ssd_head(h, act, act, dtv, cs, cs_t, causal, gmat)
            dy = dy_all[:, HP * h:HP * (h + 1)]
            prev = sp_ref[sub, h]
            ds = ds_in[h]
            dsc = _dot(dy, xdt, NT)
            q = dsc * sc
            dg = dsc * mdec
            dxdt = _dot(sc, dy, TN)
            dcs = jnp.sum(q, axis=1, keepdims=True) - jnp.sum(q.T, axis=1, keepdims=True)
            dc_h = _dot(dg, bm)
            db_h = _dot(dg, cm, TN)
            cp = _dot(cm, prev, NT)
            dcs += jnp.sum(dy * cp, axis=1, keepdims=True) * e
            ey = e * dy
            dc_h += _dot(ey, prev)
            dprev = _dot(ey, cm, TN)
            elast = jnp.exp(cs_last)
            dprev += ds * elast
            dcs_last = jnp.sum(ds * prev, keepdims=True) * elast
            bds = _dot(bm, ds, NT)
            dxdt += wdec * bds
            db_h += wdec * _dot(xdt, ds)
            dw = jnp.sum(xdt * bds, axis=1, keepdims=True) * wdec
            dcs -= dw
            dcs_last += jnp.sum(dw, keepdims=True)
            dcs += jnp.where(rowi == SSD_L - 1, dcs_last, 0.0)
            ds_out.append(dprev)
            dact[:, HP * h:HP * (h + 1)] = dxdt * dt_c + dy * dk_ref[0:1, h:h + 1]
            dcs_mat = jnp.where(lane == h, dcs, dcs_mat)
            ddtx_mat = jnp.where(lane == h, jnp.sum(dxdt * xs, axis=1, keepdims=True), ddtx_mat)
            ddk_row = jnp.where(lane1 == h, jnp.sum(dy * xs, keepdims=True), ddk_row)
            dbm[g] = db_h if dbm[g] is None else dbm[g] + db_h
            dcm[g] = dc_h if dcm[g] is None else dcm[g] + dc_h
        for g in range(2):
            dact[:, 256 + NS * g:256 + NS * (g + 1)] = dbm[g]
            dact[:, 512 + NS * g:512 + NS * (g + 1)] = dcm[g]
        r2 = lax.broadcasted_iota(jnp.int32, (SSD_L, SSD_L), 0)
        c2 = lax.broadcasted_iota(jnp.int32, (SSD_L, SSD_L), 1)
        dadt = _dot((c2 >= r2).astype(F32), dcs_mat, prec=HI)
        ddt = jnp.where(lane < NH, (dadt * a_row + ddtx_mat) * _sig(dt_raw + dtb_ref[...]), 0.0)
        ddt_ref[rs, :] = ddt
        dpre = dact[...] * _dsilu(pre)
        dcw = jnp.concatenate([_colsum(dpre * _roll(ext, 3 - k)[CH:]) for k in range(4)], axis=0)
        dext = jnp.concatenate([dpre, dnext], axis=0)
        dx_ref[rs, :] = (dext * w[3:4] + _roll(dext, n_ext - 1) * w[2:3] + _roll(dext, n_ext - 2) * w[1:2]
                         + _roll(dext, n_ext - 3) * w[0:1])[:SSD_L]
        acc = (dcw, _colsum(dpre), _colsum(ddt), _colsum(dadt * dtv) * a_row, ddk_row)
        return dpre[0:CH], ds_out, acc

    def body(x_ref, hx_ref, dt_ref, z_ref, yp_ref, dy_ref, sp_ref, cw_ref, cb_ref, dtb_ref, al_ref, dk_ref,
             dz_ref, dx_ref, ddt_ref, dcw_ref, dcb_ref, ddtb_ref, dal_ref, ddk_ref, ds_ref, dnext_ref, dact_ref):
        i = pl.program_id(0)
        acc_refs = (dcw_ref, dcb_ref, ddtb_ref, dal_ref, ddk_ref)

        @pl.when(i == 0)
        def _():
            ds_ref[...] = jnp.zeros_like(ds_ref)
            dnext_ref[...] = jnp.zeros_like(dnext_ref)
            for r in acc_refs:
                r[...] = jnp.zeros_like(r)

        refs = (x_ref, dt_ref, z_ref, yp_ref, dy_ref, sp_ref, cw_ref, cb_ref, dtb_ref, al_ref, dk_ref, dz_ref, dx_ref, ddt_ref,
                dact_ref)
        ds = [ds_ref[h] for h in range(NH)]
        dnext = dnext_ref[...]
        total = None
        for sub in reversed(range(SSD_SUB_BWD)):
            if sub == 0:
                halo = jnp.where(i == steps - 1, 0.0, hx_ref[...])
            else:
                halo = x_ref[sub * SSD_L - CH:sub * SSD_L, :]
            dnext, ds, acc = chunk(sub, halo, dnext, ds, refs)
            total = acc if total is None else tuple(a + b for a, b in zip(total, acc))
        for h in range(NH):
            ds_ref[h] = ds[h]
        dnext_ref[...] = dnext
        for r, v in zip(acc_refs, total):
            r[...] += v

    rev = lambda i: steps - 1 - i
    blk = lambda n, cb=0: pl.BlockSpec((rows, n), lambda i: (rev(i), cb))
    row = lambda n: jax.ShapeDtypeStruct((1, n), F32)
    return pl.pallas_call(
        body, name="b_ssd", grid=(steps,),
        in_specs=[blk(768, 2), pl.BlockSpec((CH, 768), lambda i: (jnp.maximum(rev(i) * per - 1, 0), 2)),
                  blk(DTW), blk(GW, 4), blk(GW), blk(GW), pl.BlockSpec((SSD_SUB_BWD, NH, HP, NS), lambda i: (rev(i), 0, 0, 0)),
                  _full((4, 768)), _row(768), _row(DTW), _row(DTW), _row(DTW)],
        out_specs=[blk(GW), blk(768), blk(DTW), _full((4, 768)), _row(768), _row(DTW), _row(DTW), _row(DTW)],
        out_shape=[jax.ShapeDtypeStruct((t, GW), F32), jax.ShapeDtypeStruct((t, 768), F32), jax.ShapeDtypeStruct((t, DTW), F32),
                   jax.ShapeDtypeStruct((4, 768), F32), row(768), row(DTW), row(DTW), row(DTW)],
        scratch_shapes=[pltpu.VMEM((NH, HP, NS), F32), pltpu.VMEM((CH, 768), F32), pltpu.VMEM((SSD_SUB_BWD, SSD_L, 768), F32)],
        compiler_params=_cparams(1),
    )(proj, proj, dtp, proj, ypre, dyc, sprev, conv_w, conv_b, dt_bias, a_log, d_skip)


def _s5_block(t):
    return min(t, 256)


def _seg_t():
    r = lax.broadcasted_iota(jnp.int32, (64, 1024), 0)
    c = lax.broadcasted_iota(jnp.int32, (64, 1024), 1)
    return (c // 16 == r).astype(F32)


def _s5_prep_math(a_re, a_im, lstep, b_re, b_im):
    step = jnp.exp(lstep)
    ars = a_re * step
    ais = a_im * step
    mag = jnp.exp(ars)
    lr = mag * jnp.cos(ais)
    li = mag * jnp.sin(ais)
    den = a_re * a_re + a_im * a_im
    nr = lr - 1.0
    f_re = (nr * a_re + li * a_im) / den
    f_im = (li * a_re - nr * a_im) / den
    seg = _seg_t()
    fr = _dot(f_re, seg, prec=HI)
    fi = _dot(f_im, seg, prec=HI)
    return lr, li, fr * b_re - fi * b_im, fr * b_im + fi * b_re, ars, ais


def _s5_prep(a_re, a_im, lstep, b_re, b_im):
    def body(ar, ai, ls, br, bi, lr_o, li_o, bbr_o, bbi_o, ars_o, ais_o):
        outs = _s5_prep_math(ar[...], ai[...], ls[...], br[...], bi[...])
        for o, v in zip((lr_o, li_o, bbr_o, bbi_o, ars_o, ais_o), outs):
            o[...] = v

    s64 = jax.ShapeDtypeStruct((16, 64), F32)
    s1k = jax.ShapeDtypeStruct((16, 1024), F32)
    return pl.pallas_call(body, name="s5_prep", out_shape=[s64, s64, s1k, s1k, s64, s64])(a_re, a_im, lstep, b_re, b_im)


def _s5_prep_bwd(a_re, a_im, lstep, b_re, b_im, dlr, dli, dbbr, dbbi):
    def body(ar, ai, ls, br, bi, g0, g1, g2, g3, o0, o1, o2, o3, o4):
        f = lambda *a: _s5_prep_math(*a)[:4]
        _, vjp = jax.vjp(f, ar[...], ai[...], ls[...], br[...], bi[...])
        for o, v in zip((o0, o1, o2, o3, o4), vjp((g0[...], g1[...], g2[...], g3[...]))):
            o[...] = v

    s64 = jax.ShapeDtypeStruct((16, 64), F32)
    s1k = jax.ShapeDtypeStruct((16, 1024), F32)
    return pl.pallas_call(body, name="s5_prep_bwd", out_shape=[s64, s64, jax.ShapeDtypeStruct((16, 1), F32), s1k, s1k])(
        a_re, a_im, lstep, b_re, b_im, dlr, dli, dbbr, dbbi)


SUB = 8


def _s5_tables(ars, ais):
    def body(ar, ai, tr, ti):
        rr = lax.broadcasted_iota(jnp.int32, (8 * SUB, S5_P), 0)
        seg, r = rr // SUB, rr % SUB
        step = jnp.where((seg == 1) | (seg == 4), 1, jnp.where((seg == 2) | (seg == 5), 2, 4))
        n = jnp.where(seg == 0, r + 1, jnp.where(seg == 7, SUB - r, step))
        fwd_gap = jnp.where(seg <= 3, r - step, SUB - step - 1 - r)
        gap = jnp.where((seg == 0) | (seg == 7), 0, fwd_gap)
        nf = n.astype(F32)
        mag = jnp.where(gap >= 0, jnp.exp(nf * ar[...]), 0.0)
        tr[...] = mag * jnp.cos(nf * ai[...])
        ti[...] = mag * jnp.sin(nf * ai[...])

    sds = jax.ShapeDtypeStruct((8 * SUB, S5_P), F32)
    return pl.pallas_call(body, name="s5_tables", out_shape=[sds] * 2)(ars, ais)


def _s5_table(tb_r, tb_i, k):
    return tb_r[SUB * k:SUB * (k + 1), :], tb_i[SUB * k:SUB * (k + 1), :]


def _s5_scan(bu_r, bu_i, tb_r, tb_i, c_r, c_i, lb):
    nt = lb // SUB
    sr, si = bu_r.reshape(nt, SUB, S5_P), bu_i.reshape(nt, SUB, S5_P)
    for j, k in enumerate((1, 2, 4)):
        mr, mi = _s5_table(tb_r, tb_i, 1 + j)
        tr, ti = pltpu.roll(sr, k, axis=1), pltpu.roll(si, k, axis=1)
        sr, si = sr + mr * tr - mi * ti, si + mr * ti + mi * tr
    pr, pi = _s5_table(tb_r, tb_i, 0)
    out_r, out_i = [], []
    for j in range(nt):
        a_r = sr[j] + pr * c_r - pi * c_i
        a_i = si[j] + pr * c_i + pi * c_r
        out_r.append(a_r)
        out_i.append(a_i)
        c_r, c_i = a_r[SUB - 1:SUB], a_i[SUB - 1:SUB]
    return jnp.concatenate(out_r, axis=0), jnp.concatenate(out_i, axis=0)


def _s5_rscan(g_r, g_i, tb_r, tb_i, n_r, n_i, lb):
    nt = lb // SUB
    gr, gi = g_r.reshape(nt, SUB, S5_P), g_i.reshape(nt, SUB, S5_P)
    for j, k in enumerate((1, 2, 4)):
        mr, mi = _s5_table(tb_r, tb_i, 4 + j)
        tr, ti = pltpu.roll(gr, SUB - k, axis=1), pltpu.roll(gi, SUB - k, axis=1)
        gr, gi = gr + mr * tr + mi * ti, gi + mr * ti - mi * tr
    qr, qi = _s5_table(tb_r, tb_i, 7)
    out_r, out_i = [None] * nt, [None] * nt
    for j in reversed(range(nt)):
        a_r = gr[j] + qr * n_r + qi * n_i
        a_i = gi[j] + qr * n_i - qi * n_r
        out_r[j], out_i[j] = a_r, a_i
        n_r, n_i = a_r[0:1], a_i[0:1]
    return jnp.concatenate(out_r, axis=0), jnp.concatenate(out_i, axis=0)


def _s5_y(u, sr, si, cre, cim, dsk):
    return _bdot(sr, cre) + _bdot(si, cim) + dsk * u


def _f_s5(proj, bmat, cre, cim, p_r, p_i, dsk, glu_w, glu_b):
    t = proj.shape[0]
    lb = _s5_block(t)
    nb = t // lb

    def body(u_ref, bm_ref, cr_ref, ci_ref, pr_ref, pi_ref, dk_ref, gw_ref, gb_ref, y_ref, car_ref, s_ref, st_ref):
        @pl.when(pl.program_id(0) == 0)
        def _():
            st_ref[...] = jnp.zeros_like(st_ref)

        u = u_ref[...]
        bu = _bdot(u, bm_ref[...])
        c_r, c_i = st_ref[0:1, 0:S5_P], st_ref[0:1, S5_P:]
        car_ref[0] = st_ref[0:1, :]
        sr, si = _s5_scan(bu[:, :S5_P], bu[:, S5_P:], pr_ref, pi_ref, c_r, c_i, lb)
        st_ref[0:1, 0:S5_P] = sr[lb - 1:lb]
        st_ref[0:1, S5_P:] = si[lb - 1:lb]
        sr_b, si_b = sr.astype(BF16), si.astype(BF16)
        s_ref[:, 0:S5_P] = sr_b
        s_ref[:, S5_P:] = si_b
        gel = _gelu(_s5_y(u, sr_b, si_b, cr_ref[...], ci_ref[...], dk_ref[...]))
        y_ref[...] = gel * _sig(_bdot(gel, gw_ref[...]) + gb_ref[...])

    return pl.pallas_call(
        body, name="f_s5", grid=(nb,),
        in_specs=[pl.BlockSpec((lb, GW), lambda i: (i, 5)),
                  _full((GW, 2 * S5_P)), _full((S5_P, GW)), _full((S5_P, GW)), _full((8 * SUB, S5_P)), _full((8 * SUB, S5_P)),
                  _row(GW), _full((GW, GW)), _row(GW)],
        out_specs=[pl.BlockSpec((lb, GW), lambda i: (i, 0)), pl.BlockSpec((1, 1, 2 * S5_P), lambda i: (i, 0, 0)),
                   pl.BlockSpec((lb, 2 * S5_P), lambda i: (i, 0))],
        out_shape=[jax.ShapeDtypeStruct((t, GW), F32), jax.ShapeDtypeStruct((nb, 1, 2 * S5_P), F32),
                   jax.ShapeDtypeStruct((t, 2 * S5_P), BF16)],
        scratch_shapes=[pltpu.VMEM((8, 2 * S5_P), F32)], compiler_params=_cparams(1),
    )(proj, bmat, cre, cim, p_r, p_i, dsk, glu_w, glu_b)


def _b_s5(proj, dyd, carries, states, bmat, cre, cim, p_r, p_i, dsk, glu_w, glu_b):
    t = proj.shape[0]
    lb = _s5_block(t)
    nb = t // lb

    def body(u_ref, dy_ref, car_ref, s_ref, bm_ref, cr_ref, ci_ref, pr_ref, pi_ref, dk_ref, gw_ref, gb_ref,
             du_ref, dbm_ref, dcr_ref, dci_ref, dlam_ref, ddk_ref, dgw_ref, dgb_ref, gc_ref):
        @pl.when(pl.program_id(0) == 0)
        def _():
            gc_ref[...] = jnp.zeros_like(gc_ref)
            for r in (dbm_ref, dcr_ref, dci_ref, dlam_ref, ddk_ref, dgw_ref, dgb_ref):
                r[...] = jnp.zeros_like(r)

        u = u_ref[...]
        bm = bm_ref[...]
        u_b = u.astype(BF16)
        c_r, c_i = car_ref[0, 0:1, 0:S5_P], car_ref[0, 0:1, S5_P:]
        cre_v, cim_v, dk, gw = cr_ref[...], ci_ref[...], dk_ref[...], gw_ref[...]
        sr_b, si_b = s_ref[:, 0:S5_P], s_ref[:, S5_P:]
        sr, si = sr_b.astype(F32), si_b.astype(F32)
        y = _dot(sr_b, cre_v) + _dot(si_b, cim_v) + dk * u
        gel = _gelu(y)
        gel_b = gel.astype(BF16)
        gate = _sig(_dot(gel_b, gw) + gb_ref[...])
        dout = dy_ref[...]
        t1 = dout * gel * gate * (1.0 - gate)
        t1_b = t1.astype(BF16)
        dgw_ref[...] += _dot(gel_b, t1_b, TN)
        dgb_ref[...] += _colsum(t1)
        dyv = (dout * gate + _dot(t1_b, gw, NT)) * _dgelu(y)
        dyv_b = dyv.astype(BF16)
        ddk_ref[...] += _colsum(dyv * u)
        dcr_ref[...] += _dot(sr_b, dyv_b, TN)
        dci_ref[...] += _dot(si_b, dyv_b, TN)
        gr = _dot(dyv_b, cre_v, NT)
        gi = _dot(dyv_b, cim_v, NT)
        row = lax.broadcasted_iota(jnp.int32, (lb, S5_P), 0)
        n_r, n_i = gc_ref[0:1, 0:S5_P], gc_ref[0:1, S5_P:]
        gr, gi = _s5_rscan(gr, gi, pr_ref, pi_ref, n_r, n_i, lb)
        gc_ref[0:1, 0:S5_P] = gr[0:1]
        gc_ref[0:1, S5_P:] = gi[0:1]
        gcat = jnp.concatenate([gr, gi], axis=1).astype(BF16)
        dbm_ref[...] += _dot(u_b, gcat, TN)
        du_ref[...] = dyv * dk + _dot(gcat, bm, NT)
        spr = jnp.where(row >= 1, _roll(sr, 1), c_r)
        spi = jnp.where(row >= 1, _roll(si, 1), c_i)
        dlam_ref[0:1, :] += _colsum(gr * spr + gi * spi)
        dlam_ref[1:2, :] += _colsum(gi * spr - gr * spi)

    rev = lambda i: nb - 1 - i
    return pl.pallas_call(
        body, name="b_s5", grid=(nb,),
        in_specs=[pl.BlockSpec((lb, GW), lambda i: (rev(i), 5)), pl.BlockSpec((lb, GW), lambda i: (rev(i), 0)),
                  pl.BlockSpec((1, 1, 2 * S5_P), lambda i: (rev(i), 0, 0)), pl.BlockSpec((lb, 2 * S5_P), lambda i: (rev(i), 0)),
                  _full((GW, 2 * S5_P)), _full((S5_P, GW)), _full((S5_P, GW)), _full((8 * SUB, S5_P)), _full((8 * SUB, S5_P)),
                  _row(GW), _full((GW, GW)), _row(GW)],
        out_specs=[pl.BlockSpec((lb, GW), lambda i: (rev(i), 0)), _full((GW, 2 * S5_P)), _full((S5_P, GW)), _full((S5_P, GW)),
                   _full((2, S5_P)), _row(GW), _full((GW, GW)), _row(GW)],
        out_shape=[jax.ShapeDtypeStruct((t, GW), F32), jax.ShapeDtypeStruct((GW, 2 * S5_P), F32),
                   jax.ShapeDtypeStruct((S5_P, GW), F32), jax.ShapeDtypeStruct((S5_P, GW), F32),
                   jax.ShapeDtypeStruct((2, S5_P), F32), jax.ShapeDtypeStruct((1, GW), F32),
                   jax.ShapeDtypeStruct((GW, GW), F32), jax.ShapeDtypeStruct((1, GW), F32)],
        scratch_shapes=[pltpu.VMEM((8, 2 * S5_P), F32)], compiler_params=_cparams(1),
    )(proj, dyd, carries, states, bmat, cre, cim, p_r, p_i, dsk, glu_w, glu_b)


def _group_norm(ys, bw):
    outs, stats = [], []
    for g, y in enumerate(ys):
        r, n = _rms(y)
        stats.append((r, n))
        outs.append(n * bw[:, GW * g:GW * (g + 1)])
    return jnp.concatenate(outs, axis=1), stats


def _f_out(ya, yb, yc, yd, bw, w_out, h, g1):
    t = h.shape[0]
    tb = _tblock(t)

    def body(a_ref, b_ref, c_ref, d_ref, bw_ref, w_ref, h_ref, g_ref, h2_ref, o_ref, cat_ref):
        cat, _ = _group_norm([a_ref[...], b_ref[...], c_ref[...], d_ref[...]], bw_ref[...])
        catb = cat.astype(BF16)
        cat_ref[...] = catb
        o = _dot(catb, w_ref[...])
        o_ref[...] = o
        h2_ref[...] = h_ref[...] + g_ref[...] * o

    yblk = pl.BlockSpec((tb, GW), lambda i: (i, 0))
    blk = pl.BlockSpec((tb, D), lambda i: (i, 0))
    return pl.pallas_call(
        body, name="f_out", grid=(t // tb,), in_specs=[yblk] * 4 + [_row(D), _full((D, D)), blk, _row(D)],
        out_specs=[blk, blk, blk],
        out_shape=[jax.ShapeDtypeStruct((t, D), F32), jax.ShapeDtypeStruct((t, D), F32), jax.ShapeDtypeStruct((t, D), BF16)],
        compiler_params=_cparams(1),
    )(ya, yb, yc, yd, bw, w_out, h, g1)


def _b_out(dh2, ya, yb, yc, yd, bw, w_out, g1):
    t = dh2.shape[0]
    tb = _tblock(t)

    def body(dh_ref, a_ref, b_ref, c_ref, d_ref, bw_ref, w_ref, g_ref, da_ref, db_ref, dc_ref, dd_ref, do_ref, dbw_ref):
        @pl.when(pl.program_id(0) == 0)
        def _():
            dbw_ref[...] = jnp.zeros_like(dbw_ref)

        do = (dh_ref[...] * g_ref[...]).astype(BF16)
        do_ref[...] = do
        dcat = _dot(do, w_ref[...], NT)
        bw_v = bw_ref[...]
        for g, (y_ref, dy_ref) in enumerate(((a_ref, da_ref), (b_ref, db_ref), (c_ref, dc_ref), (d_ref, dd_ref))):
            r, n = _rms(y_ref[...])
            dc = dcat[:, GW * g:GW * (g + 1)]
            dbw_ref[:, GW * g:GW * (g + 1)] += _colsum(dc * n)
            dy_ref[...] = _rms_bwd(r, n, dc * bw_v[:, GW * g:GW * (g + 1)])

    yblk = pl.BlockSpec((tb, GW), lambda i: (i, 0))
    blk = pl.BlockSpec((tb, D), lambda i: (i, 0))
    ysd = jax.ShapeDtypeStruct((t, GW), F32)
    return pl.pallas_call(
        body, name="b_out", grid=(t // tb,), in_specs=[blk] + [yblk] * 4 + [_row(D), _full((D, D)), _row(D)],
        out_specs=[yblk] * 4 + [blk, _row(D)],
        out_shape=[ysd] * 4 + [jax.ShapeDtypeStruct((t, D), BF16), jax.ShapeDtypeStruct((1, D), F32)],
        compiler_params=_cparams(1),
    )(dh2, ya, yb, yc, yd, bw, w_out, g1)


HB = 1024


def _f_mlp(h2, nw, sc, sh, g2, w1, w2):
    t = h2.shape[0]
    tb = _tblock(t)
    nk = HID // HB

    def body(h_ref, nw_ref, sc_ref, sh_ref, g_ref, w1_ref, w2_ref, h3_ref, m_ref, a_ref, v_ref):
        k = pl.program_id(1)

        @pl.when(k == 0)
        def _():
            _, n = _rms(h_ref[...])
            v_ref[...] = ((n * nw_ref[...]) * (1.0 + sc_ref[...]) + sh_ref[...]).astype(BF16)
            m_ref[...] = jnp.zeros_like(m_ref)

        a = _dot(v_ref[...], w1_ref[...])
        a_ref[...] = a
        ra = jnp.maximum(a, 0.0)
        m_ref[...] += _dot((ra * ra).astype(BF16), w2_ref[...])

        @pl.when(k == nk - 1)
        def _():
            h3_ref[...] = h_ref[...] + g_ref[...] * m_ref[...]

    blk = pl.BlockSpec((tb, D), lambda i, k: (i, 0))
    return pl.pallas_call(
        body, name="f_mlp", grid=(t // tb, nk),
        in_specs=[blk, _row(D), _row(D), _row(D), _row(D), pl.BlockSpec((D, HB), lambda i, k: (0, k)),
                  pl.BlockSpec((HB, D), lambda i, k: (k, 0))],
        out_specs=[blk, blk, pl.BlockSpec((tb, HB), lambda i, k: (i, k)), blk],
        out_shape=[jax.ShapeDtypeStruct((t, D), F32), jax.ShapeDtypeStruct((t, D), F32), jax.ShapeDtypeStruct((t, HID), F32),
                   jax.ShapeDtypeStruct((t, D), BF16)],
        compiler_params=_cparams(2),
    )(h2, nw, sc, sh, g2, w1, w2)


def _b_mlp(dh3, a, g2, w1, w2):
    t = dh3.shape[0]
    tb = _tblock(t)
    nk = HID // HB

    def body(dh_ref, a_ref, g_ref, w1_ref, w2_ref, dv_ref, da_ref, act_ref, dm_ref):
        k = pl.program_id(1)
        dm = (dh_ref[...] * g_ref[...]).astype(BF16)

        @pl.when(k == 0)
        def _():
            dm_ref[...] = dm
            dv_ref[...] = jnp.zeros_like(dv_ref)

        ra = jnp.maximum(a_ref[...], 0.0)
        act_ref[...] = (ra * ra).astype(BF16)
        da = (_dot(dm, w2_ref[...], NT) * (2.0 * ra)).astype(BF16)
        da_ref[...] = da
        dv_ref[...] += _dot(da, w1_ref[...], NT)

    blk = pl.BlockSpec((tb, D), lambda i, k: (i, 0))
    hblk = pl.BlockSpec((tb, HB), lambda i, k: (i, k))
    return pl.pallas_call(
        body, name="b_mlp", grid=(t // tb, nk),
        in_specs=[blk, hblk, _row(D), pl.BlockSpec((D, HB), lambda i, k: (0, k)), pl.BlockSpec((HB, D), lambda i, k: (k, 0))],
        out_specs=[blk, hblk, hblk, blk],
        out_shape=[jax.ShapeDtypeStruct((t, D), F32), jax.ShapeDtypeStruct((t, HID), BF16), jax.ShapeDtypeStruct((t, HID), BF16),
                   jax.ShapeDtypeStruct((t, D), BF16)],
        compiler_params=_cparams(2),
    )(dh3, a, g2, w1, w2)


def _b_final(h, tgt, fw):
    t = h.shape[0]
    tb = _tblock(t)

    def body(h_ref, t_ref, w_ref, dh_ref, loss_ref, dfw_ref):
        @pl.when(pl.program_id(0) == 0)
        def _():
            loss_ref[...] = jnp.zeros_like(loss_ref)
            dfw_ref[...] = jnp.zeros_like(dfw_ref)

        r, n = _rms(h_ref[...])
        wv = w_ref[...]
        err = n * wv - t_ref[...]
        loss_ref[...] += jnp.sum(err * err, keepdims=True) * (0.5 / D)
        dy = err * (1.0 / D)
        dfw_ref[...] += _colsum(dy * n)
        dh_ref[...] = _rms_bwd(r, n, dy * wv)

    blk = pl.BlockSpec((tb, D), lambda i: (i, 0))
    return pl.pallas_call(
        body, name="b_final", grid=(t // tb,), in_specs=[blk, blk, _row(D)], out_specs=[blk, _row(1), _row(D)],
        out_shape=[jax.ShapeDtypeStruct((t, D), F32), jax.ShapeDtypeStruct((1, 1), F32), jax.ShapeDtypeStruct((1, D), F32)],
        compiler_params=_cparams(1),
    )(h, tgt, fw)


_EYE16 = None


def _eye(n):
    return jnp.eye(n, dtype=F32)


def _pool_embed(pool_w):
    return jnp.einsum('gcd,gk->gckd', pool_w, _eye(4)).reshape(GW, GW)


def _pool_extract(m):
    return jnp.einsum('gcgd->gcd', m.reshape(4, 64, 4, 64))


def _bmat_embed(bb):
    return jnp.einsum('gph,gk->ghkp', bb, _eye(16)).reshape(GW, S5_P)


def _bmat_extract(m):
    return jnp.einsum('ghgp->gph', m.reshape(16, 16, 16, 64))


def _cmat_embed(cc):
    return jnp.einsum('ghp,gk->kpgh', cc, _eye(16)).reshape(S5_P, GW)


def _cmat_extract(m):
    return jnp.einsum('gpgh->ghp', m.reshape(16, 64, 16, 16))


def _pad_lanes(v, n=DTW):
    return jnp.pad(v.reshape(1, -1), ((0, 0), (0, n - v.shape[-1])))


def _w_in_layout(w_in):
    w_main = jnp.concatenate([w_in[:, :1280], w_in[:, 2052:2308], w_in[:, 1280:2048]], axis=1)
    return w_main, jnp.pad(w_in[:, 2048:2052], ((0, 0), (0, DTW - 4)))


def _layer_params(p, l, mod, w_in, rest):
    q = {'rest': rest}
    q['mod'] = [mod[k:k + 1] for k in range(6)]
    q['nw1'] = p['norm_mix_w'][l:l + 1]
    q['nw2'] = p['norm_mlp_w'][l:l + 1]
    q['w_main'], q['w_dt'] = _w_in_layout(w_in)
    q['pool_mat'] = _pool_embed(p['pool_w'][l]).astype(BF16)
    q['pool_scale'] = p['pool_scale'][l:l + 1]
    q['sconv_w'] = p['sconv_w'][l]
    q['conv_w'] = p['ssd_conv_w'][l]
    q['conv_b'] = p['ssd_conv_b'][l:l + 1]
    q['dt_bias'] = _pad_lanes(p['ssd_dt_bias'][l])
    q['a_log'] = _pad_lanes(p['ssd_a_log'][l])
    q['ssd_d'] = _pad_lanes(p['ssd_d'][l])
    q['s5_raw'] = (p['s5_a_re'][l], p['s5_a_im'][l], p['s5_log_step'][l].reshape(16, 1),
                   p['s5_b_re'][l].reshape(16, 1024), p['s5_b_im'][l].reshape(16, 1024))
    q['cre'] = _cmat_embed(p['s5_c_re'][l]).astype(BF16)
    q['cim'] = (-_cmat_embed(p['s5_c_im'][l])).astype(BF16)
    q['s5_d'] = p['s5_d'][l:l + 1]
    q['glu_w'] = p['s5_glu_w'][l].astype(BF16)
    q['glu_b'] = p['s5_glu_b'][l:l + 1]
    q['bw'] = p['branch_norm_w'][l:l + 1]
    return q


def _layer_fwd(h, q):
    sh1, sc1, g1, sh2, sc2, g2 = q['mod']
    t = h.shape[0]
    s = {'h': h}
    s['proj'], s['dtp'], s['u'] = _f_in(h, q['nw1'], sc1, sh1, q['w_main'], q['w_dt'])
    s['ya'], s['yb'] = _f_ab(s['proj'], q['pool_mat'], q['pool_scale'], q['sconv_w'])
    s['yc'], s['ypre'], s['sprev'] = _f_ssd(s['proj'], s['dtp'], q['conv_w'], q['conv_b'], q['dt_bias'], q['a_log'], q['ssd_d'])
    lr, li, bbr, bbi, ars, ais = _s5_prep(*q['s5_raw'])
    s['bmat'] = jnp.concatenate([_bmat_embed(bbr.reshape(16, 64, 16)), _bmat_embed(bbi.reshape(16, 64, 16))],
                                axis=1).astype(BF16)
    s['tables'] = _s5_tables(ars.reshape(1, S5_P), ais.reshape(1, S5_P))
    s['yd'], s['carries'], s['states'] = _f_s5(s['proj'], s['bmat'], q['cre'], q['cim'], s['tables'][0], s['tables'][1],
                                  q['s5_d'], q['glu_w'], q['glu_b'])
    q['w_out'], q['w1'], q['w2'] = q['rest']((s['ya'], s['yc'], s['yd']))
    s['h2'], s['o'], s['cat'] = _f_out(s['ya'], s['yb'], s['yc'], s['yd'], q['bw'], q['w_out'], h, g1)
    h3, s['m'], s['a'], s['v'] = _f_mlp(s['h2'], q['nw2'], sc2, sh2, g2, q['w1'], q['w2'])
    return h3, s


def _layer_bwd(dh3, q, s):
    sh1, sc1, g1, sh2, sc2, g2 = q['mod']
    g = {}
    dv, da, act, dm = _b_mlp(dh3, s['a'], g2, q['w1'], q['w2'])
    g['mlp_w1'] = _tn_matmul(s['v'], da, "dw1", col_major=True)
    g['mlp_w2'] = _tn_matmul(act, dm, "dw2")
    dh2, dsc2, dsh2, dnw2, dg2 = _b_normmod(dv, s['h2'], dh3, s['m'], q['nw2'], sc2, "b_norm_mlp")
    dya, dyb, dyc, dyd, do, dbw = _b_out(dh2, s['ya'], s['yb'], s['yc'], s['yd'], q['bw'], q['w_out'], g1)
    g['w_out'] = _tn_matmul(s['cat'], do, "dwout")
    g['branch_norm_w'] = dbw[0]
    dab, dpm, dps, dsw = _b_ab(s['proj'], dya, dyb, q['pool_mat'], q['pool_scale'], q['sconv_w'])
    g['pool_w'] = _pool_extract(dpm)
    g['pool_scale'] = dps[0]
    g['sconv_w'] = dsw
    dz, dxbc, ddt, dcw, dcb, ddtb, dal, ddk = _b_ssd(s['proj'], s['dtp'], s['ypre'], dyc, s['sprev'], q['conv_w'],
                                                     q['conv_b'], q['dt_bias'], q['a_log'], q['ssd_d'])
    g['ssd_conv_w'] = dcw
    g['ssd_conv_b'] = dcb[0]
    g['ssd_dt_bias'] = ddtb[0, :4]
    g['ssd_a_log'] = dal[0, :4]
    g['ssd_d'] = ddk[0, :4]
    tb = s['tables']
    ds5, dbmat, dcre, dcim, dlam, dd5, dgw, dgb = _b_s5(s['proj'], dyd, s['carries'], s['states'], s['bmat'], q['cre'], q['cim'],
                                                        tb[0], tb[1], q['s5_d'], q['glu_w'], q['glu_b'])
    g['s5_c_re'] = _cmat_extract(dcre)
    g['s5_c_im'] = -_cmat_extract(dcim)
    g['s5_d'] = dd5[0]
    g['s5_glu_w'] = dgw
    g['s5_glu_b'] = dgb[0]
    dbbr = _bmat_extract(dbmat[:, :S5_P]).reshape(16, 1024)
    dbbi = _bmat_extract(dbmat[:, S5_P:]).reshape(16, 1024)
    dar, dai, dls, dbr, dbi = _s5_prep_bwd(*q['s5_raw'], dlam[0].reshape(16, 64), dlam[1].reshape(16, 64), dbbr, dbbi)
    g['s5_a_re'], g['s5_a_im'], g['s5_log_step'] = dar, dai, dls[:, 0]
    g['s5_b_re'], g['s5_b_im'] = dbr, dbi
    du = _b_in_du(dab, dz, dxbc, ds5, ddt, q['w_main'], q['w_dt'])
    u = s['u']
    pieces = [_tn_matmul(u, dab, "dwin_ab"), _tn_matmul(u, dz, "dwin_z"), _tn_matmul(u, dxbc, "dwin_xbc"),
              _tn_matmul(u, ddt, "dwin_dt")[:, :4], _tn_matmul(u, ds5, "dwin_s5")]
    g['w_in'] = jnp.concatenate(pieces, axis=1)
    dh, dsc1, dsh1, dnw1, dg1 = _b_normmod(du, s['h'], dh2, s['o'], q['nw1'], sc1, "b_norm_mix")
    g['norm_mix_w'] = dnw1[0]
    g['norm_mlp_w'] = dnw2[0]
    dmod = jnp.concatenate([dsh1, dsc1, dg1, dsh2, dsc2, dg2], axis=1)
    return dh, g, dmod


def _local_step(x, tgt, p, mod, w_in_of, rest_of):
    h = x
    qs, saved = [], []
    for l in range(2):
        qs.append(_layer_params(p, l, mod[l], w_in_of(l), functools.partial(rest_of, l)))
        h, s = _layer_fwd(h, qs[l])
        saved.append(s)
    dh, loss, dfw = _b_final(h, tgt, p['final_norm_w'].reshape(1, D))
    grads = [None, None]
    dmods = [None, None]
    for l in (1, 0):
        dh, grads[l], dmods[l] = _layer_bwd(dh, qs[l], saved[l])
    out = {k: jnp.stack([grads[0][k], grads[1][k]]) for k in grads[0]}
    out['final_norm_w'] = dfw[0]
    return loss, dh, out, jnp.concatenate(dmods, axis=0)


def _pack(arrs):
    parts, rows = [], 0
    for a in arrs:
        f = a.reshape(-1).astype(F32)
        pad = (-f.shape[0]) % 1024
        f = jnp.pad(f, (0, pad)) if pad else f
        parts.append(f.reshape(-1, 128))
        rows += parts[-1].shape[0]
    if rows % 256:
        parts.append(jnp.zeros((256 - rows % 256, 128), F32))
    return jnp.concatenate(parts, axis=0)


def _unpack(buf, shapes):
    out, row = [], 0
    for shp in shapes:
        n = int(math.prod(shp)) if len(shp) else 1
        rows = (n + 1023) // 1024 * 8
        out.append(buf[row:row + rows].reshape(-1)[:n].reshape(shp))
        row += rows
    return out


def _shard_of(a, axis, k):
    n = a.shape[axis] // 4
    return lax.dynamic_slice_in_dim(a, k * n, n, axis)


def kernel(x, c, norm_mix_w, norm_mlp_w, ada_w, ada_b, w_in, pool_w, pool_scale, sconv_w, ssd_conv_w, ssd_conv_b, ssd_dt_bias, ssd_a_log, ssd_d, s5_a_re, s5_a_im, s5_log_step, s5_b_re, s5_b_im, s5_c_re, s5_c_im, s5_d, s5_glu_w, s5_glu_b, branch_norm_w, w_out, mlp_w1, mlp_w2, final_norm_w, loss_target, m_norm_mix_w, m_norm_mlp_w, m_ada_w, m_ada_b, m_w_in, m_pool_w, m_pool_scale, m_sconv_w, m_ssd_conv_w, m_ssd_conv_b, m_ssd_dt_bias, m_ssd_a_log, m_ssd_d, m_s5_a_re, m_s5_a_im, m_s5_log_step, m_s5_b_re, m_s5_b_im, m_s5_c_re, m_s5_c_im, m_s5_d, m_s5_glu_w, m_s5_glu_b, m_branch_norm_w, m_w_out, m_mlp_w1, m_mlp_w2, m_final_norm_w, v_norm_mix_w, v_norm_mlp_w, v_ada_w, v_ada_b, v_w_in, v_pool_w, v_pool_scale, v_sconv_w, v_ssd_conv_w, v_ssd_conv_b, v_ssd_dt_bias, v_ssd_a_log, v_ssd_d, v_s5_a_re, v_s5_a_im, v_s5_log_step, v_s5_b_re, v_s5_b_im, v_s5_c_re, v_s5_c_im, v_s5_d, v_s5_glu_w, v_s5_glu_b, v_branch_norm_w, v_w_out, v_mlp_w1, v_mlp_w2, v_final_norm_w):
    loc = locals()
    w = {n: loc[n] for n in WEIGHTS}
    mom = {n: loc['m_' + n] for n in WEIGHTS}
    var = {n: loc['v_' + n] for n in WEIGHTS}
    ix, iy, ic = lax.axis_index("x"), lax.axis_index("y"), lax.axis_index("c")
    chip = 2 * ix + iy
    dev = 4 * ix + 2 * iy + ic

    mine_of = lambda a: lax.dynamic_index_in_dim(a.astype(BF16), ic, axis=0, keepdims=False)
    pad_in = lambda a: jnp.pad(a.reshape(577, D), ((0, WIN_ROWS - 577), (0, 0)))
    shard = jnp.concatenate([pad_in(mine_of(w['w_in'])), mine_of(w['w_out']), mine_of(w['mlp_w1']), mine_of(w['mlp_w2'])], axis=0)

    (c_all,) = _exchange([c], EVERYONE, False, "ag_cond", stage=True)
    c_all = c_all.reshape(8, D)
    small_sh = _exchange([w[n] for n in SMALL_SHARDED], CHIPS, False, "ag_small")
    (w_in0,) = _exchange([pad_in(w['w_in'][0].astype(BF16))], CHIPS, False, "ag_win0")
    p = {n: w[n] for n in WEIGHTS if n not in BIG}
    for n, g in zip(SMALL_SHARDED, small_sh):
        ax = SMALL_SHARDED[n]
        p[n] = jnp.concatenate([g[k] for k in range(4)], axis=ax)

    def w_in_full(sh):
        cols = sh[:, :577].reshape(4, D, 577)
        return jnp.concatenate([cols[k] for k in range(4)], axis=1)

    big = {}

    def fetch(after):
        if not big:
            got = _gather_wait(sems, shard_thru, land, after, "ag_big_wait")
            got = lax.dynamic_update_slice(got, shard[None], (chip, 0, 0))
            other = _pair_swap([got.reshape(-1, D)], False, "swap_big")[0].reshape(got.shape)
            big['both'] = [jnp.where(ic == l, got, other) for l in range(2)]
        return big['both']

    def w_in_of(l):
        return w_in_full(w_in0) if l == 0 else w_in_full(fetch(None)[1])

    def rest_of(l, after):
        blk = fetch(after)[l]
        r0 = WIN_ROWS
        w_out_l = blk[:, r0:r0 + 256].reshape(D, D)
        w1_l = jnp.concatenate([blk[k, r0 + 256:r0 + 1280] for k in range(4)], axis=1)
        w2_l = blk[:, r0 + 1280:r0 + 2304].reshape(HID, D)
        return w_out_l, w1_l, w2_l

    ada_b_sh = _shard_of(w['ada_b'], 1, chip).reshape(2, 1, 6 * D // 4)
    mod_sh = _ada_fwd(c_all, w['ada_w'], ada_b_sh)
    (mod_all,) = _exchange([mod_sh], CHIPS, False, "ag_mod", stage=True)
    mine = lax.dynamic_index_in_dim(mod_all, dev, axis=2, keepdims=False)
    sems, shard_thru, land, token = _gather_start(shard, [mod_all, w_in0] + small_sh, "ag_big_start")
    mod = jnp.transpose(mine, (1, 0, 2)).reshape(2, 6, D) + token[0, 0]

    loss, grad_x, g, dmod = _local_step(x[0], loss_target[0], p, mod, w_in_of, rest_of)

    (dmod_all,) = _exchange([dmod], EVERYONE, False, "ag_dmod", stage=True)
    dmod_all = jnp.transpose(dmod_all, (1, 0, 2))
    g_ada_w, g_ada_b = _ada_bwd(c_all, _shard_of(dmod_all, 2, chip), dmod_all)

    gw_in = jnp.transpose(g['w_in'].reshape(2, D, 4, 577), (0, 2, 1, 3)).reshape(2, 4, 577, D)
    gw_in = jnp.pad(gw_in, ((0, 0), (0, 0), (0, WIN_ROWS - 577), (0, 0)))
    gw_out = g['w_out'].reshape(2, 4, 256, D)
    gw1 = g['mlp_w1']
    gw2 = g['mlp_w2'].reshape(2, 4, 1024, D)
    gws = [gw_in, gw_out, gw1, gw2]
    got = _pair_swap([a.reshape(2, -1, D) for a in gws], True, "swap_grad")
    layer = ic.astype(jnp.int32).reshape(1)
    pair = [_pair_sum(a, b.reshape(a.shape[1:]), layer, "pair_sum%d" % k, BF16) for k, (a, b) in enumerate(zip(gws, got))]
    quad = _exchange(pair, CHIPS, True, "rs_chips")
    quad = [_sum_lead(a, "rs_chip_sum%d" % k, F32) for k, a in enumerate(quad)]
    other = _pair_swap(quad, False, "swap_red")
    both = [jnp.stack([jnp.where(ic == l, a, b) for l in range(2)]) for a, b in zip(quad, other)]
    both[0] = both[0][:, :577].reshape(2, D, 577)
    red = dict(zip(('w_in', 'w_out', 'mlp_w1', 'mlp_w2'), both))
    red['ada_w'] = g_ada_w

    small_names = [n for n in WEIGHTS if n not in BIG and n != 'ada_b']
    gathered = _exchange([g[n] for n in small_names] + [loss], EVERYONE, False, "ag_smallgrad", stage=True)
    summed = _sum_many(gathered, "smallgrad_sum")
    for n, a in zip(small_names, summed[:-1]):
        a = a.reshape(w[n].shape) if n in ('s5_b_re', 's5_b_im') else a
        red[n] = _shard_of(a, SMALL_SHARDED[n], chip) if n in SMALL_SHARDED else a
    red['ada_b'] = g_ada_b
    loss_out = summed[-1].reshape(())

    delta, new_m, new_v = {}, {}, {}
    for n in BIG:
        delta[n], new_m[n], new_v[n] = _adamw(w[n], red[n], mom[n], var[n], "adamw_" + n)
    rest = [n for n in WEIGHTS if n not in BIG]
    outs = _adamw_many([w[n] for n in rest], [red[n] for n in rest], [mom[n] for n in rest], [var[n] for n in rest],
                       "adamw_small")
    for k, n in enumerate(rest):
        delta[n], new_m[n], new_v[n] = outs[3 * k], outs[3 * k + 1], outs[3 * k + 2]

    return (loss_out, grad_x[None], *[red[n] for n in WEIGHTS], *[delta[n] for n in WEIGHTS],
            *[new_m[n] for n in WEIGHTS], *[new_v[n] for n in WEIGHTS])
```

```python
import functools
import math

import jax
import jax.numpy as jnp
from jax import lax
from jax.experimental import pallas as pl
from jax.experimental.pallas import tpu as pltpu

F32 = jnp.float32
BF16 = jnp.bfloat16
HI = lax.Precision.HIGHEST

D = 1024
GW = 256
HID = 4096
EPS = 1e-6
PW = 2304
DTW = 128
SSD_L = 128
SSD_SUB = 2
SSD_SUB_BWD = 1
NH, HP, NS = 4, 64, 128
S5_P = 1024
MESH = pl.DeviceIdType.MESH

ADAM_LR, ADAM_B1, ADAM_B2, ADAM_EPS, ADAM_WD, ADAM_STEP = 0.001, 0.9, 0.999, 1e-08, 0.01, 10

NT = (((1,), (1,)), ((), ()))
TN = (((0,), (0,)), ((), ()))

WEIGHTS = ['norm_mix_w', 'norm_mlp_w', 'ada_w', 'ada_b', 'w_in', 'pool_w', 'pool_scale', 'sconv_w', 'ssd_conv_w',
           'ssd_conv_b', 'ssd_dt_bias', 'ssd_a_log', 'ssd_d', 's5_a_re', 's5_a_im', 's5_log_step', 's5_b_re', 's5_b_im',
           's5_c_re', 's5_c_im', 's5_d', 's5_glu_w', 's5_glu_b', 'branch_norm_w', 'w_out', 'mlp_w1', 'mlp_w2',
           'final_norm_w']
BIG = ('ada_w', 'w_in', 'w_out', 'mlp_w1', 'mlp_w2')
SMALL_SHARDED = {'sconv_w': 2, 'ssd_conv_w': 2, 's5_glu_w': 1}


def _cparams(n_axes, vmem_mb=48):
    return pltpu.CompilerParams(dimension_semantics=("arbitrary",) * n_axes, vmem_limit_bytes=vmem_mb * 1024 * 1024)


def _row(n):
    return pl.BlockSpec((1, n), lambda *_: (0, 0))


def _full(shape):
    nd = len(shape)
    return pl.BlockSpec(tuple(shape), lambda *_: (0,) * nd)


def _dot(a, b, dims=None, prec=None):
    if dims is None:
        dims = (((a.ndim - 1,), (0,)), ((), ()))
    return lax.dot_general(a, b, dims, preferred_element_type=F32, precision=prec)


def _bdot(a, b, dims=None):
    return _dot(a.astype(BF16), b.astype(BF16), dims)


def _sig(x):
    return jax.nn.sigmoid(x)


def _silu(x):
    return x * _sig(x)


def _dsilu(x):
    s = _sig(x)
    return s * (1.0 + x * (1.0 - s))


def _softplus(x):
    return jnp.maximum(x, 0.0) + jnp.log(1.0 + jnp.exp(-jnp.abs(x)))


_GK = math.sqrt(2.0 / math.pi)


def _gelu(x):
    return 0.5 * x * (1.0 + jnp.tanh(_GK * (x + 0.044715 * x * x * x)))


def _dgelu(x):
    th = jnp.tanh(_GK * (x + 0.044715 * x * x * x))
    return 0.5 * (1.0 + th) + 0.5 * x * (1.0 - th * th) * _GK * (1.0 + 3.0 * 0.044715 * x * x)


def _colsum(x):
    return jnp.sum(x, axis=0, keepdims=True)


def _rms(x):
    r = lax.rsqrt(jnp.mean(x * x, axis=-1, keepdims=True) + EPS)
    return r, x * r


def _rms_bwd(r, n, dn):
    return r * (dn - n * jnp.mean(dn * n, axis=-1, keepdims=True))


def _roll(x, k):
    n = x.shape[0]
    k = k % n
    return x if k == 0 else pltpu.roll(x, k, axis=0)


def _tblock(t, want=512):
    return min(t, want)


def _peer(mask):
    x, y, c = lax.axis_index("x"), lax.axis_index("y"), lax.axis_index("c")
    return (x ^ ((mask >> 2) & 1), y ^ ((mask >> 1) & 1), c ^ (mask & 1))


def _group_index(masks):
    x, y, c = lax.axis_index("x"), lax.axis_index("y"), lax.axis_index("c")
    full = 0
    for m in masks:
        full |= m
    bits = [b for b in (4, 2, 1) if full & b]

    def idx(px, py, pc):
        v = {4: px, 2: py, 1: pc}
        out = 0
        for b in bits:
            out = out * 2 + v[b]
        return out

    return idx(x, y, c), [idx(*_peer(m)) for m in masks]


def _exchange(arrs, masks, scatter, name, stage=False):
    n_arr, n_peer, n_grp = len(arrs), len(masks), len(masks) + 1

    def body(*refs):
        ins, outs = refs[:n_arr], refs[n_arr:2 * n_arr]
        send_sems, recv_sems, local_sems = refs[2 * n_arr:]
        me, peer_idx = _group_index(masks)
        copies = []
        for t in range(n_arr):
            src_me = ins[t].at[me] if scatter else ins[t]
            loc = pltpu.make_async_copy(src_me, outs[t].at[me], local_sems.at[t])
            loc.start()
            copies.append(loc)
            for j, m in enumerate(masks):
                src = ins[t].at[peer_idx[j]] if scatter else ins[t]
                cp = pltpu.make_async_remote_copy(src_ref=src, dst_ref=outs[t].at[me], send_sem=send_sems.at[t, j],
                                                  recv_sem=recv_sems.at[t, j], device_id=_peer(m), device_id_type=MESH)
                cp.start()
                copies.append(cp)
        for cp in copies:
            cp.wait()

    hbm = pl.BlockSpec(memory_space=pl.ANY)
    out_shape = [jax.ShapeDtypeStruct((n_grp,) + (a.shape[1:] if scatter else a.shape), a.dtype) for a in arrs]
    src_spec = pl.BlockSpec(memory_space=pltpu.VMEM) if stage else hbm
    outs = pl.pallas_call(
        body, name=name, in_specs=[src_spec] * n_arr, out_specs=[hbm] * n_arr, out_shape=out_shape,
        scratch_shapes=[pltpu.SemaphoreType.DMA((n_arr, n_peer)), pltpu.SemaphoreType.DMA((n_arr, n_peer)),
                        pltpu.SemaphoreType.DMA((n_arr,))],
    )(*arrs)
    return list(outs)


def _gather_copies(src_ref, land_ref, send_sems, recv_sems):
    me, _ = _group_index(CHIPS)
    return [pltpu.make_async_remote_copy(src_ref=src_ref, dst_ref=land_ref.at[me], send_sem=send_sems[j], recv_sem=recv_sems[j],
                                         device_id=_peer(m), device_id_type=MESH) for j, m in enumerate(CHIPS)]


def _gather_start(src, after, name):
    n = len(CHIPS)

    def body(src_ref, land_ref, *rest):
        sems, token = rest[len(after):len(after) + 2 * n], rest[-1]
        for cp in _gather_copies(src_ref, land_ref, sems[:n], sems[n:]):
            cp.start()
        token[...] = jnp.zeros_like(token)

    hbm = pl.BlockSpec(memory_space=pltpu.HBM)
    sem = pl.BlockSpec(memory_space=pltpu.SEMAPHORE)
    land = lax.empty((n + 1,) + src.shape, src.dtype)
    outs = pl.pallas_call(
        body, name=name,
        out_shape=(pltpu.SemaphoreType.DMA(()),) * (2 * n) + (pltpu.HBM(src.shape, src.dtype), pltpu.HBM(land.shape, land.dtype),
                                                              jax.ShapeDtypeStruct((8, 128), F32)),
        in_specs=(hbm, hbm) + (pl.BlockSpec(memory_space=pl.ANY),) * len(after),
        out_specs=(sem,) * (2 * n) + (hbm, hbm, pl.BlockSpec(memory_space=pltpu.VMEM)),
        input_output_aliases={0: 2 * n, 1: 2 * n + 1},
        compiler_params=pltpu.CompilerParams(has_side_effects=pltpu.SideEffectType.DATAFLOW_SIDE_EFFECTING),
    )(pltpu.with_memory_space_constraint(src, pltpu.HBM), pltpu.with_memory_space_constraint(land, pltpu.HBM), *after)
    return outs[:2 * n], outs[2 * n], outs[2 * n + 1], outs[2 * n + 2]


def _gather_wait(sems, src, land, after, name):
    n = len(CHIPS)

    def body(src_ref, land_ref, *rest):
        for cp in _gather_copies(src_ref, land_ref, rest[:n], rest[n:2 * n]):
            cp.wait_send()
            cp.wait_recv()

    hbm = pl.BlockSpec(memory_space=pltpu.HBM)
    sem = pl.BlockSpec(memory_space=pltpu.SEMAPHORE)
    return pl.pallas_call(
        body, name=name, out_shape=(pltpu.HBM(src.shape, src.dtype), pltpu.HBM(land.shape, land.dtype)),
        in_specs=(hbm, hbm) + (sem,) * (2 * n) + (pl.BlockSpec(memory_space=pl.ANY),) * len(after), out_specs=(hbm, hbm),
        input_output_aliases={0: 0, 1: 1},
        compiler_params=pltpu.CompilerParams(has_side_effects=pltpu.SideEffectType.DATAFLOW_SIDE_EFFECTING),
    )(src, land, *sems, *after)[1]


CHIPS = (4, 2, 6)
EVERYONE = (1, 2, 3, 4, 5, 6, 7)
SIBLING = (1,)
SWAP_ROWS = 512
WIN_ROWS = 592


def _pair_swap(arrs, other_layer, name):
    n_arr = len(arrs)
    shapes = [a.shape[-2:] for a in arrs]
    chunks = []
    for t, (rows, _) in enumerate(shapes):
        assert rows % 16 == 0
        for j, r0 in enumerate(range(0, rows, SWAP_ROWS)):
            chunks.append((t, r0, min(SWAP_ROWS, rows - r0), j % 2))

    def body(*refs):
        ins, outs = refs[:n_arr], refs[n_arr:2 * n_arr]
        bufs = refs[2 * n_arr:3 * n_arr]
        load_sems, send_sems, recv_sems = refs[3 * n_arr:]
        sibling = _peer(1)
        c = lax.axis_index("c")

        def load(k):
            t, r0, n, slot = chunks[k]
            src = ins[t].at[1 - c] if other_layer else ins[t]
            return pltpu.make_async_copy(src.at[pl.ds(r0, n)], bufs[t].at[slot, pl.ds(0, n)], load_sems.at[t, slot])

        def send(k):
            t, r0, n, slot = chunks[k]
            return pltpu.make_async_remote_copy(src_ref=bufs[t].at[slot, pl.ds(0, n)], dst_ref=outs[t].at[pl.ds(r0, n)],
                                                send_sem=send_sems.at[t, slot], recv_sem=recv_sems.at[t],
                                                device_id=sibling, device_id_type=MESH)

        in_flight = {}

        def start_load(k):
            key = (chunks[k][0], chunks[k][3])
            if key in in_flight:
                send(in_flight.pop(key)).wait_send()
            load(k).start()

        start_load(0)
        for k in range(len(chunks)):
            load(k).wait()
            if k + 1 < len(chunks):
                start_load(k + 1)
            send(k).start()
            in_flight[(chunks[k][0], chunks[k][3])] = k
        for k in in_flight.values():
            send(k).wait_send()
        for t in range(n_arr):
            pltpu.make_async_remote_copy(src_ref=outs[t], dst_ref=outs[t], send_sem=send_sems.at[t, 0],
                                         recv_sem=recv_sems.at[t], device_id=sibling, device_id_type=MESH).wait_recv()

    hbm = pl.BlockSpec(memory_space=pl.ANY)
    outs = pl.pallas_call(
        body, name=name, in_specs=[hbm] * n_arr, out_specs=[hbm] * n_arr,
        out_shape=[jax.ShapeDtypeStruct(s, a.dtype) for s, a in zip(shapes, arrs)],
        scratch_shapes=[pltpu.VMEM((2, min(SWAP_ROWS, s[0]), s[1]), a.dtype) for s, a in zip(shapes, arrs)]
        + [pltpu.SemaphoreType.DMA((n_arr, 2)), pltpu.SemaphoreType.DMA((n_arr, 2)), pltpu.SemaphoreType.DMA((n_arr,))],
        compiler_params=pltpu.CompilerParams(vmem_limit_bytes=48 * 1024 * 1024),
    )(*arrs)
    return list(outs)


def _sum_lead(a, name, out_dtype):
    n = a.shape[0]
    shape = a.shape[1:]

    def body(a_ref, o_ref):
        acc = a_ref[0].astype(F32)
        for k in range(1, n):
            acc = acc + a_ref[k].astype(F32)
        o_ref[...] = acc.astype(out_dtype)

    if len(shape) == 3:
        blk = (1,) + shape[1:]
        return pl.pallas_call(
            body, name=name, grid=(shape[0],), in_specs=[pl.BlockSpec((n,) + blk, lambda i: (0, i, 0, 0))],
            out_specs=pl.BlockSpec(blk, lambda i: (i, 0, 0)), out_shape=jax.ShapeDtypeStruct(shape, out_dtype),
            compiler_params=_cparams(1),
        )(a)
    rows, cols = shape
    rb = rows
    for cand in (512, 256, 128):
        if rows % cand == 0 and rows > cand:
            rb = cand
            break
    return pl.pallas_call(
        body, name=name, grid=(rows // rb,), in_specs=[pl.BlockSpec((n, rb, cols), lambda i: (0, i, 0))],
        out_specs=pl.BlockSpec((rb, cols), lambda i: (i, 0)), out_shape=jax.ShapeDtypeStruct((rows, cols), out_dtype),
        compiler_params=_cparams(1),
    )(a)


def _pair_sum(g, recv, layer, name, out_dtype):
    _, n, r, c = g.shape

    def body(l_ref, g_ref, r_ref, o_ref):
        o_ref[...] = (g_ref[0].astype(F32) + r_ref[...].astype(F32)).astype(out_dtype)

    return pl.pallas_call(
        body, name=name,
        grid_spec=pltpu.PrefetchScalarGridSpec(
            num_scalar_prefetch=1, grid=(n,),
            in_specs=[pl.BlockSpec((1, 1, r, c), lambda i, l: (l[0], i, 0, 0)), pl.BlockSpec((1, r, c), lambda i, l: (i, 0, 0))],
            out_specs=pl.BlockSpec((1, r, c), lambda i, l: (i, 0, 0))),
        out_shape=jax.ShapeDtypeStruct((n, r, c), out_dtype), compiler_params=_cparams(1),
    )(layer, g, recv)


def _tn_matmul(a, b, name, col_major=False):
    t, k = a.shape
    n = b.shape[1]
    tb = _tblock(t, 1024)
    kb = min(k, 1024)
    nb = min(n, 1024)
    grid = (k // kb, n // nb, t // tb)

    def body(a_ref, b_ref, o_ref):
        @pl.when(pl.program_id(2) == 0)
        def _():
            o_ref[...] = jnp.zeros_like(o_ref)

        acc = _bdot(a_ref[...], b_ref[...], TN)
        if col_major:
            o_ref[0] += acc
        else:
            o_ref[...] += acc

    if col_major:
        out_spec = pl.BlockSpec((1, kb, nb), lambda ki, ni, ti: (ni, ki, 0))
        out_shape = jax.ShapeDtypeStruct((n // nb, k, nb), F32)
    else:
        out_spec = pl.BlockSpec((kb, nb), lambda ki, ni, ti: (ki, ni))
        out_shape = jax.ShapeDtypeStruct((k, n), F32)
    return pl.pallas_call(
        body, name=name, grid=grid,
        in_specs=[pl.BlockSpec((tb, kb), lambda ki, ni, ti: (ti, ki)), pl.BlockSpec((tb, nb), lambda ki, ni, ti: (ti, ni))],
        out_specs=out_spec, out_shape=out_shape, compiler_params=_cparams(3),
    )(a, b)


def _sum_many(arrs, name):
    k = len(arrs)

    def body(*refs):
        for a_ref, o_ref in zip(refs[:k], refs[k:]):
            acc = a_ref[0]
            for j in range(1, a_ref.shape[0]):
                acc = acc + a_ref[j]
            o_ref[...] = acc

    return pl.pallas_call(body, name=name, out_shape=[jax.ShapeDtypeStruct(a.shape[1:], F32) for a in arrs],
                          compiler_params=pltpu.CompilerParams(vmem_limit_bytes=48 * 1024 * 1024))(*arrs)


def _adamw_math(w, g, m, v):
    m2 = ADAM_B1 * m + (1.0 - ADAM_B1) * g
    v2 = ADAM_B2 * v + (1.0 - ADAM_B2) * (g * g)
    m_hat = m2 / (1.0 - ADAM_B1 ** ADAM_STEP)
    v_hat = v2 / (1.0 - ADAM_B2 ** ADAM_STEP)
    return -ADAM_LR * (m_hat / (jnp.sqrt(v_hat) + ADAM_EPS) + ADAM_WD * w), m2, v2


def _adamw_many(ws, gs, ms, vs, name):
    n = len(ws)

    def body(*refs):
        ins, outs = refs[:4 * n], refs[4 * n:]
        for k in range(n):
            res = _adamw_math(ins[k][...], ins[n + k][...], ins[2 * n + k][...], ins[3 * n + k][...])
            for j in range(3):
                outs[3 * k + j][...] = res[j]

    out_shape = []
    for a in ws:
        out_shape += [jax.ShapeDtypeStruct(a.shape, F32)] * 3
    return pl.pallas_call(body, name=name, out_shape=out_shape,
                          compiler_params=pltpu.CompilerParams(vmem_limit_bytes=48 * 1024 * 1024))(*ws, *gs, *ms, *vs)


def _adamw(w, g, m, v, name):
    shape = w.shape
    cols = shape[-1]
    rows = int(math.prod(shape[:-1]))
    rb = rows
    for cand in (256, 128, 64, 32, 16, 8):
        if rows % cand == 0 and rows > cand:
            rb = cand
            break
    bc1 = 1.0 - ADAM_B1 ** ADAM_STEP
    bc2 = 1.0 - ADAM_B2 ** ADAM_STEP

    def body(w_ref, g_ref, m_ref, v_ref, d_ref, nm_ref, nv_ref):
        gg = g_ref[...]
        m2 = ADAM_B1 * m_ref[...] + (1.0 - ADAM_B1) * gg
        v2 = ADAM_B2 * v_ref[...] + (1.0 - ADAM_B2) * (gg * gg)
        m_hat = m2 / bc1
        v_hat = v2 / bc2
        d_ref[...] = -ADAM_LR * (m_hat / (jnp.sqrt(v_hat) + ADAM_EPS) + ADAM_WD * w_ref[...])
        nm_ref[...] = m2
        nv_ref[...] = v2

    spec = pl.BlockSpec((rb, cols), lambda i: (i, 0))
    sds = jax.ShapeDtypeStruct((rows, cols), F32)
    outs = pl.pallas_call(
        body, name=name, grid=(rows // rb,), in_specs=[spec] * 4, out_specs=[spec] * 3, out_shape=[sds] * 3,
        compiler_params=_cparams(1),
    )(*(z.reshape(rows, cols) for z in (w, g, m, v)))
    return tuple(o.reshape(shape) for o in outs)


def _ada_fwd(c_all, ada_w_sh, ada_b_sh):
    s = ada_w_sh.shape[2]
    sb = 512

    def body(c_ref, w_ref, b_ref, o_ref):
        cond = _silu(c_ref[...])
        o_ref[0] = _bdot(cond, w_ref[0]) + b_ref[0]

    return pl.pallas_call(
        body, name="ada_fwd", grid=(2, s // sb),
        in_specs=[_full((8, D)), pl.BlockSpec((1, D, sb), lambda l, j: (l, 0, j)), pl.BlockSpec((1, 1, sb), lambda l, j: (l, 0, j))],
        out_specs=pl.BlockSpec((1, 8, sb), lambda l, j: (l, 0, j)), out_shape=jax.ShapeDtypeStruct((2, 8, s), F32),
        compiler_params=_cparams(2),
    )(c_all, ada_w_sh, ada_b_sh)


def _ada_bwd(c_all, dmod_sh, dmod_all):
    s = dmod_sh.shape[2]
    sb = 512

    def body(c_ref, d_ref, o_ref):
        cond = _silu(c_ref[...])
        o_ref[0] = _bdot(cond, d_ref[0], TN)

    gw = pl.pallas_call(
        body, name="ada_bwd_w", grid=(2, s // sb),
        in_specs=[_full((8, D)), pl.BlockSpec((1, 8, sb), lambda l, j: (l, 0, j))],
        out_specs=pl.BlockSpec((1, D, sb), lambda l, j: (l, 0, j)), out_shape=jax.ShapeDtypeStruct((2, D, s), F32),
        compiler_params=_cparams(2),
    )(c_all, dmod_sh)

    def body_b(d_ref, o_ref):
        acc = d_ref[0, 0:1, :]
        for k in range(1, 8):
            acc = acc + d_ref[0, k:k + 1, :]
        o_ref[0] = acc

    gb = pl.pallas_call(
        body_b, name="ada_bwd_b", grid=(2,), in_specs=[pl.BlockSpec((1, 8, 6 * D), lambda l: (l, 0, 0))],
        out_specs=pl.BlockSpec((1, 1, 6 * D), lambda l: (l, 0, 0)), out_shape=jax.ShapeDtypeStruct((2, 1, 6 * D), F32),
        compiler_params=_cparams(1),
    )(dmod_all)
    return gw, gb.reshape(2, 6 * D)


def _f_in(h, nw, sc, sh, w_main, w_dt):
    t = h.shape[0]
    tb = _tblock(t)

    def body(h_ref, nw_ref, sc_ref, sh_ref, w_ref, wd_ref, p_ref, dt_ref, u_ref):
        _, n = _rms(h_ref[...])
        u = ((n * nw_ref[...]) * (1.0 + sc_ref[...]) + sh_ref[...]).astype(BF16)
        u_ref[...] = u
        p_ref[...] = _dot(u, w_ref[...], NT)
        dt_ref[...] = _dot(u, wd_ref[...], NT)

    return pl.pallas_call(
        body, name="f_in", grid=(t // tb,),
        in_specs=[pl.BlockSpec((tb, D), lambda i: (i, 0)), _row(D), _row(D), _row(D), _full((PW, D)), _full((DTW, D))],
        out_specs=[pl.BlockSpec((tb, PW), lambda i: (i, 0)), pl.BlockSpec((tb, DTW), lambda i: (i, 0)),
                   pl.BlockSpec((tb, D), lambda i: (i, 0))],
        out_shape=[jax.ShapeDtypeStruct((t, PW), F32), jax.ShapeDtypeStruct((t, DTW), F32), jax.ShapeDtypeStruct((t, D), BF16)],
        compiler_params=_cparams(1),
    )(h, nw, sc, sh, w_main, w_dt)


def _b_in_du(dab, dz, dxbc, ds5, ddt, w_main, w_dt):
    t = dab.shape[0]
    tb = _tblock(t)

    def body(a_ref, z_ref, x_ref, s_ref, d_ref, w_ref, wd_ref, o_ref):
        acc = _bdot(a_ref[...], w_ref[0:1024, :])
        acc += _bdot(z_ref[...], w_ref[1024:1280, :])
        acc += _bdot(s_ref[...], w_ref[1280:1536, :])
        acc += _bdot(x_ref[...], w_ref[1536:2304, :])
        acc += _bdot(d_ref[...], wd_ref[...])
        o_ref[...] = acc

    blk = lambda n: pl.BlockSpec((tb, n), lambda i: (i, 0))
    return pl.pallas_call(
        body, name="b_in_du", grid=(t // tb,),
        in_specs=[blk(1024), blk(256), blk(768), blk(256), blk(DTW), _full((PW, D)), _full((DTW, D))],
        out_specs=blk(D), out_shape=jax.ShapeDtypeStruct((t, D), F32), compiler_params=_cparams(1),
    )(dab, dz, dxbc, ds5, ddt, w_main, w_dt)


def _b_normmod(du, x, dres, gated, nw, sc, name):
    t = x.shape[0]
    tb = _tblock(t)

    def body(du_ref, x_ref, dr_ref, g_ref, nw_ref, sc_ref, dx_ref, dsc_ref, dsh_ref, dnw_ref, dg_ref):
        @pl.when(pl.program_id(0) == 0)
        def _():
            for r in (dsc_ref, dsh_ref, dnw_ref, dg_ref):
                r[...] = jnp.zeros_like(r)

        du_v = du_ref[...]
        r, n = _rms(x_ref[...])
        nwv = nw_ref[...]
        scale = 1.0 + sc_ref[...]
        dsc_ref[...] += _colsum(du_v * (n * nwv))
        dsh_ref[...] += _colsum(du_v)
        dnw_ref[...] += _colsum(du_v * scale * n)
        dres_v = dr_ref[...]
        dg_ref[...] += _colsum(dres_v * g_ref[...])
        dx_ref[...] = dres_v + _rms_bwd(r, n, du_v * scale * nwv)

    blk = pl.BlockSpec((tb, D), lambda i: (i, 0))
    row = jax.ShapeDtypeStruct((1, D), F32)
    return pl.pallas_call(
        body, name=name, grid=(t // tb,), in_specs=[blk, blk, blk, blk, _row(D), _row(D)],
        out_specs=[blk, _row(D), _row(D), _row(D), _row(D)], out_shape=[jax.ShapeDtypeStruct((t, D), F32), row, row, row, row],
        compiler_params=_cparams(1),
    )(du, x, dres, gated, nw, sc)


HALO = 16


def _lane_group(shape):
    return lax.broadcasted_iota(jnp.int32, shape, 1) // 64


def _window_select(g, s2, s4, s8, s16):
    return jnp.where(g == 0, s2, jnp.where(g == 1, s4, jnp.where(g == 2, s8, s16)))


def _pool_count(t0, rows):
    g = _lane_group((rows, GW))
    win = _window_select(g, 2, 4, 8, 16)
    tt = t0 + lax.broadcasted_iota(jnp.int32, (rows, GW), 0)
    return jnp.minimum(tt + 1, win).astype(F32)


def _pool_p(v_ext, t0, tb):
    s2 = v_ext + _roll(v_ext, 1)
    s4 = s2 + _roll(s2, 2)
    s8 = s4 + _roll(s4, 4)
    s16 = s8 + _roll(s8, 8)
    ws = _window_select(_lane_group(v_ext.shape), s2, s4, s8, s16)[HALO:]
    return ws / _pool_count(t0, tb) - v_ext[HALO:]


def _sconv(q_ext, w):
    return (_roll(q_ext, 2) * w[0:1] + _roll(q_ext, 1) * w[1:2] + q_ext * w[2:3])[HALO:]


def _halo_specs(t, tb, cols, col_block):
    per = tb // HALO
    last = t // HALO - 1
    prev = pl.BlockSpec((HALO, cols), lambda i: (jnp.maximum(i * per - 1, 0), col_block))
    nxt = pl.BlockSpec((HALO, cols), lambda i: (jnp.minimum((i + 1) * per, last), col_block))
    return prev, nxt


def _f_ab(proj, pool_mat, pool_scale, sconv_w):
    t = proj.shape[0]
    tb = _tblock(t)
    prev, _ = _halo_specs(t, tb, 1024, 0)

    def body(p_ref, h_ref, pm_ref, ps_ref, sw_ref, ya_ref, yb_ref):
        i = pl.program_id(0)
        halo = jnp.where(i > 0, h_ref[...], 0.0)
        ext = jnp.concatenate([halo, p_ref[...]], axis=0)
        p = _pool_p(ext[:, 0:256], i * tb, tb)
        ya_ref[...] = _bdot(p, pm_ref[...]) * ps_ref[...]
        q_ext = ext[:, 512:768] * ext[:, 768:1024]
        yb_ref[...] = p_ref[:, 256:512] * _sconv(q_ext, sw_ref[...])

    blk = pl.BlockSpec((tb, GW), lambda i: (i, 0))
    sds = jax.ShapeDtypeStruct((t, GW), F32)
    return pl.pallas_call(
        body, name="f_ab", grid=(t // tb,),
        in_specs=[pl.BlockSpec((tb, 1024), lambda i: (i, 0)), prev, _full((GW, GW)), _row(GW), _full((3, GW))],
        out_specs=[blk, blk], out_shape=[sds, sds], compiler_params=_cparams(1),
    )(proj, proj, pool_mat, pool_scale, sconv_w)


def _b_ab(proj, dya, dyb, pool_mat, pool_scale, sconv_w):
    t = proj.shape[0]
    tb = _tblock(t)
    nb = t // tb
    prev, nxt = _halo_specs(t, tb, 1024, 0)
    _, nxt_g = _halo_specs(t, tb, GW, 0)
    n_ext = tb + HALO

    def body(p_ref, hp_ref, hn_ref, da_ref, dan_ref, db_ref, dbn_ref, pm_ref, ps_ref, sw_ref,
             o_ref, dpm_ref, dps_ref, dsw_ref):
        i = pl.program_id(0)

        @pl.when(i == 0)
        def _():
            for r in (dpm_ref, dps_ref, dsw_ref):
                r[...] = jnp.zeros_like(r)

        last = i == nb - 1
        halo = jnp.where(i > 0, hp_ref[...], 0.0)
        main = p_ref[...]
        ext = jnp.concatenate([halo, main], axis=0)
        scale = ps_ref[...]
        pm = pm_ref[...]
        p = _pool_p(ext[:, 0:256], i * tb, tb)
        da = da_ref[...]
        dps_ref[...] += _colsum(da * _bdot(p, pm))
        da_ext = jnp.concatenate([da, jnp.where(last, 0.0, dan_ref[...])], axis=0)
        dys = da_ext * scale
        dpm_ref[...] += _bdot(p, dys[:tb], TN)
        dp = _bdot(dys, pm, NT)
        dpc = dp / _pool_count(i * tb, n_ext)
        a2 = dpc + _roll(dpc, n_ext - 1)
        a4 = a2 + _roll(a2, n_ext - 2)
        a8 = a4 + _roll(a4, n_ext - 4)
        a16 = a8 + _roll(a8, n_ext - 8)
        o_ref[:, 0:256] = (_window_select(_lane_group(dpc.shape), a2, a4, a8, a16) - dp)[:tb]
        w = sw_ref[...]
        gb, gc, hh = main[:, 256:512], main[:, 512:768], main[:, 768:1024]
        q_ext = ext[:, 512:768] * ext[:, 768:1024]
        db = db_ref[...]
        o_ref[:, 256:512] = db * _sconv(q_ext, w)
        gb_next = hn_ref[:, 256:512]
        dconv = jnp.concatenate([db * gb, jnp.where(last, 0.0, dbn_ref[...] * gb_next)], axis=0)
        dq = (dconv * w[2:3] + _roll(dconv, n_ext - 1) * w[1:2] + _roll(dconv, n_ext - 2) * w[0:1])[:tb]
        o_ref[:, 512:768] = dq * hh
        o_ref[:, 768:1024] = dq * gc
        dc = dconv[:tb]
        dsw_ref[0:1, :] += _colsum(dc * _roll(q_ext, 2)[HALO:])
        dsw_ref[1:2, :] += _colsum(dc * _roll(q_ext, 1)[HALO:])
        dsw_ref[2:3, :] += _colsum(dc * q_ext[HALO:])

    blk = pl.BlockSpec((tb, GW), lambda i: (i, 0))
    return pl.pallas_call(
        body, name="b_ab", grid=(nb,),
        in_specs=[pl.BlockSpec((tb, 1024), lambda i: (i, 0)), prev, nxt, blk, nxt_g, blk, nxt_g,
                  _full((GW, GW)), _row(GW), _full((3, GW))],
        out_specs=[pl.BlockSpec((tb, 1024), lambda i: (i, 0)), _full((GW, GW)), _row(GW), _full((3, GW))],
        out_shape=[jax.ShapeDtypeStruct((t, 1024), F32), jax.ShapeDtypeStruct((GW, GW), F32),
                   jax.ShapeDtypeStruct((1, GW), F32), jax.ShapeDtypeStruct((3, GW), F32)],
        compiler_params=_cparams(1),
    )(proj, proj, proj, dya, dya, dyb, dyb, pool_mat, pool_scale, sconv_w)


CH = 8


def _ssd_conv(x, halo, w, b):
    ext = jnp.concatenate([halo, x], axis=0)
    pre = ext * w[3:4] + _roll(ext, 1) * w[2:3] + _roll(ext, 2) * w[1:2] + _roll(ext, 3) * w[0:1] + b
    return pre[CH:], ext


def _ssd_common(dt_raw, dtb, alog):
    ll = dt_raw.shape[0]
    dtv = _softplus(dt_raw + dtb)
    a_row = -jnp.exp(alog)
    r = lax.broadcasted_iota(jnp.int32, (ll, ll), 0)
    c = lax.broadcasted_iota(jnp.int32, (ll, ll), 1)
    tril = (r >= c).astype(F32)
    cs = _dot(tril, dtv * a_row, prec=HI)
    return dtv, a_row, cs, cs.T, r >= c


def _ssd_bc(act_b, g):
    return act_b[:, 256 + NS * g:256 + NS * (g + 1)], act_b[:, 512 + NS * g:512 + NS * (g + 1)]


def _ssd_gmat(act_b):
    return [_dot(_ssd_bc(act_b, g)[1], _ssd_bc(act_b, g)[0], NT) for g in range(2)]


def _ssd_head(h, act, act_b, dtv, cs, cs_t, causal, gmat):
    g = h // 2
    xs = act[:, HP * h:HP * (h + 1)]
    bm, cm = _ssd_bc(act_b, g)
    cs_c = cs[:, h:h + 1]
    cs_r = cs_t[h:h + 1, :]
    mdec = jnp.where(causal, jnp.exp(jnp.minimum(cs_c - cs_r, 0.0)), 0.0)
    sc = gmat[g] * mdec
    dt_c = dtv[:, h:h + 1]
    xdt = xs * dt_c
    e = jnp.exp(cs_c)
    cs_last = cs[SSD_L - 1:SSD_L, h:h + 1]
    wdec = jnp.exp(cs_last - cs_c)
    return xs, bm, cm, cs_c, mdec, sc, dt_c, xdt, e, cs_last, wdec


def _f_ssd(proj, dtp, conv_w, conv_b, dt_bias, a_log, d_skip):
    t = proj.shape[0]
    nc = t // SSD_L
    rows = SSD_SUB * SSD_L
    per = rows // CH

    def body(x_ref, hx_ref, dt_ref, z_ref, cw_ref, cb_ref, dtb_ref, al_ref, dk_ref, y_ref, yp_ref, sp_ref, s_ref):
        i = pl.program_id(0)

        @pl.when(i == 0)
        def _():
            s_ref[...] = jnp.zeros_like(s_ref)

        state = [s_ref[h] for h in range(NH)]
        for sub in range(SSD_SUB):
            r0 = sub * SSD_L
            rs = slice(r0, r0 + SSD_L)
            halo = jnp.where(i > 0, hx_ref[...], 0.0) if sub == 0 else x_ref[r0 - CH:r0, :]
            pre, _ = _ssd_conv(x_ref[rs, :], halo, cw_ref[...], cb_ref[...])
            act = _silu(pre)
            dtv, _, cs, cs_t, causal = _ssd_common(dt_ref[rs, :], dtb_ref[...], al_ref[...])
            gmat = _ssd_gmat(act)
            for h in range(NH):
                xs, bm, cm, _, _, sc, _, xdt, e, cs_last, wdec = _ssd_head(h, act, act, dtv, cs, cs_t, causal, gmat)
                prev = state[h]
                sp_ref[sub, h] = prev
                y = _dot(sc, xdt) + e * _dot(cm, prev, NT) + xs * dk_ref[0:1, h:h + 1]
                yp_ref[rs, HP * h:HP * (h + 1)] = y
                state[h] = prev * jnp.exp(cs_last) + _dot(xdt * wdec, bm, TN)
            y_ref[rs, :] = yp_ref[rs, :] * _silu(z_ref[rs, :])
        for h in range(NH):
            s_ref[h] = state[h]

    blk = pl.BlockSpec((rows, GW), lambda i: (i, 0))
    sds = jax.ShapeDtypeStruct((t, GW), F32)
    return pl.pallas_call(
        body, name="f_ssd", grid=(nc // SSD_SUB,),
        in_specs=[pl.BlockSpec((rows, 768), lambda i: (i, 2)),
                  pl.BlockSpec((CH, 768), lambda i: (jnp.maximum(i * per - 1, 0), 2)),
                  pl.BlockSpec((rows, DTW), lambda i: (i, 0)),
                  pl.BlockSpec((rows, GW), lambda i: (i, 4)),
                  _full((4, 768)), _row(768), _row(DTW), _row(DTW), _row(DTW)],
        out_specs=[blk, blk, pl.BlockSpec((SSD_SUB, NH, HP, NS), lambda i: (i, 0, 0, 0))],
        out_shape=[sds, sds, jax.ShapeDtypeStruct((nc, NH, HP, NS), F32)],
        scratch_shapes=[pltpu.VMEM((NH, HP, NS), F32)], compiler_params=_cparams(1),
    )(proj, proj, dtp, proj, conv_w, conv_b, dt_bias, a_log, d_skip)


def _b_ssd(proj, dtp, ypre, dyc, sprev, conv_w, conv_b, dt_bias, a_log, d_skip):
    t = proj.shape[0]
    nc = t // SSD_L
    steps = nc // SSD_SUB_BWD
    rows = SSD_SUB_BWD * SSD_L
    per = rows // CH
    n_ext = SSD_L + CH

    def chunk(sub, halo, dnext, ds_in, refs):
        (x_ref, dt_ref, z_ref, yp_ref, dy_ref, sp_ref, cw_ref, cb_ref, dtb_ref, al_ref, dk_ref,
         dz_ref, dx_ref, ddt_ref, dact_ref) = refs
        rs = slice(sub * SSD_L, (sub + 1) * SSD_L)
        dact = dact_ref.at[sub]
        w = cw_ref[...]
        pre, ext = _ssd_conv(x_ref[rs, :], halo, w, cb_ref[...])
        act = _silu(pre)
        dt_raw = dt_ref[rs, :]
        dtv, a_row, cs, cs_t, causal = _ssd_common(dt_raw, dtb_ref[...], al_ref[...])
        gmat = _ssd_gmat(act)
        z = z_ref[rs, :]
        dyc_v = dy_ref[rs, :]
        dz_ref[rs, :] = dyc_v * yp_ref[rs, :] * _dsilu(z)
        dy_all = dyc_v * _silu(z)
        lane = lax.broadcasted_iota(jnp.int32, (SSD_L, DTW), 1)
        rowi = lax.broadcasted_iota(jnp.int32, (SSD_L, 1), 0)
        dcs_mat = jnp.zeros((SSD_L, DTW), F32)
        ddtx_mat = jnp.zeros((SSD_L, DTW), F32)
        ddk_row = jnp.zeros((1, DTW), F32)
        lane1 = lax.broadcasted_iota(jnp.int32, (1, DTW), 1)
        dbm = [None, None]
        dcm = [None, None]
        ds_out = []
        for h in range(NH):
            g = h // 2
            xs, bm, cm, _, mdec, sc, dt_c, xdt, e, cs_last, wdec = _ssd_head(h, act, act, dtv, cs, cs_t, causal, gmat)
            dy = dy_all[:, HP * h:HP * (h + 1)]
            prev = sp_ref[sub, h]
            ds = ds_in[h]
            dsc = _dot(dy, xdt, NT)
            q = dsc * sc
            dg = dsc * mdec
            dxdt = _dot(sc, dy, TN)
            dcs = jnp.sum(q, axis=1, keepdims=True) - jnp.sum(q.T, axis=1, keepdims=True)
            dc_h = _dot(dg, bm)
            db_h = _dot(dg, cm, TN)
            cp = _dot(cm, prev, NT)
            dcs += jnp.sum(dy * cp, axis=1, keepdims=True) * e
            ey = e * dy
            dc_h += _dot(ey, prev)
            dprev = _dot(ey, cm, TN)
            elast = jnp.exp(cs_last)
            dprev += ds * elast
            dcs_last = jnp.sum(ds * prev, keepdims=True) * elast
            bds = _dot(bm, ds, NT)
            dxdt += wdec * bds
            db_h += wdec * _dot(xdt, ds)
            dw = jnp.sum(xdt * bds, axis=1, keepdims=True) * wdec
            dcs -= dw
            dcs_last += jnp.sum(dw, keepdims=True)
            dcs += jnp.where(rowi == SSD_L - 1, dcs_last, 0.0)
            ds_out.append(dprev)
            dact[:, HP * h:HP * (h + 1)] = dxdt * dt_c + dy * dk_ref[0:1, h:h + 1]
            dcs_mat = jnp.where(lane == h, dcs, dcs_mat)
            ddtx_mat = jnp.where(lane == h, jnp.sum(dxdt * xs, axis=1, keepdims=True), ddtx_mat)
            ddk_row = jnp.where(lane1 == h, jnp.sum(dy * xs, keepdims=True), ddk_row)
            dbm[g] = db_h if dbm[g] is None else dbm[g] + db_h
            dcm[g] = dc_h if dcm[g] is None else dcm[g] + dc_h
        for g in range(2):
            dact[:, 256 + NS * g:256 + NS * (g + 1)] = dbm[g]
            dact[:, 512 + NS * g:512 + NS * (g + 1)] = dcm[g]
        r2 = lax.broadcasted_iota(jnp.int32, (SSD_L, SSD_L), 0)
        c2 = lax.broadcasted_iota(jnp.int32, (SSD_L, SSD_L), 1)
        dadt = _dot((c2 >= r2).astype(F32), dcs_mat, prec=HI)
        ddt = jnp.where(lane < NH, (dadt * a_row + ddtx_mat) * _sig(dt_raw + dtb_ref[...]), 0.0)
        ddt_ref[rs, :] = ddt
        dpre = dact[...] * _dsilu(pre)
        dcw = jnp.concatenate([_colsum(dpre * _roll(ext, 3 - k)[CH:]) for k in range(4)], axis=0)
        dext = jnp.concatenate([dpre, dnext], axis=0)
        dx_ref[rs, :] = (dext * w[3:4] + _roll(dext, n_ext - 1) * w[2:3] + _roll(dext, n_ext - 2) * w[1:2]
                         + _roll(dext, n_ext - 3) * w[0:1])[:SSD_L]
        acc = (dcw, _colsum(dpre), _colsum(ddt), _colsum(dadt * dtv) * a_row, ddk_row)
        return dpre[0:CH], ds_out, acc

    def body(x_ref, hx_ref, dt_ref, z_ref, yp_ref, dy_ref, sp_ref, cw_ref, cb_ref, dtb_ref, al_ref, dk_ref,
             dz_ref, dx_ref, ddt_ref, dcw_ref, dcb_ref, ddtb_ref, dal_ref, ddk_ref, ds_ref, dnext_ref, dact_ref):
        i = pl.program_id(0)
        acc_refs = (dcw_ref, dcb_ref, ddtb_ref, dal_ref, ddk_ref)

        @pl.when(i == 0)
        def _():
            ds_ref[...] = jnp.zeros_like(ds_ref)
            dnext_ref[...] = jnp.zeros_like(dnext_ref)
            for r in acc_refs:
                r[...] = jnp.zeros_like(r)

        refs = (x_ref, dt_ref, z_ref, yp_ref, dy_ref, sp_ref, cw_ref, cb_ref, dtb_ref, al_ref, dk_ref, dz_ref, dx_ref, ddt_ref,
                dact_ref)
        ds = [ds_ref[h] for h in range(NH)]
        dnext = dnext_ref[...]
        total = None
        for sub in reversed(range(SSD_SUB_BWD)):
            if sub == 0:
                halo = jnp.where(i == steps - 1, 0.0, hx_ref[...])
            else:
                halo = x_ref[sub * SSD_L - CH:sub * SSD_L, :]
            dnext, ds, acc = chunk(sub, halo, dnext, ds, refs)
            total = acc if total is None else tuple(a + b for a, b in zip(total, acc))
        for h in range(NH):
            ds_ref[h] = ds[h]
        dnext_ref[...] = dnext
        for r, v in zip(acc_refs, total):
            r[...] += v

    rev = lambda i: steps - 1 - i
    blk = lambda n, cb=0: pl.BlockSpec((rows, n), lambda i: (rev(i), cb))
    row = lambda n: jax.ShapeDtypeStruct((1, n), F32)
    return pl.pallas_call(
        body, name="b_ssd", grid=(steps,),
        in_specs=[blk(768, 2), pl.BlockSpec((CH, 768), lambda i: (jnp.maximum(rev(i) * per - 1, 0), 2)),
                  blk(DTW), blk(GW, 4), blk(GW), blk(GW), pl.BlockSpec((SSD_SUB_BWD, NH, HP, NS), lambda i: (rev(i), 0, 0, 0)),
                  _full((4, 768)), _row(768), _row(DTW), _row(DTW), _row(DTW)],
        out_specs=[blk(GW), blk(768), blk(DTW), _full((4, 768)), _row(768), _row(DTW), _row(DTW), _row(DTW)],
        out_shape=[jax.ShapeDtypeStruct((t, GW), F32), jax.ShapeDtypeStruct((t, 768), F32), jax.ShapeDtypeStruct((t, DTW), F32),
                   jax.ShapeDtypeStruct((4, 768), F32), row(768), row(DTW), row(DTW), row(DTW)],
        scratch_shapes=[pltpu.VMEM((NH, HP, NS), F32), pltpu.VMEM((CH, 768), F32), pltpu.VMEM((SSD_SUB_BWD, SSD_L, 768), F32)],
        compiler_params=_cparams(1),
    )(proj, proj, dtp, proj, ypre, dyc, sprev, conv_w, conv_b, dt_bias, a_log, d_skip)


def _s5_block(t):
    return min(t, 256)


def _seg_t():
    r = lax.broadcasted_iota(jnp.int32, (64, 1024), 0)
    c = lax.broadcasted_iota(jnp.int32, (64, 1024), 1)
    return (c // 16 == r).astype(F32)


def _s5_prep_math(a_re, a_im, lstep, b_re, b_im):
    step = jnp.exp(lstep)
    ars = a_re * step
    ais = a_im * step
    mag = jnp.exp(ars)
    lr = mag * jnp.cos(ais)
    li = mag * jnp.sin(ais)
    den = a_re * a_re + a_im * a_im
    nr = lr - 1.0
    f_re = (nr * a_re + li * a_im) / den
    f_im = (li * a_re - nr * a_im) / den
    seg = _seg_t()
    fr = _dot(f_re, seg, prec=HI)
    fi = _dot(f_im, seg, prec=HI)
    return lr, li, fr * b_re - fi * b_im, fr * b_im + fi * b_re, ars, ais


def _s5_prep(a_re, a_im, lstep, b_re, b_im):
    def body(ar, ai, ls, br, bi, lr_o, li_o, bbr_o, bbi_o, ars_o, ais_o):
        outs = _s5_prep_math(ar[...], ai[...], ls[...], br[...], bi[...])
        for o, v in zip((lr_o, li_o, bbr_o, bbi_o, ars_o, ais_o), outs):
            o[...] = v

    s64 = jax.ShapeDtypeStruct((16, 64), F32)
    s1k = jax.ShapeDtypeStruct((16, 1024), F32)
    return pl.pallas_call(body, name="s5_prep", out_shape=[s64, s64, s1k, s1k, s64, s64])(a_re, a_im, lstep, b_re, b_im)


def _s5_prep_bwd(a_re, a_im, lstep, b_re, b_im, dlr, dli, dbbr, dbbi):
    def body(ar, ai, ls, br, bi, g0, g1, g2, g3, o0, o1, o2, o3, o4):
        f = lambda *a: _s5_prep_math(*a)[:4]
        _, vjp = jax.vjp(f, ar[...], ai[...], ls[...], br[...], bi[...])
        for o, v in zip((o0, o1, o2, o3, o4), vjp((g0[...], g1[...], g2[...], g3[...]))):
            o[...] = v

    s64 = jax.ShapeDtypeStruct((16, 64), F32)
    s1k = jax.ShapeDtypeStruct((16, 1024), F32)
    return pl.pallas_call(body, name="s5_prep_bwd", out_shape=[s64, s64, jax.ShapeDtypeStruct((16, 1), F32), s1k, s1k])(
        a_re, a_im, lstep, b_re, b_im, dlr, dli, dbbr, dbbi)


SUB = 8


def _s5_tables(ars, ais):
    def body(ar, ai, tr, ti):
        rr = lax.broadcasted_iota(jnp.int32, (8 * SUB, S5_P), 0)
        seg, r = rr // SUB, rr % SUB
        step = jnp.where((seg == 1) | (seg == 4), 1, jnp.where((seg == 2) | (seg == 5), 2, 4))
        n = jnp.where(seg == 0, r + 1, jnp.where(seg == 7, SUB - r, step))
        fwd_gap = jnp.where(seg <= 3, r - step, SUB - step - 1 - r)
        gap = jnp.where((seg == 0) | (seg == 7), 0, fwd_gap)
        nf = n.astype(F32)
        mag = jnp.where(gap >= 0, jnp.exp(nf * ar[...]), 0.0)
        tr[...] = mag * jnp.cos(nf * ai[...])
        ti[...] = mag * jnp.sin(nf * ai[...])

    sds = jax.ShapeDtypeStruct((8 * SUB, S5_P), F32)
    return pl.pallas_call(body, name="s5_tables", out_shape=[sds] * 2)(ars, ais)


def _s5_table(tb_r, tb_i, k):
    return tb_r[SUB * k:SUB * (k + 1), :], tb_i[SUB * k:SUB * (k + 1), :]


def _s5_scan(bu_r, bu_i, tb_r, tb_i, c_r, c_i, lb):
    nt = lb // SUB
    sr, si = bu_r.reshape(nt, SUB, S5_P), bu_i.reshape(nt, SUB, S5_P)
    for j, k in enumerate((1, 2, 4)):
        mr, mi = _s5_table(tb_r, tb_i, 1 + j)
        tr, ti = pltpu.roll(sr, k, axis=1), pltpu.roll(si, k, axis=1)
        sr, si = sr + mr * tr - mi * ti, si + mr * ti + mi * tr
    pr, pi = _s5_table(tb_r, tb_i, 0)
    out_r, out_i = [], []
    for j in range(nt):
        a_r = sr[j] + pr * c_r - pi * c_i
        a_i = si[j] + pr * c_i + pi * c_r
        out_r.append(a_r)
        out_i.append(a_i)
        c_r, c_i = a_r[SUB - 1:SUB], a_i[SUB - 1:SUB]
    return jnp.concatenate(out_r, axis=0), jnp.concatenate(out_i, axis=0)


def _s5_rscan(g_r, g_i, tb_r, tb_i, n_r, n_i, lb):
    nt = lb // SUB
    gr, gi = g_r.reshape(nt, SUB, S5_P), g_i.reshape(nt, SUB, S5_P)
    for j, k in enumerate((1, 2, 4)):
        mr, mi = _s5_table(tb_r, tb_i, 4 + j)
        tr, ti = pltpu.roll(gr, SUB - k, axis=1), pltpu.roll(gi, SUB - k, axis=1)
        gr, gi = gr + mr * tr + mi * ti, gi + mr * ti - mi * tr
    qr, qi = _s5_table(tb_r, tb_i, 7)
    out_r, out_i = [None] * nt, [None] * nt
    for j in reversed(range(nt)):
        a_r = gr[j] + qr * n_r + qi * n_i
        a_i = gi[j] + qr * n_i - qi * n_r
        out_r[j], out_i[j] = a_r, a_i
        n_r, n_i = a_r[0:1], a_i[0:1]
    return jnp.concatenate(out_r, axis=0), jnp.concatenate(out_i, axis=0)


def _s5_y(u, sr, si, cre, cim, dsk):
    return _bdot(sr, cre) + _bdot(si, cim) + dsk * u


def _f_s5(proj, bmat, cre, cim, p_r, p_i, dsk, glu_w, glu_b):
    t = proj.shape[0]
    lb = _s5_block(t)
    nb = t // lb

    def body(u_ref, bm_ref, cr_ref, ci_ref, pr_ref, pi_ref, dk_ref, gw_ref, gb_ref, y_ref, car_ref, s_ref, st_ref):
        @pl.when(pl.program_id(0) == 0)
        def _():
            st_ref[...] = jnp.zeros_like(st_ref)

        u = u_ref[...]
        bu = _bdot(u, bm_ref[...])
        c_r, c_i = st_ref[0:1, 0:S5_P], st_ref[0:1, S5_P:]
        car_ref[0] = st_ref[0:1, :]
        sr, si = _s5_scan(bu[:, :S5_P], bu[:, S5_P:], pr_ref, pi_ref, c_r, c_i, lb)
        st_ref[0:1, 0:S5_P] = sr[lb - 1:lb]
        st_ref[0:1, S5_P:] = si[lb - 1:lb]
        sr_b, si_b = sr.astype(BF16), si.astype(BF16)
        s_ref[:, 0:S5_P] = sr_b
        s_ref[:, S5_P:] = si_b
        gel = _gelu(_s5_y(u, sr_b, si_b, cr_ref[...], ci_ref[...], dk_ref[...]))
        y_ref[...] = gel * _sig(_bdot(gel, gw_ref[...]) + gb_ref[...])

    return pl.pallas_call(
        body, name="f_s5", grid=(nb,),
        in_specs=[pl.BlockSpec((lb, GW), lambda i: (i, 5)),
                  _full((GW, 2 * S5_P)), _full((S5_P, GW)), _full((S5_P, GW)), _full((8 * SUB, S5_P)), _full((8 * SUB, S5_P)),
                  _row(GW), _full((GW, GW)), _row(GW)],
        out_specs=[pl.BlockSpec((lb, GW), lambda i: (i, 0)), pl.BlockSpec((1, 1, 2 * S5_P), lambda i: (i, 0, 0)),
                   pl.BlockSpec((lb, 2 * S5_P), lambda i: (i, 0))],
        out_shape=[jax.ShapeDtypeStruct((t, GW), F32), jax.ShapeDtypeStruct((nb, 1, 2 * S5_P), F32),
                   jax.ShapeDtypeStruct((t, 2 * S5_P), BF16)],
        scratch_shapes=[pltpu.VMEM((8, 2 * S5_P), F32)], compiler_params=_cparams(1),
    )(proj, bmat, cre, cim, p_r, p_i, dsk, glu_w, glu_b)


def _b_s5(proj, dyd, carries, states, bmat, cre, cim, p_r, p_i, dsk, glu_w, glu_b):
    t = proj.shape[0]
    lb = _s5_block(t)
    nb = t // lb

    def body(u_ref, dy_ref, car_ref, s_ref, bm_ref, cr_ref, ci_ref, pr_ref, pi_ref, dk_ref, gw_ref, gb_ref,
             du_ref, dbm_ref, dcr_ref, dci_ref, dlam_ref, ddk_ref, dgw_ref, dgb_ref, gc_ref):
        @pl.when(pl.program_id(0) == 0)
        def _():
            gc_ref[...] = jnp.zeros_like(gc_ref)
            for r in (dbm_ref, dcr_ref, dci_ref, dlam_ref, ddk_ref, dgw_ref, dgb_ref):
                r[...] = jnp.zeros_like(r)

        u = u_ref[...]
        bm = bm_ref[...]
        u_b = u.astype(BF16)
        c_r, c_i = car_ref[0, 0:1, 0:S5_P], car_ref[0, 0:1, S5_P:]
        cre_v, cim_v, dk, gw = cr_ref[...], ci_ref[...], dk_ref[...], gw_ref[...]
        sr_b, si_b = s_ref[:, 0:S5_P], s_ref[:, S5_P:]
        sr, si = sr_b.astype(F32), si_b.astype(F32)
        y = _dot(sr_b, cre_v) + _dot(si_b, cim_v) + dk * u
        gel = _gelu(y)
        gel_b = gel.astype(BF16)
        gate = _sig(_dot(gel_b, gw) + gb_ref[...])
        dout = dy_ref[...]
        t1 = dout * gel * gate * (1.0 - gate)
        t1_b = t1.astype(BF16)
        dgw_ref[...] += _dot(gel_b, t1_b, TN)
        dgb_ref[...] += _colsum(t1)
        dyv = (dout * gate + _dot(t1_b, gw, NT)) * _dgelu(y)
        dyv_b = dyv.astype(BF16)
        ddk_ref[...] += _colsum(dyv * u)
        dcr_ref[...] += _dot(sr_b, dyv_b, TN)
        dci_ref[...] += _dot(si_b, dyv_b, TN)
        gr = _dot(dyv_b, cre_v, NT)
        gi = _dot(dyv_b, cim_v, NT)
        row = lax.broadcasted_iota(jnp.int32, (lb, S5_P), 0)
        n_r, n_i = gc_ref[0:1, 0:S5_P], gc_ref[0:1, S5_P:]
        gr, gi = _s5_rscan(gr, gi, pr_ref, pi_ref, n_r, n_i, lb)
        gc_ref[0:1, 0:S5_P] = gr[0:1]
        gc_ref[0:1, S5_P:] = gi[0:1]
        gcat = jnp.concatenate([gr, gi], axis=1).astype(BF16)
        dbm_ref[...] += _dot(u_b, gcat, TN)
        du_ref[...] = dyv * dk + _dot(gcat, bm, NT)
        spr = jnp.where(row >= 1, _roll(sr, 1), c_r)
        spi = jnp.where(row >= 1, _roll(si, 1), c_i)
        dlam_ref[0:1, :] += _colsum(gr * spr + gi * spi)
        dlam_ref[1:2, :] += _colsum(gi * spr - gr * spi)

    rev = lambda i: nb - 1 - i
    return pl.pallas_call(
        body, name="b_s5", grid=(nb,),
        in_specs=[pl.BlockSpec((lb, GW), lambda i: (rev(i), 5)), pl.BlockSpec((lb, GW), lambda i: (rev(i), 0)),
                  pl.BlockSpec((1, 1, 2 * S5_P), lambda i: (rev(i), 0, 0)), pl.BlockSpec((lb, 2 * S5_P), lambda i: (rev(i), 0)),
                  _full((GW, 2 * S5_P)), _full((S5_P, GW)), _full((S5_P, GW)), _full((8 * SUB, S5_P)), _full((8 * SUB, S5_P)),
                  _row(GW), _full((GW, GW)), _row(GW)],
        out_specs=[pl.BlockSpec((lb, GW), lambda i: (rev(i), 0)), _full((GW, 2 * S5_P)), _full((S5_P, GW)), _full((S5_P, GW)),
                   _full((2, S5_P)), _row(GW), _full((GW, GW)), _row(GW)],
        out_shape=[jax.ShapeDtypeStruct((t, GW), F32), jax.ShapeDtypeStruct((GW, 2 * S5_P), F32),
                   jax.ShapeDtypeStruct((S5_P, GW), F32), jax.ShapeDtypeStruct((S5_P, GW), F32),
                   jax.ShapeDtypeStruct((2, S5_P), F32), jax.ShapeDtypeStruct((1, GW), F32),
                   jax.ShapeDtypeStruct((GW, GW), F32), jax.ShapeDtypeStruct((1, GW), F32)],
        scratch_shapes=[pltpu.VMEM((8, 2 * S5_P), F32)], compiler_params=_cparams(1),
    )(proj, dyd, carries, states, bmat, cre, cim, p_r, p_i, dsk, glu_w, glu_b)


def _group_norm(ys, bw):
    outs, stats = [], []
    for g, y in enumerate(ys):
        r, n = _rms(y)
        stats.append((r, n))
        outs.append(n * bw[:, GW * g:GW * (g + 1)])
    return jnp.concatenate(outs, axis=1), stats


def _f_out(ya, yb, yc, yd, bw, w_out, h, g1):
    t = h.shape[0]
    tb = _tblock(t)

    def body(a_ref, b_ref, c_ref, d_ref, bw_ref, w_ref, h_ref, g_ref, h2_ref, o_ref, cat_ref):
        cat, _ = _group_norm([a_ref[...], b_ref[...], c_ref[...], d_ref[...]], bw_ref[...])
        catb = cat.astype(BF16)
        cat_ref[...] = catb
        o = _dot(catb, w_ref[...])
        o_ref[...] = o
        h2_ref[...] = h_ref[...] + g_ref[...] * o

    yblk = pl.BlockSpec((tb, GW), lambda i: (i, 0))
    blk = pl.BlockSpec((tb, D), lambda i: (i, 0))
    return pl.pallas_call(
        body, name="f_out", grid=(t // tb,), in_specs=[yblk] * 4 + [_row(D), _full((D, D)), blk, _row(D)],
        out_specs=[blk, blk, blk],
        out_shape=[jax.ShapeDtypeStruct((t, D), F32), jax.ShapeDtypeStruct((t, D), F32), jax.ShapeDtypeStruct((t, D), BF16)],
        compiler_params=_cparams(1),
    )(ya, yb, yc, yd, bw, w_out, h, g1)


def _b_out(dh2, ya, yb, yc, yd, bw, w_out, g1):
    t = dh2.shape[0]
    tb = _tblock(t)

    def body(dh_ref, a_ref, b_ref, c_ref, d_ref, bw_ref, w_ref, g_ref, da_ref, db_ref, dc_ref, dd_ref, do_ref, dbw_ref):
        @pl.when(pl.program_id(0) == 0)
        def _():
            dbw_ref[...] = jnp.zeros_like(dbw_ref)

        do = (dh_ref[...] * g_ref[...]).astype(BF16)
        do_ref[...] = do
        dcat = _dot(do, w_ref[...], NT)
        bw_v = bw_ref[...]
        for g, (y_ref, dy_ref) in enumerate(((a_ref, da_ref), (b_ref, db_ref), (c_ref, dc_ref), (d_ref, dd_ref))):
            r, n = _rms(y_ref[...])
            dc = dcat[:, GW * g:GW * (g + 1)]
            dbw_ref[:, GW * g:GW * (g + 1)] += _colsum(dc * n)
            dy_ref[...] = _rms_bwd(r, n, dc * bw_v[:, GW * g:GW * (g + 1)])

    yblk = pl.BlockSpec((tb, GW), lambda i: (i, 0))
    blk = pl.BlockSpec((tb, D), lambda i: (i, 0))
    ysd = jax.ShapeDtypeStruct((t, GW), F32)
    return pl.pallas_call(
        body, name="b_out", grid=(t // tb,), in_specs=[blk] + [yblk] * 4 + [_row(D), _full((D, D)), _row(D)],
        out_specs=[yblk] * 4 + [blk, _row(D)],
        out_shape=[ysd] * 4 + [jax.ShapeDtypeStruct((t, D), BF16), jax.ShapeDtypeStruct((1, D), F32)],
        compiler_params=_cparams(1),
    )(dh2, ya, yb, yc, yd, bw, w_out, g1)


HB = 1024


def _f_mlp(h2, nw, sc, sh, g2, w1, w2):
    t = h2.shape[0]
    tb = _tblock(t)
    nk = HID // HB

    def body(h_ref, nw_ref, sc_ref, sh_ref, g_ref, w1_ref, w2_ref, h3_ref, m_ref, a_ref, v_ref):
        k = pl.program_id(1)

        @pl.when(k == 0)
        def _():
            _, n = _rms(h_ref[...])
            v_ref[...] = ((n * nw_ref[...]) * (1.0 + sc_ref[...]) + sh_ref[...]).astype(BF16)
            m_ref[...] = jnp.zeros_like(m_ref)

        a = _dot(v_ref[...], w1_ref[...])
        a_ref[...] = a
        ra = jnp.maximum(a, 0.0)
        m_ref[...] += _dot((ra * ra).astype(BF16), w2_ref[...])

        @pl.when(k == nk - 1)
        def _():
            h3_ref[...] = h_ref[...] + g_ref[...] * m_ref[...]

    blk = pl.BlockSpec((tb, D), lambda i, k: (i, 0))
    return pl.pallas_call(
        body, name="f_mlp", grid=(t // tb, nk),
        in_specs=[blk, _row(D), _row(D), _row(D), _row(D), pl.BlockSpec((D, HB), lambda i, k: (0, k)),
                  pl.BlockSpec((HB, D), lambda i, k: (k, 0))],
        out_specs=[blk, blk, pl.BlockSpec((tb, HB), lambda i, k: (i, k)), blk],
        out_shape=[jax.ShapeDtypeStruct((t, D), F32), jax.ShapeDtypeStruct((t, D), F32), jax.ShapeDtypeStruct((t, HID), F32),
                   jax.ShapeDtypeStruct((t, D), BF16)],
        compiler_params=_cparams(2),
    )(h2, nw, sc, sh, g2, w1, w2)


def _b_mlp(dh3, a, g2, w1, w2):
    t = dh3.shape[0]
    tb = _tblock(t)
    nk = HID // HB

    def body(dh_ref, a_ref, g_ref, w1_ref, w2_ref, dv_ref, da_ref, act_ref, dm_ref):
        k = pl.program_id(1)
        dm = (dh_ref[...] * g_ref[...]).astype(BF16)

        @pl.when(k == 0)
        def _():
            dm_ref[...] = dm
            dv_ref[...] = jnp.zeros_like(dv_ref)

        ra = jnp.maximum(a_ref[...], 0.0)
        act_ref[...] = (ra * ra).astype(BF16)
        da = (_dot(dm, w2_ref[...], NT) * (2.0 * ra)).astype(BF16)
        da_ref[...] = da
        dv_ref[...] += _dot(da, w1_ref[...], NT)

    blk = pl.BlockSpec((tb, D), lambda i, k: (i, 0))
    hblk = pl.BlockSpec((tb, HB), lambda i, k: (i, k))
    return pl.pallas_call(
        body, name="b_mlp", grid=(t // tb, nk),
        in_specs=[blk, hblk, _row(D), pl.BlockSpec((D, HB), lambda i, k: (0, k)), pl.BlockSpec((HB, D), lambda i, k: (k, 0))],
        out_specs=[blk, hblk, hblk, blk],
        out_shape=[jax.ShapeDtypeStruct((t, D), F32), jax.ShapeDtypeStruct((t, HID), BF16), jax.ShapeDtypeStruct((t, HID), BF16),
                   jax.ShapeDtypeStruct((t, D), BF16)],
        compiler_params=_cparams(2),
    )(dh3, a, g2, w1, w2)


def _b_final(h, tgt, fw):
    t = h.shape[0]
    tb = _tblock(t)

    def body(h_ref, t_ref, w_ref, dh_ref, loss_ref, dfw_ref):
        @pl.when(pl.program_id(0) == 0)
        def _():
            loss_ref[...] = jnp.zeros_like(loss_ref)
            dfw_ref[...] = jnp.zeros_like(dfw_ref)

        r, n = _rms(h_ref[...])
        wv = w_ref[...]
        err = n * wv - t_ref[...]
        loss_ref[...] += jnp.sum(err * err, keepdims=True) * (0.5 / D)
        dy = err * (1.0 / D)
        dfw_ref[...] += _colsum(dy * n)
        dh_ref[...] = _rms_bwd(r, n, dy * wv)

    blk = pl.BlockSpec((tb, D), lambda i: (i, 0))
    return pl.pallas_call(
        body, name="b_final", grid=(t // tb,), in_specs=[blk, blk, _row(D)], out_specs=[blk, _row(1), _row(D)],
        out_shape=[jax.ShapeDtypeStruct((t, D), F32), jax.ShapeDtypeStruct((1, 1), F32), jax.ShapeDtypeStruct((1, D), F32)],
        compiler_params=_cparams(1),
    )(h, tgt, fw)


_EYE16 = None


def _eye(n):
    return jnp.eye(n, dtype=F32)


def _pool_embed(pool_w):
    return jnp.einsum('gcd,gk->gckd', pool_w, _eye(4)).reshape(GW, GW)


def _pool_extract(m):
    return jnp.einsum('gcgd->gcd', m.reshape(4, 64, 4, 64))


def _bmat_embed(bb):
    return jnp.einsum('gph,gk->ghkp', bb, _eye(16)).reshape(GW, S5_P)


def _bmat_extract(m):
    return jnp.einsum('ghgp->gph', m.reshape(16, 16, 16, 64))


def _cmat_embed(cc):
    return jnp.einsum('ghp,gk->kpgh', cc, _eye(16)).reshape(S5_P, GW)


def _cmat_extract(m):
    return jnp.einsum('gpgh->ghp', m.reshape(16, 64, 16, 16))


def _pad_lanes(v, n=DTW):
    return jnp.pad(v.reshape(1, -1), ((0, 0), (0, n - v.shape[-1])))


def _w_in_layout(w_in_t):
    w_main = jnp.concatenate([w_in_t[:1280], w_in_t[2052:2308], w_in_t[1280:2048]], axis=0)
    return w_main, jnp.pad(w_in_t[2048:2052], ((0, DTW - 4), (0, 0)))


def _layer_params(p, l, mod, w_in, rest):
    q = {'rest': rest}
    q['mod'] = [mod[k:k + 1] for k in range(6)]
    q['nw1'] = p['norm_mix_w'][l:l + 1]
    q['nw2'] = p['norm_mlp_w'][l:l + 1]
    q['w_main'], q['w_dt'] = _w_in_layout(w_in)
    q['pool_mat'] = _pool_embed(p['pool_w'][l]).astype(BF16)
    q['pool_scale'] = p['pool_scale'][l:l + 1]
    q['sconv_w'] = p['sconv_w'][l]
    q['conv_w'] = p['ssd_conv_w'][l]
    q['conv_b'] = p['ssd_conv_b'][l:l + 1]
    q['dt_bias'] = _pad_lanes(p['ssd_dt_bias'][l])
    q['a_log'] = _pad_lanes(p['ssd_a_log'][l])
    q['ssd_d'] = _pad_lanes(p['ssd_d'][l])
    q['s5_raw'] = (p['s5_a_re'][l], p['s5_a_im'][l], p['s5_log_step'][l].reshape(16, 1),
                   p['s5_b_re'][l].reshape(16, 1024), p['s5_b_im'][l].reshape(16, 1024))
    q['cre'] = _cmat_embed(p['s5_c_re'][l]).astype(BF16)
    q['cim'] = (-_cmat_embed(p['s5_c_im'][l])).astype(BF16)
    q['s5_d'] = p['s5_d'][l:l + 1]
    q['glu_w'] = p['s5_glu_w'][l].astype(BF16)
    q['glu_b'] = p['s5_glu_b'][l:l + 1]
    q['bw'] = p['branch_norm_w'][l:l + 1]
    return q


def _layer_fwd(h, q):
    sh1, sc1, g1, sh2, sc2, g2 = q['mod']
    t = h.shape[0]
    s = {'h': h}
    s['proj'], s['dtp'], s['u'] = _f_in(h, q['nw1'], sc1, sh1, q['w_main'], q['w_dt'])
    s['ya'], s['yb'] = _f_ab(s['proj'], q['pool_mat'], q['pool_scale'], q['sconv_w'])
    s['yc'], s['ypre'], s['sprev'] = _f_ssd(s['proj'], s['dtp'], q['conv_w'], q['conv_b'], q['dt_bias'], q['a_log'], q['ssd_d'])
    lr, li, bbr, bbi, ars, ais = _s5_prep(*q['s5_raw'])
    s['bmat'] = jnp.concatenate([_bmat_embed(bbr.reshape(16, 64, 16)), _bmat_embed(bbi.reshape(16, 64, 16))],
                                axis=1).astype(BF16)
    s['tables'] = _s5_tables(ars.reshape(1, S5_P), ais.reshape(1, S5_P))
    s['yd'], s['carries'], s['states'] = _f_s5(s['proj'], s['bmat'], q['cre'], q['cim'], s['tables'][0], s['tables'][1],
                                  q['s5_d'], q['glu_w'], q['glu_b'])
    q['w_out'], q['w1'], q['w2'] = q['rest']((s['ya'], s['yc'], s['yd']))
    s['h2'], s['o'], s['cat'] = _f_out(s['ya'], s['yb'], s['yc'], s['yd'], q['bw'], q['w_out'], h, g1)
    h3, s['m'], s['a'], s['v'] = _f_mlp(s['h2'], q['nw2'], sc2, sh2, g2, q['w1'], q['w2'])
    return h3, s


def _layer_bwd(dh3, q, s):
    sh1, sc1, g1, sh2, sc2, g2 = q['mod']
    g = {}
    dv, da, act, dm = _b_mlp(dh3, s['a'], g2, q['w1'], q['w2'])
    g['mlp_w1'] = _tn_matmul(s['v'], da, "dw1", col_major=True)
    g['mlp_w2'] = _tn_matmul(act, dm, "dw2")
    dh2, dsc2, dsh2, dnw2, dg2 = _b_normmod(dv, s['h2'], dh3, s['m'], q['nw2'], sc2, "b_norm_mlp")
    dya, dyb, dyc, dyd, do, dbw = _b_out(dh2, s['ya'], s['yb'], s['yc'], s['yd'], q['bw'], q['w_out'], g1)
    g['w_out'] = _tn_matmul(s['cat'], do, "dwout")
    g['branch_norm_w'] = dbw[0]
    dab, dpm, dps, dsw = _b_ab(s['proj'], dya, dyb, q['pool_mat'], q['pool_scale'], q['sconv_w'])
    g['pool_w'] = _pool_extract(dpm)
    g['pool_scale'] = dps[0]
    g['sconv_w'] = dsw
    dz, dxbc, ddt, dcw, dcb, ddtb, dal, ddk = _b_ssd(s['proj'], s['dtp'], s['ypre'], dyc, s['sprev'], q['conv_w'],
                                                     q['conv_b'], q['dt_bias'], q['a_log'], q['ssd_d'])
    g['ssd_conv_w'] = dcw
    g['ssd_conv_b'] = dcb[0]
    g['ssd_dt_bias'] = ddtb[0, :4]
    g['ssd_a_log'] = dal[0, :4]
    g['ssd_d'] = ddk[0, :4]
    tb = s['tables']
    ds5, dbmat, dcre, dcim, dlam, dd5, dgw, dgb = _b_s5(s['proj'], dyd, s['carries'], s['states'], s['bmat'], q['cre'], q['cim'],
                                                        tb[0], tb[1], q['s5_d'], q['glu_w'], q['glu_b'])
    g['s5_c_re'] = _cmat_extract(dcre)
    g['s5_c_im'] = -_cmat_extract(dcim)
    g['s5_d'] = dd5[0]
    g['s5_glu_w'] = dgw
    g['s5_glu_b'] = dgb[0]
    dbbr = _bmat_extract(dbmat[:, :S5_P]).reshape(16, 1024)
    dbbi = _bmat_extract(dbmat[:, S5_P:]).reshape(16, 1024)
    dar, dai, dls, dbr, dbi = _s5_prep_bwd(*q['s5_raw'], dlam[0].reshape(16, 64), dlam[1].reshape(16, 64), dbbr, dbbi)
    g['s5_a_re'], g['s5_a_im'], g['s5_log_step'] = dar, dai, dls[:, 0]
    g['s5_b_re'], g['s5_b_im'] = dbr, dbi
    du = _b_in_du(dab, dz, dxbc, ds5, ddt, q['w_main'], q['w_dt'])
    u = s['u']
    head = jnp.concatenate([_tn_matmul(dab, u, "dwin_ab"), _tn_matmul(dz, u, "dwin_z"), _tn_matmul(dxbc, u, "dwin_xbc"),
                            _tn_matmul(ddt, u, "dwin_dt")[:8]], axis=0)
    full = lax.dynamic_update_slice(jnp.zeros((2308, D), F32), head, (0, 0))
    g['w_in'] = lax.dynamic_update_slice(full, _tn_matmul(ds5, u, "dwin_s5"), (2052, 0))
    dh, dsc1, dsh1, dnw1, dg1 = _b_normmod(du, s['h'], dh2, s['o'], q['nw1'], sc1, "b_norm_mix")
    g['norm_mix_w'] = dnw1[0]
    g['norm_mlp_w'] = dnw2[0]
    dmod = jnp.concatenate([dsh1, dsc1, dg1, dsh2, dsc2, dg2], axis=1)
    return dh, g, dmod


def _local_step(x, tgt, p, mod, w_in_of, rest_of):
    h = x
    qs, saved = [], []
    for l in range(2):
        qs.append(_layer_params(p, l, mod[l], w_in_of(l), functools.partial(rest_of, l)))
        h, s = _layer_fwd(h, qs[l])
        saved.append(s)
    dh, loss, dfw = _b_final(h, tgt, p['final_norm_w'].reshape(1, D))
    grads = [None, None]
    dmods = [None, None]
    for l in (1, 0):
        dh, grads[l], dmods[l] = _layer_bwd(dh, qs[l], saved[l])
    out = {k: jnp.stack([grads[0][k], grads[1][k]]) for k in grads[0]}
    out['final_norm_w'] = dfw[0]
    return loss, dh, out, jnp.concatenate(dmods, axis=0)


def _pack(arrs):
    parts, rows = [], 0
    for a in arrs:
        f = a.reshape(-1).astype(F32)
        pad = (-f.shape[0]) % 1024
        f = jnp.pad(f, (0, pad)) if pad else f
        parts.append(f.reshape(-1, 128))
        rows += parts[-1].shape[0]
    if rows % 256:
        parts.append(jnp.zeros((256 - rows % 256, 128), F32))
    return jnp.concatenate(parts, axis=0)


def _unpack(buf, shapes):
    out, row = [], 0
    for shp in shapes:
        n = int(math.prod(shp)) if len(shp) else 1
        rows = (n + 1023) // 1024 * 8
        out.append(buf[row:row + rows].reshape(-1)[:n].reshape(shp))
        row += rows
    return out


def _shard_of(a, axis, k):
    n = a.shape[axis] // 4
    return lax.dynamic_slice_in_dim(a, k * n, n, axis)


def kernel(x, c, norm_mix_w, norm_mlp_w, ada_w, ada_b, w_in, pool_w, pool_scale, sconv_w, ssd_conv_w, ssd_conv_b, ssd_dt_bias, ssd_a_log, ssd_d, s5_a_re, s5_a_im, s5_log_step, s5_b_re, s5_b_im, s5_c_re, s5_c_im, s5_d, s5_glu_w, s5_glu_b, branch_norm_w, w_out, mlp_w1, mlp_w2, final_norm_w, loss_target, m_norm_mix_w, m_norm_mlp_w, m_ada_w, m_ada_b, m_w_in, m_pool_w, m_pool_scale, m_sconv_w, m_ssd_conv_w, m_ssd_conv_b, m_ssd_dt_bias, m_ssd_a_log, m_ssd_d, m_s5_a_re, m_s5_a_im, m_s5_log_step, m_s5_b_re, m_s5_b_im, m_s5_c_re, m_s5_c_im, m_s5_d, m_s5_glu_w, m_s5_glu_b, m_branch_norm_w, m_w_out, m_mlp_w1, m_mlp_w2, m_final_norm_w, v_norm_mix_w, v_norm_mlp_w, v_ada_w, v_ada_b, v_w_in, v_pool_w, v_pool_scale, v_sconv_w, v_ssd_conv_w, v_ssd_conv_b, v_ssd_dt_bias, v_ssd_a_log, v_ssd_d, v_s5_a_re, v_s5_a_im, v_s5_log_step, v_s5_b_re, v_s5_b_im, v_s5_c_re, v_s5_c_im, v_s5_d, v_s5_glu_w, v_s5_glu_b, v_branch_norm_w, v_w_out, v_mlp_w1, v_mlp_w2, v_final_norm_w):
    loc = locals()
    w = {n: loc[n] for n in WEIGHTS}
    mom = {n: loc['m_' + n] for n in WEIGHTS}
    var = {n: loc['v_' + n] for n in WEIGHTS}
    ix, iy, ic = lax.axis_index("x"), lax.axis_index("y"), lax.axis_index("c")
    chip = 2 * ix + iy
    dev = 4 * ix + 2 * iy + ic

    mine_of = lambda a: lax.dynamic_index_in_dim(a.astype(BF16), ic, axis=0, keepdims=False)
    pad_in = lambda a: jnp.pad(a.T, ((0, WIN_ROWS - 577), (0, 0)))
    shard = jnp.concatenate([pad_in(mine_of(w['w_in'])), mine_of(w['w_out']), mine_of(w['mlp_w1']), mine_of(w['mlp_w2'])], axis=0)

    (c_all,) = _exchange([c], EVERYONE, False, "ag_cond", stage=True)
    c_all = c_all.reshape(8, D)
    small_sh = _exchange([w[n] for n in SMALL_SHARDED], CHIPS, False, "ag_small")
    (w_in0,) = _exchange([pad_in(w['w_in'][0].astype(BF16))], CHIPS, False, "ag_win0")
    p = {n: w[n] for n in WEIGHTS if n not in BIG}
    for n, g in zip(SMALL_SHARDED, small_sh):
        ax = SMALL_SHARDED[n]
        p[n] = jnp.concatenate([g[k] for k in range(4)], axis=ax)

    def w_in_full(sh):
        return sh[:, :577].reshape(4 * 577, D)

    big = {}

    def fetch(after):
        if not big:
            got = _gather_wait(sems, shard_thru, land, after, "ag_big_wait")
            got = lax.dynamic_update_slice(got, shard[None], (chip, 0, 0))
            other = _pair_swap([got.reshape(-1, D)], False, "swap_big")[0].reshape(got.shape)
            big['both'] = [jnp.where(ic == l, got, other) for l in range(2)]
        return big['both']

    def w_in_of(l):
        return w_in_full(w_in0) if l == 0 else w_in_full(fetch(None)[1])

    def rest_of(l, after):
        blk = fetch(after)[l]
        r0 = WIN_ROWS
        w_out_l = blk[:, r0:r0 + 256].reshape(D, D)
        w1_l = jnp.concatenate([blk[k, r0 + 256:r0 + 1280] for k in range(4)], axis=1)
        w2_l = blk[:, r0 + 1280:r0 + 2304].reshape(HID, D)
        return w_out_l, w1_l, w2_l

    ada_b_sh = _shard_of(w['ada_b'], 1, chip).reshape(2, 1, 6 * D // 4)
    mod_sh = _ada_fwd(c_all, w['ada_w'], ada_b_sh)
    (mod_all,) = _exchange([mod_sh], CHIPS, False, "ag_mod", stage=True)
    mine = lax.dynamic_index_in_dim(mod_all, dev, axis=2, keepdims=False)
    sems, shard_thru, land, token = _gather_start(shard, [mod_all, w_in0] + small_sh, "ag_big_start")
    mod = jnp.transpose(mine, (1, 0, 2)).reshape(2, 6, D) + token[0, 0]

    loss, grad_x, g, dmod = _local_step(x[0], loss_target[0], p, mod, w_in_of, rest_of)

    (dmod_all,) = _exchange([dmod], EVERYONE, False, "ag_dmod", stage=True)
    dmod_all = jnp.transpose(dmod_all, (1, 0, 2))
    g_ada_w, g_ada_b = _ada_bwd(c_all, _shard_of(dmod_all, 2, chip), dmod_all)

    gw_in = jnp.pad(g['w_in'].reshape(2, 4, 577, D), ((0, 0), (0, 0), (0, WIN_ROWS - 577), (0, 0)))
    gw_out = g['w_out'].reshape(2, 4, 256, D)
    gw1 = g['mlp_w1']
    gw2 = g['mlp_w2'].reshape(2, 4, 1024, D)
    gws = [gw_in, gw_out, gw1, gw2]
    got = _pair_swap([a.reshape(2, -1, D) for a in gws], True, "swap_grad")
    layer = ic.astype(jnp.int32).reshape(1)
    pair = [_pair_sum(a, b.reshape(a.shape[1:]), layer, "pair_sum%d" % k, BF16) for k, (a, b) in enumerate(zip(gws, got))]
    quad = _exchange(pair, CHIPS, True, "rs_chips")
    quad = [_sum_lead(a, "rs_chip_sum%d" % k, F32) for k, a in enumerate(quad)]
    other = _pair_swap(quad, False, "swap_red")
    both = [jnp.stack([jnp.where(ic == l, a, b) for l in range(2)]) for a, b in zip(quad, other)]
    both[0] = jnp.transpose(both[0][:, :577], (0, 2, 1))
    red = dict(zip(('w_in', 'w_out', 'mlp_w1', 'mlp_w2'), both))
    red['ada_w'] = g_ada_w

    small_names = [n for n in WEIGHTS if n not in BIG and n != 'ada_b']
    gathered = _exchange([g[n] for n in small_names] + [loss], EVERYONE, False, "ag_smallgrad", stage=True)
    summed = _sum_many(gathered, "smallgrad_sum")
    for n, a in zip(small_names, summed[:-1]):
        a = a.reshape(w[n].shape) if n in ('s5_b_re', 's5_b_im') else a
        red[n] = _shard_of(a, SMALL_SHARDED[n], chip) if n in SMALL_SHARDED else a
    red['ada_b'] = g_ada_b
    loss_out = summed[-1].reshape(())

    delta, new_m, new_v = {}, {}, {}
    for n in BIG:
        delta[n], new_m[n], new_v[n] = _adamw(w[n], red[n], mom[n], var[n], "adamw_" + n)
    rest = [n for n in WEIGHTS if n not in BIG]
    outs = _adamw_many([w[n] for n in rest], [red[n] for n in rest], [mom[n] for n in rest], [var[n] for n in rest],
                       "adamw_small")
    for k, n in enumerate(rest):
        delta[n], new_m[n], new_v[n] = outs[3 * k], outs[3 * k + 1], outs[3 * k + 2]

    return (loss_out, grad_x[None], *[red[n] for n in WEIGHTS], *[delta[n] for n in WEIGHTS],
            *[new_m[n] for n in WEIGHTS], *[new_v[n] for n in WEIGHTS])
```

```python
import functools
import math

import jax
import jax.numpy as jnp
from jax import lax
from jax.experimental import pallas as pl
from jax.experimental.pallas import tpu as pltpu

F32 = jnp.float32
BF16 = jnp.bfloat16
HI = lax.Precision.HIGHEST

D = 1024
GW = 256
HID = 4096
EPS = 1e-6
PW = 2304
DTW = 128
SSD_L = 128
SSD_SUB = 2
SSD_SUB_BWD = 1
NH, HP, NS = 4, 64, 128
S5_P = 1024
MESH = pl.DeviceIdType.MESH

ADAM_LR, ADAM_B1, ADAM_B2, ADAM_EPS, ADAM_WD, ADAM_STEP = 0.001, 0.9, 0.999, 1e-08, 0.01, 10

NT = (((1,), (1,)), ((), ()))
TN = (((0,), (0,)), ((), ()))

WEIGHTS = ['norm_mix_w', 'norm_mlp_w', 'ada_w', 'ada_b', 'w_in', 'pool_w', 'pool_scale', 'sconv_w', 'ssd_conv_w',
           'ssd_conv_b', 'ssd_dt_bias', 'ssd_a_log', 'ssd_d', 's5_a_re', 's5_a_im', 's5_log_step', 's5_b_re', 's5_b_im',
           's5_c_re', 's5_c_im', 's5_d', 's5_glu_w', 's5_glu_b', 'branch_norm_w', 'w_out', 'mlp_w1', 'mlp_w2',
           'final_norm_w']
BIG = ('ada_w', 'w_in', 'w_out', 'mlp_w1', 'mlp_w2')
SMALL_SHARDED = {'sconv_w': 2, 'ssd_conv_w': 2, 's5_glu_w': 1}


def _cparams(n_axes, vmem_mb=48):
    return pltpu.CompilerParams(dimension_semantics=("arbitrary",) * n_axes, vmem_limit_bytes=vmem_mb * 1024 * 1024)


def _row(n):
    return pl.BlockSpec((1, n), lambda *_: (0, 0))


def _full(shape):
    nd = len(shape)
    return pl.BlockSpec(tuple(shape), lambda *_: (0,) * nd)


def _dot(a, b, dims=None, prec=None):
    if dims is None:
        dims = (((a.ndim - 1,), (0,)), ((), ()))
    return lax.dot_general(a, b, dims, preferred_element_type=F32, precision=prec)


def _bdot(a, b, dims=None):
    return _dot(a.astype(BF16), b.astype(BF16), dims)


def _sig(x):
    return jax.nn.sigmoid(x)


def _silu(x):
    return x * _sig(x)


def _dsilu(x):
    s = _sig(x)
    return s * (1.0 + x * (1.0 - s))


def _softplus(x):
    return jnp.maximum(x, 0.0) + jnp.log(1.0 + jnp.exp(-jnp.abs(x)))


_GK = math.sqrt(2.0 / math.pi)


def _gelu(x):
    return 0.5 * x * (1.0 + jnp.tanh(_GK * (x + 0.044715 * x * x * x)))


def _dgelu(x):
    th = jnp.tanh(_GK * (x + 0.044715 * x * x * x))
    return 0.5 * (1.0 + th) + 0.5 * x * (1.0 - th * th) * _GK * (1.0 + 3.0 * 0.044715 * x * x)


def _colsum(x):
    return jnp.sum(x, axis=0, keepdims=True)


def _rms(x):
    r = lax.rsqrt(jnp.mean(x * x, axis=-1, keepdims=True) + EPS)
    return r, x * r


def _rms_bwd(r, n, dn):
    return r * (dn - n * jnp.mean(dn * n, axis=-1, keepdims=True))


def _roll(x, k):
    n = x.shape[0]
    k = k % n
    return x if k == 0 else pltpu.roll(x, k, axis=0)


def _tblock(t, want=512):
    return min(t, want)


def _peer(mask):
    x, y, c = lax.axis_index("x"), lax.axis_index("y"), lax.axis_index("c")
    return (x ^ ((mask >> 2) & 1), y ^ ((mask >> 1) & 1), c ^ (mask & 1))


def _group_index(masks):
    x, y, c = lax.axis_index("x"), lax.axis_index("y"), lax.axis_index("c")
    full = 0
    for m in masks:
        full |= m
    bits = [b for b in (4, 2, 1) if full & b]

    def idx(px, py, pc):
        v = {4: px, 2: py, 1: pc}
        out = 0
        for b in bits:
            out = out * 2 + v[b]
        return out

    return idx(x, y, c), [idx(*_peer(m)) for m in masks]


def _exchange(arrs, masks, scatter, name, stage=False):
    n_arr, n_peer, n_grp = len(arrs), len(masks), len(masks) + 1

    def body(*refs):
        ins, outs = refs[:n_arr], refs[n_arr:2 * n_arr]
        send_sems, recv_sems, local_sems = refs[2 * n_arr:]
        me, peer_idx = _group_index(masks)
        copies = []
        for t in range(n_arr):
            src_me = ins[t].at[me] if scatter else ins[t]
            loc = pltpu.make_async_copy(src_me, outs[t].at[me], local_sems.at[t])
            loc.start()
            copies.append(loc)
            for j, m in enumerate(masks):
                src = ins[t].at[peer_idx[j]] if scatter else ins[t]
                cp = pltpu.make_async_remote_copy(src_ref=src, dst_ref=outs[t].at[me], send_sem=send_sems.at[t, j],
                                                  recv_sem=recv_sems.at[t, j], device_id=_peer(m), device_id_type=MESH)
                cp.start()
                copies.append(cp)
        for cp in copies:
            cp.wait()

    hbm = pl.BlockSpec(memory_space=pl.ANY)
    out_shape = [jax.ShapeDtypeStruct((n_grp,) + (a.shape[1:] if scatter else a.shape), a.dtype) for a in arrs]
    src_spec = pl.BlockSpec(memory_space=pltpu.VMEM) if stage else hbm
    outs = pl.pallas_call(
        body, name=name, in_specs=[src_spec] * n_arr, out_specs=[hbm] * n_arr, out_shape=out_shape,
        scratch_shapes=[pltpu.SemaphoreType.DMA((n_arr, n_peer)), pltpu.SemaphoreType.DMA((n_arr, n_peer)),
                        pltpu.SemaphoreType.DMA((n_arr,))],
    )(*arrs)
    return list(outs)


def _gather_copies(src_ref, land_ref, send_sems, recv_sems):
    me, _ = _group_index(CHIPS)
    return [pltpu.make_async_remote_copy(src_ref=src_ref, dst_ref=land_ref.at[me], send_sem=send_sems[j], recv_sem=recv_sems[j],
                                         device_id=_peer(m), device_id_type=MESH) for j, m in enumerate(CHIPS)]


def _gather_start(src, after, name):
    n = len(CHIPS)

    def body(src_ref, land_ref, *rest):
        sems, token = rest[len(after):len(after) + 2 * n], rest[-1]
        for cp in _gather_copies(src_ref, land_ref, sems[:n], sems[n:]):
            cp.start()
        token[...] = jnp.zeros_like(token)

    hbm = pl.BlockSpec(memory_space=pltpu.HBM)
    sem = pl.BlockSpec(memory_space=pltpu.SEMAPHORE)
    land = lax.empty((n + 1,) + src.shape, src.dtype)
    outs = pl.pallas_call(
        body, name=name,
        out_shape=(pltpu.SemaphoreType.DMA(()),) * (2 * n) + (pltpu.HBM(src.shape, src.dtype), pltpu.HBM(land.shape, land.dtype),
                                                              jax.ShapeDtypeStruct((8, 128), F32)),
        in_specs=(hbm, hbm) + (pl.BlockSpec(memory_space=pl.ANY),) * len(after),
        out_specs=(sem,) * (2 * n) + (hbm, hbm, pl.BlockSpec(memory_space=pltpu.VMEM)),
        input_output_aliases={0: 2 * n, 1: 2 * n + 1},
        compiler_params=pltpu.CompilerParams(has_side_effects=pltpu.SideEffectType.DATAFLOW_SIDE_EFFECTING),
    )(pltpu.with_memory_space_constraint(src, pltpu.HBM), pltpu.with_memory_space_constraint(land, pltpu.HBM), *after)
    return outs[:2 * n], outs[2 * n], outs[2 * n + 1], outs[2 * n + 2]


def _gather_wait(sems, src, land, after, name):
    n = len(CHIPS)

    def body(src_ref, land_ref, *rest):
        for cp in _gather_copies(src_ref, land_ref, rest[:n], rest[n:2 * n]):
            cp.wait_send()
            cp.wait_recv()

    hbm = pl.BlockSpec(memory_space=pltpu.HBM)
    sem = pl.BlockSpec(memory_space=pltpu.SEMAPHORE)
    return pl.pallas_call(
        body, name=name, out_shape=(pltpu.HBM(src.shape, src.dtype), pltpu.HBM(land.shape, land.dtype)),
        in_specs=(hbm, hbm) + (sem,) * (2 * n) + (pl.BlockSpec(memory_space=pl.ANY),) * len(after), out_specs=(hbm, hbm),
        input_output_aliases={0: 0, 1: 1},
        compiler_params=pltpu.CompilerParams(has_side_effects=pltpu.SideEffectType.DATAFLOW_SIDE_EFFECTING),
    )(src, land, *sems, *after)[1]


CHIPS = (4, 2, 6)
EVERYONE = (1, 2, 3, 4, 5, 6, 7)
SIBLING = (1,)
SWAP_ROWS = 512
WIN_ROWS = 592


def _pair_swap(arrs, other_layer, name):
    n_arr = len(arrs)
    shapes = [a.shape[-2:] for a in arrs]
    chunks = []
    for t, (rows, _) in enumerate(shapes):
        assert rows % 16 == 0
        for j, r0 in enumerate(range(0, rows, SWAP_ROWS)):
            chunks.append((t, r0, min(SWAP_ROWS, rows - r0), j % 2))

    def body(*refs):
        ins, outs = refs[:n_arr], refs[n_arr:2 * n_arr]
        bufs = refs[2 * n_arr:3 * n_arr]
        load_sems, send_sems, recv_sems = refs[3 * n_arr:]
        sibling = _peer(1)
        c = lax.axis_index("c")

        def load(k):
            t, r0, n, slot = chunks[k]
            src = ins[t].at[1 - c] if other_layer else ins[t]
            return pltpu.make_async_copy(src.at[pl.ds(r0, n)], bufs[t].at[slot, pl.ds(0, n)], load_sems.at[t, slot])

        def send(k):
            t, r0, n, slot = chunks[k]
            return pltpu.make_async_remote_copy(src_ref=bufs[t].at[slot, pl.ds(0, n)], dst_ref=outs[t].at[pl.ds(r0, n)],
                                                send_sem=send_sems.at[t, slot], recv_sem=recv_sems.at[t],
                                                device_id=sibling, device_id_type=MESH)

        in_flight = {}

        def start_load(k):
            key = (chunks[k][0], chunks[k][3])
            if key in in_flight:
                send(in_flight.pop(key)).wait_send()
            load(k).start()

        start_load(0)
        for k in range(len(chunks)):
            load(k).wait()
            if k + 1 < len(chunks):
                start_load(k + 1)
            send(k).start()
            in_flight[(chunks[k][0], chunks[k][3])] = k
        for k in in_flight.values():
            send(k).wait_send()
        for t in range(n_arr):
            pltpu.make_async_remote_copy(src_ref=outs[t], dst_ref=outs[t], send_sem=send_sems.at[t, 0],
                                         recv_sem=recv_sems.at[t], device_id=sibling, device_id_type=MESH).wait_recv()

    hbm = pl.BlockSpec(memory_space=pl.ANY)
    outs = pl.pallas_call(
        body, name=name, in_specs=[hbm] * n_arr, out_specs=[hbm] * n_arr,
        out_shape=[jax.ShapeDtypeStruct(s, a.dtype) for s, a in zip(shapes, arrs)],
        scratch_shapes=[pltpu.VMEM((2, min(SWAP_ROWS, s[0]), s[1]), a.dtype) for s, a in zip(shapes, arrs)]
        + [pltpu.SemaphoreType.DMA((n_arr, 2)), pltpu.SemaphoreType.DMA((n_arr, 2)), pltpu.SemaphoreType.DMA((n_arr,))],
        compiler_params=pltpu.CompilerParams(vmem_limit_bytes=48 * 1024 * 1024),
    )(*arrs)
    return list(outs)


def _sum_lead(a, name, out_dtype):
    n = a.shape[0]
    shape = a.shape[1:]

    def body(a_ref, o_ref):
        acc = a_ref[0].astype(F32)
        for k in range(1, n):
            acc = acc + a_ref[k].astype(F32)
        o_ref[...] = acc.astype(out_dtype)

    if len(shape) == 3:
        blk = (1,) + shape[1:]
        return pl.pallas_call(
            body, name=name, grid=(shape[0],), in_specs=[pl.BlockSpec((n,) + blk, lambda i: (0, i, 0, 0))],
            out_specs=pl.BlockSpec(blk, lambda i: (i, 0, 0)), out_shape=jax.ShapeDtypeStruct(shape, out_dtype),
            compiler_params=_cparams(1),
        )(a)
    rows, cols = shape
    rb = rows
    for cand in (512, 256, 128):
        if rows % cand == 0 and rows > cand:
            rb = cand
            break
    return pl.pallas_call(
        body, name=name, grid=(rows // rb,), in_specs=[pl.BlockSpec((n, rb, cols), lambda i: (0, i, 0))],
        out_specs=pl.BlockSpec((rb, cols), lambda i: (i, 0)), out_shape=jax.ShapeDtypeStruct((rows, cols), out_dtype),
        compiler_params=_cparams(1),
    )(a)


def _pair_sum(g, recv, layer, name, out_dtype):
    _, n, r, c = g.shape

    def body(l_ref, g_ref, r_ref, o_ref):
        o_ref[...] = (g_ref[0].astype(F32) + r_ref[...].astype(F32)).astype(out_dtype)

    return pl.pallas_call(
        body, name=name,
        grid_spec=pltpu.PrefetchScalarGridSpec(
            num_scalar_prefetch=1, grid=(n,),
            in_specs=[pl.BlockSpec((1, 1, r, c), lambda i, l: (l[0], i, 0, 0)), pl.BlockSpec((1, r, c), lambda i, l: (i, 0, 0))],
            out_specs=pl.BlockSpec((1, r, c), lambda i, l: (i, 0, 0))),
        out_shape=jax.ShapeDtypeStruct((n, r, c), out_dtype), compiler_params=_cparams(1),
    )(layer, g, recv)


def _tn_matmul(a, b, name, col_major=False):
    t, k = a.shape
    n = b.shape[1]
    tb = _tblock(t, 1024)
    kb = min(k, 1024)
    nb = min(n, 1024)
    grid = (k // kb, n // nb, t // tb)

    def body(a_ref, b_ref, o_ref):
        @pl.when(pl.program_id(2) == 0)
        def _():
            o_ref[...] = jnp.zeros_like(o_ref)

        acc = _bdot(a_ref[...], b_ref[...], TN)
        if col_major:
            o_ref[0] += acc
        else:
            o_ref[...] += acc

    if col_major:
        out_spec = pl.BlockSpec((1, kb, nb), lambda ki, ni, ti: (ni, ki, 0))
        out_shape = jax.ShapeDtypeStruct((n // nb, k, nb), F32)
    else:
        out_spec = pl.BlockSpec((kb, nb), lambda ki, ni, ti: (ki, ni))
        out_shape = jax.ShapeDtypeStruct((k, n), F32)
    return pl.pallas_call(
        body, name=name, grid=grid,
        in_specs=[pl.BlockSpec((tb, kb), lambda ki, ni, ti: (ti, ki)), pl.BlockSpec((tb, nb), lambda ki, ni, ti: (ti, ni))],
        out_specs=out_spec, out_shape=out_shape, compiler_params=_cparams(3),
    )(a, b)


def _sum_many(arrs, name):
    k = len(arrs)

    def body(*refs):
        for a_ref, o_ref in zip(refs[:k], refs[k:]):
            acc = a_ref[0]
            for j in range(1, a_ref.shape[0]):
                acc = acc + a_ref[j]
            o_ref[...] = acc

    return pl.pallas_call(body, name=name, out_shape=[jax.ShapeDtypeStruct(a.shape[1:], F32) for a in arrs],
                          compiler_params=pltpu.CompilerParams(vmem_limit_bytes=48 * 1024 * 1024))(*arrs)


def _adamw_math(w, g, m, v):
    m2 = ADAM_B1 * m + (1.0 - ADAM_B1) * g
    v2 = ADAM_B2 * v + (1.0 - ADAM_B2) * (g * g)
    m_hat = m2 / (1.0 - ADAM_B1 ** ADAM_STEP)
    v_hat = v2 / (1.0 - ADAM_B2 ** ADAM_STEP)
    return -ADAM_LR * (m_hat / (jnp.sqrt(v_hat) + ADAM_EPS) + ADAM_WD * w), m2, v2


def _adamw_many(ws, gs, ms, vs, name):
    n = len(ws)

    def body(*refs):
        ins, outs = refs[:4 * n], refs[4 * n:]
        for k in range(n):
            res = _adamw_math(ins[k][...], ins[n + k][...], ins[2 * n + k][...], ins[3 * n + k][...])
            for j in range(3):
                outs[3 * k + j][...] = res[j]

    out_shape = []
    for a in ws:
        out_shape += [jax.ShapeDtypeStruct(a.shape, F32)] * 3
    return pl.pallas_call(body, name=name, out_shape=out_shape,
                          compiler_params=pltpu.CompilerParams(vmem_limit_bytes=48 * 1024 * 1024))(*ws, *gs, *ms, *vs)


def _adamw(w, g, m, v, name):
    shape = w.shape
    cols = shape[-1]
    rows = int(math.prod(shape[:-1]))
    rb = rows
    for cand in (256, 128, 64, 32, 16, 8):
        if rows % cand == 0 and rows > cand:
            rb = cand
            break
    bc1 = 1.0 - ADAM_B1 ** ADAM_STEP
    bc2 = 1.0 - ADAM_B2 ** ADAM_STEP

    def body(w_ref, g_ref, m_ref, v_ref, d_ref, nm_ref, nv_ref):
        gg = g_ref[...]
        m2 = ADAM_B1 * m_ref[...] + (1.0 - ADAM_B1) * gg
        v2 = ADAM_B2 * v_ref[...] + (1.0 - ADAM_B2) * (gg * gg)
        m_hat = m2 / bc1
        v_hat = v2 / bc2
        d_ref[...] = -ADAM_LR * (m_hat / (jnp.sqrt(v_hat) + ADAM_EPS) + ADAM_WD * w_ref[...])
        nm_ref[...] = m2
        nv_ref[...] = v2

    spec = pl.BlockSpec((rb, cols), lambda i: (i, 0))
    sds = jax.ShapeDtypeStruct((rows, cols), F32)
    outs = pl.pallas_call(
        body, name=name, grid=(rows // rb,), in_specs=[spec] * 4, out_specs=[spec] * 3, out_shape=[sds] * 3,
        compiler_params=_cparams(1),
    )(*(z.reshape(rows, cols) for z in (w, g, m, v)))
    return tuple(o.reshape(shape) for o in outs)


def _ada_fwd(c_all, ada_w_sh, ada_b_sh):
    s = ada_w_sh.shape[2]
    sb = 512

    def body(c_ref, w_ref, b_ref, o_ref):
        cond = _silu(c_ref[...])
        o_ref[0] = _bdot(cond, w_ref[0]) + b_ref[0]

    return pl.pallas_call(
        body, name="ada_fwd", grid=(2, s // sb),
        in_specs=[_full((8, D)), pl.BlockSpec((1, D, sb), lambda l, j: (l, 0, j)), pl.BlockSpec((1, 1, sb), lambda l, j: (l, 0, j))],
        out_specs=pl.BlockSpec((1, 8, sb), lambda l, j: (l, 0, j)), out_shape=jax.ShapeDtypeStruct((2, 8, s), F32),
        compiler_params=_cparams(2),
    )(c_all, ada_w_sh, ada_b_sh)


def _ada_bwd(c_all, dmod_sh, dmod_all):
    s = dmod_sh.shape[2]
    sb = 512

    def body(c_ref, d_ref, o_ref):
        cond = _silu(c_ref[...])
        o_ref[0] = _bdot(cond, d_ref[0], TN)

    gw = pl.pallas_call(
        body, name="ada_bwd_w", grid=(2, s // sb),
        in_specs=[_full((8, D)), pl.BlockSpec((1, 8, sb), lambda l, j: (l, 0, j))],
        out_specs=pl.BlockSpec((1, D, sb), lambda l, j: (l, 0, j)), out_shape=jax.ShapeDtypeStruct((2, D, s), F32),
        compiler_params=_cparams(2),
    )(c_all, dmod_sh)

    def body_b(d_ref, o_ref):
        acc = d_ref[0, 0:1, :]
        for k in range(1, 8):
            acc = acc + d_ref[0, k:k + 1, :]
        o_ref[0] = acc

    gb = pl.pallas_call(
        body_b, name="ada_bwd_b", grid=(2,), in_specs=[pl.BlockSpec((1, 8, 6 * D), lambda l: (l, 0, 0))],
        out_specs=pl.BlockSpec((1, 1, 6 * D), lambda l: (l, 0, 0)), out_shape=jax.ShapeDtypeStruct((2, 1, 6 * D), F32),
        compiler_params=_cparams(1),
    )(dmod_all)
    return gw, gb.reshape(2, 6 * D)


def _f_in(h, nw, sc, sh, w_main, w_dt):
    t = h.shape[0]
    tb = _tblock(t)

    def body(h_ref, nw_ref, sc_ref, sh_ref, w_ref, wd_ref, p_ref, dt_ref, u_ref):
        _, n = _rms(h_ref[...])
        u = ((n * nw_ref[...]) * (1.0 + sc_ref[...]) + sh_ref[...]).astype(BF16)
        u_ref[...] = u
        p_ref[...] = _dot(u, w_ref[...], NT)
        dt_ref[...] = _dot(u, wd_ref[...], NT)

    return pl.pallas_call(
        body, name="f_in", grid=(t // tb,),
        in_specs=[pl.BlockSpec((tb, D), lambda i: (i, 0)), _row(D), _row(D), _row(D), _full((PW, D)), _full((DTW, D))],
        out_specs=[pl.BlockSpec((tb, PW), lambda i: (i, 0)), pl.BlockSpec((tb, DTW), lambda i: (i, 0)),
                   pl.BlockSpec((tb, D), lambda i: (i, 0))],
        out_shape=[jax.ShapeDtypeStruct((t, PW), F32), jax.ShapeDtypeStruct((t, DTW), F32), jax.ShapeDtypeStruct((t, D), BF16)],
        compiler_params=_cparams(1),
    )(h, nw, sc, sh, w_main, w_dt)


def _b_in_du(dab, dz, dxbc, ds5, ddt, w_main, w_dt):
    t = dab.shape[0]
    tb = _tblock(t)

    def body(a_ref, z_ref, x_ref, s_ref, d_ref, w_ref, wd_ref, o_ref):
        acc = _bdot(a_ref[...], w_ref[0:1024, :])
        acc += _bdot(z_ref[...], w_ref[1024:1280, :])
        acc += _bdot(s_ref[...], w_ref[1280:1536, :])
        acc += _bdot(x_ref[...], w_ref[1536:2304, :])
        acc += _bdot(d_ref[...], wd_ref[...])
        o_ref[...] = acc

    blk = lambda n: pl.BlockSpec((tb, n), lambda i: (i, 0))
    return pl.pallas_call(
        body, name="b_in_du", grid=(t // tb,),
        in_specs=[blk(1024), blk(256), blk(768), blk(256), blk(DTW), _full((PW, D)), _full((DTW, D))],
        out_specs=blk(D), out_shape=jax.ShapeDtypeStruct((t, D), F32), compiler_params=_cparams(1),
    )(dab, dz, dxbc, ds5, ddt, w_main, w_dt)


def _b_normmod(du, x, dres, gated, nw, sc, name):
    t = x.shape[0]
    tb = _tblock(t)

    def body(du_ref, x_ref, dr_ref, g_ref, nw_ref, sc_ref, dx_ref, dsc_ref, dsh_ref, dnw_ref, dg_ref):
        @pl.when(pl.program_id(0) == 0)
        def _():
            for r in (dsc_ref, dsh_ref, dnw_ref, dg_ref):
                r[...] = jnp.zeros_like(r)

        du_v = du_ref[...]
        r, n = _rms(x_ref[...])
        nwv = nw_ref[...]
        scale = 1.0 + sc_ref[...]
        dsc_ref[...] += _colsum(du_v * (n * nwv))
        dsh_ref[...] += _colsum(du_v)
        dnw_ref[...] += _colsum(du_v * scale * n)
        dres_v = dr_ref[...]
        dg_ref[...] += _colsum(dres_v * g_ref[...])
        dx_ref[...] = dres_v + _rms_bwd(r, n, du_v * scale * nwv)

    blk = pl.BlockSpec((tb, D), lambda i: (i, 0))
    row = jax.ShapeDtypeStruct((1, D), F32)
    return pl.pallas_call(
        body, name=name, grid=(t // tb,), in_specs=[blk, blk, blk, blk, _row(D), _row(D)],
        out_specs=[blk, _row(D), _row(D), _row(D), _row(D)], out_shape=[jax.ShapeDtypeStruct((t, D), F32), row, row, row, row],
        compiler_params=_cparams(1),
    )(du, x, dres, gated, nw, sc)


HALO = 16


def _lane_group(shape):
    return lax.broadcasted_iota(jnp.int32, shape, 1) // 64


def _window_select(g, s2, s4, s8, s16):
    return jnp.where(g == 0, s2, jnp.where(g == 1, s4, jnp.where(g == 2, s8, s16)))


def _pool_count(t0, rows):
    g = _lane_group((rows, GW))
    win = _window_select(g, 2, 4, 8, 16)
    tt = t0 + lax.broadcasted_iota(jnp.int32, (rows, GW), 0)
    return jnp.minimum(tt + 1, win).astype(F32)


def _pool_p(v_ext, t0, tb):
    s2 = v_ext + _roll(v_ext, 1)
    s4 = s2 + _roll(s2, 2)
    s8 = s4 + _roll(s4, 4)
    s16 = s8 + _roll(s8, 8)
    ws = _window_select(_lane_group(v_ext.shape), s2, s4, s8, s16)[HALO:]
    return ws / _pool_count(t0, tb) - v_ext[HALO:]


def _sconv(q_ext, w):
    return (_roll(q_ext, 2) * w[0:1] + _roll(q_ext, 1) * w[1:2] + q_ext * w[2:3])[HALO:]


def _halo_specs(t, tb, cols, col_block):
    per = tb // HALO
    last = t // HALO - 1
    prev = pl.BlockSpec((HALO, cols), lambda i: (jnp.maximum(i * per - 1, 0), col_block))
    nxt = pl.BlockSpec((HALO, cols), lambda i: (jnp.minimum((i + 1) * per, last), col_block))
    return prev, nxt


def _f_ab(proj, pool_mat, pool_scale, sconv_w):
    t = proj.shape[0]
    tb = _tblock(t)
    prev, _ = _halo_specs(t, tb, 1024, 0)

    def body(p_ref, h_ref, pm_ref, ps_ref, sw_ref, ya_ref, yb_ref):
        i = pl.program_id(0)
        halo = jnp.where(i > 0, h_ref[...], 0.0)
        ext = jnp.concatenate([halo, p_ref[...]], axis=0)
        p = _pool_p(ext[:, 0:256], i * tb, tb)
        ya_ref[...] = _bdot(p, pm_ref[...]) * ps_ref[...]
        q_ext = ext[:, 512:768] * ext[:, 768:1024]
        yb_ref[...] = p_ref[:, 256:512] * _sconv(q_ext, sw_ref[...])

    blk = pl.BlockSpec((tb, GW), lambda i: (i, 0))
    sds = jax.ShapeDtypeStruct((t, GW), F32)
    return pl.pallas_call(
        body, name="f_ab", grid=(t // tb,),
        in_specs=[pl.BlockSpec((tb, 1024), lambda i: (i, 0)), prev, _full((GW, GW)), _row(GW), _full((3, GW))],
        out_specs=[blk, blk], out_shape=[sds, sds], compiler_params=_cparams(1),
    )(proj, proj, pool_mat, pool_scale, sconv_w)


def _b_ab(proj, dya, dyb, pool_mat, pool_scale, sconv_w):
    t = proj.shape[0]
    tb = _tblock(t)
    nb = t // tb
    prev, nxt = _halo_specs(t, tb, 1024, 0)
    _, nxt_g = _halo_specs(t, tb, GW, 0)
    n_ext = tb + HALO

    def body(p_ref, hp_ref, hn_ref, da_ref, dan_ref, db_ref, dbn_ref, pm_ref, ps_ref, sw_ref,
             o_ref, dpm_ref, dps_ref, dsw_ref):
        i = pl.program_id(0)

        @pl.when(i == 0)
        def _():
            for r in (dpm_ref, dps_ref, dsw_ref):
                r[...] = jnp.zeros_like(r)

        last = i == nb - 1
        halo = jnp.where(i > 0, hp_ref[...], 0.0)
        main = p_ref[...]
        ext = jnp.concatenate([halo, main], axis=0)
        scale = ps_ref[...]
        pm = pm_ref[...]
        p = _pool_p(ext[:, 0:256], i * tb, tb)
        da = da_ref[...]
        dps_ref[...] += _colsum(da * _bdot(p, pm))
        da_ext = jnp.concatenate([da, jnp.where(last, 0.0, dan_ref[...])], axis=0)
        dys = da_ext * scale
        dpm_ref[...] += _bdot(p, dys[:tb], TN)
        dp = _bdot(dys, pm, NT)
        dpc = dp / _pool_count(i * tb, n_ext)
        a2 = dpc + _roll(dpc, n_ext - 1)
        a4 = a2 + _roll(a2, n_ext - 2)
        a8 = a4 + _roll(a4, n_ext - 4)
        a16 = a8 + _roll(a8, n_ext - 8)
        o_ref[:, 0:256] = (_window_select(_lane_group(dpc.shape), a2, a4, a8, a16) - dp)[:tb]
        w = sw_ref[...]
        gb, gc, hh = main[:, 256:512], main[:, 512:768], main[:, 768:1024]
        q_ext = ext[:, 512:768] * ext[:, 768:1024]
        db = db_ref[...]
        o_ref[:, 256:512] = db * _sconv(q_ext, w)
        gb_next = hn_ref[:, 256:512]
        dconv = jnp.concatenate([db * gb, jnp.where(last, 0.0, dbn_ref[...] * gb_next)], axis=0)
        dq = (dconv * w[2:3] + _roll(dconv, n_ext - 1) * w[1:2] + _roll(dconv, n_ext - 2) * w[0:1])[:tb]
        o_ref[:, 512:768] = dq * hh
        o_ref[:, 768:1024] = dq * gc
        dc = dconv[:tb]
        dsw_ref[0:1, :] += _colsum(dc * _roll(q_ext, 2)[HALO:])
        dsw_ref[1:2, :] += _colsum(dc * _roll(q_ext, 1)[HALO:])
        dsw_ref[2:3, :] += _colsum(dc * q_ext[HALO:])

    blk = pl.BlockSpec((tb, GW), lambda i: (i, 0))
    return pl.pallas_call(
        body, name="b_ab", grid=(nb,),
        in_specs=[pl.BlockSpec((tb, 1024), lambda i: (i, 0)), prev, nxt, blk, nxt_g, blk, nxt_g,
                  _full((GW, GW)), _row(GW), _full((3, GW))],
        out_specs=[pl.BlockSpec((tb, 1024), lambda i: (i, 0)), _full((GW, GW)), _row(GW), _full((3, GW))],
        out_shape=[jax.ShapeDtypeStruct((t, 1024), F32), jax.ShapeDtypeStruct((GW, GW), F32),
                   jax.ShapeDtypeStruct((1, GW), F32), jax.ShapeDtypeStruct((3, GW), F32)],
        compiler_params=_cparams(1),
    )(proj, proj, proj, dya, dya, dyb, dyb, pool_mat, pool_scale, sconv_w)


CH = 8


def _ssd_conv(x, halo, w, b):
    ext = jnp.concatenate([halo, x], axis=0)
    pre = ext * w[3:4] + _roll(ext, 1) * w[2:3] + _roll(ext, 2) * w[1:2] + _roll(ext, 3) * w[0:1] + b
    return pre[CH:], ext


def _ssd_common(dt_raw, dtb, alog):
    ll = dt_raw.shape[0]
    dtv = _softplus(dt_raw + dtb)
    a_row = -jnp.exp(alog)
    r = lax.broadcasted_iota(jnp.int32, (ll, ll), 0)
    c = lax.broadcasted_iota(jnp.int32, (ll, ll), 1)
    tril = (r >= c).astype(F32)
    cs = _dot(tril, dtv * a_row, prec=HI)
    return dtv, a_row, cs, cs.T, r >= c


def _bd(a, b, ca, cb):
    return lax.dot_general(a, b, (((ca,), (cb,)), ((0,), (0,))), preferred_element_type=F32)


def _head_cols(m):
    return jnp.stack([m[:, h:h + 1] for h in range(NH)])


def _ssd_heads(act, dtv, cs, cs_t, causal):
    xs = jnp.stack([act[:, HP * h:HP * (h + 1)] for h in range(NH)])
    bm = jnp.stack([act[:, 256 + NS * (h // 2):256 + NS * (h // 2 + 1)] for h in range(NH)])
    cm = jnp.stack([act[:, 512 + NS * (h // 2):512 + NS * (h // 2 + 1)] for h in range(NH)])
    cs_c = _head_cols(cs)
    cs_r = jnp.stack([cs_t[h:h + 1, :] for h in range(NH)])
    mdec = jnp.where(causal[None], jnp.exp(jnp.minimum(cs_c - cs_r, 0.0)), 0.0)
    g2 = _bd(jnp.stack([cm[0], cm[2]]), jnp.stack([bm[0], bm[2]]), 2, 2)
    sc = jnp.stack([g2[h // 2] for h in range(NH)]) * mdec
    dt_c = _head_cols(dtv)
    xdt = xs * dt_c
    e = jnp.exp(cs_c)
    cs_last = cs_c[:, SSD_L - 1:SSD_L, :]
    wdec = jnp.exp(cs_last - cs_c)
    return xs, bm, cm, mdec, sc, dt_c, xdt, e, cs_last, wdec


def _head_scalars(row_ref):
    return jnp.stack([row_ref[0:1, h:h + 1] for h in range(NH)])


def _f_ssd(proj, dtp, conv_w, conv_b, dt_bias, a_log, d_skip):
    t = proj.shape[0]
    nc = t // SSD_L
    rows = SSD_SUB * SSD_L
    per = rows // CH

    def body(x_ref, hx_ref, dt_ref, z_ref, cw_ref, cb_ref, dtb_ref, al_ref, dk_ref, y_ref, yp_ref, sp_ref, s_ref):
        i = pl.program_id(0)

        @pl.when(i == 0)
        def _():
            s_ref[...] = jnp.zeros_like(s_ref)

        state = s_ref[...]
        dk = _head_scalars(dk_ref)
        for sub in range(SSD_SUB):
            r0 = sub * SSD_L
            rs = slice(r0, r0 + SSD_L)
            halo = jnp.where(i > 0, hx_ref[...], 0.0) if sub == 0 else x_ref[r0 - CH:r0, :]
            pre, _ = _ssd_conv(x_ref[rs, :], halo, cw_ref[...], cb_ref[...])
            act = _silu(pre)
            dtv, _, cs, cs_t, causal = _ssd_common(dt_ref[rs, :], dtb_ref[...], al_ref[...])
            xs, bm, cm, _, sc, _, xdt, e, cs_last, wdec = _ssd_heads(act, dtv, cs, cs_t, causal)
            sp_ref[sub] = state
            y = _bd(sc, xdt, 2, 1) + e * _bd(cm, state, 2, 2) + xs * dk
            for h in range(NH):
                yp_ref[rs, HP * h:HP * (h + 1)] = y[h]
            state = state * jnp.exp(cs_last) + _bd(xdt * wdec, bm, 1, 1)
            y_ref[rs, :] = yp_ref[rs, :] * _silu(z_ref[rs, :])
        s_ref[...] = state

    blk = pl.BlockSpec((rows, GW), lambda i: (i, 0))
    sds = jax.ShapeDtypeStruct((t, GW), F32)
    return pl.pallas_call(
        body, name="f_ssd", grid=(nc // SSD_SUB,),
        in_specs=[pl.BlockSpec((rows, 768), lambda i: (i, 2)),
                  pl.BlockSpec((CH, 768), lambda i: (jnp.maximum(i * per - 1, 0), 2)),
                  pl.BlockSpec((rows, DTW), lambda i: (i, 0)),
                  pl.BlockSpec((rows, GW), lambda i: (i, 4)),
                  _full((4, 768)), _row(768), _row(DTW), _row(DTW), _row(DTW)],
        out_specs=[blk, blk, pl.BlockSpec((SSD_SUB, NH, HP, NS), lambda i: (i, 0, 0, 0))],
        out_shape=[sds, sds, jax.ShapeDtypeStruct((nc, NH, HP, NS), F32)],
        scratch_shapes=[pltpu.VMEM((NH, HP, NS), F32)], compiler_params=_cparams(1),
    )(proj, proj, dtp, proj, conv_w, conv_b, dt_bias, a_log, d_skip)


def _b_ssd(proj, dtp, ypre, dyc, sprev, conv_w, conv_b, dt_bias, a_log, d_skip):
    t = proj.shape[0]
    nc = t // SSD_L
    steps = nc // SSD_SUB_BWD
    rows = SSD_SUB_BWD * SSD_L
    per = rows // CH
    n_ext = SSD_L + CH

    def chunk(sub, halo, dnext, ds_in, refs):
        (x_ref, dt_ref, z_ref, yp_ref, dy_ref, sp_ref, cw_ref, cb_ref, dtb_ref, al_ref, dk_ref,
         dz_ref, dx_ref, ddt_ref, dact_ref) = refs
        rs = slice(sub * SSD_L, (sub + 1) * SSD_L)
        dact = dact_ref.at[sub]
        w = cw_ref[...]
        pre, ext = _ssd_conv(x_ref[rs, :], halo, w, cb_ref[...])
        act = _silu(pre)
        dt_raw = dt_ref[rs, :]
        dtv, a_row, cs, cs_t, causal = _ssd_common(dt_raw, dtb_ref[...], al_ref[...])
        z = z_ref[rs, :]
        dyc_v = dy_ref[rs, :]
        dz_ref[rs, :] = dyc_v * yp_ref[rs, :] * _dsilu(z)
        dy_all = dyc_v * _silu(z)
        lane = lax.broadcasted_iota(jnp.int32, (SSD_L, DTW), 1)
        rowi = lax.broadcasted_iota(jnp.int32, (1, SSD_L, 1), 1)
        lane1 = lax.broadcasted_iota(jnp.int32, (1, DTW), 1)
        xs, bm, cm, mdec, sc, dt_c, xdt, e, cs_last, wdec = _ssd_heads(act, dtv, cs, cs_t, causal)
        dy = jnp.stack([dy_all[:, HP * h:HP * (h + 1)] for h in range(NH)])
        prev = sp_ref[sub]
        ds = ds_in
        lsum = lambda v: jnp.sum(v, axis=2, keepdims=True)
        dsc = _bd(dy, xdt, 2, 2)
        q = dsc * sc
        dg = dsc * mdec
        dxdt = _bd(sc, dy, 1, 1)
        dcs = lsum(q) - lsum(jnp.swapaxes(q, 1, 2))
        dc = _bd(dg, bm, 2, 1)
        db = _bd(dg, cm, 1, 1)
        cp = _bd(cm, prev, 2, 2)
        dcs += lsum(dy * cp) * e
        ey = e * dy
        dc += _bd(ey, prev, 2, 1)
        dprev = _bd(ey, cm, 1, 1)
        elast = jnp.exp(cs_last)
        dprev += ds * elast
        dcs_last = jnp.sum(lsum(ds * prev), axis=1, keepdims=True) * elast
        bds = _bd(bm, ds, 2, 2)
        dxdt += wdec * bds
        db += wdec * _bd(xdt, ds, 2, 1)
        dw = lsum(xdt * bds) * wdec
        dcs -= dw
        dcs_last += jnp.sum(dw, axis=1, keepdims=True)
        dcs += jnp.where(rowi == SSD_L - 1, dcs_last, 0.0)
        dxs = dxdt * dt_c + dy * _head_scalars(dk_ref)
        ddtx = lsum(dxdt * xs)
        ddk = jnp.sum(lsum(dy * xs), axis=1, keepdims=True)
        dcs_mat = jnp.zeros((SSD_L, DTW), F32)
        ddtx_mat = jnp.zeros((SSD_L, DTW), F32)
        ddk_row = jnp.zeros((1, DTW), F32)
        for h in range(NH):
            dact[:, HP * h:HP * (h + 1)] = dxs[h]
            dcs_mat = jnp.where(lane == h, dcs[h], dcs_mat)
            ddtx_mat = jnp.where(lane == h, ddtx[h], ddtx_mat)
            ddk_row = jnp.where(lane1 == h, ddk[h], ddk_row)
        for g in range(2):
            dact[:, 256 + NS * g:256 + NS * (g + 1)] = db[2 * g] + db[2 * g + 1]
            dact[:, 512 + NS * g:512 + NS * (g + 1)] = dc[2 * g] + dc[2 * g + 1]
        ds_out = dprev
        r2 = lax.broadcasted_iota(jnp.int32, (SSD_L, SSD_L), 0)
        c2 = lax.broadcasted_iota(jnp.int32, (SSD_L, SSD_L), 1)
        dadt = _dot((c2 >= r2).astype(F32), dcs_mat, prec=HI)
        ddt = jnp.where(lane < NH, (dadt * a_row + ddtx_mat) * _sig(dt_raw + dtb_ref[...]), 0.0)
        ddt_ref[rs, :] = ddt
        dpre = dact[...] * _dsilu(pre)
        dcw = jnp.concatenate([_colsum(dpre * _roll(ext, 3 - k)[CH:]) for k in range(4)], axis=0)
        dext = jnp.concatenate([dpre, dnext], axis=0)
        dx_ref[rs, :] = (dext * w[3:4] + _roll(dext, n_ext - 1) * w[2:3] + _roll(dext, n_ext - 2) * w[1:2]
                         + _roll(dext, n_ext - 3) * w[0:1])[:SSD_L]
        acc = (dcw, _colsum(dpre), _colsum(ddt), _colsum(dadt * dtv) * a_row, ddk_row)
        return dpre[0:CH], ds_out, acc

    def body(x_ref, hx_ref, dt_ref, z_ref, yp_ref, dy_ref, sp_ref, cw_ref, cb_ref, dtb_ref, al_ref, dk_ref,
             dz_ref, dx_ref, ddt_ref, dcw_ref, dcb_ref, ddtb_ref, dal_ref, ddk_ref, ds_ref, dnext_ref, dact_ref):
        i = pl.program_id(0)
        acc_refs = (dcw_ref, dcb_ref, ddtb_ref, dal_ref, ddk_ref)

        @pl.when(i == 0)
        def _():
            ds_ref[...] = jnp.zeros_like(ds_ref)
            dnext_ref[...] = jnp.zeros_like(dnext_ref)
            for r in acc_refs:
                r[...] = jnp.zeros_like(r)

        refs = (x_ref, dt_ref, z_ref, yp_ref, dy_ref, sp_ref, cw_ref, cb_ref, dtb_ref, al_ref, dk_ref, dz_ref, dx_ref, ddt_ref,
                dact_ref)
        ds = ds_ref[...]
        dnext = dnext_ref[...]
        total = None
        for sub in reversed(range(SSD_SUB_BWD)):
            if sub == 0:
                halo = jnp.where(i == steps - 1, 0.0, hx_ref[...])
            else:
                halo = x_ref[sub * SSD_L - CH:sub * SSD_L, :]
            dnext, ds, acc = chunk(sub, halo, dnext, ds, refs)
            total = acc if total is None else tuple(a + b for a, b in zip(total, acc))
        ds_ref[...] = ds
        dnext_ref[...] = dnext
        for r, v in zip(acc_refs, total):
            r[...] += v

    rev = lambda i: steps - 1 - i
    blk = lambda n, cb=0: pl.BlockSpec((rows, n), lambda i: (rev(i), cb))
    row = lambda n: jax.ShapeDtypeStruct((1, n), F32)
    return pl.pallas_call(
        body, name="b_ssd", grid=(steps,),
        in_specs=[blk(768, 2), pl.BlockSpec((CH, 768), lambda i: (jnp.maximum(rev(i) * per - 1, 0), 2)),
                  blk(DTW), blk(GW, 4), blk(GW), blk(GW), pl.BlockSpec((SSD_SUB_BWD, NH, HP, NS), lambda i: (rev(i), 0, 0, 0)),
                  _full((4, 768)), _row(768), _row(DTW), _row(DTW), _row(DTW)],
        out_specs=[blk(GW), blk(768), blk(DTW), _full((4, 768)), _row(768), _row(DTW), _row(DTW), _row(DTW)],
        out_shape=[jax.ShapeDtypeStruct((t, GW), F32), jax.ShapeDtypeStruct((t, 768), F32), jax.ShapeDtypeStruct((t, DTW), F32),
                   jax.ShapeDtypeStruct((4, 768), F32), row(768), row(DTW), row(DTW), row(DTW)],
        scratch_shapes=[pltpu.VMEM((NH, HP, NS), F32), pltpu.VMEM((CH, 768), F32), pltpu.VMEM((SSD_SUB_BWD, SSD_L, 768), F32)],
        compiler_params=_cparams(1),
    )(proj, proj, dtp, proj, ypre, dyc, sprev, conv_w, conv_b, dt_bias, a_log, d_skip)


def _s5_block(t):
    return min(t, 256)


def _seg_t():
    r = lax.broadcasted_iota(jnp.int32, (64, 1024), 0)
    c = lax.broadcasted_iota(jnp.int32, (64, 1024), 1)
    return (c // 16 == r).astype(F32)


def _s5_prep_math(a_re, a_im, lstep, b_re, b_im):
    step = jnp.exp(lstep)
    ars = a_re * step
    ais = a_im * step
    mag = jnp.exp(ars)
    lr = mag * jnp.cos(ais)
    li = mag * jnp.sin(ais)
    den = a_re * a_re + a_im * a_im
    nr = lr - 1.0
    f_re = (nr * a_re + li * a_im) / den
    f_im = (li * a_re - nr * a_im) / den
    seg = _seg_t()
    fr = _dot(f_re, seg, prec=HI)
    fi = _dot(f_im, seg, prec=HI)
    return lr, li, fr * b_re - fi * b_im, fr * b_im + fi * b_re, ars, ais


def _s5_prep(a_re, a_im, lstep, b_re, b_im):
    def body(ar, ai, ls, br, bi, lr_o, li_o, bbr_o, bbi_o, ars_o, ais_o):
        outs = _s5_prep_math(ar[...], ai[...], ls[...], br[...], bi[...])
        for o, v in zip((lr_o, li_o, bbr_o, bbi_o, ars_o, ais_o), outs):
            o[...] = v

    s64 = jax.ShapeDtypeStruct((16, 64), F32)
    s1k = jax.ShapeDtypeStruct((16, 1024), F32)
    return pl.pallas_call(body, name="s5_prep", out_shape=[s64, s64, s1k, s1k, s64, s64])(a_re, a_im, lstep, b_re, b_im)


def _s5_prep_bwd(a_re, a_im, lstep, b_re, b_im, dlr, dli, dbbr, dbbi):
    def body(ar, ai, ls, br, bi, g0, g1, g2, g3, o0, o1, o2, o3, o4):
        f = lambda *a: _s5_prep_math(*a)[:4]
        _, vjp = jax.vjp(f, ar[...], ai[...], ls[...], br[...], bi[...])
        for o, v in zip((o0, o1, o2, o3, o4), vjp((g0[...], g1[...], g2[...], g3[...]))):
            o[...] = v

    s64 = jax.ShapeDtypeStruct((16, 64), F32)
    s1k = jax.ShapeDtypeStruct((16, 1024), F32)
    return pl.pallas_call(body, name="s5_prep_bwd", out_shape=[s64, s64, jax.ShapeDtypeStruct((16, 1), F32), s1k, s1k])(
        a_re, a_im, lstep, b_re, b_im, dlr, dli, dbbr, dbbi)


SUB = 8


def _s5_tables(ars, ais):
    def body(ar, ai, tr, ti):
        rr = lax.broadcasted_iota(jnp.int32, (8 * SUB, S5_P), 0)
        seg, r = rr // SUB, rr % SUB
        step = jnp.where((seg == 1) | (seg == 4), 1, jnp.where((seg == 2) | (seg == 5), 2, 4))
        n = jnp.where(seg == 0, r + 1, jnp.where(seg == 7, SUB - r, step))
        fwd_gap = jnp.where(seg <= 3, r - step, SUB - step - 1 - r)
        gap = jnp.where((seg == 0) | (seg == 7), 0, fwd_gap)
        nf = n.astype(F32)
        mag = jnp.where(gap >= 0, jnp.exp(nf * ar[...]), 0.0)
        tr[...] = mag * jnp.cos(nf * ai[...])
        ti[...] = mag * jnp.sin(nf * ai[...])

    sds = jax.ShapeDtypeStruct((8 * SUB, S5_P), F32)
    return pl.pallas_call(body, name="s5_tables", out_shape=[sds] * 2)(ars, ais)


def _s5_table(tb_r, tb_i, k):
    return tb_r[SUB * k:SUB * (k + 1), :], tb_i[SUB * k:SUB * (k + 1), :]


def _s5_scan(bu_r, bu_i, tb_r, tb_i, c_r, c_i, lb):
    nt = lb // SUB
    sr, si = bu_r.reshape(nt, SUB, S5_P), bu_i.reshape(nt, SUB, S5_P)
    for j, k in enumerate((1, 2, 4)):
        mr, mi = _s5_table(tb_r, tb_i, 1 + j)
        tr, ti = pltpu.roll(sr, k, axis=1), pltpu.roll(si, k, axis=1)
        sr, si = sr + mr * tr - mi * ti, si + mr * ti + mi * tr
    pr, pi = _s5_table(tb_r, tb_i, 0)
    out_r, out_i = [], []
    for j in range(nt):
        a_r = sr[j] + pr * c_r - pi * c_i
        a_i = si[j] + pr * c_i + pi * c_r
        out_r.append(a_r)
        out_i.append(a_i)
        c_r, c_i = a_r[SUB - 1:SUB], a_i[SUB - 1:SUB]
    return jnp.concatenate(out_r, axis=0), jnp.concatenate(out_i, axis=0)


def _s5_rscan(g_r, g_i, tb_r, tb_i, n_r, n_i, lb):
    nt = lb // SUB
    gr, gi = g_r.reshape(nt, SUB, S5_P), g_i.reshape(nt, SUB, S5_P)
    for j, k in enumerate((1, 2, 4)):
        mr, mi = _s5_table(tb_r, tb_i, 4 + j)
        tr, ti = pltpu.roll(gr, SUB - k, axis=1), pltpu.roll(gi, SUB - k, axis=1)
        gr, gi = gr + mr * tr + mi * ti, gi + mr * ti - mi * tr
    qr, qi = _s5_table(tb_r, tb_i, 7)
    out_r, out_i = [None] * nt, [None] * nt
    for j in reversed(range(nt)):
        a_r = gr[j] + qr * n_r + qi * n_i
        a_i = gi[j] + qr * n_i - qi * n_r
        out_r[j], out_i[j] = a_r, a_i
        n_r, n_i = a_r[0:1], a_i[0:1]
    return jnp.concatenate(out_r, axis=0), jnp.concatenate(out_i, axis=0)


def _s5_y(u, sr, si, cre, cim, dsk):
    return _bdot(sr, cre) + _bdot(si, cim) + dsk * u


def _f_s5(proj, bmat, cre, cim, p_r, p_i, dsk, glu_w, glu_b):
    t = proj.shape[0]
    lb = _s5_block(t)
    nb = t // lb

    def body(u_ref, bm_ref, cr_ref, ci_ref, pr_ref, pi_ref, dk_ref, gw_ref, gb_ref, y_ref, car_ref, s_ref, st_ref):
        @pl.when(pl.program_id(0) == 0)
        def _():
            st_ref[...] = jnp.zeros_like(st_ref)

        u = u_ref[...]
        bu = _bdot(u, bm_ref[...])
        c_r, c_i = st_ref[0:1, 0:S5_P], st_ref[0:1, S5_P:]
        car_ref[0] = st_ref[0:1, :]
        sr, si = _s5_scan(bu[:, :S5_P], bu[:, S5_P:], pr_ref, pi_ref, c_r, c_i, lb)
        st_ref[0:1, 0:S5_P] = sr[lb - 1:lb]
        st_ref[0:1, S5_P:] = si[lb - 1:lb]
        sr_b, si_b = sr.astype(BF16), si.astype(BF16)
        s_ref[:, 0:S5_P] = sr_b
        s_ref[:, S5_P:] = si_b
        gel = _gelu(_s5_y(u, sr_b, si_b, cr_ref[...], ci_ref[...], dk_ref[...]))
        y_ref[...] = gel * _sig(_bdot(gel, gw_ref[...]) + gb_ref[...])

    return pl.pallas_call(
        body, name="f_s5", grid=(nb,),
        in_specs=[pl.BlockSpec((lb, GW), lambda i: (i, 5)),
                  _full((GW, 2 * S5_P)), _full((S5_P, GW)), _full((S5_P, GW)), _full((8 * SUB, S5_P)), _full((8 * SUB, S5_P)),
                  _row(GW), _full((GW, GW)), _row(GW)],
        out_specs=[pl.BlockSpec((lb, GW), lambda i: (i, 0)), pl.BlockSpec((1, 1, 2 * S5_P), lambda i: (i, 0, 0)),
                   pl.BlockSpec((lb, 2 * S5_P), lambda i: (i, 0))],
        out_shape=[jax.ShapeDtypeStruct((t, GW), F32), jax.ShapeDtypeStruct((nb, 1, 2 * S5_P), F32),
                   jax.ShapeDtypeStruct((t, 2 * S5_P), BF16)],
        scratch_shapes=[pltpu.VMEM((8, 2 * S5_P), F32)], compiler_params=_cparams(1),
    )(proj, bmat, cre, cim, p_r, p_i, dsk, glu_w, glu_b)


def _b_s5(proj, dyd, carries, states, bmat, cre, cim, p_r, p_i, dsk, glu_w, glu_b):
    t = proj.shape[0]
    lb = _s5_block(t)
    nb = t // lb

    def body(u_ref, dy_ref, car_ref, s_ref, bm_ref, cr_ref, ci_ref, pr_ref, pi_ref, dk_ref, gw_ref, gb_ref,
             du_ref, dbm_ref, dcr_ref, dci_ref, dlam_ref, ddk_ref, dgw_ref, dgb_ref, gc_ref):
        @pl.when(pl.program_id(0) == 0)
        def _():
            gc_ref[...] = jnp.zeros_like(gc_ref)
            for r in (dbm_ref, dcr_ref, dci_ref, dlam_ref, ddk_ref, dgw_ref, dgb_ref):
                r[...] = jnp.zeros_like(r)

        u = u_ref[...]
        bm = bm_ref[...]
        u_b = u.astype(BF16)
        c_r, c_i = car_ref[0, 0:1, 0:S5_P], car_ref[0, 0:1, S5_P:]
        cre_v, cim_v, dk, gw = cr_ref[...], ci_ref[...], dk_ref[...], gw_ref[...]
        sr_b, si_b = s_ref[:, 0:S5_P], s_ref[:, S5_P:]
        sr, si = sr_b.astype(F32), si_b.astype(F32)
        y = _dot(sr_b, cre_v) + _dot(si_b, cim_v) + dk * u
        gel = _gelu(y)
        gel_b = gel.astype(BF16)
        gate = _sig(_dot(gel_b, gw) + gb_ref[...])
        dout = dy_ref[...]
        t1 = dout * gel * gate * (1.0 - gate)
        t1_b = t1.astype(BF16)
        dgw_ref[...] += _dot(gel_b, t1_b, TN)
        dgb_ref[...] += _colsum(t1)
        dyv = (dout * gate + _dot(t1_b, gw, NT)) * _dgelu(y)
        dyv_b = dyv.astype(BF16)
        ddk_ref[...] += _colsum(dyv * u)
        dcr_ref[...] += _dot(sr_b, dyv_b, TN)
        dci_ref[...] += _dot(si_b, dyv_b, TN)
        gr = _dot(dyv_b, cre_v, NT)
        gi = _dot(dyv_b, cim_v, NT)
        row = lax.broadcasted_iota(jnp.int32, (lb, S5_P), 0)
        n_r, n_i = gc_ref[0:1, 0:S5_P], gc_ref[0:1, S5_P:]
        gr, gi = _s5_rscan(gr, gi, pr_ref, pi_ref, n_r, n_i, lb)
        gc_ref[0:1, 0:S5_P] = gr[0:1]
        gc_ref[0:1, S5_P:] = gi[0:1]
        gcat = jnp.concatenate([gr, gi], axis=1).astype(BF16)
        dbm_ref[...] += _dot(u_b, gcat, TN)
        du_ref[...] = dyv * dk + _dot(gcat, bm, NT)
        spr = jnp.where(row >= 1, _roll(sr, 1), c_r)
        spi = jnp.where(row >= 1, _roll(si, 1), c_i)
        dlam_ref[0:1, :] += _colsum(gr * spr + gi * spi)
        dlam_ref[1:2, :] += _colsum(gi * spr - gr * spi)

    rev = lambda i: nb - 1 - i
    return pl.pallas_call(
        body, name="b_s5", grid=(nb,),
        in_specs=[pl.BlockSpec((lb, GW), lambda i: (rev(i), 5)), pl.BlockSpec((lb, GW), lambda i: (rev(i), 0)),
                  pl.BlockSpec((1, 1, 2 * S5_P), lambda i: (rev(i), 0, 0)), pl.BlockSpec((lb, 2 * S5_P), lambda i: (rev(i), 0)),
                  _full((GW, 2 * S5_P)), _full((S5_P, GW)), _full((S5_P, GW)), _full((8 * SUB, S5_P)), _full((8 * SUB, S5_P)),
                  _row(GW), _full((GW, GW)), _row(GW)],
        out_specs=[pl.BlockSpec((lb, GW), lambda i: (rev(i), 0)), _full((GW, 2 * S5_P)), _full((S5_P, GW)), _full((S5_P, GW)),
                   _full((2, S5_P)), _row(GW), _full((GW, GW)), _row(GW)],
        out_shape=[jax.ShapeDtypeStruct((t, GW), F32), jax.ShapeDtypeStruct((GW, 2 * S5_P), F32),
                   jax.ShapeDtypeStruct((S5_P, GW), F32), jax.ShapeDtypeStruct((S5_P, GW), F32),
                   jax.ShapeDtypeStruct((2, S5_P), F32), jax.ShapeDtypeStruct((1, GW), F32),
                   jax.ShapeDtypeStruct((GW, GW), F32), jax.ShapeDtypeStruct((1, GW), F32)],
        scratch_shapes=[pltpu.VMEM((8, 2 * S5_P), F32)], compiler_params=_cparams(1),
    )(proj, dyd, carries, states, bmat, cre, cim, p_r, p_i, dsk, glu_w, glu_b)


def _group_norm(ys, bw):
    outs, stats = [], []
    for g, y in enumerate(ys):
        r, n = _rms(y)
        stats.append((r, n))
        outs.append(n * bw[:, GW * g:GW * (g + 1)])
    return jnp.concatenate(outs, axis=1), stats


def _f_out(ya, yb, yc, yd, bw, w_out, h, g1):
    t = h.shape[0]
    tb = _tblock(t)

    def body(a_ref, b_ref, c_ref, d_ref, bw_ref, w_ref, h_ref, g_ref, h2_ref, o_ref, cat_ref):
        cat, _ = _group_norm([a_ref[...], b_ref[...], c_ref[...], d_ref[...]], bw_ref[...])
        catb = cat.astype(BF16)
        cat_ref[...] = catb
        o = _dot(catb, w_ref[...])
        o_ref[...] = o
        h2_ref[...] = h_ref[...] + g_ref[...] * o

    yblk = pl.BlockSpec((tb, GW), lambda i: (i, 0))
    blk = pl.BlockSpec((tb, D), lambda i: (i, 0))
    return pl.pallas_call(
        body, name="f_out", grid=(t // tb,), in_specs=[yblk] * 4 + [_row(D), _full((D, D)), blk, _row(D)],
        out_specs=[blk, blk, blk],
        out_shape=[jax.ShapeDtypeStruct((t, D), F32), jax.ShapeDtypeStruct((t, D), F32), jax.ShapeDtypeStruct((t, D), BF16)],
        compiler_params=_cparams(1),
    )(ya, yb, yc, yd, bw, w_out, h, g1)


def _b_out(dh2, ya, yb, yc, yd, bw, w_out, g1):
    t = dh2.shape[0]
    tb = _tblock(t)

    def body(dh_ref, a_ref, b_ref, c_ref, d_ref, bw_ref, w_ref, g_ref, da_ref, db_ref, dc_ref, dd_ref, do_ref, dbw_ref):
        @pl.when(pl.program_id(0) == 0)
        def _():
            dbw_ref[...] = jnp.zeros_like(dbw_ref)

        do = (dh_ref[...] * g_ref[...]).astype(BF16)
        do_ref[...] = do
        dcat = _dot(do, w_ref[...], NT)
        bw_v = bw_ref[...]
        for g, (y_ref, dy_ref) in enumerate(((a_ref, da_ref), (b_ref, db_ref), (c_ref, dc_ref), (d_ref, dd_ref))):
            r, n = _rms(y_ref[...])
            dc = dcat[:, GW * g:GW * (g + 1)]
            dbw_ref[:, GW * g:GW * (g + 1)] += _colsum(dc * n)
            dy_ref[...] = _rms_bwd(r, n, dc * bw_v[:, GW * g:GW * (g + 1)])

    yblk = pl.BlockSpec((tb, GW), lambda i: (i, 0))
    blk = pl.BlockSpec((tb, D), lambda i: (i, 0))
    ysd = jax.ShapeDtypeStruct((t, GW), F32)
    return pl.pallas_call(
        body, name="b_out", grid=(t // tb,), in_specs=[blk] + [yblk] * 4 + [_row(D), _full((D, D)), _row(D)],
        out_specs=[yblk] * 4 + [blk, _row(D)],
        out_shape=[ysd] * 4 + [jax.ShapeDtypeStruct((t, D), BF16), jax.ShapeDtypeStruct((1, D), F32)],
        compiler_params=_cparams(1),
    )(dh2, ya, yb, yc, yd, bw, w_out, g1)


HB = 1024


def _f_mlp(h2, nw, sc, sh, g2, w1, w2):
    t = h2.shape[0]
    tb = _tblock(t)
    nk = HID // HB

    def body(h_ref, nw_ref, sc_ref, sh_ref, g_ref, w1_ref, w2_ref, h3_ref, m_ref, a_ref, v_ref):
        k = pl.program_id(1)

        @pl.when(k == 0)
        def _():
            _, n = _rms(h_ref[...])
            v_ref[...] = ((n * nw_ref[...]) * (1.0 + sc_ref[...]) + sh_ref[...]).astype(BF16)
            m_ref[...] = jnp.zeros_like(m_ref)

        a = _dot(v_ref[...], w1_ref[...])
        a_ref[...] = a
        ra = jnp.maximum(a, 0.0)
        m_ref[...] += _dot((ra * ra).astype(BF16), w2_ref[...])

        @pl.when(k == nk - 1)
        def _():
            h3_ref[...] = h_ref[...] + g_ref[...] * m_ref[...]

    blk = pl.BlockSpec((tb, D), lambda i, k: (i, 0))
    return pl.pallas_call(
        body, name="f_mlp", grid=(t // tb, nk),
        in_specs=[blk, _row(D), _row(D), _row(D), _row(D), pl.BlockSpec((D, HB), lambda i, k: (0, k)),
                  pl.BlockSpec((HB, D), lambda i, k: (k, 0))],
        out_specs=[blk, blk, pl.BlockSpec((tb, HB), lambda i, k: (i, k)), blk],
        out_shape=[jax.ShapeDtypeStruct((t, D), F32), jax.ShapeDtypeStruct((t, D), F32), jax.ShapeDtypeStruct((t, HID), F32),
                   jax.ShapeDtypeStruct((t, D), BF16)],
        compiler_params=_cparams(2),
    )(h2, nw, sc, sh, g2, w1, w2)


def _b_mlp(dh3, a, g2, w1, w2):
    t = dh3.shape[0]
    tb = _tblock(t)
    nk = HID // HB

    def body(dh_ref, a_ref, g_ref, w1_ref, w2_ref, dv_ref, da_ref, act_ref, dm_ref):
        k = pl.program_id(1)
        dm = (dh_ref[...] * g_ref[...]).astype(BF16)

        @pl.when(k == 0)
        def _():
            dm_ref[...] = dm
            dv_ref[...] = jnp.zeros_like(dv_ref)

        ra = jnp.maximum(a_ref[...], 0.0)
        act_ref[...] = (ra * ra).astype(BF16)
        da = (_dot(dm, w2_ref[...], NT) * (2.0 * ra)).astype(BF16)
        da_ref[...] = da
        dv_ref[...] += _dot(da, w1_ref[...], NT)

    blk = pl.BlockSpec((tb, D), lambda i, k: (i, 0))
    hblk = pl.BlockSpec((tb, HB), lambda i, k: (i, k))
    return pl.pallas_call(
        body, name="b_mlp", grid=(t // tb, nk),
        in_specs=[blk, hblk, _row(D), pl.BlockSpec((D, HB), lambda i, k: (0, k)), pl.BlockSpec((HB, D), lambda i, k: (k, 0))],
        out_specs=[blk, hblk, hblk, blk],
        out_shape=[jax.ShapeDtypeStruct((t, D), F32), jax.ShapeDtypeStruct((t, HID), BF16), jax.ShapeDtypeStruct((t, HID), BF16),
                   jax.ShapeDtypeStruct((t, D), BF16)],
        compiler_params=_cparams(2),
    )(dh3, a, g2, w1, w2)


def _b_final(h, tgt, fw):
    t = h.shape[0]
    tb = _tblock(t)

    def body(h_ref, t_ref, w_ref, dh_ref, loss_ref, dfw_ref):
        @pl.when(pl.program_id(0) == 0)
        def _():
            loss_ref[...] = jnp.zeros_like(loss_ref)
            dfw_ref[...] = jnp.zeros_like(dfw_ref)

        r, n = _rms(h_ref[...])
        wv = w_ref[...]
        err = n * wv - t_ref[...]
        loss_ref[...] += jnp.sum(err * err, keepdims=True) * (0.5 / D)
        dy = err * (1.0 / D)
        dfw_ref[...] += _colsum(dy * n)
        dh_ref[...] = _rms_bwd(r, n, dy * wv)

    blk = pl.BlockSpec((tb, D), lambda i: (i, 0))
    return pl.pallas_call(
        body, name="b_final", grid=(t // tb,), in_specs=[blk, blk, _row(D)], out_specs=[blk, _row(1), _row(D)],
        out_shape=[jax.ShapeDtypeStruct((t, D), F32), jax.ShapeDtypeStruct((1, 1), F32), jax.ShapeDtypeStruct((1, D), F32)],
        compiler_params=_cparams(1),
    )(h, tgt, fw)


_EYE16 = None


def _eye(n):
    return jnp.eye(n, dtype=F32)


def _pool_embed(pool_w):
    return jnp.einsum('gcd,gk->gckd', pool_w, _eye(4)).reshape(GW, GW)


def _pool_extract(m):
    return jnp.einsum('gcgd->gcd', m.reshape(4, 64, 4, 64))


def _bmat_embed(bb):
    return jnp.einsum('gph,gk->ghkp', bb, _eye(16)).reshape(GW, S5_P)


def _bmat_extract(m):
    return jnp.einsum('ghgp->gph', m.reshape(16, 16, 16, 64))


def _cmat_embed(cc):
    return jnp.einsum('ghp,gk->kpgh', cc, _eye(16)).reshape(S5_P, GW)


def _cmat_extract(m):
    return jnp.einsum('gpgh->ghp', m.reshape(16, 64, 16, 16))


def _pad_lanes(v, n=DTW):
    return jnp.pad(v.reshape(1, -1), ((0, 0), (0, n - v.shape[-1])))


def _w_in_layout(w_in_t):
    w_main = jnp.concatenate([w_in_t[:1280], w_in_t[2052:2308], w_in_t[1280:2048]], axis=0)
    return w_main, jnp.pad(w_in_t[2048:2052], ((0, DTW - 4), (0, 0)))


def _layer_params(p, l, mod, w_in, rest):
    q = {'rest': rest}
    q['mod'] = [mod[k:k + 1] for k in range(6)]
    q['nw1'] = p['norm_mix_w'][l:l + 1]
    q['nw2'] = p['norm_mlp_w'][l:l + 1]
    q['w_main'], q['w_dt'] = _w_in_layout(w_in)
    q['pool_mat'] = _pool_embed(p['pool_w'][l]).astype(BF16)
    q['pool_scale'] = p['pool_scale'][l:l + 1]
    q['sconv_w'] = p['sconv_w'][l]
    q['conv_w'] = p['ssd_conv_w'][l]
    q['conv_b'] = p['ssd_conv_b'][l:l + 1]
    q['dt_bias'] = _pad_lanes(p['ssd_dt_bias'][l])
    q['a_log'] = _pad_lanes(p['ssd_a_log'][l])
    q['ssd_d'] = _pad_lanes(p['ssd_d'][l])
    q['s5_raw'] = (p['s5_a_re'][l], p['s5_a_im'][l], p['s5_log_step'][l].reshape(16, 1),
                   p['s5_b_re'][l].reshape(16, 1024), p['s5_b_im'][l].reshape(16, 1024))
    q['cre'] = _cmat_embed(p['s5_c_re'][l]).astype(BF16)
    q['cim'] = (-_cmat_embed(p['s5_c_im'][l])).astype(BF16)
    q['s5_d'] = p['s5_d'][l:l + 1]
    q['glu_w'] = p['s5_glu_w'][l].astype(BF16)
    q['glu_b'] = p['s5_glu_b'][l:l + 1]
    q['bw'] = p['branch_norm_w'][l:l + 1]
    return q


def _layer_fwd(h, q):
    sh1, sc1, g1, sh2, sc2, g2 = q['mod']
    t = h.shape[0]
    s = {'h': h}
    s['proj'], s['dtp'], s['u'] = _f_in(h, q['nw1'], sc1, sh1, q['w_main'], q['w_dt'])
    s['ya'], s['yb'] = _f_ab(s['proj'], q['pool_mat'], q['pool_scale'], q['sconv_w'])
    s['yc'], s['ypre'], s['sprev'] = _f_ssd(s['proj'], s['dtp'], q['conv_w'], q['conv_b'], q['dt_bias'], q['a_log'], q['ssd_d'])
    lr, li, bbr, bbi, ars, ais = _s5_prep(*q['s5_raw'])
    s['bmat'] = jnp.concatenate([_bmat_embed(bbr.reshape(16, 64, 16)), _bmat_embed(bbi.reshape(16, 64, 16))],
                                axis=1).astype(BF16)
    s['tables'] = _s5_tables(ars.reshape(1, S5_P), ais.reshape(1, S5_P))
    s['yd'], s['carries'], s['states'] = _f_s5(s['proj'], s['bmat'], q['cre'], q['cim'], s['tables'][0], s['tables'][1],
                                  q['s5_d'], q['glu_w'], q['glu_b'])
    q['w_out'], q['w1'], q['w2'] = q['rest']((s['ya'], s['yc'], s['yd']))
    s['h2'], s['o'], s['cat'] = _f_out(s['ya'], s['yb'], s['yc'], s['yd'], q['bw'], q['w_out'], h, g1)
    h3, s['m'], s['a'], s['v'] = _f_mlp(s['h2'], q['nw2'], sc2, sh2, g2, q['w1'], q['w2'])
    return h3, s


def _layer_bwd(dh3, q, s):
    sh1, sc1, g1, sh2, sc2, g2 = q['mod']
    g = {}
    dv, da, act, dm = _b_mlp(dh3, s['a'], g2, q['w1'], q['w2'])
    g['mlp_w1'] = _tn_matmul(s['v'], da, "dw1", col_major=True)
    g['mlp_w2'] = _tn_matmul(act, dm, "dw2")
    dh2, dsc2, dsh2, dnw2, dg2 = _b_normmod(dv, s['h2'], dh3, s['m'], q['nw2'], sc2, "b_norm_mlp")
    dya, dyb, dyc, dyd, do, dbw = _b_out(dh2, s['ya'], s['yb'], s['yc'], s['yd'], q['bw'], q['w_out'], g1)
    g['w_out'] = _tn_matmul(s['cat'], do, "dwout")
    g['branch_norm_w'] = dbw[0]
    dab, dpm, dps, dsw = _b_ab(s['proj'], dya, dyb, q['pool_mat'], q['pool_scale'], q['sconv_w'])
    g['pool_w'] = _pool_extract(dpm)
    g['pool_scale'] = dps[0]
    g['sconv_w'] = dsw
    dz, dxbc, ddt, dcw, dcb, ddtb, dal, ddk = _b_ssd(s['proj'], s['dtp'], s['ypre'], dyc, s['sprev'], q['conv_w'],
                                                     q['conv_b'], q['dt_bias'], q['a_log'], q['ssd_d'])
    g['ssd_conv_w'] = dcw
    g['ssd_conv_b'] = dcb[0]
    g['ssd_dt_bias'] = ddtb[0, :4]
    g['ssd_a_log'] = dal[0, :4]
    g['ssd_d'] = ddk[0, :4]
    tb = s['tables']
    ds5, dbmat, dcre, dcim, dlam, dd5, dgw, dgb = _b_s5(s['proj'], dyd, s['carries'], s['states'], s['bmat'], q['cre'], q['cim'],
                                                        tb[0], tb[1], q['s5_d'], q['glu_w'], q['glu_b'])
    g['s5_c_re'] = _cmat_extract(dcre)
    g['s5_c_im'] = -_cmat_extract(dcim)
    g['s5_d'] = dd5[0]
    g['s5_glu_w'] = dgw
    g['s5_glu_b'] = dgb[0]
    dbbr = _bmat_extract(dbmat[:, :S5_P]).reshape(16, 1024)
    dbbi = _bmat_extract(dbmat[:, S5_P:]).reshape(16, 1024)
    dar, dai, dls, dbr, dbi = _s5_prep_bwd(*q['s5_raw'], dlam[0].reshape(16, 64), dlam[1].reshape(16, 64), dbbr, dbbi)
    g['s5_a_re'], g['s5_a_im'], g['s5_log_step'] = dar, dai, dls[:, 0]
    g['s5_b_re'], g['s5_b_im'] = dbr, dbi
    du = _b_in_du(dab, dz, dxbc, ds5, ddt, q['w_main'], q['w_dt'])
    u = s['u']
    head = jnp.concatenate([_tn_matmul(dab, u, "dwin_ab"), _tn_matmul(dz, u, "dwin_z"), _tn_matmul(dxbc, u, "dwin_xbc"),
                            _tn_matmul(ddt, u, "dwin_dt")[:8]], axis=0)
    full = lax.dynamic_update_slice(jnp.zeros((2308, D), F32), head, (0, 0))
    g['w_in'] = lax.dynamic_update_slice(full, _tn_matmul(ds5, u, "dwin_s5"), (2052, 0))
    dh, dsc1, dsh1, dnw1, dg1 = _b_normmod(du, s['h'], dh2, s['o'], q['nw1'], sc1, "b_norm_mix")
    g['norm_mix_w'] = dnw1[0]
    g['norm_mlp_w'] = dnw2[0]
    dmod = jnp.concatenate([dsh1, dsc1, dg1, dsh2, dsc2, dg2], axis=1)
    return dh, g, dmod


def _local_step(x, tgt, p, mod, w_in_of, rest_of):
    h = x
    qs, saved = [], []
    for l in range(2):
        qs.append(_layer_params(p, l, mod[l], w_in_of(l), functools.partial(rest_of, l)))
        h, s = _layer_fwd(h, qs[l])
        saved.append(s)
    dh, loss, dfw = _b_final(h, tgt, p['final_norm_w'].reshape(1, D))
    grads = [None, None]
    dmods = [None, None]
    for l in (1, 0):
        dh, grads[l], dmods[l] = _layer_bwd(dh, qs[l], saved[l])
    out = {k: jnp.stack([grads[0][k], grads[1][k]]) for k in grads[0]}
    out['final_norm_w'] = dfw[0]
    return loss, dh, out, jnp.concatenate(dmods, axis=0)


def _pack(arrs):
    parts, rows = [], 0
    for a in arrs:
        f = a.reshape(-1).astype(F32)
        pad = (-f.shape[0]) % 1024
        f = jnp.pad(f, (0, pad)) if pad else f
        parts.append(f.reshape(-1, 128))
        rows += parts[-1].shape[0]
    if rows % 256:
        parts.append(jnp.zeros((256 - rows % 256, 128), F32))
    return jnp.concatenate(parts, axis=0)


def _unpack(buf, shapes):
    out, row = [], 0
    for shp in shapes:
        n = int(math.prod(shp)) if len(shp) else 1
        rows = (n + 1023) // 1024 * 8
        out.append(buf[row:row + rows].reshape(-1)[:n].reshape(shp))
        row += rows
    return out


def _shard_of(a, axis, k):
    n = a.shape[axis] // 4
    return lax.dynamic_slice_in_dim(a, k * n, n, axis)


def kernel(x, c, norm_mix_w, norm_mlp_w, ada_w, ada_b, w_in, pool_w, pool_scale, sconv_w, ssd_conv_w, ssd_conv_b, ssd_dt_bias, ssd_a_log, ssd_d, s5_a_re, s5_a_im, s5_log_step, s5_b_re, s5_b_im, s5_c_re, s5_c_im, s5_d, s5_glu_w, s5_glu_b, branch_norm_w, w_out, mlp_w1, mlp_w2, final_norm_w, loss_target, m_norm_mix_w, m_norm_mlp_w, m_ada_w, m_ada_b, m_w_in, m_pool_w, m_pool_scale, m_sconv_w, m_ssd_conv_w, m_ssd_conv_b, m_ssd_dt_bias, m_ssd_a_log, m_ssd_d, m_s5_a_re, m_s5_a_im, m_s5_log_step, m_s5_b_re, m_s5_b_im, m_s5_c_re, m_s5_c_im, m_s5_d, m_s5_glu_w, m_s5_glu_b, m_branch_norm_w, m_w_out, m_mlp_w1, m_mlp_w2, m_final_norm_w, v_norm_mix_w, v_norm_mlp_w, v_ada_w, v_ada_b, v_w_in, v_pool_w, v_pool_scale, v_sconv_w, v_ssd_conv_w, v_ssd_conv_b, v_ssd_dt_bias, v_ssd_a_log, v_ssd_d, v_s5_a_re, v_s5_a_im, v_s5_log_step, v_s5_b_re, v_s5_b_im, v_s5_c_re, v_s5_c_im, v_s5_d, v_s5_glu_w, v_s5_glu_b, v_branch_norm_w, v_w_out, v_mlp_w1, v_mlp_w2, v_final_norm_w):
    loc = locals()
    w = {n: loc[n] for n in WEIGHTS}
    mom = {n: loc['m_' + n] for n in WEIGHTS}
    var = {n: loc['v_' + n] for n in WEIGHTS}
    ix, iy, ic = lax.axis_index("x"), lax.axis_index("y"), lax.axis_index("c")
    chip = 2 * ix + iy
    dev = 4 * ix + 2 * iy + ic

    mine_of = lambda a: lax.dynamic_index_in_dim(a.astype(BF16), ic, axis=0, keepdims=False)
    pad_in = lambda a: jnp.pad(a.T, ((0, WIN_ROWS - 577), (0, 0)))
    shard = jnp.concatenate([pad_in(mine_of(w['w_in'])), mine_of(w['w_out']), mine_of(w['mlp_w1']), mine_of(w['mlp_w2'])], axis=0)

    (c_all,) = _exchange([c], EVERYONE, False, "ag_cond", stage=True)
    c_all = c_all.reshape(8, D)
    small_sh = _exchange([w[n] for n in SMALL_SHARDED], CHIPS, False, "ag_small")
    (w_in0,) = _exchange([pad_in(w['w_in'][0].astype(BF16))], CHIPS, False, "ag_win0")
    p = {n: w[n] for n in WEIGHTS if n not in BIG}
    for n, g in zip(SMALL_SHARDED, small_sh):
        ax = SMALL_SHARDED[n]
        p[n] = jnp.concatenate([g[k] for k in range(4)], axis=ax)

    def w_in_full(sh):
        return sh[:, :577].reshape(4 * 577, D)

    big = {}

    def fetch(after):
        if not big:
            got = _gather_wait(sems, shard_thru, land, after, "ag_big_wait")
            got = lax.dynamic_update_slice(got, shard[None], (chip, 0, 0))
            other = _pair_swap([got.reshape(-1, D)], False, "swap_big")[0].reshape(got.shape)
            big['both'] = [jnp.where(ic == l, got, other) for l in range(2)]
        return big['both']

    def w_in_of(l):
        return w_in_full(w_in0) if l == 0 else w_in_full(fetch(None)[1])

    def rest_of(l, after):
        blk = fetch(after)[l]
        r0 = WIN_ROWS
        w_out_l = blk[:, r0:r0 + 256].reshape(D, D)
        w1_l = jnp.concatenate([blk[k, r0 + 256:r0 + 1280] for k in range(4)], axis=1)
        w2_l = blk[:, r0 + 1280:r0 + 2304].reshape(HID, D)
        return w_out_l, w1_l, w2_l

    ada_b_sh = _shard_of(w['ada_b'], 1, chip).reshape(2, 1, 6 * D // 4)
    mod_sh = _ada_fwd(c_all, w['ada_w'], ada_b_sh)
    (mod_all,) = _exchange([mod_sh], CHIPS, False, "ag_mod", stage=True)
    mine = lax.dynamic_index_in_dim(mod_all, dev, axis=2, keepdims=False)
    sems, shard_thru, land, token = _gather_start(shard, [mod_all, w_in0] + small_sh, "ag_big_start")
    mod = jnp.transpose(mine, (1, 0, 2)).reshape(2, 6, D) + token[0, 0]

    loss, grad_x, g, dmod = _local_step(x[0], loss_target[0], p, mod, w_in_of, rest_of)

    (dmod_all,) = _exchange([dmod], EVERYONE, False, "ag_dmod", stage=True)
    dmod_all = jnp.transpose(dmod_all, (1, 0, 2))
    g_ada_w, g_ada_b = _ada_bwd(c_all, _shard_of(dmod_all, 2, chip), dmod_all)

    gw_in = jnp.pad(g['w_in'].reshape(2, 4, 577, D), ((0, 0), (0, 0), (0, WIN_ROWS - 577), (0, 0)))
    gw_out = g['w_out'].reshape(2, 4, 256, D)
    gw1 = g['mlp_w1']
    gw2 = g['mlp_w2'].reshape(2, 4, 1024, D)
    gws = [gw_in, gw_out, gw1, gw2]
    got = _pair_swap([a.reshape(2, -1, D) for a in gws], True, "swap_grad")
    layer = ic.astype(jnp.int32).reshape(1)
    pair = [_pair_sum(a, b.reshape(a.shape[1:]), layer, "pair_sum%d" % k, BF16) for k, (a, b) in enumerate(zip(gws, got))]
    quad = _exchange(pair, CHIPS, True, "rs_chips")
    quad = [_sum_lead(a, "rs_chip_sum%d" % k, F32) for k, a in enumerate(quad)]
    other = _pair_swap(quad, False, "swap_red")
    both = [jnp.stack([jnp.where(ic == l, a, b) for l in range(2)]) for a, b in zip(quad, other)]
    both[0] = jnp.transpose(both[0][:, :577], (0, 2, 1))
    red = dict(zip(('w_in', 'w_out', 'mlp_w1', 'mlp_w2'), both))
    red['ada_w'] = g_ada_w

    small_names = [n for n in WEIGHTS if n not in BIG and n != 'ada_b']
    gathered = _exchange([g[n] for n in small_names] + [loss], EVERYONE, False, "ag_smallgrad", stage=True)
    summed = _sum_many(gathered, "smallgrad_sum")
    for n, a in zip(small_names, summed[:-1]):
        a = a.reshape(w[n].shape) if n in ('s5_b_re', 's5_b_im') else a
        red[n] = _shard_of(a, SMALL_SHARDED[n], chip) if n in SMALL_SHARDED else a
    red['ada_b'] = g_ada_b
    loss_out = summed[-1].reshape(())

    delta, new_m, new_v = {}, {}, {}
    for n in BIG:
        delta[n], new_m[n], new_v[n] = _adamw(w[n], red[n], mom[n], var[n], "adamw_" + n)
    rest = [n for n in WEIGHTS if n not in BIG]
    outs = _adamw_many([w[n] for n in rest], [red[n] for n in rest], [mom[n] for n in rest], [var[n] for n in rest],
                       "adamw_small")
    for k, n in enumerate(rest):
        delta[n], new_m[n], new_v[n] = outs[3 * k], outs[3 * k + 1], outs[3 * k + 2]

    return (loss_out, grad_x[None], *[red[n] for n in WEIGHTS], *[delta[n] for n in WEIGHTS],
            *[new_m[n] for n in WEIGHTS], *[new_v[n] for n in WEIGHTS])
```

```python
import functools
import math

import jax
import jax.numpy as jnp
from jax import lax
from jax.experimental import pallas as pl
from jax.experimental.pallas import tpu as pltpu

F32 = jnp.float32
BF16 = jnp.bfloat16
HI = lax.Precision.HIGHEST

D = 1024
GW = 256
HID = 4096
EPS = 1e-6
PW = 2304
DTW = 128
SSD_L = 128
SSD_SUB = 2
SSD_SUB_BWD = 1
NH, HP, NS = 4, 64, 128
S5_P = 1024
MESH = pl.DeviceIdType.MESH

ADAM_LR, ADAM_B1, ADAM_B2, ADAM_EPS, ADAM_WD, ADAM_STEP = 0.001, 0.9, 0.999, 1e-08, 0.01, 10

NT = (((1,), (1,)), ((), ()))
TN = (((0,), (0,)), ((), ()))

WEIGHTS = ['norm_mix_w', 'norm_mlp_w', 'ada_w', 'ada_b', 'w_in', 'pool_w', 'pool_scale', 'sconv_w', 'ssd_conv_w',
           'ssd_conv_b', 'ssd_dt_bias', 'ssd_a_log', 'ssd_d', 's5_a_re', 's5_a_im', 's5_log_step', 's5_b_re', 's5_b_im',
           's5_c_re', 's5_c_im', 's5_d', 's5_glu_w', 's5_glu_b', 'branch_norm_w', 'w_out', 'mlp_w1', 'mlp_w2',
           'final_norm_w']
BIG = ('ada_w', 'w_in', 'w_out', 'mlp_w1', 'mlp_w2')
SMALL_SHARDED = {'sconv_w': 2, 'ssd_conv_w': 2, 's5_glu_w': 1}


def _cparams(n_axes, vmem_mb=48):
    return pltpu.CompilerParams(dimension_semantics=("arbitrary",) * n_axes, vmem_limit_bytes=vmem_mb * 1024 * 1024)


def _row(n):
    return pl.BlockSpec((1, n), lambda *_: (0, 0))


def _full(shape):
    nd = len(shape)
    return pl.BlockSpec(tuple(shape), lambda *_: (0,) * nd)


def _dot(a, b, dims=None, prec=None):
    if dims is None:
        dims = (((a.ndim - 1,), (0,)), ((), ()))
    return lax.dot_general(a, b, dims, preferred_element_type=F32, precision=prec)


def _bdot(a, b, dims=None):
    return _dot(a.astype(BF16), b.astype(BF16), dims)


def _sig(x):
    return jax.nn.sigmoid(x)


def _silu(x):
    return x * _sig(x)


def _dsilu(x):
    s = _sig(x)
    return s * (1.0 + x * (1.0 - s))


def _softplus(x):
    return jnp.maximum(x, 0.0) + jnp.log(1.0 + jnp.exp(-jnp.abs(x)))


_GK = math.sqrt(2.0 / math.pi)


def _gelu(x):
    return 0.5 * x * (1.0 + jnp.tanh(_GK * (x + 0.044715 * x * x * x)))


def _dgelu(x):
    th = jnp.tanh(_GK * (x + 0.044715 * x * x * x))
    return 0.5 * (1.0 + th) + 0.5 * x * (1.0 - th * th) * _GK * (1.0 + 3.0 * 0.044715 * x * x)


def _colsum(x):
    return jnp.sum(x, axis=0, keepdims=True)


def _rms(x):
    r = lax.rsqrt(jnp.mean(x * x, axis=-1, keepdims=True) + EPS)
    return r, x * r


def _rms_bwd(r, n, dn):
    return r * (dn - n * jnp.mean(dn * n, axis=-1, keepdims=True))


def _roll(x, k):
    n = x.shape[0]
    k = k % n
    return x if k == 0 else pltpu.roll(x, k, axis=0)


def _tblock(t, want=512):
    return min(t, want)


def _peer(mask):
    x, y, c = lax.axis_index("x"), lax.axis_index("y"), lax.axis_index("c")
    return (x ^ ((mask >> 2) & 1), y ^ ((mask >> 1) & 1), c ^ (mask & 1))


def _group_index(masks):
    x, y, c = lax.axis_index("x"), lax.axis_index("y"), lax.axis_index("c")
    full = 0
    for m in masks:
        full |= m
    bits = [b for b in (4, 2, 1) if full & b]

    def idx(px, py, pc):
        v = {4: px, 2: py, 1: pc}
        out = 0
        for b in bits:
            out = out * 2 + v[b]
        return out

    return idx(x, y, c), [idx(*_peer(m)) for m in masks]


def _exchange(arrs, masks, scatter, name, stage=False):
    n_arr, n_peer, n_grp = len(arrs), len(masks), len(masks) + 1

    def body(*refs):
        ins, outs = refs[:n_arr], refs[n_arr:2 * n_arr]
        send_sems, recv_sems, local_sems = refs[2 * n_arr:]
        me, peer_idx = _group_index(masks)
        copies = []
        for t in range(n_arr):
            src_me = ins[t].at[me] if scatter else ins[t]
            loc = pltpu.make_async_copy(src_me, outs[t].at[me], local_sems.at[t])
            loc.start()
            copies.append(loc)
            for j, m in enumerate(masks):
                src = ins[t].at[peer_idx[j]] if scatter else ins[t]
                cp = pltpu.make_async_remote_copy(src_ref=src, dst_ref=outs[t].at[me], send_sem=send_sems.at[t, j],
                                                  recv_sem=recv_sems.at[t, j], device_id=_peer(m), device_id_type=MESH)
                cp.start()
                copies.append(cp)
        for cp in copies:
            cp.wait()

    hbm = pl.BlockSpec(memory_space=pl.ANY)
    out_shape = [jax.ShapeDtypeStruct((n_grp,) + (a.shape[1:] if scatter else a.shape), a.dtype) for a in arrs]
    src_spec = pl.BlockSpec(memory_space=pltpu.VMEM) if stage else hbm
    outs = pl.pallas_call(
        body, name=name, in_specs=[src_spec] * n_arr, out_specs=[hbm] * n_arr, out_shape=out_shape,
        scratch_shapes=[pltpu.SemaphoreType.DMA((n_arr, n_peer)), pltpu.SemaphoreType.DMA((n_arr, n_peer)),
                        pltpu.SemaphoreType.DMA((n_arr,))],
    )(*arrs)
    return list(outs)


def _gather_copies(src_ref, land_ref, send_sems, recv_sems):
    me, _ = _group_index(CHIPS)
    return [pltpu.make_async_remote_copy(src_ref=src_ref, dst_ref=land_ref.at[me], send_sem=send_sems[j], recv_sem=recv_sems[j],
                                         device_id=_peer(m), device_id_type=MESH) for j, m in enumerate(CHIPS)]


def _gather_start(src, after, name):
    n = len(CHIPS)

    def body(src_ref, land_ref, *rest):
        sems, token = rest[len(after):len(after) + 2 * n], rest[-1]
        for cp in _gather_copies(src_ref, land_ref, sems[:n], sems[n:]):
            cp.start()
        token[...] = jnp.zeros_like(token)

    hbm = pl.BlockSpec(memory_space=pltpu.HBM)
    sem = pl.BlockSpec(memory_space=pltpu.SEMAPHORE)
    land = lax.empty((n + 1,) + src.shape, src.dtype)
    outs = pl.pallas_call(
        body, name=name,
        out_shape=(pltpu.SemaphoreType.DMA(()),) * (2 * n) + (pltpu.HBM(src.shape, src.dtype), pltpu.HBM(land.shape, land.dtype),
                                                              jax.ShapeDtypeStruct((8, 128), F32)),
        in_specs=(hbm, hbm) + (pl.BlockSpec(memory_space=pl.ANY),) * len(after),
        out_specs=(sem,) * (2 * n) + (hbm, hbm, pl.BlockSpec(memory_space=pltpu.VMEM)),
        input_output_aliases={0: 2 * n, 1: 2 * n + 1},
        compiler_params=pltpu.CompilerParams(has_side_effects=pltpu.SideEffectType.DATAFLOW_SIDE_EFFECTING),
    )(pltpu.with_memory_space_constraint(src, pltpu.HBM), pltpu.with_memory_space_constraint(land, pltpu.HBM), *after)
    return outs[:2 * n], outs[2 * n], outs[2 * n + 1], outs[2 * n + 2]


def _gather_wait(sems, src, land, after, name):
    n = len(CHIPS)

    def body(src_ref, land_ref, *rest):
        for cp in _gather_copies(src_ref, land_ref, rest[:n], rest[n:2 * n]):
            cp.wait_send()
            cp.wait_recv()

    hbm = pl.BlockSpec(memory_space=pltpu.HBM)
    sem = pl.BlockSpec(memory_space=pltpu.SEMAPHORE)
    return pl.pallas_call(
        body, name=name, out_shape=(pltpu.HBM(src.shape, src.dtype), pltpu.HBM(land.shape, land.dtype)),
        in_specs=(hbm, hbm) + (sem,) * (2 * n) + (pl.BlockSpec(memory_space=pl.ANY),) * len(after), out_specs=(hbm, hbm),
        input_output_aliases={0: 0, 1: 1},
        compiler_params=pltpu.CompilerParams(has_side_effects=pltpu.SideEffectType.DATAFLOW_SIDE_EFFECTING),
    )(src, land, *sems, *after)[1]


CHIPS = (4, 2, 6)
EVERYONE = (1, 2, 3, 4, 5, 6, 7)
SIBLING = (1,)
SWAP_ROWS = 512
WIN_ROWS = 592


def _pair_swap(arrs, other_layer, name):
    n_arr = len(arrs)
    shapes = [a.shape[-2:] for a in arrs]
    chunks = []
    for t, (rows, _) in enumerate(shapes):
        assert rows % 16 == 0
        for j, r0 in enumerate(range(0, rows, SWAP_ROWS)):
            chunks.append((t, r0, min(SWAP_ROWS, rows - r0), j % 2))

    def body(*refs):
        ins, outs = refs[:n_arr], refs[n_arr:2 * n_arr]
        bufs = refs[2 * n_arr:3 * n_arr]
        load_sems, send_sems, recv_sems = refs[3 * n_arr:]
        sibling = _peer(1)
        c = lax.axis_index("c")

        def load(k):
            t, r0, n, slot = chunks[k]
            src = ins[t].at[1 - c] if other_layer else ins[t]
            return pltpu.make_async_copy(src.at[pl.ds(r0, n)], bufs[t].at[slot, pl.ds(0, n)], load_sems.at[t, slot])

        def send(k):
            t, r0, n, slot = chunks[k]
            return pltpu.make_async_remote_copy(src_ref=bufs[t].at[slot, pl.ds(0, n)], dst_ref=outs[t].at[pl.ds(r0, n)],
                                                send_sem=send_sems.at[t, slot], recv_sem=recv_sems.at[t],
                                                device_id=sibling, device_id_type=MESH)

        in_flight = {}

        def start_load(k):
            key = (chunks[k][0], chunks[k][3])
            if key in in_flight:
                send(in_flight.pop(key)).wait_send()
            load(k).start()

        start_load(0)
        for k in range(len(chunks)):
            load(k).wait()
            if k + 1 < len(chunks):
                start_load(k + 1)
            send(k).start()
            in_flight[(chunks[k][0], chunks[k][3])] = k
        for k in in_flight.values():
            send(k).wait_send()
        for t in range(n_arr):
            pltpu.make_async_remote_copy(src_ref=outs[t], dst_ref=outs[t], send_sem=send_sems.at[t, 0],
                                         recv_sem=recv_sems.at[t], device_id=sibling, device_id_type=MESH).wait_recv()

    hbm = pl.BlockSpec(memory_space=pl.ANY)
    outs = pl.pallas_call(
        body, name=name, in_specs=[hbm] * n_arr, out_specs=[hbm] * n_arr,
        out_shape=[jax.ShapeDtypeStruct(s, a.dtype) for s, a in zip(shapes, arrs)],
        scratch_shapes=[pltpu.VMEM((2, min(SWAP_ROWS, s[0]), s[1]), a.dtype) for s, a in zip(shapes, arrs)]
        + [pltpu.SemaphoreType.DMA((n_arr, 2)), pltpu.SemaphoreType.DMA((n_arr, 2)), pltpu.SemaphoreType.DMA((n_arr,))],
        compiler_params=pltpu.CompilerParams(vmem_limit_bytes=48 * 1024 * 1024),
    )(*arrs)
    return list(outs)


def _sum_lead(a, name, out_dtype):
    n = a.shape[0]
    shape = a.shape[1:]

    def body(a_ref, o_ref):
        acc = a_ref[0].astype(F32)
        for k in range(1, n):
            acc = acc + a_ref[k].astype(F32)
        o_ref[...] = acc.astype(out_dtype)

    if len(shape) == 3:
        blk = (1,) + shape[1:]
        return pl.pallas_call(
            body, name=name, grid=(shape[0],), in_specs=[pl.BlockSpec((n,) + blk, lambda i: (0, i, 0, 0))],
            out_specs=pl.BlockSpec(blk, lambda i: (i, 0, 0)), out_shape=jax.ShapeDtypeStruct(shape, out_dtype),
            compiler_params=_cparams(1),
        )(a)
    rows, cols = shape
    rb = rows
    for cand in (512, 256, 128):
        if rows % cand == 0 and rows > cand:
            rb = cand
            break
    return pl.pallas_call(
        body, name=name, grid=(rows // rb,), in_specs=[pl.BlockSpec((n, rb, cols), lambda i: (0, i, 0))],
        out_specs=pl.BlockSpec((rb, cols), lambda i: (i, 0)), out_shape=jax.ShapeDtypeStruct((rows, cols), out_dtype),
        compiler_params=_cparams(1),
    )(a)


def _pair_sum(g, recv, layer, name, out_dtype):
    _, n, r, c = g.shape

    def body(l_ref, g_ref, r_ref, o_ref):
        o_ref[...] = (g_ref[0].astype(F32) + r_ref[...].astype(F32)).astype(out_dtype)

    return pl.pallas_call(
        body, name=name,
        grid_spec=pltpu.PrefetchScalarGridSpec(
            num_scalar_prefetch=1, grid=(n,),
            in_specs=[pl.BlockSpec((1, 1, r, c), lambda i, l: (l[0], i, 0, 0)), pl.BlockSpec((1, r, c), lambda i, l: (i, 0, 0))],
            out_specs=pl.BlockSpec((1, r, c), lambda i, l: (i, 0, 0))),
        out_shape=jax.ShapeDtypeStruct((n, r, c), out_dtype), compiler_params=_cparams(1),
    )(layer, g, recv)


def _tn_matmul(a, b, name, col_major=False):
    t, k = a.shape
    n = b.shape[1]
    tb = _tblock(t, 1024)
    kb = min(k, 1024)
    nb = min(n, 1024)
    grid = (k // kb, n // nb, t // tb)

    def body(a_ref, b_ref, o_ref):
        @pl.when(pl.program_id(2) == 0)
        def _():
            o_ref[...] = jnp.zeros_like(o_ref)

        acc = _bdot(a_ref[...], b_ref[...], TN)
        if col_major:
            o_ref[0] += acc
        else:
            o_ref[...] += acc

    if col_major:
        out_spec = pl.BlockSpec((1, kb, nb), lambda ki, ni, ti: (ni, ki, 0))
        out_shape = jax.ShapeDtypeStruct((n // nb, k, nb), F32)
    else:
        out_spec = pl.BlockSpec((kb, nb), lambda ki, ni, ti: (ki, ni))
        out_shape = jax.ShapeDtypeStruct((k, n), F32)
    return pl.pallas_call(
        body, name=name, grid=grid,
        in_specs=[pl.BlockSpec((tb, kb), lambda ki, ni, ti: (ti, ki)), pl.BlockSpec((tb, nb), lambda ki, ni, ti: (ti, ni))],
        out_specs=out_spec, out_shape=out_shape, compiler_params=_cparams(3),
    )(a, b)


def _sum_many(arrs, name):
    k = len(arrs)

    def body(*refs):
        for a_ref, o_ref in zip(refs[:k], refs[k:]):
            acc = a_ref[0]
            for j in range(1, a_ref.shape[0]):
                acc = acc + a_ref[j]
            o_ref[...] = acc

    return pl.pallas_call(body, name=name, out_shape=[jax.ShapeDtypeStruct(a.shape[1:], F32) for a in arrs],
                          compiler_params=pltpu.CompilerParams(vmem_limit_bytes=48 * 1024 * 1024))(*arrs)


def _adamw_math(w, g, m, v):
    m2 = ADAM_B1 * m + (1.0 - ADAM_B1) * g
    v2 = ADAM_B2 * v + (1.0 - ADAM_B2) * (g * g)
    m_hat = m2 / (1.0 - ADAM_B1 ** ADAM_STEP)
    v_hat = v2 / (1.0 - ADAM_B2 ** ADAM_STEP)
    return -ADAM_LR * (m_hat / (jnp.sqrt(v_hat) + ADAM_EPS) + ADAM_WD * w), m2, v2


def _adamw_many(ws, gs, ms, vs, name):
    n = len(ws)

    def body(*refs):
        ins, outs = refs[:4 * n], refs[4 * n:]
        for k in range(n):
            res = _adamw_math(ins[k][...], ins[n + k][...], ins[2 * n + k][...], ins[3 * n + k][...])
            for j in range(3):
                outs[3 * k + j][...] = res[j]

    out_shape = []
    for a in ws:
        out_shape += [jax.ShapeDtypeStruct(a.shape, F32)] * 3
    return pl.pallas_call(body, name=name, out_shape=out_shape,
                          compiler_params=pltpu.CompilerParams(vmem_limit_bytes=48 * 1024 * 1024))(*ws, *gs, *ms, *vs)


def _adamw(w, g, m, v, name):
    shape = w.shape
    cols = shape[-1]
    rows = int(math.prod(shape[:-1]))
    rb = rows
    for cand in (256, 128, 64, 32, 16, 8):
        if rows % cand == 0 and rows > cand:
            rb = cand
            break
    bc1 = 1.0 - ADAM_B1 ** ADAM_STEP
    bc2 = 1.0 - ADAM_B2 ** ADAM_STEP

    def body(w_ref, g_ref, m_ref, v_ref, d_ref, nm_ref, nv_ref):
        gg = g_ref[...]
        m2 = ADAM_B1 * m_ref[...] + (1.0 - ADAM_B1) * gg
        v2 = ADAM_B2 * v_ref[...] + (1.0 - ADAM_B2) * (gg * gg)
        m_hat = m2 / bc1
        v_hat = v2 / bc2
        d_ref[...] = -ADAM_LR * (m_hat / (jnp.sqrt(v_hat) + ADAM_EPS) + ADAM_WD * w_ref[...])
        nm_ref[...] = m2
        nv_ref[...] = v2

    spec = pl.BlockSpec((rb, cols), lambda i: (i, 0))
    sds = jax.ShapeDtypeStruct((rows, cols), F32)
    outs = pl.pallas_call(
        body, name=name, grid=(rows // rb,), in_specs=[spec] * 4, out_specs=[spec] * 3, out_shape=[sds] * 3,
        compiler_params=_cparams(1),
    )(*(z.reshape(rows, cols) for z in (w, g, m, v)))
    return tuple(o.reshape(shape) for o in outs)


def _ada_fwd(c_all, ada_w_sh, ada_b_sh):
    s = ada_w_sh.shape[2]
    sb = 512

    def body(c_ref, w_ref, b_ref, o_ref):
        cond = _silu(c_ref[...])
        o_ref[0] = _bdot(cond, w_ref[0]) + b_ref[0]

    return pl.pallas_call(
        body, name="ada_fwd", grid=(2, s // sb),
        in_specs=[_full((8, D)), pl.BlockSpec((1, D, sb), lambda l, j: (l, 0, j)), pl.BlockSpec((1, 1, sb), lambda l, j: (l, 0, j))],
        out_specs=pl.BlockSpec((1, 8, sb), lambda l, j: (l, 0, j)), out_shape=jax.ShapeDtypeStruct((2, 8, s), F32),
        compiler_params=_cparams(2),
    )(c_all, ada_w_sh, ada_b_sh)


def _ada_bwd(c_all, dmod_sh, dmod_all):
    s = dmod_sh.shape[2]
    sb = 512

    def body(c_ref, d_ref, o_ref):
        cond = _silu(c_ref[...])
        o_ref[0] = _bdot(cond, d_ref[0], TN)

    gw = pl.pallas_call(
        body, name="ada_bwd_w", grid=(2, s // sb),
        in_specs=[_full((8, D)), pl.BlockSpec((1, 8, sb), lambda l, j: (l, 0, j))],
        out_specs=pl.BlockSpec((1, D, sb), lambda l, j: (l, 0, j)), out_shape=jax.ShapeDtypeStruct((2, D, s), F32),
        compiler_params=_cparams(2),
    )(c_all, dmod_sh)

    def body_b(d_ref, o_ref):
        acc = d_ref[0, 0:1, :]
        for k in range(1, 8):
            acc = acc + d_ref[0, k:k + 1, :]
        o_ref[0] = acc

    gb = pl.pallas_call(
        body_b, name="ada_bwd_b", grid=(2,), in_specs=[pl.BlockSpec((1, 8, 6 * D), lambda l: (l, 0, 0))],
        out_specs=pl.BlockSpec((1, 1, 6 * D), lambda l: (l, 0, 0)), out_shape=jax.ShapeDtypeStruct((2, 1, 6 * D), F32),
        compiler_params=_cparams(1),
    )(dmod_all)
    return gw, gb.reshape(2, 6 * D)


def _f_in(h, nw, sc, sh, w_main, w_dt):
    t = h.shape[0]
    tb = _tblock(t)

    def body(h_ref, nw_ref, sc_ref, sh_ref, w_ref, wd_ref, p_ref, dt_ref, u_ref):
        _, n = _rms(h_ref[...])
        u = ((n * nw_ref[...]) * (1.0 + sc_ref[...]) + sh_ref[...]).astype(BF16)
        u_ref[...] = u
        p_ref[...] = _dot(u, w_ref[...], NT)
        dt_ref[...] = _dot(u, wd_ref[...], NT)

    return pl.pallas_call(
        body, name="f_in", grid=(t // tb,),
        in_specs=[pl.BlockSpec((tb, D), lambda i: (i, 0)), _row(D), _row(D), _row(D), _full((PW, D)), _full((DTW, D))],
        out_specs=[pl.BlockSpec((tb, PW), lambda i: (i, 0)), pl.BlockSpec((tb, DTW), lambda i: (i, 0)),
                   pl.BlockSpec((tb, D), lambda i: (i, 0))],
        out_shape=[jax.ShapeDtypeStruct((t, PW), F32), jax.ShapeDtypeStruct((t, DTW), F32), jax.ShapeDtypeStruct((t, D), BF16)],
        compiler_params=_cparams(1),
    )(h, nw, sc, sh, w_main, w_dt)


def _b_in_du(dab, dz, dxbc, ds5, ddt, w_main, w_dt):
    t = dab.shape[0]
    tb = _tblock(t)

    def body(a_ref, z_ref, x_ref, s_ref, d_ref, w_ref, wd_ref, o_ref):
        acc = _bdot(a_ref[...], w_ref[0:1024, :])
        acc += _bdot(z_ref[...], w_ref[1024:1280, :])
        acc += _bdot(s_ref[...], w_ref[1280:1536, :])
        acc += _bdot(x_ref[...], w_ref[1536:2304, :])
        acc += _bdot(d_ref[...], wd_ref[...])
        o_ref[...] = acc

    blk = lambda n: pl.BlockSpec((tb, n), lambda i: (i, 0))
    return pl.pallas_call(
        body, name="b_in_du", grid=(t // tb,),
        in_specs=[blk(1024), blk(256), blk(768), blk(256), blk(DTW), _full((PW, D)), _full((DTW, D))],
        out_specs=blk(D), out_shape=jax.ShapeDtypeStruct((t, D), F32), compiler_params=_cparams(1),
    )(dab, dz, dxbc, ds5, ddt, w_main, w_dt)


def _b_normmod(du, x, dres, gated, nw, sc, name):
    t = x.shape[0]
    tb = _tblock(t)

    def body(du_ref, x_ref, dr_ref, g_ref, nw_ref, sc_ref, dx_ref, dsc_ref, dsh_ref, dnw_ref, dg_ref):
        @pl.when(pl.program_id(0) == 0)
        def _():
            for r in (dsc_ref, dsh_ref, dnw_ref, dg_ref):
                r[...] = jnp.zeros_like(r)

        du_v = du_ref[...]
        r, n = _rms(x_ref[...])
        nwv = nw_ref[...]
        scale = 1.0 + sc_ref[...]
        dsc_ref[...] += _colsum(du_v * (n * nwv))
        dsh_ref[...] += _colsum(du_v)
        dnw_ref[...] += _colsum(du_v * scale * n)
        dres_v = dr_ref[...]
        dg_ref[...] += _colsum(dres_v * g_ref[...])
        dx_ref[...] = dres_v + _rms_bwd(r, n, du_v * scale * nwv)

    blk = pl.BlockSpec((tb, D), lambda i: (i, 0))
    row = jax.ShapeDtypeStruct((1, D), F32)
    return pl.pallas_call(
        body, name=name, grid=(t // tb,), in_specs=[blk, blk, blk, blk, _row(D), _row(D)],
        out_specs=[blk, _row(D), _row(D), _row(D), _row(D)], out_shape=[jax.ShapeDtypeStruct((t, D), F32), row, row, row, row],
        compiler_params=_cparams(1),
    )(du, x, dres, gated, nw, sc)


HALO = 16


def _lane_group(shape):
    return lax.broadcasted_iota(jnp.int32, shape, 1) // 64


def _window_select(g, s2, s4, s8, s16):
    return jnp.where(g == 0, s2, jnp.where(g == 1, s4, jnp.where(g == 2, s8, s16)))


def _pool_count(t0, rows):
    g = _lane_group((rows, GW))
    win = _window_select(g, 2, 4, 8, 16)
    tt = t0 + lax.broadcasted_iota(jnp.int32, (rows, GW), 0)
    return jnp.minimum(tt + 1, win).astype(F32)


def _pool_p(v_ext, t0, tb):
    s2 = v_ext + _roll(v_ext, 1)
    s4 = s2 + _roll(s2, 2)
    s8 = s4 + _roll(s4, 4)
    s16 = s8 + _roll(s8, 8)
    ws = _window_select(_lane_group(v_ext.shape), s2, s4, s8, s16)[HALO:]
    return ws / _pool_count(t0, tb) - v_ext[HALO:]


def _sconv(q_ext, w):
    return (_roll(q_ext, 2) * w[0:1] + _roll(q_ext, 1) * w[1:2] + q_ext * w[2:3])[HALO:]


def _halo_specs(t, tb, cols, col_block):
    per = tb // HALO
    last = t // HALO - 1
    prev = pl.BlockSpec((HALO, cols), lambda i: (jnp.maximum(i * per - 1, 0), col_block))
    nxt = pl.BlockSpec((HALO, cols), lambda i: (jnp.minimum((i + 1) * per, last), col_block))
    return prev, nxt


def _f_ab(proj, pool_mat, pool_scale, sconv_w):
    t = proj.shape[0]
    tb = _tblock(t)
    prev, _ = _halo_specs(t, tb, 1024, 0)

    def body(p_ref, h_ref, pm_ref, ps_ref, sw_ref, ya_ref, yb_ref):
        i = pl.program_id(0)
        halo = jnp.where(i > 0, h_ref[...], 0.0)
        ext = jnp.concatenate([halo, p_ref[...]], axis=0)
        p = _pool_p(ext[:, 0:256], i * tb, tb)
        ya_ref[...] = _bdot(p, pm_ref[...]) * ps_ref[...]
        q_ext = ext[:, 512:768] * ext[:, 768:1024]
        yb_ref[...] = p_ref[:, 256:512] * _sconv(q_ext, sw_ref[...])

    blk = pl.BlockSpec((tb, GW), lambda i: (i, 0))
    sds = jax.ShapeDtypeStruct((t, GW), F32)
    return pl.pallas_call(
        body, name="f_ab", grid=(t // tb,),
        in_specs=[pl.BlockSpec((tb, 1024), lambda i: (i, 0)), prev, _full((GW, GW)), _row(GW), _full((3, GW))],
        out_specs=[blk, blk], out_shape=[sds, sds], compiler_params=_cparams(1),
    )(proj, proj, pool_mat, pool_scale, sconv_w)


def _b_ab(proj, dya, dyb, pool_mat, pool_scale, sconv_w):
    t = proj.shape[0]
    tb = _tblock(t)
    nb = t // tb
    prev, nxt = _halo_specs(t, tb, 1024, 0)
    _, nxt_g = _halo_specs(t, tb, GW, 0)
    n_ext = tb + HALO

    def body(p_ref, hp_ref, hn_ref, da_ref, dan_ref, db_ref, dbn_ref, pm_ref, ps_ref, sw_ref,
             o_ref, dpm_ref, dps_ref, dsw_ref):
        i = pl.program_id(0)

        @pl.when(i == 0)
        def _():
            for r in (dpm_ref, dps_ref, dsw_ref):
                r[...] = jnp.zeros_like(r)

        last = i == nb - 1
        halo = jnp.where(i > 0, hp_ref[...], 0.0)
        main = p_ref[...]
        ext = jnp.concatenate([halo, main], axis=0)
        scale = ps_ref[...]
        pm = pm_ref[...]
        p = _pool_p(ext[:, 0:256], i * tb, tb)
        da = da_ref[...]
        dps_ref[...] += _colsum(da * _bdot(p, pm))
        da_ext = jnp.concatenate([da, jnp.where(last, 0.0, dan_ref[...])], axis=0)
        dys = da_ext * scale
        dpm_ref[...] += _bdot(p, dys[:tb], TN)
        dp = _bdot(dys, pm, NT)
        dpc = dp / _pool_count(i * tb, n_ext)
        a2 = dpc + _roll(dpc, n_ext - 1)
        a4 = a2 + _roll(a2, n_ext - 2)
        a8 = a4 + _roll(a4, n_ext - 4)
        a16 = a8 + _roll(a8, n_ext - 8)
        o_ref[:, 0:256] = (_window_select(_lane_group(dpc.shape), a2, a4, a8, a16) - dp)[:tb]
        w = sw_ref[...]
        gb, gc, hh = main[:, 256:512], main[:, 512:768], main[:, 768:1024]
        q_ext = ext[:, 512:768] * ext[:, 768:1024]
        db = db_ref[...]
        o_ref[:, 256:512] = db * _sconv(q_ext, w)
        gb_next = hn_ref[:, 256:512]
        dconv = jnp.concatenate([db * gb, jnp.where(last, 0.0, dbn_ref[...] * gb_next)], axis=0)
        dq = (dconv * w[2:3] + _roll(dconv, n_ext - 1) * w[1:2] + _roll(dconv, n_ext - 2) * w[0:1])[:tb]
        o_ref[:, 512:768] = dq * hh
        o_ref[:, 768:1024] = dq * gc
        dc = dconv[:tb]
        dsw_ref[0:1, :] += _colsum(dc * _roll(q_ext, 2)[HALO:])
        dsw_ref[1:2, :] += _colsum(dc * _roll(q_ext, 1)[HALO:])
        dsw_ref[2:3, :] += _colsum(dc * q_ext[HALO:])

    blk = pl.BlockSpec((tb, GW), lambda i: (i, 0))
    return pl.pallas_call(
        body, name="b_ab", grid=(nb,),
        in_specs=[pl.BlockSpec((tb, 1024), lambda i: (i, 0)), prev, nxt, blk, nxt_g, blk, nxt_g,
                  _full((GW, GW)), _row(GW), _full((3, GW))],
        out_specs=[pl.BlockSpec((tb, 1024), lambda i: (i, 0)), _full((GW, GW)), _row(GW), _full((3, GW))],
        out_shape=[jax.ShapeDtypeStruct((t, 1024), F32), jax.ShapeDtypeStruct((GW, GW), F32),
                   jax.ShapeDtypeStruct((1, GW), F32), jax.ShapeDtypeStruct((3, GW), F32)],
        compiler_params=_cparams(1),
    )(proj, proj, proj, dya, dya, dyb, dyb, pool_mat, pool_scale, sconv_w)


CH = 8


def _ssd_conv(x, halo, w, b):
    ext = jnp.concatenate([halo, x], axis=0)
    pre = ext * w[3:4] + _roll(ext, 1) * w[2:3] + _roll(ext, 2) * w[1:2] + _roll(ext, 3) * w[0:1] + b
    return pre[CH:], ext


def _ssd_common(dt_raw, dtb, alog):
    ll = dt_raw.shape[0]
    dtv = _softplus(dt_raw + dtb)
    a_row = -jnp.exp(alog)
    r = lax.broadcasted_iota(jnp.int32, (ll, ll), 0)
    c = lax.broadcasted_iota(jnp.int32, (ll, ll), 1)
    tril = (r >= c).astype(F32)
    cs = _dot(tril, dtv * a_row, prec=HI)
    return dtv, a_row, cs, cs.T, r >= c


def _bd(a, b, ca, cb):
    return lax.dot_general(a, b, (((ca,), (cb,)), ((0,), (0,))), preferred_element_type=F32)


def _head_cols(m):
    return jnp.stack([m[:, h:h + 1] for h in range(NH)])


def _ssd_heads(act, dtv, cs, cs_t, causal):
    xs = jnp.stack([act[:, HP * h:HP * (h + 1)] for h in range(NH)])
    bm = jnp.stack([act[:, 256 + NS * (h // 2):256 + NS * (h // 2 + 1)] for h in range(NH)])
    cm = jnp.stack([act[:, 512 + NS * (h // 2):512 + NS * (h // 2 + 1)] for h in range(NH)])
    cs_c = _head_cols(cs)
    cs_r = jnp.stack([cs_t[h:h + 1, :] for h in range(NH)])
    mdec = jnp.where(causal[None], jnp.exp(jnp.minimum(cs_c - cs_r, 0.0)), 0.0)
    g2 = _bd(jnp.stack([cm[0], cm[2]]), jnp.stack([bm[0], bm[2]]), 2, 2)
    sc = jnp.stack([g2[h // 2] for h in range(NH)]) * mdec
    dt_c = _head_cols(dtv)
    xdt = xs * dt_c
    e = jnp.exp(cs_c)
    cs_last = cs_c[:, SSD_L - 1:SSD_L, :]
    wdec = jnp.exp(cs_last - cs_c)
    return xs, bm, cm, mdec, sc, dt_c, xdt, e, cs_last, wdec


def _head_scalars(row_ref):
    return jnp.stack([row_ref[0:1, h:h + 1] for h in range(NH)])


def _f_ssd(proj, dtp, conv_w, conv_b, dt_bias, a_log, d_skip):
    t = proj.shape[0]
    nc = t // SSD_L
    rows = SSD_SUB * SSD_L
    per = rows // CH

    def body(x_ref, hx_ref, dt_ref, z_ref, cw_ref, cb_ref, dtb_ref, al_ref, dk_ref, y_ref, yp_ref, sp_ref, s_ref):
        i = pl.program_id(0)

        @pl.when(i == 0)
        def _():
            s_ref[...] = jnp.zeros_like(s_ref)

        state = s_ref[...]
        dk = _head_scalars(dk_ref)
        for sub in range(SSD_SUB):
            r0 = sub * SSD_L
            rs = slice(r0, r0 + SSD_L)
            halo = jnp.where(i > 0, hx_ref[...], 0.0) if sub == 0 else x_ref[r0 - CH:r0, :]
            pre, _ = _ssd_conv(x_ref[rs, :], halo, cw_ref[...], cb_ref[...])
            act = _silu(pre)
            dtv, _, cs, cs_t, causal = _ssd_common(dt_ref[rs, :], dtb_ref[...], al_ref[...])
            xs, bm, cm, _, sc, _, xdt, e, cs_last, wdec = _ssd_heads(act, dtv, cs, cs_t, causal)
            sp_ref[sub] = state
            y = _bd(sc, xdt, 2, 1) + e * _bd(cm, state, 2, 2) + xs * dk
            for h in range(NH):
                yp_ref[rs, HP * h:HP * (h + 1)] = y[h]
            state = state * jnp.exp(cs_last) + _bd(xdt * wdec, bm, 1, 1)
            y_ref[rs, :] = yp_ref[rs, :] * _silu(z_ref[rs, :])
        s_ref[...] = state

    blk = pl.BlockSpec((rows, GW), lambda i: (i, 0))
    sds = jax.ShapeDtypeStruct((t, GW), F32)
    return pl.pallas_call(
        body, name="f_ssd", grid=(nc // SSD_SUB,),
        in_specs=[pl.BlockSpec((rows, 768), lambda i: (i, 2)),
                  pl.BlockSpec((CH, 768), lambda i: (jnp.maximum(i * per - 1, 0), 2)),
                  pl.BlockSpec((rows, DTW), lambda i: (i, 0)),
                  pl.BlockSpec((rows, GW), lambda i: (i, 4)),
                  _full((4, 768)), _row(768), _row(DTW), _row(DTW), _row(DTW)],
        out_specs=[blk, blk, pl.BlockSpec((SSD_SUB, NH, HP, NS), lambda i: (i, 0, 0, 0))],
        out_shape=[sds, sds, jax.ShapeDtypeStruct((nc, NH, HP, NS), F32)],
        scratch_shapes=[pltpu.VMEM((NH, HP, NS), F32)], compiler_params=_cparams(1),
    )(proj, proj, dtp, proj, conv_w, conv_b, dt_bias, a_log, d_skip)


def _b_ssd(proj, dtp, ypre, dyc, sprev, conv_w, conv_b, dt_bias, a_log, d_skip):
    t = proj.shape[0]
    nc = t // SSD_L
    steps = nc // SSD_SUB_BWD
    rows = SSD_SUB_BWD * SSD_L
    per = rows // CH
    n_ext = SSD_L + CH

    def chunk(sub, halo, dnext, ds_in, refs):
        (x_ref, dt_ref, z_ref, yp_ref, dy_ref, sp_ref, cw_ref, cb_ref, dtb_ref, al_ref, dk_ref,
         dz_ref, dx_ref, ddt_ref, dact_ref) = refs
        rs = slice(sub * SSD_L, (sub + 1) * SSD_L)
        dact = dact_ref.at[sub]
        w = cw_ref[...]
        pre, ext = _ssd_conv(x_ref[rs, :], halo, w, cb_ref[...])
        act = _silu(pre)
        dt_raw = dt_ref[rs, :]
        dtv, a_row, cs, cs_t, causal = _ssd_common(dt_raw, dtb_ref[...], al_ref[...])
        z = z_ref[rs, :]
        dyc_v = dy_ref[rs, :]
        dz_ref[rs, :] = dyc_v * yp_ref[rs, :] * _dsilu(z)
        dy_all = dyc_v * _silu(z)
        lane = lax.broadcasted_iota(jnp.int32, (SSD_L, DTW), 1)
        rowi = lax.broadcasted_iota(jnp.int32, (1, SSD_L, 1), 1)
        lane1 = lax.broadcasted_iota(jnp.int32, (1, DTW), 1)
        xs, bm, cm, mdec, sc, dt_c, xdt, e, cs_last, wdec = _ssd_heads(act, dtv, cs, cs_t, causal)
        dy = jnp.stack([dy_all[:, HP * h:HP * (h + 1)] for h in range(NH)])
        prev = sp_ref[sub]
        ds = ds_in
        lsum = lambda v: jnp.sum(v, axis=2, keepdims=True)
        dsc = _bd(dy, xdt, 2, 2)
        q = dsc * sc
        dg = dsc * mdec
        dxdt = _bd(sc, dy, 1, 1)
        dcs = lsum(q) - lsum(jnp.swapaxes(q, 1, 2))
        dc = _bd(dg, bm, 2, 1)
        db = _bd(dg, cm, 1, 1)
        cp = _bd(cm, prev, 2, 2)
        dcs += lsum(dy * cp) * e
        ey = e * dy
        dc += _bd(ey, prev, 2, 1)
        dprev = _bd(ey, cm, 1, 1)
        elast = jnp.exp(cs_last)
        dprev += ds * elast
        dcs_last = jnp.sum(lsum(ds * prev), axis=1, keepdims=True) * elast
        bds = _bd(bm, ds, 2, 2)
        dxdt += wdec * bds
        db += wdec * _bd(xdt, ds, 2, 1)
        dw = lsum(xdt * bds) * wdec
        dcs -= dw
        dcs_last += jnp.sum(dw, axis=1, keepdims=True)
        dcs += jnp.where(rowi == SSD_L - 1, dcs_last, 0.0)
        dxs = dxdt * dt_c + dy * _head_scalars(dk_ref)
        ddtx = lsum(dxdt * xs)
        ddk = jnp.sum(lsum(dy * xs), axis=1, keepdims=True)
        dcs_mat = jnp.zeros((SSD_L, DTW), F32)
        ddtx_mat = jnp.zeros((SSD_L, DTW), F32)
        ddk_row = jnp.zeros((1, DTW), F32)
        for h in range(NH):
            dact[:, HP * h:HP * (h + 1)] = dxs[h]
            dcs_mat = jnp.where(lane == h, dcs[h], dcs_mat)
            ddtx_mat = jnp.where(lane == h, ddtx[h], ddtx_mat)
            ddk_row = jnp.where(lane1 == h, ddk[h], ddk_row)
        for g in range(2):
            dact[:, 256 + NS * g:256 + NS * (g + 1)] = db[2 * g] + db[2 * g + 1]
            dact[:, 512 + NS * g:512 + NS * (g + 1)] = dc[2 * g] + dc[2 * g + 1]
        ds_out = dprev
        r2 = lax.broadcasted_iota(jnp.int32, (SSD_L, SSD_L), 0)
        c2 = lax.broadcasted_iota(jnp.int32, (SSD_L, SSD_L), 1)
        dadt = _dot((c2 >= r2).astype(F32), dcs_mat, prec=HI)
        ddt = jnp.where(lane < NH, (dadt * a_row + ddtx_mat) * _sig(dt_raw + dtb_ref[...]), 0.0)
        ddt_ref[rs, :] = ddt
        dpre = dact[...] * _dsilu(pre)
        dcw = jnp.concatenate([_colsum(dpre * _roll(ext, 3 - k)[CH:]) for k in range(4)], axis=0)
        dext = jnp.concatenate([dpre, dnext], axis=0)
        dx_ref[rs, :] = (dext * w[3:4] + _roll(dext, n_ext - 1) * w[2:3] + _roll(dext, n_ext - 2) * w[1:2]
                         + _roll(dext, n_ext - 3) * w[0:1])[:SSD_L]
        acc = (dcw, _colsum(dpre), _colsum(ddt), _colsum(dadt * dtv) * a_row, ddk_row)
        return dpre[0:CH], ds_out, acc

    def body(x_ref, hx_ref, dt_ref, z_ref, yp_ref, dy_ref, sp_ref, cw_ref, cb_ref, dtb_ref, al_ref, dk_ref,
             dz_ref, dx_ref, ddt_ref, dcw_ref, dcb_ref, ddtb_ref, dal_ref, ddk_ref, ds_ref, dnext_ref, dact_ref):
        i = pl.program_id(0)
        acc_refs = (dcw_ref, dcb_ref, ddtb_ref, dal_ref, ddk_ref)

        @pl.when(i == 0)
        def _():
            ds_ref[...] = jnp.zeros_like(ds_ref)
            dnext_ref[...] = jnp.zeros_like(dnext_ref)
            for r in acc_refs:
                r[...] = jnp.zeros_like(r)

        refs = (x_ref, dt_ref, z_ref, yp_ref, dy_ref, sp_ref, cw_ref, cb_ref, dtb_ref, al_ref, dk_ref, dz_ref, dx_ref, ddt_ref,
                dact_ref)
        ds = ds_ref[...]
        dnext = dnext_ref[...]
        total = None
        for sub in reversed(range(SSD_SUB_BWD)):
            if sub == 0:
                halo = jnp.where(i == steps - 1, 0.0, hx_ref[...])
            else:
                halo = x_ref[sub * SSD_L - CH:sub * SSD_L, :]
            dnext, ds, acc = chunk(sub, halo, dnext, ds, refs)
            total = acc if total is None else tuple(a + b for a, b in zip(total, acc))
        ds_ref[...] = ds
        dnext_ref[...] = dnext
        for r, v in zip(acc_refs, total):
            r[...] += v

    rev = lambda i: steps - 1 - i
    blk = lambda n, cb=0: pl.BlockSpec((rows, n), lambda i: (rev(i), cb))
    row = lambda n: jax.ShapeDtypeStruct((1, n), F32)
    return pl.pallas_call(
        body, name="b_ssd", grid=(steps,),
        in_specs=[blk(768, 2), pl.BlockSpec((CH, 768), lambda i: (jnp.maximum(rev(i) * per - 1, 0), 2)),
                  blk(DTW), blk(GW, 4), blk(GW), blk(GW), pl.BlockSpec((SSD_SUB_BWD, NH, HP, NS), lambda i: (rev(i), 0, 0, 0)),
                  _full((4, 768)), _row(768), _row(DTW), _row(DTW), _row(DTW)],
        out_specs=[blk(GW), blk(768), blk(DTW), _full((4, 768)), _row(768), _row(DTW), _row(DTW), _row(DTW)],
        out_shape=[jax.ShapeDtypeStruct((t, GW), F32), jax.ShapeDtypeStruct((t, 768), F32), jax.ShapeDtypeStruct((t, DTW), F32),
                   jax.ShapeDtypeStruct((4, 768), F32), row(768), row(DTW), row(DTW), row(DTW)],
        scratch_shapes=[pltpu.VMEM((NH, HP, NS), F32), pltpu.VMEM((CH, 768), F32), pltpu.VMEM((SSD_SUB_BWD, SSD_L, 768), F32)],
        compiler_params=_cparams(1),
    )(proj, proj, dtp, proj, ypre, dyc, sprev, conv_w, conv_b, dt_bias, a_log, d_skip)


def _s5_block(t):
    return min(t, 256)


def _seg_t():
    r = lax.broadcasted_iota(jnp.int32, (64, 1024), 0)
    c = lax.broadcasted_iota(jnp.int32, (64, 1024), 1)
    return (c // 16 == r).astype(F32)


def _s5_prep_math(a_re, a_im, lstep, b_re, b_im):
    step = jnp.exp(lstep)
    ars = a_re * step
    ais = a_im * step
    mag = jnp.exp(ars)
    lr = mag * jnp.cos(ais)
    li = mag * jnp.sin(ais)
    den = a_re * a_re + a_im * a_im
    nr = lr - 1.0
    f_re = (nr * a_re + li * a_im) / den
    f_im = (li * a_re - nr * a_im) / den
    seg = _seg_t()
    fr = _dot(f_re, seg, prec=HI)
    fi = _dot(f_im, seg, prec=HI)
    return lr, li, fr * b_re - fi * b_im, fr * b_im + fi * b_re, ars, ais


def _s5_prep(a_re, a_im, lstep, b_re, b_im):
    def body(ar, ai, ls, br, bi, lr_o, li_o, bbr_o, bbi_o, ars_o, ais_o):
        outs = _s5_prep_math(ar[...], ai[...], ls[...], br[...], bi[...])
        for o, v in zip((lr_o, li_o, bbr_o, bbi_o, ars_o, ais_o), outs):
            o[...] = v

    s64 = jax.ShapeDtypeStruct((16, 64), F32)
    s1k = jax.ShapeDtypeStruct((16, 1024), F32)
    return pl.pallas_call(body, name="s5_prep", out_shape=[s64, s64, s1k, s1k, s64, s64])(a_re, a_im, lstep, b_re, b_im)


def _s5_prep_bwd(a_re, a_im, lstep, b_re, b_im, dlr, dli, dbbr, dbbi):
    def body(ar, ai, ls, br, bi, g0, g1, g2, g3, o0, o1, o2, o3, o4):
        f = lambda *a: _s5_prep_math(*a)[:4]
        _, vjp = jax.vjp(f, ar[...], ai[...], ls[...], br[...], bi[...])
        for o, v in zip((o0, o1, o2, o3, o4), vjp((g0[...], g1[...], g2[...], g3[...]))):
            o[...] = v

    s64 = jax.ShapeDtypeStruct((16, 64), F32)
    s1k = jax.ShapeDtypeStruct((16, 1024), F32)
    return pl.pallas_call(body, name="s5_prep_bwd", out_shape=[s64, s64, jax.ShapeDtypeStruct((16, 1), F32), s1k, s1k])(
        a_re, a_im, lstep, b_re, b_im, dlr, dli, dbbr, dbbi)


SUB = 8


def _s5_tables(ars, ais):
    def body(ar, ai, tr, ti):
        rr = lax.broadcasted_iota(jnp.int32, (8 * SUB, S5_P), 0)
        seg, r = rr // SUB, rr % SUB
        step = jnp.where((seg == 1) | (seg == 4), 1, jnp.where((seg == 2) | (seg == 5), 2, 4))
        n = jnp.where(seg == 0, r + 1, jnp.where(seg == 7, SUB - r, step))
        fwd_gap = jnp.where(seg <= 3, r - step, SUB - step - 1 - r)
        gap = jnp.where((seg == 0) | (seg == 7), 0, fwd_gap)
        nf = n.astype(F32)
        mag = jnp.where(gap >= 0, jnp.exp(nf * ar[...]), 0.0)
        tr[...] = mag * jnp.cos(nf * ai[...])
        ti[...] = mag * jnp.sin(nf * ai[...])

    sds = jax.ShapeDtypeStruct((8 * SUB, S5_P), F32)
    return pl.pallas_call(body, name="s5_tables", out_shape=[sds] * 2)(ars, ais)


def _s5_table(tb_r, tb_i, k):
    return tb_r[SUB * k:SUB * (k + 1), :], tb_i[SUB * k:SUB * (k + 1), :]


def _s5_scan(bu_r, bu_i, tb_r, tb_i, c_r, c_i, lb):
    nt = lb // SUB
    sr, si = bu_r.reshape(nt, SUB, S5_P), bu_i.reshape(nt, SUB, S5_P)
    for j, k in enumerate((1, 2, 4)):
        mr, mi = _s5_table(tb_r, tb_i, 1 + j)
        tr, ti = pltpu.roll(sr, k, axis=1), pltpu.roll(si, k, axis=1)
        sr, si = sr + mr * tr - mi * ti, si + mr * ti + mi * tr
    pr, pi = _s5_table(tb_r, tb_i, 0)
    out_r, out_i = [], []
    for j in range(nt):
        a_r = sr[j] + pr * c_r - pi * c_i
        a_i = si[j] + pr * c_i + pi * c_r
        out_r.append(a_r)
        out_i.append(a_i)
        c_r, c_i = a_r[SUB - 1:SUB], a_i[SUB - 1:SUB]
    return jnp.concatenate(out_r, axis=0), jnp.concatenate(out_i, axis=0)


def _s5_rscan(g_r, g_i, tb_r, tb_i, n_r, n_i, lb):
    nt = lb // SUB
    gr, gi = g_r.reshape(nt, SUB, S5_P), g_i.reshape(nt, SUB, S5_P)
    for j, k in enumerate((1, 2, 4)):
        mr, mi = _s5_table(tb_r, tb_i, 4 + j)
        tr, ti = pltpu.roll(gr, SUB - k, axis=1), pltpu.roll(gi, SUB - k, axis=1)
        gr, gi = gr + mr * tr + mi * ti, gi + mr * ti - mi * tr
    qr, qi = _s5_table(tb_r, tb_i, 7)
    out_r, out_i = [None] * nt, [None] * nt
    for j in reversed(range(nt)):
        a_r = gr[j] + qr * n_r + qi * n_i
        a_i = gi[j] + qr * n_i - qi * n_r
        out_r[j], out_i[j] = a_r, a_i
        n_r, n_i = a_r[0:1], a_i[0:1]
    return jnp.concatenate(out_r, axis=0), jnp.concatenate(out_i, axis=0)


def _s5_y(u, sr, si, cre, cim, dsk):
    return _bdot(sr, cre) + _bdot(si, cim) + dsk * u


def _f_s5(proj, bmat, cre, cim, p_r, p_i, dsk, glu_w, glu_b):
    t = proj.shape[0]
    lb = _s5_block(t)
    nb = t // lb

    def body(u_ref, bm_ref, cr_ref, ci_ref, pr_ref, pi_ref, dk_ref, gw_ref, gb_ref, y_ref, car_ref, s_ref, st_ref):
        @pl.when(pl.program_id(0) == 0)
        def _():
            st_ref[...] = jnp.zeros_like(st_ref)

        u = u_ref[...]
        bu = _bdot(u, bm_ref[...])
        c_r, c_i = st_ref[0:1, 0:S5_P], st_ref[0:1, S5_P:]
        car_ref[0] = st_ref[0:1, :]
        sr, si = _s5_scan(bu[:, :S5_P], bu[:, S5_P:], pr_ref, pi_ref, c_r, c_i, lb)
        st_ref[0:1, 0:S5_P] = sr[lb - 1:lb]
        st_ref[0:1, S5_P:] = si[lb - 1:lb]
        sr_b, si_b = sr.astype(BF16), si.astype(BF16)
        s_ref[:, 0:S5_P] = sr_b
        s_ref[:, S5_P:] = si_b
        gel = _gelu(_s5_y(u, sr_b, si_b, cr_ref[...], ci_ref[...], dk_ref[...]))
        y_ref[...] = gel * _sig(_bdot(gel, gw_ref[...]) + gb_ref[...])

    return pl.pallas_call(
        body, name="f_s5", grid=(nb,),
        in_specs=[pl.BlockSpec((lb, GW), lambda i: (i, 5)),
                  _full((GW, 2 * S5_P)), _full((S5_P, GW)), _full((S5_P, GW)), _full((8 * SUB, S5_P)), _full((8 * SUB, S5_P)),
                  _row(GW), _full((GW, GW)), _row(GW)],
        out_specs=[pl.BlockSpec((lb, GW), lambda i: (i, 0)), pl.BlockSpec((1, 1, 2 * S5_P), lambda i: (i, 0, 0)),
                   pl.BlockSpec((lb, 2 * S5_P), lambda i: (i, 0))],
        out_shape=[jax.ShapeDtypeStruct((t, GW), F32), jax.ShapeDtypeStruct((nb, 1, 2 * S5_P), F32),
                   jax.ShapeDtypeStruct((t, 2 * S5_P), BF16)],
        scratch_shapes=[pltpu.VMEM((8, 2 * S5_P), F32)], compiler_params=_cparams(1),
    )(proj, bmat, cre, cim, p_r, p_i, dsk, glu_w, glu_b)


def _b_s5(proj, dyd, carries, states, bmat, cre, cim, p_r, p_i, dsk, glu_w, glu_b):
    t = proj.shape[0]
    lb = _s5_block(t)
    nb = t // lb

    def body(u_ref, dy_ref, car_ref, s_ref, bm_ref, cr_ref, ci_ref, pr_ref, pi_ref, dk_ref, gw_ref, gb_ref,
             du_ref, dbm_ref, dcr_ref, dci_ref, dlam_ref, ddk_ref, dgw_ref, dgb_ref, gc_ref):
        @pl.when(pl.program_id(0) == 0)
        def _():
            gc_ref[...] = jnp.zeros_like(gc_ref)
            for r in (dbm_ref, dcr_ref, dci_ref, dlam_ref, ddk_ref, dgw_ref, dgb_ref):
                r[...] = jnp.zeros_like(r)

        u = u_ref[...]
        bm = bm_ref[...]
        u_b = u.astype(BF16)
        c_r, c_i = car_ref[0, 0:1, 0:S5_P], car_ref[0, 0:1, S5_P:]
        cre_v, cim_v, dk, gw = cr_ref[...], ci_ref[...], dk_ref[...], gw_ref[...]
        sr_b, si_b = s_ref[:, 0:S5_P], s_ref[:, S5_P:]
        sr, si = sr_b.astype(F32), si_b.astype(F32)
        y = _dot(sr_b, cre_v) + _dot(si_b, cim_v) + dk * u
        gel = _gelu(y)
        gel_b = gel.astype(BF16)
        gate = _sig(_dot(gel_b, gw) + gb_ref[...])
        dout = dy_ref[...]
        t1 = dout * gel * gate * (1.0 - gate)
        t1_b = t1.astype(BF16)
        dgw_ref[...] += _dot(gel_b, t1_b, TN)
        dgb_ref[...] += _colsum(t1)
        dyv = (dout * gate + _dot(t1_b, gw, NT)) * _dgelu(y)
        dyv_b = dyv.astype(BF16)
        ddk_ref[...] += _colsum(dyv * u)
        dcr_ref[...] += _dot(sr_b, dyv_b, TN)
        dci_ref[...] += _dot(si_b, dyv_b, TN)
        gr = _dot(dyv_b, cre_v, NT)
        gi = _dot(dyv_b, cim_v, NT)
        row = lax.broadcasted_iota(jnp.int32, (lb, S5_P), 0)
        n_r, n_i = gc_ref[0:1, 0:S5_P], gc_ref[0:1, S5_P:]
        gr, gi = _s5_rscan(gr, gi, pr_ref, pi_ref, n_r, n_i, lb)
        gc_ref[0:1, 0:S5_P] = gr[0:1]
        gc_ref[0:1, S5_P:] = gi[0:1]
        gcat = jnp.concatenate([gr, gi], axis=1).astype(BF16)
        dbm_ref[...] += _dot(u_b, gcat, TN)
        du_ref[...] = dyv * dk + _dot(gcat, bm, NT)
        spr = jnp.where(row >= 1, _roll(sr, 1), c_r)
        spi = jnp.where(row >= 1, _roll(si, 1), c_i)
        dlam_ref[0:1, :] += _colsum(gr * spr + gi * spi)
        dlam_ref[1:2, :] += _colsum(gi * spr - gr * spi)

    rev = lambda i: nb - 1 - i
    return pl.pallas_call(
        body, name="b_s5", grid=(nb,),
        in_specs=[pl.BlockSpec((lb, GW), lambda i: (rev(i), 5)), pl.BlockSpec((lb, GW), lambda i: (rev(i), 0)),
                  pl.BlockSpec((1, 1, 2 * S5_P), lambda i: (rev(i), 0, 0)), pl.BlockSpec((lb, 2 * S5_P), lambda i: (rev(i), 0)),
                  _full((GW, 2 * S5_P)), _full((S5_P, GW)), _full((S5_P, GW)), _full((8 * SUB, S5_P)), _full((8 * SUB, S5_P)),
                  _row(GW), _full((GW, GW)), _row(GW)],
        out_specs=[pl.BlockSpec((lb, GW), lambda i: (rev(i), 0)), _full((GW, 2 * S5_P)), _full((S5_P, GW)), _full((S5_P, GW)),
                   _full((2, S5_P)), _row(GW), _full((GW, GW)), _row(GW)],
        out_shape=[jax.ShapeDtypeStruct((t, GW), F32), jax.ShapeDtypeStruct((GW, 2 * S5_P), F32),
                   jax.ShapeDtypeStruct((S5_P, GW), F32), jax.ShapeDtypeStruct((S5_P, GW), F32),
                   jax.ShapeDtypeStruct((2, S5_P), F32), jax.ShapeDtypeStruct((1, GW), F32),
                   jax.ShapeDtypeStruct((GW, GW), F32), jax.ShapeDtypeStruct((1, GW), F32)],
        scratch_shapes=[pltpu.VMEM((8, 2 * S5_P), F32)], compiler_params=_cparams(1),
    )(proj, dyd, carries, states, bmat, cre, cim, p_r, p_i, dsk, glu_w, glu_b)


def _group_norm(ys, bw):
    outs, stats = [], []
    for g, y in enumerate(ys):
        r, n = _rms(y)
        stats.append((r, n))
        outs.append(n * bw[:, GW * g:GW * (g + 1)])
    return jnp.concatenate(outs, axis=1), stats


def _f_out(ya, yb, yc, yd, bw, w_out, h, g1):
    t = h.shape[0]
    tb = _tblock(t)

    def body(a_ref, b_ref, c_ref, d_ref, bw_ref, w_ref, h_ref, g_ref, h2_ref, o_ref, cat_ref):
        cat, _ = _group_norm([a_ref[...], b_ref[...], c_ref[...], d_ref[...]], bw_ref[...])
        catb = cat.astype(BF16)
        cat_ref[...] = catb
        o = _dot(catb, w_ref[...])
        o_ref[...] = o
        h2_ref[...] = h_ref[...] + g_ref[...] * o

    yblk = pl.BlockSpec((tb, GW), lambda i: (i, 0))
    blk = pl.BlockSpec((tb, D), lambda i: (i, 0))
    return pl.pallas_call(
        body, name="f_out", grid=(t // tb,), in_specs=[yblk] * 4 + [_row(D), _full((D, D)), blk, _row(D)],
        out_specs=[blk, blk, blk],
        out_shape=[jax.ShapeDtypeStruct((t, D), F32), jax.ShapeDtypeStruct((t, D), F32), jax.ShapeDtypeStruct((t, D), BF16)],
        compiler_params=_cparams(1),
    )(ya, yb, yc, yd, bw, w_out, h, g1)


def _b_out(dh2, ya, yb, yc, yd, bw, w_out, g1):
    t = dh2.shape[0]
    tb = _tblock(t)

    def body(dh_ref, a_ref, b_ref, c_ref, d_ref, bw_ref, w_ref, g_ref, da_ref, db_ref, dc_ref, dd_ref, do_ref, dbw_ref):
        @pl.when(pl.program_id(0) == 0)
        def _():
            dbw_ref[...] = jnp.zeros_like(dbw_ref)

        do = (dh_ref[...] * g_ref[...]).astype(BF16)
        do_ref[...] = do
        dcat = _dot(do, w_ref[...], NT)
        bw_v = bw_ref[...]
        for g, (y_ref, dy_ref) in enumerate(((a_ref, da_ref), (b_ref, db_ref), (c_ref, dc_ref), (d_ref, dd_ref))):
            r, n = _rms(y_ref[...])
            dc = dcat[:, GW * g:GW * (g + 1)]
            dbw_ref[:, GW * g:GW * (g + 1)] += _colsum(dc * n)
            dy_ref[...] = _rms_bwd(r, n, dc * bw_v[:, GW * g:GW * (g + 1)])

    yblk = pl.BlockSpec((tb, GW), lambda i: (i, 0))
    blk = pl.BlockSpec((tb, D), lambda i: (i, 0))
    ysd = jax.ShapeDtypeStruct((t, GW), F32)
    return pl.pallas_call(
        body, name="b_out", grid=(t // tb,), in_specs=[blk] + [yblk] * 4 + [_row(D), _full((D, D)), _row(D)],
        out_specs=[yblk] * 4 + [blk, _row(D)],
        out_shape=[ysd] * 4 + [jax.ShapeDtypeStruct((t, D), BF16), jax.ShapeDtypeStruct((1, D), F32)],
        compiler_params=_cparams(1),
    )(dh2, ya, yb, yc, yd, bw, w_out, g1)


HB = 512
MLP_ROWS = 1024


def _f_mlp(h2, nw, sc, sh, g2, w1, w2):
    t = h2.shape[0]
    tb = _tblock(t, MLP_ROWS)
    nk = HID // HB

    def body(h_ref, nw_ref, sc_ref, sh_ref, g_ref, w1_ref, w2_ref, h3_ref, m_ref, a_ref, v_ref):
        k = pl.program_id(1)

        @pl.when(k == 0)
        def _():
            _, n = _rms(h_ref[...])
            v_ref[...] = ((n * nw_ref[...]) * (1.0 + sc_ref[...]) + sh_ref[...]).astype(BF16)
            m_ref[...] = jnp.zeros_like(m_ref)

        a = _dot(v_ref[...], w1_ref[...])
        a_ref[...] = a.astype(BF16)
        ra = jnp.maximum(a, 0.0)
        m_ref[...] += _dot((ra * ra).astype(BF16), w2_ref[...])

        @pl.when(k == nk - 1)
        def _():
            h3_ref[...] = h_ref[...] + g_ref[...] * m_ref[...]

    blk = pl.BlockSpec((tb, D), lambda i, k: (i, 0))
    return pl.pallas_call(
        body, name="f_mlp", grid=(t // tb, nk),
        in_specs=[blk, _row(D), _row(D), _row(D), _row(D), pl.BlockSpec((D, HB), lambda i, k: (0, k)),
                  pl.BlockSpec((HB, D), lambda i, k: (k, 0))],
        out_specs=[blk, blk, pl.BlockSpec((tb, HB), lambda i, k: (i, k)), blk],
        out_shape=[jax.ShapeDtypeStruct((t, D), F32), jax.ShapeDtypeStruct((t, D), F32), jax.ShapeDtypeStruct((t, HID), BF16),
                   jax.ShapeDtypeStruct((t, D), BF16)],
        compiler_params=_cparams(2),
    )(h2, nw, sc, sh, g2, w1, w2)


def _b_mlp(dh3, a, g2, w1, w2):
    t = dh3.shape[0]
    tb = _tblock(t, MLP_ROWS)
    nk = HID // HB

    def body(dh_ref, a_ref, g_ref, w1_ref, w2_ref, dv_ref, da_ref, act_ref, dm_ref):
        k = pl.program_id(1)
        dm = (dh_ref[...] * g_ref[...]).astype(BF16)

        @pl.when(k == 0)
        def _():
            dm_ref[...] = dm
            dv_ref[...] = jnp.zeros_like(dv_ref)

        ra = jnp.maximum(a_ref[...].astype(F32), 0.0)
        act_ref[...] = (ra * ra).astype(BF16)
        da = (_dot(dm, w2_ref[...], NT) * (2.0 * ra)).astype(BF16)
        da_ref[...] = da
        dv_ref[...] += _dot(da, w1_ref[...], NT)

    blk = pl.BlockSpec((tb, D), lambda i, k: (i, 0))
    hblk = pl.BlockSpec((tb, HB), lambda i, k: (i, k))
    return pl.pallas_call(
        body, name="b_mlp", grid=(t // tb, nk),
        in_specs=[blk, hblk, _row(D), pl.BlockSpec((D, HB), lambda i, k: (0, k)), pl.BlockSpec((HB, D), lambda i, k: (k, 0))],
        out_specs=[blk, hblk, hblk, blk],
        out_shape=[jax.ShapeDtypeStruct((t, D), F32), jax.ShapeDtypeStruct((t, HID), BF16), jax.ShapeDtypeStruct((t, HID), BF16),
                   jax.ShapeDtypeStruct((t, D), BF16)],
        compiler_params=_cparams(2),
    )(dh3, a, g2, w1, w2)


def _b_final(h, tgt, fw):
    t = h.shape[0]
    tb = _tblock(t)

    def body(h_ref, t_ref, w_ref, dh_ref, loss_ref, dfw_ref):
        @pl.when(pl.program_id(0) == 0)
        def _():
            loss_ref[...] = jnp.zeros_like(loss_ref)
            dfw_ref[...] = jnp.zeros_like(dfw_ref)

        r, n = _rms(h_ref[...])
        wv = w_ref[...]
        err = n * wv - t_ref[...]
        loss_ref[...] += jnp.sum(err * err, keepdims=True) * (0.5 / D)
        dy = err * (1.0 / D)
        dfw_ref[...] += _colsum(dy * n)
        dh_ref[...] = _rms_bwd(r, n, dy * wv)

    blk = pl.BlockSpec((tb, D), lambda i: (i, 0))
    return pl.pallas_call(
        body, name="b_final", grid=(t // tb,), in_specs=[blk, blk, _row(D)], out_specs=[blk, _row(1), _row(D)],
        out_shape=[jax.ShapeDtypeStruct((t, D), F32), jax.ShapeDtypeStruct((1, 1), F32), jax.ShapeDtypeStruct((1, D), F32)],
        compiler_params=_cparams(1),
    )(h, tgt, fw)


_EYE16 = None


def _eye(n):
    return jnp.eye(n, dtype=F32)


def _pool_embed(pool_w):
    return jnp.einsum('gcd,gk->gckd', pool_w, _eye(4)).reshape(GW, GW)


def _pool_extract(m):
    return jnp.einsum('gcgd->gcd', m.reshape(4, 64, 4, 64))


def _bmat_embed(bb):
    return jnp.einsum('gph,gk->ghkp', bb, _eye(16)).reshape(GW, S5_P)


def _bmat_extract(m):
    return jnp.einsum('ghgp->gph', m.reshape(16, 16, 16, 64))


def _cmat_embed(cc):
    return jnp.einsum('ghp,gk->kpgh', cc, _eye(16)).reshape(S5_P, GW)


def _cmat_extract(m):
    return jnp.einsum('gpgh->ghp', m.reshape(16, 64, 16, 16))


def _pad_lanes(v, n=DTW):
    return jnp.pad(v.reshape(1, -1), ((0, 0), (0, n - v.shape[-1])))


def _w_in_layout(w_in_t):
    w_main = jnp.concatenate([w_in_t[:1280], w_in_t[2052:2308], w_in_t[1280:2048]], axis=0)
    return w_main, jnp.pad(w_in_t[2048:2052], ((0, DTW - 4), (0, 0)))


def _layer_params(p, l, mod, w_in, rest):
    q = {'rest': rest}
    q['mod'] = [mod[k:k + 1] for k in range(6)]
    q['nw1'] = p['norm_mix_w'][l:l + 1]
    q['nw2'] = p['norm_mlp_w'][l:l + 1]
    q['w_main'], q['w_dt'] = _w_in_layout(w_in)
    q['pool_mat'] = _pool_embed(p['pool_w'][l]).astype(BF16)
    q['pool_scale'] = p['pool_scale'][l:l + 1]
    q['sconv_w'] = p['sconv_w'][l]
    q['conv_w'] = p['ssd_conv_w'][l]
    q['conv_b'] = p['ssd_conv_b'][l:l + 1]
    q['dt_bias'] = _pad_lanes(p['ssd_dt_bias'][l])
    q['a_log'] = _pad_lanes(p['ssd_a_log'][l])
    q['ssd_d'] = _pad_lanes(p['ssd_d'][l])
    q['s5_raw'] = (p['s5_a_re'][l], p['s5_a_im'][l], p['s5_log_step'][l].reshape(16, 1),
                   p['s5_b_re'][l].reshape(16, 1024), p['s5_b_im'][l].reshape(16, 1024))
    q['cre'] = _cmat_embed(p['s5_c_re'][l]).astype(BF16)
    q['cim'] = (-_cmat_embed(p['s5_c_im'][l])).astype(BF16)
    q['s5_d'] = p['s5_d'][l:l + 1]
    q['glu_w'] = p['s5_glu_w'][l].astype(BF16)
    q['glu_b'] = p['s5_glu_b'][l:l + 1]
    q['bw'] = p['branch_norm_w'][l:l + 1]
    return q


def _layer_fwd(h, q):
    sh1, sc1, g1, sh2, sc2, g2 = q['mod']
    t = h.shape[0]
    s = {'h': h}
    s['proj'], s['dtp'], s['u'] = _f_in(h, q['nw1'], sc1, sh1, q['w_main'], q['w_dt'])
    s['ya'], s['yb'] = _f_ab(s['proj'], q['pool_mat'], q['pool_scale'], q['sconv_w'])
    s['yc'], s['ypre'], s['sprev'] = _f_ssd(s['proj'], s['dtp'], q['conv_w'], q['conv_b'], q['dt_bias'], q['a_log'], q['ssd_d'])
    lr, li, bbr, bbi, ars, ais = _s5_prep(*q['s5_raw'])
    s['bmat'] = jnp.concatenate([_bmat_embed(bbr.reshape(16, 64, 16)), _bmat_embed(bbi.reshape(16, 64, 16))],
                                axis=1).astype(BF16)
    s['tables'] = _s5_tables(ars.reshape(1, S5_P), ais.reshape(1, S5_P))
    s['yd'], s['carries'], s['states'] = _f_s5(s['proj'], s['bmat'], q['cre'], q['cim'], s['tables'][0], s['tables'][1],
                                  q['s5_d'], q['glu_w'], q['glu_b'])
    q['w_out'], q['w1'], q['w2'] = q['rest']((s['ya'], s['yc'], s['yd']))
    s['h2'], s['o'], s['cat'] = _f_out(s['ya'], s['yb'], s['yc'], s['yd'], q['bw'], q['w_out'], h, g1)
    h3, s['m'], s['a'], s['v'] = _f_mlp(s['h2'], q['nw2'], sc2, sh2, g2, q['w1'], q['w2'])
    return h3, s


def _layer_bwd(dh3, q, s):
    sh1, sc1, g1, sh2, sc2, g2 = q['mod']
    g = {}
    dv, da, act, dm = _b_mlp(dh3, s['a'], g2, q['w1'], q['w2'])
    g['mlp_w1'] = _tn_matmul(s['v'], da, "dw1", col_major=True)
    g['mlp_w2'] = _tn_matmul(act, dm, "dw2")
    dh2, dsc2, dsh2, dnw2, dg2 = _b_normmod(dv, s['h2'], dh3, s['m'], q['nw2'], sc2, "b_norm_mlp")
    dya, dyb, dyc, dyd, do, dbw = _b_out(dh2, s['ya'], s['yb'], s['yc'], s['yd'], q['bw'], q['w_out'], g1)
    g['w_out'] = _tn_matmul(s['cat'], do, "dwout")
    g['branch_norm_w'] = dbw[0]
    dab, dpm, dps, dsw = _b_ab(s['proj'], dya, dyb, q['pool_mat'], q['pool_scale'], q['sconv_w'])
    g['pool_w'] = _pool_extract(dpm)
    g['pool_scale'] = dps[0]
    g['sconv_w'] = dsw
    dz, dxbc, ddt, dcw, dcb, ddtb, dal, ddk = _b_ssd(s['proj'], s['dtp'], s['ypre'], dyc, s['sprev'], q['conv_w'],
                                                     q['conv_b'], q['dt_bias'], q['a_log'], q['ssd_d'])
    g['ssd_conv_w'] = dcw
    g['ssd_conv_b'] = dcb[0]
    g['ssd_dt_bias'] = ddtb[0, :4]
    g['ssd_a_log'] = dal[0, :4]
    g['ssd_d'] = ddk[0, :4]
    tb = s['tables']
    ds5, dbmat, dcre, dcim, dlam, dd5, dgw, dgb = _b_s5(s['proj'], dyd, s['carries'], s['states'], s['bmat'], q['cre'], q['cim'],
                                                        tb[0], tb[1], q['s5_d'], q['glu_w'], q['glu_b'])
    g['s5_c_re'] = _cmat_extract(dcre)
    g['s5_c_im'] = -_cmat_extract(dcim)
    g['s5_d'] = dd5[0]
    g['s5_glu_w'] = dgw
    g['s5_glu_b'] = dgb[0]
    dbbr = _bmat_extract(dbmat[:, :S5_P]).reshape(16, 1024)
    dbbi = _bmat_extract(dbmat[:, S5_P:]).reshape(16, 1024)
    dar, dai, dls, dbr, dbi = _s5_prep_bwd(*q['s5_raw'], dlam[0].reshape(16, 64), dlam[1].reshape(16, 64), dbbr, dbbi)
    g['s5_a_re'], g['s5_a_im'], g['s5_log_step'] = dar, dai, dls[:, 0]
    g['s5_b_re'], g['s5_b_im'] = dbr, dbi
    du = _b_in_du(dab, dz, dxbc, ds5, ddt, q['w_main'], q['w_dt'])
    u = s['u']
    head = jnp.concatenate([_tn_matmul(dab, u, "dwin_ab"), _tn_matmul(dz, u, "dwin_z"), _tn_matmul(dxbc, u, "dwin_xbc"),
                            _tn_matmul(ddt, u, "dwin_dt")[:8]], axis=0)
    full = lax.dynamic_update_slice(jnp.zeros((2308, D), F32), head, (0, 0))
    g['w_in'] = lax.dynamic_update_slice(full, _tn_matmul(ds5, u, "dwin_s5"), (2052, 0))
    dh, dsc1, dsh1, dnw1, dg1 = _b_normmod(du, s['h'], dh2, s['o'], q['nw1'], sc1, "b_norm_mix")
    g['norm_mix_w'] = dnw1[0]
    g['norm_mlp_w'] = dnw2[0]
    dmod = jnp.concatenate([dsh1, dsc1, dg1, dsh2, dsc2, dg2], axis=1)
    return dh, g, dmod


def _local_step(x, tgt, p, mod, w_in_of, rest_of):
    h = x
    qs, saved = [], []
    for l in range(2):
        qs.append(_layer_params(p, l, mod[l], w_in_of(l), functools.partial(rest_of, l)))
        h, s = _layer_fwd(h, qs[l])
        saved.append(s)
    dh, loss, dfw = _b_final(h, tgt, p['final_norm_w'].reshape(1, D))
    grads = [None, None]
    dmods = [None, None]
    for l in (1, 0):
        dh, grads[l], dmods[l] = _layer_bwd(dh, qs[l], saved[l])
    out = {k: jnp.stack([grads[0][k], grads[1][k]]) for k in grads[0]}
    out['final_norm_w'] = dfw[0]
    return loss, dh, out, jnp.concatenate(dmods, axis=0)


def _pack(arrs):
    parts, rows = [], 0
    for a in arrs:
        f = a.reshape(-1).astype(F32)
        pad = (-f.shape[0]) % 1024
        f = jnp.pad(f, (0, pad)) if pad else f
        parts.append(f.reshape(-1, 128))
        rows += parts[-1].shape[0]
    if rows % 256:
        parts.append(jnp.zeros((256 - rows % 256, 128), F32))
    return jnp.concatenate(parts, axis=0)


def _unpack(buf, shapes):
    out, row = [], 0
    for shp in shapes:
        n = int(math.prod(shp)) if len(shp) else 1
        rows = (n + 1023) // 1024 * 8
        out.append(buf[row:row + rows].reshape(-1)[:n].reshape(shp))
        row += rows
    return out


def _shard_of(a, axis, k):
    n = a.shape[axis] // 4
    return lax.dynamic_slice_in_dim(a, k * n, n, axis)


def kernel(x, c, norm_mix_w, norm_mlp_w, ada_w, ada_b, w_in, pool_w, pool_scale, sconv_w, ssd_conv_w, ssd_conv_b, ssd_dt_bias, ssd_a_log, ssd_d, s5_a_re, s5_a_im, s5_log_step, s5_b_re, s5_b_im, s5_c_re, s5_c_im, s5_d, s5_glu_w, s5_glu_b, branch_norm_w, w_out, mlp_w1, mlp_w2, final_norm_w, loss_target, m_norm_mix_w, m_norm_mlp_w, m_ada_w, m_ada_b, m_w_in, m_pool_w, m_pool_scale, m_sconv_w, m_ssd_conv_w, m_ssd_conv_b, m_ssd_dt_bias, m_ssd_a_log, m_ssd_d, m_s5_a_re, m_s5_a_im, m_s5_log_step, m_s5_b_re, m_s5_b_im, m_s5_c_re, m_s5_c_im, m_s5_d, m_s5_glu_w, m_s5_glu_b, m_branch_norm_w, m_w_out, m_mlp_w1, m_mlp_w2, m_final_norm_w, v_norm_mix_w, v_norm_mlp_w, v_ada_w, v_ada_b, v_w_in, v_pool_w, v_pool_scale, v_sconv_w, v_ssd_conv_w, v_ssd_conv_b, v_ssd_dt_bias, v_ssd_a_log, v_ssd_d, v_s5_a_re, v_s5_a_im, v_s5_log_step, v_s5_b_re, v_s5_b_im, v_s5_c_re, v_s5_c_im, v_s5_d, v_s5_glu_w, v_s5_glu_b, v_branch_norm_w, v_w_out, v_mlp_w1, v_mlp_w2, v_final_norm_w):
    loc = locals()
    w = {n: loc[n] for n in WEIGHTS}
    mom = {n: loc['m_' + n] for n in WEIGHTS}
    var = {n: loc['v_' + n] for n in WEIGHTS}
    ix, iy, ic = lax.axis_index("x"), lax.axis_index("y"), lax.axis_index("c")
    chip = 2 * ix + iy
    dev = 4 * ix + 2 * iy + ic

    mine_of = lambda a: lax.dynamic_index_in_dim(a.astype(BF16), ic, axis=0, keepdims=False)
    pad_in = lambda a: jnp.pad(a.T, ((0, WIN_ROWS - 577), (0, 0)))
    shard = jnp.concatenate([pad_in(mine_of(w['w_in'])), mine_of(w['w_out']), mine_of(w['mlp_w1']), mine_of(w['mlp_w2'])], axis=0)

    (c_all,) = _exchange([c], EVERYONE, False, "ag_cond", stage=True)
    c_all = c_all.reshape(8, D)
    small_sh = _exchange([w[n] for n in SMALL_SHARDED], CHIPS, False, "ag_small")
    (w_in0,) = _exchange([pad_in(w['w_in'][0].astype(BF16))], CHIPS, False, "ag_win0")
    p = {n: w[n] for n in WEIGHTS if n not in BIG}
    for n, g in zip(SMALL_SHARDED, small_sh):
        ax = SMALL_SHARDED[n]
        p[n] = jnp.concatenate([g[k] for k in range(4)], axis=ax)

    def w_in_full(sh):
        return sh[:, :577].reshape(4 * 577, D)

    big = {}

    def fetch(after):
        if not big:
            got = _gather_wait(sems, shard_thru, land, after, "ag_big_wait")
            got = lax.dynamic_update_slice(got, shard[None], (chip, 0, 0))
            other = _pair_swap([got.reshape(-1, D)], False, "swap_big")[0].reshape(got.shape)
            big['both'] = [jnp.where(ic == l, got, other) for l in range(2)]
        return big['both']

    def w_in_of(l):
        return w_in_full(w_in0) if l == 0 else w_in_full(fetch(None)[1])

    def rest_of(l, after):
        blk = fetch(after)[l]
        r0 = WIN_ROWS
        w_out_l = blk[:, r0:r0 + 256].reshape(D, D)
        w1_l = jnp.concatenate([blk[k, r0 + 256:r0 + 1280] for k in range(4)], axis=1)
        w2_l = blk[:, r0 + 1280:r0 + 2304].reshape(HID, D)
        return w_out_l, w1_l, w2_l

    ada_b_sh = _shard_of(w['ada_b'], 1, chip).reshape(2, 1, 6 * D // 4)
    mod_sh = _ada_fwd(c_all, w['ada_w'], ada_b_sh)
    (mod_all,) = _exchange([mod_sh], CHIPS, False, "ag_mod", stage=True)
    mine = lax.dynamic_index_in_dim(mod_all, dev, axis=2, keepdims=False)
    sems, shard_thru, land, token = _gather_start(shard, [mod_all, w_in0] + small_sh, "ag_big_start")
    mod = jnp.transpose(mine, (1, 0, 2)).reshape(2, 6, D) + token[0, 0]

    loss, grad_x, g, dmod = _local_step(x[0], loss_target[0], p, mod, w_in_of, rest_of)

    (dmod_all,) = _exchange([dmod], EVERYONE, False, "ag_dmod", stage=True)
    dmod_all = jnp.transpose(dmod_all, (1, 0, 2))
    g_ada_w, g_ada_b = _ada_bwd(c_all, _shard_of(dmod_all, 2, chip), dmod_all)

    gw_in = jnp.pad(g['w_in'].reshape(2, 4, 577, D), ((0, 0), (0, 0), (0, WIN_ROWS - 577), (0, 0)))
    gw_out = g['w_out'].reshape(2, 4, 256, D)
    gw1 = g['mlp_w1']
    gw2 = g['mlp_w2'].reshape(2, 4, 1024, D)
    gws = [gw_in, gw_out, gw1, gw2]
    got = _pair_swap([a.reshape(2, -1, D) for a in gws], True, "swap_grad")
    layer = ic.astype(jnp.int32).reshape(1)
    pair = [_pair_sum(a, b.reshape(a.shape[1:]), layer, "pair_sum%d" % k, BF16) for k, (a, b) in enumerate(zip(gws, got))]
    quad = _exchange(pair, CHIPS, True, "rs_chips")
    quad = [_sum_lead(a, "rs_chip_sum%d" % k, F32) for k, a in enumerate(quad)]
    other = _pair_swap(quad, False, "swap_red")
    both = [jnp.stack([jnp.where(ic == l, a, b) for l in range(2)]) for a, b in zip(quad, other)]
    both[0] = jnp.transpose(both[0][:, :577], (0, 2, 1))
    red = dict(zip(('w_in', 'w_out', 'mlp_w1', 'mlp_w2'), both))
    red['ada_w'] = g_ada_w

    small_names = [n for n in WEIGHTS if n not in BIG and n != 'ada_b']
    pair_parts = _exchange([g[n] for n in small_names] + [loss], SIBLING, False, "ag_smallpair", stage=True)
    chip_parts = _exchange(_sum_many(pair_parts, "smallpair_sum"), CHIPS, False, "ag_smallgrad", stage=True)
    summed = _sum_many(chip_parts, "smallgrad_sum")
    for n, a in zip(small_names, summed[:-1]):
        a = a.reshape(w[n].shape) if n in ('s5_b_re', 's5_b_im') else a
        red[n] = _shard_of(a, SMALL_SHARDED[n], chip) if n in SMALL_SHARDED else a
    red['ada_b'] = g_ada_b
    loss_out = summed[-1].reshape(())

    delta, new_m, new_v = {}, {}, {}
    for n in BIG:
        delta[n], new_m[n], new_v[n] = _adamw(w[n], red[n], mom[n], var[n], "adamw_" + n)
    rest = [n for n in WEIGHTS if n not in BIG]
    outs = _adamw_many([w[n] for n in rest], [red[n] for n in rest], [mom[n] for n in rest], [var[n] for n in rest],
                       "adamw_small")
    for k, n in enumerate(rest):
        delta[n], new_m[n], new_v[n] = outs[3 * k], outs[3 * k + 1], outs[3 * k + 2]

    return (loss_out, grad_x[None], *[red[n] for n in WEIGHTS], *[delta[n] for n in WEIGHTS],
            *[new_m[n] for n in WEIGHTS], *[new_v[n] for n in WEIGHTS])
```

```python
import functools
import math

import jax
import jax.numpy as jnp
from jax import lax
from jax.experimental import pallas as pl
from jax.experimental.pallas import tpu as pltpu

F32 = jnp.float32
BF16 = jnp.bfloat16
HI = lax.Precision.HIGHEST

D = 1024
GW = 256
HID = 4096
EPS = 1e-6
PW = 2304
DTW = 128
SSD_L = 128
SSD_SUB = 2
SSD_SUB_BWD = 1
NH, HP, NS = 4, 64, 128
S5_P = 1024
MESH = pl.DeviceIdType.MESH

ADAM_LR, ADAM_B1, ADAM_B2, ADAM_EPS, ADAM_WD, ADAM_STEP = 0.001, 0.9, 0.999, 1e-08, 0.01, 10

NT = (((1,), (1,)), ((), ()))
TN = (((0,), (0,)), ((), ()))

WEIGHTS = ['norm_mix_w', 'norm_mlp_w', 'ada_w', 'ada_b', 'w_in', 'pool_w', 'pool_scale', 'sconv_w', 'ssd_conv_w',
           'ssd_conv_b', 'ssd_dt_bias', 'ssd_a_log', 'ssd_d', 's5_a_re', 's5_a_im', 's5_log_step', 's5_b_re', 's5_b_im',
           's5_c_re', 's5_c_im', 's5_d', 's5_glu_w', 's5_glu_b', 'branch_norm_w', 'w_out', 'mlp_w1', 'mlp_w2',
           'final_norm_w']
BIG = ('ada_w', 'w_in', 'w_out', 'mlp_w1', 'mlp_w2')
SMALL_SHARDED = {'sconv_w': 2, 'ssd_conv_w': 2, 's5_glu_w': 1}


def _cparams(n_axes, vmem_mb=48):
    return pltpu.CompilerParams(dimension_semantics=("arbitrary",) * n_axes, vmem_limit_bytes=vmem_mb * 1024 * 1024)


def _row(n):
    return pl.BlockSpec((1, n), lambda *_: (0, 0))


def _full(shape):
    nd = len(shape)
    return pl.BlockSpec(tuple(shape), lambda *_: (0,) * nd)


def _dot(a, b, dims=None, prec=None):
    if dims is None:
        dims = (((a.ndim - 1,), (0,)), ((), ()))
    return lax.dot_general(a, b, dims, preferred_element_type=F32, precision=prec)


def _bdot(a, b, dims=None):
    return _dot(a.astype(BF16), b.astype(BF16), dims)


def _sig(x):
    return jax.nn.sigmoid(x)


def _silu(x):
    return x * _sig(x)


def _dsilu(x):
    s = _sig(x)
    return s * (1.0 + x * (1.0 - s))


def _softplus(x):
    return jnp.maximum(x, 0.0) + jnp.log(1.0 + jnp.exp(-jnp.abs(x)))


_GK = math.sqrt(2.0 / math.pi)


def _gelu(x):
    return 0.5 * x * (1.0 + jnp.tanh(_GK * (x + 0.044715 * x * x * x)))


def _dgelu(x):
    th = jnp.tanh(_GK * (x + 0.044715 * x * x * x))
    return 0.5 * (1.0 + th) + 0.5 * x * (1.0 - th * th) * _GK * (1.0 + 3.0 * 0.044715 * x * x)


def _colsum(x):
    return jnp.sum(x, axis=0, keepdims=True)


def _rms(x):
    r = lax.rsqrt(jnp.mean(x * x, axis=-1, keepdims=True) + EPS)
    return r, x * r


def _rms_bwd(r, n, dn):
    return r * (dn - n * jnp.mean(dn * n, axis=-1, keepdims=True))


def _roll(x, k):
    n = x.shape[0]
    k = k % n
    return x if k == 0 else pltpu.roll(x, k, axis=0)


def _tblock(t, want=512):
    return min(t, want)


def _peer(mask):
    x, y, c = lax.axis_index("x"), lax.axis_index("y"), lax.axis_index("c")
    return (x ^ ((mask >> 2) & 1), y ^ ((mask >> 1) & 1), c ^ (mask & 1))


def _group_index(masks):
    x, y, c = lax.axis_index("x"), lax.axis_index("y"), lax.axis_index("c")
    full = 0
    for m in masks:
        full |= m
    bits = [b for b in (4, 2, 1) if full & b]

    def idx(px, py, pc):
        v = {4: px, 2: py, 1: pc}
        out = 0
        for b in bits:
            out = out * 2 + v[b]
        return out

    return idx(x, y, c), [idx(*_peer(m)) for m in masks]


def _exchange(arrs, masks, scatter, name, stage=False):
    n_arr, n_peer, n_grp = len(arrs), len(masks), len(masks) + 1

    def body(*refs):
        ins, outs = refs[:n_arr], refs[n_arr:2 * n_arr]
        send_sems, recv_sems, local_sems = refs[2 * n_arr:]
        me, peer_idx = _group_index(masks)
        copies = []
        for t in range(n_arr):
            src_me = ins[t].at[me] if scatter else ins[t]
            loc = pltpu.make_async_copy(src_me, outs[t].at[me], local_sems.at[t])
            loc.start()
            copies.append(loc)
            for j, m in enumerate(masks):
                src = ins[t].at[peer_idx[j]] if scatter else ins[t]
                cp = pltpu.make_async_remote_copy(src_ref=src, dst_ref=outs[t].at[me], send_sem=send_sems.at[t, j],
                                                  recv_sem=recv_sems.at[t, j], device_id=_peer(m), device_id_type=MESH)
                cp.start()
                copies.append(cp)
        for cp in copies:
            cp.wait()

    hbm = pl.BlockSpec(memory_space=pl.ANY)
    out_shape = [jax.ShapeDtypeStruct((n_grp,) + (a.shape[1:] if scatter else a.shape), a.dtype) for a in arrs]
    src_spec = pl.BlockSpec(memory_space=pltpu.VMEM) if stage else hbm
    outs = pl.pallas_call(
        body, name=name, in_specs=[src_spec] * n_arr, out_specs=[hbm] * n_arr, out_shape=out_shape,
        scratch_shapes=[pltpu.SemaphoreType.DMA((n_arr, n_peer)), pltpu.SemaphoreType.DMA((n_arr, n_peer)),
                        pltpu.SemaphoreType.DMA((n_arr,))],
    )(*arrs)
    return list(outs)


def _split_copies(src_refs, land_refs, sems, scatter):
    me, peer_idx = _group_index(CHIPS)
    n = len(CHIPS) * len(src_refs)
    copies = []
    for t, (src_ref, land_ref) in enumerate(zip(src_refs, land_refs)):
        for j, m in enumerate(CHIPS):
            k = len(CHIPS) * t + j
            copies.append(pltpu.make_async_remote_copy(
                src_ref=src_ref.at[peer_idx[j]] if scatter else src_ref, dst_ref=land_ref.at[me], send_sem=sems[k],
                recv_sem=sems[n + k], device_id=_peer(m), device_id_type=MESH))
    return copies


def _split_start(srcs, after, scatter, name):
    n_arr, n_sem = len(srcs), 2 * len(CHIPS) * len(srcs)

    def body(*refs):
        src_refs, land_refs = refs[:n_arr], refs[n_arr:2 * n_arr]
        outs = refs[2 * n_arr + len(after):]
        for cp in _split_copies(src_refs, land_refs, outs[:n_sem], scatter):
            cp.start()
        outs[-1][...] = jnp.zeros_like(outs[-1])

    hbm = pl.BlockSpec(memory_space=pltpu.HBM)
    sem = pl.BlockSpec(memory_space=pltpu.SEMAPHORE)
    lands = [lax.empty((len(CHIPS) + 1,) + (a.shape[1:] if scatter else a.shape), a.dtype) for a in srcs]
    as_hbm = lambda a: pltpu.with_memory_space_constraint(a, pltpu.HBM)
    outs = pl.pallas_call(
        body, name=name,
        out_shape=(pltpu.SemaphoreType.DMA(()),) * n_sem + tuple(pltpu.HBM(a.shape, a.dtype) for a in srcs + lands)
        + (jax.ShapeDtypeStruct((8, 128), F32),),
        in_specs=(hbm,) * (2 * n_arr) + (pl.BlockSpec(memory_space=pl.ANY),) * len(after),
        out_specs=(sem,) * n_sem + (hbm,) * (2 * n_arr) + (pl.BlockSpec(memory_space=pltpu.VMEM),),
        input_output_aliases={t: n_sem + t for t in range(2 * n_arr)},
        compiler_params=pltpu.CompilerParams(has_side_effects=pltpu.SideEffectType.DATAFLOW_SIDE_EFFECTING),
    )(*[as_hbm(a) for a in srcs + lands], *after)
    return outs[:n_sem], list(outs[n_sem:n_sem + n_arr]), list(outs[n_sem + n_arr:n_sem + 2 * n_arr]), outs[-1]


def _split_wait(sems, srcs, lands, after, scatter, name):
    n_arr, n_sem = len(srcs), len(sems)

    def body(*refs):
        src_refs, land_refs = refs[:n_arr], refs[n_arr:2 * n_arr]
        for cp in _split_copies(src_refs, land_refs, refs[2 * n_arr:2 * n_arr + n_sem], scatter):
            cp.wait_send()
            cp.wait_recv()

    hbm = pl.BlockSpec(memory_space=pltpu.HBM)
    sem = pl.BlockSpec(memory_space=pltpu.SEMAPHORE)
    outs = pl.pallas_call(
        body, name=name, out_shape=tuple(pltpu.HBM(a.shape, a.dtype) for a in srcs + lands),
        in_specs=(hbm,) * (2 * n_arr) + (sem,) * n_sem + (pl.BlockSpec(memory_space=pl.ANY),) * len(after),
        out_specs=(hbm,) * (2 * n_arr), input_output_aliases={t: t for t in range(2 * n_arr)},
        compiler_params=pltpu.CompilerParams(has_side_effects=pltpu.SideEffectType.DATAFLOW_SIDE_EFFECTING),
    )(*srcs, *lands, *sems, *after)
    return list(outs[:n_arr]), list(outs[n_arr:])


CHIPS = (4, 2, 6)
EVERYONE = (1, 2, 3, 4, 5, 6, 7)
SIBLING = (1,)
SWAP_ROWS = 512
WIN_ROWS = 592


def _pair_swap(arrs, other_layer, name):
    n_arr = len(arrs)
    shapes = [a.shape[-2:] for a in arrs]
    chunks = []
    for t, (rows, _) in enumerate(shapes):
        assert rows % 16 == 0
        for j, r0 in enumerate(range(0, rows, SWAP_ROWS)):
            chunks.append((t, r0, min(SWAP_ROWS, rows - r0), j % 2))

    def body(*refs):
        ins, outs = refs[:n_arr], refs[n_arr:2 * n_arr]
        bufs = refs[2 * n_arr:3 * n_arr]
        load_sems, send_sems, recv_sems = refs[3 * n_arr:]
        sibling = _peer(1)
        c = lax.axis_index("c")

        def load(k):
            t, r0, n, slot = chunks[k]
            src = ins[t].at[1 - c] if other_layer else ins[t]
            return pltpu.make_async_copy(src.at[pl.ds(r0, n)], bufs[t].at[slot, pl.ds(0, n)], load_sems.at[t, slot])

        def send(k):
            t, r0, n, slot = chunks[k]
            return pltpu.make_async_remote_copy(src_ref=bufs[t].at[slot, pl.ds(0, n)], dst_ref=outs[t].at[pl.ds(r0, n)],
                                                send_sem=send_sems.at[t, slot], recv_sem=recv_sems.at[t],
                                                device_id=sibling, device_id_type=MESH)

        in_flight = {}

        def start_load(k):
            key = (chunks[k][0], chunks[k][3])
            if key in in_flight:
                send(in_flight.pop(key)).wait_send()
            load(k).start()

        start_load(0)
        for k in range(len(chunks)):
            load(k).wait()
            if k + 1 < len(chunks):
                start_load(k + 1)
            send(k).start()
            in_flight[(chunks[k][0], chunks[k][3])] = k
        for k in in_flight.values():
            send(k).wait_send()
        for t in range(n_arr):
            pltpu.make_async_remote_copy(src_ref=outs[t], dst_ref=outs[t], send_sem=send_sems.at[t, 0],
                                         recv_sem=recv_sems.at[t], device_id=sibling, device_id_type=MESH).wait_recv()

    hbm = pl.BlockSpec(memory_space=pl.ANY)
    outs = pl.pallas_call(
        body, name=name, in_specs=[hbm] * n_arr, out_specs=[hbm] * n_arr,
        out_shape=[jax.ShapeDtypeStruct(s, a.dtype) for s, a in zip(shapes, arrs)],
        scratch_shapes=[pltpu.VMEM((2, min(SWAP_ROWS, s[0]), s[1]), a.dtype) for s, a in zip(shapes, arrs)]
        + [pltpu.SemaphoreType.DMA((n_arr, 2)), pltpu.SemaphoreType.DMA((n_arr, 2)), pltpu.SemaphoreType.DMA((n_arr,))],
        compiler_params=pltpu.CompilerParams(vmem_limit_bytes=48 * 1024 * 1024),
    )(*arrs)
    return list(outs)


def _sum_lead(a, name, out_dtype):
    n = a.shape[0]
    shape = a.shape[1:]

    def body(a_ref, o_ref):
        acc = a_ref[0].astype(F32)
        for k in range(1, n):
            acc = acc + a_ref[k].astype(F32)
        o_ref[...] = acc.astype(out_dtype)

    if len(shape) == 3:
        blk = (1,) + shape[1:]
        return pl.pallas_call(
            body, name=name, grid=(shape[0],), in_specs=[pl.BlockSpec((n,) + blk, lambda i: (0, i, 0, 0))],
            out_specs=pl.BlockSpec(blk, lambda i: (i, 0, 0)), out_shape=jax.ShapeDtypeStruct(shape, out_dtype),
            compiler_params=_cparams(1),
        )(a)
    rows, cols = shape
    rb = rows
    for cand in (512, 256, 128):
        if rows % cand == 0 and rows > cand:
            rb = cand
            break
    return pl.pallas_call(
        body, name=name, grid=(rows // rb,), in_specs=[pl.BlockSpec((n, rb, cols), lambda i: (0, i, 0))],
        out_specs=pl.BlockSpec((rb, cols), lambda i: (i, 0)), out_shape=jax.ShapeDtypeStruct((rows, cols), out_dtype),
        compiler_params=_cparams(1),
    )(a)


def _pair_sum(g, recv, layer, name, out_dtype):
    _, n, r, c = g.shape

    def body(l_ref, g_ref, r_ref, o_ref):
        o_ref[...] = (g_ref[0].astype(F32) + r_ref[...].astype(F32)).astype(out_dtype)

    return pl.pallas_call(
        body, name=name,
        grid_spec=pltpu.PrefetchScalarGridSpec(
            num_scalar_prefetch=1, grid=(n,),
            in_specs=[pl.BlockSpec((1, 1, r, c), lambda i, l: (l[0], i, 0, 0)), pl.BlockSpec((1, r, c), lambda i, l: (i, 0, 0))],
            out_specs=pl.BlockSpec((1, r, c), lambda i, l: (i, 0, 0))),
        out_shape=jax.ShapeDtypeStruct((n, r, c), out_dtype), compiler_params=_cparams(1),
    )(layer, g, recv)


def _tn_matmul(a, b, name, col_major=False):
    t, k = a.shape
    n = b.shape[1]
    tb = _tblock(t, 1024)
    kb = min(k, 1024)
    nb = min(n, 1024)
    grid = (k // kb, n // nb, t // tb)

    def body(a_ref, b_ref, o_ref):
        @pl.when(pl.program_id(2) == 0)
        def _():
            o_ref[...] = jnp.zeros_like(o_ref)

        acc = _bdot(a_ref[...], b_ref[...], TN)
        if col_major:
            o_ref[0] += acc
        else:
            o_ref[...] += acc

    if col_major:
        out_spec = pl.BlockSpec((1, kb, nb), lambda ki, ni, ti: (ni, ki, 0))
        out_shape = jax.ShapeDtypeStruct((n // nb, k, nb), F32)
    else:
        out_spec = pl.BlockSpec((kb, nb), lambda ki, ni, ti: (ki, ni))
        out_shape = jax.ShapeDtypeStruct((k, n), F32)
    return pl.pallas_call(
        body, name=name, grid=grid,
        in_specs=[pl.BlockSpec((tb, kb), lambda ki, ni, ti: (ti, ki)), pl.BlockSpec((tb, nb), lambda ki, ni, ti: (ti, ni))],
        out_specs=out_spec, out_shape=out_shape, compiler_params=_cparams(3),
    )(a, b)


def _sum_many(arrs, name):
    k = len(arrs)

    def body(*refs):
        for a_ref, o_ref in zip(refs[:k], refs[k:]):
            acc = a_ref[0]
            for j in range(1, a_ref.shape[0]):
                acc = acc + a_ref[j]
            o_ref[...] = acc

    return pl.pallas_call(body, name=name, out_shape=[jax.ShapeDtypeStruct(a.shape[1:], F32) for a in arrs],
                          compiler_params=pltpu.CompilerParams(vmem_limit_bytes=48 * 1024 * 1024))(*arrs)


def _adamw_math(w, g, m, v):
    m2 = ADAM_B1 * m + (1.0 - ADAM_B1) * g
    v2 = ADAM_B2 * v + (1.0 - ADAM_B2) * (g * g)
    m_hat = m2 / (1.0 - ADAM_B1 ** ADAM_STEP)
    v_hat = v2 / (1.0 - ADAM_B2 ** ADAM_STEP)
    return -ADAM_LR * (m_hat / (jnp.sqrt(v_hat) + ADAM_EPS) + ADAM_WD * w), m2, v2


def _adamw_many(ws, gs, ms, vs, name):
    n = len(ws)

    def body(*refs):
        ins, outs = refs[:4 * n], refs[4 * n:]
        for k in range(n):
            res = _adamw_math(ins[k][...], ins[n + k][...], ins[2 * n + k][...], ins[3 * n + k][...])
            for j in range(3):
                outs[3 * k + j][...] = res[j]

    out_shape = []
    for a in ws:
        out_shape += [jax.ShapeDtypeStruct(a.shape, F32)] * 3
    return pl.pallas_call(body, name=name, out_shape=out_shape,
                          compiler_params=pltpu.CompilerParams(vmem_limit_bytes=48 * 1024 * 1024))(*ws, *gs, *ms, *vs)


def _adamw(w, g, m, v, name):
    shape = w.shape
    cols = shape[-1]
    rows = int(math.prod(shape[:-1]))
    rb = rows
    for cand in (256, 128, 64, 32, 16, 8):
        if rows % cand == 0 and rows > cand:
            rb = cand
            break
    bc1 = 1.0 - ADAM_B1 ** ADAM_STEP
    bc2 = 1.0 - ADAM_B2 ** ADAM_STEP

    def body(w_ref, g_ref, m_ref, v_ref, d_ref, nm_ref, nv_ref):
        gg = g_ref[...]
        m2 = ADAM_B1 * m_ref[...] + (1.0 - ADAM_B1) * gg
        v2 = ADAM_B2 * v_ref[...] + (1.0 - ADAM_B2) * (gg * gg)
        m_hat = m2 / bc1
        v_hat = v2 / bc2
        d_ref[...] = -ADAM_LR * (m_hat / (jnp.sqrt(v_hat) + ADAM_EPS) + ADAM_WD * w_ref[...])
        nm_ref[...] = m2
        nv_ref[...] = v2

    spec = pl.BlockSpec((rb, cols), lambda i: (i, 0))
    sds = jax.ShapeDtypeStruct((rows, cols), F32)
    outs = pl.pallas_call(
        body, name=name, grid=(rows // rb,), in_specs=[spec] * 4, out_specs=[spec] * 3, out_shape=[sds] * 3,
        compiler_params=_cparams(1),
    )(*(z.reshape(rows, cols) for z in (w, g, m, v)))
    return tuple(o.reshape(shape) for o in outs)


def _ada_fwd(c_all, ada_w_sh, ada_b_sh):
    s = ada_w_sh.shape[2]
    sb = 512

    def body(c_ref, w_ref, b_ref, o_ref):
        cond = _silu(c_ref[...])
        o_ref[0] = _bdot(cond, w_ref[0]) + b_ref[0]

    return pl.pallas_call(
        body, name="ada_fwd", grid=(2, s // sb),
        in_specs=[_full((8, D)), pl.BlockSpec((1, D, sb), lambda l, j: (l, 0, j)), pl.BlockSpec((1, 1, sb), lambda l, j: (l, 0, j))],
        out_specs=pl.BlockSpec((1, 8, sb), lambda l, j: (l, 0, j)), out_shape=jax.ShapeDtypeStruct((2, 8, s), F32),
        compiler_params=_cparams(2),
    )(c_all, ada_w_sh, ada_b_sh)


def _ada_bwd(c_all, dmod_sh, dmod_all):
    s = dmod_sh.shape[2]
    sb = 512

    def body(c_ref, d_ref, o_ref):
        cond = _silu(c_ref[...])
        o_ref[0] = _bdot(cond, d_ref[0], TN)

    gw = pl.pallas_call(
        body, name="ada_bwd_w", grid=(2, s // sb),
        in_specs=[_full((8, D)), pl.BlockSpec((1, 8, sb), lambda l, j: (l, 0, j))],
        out_specs=pl.BlockSpec((1, D, sb), lambda l, j: (l, 0, j)), out_shape=jax.ShapeDtypeStruct((2, D, s), F32),
        compiler_params=_cparams(2),
    )(c_all, dmod_sh)

    def body_b(d_ref, o_ref):
        acc = d_ref[0, 0:1, :]
        for k in range(1, 8):
            acc = acc + d_ref[0, k:k + 1, :]
        o_ref[0] = acc

    gb = pl.pallas_call(
        body_b, name="ada_bwd_b", grid=(2,), in_specs=[pl.BlockSpec((1, 8, 6 * D), lambda l: (l, 0, 0))],
        out_specs=pl.BlockSpec((1, 1, 6 * D), lambda l: (l, 0, 0)), out_shape=jax.ShapeDtypeStruct((2, 1, 6 * D), F32),
        compiler_params=_cparams(1),
    )(dmod_all)
    return gw, gb.reshape(2, 6 * D)


def _f_in(h, nw, sc, sh, w_main, w_dt):
    t = h.shape[0]
    tb = _tblock(t)

    def body(h_ref, nw_ref, sc_ref, sh_ref, w_ref, wd_ref, p_ref, dt_ref, u_ref):
        _, n = _rms(h_ref[...])
        u = ((n * nw_ref[...]) * (1.0 + sc_ref[...]) + sh_ref[...]).astype(BF16)
        u_ref[...] = u
        p_ref[...] = _dot(u, w_ref[...], NT)
        dt_ref[...] = _dot(u, wd_ref[...], NT)

    return pl.pallas_call(
        body, name="f_in", grid=(t // tb,),
        in_specs=[pl.BlockSpec((tb, D), lambda i: (i, 0)), _row(D), _row(D), _row(D), _full((PW, D)), _full((DTW, D))],
        out_specs=[pl.BlockSpec((tb, PW), lambda i: (i, 0)), pl.BlockSpec((tb, DTW), lambda i: (i, 0)),
                   pl.BlockSpec((tb, D), lambda i: (i, 0))],
        out_shape=[jax.ShapeDtypeStruct((t, PW), F32), jax.ShapeDtypeStruct((t, DTW), F32), jax.ShapeDtypeStruct((t, D), BF16)],
        compiler_params=_cparams(1),
    )(h, nw, sc, sh, w_main, w_dt)


def _b_in_du(dab, dz, dxbc, ds5, ddt, w_main, w_dt):
    t = dab.shape[0]
    tb = _tblock(t)

    def body(a_ref, z_ref, x_ref, s_ref, d_ref, w_ref, wd_ref, o_ref):
        acc = _bdot(a_ref[...], w_ref[0:1024, :])
        acc += _bdot(z_ref[...], w_ref[1024:1280, :])
        acc += _bdot(s_ref[...], w_ref[1280:1536, :])
        acc += _bdot(x_ref[...], w_ref[1536:2304, :])
        acc += _bdot(d_ref[...], wd_ref[...])
        o_ref[...] = acc

    blk = lambda n: pl.BlockSpec((tb, n), lambda i: (i, 0))
    return pl.pallas_call(
        body, name="b_in_du", grid=(t // tb,),
        in_specs=[blk(1024), blk(256), blk(768), blk(256), blk(DTW), _full((PW, D)), _full((DTW, D))],
        out_specs=blk(D), out_shape=jax.ShapeDtypeStruct((t, D), F32), compiler_params=_cparams(1),
    )(dab, dz, dxbc, ds5, ddt, w_main, w_dt)


def _b_normmod(du, x, dres, gated, nw, sc, name):
    t = x.shape[0]
    tb = _tblock(t)

    def body(du_ref, x_ref, dr_ref, g_ref, nw_ref, sc_ref, dx_ref, dsc_ref, dsh_ref, dnw_ref, dg_ref):
        @pl.when(pl.program_id(0) == 0)
        def _():
            for r in (dsc_ref, dsh_ref, dnw_ref, dg_ref):
                r[...] = jnp.zeros_like(r)

        du_v = du_ref[...]
        r, n = _rms(x_ref[...])
        nwv = nw_ref[...]
        scale = 1.0 + sc_ref[...]
        dsc_ref[...] += _colsum(du_v * (n * nwv))
        dsh_ref[...] += _colsum(du_v)
        dnw_ref[...] += _colsum(du_v * scale * n)
        dres_v = dr_ref[...]
        dg_ref[...] += _colsum(dres_v * g_ref[...])
        dx_ref[...] = dres_v + _rms_bwd(r, n, du_v * scale * nwv)

    blk = pl.BlockSpec((tb, D), lambda i: (i, 0))
    row = jax.ShapeDtypeStruct((1, D), F32)
    return pl.pallas_call(
        body, name=name, grid=(t // tb,), in_specs=[blk, blk, blk, blk, _row(D), _row(D)],
        out_specs=[blk, _row(D), _row(D), _row(D), _row(D)], out_shape=[jax.ShapeDtypeStruct((t, D), F32), row, row, row, row],
        compiler_params=_cparams(1),
    )(du, x, dres, gated, nw, sc)


HALO = 16


def _lane_group(shape):
    return lax.broadcasted_iota(jnp.int32, shape, 1) // 64


def _window_select(g, s2, s4, s8, s16):
    return jnp.where(g == 0, s2, jnp.where(g == 1, s4, jnp.where(g == 2, s8, s16)))


def _pool_count(t0, rows):
    g = _lane_group((rows, GW))
    win = _window_select(g, 2, 4, 8, 16)
    tt = t0 + lax.broadcasted_iota(jnp.int32, (rows, GW), 0)
    return jnp.minimum(tt + 1, win).astype(F32)


def _pool_p(v_ext, t0, tb):
    s2 = v_ext + _roll(v_ext, 1)
    s4 = s2 + _roll(s2, 2)
    s8 = s4 + _roll(s4, 4)
    s16 = s8 + _roll(s8, 8)
    ws = _window_select(_lane_group(v_ext.shape), s2, s4, s8, s16)[HALO:]
    return ws / _pool_count(t0, tb) - v_ext[HALO:]


def _sconv(q_ext, w):
    return (_roll(q_ext, 2) * w[0:1] + _roll(q_ext, 1) * w[1:2] + q_ext * w[2:3])[HALO:]


def _halo_specs(t, tb, cols, col_block):
    per = tb // HALO
    last = t // HALO - 1
    prev = pl.BlockSpec((HALO, cols), lambda i: (jnp.maximum(i * per - 1, 0), col_block))
    nxt = pl.BlockSpec((HALO, cols), lambda i: (jnp.minimum((i + 1) * per, last), col_block))
    return prev, nxt


def _f_ab(proj, pool_mat, pool_scale, sconv_w):
    t = proj.shape[0]
    tb = _tblock(t)
    prev, _ = _halo_specs(t, tb, 1024, 0)

    def body(p_ref, h_ref, pm_ref, ps_ref, sw_ref, ya_ref, yb_ref):
        i = pl.program_id(0)
        halo = jnp.where(i > 0, h_ref[...], 0.0)
        ext = jnp.concatenate([halo, p_ref[...]], axis=0)
        p = _pool_p(ext[:, 0:256], i * tb, tb)
        ya_ref[...] = _bdot(p, pm_ref[...]) * ps_ref[...]
        q_ext = ext[:, 512:768] * ext[:, 768:1024]
        yb_ref[...] = p_ref[:, 256:512] * _sconv(q_ext, sw_ref[...])

    blk = pl.BlockSpec((tb, GW), lambda i: (i, 0))
    sds = jax.ShapeDtypeStruct((t, GW), F32)
    return pl.pallas_call(
        body, name="f_ab", grid=(t // tb,),
        in_specs=[pl.BlockSpec((tb, 1024), lambda i: (i, 0)), prev, _full((GW, GW)), _row(GW), _full((3, GW))],
        out_specs=[blk, blk], out_shape=[sds, sds], compiler_params=_cparams(1),
    )(proj, proj, pool_mat, pool_scale, sconv_w)


def _b_ab(proj, dya, dyb, pool_mat, pool_scale, sconv_w):
    t = proj.shape[0]
    tb = _tblock(t)
    nb = t // tb
    prev, nxt = _halo_specs(t, tb, 1024, 0)
    _, nxt_g = _halo_specs(t, tb, GW, 0)
    n_ext = tb + HALO

    def body(p_ref, hp_ref, hn_ref, da_ref, dan_ref, db_ref, dbn_ref, pm_ref, ps_ref, sw_ref,
             o_ref, dpm_ref, dps_ref, dsw_ref):
        i = pl.program_id(0)

        @pl.when(i == 0)
        def _():
            for r in (dpm_ref, dps_ref, dsw_ref):
                r[...] = jnp.zeros_like(r)

        last = i == nb - 1
        halo = jnp.where(i > 0, hp_ref[...], 0.0)
        main = p_ref[...]
        ext = jnp.concatenate([halo, main], axis=0)
        scale = ps_ref[...]
        pm = pm_ref[...]
        p = _pool_p(ext[:, 0:256], i * tb, tb)
        da = da_ref[...]
        dps_ref[...] += _colsum(da * _bdot(p, pm))
        da_ext = jnp.concatenate([da, jnp.where(last, 0.0, dan_ref[...])], axis=0)
        dys = da_ext * scale
        dpm_ref[...] += _bdot(p, dys[:tb], TN)
        dp = _bdot(dys, pm, NT)
        dpc = dp / _pool_count(i * tb, n_ext)
        a2 = dpc + _roll(dpc, n_ext - 1)
        a4 = a2 + _roll(a2, n_ext - 2)
        a8 = a4 + _roll(a4, n_ext - 4)
        a16 = a8 + _roll(a8, n_ext - 8)
        o_ref[:, 0:256] = (_window_select(_lane_group(dpc.shape), a2, a4, a8, a16) - dp)[:tb]
        w = sw_ref[...]
        gb, gc, hh = main[:, 256:512], main[:, 512:768], main[:, 768:1024]
        q_ext = ext[:, 512:768] * ext[:, 768:1024]
        db = db_ref[...]
        o_ref[:, 256:512] = db * _sconv(q_ext, w)
        gb_next = hn_ref[:, 256:512]
        dconv = jnp.concatenate([db * gb, jnp.where(last, 0.0, dbn_ref[...] * gb_next)], axis=0)
        dq = (dconv * w[2:3] + _roll(dconv, n_ext - 1) * w[1:2] + _roll(dconv, n_ext - 2) * w[0:1])[:tb]
        o_ref[:, 512:768] = dq * hh
        o_ref[:, 768:1024] = dq * gc
        dc = dconv[:tb]
        dsw_ref[0:1, :] += _colsum(dc * _roll(q_ext, 2)[HALO:])
        dsw_ref[1:2, :] += _colsum(dc * _roll(q_ext, 1)[HALO:])
        dsw_ref[2:3, :] += _colsum(dc * q_ext[HALO:])

    blk = pl.BlockSpec((tb, GW), lambda i: (i, 0))
    return pl.pallas_call(
        body, name="b_ab", grid=(nb,),
        in_specs=[pl.BlockSpec((tb, 1024), lambda i: (i, 0)), prev, nxt, blk, nxt_g, blk, nxt_g,
                  _full((GW, GW)), _row(GW), _full((3, GW))],
        out_specs=[pl.BlockSpec((tb, 1024), lambda i: (i, 0)), _full((GW, GW)), _row(GW), _full((3, GW))],
        out_shape=[jax.ShapeDtypeStruct((t, 1024), F32), jax.ShapeDtypeStruct((GW, GW), F32),
                   jax.ShapeDtypeStruct((1, GW), F32), jax.ShapeDtypeStruct((3, GW), F32)],
        compiler_params=_cparams(1),
    )(proj, proj, proj, dya, dya, dyb, dyb, pool_mat, pool_scale, sconv_w)


CH = 8


def _ssd_conv(x, halo, w, b):
    ext = jnp.concatenate([halo, x], axis=0)
    pre = ext * w[3:4] + _roll(ext, 1) * w[2:3] + _roll(ext, 2) * w[1:2] + _roll(ext, 3) * w[0:1] + b
    return pre[CH:], ext


def _ssd_common(dt_raw, dtb, alog):
    ll = dt_raw.shape[0]
    dtv = _softplus(dt_raw + dtb)
    a_row = -jnp.exp(alog)
    r = lax.broadcasted_iota(jnp.int32, (ll, ll), 0)
    c = lax.broadcasted_iota(jnp.int32, (ll, ll), 1)
    tril = (r >= c).astype(F32)
    cs = _dot(tril, dtv * a_row, prec=HI)
    return dtv, a_row, cs, cs.T, r >= c


def _bd(a, b, ca, cb):
    return lax.dot_general(a, b, (((ca,), (cb,)), ((0,), (0,))), preferred_element_type=F32)


def _head_cols(m):
    return jnp.stack([m[:, h:h + 1] for h in range(NH)])


def _ssd_heads(act, dtv, cs, cs_t, causal):
    xs = jnp.stack([act[:, HP * h:HP * (h + 1)] for h in range(NH)])
    bm = jnp.stack([act[:, 256 + NS * (h // 2):256 + NS * (h // 2 + 1)] for h in range(NH)])
    cm = jnp.stack([act[:, 512 + NS * (h // 2):512 + NS * (h // 2 + 1)] for h in range(NH)])
    cs_c = _head_cols(cs)
    cs_r = jnp.stack([cs_t[h:h + 1, :] for h in range(NH)])
    mdec = jnp.where(causal[None], jnp.exp(jnp.minimum(cs_c - cs_r, 0.0)), 0.0)
    g2 = _bd(jnp.stack([cm[0], cm[2]]), jnp.stack([bm[0], bm[2]]), 2, 2)
    sc = jnp.stack([g2[h // 2] for h in range(NH)]) * mdec
    dt_c = _head_cols(dtv)
    xdt = xs * dt_c
    e = jnp.exp(cs_c)
    cs_last = cs_c[:, SSD_L - 1:SSD_L, :]
    wdec = jnp.exp(cs_last - cs_c)
    return xs, bm, cm, mdec, sc, dt_c, xdt, e, cs_last, wdec


def _head_scalars(row_ref):
    return jnp.stack([row_ref[0:1, h:h + 1] for h in range(NH)])


def _f_ssd(proj, dtp, conv_w, conv_b, dt_bias, a_log, d_skip):
    t = proj.shape[0]
    nc = t // SSD_L
    rows = SSD_SUB * SSD_L
    per = rows // CH

    def body(x_ref, hx_ref, dt_ref, z_ref, cw_ref, cb_ref, dtb_ref, al_ref, dk_ref, y_ref, yp_ref, sp_ref, s_ref):
        i = pl.program_id(0)

        @pl.when(i == 0)
        def _():
            s_ref[...] = jnp.zeros_like(s_ref)

        state = s_ref[...]
        dk = _head_scalars(dk_ref)
        for sub in range(SSD_SUB):
            r0 = sub * SSD_L
            rs = slice(r0, r0 + SSD_L)
            halo = jnp.where(i > 0, hx_ref[...], 0.0) if sub == 0 else x_ref[r0 - CH:r0, :]
            pre, _ = _ssd_conv(x_ref[rs, :], halo, cw_ref[...], cb_ref[...])
            act = _silu(pre)
            dtv, _, cs, cs_t, causal = _ssd_common(dt_ref[rs, :], dtb_ref[...], al_ref[...])
            xs, bm, cm, _, sc, _, xdt, e, cs_last, wdec = _ssd_heads(act, dtv, cs, cs_t, causal)
            sp_ref[sub] = state
            y = _bd(sc, xdt, 2, 1) + e * _bd(cm, state, 2, 2) + xs * dk
            for h in range(NH):
                yp_ref[rs, HP * h:HP * (h + 1)] = y[h]
            state = state * jnp.exp(cs_last) + _bd(xdt * wdec, bm, 1, 1)
            y_ref[rs, :] = yp_ref[rs, :] * _silu(z_ref[rs, :])
        s_ref[...] = state

    blk = pl.BlockSpec((rows, GW), lambda i: (i, 0))
    sds = jax.ShapeDtypeStruct((t, GW), F32)
    return pl.pallas_call(
        body, name="f_ssd", grid=(nc // SSD_SUB,),
        in_specs=[pl.BlockSpec((rows, 768), lambda i: (i, 2)),
                  pl.BlockSpec((CH, 768), lambda i: (jnp.maximum(i * per - 1, 0), 2)),
                  pl.BlockSpec((rows, DTW), lambda i: (i, 0)),
                  pl.BlockSpec((rows, GW), lambda i: (i, 4)),
                  _full((4, 768)), _row(768), _row(DTW), _row(DTW), _row(DTW)],
        out_specs=[blk, blk, pl.BlockSpec((SSD_SUB, NH, HP, NS), lambda i: (i, 0, 0, 0))],
        out_shape=[sds, sds, jax.ShapeDtypeStruct((nc, NH, HP, NS), F32)],
        scratch_shapes=[pltpu.VMEM((NH, HP, NS), F32)], compiler_params=_cparams(1),
    )(proj, proj, dtp, proj, conv_w, conv_b, dt_bias, a_log, d_skip)


def _b_ssd(proj, dtp, ypre, dyc, sprev, conv_w, conv_b, dt_bias, a_log, d_skip):
    t = proj.shape[0]
    nc = t // SSD_L
    steps = nc // SSD_SUB_BWD
    rows = SSD_SUB_BWD * SSD_L
    per = rows // CH
    n_ext = SSD_L + CH

    def chunk(sub, halo, dnext, ds_in, refs):
        (x_ref, dt_ref, z_ref, yp_ref, dy_ref, sp_ref, cw_ref, cb_ref, dtb_ref, al_ref, dk_ref,
         dz_ref, dx_ref, ddt_ref, dact_ref) = refs
        rs = slice(sub * SSD_L, (sub + 1) * SSD_L)
        dact = dact_ref.at[sub]
        w = cw_ref[...]
        pre, ext = _ssd_conv(x_ref[rs, :], halo, w, cb_ref[...])
        act = _silu(pre)
        dt_raw = dt_ref[rs, :]
        dtv, a_row, cs, cs_t, causal = _ssd_common(dt_raw, dtb_ref[...], al_ref[...])
        z = z_ref[rs, :]
        dyc_v = dy_ref[rs, :]
        dz_ref[rs, :] = dyc_v * yp_ref[rs, :] * _dsilu(z)
        dy_all = dyc_v * _silu(z)
        lane = lax.broadcasted_iota(jnp.int32, (SSD_L, DTW), 1)
        rowi = lax.broadcasted_iota(jnp.int32, (1, SSD_L, 1), 1)
        lane1 = lax.broadcasted_iota(jnp.int32, (1, DTW), 1)
        xs, bm, cm, mdec, sc, dt_c, xdt, e, cs_last, wdec = _ssd_heads(act, dtv, cs, cs_t, causal)
        dy = jnp.stack([dy_all[:, HP * h:HP * (h + 1)] for h in range(NH)])
        prev = sp_ref[sub]
        ds = ds_in
        lsum = lambda v: jnp.sum(v, axis=2, keepdims=True)
        dsc = _bd(dy, xdt, 2, 2)
        q = dsc * sc
        dg = dsc * mdec
        dxdt = _bd(sc, dy, 1, 1)
        dcs = lsum(q) - lsum(jnp.swapaxes(q, 1, 2))
        dc = _bd(dg, bm, 2, 1)
        db = _bd(dg, cm, 1, 1)
        cp = _bd(cm, prev, 2, 2)
        dcs += lsum(dy * cp) * e
        ey = e * dy
        dc += _bd(ey, prev, 2, 1)
        dprev = _bd(ey, cm, 1, 1)
        elast = jnp.exp(cs_last)
        dprev += ds * elast
        dcs_last = jnp.sum(lsum(ds * prev), axis=1, keepdims=True) * elast
        bds = _bd(bm, ds, 2, 2)
        dxdt += wdec * bds
        db += wdec * _bd(xdt, ds, 2, 1)
        dw = lsum(xdt * bds) * wdec
        dcs -= dw
        dcs_last += jnp.sum(dw, axis=1, keepdims=True)
        dcs += jnp.where(rowi == SSD_L - 1, dcs_last, 0.0)
        dxs = dxdt * dt_c + dy * _head_scalars(dk_ref)
        ddtx = lsum(dxdt * xs)
        ddk = jnp.sum(lsum(dy * xs), axis=1, keepdims=True)
        dcs_mat = jnp.zeros((SSD_L, DTW), F32)
        ddtx_mat = jnp.zeros((SSD_L, DTW), F32)
        ddk_row = jnp.zeros((1, DTW), F32)
        for h in range(NH):
            dact[:, HP * h:HP * (h + 1)] = dxs[h]
            dcs_mat = jnp.where(lane == h, dcs[h], dcs_mat)
            ddtx_mat = jnp.where(lane == h, ddtx[h], ddtx_mat)
            ddk_row = jnp.where(lane1 == h, ddk[h], ddk_row)
        for g in range(2):
            dact[:, 256 + NS * g:256 + NS * (g + 1)] = db[2 * g] + db[2 * g + 1]
            dact[:, 512 + NS * g:512 + NS * (g + 1)] = dc[2 * g] + dc[2 * g + 1]
        ds_out = dprev
        r2 = lax.broadcasted_iota(jnp.int32, (SSD_L, SSD_L), 0)
        c2 = lax.broadcasted_iota(jnp.int32, (SSD_L, SSD_L), 1)
        dadt = _dot((c2 >= r2).astype(F32), dcs_mat, prec=HI)
        ddt = jnp.where(lane < NH, (dadt * a_row + ddtx_mat) * _sig(dt_raw + dtb_ref[...]), 0.0)
        ddt_ref[rs, :] = ddt
        dpre = dact[...] * _dsilu(pre)
        dcw = jnp.concatenate([_colsum(dpre * _roll(ext, 3 - k)[CH:]) for k in range(4)], axis=0)
        dext = jnp.concatenate([dpre, dnext], axis=0)
        dx_ref[rs, :] = (dext * w[3:4] + _roll(dext, n_ext - 1) * w[2:3] + _roll(dext, n_ext - 2) * w[1:2]
                         + _roll(dext, n_ext - 3) * w[0:1])[:SSD_L]
        acc = (dcw, _colsum(dpre), _colsum(ddt), _colsum(dadt * dtv) * a_row, ddk_row)
        return dpre[0:CH], ds_out, acc

    def body(x_ref, hx_ref, dt_ref, z_ref, yp_ref, dy_ref, sp_ref, cw_ref, cb_ref, dtb_ref, al_ref, dk_ref,
             dz_ref, dx_ref, ddt_ref, dcw_ref, dcb_ref, ddtb_ref, dal_ref, ddk_ref, ds_ref, dnext_ref, dact_ref):
        i = pl.program_id(0)
        acc_refs = (dcw_ref, dcb_ref, ddtb_ref, dal_ref, ddk_ref)

        @pl.when(i == 0)
        def _():
            ds_ref[...] = jnp.zeros_like(ds_ref)
            dnext_ref[...] = jnp.zeros_like(dnext_ref)
            for r in acc_refs:
                r[...] = jnp.zeros_like(r)

        refs = (x_ref, dt_ref, z_ref, yp_ref, dy_ref, sp_ref, cw_ref, cb_ref, dtb_ref, al_ref, dk_ref, dz_ref, dx_ref, ddt_ref,
                dact_ref)
        ds = ds_ref[...]
        dnext = dnext_ref[...]
        total = None
        for sub in reversed(range(SSD_SUB_BWD)):
            if sub == 0:
                halo = jnp.where(i == steps - 1, 0.0, hx_ref[...])
            else:
                halo = x_ref[sub * SSD_L - CH:sub * SSD_L, :]
            dnext, ds, acc = chunk(sub, halo, dnext, ds, refs)
            total = acc if total is None else tuple(a + b for a, b in zip(total, acc))
        ds_ref[...] = ds
        dnext_ref[...] = dnext
        for r, v in zip(acc_refs, total):
            r[...] += v

    rev = lambda i: steps - 1 - i
    blk = lambda n, cb=0: pl.BlockSpec((rows, n), lambda i: (rev(i), cb))
    row = lambda n: jax.ShapeDtypeStruct((1, n), F32)
    return pl.pallas_call(
        body, name="b_ssd", grid=(steps,),
        in_specs=[blk(768, 2), pl.BlockSpec((CH, 768), lambda i: (jnp.maximum(rev(i) * per - 1, 0), 2)),
                  blk(DTW), blk(GW, 4), blk(GW), blk(GW), pl.BlockSpec((SSD_SUB_BWD, NH, HP, NS), lambda i: (rev(i), 0, 0, 0)),
                  _full((4, 768)), _row(768), _row(DTW), _row(DTW), _row(DTW)],
        out_specs=[blk(GW), blk(768), blk(DTW), _full((4, 768)), _row(768), _row(DTW), _row(DTW), _row(DTW)],
        out_shape=[jax.ShapeDtypeStruct((t, GW), F32), jax.ShapeDtypeStruct((t, 768), F32), jax.ShapeDtypeStruct((t, DTW), F32),
                   jax.ShapeDtypeStruct((4, 768), F32), row(768), row(DTW), row(DTW), row(DTW)],
        scratch_shapes=[pltpu.VMEM((NH, HP, NS), F32), pltpu.VMEM((CH, 768), F32), pltpu.VMEM((SSD_SUB_BWD, SSD_L, 768), F32)],
        compiler_params=_cparams(1),
    )(proj, proj, dtp, proj, ypre, dyc, sprev, conv_w, conv_b, dt_bias, a_log, d_skip)


def _s5_block(t):
    return min(t, 256)


def _seg_t():
    r = lax.broadcasted_iota(jnp.int32, (64, 1024), 0)
    c = lax.broadcasted_iota(jnp.int32, (64, 1024), 1)
    return (c // 16 == r).astype(F32)


def _s5_prep_math(a_re, a_im, lstep, b_re, b_im):
    step = jnp.exp(lstep)
    ars = a_re * step
    ais = a_im * step
    mag = jnp.exp(ars)
    lr = mag * jnp.cos(ais)
    li = mag * jnp.sin(ais)
    den = a_re * a_re + a_im * a_im
    nr = lr - 1.0
    f_re = (nr * a_re + li * a_im) / den
    f_im = (li * a_re - nr * a_im) / den
    seg = _seg_t()
    fr = _dot(f_re, seg, prec=HI)
    fi = _dot(f_im, seg, prec=HI)
    return lr, li, fr * b_re - fi * b_im, fr * b_im + fi * b_re, ars, ais


def _s5_prep(a_re, a_im, lstep, b_re, b_im):
    def body(ar, ai, ls, br, bi, lr_o, li_o, bbr_o, bbi_o, ars_o, ais_o):
        outs = _s5_prep_math(ar[...], ai[...], ls[...], br[...], bi[...])
        for o, v in zip((lr_o, li_o, bbr_o, bbi_o, ars_o, ais_o), outs):
            o[...] = v

    s64 = jax.ShapeDtypeStruct((16, 64), F32)
    s1k = jax.ShapeDtypeStruct((16, 1024), F32)
    return pl.pallas_call(body, name="s5_prep", out_shape=[s64, s64, s1k, s1k, s64, s64])(a_re, a_im, lstep, b_re, b_im)


def _s5_prep_bwd(a_re, a_im, lstep, b_re, b_im, dlr, dli, dbbr, dbbi):
    def body(ar, ai, ls, br, bi, g0, g1, g2, g3, o0, o1, o2, o3, o4):
        f = lambda *a: _s5_prep_math(*a)[:4]
        _, vjp = jax.vjp(f, ar[...], ai[...], ls[...], br[...], bi[...])
        for o, v in zip((o0, o1, o2, o3, o4), vjp((g0[...], g1[...], g2[...], g3[...]))):
            o[...] = v

    s64 = jax.ShapeDtypeStruct((16, 64), F32)
    s1k = jax.ShapeDtypeStruct((16, 1024), F32)
    return pl.pallas_call(body, name="s5_prep_bwd", out_shape=[s64, s64, jax.ShapeDtypeStruct((16, 1), F32), s1k, s1k])(
        a_re, a_im, lstep, b_re, b_im, dlr, dli, dbbr, dbbi)


SUB = 8


def _s5_tables(ars, ais):
    def body(ar, ai, tr, ti):
        rr = lax.broadcasted_iota(jnp.int32, (8 * SUB, S5_P), 0)
        seg, r = rr // SUB, rr % SUB
        step = jnp.where((seg == 1) | (seg == 4), 1, jnp.where((seg == 2) | (seg == 5), 2, 4))
        n = jnp.where(seg == 0, r + 1, jnp.where(seg == 7, SUB - r, step))
        fwd_gap = jnp.where(seg <= 3, r - step, SUB - step - 1 - r)
        gap = jnp.where((seg == 0) | (seg == 7), 0, fwd_gap)
        nf = n.astype(F32)
        mag = jnp.where(gap >= 0, jnp.exp(nf * ar[...]), 0.0)
        tr[...] = mag * jnp.cos(nf * ai[...])
        ti[...] = mag * jnp.sin(nf * ai[...])

    sds = jax.ShapeDtypeStruct((8 * SUB, S5_P), F32)
    return pl.pallas_call(body, name="s5_tables", out_shape=[sds] * 2)(ars, ais)


def _s5_table(tb_r, tb_i, k):
    return tb_r[SUB * k:SUB * (k + 1), :], tb_i[SUB * k:SUB * (k + 1), :]


def _s5_scan(bu_r, bu_i, tb_r, tb_i, c_r, c_i, lb):
    nt = lb // SUB
    sr, si = bu_r.reshape(nt, SUB, S5_P), bu_i.reshape(nt, SUB, S5_P)
    for j, k in enumerate((1, 2, 4)):
        mr, mi = _s5_table(tb_r, tb_i, 1 + j)
        tr, ti = pltpu.roll(sr, k, axis=1), pltpu.roll(si, k, axis=1)
        sr, si = sr + mr * tr - mi * ti, si + mr * ti + mi * tr
    pr, pi = _s5_table(tb_r, tb_i, 0)
    out_r, out_i = [], []
    for j in range(nt):
        a_r = sr[j] + pr * c_r - pi * c_i
        a_i = si[j] + pr * c_i + pi * c_r
        out_r.append(a_r)
        out_i.append(a_i)
        c_r, c_i = a_r[SUB - 1:SUB], a_i[SUB - 1:SUB]
    return jnp.concatenate(out_r, axis=0), jnp.concatenate(out_i, axis=0)


def _s5_rscan(g_r, g_i, tb_r, tb_i, n_r, n_i, lb):
    nt = lb // SUB
    gr, gi = g_r.reshape(nt, SUB, S5_P), g_i.reshape(nt, SUB, S5_P)
    for j, k in enumerate((1, 2, 4)):
        mr, mi = _s5_table(tb_r, tb_i, 4 + j)
        tr, ti = pltpu.roll(gr, SUB - k, axis=1), pltpu.roll(gi, SUB - k, axis=1)
        gr, gi = gr + mr * tr + mi * ti, gi + mr * ti - mi * tr
    qr, qi = _s5_table(tb_r, tb_i, 7)
    out_r, out_i = [None] * nt, [None] * nt
    for j in reversed(range(nt)):
        a_r = gr[j] + qr * n_r + qi * n_i
        a_i = gi[j] + qr * n_i - qi * n_r
        out_r[j], out_i[j] = a_r, a_i
        n_r, n_i = a_r[0:1], a_i[0:1]
    return jnp.concatenate(out_r, axis=0), jnp.concatenate(out_i, axis=0)


def _s5_y(u, sr, si, cre, cim, dsk):
    return _bdot(sr, cre) + _bdot(si, cim) + dsk * u


def _f_s5(proj, bmat, cre, cim, p_r, p_i, dsk, glu_w, glu_b):
    t = proj.shape[0]
    lb = _s5_block(t)
    nb = t // lb

    def body(u_ref, bm_ref, cr_ref, ci_ref, pr_ref, pi_ref, dk_ref, gw_ref, gb_ref, y_ref, car_ref, s_ref, st_ref):
        @pl.when(pl.program_id(0) == 0)
        def _():
            st_ref[...] = jnp.zeros_like(st_ref)

        u = u_ref[...]
        bu = _bdot(u, bm_ref[...])
        c_r, c_i = st_ref[0:1, 0:S5_P], st_ref[0:1, S5_P:]
        car_ref[0] = st_ref[0:1, :]
        sr, si = _s5_scan(bu[:, :S5_P], bu[:, S5_P:], pr_ref, pi_ref, c_r, c_i, lb)
        st_ref[0:1, 0:S5_P] = sr[lb - 1:lb]
        st_ref[0:1, S5_P:] = si[lb - 1:lb]
        sr_b, si_b = sr.astype(BF16), si.astype(BF16)
        s_ref[:, 0:S5_P] = sr_b
        s_ref[:, S5_P:] = si_b
        gel = _gelu(_s5_y(u, sr_b, si_b, cr_ref[...], ci_ref[...], dk_ref[...]))
        y_ref[...] = gel * _sig(_bdot(gel, gw_ref[...]) + gb_ref[...])

    return pl.pallas_call(
        body, name="f_s5", grid=(nb,),
        in_specs=[pl.BlockSpec((lb, GW), lambda i: (i, 5)),
                  _full((GW, 2 * S5_P)), _full((S5_P, GW)), _full((S5_P, GW)), _full((8 * SUB, S5_P)), _full((8 * SUB, S5_P)),
                  _row(GW), _full((GW, GW)), _row(GW)],
        out_specs=[pl.BlockSpec((lb, GW), lambda i: (i, 0)), pl.BlockSpec((1, 1, 2 * S5_P), lambda i: (i, 0, 0)),
                   pl.BlockSpec((lb, 2 * S5_P), lambda i: (i, 0))],
        out_shape=[jax.ShapeDtypeStruct((t, GW), F32), jax.ShapeDtypeStruct((nb, 1, 2 * S5_P), F32),
                   jax.ShapeDtypeStruct((t, 2 * S5_P), BF16)],
        scratch_shapes=[pltpu.VMEM((8, 2 * S5_P), F32)], compiler_params=_cparams(1),
    )(proj, bmat, cre, cim, p_r, p_i, dsk, glu_w, glu_b)


def _b_s5(proj, dyd, carries, states, bmat, cre, cim, p_r, p_i, dsk, glu_w, glu_b):
    t = proj.shape[0]
    lb = _s5_block(t)
    nb = t // lb

    def body(u_ref, dy_ref, car_ref, s_ref, bm_ref, cr_ref, ci_ref, pr_ref, pi_ref, dk_ref, gw_ref, gb_ref,
             du_ref, dbm_ref, dcr_ref, dci_ref, dlam_ref, ddk_ref, dgw_ref, dgb_ref, gc_ref):
        @pl.when(pl.program_id(0) == 0)
        def _():
            gc_ref[...] = jnp.zeros_like(gc_ref)
            for r in (dbm_ref, dcr_ref, dci_ref, dlam_ref, ddk_ref, dgw_ref, dgb_ref):
                r[...] = jnp.zeros_like(r)

        u = u_ref[...]
        bm = bm_ref[...]
        u_b = u.astype(BF16)
        c_r, c_i = car_ref[0, 0:1, 0:S5_P], car_ref[0, 0:1, S5_P:]
        cre_v, cim_v, dk, gw = cr_ref[...], ci_ref[...], dk_ref[...], gw_ref[...]
        sr_b, si_b = s_ref[:, 0:S5_P], s_ref[:, S5_P:]
        sr, si = sr_b.astype(F32), si_b.astype(F32)
        y = _dot(sr_b, cre_v) + _dot(si_b, cim_v) + dk * u
        gel = _gelu(y)
        gel_b = gel.astype(BF16)
        gate = _sig(_dot(gel_b, gw) + gb_ref[...])
        dout = dy_ref[...]
        t1 = dout * gel * gate * (1.0 - gate)
        t1_b = t1.astype(BF16)
        dgw_ref[...] += _dot(gel_b, t1_b, TN)
        dgb_ref[...] += _colsum(t1)
        dyv = (dout * gate + _dot(t1_b, gw, NT)) * _dgelu(y)
        dyv_b = dyv.astype(BF16)
        ddk_ref[...] += _colsum(dyv * u)
        dcr_ref[...] += _dot(sr_b, dyv_b, TN)
        dci_ref[...] += _dot(si_b, dyv_b, TN)
        gr = _dot(dyv_b, cre_v, NT)
        gi = _dot(dyv_b, cim_v, NT)
        row = lax.broadcasted_iota(jnp.int32, (lb, S5_P), 0)
        n_r, n_i = gc_ref[0:1, 0:S5_P], gc_ref[0:1, S5_P:]
        gr, gi = _s5_rscan(gr, gi, pr_ref, pi_ref, n_r, n_i, lb)
        gc_ref[0:1, 0:S5_P] = gr[0:1]
        gc_ref[0:1, S5_P:] = gi[0:1]
        gcat = jnp.concatenate([gr, gi], axis=1).astype(BF16)
        dbm_ref[...] += _dot(u_b, gcat, TN)
        du_ref[...] = dyv * dk + _dot(gcat, bm, NT)
        spr = jnp.where(row >= 1, _roll(sr, 1), c_r)
        spi = jnp.where(row >= 1, _roll(si, 1), c_i)
        dlam_ref[0:1, :] += _colsum(gr * spr + gi * spi)
        dlam_ref[1:2, :] += _colsum(gi * spr - gr * spi)

    rev = lambda i: nb - 1 - i
    return pl.pallas_call(
        body, name="b_s5", grid=(nb,),
        in_specs=[pl.BlockSpec((lb, GW), lambda i: (rev(i), 5)), pl.BlockSpec((lb, GW), lambda i: (rev(i), 0)),
                  pl.BlockSpec((1, 1, 2 * S5_P), lambda i: (rev(i), 0, 0)), pl.BlockSpec((lb, 2 * S5_P), lambda i: (rev(i), 0)),
                  _full((GW, 2 * S5_P)), _full((S5_P, GW)), _full((S5_P, GW)), _full((8 * SUB, S5_P)), _full((8 * SUB, S5_P)),
                  _row(GW), _full((GW, GW)), _row(GW)],
        out_specs=[pl.BlockSpec((lb, GW), lambda i: (rev(i), 0)), _full((GW, 2 * S5_P)), _full((S5_P, GW)), _full((S5_P, GW)),
                   _full((2, S5_P)), _row(GW), _full((GW, GW)), _row(GW)],
        out_shape=[jax.ShapeDtypeStruct((t, GW), F32), jax.ShapeDtypeStruct((GW, 2 * S5_P), F32),
                   jax.ShapeDtypeStruct((S5_P, GW), F32), jax.ShapeDtypeStruct((S5_P, GW), F32),
                   jax.ShapeDtypeStruct((2, S5_P), F32), jax.ShapeDtypeStruct((1, GW), F32),
                   jax.ShapeDtypeStruct((GW, GW), F32), jax.ShapeDtypeStruct((1, GW), F32)],
        scratch_shapes=[pltpu.VMEM((8, 2 * S5_P), F32)], compiler_params=_cparams(1),
    )(proj, dyd, carries, states, bmat, cre, cim, p_r, p_i, dsk, glu_w, glu_b)


def _group_norm(ys, bw):
    outs, stats = [], []
    for g, y in enumerate(ys):
        r, n = _rms(y)
        stats.append((r, n))
        outs.append(n * bw[:, GW * g:GW * (g + 1)])
    return jnp.concatenate(outs, axis=1), stats


def _f_out(ya, yb, yc, yd, bw, w_out, h, g1):
    t = h.shape[0]
    tb = _tblock(t)

    def body(a_ref, b_ref, c_ref, d_ref, bw_ref, w_ref, h_ref, g_ref, h2_ref, o_ref, cat_ref):
        cat, _ = _group_norm([a_ref[...], b_ref[...], c_ref[...], d_ref[...]], bw_ref[...])
        catb = cat.astype(BF16)
        cat_ref[...] = catb
        o = _dot(catb, w_ref[...])
        o_ref[...] = o
        h2_ref[...] = h_ref[...] + g_ref[...] * o

    yblk = pl.BlockSpec((tb, GW), lambda i: (i, 0))
    blk = pl.BlockSpec((tb, D), lambda i: (i, 0))
    return pl.pallas_call(
        body, name="f_out", grid=(t // tb,), in_specs=[yblk] * 4 + [_row(D), _full((D, D)), blk, _row(D)],
        out_specs=[blk, blk, blk],
        out_shape=[jax.ShapeDtypeStruct((t, D), F32), jax.ShapeDtypeStruct((t, D), F32), jax.ShapeDtypeStruct((t, D), BF16)],
        compiler_params=_cparams(1),
    )(ya, yb, yc, yd, bw, w_out, h, g1)


def _b_out(dh2, ya, yb, yc, yd, bw, w_out, g1):
    t = dh2.shape[0]
    tb = _tblock(t)

    def body(dh_ref, a_ref, b_ref, c_ref, d_ref, bw_ref, w_ref, g_ref, da_ref, db_ref, dc_ref, dd_ref, do_ref, dbw_ref):
        @pl.when(pl.program_id(0) == 0)
        def _():
            dbw_ref[...] = jnp.zeros_like(dbw_ref)

        do = (dh_ref[...] * g_ref[...]).astype(BF16)
        do_ref[...] = do
        dcat = _dot(do, w_ref[...], NT)
        bw_v = bw_ref[...]
        for g, (y_ref, dy_ref) in enumerate(((a_ref, da_ref), (b_ref, db_ref), (c_ref, dc_ref), (d_ref, dd_ref))):
            r, n = _rms(y_ref[...])
            dc = dcat[:, GW * g:GW * (g + 1)]
            dbw_ref[:, GW * g:GW * (g + 1)] += _colsum(dc * n)
            dy_ref[...] = _rms_bwd(r, n, dc * bw_v[:, GW * g:GW * (g + 1)])

    yblk = pl.BlockSpec((tb, GW), lambda i: (i, 0))
    blk = pl.BlockSpec((tb, D), lambda i: (i, 0))
    ysd = jax.ShapeDtypeStruct((t, GW), F32)
    return pl.pallas_call(
        body, name="b_out", grid=(t // tb,), in_specs=[blk] + [yblk] * 4 + [_row(D), _full((D, D)), _row(D)],
        out_specs=[yblk] * 4 + [blk, _row(D)],
        out_shape=[ysd] * 4 + [jax.ShapeDtypeStruct((t, D), BF16), jax.ShapeDtypeStruct((1, D), F32)],
        compiler_params=_cparams(1),
    )(dh2, ya, yb, yc, yd, bw, w_out, g1)


HB = 512
MLP_ROWS = 1024


def _f_mlp(h2, nw, sc, sh, g2, w1, w2):
    t = h2.shape[0]
    tb = _tblock(t, MLP_ROWS)
    nk = HID // HB

    def body(h_ref, nw_ref, sc_ref, sh_ref, g_ref, w1_ref, w2_ref, h3_ref, m_ref, a_ref, v_ref):
        k = pl.program_id(1)

        @pl.when(k == 0)
        def _():
            _, n = _rms(h_ref[...])
            v_ref[...] = ((n * nw_ref[...]) * (1.0 + sc_ref[...]) + sh_ref[...]).astype(BF16)
            m_ref[...] = jnp.zeros_like(m_ref)

        a = _dot(v_ref[...], w1_ref[...])
        a_ref[...] = a.astype(BF16)
        ra = jnp.maximum(a, 0.0)
        m_ref[...] += _dot((ra * ra).astype(BF16), w2_ref[...])

        @pl.when(k == nk - 1)
        def _():
            h3_ref[...] = h_ref[...] + g_ref[...] * m_ref[...]

    blk = pl.BlockSpec((tb, D), lambda i, k: (i, 0))
    return pl.pallas_call(
        body, name="f_mlp", grid=(t // tb, nk),
        in_specs=[blk, _row(D), _row(D), _row(D), _row(D), pl.BlockSpec((D, HB), lambda i, k: (0, k)),
                  pl.BlockSpec((HB, D), lambda i, k: (k, 0))],
        out_specs=[blk, blk, pl.BlockSpec((tb, HB), lambda i, k: (i, k)), blk],
        out_shape=[jax.ShapeDtypeStruct((t, D), F32), jax.ShapeDtypeStruct((t, D), F32), jax.ShapeDtypeStruct((t, HID), BF16),
                   jax.ShapeDtypeStruct((t, D), BF16)],
        compiler_params=_cparams(2),
    )(h2, nw, sc, sh, g2, w1, w2)


def _b_mlp(dh3, a, g2, w1, w2):
    t = dh3.shape[0]
    tb = _tblock(t, MLP_ROWS)
    nk = HID // HB

    def body(dh_ref, a_ref, g_ref, w1_ref, w2_ref, dv_ref, da_ref, act_ref, dm_ref):
        k = pl.program_id(1)
        dm = (dh_ref[...] * g_ref[...]).astype(BF16)

        @pl.when(k == 0)
        def _():
            dm_ref[...] = dm
            dv_ref[...] = jnp.zeros_like(dv_ref)

        ra = jnp.maximum(a_ref[...].astype(F32), 0.0)
        act_ref[...] = (ra * ra).astype(BF16)
        da = (_dot(dm, w2_ref[...], NT) * (2.0 * ra)).astype(BF16)
        da_ref[...] = da
        dv_ref[...] += _dot(da, w1_ref[...], NT)

    blk = pl.BlockSpec((tb, D), lambda i, k: (i, 0))
    hblk = pl.BlockSpec((tb, HB), lambda i, k: (i, k))
    return pl.pallas_call(
        body, name="b_mlp", grid=(t // tb, nk),
        in_specs=[blk, hblk, _row(D), pl.BlockSpec((D, HB), lambda i, k: (0, k)), pl.BlockSpec((HB, D), lambda i, k: (k, 0))],
        out_specs=[blk, hblk, hblk, blk],
        out_shape=[jax.ShapeDtypeStruct((t, D), F32), jax.ShapeDtypeStruct((t, HID), BF16), jax.ShapeDtypeStruct((t, HID), BF16),
                   jax.ShapeDtypeStruct((t, D), BF16)],
        compiler_params=_cparams(2),
    )(dh3, a, g2, w1, w2)


def _b_final(h, tgt, fw):
    t = h.shape[0]
    tb = _tblock(t)

    def body(h_ref, t_ref, w_ref, dh_ref, loss_ref, dfw_ref):
        @pl.when(pl.program_id(0) == 0)
        def _():
            loss_ref[...] = jnp.zeros_like(loss_ref)
            dfw_ref[...] = jnp.zeros_like(dfw_ref)

        r, n = _rms(h_ref[...])
        wv = w_ref[...]
        err = n * wv - t_ref[...]
        loss_ref[...] += jnp.sum(err * err, keepdims=True) * (0.5 / D)
        dy = err * (1.0 / D)
        dfw_ref[...] += _colsum(dy * n)
        dh_ref[...] = _rms_bwd(r, n, dy * wv)

    blk = pl.BlockSpec((tb, D), lambda i: (i, 0))
    return pl.pallas_call(
        body, name="b_final", grid=(t // tb,), in_specs=[blk, blk, _row(D)], out_specs=[blk, _row(1), _row(D)],
        out_shape=[jax.ShapeDtypeStruct((t, D), F32), jax.ShapeDtypeStruct((1, 1), F32), jax.ShapeDtypeStruct((1, D), F32)],
        compiler_params=_cparams(1),
    )(h, tgt, fw)


_EYE16 = None


def _eye(n):
    return jnp.eye(n, dtype=F32)


def _pool_embed(pool_w):
    return jnp.einsum('gcd,gk->gckd', pool_w, _eye(4)).reshape(GW, GW)


def _pool_extract(m):
    return jnp.einsum('gcgd->gcd', m.reshape(4, 64, 4, 64))


def _bmat_embed(bb):
    return jnp.einsum('gph,gk->ghkp', bb, _eye(16)).reshape(GW, S5_P)


def _bmat_extract(m):
    return jnp.einsum('ghgp->gph', m.reshape(16, 16, 16, 64))


def _cmat_embed(cc):
    return jnp.einsum('ghp,gk->kpgh', cc, _eye(16)).reshape(S5_P, GW)


def _cmat_extract(m):
    return jnp.einsum('gpgh->ghp', m.reshape(16, 64, 16, 16))


def _pad_lanes(v, n=DTW):
    return jnp.pad(v.reshape(1, -1), ((0, 0), (0, n - v.shape[-1])))


def _w_in_layout(w_in_t):
    w_main = jnp.concatenate([w_in_t[:1280], w_in_t[2052:2308], w_in_t[1280:2048]], axis=0)
    return w_main, jnp.pad(w_in_t[2048:2052], ((0, DTW - 4), (0, 0)))


def _layer_params(p, l, mod, w_in, rest):
    q = {'rest': rest}
    q['mod'] = [mod[k:k + 1] for k in range(6)]
    q['nw1'] = p['norm_mix_w'][l:l + 1]
    q['nw2'] = p['norm_mlp_w'][l:l + 1]
    q['w_main'], q['w_dt'] = _w_in_layout(w_in)
    q['pool_mat'] = _pool_embed(p['pool_w'][l]).astype(BF16)
    q['pool_scale'] = p['pool_scale'][l:l + 1]
    q['sconv_w'] = p['sconv_w'][l]
    q['conv_w'] = p['ssd_conv_w'][l]
    q['conv_b'] = p['ssd_conv_b'][l:l + 1]
    q['dt_bias'] = _pad_lanes(p['ssd_dt_bias'][l])
    q['a_log'] = _pad_lanes(p['ssd_a_log'][l])
    q['ssd_d'] = _pad_lanes(p['ssd_d'][l])
    q['s5_raw'] = (p['s5_a_re'][l], p['s5_a_im'][l], p['s5_log_step'][l].reshape(16, 1),
                   p['s5_b_re'][l].reshape(16, 1024), p['s5_b_im'][l].reshape(16, 1024))
    q['cre'] = _cmat_embed(p['s5_c_re'][l]).astype(BF16)
    q['cim'] = (-_cmat_embed(p['s5_c_im'][l])).astype(BF16)
    q['s5_d'] = p['s5_d'][l:l + 1]
    q['glu_w'] = p['s5_glu_w'][l].astype(BF16)
    q['glu_b'] = p['s5_glu_b'][l:l + 1]
    q['bw'] = p['branch_norm_w'][l:l + 1]
    return q


def _layer_fwd(h, q):
    sh1, sc1, g1, sh2, sc2, g2 = q['mod']
    t = h.shape[0]
    s = {'h': h}
    s['proj'], s['dtp'], s['u'] = _f_in(h, q['nw1'], sc1, sh1, q['w_main'], q['w_dt'])
    s['ya'], s['yb'] = _f_ab(s['proj'], q['pool_mat'], q['pool_scale'], q['sconv_w'])
    s['yc'], s['ypre'], s['sprev'] = _f_ssd(s['proj'], s['dtp'], q['conv_w'], q['conv_b'], q['dt_bias'], q['a_log'], q['ssd_d'])
    lr, li, bbr, bbi, ars, ais = _s5_prep(*q['s5_raw'])
    s['bmat'] = jnp.concatenate([_bmat_embed(bbr.reshape(16, 64, 16)), _bmat_embed(bbi.reshape(16, 64, 16))],
                                axis=1).astype(BF16)
    s['tables'] = _s5_tables(ars.reshape(1, S5_P), ais.reshape(1, S5_P))
    s['yd'], s['carries'], s['states'] = _f_s5(s['proj'], s['bmat'], q['cre'], q['cim'], s['tables'][0], s['tables'][1],
                                  q['s5_d'], q['glu_w'], q['glu_b'])
    q['w_out'], q['w1'], q['w2'] = q['rest']((s['ya'], s['yc'], s['yd']))
    s['h2'], s['o'], s['cat'] = _f_out(s['ya'], s['yb'], s['yc'], s['yd'], q['bw'], q['w_out'], h, g1)
    h3, s['m'], s['a'], s['v'] = _f_mlp(s['h2'], q['nw2'], sc2, sh2, g2, q['w1'], q['w2'])
    return h3, s


def _layer_bwd(dh3, q, s, early=None):
    sh1, sc1, g1, sh2, sc2, g2 = q['mod']
    g = {}
    dv, da, act, dm = _b_mlp(dh3, s['a'], g2, q['w1'], q['w2'])
    g['mlp_w1'] = _tn_matmul(s['v'], da, "dw1", col_major=True)
    g['mlp_w2'] = _tn_matmul(act, dm, "dw2")
    dh2, dsc2, dsh2, dnw2, dg2 = _b_normmod(dv, s['h2'], dh3, s['m'], q['nw2'], sc2, "b_norm_mlp")
    dya, dyb, dyc, dyd, do, dbw = _b_out(dh2, s['ya'], s['yb'], s['yc'], s['yd'], q['bw'], q['w_out'], g1)
    g['w_out'] = _tn_matmul(s['cat'], do, "dwout")
    g['branch_norm_w'] = dbw[0]
    if early is not None:
        zero = early(g)[0, 0]
        q = dict(q, pool_scale=q['pool_scale'] + zero, conv_b=q['conv_b'] + zero, s5_d=q['s5_d'] + zero)
    dab, dpm, dps, dsw = _b_ab(s['proj'], dya, dyb, q['pool_mat'], q['pool_scale'], q['sconv_w'])
    g['pool_w'] = _pool_extract(dpm)
    g['pool_scale'] = dps[0]
    g['sconv_w'] = dsw
    dz, dxbc, ddt, dcw, dcb, ddtb, dal, ddk = _b_ssd(s['proj'], s['dtp'], s['ypre'], dyc, s['sprev'], q['conv_w'],
                                                     q['conv_b'], q['dt_bias'], q['a_log'], q['ssd_d'])
    g['ssd_conv_w'] = dcw
    g['ssd_conv_b'] = dcb[0]
    g['ssd_dt_bias'] = ddtb[0, :4]
    g['ssd_a_log'] = dal[0, :4]
    g['ssd_d'] = ddk[0, :4]
    tb = s['tables']
    ds5, dbmat, dcre, dcim, dlam, dd5, dgw, dgb = _b_s5(s['proj'], dyd, s['carries'], s['states'], s['bmat'], q['cre'], q['cim'],
                                                        tb[0], tb[1], q['s5_d'], q['glu_w'], q['glu_b'])
    g['s5_c_re'] = _cmat_extract(dcre)
    g['s5_c_im'] = -_cmat_extract(dcim)
    g['s5_d'] = dd5[0]
    g['s5_glu_w'] = dgw
    g['s5_glu_b'] = dgb[0]
    dbbr = _bmat_extract(dbmat[:, :S5_P]).reshape(16, 1024)
    dbbi = _bmat_extract(dbmat[:, S5_P:]).reshape(16, 1024)
    dar, dai, dls, dbr, dbi = _s5_prep_bwd(*q['s5_raw'], dlam[0].reshape(16, 64), dlam[1].reshape(16, 64), dbbr, dbbi)
    g['s5_a_re'], g['s5_a_im'], g['s5_log_step'] = dar, dai, dls[:, 0]
    g['s5_b_re'], g['s5_b_im'] = dbr, dbi
    du = _b_in_du(dab, dz, dxbc, ds5, ddt, q['w_main'], q['w_dt'])
    u = s['u']
    head = jnp.concatenate([_tn_matmul(dab, u, "dwin_ab"), _tn_matmul(dz, u, "dwin_z"), _tn_matmul(dxbc, u, "dwin_xbc"),
                            _tn_matmul(ddt, u, "dwin_dt")[:8]], axis=0)
    full = lax.dynamic_update_slice(jnp.zeros((2308, D), F32), head, (0, 0))
    g['w_in'] = lax.dynamic_update_slice(full, _tn_matmul(ds5, u, "dwin_s5"), (2052, 0))
    dh, dsc1, dsh1, dnw1, dg1 = _b_normmod(du, s['h'], dh2, s['o'], q['nw1'], sc1, "b_norm_mix")
    g['norm_mix_w'] = dnw1[0]
    g['norm_mlp_w'] = dnw2[0]
    dmod = jnp.concatenate([dsh1, dsc1, dg1, dsh2, dsc2, dg2], axis=1)
    return dh, g, dmod


def _local_step(x, tgt, p, mod, w_in_of, rest_of, early=None):
    h = x
    qs, saved = [], []
    for l in range(2):
        qs.append(_layer_params(p, l, mod[l], w_in_of(l), functools.partial(rest_of, l)))
        h, s = _layer_fwd(h, qs[l])
        saved.append(s)
    dh, loss, dfw = _b_final(h, tgt, p['final_norm_w'].reshape(1, D))
    grads = [None, None]
    dmods = [None, None]
    dh, grads[1], dmods[1] = _layer_bwd(dh, qs[1], saved[1])
    hook = None if early is None else (lambda g0: early(g0, grads[1]))
    dh, grads[0], dmods[0] = _layer_bwd(dh, qs[0], saved[0], hook)
    skip = () if early is None else ('w_out', 'mlp_w1', 'mlp_w2')
    out = {k: jnp.stack([grads[0][k], grads[1][k]]) for k in grads[0] if k not in skip}
    out['final_norm_w'] = dfw[0]
    return loss, dh, out, jnp.concatenate(dmods, axis=0)


def _pack(arrs):
    parts, rows = [], 0
    for a in arrs:
        f = a.reshape(-1).astype(F32)
        pad = (-f.shape[0]) % 1024
        f = jnp.pad(f, (0, pad)) if pad else f
        parts.append(f.reshape(-1, 128))
        rows += parts[-1].shape[0]
    if rows % 256:
        parts.append(jnp.zeros((256 - rows % 256, 128), F32))
    return jnp.concatenate(parts, axis=0)


def _unpack(buf, shapes):
    out, row = [], 0
    for shp in shapes:
        n = int(math.prod(shp)) if len(shp) else 1
        rows = (n + 1023) // 1024 * 8
        out.append(buf[row:row + rows].reshape(-1)[:n].reshape(shp))
        row += rows
    return out


def _shard_of(a, axis, k):
    n = a.shape[axis] // 4
    return lax.dynamic_slice_in_dim(a, k * n, n, axis)


def kernel(x, c, norm_mix_w, norm_mlp_w, ada_w, ada_b, w_in, pool_w, pool_scale, sconv_w, ssd_conv_w, ssd_conv_b, ssd_dt_bias, ssd_a_log, ssd_d, s5_a_re, s5_a_im, s5_log_step, s5_b_re, s5_b_im, s5_c_re, s5_c_im, s5_d, s5_glu_w, s5_glu_b, branch_norm_w, w_out, mlp_w1, mlp_w2, final_norm_w, loss_target, m_norm_mix_w, m_norm_mlp_w, m_ada_w, m_ada_b, m_w_in, m_pool_w, m_pool_scale, m_sconv_w, m_ssd_conv_w, m_ssd_conv_b, m_ssd_dt_bias, m_ssd_a_log, m_ssd_d, m_s5_a_re, m_s5_a_im, m_s5_log_step, m_s5_b_re, m_s5_b_im, m_s5_c_re, m_s5_c_im, m_s5_d, m_s5_glu_w, m_s5_glu_b, m_branch_norm_w, m_w_out, m_mlp_w1, m_mlp_w2, m_final_norm_w, v_norm_mix_w, v_norm_mlp_w, v_ada_w, v_ada_b, v_w_in, v_pool_w, v_pool_scale, v_sconv_w, v_ssd_conv_w, v_ssd_conv_b, v_ssd_dt_bias, v_ssd_a_log, v_ssd_d, v_s5_a_re, v_s5_a_im, v_s5_log_step, v_s5_b_re, v_s5_b_im, v_s5_c_re, v_s5_c_im, v_s5_d, v_s5_glu_w, v_s5_glu_b, v_branch_norm_w, v_w_out, v_mlp_w1, v_mlp_w2, v_final_norm_w):
    loc = locals()
    w = {n: loc[n] for n in WEIGHTS}
    mom = {n: loc['m_' + n] for n in WEIGHTS}
    var = {n: loc['v_' + n] for n in WEIGHTS}
    ix, iy, ic = lax.axis_index("x"), lax.axis_index("y"), lax.axis_index("c")
    chip = 2 * ix + iy
    dev = 4 * ix + 2 * iy + ic

    mine_of = lambda a: lax.dynamic_index_in_dim(a.astype(BF16), ic, axis=0, keepdims=False)
    pad_in = lambda a: jnp.pad(a.T, ((0, WIN_ROWS - 577), (0, 0)))
    shard = jnp.concatenate([pad_in(mine_of(w['w_in'])), mine_of(w['w_out']), mine_of(w['mlp_w1']), mine_of(w['mlp_w2'])], axis=0)

    (c_all,) = _exchange([c], EVERYONE, False, "ag_cond", stage=True)
    c_all = c_all.reshape(8, D)
    small_sh = _exchange([w[n] for n in SMALL_SHARDED], CHIPS, False, "ag_small")
    (w_in0,) = _exchange([pad_in(w['w_in'][0].astype(BF16))], CHIPS, False, "ag_win0")
    p = {n: w[n] for n in WEIGHTS if n not in BIG}
    for n, g in zip(SMALL_SHARDED, small_sh):
        ax = SMALL_SHARDED[n]
        p[n] = jnp.concatenate([g[k] for k in range(4)], axis=ax)

    def w_in_full(sh):
        return sh[:, :577].reshape(4 * 577, D)

    big = {}

    def fetch(after):
        if not big:
            (mine,), (got,) = _split_wait(sems, shard_thru, land, after, False, "ag_big_wait")
            got = lax.dynamic_update_slice(got, mine[None], (chip, 0, 0))
            other = _pair_swap([got.reshape(-1, D)], False, "swap_big")[0].reshape(got.shape)
            big['both'] = [jnp.where(ic == l, got, other) for l in range(2)]
        return big['both']

    def w_in_of(l):
        return w_in_full(w_in0) if l == 0 else w_in_full(fetch(None)[1])

    def rest_of(l, after):
        blk = fetch(after)[l]
        r0 = WIN_ROWS
        w_out_l = blk[:, r0:r0 + 256].reshape(D, D)
        w1_l = jnp.concatenate([blk[k, r0 + 256:r0 + 1280] for k in range(4)], axis=1)
        w2_l = blk[:, r0 + 1280:r0 + 2304].reshape(HID, D)
        return w_out_l, w1_l, w2_l

    ada_b_sh = _shard_of(w['ada_b'], 1, chip).reshape(2, 1, 6 * D // 4)
    mod_sh = _ada_fwd(c_all, w['ada_w'], ada_b_sh)
    (mod_all,) = _exchange([mod_sh], CHIPS, False, "ag_mod", stage=True)
    mine = lax.dynamic_index_in_dim(mod_all, dev, axis=2, keepdims=False)
    sems, shard_thru, land, token = _split_start([shard], [mod_all, w_in0] + small_sh, False, "ag_big_start")
    mod = jnp.transpose(mine, (1, 0, 2)).reshape(2, 6, D) + token[0, 0]

    layer = ic.astype(jnp.int32).reshape(1)
    flight = {}

    def early(g0, g1):
        both_layers = lambda n: jnp.stack([g0[n], g1[n]])
        gws = [both_layers('w_out').reshape(2, 4, 256, D), both_layers('mlp_w1'), both_layers('mlp_w2').reshape(2, 4, 1024, D)]
        got = _pair_swap([a.reshape(2, -1, D) for a in gws], True, "swap_grad")
        pair = [_pair_sum(a, b.reshape(a.shape[1:]), layer, "pair_sum%d" % (k + 1), BF16) for k, (a, b) in enumerate(zip(gws, got))]
        flight['sems'], flight['srcs'], flight['lands'], token = _split_start(pair, [], True, "rs_start")
        return token

    loss, grad_x, g, dmod = _local_step(x[0], loss_target[0], p, mod, w_in_of, rest_of, early)

    (dmod_all,) = _exchange([dmod], EVERYONE, False, "ag_dmod", stage=True)
    dmod_all = jnp.transpose(dmod_all, (1, 0, 2))
    g_ada_w, g_ada_b = _ada_bwd(c_all, _shard_of(dmod_all, 2, chip), dmod_all)

    sent, lands = _split_wait(flight['sems'], flight['srcs'], flight['lands'], [grad_x, g['w_in']], True, "rs_wait")
    quad = []
    for k, (land, mine) in enumerate(zip(lands, sent)):
        own = lax.dynamic_index_in_dim(mine, chip, axis=0, keepdims=True)
        quad.append(_sum_lead(lax.dynamic_update_slice(land, own, (chip, 0, 0)), "rs_chip_sum%d" % (k + 1), F32))
    gw_in = jnp.pad(g['w_in'].reshape(2, 4, 577, D), ((0, 0), (0, 0), (0, WIN_ROWS - 577), (0, 0)))
    (got_in,) = _pair_swap([gw_in.reshape(2, -1, D)], True, "swap_grad_in")
    pair_in = _pair_sum(gw_in, got_in.reshape(gw_in.shape[1:]), layer, "pair_sum0", BF16)
    (quad_in,) = _exchange([pair_in], CHIPS, True, "rs_chips")
    quad = [_sum_lead(quad_in, "rs_chip_sum0", F32)] + quad
    other = _pair_swap(quad, False, "swap_red")
    both = [jnp.stack([jnp.where(ic == l, a, b) for l in range(2)]) for a, b in zip(quad, other)]
    both[0] = jnp.transpose(both[0][:, :577], (0, 2, 1))
    red = dict(zip(('w_in', 'w_out', 'mlp_w1', 'mlp_w2'), both))
    red['ada_w'] = g_ada_w

    small_names = [n for n in WEIGHTS if n not in BIG and n != 'ada_b']
    pair_parts = _exchange([g[n] for n in small_names] + [loss], SIBLING, False, "ag_smallpair", stage=True)
    chip_parts = _exchange(_sum_many(pair_parts, "smallpair_sum"), CHIPS, False, "ag_smallgrad", stage=True)
    summed = _sum_many(chip_parts, "smallgrad_sum")
    for n, a in zip(small_names, summed[:-1]):
        a = a.reshape(w[n].shape) if n in ('s5_b_re', 's5_b_im') else a
        red[n] = _shard_of(a, SMALL_SHARDED[n], chip) if n in SMALL_SHARDED else a
    red['ada_b'] = g_ada_b
    loss_out = summed[-1].reshape(())

    delta, new_m, new_v = {}, {}, {}
    for n in BIG:
        delta[n], new_m[n], new_v[n] = _adamw(w[n], red[n], mom[n], var[n], "adamw_" + n)
    rest = [n for n in WEIGHTS if n not in BIG]
    outs = _adamw_many([w[n] for n in rest], [red[n] for n in rest], [mom[n] for n in rest], [var[n] for n in rest],
                       "adamw_small")
    for k, n in enumerate(rest):
        delta[n], new_m[n], new_v[n] = outs[3 * k], outs[3 * k + 1], outs[3 * k + 2]

    return (loss_out, grad_x[None], *[red[n] for n in WEIGHTS], *[delta[n] for n in WEIGHTS],
            *[new_m[n] for n in WEIGHTS], *[new_v[n] for n in WEIGHTS])
```

```python
import functools
import math

import jax
import jax.numpy as jnp
from jax import lax
from jax.experimental import pallas as pl
from jax.experimental.pallas import tpu as pltpu

F32 = jnp.float32
BF16 = jnp.bfloat16
HI = lax.Precision.HIGHEST

D = 1024
GW = 256
HID = 4096
EPS = 1e-6
PW = 2304
DTW = 128
SSD_L = 128
SSD_SUB = 2
SSD_SUB_BWD = 1
NH, HP, NS = 4, 64, 128
S5_P = 1024
MESH = pl.DeviceIdType.MESH

ADAM_LR, ADAM_B1, ADAM_B2, ADAM_EPS, ADAM_WD, ADAM_STEP = 0.001, 0.9, 0.999, 1e-08, 0.01, 10

NT = (((1,), (1,)), ((), ()))
TN = (((0,), (0,)), ((), ()))

WEIGHTS = ['norm_mix_w', 'norm_mlp_w', 'ada_w', 'ada_b', 'w_in', 'pool_w', 'pool_scale', 'sconv_w', 'ssd_conv_w',
           'ssd_conv_b', 'ssd_dt_bias', 'ssd_a_log', 'ssd_d', 's5_a_re', 's5_a_im', 's5_log_step', 's5_b_re', 's5_b_im',
           's5_c_re', 's5_c_im', 's5_d', 's5_glu_w', 's5_glu_b', 'branch_norm_w', 'w_out', 'mlp_w1', 'mlp_w2',
           'final_norm_w']
BIG = ('ada_w', 'w_in', 'w_out', 'mlp_w1', 'mlp_w2')
SMALL_SHARDED = {'sconv_w': 2, 'ssd_conv_w': 2, 's5_glu_w': 1}


def _cparams(n_axes, vmem_mb=48):
    return pltpu.CompilerParams(dimension_semantics=("arbitrary",) * n_axes, vmem_limit_bytes=vmem_mb * 1024 * 1024)


def _row(n):
    return pl.BlockSpec((1, n), lambda *_: (0, 0))


def _full(shape):
    nd = len(shape)
    return pl.BlockSpec(tuple(shape), lambda *_: (0,) * nd)


def _dot(a, b, dims=None, prec=None):
    if dims is None:
        dims = (((a.ndim - 1,), (0,)), ((), ()))
    return lax.dot_general(a, b, dims, preferred_element_type=F32, precision=prec)


def _bdot(a, b, dims=None):
    return _dot(a.astype(BF16), b.astype(BF16), dims)


def _sig(x):
    return jax.nn.sigmoid(x)


def _silu(x):
    return x * _sig(x)


def _dsilu(x):
    s = _sig(x)
    return s * (1.0 + x * (1.0 - s))


def _softplus(x):
    return jnp.maximum(x, 0.0) + jnp.log(1.0 + jnp.exp(-jnp.abs(x)))


_GK = math.sqrt(2.0 / math.pi)


def _gelu(x):
    return 0.5 * x * (1.0 + jnp.tanh(_GK * (x + 0.044715 * x * x * x)))


def _dgelu(x):
    th = jnp.tanh(_GK * (x + 0.044715 * x * x * x))
    return 0.5 * (1.0 + th) + 0.5 * x * (1.0 - th * th) * _GK * (1.0 + 3.0 * 0.044715 * x * x)


def _colsum(x):
    return jnp.sum(x, axis=0, keepdims=True)


def _rms(x):
    r = lax.rsqrt(jnp.mean(x * x, axis=-1, keepdims=True) + EPS)
    return r, x * r


def _rms_bwd(r, n, dn):
    return r * (dn - n * jnp.mean(dn * n, axis=-1, keepdims=True))


def _roll(x, k):
    n = x.shape[0]
    k = k % n
    return x if k == 0 else pltpu.roll(x, k, axis=0)


def _tblock(t, want=512):
    return min(t, want)


def _peer(mask):
    x, y, c = lax.axis_index("x"), lax.axis_index("y"), lax.axis_index("c")
    return (x ^ ((mask >> 2) & 1), y ^ ((mask >> 1) & 1), c ^ (mask & 1))


def _group_index(masks):
    x, y, c = lax.axis_index("x"), lax.axis_index("y"), lax.axis_index("c")
    full = 0
    for m in masks:
        full |= m
    bits = [b for b in (4, 2, 1) if full & b]

    def idx(px, py, pc):
        v = {4: px, 2: py, 1: pc}
        out = 0
        for b in bits:
            out = out * 2 + v[b]
        return out

    return idx(x, y, c), [idx(*_peer(m)) for m in masks]


def _exchange(arrs, masks, scatter, name, stage=False):
    n_arr, n_peer, n_grp = len(arrs), len(masks), len(masks) + 1

    def body(*refs):
        ins, outs = refs[:n_arr], refs[n_arr:2 * n_arr]
        send_sems, recv_sems, local_sems = refs[2 * n_arr:2 * n_arr + 3]
        if stage:
            bufs, load_sems = refs[2 * n_arr + 3:3 * n_arr + 3], refs[3 * n_arr + 3]
            loads = [pltpu.make_async_copy(ins[t], bufs[t], load_sems.at[t]) for t in range(n_arr)]
            for ld in loads:
                ld.start()
            for ld in loads:
                ld.wait()
            ins = bufs
        me, peer_idx = _group_index(masks)
        copies = []
        for t in range(n_arr):
            src_me = ins[t].at[me] if scatter else ins[t]
            loc = pltpu.make_async_copy(src_me, outs[t].at[me], local_sems.at[t])
            loc.start()
            copies.append(loc)
            for j, m in enumerate(masks):
                src = ins[t].at[peer_idx[j]] if scatter else ins[t]
                cp = pltpu.make_async_remote_copy(src_ref=src, dst_ref=outs[t].at[me], send_sem=send_sems.at[t, j],
                                                  recv_sem=recv_sems.at[t, j], device_id=_peer(m), device_id_type=MESH)
                cp.start()
                copies.append(cp)
        for cp in copies:
            cp.wait()

    hbm = pl.BlockSpec(memory_space=pl.ANY)
    out_shape = [jax.ShapeDtypeStruct((n_grp,) + (a.shape[1:] if scatter else a.shape), a.dtype) for a in arrs]
    staging = [pltpu.VMEM(a.shape, a.dtype) for a in arrs] + [pltpu.SemaphoreType.DMA((n_arr,))] if stage else []
    outs = pl.pallas_call(
        body, name=name, in_specs=[hbm] * n_arr, out_specs=[hbm] * n_arr, out_shape=out_shape,
        scratch_shapes=[pltpu.SemaphoreType.DMA((n_arr, n_peer)), pltpu.SemaphoreType.DMA((n_arr, n_peer)),
                        pltpu.SemaphoreType.DMA((n_arr,))] + staging,
        compiler_params=pltpu.CompilerParams(vmem_limit_bytes=48 * 1024 * 1024),
    )(*arrs)
    return list(outs)


def _split_copies(src_refs, land_refs, sems, scatter):
    me, peer_idx = _group_index(CHIPS)
    n = len(CHIPS) * len(src_refs)
    copies = []
    for t, (src_ref, land_ref) in enumerate(zip(src_refs, land_refs)):
        for j, m in enumerate(CHIPS):
            k = len(CHIPS) * t + j
            copies.append(pltpu.make_async_remote_copy(
                src_ref=src_ref.at[peer_idx[j]] if scatter else src_ref, dst_ref=land_ref.at[me], send_sem=sems[k],
                recv_sem=sems[n + k], device_id=_peer(m), device_id_type=MESH))
    return copies


def _split_start(srcs, after, scatter, name):
    n_arr, n_sem = len(srcs), 2 * len(CHIPS) * len(srcs)

    def body(*refs):
        src_refs, land_refs = refs[:n_arr], refs[n_arr:2 * n_arr]
        outs = refs[2 * n_arr + len(after):]
        for cp in _split_copies(src_refs, land_refs, outs[:n_sem], scatter):
            cp.start()
        outs[-1][...] = jnp.zeros_like(outs[-1])

    hbm = pl.BlockSpec(memory_space=pltpu.HBM)
    sem = pl.BlockSpec(memory_space=pltpu.SEMAPHORE)
    lands = [lax.empty((len(CHIPS) + 1,) + (a.shape[1:] if scatter else a.shape), a.dtype) for a in srcs]
    as_hbm = lambda a: pltpu.with_memory_space_constraint(a, pltpu.HBM)
    outs = pl.pallas_call(
        body, name=name,
        out_shape=(pltpu.SemaphoreType.DMA(()),) * n_sem + tuple(pltpu.HBM(a.shape, a.dtype) for a in srcs + lands)
        + (jax.ShapeDtypeStruct((8, 128), F32),),
        in_specs=(hbm,) * (2 * n_arr) + (pl.BlockSpec(memory_space=pl.ANY),) * len(after),
        out_specs=(sem,) * n_sem + (hbm,) * (2 * n_arr) + (pl.BlockSpec(memory_space=pltpu.VMEM),),
        input_output_aliases={t: n_sem + t for t in range(2 * n_arr)},
        compiler_params=pltpu.CompilerParams(has_side_effects=pltpu.SideEffectType.DATAFLOW_SIDE_EFFECTING),
    )(*[as_hbm(a) for a in srcs + lands], *after)
    return outs[:n_sem], list(outs[n_sem:n_sem + n_arr]), list(outs[n_sem + n_arr:n_sem + 2 * n_arr]), outs[-1]


def _split_wait(sems, srcs, lands, after, scatter, name):
    n_arr, n_sem = len(srcs), len(sems)

    def body(*refs):
        src_refs, land_refs = refs[:n_arr], refs[n_arr:2 * n_arr]
        for cp in _split_copies(src_refs, land_refs, refs[2 * n_arr:2 * n_arr + n_sem], scatter):
            cp.wait_send()
            cp.wait_recv()

    hbm = pl.BlockSpec(memory_space=pltpu.HBM)
    sem = pl.BlockSpec(memory_space=pltpu.SEMAPHORE)
    outs = pl.pallas_call(
        body, name=name, out_shape=tuple(pltpu.HBM(a.shape, a.dtype) for a in srcs + lands),
        in_specs=(hbm,) * (2 * n_arr) + (sem,) * n_sem + (pl.BlockSpec(memory_space=pl.ANY),) * len(after),
        out_specs=(hbm,) * (2 * n_arr), input_output_aliases={t: t for t in range(2 * n_arr)},
        compiler_params=pltpu.CompilerParams(has_side_effects=pltpu.SideEffectType.DATAFLOW_SIDE_EFFECTING),
    )(*srcs, *lands, *sems, *after)
    return list(outs[:n_arr]), list(outs[n_arr:])


CHIPS = (4, 2, 6)
EVERYONE = (1, 2, 3, 4, 5, 6, 7)
SIBLING = (1,)
SWAP_ROWS = 512
WIN_ROWS = 592


def _pair_swap(arrs, other_layer, name):
    n_arr = len(arrs)
    shapes = [a.shape[-2:] for a in arrs]
    chunks = []
    for t, (rows, _) in enumerate(shapes):
        assert rows % 16 == 0
        for j, r0 in enumerate(range(0, rows, SWAP_ROWS)):
            chunks.append((t, r0, min(SWAP_ROWS, rows - r0), j % 2))

    def body(*refs):
        ins, outs = refs[:n_arr], refs[n_arr:2 * n_arr]
        bufs = refs[2 * n_arr:3 * n_arr]
        load_sems, send_sems, recv_sems = refs[3 * n_arr:]
        sibling = _peer(1)
        c = lax.axis_index("c")

        def load(k):
            t, r0, n, slot = chunks[k]
            src = ins[t].at[1 - c] if other_layer else ins[t]
            return pltpu.make_async_copy(src.at[pl.ds(r0, n)], bufs[t].at[slot, pl.ds(0, n)], load_sems.at[t, slot])

        def send(k):
            t, r0, n, slot = chunks[k]
            return pltpu.make_async_remote_copy(src_ref=bufs[t].at[slot, pl.ds(0, n)], dst_ref=outs[t].at[pl.ds(r0, n)],
                                                send_sem=send_sems.at[t, slot], recv_sem=recv_sems.at[t],
                                                device_id=sibling, device_id_type=MESH)

        in_flight = {}

        def start_load(k):
            key = (chunks[k][0], chunks[k][3])
            if key in in_flight:
                send(in_flight.pop(key)).wait_send()
            load(k).start()

        start_load(0)
        for k in range(len(chunks)):
            load(k).wait()
            if k + 1 < len(chunks):
                start_load(k + 1)
            send(k).start()
            in_flight[(chunks[k][0], chunks[k][3])] = k
        for k in in_flight.values():
            send(k).wait_send()
        for t in range(n_arr):
            pltpu.make_async_remote_copy(src_ref=outs[t], dst_ref=outs[t], send_sem=send_sems.at[t, 0],
                                         recv_sem=recv_sems.at[t], device_id=sibling, device_id_type=MESH).wait_recv()

    hbm = pl.BlockSpec(memory_space=pl.ANY)
    outs = pl.pallas_call(
        body, name=name, in_specs=[hbm] * n_arr, out_specs=[hbm] * n_arr,
        out_shape=[jax.ShapeDtypeStruct(s, a.dtype) for s, a in zip(shapes, arrs)],
        scratch_shapes=[pltpu.VMEM((2, min(SWAP_ROWS, s[0]), s[1]), a.dtype) for s, a in zip(shapes, arrs)]
        + [pltpu.SemaphoreType.DMA((n_arr, 2)), pltpu.SemaphoreType.DMA((n_arr, 2)), pltpu.SemaphoreType.DMA((n_arr,))],
        compiler_params=pltpu.CompilerParams(vmem_limit_bytes=48 * 1024 * 1024),
    )(*arrs)
    return list(outs)


def _sum_lead(a, name, out_dtype):
    n = a.shape[0]
    shape = a.shape[1:]

    def body(a_ref, o_ref):
        acc = a_ref[0].astype(F32)
        for k in range(1, n):
            acc = acc + a_ref[k].astype(F32)
        o_ref[...] = acc.astype(out_dtype)

    if len(shape) == 3:
        blk = (1,) + shape[1:]
        return pl.pallas_call(
            body, name=name, grid=(shape[0],), in_specs=[pl.BlockSpec((n,) + blk, lambda i: (0, i, 0, 0))],
            out_specs=pl.BlockSpec(blk, lambda i: (i, 0, 0)), out_shape=jax.ShapeDtypeStruct(shape, out_dtype),
            compiler_params=_cparams(1),
        )(a)
    rows, cols = shape
    rb = rows
    for cand in (512, 256, 128):
        if rows % cand == 0 and rows > cand:
            rb = cand
            break
    return pl.pallas_call(
        body, name=name, grid=(rows // rb,), in_specs=[pl.BlockSpec((n, rb, cols), lambda i: (0, i, 0))],
        out_specs=pl.BlockSpec((rb, cols), lambda i: (i, 0)), out_shape=jax.ShapeDtypeStruct((rows, cols), out_dtype),
        compiler_params=_cparams(1),
    )(a)


def _pair_sum(g, recv, layer, name, out_dtype):
    _, n, r, c = g.shape

    def body(l_ref, g_ref, r_ref, o_ref):
        o_ref[...] = (g_ref[0].astype(F32) + r_ref[...].astype(F32)).astype(out_dtype)

    return pl.pallas_call(
        body, name=name,
        grid_spec=pltpu.PrefetchScalarGridSpec(
            num_scalar_prefetch=1, grid=(n,),
            in_specs=[pl.BlockSpec((1, 1, r, c), lambda i, l: (l[0], i, 0, 0)), pl.BlockSpec((1, r, c), lambda i, l: (i, 0, 0))],
            out_specs=pl.BlockSpec((1, r, c), lambda i, l: (i, 0, 0))),
        out_shape=jax.ShapeDtypeStruct((n, r, c), out_dtype), compiler_params=_cparams(1),
    )(layer, g, recv)


def _tn_matmul(a, b, name, col_major=False, into=None, layer=0):
    t, k = a.shape
    n = b.shape[1]
    tb = _tblock(t, 1024)
    kb = min(k, 1024)
    nb = min(n, 1024)
    grid = (k // kb, n // nb, t // tb)
    lead = (into is not None) + col_major

    def body(a_ref, b_ref, *rest):
        o_ref = rest[-1]
        for _ in range(lead):
            o_ref = o_ref.at[0]

        @pl.when(pl.program_id(2) == 0)
        def _():
            o_ref[...] = jnp.zeros_like(o_ref)

        o_ref[...] += _bdot(a_ref[...], b_ref[...], TN)

    if col_major:
        block, index, shape = (1, kb, nb), (lambda ki, ni: (ni, ki, 0)), (n // nb, k, nb)
    else:
        block, index, shape = (kb, nb), (lambda ki, ni: (ki, ni)), (k, n)
    in_specs = [pl.BlockSpec((tb, kb), lambda ki, ni, ti: (ti, ki)), pl.BlockSpec((tb, nb), lambda ki, ni, ti: (ti, ni))]
    if into is None:
        return pl.pallas_call(
            body, name=name, grid=grid, in_specs=in_specs, out_specs=pl.BlockSpec(block, lambda ki, ni, ti: index(ki, ni)),
            out_shape=jax.ShapeDtypeStruct(shape, F32), compiler_params=_cparams(3),
        )(a, b)
    assert into.shape == (2,) + shape
    return pl.pallas_call(
        body, name=name, grid=grid, in_specs=in_specs + [pl.BlockSpec(memory_space=pl.ANY)],
        out_specs=pl.BlockSpec((1,) + block, lambda ki, ni, ti: (layer,) + index(ki, ni)),
        out_shape=jax.ShapeDtypeStruct(into.shape, F32), input_output_aliases={2: 0}, compiler_params=_cparams(3),
    )(a, b, into)


def _sum_many(arrs, name):
    k = len(arrs)

    def body(*refs):
        for a_ref, o_ref in zip(refs[:k], refs[k:]):
            acc = a_ref[0]
            for j in range(1, a_ref.shape[0]):
                acc = acc + a_ref[j]
            o_ref[...] = acc

    return pl.pallas_call(body, name=name, grid=(1,), in_specs=[_full(a.shape) for a in arrs],
                          out_specs=[_full(a.shape[1:]) for a in arrs],
                          out_shape=[jax.ShapeDtypeStruct(a.shape[1:], F32) for a in arrs], compiler_params=_cparams(1))(*arrs)


def _adamw_math(w, g, m, v):
    m2 = ADAM_B1 * m + (1.0 - ADAM_B1) * g
    v2 = ADAM_B2 * v + (1.0 - ADAM_B2) * (g * g)
    m_hat = m2 / (1.0 - ADAM_B1 ** ADAM_STEP)
    v_hat = v2 / (1.0 - ADAM_B2 ** ADAM_STEP)
    return -ADAM_LR * (m_hat / (jnp.sqrt(v_hat) + ADAM_EPS) + ADAM_WD * w), m2, v2


def _adamw_many(ws, gs, ms, vs, name):
    n = len(ws)

    def body(*refs):
        ins, outs = refs[:4 * n], refs[4 * n:]
        for k in range(n):
            res = _adamw_math(ins[k][...], ins[n + k][...], ins[2 * n + k][...], ins[3 * n + k][...])
            for j in range(3):
                outs[3 * k + j][...] = res[j]

    out_shape = []
    for a in ws:
        out_shape += [jax.ShapeDtypeStruct(a.shape, F32)] * 3
    return pl.pallas_call(body, name=name, grid=(1,), in_specs=[_full(a.shape) for a in ws] * 4,
                          out_specs=[_full(s.shape) for s in out_shape], out_shape=out_shape,
                          compiler_params=_cparams(1))(*ws, *gs, *ms, *vs)


def _adamw(w, g, m, v, name):
    shape = w.shape
    cols = shape[-1]
    rows = int(math.prod(shape[:-1]))
    rb = rows
    for cand in (256, 128, 64, 32, 16, 8):
        if rows % cand == 0 and rows > cand:
            rb = cand
            break
    bc1 = 1.0 - ADAM_B1 ** ADAM_STEP
    bc2 = 1.0 - ADAM_B2 ** ADAM_STEP

    def body(w_ref, g_ref, m_ref, v_ref, d_ref, nm_ref, nv_ref):
        gg = g_ref[...]
        m2 = ADAM_B1 * m_ref[...] + (1.0 - ADAM_B1) * gg
        v2 = ADAM_B2 * v_ref[...] + (1.0 - ADAM_B2) * (gg * gg)
        m_hat = m2 / bc1
        v_hat = v2 / bc2
        d_ref[...] = -ADAM_LR * (m_hat / (jnp.sqrt(v_hat) + ADAM_EPS) + ADAM_WD * w_ref[...])
        nm_ref[...] = m2
        nv_ref[...] = v2

    spec = pl.BlockSpec((rb, cols), lambda i: (i, 0))
    sds = jax.ShapeDtypeStruct((rows, cols), F32)
    outs = pl.pallas_call(
        body, name=name, grid=(rows // rb,), in_specs=[spec] * 4, out_specs=[spec] * 3, out_shape=[sds] * 3,
        compiler_params=_cparams(1),
    )(*(z.reshape(rows, cols) for z in (w, g, m, v)))
    return tuple(o.reshape(shape) for o in outs)


def _ada_fwd(c_all, ada_w_sh, ada_b_sh):
    s = ada_w_sh.shape[2]
    sb = 512

    def body(c_ref, w_ref, b_ref, o_ref):
        cond = _silu(c_ref[...])
        o_ref[0] = _bdot(cond, w_ref[0]) + b_ref[0]

    return pl.pallas_call(
        body, name="ada_fwd", grid=(2, s // sb),
        in_specs=[_full((8, D)), pl.BlockSpec((1, D, sb), lambda l, j: (l, 0, j)), pl.BlockSpec((1, 1, sb), lambda l, j: (l, 0, j))],
        out_specs=pl.BlockSpec((1, 8, sb), lambda l, j: (l, 0, j)), out_shape=jax.ShapeDtypeStruct((2, 8, s), F32),
        compiler_params=_cparams(2),
    )(c_all, ada_w_sh, ada_b_sh)


def _ada_bwd(c_all, dmod_sh, dmod_all):
    s = dmod_sh.shape[2]
    sb = 512

    def body(c_ref, d_ref, o_ref):
        cond = _silu(c_ref[...])
        o_ref[0] = _bdot(cond, d_ref[0], TN)

    gw = pl.pallas_call(
        body, name="ada_bwd_w", grid=(2, s // sb),
        in_specs=[_full((8, D)), pl.BlockSpec((1, 8, sb), lambda l, j: (l, 0, j))],
        out_specs=pl.BlockSpec((1, D, sb), lambda l, j: (l, 0, j)), out_shape=jax.ShapeDtypeStruct((2, D, s), F32),
        compiler_params=_cparams(2),
    )(c_all, dmod_sh)

    def body_b(d_ref, o_ref):
        acc = d_ref[0, 0:1, :]
        for k in range(1, 8):
            acc = acc + d_ref[0, k:k + 1, :]
        o_ref[0] = acc

    gb = pl.pallas_call(
        body_b, name="ada_bwd_b", grid=(2,), in_specs=[pl.BlockSpec((1, 8, 6 * D), lambda l: (l, 0, 0))],
        out_specs=pl.BlockSpec((1, 1, 6 * D), lambda l: (l, 0, 0)), out_shape=jax.ShapeDtypeStruct((2, 1, 6 * D), F32),
        compiler_params=_cparams(1),
    )(dmod_all)
    return gw, gb.reshape(2, 6 * D)


def _f_in(h, nw, sc, sh, w_main, w_dt):
    t = h.shape[0]
    tb = _tblock(t)

    def body(h_ref, nw_ref, sc_ref, sh_ref, w_ref, wd_ref, p_ref, dt_ref, u_ref):
        _, n = _rms(h_ref[...])
        u = ((n * nw_ref[...]) * (1.0 + sc_ref[...]) + sh_ref[...]).astype(BF16)
        u_ref[...] = u
        p_ref[...] = _dot(u, w_ref[...], NT)
        dt_ref[...] = _dot(u, wd_ref[...], NT)

    return pl.pallas_call(
        body, name="f_in", grid=(t // tb,),
        in_specs=[pl.BlockSpec((tb, D), lambda i: (i, 0)), _row(D), _row(D), _row(D), _full((PW, D)), _full((DTW, D))],
        out_specs=[pl.BlockSpec((tb, PW), lambda i: (i, 0)), pl.BlockSpec((tb, DTW), lambda i: (i, 0)),
                   pl.BlockSpec((tb, D), lambda i: (i, 0))],
        out_shape=[jax.ShapeDtypeStruct((t, PW), F32), jax.ShapeDtypeStruct((t, DTW), F32), jax.ShapeDtypeStruct((t, D), BF16)],
        compiler_params=_cparams(1),
    )(h, nw, sc, sh, w_main, w_dt)


def _b_in_du(dab, dz, dxbc, ds5, ddt, w_main, w_dt):
    t = dab.shape[0]
    tb = _tblock(t)

    def body(a_ref, z_ref, x_ref, s_ref, d_ref, w_ref, wd_ref, o_ref):
        acc = _bdot(a_ref[...], w_ref[0:1024, :])
        acc += _bdot(z_ref[...], w_ref[1024:1280, :])
        acc += _bdot(s_ref[...], w_ref[1280:1536, :])
        acc += _bdot(x_ref[...], w_ref[1536:2304, :])
        acc += _bdot(d_ref[...], wd_ref[...])
        o_ref[...] = acc

    blk = lambda n: pl.BlockSpec((tb, n), lambda i: (i, 0))
    return pl.pallas_call(
        body, name="b_in_du", grid=(t // tb,),
        in_specs=[blk(1024), blk(256), blk(768), blk(256), blk(DTW), _full((PW, D)), _full((DTW, D))],
        out_specs=blk(D), out_shape=jax.ShapeDtypeStruct((t, D), F32), compiler_params=_cparams(1),
    )(dab, dz, dxbc, ds5, ddt, w_main, w_dt)


def _b_normmod(du, x, dres, gated, nw, sc, name):
    t = x.shape[0]
    tb = _tblock(t)

    def body(du_ref, x_ref, dr_ref, g_ref, nw_ref, sc_ref, dx_ref, dsc_ref, dsh_ref, dnw_ref, dg_ref):
        @pl.when(pl.program_id(0) == 0)
        def _():
            for r in (dsc_ref, dsh_ref, dnw_ref, dg_ref):
                r[...] = jnp.zeros_like(r)

        du_v = du_ref[...]
        r, n = _rms(x_ref[...])
        nwv = nw_ref[...]
        scale = 1.0 + sc_ref[...]
        dsc_ref[...] += _colsum(du_v * (n * nwv))
        dsh_ref[...] += _colsum(du_v)
        dnw_ref[...] += _colsum(du_v * scale * n)
        dres_v = dr_ref[...]
        dg_ref[...] += _colsum(dres_v * g_ref[...])
        dx_ref[...] = dres_v + _rms_bwd(r, n, du_v * scale * nwv)

    blk = pl.BlockSpec((tb, D), lambda i: (i, 0))
    row = jax.ShapeDtypeStruct((1, D), F32)
    return pl.pallas_call(
        body, name=name, grid=(t // tb,), in_specs=[blk, blk, blk, blk, _row(D), _row(D)],
        out_specs=[blk, _row(D), _row(D), _row(D), _row(D)], out_shape=[jax.ShapeDtypeStruct((t, D), F32), row, row, row, row],
        compiler_params=_cparams(1),
    )(du, x, dres, gated, nw, sc)


HALO = 16


def _lane_group(shape):
    return lax.broadcasted_iota(jnp.int32, shape, 1) // 64


def _window_select(g, s2, s4, s8, s16):
    return jnp.where(g == 0, s2, jnp.where(g == 1, s4, jnp.where(g == 2, s8, s16)))


def _pool_count(t0, rows):
    g = _lane_group((rows, GW))
    win = _window_select(g, 2, 4, 8, 16)
    tt = t0 + lax.broadcasted_iota(jnp.int32, (rows, GW), 0)
    return jnp.minimum(tt + 1, win).astype(F32)


def _pool_p(v_ext, t0, tb):
    s2 = v_ext + _roll(v_ext, 1)
    s4 = s2 + _roll(s2, 2)
    s8 = s4 + _roll(s4, 4)
    s16 = s8 + _roll(s8, 8)
    ws = _window_select(_lane_group(v_ext.shape), s2, s4, s8, s16)[HALO:]
    return ws / _pool_count(t0, tb) - v_ext[HALO:]


def _sconv(q_ext, w):
    return (_roll(q_ext, 2) * w[0:1] + _roll(q_ext, 1) * w[1:2] + q_ext * w[2:3])[HALO:]


def _halo_specs(t, tb, cols, col_block):
    per = tb // HALO
    last = t // HALO - 1
    prev = pl.BlockSpec((HALO, cols), lambda i: (jnp.maximum(i * per - 1, 0), col_block))
    nxt = pl.BlockSpec((HALO, cols), lambda i: (jnp.minimum((i + 1) * per, last), col_block))
    return prev, nxt


def _f_ab(proj, pool_mat, pool_scale, sconv_w):
    t = proj.shape[0]
    tb = _tblock(t)
    prev, _ = _halo_specs(t, tb, 1024, 0)

    def body(p_ref, h_ref, pm_ref, ps_ref, sw_ref, ya_ref, yb_ref):
        i = pl.program_id(0)
        halo = jnp.where(i > 0, h_ref[...], 0.0)
        ext = jnp.concatenate([halo, p_ref[...]], axis=0)
        p = _pool_p(ext[:, 0:256], i * tb, tb)
        ya_ref[...] = _bdot(p, pm_ref[...]) * ps_ref[...]
        q_ext = ext[:, 512:768] * ext[:, 768:1024]
        yb_ref[...] = p_ref[:, 256:512] * _sconv(q_ext, sw_ref[...])

    blk = pl.BlockSpec((tb, GW), lambda i: (i, 0))
    sds = jax.ShapeDtypeStruct((t, GW), F32)
    return pl.pallas_call(
        body, name="f_ab", grid=(t // tb,),
        in_specs=[pl.BlockSpec((tb, 1024), lambda i: (i, 0)), prev, _full((GW, GW)), _row(GW), _full((3, GW))],
        out_specs=[blk, blk], out_shape=[sds, sds], compiler_params=_cparams(1),
    )(proj, proj, pool_mat, pool_scale, sconv_w)


def _b_ab(proj, dya, dyb, pool_mat, pool_scale, sconv_w):
    t = proj.shape[0]
    tb = _tblock(t)
    nb = t // tb
    prev, nxt = _halo_specs(t, tb, 1024, 0)
    _, nxt_g = _halo_specs(t, tb, GW, 0)
    n_ext = tb + HALO

    def body(p_ref, hp_ref, hn_ref, da_ref, dan_ref, db_ref, dbn_ref, pm_ref, ps_ref, sw_ref,
             o_ref, dpm_ref, dps_ref, dsw_ref):
        i = pl.program_id(0)

        @pl.when(i == 0)
        def _():
            for r in (dpm_ref, dps_ref, dsw_ref):
                r[...] = jnp.zeros_like(r)

        last = i == nb - 1
        halo = jnp.where(i > 0, hp_ref[...], 0.0)
        main = p_ref[...]
        ext = jnp.concatenate([halo, main], axis=0)
        scale = ps_ref[...]
        pm = pm_ref[...]
        p = _pool_p(ext[:, 0:256], i * tb, tb)
        da = da_ref[...]
        dps_ref[...] += _colsum(da * _bdot(p, pm))
        da_ext = jnp.concatenate([da, jnp.where(last, 0.0, dan_ref[...])], axis=0)
        dys = da_ext * scale
        dpm_ref[...] += _bdot(p, dys[:tb], TN)
        dp = _bdot(dys, pm, NT)
        dpc = dp / _pool_count(i * tb, n_ext)
        a2 = dpc + _roll(dpc, n_ext - 1)
        a4 = a2 + _roll(a2, n_ext - 2)
        a8 = a4 + _roll(a4, n_ext - 4)
        a16 = a8 + _roll(a8, n_ext - 8)
        o_ref[:, 0:256] = (_window_select(_lane_group(dpc.shape), a2, a4, a8, a16) - dp)[:tb]
        w = sw_ref[...]
        gb, gc, hh = main[:, 256:512], main[:, 512:768], main[:, 768:1024]
        q_ext = ext[:, 512:768] * ext[:, 768:1024]
        db = db_ref[...]
        o_ref[:, 256:512] = db * _sconv(q_ext, w)
        gb_next = hn_ref[:, 256:512]
        dconv = jnp.concatenate([db * gb, jnp.where(last, 0.0, dbn_ref[...] * gb_next)], axis=0)
        dq = (dconv * w[2:3] + _roll(dconv, n_ext - 1) * w[1:2] + _roll(dconv, n_ext - 2) * w[0:1])[:tb]
        o_ref[:, 512:768] = dq * hh
        o_ref[:, 768:1024] = dq * gc
        dc = dconv[:tb]
        dsw_ref[0:1, :] += _colsum(dc * _roll(q_ext, 2)[HALO:])
        dsw_ref[1:2, :] += _colsum(dc * _roll(q_ext, 1)[HALO:])
        dsw_ref[2:3, :] += _colsum(dc * q_ext[HALO:])

    blk = pl.BlockSpec((tb, GW), lambda i: (i, 0))
    return pl.pallas_call(
        body, name="b_ab", grid=(nb,),
        in_specs=[pl.BlockSpec((tb, 1024), lambda i: (i, 0)), prev, nxt, blk, nxt_g, blk, nxt_g,
                  _full((GW, GW)), _row(GW), _full((3, GW))],
        out_specs=[pl.BlockSpec((tb, 1024), lambda i: (i, 0)), _full((GW, GW)), _row(GW), _full((3, GW))],
        out_shape=[jax.ShapeDtypeStruct((t, 1024), F32), jax.ShapeDtypeStruct((GW, GW), F32),
                   jax.ShapeDtypeStruct((1, GW), F32), jax.ShapeDtypeStruct((3, GW), F32)],
        compiler_params=_cparams(1),
    )(proj, proj, proj, dya, dya, dyb, dyb, pool_mat, pool_scale, sconv_w)


CH = 8


def _ssd_conv(x, halo, w, b):
    ext = jnp.concatenate([halo, x], axis=0)
    pre = ext * w[3:4] + _roll(ext, 1) * w[2:3] + _roll(ext, 2) * w[1:2] + _roll(ext, 3) * w[0:1] + b
    return pre[CH:], ext


def _ssd_common(dt_raw, dtb, alog):
    ll = dt_raw.shape[0]
    dtv = _softplus(dt_raw + dtb)
    a_row = -jnp.exp(alog)
    r = lax.broadcasted_iota(jnp.int32, (ll, ll), 0)
    c = lax.broadcasted_iota(jnp.int32, (ll, ll), 1)
    tril = (r >= c).astype(F32)
    cs = _dot(tril, dtv * a_row, prec=HI)
    return dtv, a_row, cs, cs.T, r >= c


def _bd(a, b, ca, cb):
    return lax.dot_general(a, b, (((ca,), (cb,)), ((0,), (0,))), preferred_element_type=F32)


def _head_cols(m):
    return jnp.stack([m[:, h:h + 1] for h in range(NH)])


def _ssd_heads(act, dtv, cs, cs_t, causal):
    xs = jnp.stack([act[:, HP * h:HP * (h + 1)] for h in range(NH)])
    bm = jnp.stack([act[:, 256 + NS * (h // 2):256 + NS * (h // 2 + 1)] for h in range(NH)])
    cm = jnp.stack([act[:, 512 + NS * (h // 2):512 + NS * (h // 2 + 1)] for h in range(NH)])
    cs_c = _head_cols(cs)
    cs_r = jnp.stack([cs_t[h:h + 1, :] for h in range(NH)])
    mdec = jnp.where(causal[None], jnp.exp(jnp.minimum(cs_c - cs_r, 0.0)), 0.0)
    g2 = _bd(jnp.stack([cm[0], cm[2]]), jnp.stack([bm[0], bm[2]]), 2, 2)
    sc = jnp.stack([g2[h // 2] for h in range(NH)]) * mdec
    dt_c = _head_cols(dtv)
    xdt = xs * dt_c
    e = jnp.exp(cs_c)
    cs_last = cs_c[:, SSD_L - 1:SSD_L, :]
    wdec = jnp.exp(cs_last - cs_c)
    return xs, bm, cm, mdec, sc, dt_c, xdt, e, cs_last, wdec


def _head_scalars(row_ref):
    return jnp.stack([row_ref[0:1, h:h + 1] for h in range(NH)])


def _f_ssd(proj, dtp, conv_w, conv_b, dt_bias, a_log, d_skip):
    t = proj.shape[0]
    nc = t // SSD_L
    rows = SSD_SUB * SSD_L
    per = rows // CH

    def body(x_ref, hx_ref, dt_ref, z_ref, cw_ref, cb_ref, dtb_ref, al_ref, dk_ref, y_ref, yp_ref, sp_ref, s_ref):
        i = pl.program_id(0)

        @pl.when(i == 0)
        def _():
            s_ref[...] = jnp.zeros_like(s_ref)

        state = s_ref[...]
        dk = _head_scalars(dk_ref)
        for sub in range(SSD_SUB):
            r0 = sub * SSD_L
            rs = slice(r0, r0 + SSD_L)
            halo = jnp.where(i > 0, hx_ref[...], 0.0) if sub == 0 else x_ref[r0 - CH:r0, :]
            pre, _ = _ssd_conv(x_ref[rs, :], halo, cw_ref[...], cb_ref[...])
            act = _silu(pre)
            dtv, _, cs, cs_t, causal = _ssd_common(dt_ref[rs, :], dtb_ref[...], al_ref[...])
            xs, bm, cm, _, sc, _, xdt, e, cs_last, wdec = _ssd_heads(act, dtv, cs, cs_t, causal)
            sp_ref[sub] = state
            y = _bd(sc, xdt, 2, 1) + e * _bd(cm, state, 2, 2) + xs * dk
            for h in range(NH):
                yp_ref[rs, HP * h:HP * (h + 1)] = y[h]
            state = state * jnp.exp(cs_last) + _bd(xdt * wdec, bm, 1, 1)
            y_ref[rs, :] = yp_ref[rs, :] * _silu(z_ref[rs, :])
        s_ref[...] = state

    blk = pl.BlockSpec((rows, GW), lambda i: (i, 0))
    sds = jax.ShapeDtypeStruct((t, GW), F32)
    return pl.pallas_call(
        body, name="f_ssd", grid=(nc // SSD_SUB,),
        in_specs=[pl.BlockSpec((rows, 768), lambda i: (i, 2)),
                  pl.BlockSpec((CH, 768), lambda i: (jnp.maximum(i * per - 1, 0), 2)),
                  pl.BlockSpec((rows, DTW), lambda i: (i, 0)),
                  pl.BlockSpec((rows, GW), lambda i: (i, 4)),
                  _full((4, 768)), _row(768), _row(DTW), _row(DTW), _row(DTW)],
        out_specs=[blk, blk, pl.BlockSpec((SSD_SUB, NH, HP, NS), lambda i: (i, 0, 0, 0))],
        out_shape=[sds, sds, jax.ShapeDtypeStruct((nc, NH, HP, NS), F32)],
        scratch_shapes=[pltpu.VMEM((NH, HP, NS), F32)], compiler_params=_cparams(1),
    )(proj, proj, dtp, proj, conv_w, conv_b, dt_bias, a_log, d_skip)


def _b_ssd(proj, dtp, ypre, dyc, sprev, conv_w, conv_b, dt_bias, a_log, d_skip):
    t = proj.shape[0]
    nc = t // SSD_L
    steps = nc // SSD_SUB_BWD
    rows = SSD_SUB_BWD * SSD_L
    per = rows // CH
    n_ext = SSD_L + CH

    def chunk(sub, halo, dnext, ds_in, refs):
        (x_ref, dt_ref, z_ref, yp_ref, dy_ref, sp_ref, cw_ref, cb_ref, dtb_ref, al_ref, dk_ref,
         dz_ref, dx_ref, ddt_ref, dact_ref) = refs
        rs = slice(sub * SSD_L, (sub + 1) * SSD_L)
        dact = dact_ref.at[sub]
        w = cw_ref[...]
        pre, ext = _ssd_conv(x_ref[rs, :], halo, w, cb_ref[...])
        act = _silu(pre)
        dt_raw = dt_ref[rs, :]
        dtv, a_row, cs, cs_t, causal = _ssd_common(dt_raw, dtb_ref[...], al_ref[...])
        z = z_ref[rs, :]
        dyc_v = dy_ref[rs, :]
        dz_ref[rs, :] = dyc_v * yp_ref[rs, :] * _dsilu(z)
        dy_all = dyc_v * _silu(z)
        lane = lax.broadcasted_iota(jnp.int32, (SSD_L, DTW), 1)
        rowi = lax.broadcasted_iota(jnp.int32, (1, SSD_L, 1), 1)
        lane1 = lax.broadcasted_iota(jnp.int32, (1, DTW), 1)
        xs, bm, cm, mdec, sc, dt_c, xdt, e, cs_last, wdec = _ssd_heads(act, dtv, cs, cs_t, causal)
        dy = jnp.stack([dy_all[:, HP * h:HP * (h + 1)] for h in range(NH)])
        prev = sp_ref[sub]
        ds = ds_in
        lsum = lambda v: jnp.sum(v, axis=2, keepdims=True)
        dsc = _bd(dy, xdt, 2, 2)
        q = dsc * sc
        dg = dsc * mdec
        dxdt = _bd(sc, dy, 1, 1)
        dcs = lsum(q) - lsum(jnp.swapaxes(q, 1, 2))
        dc = _bd(dg, bm, 2, 1)
        db = _bd(dg, cm, 1, 1)
        cp = _bd(cm, prev, 2, 2)
        dcs += lsum(dy * cp) * e
        ey = e * dy
        dc += _bd(ey, prev, 2, 1)
        dprev = _bd(ey, cm, 1, 1)
        elast = jnp.exp(cs_last)
        dprev += ds * elast
        dcs_last = jnp.sum(lsum(ds * prev), axis=1, keepdims=True) * elast
        bds = _bd(bm, ds, 2, 2)
        dxdt += wdec * bds
        db += wdec * _bd(xdt, ds, 2, 1)
        dw = lsum(xdt * bds) * wdec
        dcs -= dw
        dcs_last += jnp.sum(dw, axis=1, keepdims=True)
        dcs += jnp.where(rowi == SSD_L - 1, dcs_last, 0.0)
        dxs = dxdt * dt_c + dy * _head_scalars(dk_ref)
        ddtx = lsum(dxdt * xs)
        ddk = jnp.sum(lsum(dy * xs), axis=1, keepdims=True)
        dcs_mat = jnp.zeros((SSD_L, DTW), F32)
        ddtx_mat = jnp.zeros((SSD_L, DTW), F32)
        ddk_row = jnp.zeros((1, DTW), F32)
        for h in range(NH):
            dact[:, HP * h:HP * (h + 1)] = dxs[h]
            dcs_mat = jnp.where(lane == h, dcs[h], dcs_mat)
            ddtx_mat = jnp.where(lane == h, ddtx[h], ddtx_mat)
            ddk_row = jnp.where(lane1 == h, ddk[h], ddk_row)
        for g in range(2):
            dact[:, 256 + NS * g:256 + NS * (g + 1)] = db[2 * g] + db[2 * g + 1]
            dact[:, 512 + NS * g:512 + NS * (g + 1)] = dc[2 * g] + dc[2 * g + 1]
        ds_out = dprev
        r2 = lax.broadcasted_iota(jnp.int32, (SSD_L, SSD_L), 0)
        c2 = lax.broadcasted_iota(jnp.int32, (SSD_L, SSD_L), 1)
        dadt = _dot((c2 >= r2).astype(F32), dcs_mat, prec=HI)
        ddt = jnp.where(lane < NH, (dadt * a_row + ddtx_mat) * _sig(dt_raw + dtb_ref[...]), 0.0)
        ddt_ref[rs, :] = ddt
        dpre = dact[...] * _dsilu(pre)
        dcw = jnp.concatenate([_colsum(dpre * _roll(ext, 3 - k)[CH:]) for k in range(4)], axis=0)
        dext = jnp.concatenate([dpre, dnext], axis=0)
        dx_ref[rs, :] = (dext * w[3:4] + _roll(dext, n_ext - 1) * w[2:3] + _roll(dext, n_ext - 2) * w[1:2]
                         + _roll(dext, n_ext - 3) * w[0:1])[:SSD_L]
        acc = (dcw, _colsum(dpre), _colsum(ddt), _colsum(dadt * dtv) * a_row, ddk_row)
        return dpre[0:CH], ds_out, acc

    def body(x_ref, hx_ref, dt_ref, z_ref, yp_ref, dy_ref, sp_ref, cw_ref, cb_ref, dtb_ref, al_ref, dk_ref,
             dz_ref, dx_ref, ddt_ref, dcw_ref, dcb_ref, ddtb_ref, dal_ref, ddk_ref, ds_ref, dnext_ref, dact_ref):
        i = pl.program_id(0)
        acc_refs = (dcw_ref, dcb_ref, ddtb_ref, dal_ref, ddk_ref)

        @pl.when(i == 0)
        def _():
            ds_ref[...] = jnp.zeros_like(ds_ref)
            dnext_ref[...] = jnp.zeros_like(dnext_ref)
            for r in acc_refs:
                r[...] = jnp.zeros_like(r)

        refs = (x_ref, dt_ref, z_ref, yp_ref, dy_ref, sp_ref, cw_ref, cb_ref, dtb_ref, al_ref, dk_ref, dz_ref, dx_ref, ddt_ref,
                dact_ref)
        ds = ds_ref[...]
        dnext = dnext_ref[...]
        total = None
        for sub in reversed(range(SSD_SUB_BWD)):
            if sub == 0:
                halo = jnp.where(i == steps - 1, 0.0, hx_ref[...])
            else:
                halo = x_ref[sub * SSD_L - CH:sub * SSD_L, :]
            dnext, ds, acc = chunk(sub, halo, dnext, ds, refs)
            total = acc if total is None else tuple(a + b for a, b in zip(total, acc))
        ds_ref[...] = ds
        dnext_ref[...] = dnext
        for r, v in zip(acc_refs, total):
            r[...] += v

    rev = lambda i: steps - 1 - i
    blk = lambda n, cb=0: pl.BlockSpec((rows, n), lambda i: (rev(i), cb))
    row = lambda n: jax.ShapeDtypeStruct((1, n), F32)
    return pl.pallas_call(
        body, name="b_ssd", grid=(steps,),
        in_specs=[blk(768, 2), pl.BlockSpec((CH, 768), lambda i: (jnp.maximum(rev(i) * per - 1, 0), 2)),
                  blk(DTW), blk(GW, 4), blk(GW), blk(GW), pl.BlockSpec((SSD_SUB_BWD, NH, HP, NS), lambda i: (rev(i), 0, 0, 0)),
                  _full((4, 768)), _row(768), _row(DTW), _row(DTW), _row(DTW)],
        out_specs=[blk(GW), blk(768), blk(DTW), _full((4, 768)), _row(768), _row(DTW), _row(DTW), _row(DTW)],
        out_shape=[jax.ShapeDtypeStruct((t, GW), F32), jax.ShapeDtypeStruct((t, 768), F32), jax.ShapeDtypeStruct((t, DTW), F32),
                   jax.ShapeDtypeStruct((4, 768), F32), row(768), row(DTW), row(DTW), row(DTW)],
        scratch_shapes=[pltpu.VMEM((NH, HP, NS), F32), pltpu.VMEM((CH, 768), F32), pltpu.VMEM((SSD_SUB_BWD, SSD_L, 768), F32)],
        compiler_params=_cparams(1),
    )(proj, proj, dtp, proj, ypre, dyc, sprev, conv_w, conv_b, dt_bias, a_log, d_skip)


def _s5_block(t):
    return min(t, 256)


def _seg_t():
    r = lax.broadcasted_iota(jnp.int32, (64, 1024), 0)
    c = lax.broadcasted_iota(jnp.int32, (64, 1024), 1)
    return (c // 16 == r).astype(F32)


def _s5_prep_math(a_re, a_im, lstep, b_re, b_im):
    step = jnp.exp(lstep)
    ars = a_re * step
    ais = a_im * step
    mag = jnp.exp(ars)
    lr = mag * jnp.cos(ais)
    li = mag * jnp.sin(ais)
    den = a_re * a_re + a_im * a_im
    nr = lr - 1.0
    f_re = (nr * a_re + li * a_im) / den
    f_im = (li * a_re - nr * a_im) / den
    seg = _seg_t()
    fr = _dot(f_re, seg, prec=HI)
    fi = _dot(f_im, seg, prec=HI)
    return lr, li, fr * b_re - fi * b_im, fr * b_im + fi * b_re, ars, ais


def _s5_prep(a_re, a_im, lstep, b_re, b_im):
    def body(ar, ai, ls, br, bi, lr_o, li_o, bbr_o, bbi_o, ars_o, ais_o):
        outs = _s5_prep_math(ar[...], ai[...], ls[...], br[...], bi[...])
        for o, v in zip((lr_o, li_o, bbr_o, bbi_o, ars_o, ais_o), outs):
            o[...] = v

    s64 = jax.ShapeDtypeStruct((16, 64), F32)
    s1k = jax.ShapeDtypeStruct((16, 1024), F32)
    return pl.pallas_call(body, name="s5_prep", out_shape=[s64, s64, s1k, s1k, s64, s64])(a_re, a_im, lstep, b_re, b_im)


def _s5_prep_bwd(a_re, a_im, lstep, b_re, b_im, dlr, dli, dbbr, dbbi):
    def body(ar, ai, ls, br, bi, g0, g1, g2, g3, o0, o1, o2, o3, o4):
        f = lambda *a: _s5_prep_math(*a)[:4]
        _, vjp = jax.vjp(f, ar[...], ai[...], ls[...], br[...], bi[...])
        for o, v in zip((o0, o1, o2, o3, o4), vjp((g0[...], g1[...], g2[...], g3[...]))):
            o[...] = v

    s64 = jax.ShapeDtypeStruct((16, 64), F32)
    s1k = jax.ShapeDtypeStruct((16, 1024), F32)
    return pl.pallas_call(body, name="s5_prep_bwd", out_shape=[s64, s64, jax.ShapeDtypeStruct((16, 1), F32), s1k, s1k])(
        a_re, a_im, lstep, b_re, b_im, dlr, dli, dbbr, dbbi)


SUB = 8


def _s5_tables(ars, ais):
    def body(ar, ai, tr, ti):
        rr = lax.broadcasted_iota(jnp.int32, (8 * SUB, S5_P), 0)
        seg, r = rr // SUB, rr % SUB
        step = jnp.where((seg == 1) | (seg == 4), 1, jnp.where((seg == 2) | (seg == 5), 2, 4))
        n = jnp.where(seg == 0, r + 1, jnp.where(seg == 7, SUB - r, step))
        fwd_gap = jnp.where(seg <= 3, r - step, SUB - step - 1 - r)
        gap = jnp.where((seg == 0) | (seg == 7), 0, fwd_gap)
        nf = n.astype(F32)
        mag = jnp.where(gap >= 0, jnp.exp(nf * ar[...]), 0.0)
        tr[...] = mag * jnp.cos(nf * ai[...])
        ti[...] = mag * jnp.sin(nf * ai[...])

    sds = jax.ShapeDtypeStruct((8 * SUB, S5_P), F32)
    return pl.pallas_call(body, name="s5_tables", out_shape=[sds] * 2)(ars, ais)


def _s5_table(tb_r, tb_i, k):
    return tb_r[SUB * k:SUB * (k + 1), :], tb_i[SUB * k:SUB * (k + 1), :]


def _s5_scan(bu_r, bu_i, tb_r, tb_i, c_r, c_i, lb):
    nt = lb // SUB
    sr, si = bu_r.reshape(nt, SUB, S5_P), bu_i.reshape(nt, SUB, S5_P)
    for j, k in enumerate((1, 2, 4)):
        mr, mi = _s5_table(tb_r, tb_i, 1 + j)
        tr, ti = pltpu.roll(sr, k, axis=1), pltpu.roll(si, k, axis=1)
        sr, si = sr + mr * tr - mi * ti, si + mr * ti + mi * tr
    pr, pi = _s5_table(tb_r, tb_i, 0)
    out_r, out_i = [], []
    for j in range(nt):
        a_r = sr[j] + pr * c_r - pi * c_i
        a_i = si[j] + pr * c_i + pi * c_r
        out_r.append(a_r)
        out_i.append(a_i)
        c_r, c_i = a_r[SUB - 1:SUB], a_i[SUB - 1:SUB]
    return jnp.concatenate(out_r, axis=0), jnp.concatenate(out_i, axis=0)


def _s5_rscan(g_r, g_i, tb_r, tb_i, n_r, n_i, lb):
    nt = lb // SUB
    gr, gi = g_r.reshape(nt, SUB, S5_P), g_i.reshape(nt, SUB, S5_P)
    for j, k in enumerate((1, 2, 4)):
        mr, mi = _s5_table(tb_r, tb_i, 4 + j)
        tr, ti = pltpu.roll(gr, SUB - k, axis=1), pltpu.roll(gi, SUB - k, axis=1)
        gr, gi = gr + mr * tr + mi * ti, gi + mr * ti - mi * tr
    qr, qi = _s5_table(tb_r, tb_i, 7)
    out_r, out_i = [None] * nt, [None] * nt
    for j in reversed(range(nt)):
        a_r = gr[j] + qr * n_r + qi * n_i
        a_i = gi[j] + qr * n_i - qi * n_r
        out_r[j], out_i[j] = a_r, a_i
        n_r, n_i = a_r[0:1], a_i[0:1]
    return jnp.concatenate(out_r, axis=0), jnp.concatenate(out_i, axis=0)


def _s5_y(u, sr, si, cre, cim, dsk):
    return _bdot(sr, cre) + _bdot(si, cim) + dsk * u


def _f_s5(proj, bmat, cre, cim, p_r, p_i, dsk, glu_w, glu_b):
    t = proj.shape[0]
    lb = _s5_block(t)
    nb = t // lb

    def body(u_ref, bm_ref, cr_ref, ci_ref, pr_ref, pi_ref, dk_ref, gw_ref, gb_ref, y_ref, car_ref, s_ref, st_ref):
        @pl.when(pl.program_id(0) == 0)
        def _():
            st_ref[...] = jnp.zeros_like(st_ref)

        u = u_ref[...]
        bu = _bdot(u, bm_ref[...])
        c_r, c_i = st_ref[0:1, 0:S5_P], st_ref[0:1, S5_P:]
        car_ref[0] = st_ref[0:1, :]
        sr, si = _s5_scan(bu[:, :S5_P], bu[:, S5_P:], pr_ref, pi_ref, c_r, c_i, lb)
        st_ref[0:1, 0:S5_P] = sr[lb - 1:lb]
        st_ref[0:1, S5_P:] = si[lb - 1:lb]
        sr_b, si_b = sr.astype(BF16), si.astype(BF16)
        s_ref[:, 0:S5_P] = sr_b
        s_ref[:, S5_P:] = si_b
        gel = _gelu(_s5_y(u, sr_b, si_b, cr_ref[...], ci_ref[...], dk_ref[...]))
        y_ref[...] = gel * _sig(_bdot(gel, gw_ref[...]) + gb_ref[...])

    return pl.pallas_call(
        body, name="f_s5", grid=(nb,),
        in_specs=[pl.BlockSpec((lb, GW), lambda i: (i, 5)),
                  _full((GW, 2 * S5_P)), _full((S5_P, GW)), _full((S5_P, GW)), _full((8 * SUB, S5_P)), _full((8 * SUB, S5_P)),
                  _row(GW), _full((GW, GW)), _row(GW)],
        out_specs=[pl.BlockSpec((lb, GW), lambda i: (i, 0)), pl.BlockSpec((1, 1, 2 * S5_P), lambda i: (i, 0, 0)),
                   pl.BlockSpec((lb, 2 * S5_P), lambda i: (i, 0))],
        out_shape=[jax.ShapeDtypeStruct((t, GW), F32), jax.ShapeDtypeStruct((nb, 1, 2 * S5_P), F32),
                   jax.ShapeDtypeStruct((t, 2 * S5_P), BF16)],
        scratch_shapes=[pltpu.VMEM((8, 2 * S5_P), F32)], compiler_params=_cparams(1),
    )(proj, bmat, cre, cim, p_r, p_i, dsk, glu_w, glu_b)


def _b_s5(proj, dyd, carries, states, bmat, cre, cim, p_r, p_i, dsk, glu_w, glu_b):
    t = proj.shape[0]
    lb = _s5_block(t)
    nb = t // lb

    def body(u_ref, dy_ref, car_ref, s_ref, bm_ref, cr_ref, ci_ref, pr_ref, pi_ref, dk_ref, gw_ref, gb_ref,
             du_ref, dbm_ref, dcr_ref, dci_ref, dlam_ref, ddk_ref, dgw_ref, dgb_ref, gc_ref):
        @pl.when(pl.program_id(0) == 0)
        def _():
            gc_ref[...] = jnp.zeros_like(gc_ref)
            for r in (dbm_ref, dcr_ref, dci_ref, dlam_ref, ddk_ref, dgw_ref, dgb_ref):
                r[...] = jnp.zeros_like(r)

        u = u_ref[...]
        bm = bm_ref[...]
        u_b = u.astype(BF16)
        c_r, c_i = car_ref[0, 0:1, 0:S5_P], car_ref[0, 0:1, S5_P:]
        cre_v, cim_v, dk, gw = cr_ref[...], ci_ref[...], dk_ref[...], gw_ref[...]
        sr_b, si_b = s_ref[:, 0:S5_P], s_ref[:, S5_P:]
        sr, si = sr_b.astype(F32), si_b.astype(F32)
        y = _dot(sr_b, cre_v) + _dot(si_b, cim_v) + dk * u
        gel = _gelu(y)
        gel_b = gel.astype(BF16)
        gate = _sig(_dot(gel_b, gw) + gb_ref[...])
        dout = dy_ref[...]
        t1 = dout * gel * gate * (1.0 - gate)
        t1_b = t1.astype(BF16)
        dgw_ref[...] += _dot(gel_b, t1_b, TN)
        dgb_ref[...] += _colsum(t1)
        dyv = (dout * gate + _dot(t1_b, gw, NT)) * _dgelu(y)
        dyv_b = dyv.astype(BF16)
        ddk_ref[...] += _colsum(dyv * u)
        dcr_ref[...] += _dot(sr_b, dyv_b, TN)
        dci_ref[...] += _dot(si_b, dyv_b, TN)
        gr = _dot(dyv_b, cre_v, NT)
        gi = _dot(dyv_b, cim_v, NT)
        row = lax.broadcasted_iota(jnp.int32, (lb, S5_P), 0)
        n_r, n_i = gc_ref[0:1, 0:S5_P], gc_ref[0:1, S5_P:]
        gr, gi = _s5_rscan(gr, gi, pr_ref, pi_ref, n_r, n_i, lb)
        gc_ref[0:1, 0:S5_P] = gr[0:1]
        gc_ref[0:1, S5_P:] = gi[0:1]
        gcat = jnp.concatenate([gr, gi], axis=1).astype(BF16)
        dbm_ref[...] += _dot(u_b, gcat, TN)
        du_ref[...] = dyv * dk + _dot(gcat, bm, NT)
        spr = jnp.where(row >= 1, _roll(sr, 1), c_r)
        spi = jnp.where(row >= 1, _roll(si, 1), c_i)
        dlam_ref[0:1, :] += _colsum(gr * spr + gi * spi)
        dlam_ref[1:2, :] += _colsum(gi * spr - gr * spi)

    rev = lambda i: nb - 1 - i
    return pl.pallas_call(
        body, name="b_s5", grid=(nb,),
        in_specs=[pl.BlockSpec((lb, GW), lambda i: (rev(i), 5)), pl.BlockSpec((lb, GW), lambda i: (rev(i), 0)),
                  pl.BlockSpec((1, 1, 2 * S5_P), lambda i: (rev(i), 0, 0)), pl.BlockSpec((lb, 2 * S5_P), lambda i: (rev(i), 0)),
                  _full((GW, 2 * S5_P)), _full((S5_P, GW)), _full((S5_P, GW)), _full((8 * SUB, S5_P)), _full((8 * SUB, S5_P)),
                  _row(GW), _full((GW, GW)), _row(GW)],
        out_specs=[pl.BlockSpec((lb, GW), lambda i: (rev(i), 0)), _full((GW, 2 * S5_P)), _full((S5_P, GW)), _full((S5_P, GW)),
                   _full((2, S5_P)), _row(GW), _full((GW, GW)), _row(GW)],
        out_shape=[jax.ShapeDtypeStruct((t, GW), F32), jax.ShapeDtypeStruct((GW, 2 * S5_P), F32),
                   jax.ShapeDtypeStruct((S5_P, GW), F32), jax.ShapeDtypeStruct((S5_P, GW), F32),
                   jax.ShapeDtypeStruct((2, S5_P), F32), jax.ShapeDtypeStruct((1, GW), F32),
                   jax.ShapeDtypeStruct((GW, GW), F32), jax.ShapeDtypeStruct((1, GW), F32)],
        scratch_shapes=[pltpu.VMEM((8, 2 * S5_P), F32)], compiler_params=_cparams(1),
    )(proj, dyd, carries, states, bmat, cre, cim, p_r, p_i, dsk, glu_w, glu_b)


def _group_norm(ys, bw):
    outs, stats = [], []
    for g, y in enumerate(ys):
        r, n = _rms(y)
        stats.append((r, n))
        outs.append(n * bw[:, GW * g:GW * (g + 1)])
    return jnp.concatenate(outs, axis=1), stats


def _f_out(ya, yb, yc, yd, bw, w_out, h, g1):
    t = h.shape[0]
    tb = _tblock(t)

    def body(a_ref, b_ref, c_ref, d_ref, bw_ref, w_ref, h_ref, g_ref, h2_ref, o_ref, cat_ref):
        cat, _ = _group_norm([a_ref[...], b_ref[...], c_ref[...], d_ref[...]], bw_ref[...])
        catb = cat.astype(BF16)
        cat_ref[...] = catb
        o = _dot(catb, w_ref[...])
        o_ref[...] = o
        h2_ref[...] = h_ref[...] + g_ref[...] * o

    yblk = pl.BlockSpec((tb, GW), lambda i: (i, 0))
    blk = pl.BlockSpec((tb, D), lambda i: (i, 0))
    return pl.pallas_call(
        body, name="f_out", grid=(t // tb,), in_specs=[yblk] * 4 + [_row(D), _full((D, D)), blk, _row(D)],
        out_specs=[blk, blk, blk],
        out_shape=[jax.ShapeDtypeStruct((t, D), F32), jax.ShapeDtypeStruct((t, D), F32), jax.ShapeDtypeStruct((t, D), BF16)],
        compiler_params=_cparams(1),
    )(ya, yb, yc, yd, bw, w_out, h, g1)


def _b_out(dh2, ya, yb, yc, yd, bw, w_out, g1):
    t = dh2.shape[0]
    tb = _tblock(t)

    def body(dh_ref, a_ref, b_ref, c_ref, d_ref, bw_ref, w_ref, g_ref, da_ref, db_ref, dc_ref, dd_ref, do_ref, dbw_ref):
        @pl.when(pl.program_id(0) == 0)
        def _():
            dbw_ref[...] = jnp.zeros_like(dbw_ref)

        do = (dh_ref[...] * g_ref[...]).astype(BF16)
        do_ref[...] = do
        dcat = _dot(do, w_ref[...], NT)
        bw_v = bw_ref[...]
        for g, (y_ref, dy_ref) in enumerate(((a_ref, da_ref), (b_ref, db_ref), (c_ref, dc_ref), (d_ref, dd_ref))):
            r, n = _rms(y_ref[...])
            dc = dcat[:, GW * g:GW * (g + 1)]
            dbw_ref[:, GW * g:GW * (g + 1)] += _colsum(dc * n)
            dy_ref[...] = _rms_bwd(r, n, dc * bw_v[:, GW * g:GW * (g + 1)])

    yblk = pl.BlockSpec((tb, GW), lambda i: (i, 0))
    blk = pl.BlockSpec((tb, D), lambda i: (i, 0))
    ysd = jax.ShapeDtypeStruct((t, GW), F32)
    return pl.pallas_call(
        body, name="b_out", grid=(t // tb,), in_specs=[blk] + [yblk] * 4 + [_row(D), _full((D, D)), _row(D)],
        out_specs=[yblk] * 4 + [blk, _row(D)],
        out_shape=[ysd] * 4 + [jax.ShapeDtypeStruct((t, D), BF16), jax.ShapeDtypeStruct((1, D), F32)],
        compiler_params=_cparams(1),
    )(dh2, ya, yb, yc, yd, bw, w_out, g1)


HB = 512
MLP_ROWS = 1024


def _w1_spec():
    per = HID // 4 // HB
    return pl.BlockSpec((1, D, HB), lambda i, k: (k // per, 0, k % per))


def _f_mlp(h2, nw, sc, sh, g2, w1, w2):
    t = h2.shape[0]
    tb = _tblock(t, MLP_ROWS)
    nk = HID // HB

    def body(h_ref, nw_ref, sc_ref, sh_ref, g_ref, w1_ref, w2_ref, h3_ref, m_ref, a_ref, v_ref):
        k = pl.program_id(1)

        @pl.when(k == 0)
        def _():
            _, n = _rms(h_ref[...])
            v_ref[...] = ((n * nw_ref[...]) * (1.0 + sc_ref[...]) + sh_ref[...]).astype(BF16)
            m_ref[...] = jnp.zeros_like(m_ref)

        a = _dot(v_ref[...], w1_ref[0])
        a_ref[...] = a.astype(BF16)
        ra = jnp.maximum(a, 0.0)
        m_ref[...] += _dot((ra * ra).astype(BF16), w2_ref[...])

        @pl.when(k == nk - 1)
        def _():
            h3_ref[...] = h_ref[...] + g_ref[...] * m_ref[...]

    blk = pl.BlockSpec((tb, D), lambda i, k: (i, 0))
    return pl.pallas_call(
        body, name="f_mlp", grid=(t // tb, nk),
        in_specs=[blk, _row(D), _row(D), _row(D), _row(D), _w1_spec(),
                  pl.BlockSpec((HB, D), lambda i, k: (k, 0))],
        out_specs=[blk, blk, pl.BlockSpec((tb, HB), lambda i, k: (i, k)), blk],
        out_shape=[jax.ShapeDtypeStruct((t, D), F32), jax.ShapeDtypeStruct((t, D), F32), jax.ShapeDtypeStruct((t, HID), BF16),
                   jax.ShapeDtypeStruct((t, D), BF16)],
        compiler_params=_cparams(2),
    )(h2, nw, sc, sh, g2, w1, w2)


def _b_mlp(dh3, a, g2, w1, w2):
    t = dh3.shape[0]
    tb = _tblock(t, MLP_ROWS)
    nk = HID // HB

    def body(dh_ref, a_ref, g_ref, w1_ref, w2_ref, dv_ref, da_ref, act_ref, dm_ref):
        k = pl.program_id(1)
        dm = (dh_ref[...] * g_ref[...]).astype(BF16)

        @pl.when(k == 0)
        def _():
            dm_ref[...] = dm
            dv_ref[...] = jnp.zeros_like(dv_ref)

        ra = jnp.maximum(a_ref[...].astype(F32), 0.0)
        act_ref[...] = (ra * ra).astype(BF16)
        da = (_dot(dm, w2_ref[...], NT) * (2.0 * ra)).astype(BF16)
        da_ref[...] = da
        dv_ref[...] += _dot(da, w1_ref[0], NT)

    blk = pl.BlockSpec((tb, D), lambda i, k: (i, 0))
    hblk = pl.BlockSpec((tb, HB), lambda i, k: (i, k))
    return pl.pallas_call(
        body, name="b_mlp", grid=(t // tb, nk),
        in_specs=[blk, hblk, _row(D), _w1_spec(), pl.BlockSpec((HB, D), lambda i, k: (k, 0))],
        out_specs=[blk, hblk, hblk, blk],
        out_shape=[jax.ShapeDtypeStruct((t, D), F32), jax.ShapeDtypeStruct((t, HID), BF16), jax.ShapeDtypeStruct((t, HID), BF16),
                   jax.ShapeDtypeStruct((t, D), BF16)],
        compiler_params=_cparams(2),
    )(dh3, a, g2, w1, w2)


def _b_final(h, tgt, fw):
    t = h.shape[0]
    tb = _tblock(t)

    def body(h_ref, t_ref, w_ref, dh_ref, loss_ref, dfw_ref):
        @pl.when(pl.program_id(0) == 0)
        def _():
            loss_ref[...] = jnp.zeros_like(loss_ref)
            dfw_ref[...] = jnp.zeros_like(dfw_ref)

        r, n = _rms(h_ref[...])
        wv = w_ref[...]
        err = n * wv - t_ref[...]
        loss_ref[...] += jnp.sum(err * err, keepdims=True) * (0.5 / D)
        dy = err * (1.0 / D)
        dfw_ref[...] += _colsum(dy * n)
        dh_ref[...] = _rms_bwd(r, n, dy * wv)

    blk = pl.BlockSpec((tb, D), lambda i: (i, 0))
    return pl.pallas_call(
        body, name="b_final", grid=(t // tb,), in_specs=[blk, blk, _row(D)], out_specs=[blk, _row(1), _row(D)],
        out_shape=[jax.ShapeDtypeStruct((t, D), F32), jax.ShapeDtypeStruct((1, 1), F32), jax.ShapeDtypeStruct((1, D), F32)],
        compiler_params=_cparams(1),
    )(h, tgt, fw)


_EYE16 = None


def _eye(n):
    return jnp.eye(n, dtype=F32)


def _pool_embed(pool_w):
    return jnp.einsum('gcd,gk->gckd', pool_w, _eye(4)).reshape(GW, GW)


def _pool_extract(m):
    return jnp.einsum('gcgd->gcd', m.reshape(4, 64, 4, 64))


def _bmat_embed(bb):
    return jnp.einsum('gph,gk->ghkp', bb, _eye(16)).reshape(GW, S5_P)


def _bmat_extract(m):
    return jnp.einsum('ghgp->gph', m.reshape(16, 16, 16, 64))


def _cmat_embed(cc):
    return jnp.einsum('ghp,gk->kpgh', cc, _eye(16)).reshape(S5_P, GW)


def _cmat_extract(m):
    return jnp.einsum('gpgh->ghp', m.reshape(16, 64, 16, 16))


def _pad_lanes(v, n=DTW):
    return jnp.pad(v.reshape(1, -1), ((0, 0), (0, n - v.shape[-1])))


def _w_in_layout(w_in_t):
    w_main = jnp.concatenate([w_in_t[:1280], w_in_t[2052:2308], w_in_t[1280:2048]], axis=0)
    return w_main, jnp.pad(w_in_t[2048:2052], ((0, DTW - 4), (0, 0)))


def _layer_params(p, l, mod, w_in, rest):
    q = {'rest': rest}
    q['mod'] = [mod[k:k + 1] for k in range(6)]
    q['nw1'] = p['norm_mix_w'][l:l + 1]
    q['nw2'] = p['norm_mlp_w'][l:l + 1]
    q['w_main'], q['w_dt'] = _w_in_layout(w_in)
    q['pool_mat'] = _pool_embed(p['pool_w'][l]).astype(BF16)
    q['pool_scale'] = p['pool_scale'][l:l + 1]
    q['sconv_w'] = p['sconv_w'][l]
    q['conv_w'] = p['ssd_conv_w'][l]
    q['conv_b'] = p['ssd_conv_b'][l:l + 1]
    q['dt_bias'] = _pad_lanes(p['ssd_dt_bias'][l])
    q['a_log'] = _pad_lanes(p['ssd_a_log'][l])
    q['ssd_d'] = _pad_lanes(p['ssd_d'][l])
    q['s5_raw'] = (p['s5_a_re'][l], p['s5_a_im'][l], p['s5_log_step'][l].reshape(16, 1),
                   p['s5_b_re'][l].reshape(16, 1024), p['s5_b_im'][l].reshape(16, 1024))
    q['cre'] = _cmat_embed(p['s5_c_re'][l]).astype(BF16)
    q['cim'] = (-_cmat_embed(p['s5_c_im'][l])).astype(BF16)
    q['s5_d'] = p['s5_d'][l:l + 1]
    q['glu_w'] = p['s5_glu_w'][l].astype(BF16)
    q['glu_b'] = p['s5_glu_b'][l:l + 1]
    q['bw'] = p['branch_norm_w'][l:l + 1]
    return q


def _layer_fwd(h, q):
    sh1, sc1, g1, sh2, sc2, g2 = q['mod']
    t = h.shape[0]
    s = {'h': h}
    s['proj'], s['dtp'], s['u'] = _f_in(h, q['nw1'], sc1, sh1, q['w_main'], q['w_dt'])
    s['ya'], s['yb'] = _f_ab(s['proj'], q['pool_mat'], q['pool_scale'], q['sconv_w'])
    s['yc'], s['ypre'], s['sprev'] = _f_ssd(s['proj'], s['dtp'], q['conv_w'], q['conv_b'], q['dt_bias'], q['a_log'], q['ssd_d'])
    lr, li, bbr, bbi, ars, ais = _s5_prep(*q['s5_raw'])
    s['bmat'] = jnp.concatenate([_bmat_embed(bbr.reshape(16, 64, 16)), _bmat_embed(bbi.reshape(16, 64, 16))],
                                axis=1).astype(BF16)
    s['tables'] = _s5_tables(ars.reshape(1, S5_P), ais.reshape(1, S5_P))
    s['yd'], s['carries'], s['states'] = _f_s5(s['proj'], s['bmat'], q['cre'], q['cim'], s['tables'][0], s['tables'][1],
                                  q['s5_d'], q['glu_w'], q['glu_b'])
    q['w_out'], q['w1'], q['w2'] = q['rest']((s['ya'], s['yc'], s['yd']))
    s['h2'], s['o'], s['cat'] = _f_out(s['ya'], s['yb'], s['yc'], s['yd'], q['bw'], q['w_out'], h, g1)
    h3, s['m'], s['a'], s['v'] = _f_mlp(s['h2'], q['nw2'], sc2, sh2, g2, q['w1'], q['w2'])
    return h3, s


STACKED = {'mlp_w1': (2, 4, D, HID // 4), 'mlp_w2': (2, HID, D), 'w_out': (2, D, D)}


def _layer_bwd(dh3, q, s, l, stacked, early=None):
    sh1, sc1, g1, sh2, sc2, g2 = q['mod']
    g = {}
    dv, da, act, dm = _b_mlp(dh3, s['a'], g2, q['w1'], q['w2'])
    g['mlp_w1'] = _tn_matmul(s['v'], da, "dw1", col_major=True, into=stacked['mlp_w1'], layer=l)
    g['mlp_w2'] = _tn_matmul(act, dm, "dw2", into=stacked['mlp_w2'], layer=l)
    dh2, dsc2, dsh2, dnw2, dg2 = _b_normmod(dv, s['h2'], dh3, s['m'], q['nw2'], sc2, "b_norm_mlp")
    dya, dyb, dyc, dyd, do, dbw = _b_out(dh2, s['ya'], s['yb'], s['yc'], s['yd'], q['bw'], q['w_out'], g1)
    g['w_out'] = _tn_matmul(s['cat'], do, "dwout", into=stacked['w_out'], layer=l)
    g['branch_norm_w'] = dbw[0]
    if early is not None:
        zero = early(g)[0, 0]
        q = dict(q, pool_scale=q['pool_scale'] + zero, conv_b=q['conv_b'] + zero, s5_d=q['s5_d'] + zero)
    dab, dpm, dps, dsw = _b_ab(s['proj'], dya, dyb, q['pool_mat'], q['pool_scale'], q['sconv_w'])
    g['pool_w'] = _pool_extract(dpm)
    g['pool_scale'] = dps[0]
    g['sconv_w'] = dsw
    dz, dxbc, ddt, dcw, dcb, ddtb, dal, ddk = _b_ssd(s['proj'], s['dtp'], s['ypre'], dyc, s['sprev'], q['conv_w'],
                                                     q['conv_b'], q['dt_bias'], q['a_log'], q['ssd_d'])
    g['ssd_conv_w'] = dcw
    g['ssd_conv_b'] = dcb[0]
    g['ssd_dt_bias'] = ddtb[0, :4]
    g['ssd_a_log'] = dal[0, :4]
    g['ssd_d'] = ddk[0, :4]
    tb = s['tables']
    ds5, dbmat, dcre, dcim, dlam, dd5, dgw, dgb = _b_s5(s['proj'], dyd, s['carries'], s['states'], s['bmat'], q['cre'], q['cim'],
                                                        tb[0], tb[1], q['s5_d'], q['glu_w'], q['glu_b'])
    g['s5_c_re'] = _cmat_extract(dcre)
    g['s5_c_im'] = -_cmat_extract(dcim)
    g['s5_d'] = dd5[0]
    g['s5_glu_w'] = dgw
    g['s5_glu_b'] = dgb[0]
    dbbr = _bmat_extract(dbmat[:, :S5_P]).reshape(16, 1024)
    dbbi = _bmat_extract(dbmat[:, S5_P:]).reshape(16, 1024)
    dar, dai, dls, dbr, dbi = _s5_prep_bwd(*q['s5_raw'], dlam[0].reshape(16, 64), dlam[1].reshape(16, 64), dbbr, dbbi)
    g['s5_a_re'], g['s5_a_im'], g['s5_log_step'] = dar, dai, dls[:, 0]
    g['s5_b_re'], g['s5_b_im'] = dbr, dbi
    du = _b_in_du(dab, dz, dxbc, ds5, ddt, q['w_main'], q['w_dt'])
    u = s['u']
    head = jnp.concatenate([_tn_matmul(dab, u, "dwin_ab"), _tn_matmul(dz, u, "dwin_z"), _tn_matmul(dxbc, u, "dwin_xbc"),
                            _tn_matmul(ddt, u, "dwin_dt")[:8]], axis=0)
    full = lax.dynamic_update_slice(jnp.zeros((2308, D), F32), head, (0, 0))
    g['w_in'] = lax.dynamic_update_slice(full, _tn_matmul(ds5, u, "dwin_s5"), (2052, 0))
    dh, dsc1, dsh1, dnw1, dg1 = _b_normmod(du, s['h'], dh2, s['o'], q['nw1'], sc1, "b_norm_mix")
    g['norm_mix_w'] = dnw1[0]
    g['norm_mlp_w'] = dnw2[0]
    dmod = jnp.concatenate([dsh1, dsc1, dg1, dsh2, dsc2, dg2], axis=1)
    return dh, g, dmod


def _local_step(x, tgt, p, mod, w_in_of, rest_of, early=None):
    h = x
    qs, saved = [], []
    for l in range(2):
        qs.append(_layer_params(p, l, mod[l], w_in_of(l), functools.partial(rest_of, l)))
        h, s = _layer_fwd(h, qs[l])
        saved.append(s)
    dh, loss, dfw = _b_final(h, tgt, p['final_norm_w'].reshape(1, D))
    grads = [None, None]
    dmods = [None, None]
    dh, grads[1], dmods[1] = _layer_bwd(dh, qs[1], saved[1], 1, {k: lax.empty(shp, F32) for k, shp in STACKED.items()})
    dh, grads[0], dmods[0] = _layer_bwd(dh, qs[0], saved[0], 0, grads[1], early)
    out = {k: jnp.stack([grads[0][k], grads[1][k]]) for k in grads[0] if k not in STACKED}
    if early is None:
        out.update({k: grads[0][k] for k in STACKED})
    out['final_norm_w'] = dfw[0]
    return loss, dh, out, jnp.concatenate(dmods, axis=0)


def _pack(arrs):
    parts, rows = [], 0
    for a in arrs:
        f = a.reshape(-1).astype(F32)
        pad = (-f.shape[0]) % 1024
        f = jnp.pad(f, (0, pad)) if pad else f
        parts.append(f.reshape(-1, 128))
        rows += parts[-1].shape[0]
    if rows % 256:
        parts.append(jnp.zeros((256 - rows % 256, 128), F32))
    return jnp.concatenate(parts, axis=0)


def _unpack(buf, shapes):
    out, row = [], 0
    for shp in shapes:
        n = int(math.prod(shp)) if len(shp) else 1
        rows = (n + 1023) // 1024 * 8
        out.append(buf[row:row + rows].reshape(-1)[:n].reshape(shp))
        row += rows
    return out


def _shard_of(a, axis, k):
    n = a.shape[axis] // 4
    return lax.dynamic_slice_in_dim(a, k * n, n, axis)


def kernel(x, c, norm_mix_w, norm_mlp_w, ada_w, ada_b, w_in, pool_w, pool_scale, sconv_w, ssd_conv_w, ssd_conv_b, ssd_dt_bias, ssd_a_log, ssd_d, s5_a_re, s5_a_im, s5_log_step, s5_b_re, s5_b_im, s5_c_re, s5_c_im, s5_d, s5_glu_w, s5_glu_b, branch_norm_w, w_out, mlp_w1, mlp_w2, final_norm_w, loss_target, m_norm_mix_w, m_norm_mlp_w, m_ada_w, m_ada_b, m_w_in, m_pool_w, m_pool_scale, m_sconv_w, m_ssd_conv_w, m_ssd_conv_b, m_ssd_dt_bias, m_ssd_a_log, m_ssd_d, m_s5_a_re, m_s5_a_im, m_s5_log_step, m_s5_b_re, m_s5_b_im, m_s5_c_re, m_s5_c_im, m_s5_d, m_s5_glu_w, m_s5_glu_b, m_branch_norm_w, m_w_out, m_mlp_w1, m_mlp_w2, m_final_norm_w, v_norm_mix_w, v_norm_mlp_w, v_ada_w, v_ada_b, v_w_in, v_pool_w, v_pool_scale, v_sconv_w, v_ssd_conv_w, v_ssd_conv_b, v_ssd_dt_bias, v_ssd_a_log, v_ssd_d, v_s5_a_re, v_s5_a_im, v_s5_log_step, v_s5_b_re, v_s5_b_im, v_s5_c_re, v_s5_c_im, v_s5_d, v_s5_glu_w, v_s5_glu_b, v_branch_norm_w, v_w_out, v_mlp_w1, v_mlp_w2, v_final_norm_w):
    loc = locals()
    w = {n: loc[n] for n in WEIGHTS}
    mom = {n: loc['m_' + n] for n in WEIGHTS}
    var = {n: loc['v_' + n] for n in WEIGHTS}
    ix, iy, ic = lax.axis_index("x"), lax.axis_index("y"), lax.axis_index("c")
    chip = 2 * ix + iy
    dev = 4 * ix + 2 * iy + ic

    mine_of = lambda a: lax.dynamic_index_in_dim(a.astype(BF16), ic, axis=0, keepdims=False)
    pad_in = lambda a: jnp.pad(a.T, ((0, WIN_ROWS - 577), (0, 0)))
    shard = jnp.concatenate([pad_in(mine_of(w['w_in'])), mine_of(w['w_out']), mine_of(w['mlp_w1']), mine_of(w['mlp_w2'])], axis=0)

    (c_all,) = _exchange([c], EVERYONE, False, "ag_cond", stage=True)
    c_all = c_all.reshape(8, D)
    small_sh = _exchange([w[n] for n in SMALL_SHARDED], CHIPS, False, "ag_small")
    (w_in0,) = _exchange([pad_in(w['w_in'][0].astype(BF16))], CHIPS, False, "ag_win0")
    p = {n: w[n] for n in WEIGHTS if n not in BIG}
    for n, g in zip(SMALL_SHARDED, small_sh):
        ax = SMALL_SHARDED[n]
        p[n] = jnp.concatenate([g[k] for k in range(4)], axis=ax)

    def w_in_full(sh):
        return sh[:, :577].reshape(4 * 577, D)

    big = {}

    def fetch(after):
        if not big:
            (mine,), (got,) = _split_wait(sems, shard_thru, land, after, False, "ag_big_wait")
            got = lax.dynamic_update_slice(got, mine[None], (chip, 0, 0))
            other = _pair_swap([got.reshape(-1, D)], False, "swap_big")[0].reshape(got.shape)
            big['both'] = [jnp.where(ic == l, got, other) for l in range(2)]
        return big['both']

    def w_in_of(l):
        return w_in_full(w_in0) if l == 0 else w_in_full(fetch(None)[1])

    def rest_of(l, after):
        blk = fetch(after)[l]
        r0 = WIN_ROWS
        w_out_l = blk[:, r0:r0 + 256].reshape(D, D)
        w1_l = blk[:, r0 + 256:r0 + 1280]
        w2_l = blk[:, r0 + 1280:r0 + 2304].reshape(HID, D)
        return w_out_l, w1_l, w2_l

    ada_b_sh = _shard_of(w['ada_b'], 1, chip).reshape(2, 1, 6 * D // 4)
    mod_sh = _ada_fwd(c_all, w['ada_w'], ada_b_sh)
    (mod_all,) = _exchange([mod_sh], CHIPS, False, "ag_mod", stage=True)
    mine = lax.dynamic_index_in_dim(mod_all, dev, axis=2, keepdims=False)
    sems, shard_thru, land, token = _split_start([shard], [mod_all, w_in0] + small_sh, False, "ag_big_start")
    mod = jnp.transpose(mine, (1, 0, 2)).reshape(2, 6, D) + token[0, 0]

    layer = ic.astype(jnp.int32).reshape(1)
    flight = {}

    def early(g0):
        gws = [g0['w_out'].reshape(2, 4, 256, D), g0['mlp_w1'], g0['mlp_w2'].reshape(2, 4, 1024, D)]
        got = _pair_swap([a.reshape(2, -1, D) for a in gws], True, "swap_grad")
        pair = [_pair_sum(a, b.reshape(a.shape[1:]), layer, "pair_sum%d" % (k + 1), BF16) for k, (a, b) in enumerate(zip(gws, got))]
        flight['sems'], flight['srcs'], flight['lands'], token = _split_start(pair, [], True, "rs_start")
        return token

    loss, grad_x, g, dmod = _local_step(x[0], loss_target[0], p, mod, w_in_of, rest_of, early)

    (dmod_all,) = _exchange([dmod], EVERYONE, False, "ag_dmod", stage=True)
    dmod_all = jnp.transpose(dmod_all, (1, 0, 2))
    g_ada_w, g_ada_b = _ada_bwd(c_all, _shard_of(dmod_all, 2, chip), dmod_all)

    sent, lands = _split_wait(flight['sems'], flight['srcs'], flight['lands'], [grad_x, g['w_in']], True, "rs_wait")
    quad = []
    for k, (land, mine) in enumerate(zip(lands, sent)):
        own = lax.dynamic_index_in_dim(mine, chip, axis=0, keepdims=True)
        quad.append(_sum_lead(lax.dynamic_update_slice(land, own, (chip, 0, 0)), "rs_chip_sum%d" % (k + 1), F32))
    gw_in = jnp.pad(g['w_in'].reshape(2, 4, 577, D), ((0, 0), (0, 0), (0, WIN_ROWS - 577), (0, 0)))
    (got_in,) = _pair_swap([gw_in.reshape(2, -1, D)], True, "swap_grad_in")
    pair_in = _pair_sum(gw_in, got_in.reshape(gw_in.shape[1:]), layer, "pair_sum0", BF16)
    (quad_in,) = _exchange([pair_in], CHIPS, True, "rs_chips")
    quad = [_sum_lead(quad_in, "rs_chip_sum0", F32)] + quad
    other = _pair_swap(quad, False, "swap_red")
    both = [jnp.stack([jnp.where(ic == l, a, b) for l in range(2)]) for a, b in zip(quad, other)]
    both[0] = jnp.transpose(both[0][:, :577], (0, 2, 1))
    red = dict(zip(('w_in', 'w_out', 'mlp_w1', 'mlp_w2'), both))
    red['ada_w'] = g_ada_w

    small_names = [n for n in WEIGHTS if n not in BIG and n != 'ada_b']
    pair_parts = _exchange([g[n] for n in small_names] + [loss], SIBLING, False, "ag_smallpair", stage=True)
    chip_parts = _exchange(_sum_many(pair_parts, "smallpair_sum"), CHIPS, False, "ag_smallgrad", stage=True)
    summed = _sum_many(chip_parts, "smallgrad_sum")
    for n, a in zip(small_names, summed[:-1]):
        a = a.reshape(w[n].shape) if n in ('s5_b_re', 's5_b_im') else a
        red[n] = _shard_of(a, SMALL_SHARDED[n], chip) if n in SMALL_SHARDED else a
    red['ada_b'] = g_ada_b
    loss_out = summed[-1].reshape(())

    delta, new_m, new_v = {}, {}, {}
    for n in BIG:
        delta[n], new_m[n], new_v[n] = _adamw(w[n], red[n], mom[n], var[n], "adamw_" + n)
    rest = [n for n in WEIGHTS if n not in BIG]
    lanes = lambda n, a: a.reshape(2, 16, 1024) if n in ('s5_b_re', 's5_b_im') else a
    outs = _adamw_many(*[[lanes(n, src[n]) for n in rest] for src in (w, red, mom, var)], "adamw_small")
    for k, n in enumerate(rest):
        delta[n], new_m[n], new_v[n] = (outs[3 * k + j].reshape(w[n].shape) for j in range(3))

    return (loss_out, grad_x[None], *[red[n] for n in WEIGHTS], *[delta[n] for n in WEIGHTS],
            *[new_m[n] for n in WEIGHTS], *[new_v[n] for n in WEIGHTS])
```

```python
import functools
import math

import jax
import jax.numpy as jnp
from jax import lax
from jax.experimental import pallas as pl
from jax.experimental.pallas import tpu as pltpu

F32 = jnp.float32
BF16 = jnp.bfloat16
HI = lax.Precision.HIGHEST

D = 1024
GW = 256
HID = 4096
EPS = 1e-6
PW = 2304
DTW = 128
SSD_L = 128
SSD_SUB = 2
SSD_SUB_BWD = 2
NH, HP, NS = 4, 64, 128
S5_P = 1024
MESH = pl.DeviceIdType.MESH

ADAM_LR, ADAM_B1, ADAM_B2, ADAM_EPS, ADAM_WD, ADAM_STEP = 0.001, 0.9, 0.999, 1e-08, 0.01, 10

NT = (((1,), (1,)), ((), ()))
TN = (((0,), (0,)), ((), ()))

WEIGHTS = ['norm_mix_w', 'norm_mlp_w', 'ada_w', 'ada_b', 'w_in', 'pool_w', 'pool_scale', 'sconv_w', 'ssd_conv_w',
           'ssd_conv_b', 'ssd_dt_bias', 'ssd_a_log', 'ssd_d', 's5_a_re', 's5_a_im', 's5_log_step', 's5_b_re', 's5_b_im',
           's5_c_re', 's5_c_im', 's5_d', 's5_glu_w', 's5_glu_b', 'branch_norm_w', 'w_out', 'mlp_w1', 'mlp_w2',
           'final_norm_w']
BIG = ('ada_w', 'w_in', 'w_out', 'mlp_w1', 'mlp_w2')
SMALL_SHARDED = {'sconv_w': 2, 'ssd_conv_w': 2, 's5_glu_w': 1}


def _cparams(n_axes, vmem_mb=48):
    return pltpu.CompilerParams(dimension_semantics=("arbitrary",) * n_axes, vmem_limit_bytes=vmem_mb * 1024 * 1024)


def _row(n):
    return pl.BlockSpec((1, n), lambda *_: (0, 0))


def _full(shape):
    nd = len(shape)
    return pl.BlockSpec(tuple(shape), lambda *_: (0,) * nd)


def _dot(a, b, dims=None, prec=None):
    if dims is None:
        dims = (((a.ndim - 1,), (0,)), ((), ()))
    return lax.dot_general(a, b, dims, preferred_element_type=F32, precision=prec)


def _bdot(a, b, dims=None):
    return _dot(a.astype(BF16), b.astype(BF16), dims)


def _sig(x):
    return jax.nn.sigmoid(x)


def _silu(x):
    return x * _sig(x)


def _dsilu(x):
    s = _sig(x)
    return s * (1.0 + x * (1.0 - s))


def _softplus(x):
    return jnp.maximum(x, 0.0) + jnp.log(1.0 + jnp.exp(-jnp.abs(x)))


_GK = math.sqrt(2.0 / math.pi)


def _gelu(x):
    return 0.5 * x * (1.0 + jnp.tanh(_GK * (x + 0.044715 * x * x * x)))


def _dgelu(x):
    th = jnp.tanh(_GK * (x + 0.044715 * x * x * x))
    return 0.5 * (1.0 + th) + 0.5 * x * (1.0 - th * th) * _GK * (1.0 + 3.0 * 0.044715 * x * x)


def _colsum(x):
    return jnp.sum(x, axis=0, keepdims=True)


def _rms(x):
    r = lax.rsqrt(jnp.mean(x * x, axis=-1, keepdims=True) + EPS)
    return r, x * r


def _rms_bwd(r, n, dn):
    return r * (dn - n * jnp.mean(dn * n, axis=-1, keepdims=True))


def _roll(x, k):
    n = x.shape[0]
    k = k % n
    return x if k == 0 else pltpu.roll(x, k, axis=0)


def _tblock(t, want=512):
    return min(t, want)


def _peer(mask):
    x, y, c = lax.axis_index("x"), lax.axis_index("y"), lax.axis_index("c")
    return (x ^ ((mask >> 2) & 1), y ^ ((mask >> 1) & 1), c ^ (mask & 1))


def _group_index(masks):
    x, y, c = lax.axis_index("x"), lax.axis_index("y"), lax.axis_index("c")
    full = 0
    for m in masks:
        full |= m
    bits = [b for b in (4, 2, 1) if full & b]

    def idx(px, py, pc):
        v = {4: px, 2: py, 1: pc}
        out = 0
        for b in bits:
            out = out * 2 + v[b]
        return out

    return idx(x, y, c), [idx(*_peer(m)) for m in masks]


def _exchange(arrs, masks, scatter, name, stage=False):
    n_arr, n_peer, n_grp = len(arrs), len(masks), len(masks) + 1

    def body(*refs):
        ins, outs = refs[:n_arr], refs[n_arr:2 * n_arr]
        send_sems, recv_sems, local_sems = refs[2 * n_arr:2 * n_arr + 3]
        if stage:
            bufs, load_sems = refs[2 * n_arr + 3:3 * n_arr + 3], refs[3 * n_arr + 3]
            loads = [pltpu.make_async_copy(ins[t], bufs[t], load_sems.at[t]) for t in range(n_arr)]
            for ld in loads:
                ld.start()
            for ld in loads:
                ld.wait()
            ins = bufs
        me, peer_idx = _group_index(masks)
        copies = []
        for t in range(n_arr):
            src_me = ins[t].at[me] if scatter else ins[t]
            loc = pltpu.make_async_copy(src_me, outs[t].at[me], local_sems.at[t])
            loc.start()
            copies.append(loc)
            for j, m in enumerate(masks):
                src = ins[t].at[peer_idx[j]] if scatter else ins[t]
                cp = pltpu.make_async_remote_copy(src_ref=src, dst_ref=outs[t].at[me], send_sem=send_sems.at[t, j],
                                                  recv_sem=recv_sems.at[t, j], device_id=_peer(m), device_id_type=MESH)
                cp.start()
                copies.append(cp)
        for cp in copies:
            cp.wait()

    hbm = pl.BlockSpec(memory_space=pl.ANY)
    out_shape = [jax.ShapeDtypeStruct((n_grp,) + (a.shape[1:] if scatter else a.shape), a.dtype) for a in arrs]
    staging = [pltpu.VMEM(a.shape, a.dtype) for a in arrs] + [pltpu.SemaphoreType.DMA((n_arr,))] if stage else []
    outs = pl.pallas_call(
        body, name=name, in_specs=[hbm] * n_arr, out_specs=[hbm] * n_arr, out_shape=out_shape,
        scratch_shapes=[pltpu.SemaphoreType.DMA((n_arr, n_peer)), pltpu.SemaphoreType.DMA((n_arr, n_peer)),
                        pltpu.SemaphoreType.DMA((n_arr,))] + staging,
        compiler_params=pltpu.CompilerParams(vmem_limit_bytes=48 * 1024 * 1024),
    )(*arrs)
    return list(outs)


def _split_copies(src_refs, land_refs, sems, scatter):
    me, peer_idx = _group_index(CHIPS)
    n = len(CHIPS) * len(src_refs)
    copies = []
    for t, (src_ref, land_ref) in enumerate(zip(src_refs, land_refs)):
        for j, m in enumerate(CHIPS):
            k = len(CHIPS) * t + j
            copies.append(pltpu.make_async_remote_copy(
                src_ref=src_ref.at[peer_idx[j]] if scatter else src_ref, dst_ref=land_ref.at[me], send_sem=sems[k],
                recv_sem=sems[n + k], device_id=_peer(m), device_id_type=MESH))
    return copies


def _split_start(srcs, after, scatter, name):
    n_arr, n_sem = len(srcs), 2 * len(CHIPS) * len(srcs)

    def body(*refs):
        src_refs, land_refs = refs[:n_arr], refs[n_arr:2 * n_arr]
        outs = refs[2 * n_arr + len(after):]
        for cp in _split_copies(src_refs, land_refs, outs[:n_sem], scatter):
            cp.start()
        outs[-1][...] = jnp.zeros_like(outs[-1])

    hbm = pl.BlockSpec(memory_space=pltpu.HBM)
    sem = pl.BlockSpec(memory_space=pltpu.SEMAPHORE)
    lands = [lax.empty((len(CHIPS) + 1,) + (a.shape[1:] if scatter else a.shape), a.dtype) for a in srcs]
    as_hbm = lambda a: pltpu.with_memory_space_constraint(a, pltpu.HBM)
    outs = pl.pallas_call(
        body, name=name,
        out_shape=(pltpu.SemaphoreType.DMA(()),) * n_sem + tuple(pltpu.HBM(a.shape, a.dtype) for a in srcs + lands)
        + (jax.ShapeDtypeStruct((8, 128), F32),),
        in_specs=(hbm,) * (2 * n_arr) + (pl.BlockSpec(memory_space=pl.ANY),) * len(after),
        out_specs=(sem,) * n_sem + (hbm,) * (2 * n_arr) + (pl.BlockSpec(memory_space=pltpu.VMEM),),
        input_output_aliases={t: n_sem + t for t in range(2 * n_arr)},
        compiler_params=pltpu.CompilerParams(has_side_effects=pltpu.SideEffectType.DATAFLOW_SIDE_EFFECTING),
    )(*[as_hbm(a) for a in srcs + lands], *after)
    return outs[:n_sem], list(outs[n_sem:n_sem + n_arr]), list(outs[n_sem + n_arr:n_sem + 2 * n_arr]), outs[-1]


def _split_wait(sems, srcs, lands, after, scatter, name):
    n_arr, n_sem = len(srcs), len(sems)

    def body(*refs):
        src_refs, land_refs = refs[:n_arr], refs[n_arr:2 * n_arr]
        for cp in _split_copies(src_refs, land_refs, refs[2 * n_arr:2 * n_arr + n_sem], scatter):
            cp.wait_send()
            cp.wait_recv()

    hbm = pl.BlockSpec(memory_space=pltpu.HBM)
    sem = pl.BlockSpec(memory_space=pltpu.SEMAPHORE)
    outs = pl.pallas_call(
        body, name=name, out_shape=tuple(pltpu.HBM(a.shape, a.dtype) for a in srcs + lands),
        in_specs=(hbm,) * (2 * n_arr) + (sem,) * n_sem + (pl.BlockSpec(memory_space=pl.ANY),) * len(after),
        out_specs=(hbm,) * (2 * n_arr), input_output_aliases={t: t for t in range(2 * n_arr)},
        compiler_params=pltpu.CompilerParams(has_side_effects=pltpu.SideEffectType.DATAFLOW_SIDE_EFFECTING),
    )(*srcs, *lands, *sems, *after)
    return list(outs[:n_arr]), list(outs[n_arr:])


CHIPS = (4, 2, 6)
EVERYONE = (1, 2, 3, 4, 5, 6, 7)
SIBLING = (1,)
SWAP_ROWS = 512
WIN_ROWS = 592


def _pair_swap(arrs, other_layer, name, narrow=False):
    n_arr = len(arrs)
    shapes = [a.shape[-2:] for a in arrs]
    out_dtypes = [BF16 if narrow else a.dtype for a in arrs]
    chunks = []
    for t, (rows, _) in enumerate(shapes):
        assert rows % 16 == 0
        for j, r0 in enumerate(range(0, rows, SWAP_ROWS)):
            chunks.append((t, r0, min(SWAP_ROWS, rows - r0), j % 2))

    def body(*refs):
        ins, outs = refs[:n_arr], refs[n_arr:2 * n_arr]
        bufs = refs[2 * n_arr:3 * n_arr]
        out_bufs = refs[3 * n_arr:4 * n_arr] if narrow else bufs
        load_sems, send_sems, recv_sems = refs[-3:]
        sibling = _peer(1)
        c = lax.axis_index("c")

        def load(k):
            t, r0, n, slot = chunks[k]
            src = ins[t].at[1 - c] if other_layer else ins[t]
            return pltpu.make_async_copy(src.at[pl.ds(r0, n)], bufs[t].at[slot, pl.ds(0, n)], load_sems.at[t, slot])

        def send(k):
            t, r0, n, slot = chunks[k]
            return pltpu.make_async_remote_copy(src_ref=out_bufs[t].at[slot, pl.ds(0, n)], dst_ref=outs[t].at[pl.ds(r0, n)],
                                                send_sem=send_sems.at[t, slot], recv_sem=recv_sems.at[t],
                                                device_id=sibling, device_id_type=MESH)

        in_flight = {}

        def drain(k):
            key = (chunks[k][0], chunks[k][3])
            if key in in_flight:
                send(in_flight.pop(key)).wait_send()

        def start_load(k):
            if not narrow:
                drain(k)
            load(k).start()

        start_load(0)
        for k in range(len(chunks)):
            t, _, n, slot = chunks[k]
            load(k).wait()
            if k + 1 < len(chunks):
                start_load(k + 1)
            if narrow:
                drain(k)
                out_bufs[t][slot, pl.ds(0, n), :] = bufs[t][slot, pl.ds(0, n), :].astype(BF16)
            send(k).start()
            in_flight[(t, slot)] = k
        for k in in_flight.values():
            send(k).wait_send()
        for t in range(n_arr):
            pltpu.make_async_remote_copy(src_ref=outs[t], dst_ref=outs[t], send_sem=send_sems.at[t, 0],
                                         recv_sem=recv_sems.at[t], device_id=sibling, device_id_type=MESH).wait_recv()

    hbm = pl.BlockSpec(memory_space=pl.ANY)
    outs = pl.pallas_call(
        body, name=name, in_specs=[hbm] * n_arr, out_specs=[hbm] * n_arr,
        out_shape=[jax.ShapeDtypeStruct(s, dt) for s, dt in zip(shapes, out_dtypes)],
        scratch_shapes=[pltpu.VMEM((2, min(SWAP_ROWS, s[0]), s[1]), a.dtype) for s, a in zip(shapes, arrs)]
        + ([pltpu.VMEM((2, min(SWAP_ROWS, s[0]), s[1]), BF16) for s in shapes] if narrow else [])
        + [pltpu.SemaphoreType.DMA((n_arr, 2)), pltpu.SemaphoreType.DMA((n_arr, 2)), pltpu.SemaphoreType.DMA((n_arr,))],
        compiler_params=pltpu.CompilerParams(vmem_limit_bytes=48 * 1024 * 1024),
    )(*arrs)
    return list(outs)


def _sum_lead(a, name, out_dtype):
    n = a.shape[0]
    shape = a.shape[1:]

    def body(a_ref, o_ref):
        acc = a_ref[0].astype(F32)
        for k in range(1, n):
            acc = acc + a_ref[k].astype(F32)
        o_ref[...] = acc.astype(out_dtype)

    if len(shape) == 3:
        blk = (1,) + shape[1:]
        return pl.pallas_call(
            body, name=name, grid=(shape[0],), in_specs=[pl.BlockSpec((n,) + blk, lambda i: (0, i, 0, 0))],
            out_specs=pl.BlockSpec(blk, lambda i: (i, 0, 0)), out_shape=jax.ShapeDtypeStruct(shape, out_dtype),
            compiler_params=_cparams(1),
        )(a)
    rows, cols = shape
    rb = rows
    for cand in (512, 256, 128):
        if rows % cand == 0 and rows > cand:
            rb = cand
            break
    return pl.pallas_call(
        body, name=name, grid=(rows // rb,), in_specs=[pl.BlockSpec((n, rb, cols), lambda i: (0, i, 0))],
        out_specs=pl.BlockSpec((rb, cols), lambda i: (i, 0)), out_shape=jax.ShapeDtypeStruct((rows, cols), out_dtype),
        compiler_params=_cparams(1),
    )(a)


def _pair_sum(g, recv, layer, name, out_dtype):
    _, n, r, c = g.shape

    def body(l_ref, g_ref, r_ref, o_ref):
        o_ref[...] = (g_ref[0].astype(F32) + r_ref[...].astype(F32)).astype(out_dtype)

    return pl.pallas_call(
        body, name=name,
        grid_spec=pltpu.PrefetchScalarGridSpec(
            num_scalar_prefetch=1, grid=(n,),
            in_specs=[pl.BlockSpec((1, 1, r, c), lambda i, l: (l[0], i, 0, 0)), pl.BlockSpec((1, r, c), lambda i, l: (i, 0, 0))],
            out_specs=pl.BlockSpec((1, r, c), lambda i, l: (i, 0, 0))),
        out_shape=jax.ShapeDtypeStruct((n, r, c), out_dtype), compiler_params=_cparams(1),
    )(layer, g, recv)


def _tn_matmul(a, b, name, col_major=False, into=None, layer=0):
    t, k = a.shape
    n = b.shape[1]
    tb = _tblock(t, 1024)
    kb = min(k, 1024)
    nb = min(n, 1024)
    grid = (k // kb, n // nb, t // tb)
    lead = (into is not None) + col_major

    def body(a_ref, b_ref, *rest):
        o_ref = rest[-1]
        for _ in range(lead):
            o_ref = o_ref.at[0]

        @pl.when(pl.program_id(2) == 0)
        def _():
            o_ref[...] = jnp.zeros_like(o_ref)

        o_ref[...] += _bdot(a_ref[...], b_ref[...], TN)

    if col_major:
        block, index, shape = (1, kb, nb), (lambda ki, ni: (ni, ki, 0)), (n // nb, k, nb)
    else:
        block, index, shape = (kb, nb), (lambda ki, ni: (ki, ni)), (k, n)
    in_specs = [pl.BlockSpec((tb, kb), lambda ki, ni, ti: (ti, ki)), pl.BlockSpec((tb, nb), lambda ki, ni, ti: (ti, ni))]
    if into is None:
        return pl.pallas_call(
            body, name=name, grid=grid, in_specs=in_specs, out_specs=pl.BlockSpec(block, lambda ki, ni, ti: index(ki, ni)),
            out_shape=jax.ShapeDtypeStruct(shape, F32), compiler_params=_cparams(3),
        )(a, b)
    assert into.shape == (2,) + shape
    return pl.pallas_call(
        body, name=name, grid=grid, in_specs=in_specs + [pl.BlockSpec(memory_space=pl.ANY)],
        out_specs=pl.BlockSpec((1,) + block, lambda ki, ni, ti: (layer,) + index(ki, ni)),
        out_shape=jax.ShapeDtypeStruct(into.shape, F32), input_output_aliases={2: 0}, compiler_params=_cparams(3),
    )(a, b, into)


def _sum_many(arrs, name):
    k = len(arrs)

    def body(*refs):
        for a_ref, o_ref in zip(refs[:k], refs[k:]):
            acc = a_ref[0]
            for j in range(1, a_ref.shape[0]):
                acc = acc + a_ref[j]
            o_ref[...] = acc

    return pl.pallas_call(body, name=name, grid=(1,), in_specs=[_full(a.shape) for a in arrs],
                          out_specs=[_full(a.shape[1:]) for a in arrs],
                          out_shape=[jax.ShapeDtypeStruct(a.shape[1:], F32) for a in arrs], compiler_params=_cparams(1))(*arrs)


def _adamw_math(w, g, m, v):
    m2 = ADAM_B1 * m + (1.0 - ADAM_B1) * g
    v2 = ADAM_B2 * v + (1.0 - ADAM_B2) * (g * g)
    m_hat = m2 / (1.0 - ADAM_B1 ** ADAM_STEP)
    v_hat = v2 / (1.0 - ADAM_B2 ** ADAM_STEP)
    return -ADAM_LR * (m_hat / (jnp.sqrt(v_hat) + ADAM_EPS) + ADAM_WD * w), m2, v2


def _adamw_many(ws, gs, ms, vs, name):
    n = len(ws)

    def body(*refs):
        ins, outs = refs[:4 * n], refs[4 * n:]
        for k in range(n):
            res = _adamw_math(ins[k][...], ins[n + k][...], ins[2 * n + k][...], ins[3 * n + k][...])
            for j in range(3):
                outs[3 * k + j][...] = res[j]

    out_shape = []
    for a in ws:
        out_shape += [jax.ShapeDtypeStruct(a.shape, F32)] * 3
    return pl.pallas_call(body, name=name, grid=(1,), in_specs=[_full(a.shape) for a in ws] * 4,
                          out_specs=[_full(s.shape) for s in out_shape], out_shape=out_shape,
                          compiler_params=_cparams(1))(*ws, *gs, *ms, *vs)


def _adamw(w, g, m, v, name):
    shape = w.shape
    cols = shape[-1]
    rows = int(math.prod(shape[:-1]))
    rb = rows
    for cand in (256, 128, 64, 32, 16, 8):
        if rows % cand == 0 and rows > cand:
            rb = cand
            break
    bc1 = 1.0 - ADAM_B1 ** ADAM_STEP
    bc2 = 1.0 - ADAM_B2 ** ADAM_STEP

    def body(w_ref, g_ref, m_ref, v_ref, d_ref, nm_ref, nv_ref):
        gg = g_ref[...]
        m2 = ADAM_B1 * m_ref[...] + (1.0 - ADAM_B1) * gg
        v2 = ADAM_B2 * v_ref[...] + (1.0 - ADAM_B2) * (gg * gg)
        m_hat = m2 / bc1
        v_hat = v2 / bc2
        d_ref[...] = -ADAM_LR * (m_hat / (jnp.sqrt(v_hat) + ADAM_EPS) + ADAM_WD * w_ref[...])
        nm_ref[...] = m2
        nv_ref[...] = v2

    spec = pl.BlockSpec((rb, cols), lambda i: (i, 0))
    sds = jax.ShapeDtypeStruct((rows, cols), F32)
    outs = pl.pallas_call(
        body, name=name, grid=(rows // rb,), in_specs=[spec] * 4, out_specs=[spec] * 3, out_shape=[sds] * 3,
        compiler_params=_cparams(1),
    )(*(z.reshape(rows, cols) for z in (w, g, m, v)))
    return tuple(o.reshape(shape) for o in outs)


def _ada_fwd(c_all, ada_w_sh, ada_b_sh):
    s = ada_w_sh.shape[2]
    sb = 512

    def body(c_ref, w_ref, b_ref, o_ref):
        cond = _silu(c_ref[...])
        o_ref[0] = _bdot(cond, w_ref[0]) + b_ref[0]

    return pl.pallas_call(
        body, name="ada_fwd", grid=(2, s // sb),
        in_specs=[_full((8, D)), pl.BlockSpec((1, D, sb), lambda l, j: (l, 0, j)), pl.BlockSpec((1, 1, sb), lambda l, j: (l, 0, j))],
        out_specs=pl.BlockSpec((1, 8, sb), lambda l, j: (l, 0, j)), out_shape=jax.ShapeDtypeStruct((2, 8, s), F32),
        compiler_params=_cparams(2),
    )(c_all, ada_w_sh, ada_b_sh)


def _ada_bwd(c_all, dmod_sh, dmod_all):
    s = dmod_sh.shape[2]
    sb = 512

    def body(c_ref, d_ref, o_ref):
        cond = _silu(c_ref[...])
        o_ref[0] = _bdot(cond, d_ref[0], TN)

    gw = pl.pallas_call(
        body, name="ada_bwd_w", grid=(2, s // sb),
        in_specs=[_full((8, D)), pl.BlockSpec((1, 8, sb), lambda l, j: (l, 0, j))],
        out_specs=pl.BlockSpec((1, D, sb), lambda l, j: (l, 0, j)), out_shape=jax.ShapeDtypeStruct((2, D, s), F32),
        compiler_params=_cparams(2),
    )(c_all, dmod_sh)

    def body_b(d_ref, o_ref):
        acc = d_ref[0, 0:1, :]
        for k in range(1, 8):
            acc = acc + d_ref[0, k:k + 1, :]
        o_ref[0] = acc

    gb = pl.pallas_call(
        body_b, name="ada_bwd_b", grid=(2,), in_specs=[pl.BlockSpec((1, 8, 6 * D), lambda l: (l, 0, 0))],
        out_specs=pl.BlockSpec((1, 1, 6 * D), lambda l: (l, 0, 0)), out_shape=jax.ShapeDtypeStruct((2, 1, 6 * D), F32),
        compiler_params=_cparams(1),
    )(dmod_all)
    return gw, gb.reshape(2, 6 * D)


def _f_in(h, nw, sc, sh, w_main, w_dt):
    t = h.shape[0]
    tb = _tblock(t)

    def body(h_ref, nw_ref, sc_ref, sh_ref, w_ref, wd_ref, p_ref, dt_ref, u_ref):
        _, n = _rms(h_ref[...])
        u = ((n * nw_ref[...]) * (1.0 + sc_ref[...]) + sh_ref[...]).astype(BF16)
        u_ref[...] = u
        p_ref[...] = _dot(u, w_ref[...], NT)
        dt_ref[...] = _dot(u, wd_ref[...], NT)

    return pl.pallas_call(
        body, name="f_in", grid=(t // tb,),
        in_specs=[pl.BlockSpec((tb, D), lambda i: (i, 0)), _row(D), _row(D), _row(D), _full((PW, D)), _full((DTW, D))],
        out_specs=[pl.BlockSpec((tb, PW), lambda i: (i, 0)), pl.BlockSpec((tb, DTW), lambda i: (i, 0)),
                   pl.BlockSpec((tb, D), lambda i: (i, 0))],
        out_shape=[jax.ShapeDtypeStruct((t, PW), F32), jax.ShapeDtypeStruct((t, DTW), F32), jax.ShapeDtypeStruct((t, D), BF16)],
        compiler_params=_cparams(1),
    )(h, nw, sc, sh, w_main, w_dt)


def _norm_bwd_step(du_v, x, dres_v, gated, nwv, scv, dx_ref, dsc_ref, dsh_ref, dnw_ref, dg_ref):
    r, n = _rms(x)
    scale = 1.0 + scv
    dsc_ref[...] += _colsum(du_v * (n * nwv))
    dsh_ref[...] += _colsum(du_v)
    dnw_ref[...] += _colsum(du_v * scale * n)
    dg_ref[...] += _colsum(dres_v * gated)
    dx_ref[...] = dres_v + _rms_bwd(r, n, du_v * scale * nwv)


def _b_in(dab, dz, dxbc, ds5, ddt, w_main, w_dt, x, dres, gated, nw, sc):
    t = dab.shape[0]
    tb = _tblock(t)

    def body(a_ref, z_ref, x_ref, s_ref, d_ref, w_ref, wd_ref, h_ref, dr_ref, g_ref, nw_ref, sc_ref,
             dx_ref, dsc_ref, dsh_ref, dnw_ref, dg_ref):
        @pl.when(pl.program_id(0) == 0)
        def _():
            for r in (dsc_ref, dsh_ref, dnw_ref, dg_ref):
                r[...] = jnp.zeros_like(r)

        du = _bdot(a_ref[...], w_ref[0:1024, :])
        du += _bdot(z_ref[...], w_ref[1024:1280, :])
        du += _bdot(s_ref[...], w_ref[1280:1536, :])
        du += _bdot(x_ref[...], w_ref[1536:2304, :])
        du += _bdot(d_ref[...], wd_ref[...])
        _norm_bwd_step(du, h_ref[...], dr_ref[...], g_ref[...], nw_ref[...], sc_ref[...], dx_ref, dsc_ref, dsh_ref, dnw_ref, dg_ref)

    blk = lambda n: pl.BlockSpec((tb, n), lambda i: (i, 0))
    row = jax.ShapeDtypeStruct((1, D), F32)
    return pl.pallas_call(
        body, name="b_in", grid=(t // tb,),
        in_specs=[blk(1024), blk(256), blk(768), blk(256), blk(DTW), _full((PW, D)), _full((DTW, D)),
                  blk(D), blk(D), blk(D), _row(D), _row(D)],
        out_specs=[blk(D), _row(D), _row(D), _row(D), _row(D)],
        out_shape=[jax.ShapeDtypeStruct((t, D), F32), row, row, row, row], compiler_params=_cparams(1),
    )(dab, dz, dxbc, ds5, ddt, w_main, w_dt, x, dres, gated, nw, sc)


def _b_normmod(du, x, dres, gated, nw, sc, name):
    t = x.shape[0]
    tb = _tblock(t)

    def body(du_ref, x_ref, dr_ref, g_ref, nw_ref, sc_ref, dx_ref, dsc_ref, dsh_ref, dnw_ref, dg_ref):
        @pl.when(pl.program_id(0) == 0)
        def _():
            for r in (dsc_ref, dsh_ref, dnw_ref, dg_ref):
                r[...] = jnp.zeros_like(r)

        _norm_bwd_step(du_ref[...], x_ref[...], dr_ref[...], g_ref[...], nw_ref[...], sc_ref[...],
                       dx_ref, dsc_ref, dsh_ref, dnw_ref, dg_ref)

    blk = pl.BlockSpec((tb, D), lambda i: (i, 0))
    row = jax.ShapeDtypeStruct((1, D), F32)
    return pl.pallas_call(
        body, name=name, grid=(t // tb,), in_specs=[blk, blk, blk, blk, _row(D), _row(D)],
        out_specs=[blk, _row(D), _row(D), _row(D), _row(D)], out_shape=[jax.ShapeDtypeStruct((t, D), F32), row, row, row, row],
        compiler_params=_cparams(1),
    )(du, x, dres, gated, nw, sc)


HALO = 16


def _lane_group(shape):
    return lax.broadcasted_iota(jnp.int32, shape, 1) // 64


def _window_select(g, s2, s4, s8, s16):
    return jnp.where(g == 0, s2, jnp.where(g == 1, s4, jnp.where(g == 2, s8, s16)))


def _pool_count(t0, rows):
    g = _lane_group((rows, GW))
    win = _window_select(g, 2, 4, 8, 16)
    tt = t0 + lax.broadcasted_iota(jnp.int32, (rows, GW), 0)
    return jnp.minimum(tt + 1, win).astype(F32)


def _pool_p(v_ext, t0, tb):
    s2 = v_ext + _roll(v_ext, 1)
    s4 = s2 + _roll(s2, 2)
    s8 = s4 + _roll(s4, 4)
    s16 = s8 + _roll(s8, 8)
    ws = _window_select(_lane_group(v_ext.shape), s2, s4, s8, s16)[HALO:]
    return ws / _pool_count(t0, tb) - v_ext[HALO:]


def _sconv(q_ext, w):
    return (_roll(q_ext, 2) * w[0:1] + _roll(q_ext, 1) * w[1:2] + q_ext * w[2:3])[HALO:]


def _halo_specs(t, tb, cols, col_block):
    per = tb // HALO
    last = t // HALO - 1
    prev = pl.BlockSpec((HALO, cols), lambda i: (jnp.maximum(i * per - 1, 0), col_block))
    nxt = pl.BlockSpec((HALO, cols), lambda i: (jnp.minimum((i + 1) * per, last), col_block))
    return prev, nxt


def _f_ab(proj, pool_mat, pool_scale, sconv_w):
    t = proj.shape[0]
    tb = _tblock(t)
    prev, _ = _halo_specs(t, tb, 1024, 0)

    def body(p_ref, h_ref, pm_ref, ps_ref, sw_ref, ya_ref, yb_ref):
        i = pl.program_id(0)
        halo = jnp.where(i > 0, h_ref[...], 0.0)
        ext = jnp.concatenate([halo, p_ref[...]], axis=0)
        p = _pool_p(ext[:, 0:256], i * tb, tb)
        ya_ref[...] = _bdot(p, pm_ref[...]) * ps_ref[...]
        q_ext = ext[:, 512:768] * ext[:, 768:1024]
        yb_ref[...] = p_ref[:, 256:512] * _sconv(q_ext, sw_ref[...])

    blk = pl.BlockSpec((tb, GW), lambda i: (i, 0))
    sds = jax.ShapeDtypeStruct((t, GW), F32)
    return pl.pallas_call(
        body, name="f_ab", grid=(t // tb,),
        in_specs=[pl.BlockSpec((tb, 1024), lambda i: (i, 0)), prev, _full((GW, GW)), _row(GW), _full((3, GW))],
        out_specs=[blk, blk], out_shape=[sds, sds], compiler_params=_cparams(1),
    )(proj, proj, pool_mat, pool_scale, sconv_w)


def _b_ab(proj, dya, dyb, pool_mat, pool_scale, sconv_w):
    t = proj.shape[0]
    tb = _tblock(t)
    nb = t // tb
    prev, nxt = _halo_specs(t, tb, 1024, 0)
    _, nxt_g = _halo_specs(t, tb, GW, 0)
    n_ext = tb + HALO

    def body(p_ref, hp_ref, hn_ref, da_ref, dan_ref, db_ref, dbn_ref, pm_ref, ps_ref, sw_ref,
             o_ref, dpm_ref, dps_ref, dsw_ref):
        i = pl.program_id(0)

        @pl.when(i == 0)
        def _():
            for r in (dpm_ref, dps_ref, dsw_ref):
                r[...] = jnp.zeros_like(r)

        last = i == nb - 1
        halo = jnp.where(i > 0, hp_ref[...], 0.0)
        main = p_ref[...]
        ext = jnp.concatenate([halo, main], axis=0)
        scale = ps_ref[...]
        pm = pm_ref[...]
        p = _pool_p(ext[:, 0:256], i * tb, tb)
        da = da_ref[...]
        dps_ref[...] += _colsum(da * _bdot(p, pm))
        da_ext = jnp.concatenate([da, jnp.where(last, 0.0, dan_ref[...])], axis=0)
        dys = da_ext * scale
        dpm_ref[...] += _bdot(p, dys[:tb], TN)
        dp = _bdot(dys, pm, NT)
        dpc = dp / _pool_count(i * tb, n_ext)
        a2 = dpc + _roll(dpc, n_ext - 1)
        a4 = a2 + _roll(a2, n_ext - 2)
        a8 = a4 + _roll(a4, n_ext - 4)
        a16 = a8 + _roll(a8, n_ext - 8)
        o_ref[:, 0:256] = (_window_select(_lane_group(dpc.shape), a2, a4, a8, a16) - dp)[:tb]
        w = sw_ref[...]
        gb, gc, hh = main[:, 256:512], main[:, 512:768], main[:, 768:1024]
        q_ext = ext[:, 512:768] * ext[:, 768:1024]
        db = db_ref[...]
        o_ref[:, 256:512] = db * _sconv(q_ext, w)
        gb_next = hn_ref[:, 256:512]
        dconv = jnp.concatenate([db * gb, jnp.where(last, 0.0, dbn_ref[...] * gb_next)], axis=0)
        dq = (dconv * w[2:3] + _roll(dconv, n_ext - 1) * w[1:2] + _roll(dconv, n_ext - 2) * w[0:1])[:tb]
        o_ref[:, 512:768] = dq * hh
        o_ref[:, 768:1024] = dq * gc
        dc = dconv[:tb]
        dsw_ref[0:1, :] += _colsum(dc * _roll(q_ext, 2)[HALO:])
        dsw_ref[1:2, :] += _colsum(dc * _roll(q_ext, 1)[HALO:])
        dsw_ref[2:3, :] += _colsum(dc * q_ext[HALO:])

    blk = pl.BlockSpec((tb, GW), lambda i: (i, 0))
    return pl.pallas_call(
        body, name="b_ab", grid=(nb,),
        in_specs=[pl.BlockSpec((tb, 1024), lambda i: (i, 0)), prev, nxt, blk, nxt_g, blk, nxt_g,
                  _full((GW, GW)), _row(GW), _full((3, GW))],
        out_specs=[pl.BlockSpec((tb, 1024), lambda i: (i, 0)), _full((GW, GW)), _row(GW), _full((3, GW))],
        out_shape=[jax.ShapeDtypeStruct((t, 1024), F32), jax.ShapeDtypeStruct((GW, GW), F32),
                   jax.ShapeDtypeStruct((1, GW), F32), jax.ShapeDtypeStruct((3, GW), F32)],
        compiler_params=_cparams(1),
    )(proj, proj, proj, dya, dya, dyb, dyb, pool_mat, pool_scale, sconv_w)


CH = 8


def _ssd_conv(x, halo, w, b):
    ext = jnp.concatenate([halo, x], axis=0)
    pre = ext * w[3:4] + _roll(ext, 1) * w[2:3] + _roll(ext, 2) * w[1:2] + _roll(ext, 3) * w[0:1] + b
    return pre[CH:], ext


def _ssd_common(dt_raw, dtb, alog):
    ll = dt_raw.shape[0]
    dtv = _softplus(dt_raw + dtb)
    a_row = -jnp.exp(alog)
    r = lax.broadcasted_iota(jnp.int32, (ll, ll), 0)
    c = lax.broadcasted_iota(jnp.int32, (ll, ll), 1)
    tril = (r >= c).astype(F32)
    cs = _dot(tril, dtv * a_row, prec=HI)
    return dtv, a_row, cs, cs.T, r >= c


def _bd(a, b, ca, cb):
    return lax.dot_general(a, b, (((ca,), (cb,)), ((0,), (0,))), preferred_element_type=F32)


def _head_cols(m):
    return jnp.stack([m[:, h:h + 1] for h in range(NH)])


def _ssd_heads(act, dtv, cs, cs_t, causal):
    xs = jnp.stack([act[:, HP * h:HP * (h + 1)] for h in range(NH)])
    bm = jnp.stack([act[:, 256 + NS * (h // 2):256 + NS * (h // 2 + 1)] for h in range(NH)])
    cm = jnp.stack([act[:, 512 + NS * (h // 2):512 + NS * (h // 2 + 1)] for h in range(NH)])
    cs_c = _head_cols(cs)
    cs_r = jnp.stack([cs_t[h:h + 1, :] for h in range(NH)])
    mdec = jnp.where(causal[None], jnp.exp(jnp.minimum(cs_c - cs_r, 0.0)), 0.0)
    g2 = _bd(jnp.stack([cm[0], cm[2]]), jnp.stack([bm[0], bm[2]]), 2, 2)
    sc = jnp.stack([g2[h // 2] for h in range(NH)]) * mdec
    dt_c = _head_cols(dtv)
    xdt = xs * dt_c
    e = jnp.exp(cs_c)
    cs_last = cs_c[:, SSD_L - 1:SSD_L, :]
    wdec = jnp.exp(cs_last - cs_c)
    return xs, bm, cm, mdec, sc, dt_c, xdt, e, cs_last, wdec


def _head_scalars(row_ref):
    return jnp.stack([row_ref[0:1, h:h + 1] for h in range(NH)])


def _f_ssd(proj, dtp, conv_w, conv_b, dt_bias, a_log, d_skip):
    t = proj.shape[0]
    nc = t // SSD_L
    rows = SSD_SUB * SSD_L
    per = rows // CH

    def body(x_ref, hx_ref, dt_ref, z_ref, cw_ref, cb_ref, dtb_ref, al_ref, dk_ref, y_ref, yp_ref, sp_ref, s_ref):
        i = pl.program_id(0)

        @pl.when(i == 0)
        def _():
            s_ref[...] = jnp.zeros_like(s_ref)

        state = s_ref[...]
        dk = _head_scalars(dk_ref)
        for sub in range(SSD_SUB):
            r0 = sub * SSD_L
            rs = slice(r0, r0 + SSD_L)
            halo = jnp.where(i > 0, hx_ref[...], 0.0) if sub == 0 else x_ref[r0 - CH:r0, :]
            pre, _ = _ssd_conv(x_ref[rs, :], halo, cw_ref[...], cb_ref[...])
            act = _silu(pre)
            dtv, _, cs, cs_t, causal = _ssd_common(dt_ref[rs, :], dtb_ref[...], al_ref[...])
            xs, bm, cm, _, sc, _, xdt, e, cs_last, wdec = _ssd_heads(act, dtv, cs, cs_t, causal)
            sp_ref[sub] = state
            y = _bd(sc, xdt, 2, 1) + e * _bd(cm, state, 2, 2) + xs * dk
            for h in range(NH):
                yp_ref[rs, HP * h:HP * (h + 1)] = y[h]
            state = state * jnp.exp(cs_last) + _bd(xdt * wdec, bm, 1, 1)
            y_ref[rs, :] = yp_ref[rs, :] * _silu(z_ref[rs, :])
        s_ref[...] = state

    blk = pl.BlockSpec((rows, GW), lambda i: (i, 0))
    sds = jax.ShapeDtypeStruct((t, GW), F32)
    return pl.pallas_call(
        body, name="f_ssd", grid=(nc // SSD_SUB,),
        in_specs=[pl.BlockSpec((rows, 768), lambda i: (i, 2)),
                  pl.BlockSpec((CH, 768), lambda i: (jnp.maximum(i * per - 1, 0), 2)),
                  pl.BlockSpec((rows, DTW), lambda i: (i, 0)),
                  pl.BlockSpec((rows, GW), lambda i: (i, 4)),
                  _full((4, 768)), _row(768), _row(DTW), _row(DTW), _row(DTW)],
        out_specs=[blk, blk, pl.BlockSpec((SSD_SUB, NH, HP, NS), lambda i: (i, 0, 0, 0))],
        out_shape=[sds, sds, jax.ShapeDtypeStruct((nc, NH, HP, NS), F32)],
        scratch_shapes=[pltpu.VMEM((NH, HP, NS), F32)], compiler_params=_cparams(1),
    )(proj, proj, dtp, proj, conv_w, conv_b, dt_bias, a_log, d_skip)


def _b_ssd(proj, dtp, ypre, dyc, sprev, conv_w, conv_b, dt_bias, a_log, d_skip):
    t = proj.shape[0]
    nc = t // SSD_L
    steps = nc // SSD_SUB_BWD
    rows = SSD_SUB_BWD * SSD_L
    per = rows // CH
    n_ext = SSD_L + CH

    def chunk(sub, halo, dnext, ds_in, refs):
        (x_ref, dt_ref, z_ref, yp_ref, dy_ref, sp_ref, cw_ref, cb_ref, dtb_ref, al_ref, dk_ref,
         dz_ref, dx_ref, ddt_ref, dact_ref) = refs
        rs = slice(sub * SSD_L, (sub + 1) * SSD_L)
        dact = dact_ref.at[sub]
        w = cw_ref[...]
        pre, ext = _ssd_conv(x_ref[rs, :], halo, w, cb_ref[...])
        act = _silu(pre)
        dt_raw = dt_ref[rs, :]
        dtv, a_row, cs, cs_t, causal = _ssd_common(dt_raw, dtb_ref[...], al_ref[...])
        z = z_ref[rs, :]
        dyc_v = dy_ref[rs, :]
        dz_ref[rs, :] = dyc_v * yp_ref[rs, :] * _dsilu(z)
        dy_all = dyc_v * _silu(z)
        lane = lax.broadcasted_iota(jnp.int32, (SSD_L, DTW), 1)
        rowi = lax.broadcasted_iota(jnp.int32, (1, SSD_L, 1), 1)
        lane1 = lax.broadcasted_iota(jnp.int32, (1, DTW), 1)
        xs, bm, cm, mdec, sc, dt_c, xdt, e, cs_last, wdec = _ssd_heads(act, dtv, cs, cs_t, causal)
        dy = jnp.stack([dy_all[:, HP * h:HP * (h + 1)] for h in range(NH)])
        prev = sp_ref[sub]
        ds = ds_in
        lsum = lambda v: jnp.sum(v, axis=2, keepdims=True)
        dsc = _bd(dy, xdt, 2, 2)
        q = dsc * sc
        dg = dsc * mdec
        dxdt = _bd(sc, dy, 1, 1)
        dcs = lsum(q) - lsum(jnp.swapaxes(q, 1, 2))
        dc = _bd(dg, bm, 2, 1)
        db = _bd(dg, cm, 1, 1)
        cp = _bd(cm, prev, 2, 2)
        dcs += lsum(dy * cp) * e
        ey = e * dy
        dc += _bd(ey, prev, 2, 1)
        dprev = _bd(ey, cm, 1, 1)
        elast = jnp.exp(cs_last)
        dprev += ds * elast
        dcs_last = jnp.sum(lsum(ds * prev), axis=1, keepdims=True) * elast
        bds = _bd(bm, ds, 2, 2)
        dxdt += wdec * bds
        db += wdec * _bd(xdt, ds, 2, 1)
        dw = lsum(xdt * bds) * wdec
        dcs -= dw
        dcs_last += jnp.sum(dw, axis=1, keepdims=True)
        dcs += jnp.where(rowi == SSD_L - 1, dcs_last, 0.0)
        dxs = dxdt * dt_c + dy * _head_scalars(dk_ref)
        ddtx = lsum(dxdt * xs)
        ddk = jnp.sum(lsum(dy * xs), axis=1, keepdims=True)
        dcs_mat = jnp.zeros((SSD_L, DTW), F32)
        ddtx_mat = jnp.zeros((SSD_L, DTW), F32)
        ddk_row = jnp.zeros((1, DTW), F32)
        for h in range(NH):
            dact[:, HP * h:HP * (h + 1)] = dxs[h]
            dcs_mat = jnp.where(lane == h, dcs[h], dcs_mat)
            ddtx_mat = jnp.where(lane == h, ddtx[h], ddtx_mat)
            ddk_row = jnp.where(lane1 == h, ddk[h], ddk_row)
        for g in range(2):
            dact[:, 256 + NS * g:256 + NS * (g + 1)] = db[2 * g] + db[2 * g + 1]
            dact[:, 512 + NS * g:512 + NS * (g + 1)] = dc[2 * g] + dc[2 * g + 1]
        ds_out = dprev
        r2 = lax.broadcasted_iota(jnp.int32, (SSD_L, SSD_L), 0)
        c2 = lax.broadcasted_iota(jnp.int32, (SSD_L, SSD_L), 1)
        dadt = _dot((c2 >= r2).astype(F32), dcs_mat, prec=HI)
        ddt = jnp.where(lane < NH, (dadt * a_row + ddtx_mat) * _sig(dt_raw + dtb_ref[...]), 0.0)
        ddt_ref[rs, :] = ddt
        dpre = dact[...] * _dsilu(pre)
        dcw = jnp.concatenate([_colsum(dpre * _roll(ext, 3 - k)[CH:]) for k in range(4)], axis=0)
        dext = jnp.concatenate([dpre, dnext], axis=0)
        dx_ref[rs, :] = (dext * w[3:4] + _roll(dext, n_ext - 1) * w[2:3] + _roll(dext, n_ext - 2) * w[1:2]
                         + _roll(dext, n_ext - 3) * w[0:1])[:SSD_L]
        acc = (dcw, _colsum(dpre), _colsum(ddt), _colsum(dadt * dtv) * a_row, ddk_row)
        return dpre[0:CH], ds_out, acc

    def body(x_ref, hx_ref, dt_ref, z_ref, yp_ref, dy_ref, sp_ref, cw_ref, cb_ref, dtb_ref, al_ref, dk_ref,
             dz_ref, dx_ref, ddt_ref, dcw_ref, dcb_ref, ddtb_ref, dal_ref, ddk_ref, ds_ref, dnext_ref, dact_ref):
        i = pl.program_id(0)
        acc_refs = (dcw_ref, dcb_ref, ddtb_ref, dal_ref, ddk_ref)

        @pl.when(i == 0)
        def _():
            ds_ref[...] = jnp.zeros_like(ds_ref)
            dnext_ref[...] = jnp.zeros_like(dnext_ref)
            for r in acc_refs:
                r[...] = jnp.zeros_like(r)

        refs = (x_ref, dt_ref, z_ref, yp_ref, dy_ref, sp_ref, cw_ref, cb_ref, dtb_ref, al_ref, dk_ref, dz_ref, dx_ref, ddt_ref,
                dact_ref)
        ds = ds_ref[...]
        dnext = dnext_ref[...]
        total = None
        for sub in reversed(range(SSD_SUB_BWD)):
            if sub == 0:
                halo = jnp.where(i == steps - 1, 0.0, hx_ref[...])
            else:
                halo = x_ref[sub * SSD_L - CH:sub * SSD_L, :]
            dnext, ds, acc = chunk(sub, halo, dnext, ds, refs)
            total = acc if total is None else tuple(a + b for a, b in zip(total, acc))
        ds_ref[...] = ds
        dnext_ref[...] = dnext
        for r, v in zip(acc_refs, total):
            r[...] += v

    rev = lambda i: steps - 1 - i
    blk = lambda n, cb=0: pl.BlockSpec((rows, n), lambda i: (rev(i), cb))
    row = lambda n: jax.ShapeDtypeStruct((1, n), F32)
    return pl.pallas_call(
        body, name="b_ssd", grid=(steps,),
        in_specs=[blk(768, 2), pl.BlockSpec((CH, 768), lambda i: (jnp.maximum(rev(i) * per - 1, 0), 2)),
                  blk(DTW), blk(GW, 4), blk(GW), blk(GW), pl.BlockSpec((SSD_SUB_BWD, NH, HP, NS), lambda i: (rev(i), 0, 0, 0)),
                  _full((4, 768)), _row(768), _row(DTW), _row(DTW), _row(DTW)],
        out_specs=[blk(GW), blk(768), blk(DTW), _full((4, 768)), _row(768), _row(DTW), _row(DTW), _row(DTW)],
        out_shape=[jax.ShapeDtypeStruct((t, GW), F32), jax.ShapeDtypeStruct((t, 768), F32), jax.ShapeDtypeStruct((t, DTW), F32),
                   jax.ShapeDtypeStruct((4, 768), F32), row(768), row(DTW), row(DTW), row(DTW)],
        scratch_shapes=[pltpu.VMEM((NH, HP, NS), F32), pltpu.VMEM((CH, 768), F32), pltpu.VMEM((SSD_SUB_BWD, SSD_L, 768), F32)],
        compiler_params=_cparams(1),
    )(proj, proj, dtp, proj, ypre, dyc, sprev, conv_w, conv_b, dt_bias, a_log, d_skip)


def _s5_block(t):
    return min(t, 256)


def _seg_t():
    r = lax.broadcasted_iota(jnp.int32, (64, 1024), 0)
    c = lax.broadcasted_iota(jnp.int32, (64, 1024), 1)
    return (c // 16 == r).astype(F32)


def _s5_prep_math(a_re, a_im, lstep, b_re, b_im):
    step = jnp.exp(lstep)
    ars = a_re * step
    ais = a_im * step
    mag = jnp.exp(ars)
    lr = mag * jnp.cos(ais)
    li = mag * jnp.sin(ais)
    den = a_re * a_re + a_im * a_im
    nr = lr - 1.0
    f_re = (nr * a_re + li * a_im) / den
    f_im = (li * a_re - nr * a_im) / den
    seg = _seg_t()
    fr = _dot(f_re, seg, prec=HI)
    fi = _dot(f_im, seg, prec=HI)
    return lr, li, fr * b_re - fi * b_im, fr * b_im + fi * b_re, ars, ais


def _s5_prep(a_re, a_im, lstep, b_re, b_im):
    def body(ar, ai, ls, br, bi, lr_o, li_o, bbr_o, bbi_o, ars_o, ais_o):
        outs = _s5_prep_math(ar[...], ai[...], ls[...], br[...], bi[...])
        for o, v in zip((lr_o, li_o, bbr_o, bbi_o, ars_o, ais_o), outs):
            o[...] = v

    s64 = jax.ShapeDtypeStruct((16, 64), F32)
    s1k = jax.ShapeDtypeStruct((16, 1024), F32)
    return pl.pallas_call(body, name="s5_prep", out_shape=[s64, s64, s1k, s1k, s64, s64])(a_re, a_im, lstep, b_re, b_im)


def _s5_prep_bwd(a_re, a_im, lstep, b_re, b_im, dlr, dli, dbbr, dbbi):
    def body(ar, ai, ls, br, bi, g0, g1, g2, g3, o0, o1, o2, o3, o4):
        f = lambda *a: _s5_prep_math(*a)[:4]
        _, vjp = jax.vjp(f, ar[...], ai[...], ls[...], br[...], bi[...])
        for o, v in zip((o0, o1, o2, o3, o4), vjp((g0[...], g1[...], g2[...], g3[...]))):
            o[...] = v

    s64 = jax.ShapeDtypeStruct((16, 64), F32)
    s1k = jax.ShapeDtypeStruct((16, 1024), F32)
    return pl.pallas_call(body, name="s5_prep_bwd", out_shape=[s64, s64, jax.ShapeDtypeStruct((16, 1), F32), s1k, s1k])(
        a_re, a_im, lstep, b_re, b_im, dlr, dli, dbbr, dbbi)


SUB = 8


def _s5_tables(ars, ais):
    def body(ar, ai, tr, ti):
        rr = lax.broadcasted_iota(jnp.int32, (8 * SUB, S5_P), 0)
        seg, r = rr // SUB, rr % SUB
        step = jnp.where((seg == 1) | (seg == 4), 1, jnp.where((seg == 2) | (seg == 5), 2, 4))
        n = jnp.where(seg == 0, r + 1, jnp.where(seg == 7, SUB - r, step))
        fwd_gap = jnp.where(seg <= 3, r - step, SUB - step - 1 - r)
        gap = jnp.where((seg == 0) | (seg == 7), 0, fwd_gap)
        nf = n.astype(F32)
        mag = jnp.where(gap >= 0, jnp.exp(nf * ar[...]), 0.0)
        tr[...] = mag * jnp.cos(nf * ai[...])
        ti[...] = mag * jnp.sin(nf * ai[...])

    sds = jax.ShapeDtypeStruct((8 * SUB, S5_P), F32)
    return pl.pallas_call(body, name="s5_tables", out_shape=[sds] * 2)(ars, ais)


def _s5_table(tb_r, tb_i, k):
    return tb_r[SUB * k:SUB * (k + 1), :], tb_i[SUB * k:SUB * (k + 1), :]


def _s5_scan(bu_r, bu_i, tb_r, tb_i, c_r, c_i, lb):
    nt = lb // SUB
    sr, si = bu_r.reshape(nt, SUB, S5_P), bu_i.reshape(nt, SUB, S5_P)
    for j, k in enumerate((1, 2, 4)):
        mr, mi = _s5_table(tb_r, tb_i, 1 + j)
        tr, ti = pltpu.roll(sr, k, axis=1), pltpu.roll(si, k, axis=1)
        sr, si = sr + mr * tr - mi * ti, si + mr * ti + mi * tr
    pr, pi = _s5_table(tb_r, tb_i, 0)
    out_r, out_i = [], []
    for j in range(nt):
        a_r = sr[j] + pr * c_r - pi * c_i
        a_i = si[j] + pr * c_i + pi * c_r
        out_r.append(a_r)
        out_i.append(a_i)
        c_r, c_i = a_r[SUB - 1:SUB], a_i[SUB - 1:SUB]
    return jnp.concatenate(out_r, axis=0), jnp.concatenate(out_i, axis=0)


def _s5_rscan(g_r, g_i, tb_r, tb_i, n_r, n_i, lb):
    nt = lb // SUB
    gr, gi = g_r.reshape(nt, SUB, S5_P), g_i.reshape(nt, SUB, S5_P)
    for j, k in enumerate((1, 2, 4)):
        mr, mi = _s5_table(tb_r, tb_i, 4 + j)
        tr, ti = pltpu.roll(gr, SUB - k, axis=1), pltpu.roll(gi, SUB - k, axis=1)
        gr, gi = gr + mr * tr + mi * ti, gi + mr * ti - mi * tr
    qr, qi = _s5_table(tb_r, tb_i, 7)
    out_r, out_i = [None] * nt, [None] * nt
    for j in reversed(range(nt)):
        a_r = gr[j] + qr * n_r + qi * n_i
        a_i = gi[j] + qr * n_i - qi * n_r
        out_r[j], out_i[j] = a_r, a_i
        n_r, n_i = a_r[0:1], a_i[0:1]
    return jnp.concatenate(out_r, axis=0), jnp.concatenate(out_i, axis=0)


def _s5_y(u, sr, si, cre, cim, dsk):
    return _bdot(sr, cre) + _bdot(si, cim) + dsk * u


def _f_s5(proj, bmat, cre, cim, p_r, p_i, dsk, glu_w, glu_b):
    t = proj.shape[0]
    lb = _s5_block(t)
    nb = t // lb

    def body(u_ref, bm_ref, cr_ref, ci_ref, pr_ref, pi_ref, dk_ref, gw_ref, gb_ref, y_ref, car_ref, s_ref, st_ref):
        @pl.when(pl.program_id(0) == 0)
        def _():
            st_ref[...] = jnp.zeros_like(st_ref)

        u = u_ref[...]
        bu = _bdot(u, bm_ref[...])
        c_r, c_i = st_ref[0:1, 0:S5_P], st_ref[0:1, S5_P:]
        car_ref[0] = st_ref[0:1, :]
        sr, si = _s5_scan(bu[:, :S5_P], bu[:, S5_P:], pr_ref, pi_ref, c_r, c_i, lb)
        st_ref[0:1, 0:S5_P] = sr[lb - 1:lb]
        st_ref[0:1, S5_P:] = si[lb - 1:lb]
        sr_b, si_b = sr.astype(BF16), si.astype(BF16)
        s_ref[:, 0:S5_P] = sr_b
        s_ref[:, S5_P:] = si_b
        gel = _gelu(_s5_y(u, sr_b, si_b, cr_ref[...], ci_ref[...], dk_ref[...]))
        y_ref[...] = gel * _sig(_bdot(gel, gw_ref[...]) + gb_ref[...])

    return pl.pallas_call(
        body, name="f_s5", grid=(nb,),
        in_specs=[pl.BlockSpec((lb, GW), lambda i: (i, 5)),
                  _full((GW, 2 * S5_P)), _full((S5_P, GW)), _full((S5_P, GW)), _full((8 * SUB, S5_P)), _full((8 * SUB, S5_P)),
                  _row(GW), _full((GW, GW)), _row(GW)],
        out_specs=[pl.BlockSpec((lb, GW), lambda i: (i, 0)), pl.BlockSpec((1, 1, 2 * S5_P), lambda i: (i, 0, 0)),
                   pl.BlockSpec((lb, 2 * S5_P), lambda i: (i, 0))],
        out_shape=[jax.ShapeDtypeStruct((t, GW), F32), jax.ShapeDtypeStruct((nb, 1, 2 * S5_P), F32),
                   jax.ShapeDtypeStruct((t, 2 * S5_P), BF16)],
        scratch_shapes=[pltpu.VMEM((8, 2 * S5_P), F32)], compiler_params=_cparams(1),
    )(proj, bmat, cre, cim, p_r, p_i, dsk, glu_w, glu_b)


def _b_s5(proj, dyd, carries, states, bmat, cre, cim, p_r, p_i, dsk, glu_w, glu_b):
    t = proj.shape[0]
    lb = _s5_block(t)
    nb = t // lb

    def body(u_ref, dy_ref, car_ref, s_ref, bm_ref, cr_ref, ci_ref, pr_ref, pi_ref, dk_ref, gw_ref, gb_ref,
             du_ref, dbm_ref, dcr_ref, dci_ref, dlam_ref, ddk_ref, dgw_ref, dgb_ref, gc_ref):
        @pl.when(pl.program_id(0) == 0)
        def _():
            gc_ref[...] = jnp.zeros_like(gc_ref)
            for r in (dbm_ref, dcr_ref, dci_ref, dlam_ref, ddk_ref, dgw_ref, dgb_ref):
                r[...] = jnp.zeros_like(r)

        u = u_ref[...]
        bm = bm_ref[...]
        u_b = u.astype(BF16)
        c_r, c_i = car_ref[0, 0:1, 0:S5_P], car_ref[0, 0:1, S5_P:]
        cre_v, cim_v, dk, gw = cr_ref[...], ci_ref[...], dk_ref[...], gw_ref[...]
        sr_b, si_b = s_ref[:, 0:S5_P], s_ref[:, S5_P:]
        sr, si = sr_b.astype(F32), si_b.astype(F32)
        y = _dot(sr_b, cre_v) + _dot(si_b, cim_v) + dk * u
        gel = _gelu(y)
        gel_b = gel.astype(BF16)
        gate = _sig(_dot(gel_b, gw) + gb_ref[...])
        dout = dy_ref[...]
        t1 = dout * gel * gate * (1.0 - gate)
        t1_b = t1.astype(BF16)
        dgw_ref[...] += _dot(gel_b, t1_b, TN)
        dgb_ref[...] += _colsum(t1)
        dyv = (dout * gate + _dot(t1_b, gw, NT)) * _dgelu(y)
        dyv_b = dyv.astype(BF16)
        ddk_ref[...] += _colsum(dyv * u)
        dcr_ref[...] += _dot(sr_b, dyv_b, TN)
        dci_ref[...] += _dot(si_b, dyv_b, TN)
        gr = _dot(dyv_b, cre_v, NT)
        gi = _dot(dyv_b, cim_v, NT)
        row = lax.broadcasted_iota(jnp.int32, (lb, S5_P), 0)
        n_r, n_i = gc_ref[0:1, 0:S5_P], gc_ref[0:1, S5_P:]
        gr, gi = _s5_rscan(gr, gi, pr_ref, pi_ref, n_r, n_i, lb)
        gc_ref[0:1, 0:S5_P] = gr[0:1]
        gc_ref[0:1, S5_P:] = gi[0:1]
        gcat = jnp.concatenate([gr, gi], axis=1).astype(BF16)
        dbm_ref[...] += _dot(u_b, gcat, TN)
        du_ref[...] = dyv * dk + _dot(gcat, bm, NT)
        spr = jnp.where(row >= 1, _roll(sr, 1), c_r)
        spi = jnp.where(row >= 1, _roll(si, 1), c_i)
        dlam_ref[0:1, :] += _colsum(gr * spr + gi * spi)
        dlam_ref[1:2, :] += _colsum(gi * spr - gr * spi)

    rev = lambda i: nb - 1 - i
    return pl.pallas_call(
        body, name="b_s5", grid=(nb,),
        in_specs=[pl.BlockSpec((lb, GW), lambda i: (rev(i), 5)), pl.BlockSpec((lb, GW), lambda i: (rev(i), 0)),
                  pl.BlockSpec((1, 1, 2 * S5_P), lambda i: (rev(i), 0, 0)), pl.BlockSpec((lb, 2 * S5_P), lambda i: (rev(i), 0)),
                  _full((GW, 2 * S5_P)), _full((S5_P, GW)), _full((S5_P, GW)), _full((8 * SUB, S5_P)), _full((8 * SUB, S5_P)),
                  _row(GW), _full((GW, GW)), _row(GW)],
        out_specs=[pl.BlockSpec((lb, GW), lambda i: (rev(i), 0)), _full((GW, 2 * S5_P)), _full((S5_P, GW)), _full((S5_P, GW)),
                   _full((2, S5_P)), _row(GW), _full((GW, GW)), _row(GW)],
        out_shape=[jax.ShapeDtypeStruct((t, GW), F32), jax.ShapeDtypeStruct((GW, 2 * S5_P), F32),
                   jax.ShapeDtypeStruct((S5_P, GW), F32), jax.ShapeDtypeStruct((S5_P, GW), F32),
                   jax.ShapeDtypeStruct((2, S5_P), F32), jax.ShapeDtypeStruct((1, GW), F32),
                   jax.ShapeDtypeStruct((GW, GW), F32), jax.ShapeDtypeStruct((1, GW), F32)],
        scratch_shapes=[pltpu.VMEM((8, 2 * S5_P), F32)], compiler_params=_cparams(1),
    )(proj, dyd, carries, states, bmat, cre, cim, p_r, p_i, dsk, glu_w, glu_b)


def _group_norm(ys, bw):
    outs, stats = [], []
    for g, y in enumerate(ys):
        r, n = _rms(y)
        stats.append((r, n))
        outs.append(n * bw[:, GW * g:GW * (g + 1)])
    return jnp.concatenate(outs, axis=1), stats


def _f_out(ya, yb, yc, yd, bw, w_out, h, g1):
    t = h.shape[0]
    tb = _tblock(t)

    def body(a_ref, b_ref, c_ref, d_ref, bw_ref, w_ref, h_ref, g_ref, h2_ref, o_ref, cat_ref):
        cat, _ = _group_norm([a_ref[...], b_ref[...], c_ref[...], d_ref[...]], bw_ref[...])
        catb = cat.astype(BF16)
        cat_ref[...] = catb
        o = _dot(catb, w_ref[...])
        o_ref[...] = o
        h2_ref[...] = h_ref[...] + g_ref[...] * o

    yblk = pl.BlockSpec((tb, GW), lambda i: (i, 0))
    blk = pl.BlockSpec((tb, D), lambda i: (i, 0))
    return pl.pallas_call(
        body, name="f_out", grid=(t // tb,), in_specs=[yblk] * 4 + [_row(D), _full((D, D)), blk, _row(D)],
        out_specs=[blk, blk, blk],
        out_shape=[jax.ShapeDtypeStruct((t, D), F32), jax.ShapeDtypeStruct((t, D), F32), jax.ShapeDtypeStruct((t, D), BF16)],
        compiler_params=_cparams(1),
    )(ya, yb, yc, yd, bw, w_out, h, g1)


def _b_out(dh2, ya, yb, yc, yd, bw, w_out, g1):
    t = dh2.shape[0]
    tb = _tblock(t)

    def body(dh_ref, a_ref, b_ref, c_ref, d_ref, bw_ref, w_ref, g_ref, da_ref, db_ref, dc_ref, dd_ref, do_ref, dbw_ref):
        @pl.when(pl.program_id(0) == 0)
        def _():
            dbw_ref[...] = jnp.zeros_like(dbw_ref)

        do = (dh_ref[...] * g_ref[...]).astype(BF16)
        do_ref[...] = do
        dcat = _dot(do, w_ref[...], NT)
        bw_v = bw_ref[...]
        for g, (y_ref, dy_ref) in enumerate(((a_ref, da_ref), (b_ref, db_ref), (c_ref, dc_ref), (d_ref, dd_ref))):
            r, n = _rms(y_ref[...])
            dc = dcat[:, GW * g:GW * (g + 1)]
            dbw_ref[:, GW * g:GW * (g + 1)] += _colsum(dc * n)
            dy_ref[...] = _rms_bwd(r, n, dc * bw_v[:, GW * g:GW * (g + 1)])

    yblk = pl.BlockSpec((tb, GW), lambda i: (i, 0))
    blk = pl.BlockSpec((tb, D), lambda i: (i, 0))
    ysd = jax.ShapeDtypeStruct((t, GW), F32)
    return pl.pallas_call(
        body, name="b_out", grid=(t // tb,), in_specs=[blk] + [yblk] * 4 + [_row(D), _full((D, D)), _row(D)],
        out_specs=[yblk] * 4 + [blk, _row(D)],
        out_shape=[ysd] * 4 + [jax.ShapeDtypeStruct((t, D), BF16), jax.ShapeDtypeStruct((1, D), F32)],
        compiler_params=_cparams(1),
    )(dh2, ya, yb, yc, yd, bw, w_out, g1)


HB = 512
MLP_ROWS = 1024


def _w1_spec():
    per = HID // 4 // HB
    return pl.BlockSpec((1, D, HB), lambda i, k: (k // per, 0, k % per))


def _f_mlp(h2, nw, sc, sh, g2, w1, w2):
    t = h2.shape[0]
    tb = _tblock(t, MLP_ROWS)
    nk = HID // HB

    def body(h_ref, nw_ref, sc_ref, sh_ref, g_ref, w1_ref, w2_ref, h3_ref, m_ref, a_ref, v_ref):
        k = pl.program_id(1)

        @pl.when(k == 0)
        def _():
            _, n = _rms(h_ref[...])
            v_ref[...] = ((n * nw_ref[...]) * (1.0 + sc_ref[...]) + sh_ref[...]).astype(BF16)
            m_ref[...] = jnp.zeros_like(m_ref)

        a = _dot(v_ref[...], w1_ref[0])
        a_ref[...] = a.astype(BF16)
        ra = jnp.maximum(a, 0.0)
        m_ref[...] += _dot((ra * ra).astype(BF16), w2_ref[...])

        @pl.when(k == nk - 1)
        def _():
            h3_ref[...] = h_ref[...] + g_ref[...] * m_ref[...]

    blk = pl.BlockSpec((tb, D), lambda i, k: (i, 0))
    return pl.pallas_call(
        body, name="f_mlp", grid=(t // tb, nk),
        in_specs=[blk, _row(D), _row(D), _row(D), _row(D), _w1_spec(),
                  pl.BlockSpec((HB, D), lambda i, k: (k, 0))],
        out_specs=[blk, blk, pl.BlockSpec((tb, HB), lambda i, k: (i, k)), blk],
        out_shape=[jax.ShapeDtypeStruct((t, D), F32), jax.ShapeDtypeStruct((t, D), F32), jax.ShapeDtypeStruct((t, HID), BF16),
                   jax.ShapeDtypeStruct((t, D), BF16)],
        compiler_params=_cparams(2),
    )(h2, nw, sc, sh, g2, w1, w2)


def _b_mlp(dh3, a, g2, w1, w2):
    t = dh3.shape[0]
    tb = _tblock(t, MLP_ROWS)
    nk = HID // HB

    def body(dh_ref, a_ref, g_ref, w1_ref, w2_ref, dv_ref, da_ref, act_ref, dm_ref):
        k = pl.program_id(1)
        dm = (dh_ref[...] * g_ref[...]).astype(BF16)

        @pl.when(k == 0)
        def _():
            dm_ref[...] = dm
            dv_ref[...] = jnp.zeros_like(dv_ref)

        ra = jnp.maximum(a_ref[...].astype(F32), 0.0)
        act_ref[...] = (ra * ra).astype(BF16)
        da = (_dot(dm, w2_ref[...], NT) * (2.0 * ra)).astype(BF16)
        da_ref[...] = da
        dv_ref[...] += _dot(da, w1_ref[0], NT)

    blk = pl.BlockSpec((tb, D), lambda i, k: (i, 0))
    hblk = pl.BlockSpec((tb, HB), lambda i, k: (i, k))
    return pl.pallas_call(
        body, name="b_mlp", grid=(t // tb, nk),
        in_specs=[blk, hblk, _row(D), _w1_spec(), pl.BlockSpec((HB, D), lambda i, k: (k, 0))],
        out_specs=[blk, hblk, hblk, blk],
        out_shape=[jax.ShapeDtypeStruct((t, D), F32), jax.ShapeDtypeStruct((t, HID), BF16), jax.ShapeDtypeStruct((t, HID), BF16),
                   jax.ShapeDtypeStruct((t, D), BF16)],
        compiler_params=_cparams(2),
    )(dh3, a, g2, w1, w2)


def _b_final(h, tgt, fw):
    t = h.shape[0]
    tb = _tblock(t)

    def body(h_ref, t_ref, w_ref, dh_ref, loss_ref, dfw_ref):
        @pl.when(pl.program_id(0) == 0)
        def _():
            loss_ref[...] = jnp.zeros_like(loss_ref)
            dfw_ref[...] = jnp.zeros_like(dfw_ref)

        r, n = _rms(h_ref[...])
        wv = w_ref[...]
        err = n * wv - t_ref[...]
        loss_ref[...] += jnp.sum(err * err, keepdims=True) * (0.5 / D)
        dy = err * (1.0 / D)
        dfw_ref[...] += _colsum(dy * n)
        dh_ref[...] = _rms_bwd(r, n, dy * wv)

    blk = pl.BlockSpec((tb, D), lambda i: (i, 0))
    return pl.pallas_call(
        body, name="b_final", grid=(t // tb,), in_specs=[blk, blk, _row(D)], out_specs=[blk, _row(1), _row(D)],
        out_shape=[jax.ShapeDtypeStruct((t, D), F32), jax.ShapeDtypeStruct((1, 1), F32), jax.ShapeDtypeStruct((1, D), F32)],
        compiler_params=_cparams(1),
    )(h, tgt, fw)


_EYE16 = None


def _eye(n):
    return jnp.eye(n, dtype=F32)


def _pool_embed(pool_w):
    return jnp.einsum('gcd,gk->gckd', pool_w, _eye(4)).reshape(GW, GW)


def _pool_extract(m):
    return jnp.einsum('gcgd->gcd', m.reshape(4, 64, 4, 64))


def _bmat_embed(bb):
    return jnp.einsum('gph,gk->ghkp', bb, _eye(16)).reshape(GW, S5_P)


def _bmat_extract(m):
    return jnp.einsum('ghgp->gph', m.reshape(16, 16, 16, 64))


def _cmat_embed(cc):
    return jnp.einsum('ghp,gk->kpgh', cc, _eye(16)).reshape(S5_P, GW)


def _cmat_extract(m):
    return jnp.einsum('gpgh->ghp', m.reshape(16, 64, 16, 16))


def _pad_lanes(v, n=DTW):
    return jnp.pad(v.reshape(1, -1), ((0, 0), (0, n - v.shape[-1])))


def _w_in_layout(w_in_t):
    w_main = jnp.concatenate([w_in_t[:1280], w_in_t[2052:2308], w_in_t[1280:2048]], axis=0)
    return w_main, jnp.pad(w_in_t[2048:2052], ((0, DTW - 4), (0, 0)))


def _layer_params(p, l, mod, w_in, rest):
    q = {'rest': rest}
    q['mod'] = [mod[k:k + 1] for k in range(6)]
    q['nw1'] = p['norm_mix_w'][l:l + 1]
    q['nw2'] = p['norm_mlp_w'][l:l + 1]
    q['w_main'], q['w_dt'] = _w_in_layout(w_in)
    q['pool_mat'] = _pool_embed(p['pool_w'][l]).astype(BF16)
    q['pool_scale'] = p['pool_scale'][l:l + 1]
    q['sconv_w'] = p['sconv_w'][l]
    q['conv_w'] = p['ssd_conv_w'][l]
    q['conv_b'] = p['ssd_conv_b'][l:l + 1]
    q['dt_bias'] = _pad_lanes(p['ssd_dt_bias'][l])
    q['a_log'] = _pad_lanes(p['ssd_a_log'][l])
    q['ssd_d'] = _pad_lanes(p['ssd_d'][l])
    q['s5_raw'] = (p['s5_a_re'][l], p['s5_a_im'][l], p['s5_log_step'][l].reshape(16, 1),
                   p['s5_b_re'][l].reshape(16, 1024), p['s5_b_im'][l].reshape(16, 1024))
    q['cre'] = _cmat_embed(p['s5_c_re'][l]).astype(BF16)
    q['cim'] = (-_cmat_embed(p['s5_c_im'][l])).astype(BF16)
    q['s5_d'] = p['s5_d'][l:l + 1]
    q['glu_w'] = p['s5_glu_w'][l].astype(BF16)
    q['glu_b'] = p['s5_glu_b'][l:l + 1]
    q['bw'] = p['branch_norm_w'][l:l + 1]
    return q


def _layer_fwd(h, q):
    sh1, sc1, g1, sh2, sc2, g2 = q['mod']
    t = h.shape[0]
    s = {'h': h}
    s['proj'], s['dtp'], s['u'] = _f_in(h, q['nw1'], sc1, sh1, q['w_main'], q['w_dt'])
    s['ya'], s['yb'] = _f_ab(s['proj'], q['pool_mat'], q['pool_scale'], q['sconv_w'])
    s['yc'], s['ypre'], s['sprev'] = _f_ssd(s['proj'], s['dtp'], q['conv_w'], q['conv_b'], q['dt_bias'], q['a_log'], q['ssd_d'])
    lr, li, bbr, bbi, ars, ais = _s5_prep(*q['s5_raw'])
    s['bmat'] = jnp.concatenate([_bmat_embed(bbr.reshape(16, 64, 16)), _bmat_embed(bbi.reshape(16, 64, 16))],
                                axis=1).astype(BF16)
    s['tables'] = _s5_tables(ars.reshape(1, S5_P), ais.reshape(1, S5_P))
    s['yd'], s['carries'], s['states'] = _f_s5(s['proj'], s['bmat'], q['cre'], q['cim'], s['tables'][0], s['tables'][1],
                                  q['s5_d'], q['glu_w'], q['glu_b'])
    q['w_out'], q['w1'], q['w2'] = q['rest']((s['ya'], s['yc'], s['yd']))
    s['h2'], s['o'], s['cat'] = _f_out(s['ya'], s['yb'], s['yc'], s['yd'], q['bw'], q['w_out'], h, g1)
    h3, s['m'], s['a'], s['v'] = _f_mlp(s['h2'], q['nw2'], sc2, sh2, g2, q['w1'], q['w2'])
    return h3, s


STACKED = {'mlp_w1': (2, 4, D, HID // 4), 'mlp_w2': (2, HID, D), 'w_out': (2, D, D)}


def _layer_bwd(dh3, q, s, l, stacked, early=None):
    sh1, sc1, g1, sh2, sc2, g2 = q['mod']
    g = {}
    dv, da, act, dm = _b_mlp(dh3, s['a'], g2, q['w1'], q['w2'])
    g['mlp_w1'] = _tn_matmul(s['v'], da, "dw1", col_major=True, into=stacked['mlp_w1'], layer=l)
    g['mlp_w2'] = _tn_matmul(act, dm, "dw2", into=stacked['mlp_w2'], layer=l)
    dh2, dsc2, dsh2, dnw2, dg2 = _b_normmod(dv, s['h2'], dh3, s['m'], q['nw2'], sc2, "b_norm_mlp")
    dya, dyb, dyc, dyd, do, dbw = _b_out(dh2, s['ya'], s['yb'], s['yc'], s['yd'], q['bw'], q['w_out'], g1)
    g['w_out'] = _tn_matmul(s['cat'], do, "dwout", into=stacked['w_out'], layer=l)
    g['branch_norm_w'] = dbw[0]
    if early is not None:
        zero = early(g)[0, 0]
        q = dict(q, pool_scale=q['pool_scale'] + zero, conv_b=q['conv_b'] + zero, s5_d=q['s5_d'] + zero)
    dab, dpm, dps, dsw = _b_ab(s['proj'], dya, dyb, q['pool_mat'], q['pool_scale'], q['sconv_w'])
    g['pool_w'] = _pool_extract(dpm)
    g['pool_scale'] = dps[0]
    g['sconv_w'] = dsw
    dz, dxbc, ddt, dcw, dcb, ddtb, dal, ddk = _b_ssd(s['proj'], s['dtp'], s['ypre'], dyc, s['sprev'], q['conv_w'],
                                                     q['conv_b'], q['dt_bias'], q['a_log'], q['ssd_d'])
    g['ssd_conv_w'] = dcw
    g['ssd_conv_b'] = dcb[0]
    g['ssd_dt_bias'] = ddtb[0, :4]
    g['ssd_a_log'] = dal[0, :4]
    g['ssd_d'] = ddk[0, :4]
    tb = s['tables']
    ds5, dbmat, dcre, dcim, dlam, dd5, dgw, dgb = _b_s5(s['proj'], dyd, s['carries'], s['states'], s['bmat'], q['cre'], q['cim'],
                                                        tb[0], tb[1], q['s5_d'], q['glu_w'], q['glu_b'])
    g['s5_c_re'] = _cmat_extract(dcre)
    g['s5_c_im'] = -_cmat_extract(dcim)
    g['s5_d'] = dd5[0]
    g['s5_glu_w'] = dgw
    g['s5_glu_b'] = dgb[0]
    dbbr = _bmat_extract(dbmat[:, :S5_P]).reshape(16, 1024)
    dbbi = _bmat_extract(dbmat[:, S5_P:]).reshape(16, 1024)
    dar, dai, dls, dbr, dbi = _s5_prep_bwd(*q['s5_raw'], dlam[0].reshape(16, 64), dlam[1].reshape(16, 64), dbbr, dbbi)
    g['s5_a_re'], g['s5_a_im'], g['s5_log_step'] = dar, dai, dls[:, 0]
    g['s5_b_re'], g['s5_b_im'] = dbr, dbi
    dh, dsc1, dsh1, dnw1, dg1 = _b_in(dab, dz, dxbc, ds5, ddt, q['w_main'], q['w_dt'], s['h'], dh2, s['o'], q['nw1'], sc1)
    u = s['u']
    head = jnp.concatenate([_tn_matmul(dab, u, "dwin_ab"), _tn_matmul(dz, u, "dwin_z"), _tn_matmul(dxbc, u, "dwin_xbc"),
                            _tn_matmul(ddt, u, "dwin_dt")[:8]], axis=0)
    full = lax.dynamic_update_slice(jnp.zeros((2308, D), F32), head, (0, 0))
    g['w_in'] = lax.dynamic_update_slice(full, _tn_matmul(ds5, u, "dwin_s5"), (2052, 0))
    g['norm_mix_w'] = dnw1[0]
    g['norm_mlp_w'] = dnw2[0]
    dmod = jnp.concatenate([dsh1, dsc1, dg1, dsh2, dsc2, dg2], axis=1)
    return dh, g, dmod


def _local_step(x, tgt, p, mod, w_in_of, rest_of, early=None):
    h = x
    qs, saved = [], []
    for l in range(2):
        qs.append(_layer_params(p, l, mod[l], w_in_of(l), functools.partial(rest_of, l)))
        h, s = _layer_fwd(h, qs[l])
        saved.append(s)
    dh, loss, dfw = _b_final(h, tgt, p['final_norm_w'].reshape(1, D))
    grads = [None, None]
    dmods = [None, None]
    dh, grads[1], dmods[1] = _layer_bwd(dh, qs[1], saved[1], 1, {k: lax.empty(shp, F32) for k, shp in STACKED.items()})
    dh, grads[0], dmods[0] = _layer_bwd(dh, qs[0], saved[0], 0, grads[1], early)
    out = {k: jnp.stack([grads[0][k], grads[1][k]]) for k in grads[0] if k not in STACKED}
    if early is None:
        out.update({k: grads[0][k] for k in STACKED})
    out['final_norm_w'] = dfw[0]
    return loss, dh, out, jnp.concatenate(dmods, axis=0)


def _pack(arrs):
    parts, rows = [], 0
    for a in arrs:
        f = a.reshape(-1).astype(F32)
        pad = (-f.shape[0]) % 1024
        f = jnp.pad(f, (0, pad)) if pad else f
        parts.append(f.reshape(-1, 128))
        rows += parts[-1].shape[0]
    if rows % 256:
        parts.append(jnp.zeros((256 - rows % 256, 128), F32))
    return jnp.concatenate(parts, axis=0)


def _unpack(buf, shapes):
    out, row = [], 0
    for shp in shapes:
        n = int(math.prod(shp)) if len(shp) else 1
        rows = (n + 1023) // 1024 * 8
        out.append(buf[row:row + rows].reshape(-1)[:n].reshape(shp))
        row += rows
    return out


def _shard_of(a, axis, k):
    n = a.shape[axis] // 4
    return lax.dynamic_slice_in_dim(a, k * n, n, axis)


def kernel(x, c, norm_mix_w, norm_mlp_w, ada_w, ada_b, w_in, pool_w, pool_scale, sconv_w, ssd_conv_w, ssd_conv_b, ssd_dt_bias, ssd_a_log, ssd_d, s5_a_re, s5_a_im, s5_log_step, s5_b_re, s5_b_im, s5_c_re, s5_c_im, s5_d, s5_glu_w, s5_glu_b, branch_norm_w, w_out, mlp_w1, mlp_w2, final_norm_w, loss_target, m_norm_mix_w, m_norm_mlp_w, m_ada_w, m_ada_b, m_w_in, m_pool_w, m_pool_scale, m_sconv_w, m_ssd_conv_w, m_ssd_conv_b, m_ssd_dt_bias, m_ssd_a_log, m_ssd_d, m_s5_a_re, m_s5_a_im, m_s5_log_step, m_s5_b_re, m_s5_b_im, m_s5_c_re, m_s5_c_im, m_s5_d, m_s5_glu_w, m_s5_glu_b, m_branch_norm_w, m_w_out, m_mlp_w1, m_mlp_w2, m_final_norm_w, v_norm_mix_w, v_norm_mlp_w, v_ada_w, v_ada_b, v_w_in, v_pool_w, v_pool_scale, v_sconv_w, v_ssd_conv_w, v_ssd_conv_b, v_ssd_dt_bias, v_ssd_a_log, v_ssd_d, v_s5_a_re, v_s5_a_im, v_s5_log_step, v_s5_b_re, v_s5_b_im, v_s5_c_re, v_s5_c_im, v_s5_d, v_s5_glu_w, v_s5_glu_b, v_branch_norm_w, v_w_out, v_mlp_w1, v_mlp_w2, v_final_norm_w):
    loc = locals()
    w = {n: loc[n] for n in WEIGHTS}
    mom = {n: loc['m_' + n] for n in WEIGHTS}
    var = {n: loc['v_' + n] for n in WEIGHTS}
    ix, iy, ic = lax.axis_index("x"), lax.axis_index("y"), lax.axis_index("c")
    chip = 2 * ix + iy
    dev = 4 * ix + 2 * iy + ic

    mine_of = lambda a: lax.dynamic_index_in_dim(a.astype(BF16), ic, axis=0, keepdims=False)
    pad_in = lambda a: jnp.pad(a.T, ((0, WIN_ROWS - 577), (0, 0)))
    shard = jnp.concatenate([pad_in(mine_of(w['w_in'])), mine_of(w['w_out']), mine_of(w['mlp_w1']), mine_of(w['mlp_w2'])], axis=0)

    (c_all,) = _exchange([c], EVERYONE, False, "ag_cond", stage=True)
    c_all = c_all.reshape(8, D)
    small_sh = _exchange([w[n] for n in SMALL_SHARDED], CHIPS, False, "ag_small")
    (w_in0,) = _exchange([pad_in(w['w_in'][0].astype(BF16))], CHIPS, False, "ag_win0")
    p = {n: w[n] for n in WEIGHTS if n not in BIG}
    for n, g in zip(SMALL_SHARDED, small_sh):
        ax = SMALL_SHARDED[n]
        p[n] = jnp.concatenate([g[k] for k in range(4)], axis=ax)

    def w_in_full(sh):
        return sh[:, :577].reshape(4 * 577, D)

    big = {}

    def fetch(after):
        if not big:
            (mine,), (got,) = _split_wait(sems, shard_thru, land, after, False, "ag_big_wait")
            got = lax.dynamic_update_slice(got, mine[None], (chip, 0, 0))
            other = _pair_swap([got.reshape(-1, D)], False, "swap_big")[0].reshape(got.shape)
            big['both'] = [jnp.where(ic == l, got, other) for l in range(2)]
        return big['both']

    def w_in_of(l):
        return w_in_full(w_in0) if l == 0 else w_in_full(fetch(None)[1])

    def rest_of(l, after):
        blk = fetch(after)[l]
        r0 = WIN_ROWS
        w_out_l = blk[:, r0:r0 + 256].reshape(D, D)
        w1_l = blk[:, r0 + 256:r0 + 1280]
        w2_l = blk[:, r0 + 1280:r0 + 2304].reshape(HID, D)
        return w_out_l, w1_l, w2_l

    ada_b_sh = _shard_of(w['ada_b'], 1, chip).reshape(2, 1, 6 * D // 4)
    mod_sh = _ada_fwd(c_all, w['ada_w'], ada_b_sh)
    (mod_all,) = _exchange([mod_sh], CHIPS, False, "ag_mod", stage=True)
    mine = lax.dynamic_index_in_dim(mod_all, dev, axis=2, keepdims=False)
    sems, shard_thru, land, token = _split_start([shard], [mod_all, w_in0] + small_sh, False, "ag_big_start")
    mod = jnp.transpose(mine, (1, 0, 2)).reshape(2, 6, D) + token[0, 0]

    layer = ic.astype(jnp.int32).reshape(1)
    flight = {}

    def early(g0):
        gws = [g0['w_out'].reshape(2, 4, 256, D), g0['mlp_w1'], g0['mlp_w2'].reshape(2, 4, 1024, D)]
        got = _pair_swap([a.reshape(2, -1, D) for a in gws], True, "swap_grad", narrow=True)
        pair = [_pair_sum(a, b.reshape(a.shape[1:]), layer, "pair_sum%d" % (k + 1), BF16) for k, (a, b) in enumerate(zip(gws, got))]
        flight['sems'], flight['srcs'], flight['lands'], token = _split_start(pair, [], True, "rs_start")
        return token

    loss, grad_x, g, dmod = _local_step(x[0], loss_target[0], p, mod, w_in_of, rest_of, early)

    (dmod_all,) = _exchange([dmod], EVERYONE, False, "ag_dmod", stage=True)
    dmod_all = jnp.transpose(dmod_all, (1, 0, 2))
    g_ada_w, g_ada_b = _ada_bwd(c_all, _shard_of(dmod_all, 2, chip), dmod_all)

    sent, lands = _split_wait(flight['sems'], flight['srcs'], flight['lands'], [grad_x, g['w_in']], True, "rs_wait")
    quad = []
    for k, (land, mine) in enumerate(zip(lands, sent)):
        own = lax.dynamic_index_in_dim(mine, chip, axis=0, keepdims=True)
        quad.append(_sum_lead(lax.dynamic_update_slice(land, own, (chip, 0, 0)), "rs_chip_sum%d" % (k + 1), F32))
    gw_in = jnp.pad(g['w_in'].reshape(2, 4, 577, D), ((0, 0), (0, 0), (0, WIN_ROWS - 577), (0, 0)))
    (got_in,) = _pair_swap([gw_in.reshape(2, -1, D)], True, "swap_grad_in", narrow=True)
    pair_in = _pair_sum(gw_in, got_in.reshape(gw_in.shape[1:]), layer, "pair_sum0", BF16)
    (quad_in,) = _exchange([pair_in], CHIPS, True, "rs_chips")
    quad = [_sum_lead(quad_in, "rs_chip_sum0", F32)] + quad
    other = _pair_swap(quad, False, "swap_red")
    both = [jnp.stack([jnp.where(ic == l, a, b) for l in range(2)]) for a, b in zip(quad, other)]
    both[0] = jnp.transpose(both[0][:, :577], (0, 2, 1))
    red = dict(zip(('w_in', 'w_out', 'mlp_w1', 'mlp_w2'), both))
    red['ada_w'] = g_ada_w

    small_names = [n for n in WEIGHTS if n not in BIG and n != 'ada_b']
    pair_parts = _exchange([g[n] for n in small_names] + [loss], SIBLING, False, "ag_smallpair", stage=True)
    chip_parts = _exchange(_sum_many(pair_parts, "smallpair_sum"), CHIPS, False, "ag_smallgrad", stage=True)
    summed = _sum_many(chip_parts, "smallgrad_sum")
    for n, a in zip(small_names, summed[:-1]):
        a = a.reshape(w[n].shape) if n in ('s5_b_re', 's5_b_im') else a
        red[n] = _shard_of(a, SMALL_SHARDED[n], chip) if n in SMALL_SHARDED else a
    red['ada_b'] = g_ada_b
    loss_out = summed[-1].reshape(())

    delta, new_m, new_v = {}, {}, {}
    for n in BIG:
        delta[n], new_m[n], new_v[n] = _adamw(w[n], red[n], mom[n], var[n], "adamw_" + n)
    rest = [n for n in WEIGHTS if n not in BIG]
    lanes = lambda n, a: a.reshape(2, 16, 1024) if n in ('s5_b_re', 's5_b_im') else a
    outs = _adamw_many(*[[lanes(n, src[n]) for n in rest] for src in (w, red, mom, var)], "adamw_small")
    for k, n in enumerate(rest):
        delta[n], new_m[n], new_v[n] = (outs[3 * k + j].reshape(w[n].shape) for j in range(3))

    return (loss_out, grad_x[None], *[red[n] for n in WEIGHTS], *[delta[n] for n in WEIGHTS],
            *[new_m[n] for n in WEIGHTS], *[new_v[n] for n in WEIGHTS])
```

```python
import functools
import math

import jax
import jax.numpy as jnp
from jax import lax
from jax.experimental import pallas as pl
from jax.experimental.pallas import tpu as pltpu

F32 = jnp.float32
BF16 = jnp.bfloat16
HI = lax.Precision.HIGHEST

D = 1024
GW = 256
HID = 4096
EPS = 1e-6
PW = 2304
DTW = 128
SSD_L = 128
SSD_SUB = 2
SSD_SUB_BWD = 2
NH, HP, NS = 4, 64, 128
S5_P = 1024
MESH = pl.DeviceIdType.MESH

ADAM_LR, ADAM_B1, ADAM_B2, ADAM_EPS, ADAM_WD, ADAM_STEP = 0.001, 0.9, 0.999, 1e-08, 0.01, 10

NT = (((1,), (1,)), ((), ()))
TN = (((0,), (0,)), ((), ()))

WEIGHTS = ['norm_mix_w', 'norm_mlp_w', 'ada_w', 'ada_b', 'w_in', 'pool_w', 'pool_scale', 'sconv_w', 'ssd_conv_w',
           'ssd_conv_b', 'ssd_dt_bias', 'ssd_a_log', 'ssd_d', 's5_a_re', 's5_a_im', 's5_log_step', 's5_b_re', 's5_b_im',
           's5_c_re', 's5_c_im', 's5_d', 's5_glu_w', 's5_glu_b', 'branch_norm_w', 'w_out', 'mlp_w1', 'mlp_w2',
           'final_norm_w']
BIG = ('ada_w', 'w_in', 'w_out', 'mlp_w1', 'mlp_w2')
SMALL_SHARDED = {'sconv_w': 2, 'ssd_conv_w': 2, 's5_glu_w': 1}


def _cparams(n_axes, vmem_mb=48):
    return pltpu.CompilerParams(dimension_semantics=("arbitrary",) * n_axes, vmem_limit_bytes=vmem_mb * 1024 * 1024)


def _row(n):
    return pl.BlockSpec((1, n), lambda *_: (0, 0))


def _full(shape):
    nd = len(shape)
    return pl.BlockSpec(tuple(shape), lambda *_: (0,) * nd)


def _dot(a, b, dims=None, prec=None):
    if dims is None:
        dims = (((a.ndim - 1,), (0,)), ((), ()))
    return lax.dot_general(a, b, dims, preferred_element_type=F32, precision=prec)


def _bdot(a, b, dims=None):
    return _dot(a.astype(BF16), b.astype(BF16), dims)


def _sig(x):
    return jax.nn.sigmoid(x)


def _silu(x):
    return x * _sig(x)


def _dsilu(x):
    s = _sig(x)
    return s * (1.0 + x * (1.0 - s))


def _softplus(x):
    return jnp.maximum(x, 0.0) + jnp.log(1.0 + jnp.exp(-jnp.abs(x)))


_GK = math.sqrt(2.0 / math.pi)


def _gelu(x):
    return 0.5 * x * (1.0 + jnp.tanh(_GK * (x + 0.044715 * x * x * x)))


def _dgelu(x):
    th = jnp.tanh(_GK * (x + 0.044715 * x * x * x))
    return 0.5 * (1.0 + th) + 0.5 * x * (1.0 - th * th) * _GK * (1.0 + 3.0 * 0.044715 * x * x)


def _colsum(x):
    return jnp.sum(x, axis=0, keepdims=True)


def _rms(x):
    r = lax.rsqrt(jnp.mean(x * x, axis=-1, keepdims=True) + EPS)
    return r, x * r


def _rms_bwd(r, n, dn):
    return r * (dn - n * jnp.mean(dn * n, axis=-1, keepdims=True))


def _roll(x, k):
    n = x.shape[0]
    k = k % n
    return x if k == 0 else pltpu.roll(x, k, axis=0)


def _tblock(t, want=512):
    return min(t, want)


def _peer(mask):
    x, y, c = lax.axis_index("x"), lax.axis_index("y"), lax.axis_index("c")
    return (x ^ ((mask >> 2) & 1), y ^ ((mask >> 1) & 1), c ^ (mask & 1))


def _group_index(masks):
    x, y, c = lax.axis_index("x"), lax.axis_index("y"), lax.axis_index("c")
    full = 0
    for m in masks:
        full |= m
    bits = [b for b in (4, 2, 1) if full & b]

    def idx(px, py, pc):
        v = {4: px, 2: py, 1: pc}
        out = 0
        for b in bits:
            out = out * 2 + v[b]
        return out

    return idx(x, y, c), [idx(*_peer(m)) for m in masks]


def _exchange(arrs, masks, scatter, name, stage=False):
    n_arr, n_peer, n_grp = len(arrs), len(masks), len(masks) + 1

    def body(*refs):
        ins, outs = refs[:n_arr], refs[n_arr:2 * n_arr]
        send_sems, recv_sems, local_sems = refs[2 * n_arr:2 * n_arr + 3]
        if stage:
            bufs, load_sems = refs[2 * n_arr + 3:3 * n_arr + 3], refs[3 * n_arr + 3]
            loads = [pltpu.make_async_copy(ins[t], bufs[t], load_sems.at[t]) for t in range(n_arr)]
            for ld in loads:
                ld.start()
            for ld in loads:
                ld.wait()
            ins = bufs
        me, peer_idx = _group_index(masks)
        copies = []
        for t in range(n_arr):
            src_me = ins[t].at[me] if scatter else ins[t]
            loc = pltpu.make_async_copy(src_me, outs[t].at[me], local_sems.at[t])
            loc.start()
            copies.append(loc)
            for j, m in enumerate(masks):
                src = ins[t].at[peer_idx[j]] if scatter else ins[t]
                cp = pltpu.make_async_remote_copy(src_ref=src, dst_ref=outs[t].at[me], send_sem=send_sems.at[t, j],
                                                  recv_sem=recv_sems.at[t, j], device_id=_peer(m), device_id_type=MESH)
                cp.start()
                copies.append(cp)
        for cp in copies:
            cp.wait()

    hbm = pl.BlockSpec(memory_space=pl.ANY)
    out_shape = [jax.ShapeDtypeStruct((n_grp,) + (a.shape[1:] if scatter else a.shape), a.dtype) for a in arrs]
    staging = [pltpu.VMEM(a.shape, a.dtype) for a in arrs] + [pltpu.SemaphoreType.DMA((n_arr,))] if stage else []
    outs = pl.pallas_call(
        body, name=name, in_specs=[hbm] * n_arr, out_specs=[hbm] * n_arr, out_shape=out_shape,
        scratch_shapes=[pltpu.SemaphoreType.DMA((n_arr, n_peer)), pltpu.SemaphoreType.DMA((n_arr, n_peer)),
                        pltpu.SemaphoreType.DMA((n_arr,))] + staging,
        compiler_params=pltpu.CompilerParams(vmem_limit_bytes=48 * 1024 * 1024),
    )(*arrs)
    return list(outs)


def _split_copies(src_refs, land_refs, sems, scatter):
    me, peer_idx = _group_index(CHIPS)
    n = len(CHIPS) * len(src_refs)
    copies = []
    for t, (src_ref, land_ref) in enumerate(zip(src_refs, land_refs)):
        for j, m in enumerate(CHIPS):
            k = len(CHIPS) * t + j
            copies.append(pltpu.make_async_remote_copy(
                src_ref=src_ref.at[peer_idx[j]] if scatter else src_ref, dst_ref=land_ref.at[me], send_sem=sems[k],
                recv_sem=sems[n + k], device_id=_peer(m), device_id_type=MESH))
    return copies


def _split_start(srcs, after, scatter, name):
    n_arr, n_sem = len(srcs), 2 * len(CHIPS) * len(srcs)

    def body(*refs):
        src_refs, land_refs = refs[:n_arr], refs[n_arr:2 * n_arr]
        outs = refs[2 * n_arr + len(after):]
        for cp in _split_copies(src_refs, land_refs, outs[:n_sem], scatter):
            cp.start()
        outs[-1][...] = jnp.zeros_like(outs[-1])

    hbm = pl.BlockSpec(memory_space=pltpu.HBM)
    sem = pl.BlockSpec(memory_space=pltpu.SEMAPHORE)
    lands = [lax.empty((len(CHIPS) + 1,) + (a.shape[1:] if scatter else a.shape), a.dtype) for a in srcs]
    as_hbm = lambda a: pltpu.with_memory_space_constraint(a, pltpu.HBM)
    outs = pl.pallas_call(
        body, name=name,
        out_shape=(pltpu.SemaphoreType.DMA(()),) * n_sem + tuple(pltpu.HBM(a.shape, a.dtype) for a in srcs + lands)
        + (jax.ShapeDtypeStruct((8, 128), F32),),
        in_specs=(hbm,) * (2 * n_arr) + (pl.BlockSpec(memory_space=pl.ANY),) * len(after),
        out_specs=(sem,) * n_sem + (hbm,) * (2 * n_arr) + (pl.BlockSpec(memory_space=pltpu.VMEM),),
        input_output_aliases={t: n_sem + t for t in range(2 * n_arr)},
        compiler_params=pltpu.CompilerParams(has_side_effects=pltpu.SideEffectType.DATAFLOW_SIDE_EFFECTING),
    )(*[as_hbm(a) for a in srcs + lands], *after)
    return outs[:n_sem], list(outs[n_sem:n_sem + n_arr]), list(outs[n_sem + n_arr:n_sem + 2 * n_arr]), outs[-1]


def _split_wait(sems, srcs, lands, after, scatter, name):
    n_arr, n_sem = len(srcs), len(sems)

    def body(*refs):
        src_refs, land_refs = refs[:n_arr], refs[n_arr:2 * n_arr]
        for cp in _split_copies(src_refs, land_refs, refs[2 * n_arr:2 * n_arr + n_sem], scatter):
            cp.wait_send()
            cp.wait_recv()

    hbm = pl.BlockSpec(memory_space=pltpu.HBM)
    sem = pl.BlockSpec(memory_space=pltpu.SEMAPHORE)
    outs = pl.pallas_call(
        body, name=name, out_shape=tuple(pltpu.HBM(a.shape, a.dtype) for a in srcs + lands),
        in_specs=(hbm,) * (2 * n_arr) + (sem,) * n_sem + (pl.BlockSpec(memory_space=pl.ANY),) * len(after),
        out_specs=(hbm,) * (2 * n_arr), input_output_aliases={t: t for t in range(2 * n_arr)},
        compiler_params=pltpu.CompilerParams(has_side_effects=pltpu.SideEffectType.DATAFLOW_SIDE_EFFECTING),
    )(*srcs, *lands, *sems, *after)
    return list(outs[:n_arr]), list(outs[n_arr:])


CHIPS = (4, 2, 6)
EVERYONE = (1, 2, 3, 4, 5, 6, 7)
SIBLING = (1,)
SWAP_ROWS = 512
WIN_ROWS = 592


def _pair_swap(arrs, other_layer, name, narrow=False):
    n_arr = len(arrs)
    shapes = [a.shape[-2:] for a in arrs]
    out_dtypes = [BF16 if narrow else a.dtype for a in arrs]
    chunks = []
    for t, (rows, _) in enumerate(shapes):
        assert rows % 16 == 0
        for j, r0 in enumerate(range(0, rows, SWAP_ROWS)):
            chunks.append((t, r0, min(SWAP_ROWS, rows - r0), j % 2))

    def body(*refs):
        ins, outs = refs[:n_arr], refs[n_arr:2 * n_arr]
        bufs = refs[2 * n_arr:3 * n_arr]
        out_bufs = refs[3 * n_arr:4 * n_arr] if narrow else bufs
        load_sems, send_sems, recv_sems = refs[-3:]
        sibling = _peer(1)
        c = lax.axis_index("c")

        def load(k):
            t, r0, n, slot = chunks[k]
            src = ins[t].at[1 - c] if other_layer else ins[t]
            return pltpu.make_async_copy(src.at[pl.ds(r0, n)], bufs[t].at[slot, pl.ds(0, n)], load_sems.at[t, slot])

        def send(k):
            t, r0, n, slot = chunks[k]
            return pltpu.make_async_remote_copy(src_ref=out_bufs[t].at[slot, pl.ds(0, n)], dst_ref=outs[t].at[pl.ds(r0, n)],
                                                send_sem=send_sems.at[t, slot], recv_sem=recv_sems.at[t],
                                                device_id=sibling, device_id_type=MESH)

        in_flight = {}

        def drain(k):
            key = (chunks[k][0], chunks[k][3])
            if key in in_flight:
                send(in_flight.pop(key)).wait_send()

        def start_load(k):
            if not narrow:
                drain(k)
            load(k).start()

        start_load(0)
        for k in range(len(chunks)):
            t, _, n, slot = chunks[k]
            load(k).wait()
            if k + 1 < len(chunks):
                start_load(k + 1)
            if narrow:
                drain(k)
                out_bufs[t][slot, pl.ds(0, n), :] = bufs[t][slot, pl.ds(0, n), :].astype(BF16)
            send(k).start()
            in_flight[(t, slot)] = k
        for k in in_flight.values():
            send(k).wait_send()
        for t in range(n_arr):
            pltpu.make_async_remote_copy(src_ref=outs[t], dst_ref=outs[t], send_sem=send_sems.at[t, 0],
                                         recv_sem=recv_sems.at[t], device_id=sibling, device_id_type=MESH).wait_recv()

    hbm = pl.BlockSpec(memory_space=pl.ANY)
    outs = pl.pallas_call(
        body, name=name, in_specs=[hbm] * n_arr, out_specs=[hbm] * n_arr,
        out_shape=[jax.ShapeDtypeStruct(s, dt) for s, dt in zip(shapes, out_dtypes)],
        scratch_shapes=[pltpu.VMEM((2, min(SWAP_ROWS, s[0]), s[1]), a.dtype) for s, a in zip(shapes, arrs)]
        + ([pltpu.VMEM((2, min(SWAP_ROWS, s[0]), s[1]), BF16) for s in shapes] if narrow else [])
        + [pltpu.SemaphoreType.DMA((n_arr, 2)), pltpu.SemaphoreType.DMA((n_arr, 2)), pltpu.SemaphoreType.DMA((n_arr,))],
        compiler_params=pltpu.CompilerParams(vmem_limit_bytes=48 * 1024 * 1024),
    )(*arrs)
    return list(outs)


def _sum_lead(a, name, out_dtype):
    n = a.shape[0]
    shape = a.shape[1:]

    def body(a_ref, o_ref):
        acc = a_ref[0].astype(F32)
        for k in range(1, n):
            acc = acc + a_ref[k].astype(F32)
        o_ref[...] = acc.astype(out_dtype)

    if len(shape) == 3:
        blk = (1,) + shape[1:]
        return pl.pallas_call(
            body, name=name, grid=(shape[0],), in_specs=[pl.BlockSpec((n,) + blk, lambda i: (0, i, 0, 0))],
            out_specs=pl.BlockSpec(blk, lambda i: (i, 0, 0)), out_shape=jax.ShapeDtypeStruct(shape, out_dtype),
            compiler_params=_cparams(1),
        )(a)
    rows, cols = shape
    rb = rows
    for cand in (512, 256, 128):
        if rows % cand == 0 and rows > cand:
            rb = cand
            break
    return pl.pallas_call(
        body, name=name, grid=(rows // rb,), in_specs=[pl.BlockSpec((n, rb, cols), lambda i: (0, i, 0))],
        out_specs=pl.BlockSpec((rb, cols), lambda i: (i, 0)), out_shape=jax.ShapeDtypeStruct((rows, cols), out_dtype),
        compiler_params=_cparams(1),
    )(a)


def _pair_sum(g, recv, layer, name, out_dtype):
    _, n, r, c = g.shape

    def body(l_ref, g_ref, r_ref, o_ref):
        o_ref[...] = (g_ref[0].astype(F32) + r_ref[...].astype(F32)).astype(out_dtype)

    return pl.pallas_call(
        body, name=name,
        grid_spec=pltpu.PrefetchScalarGridSpec(
            num_scalar_prefetch=1, grid=(n,),
            in_specs=[pl.BlockSpec((1, 1, r, c), lambda i, l: (l[0], i, 0, 0)), pl.BlockSpec((1, r, c), lambda i, l: (i, 0, 0))],
            out_specs=pl.BlockSpec((1, r, c), lambda i, l: (i, 0, 0))),
        out_shape=jax.ShapeDtypeStruct((n, r, c), out_dtype), compiler_params=_cparams(1),
    )(layer, g, recv)


def _tn_matmul(a, b, name, col_major=False, into=None, layer=0):
    t, k = a.shape
    n = b.shape[1]
    tb = _tblock(t, 1024)
    kb = min(k, 1024)
    nb = min(n, 1024)
    grid = (k // kb, n // nb, t // tb)
    lead = (into is not None) + col_major

    def body(a_ref, b_ref, *rest):
        o_ref = rest[-1]
        for _ in range(lead):
            o_ref = o_ref.at[0]

        @pl.when(pl.program_id(2) == 0)
        def _():
            o_ref[...] = jnp.zeros_like(o_ref)

        o_ref[...] += _bdot(a_ref[...], b_ref[...], TN)

    if col_major:
        block, index, shape = (1, kb, nb), (lambda ki, ni: (ni, ki, 0)), (n // nb, k, nb)
    else:
        block, index, shape = (kb, nb), (lambda ki, ni: (ki, ni)), (k, n)
    in_specs = [pl.BlockSpec((tb, kb), lambda ki, ni, ti: (ti, ki)), pl.BlockSpec((tb, nb), lambda ki, ni, ti: (ti, ni))]
    if into is None:
        return pl.pallas_call(
            body, name=name, grid=grid, in_specs=in_specs, out_specs=pl.BlockSpec(block, lambda ki, ni, ti: index(ki, ni)),
            out_shape=jax.ShapeDtypeStruct(shape, F32), compiler_params=_cparams(3),
        )(a, b)
    assert into.shape == (2,) + shape
    return pl.pallas_call(
        body, name=name, grid=grid, in_specs=in_specs + [pl.BlockSpec(memory_space=pl.ANY)],
        out_specs=pl.BlockSpec((1,) + block, lambda ki, ni, ti: (layer,) + index(ki, ni)),
        out_shape=jax.ShapeDtypeStruct(into.shape, F32), input_output_aliases={2: 0}, compiler_params=_cparams(3),
    )(a, b, into)


def _sum_many(arrs, name):
    k = len(arrs)

    def body(*refs):
        for a_ref, o_ref in zip(refs[:k], refs[k:]):
            acc = a_ref[0]
            for j in range(1, a_ref.shape[0]):
                acc = acc + a_ref[j]
            o_ref[...] = acc

    return pl.pallas_call(body, name=name, grid=(1,), in_specs=[_full(a.shape) for a in arrs],
                          out_specs=[_full(a.shape[1:]) for a in arrs],
                          out_shape=[jax.ShapeDtypeStruct(a.shape[1:], F32) for a in arrs], compiler_params=_cparams(1))(*arrs)


def _adamw_math(w, g, m, v):
    m2 = ADAM_B1 * m + (1.0 - ADAM_B1) * g
    v2 = ADAM_B2 * v + (1.0 - ADAM_B2) * (g * g)
    m_hat = m2 / (1.0 - ADAM_B1 ** ADAM_STEP)
    v_hat = v2 / (1.0 - ADAM_B2 ** ADAM_STEP)
    return -ADAM_LR * (m_hat / (jnp.sqrt(v_hat) + ADAM_EPS) + ADAM_WD * w), m2, v2


def _adamw_many(ws, gs, ms, vs, name):
    n = len(ws)

    def body(*refs):
        ins, outs = refs[:4 * n], refs[4 * n:]
        for k in range(n):
            res = _adamw_math(ins[k][...], ins[n + k][...], ins[2 * n + k][...], ins[3 * n + k][...])
            for j in range(3):
                outs[3 * k + j][...] = res[j]

    out_shape = []
    for a in ws:
        out_shape += [jax.ShapeDtypeStruct(a.shape, F32)] * 3
    return pl.pallas_call(body, name=name, grid=(1,), in_specs=[_full(a.shape) for a in ws] * 4,
                          out_specs=[_full(s.shape) for s in out_shape], out_shape=out_shape,
                          compiler_params=_cparams(1))(*ws, *gs, *ms, *vs)


def _adamw(w, g, m, v, name):
    shape = w.shape
    cols = shape[-1]
    rows = int(math.prod(shape[:-1]))
    rb = rows
    for cand in (256, 128, 64, 32, 16, 8):
        if rows % cand == 0 and rows > cand:
            rb = cand
            break
    bc1 = 1.0 - ADAM_B1 ** ADAM_STEP
    bc2 = 1.0 - ADAM_B2 ** ADAM_STEP

    def body(w_ref, g_ref, m_ref, v_ref, d_ref, nm_ref, nv_ref):
        gg = g_ref[...]
        m2 = ADAM_B1 * m_ref[...] + (1.0 - ADAM_B1) * gg
        v2 = ADAM_B2 * v_ref[...] + (1.0 - ADAM_B2) * (gg * gg)
        m_hat = m2 / bc1
        v_hat = v2 / bc2
        d_ref[...] = -ADAM_LR * (m_hat / (jnp.sqrt(v_hat) + ADAM_EPS) + ADAM_WD * w_ref[...])
        nm_ref[...] = m2
        nv_ref[...] = v2

    spec = pl.BlockSpec((rb, cols), lambda i: (i, 0))
    sds = jax.ShapeDtypeStruct((rows, cols), F32)
    outs = pl.pallas_call(
        body, name=name, grid=(rows // rb,), in_specs=[spec] * 4, out_specs=[spec] * 3, out_shape=[sds] * 3,
        compiler_params=_cparams(1),
    )(*(z.reshape(rows, cols) for z in (w, g, m, v)))
    return tuple(o.reshape(shape) for o in outs)


def _ada_fwd(c_all, ada_w_sh, ada_b_sh):
    s = ada_w_sh.shape[2]
    sb = 512

    def body(c_ref, w_ref, b_ref, o_ref):
        cond = _silu(c_ref[...])
        o_ref[0] = _bdot(cond, w_ref[0]) + b_ref[0]

    return pl.pallas_call(
        body, name="ada_fwd", grid=(2, s // sb),
        in_specs=[_full((8, D)), pl.BlockSpec((1, D, sb), lambda l, j: (l, 0, j)), pl.BlockSpec((1, 1, sb), lambda l, j: (l, 0, j))],
        out_specs=pl.BlockSpec((1, 8, sb), lambda l, j: (l, 0, j)), out_shape=jax.ShapeDtypeStruct((2, 8, s), F32),
        compiler_params=_cparams(2),
    )(c_all, ada_w_sh, ada_b_sh)


def _ada_bwd(c_all, dmod_sh, dmod_all):
    s = dmod_sh.shape[2]
    sb = 512

    def body(c_ref, d_ref, o_ref):
        cond = _silu(c_ref[...])
        o_ref[0] = _bdot(cond, d_ref[0], TN)

    gw = pl.pallas_call(
        body, name="ada_bwd_w", grid=(2, s // sb),
        in_specs=[_full((8, D)), pl.BlockSpec((1, 8, sb), lambda l, j: (l, 0, j))],
        out_specs=pl.BlockSpec((1, D, sb), lambda l, j: (l, 0, j)), out_shape=jax.ShapeDtypeStruct((2, D, s), F32),
        compiler_params=_cparams(2),
    )(c_all, dmod_sh)

    def body_b(d_ref, o_ref):
        acc = d_ref[0, 0:1, :]
        for k in range(1, 8):
            acc = acc + d_ref[0, k:k + 1, :]
        o_ref[0] = acc

    gb = pl.pallas_call(
        body_b, name="ada_bwd_b", grid=(2,), in_specs=[pl.BlockSpec((1, 8, 6 * D), lambda l: (l, 0, 0))],
        out_specs=pl.BlockSpec((1, 1, 6 * D), lambda l: (l, 0, 0)), out_shape=jax.ShapeDtypeStruct((2, 1, 6 * D), F32),
        compiler_params=_cparams(1),
    )(dmod_all)
    return gw, gb.reshape(2, 6 * D)


def _f_in(h, nw, sc, sh, w_main, w_dt):
    t = h.shape[0]
    tb = _tblock(t)

    def body(h_ref, nw_ref, sc_ref, sh_ref, w_ref, wd_ref, p_ref, dt_ref, u_ref):
        _, n = _rms(h_ref[...])
        u = ((n * nw_ref[...]) * (1.0 + sc_ref[...]) + sh_ref[...]).astype(BF16)
        u_ref[...] = u
        p_ref[...] = _dot(u, w_ref[...], NT)
        dt_ref[...] = _dot(u, wd_ref[...], NT)

    return pl.pallas_call(
        body, name="f_in", grid=(t // tb,),
        in_specs=[pl.BlockSpec((tb, D), lambda i: (i, 0)), _row(D), _row(D), _row(D), _full((PW, D)), _full((DTW, D))],
        out_specs=[pl.BlockSpec((tb, PW), lambda i: (i, 0)), pl.BlockSpec((tb, DTW), lambda i: (i, 0)),
                   pl.BlockSpec((tb, D), lambda i: (i, 0))],
        out_shape=[jax.ShapeDtypeStruct((t, PW), F32), jax.ShapeDtypeStruct((t, DTW), F32), jax.ShapeDtypeStruct((t, D), BF16)],
        compiler_params=_cparams(1),
    )(h, nw, sc, sh, w_main, w_dt)


def _norm_bwd_step(du_v, x, dres_v, gated, nwv, scv, dx_ref, dsc_ref, dsh_ref, dnw_ref, dg_ref):
    r, n = _rms(x)
    scale = 1.0 + scv
    dsc_ref[...] += _colsum(du_v * (n * nwv))
    dsh_ref[...] += _colsum(du_v)
    dnw_ref[...] += _colsum(du_v * scale * n)
    dg_ref[...] += _colsum(dres_v * gated)
    dx_ref[...] = dres_v + _rms_bwd(r, n, du_v * scale * nwv)


def _b_in(dab, dz, dxbc, ds5, ddt, w_main, w_dt, x, dres, gated, nw, sc):
    t = dab.shape[0]
    tb = _tblock(t)

    def body(a_ref, z_ref, x_ref, s_ref, d_ref, w_ref, wd_ref, h_ref, dr_ref, g_ref, nw_ref, sc_ref,
             dx_ref, dsc_ref, dsh_ref, dnw_ref, dg_ref):
        @pl.when(pl.program_id(0) == 0)
        def _():
            for r in (dsc_ref, dsh_ref, dnw_ref, dg_ref):
                r[...] = jnp.zeros_like(r)

        du = _bdot(a_ref[...], w_ref[0:1024, :])
        du += _bdot(z_ref[...], w_ref[1024:1280, :])
        du += _bdot(s_ref[...], w_ref[1280:1536, :])
        du += _bdot(x_ref[...], w_ref[1536:2304, :])
        du += _bdot(d_ref[...], wd_ref[...])
        _norm_bwd_step(du, h_ref[...], dr_ref[...], g_ref[...], nw_ref[...], sc_ref[...], dx_ref, dsc_ref, dsh_ref, dnw_ref, dg_ref)

    blk = lambda n: pl.BlockSpec((tb, n), lambda i: (i, 0))
    row = jax.ShapeDtypeStruct((1, D), F32)
    return pl.pallas_call(
        body, name="b_in", grid=(t // tb,),
        in_specs=[blk(1024), blk(256), blk(768), blk(256), blk(DTW), _full((PW, D)), _full((DTW, D)),
                  blk(D), blk(D), blk(D), _row(D), _row(D)],
        out_specs=[blk(D), _row(D), _row(D), _row(D), _row(D)],
        out_shape=[jax.ShapeDtypeStruct((t, D), F32), row, row, row, row], compiler_params=_cparams(1),
    )(dab, dz, dxbc, ds5, ddt, w_main, w_dt, x, dres, gated, nw, sc)


def _b_normmod(du, x, dres, gated, nw, sc, name):
    t = x.shape[0]
    tb = _tblock(t)

    def body(du_ref, x_ref, dr_ref, g_ref, nw_ref, sc_ref, dx_ref, dsc_ref, dsh_ref, dnw_ref, dg_ref):
        @pl.when(pl.program_id(0) == 0)
        def _():
            for r in (dsc_ref, dsh_ref, dnw_ref, dg_ref):
                r[...] = jnp.zeros_like(r)

        _norm_bwd_step(du_ref[...], x_ref[...], dr_ref[...], g_ref[...], nw_ref[...], sc_ref[...],
                       dx_ref, dsc_ref, dsh_ref, dnw_ref, dg_ref)

    blk = pl.BlockSpec((tb, D), lambda i: (i, 0))
    row = jax.ShapeDtypeStruct((1, D), F32)
    return pl.pallas_call(
        body, name=name, grid=(t // tb,), in_specs=[blk, blk, blk, blk, _row(D), _row(D)],
        out_specs=[blk, _row(D), _row(D), _row(D), _row(D)], out_shape=[jax.ShapeDtypeStruct((t, D), F32), row, row, row, row],
        compiler_params=_cparams(1),
    )(du, x, dres, gated, nw, sc)


HALO = 16


def _lane_group(shape):
    return lax.broadcasted_iota(jnp.int32, shape, 1) // 64


def _window_select(g, s2, s4, s8, s16):
    return jnp.where(g == 0, s2, jnp.where(g == 1, s4, jnp.where(g == 2, s8, s16)))


def _pool_count(t0, rows):
    g = _lane_group((rows, GW))
    win = _window_select(g, 2, 4, 8, 16)
    tt = t0 + lax.broadcasted_iota(jnp.int32, (rows, GW), 0)
    return jnp.minimum(tt + 1, win).astype(F32)


def _pool_p(v_ext, t0, tb):
    s2 = v_ext + _roll(v_ext, 1)
    s4 = s2 + _roll(s2, 2)
    s8 = s4 + _roll(s4, 4)
    s16 = s8 + _roll(s8, 8)
    ws = _window_select(_lane_group(v_ext.shape), s2, s4, s8, s16)[HALO:]
    return ws / _pool_count(t0, tb) - v_ext[HALO:]


def _sconv(q_ext, w):
    return (_roll(q_ext, 2) * w[0:1] + _roll(q_ext, 1) * w[1:2] + q_ext * w[2:3])[HALO:]


def _halo_specs(t, tb, cols, col_block):
    per = tb // HALO
    last = t // HALO - 1
    prev = pl.BlockSpec((HALO, cols), lambda i: (jnp.maximum(i * per - 1, 0), col_block))
    nxt = pl.BlockSpec((HALO, cols), lambda i: (jnp.minimum((i + 1) * per, last), col_block))
    return prev, nxt


def _f_ab(proj, pool_mat, pool_scale, sconv_w):
    t = proj.shape[0]
    tb = _tblock(t)
    prev, _ = _halo_specs(t, tb, 1024, 0)

    def body(p_ref, h_ref, pm_ref, ps_ref, sw_ref, ya_ref, yb_ref):
        i = pl.program_id(0)
        halo = jnp.where(i > 0, h_ref[...], 0.0)
        ext = jnp.concatenate([halo, p_ref[...]], axis=0)
        p = _pool_p(ext[:, 0:256], i * tb, tb)
        ya_ref[...] = _bdot(p, pm_ref[...]) * ps_ref[...]
        q_ext = ext[:, 512:768] * ext[:, 768:1024]
        yb_ref[...] = p_ref[:, 256:512] * _sconv(q_ext, sw_ref[...])

    blk = pl.BlockSpec((tb, GW), lambda i: (i, 0))
    sds = jax.ShapeDtypeStruct((t, GW), F32)
    return pl.pallas_call(
        body, name="f_ab", grid=(t // tb,),
        in_specs=[pl.BlockSpec((tb, 1024), lambda i: (i, 0)), prev, _full((GW, GW)), _row(GW), _full((3, GW))],
        out_specs=[blk, blk], out_shape=[sds, sds], compiler_params=_cparams(1),
    )(proj, proj, pool_mat, pool_scale, sconv_w)


def _b_ab(proj, dya, dyb, pool_mat, pool_scale, sconv_w):
    t = proj.shape[0]
    tb = _tblock(t)
    nb = t // tb
    prev, nxt = _halo_specs(t, tb, 1024, 0)
    _, nxt_g = _halo_specs(t, tb, GW, 0)
    n_ext = tb + HALO

    def body(p_ref, hp_ref, hn_ref, da_ref, dan_ref, db_ref, dbn_ref, pm_ref, ps_ref, sw_ref,
             o_ref, dpm_ref, dps_ref, dsw_ref):
        i = pl.program_id(0)

        @pl.when(i == 0)
        def _():
            for r in (dpm_ref, dps_ref, dsw_ref):
                r[...] = jnp.zeros_like(r)

        last = i == nb - 1
        halo = jnp.where(i > 0, hp_ref[...], 0.0)
        main = p_ref[...]
        ext = jnp.concatenate([halo, main], axis=0)
        scale = ps_ref[...]
        pm = pm_ref[...]
        p = _pool_p(ext[:, 0:256], i * tb, tb)
        da = da_ref[...]
        dps_ref[...] += _colsum(da * _bdot(p, pm))
        da_ext = jnp.concatenate([da, jnp.where(last, 0.0, dan_ref[...])], axis=0)
        dys = da_ext * scale
        dpm_ref[...] += _bdot(p, dys[:tb], TN)
        dp = _bdot(dys, pm, NT)
        dpc = dp / _pool_count(i * tb, n_ext)
        a2 = dpc + _roll(dpc, n_ext - 1)
        a4 = a2 + _roll(a2, n_ext - 2)
        a8 = a4 + _roll(a4, n_ext - 4)
        a16 = a8 + _roll(a8, n_ext - 8)
        o_ref[:, 0:256] = (_window_select(_lane_group(dpc.shape), a2, a4, a8, a16) - dp)[:tb]
        w = sw_ref[...]
        gb, gc, hh = main[:, 256:512], main[:, 512:768], main[:, 768:1024]
        q_ext = ext[:, 512:768] * ext[:, 768:1024]
        db = db_ref[...]
        o_ref[:, 256:512] = db * _sconv(q_ext, w)
        gb_next = hn_ref[:, 256:512]
        dconv = jnp.concatenate([db * gb, jnp.where(last, 0.0, dbn_ref[...] * gb_next)], axis=0)
        dq = (dconv * w[2:3] + _roll(dconv, n_ext - 1) * w[1:2] + _roll(dconv, n_ext - 2) * w[0:1])[:tb]
        o_ref[:, 512:768] = dq * hh
        o_ref[:, 768:1024] = dq * gc
        dc = dconv[:tb]
        dsw_ref[0:1, :] += _colsum(dc * _roll(q_ext, 2)[HALO:])
        dsw_ref[1:2, :] += _colsum(dc * _roll(q_ext, 1)[HALO:])
        dsw_ref[2:3, :] += _colsum(dc * q_ext[HALO:])

    blk = pl.BlockSpec((tb, GW), lambda i: (i, 0))
    return pl.pallas_call(
        body, name="b_ab", grid=(nb,),
        in_specs=[pl.BlockSpec((tb, 1024), lambda i: (i, 0)), prev, nxt, blk, nxt_g, blk, nxt_g,
                  _full((GW, GW)), _row(GW), _full((3, GW))],
        out_specs=[pl.BlockSpec((tb, 1024), lambda i: (i, 0)), _full((GW, GW)), _row(GW), _full((3, GW))],
        out_shape=[jax.ShapeDtypeStruct((t, 1024), F32), jax.ShapeDtypeStruct((GW, GW), F32),
                   jax.ShapeDtypeStruct((1, GW), F32), jax.ShapeDtypeStruct((3, GW), F32)],
        compiler_params=_cparams(1),
    )(proj, proj, proj, dya, dya, dyb, dyb, pool_mat, pool_scale, sconv_w)


CH = 8


def _ssd_conv(x, halo, w, b):
    ext = jnp.concatenate([halo, x], axis=0)
    pre = ext * w[3:4] + _roll(ext, 1) * w[2:3] + _roll(ext, 2) * w[1:2] + _roll(ext, 3) * w[0:1] + b
    return pre[CH:], ext


def _ssd_common(dt_raw, dtb, alog):
    ll = dt_raw.shape[0]
    dtv = _softplus(dt_raw + dtb)
    a_row = -jnp.exp(alog)
    r = lax.broadcasted_iota(jnp.int32, (ll, ll), 0)
    c = lax.broadcasted_iota(jnp.int32, (ll, ll), 1)
    tril = (r >= c).astype(F32)
    cs = _dot(tril, dtv * a_row, prec=HI)
    return dtv, a_row, cs, cs.T, r >= c


def _bd(a, b, ca, cb):
    return lax.dot_general(a, b, (((ca,), (cb,)), ((0,), (0,))), preferred_element_type=F32)


def _head_cols(m):
    return jnp.stack([m[:, h:h + 1] for h in range(NH)])


def _ssd_heads(act, dtv, cs, cs_t, causal):
    xs = jnp.stack([act[:, HP * h:HP * (h + 1)] for h in range(NH)])
    bm = jnp.stack([act[:, 256 + NS * (h // 2):256 + NS * (h // 2 + 1)] for h in range(NH)])
    cm = jnp.stack([act[:, 512 + NS * (h // 2):512 + NS * (h // 2 + 1)] for h in range(NH)])
    cs_c = _head_cols(cs)
    cs_r = jnp.stack([cs_t[h:h + 1, :] for h in range(NH)])
    mdec = jnp.where(causal[None], jnp.exp(jnp.minimum(cs_c - cs_r, 0.0)), 0.0)
    g2 = _bd(jnp.stack([cm[0], cm[2]]), jnp.stack([bm[0], bm[2]]), 2, 2)
    sc = jnp.stack([g2[h // 2] for h in range(NH)]) * mdec
    dt_c = _head_cols(dtv)
    xdt = xs * dt_c
    e = jnp.exp(cs_c)
    cs_last = cs_c[:, SSD_L - 1:SSD_L, :]
    wdec = jnp.exp(cs_last - cs_c)
    return xs, bm, cm, mdec, sc, dt_c, xdt, e, cs_last, wdec


def _head_scalars(row_ref):
    return jnp.stack([row_ref[0:1, h:h + 1] for h in range(NH)])


def _f_ssd(proj, dtp, conv_w, conv_b, dt_bias, a_log, d_skip):
    t = proj.shape[0]
    nc = t // SSD_L
    rows = SSD_SUB * SSD_L
    per = rows // CH

    def body(x_ref, hx_ref, dt_ref, z_ref, cw_ref, cb_ref, dtb_ref, al_ref, dk_ref, y_ref, yp_ref, sp_ref, s_ref):
        i = pl.program_id(0)

        @pl.when(i == 0)
        def _():
            s_ref[...] = jnp.zeros_like(s_ref)

        state = s_ref[...]
        dk = _head_scalars(dk_ref)
        for sub in range(SSD_SUB):
            r0 = sub * SSD_L
            rs = slice(r0, r0 + SSD_L)
            halo = jnp.where(i > 0, hx_ref[...], 0.0) if sub == 0 else x_ref[r0 - CH:r0, :]
            pre, _ = _ssd_conv(x_ref[rs, :], halo, cw_ref[...], cb_ref[...])
            act = _silu(pre)
            dtv, _, cs, cs_t, causal = _ssd_common(dt_ref[rs, :], dtb_ref[...], al_ref[...])
            xs, bm, cm, _, sc, _, xdt, e, cs_last, wdec = _ssd_heads(act, dtv, cs, cs_t, causal)
            sp_ref[sub] = state
            y = _bd(sc, xdt, 2, 1) + e * _bd(cm, state, 2, 2) + xs * dk
            for h in range(NH):
                yp_ref[rs, HP * h:HP * (h + 1)] = y[h]
            state = state * jnp.exp(cs_last) + _bd(xdt * wdec, bm, 1, 1)
            y_ref[rs, :] = yp_ref[rs, :] * _silu(z_ref[rs, :])
        s_ref[...] = state

    blk = pl.BlockSpec((rows, GW), lambda i: (i, 0))
    sds = jax.ShapeDtypeStruct((t, GW), F32)
    return pl.pallas_call(
        body, name="f_ssd", grid=(nc // SSD_SUB,),
        in_specs=[pl.BlockSpec((rows, 768), lambda i: (i, 2)),
                  pl.BlockSpec((CH, 768), lambda i: (jnp.maximum(i * per - 1, 0), 2)),
                  pl.BlockSpec((rows, DTW), lambda i: (i, 0)),
                  pl.BlockSpec((rows, GW), lambda i: (i, 4)),
                  _full((4, 768)), _row(768), _row(DTW), _row(DTW), _row(DTW)],
        out_specs=[blk, blk, pl.BlockSpec((SSD_SUB, NH, HP, NS), lambda i: (i, 0, 0, 0))],
        out_shape=[sds, sds, jax.ShapeDtypeStruct((nc, NH, HP, NS), F32)],
        scratch_shapes=[pltpu.VMEM((NH, HP, NS), F32)], compiler_params=_cparams(1),
    )(proj, proj, dtp, proj, conv_w, conv_b, dt_bias, a_log, d_skip)


def _b_ssd(proj, dtp, ypre, dyc, sprev, conv_w, conv_b, dt_bias, a_log, d_skip):
    t = proj.shape[0]
    nc = t // SSD_L
    steps = nc // SSD_SUB_BWD
    rows = SSD_SUB_BWD * SSD_L
    per = rows // CH
    n_ext = SSD_L + CH

    def chunk(sub, halo, dnext, ds_in, refs):
        (x_ref, dt_ref, z_ref, yp_ref, dy_ref, sp_ref, cw_ref, cb_ref, dtb_ref, al_ref, dk_ref,
         dz_ref, dx_ref, ddt_ref, dact_ref) = refs
        rs = slice(sub * SSD_L, (sub + 1) * SSD_L)
        dact = dact_ref.at[sub]
        w = cw_ref[...]
        pre, ext = _ssd_conv(x_ref[rs, :], halo, w, cb_ref[...])
        act = _silu(pre)
        dt_raw = dt_ref[rs, :]
        dtv, a_row, cs, cs_t, causal = _ssd_common(dt_raw, dtb_ref[...], al_ref[...])
        z = z_ref[rs, :]
        dyc_v = dy_ref[rs, :]
        dz_ref[rs, :] = dyc_v * yp_ref[rs, :] * _dsilu(z)
        dy_all = dyc_v * _silu(z)
        lane = lax.broadcasted_iota(jnp.int32, (SSD_L, DTW), 1)
        rowi = lax.broadcasted_iota(jnp.int32, (1, SSD_L, 1), 1)
        lane1 = lax.broadcasted_iota(jnp.int32, (1, DTW), 1)
        xs, bm, cm, mdec, sc, dt_c, xdt, e, cs_last, wdec = _ssd_heads(act, dtv, cs, cs_t, causal)
        dy = jnp.stack([dy_all[:, HP * h:HP * (h + 1)] for h in range(NH)])
        prev = sp_ref[sub]
        ds = ds_in
        lsum = lambda v: jnp.sum(v, axis=2, keepdims=True)
        dsc = _bd(dy, xdt, 2, 2)
        q = dsc * sc
        dg = dsc * mdec
        dxdt = _bd(sc, dy, 1, 1)
        dcs = lsum(q) - lsum(jnp.swapaxes(q, 1, 2))
        dc = _bd(dg, bm, 2, 1)
        db = _bd(dg, cm, 1, 1)
        cp = _bd(cm, prev, 2, 2)
        dcs += lsum(dy * cp) * e
        ey = e * dy
        dc += _bd(ey, prev, 2, 1)
        dprev = _bd(ey, cm, 1, 1)
        elast = jnp.exp(cs_last)
        dprev += ds * elast
        dcs_last = jnp.sum(lsum(ds * prev), axis=1, keepdims=True) * elast
        bds = _bd(bm, ds, 2, 2)
        dxdt += wdec * bds
        db += wdec * _bd(xdt, ds, 2, 1)
        dw = lsum(xdt * bds) * wdec
        dcs -= dw
        dcs_last += jnp.sum(dw, axis=1, keepdims=True)
        dcs += jnp.where(rowi == SSD_L - 1, dcs_last, 0.0)
        dxs = dxdt * dt_c + dy * _head_scalars(dk_ref)
        ddtx = lsum(dxdt * xs)
        ddk = jnp.sum(lsum(dy * xs), axis=1, keepdims=True)
        dcs_mat = jnp.zeros((SSD_L, DTW), F32)
        ddtx_mat = jnp.zeros((SSD_L, DTW), F32)
        ddk_row = jnp.zeros((1, DTW), F32)
        for h in range(NH):
            dact[:, HP * h:HP * (h + 1)] = dxs[h]
            dcs_mat = jnp.where(lane == h, dcs[h], dcs_mat)
            ddtx_mat = jnp.where(lane == h, ddtx[h], ddtx_mat)
            ddk_row = jnp.where(lane1 == h, ddk[h], ddk_row)
        for g in range(2):
            dact[:, 256 + NS * g:256 + NS * (g + 1)] = db[2 * g] + db[2 * g + 1]
            dact[:, 512 + NS * g:512 + NS * (g + 1)] = dc[2 * g] + dc[2 * g + 1]
        ds_out = dprev
        r2 = lax.broadcasted_iota(jnp.int32, (SSD_L, SSD_L), 0)
        c2 = lax.broadcasted_iota(jnp.int32, (SSD_L, SSD_L), 1)
        dadt = _dot((c2 >= r2).astype(F32), dcs_mat, prec=HI)
        ddt = jnp.where(lane < NH, (dadt * a_row + ddtx_mat) * _sig(dt_raw + dtb_ref[...]), 0.0)
        ddt_ref[rs, :] = ddt
        dpre = dact[...] * _dsilu(pre)
        dcw = jnp.concatenate([_colsum(dpre * _roll(ext, 3 - k)[CH:]) for k in range(4)], axis=0)
        dext = jnp.concatenate([dpre, dnext], axis=0)
        dx_ref[rs, :] = (dext * w[3:4] + _roll(dext, n_ext - 1) * w[2:3] + _roll(dext, n_ext - 2) * w[1:2]
                         + _roll(dext, n_ext - 3) * w[0:1])[:SSD_L]
        acc = (dcw, _colsum(dpre), _colsum(ddt), _colsum(dadt * dtv) * a_row, ddk_row)
        return dpre[0:CH], ds_out, acc

    def body(x_ref, hx_ref, dt_ref, z_ref, yp_ref, dy_ref, sp_ref, cw_ref, cb_ref, dtb_ref, al_ref, dk_ref,
             dz_ref, dx_ref, ddt_ref, dcw_ref, dcb_ref, ddtb_ref, dal_ref, ddk_ref, ds_ref, dnext_ref, dact_ref):
        i = pl.program_id(0)
        acc_refs = (dcw_ref, dcb_ref, ddtb_ref, dal_ref, ddk_ref)

        @pl.when(i == 0)
        def _():
            ds_ref[...] = jnp.zeros_like(ds_ref)
            dnext_ref[...] = jnp.zeros_like(dnext_ref)
            for r in acc_refs:
                r[...] = jnp.zeros_like(r)

        refs = (x_ref, dt_ref, z_ref, yp_ref, dy_ref, sp_ref, cw_ref, cb_ref, dtb_ref, al_ref, dk_ref, dz_ref, dx_ref, ddt_ref,
                dact_ref)
        ds = ds_ref[...]
        dnext = dnext_ref[...]
        total = None
        for sub in reversed(range(SSD_SUB_BWD)):
            if sub == 0:
                halo = jnp.where(i == steps - 1, 0.0, hx_ref[...])
            else:
                halo = x_ref[sub * SSD_L - CH:sub * SSD_L, :]
            dnext, ds, acc = chunk(sub, halo, dnext, ds, refs)
            total = acc if total is None else tuple(a + b for a, b in zip(total, acc))
        ds_ref[...] = ds
        dnext_ref[...] = dnext
        for r, v in zip(acc_refs, total):
            r[...] += v

    rev = lambda i: steps - 1 - i
    blk = lambda n, cb=0: pl.BlockSpec((rows, n), lambda i: (rev(i), cb))
    row = lambda n: jax.ShapeDtypeStruct((1, n), F32)
    return pl.pallas_call(
        body, name="b_ssd", grid=(steps,),
        in_specs=[blk(768, 2), pl.BlockSpec((CH, 768), lambda i: (jnp.maximum(rev(i) * per - 1, 0), 2)),
                  blk(DTW), blk(GW, 4), blk(GW), blk(GW), pl.BlockSpec((SSD_SUB_BWD, NH, HP, NS), lambda i: (rev(i), 0, 0, 0)),
                  _full((4, 768)), _row(768), _row(DTW), _row(DTW), _row(DTW)],
        out_specs=[blk(GW), blk(768), blk(DTW), _full((4, 768)), _row(768), _row(DTW), _row(DTW), _row(DTW)],
        out_shape=[jax.ShapeDtypeStruct((t, GW), F32), jax.ShapeDtypeStruct((t, 768), F32), jax.ShapeDtypeStruct((t, DTW), F32),
                   jax.ShapeDtypeStruct((4, 768), F32), row(768), row(DTW), row(DTW), row(DTW)],
        scratch_shapes=[pltpu.VMEM((NH, HP, NS), F32), pltpu.VMEM((CH, 768), F32), pltpu.VMEM((SSD_SUB_BWD, SSD_L, 768), F32)],
        compiler_params=_cparams(1),
    )(proj, proj, dtp, proj, ypre, dyc, sprev, conv_w, conv_b, dt_bias, a_log, d_skip)


def _s5_block(t):
    return min(t, 256)


def _seg_t():
    r = lax.broadcasted_iota(jnp.int32, (64, 1024), 0)
    c = lax.broadcasted_iota(jnp.int32, (64, 1024), 1)
    return (c // 16 == r).astype(F32)


def _s5_prep_math(a_re, a_im, lstep, b_re, b_im):
    step = jnp.exp(lstep)
    ars = a_re * step
    ais = a_im * step
    mag = jnp.exp(ars)
    lr = mag * jnp.cos(ais)
    li = mag * jnp.sin(ais)
    den = a_re * a_re + a_im * a_im
    nr = lr - 1.0
    f_re = (nr * a_re + li * a_im) / den
    f_im = (li * a_re - nr * a_im) / den
    seg = _seg_t()
    fr = _dot(f_re, seg, prec=HI)
    fi = _dot(f_im, seg, prec=HI)
    return lr, li, fr * b_re - fi * b_im, fr * b_im + fi * b_re, ars, ais


def _s5_prep(a_re, a_im, lstep, b_re, b_im):
    def body(ar, ai, ls, br, bi, lr_o, li_o, bbr_o, bbi_o, ars_o, ais_o):
        outs = _s5_prep_math(ar[...], ai[...], ls[...], br[...], bi[...])
        for o, v in zip((lr_o, li_o, bbr_o, bbi_o, ars_o, ais_o), outs):
            o[...] = v

    s64 = jax.ShapeDtypeStruct((16, 64), F32)
    s1k = jax.ShapeDtypeStruct((16, 1024), F32)
    return pl.pallas_call(body, name="s5_prep", out_shape=[s64, s64, s1k, s1k, s64, s64])(a_re, a_im, lstep, b_re, b_im)


def _s5_prep_bwd(a_re, a_im, lstep, b_re, b_im, dlr, dli, dbbr, dbbi):
    def body(ar, ai, ls, br, bi, g0, g1, g2, g3, o0, o1, o2, o3, o4):
        f = lambda *a: _s5_prep_math(*a)[:4]
        _, vjp = jax.vjp(f, ar[...], ai[...], ls[...], br[...], bi[...])
        for o, v in zip((o0, o1, o2, o3, o4), vjp((g0[...], g1[...], g2[...], g3[...]))):
            o[...] = v

    s64 = jax.ShapeDtypeStruct((16, 64), F32)
    s1k = jax.ShapeDtypeStruct((16, 1024), F32)
    return pl.pallas_call(body, name="s5_prep_bwd", out_shape=[s64, s64, jax.ShapeDtypeStruct((16, 1), F32), s1k, s1k])(
        a_re, a_im, lstep, b_re, b_im, dlr, dli, dbbr, dbbi)


SUB = 8


def _s5_tables(ars, ais):
    def body(ar, ai, tr, ti):
        rr = lax.broadcasted_iota(jnp.int32, (8 * SUB, S5_P), 0)
        seg, r = rr // SUB, rr % SUB
        step = jnp.where((seg == 1) | (seg == 4), 1, jnp.where((seg == 2) | (seg == 5), 2, 4))
        n = jnp.where(seg == 0, r + 1, jnp.where(seg == 7, SUB - r, step))
        fwd_gap = jnp.where(seg <= 3, r - step, SUB - step - 1 - r)
        gap = jnp.where((seg == 0) | (seg == 7), 0, fwd_gap)
        nf = n.astype(F32)
        mag = jnp.where(gap >= 0, jnp.exp(nf * ar[...]), 0.0)
        tr[...] = mag * jnp.cos(nf * ai[...])
        ti[...] = mag * jnp.sin(nf * ai[...])

    sds = jax.ShapeDtypeStruct((8 * SUB, S5_P), F32)
    return pl.pallas_call(body, name="s5_tables", out_shape=[sds] * 2)(ars, ais)


def _s5_table(tb_r, tb_i, k):
    return tb_r[SUB * k:SUB * (k + 1), :], tb_i[SUB * k:SUB * (k + 1), :]


def _s5_scan(bu_r, bu_i, tb_r, tb_i, c_r, c_i, lb):
    nt = lb // SUB
    sr, si = bu_r.reshape(nt, SUB, S5_P), bu_i.reshape(nt, SUB, S5_P)
    for j, k in enumerate((1, 2, 4)):
        mr, mi = _s5_table(tb_r, tb_i, 1 + j)
        tr, ti = pltpu.roll(sr, k, axis=1), pltpu.roll(si, k, axis=1)
        sr, si = sr + mr * tr - mi * ti, si + mr * ti + mi * tr
    pr, pi = _s5_table(tb_r, tb_i, 0)
    out_r, out_i = [], []
    for j in range(nt):
        a_r = sr[j] + pr * c_r - pi * c_i
        a_i = si[j] + pr * c_i + pi * c_r
        out_r.append(a_r)
        out_i.append(a_i)
        c_r, c_i = a_r[SUB - 1:SUB], a_i[SUB - 1:SUB]
    return jnp.concatenate(out_r, axis=0), jnp.concatenate(out_i, axis=0)


def _s5_rscan(g_r, g_i, tb_r, tb_i, n_r, n_i, lb):
    nt = lb // SUB
    gr, gi = g_r.reshape(nt, SUB, S5_P), g_i.reshape(nt, SUB, S5_P)
    for j, k in enumerate((1, 2, 4)):
        mr, mi = _s5_table(tb_r, tb_i, 4 + j)
        tr, ti = pltpu.roll(gr, SUB - k, axis=1), pltpu.roll(gi, SUB - k, axis=1)
        gr, gi = gr + mr * tr + mi * ti, gi + mr * ti - mi * tr
    qr, qi = _s5_table(tb_r, tb_i, 7)
    out_r, out_i = [None] * nt, [None] * nt
    for j in reversed(range(nt)):
        a_r = gr[j] + qr * n_r + qi * n_i
        a_i = gi[j] + qr * n_i - qi * n_r
        out_r[j], out_i[j] = a_r, a_i
        n_r, n_i = a_r[0:1], a_i[0:1]
    return jnp.concatenate(out_r, axis=0), jnp.concatenate(out_i, axis=0)


def _s5_y(u, sr, si, cre, cim, dsk):
    return _bdot(sr, cre) + _bdot(si, cim) + dsk * u


def _f_s5(proj, bmat, cre, cim, p_r, p_i, dsk, glu_w, glu_b):
    t = proj.shape[0]
    lb = _s5_block(t)
    nb = t // lb

    def body(u_ref, bm_ref, cr_ref, ci_ref, pr_ref, pi_ref, dk_ref, gw_ref, gb_ref, y_ref, car_ref, s_ref, st_ref):
        @pl.when(pl.program_id(0) == 0)
        def _():
            st_ref[...] = jnp.zeros_like(st_ref)

        u = u_ref[...]
        bu = _bdot(u, bm_ref[...])
        c_r, c_i = st_ref[0:1, 0:S5_P], st_ref[0:1, S5_P:]
        car_ref[0] = st_ref[0:1, :]
        sr, si = _s5_scan(bu[:, :S5_P], bu[:, S5_P:], pr_ref, pi_ref, c_r, c_i, lb)
        st_ref[0:1, 0:S5_P] = sr[lb - 1:lb]
        st_ref[0:1, S5_P:] = si[lb - 1:lb]
        sr_b, si_b = sr.astype(BF16), si.astype(BF16)
        s_ref[:, 0:S5_P] = sr_b
        s_ref[:, S5_P:] = si_b
        gel = _gelu(_s5_y(u, sr_b, si_b, cr_ref[...], ci_ref[...], dk_ref[...]))
        y_ref[...] = gel * _sig(_bdot(gel, gw_ref[...]) + gb_ref[...])

    return pl.pallas_call(
        body, name="f_s5", grid=(nb,),
        in_specs=[pl.BlockSpec((lb, GW), lambda i: (i, 5)),
                  _full((GW, 2 * S5_P)), _full((S5_P, GW)), _full((S5_P, GW)), _full((8 * SUB, S5_P)), _full((8 * SUB, S5_P)),
                  _row(GW), _full((GW, GW)), _row(GW)],
        out_specs=[pl.BlockSpec((lb, GW), lambda i: (i, 0)), pl.BlockSpec((1, 1, 2 * S5_P), lambda i: (i, 0, 0)),
                   pl.BlockSpec((lb, 2 * S5_P), lambda i: (i, 0))],
        out_shape=[jax.ShapeDtypeStruct((t, GW), F32), jax.ShapeDtypeStruct((nb, 1, 2 * S5_P), F32),
                   jax.ShapeDtypeStruct((t, 2 * S5_P), BF16)],
        scratch_shapes=[pltpu.VMEM((8, 2 * S5_P), F32)], compiler_params=_cparams(1),
    )(proj, bmat, cre, cim, p_r, p_i, dsk, glu_w, glu_b)


def _b_s5(proj, dyd, carries, states, bmat, cre, cim, p_r, p_i, dsk, glu_w, glu_b):
    t = proj.shape[0]
    lb = _s5_block(t)
    nb = t // lb

    def body(u_ref, dy_ref, car_ref, s_ref, bm_ref, cr_ref, ci_ref, pr_ref, pi_ref, dk_ref, gw_ref, gb_ref,
             du_ref, dbm_ref, dcr_ref, dci_ref, dlam_ref, ddk_ref, dgw_ref, dgb_ref, gc_ref):
        @pl.when(pl.program_id(0) == 0)
        def _():
            gc_ref[...] = jnp.zeros_like(gc_ref)
            for r in (dbm_ref, dcr_ref, dci_ref, dlam_ref, ddk_ref, dgw_ref, dgb_ref):
                r[...] = jnp.zeros_like(r)

        u = u_ref[...]
        bm = bm_ref[...]
        u_b = u.astype(BF16)
        c_r, c_i = car_ref[0, 0:1, 0:S5_P], car_ref[0, 0:1, S5_P:]
        cre_v, cim_v, dk, gw = cr_ref[...], ci_ref[...], dk_ref[...], gw_ref[...]
        sr_b, si_b = s_ref[:, 0:S5_P], s_ref[:, S5_P:]
        sr, si = sr_b.astype(F32), si_b.astype(F32)
        y = _dot(sr_b, cre_v) + _dot(si_b, cim_v) + dk * u
        gel = _gelu(y)
        gel_b = gel.astype(BF16)
        gate = _sig(_dot(gel_b, gw) + gb_ref[...])
        dout = dy_ref[...]
        t1 = dout * gel * gate * (1.0 - gate)
        t1_b = t1.astype(BF16)
        dgw_ref[...] += _dot(gel_b, t1_b, TN)
        dgb_ref[...] += _colsum(t1)
        dyv = (dout * gate + _dot(t1_b, gw, NT)) * _dgelu(y)
        dyv_b = dyv.astype(BF16)
        ddk_ref[...] += _colsum(dyv * u)
        dcr_ref[...] += _dot(sr_b, dyv_b, TN)
        dci_ref[...] += _dot(si_b, dyv_b, TN)
        gr = _dot(dyv_b, cre_v, NT)
        gi = _dot(dyv_b, cim_v, NT)
        row = lax.broadcasted_iota(jnp.int32, (lb, S5_P), 0)
        n_r, n_i = gc_ref[0:1, 0:S5_P], gc_ref[0:1, S5_P:]
        gr, gi = _s5_rscan(gr, gi, pr_ref, pi_ref, n_r, n_i, lb)
        gc_ref[0:1, 0:S5_P] = gr[0:1]
        gc_ref[0:1, S5_P:] = gi[0:1]
        gcat = jnp.concatenate([gr, gi], axis=1).astype(BF16)
        dbm_ref[...] += _dot(u_b, gcat, TN)
        du_ref[...] = dyv * dk + _dot(gcat, bm, NT)
        spr = jnp.where(row >= 1, _roll(sr, 1), c_r)
        spi = jnp.where(row >= 1, _roll(si, 1), c_i)
        dlam_ref[0:1, :] += _colsum(gr * spr + gi * spi)
        dlam_ref[1:2, :] += _colsum(gi * spr - gr * spi)

    rev = lambda i: nb - 1 - i
    return pl.pallas_call(
        body, name="b_s5", grid=(nb,),
        in_specs=[pl.BlockSpec((lb, GW), lambda i: (rev(i), 5)), pl.BlockSpec((lb, GW), lambda i: (rev(i), 0)),
                  pl.BlockSpec((1, 1, 2 * S5_P), lambda i: (rev(i), 0, 0)), pl.BlockSpec((lb, 2 * S5_P), lambda i: (rev(i), 0)),
                  _full((GW, 2 * S5_P)), _full((S5_P, GW)), _full((S5_P, GW)), _full((8 * SUB, S5_P)), _full((8 * SUB, S5_P)),
                  _row(GW), _full((GW, GW)), _row(GW)],
        out_specs=[pl.BlockSpec((lb, GW), lambda i: (rev(i), 0)), _full((GW, 2 * S5_P)), _full((S5_P, GW)), _full((S5_P, GW)),
                   _full((2, S5_P)), _row(GW), _full((GW, GW)), _row(GW)],
        out_shape=[jax.ShapeDtypeStruct((t, GW), F32), jax.ShapeDtypeStruct((GW, 2 * S5_P), F32),
                   jax.ShapeDtypeStruct((S5_P, GW), F32), jax.ShapeDtypeStruct((S5_P, GW), F32),
                   jax.ShapeDtypeStruct((2, S5_P), F32), jax.ShapeDtypeStruct((1, GW), F32),
                   jax.ShapeDtypeStruct((GW, GW), F32), jax.ShapeDtypeStruct((1, GW), F32)],
        scratch_shapes=[pltpu.VMEM((8, 2 * S5_P), F32)], compiler_params=_cparams(1),
    )(proj, dyd, carries, states, bmat, cre, cim, p_r, p_i, dsk, glu_w, glu_b)


def _group_norm(ys, bw):
    outs, stats = [], []
    for g, y in enumerate(ys):
        r, n = _rms(y)
        stats.append((r, n))
        outs.append(n * bw[:, GW * g:GW * (g + 1)])
    return jnp.concatenate(outs, axis=1), stats


def _f_out(ya, yb, yc, yd, bw, w_out, h, g1):
    t = h.shape[0]
    tb = _tblock(t)

    def body(a_ref, b_ref, c_ref, d_ref, bw_ref, w_ref, h_ref, g_ref, h2_ref, o_ref, cat_ref):
        cat, _ = _group_norm([a_ref[...], b_ref[...], c_ref[...], d_ref[...]], bw_ref[...])
        catb = cat.astype(BF16)
        cat_ref[...] = catb
        o = _dot(catb, w_ref[...])
        o_ref[...] = o.astype(BF16)
        h2_ref[...] = h_ref[...] + g_ref[...] * o

    yblk = pl.BlockSpec((tb, GW), lambda i: (i, 0))
    blk = pl.BlockSpec((tb, D), lambda i: (i, 0))
    return pl.pallas_call(
        body, name="f_out", grid=(t // tb,), in_specs=[yblk] * 4 + [_row(D), _full((D, D)), blk, _row(D)],
        out_specs=[blk, blk, blk],
        out_shape=[jax.ShapeDtypeStruct((t, D), F32), jax.ShapeDtypeStruct((t, D), BF16), jax.ShapeDtypeStruct((t, D), BF16)],
        compiler_params=_cparams(1),
    )(ya, yb, yc, yd, bw, w_out, h, g1)


def _b_out(dh2, ya, yb, yc, yd, bw, w_out, g1):
    t = dh2.shape[0]
    tb = _tblock(t)

    def body(dh_ref, a_ref, b_ref, c_ref, d_ref, bw_ref, w_ref, g_ref, da_ref, db_ref, dc_ref, dd_ref, do_ref, dbw_ref):
        @pl.when(pl.program_id(0) == 0)
        def _():
            dbw_ref[...] = jnp.zeros_like(dbw_ref)

        do = (dh_ref[...] * g_ref[...]).astype(BF16)
        do_ref[...] = do
        dcat = _dot(do, w_ref[...], NT)
        bw_v = bw_ref[...]
        for g, (y_ref, dy_ref) in enumerate(((a_ref, da_ref), (b_ref, db_ref), (c_ref, dc_ref), (d_ref, dd_ref))):
            r, n = _rms(y_ref[...])
            dc = dcat[:, GW * g:GW * (g + 1)]
            dbw_ref[:, GW * g:GW * (g + 1)] += _colsum(dc * n)
            dy_ref[...] = _rms_bwd(r, n, dc * bw_v[:, GW * g:GW * (g + 1)])

    yblk = pl.BlockSpec((tb, GW), lambda i: (i, 0))
    blk = pl.BlockSpec((tb, D), lambda i: (i, 0))
    ysd = jax.ShapeDtypeStruct((t, GW), F32)
    return pl.pallas_call(
        body, name="b_out", grid=(t // tb,), in_specs=[blk] + [yblk] * 4 + [_row(D), _full((D, D)), _row(D)],
        out_specs=[yblk] * 4 + [blk, _row(D)],
        out_shape=[ysd] * 4 + [jax.ShapeDtypeStruct((t, D), BF16), jax.ShapeDtypeStruct((1, D), F32)],
        compiler_params=_cparams(1),
    )(dh2, ya, yb, yc, yd, bw, w_out, g1)


HB = 512
MLP_ROWS = 1024


def _w1_spec():
    per = HID // 4 // HB
    return pl.BlockSpec((1, D, HB), lambda i, k: (k // per, 0, k % per))


def _f_mlp(h2, nw, sc, sh, g2, w1, w2):
    t = h2.shape[0]
    tb = _tblock(t, MLP_ROWS)
    nk = HID // HB

    def body(h_ref, nw_ref, sc_ref, sh_ref, g_ref, w1_ref, w2_ref, h3_ref, m_ref, a_ref, v_ref, acc_ref):
        k = pl.program_id(1)

        @pl.when(k == 0)
        def _():
            _, n = _rms(h_ref[...])
            v_ref[...] = ((n * nw_ref[...]) * (1.0 + sc_ref[...]) + sh_ref[...]).astype(BF16)
            acc_ref[...] = jnp.zeros_like(acc_ref)

        a = _dot(v_ref[...], w1_ref[0])
        a_ref[...] = a.astype(BF16)
        ra = jnp.maximum(a, 0.0)
        acc_ref[...] += _dot((ra * ra).astype(BF16), w2_ref[...])

        @pl.when(k == nk - 1)
        def _():
            m = acc_ref[...]
            m_ref[...] = m.astype(BF16)
            h3_ref[...] = h_ref[...] + g_ref[...] * m

    blk = pl.BlockSpec((tb, D), lambda i, k: (i, 0))
    return pl.pallas_call(
        body, name="f_mlp", grid=(t // tb, nk),
        in_specs=[blk, _row(D), _row(D), _row(D), _row(D), _w1_spec(),
                  pl.BlockSpec((HB, D), lambda i, k: (k, 0))],
        out_specs=[blk, blk, pl.BlockSpec((tb, HB), lambda i, k: (i, k)), blk],
        out_shape=[jax.ShapeDtypeStruct((t, D), F32), jax.ShapeDtypeStruct((t, D), BF16), jax.ShapeDtypeStruct((t, HID), BF16),
                   jax.ShapeDtypeStruct((t, D), BF16)],
        scratch_shapes=[pltpu.VMEM((tb, D), F32)], compiler_params=_cparams(2),
    )(h2, nw, sc, sh, g2, w1, w2)


def _b_mlp(dh3, a, g2, w1, w2):
    t = dh3.shape[0]
    tb = _tblock(t, MLP_ROWS)
    nk = HID // HB

    def body(dh_ref, a_ref, g_ref, w1_ref, w2_ref, dv_ref, da_ref, act_ref, dm_ref):
        k = pl.program_id(1)
        dm = (dh_ref[...] * g_ref[...]).astype(BF16)

        @pl.when(k == 0)
        def _():
            dm_ref[...] = dm
            dv_ref[...] = jnp.zeros_like(dv_ref)

        ra = jnp.maximum(a_ref[...].astype(F32), 0.0)
        act_ref[...] = (ra * ra).astype(BF16)
        da = (_dot(dm, w2_ref[...], NT) * (2.0 * ra)).astype(BF16)
        da_ref[...] = da
        dv_ref[...] += _dot(da, w1_ref[0], NT)

    blk = pl.BlockSpec((tb, D), lambda i, k: (i, 0))
    hblk = pl.BlockSpec((tb, HB), lambda i, k: (i, k))
    return pl.pallas_call(
        body, name="b_mlp", grid=(t // tb, nk),
        in_specs=[blk, hblk, _row(D), _w1_spec(), pl.BlockSpec((HB, D), lambda i, k: (k, 0))],
        out_specs=[blk, hblk, hblk, blk],
        out_shape=[jax.ShapeDtypeStruct((t, D), F32), jax.ShapeDtypeStruct((t, HID), BF16), jax.ShapeDtypeStruct((t, HID), BF16),
                   jax.ShapeDtypeStruct((t, D), BF16)],
        compiler_params=_cparams(2),
    )(dh3, a, g2, w1, w2)


def _b_final(h, tgt, fw):
    t = h.shape[0]
    tb = _tblock(t)

    def body(h_ref, t_ref, w_ref, dh_ref, loss_ref, dfw_ref):
        @pl.when(pl.program_id(0) == 0)
        def _():
            loss_ref[...] = jnp.zeros_like(loss_ref)
            dfw_ref[...] = jnp.zeros_like(dfw_ref)

        r, n = _rms(h_ref[...])
        wv = w_ref[...]
        err = n * wv - t_ref[...]
        loss_ref[...] += jnp.sum(err * err, keepdims=True) * (0.5 / D)
        dy = err * (1.0 / D)
        dfw_ref[...] += _colsum(dy * n)
        dh_ref[...] = _rms_bwd(r, n, dy * wv)

    blk = pl.BlockSpec((tb, D), lambda i: (i, 0))
    return pl.pallas_call(
        body, name="b_final", grid=(t // tb,), in_specs=[blk, blk, _row(D)], out_specs=[blk, _row(1), _row(D)],
        out_shape=[jax.ShapeDtypeStruct((t, D), F32), jax.ShapeDtypeStruct((1, 1), F32), jax.ShapeDtypeStruct((1, D), F32)],
        compiler_params=_cparams(1),
    )(h, tgt, fw)


def _eye(n):
    return jnp.eye(n, dtype=F32)


def _pool_embed(pool_w):
    return jnp.einsum('gcd,gk->gckd', pool_w, _eye(4)).reshape(GW, GW)


def _pool_extract(m):
    return jnp.einsum('gcgd->gcd', m.reshape(4, 64, 4, 64))


def _bmat_embed(bb):
    return jnp.einsum('gph,gk->ghkp', bb, _eye(16)).reshape(GW, S5_P)


def _bmat_extract(m):
    return jnp.einsum('ghgp->gph', m.reshape(16, 16, 16, 64))


def _cmat_embed(cc):
    return jnp.einsum('ghp,gk->kpgh', cc, _eye(16)).reshape(S5_P, GW)


def _cmat_extract(m):
    return jnp.einsum('gpgh->ghp', m.reshape(16, 64, 16, 16))


def _pad_lanes(v, n=DTW):
    return jnp.pad(v.reshape(1, -1), ((0, 0), (0, n - v.shape[-1])))


def _w_in_layout(w_in_t):
    w_main = jnp.concatenate([w_in_t[:1280], w_in_t[2052:2308], w_in_t[1280:2048]], axis=0)
    return w_main, jnp.pad(w_in_t[2048:2052], ((0, DTW - 4), (0, 0)))


def _layer_params(p, l, mod, w_in, rest):
    q = {'rest': rest}
    q['mod'] = [mod[k:k + 1] for k in range(6)]
    q['nw1'] = p['norm_mix_w'][l:l + 1]
    q['nw2'] = p['norm_mlp_w'][l:l + 1]
    q['w_main'], q['w_dt'] = _w_in_layout(w_in)
    q['pool_mat'] = _pool_embed(p['pool_w'][l]).astype(BF16)
    q['pool_scale'] = p['pool_scale'][l:l + 1]
    q['sconv_w'] = p['sconv_w'][l]
    q['conv_w'] = p['ssd_conv_w'][l]
    q['conv_b'] = p['ssd_conv_b'][l:l + 1]
    q['dt_bias'] = _pad_lanes(p['ssd_dt_bias'][l])
    q['a_log'] = _pad_lanes(p['ssd_a_log'][l])
    q['ssd_d'] = _pad_lanes(p['ssd_d'][l])
    q['s5_raw'] = (p['s5_a_re'][l], p['s5_a_im'][l], p['s5_log_step'][l].reshape(16, 1),
                   p['s5_b_re'][l].reshape(16, 1024), p['s5_b_im'][l].reshape(16, 1024))
    q['cre'] = _cmat_embed(p['s5_c_re'][l]).astype(BF16)
    q['cim'] = (-_cmat_embed(p['s5_c_im'][l])).astype(BF16)
    q['s5_d'] = p['s5_d'][l:l + 1]
    q['glu_w'] = p['s5_glu_w'][l].astype(BF16)
    q['glu_b'] = p['s5_glu_b'][l:l + 1]
    q['bw'] = p['branch_norm_w'][l:l + 1]
    return q


def _layer_fwd(h, q):
    sh1, sc1, g1, sh2, sc2, g2 = q['mod']
    t = h.shape[0]
    s = {'h': h}
    s['proj'], s['dtp'], s['u'] = _f_in(h, q['nw1'], sc1, sh1, q['w_main'], q['w_dt'])
    s['ya'], s['yb'] = _f_ab(s['proj'], q['pool_mat'], q['pool_scale'], q['sconv_w'])
    s['yc'], s['ypre'], s['sprev'] = _f_ssd(s['proj'], s['dtp'], q['conv_w'], q['conv_b'], q['dt_bias'], q['a_log'], q['ssd_d'])
    lr, li, bbr, bbi, ars, ais = _s5_prep(*q['s5_raw'])
    s['bmat'] = jnp.concatenate([_bmat_embed(bbr.reshape(16, 64, 16)), _bmat_embed(bbi.reshape(16, 64, 16))],
                                axis=1).astype(BF16)
    s['tables'] = _s5_tables(ars.reshape(1, S5_P), ais.reshape(1, S5_P))
    s['yd'], s['carries'], s['states'] = _f_s5(s['proj'], s['bmat'], q['cre'], q['cim'], s['tables'][0], s['tables'][1],
                                  q['s5_d'], q['glu_w'], q['glu_b'])
    q['w_out'], q['w1'], q['w2'] = q['rest']((s['ya'], s['yc'], s['yd']))
    s['h2'], s['o'], s['cat'] = _f_out(s['ya'], s['yb'], s['yc'], s['yd'], q['bw'], q['w_out'], h, g1)
    h3, s['m'], s['a'], s['v'] = _f_mlp(s['h2'], q['nw2'], sc2, sh2, g2, q['w1'], q['w2'])
    return h3, s


STACKED = {'mlp_w1': (2, 4, D, HID // 4), 'mlp_w2': (2, HID, D), 'w_out': (2, D, D)}


def _layer_bwd(dh3, q, s, l, stacked, early=None):
    sh1, sc1, g1, sh2, sc2, g2 = q['mod']
    g = {}
    dv, da, act, dm = _b_mlp(dh3, s['a'], g2, q['w1'], q['w2'])
    g['mlp_w1'] = _tn_matmul(s['v'], da, "dw1", col_major=True, into=stacked['mlp_w1'], layer=l)
    g['mlp_w2'] = _tn_matmul(act, dm, "dw2", into=stacked['mlp_w2'], layer=l)
    dh2, dsc2, dsh2, dnw2, dg2 = _b_normmod(dv, s['h2'], dh3, s['m'], q['nw2'], sc2, "b_norm_mlp")
    dya, dyb, dyc, dyd, do, dbw = _b_out(dh2, s['ya'], s['yb'], s['yc'], s['yd'], q['bw'], q['w_out'], g1)
    g['w_out'] = _tn_matmul(s['cat'], do, "dwout", into=stacked['w_out'], layer=l)
    g['branch_norm_w'] = dbw[0]
    if early is not None:
        zero = early(g)[0, 0]
        q = dict(q, pool_scale=q['pool_scale'] + zero, conv_b=q['conv_b'] + zero, s5_d=q['s5_d'] + zero)
    dab, dpm, dps, dsw = _b_ab(s['proj'], dya, dyb, q['pool_mat'], q['pool_scale'], q['sconv_w'])
    g['pool_w'] = _pool_extract(dpm)
    g['pool_scale'] = dps[0]
    g['sconv_w'] = dsw
    dz, dxbc, ddt, dcw, dcb, ddtb, dal, ddk = _b_ssd(s['proj'], s['dtp'], s['ypre'], dyc, s['sprev'], q['conv_w'],
                                                     q['conv_b'], q['dt_bias'], q['a_log'], q['ssd_d'])
    g['ssd_conv_w'] = dcw
    g['ssd_conv_b'] = dcb[0]
    g['ssd_dt_bias'] = ddtb[0, :4]
    g['ssd_a_log'] = dal[0, :4]
    g['ssd_d'] = ddk[0, :4]
    tb = s['tables']
    ds5, dbmat, dcre, dcim, dlam, dd5, dgw, dgb = _b_s5(s['proj'], dyd, s['carries'], s['states'], s['bmat'], q['cre'], q['cim'],
                                                        tb[0], tb[1], q['s5_d'], q['glu_w'], q['glu_b'])
    g['s5_c_re'] = _cmat_extract(dcre)
    g['s5_c_im'] = -_cmat_extract(dcim)
    g['s5_d'] = dd5[0]
    g['s5_glu_w'] = dgw
    g['s5_glu_b'] = dgb[0]
    dbbr = _bmat_extract(dbmat[:, :S5_P]).reshape(16, 1024)
    dbbi = _bmat_extract(dbmat[:, S5_P:]).reshape(16, 1024)
    dar, dai, dls, dbr, dbi = _s5_prep_bwd(*q['s5_raw'], dlam[0].reshape(16, 64), dlam[1].reshape(16, 64), dbbr, dbbi)
    g['s5_a_re'], g['s5_a_im'], g['s5_log_step'] = dar, dai, dls[:, 0]
    g['s5_b_re'], g['s5_b_im'] = dbr, dbi
    dh, dsc1, dsh1, dnw1, dg1 = _b_in(dab, dz, dxbc, ds5, ddt, q['w_main'], q['w_dt'], s['h'], dh2, s['o'], q['nw1'], sc1)
    u = s['u']
    head = jnp.concatenate([_tn_matmul(dab, u, "dwin_ab"), _tn_matmul(dz, u, "dwin_z"), _tn_matmul(dxbc, u, "dwin_xbc"),
                            _tn_matmul(ddt, u, "dwin_dt")[:8]], axis=0)
    full = lax.dynamic_update_slice(jnp.zeros((2308, D), F32), head, (0, 0))
    g['w_in'] = lax.dynamic_update_slice(full, _tn_matmul(ds5, u, "dwin_s5"), (2052, 0))
    g['norm_mix_w'] = dnw1[0]
    g['norm_mlp_w'] = dnw2[0]
    dmod = jnp.concatenate([dsh1, dsc1, dg1, dsh2, dsc2, dg2], axis=1)
    return dh, g, dmod


def _local_step(x, tgt, p, mod, w_in_of, rest_of, early=None):
    h = x
    qs, saved = [], []
    for l in range(2):
        qs.append(_layer_params(p, l, mod[l], w_in_of(l), functools.partial(rest_of, l)))
        h, s = _layer_fwd(h, qs[l])
        saved.append(s)
    dh, loss, dfw = _b_final(h, tgt, p['final_norm_w'].reshape(1, D))
    grads = [None, None]
    dmods = [None, None]
    dh, grads[1], dmods[1] = _layer_bwd(dh, qs[1], saved[1], 1, {k: lax.empty(shp, F32) for k, shp in STACKED.items()})
    dh, grads[0], dmods[0] = _layer_bwd(dh, qs[0], saved[0], 0, grads[1], early)
    out = {k: jnp.stack([grads[0][k], grads[1][k]]) for k in grads[0] if k not in STACKED}
    if early is None:
        out.update({k: grads[0][k] for k in STACKED})
    out['final_norm_w'] = dfw[0]
    return loss, dh, out, jnp.concatenate(dmods, axis=0)


def _shard_of(a, axis, k):
    n = a.shape[axis] // 4
    return lax.dynamic_slice_in_dim(a, k * n, n, axis)


def kernel(x, c, norm_mix_w, norm_mlp_w, ada_w, ada_b, w_in, pool_w, pool_scale, sconv_w, ssd_conv_w, ssd_conv_b, ssd_dt_bias, ssd_a_log, ssd_d, s5_a_re, s5_a_im, s5_log_step, s5_b_re, s5_b_im, s5_c_re, s5_c_im, s5_d, s5_glu_w, s5_glu_b, branch_norm_w, w_out, mlp_w1, mlp_w2, final_norm_w, loss_target, m_norm_mix_w, m_norm_mlp_w, m_ada_w, m_ada_b, m_w_in, m_pool_w, m_pool_scale, m_sconv_w, m_ssd_conv_w, m_ssd_conv_b, m_ssd_dt_bias, m_ssd_a_log, m_ssd_d, m_s5_a_re, m_s5_a_im, m_s5_log_step, m_s5_b_re, m_s5_b_im, m_s5_c_re, m_s5_c_im, m_s5_d, m_s5_glu_w, m_s5_glu_b, m_branch_norm_w, m_w_out, m_mlp_w1, m_mlp_w2, m_final_norm_w, v_norm_mix_w, v_norm_mlp_w, v_ada_w, v_ada_b, v_w_in, v_pool_w, v_pool_scale, v_sconv_w, v_ssd_conv_w, v_ssd_conv_b, v_ssd_dt_bias, v_ssd_a_log, v_ssd_d, v_s5_a_re, v_s5_a_im, v_s5_log_step, v_s5_b_re, v_s5_b_im, v_s5_c_re, v_s5_c_im, v_s5_d, v_s5_glu_w, v_s5_glu_b, v_branch_norm_w, v_w_out, v_mlp_w1, v_mlp_w2, v_final_norm_w):
    loc = locals()
    w = {n: loc[n] for n in WEIGHTS}
    mom = {n: loc['m_' + n] for n in WEIGHTS}
    var = {n: loc['v_' + n] for n in WEIGHTS}
    ix, iy, ic = lax.axis_index("x"), lax.axis_index("y"), lax.axis_index("c")
    chip = 2 * ix + iy
    dev = 4 * ix + 2 * iy + ic

    mine_of = lambda a: lax.dynamic_index_in_dim(a.astype(BF16), ic, axis=0, keepdims=False)
    pad_in = lambda a: jnp.pad(a.T, ((0, WIN_ROWS - 577), (0, 0)))
    shard = jnp.concatenate([pad_in(mine_of(w['w_in'])), mine_of(w['w_out']), mine_of(w['mlp_w1']), mine_of(w['mlp_w2'])], axis=0)

    (c_all,) = _exchange([c], EVERYONE, False, "ag_cond", stage=True)
    c_all = c_all.reshape(8, D)
    small_sh = _exchange([w[n] for n in SMALL_SHARDED], CHIPS, False, "ag_small")
    (w_in0,) = _exchange([pad_in(w['w_in'][0].astype(BF16))], CHIPS, False, "ag_win0")
    p = {n: w[n] for n in WEIGHTS if n not in BIG}
    for n, g in zip(SMALL_SHARDED, small_sh):
        ax = SMALL_SHARDED[n]
        p[n] = jnp.concatenate([g[k] for k in range(4)], axis=ax)

    def w_in_full(sh):
        return sh[:, :577].reshape(4 * 577, D)

    big = {}

    def fetch(after):
        if not big:
            (mine,), (got,) = _split_wait(sems, shard_thru, land, after, False, "ag_big_wait")
            got = lax.dynamic_update_slice(got, mine[None], (chip, 0, 0))
            other = _pair_swap([got.reshape(-1, D)], False, "swap_big")[0].reshape(got.shape)
            big['both'] = [jnp.where(ic == l, got, other) for l in range(2)]
        return big['both']

    def w_in_of(l):
        return w_in_full(w_in0) if l == 0 else w_in_full(fetch(None)[1])

    def rest_of(l, after):
        blk = fetch(after)[l]
        r0 = WIN_ROWS
        w_out_l = blk[:, r0:r0 + 256].reshape(D, D)
        w1_l = blk[:, r0 + 256:r0 + 1280]
        w2_l = blk[:, r0 + 1280:r0 + 2304].reshape(HID, D)
        return w_out_l, w1_l, w2_l

    ada_b_sh = _shard_of(w['ada_b'], 1, chip).reshape(2, 1, 6 * D // 4)
    mod_sh = _ada_fwd(c_all, w['ada_w'], ada_b_sh)
    (mod_all,) = _exchange([mod_sh], CHIPS, False, "ag_mod", stage=True)
    mine = lax.dynamic_index_in_dim(mod_all, dev, axis=2, keepdims=False)
    sems, shard_thru, land, token = _split_start([shard], [mod_all, w_in0] + small_sh, False, "ag_big_start")
    mod = jnp.transpose(mine, (1, 0, 2)).reshape(2, 6, D) + token[0, 0]

    layer = ic.astype(jnp.int32).reshape(1)
    flight = {}

    def early(g0):
        gws = [g0['w_out'].reshape(2, 4, 256, D), g0['mlp_w1'], g0['mlp_w2'].reshape(2, 4, 1024, D)]
        got = _pair_swap([a.reshape(2, -1, D) for a in gws], True, "swap_grad", narrow=True)
        pair = [_pair_sum(a, b.reshape(a.shape[1:]), layer, "pair_sum%d" % (k + 1), BF16) for k, (a, b) in enumerate(zip(gws, got))]
        flight['sems'], flight['srcs'], flight['lands'], token = _split_start(pair, [], True, "rs_start")
        return token

    loss, grad_x, g, dmod = _local_step(x[0], loss_target[0], p, mod, w_in_of, rest_of, early)

    (dmod_all,) = _exchange([dmod], EVERYONE, False, "ag_dmod", stage=True)
    dmod_all = jnp.transpose(dmod_all, (1, 0, 2))
    g_ada_w, g_ada_b = _ada_bwd(c_all, _shard_of(dmod_all, 2, chip), dmod_all)

    sent, lands = _split_wait(flight['sems'], flight['srcs'], flight['lands'], [grad_x, g['w_in']], True, "rs_wait")
    quad = []
    for k, (land, mine) in enumerate(zip(lands, sent)):
        own = lax.dynamic_index_in_dim(mine, chip, axis=0, keepdims=True)
        quad.append(_sum_lead(lax.dynamic_update_slice(land, own, (chip, 0, 0)), "rs_chip_sum%d" % (k + 1), F32))
    gw_in = jnp.pad(g['w_in'].reshape(2, 4, 577, D), ((0, 0), (0, 0), (0, WIN_ROWS - 577), (0, 0)))
    (got_in,) = _pair_swap([gw_in.reshape(2, -1, D)], True, "swap_grad_in", narrow=True)
    pair_in = _pair_sum(gw_in, got_in.reshape(gw_in.shape[1:]), layer, "pair_sum0", BF16)
    (quad_in,) = _exchange([pair_in], CHIPS, True, "rs_chips")
    quad = [_sum_lead(quad_in, "rs_chip_sum0", F32)] + quad
    other = _pair_swap(quad, False, "swap_red")
    both = [jnp.stack([jnp.where(ic == l, a, b) for l in range(2)]) for a, b in zip(quad, other)]
    both[0] = jnp.transpose(both[0][:, :577], (0, 2, 1))
    red = dict(zip(('w_in', 'w_out', 'mlp_w1', 'mlp_w2'), both))
    red['ada_w'] = g_ada_w

    small_names = [n for n in WEIGHTS if n not in BIG and n != 'ada_b']
    pair_parts = _exchange([g[n] for n in small_names] + [loss], SIBLING, False, "ag_smallpair", stage=True)
    chip_parts = _exchange(_sum_many(pair_parts, "smallpair_sum"), CHIPS, False, "ag_smallgrad", stage=True)
    summed = _sum_many(chip_parts, "smallgrad_sum")
    for n, a in zip(small_names, summed[:-1]):
        a = a.reshape(w[n].shape) if n in ('s5_b_re', 's5_b_im') else a
        red[n] = _shard_of(a, SMALL_SHARDED[n], chip) if n in SMALL_SHARDED else a
    red['ada_b'] = g_ada_b
    loss_out = summed[-1].reshape(())

    delta, new_m, new_v = {}, {}, {}
    for n in BIG:
        delta[n], new_m[n], new_v[n] = _adamw(w[n], red[n], mom[n], var[n], "adamw_" + n)
    rest = [n for n in WEIGHTS if n not in BIG]
    lanes = lambda n, a: a.reshape(2, 16, 1024) if n in ('s5_b_re', 's5_b_im') else a
    outs = _adamw_many(*[[lanes(n, src[n]) for n in rest] for src in (w, red, mom, var)], "adamw_small")
    for k, n in enumerate(rest):
        delta[n], new_m[n], new_v[n] = (outs[3 * k + j].reshape(w[n].shape) for j in range(3))

    return (loss_out, grad_x[None], *[red[n] for n in WEIGHTS], *[delta[n] for n in WEIGHTS],
            *[new_m[n] for n in WEIGHTS], *[new_v[n] for n in WEIGHTS])
```

```python
import functools
import math

import jax
import jax.numpy as jnp
from jax import lax
from jax.experimental import pallas as pl
from jax.experimental.pallas import tpu as pltpu

F32 = jnp.float32
BF16 = jnp.bfloat16
HI = lax.Precision.HIGHEST

D = 1024
GW = 256
HID = 4096
EPS = 1e-6
PW = 2304
DTW = 128
SSD_L = 128
SSD_SUB = 2
SSD_SUB_BWD = 2
NH, HP, NS = 4, 64, 128
S5_P = 1024
MESH = pl.DeviceIdType.MESH

ADAM_LR, ADAM_B1, ADAM_B2, ADAM_EPS, ADAM_WD, ADAM_STEP = 0.001, 0.9, 0.999, 1e-08, 0.01, 10

NT = (((1,), (1,)), ((), ()))
TN = (((0,), (0,)), ((), ()))

WEIGHTS = ['norm_mix_w', 'norm_mlp_w', 'ada_w', 'ada_b', 'w_in', 'pool_w', 'pool_scale', 'sconv_w', 'ssd_conv_w',
           'ssd_conv_b', 'ssd_dt_bias', 'ssd_a_log', 'ssd_d', 's5_a_re', 's5_a_im', 's5_log_step', 's5_b_re', 's5_b_im',
           's5_c_re', 's5_c_im', 's5_d', 's5_glu_w', 's5_glu_b', 'branch_norm_w', 'w_out', 'mlp_w1', 'mlp_w2',
           'final_norm_w']
BIG = ('ada_w', 'w_in', 'w_out', 'mlp_w1', 'mlp_w2')
SMALL_SHARDED = {'sconv_w': 2, 'ssd_conv_w': 2, 's5_glu_w': 1}


def _cparams(n_axes, vmem_mb=48):
    return pltpu.CompilerParams(dimension_semantics=("arbitrary",) * n_axes, vmem_limit_bytes=vmem_mb * 1024 * 1024)


def _row(n):
    return pl.BlockSpec((1, n), lambda *_: (0, 0))


def _full(shape):
    nd = len(shape)
    return pl.BlockSpec(tuple(shape), lambda *_: (0,) * nd)


def _dot(a, b, dims=None, prec=None):
    if dims is None:
        dims = (((a.ndim - 1,), (0,)), ((), ()))
    return lax.dot_general(a, b, dims, preferred_element_type=F32, precision=prec)


def _bdot(a, b, dims=None):
    return _dot(a.astype(BF16), b.astype(BF16), dims)


def _sig(x):
    return jax.nn.sigmoid(x)


def _silu(x):
    return x * _sig(x)


def _dsilu(x):
    s = _sig(x)
    return s * (1.0 + x * (1.0 - s))


def _softplus(x):
    return jnp.maximum(x, 0.0) + jnp.log(1.0 + jnp.exp(-jnp.abs(x)))


_GK = math.sqrt(2.0 / math.pi)


def _gelu(x):
    return 0.5 * x * (1.0 + jnp.tanh(_GK * (x + 0.044715 * x * x * x)))


def _dgelu(x):
    th = jnp.tanh(_GK * (x + 0.044715 * x * x * x))
    return 0.5 * (1.0 + th) + 0.5 * x * (1.0 - th * th) * _GK * (1.0 + 3.0 * 0.044715 * x * x)


def _colsum(x):
    return jnp.sum(x, axis=0, keepdims=True)


def _rms(x):
    r = lax.rsqrt(jnp.mean(x * x, axis=-1, keepdims=True) + EPS)
    return r, x * r


def _rms_bwd(r, n, dn):
    return r * (dn - n * jnp.mean(dn * n, axis=-1, keepdims=True))


def _roll(x, k):
    n = x.shape[0]
    k = k % n
    return x if k == 0 else pltpu.roll(x, k, axis=0)


def _tblock(t, want=512):
    return min(t, want)


def _peer(mask):
    x, y, c = lax.axis_index("x"), lax.axis_index("y"), lax.axis_index("c")
    return (x ^ ((mask >> 2) & 1), y ^ ((mask >> 1) & 1), c ^ (mask & 1))


def _group_index(masks):
    x, y, c = lax.axis_index("x"), lax.axis_index("y"), lax.axis_index("c")
    full = 0
    for m in masks:
        full |= m
    bits = [b for b in (4, 2, 1) if full & b]

    def idx(px, py, pc):
        v = {4: px, 2: py, 1: pc}
        out = 0
        for b in bits:
            out = out * 2 + v[b]
        return out

    return idx(x, y, c), [idx(*_peer(m)) for m in masks]


def _exchange(arrs, masks, scatter, name, stage=False):
    n_arr, n_peer, n_grp = len(arrs), len(masks), len(masks) + 1

    def body(*refs):
        ins, outs = refs[:n_arr], refs[n_arr:2 * n_arr]
        send_sems, recv_sems, local_sems = refs[2 * n_arr:2 * n_arr + 3]
        if stage:
            bufs, load_sems = refs[2 * n_arr + 3:3 * n_arr + 3], refs[3 * n_arr + 3]
            loads = [pltpu.make_async_copy(ins[t], bufs[t], load_sems.at[t]) for t in range(n_arr)]
            for ld in loads:
                ld.start()
            for ld in loads:
                ld.wait()
            ins = bufs
        me, peer_idx = _group_index(masks)
        copies = []
        for t in range(n_arr):
            src_me = ins[t].at[me] if scatter else ins[t]
            loc = pltpu.make_async_copy(src_me, outs[t].at[me], local_sems.at[t])
            loc.start()
            copies.append(loc)
            for j, m in enumerate(masks):
                src = ins[t].at[peer_idx[j]] if scatter else ins[t]
                cp = pltpu.make_async_remote_copy(src_ref=src, dst_ref=outs[t].at[me], send_sem=send_sems.at[t, j],
                                                  recv_sem=recv_sems.at[t, j], device_id=_peer(m), device_id_type=MESH)
                cp.start()
                copies.append(cp)
        for cp in copies:
            cp.wait()

    hbm = pl.BlockSpec(memory_space=pl.ANY)
    out_shape = [jax.ShapeDtypeStruct((n_grp,) + (a.shape[1:] if scatter else a.shape), a.dtype) for a in arrs]
    staging = [pltpu.VMEM(a.shape, a.dtype) for a in arrs] + [pltpu.SemaphoreType.DMA((n_arr,))] if stage else []
    outs = pl.pallas_call(
        body, name=name, in_specs=[hbm] * n_arr, out_specs=[hbm] * n_arr, out_shape=out_shape,
        scratch_shapes=[pltpu.SemaphoreType.DMA((n_arr, n_peer)), pltpu.SemaphoreType.DMA((n_arr, n_peer)),
                        pltpu.SemaphoreType.DMA((n_arr,))] + staging,
        compiler_params=pltpu.CompilerParams(vmem_limit_bytes=48 * 1024 * 1024),
    )(*arrs)
    return list(outs)


def _split_copies(src_refs, land_refs, sems, scatter, per_core):
    me, peer_idx = _group_index(CHIPS)
    n = len(CHIPS) * len(src_refs)
    copies = []
    for t, (src_ref, land_ref) in enumerate(zip(src_refs, land_refs)):
        zone = land_ref.at[lax.axis_index("c")] if per_core else land_ref
        for j, m in enumerate(CHIPS):
            k = len(CHIPS) * t + j
            copies.append(pltpu.make_async_remote_copy(
                src_ref=src_ref.at[peer_idx[j]] if scatter else src_ref, dst_ref=zone.at[me], send_sem=sems[k],
                recv_sem=sems[n + k], device_id=_peer(m), device_id_type=MESH))
    return copies


def _split_start(srcs, after, scatter, name, per_core=False):
    n_arr, n_sem = len(srcs), 2 * len(CHIPS) * len(srcs)

    def body(*refs):
        src_refs, land_refs = refs[:n_arr], refs[n_arr:2 * n_arr]
        outs = refs[2 * n_arr + len(after):]
        for cp in _split_copies(src_refs, land_refs, outs[:n_sem], scatter, per_core):
            cp.start()
        outs[-1][...] = jnp.zeros_like(outs[-1])

    hbm = pl.BlockSpec(memory_space=pltpu.HBM)
    sem = pl.BlockSpec(memory_space=pltpu.SEMAPHORE)
    lands = [lax.empty(((2,) if per_core else ()) + (len(CHIPS) + 1,) + (a.shape[1:] if scatter else a.shape), a.dtype)
             for a in srcs]
    as_hbm = lambda a: pltpu.with_memory_space_constraint(a, pltpu.HBM)
    outs = pl.pallas_call(
        body, name=name,
        out_shape=(pltpu.SemaphoreType.DMA(()),) * n_sem + tuple(pltpu.HBM(a.shape, a.dtype) for a in srcs + lands)
        + (jax.ShapeDtypeStruct((8, 128), F32),),
        in_specs=(hbm,) * (2 * n_arr) + (pl.BlockSpec(memory_space=pl.ANY),) * len(after),
        out_specs=(sem,) * n_sem + (hbm,) * (2 * n_arr) + (pl.BlockSpec(memory_space=pltpu.VMEM),),
        input_output_aliases={t: n_sem + t for t in range(2 * n_arr)},
        compiler_params=pltpu.CompilerParams(has_side_effects=pltpu.SideEffectType.DATAFLOW_SIDE_EFFECTING),
    )(*[as_hbm(a) for a in srcs + lands], *after)
    return outs[:n_sem], list(outs[n_sem:n_sem + n_arr]), list(outs[n_sem + n_arr:n_sem + 2 * n_arr]), outs[-1]


def _split_wait(sems, srcs, lands, after, scatter, name, per_core=False):
    n_arr, n_sem = len(srcs), len(sems)

    def body(*refs):
        src_refs, land_refs = refs[:n_arr], refs[n_arr:2 * n_arr]
        for cp in _split_copies(src_refs, land_refs, refs[2 * n_arr:2 * n_arr + n_sem], scatter, per_core):
            cp.wait_send()
            cp.wait_recv()

    hbm = pl.BlockSpec(memory_space=pltpu.HBM)
    sem = pl.BlockSpec(memory_space=pltpu.SEMAPHORE)
    outs = pl.pallas_call(
        body, name=name, out_shape=tuple(pltpu.HBM(a.shape, a.dtype) for a in srcs + lands),
        in_specs=(hbm,) * (2 * n_arr) + (sem,) * n_sem + (pl.BlockSpec(memory_space=pl.ANY),) * len(after),
        out_specs=(hbm,) * (2 * n_arr), input_output_aliases={t: t for t in range(2 * n_arr)},
        compiler_params=pltpu.CompilerParams(has_side_effects=pltpu.SideEffectType.DATAFLOW_SIDE_EFFECTING),
    )(*srcs, *lands, *sems, *after)
    return list(outs[:n_arr]), list(outs[n_arr:])


CHIPS = (4, 2, 6)
EVERYONE = (1, 2, 3, 4, 5, 6, 7)
SIBLING = (1,)
SWAP_ROWS = 512
WIN_ROWS = 592


def _pair_swap(arrs, other_layer, name, narrow=False, fill=False):
    assert not (fill and (other_layer or narrow))
    n_arr = len(arrs)
    shapes = [a.shape[-2:] for a in arrs]
    out_dtypes = [BF16 if narrow else a.dtype for a in arrs]
    chunks = []
    for t, (rows, _) in enumerate(shapes):
        assert rows % 16 == 0
        for j, r0 in enumerate(range(0, rows, SWAP_ROWS)):
            chunks.append((t, r0, min(SWAP_ROWS, rows - r0), j % 2))

    def body(*refs):
        ins, outs = refs[:n_arr], refs[n_arr:2 * n_arr]
        bufs = refs[2 * n_arr:3 * n_arr]
        out_bufs = refs[3 * n_arr:4 * n_arr] if narrow else bufs
        load_sems, send_sems, recv_sems = refs[-3:]
        sibling = _peer(1)
        c = lax.axis_index("c")

        def load(k):
            t, r0, n, slot = chunks[k]
            src = ins[t].at[1 - c] if other_layer else ins[t].at[c] if fill else ins[t]
            return pltpu.make_async_copy(src.at[pl.ds(r0, n)], bufs[t].at[slot, pl.ds(0, n)], load_sems.at[t, slot])

        def send(k):
            t, r0, n, slot = chunks[k]
            dst = outs[t].at[c] if fill else outs[t]
            return pltpu.make_async_remote_copy(src_ref=out_bufs[t].at[slot, pl.ds(0, n)], dst_ref=dst.at[pl.ds(r0, n)],
                                                send_sem=send_sems.at[t, slot], recv_sem=recv_sems.at[t],
                                                device_id=sibling, device_id_type=MESH)

        in_flight = {}

        def drain(k):
            key = (chunks[k][0], chunks[k][3])
            if key in in_flight:
                send(in_flight.pop(key)).wait_send()

        def start_load(k):
            if not narrow:
                drain(k)
            load(k).start()

        start_load(0)
        for k in range(len(chunks)):
            t, _, n, slot = chunks[k]
            load(k).wait()
            if k + 1 < len(chunks):
                start_load(k + 1)
            if narrow:
                drain(k)
                out_bufs[t][slot, pl.ds(0, n), :] = bufs[t][slot, pl.ds(0, n), :].astype(BF16)
            send(k).start()
            in_flight[(t, slot)] = k
        for k in in_flight.values():
            send(k).wait_send()
        for t in range(n_arr):
            landed = outs[t].at[1 - c] if fill else outs[t]
            pltpu.make_async_remote_copy(src_ref=landed, dst_ref=landed, send_sem=send_sems.at[t, 0],
                                         recv_sem=recv_sems.at[t], device_id=sibling, device_id_type=MESH).wait_recv()

    hbm = pl.BlockSpec(memory_space=pl.ANY)
    outs = pl.pallas_call(
        body, name=name, in_specs=[hbm] * n_arr, out_specs=[hbm] * n_arr,
        out_shape=[jax.ShapeDtypeStruct(a.shape if fill else s, dt) for a, s, dt in zip(arrs, shapes, out_dtypes)],
        input_output_aliases={t: t for t in range(n_arr)} if fill else {},
        scratch_shapes=[pltpu.VMEM((2, min(SWAP_ROWS, s[0]), s[1]), a.dtype) for s, a in zip(shapes, arrs)]
        + ([pltpu.VMEM((2, min(SWAP_ROWS, s[0]), s[1]), BF16) for s in shapes] if narrow else [])
        + [pltpu.SemaphoreType.DMA((n_arr, 2)), pltpu.SemaphoreType.DMA((n_arr, 2)), pltpu.SemaphoreType.DMA((n_arr,))],
        compiler_params=pltpu.CompilerParams(vmem_limit_bytes=48 * 1024 * 1024),
    )(*arrs)
    return list(outs)


def _sum_lead(a, name, out_dtype):
    n = a.shape[0]
    shape = a.shape[1:]

    def body(a_ref, o_ref):
        acc = a_ref[0].astype(F32)
        for k in range(1, n):
            acc = acc + a_ref[k].astype(F32)
        o_ref[...] = acc.astype(out_dtype)

    if len(shape) == 3:
        blk = (1,) + shape[1:]
        return pl.pallas_call(
            body, name=name, grid=(shape[0],), in_specs=[pl.BlockSpec((n,) + blk, lambda i: (0, i, 0, 0))],
            out_specs=pl.BlockSpec(blk, lambda i: (i, 0, 0)), out_shape=jax.ShapeDtypeStruct(shape, out_dtype),
            compiler_params=_cparams(1),
        )(a)
    rows, cols = shape
    rb = rows
    for cand in (512, 256, 128):
        if rows % cand == 0 and rows > cand:
            rb = cand
            break
    return pl.pallas_call(
        body, name=name, grid=(rows // rb,), in_specs=[pl.BlockSpec((n, rb, cols), lambda i: (0, i, 0))],
        out_specs=pl.BlockSpec((rb, cols), lambda i: (i, 0)), out_shape=jax.ShapeDtypeStruct((rows, cols), out_dtype),
        compiler_params=_cparams(1),
    )(a)


def _pair_sum(g, recv, layer, name, out_dtype):
    _, n, r, c = g.shape

    def body(l_ref, g_ref, r_ref, o_ref):
        o_ref[...] = (g_ref[0].astype(F32) + r_ref[...].astype(F32)).astype(out_dtype)

    return pl.pallas_call(
        body, name=name,
        grid_spec=pltpu.PrefetchScalarGridSpec(
            num_scalar_prefetch=1, grid=(n,),
            in_specs=[pl.BlockSpec((1, 1, r, c), lambda i, l: (l[0], i, 0, 0)), pl.BlockSpec((1, r, c), lambda i, l: (i, 0, 0))],
            out_specs=pl.BlockSpec((1, r, c), lambda i, l: (i, 0, 0))),
        out_shape=jax.ShapeDtypeStruct((n, r, c), out_dtype), compiler_params=_cparams(1),
    )(layer, g, recv)


def _tn_matmul(a, b, name, col_major=False, into=None, layer=0):
    t, k = a.shape
    n = b.shape[1]
    tb = _tblock(t, 1024)
    kb = min(k, 1024)
    nb = min(n, 1024)
    grid = (k // kb, n // nb, t // tb)
    lead = (into is not None) + col_major

    def body(a_ref, b_ref, *rest):
        o_ref = rest[-1]
        for _ in range(lead):
            o_ref = o_ref.at[0]

        @pl.when(pl.program_id(2) == 0)
        def _():
            o_ref[...] = jnp.zeros_like(o_ref)

        o_ref[...] += _bdot(a_ref[...], b_ref[...], TN)

    if col_major:
        block, index, shape = (1, kb, nb), (lambda ki, ni: (ni, ki, 0)), (n // nb, k, nb)
    else:
        block, index, shape = (kb, nb), (lambda ki, ni: (ki, ni)), (k, n)
    in_specs = [pl.BlockSpec((tb, kb), lambda ki, ni, ti: (ti, ki)), pl.BlockSpec((tb, nb), lambda ki, ni, ti: (ti, ni))]
    if into is None:
        return pl.pallas_call(
            body, name=name, grid=grid, in_specs=in_specs, out_specs=pl.BlockSpec(block, lambda ki, ni, ti: index(ki, ni)),
            out_shape=jax.ShapeDtypeStruct(shape, F32), compiler_params=_cparams(3),
        )(a, b)
    assert into.shape == (2,) + shape
    return pl.pallas_call(
        body, name=name, grid=grid, in_specs=in_specs + [pl.BlockSpec(memory_space=pl.ANY)],
        out_specs=pl.BlockSpec((1,) + block, lambda ki, ni, ti: (layer,) + index(ki, ni)),
        out_shape=jax.ShapeDtypeStruct(into.shape, F32), input_output_aliases={2: 0}, compiler_params=_cparams(3),
    )(a, b, into)


def _sum_many(arrs, name):
    k = len(arrs)

    def body(*refs):
        for a_ref, o_ref in zip(refs[:k], refs[k:]):
            acc = a_ref[0]
            for j in range(1, a_ref.shape[0]):
                acc = acc + a_ref[j]
            o_ref[...] = acc

    return pl.pallas_call(body, name=name, grid=(1,), in_specs=[_full(a.shape) for a in arrs],
                          out_specs=[_full(a.shape[1:]) for a in arrs],
                          out_shape=[jax.ShapeDtypeStruct(a.shape[1:], F32) for a in arrs], compiler_params=_cparams(1))(*arrs)


def _adamw_math(w, g, m, v):
    m2 = ADAM_B1 * m + (1.0 - ADAM_B1) * g
    v2 = ADAM_B2 * v + (1.0 - ADAM_B2) * (g * g)
    m_hat = m2 / (1.0 - ADAM_B1 ** ADAM_STEP)
    v_hat = v2 / (1.0 - ADAM_B2 ** ADAM_STEP)
    return -ADAM_LR * (m_hat / (jnp.sqrt(v_hat) + ADAM_EPS) + ADAM_WD * w), m2, v2


def _adamw_many(ws, gs, ms, vs, name):
    n = len(ws)

    def body(*refs):
        ins, outs = refs[:4 * n], refs[4 * n:]
        for k in range(n):
            res = _adamw_math(ins[k][...], ins[n + k][...], ins[2 * n + k][...], ins[3 * n + k][...])
            for j in range(3):
                outs[3 * k + j][...] = res[j]

    out_shape = []
    for a in ws:
        out_shape += [jax.ShapeDtypeStruct(a.shape, F32)] * 3
    return pl.pallas_call(body, name=name, grid=(1,), in_specs=[_full(a.shape) for a in ws] * 4,
                          out_specs=[_full(s.shape) for s in out_shape], out_shape=out_shape,
                          compiler_params=_cparams(1))(*ws, *gs, *ms, *vs)


def _adamw(w, g, m, v, name):
    shape = w.shape
    cols = shape[-1]
    rows = int(math.prod(shape[:-1]))
    rb = rows
    for cand in (256, 128, 64, 32, 16, 8):
        if rows % cand == 0 and rows > cand:
            rb = cand
            break
    bc1 = 1.0 - ADAM_B1 ** ADAM_STEP
    bc2 = 1.0 - ADAM_B2 ** ADAM_STEP

    def body(w_ref, g_ref, m_ref, v_ref, d_ref, nm_ref, nv_ref):
        gg = g_ref[...]
        m2 = ADAM_B1 * m_ref[...] + (1.0 - ADAM_B1) * gg
        v2 = ADAM_B2 * v_ref[...] + (1.0 - ADAM_B2) * (gg * gg)
        m_hat = m2 / bc1
        v_hat = v2 / bc2
        d_ref[...] = -ADAM_LR * (m_hat / (jnp.sqrt(v_hat) + ADAM_EPS) + ADAM_WD * w_ref[...])
        nm_ref[...] = m2
        nv_ref[...] = v2

    spec = pl.BlockSpec((rb, cols), lambda i: (i, 0))
    sds = jax.ShapeDtypeStruct((rows, cols), F32)
    outs = pl.pallas_call(
        body, name=name, grid=(rows // rb,), in_specs=[spec] * 4, out_specs=[spec] * 3, out_shape=[sds] * 3,
        compiler_params=_cparams(1),
    )(*(z.reshape(rows, cols) for z in (w, g, m, v)))
    return tuple(o.reshape(shape) for o in outs)


def _ada_fwd(c_all, ada_w_sh, ada_b_sh):
    s = ada_w_sh.shape[2]
    sb = 512

    def body(c_ref, w_ref, b_ref, o_ref):
        cond = _silu(c_ref[...])
        o_ref[0] = _bdot(cond, w_ref[0]) + b_ref[0]

    return pl.pallas_call(
        body, name="ada_fwd", grid=(2, s // sb),
        in_specs=[_full((8, D)), pl.BlockSpec((1, D, sb), lambda l, j: (l, 0, j)), pl.BlockSpec((1, 1, sb), lambda l, j: (l, 0, j))],
        out_specs=pl.BlockSpec((1, 8, sb), lambda l, j: (l, 0, j)), out_shape=jax.ShapeDtypeStruct((2, 8, s), F32),
        compiler_params=_cparams(2),
    )(c_all, ada_w_sh, ada_b_sh)


def _ada_bwd(c_all, dmod_sh, dmod_all):
    s = dmod_sh.shape[2]
    sb = 512

    def body(c_ref, d_ref, o_ref):
        cond = _silu(c_ref[...])
        o_ref[0] = _bdot(cond, d_ref[0], TN)

    gw = pl.pallas_call(
        body, name="ada_bwd_w", grid=(2, s // sb),
        in_specs=[_full((8, D)), pl.BlockSpec((1, 8, sb), lambda l, j: (l, 0, j))],
        out_specs=pl.BlockSpec((1, D, sb), lambda l, j: (l, 0, j)), out_shape=jax.ShapeDtypeStruct((2, D, s), F32),
        compiler_params=_cparams(2),
    )(c_all, dmod_sh)

    def body_b(d_ref, o_ref):
        acc = d_ref[0, 0:1, :]
        for k in range(1, 8):
            acc = acc + d_ref[0, k:k + 1, :]
        o_ref[0] = acc

    gb = pl.pallas_call(
        body_b, name="ada_bwd_b", grid=(2,), in_specs=[pl.BlockSpec((1, 8, 6 * D), lambda l: (l, 0, 0))],
        out_specs=pl.BlockSpec((1, 1, 6 * D), lambda l: (l, 0, 0)), out_shape=jax.ShapeDtypeStruct((2, 1, 6 * D), F32),
        compiler_params=_cparams(1),
    )(dmod_all)
    return gw, gb.reshape(2, 6 * D)


def _f_in(h, nw, sc, sh, w_main, w_dt):
    t = h.shape[0]
    tb = _tblock(t)

    def body(h_ref, nw_ref, sc_ref, sh_ref, w_ref, wd_ref, p_ref, dt_ref, u_ref):
        _, n = _rms(h_ref[...])
        u = ((n * nw_ref[...]) * (1.0 + sc_ref[...]) + sh_ref[...]).astype(BF16)
        u_ref[...] = u
        p_ref[...] = _dot(u, w_ref[...], NT)
        dt_ref[...] = _dot(u, wd_ref[...], NT)

    return pl.pallas_call(
        body, name="f_in", grid=(t // tb,),
        in_specs=[pl.BlockSpec((tb, D), lambda i: (i, 0)), _row(D), _row(D), _row(D), _full((PW, D)), _full((DTW, D))],
        out_specs=[pl.BlockSpec((tb, PW), lambda i: (i, 0)), pl.BlockSpec((tb, DTW), lambda i: (i, 0)),
                   pl.BlockSpec((tb, D), lambda i: (i, 0))],
        out_shape=[jax.ShapeDtypeStruct((t, PW), F32), jax.ShapeDtypeStruct((t, DTW), F32), jax.ShapeDtypeStruct((t, D), BF16)],
        compiler_params=_cparams(1),
    )(h, nw, sc, sh, w_main, w_dt)


def _norm_bwd_step(du_v, x, dres_v, gated, nwv, scv, dx_ref, dsc_ref, dsh_ref, dnw_ref, dg_ref):
    r, n = _rms(x)
    scale = 1.0 + scv
    dsc_ref[...] += _colsum(du_v * (n * nwv))
    dsh_ref[...] += _colsum(du_v)
    dnw_ref[...] += _colsum(du_v * scale * n)
    dg_ref[...] += _colsum(dres_v * gated)
    dx_ref[...] = dres_v + _rms_bwd(r, n, du_v * scale * nwv)


def _b_in(dab, dz, dxbc, ds5, ddt, w_main, w_dt, x, dres, gated, nw, sc):
    t = dab.shape[0]
    tb = _tblock(t)

    def body(a_ref, z_ref, x_ref, s_ref, d_ref, w_ref, wd_ref, h_ref, dr_ref, g_ref, nw_ref, sc_ref,
             dx_ref, dsc_ref, dsh_ref, dnw_ref, dg_ref):
        @pl.when(pl.program_id(0) == 0)
        def _():
            for r in (dsc_ref, dsh_ref, dnw_ref, dg_ref):
                r[...] = jnp.zeros_like(r)

        du = _bdot(a_ref[...], w_ref[0:1024, :])
        du += _bdot(z_ref[...], w_ref[1024:1280, :])
        du += _bdot(s_ref[...], w_ref[1280:1536, :])
        du += _bdot(x_ref[...], w_ref[1536:2304, :])
        du += _bdot(d_ref[...], wd_ref[...])
        _norm_bwd_step(du, h_ref[...], dr_ref[...], g_ref[...], nw_ref[...], sc_ref[...], dx_ref, dsc_ref, dsh_ref, dnw_ref, dg_ref)

    blk = lambda n: pl.BlockSpec((tb, n), lambda i: (i, 0))
    row = jax.ShapeDtypeStruct((1, D), F32)
    return pl.pallas_call(
        body, name="b_in", grid=(t // tb,),
        in_specs=[blk(1024), blk(256), blk(768), blk(256), blk(DTW), _full((PW, D)), _full((DTW, D)),
                  blk(D), blk(D), blk(D), _row(D), _row(D)],
        out_specs=[blk(D), _row(D), _row(D), _row(D), _row(D)],
        out_shape=[jax.ShapeDtypeStruct((t, D), F32), row, row, row, row], compiler_params=_cparams(1),
    )(dab, dz, dxbc, ds5, ddt, w_main, w_dt, x, dres, gated, nw, sc)


def _b_normmod(du, x, dres, gated, nw, sc, name):
    t = x.shape[0]
    tb = _tblock(t)

    def body(du_ref, x_ref, dr_ref, g_ref, nw_ref, sc_ref, dx_ref, dsc_ref, dsh_ref, dnw_ref, dg_ref):
        @pl.when(pl.program_id(0) == 0)
        def _():
            for r in (dsc_ref, dsh_ref, dnw_ref, dg_ref):
                r[...] = jnp.zeros_like(r)

        _norm_bwd_step(du_ref[...], x_ref[...], dr_ref[...], g_ref[...], nw_ref[...], sc_ref[...],
                       dx_ref, dsc_ref, dsh_ref, dnw_ref, dg_ref)

    blk = pl.BlockSpec((tb, D), lambda i: (i, 0))
    row = jax.ShapeDtypeStruct((1, D), F32)
    return pl.pallas_call(
        body, name=name, grid=(t // tb,), in_specs=[blk, blk, blk, blk, _row(D), _row(D)],
        out_specs=[blk, _row(D), _row(D), _row(D), _row(D)], out_shape=[jax.ShapeDtypeStruct((t, D), F32), row, row, row, row],
        compiler_params=_cparams(1),
    )(du, x, dres, gated, nw, sc)


HALO = 16


def _lane_group(shape):
    return lax.broadcasted_iota(jnp.int32, shape, 1) // 64


def _window_select(g, s2, s4, s8, s16):
    return jnp.where(g == 0, s2, jnp.where(g == 1, s4, jnp.where(g == 2, s8, s16)))


def _pool_count(t0, rows):
    g = _lane_group((rows, GW))
    win = _window_select(g, 2, 4, 8, 16)
    tt = t0 + lax.broadcasted_iota(jnp.int32, (rows, GW), 0)
    return jnp.minimum(tt + 1, win).astype(F32)


def _pool_p(v_ext, t0, tb):
    s2 = v_ext + _roll(v_ext, 1)
    s4 = s2 + _roll(s2, 2)
    s8 = s4 + _roll(s4, 4)
    s16 = s8 + _roll(s8, 8)
    ws = _window_select(_lane_group(v_ext.shape), s2, s4, s8, s16)[HALO:]
    return ws / _pool_count(t0, tb) - v_ext[HALO:]


def _sconv(q_ext, w):
    return (_roll(q_ext, 2) * w[0:1] + _roll(q_ext, 1) * w[1:2] + q_ext * w[2:3])[HALO:]


def _halo_specs(t, tb, cols, col_block):
    per = tb // HALO
    last = t // HALO - 1
    prev = pl.BlockSpec((HALO, cols), lambda i: (jnp.maximum(i * per - 1, 0), col_block))
    nxt = pl.BlockSpec((HALO, cols), lambda i: (jnp.minimum((i + 1) * per, last), col_block))
    return prev, nxt


def _f_ab(proj, pool_mat, pool_scale, sconv_w):
    t = proj.shape[0]
    tb = _tblock(t)
    prev, _ = _halo_specs(t, tb, 1024, 0)

    def body(p_ref, h_ref, pm_ref, ps_ref, sw_ref, ya_ref, yb_ref):
        i = pl.program_id(0)
        halo = jnp.where(i > 0, h_ref[...], 0.0)
        ext = jnp.concatenate([halo, p_ref[...]], axis=0)
        p = _pool_p(ext[:, 0:256], i * tb, tb)
        ya_ref[...] = _bdot(p, pm_ref[...]) * ps_ref[...]
        q_ext = ext[:, 512:768] * ext[:, 768:1024]
        yb_ref[...] = p_ref[:, 256:512] * _sconv(q_ext, sw_ref[...])

    blk = pl.BlockSpec((tb, GW), lambda i: (i, 0))
    sds = jax.ShapeDtypeStruct((t, GW), F32)
    return pl.pallas_call(
        body, name="f_ab", grid=(t // tb,),
        in_specs=[pl.BlockSpec((tb, 1024), lambda i: (i, 0)), prev, _full((GW, GW)), _row(GW), _full((3, GW))],
        out_specs=[blk, blk], out_shape=[sds, sds], compiler_params=_cparams(1),
    )(proj, proj, pool_mat, pool_scale, sconv_w)


def _b_ab(proj, dya, dyb, pool_mat, pool_scale, sconv_w):
    t = proj.shape[0]
    tb = _tblock(t)
    nb = t // tb
    prev, nxt = _halo_specs(t, tb, 1024, 0)
    _, nxt_g = _halo_specs(t, tb, GW, 0)
    n_ext = tb + HALO

    def body(p_ref, hp_ref, hn_ref, da_ref, dan_ref, db_ref, dbn_ref, pm_ref, ps_ref, sw_ref,
             o_ref, dpm_ref, dps_ref, dsw_ref):
        i = pl.program_id(0)

        @pl.when(i == 0)
        def _():
            for r in (dpm_ref, dps_ref, dsw_ref):
                r[...] = jnp.zeros_like(r)

        last = i == nb - 1
        halo = jnp.where(i > 0, hp_ref[...], 0.0)
        main = p_ref[...]
        ext = jnp.concatenate([halo, main], axis=0)
        scale = ps_ref[...]
        pm = pm_ref[...]
        p = _pool_p(ext[:, 0:256], i * tb, tb)
        da = da_ref[...]
        dps_ref[...] += _colsum(da * _bdot(p, pm))
        da_ext = jnp.concatenate([da, jnp.where(last, 0.0, dan_ref[...])], axis=0)
        dys = da_ext * scale
        dpm_ref[...] += _bdot(p, dys[:tb], TN)
        dp = _bdot(dys, pm, NT)
        dpc = dp / _pool_count(i * tb, n_ext)
        a2 = dpc + _roll(dpc, n_ext - 1)
        a4 = a2 + _roll(a2, n_ext - 2)
        a8 = a4 + _roll(a4, n_ext - 4)
        a16 = a8 + _roll(a8, n_ext - 8)
        o_ref[:, 0:256] = (_window_select(_lane_group(dpc.shape), a2, a4, a8, a16) - dp)[:tb]
        w = sw_ref[...]
        gb, gc, hh = main[:, 256:512], main[:, 512:768], main[:, 768:1024]
        q_ext = ext[:, 512:768] * ext[:, 768:1024]
        db = db_ref[...]
        o_ref[:, 256:512] = db * _sconv(q_ext, w)
        gb_next = hn_ref[:, 256:512]
        dconv = jnp.concatenate([db * gb, jnp.where(last, 0.0, dbn_ref[...] * gb_next)], axis=0)
        dq = (dconv * w[2:3] + _roll(dconv, n_ext - 1) * w[1:2] + _roll(dconv, n_ext - 2) * w[0:1])[:tb]
        o_ref[:, 512:768] = dq * hh
        o_ref[:, 768:1024] = dq * gc
        dc = dconv[:tb]
        dsw_ref[0:1, :] += _colsum(dc * _roll(q_ext, 2)[HALO:])
        dsw_ref[1:2, :] += _colsum(dc * _roll(q_ext, 1)[HALO:])
        dsw_ref[2:3, :] += _colsum(dc * q_ext[HALO:])

    blk = pl.BlockSpec((tb, GW), lambda i: (i, 0))
    return pl.pallas_call(
        body, name="b_ab", grid=(nb,),
        in_specs=[pl.BlockSpec((tb, 1024), lambda i: (i, 0)), prev, nxt, blk, nxt_g, blk, nxt_g,
                  _full((GW, GW)), _row(GW), _full((3, GW))],
        out_specs=[pl.BlockSpec((tb, 1024), lambda i: (i, 0)), _full((GW, GW)), _row(GW), _full((3, GW))],
        out_shape=[jax.ShapeDtypeStruct((t, 1024), F32), jax.ShapeDtypeStruct((GW, GW), F32),
                   jax.ShapeDtypeStruct((1, GW), F32), jax.ShapeDtypeStruct((3, GW), F32)],
        compiler_params=_cparams(1),
    )(proj, proj, proj, dya, dya, dyb, dyb, pool_mat, pool_scale, sconv_w)


CH = 8


def _ssd_conv(x, halo, w, b):
    ext = jnp.concatenate([halo, x], axis=0)
    pre = ext * w[3:4] + _roll(ext, 1) * w[2:3] + _roll(ext, 2) * w[1:2] + _roll(ext, 3) * w[0:1] + b
    return pre[CH:], ext


def _ssd_common(dt_raw, dtb, alog):
    ll = dt_raw.shape[0]
    dtv = _softplus(dt_raw + dtb)
    a_row = -jnp.exp(alog)
    r = lax.broadcasted_iota(jnp.int32, (ll, ll), 0)
    c = lax.broadcasted_iota(jnp.int32, (ll, ll), 1)
    tril = (r >= c).astype(F32)
    cs = _dot(tril, dtv * a_row, prec=HI)
    return dtv, a_row, cs, cs.T, r >= c


def _bd(a, b, ca, cb):
    return lax.dot_general(a, b, (((ca,), (cb,)), ((0,), (0,))), preferred_element_type=F32)


def _head_cols(m):
    return jnp.stack([m[:, h:h + 1] for h in range(NH)])


def _ssd_heads(act, dtv, cs, cs_t, causal):
    xs = jnp.stack([act[:, HP * h:HP * (h + 1)] for h in range(NH)])
    bm = jnp.stack([act[:, 256 + NS * (h // 2):256 + NS * (h // 2 + 1)] for h in range(NH)])
    cm = jnp.stack([act[:, 512 + NS * (h // 2):512 + NS * (h // 2 + 1)] for h in range(NH)])
    cs_c = _head_cols(cs)
    cs_r = jnp.stack([cs_t[h:h + 1, :] for h in range(NH)])
    mdec = jnp.where(causal[None], jnp.exp(jnp.minimum(cs_c - cs_r, 0.0)), 0.0)
    g2 = _bd(jnp.stack([cm[0], cm[2]]), jnp.stack([bm[0], bm[2]]), 2, 2)
    sc = jnp.stack([g2[h // 2] for h in range(NH)]) * mdec
    dt_c = _head_cols(dtv)
    xdt = xs * dt_c
    e = jnp.exp(cs_c)
    cs_last = cs_c[:, SSD_L - 1:SSD_L, :]
    wdec = jnp.exp(cs_last - cs_c)
    return xs, bm, cm, mdec, sc, dt_c, xdt, e, cs_last, wdec


def _head_scalars(row_ref):
    return jnp.stack([row_ref[0:1, h:h + 1] for h in range(NH)])


def _f_ssd(proj, dtp, conv_w, conv_b, dt_bias, a_log, d_skip):
    t = proj.shape[0]
    nc = t // SSD_L
    rows = SSD_SUB * SSD_L
    per = rows // CH

    def body(x_ref, hx_ref, dt_ref, z_ref, cw_ref, cb_ref, dtb_ref, al_ref, dk_ref, y_ref, yp_ref, sp_ref, s_ref):
        i = pl.program_id(0)

        @pl.when(i == 0)
        def _():
            s_ref[...] = jnp.zeros_like(s_ref)

        state = s_ref[...]
        dk = _head_scalars(dk_ref)
        for sub in range(SSD_SUB):
            r0 = sub * SSD_L
            rs = slice(r0, r0 + SSD_L)
            halo = jnp.where(i > 0, hx_ref[...], 0.0) if sub == 0 else x_ref[r0 - CH:r0, :]
            pre, _ = _ssd_conv(x_ref[rs, :], halo, cw_ref[...], cb_ref[...])
            act = _silu(pre)
            dtv, _, cs, cs_t, causal = _ssd_common(dt_ref[rs, :], dtb_ref[...], al_ref[...])
            xs, bm, cm, _, sc, _, xdt, e, cs_last, wdec = _ssd_heads(act, dtv, cs, cs_t, causal)
            sp_ref[sub] = state
            y = _bd(sc, xdt, 2, 1) + e * _bd(cm, state, 2, 2) + xs * dk
            for h in range(NH):
                yp_ref[rs, HP * h:HP * (h + 1)] = y[h]
            state = state * jnp.exp(cs_last) + _bd(xdt * wdec, bm, 1, 1)
            y_ref[rs, :] = yp_ref[rs, :] * _silu(z_ref[rs, :])
        s_ref[...] = state

    blk = pl.BlockSpec((rows, GW), lambda i: (i, 0))
    sds = jax.ShapeDtypeStruct((t, GW), F32)
    return pl.pallas_call(
        body, name="f_ssd", grid=(nc // SSD_SUB,),
        in_specs=[pl.BlockSpec((rows, 768), lambda i: (i, 2)),
                  pl.BlockSpec((CH, 768), lambda i: (jnp.maximum(i * per - 1, 0), 2)),
                  pl.BlockSpec((rows, DTW), lambda i: (i, 0)),
                  pl.BlockSpec((rows, GW), lambda i: (i, 4)),
                  _full((4, 768)), _row(768), _row(DTW), _row(DTW), _row(DTW)],
        out_specs=[blk, blk, pl.BlockSpec((SSD_SUB, NH, HP, NS), lambda i: (i, 0, 0, 0))],
        out_shape=[sds, sds, jax.ShapeDtypeStruct((nc, NH, HP, NS), F32)],
        scratch_shapes=[pltpu.VMEM((NH, HP, NS), F32)], compiler_params=_cparams(1),
    )(proj, proj, dtp, proj, conv_w, conv_b, dt_bias, a_log, d_skip)


def _b_ssd(proj, dtp, ypre, dyc, sprev, conv_w, conv_b, dt_bias, a_log, d_skip):
    t = proj.shape[0]
    nc = t // SSD_L
    steps = nc // SSD_SUB_BWD
    rows = SSD_SUB_BWD * SSD_L
    per = rows // CH
    n_ext = SSD_L + CH

    def chunk(sub, halo, dnext, ds_in, refs):
        (x_ref, dt_ref, z_ref, yp_ref, dy_ref, sp_ref, cw_ref, cb_ref, dtb_ref, al_ref, dk_ref,
         dz_ref, dx_ref, ddt_ref, dact_ref) = refs
        rs = slice(sub * SSD_L, (sub + 1) * SSD_L)
        dact = dact_ref.at[sub]
        w = cw_ref[...]
        pre, ext = _ssd_conv(x_ref[rs, :], halo, w, cb_ref[...])
        act = _silu(pre)
        dt_raw = dt_ref[rs, :]
        dtv, a_row, cs, cs_t, causal = _ssd_common(dt_raw, dtb_ref[...], al_ref[...])
        z = z_ref[rs, :]
        dyc_v = dy_ref[rs, :]
        dz_ref[rs, :] = dyc_v * yp_ref[rs, :] * _dsilu(z)
        dy_all = dyc_v * _silu(z)
        lane = lax.broadcasted_iota(jnp.int32, (SSD_L, DTW), 1)
        rowi = lax.broadcasted_iota(jnp.int32, (1, SSD_L, 1), 1)
        lane1 = lax.broadcasted_iota(jnp.int32, (1, DTW), 1)
        xs, bm, cm, mdec, sc, dt_c, xdt, e, cs_last, wdec = _ssd_heads(act, dtv, cs, cs_t, causal)
        dy = jnp.stack([dy_all[:, HP * h:HP * (h + 1)] for h in range(NH)])
        prev = sp_ref[sub]
        ds = ds_in
        lsum = lambda v: jnp.sum(v, axis=2, keepdims=True)
        dsc = _bd(dy, xdt, 2, 2)
        q = dsc * sc
        dg = dsc * mdec
        dxdt = _bd(sc, dy, 1, 1)
        dcs = lsum(q) - lsum(jnp.swapaxes(q, 1, 2))
        dc = _bd(dg, bm, 2, 1)
        db = _bd(dg, cm, 1, 1)
        cp = _bd(cm, prev, 2, 2)
        dcs += lsum(dy * cp) * e
        ey = e * dy
        dc += _bd(ey, prev, 2, 1)
        dprev = _bd(ey, cm, 1, 1)
        elast = jnp.exp(cs_last)
        dprev += ds * elast
        dcs_last = jnp.sum(lsum(ds * prev), axis=1, keepdims=True) * elast
        bds = _bd(bm, ds, 2, 2)
        dxdt += wdec * bds
        db += wdec * _bd(xdt, ds, 2, 1)
        dw = lsum(xdt * bds) * wdec
        dcs -= dw
        dcs_last += jnp.sum(dw, axis=1, keepdims=True)
        dcs += jnp.where(rowi == SSD_L - 1, dcs_last, 0.0)
        dxs = dxdt * dt_c + dy * _head_scalars(dk_ref)
        ddtx = lsum(dxdt * xs)
        ddk = jnp.sum(lsum(dy * xs), axis=1, keepdims=True)
        dcs_mat = jnp.zeros((SSD_L, DTW), F32)
        ddtx_mat = jnp.zeros((SSD_L, DTW), F32)
        ddk_row = jnp.zeros((1, DTW), F32)
        for h in range(NH):
            dact[:, HP * h:HP * (h + 1)] = dxs[h]
            dcs_mat = jnp.where(lane == h, dcs[h], dcs_mat)
            ddtx_mat = jnp.where(lane == h, ddtx[h], ddtx_mat)
            ddk_row = jnp.where(lane1 == h, ddk[h], ddk_row)
        for g in range(2):
            dact[:, 256 + NS * g:256 + NS * (g + 1)] = db[2 * g] + db[2 * g + 1]
            dact[:, 512 + NS * g:512 + NS * (g + 1)] = dc[2 * g] + dc[2 * g + 1]
        ds_out = dprev
        r2 = lax.broadcasted_iota(jnp.int32, (SSD_L, SSD_L), 0)
        c2 = lax.broadcasted_iota(jnp.int32, (SSD_L, SSD_L), 1)
        dadt = _dot((c2 >= r2).astype(F32), dcs_mat, prec=HI)
        ddt = jnp.where(lane < NH, (dadt * a_row + ddtx_mat) * _sig(dt_raw + dtb_ref[...]), 0.0)
        ddt_ref[rs, :] = ddt
        dpre = dact[...] * _dsilu(pre)
        dcw = jnp.concatenate([_colsum(dpre * _roll(ext, 3 - k)[CH:]) for k in range(4)], axis=0)
        dext = jnp.concatenate([dpre, dnext], axis=0)
        dx_ref[rs, :] = (dext * w[3:4] + _roll(dext, n_ext - 1) * w[2:3] + _roll(dext, n_ext - 2) * w[1:2]
                         + _roll(dext, n_ext - 3) * w[0:1])[:SSD_L]
        acc = (dcw, _colsum(dpre), _colsum(ddt), _colsum(dadt * dtv) * a_row, ddk_row)
        return dpre[0:CH], ds_out, acc

    def body(x_ref, hx_ref, dt_ref, z_ref, yp_ref, dy_ref, sp_ref, cw_ref, cb_ref, dtb_ref, al_ref, dk_ref,
             dz_ref, dx_ref, ddt_ref, dcw_ref, dcb_ref, ddtb_ref, dal_ref, ddk_ref, ds_ref, dnext_ref, dact_ref):
        i = pl.program_id(0)
        acc_refs = (dcw_ref, dcb_ref, ddtb_ref, dal_ref, ddk_ref)

        @pl.when(i == 0)
        def _():
            ds_ref[...] = jnp.zeros_like(ds_ref)
            dnext_ref[...] = jnp.zeros_like(dnext_ref)
            for r in acc_refs:
                r[...] = jnp.zeros_like(r)

        refs = (x_ref, dt_ref, z_ref, yp_ref, dy_ref, sp_ref, cw_ref, cb_ref, dtb_ref, al_ref, dk_ref, dz_ref, dx_ref, ddt_ref,
                dact_ref)
        ds = ds_ref[...]
        dnext = dnext_ref[...]
        total = None
        for sub in reversed(range(SSD_SUB_BWD)):
            if sub == 0:
                halo = jnp.where(i == steps - 1, 0.0, hx_ref[...])
            else:
                halo = x_ref[sub * SSD_L - CH:sub * SSD_L, :]
            dnext, ds, acc = chunk(sub, halo, dnext, ds, refs)
            total = acc if total is None else tuple(a + b for a, b in zip(total, acc))
        ds_ref[...] = ds
        dnext_ref[...] = dnext
        for r, v in zip(acc_refs, total):
            r[...] += v

    rev = lambda i: steps - 1 - i
    blk = lambda n, cb=0: pl.BlockSpec((rows, n), lambda i: (rev(i), cb))
    row = lambda n: jax.ShapeDtypeStruct((1, n), F32)
    return pl.pallas_call(
        body, name="b_ssd", grid=(steps,),
        in_specs=[blk(768, 2), pl.BlockSpec((CH, 768), lambda i: (jnp.maximum(rev(i) * per - 1, 0), 2)),
                  blk(DTW), blk(GW, 4), blk(GW), blk(GW), pl.BlockSpec((SSD_SUB_BWD, NH, HP, NS), lambda i: (rev(i), 0, 0, 0)),
                  _full((4, 768)), _row(768), _row(DTW), _row(DTW), _row(DTW)],
        out_specs=[blk(GW), blk(768), blk(DTW), _full((4, 768)), _row(768), _row(DTW), _row(DTW), _row(DTW)],
        out_shape=[jax.ShapeDtypeStruct((t, GW), F32), jax.ShapeDtypeStruct((t, 768), F32), jax.ShapeDtypeStruct((t, DTW), F32),
                   jax.ShapeDtypeStruct((4, 768), F32), row(768), row(DTW), row(DTW), row(DTW)],
        scratch_shapes=[pltpu.VMEM((NH, HP, NS), F32), pltpu.VMEM((CH, 768), F32), pltpu.VMEM((SSD_SUB_BWD, SSD_L, 768), F32)],
        compiler_params=_cparams(1),
    )(proj, proj, dtp, proj, ypre, dyc, sprev, conv_w, conv_b, dt_bias, a_log, d_skip)


def _s5_block(t):
    return min(t, 256)


def _seg_t():
    r = lax.broadcasted_iota(jnp.int32, (64, 1024), 0)
    c = lax.broadcasted_iota(jnp.int32, (64, 1024), 1)
    return (c // 16 == r).astype(F32)


def _s5_prep_math(a_re, a_im, lstep, b_re, b_im):
    step = jnp.exp(lstep)
    ars = a_re * step
    ais = a_im * step
    mag = jnp.exp(ars)
    lr = mag * jnp.cos(ais)
    li = mag * jnp.sin(ais)
    den = a_re * a_re + a_im * a_im
    nr = lr - 1.0
    f_re = (nr * a_re + li * a_im) / den
    f_im = (li * a_re - nr * a_im) / den
    seg = _seg_t()
    fr = _dot(f_re, seg, prec=HI)
    fi = _dot(f_im, seg, prec=HI)
    return lr, li, fr * b_re - fi * b_im, fr * b_im + fi * b_re, ars, ais


def _s5_prep(a_re, a_im, lstep, b_re, b_im):
    def body(ar, ai, ls, br, bi, lr_o, li_o, bbr_o, bbi_o, ars_o, ais_o):
        outs = _s5_prep_math(ar[...], ai[...], ls[...], br[...], bi[...])
        for o, v in zip((lr_o, li_o, bbr_o, bbi_o, ars_o, ais_o), outs):
            o[...] = v

    s64 = jax.ShapeDtypeStruct((16, 64), F32)
    s1k = jax.ShapeDtypeStruct((16, 1024), F32)
    return pl.pallas_call(body, name="s5_prep", out_shape=[s64, s64, s1k, s1k, s64, s64])(a_re, a_im, lstep, b_re, b_im)


def _s5_prep_bwd(a_re, a_im, lstep, b_re, b_im, dlr, dli, dbbr, dbbi):
    def body(ar, ai, ls, br, bi, g0, g1, g2, g3, o0, o1, o2, o3, o4):
        f = lambda *a: _s5_prep_math(*a)[:4]
        _, vjp = jax.vjp(f, ar[...], ai[...], ls[...], br[...], bi[...])
        for o, v in zip((o0, o1, o2, o3, o4), vjp((g0[...], g1[...], g2[...], g3[...]))):
            o[...] = v

    s64 = jax.ShapeDtypeStruct((16, 64), F32)
    s1k = jax.ShapeDtypeStruct((16, 1024), F32)
    return pl.pallas_call(body, name="s5_prep_bwd", out_shape=[s64, s64, jax.ShapeDtypeStruct((16, 1), F32), s1k, s1k])(
        a_re, a_im, lstep, b_re, b_im, dlr, dli, dbbr, dbbi)


SUB = 8


def _s5_tables(ars, ais):
    def body(ar, ai, tr, ti):
        rr = lax.broadcasted_iota(jnp.int32, (8 * SUB, S5_P), 0)
        seg, r = rr // SUB, rr % SUB
        step = jnp.where((seg == 1) | (seg == 4), 1, jnp.where((seg == 2) | (seg == 5), 2, 4))
        n = jnp.where(seg == 0, r + 1, jnp.where(seg == 7, SUB - r, step))
        fwd_gap = jnp.where(seg <= 3, r - step, SUB - step - 1 - r)
        gap = jnp.where((seg == 0) | (seg == 7), 0, fwd_gap)
        nf = n.astype(F32)
        mag = jnp.where(gap >= 0, jnp.exp(nf * ar[...]), 0.0)
        tr[...] = mag * jnp.cos(nf * ai[...])
        ti[...] = mag * jnp.sin(nf * ai[...])

    sds = jax.ShapeDtypeStruct((8 * SUB, S5_P), F32)
    return pl.pallas_call(body, name="s5_tables", out_shape=[sds] * 2)(ars, ais)


def _s5_table(tb_r, tb_i, k):
    return tb_r[SUB * k:SUB * (k + 1), :], tb_i[SUB * k:SUB * (k + 1), :]


def _s5_scan(bu_r, bu_i, tb_r, tb_i, c_r, c_i, lb):
    nt = lb // SUB
    sr, si = bu_r.reshape(nt, SUB, S5_P), bu_i.reshape(nt, SUB, S5_P)
    for j, k in enumerate((1, 2, 4)):
        mr, mi = _s5_table(tb_r, tb_i, 1 + j)
        tr, ti = pltpu.roll(sr, k, axis=1), pltpu.roll(si, k, axis=1)
        sr, si = sr + mr * tr - mi * ti, si + mr * ti + mi * tr
    pr, pi = _s5_table(tb_r, tb_i, 0)
    out_r, out_i = [], []
    for j in range(nt):
        a_r = sr[j] + pr * c_r - pi * c_i
        a_i = si[j] + pr * c_i + pi * c_r
        out_r.append(a_r)
        out_i.append(a_i)
        c_r, c_i = a_r[SUB - 1:SUB], a_i[SUB - 1:SUB]
    return jnp.concatenate(out_r, axis=0), jnp.concatenate(out_i, axis=0)


def _s5_rscan(g_r, g_i, tb_r, tb_i, n_r, n_i, lb):
    nt = lb // SUB
    gr, gi = g_r.reshape(nt, SUB, S5_P), g_i.reshape(nt, SUB, S5_P)
    for j, k in enumerate((1, 2, 4)):
        mr, mi = _s5_table(tb_r, tb_i, 4 + j)
        tr, ti = pltpu.roll(gr, SUB - k, axis=1), pltpu.roll(gi, SUB - k, axis=1)
        gr, gi = gr + mr * tr + mi * ti, gi + mr * ti - mi * tr
    qr, qi = _s5_table(tb_r, tb_i, 7)
    out_r, out_i = [None] * nt, [None] * nt
    for j in reversed(range(nt)):
        a_r = gr[j] + qr * n_r + qi * n_i
        a_i = gi[j] + qr * n_i - qi * n_r
        out_r[j], out_i[j] = a_r, a_i
        n_r, n_i = a_r[0:1], a_i[0:1]
    return jnp.concatenate(out_r, axis=0), jnp.concatenate(out_i, axis=0)


def _s5_y(u, sr, si, cre, cim, dsk):
    return _bdot(sr, cre) + _bdot(si, cim) + dsk * u


def _f_s5(proj, bmat, cre, cim, p_r, p_i, dsk, glu_w, glu_b):
    t = proj.shape[0]
    lb = _s5_block(t)
    nb = t // lb

    def body(u_ref, bm_ref, cr_ref, ci_ref, pr_ref, pi_ref, dk_ref, gw_ref, gb_ref, y_ref, car_ref, s_ref, st_ref):
        @pl.when(pl.program_id(0) == 0)
        def _():
            st_ref[...] = jnp.zeros_like(st_ref)

        u = u_ref[...]
        bu = _bdot(u, bm_ref[...])
        c_r, c_i = st_ref[0:1, 0:S5_P], st_ref[0:1, S5_P:]
        car_ref[0] = st_ref[0:1, :]
        sr, si = _s5_scan(bu[:, :S5_P], bu[:, S5_P:], pr_ref, pi_ref, c_r, c_i, lb)
        st_ref[0:1, 0:S5_P] = sr[lb - 1:lb]
        st_ref[0:1, S5_P:] = si[lb - 1:lb]
        sr_b, si_b = sr.astype(BF16), si.astype(BF16)
        s_ref[:, 0:S5_P] = sr_b
        s_ref[:, S5_P:] = si_b
        gel = _gelu(_s5_y(u, sr_b, si_b, cr_ref[...], ci_ref[...], dk_ref[...]))
        y_ref[...] = gel * _sig(_bdot(gel, gw_ref[...]) + gb_ref[...])

    return pl.pallas_call(
        body, name="f_s5", grid=(nb,),
        in_specs=[pl.BlockSpec((lb, GW), lambda i: (i, 5)),
                  _full((GW, 2 * S5_P)), _full((S5_P, GW)), _full((S5_P, GW)), _full((8 * SUB, S5_P)), _full((8 * SUB, S5_P)),
                  _row(GW), _full((GW, GW)), _row(GW)],
        out_specs=[pl.BlockSpec((lb, GW), lambda i: (i, 0)), pl.BlockSpec((1, 1, 2 * S5_P), lambda i: (i, 0, 0)),
                   pl.BlockSpec((lb, 2 * S5_P), lambda i: (i, 0))],
        out_shape=[jax.ShapeDtypeStruct((t, GW), F32), jax.ShapeDtypeStruct((nb, 1, 2 * S5_P), F32),
                   jax.ShapeDtypeStruct((t, 2 * S5_P), BF16)],
        scratch_shapes=[pltpu.VMEM((8, 2 * S5_P), F32)], compiler_params=_cparams(1),
    )(proj, bmat, cre, cim, p_r, p_i, dsk, glu_w, glu_b)


def _b_s5(proj, dyd, carries, states, bmat, cre, cim, p_r, p_i, dsk, glu_w, glu_b):
    t = proj.shape[0]
    lb = _s5_block(t)
    nb = t // lb

    def body(u_ref, dy_ref, car_ref, s_ref, bm_ref, cr_ref, ci_ref, pr_ref, pi_ref, dk_ref, gw_ref, gb_ref,
             du_ref, dbm_ref, dcr_ref, dci_ref, dlam_ref, ddk_ref, dgw_ref, dgb_ref, gc_ref):
        @pl.when(pl.program_id(0) == 0)
        def _():
            gc_ref[...] = jnp.zeros_like(gc_ref)
            for r in (dbm_ref, dcr_ref, dci_ref, dlam_ref, ddk_ref, dgw_ref, dgb_ref):
                r[...] = jnp.zeros_like(r)

        u = u_ref[...]
        bm = bm_ref[...]
        u_b = u.astype(BF16)
        c_r, c_i = car_ref[0, 0:1, 0:S5_P], car_ref[0, 0:1, S5_P:]
        cre_v, cim_v, dk, gw = cr_ref[...], ci_ref[...], dk_ref[...], gw_ref[...]
        sr_b, si_b = s_ref[:, 0:S5_P], s_ref[:, S5_P:]
        sr, si = sr_b.astype(F32), si_b.astype(F32)
        y = _dot(sr_b, cre_v) + _dot(si_b, cim_v) + dk * u
        gel = _gelu(y)
        gel_b = gel.astype(BF16)
        gate = _sig(_dot(gel_b, gw) + gb_ref[...])
        dout = dy_ref[...]
        t1 = dout * gel * gate * (1.0 - gate)
        t1_b = t1.astype(BF16)
        dgw_ref[...] += _dot(gel_b, t1_b, TN)
        dgb_ref[...] += _colsum(t1)
        dyv = (dout * gate + _dot(t1_b, gw, NT)) * _dgelu(y)
        dyv_b = dyv.astype(BF16)
        ddk_ref[...] += _colsum(dyv * u)
        dcr_ref[...] += _dot(sr_b, dyv_b, TN)
        dci_ref[...] += _dot(si_b, dyv_b, TN)
        gr = _dot(dyv_b, cre_v, NT)
        gi = _dot(dyv_b, cim_v, NT)
        row = lax.broadcasted_iota(jnp.int32, (lb, S5_P), 0)
        n_r, n_i = gc_ref[0:1, 0:S5_P], gc_ref[0:1, S5_P:]
        gr, gi = _s5_rscan(gr, gi, pr_ref, pi_ref, n_r, n_i, lb)
        gc_ref[0:1, 0:S5_P] = gr[0:1]
        gc_ref[0:1, S5_P:] = gi[0:1]
        gcat = jnp.concatenate([gr, gi], axis=1).astype(BF16)
        dbm_ref[...] += _dot(u_b, gcat, TN)
        du_ref[...] = dyv * dk + _dot(gcat, bm, NT)
        spr = jnp.where(row >= 1, _roll(sr, 1), c_r)
        spi = jnp.where(row >= 1, _roll(si, 1), c_i)
        dlam_ref[0:1, :] += _colsum(gr * spr + gi * spi)
        dlam_ref[1:2, :] += _colsum(gi * spr - gr * spi)

    rev = lambda i: nb - 1 - i
    return pl.pallas_call(
        body, name="b_s5", grid=(nb,),
        in_specs=[pl.BlockSpec((lb, GW), lambda i: (rev(i), 5)), pl.BlockSpec((lb, GW), lambda i: (rev(i), 0)),
                  pl.BlockSpec((1, 1, 2 * S5_P), lambda i: (rev(i), 0, 0)), pl.BlockSpec((lb, 2 * S5_P), lambda i: (rev(i), 0)),
                  _full((GW, 2 * S5_P)), _full((S5_P, GW)), _full((S5_P, GW)), _full((8 * SUB, S5_P)), _full((8 * SUB, S5_P)),
                  _row(GW), _full((GW, GW)), _row(GW)],
        out_specs=[pl.BlockSpec((lb, GW), lambda i: (rev(i), 0)), _full((GW, 2 * S5_P)), _full((S5_P, GW)), _full((S5_P, GW)),
                   _full((2, S5_P)), _row(GW), _full((GW, GW)), _row(GW)],
        out_shape=[jax.ShapeDtypeStruct((t, GW), F32), jax.ShapeDtypeStruct((GW, 2 * S5_P), F32),
                   jax.ShapeDtypeStruct((S5_P, GW), F32), jax.ShapeDtypeStruct((S5_P, GW), F32),
                   jax.ShapeDtypeStruct((2, S5_P), F32), jax.ShapeDtypeStruct((1, GW), F32),
                   jax.ShapeDtypeStruct((GW, GW), F32), jax.ShapeDtypeStruct((1, GW), F32)],
        scratch_shapes=[pltpu.VMEM((8, 2 * S5_P), F32)], compiler_params=_cparams(1),
    )(proj, dyd, carries, states, bmat, cre, cim, p_r, p_i, dsk, glu_w, glu_b)


def _group_norm(ys, bw):
    outs, stats = [], []
    for g, y in enumerate(ys):
        r, n = _rms(y)
        stats.append((r, n))
        outs.append(n * bw[:, GW * g:GW * (g + 1)])
    return jnp.concatenate(outs, axis=1), stats


def _f_out(ya, yb, yc, yd, bw, w_out, h, g1):
    t = h.shape[0]
    tb = _tblock(t)

    def body(a_ref, b_ref, c_ref, d_ref, bw_ref, w_ref, h_ref, g_ref, h2_ref, o_ref, cat_ref):
        cat, _ = _group_norm([a_ref[...], b_ref[...], c_ref[...], d_ref[...]], bw_ref[...])
        catb = cat.astype(BF16)
        cat_ref[...] = catb
        o = _dot(catb, w_ref[...])
        o_ref[...] = o.astype(BF16)
        h2_ref[...] = h_ref[...] + g_ref[...] * o

    yblk = pl.BlockSpec((tb, GW), lambda i: (i, 0))
    blk = pl.BlockSpec((tb, D), lambda i: (i, 0))
    return pl.pallas_call(
        body, name="f_out", grid=(t // tb,), in_specs=[yblk] * 4 + [_row(D), _full((D, D)), blk, _row(D)],
        out_specs=[blk, blk, blk],
        out_shape=[jax.ShapeDtypeStruct((t, D), F32), jax.ShapeDtypeStruct((t, D), BF16), jax.ShapeDtypeStruct((t, D), BF16)],
        compiler_params=_cparams(1),
    )(ya, yb, yc, yd, bw, w_out, h, g1)


def _b_out(dh2, ya, yb, yc, yd, bw, w_out, g1):
    t = dh2.shape[0]
    tb = _tblock(t)

    def body(dh_ref, a_ref, b_ref, c_ref, d_ref, bw_ref, w_ref, g_ref, da_ref, db_ref, dc_ref, dd_ref, do_ref, dbw_ref):
        @pl.when(pl.program_id(0) == 0)
        def _():
            dbw_ref[...] = jnp.zeros_like(dbw_ref)

        do = (dh_ref[...] * g_ref[...]).astype(BF16)
        do_ref[...] = do
        dcat = _dot(do, w_ref[...], NT)
        bw_v = bw_ref[...]
        for g, (y_ref, dy_ref) in enumerate(((a_ref, da_ref), (b_ref, db_ref), (c_ref, dc_ref), (d_ref, dd_ref))):
            r, n = _rms(y_ref[...])
            dc = dcat[:, GW * g:GW * (g + 1)]
            dbw_ref[:, GW * g:GW * (g + 1)] += _colsum(dc * n)
            dy_ref[...] = _rms_bwd(r, n, dc * bw_v[:, GW * g:GW * (g + 1)])

    yblk = pl.BlockSpec((tb, GW), lambda i: (i, 0))
    blk = pl.BlockSpec((tb, D), lambda i: (i, 0))
    ysd = jax.ShapeDtypeStruct((t, GW), F32)
    return pl.pallas_call(
        body, name="b_out", grid=(t // tb,), in_specs=[blk] + [yblk] * 4 + [_row(D), _full((D, D)), _row(D)],
        out_specs=[yblk] * 4 + [blk, _row(D)],
        out_shape=[ysd] * 4 + [jax.ShapeDtypeStruct((t, D), BF16), jax.ShapeDtypeStruct((1, D), F32)],
        compiler_params=_cparams(1),
    )(dh2, ya, yb, yc, yd, bw, w_out, g1)


HB = 512
MLP_ROWS = 1024


def _w1_spec():
    per = HID // 4 // HB
    return pl.BlockSpec((1, D, HB), lambda i, k: (k // per, 0, k % per))


def _f_mlp(h2, nw, sc, sh, g2, w1, w2):
    t = h2.shape[0]
    tb = _tblock(t, MLP_ROWS)
    nk = HID // HB

    def body(h_ref, nw_ref, sc_ref, sh_ref, g_ref, w1_ref, w2_ref, h3_ref, m_ref, a_ref, v_ref, acc_ref):
        k = pl.program_id(1)

        @pl.when(k == 0)
        def _():
            _, n = _rms(h_ref[...])
            v_ref[...] = ((n * nw_ref[...]) * (1.0 + sc_ref[...]) + sh_ref[...]).astype(BF16)
            acc_ref[...] = jnp.zeros_like(acc_ref)

        a = _dot(v_ref[...], w1_ref[0])
        a_ref[...] = a.astype(BF16)
        ra = jnp.maximum(a, 0.0)
        acc_ref[...] += _dot((ra * ra).astype(BF16), w2_ref[...])

        @pl.when(k == nk - 1)
        def _():
            m = acc_ref[...]
            m_ref[...] = m.astype(BF16)
            h3_ref[...] = h_ref[...] + g_ref[...] * m

    blk = pl.BlockSpec((tb, D), lambda i, k: (i, 0))
    return pl.pallas_call(
        body, name="f_mlp", grid=(t // tb, nk),
        in_specs=[blk, _row(D), _row(D), _row(D), _row(D), _w1_spec(),
                  pl.BlockSpec((HB, D), lambda i, k: (k, 0))],
        out_specs=[blk, blk, pl.BlockSpec((tb, HB), lambda i, k: (i, k)), blk],
        out_shape=[jax.ShapeDtypeStruct((t, D), F32), jax.ShapeDtypeStruct((t, D), BF16), jax.ShapeDtypeStruct((t, HID), BF16),
                   jax.ShapeDtypeStruct((t, D), BF16)],
        scratch_shapes=[pltpu.VMEM((tb, D), F32)], compiler_params=_cparams(2),
    )(h2, nw, sc, sh, g2, w1, w2)


def _b_mlp(dh3, a, g2, w1, w2):
    t = dh3.shape[0]
    tb = _tblock(t, MLP_ROWS)
    nk = HID // HB

    def body(dh_ref, a_ref, g_ref, w1_ref, w2_ref, dv_ref, da_ref, act_ref, dm_ref):
        k = pl.program_id(1)
        dm = (dh_ref[...] * g_ref[...]).astype(BF16)

        @pl.when(k == 0)
        def _():
            dm_ref[...] = dm
            dv_ref[...] = jnp.zeros_like(dv_ref)

        ra = jnp.maximum(a_ref[...].astype(F32), 0.0)
        act_ref[...] = (ra * ra).astype(BF16)
        da = (_dot(dm, w2_ref[...], NT) * (2.0 * ra)).astype(BF16)
        da_ref[...] = da
        dv_ref[...] += _dot(da, w1_ref[0], NT)

    blk = pl.BlockSpec((tb, D), lambda i, k: (i, 0))
    hblk = pl.BlockSpec((tb, HB), lambda i, k: (i, k))
    return pl.pallas_call(
        body, name="b_mlp", grid=(t // tb, nk),
        in_specs=[blk, hblk, _row(D), _w1_spec(), pl.BlockSpec((HB, D), lambda i, k: (k, 0))],
        out_specs=[blk, hblk, hblk, blk],
        out_shape=[jax.ShapeDtypeStruct((t, D), F32), jax.ShapeDtypeStruct((t, HID), BF16), jax.ShapeDtypeStruct((t, HID), BF16),
                   jax.ShapeDtypeStruct((t, D), BF16)],
        compiler_params=_cparams(2),
    )(dh3, a, g2, w1, w2)


def _b_final(h, tgt, fw):
    t = h.shape[0]
    tb = _tblock(t)

    def body(h_ref, t_ref, w_ref, dh_ref, loss_ref, dfw_ref):
        @pl.when(pl.program_id(0) == 0)
        def _():
            loss_ref[...] = jnp.zeros_like(loss_ref)
            dfw_ref[...] = jnp.zeros_like(dfw_ref)

        r, n = _rms(h_ref[...])
        wv = w_ref[...]
        err = n * wv - t_ref[...]
        loss_ref[...] += jnp.sum(err * err, keepdims=True) * (0.5 / D)
        dy = err * (1.0 / D)
        dfw_ref[...] += _colsum(dy * n)
        dh_ref[...] = _rms_bwd(r, n, dy * wv)

    blk = pl.BlockSpec((tb, D), lambda i: (i, 0))
    return pl.pallas_call(
        body, name="b_final", grid=(t // tb,), in_specs=[blk, blk, _row(D)], out_specs=[blk, _row(1), _row(D)],
        out_shape=[jax.ShapeDtypeStruct((t, D), F32), jax.ShapeDtypeStruct((1, 1), F32), jax.ShapeDtypeStruct((1, D), F32)],
        compiler_params=_cparams(1),
    )(h, tgt, fw)


def _eye(n):
    return jnp.eye(n, dtype=F32)


def _pool_embed(pool_w):
    return jnp.einsum('gcd,gk->gckd', pool_w, _eye(4)).reshape(GW, GW)


def _pool_extract(m):
    return jnp.einsum('gcgd->gcd', m.reshape(4, 64, 4, 64))


def _bmat_embed(bb):
    return jnp.einsum('gph,gk->ghkp', bb, _eye(16)).reshape(GW, S5_P)


def _bmat_extract(m):
    return jnp.einsum('ghgp->gph', m.reshape(16, 16, 16, 64))


def _cmat_embed(cc):
    return jnp.einsum('ghp,gk->kpgh', cc, _eye(16)).reshape(S5_P, GW)


def _cmat_extract(m):
    return jnp.einsum('gpgh->ghp', m.reshape(16, 64, 16, 16))


def _pad_lanes(v, n=DTW):
    return jnp.pad(v.reshape(1, -1), ((0, 0), (0, n - v.shape[-1])))


def _w_in_layout(w_in_t):
    w_main = jnp.concatenate([w_in_t[:1280], w_in_t[2052:2308], w_in_t[1280:2048]], axis=0)
    return w_main, jnp.pad(w_in_t[2048:2052], ((0, DTW - 4), (0, 0)))


def _layer_params(p, l, mod, w_in, rest):
    q = {'rest': rest}
    q['mod'] = [mod[k:k + 1] for k in range(6)]
    q['nw1'] = p['norm_mix_w'][l:l + 1]
    q['nw2'] = p['norm_mlp_w'][l:l + 1]
    q['w_main'], q['w_dt'] = _w_in_layout(w_in)
    q['pool_mat'] = _pool_embed(p['pool_w'][l]).astype(BF16)
    q['pool_scale'] = p['pool_scale'][l:l + 1]
    q['sconv_w'] = p['sconv_w'][l]
    q['conv_w'] = p['ssd_conv_w'][l]
    q['conv_b'] = p['ssd_conv_b'][l:l + 1]
    q['dt_bias'] = _pad_lanes(p['ssd_dt_bias'][l])
    q['a_log'] = _pad_lanes(p['ssd_a_log'][l])
    q['ssd_d'] = _pad_lanes(p['ssd_d'][l])
    q['s5_raw'] = (p['s5_a_re'][l], p['s5_a_im'][l], p['s5_log_step'][l].reshape(16, 1),
                   p['s5_b_re'][l].reshape(16, 1024), p['s5_b_im'][l].reshape(16, 1024))
    q['cre'] = _cmat_embed(p['s5_c_re'][l]).astype(BF16)
    q['cim'] = (-_cmat_embed(p['s5_c_im'][l])).astype(BF16)
    q['s5_d'] = p['s5_d'][l:l + 1]
    q['glu_w'] = p['s5_glu_w'][l].astype(BF16)
    q['glu_b'] = p['s5_glu_b'][l:l + 1]
    q['bw'] = p['branch_norm_w'][l:l + 1]
    return q


def _layer_fwd(h, q):
    sh1, sc1, g1, sh2, sc2, g2 = q['mod']
    t = h.shape[0]
    s = {'h': h}
    s['proj'], s['dtp'], s['u'] = _f_in(h, q['nw1'], sc1, sh1, q['w_main'], q['w_dt'])
    s['ya'], s['yb'] = _f_ab(s['proj'], q['pool_mat'], q['pool_scale'], q['sconv_w'])
    s['yc'], s['ypre'], s['sprev'] = _f_ssd(s['proj'], s['dtp'], q['conv_w'], q['conv_b'], q['dt_bias'], q['a_log'], q['ssd_d'])
    lr, li, bbr, bbi, ars, ais = _s5_prep(*q['s5_raw'])
    s['bmat'] = jnp.concatenate([_bmat_embed(bbr.reshape(16, 64, 16)), _bmat_embed(bbi.reshape(16, 64, 16))],
                                axis=1).astype(BF16)
    s['tables'] = _s5_tables(ars.reshape(1, S5_P), ais.reshape(1, S5_P))
    s['yd'], s['carries'], s['states'] = _f_s5(s['proj'], s['bmat'], q['cre'], q['cim'], s['tables'][0], s['tables'][1],
                                  q['s5_d'], q['glu_w'], q['glu_b'])
    q['w_out'], q['w1'], q['w2'] = q['rest']((s['ya'], s['yc'], s['yd']))
    s['h2'], s['o'], s['cat'] = _f_out(s['ya'], s['yb'], s['yc'], s['yd'], q['bw'], q['w_out'], h, g1)
    h3, s['m'], s['a'], s['v'] = _f_mlp(s['h2'], q['nw2'], sc2, sh2, g2, q['w1'], q['w2'])
    return h3, s


STACKED = {'mlp_w1': (2, 4, D, HID // 4), 'mlp_w2': (2, HID, D), 'w_out': (2, D, D)}


def _layer_bwd(dh3, q, s, l, stacked, early=None):
    sh1, sc1, g1, sh2, sc2, g2 = q['mod']
    g = {}
    dv, da, act, dm = _b_mlp(dh3, s['a'], g2, q['w1'], q['w2'])
    g['mlp_w1'] = _tn_matmul(s['v'], da, "dw1", col_major=True, into=stacked['mlp_w1'], layer=l)
    g['mlp_w2'] = _tn_matmul(act, dm, "dw2", into=stacked['mlp_w2'], layer=l)
    dh2, dsc2, dsh2, dnw2, dg2 = _b_normmod(dv, s['h2'], dh3, s['m'], q['nw2'], sc2, "b_norm_mlp")
    dya, dyb, dyc, dyd, do, dbw = _b_out(dh2, s['ya'], s['yb'], s['yc'], s['yd'], q['bw'], q['w_out'], g1)
    g['w_out'] = _tn_matmul(s['cat'], do, "dwout", into=stacked['w_out'], layer=l)
    g['branch_norm_w'] = dbw[0]
    if early is not None:
        zero = early(g)[0, 0]
        q = dict(q, pool_scale=q['pool_scale'] + zero, conv_b=q['conv_b'] + zero, s5_d=q['s5_d'] + zero)
    dab, dpm, dps, dsw = _b_ab(s['proj'], dya, dyb, q['pool_mat'], q['pool_scale'], q['sconv_w'])
    g['pool_w'] = _pool_extract(dpm)
    g['pool_scale'] = dps[0]
    g['sconv_w'] = dsw
    dz, dxbc, ddt, dcw, dcb, ddtb, dal, ddk = _b_ssd(s['proj'], s['dtp'], s['ypre'], dyc, s['sprev'], q['conv_w'],
                                                     q['conv_b'], q['dt_bias'], q['a_log'], q['ssd_d'])
    g['ssd_conv_w'] = dcw
    g['ssd_conv_b'] = dcb[0]
    g['ssd_dt_bias'] = ddtb[0, :4]
    g['ssd_a_log'] = dal[0, :4]
    g['ssd_d'] = ddk[0, :4]
    tb = s['tables']
    ds5, dbmat, dcre, dcim, dlam, dd5, dgw, dgb = _b_s5(s['proj'], dyd, s['carries'], s['states'], s['bmat'], q['cre'], q['cim'],
                                                        tb[0], tb[1], q['s5_d'], q['glu_w'], q['glu_b'])
    g['s5_c_re'] = _cmat_extract(dcre)
    g['s5_c_im'] = -_cmat_extract(dcim)
    g['s5_d'] = dd5[0]
    g['s5_glu_w'] = dgw
    g['s5_glu_b'] = dgb[0]
    dbbr = _bmat_extract(dbmat[:, :S5_P]).reshape(16, 1024)
    dbbi = _bmat_extract(dbmat[:, S5_P:]).reshape(16, 1024)
    dar, dai, dls, dbr, dbi = _s5_prep_bwd(*q['s5_raw'], dlam[0].reshape(16, 64), dlam[1].reshape(16, 64), dbbr, dbbi)
    g['s5_a_re'], g['s5_a_im'], g['s5_log_step'] = dar, dai, dls[:, 0]
    g['s5_b_re'], g['s5_b_im'] = dbr, dbi
    dh, dsc1, dsh1, dnw1, dg1 = _b_in(dab, dz, dxbc, ds5, ddt, q['w_main'], q['w_dt'], s['h'], dh2, s['o'], q['nw1'], sc1)
    u = s['u']
    head = jnp.concatenate([_tn_matmul(dab, u, "dwin_ab"), _tn_matmul(dz, u, "dwin_z"), _tn_matmul(dxbc, u, "dwin_xbc"),
                            _tn_matmul(ddt, u, "dwin_dt")[:8]], axis=0)
    full = lax.dynamic_update_slice(jnp.zeros((2308, D), F32), head, (0, 0))
    g['w_in'] = lax.dynamic_update_slice(full, _tn_matmul(ds5, u, "dwin_s5"), (2052, 0))
    g['norm_mix_w'] = dnw1[0]
    g['norm_mlp_w'] = dnw2[0]
    dmod = jnp.concatenate([dsh1, dsc1, dg1, dsh2, dsc2, dg2], axis=1)
    return dh, g, dmod


def _local_step(x, tgt, p, mod, w_in_of, rest_of, early=None):
    h = x
    qs, saved = [], []
    for l in range(2):
        qs.append(_layer_params(p, l, mod[l], w_in_of(l), functools.partial(rest_of, l)))
        h, s = _layer_fwd(h, qs[l])
        saved.append(s)
    dh, loss, dfw = _b_final(h, tgt, p['final_norm_w'].reshape(1, D))
    grads = [None, None]
    dmods = [None, None]
    dh, grads[1], dmods[1] = _layer_bwd(dh, qs[1], saved[1], 1, {k: lax.empty(shp, F32) for k, shp in STACKED.items()})
    dh, grads[0], dmods[0] = _layer_bwd(dh, qs[0], saved[0], 0, grads[1], early)
    out = {k: jnp.stack([grads[0][k], grads[1][k]]) for k in grads[0] if k not in STACKED}
    if early is None:
        out.update({k: grads[0][k] for k in STACKED})
    out['final_norm_w'] = dfw[0]
    return loss, dh, out, jnp.concatenate(dmods, axis=0)


def _shard_of(a, axis, k):
    n = a.shape[axis] // 4
    return lax.dynamic_slice_in_dim(a, k * n, n, axis)


def kernel(x, c, norm_mix_w, norm_mlp_w, ada_w, ada_b, w_in, pool_w, pool_scale, sconv_w, ssd_conv_w, ssd_conv_b, ssd_dt_bias, ssd_a_log, ssd_d, s5_a_re, s5_a_im, s5_log_step, s5_b_re, s5_b_im, s5_c_re, s5_c_im, s5_d, s5_glu_w, s5_glu_b, branch_norm_w, w_out, mlp_w1, mlp_w2, final_norm_w, loss_target, m_norm_mix_w, m_norm_mlp_w, m_ada_w, m_ada_b, m_w_in, m_pool_w, m_pool_scale, m_sconv_w, m_ssd_conv_w, m_ssd_conv_b, m_ssd_dt_bias, m_ssd_a_log, m_ssd_d, m_s5_a_re, m_s5_a_im, m_s5_log_step, m_s5_b_re, m_s5_b_im, m_s5_c_re, m_s5_c_im, m_s5_d, m_s5_glu_w, m_s5_glu_b, m_branch_norm_w, m_w_out, m_mlp_w1, m_mlp_w2, m_final_norm_w, v_norm_mix_w, v_norm_mlp_w, v_ada_w, v_ada_b, v_w_in, v_pool_w, v_pool_scale, v_sconv_w, v_ssd_conv_w, v_ssd_conv_b, v_ssd_dt_bias, v_ssd_a_log, v_ssd_d, v_s5_a_re, v_s5_a_im, v_s5_log_step, v_s5_b_re, v_s5_b_im, v_s5_c_re, v_s5_c_im, v_s5_d, v_s5_glu_w, v_s5_glu_b, v_branch_norm_w, v_w_out, v_mlp_w1, v_mlp_w2, v_final_norm_w):
    loc = locals()
    w = {n: loc[n] for n in WEIGHTS}
    mom = {n: loc['m_' + n] for n in WEIGHTS}
    var = {n: loc['v_' + n] for n in WEIGHTS}
    ix, iy, ic = lax.axis_index("x"), lax.axis_index("y"), lax.axis_index("c")
    chip = 2 * ix + iy
    dev = 4 * ix + 2 * iy + ic

    mine_of = lambda a: lax.dynamic_index_in_dim(a.astype(BF16), ic, axis=0, keepdims=False)
    pad_in = lambda a: jnp.pad(a.T, ((0, WIN_ROWS - 577), (0, 0)))
    shard = jnp.concatenate([pad_in(mine_of(w['w_in'])), mine_of(w['w_out']), mine_of(w['mlp_w1']), mine_of(w['mlp_w2'])], axis=0)

    (c_all,) = _exchange([c], EVERYONE, False, "ag_cond", stage=True)
    c_all = c_all.reshape(8, D)
    small_sh = _exchange([w[n] for n in SMALL_SHARDED], CHIPS, False, "ag_small")
    (w_in0,) = _exchange([pad_in(w['w_in'][0].astype(BF16))], CHIPS, False, "ag_win0")
    p = {n: w[n] for n in WEIGHTS if n not in BIG}
    for n, g in zip(SMALL_SHARDED, small_sh):
        ax = SMALL_SHARDED[n]
        p[n] = jnp.concatenate([g[k] for k in range(4)], axis=ax)

    def w_in_full(sh):
        return sh[:, :577].reshape(4 * 577, D)

    big = {}

    def fetch(after):
        if not big:
            (mine,), (got,) = _split_wait(sems, shard_thru, land, after, False, "ag_big_wait", per_core=True)
            got = lax.dynamic_update_slice(got, mine[None, None], (ic, chip, 0, 0))
            (both,) = _pair_swap([got.reshape(2, -1, D)], False, "swap_big", fill=True)
            big['both'] = both.reshape(got.shape)
        return big['both']

    def w_in_of(l):
        return w_in_full(w_in0) if l == 0 else w_in_full(fetch(None)[1])

    def rest_of(l, after):
        blk = fetch(after)[l]
        r0 = WIN_ROWS
        w_out_l = blk[:, r0:r0 + 256].reshape(D, D)
        w1_l = blk[:, r0 + 256:r0 + 1280]
        w2_l = blk[:, r0 + 1280:r0 + 2304].reshape(HID, D)
        return w_out_l, w1_l, w2_l

    ada_b_sh = _shard_of(w['ada_b'], 1, chip).reshape(2, 1, 6 * D // 4)
    mod_sh = _ada_fwd(c_all, w['ada_w'], ada_b_sh)
    (mod_all,) = _exchange([mod_sh], CHIPS, False, "ag_mod", stage=True)
    mine = lax.dynamic_index_in_dim(mod_all, dev, axis=2, keepdims=False)
    sems, shard_thru, land, token = _split_start([shard], [mod_all, w_in0] + small_sh, False, "ag_big_start", per_core=True)
    mod = jnp.transpose(mine, (1, 0, 2)).reshape(2, 6, D) + token[0, 0]

    layer = ic.astype(jnp.int32).reshape(1)
    flight = {}

    def early(g0):
        gws = [g0['w_out'].reshape(2, 4, 256, D), g0['mlp_w1'], g0['mlp_w2'].reshape(2, 4, 1024, D)]
        got = _pair_swap([a.reshape(2, -1, D) for a in gws], True, "swap_grad", narrow=True)
        pair = [_pair_sum(a, b.reshape(a.shape[1:]), layer, "pair_sum%d" % (k + 1), BF16) for k, (a, b) in enumerate(zip(gws, got))]
        flight['sems'], flight['srcs'], flight['lands'], token = _split_start(pair, [], True, "rs_start")
        return token

    loss, grad_x, g, dmod = _local_step(x[0], loss_target[0], p, mod, w_in_of, rest_of, early)

    (dmod_all,) = _exchange([dmod], EVERYONE, False, "ag_dmod", stage=True)
    dmod_all = jnp.transpose(dmod_all, (1, 0, 2))
    g_ada_w, g_ada_b = _ada_bwd(c_all, _shard_of(dmod_all, 2, chip), dmod_all)

    sent, lands = _split_wait(flight['sems'], flight['srcs'], flight['lands'], [grad_x, g['w_in']], True, "rs_wait")
    quad = []
    for k, (land, mine) in enumerate(zip(lands, sent)):
        own = lax.dynamic_index_in_dim(mine, chip, axis=0, keepdims=True)
        quad.append(_sum_lead(lax.dynamic_update_slice(land, own, (chip, 0, 0)), "rs_chip_sum%d" % (k + 1), F32))
    gw_in = jnp.pad(g['w_in'].reshape(2, 4, 577, D), ((0, 0), (0, 0), (0, WIN_ROWS - 577), (0, 0)))
    (got_in,) = _pair_swap([gw_in.reshape(2, -1, D)], True, "swap_grad_in", narrow=True)
    pair_in = _pair_sum(gw_in, got_in.reshape(gw_in.shape[1:]), layer, "pair_sum0", BF16)
    (quad_in,) = _exchange([pair_in], CHIPS, True, "rs_chips")
    quad = [_sum_lead(quad_in, "rs_chip_sum0", F32)] + quad
    halves = [lax.dynamic_update_slice(lax.empty((2,) + a.shape, F32), a[None], (ic, 0, 0)) for a in quad]
    both = _pair_swap(halves, False, "swap_red", fill=True)
    both[0] = jnp.transpose(both[0][:, :577], (0, 2, 1))
    red = dict(zip(('w_in', 'w_out', 'mlp_w1', 'mlp_w2'), both))
    red['ada_w'] = g_ada_w

    small_names = [n for n in WEIGHTS if n not in BIG and n != 'ada_b']
    pair_parts = _exchange([g[n] for n in small_names] + [loss], SIBLING, False, "ag_smallpair", stage=True)
    chip_parts = _exchange(_sum_many(pair_parts, "smallpair_sum"), CHIPS, False, "ag_smallgrad", stage=True)
    summed = _sum_many(chip_parts, "smallgrad_sum")
    for n, a in zip(small_names, summed[:-1]):
        a = a.reshape(w[n].shape) if n in ('s5_b_re', 's5_b_im') else a
        red[n] = _shard_of(a, SMALL_SHARDED[n], chip) if n in SMALL_SHARDED else a
    red['ada_b'] = g_ada_b
    loss_out = summed[-1].reshape(())

    delta, new_m, new_v = {}, {}, {}
    for n in BIG:
        delta[n], new_m[n], new_v[n] = _adamw(w[n], red[n], mom[n], var[n], "adamw_" + n)
    rest = [n for n in WEIGHTS if n not in BIG]
    lanes = lambda n, a: a.reshape(2, 16, 1024) if n in ('s5_b_re', 's5_b_im') else a
    outs = _adamw_many(*[[lanes(n, src[n]) for n in rest] for src in (w, red, mom, var)], "adamw_small")
    for k, n in enumerate(rest):
        delta[n], new_m[n], new_v[n] = (outs[3 * k + j].reshape(w[n].shape) for j in range(3))

    return (loss_out, grad_x[None], *[red[n] for n in WEIGHTS], *[delta[n] for n in WEIGHTS],
            *[new_m[n] for n in WEIGHTS], *[new_v[n] for n in WEIGHTS])
```

```python
import functools
import math

import jax
import jax.numpy as jnp
from jax import lax
from jax.experimental import pallas as pl
from jax.experimental.pallas import tpu as pltpu

F32 = jnp.float32
BF16 = jnp.bfloat16
HI = lax.Precision.HIGHEST

D = 1024
GW = 256
HID = 4096
EPS = 1e-6
PW = 2304
DTW = 128
SSD_L = 128
SSD_SUB = 2
SSD_SUB_BWD = 2
NH, HP, NS = 4, 64, 128
S5_P = 1024
MESH = pl.DeviceIdType.MESH

ADAM_LR, ADAM_B1, ADAM_B2, ADAM_EPS, ADAM_WD, ADAM_STEP = 0.001, 0.9, 0.999, 1e-08, 0.01, 10

NT = (((1,), (1,)), ((), ()))
TN = (((0,), (0,)), ((), ()))

WEIGHTS = ['norm_mix_w', 'norm_mlp_w', 'ada_w', 'ada_b', 'w_in', 'pool_w', 'pool_scale', 'sconv_w', 'ssd_conv_w',
           'ssd_conv_b', 'ssd_dt_bias', 'ssd_a_log', 'ssd_d', 's5_a_re', 's5_a_im', 's5_log_step', 's5_b_re', 's5_b_im',
           's5_c_re', 's5_c_im', 's5_d', 's5_glu_w', 's5_glu_b', 'branch_norm_w', 'w_out', 'mlp_w1', 'mlp_w2',
           'final_norm_w']
BIG = ('ada_w', 'w_in', 'w_out', 'mlp_w1', 'mlp_w2')
SMALL_SHARDED = {'sconv_w': 2, 'ssd_conv_w': 2, 's5_glu_w': 1}


def _cparams(n_axes, vmem_mb=48):
    return pltpu.CompilerParams(dimension_semantics=("arbitrary",) * n_axes, vmem_limit_bytes=vmem_mb * 1024 * 1024)


def _row(n):
    return pl.BlockSpec((1, n), lambda *_: (0, 0))


def _full(shape):
    nd = len(shape)
    return pl.BlockSpec(tuple(shape), lambda *_: (0,) * nd)


def _dot(a, b, dims=None, prec=None):
    if dims is None:
        dims = (((a.ndim - 1,), (0,)), ((), ()))
    return lax.dot_general(a, b, dims, preferred_element_type=F32, precision=prec)


def _bdot(a, b, dims=None):
    return _dot(a.astype(BF16), b.astype(BF16), dims)


def _sig(x):
    return jax.nn.sigmoid(x)


def _silu(x):
    return x * _sig(x)


def _dsilu(x):
    s = _sig(x)
    return s * (1.0 + x * (1.0 - s))


def _softplus(x):
    return jnp.maximum(x, 0.0) + jnp.log(1.0 + jnp.exp(-jnp.abs(x)))


_GK = math.sqrt(2.0 / math.pi)


def _gelu(x):
    return 0.5 * x * (1.0 + jnp.tanh(_GK * (x + 0.044715 * x * x * x)))


def _dgelu(x):
    th = jnp.tanh(_GK * (x + 0.044715 * x * x * x))
    return 0.5 * (1.0 + th) + 0.5 * x * (1.0 - th * th) * _GK * (1.0 + 3.0 * 0.044715 * x * x)


def _colsum(x):
    return jnp.sum(x, axis=0, keepdims=True)


def _rms(x):
    r = lax.rsqrt(jnp.mean(x * x, axis=-1, keepdims=True) + EPS)
    return r, x * r


def _rms_bwd(r, n, dn):
    return r * (dn - n * jnp.mean(dn * n, axis=-1, keepdims=True))


def _roll(x, k):
    n = x.shape[0]
    k = k % n
    return x if k == 0 else pltpu.roll(x, k, axis=0)


def _tblock(t, want=512):
    return min(t, want)


def _peer(mask):
    x, y, c = lax.axis_index("x"), lax.axis_index("y"), lax.axis_index("c")
    return (x ^ ((mask >> 2) & 1), y ^ ((mask >> 1) & 1), c ^ (mask & 1))


def _group_index(masks):
    x, y, c = lax.axis_index("x"), lax.axis_index("y"), lax.axis_index("c")
    full = 0
    for m in masks:
        full |= m
    bits = [b for b in (4, 2, 1) if full & b]

    def idx(px, py, pc):
        v = {4: px, 2: py, 1: pc}
        out = 0
        for b in bits:
            out = out * 2 + v[b]
        return out

    return idx(x, y, c), [idx(*_peer(m)) for m in masks]


def _exchange(arrs, masks, scatter, name, stage=False):
    n_arr, n_peer, n_grp = len(arrs), len(masks), len(masks) + 1

    def body(*refs):
        ins, outs = refs[:n_arr], refs[n_arr:2 * n_arr]
        send_sems, recv_sems, local_sems = refs[2 * n_arr:2 * n_arr + 3]
        if stage:
            bufs, load_sems = refs[2 * n_arr + 3:3 * n_arr + 3], refs[3 * n_arr + 3]
            loads = [pltpu.make_async_copy(ins[t], bufs[t], load_sems.at[t]) for t in range(n_arr)]
            for ld in loads:
                ld.start()
            for ld in loads:
                ld.wait()
            ins = bufs
        me, peer_idx = _group_index(masks)
        copies = []
        for t in range(n_arr):
            src_me = ins[t].at[me] if scatter else ins[t]
            loc = pltpu.make_async_copy(src_me, outs[t].at[me], local_sems.at[t])
            loc.start()
            copies.append(loc)
            for j, m in enumerate(masks):
                src = ins[t].at[peer_idx[j]] if scatter else ins[t]
                cp = pltpu.make_async_remote_copy(src_ref=src, dst_ref=outs[t].at[me], send_sem=send_sems.at[t, j],
                                                  recv_sem=recv_sems.at[t, j], device_id=_peer(m), device_id_type=MESH)
                cp.start()
                copies.append(cp)
        for cp in copies:
            cp.wait()

    hbm = pl.BlockSpec(memory_space=pl.ANY)
    out_shape = [jax.ShapeDtypeStruct((n_grp,) + (a.shape[1:] if scatter else a.shape), a.dtype) for a in arrs]
    staging = [pltpu.VMEM(a.shape, a.dtype) for a in arrs] + [pltpu.SemaphoreType.DMA((n_arr,))] if stage else []
    outs = pl.pallas_call(
        body, name=name, in_specs=[hbm] * n_arr, out_specs=[hbm] * n_arr, out_shape=out_shape,
        scratch_shapes=[pltpu.SemaphoreType.DMA((n_arr, n_peer)), pltpu.SemaphoreType.DMA((n_arr, n_peer)),
                        pltpu.SemaphoreType.DMA((n_arr,))] + staging,
        compiler_params=pltpu.CompilerParams(vmem_limit_bytes=48 * 1024 * 1024),
    )(*arrs)
    return list(outs)


def _split_copies(src_refs, land_refs, sems, scatter, per_core):
    me, peer_idx = _group_index(CHIPS)
    n = len(CHIPS) * len(src_refs)
    copies = []
    for t, (src_ref, land_ref) in enumerate(zip(src_refs, land_refs)):
        zone = land_ref.at[lax.axis_index("c")] if per_core else land_ref
        for j, m in enumerate(CHIPS):
            k = len(CHIPS) * t + j
            copies.append(pltpu.make_async_remote_copy(
                src_ref=src_ref.at[peer_idx[j]] if scatter else src_ref, dst_ref=zone.at[me], send_sem=sems[k],
                recv_sem=sems[n + k], device_id=_peer(m), device_id_type=MESH))
    return copies


def _split_start(srcs, after, scatter, name, per_core=False):
    n_arr, n_sem = len(srcs), 2 * len(CHIPS) * len(srcs)

    def body(*refs):
        src_refs, land_refs = refs[:n_arr], refs[n_arr:2 * n_arr]
        outs = refs[2 * n_arr + len(after):]
        for cp in _split_copies(src_refs, land_refs, outs[:n_sem], scatter, per_core):
            cp.start()
        outs[-1][...] = jnp.zeros_like(outs[-1])

    hbm = pl.BlockSpec(memory_space=pltpu.HBM)
    sem = pl.BlockSpec(memory_space=pltpu.SEMAPHORE)
    lands = [lax.empty(((2,) if per_core else ()) + (len(CHIPS) + 1,) + (a.shape[1:] if scatter else a.shape), a.dtype)
             for a in srcs]
    as_hbm = lambda a: pltpu.with_memory_space_constraint(a, pltpu.HBM)
    outs = pl.pallas_call(
        body, name=name,
        out_shape=(pltpu.SemaphoreType.DMA(()),) * n_sem + tuple(pltpu.HBM(a.shape, a.dtype) for a in srcs + lands)
        + (jax.ShapeDtypeStruct((8, 128), F32),),
        in_specs=(hbm,) * (2 * n_arr) + (pl.BlockSpec(memory_space=pl.ANY),) * len(after),
        out_specs=(sem,) * n_sem + (hbm,) * (2 * n_arr) + (pl.BlockSpec(memory_space=pltpu.VMEM),),
        input_output_aliases={t: n_sem + t for t in range(2 * n_arr)},
        compiler_params=pltpu.CompilerParams(has_side_effects=pltpu.SideEffectType.DATAFLOW_SIDE_EFFECTING),
    )(*[as_hbm(a) for a in srcs + lands], *after)
    return outs[:n_sem], list(outs[n_sem:n_sem + n_arr]), list(outs[n_sem + n_arr:n_sem + 2 * n_arr]), outs[-1]


def _split_wait(sems, srcs, lands, after, scatter, name, per_core=False):
    n_arr, n_sem = len(srcs), len(sems)

    def body(*refs):
        src_refs, land_refs = refs[:n_arr], refs[n_arr:2 * n_arr]
        for cp in _split_copies(src_refs, land_refs, refs[2 * n_arr:2 * n_arr + n_sem], scatter, per_core):
            cp.wait_send()
            cp.wait_recv()

    hbm = pl.BlockSpec(memory_space=pltpu.HBM)
    sem = pl.BlockSpec(memory_space=pltpu.SEMAPHORE)
    outs = pl.pallas_call(
        body, name=name, out_shape=tuple(pltpu.HBM(a.shape, a.dtype) for a in srcs + lands),
        in_specs=(hbm,) * (2 * n_arr) + (sem,) * n_sem + (pl.BlockSpec(memory_space=pl.ANY),) * len(after),
        out_specs=(hbm,) * (2 * n_arr), input_output_aliases={t: t for t in range(2 * n_arr)},
        compiler_params=pltpu.CompilerParams(has_side_effects=pltpu.SideEffectType.DATAFLOW_SIDE_EFFECTING),
    )(*srcs, *lands, *sems, *after)
    return list(outs[:n_arr]), list(outs[n_arr:])


CHIPS = (4, 2, 6)
EVERYONE = (1, 2, 3, 4, 5, 6, 7)
SIBLING = (1,)
SWAP_ROWS = 512
WIN_ROWS = 592


def _pair_swap(arrs, other_layer, name, narrow=False, fill=False):
    assert not (fill and (other_layer or narrow))
    n_arr = len(arrs)
    shapes = [a.shape[-2:] for a in arrs]
    out_dtypes = [BF16 if narrow else a.dtype for a in arrs]
    chunks = []
    for t, (rows, _) in enumerate(shapes):
        assert rows % 16 == 0
        for j, r0 in enumerate(range(0, rows, SWAP_ROWS)):
            chunks.append((t, r0, min(SWAP_ROWS, rows - r0), j % 2))

    def body(*refs):
        ins, outs = refs[:n_arr], refs[n_arr:2 * n_arr]
        bufs = refs[2 * n_arr:3 * n_arr]
        out_bufs = refs[3 * n_arr:4 * n_arr] if narrow else bufs
        load_sems, send_sems, recv_sems = refs[-3:]
        sibling = _peer(1)
        c = lax.axis_index("c")

        def load(k):
            t, r0, n, slot = chunks[k]
            src = ins[t].at[1 - c] if other_layer else ins[t].at[c] if fill else ins[t]
            return pltpu.make_async_copy(src.at[pl.ds(r0, n)], bufs[t].at[slot, pl.ds(0, n)], load_sems.at[t, slot])

        def send(k):
            t, r0, n, slot = chunks[k]
            dst = outs[t].at[c] if fill else outs[t]
            return pltpu.make_async_remote_copy(src_ref=out_bufs[t].at[slot, pl.ds(0, n)], dst_ref=dst.at[pl.ds(r0, n)],
                                                send_sem=send_sems.at[t, slot], recv_sem=recv_sems.at[t],
                                                device_id=sibling, device_id_type=MESH)

        in_flight = {}

        def drain(k):
            key = (chunks[k][0], chunks[k][3])
            if key in in_flight:
                send(in_flight.pop(key)).wait_send()

        def start_load(k):
            if not narrow:
                drain(k)
            load(k).start()

        start_load(0)
        for k in range(len(chunks)):
            t, _, n, slot = chunks[k]
            load(k).wait()
            if k + 1 < len(chunks):
                start_load(k + 1)
            if narrow:
                drain(k)
                out_bufs[t][slot, pl.ds(0, n), :] = bufs[t][slot, pl.ds(0, n), :].astype(BF16)
            send(k).start()
            in_flight[(t, slot)] = k
        for k in in_flight.values():
            send(k).wait_send()
        for t in range(n_arr):
            landed = outs[t].at[1 - c] if fill else outs[t]
            pltpu.make_async_remote_copy(src_ref=landed, dst_ref=landed, send_sem=send_sems.at[t, 0],
                                         recv_sem=recv_sems.at[t], device_id=sibling, device_id_type=MESH).wait_recv()

    hbm = pl.BlockSpec(memory_space=pl.ANY)
    outs = pl.pallas_call(
        body, name=name, in_specs=[hbm] * n_arr, out_specs=[hbm] * n_arr,
        out_shape=[jax.ShapeDtypeStruct(a.shape if fill else s, dt) for a, s, dt in zip(arrs, shapes, out_dtypes)],
        input_output_aliases={t: t for t in range(n_arr)} if fill else {},
        scratch_shapes=[pltpu.VMEM((2, min(SWAP_ROWS, s[0]), s[1]), a.dtype) for s, a in zip(shapes, arrs)]
        + ([pltpu.VMEM((2, min(SWAP_ROWS, s[0]), s[1]), BF16) for s in shapes] if narrow else [])
        + [pltpu.SemaphoreType.DMA((n_arr, 2)), pltpu.SemaphoreType.DMA((n_arr, 2)), pltpu.SemaphoreType.DMA((n_arr,))],
        compiler_params=pltpu.CompilerParams(vmem_limit_bytes=48 * 1024 * 1024),
    )(*arrs)
    return list(outs)


def _sum_lead(a, name, out_dtype):
    n = a.shape[0]
    shape = a.shape[1:]

    def body(a_ref, o_ref):
        acc = a_ref[0].astype(F32)
        for k in range(1, n):
            acc = acc + a_ref[k].astype(F32)
        o_ref[...] = acc.astype(out_dtype)

    if len(shape) == 3:
        blk = (1,) + shape[1:]
        return pl.pallas_call(
            body, name=name, grid=(shape[0],), in_specs=[pl.BlockSpec((n,) + blk, lambda i: (0, i, 0, 0))],
            out_specs=pl.BlockSpec(blk, lambda i: (i, 0, 0)), out_shape=jax.ShapeDtypeStruct(shape, out_dtype),
            compiler_params=_cparams(1),
        )(a)
    rows, cols = shape
    rb = rows
    for cand in (512, 256, 128):
        if rows % cand == 0 and rows > cand:
            rb = cand
            break
    return pl.pallas_call(
        body, name=name, grid=(rows // rb,), in_specs=[pl.BlockSpec((n, rb, cols), lambda i: (0, i, 0))],
        out_specs=pl.BlockSpec((rb, cols), lambda i: (i, 0)), out_shape=jax.ShapeDtypeStruct((rows, cols), out_dtype),
        compiler_params=_cparams(1),
    )(a)


def _pair_sum(g, recv, layer, name, out_dtype):
    _, n, r, c = g.shape

    def body(l_ref, g_ref, r_ref, o_ref):
        o_ref[...] = (g_ref[0].astype(F32) + r_ref[...].astype(F32)).astype(out_dtype)

    return pl.pallas_call(
        body, name=name,
        grid_spec=pltpu.PrefetchScalarGridSpec(
            num_scalar_prefetch=1, grid=(n,),
            in_specs=[pl.BlockSpec((1, 1, r, c), lambda i, l: (l[0], i, 0, 0)), pl.BlockSpec((1, r, c), lambda i, l: (i, 0, 0))],
            out_specs=pl.BlockSpec((1, r, c), lambda i, l: (i, 0, 0))),
        out_shape=jax.ShapeDtypeStruct((n, r, c), out_dtype), compiler_params=_cparams(1),
    )(layer, g, recv)


def _tn_matmul(a, b, name, col_major=False, into=None, layer=0):
    t, k = a.shape
    n = b.shape[1]
    tb = _tblock(t, 1024)
    kb = min(k, 1024)
    nb = min(n, 1024)
    grid = (k // kb, n // nb, t // tb)
    lead = (into is not None) + col_major

    def body(a_ref, b_ref, *rest):
        o_ref = rest[-1]
        for _ in range(lead):
            o_ref = o_ref.at[0]

        @pl.when(pl.program_id(2) == 0)
        def _():
            o_ref[...] = jnp.zeros_like(o_ref)

        o_ref[...] += _bdot(a_ref[...], b_ref[...], TN)

    if col_major:
        block, index, shape = (1, kb, nb), (lambda ki, ni: (ni, ki, 0)), (n // nb, k, nb)
    else:
        block, index, shape = (kb, nb), (lambda ki, ni: (ki, ni)), (k, n)
    in_specs = [pl.BlockSpec((tb, kb), lambda ki, ni, ti: (ti, ki)), pl.BlockSpec((tb, nb), lambda ki, ni, ti: (ti, ni))]
    if into is None:
        return pl.pallas_call(
            body, name=name, grid=grid, in_specs=in_specs, out_specs=pl.BlockSpec(block, lambda ki, ni, ti: index(ki, ni)),
            out_shape=jax.ShapeDtypeStruct(shape, F32), compiler_params=_cparams(3),
        )(a, b)
    assert into.shape == (2,) + shape
    return pl.pallas_call(
        body, name=name, grid=grid, in_specs=in_specs + [pl.BlockSpec(memory_space=pl.ANY)],
        out_specs=pl.BlockSpec((1,) + block, lambda ki, ni, ti: (layer,) + index(ki, ni)),
        out_shape=jax.ShapeDtypeStruct(into.shape, F32), input_output_aliases={2: 0}, compiler_params=_cparams(3),
    )(a, b, into)


def _sum_many(arrs, name):
    k = len(arrs)

    def body(*refs):
        for a_ref, o_ref in zip(refs[:k], refs[k:]):
            acc = a_ref[0]
            for j in range(1, a_ref.shape[0]):
                acc = acc + a_ref[j]
            o_ref[...] = acc

    return pl.pallas_call(body, name=name, grid=(1,), in_specs=[_full(a.shape) for a in arrs],
                          out_specs=[_full(a.shape[1:]) for a in arrs],
                          out_shape=[jax.ShapeDtypeStruct(a.shape[1:], F32) for a in arrs], compiler_params=_cparams(1))(*arrs)


def _adamw_math(w, g, m, v):
    m2 = ADAM_B1 * m + (1.0 - ADAM_B1) * g
    v2 = ADAM_B2 * v + (1.0 - ADAM_B2) * (g * g)
    m_hat = m2 / (1.0 - ADAM_B1 ** ADAM_STEP)
    v_hat = v2 / (1.0 - ADAM_B2 ** ADAM_STEP)
    return -ADAM_LR * (m_hat / (jnp.sqrt(v_hat) + ADAM_EPS) + ADAM_WD * w), m2, v2


def _adamw_many(ws, gs, ms, vs, name):
    n = len(ws)

    def body(*refs):
        ins, outs = refs[:4 * n], refs[4 * n:]
        for k in range(n):
            res = _adamw_math(ins[k][...], ins[n + k][...], ins[2 * n + k][...], ins[3 * n + k][...])
            for j in range(3):
                outs[3 * k + j][...] = res[j]

    out_shape = []
    for a in ws:
        out_shape += [jax.ShapeDtypeStruct(a.shape, F32)] * 3
    return pl.pallas_call(body, name=name, grid=(1,), in_specs=[_full(a.shape) for a in ws] * 4,
                          out_specs=[_full(s.shape) for s in out_shape], out_shape=out_shape,
                          compiler_params=_cparams(1))(*ws, *gs, *ms, *vs)


def _adamw(w, g, m, v, name):
    shape = w.shape
    cols = shape[-1]
    rows = int(math.prod(shape[:-1]))
    rb = rows
    for cand in (256, 128, 64, 32, 16, 8):
        if rows % cand == 0 and rows > cand:
            rb = cand
            break
    bc1 = 1.0 - ADAM_B1 ** ADAM_STEP
    bc2 = 1.0 - ADAM_B2 ** ADAM_STEP

    def body(w_ref, g_ref, m_ref, v_ref, d_ref, nm_ref, nv_ref):
        gg = g_ref[...]
        m2 = ADAM_B1 * m_ref[...] + (1.0 - ADAM_B1) * gg
        v2 = ADAM_B2 * v_ref[...] + (1.0 - ADAM_B2) * (gg * gg)
        m_hat = m2 / bc1
        v_hat = v2 / bc2
        d_ref[...] = -ADAM_LR * (m_hat / (jnp.sqrt(v_hat) + ADAM_EPS) + ADAM_WD * w_ref[...])
        nm_ref[...] = m2
        nv_ref[...] = v2

    spec = pl.BlockSpec((rb, cols), lambda i: (i, 0))
    sds = jax.ShapeDtypeStruct((rows, cols), F32)
    outs = pl.pallas_call(
        body, name=name, grid=(rows // rb,), in_specs=[spec] * 4, out_specs=[spec] * 3, out_shape=[sds] * 3,
        compiler_params=_cparams(1),
    )(*(z.reshape(rows, cols) for z in (w, g, m, v)))
    return tuple(o.reshape(shape) for o in outs)


def _ada_fwd(c_all, ada_w_sh, ada_b_sh):
    s = ada_w_sh.shape[2]
    sb = 512

    def body(c_ref, w_ref, b_ref, o_ref):
        cond = _silu(c_ref[...])
        o_ref[0] = _bdot(cond, w_ref[0]) + b_ref[0]

    return pl.pallas_call(
        body, name="ada_fwd", grid=(2, s // sb),
        in_specs=[_full((8, D)), pl.BlockSpec((1, D, sb), lambda l, j: (l, 0, j)), pl.BlockSpec((1, 1, sb), lambda l, j: (l, 0, j))],
        out_specs=pl.BlockSpec((1, 8, sb), lambda l, j: (l, 0, j)), out_shape=jax.ShapeDtypeStruct((2, 8, s), F32),
        compiler_params=_cparams(2),
    )(c_all, ada_w_sh, ada_b_sh)


def _ada_bwd(c_all, dmod_sh, dmod_all):
    s = dmod_sh.shape[2]
    sb = 512

    def body(c_ref, d_ref, o_ref):
        cond = _silu(c_ref[...])
        o_ref[0] = _bdot(cond, d_ref[0], TN)

    gw = pl.pallas_call(
        body, name="ada_bwd_w", grid=(2, s // sb),
        in_specs=[_full((8, D)), pl.BlockSpec((1, 8, sb), lambda l, j: (l, 0, j))],
        out_specs=pl.BlockSpec((1, D, sb), lambda l, j: (l, 0, j)), out_shape=jax.ShapeDtypeStruct((2, D, s), F32),
        compiler_params=_cparams(2),
    )(c_all, dmod_sh)

    def body_b(d_ref, o_ref):
        acc = d_ref[0, 0:1, :]
        for k in range(1, 8):
            acc = acc + d_ref[0, k:k + 1, :]
        o_ref[0] = acc

    gb = pl.pallas_call(
        body_b, name="ada_bwd_b", grid=(2,), in_specs=[pl.BlockSpec((1, 8, 6 * D), lambda l: (l, 0, 0))],
        out_specs=pl.BlockSpec((1, 1, 6 * D), lambda l: (l, 0, 0)), out_shape=jax.ShapeDtypeStruct((2, 1, 6 * D), F32),
        compiler_params=_cparams(1),
    )(dmod_all)
    return gw, gb.reshape(2, 6 * D)


def _f_in(h, nw, sc, sh, w_main, w_dt):
    t = h.shape[0]
    tb = _tblock(t)

    def body(h_ref, nw_ref, sc_ref, sh_ref, w_ref, wd_ref, p_ref, dt_ref, u_ref):
        _, n = _rms(h_ref[...])
        u = ((n * nw_ref[...]) * (1.0 + sc_ref[...]) + sh_ref[...]).astype(BF16)
        u_ref[...] = u
        p_ref[...] = _dot(u, w_ref[...], NT)
        dt_ref[...] = _dot(u, wd_ref[...], NT)

    return pl.pallas_call(
        body, name="f_in", grid=(t // tb,),
        in_specs=[pl.BlockSpec((tb, D), lambda i: (i, 0)), _row(D), _row(D), _row(D), _full((PW, D)), _full((DTW, D))],
        out_specs=[pl.BlockSpec((tb, PW), lambda i: (i, 0)), pl.BlockSpec((tb, DTW), lambda i: (i, 0)),
                   pl.BlockSpec((tb, D), lambda i: (i, 0))],
        out_shape=[jax.ShapeDtypeStruct((t, PW), F32), jax.ShapeDtypeStruct((t, DTW), F32), jax.ShapeDtypeStruct((t, D), BF16)],
        compiler_params=_cparams(1),
    )(h, nw, sc, sh, w_main, w_dt)


def _norm_bwd_step(du_v, x, dres_v, gated, nwv, scv, dx_ref, dsc_ref, dsh_ref, dnw_ref, dg_ref):
    r, n = _rms(x)
    scale = 1.0 + scv
    dsc_ref[...] += _colsum(du_v * (n * nwv))
    dsh_ref[...] += _colsum(du_v)
    dnw_ref[...] += _colsum(du_v * scale * n)
    dg_ref[...] += _colsum(dres_v * gated)
    dx_ref[...] = dres_v + _rms_bwd(r, n, du_v * scale * nwv)


def _b_in(dab, dz, dxbc, ds5, ddt, w_main, w_dt, x, dres, gated, nw, sc):
    t = dab.shape[0]
    tb = _tblock(t)

    def body(a_ref, z_ref, x_ref, s_ref, d_ref, w_ref, wd_ref, h_ref, dr_ref, g_ref, nw_ref, sc_ref,
             dx_ref, dsc_ref, dsh_ref, dnw_ref, dg_ref):
        @pl.when(pl.program_id(0) == 0)
        def _():
            for r in (dsc_ref, dsh_ref, dnw_ref, dg_ref):
                r[...] = jnp.zeros_like(r)

        du = _bdot(a_ref[...], w_ref[0:1024, :])
        du += _bdot(z_ref[...], w_ref[1024:1280, :])
        du += _bdot(s_ref[...], w_ref[1280:1536, :])
        du += _bdot(x_ref[...], w_ref[1536:2304, :])
        du += _bdot(d_ref[...], wd_ref[...])
        _norm_bwd_step(du, h_ref[...], dr_ref[...], g_ref[...], nw_ref[...], sc_ref[...], dx_ref, dsc_ref, dsh_ref, dnw_ref, dg_ref)

    blk = lambda n: pl.BlockSpec((tb, n), lambda i: (i, 0))
    row = jax.ShapeDtypeStruct((1, D), F32)
    return pl.pallas_call(
        body, name="b_in", grid=(t // tb,),
        in_specs=[blk(1024), blk(256), blk(768), blk(256), blk(DTW), _full((PW, D)), _full((DTW, D)),
                  blk(D), blk(D), blk(D), _row(D), _row(D)],
        out_specs=[blk(D), _row(D), _row(D), _row(D), _row(D)],
        out_shape=[jax.ShapeDtypeStruct((t, D), F32), row, row, row, row], compiler_params=_cparams(1),
    )(dab, dz, dxbc, ds5, ddt, w_main, w_dt, x, dres, gated, nw, sc)


def _b_normmod(du, x, dres, gated, nw, sc, name):
    t = x.shape[0]
    tb = _tblock(t)

    def body(du_ref, x_ref, dr_ref, g_ref, nw_ref, sc_ref, dx_ref, dsc_ref, dsh_ref, dnw_ref, dg_ref):
        @pl.when(pl.program_id(0) == 0)
        def _():
            for r in (dsc_ref, dsh_ref, dnw_ref, dg_ref):
                r[...] = jnp.zeros_like(r)

        _norm_bwd_step(du_ref[...], x_ref[...], dr_ref[...], g_ref[...], nw_ref[...], sc_ref[...],
                       dx_ref, dsc_ref, dsh_ref, dnw_ref, dg_ref)

    blk = pl.BlockSpec((tb, D), lambda i: (i, 0))
    row = jax.ShapeDtypeStruct((1, D), F32)
    return pl.pallas_call(
        body, name=name, grid=(t // tb,), in_specs=[blk, blk, blk, blk, _row(D), _row(D)],
        out_specs=[blk, _row(D), _row(D), _row(D), _row(D)], out_shape=[jax.ShapeDtypeStruct((t, D), F32), row, row, row, row],
        compiler_params=_cparams(1),
    )(du, x, dres, gated, nw, sc)


HALO = 16


def _lane_group(shape):
    return lax.broadcasted_iota(jnp.int32, shape, 1) // 64


def _window_select(g, s2, s4, s8, s16):
    return jnp.where(g == 0, s2, jnp.where(g == 1, s4, jnp.where(g == 2, s8, s16)))


def _pool_count(t0, rows):
    g = _lane_group((rows, GW))
    win = _window_select(g, 2, 4, 8, 16)
    tt = t0 + lax.broadcasted_iota(jnp.int32, (rows, GW), 0)
    return jnp.minimum(tt + 1, win).astype(F32)


def _pool_p(v_ext, t0, tb):
    s2 = v_ext + _roll(v_ext, 1)
    s4 = s2 + _roll(s2, 2)
    s8 = s4 + _roll(s4, 4)
    s16 = s8 + _roll(s8, 8)
    ws = _window_select(_lane_group(v_ext.shape), s2, s4, s8, s16)[HALO:]
    return ws / _pool_count(t0, tb) - v_ext[HALO:]


def _sconv(q_ext, w):
    return (_roll(q_ext, 2) * w[0:1] + _roll(q_ext, 1) * w[1:2] + q_ext * w[2:3])[HALO:]


def _halo_specs(t, tb, cols, col_block):
    per = tb // HALO
    last = t // HALO - 1
    prev = pl.BlockSpec((HALO, cols), lambda i: (jnp.maximum(i * per - 1, 0), col_block))
    nxt = pl.BlockSpec((HALO, cols), lambda i: (jnp.minimum((i + 1) * per, last), col_block))
    return prev, nxt


def _f_ab(proj, pool_mat, pool_scale, sconv_w):
    t = proj.shape[0]
    tb = _tblock(t)
    prev, _ = _halo_specs(t, tb, 1024, 0)

    def body(p_ref, h_ref, pm_ref, ps_ref, sw_ref, ya_ref, yb_ref):
        i = pl.program_id(0)
        halo = jnp.where(i > 0, h_ref[...], 0.0)
        ext = jnp.concatenate([halo, p_ref[...]], axis=0)
        p = _pool_p(ext[:, 0:256], i * tb, tb)
        ya_ref[...] = _bdot(p, pm_ref[...]) * ps_ref[...]
        q_ext = ext[:, 512:768] * ext[:, 768:1024]
        yb_ref[...] = p_ref[:, 256:512] * _sconv(q_ext, sw_ref[...])

    blk = pl.BlockSpec((tb, GW), lambda i: (i, 0))
    sds = jax.ShapeDtypeStruct((t, GW), F32)
    return pl.pallas_call(
        body, name="f_ab", grid=(t // tb,),
        in_specs=[pl.BlockSpec((tb, 1024), lambda i: (i, 0)), prev, _full((GW, GW)), _row(GW), _full((3, GW))],
        out_specs=[blk, blk], out_shape=[sds, sds], compiler_params=_cparams(1),
    )(proj, proj, pool_mat, pool_scale, sconv_w)


def _b_ab(proj, dya, dyb, pool_mat, pool_scale, sconv_w):
    t = proj.shape[0]
    tb = _tblock(t)
    nb = t // tb
    prev, nxt = _halo_specs(t, tb, 1024, 0)
    _, nxt_g = _halo_specs(t, tb, GW, 0)
    n_ext = tb + HALO

    def body(p_ref, hp_ref, hn_ref, da_ref, dan_ref, db_ref, dbn_ref, pm_ref, ps_ref, sw_ref,
             o_ref, dpm_ref, dps_ref, dsw_ref):
        i = pl.program_id(0)

        @pl.when(i == 0)
        def _():
            for r in (dpm_ref, dps_ref, dsw_ref):
                r[...] = jnp.zeros_like(r)

        last = i == nb - 1
        halo = jnp.where(i > 0, hp_ref[...], 0.0)
        main = p_ref[...]
        ext = jnp.concatenate([halo, main], axis=0)
        scale = ps_ref[...]
        pm = pm_ref[...]
        p = _pool_p(ext[:, 0:256], i * tb, tb)
        da = da_ref[...]
        dps_ref[...] += _colsum(da * _bdot(p, pm))
        da_ext = jnp.concatenate([da, jnp.where(last, 0.0, dan_ref[...])], axis=0)
        dys = da_ext * scale
        dpm_ref[...] += _bdot(p, dys[:tb], TN)
        dp = _bdot(dys, pm, NT)
        dpc = dp / _pool_count(i * tb, n_ext)
        a2 = dpc + _roll(dpc, n_ext - 1)
        a4 = a2 + _roll(a2, n_ext - 2)
        a8 = a4 + _roll(a4, n_ext - 4)
        a16 = a8 + _roll(a8, n_ext - 8)
        o_ref[:, 0:256] = (_window_select(_lane_group(dpc.shape), a2, a4, a8, a16) - dp)[:tb]
        w = sw_ref[...]
        gb, gc, hh = main[:, 256:512], main[:, 512:768], main[:, 768:1024]
        q_ext = ext[:, 512:768] * ext[:, 768:1024]
        db = db_ref[...]
        o_ref[:, 256:512] = db * _sconv(q_ext, w)
        gb_next = hn_ref[:, 256:512]
        dconv = jnp.concatenate([db * gb, jnp.where(last, 0.0, dbn_ref[...] * gb_next)], axis=0)
        dq = (dconv * w[2:3] + _roll(dconv, n_ext - 1) * w[1:2] + _roll(dconv, n_ext - 2) * w[0:1])[:tb]
        o_ref[:, 512:768] = dq * hh
        o_ref[:, 768:1024] = dq * gc
        dc = dconv[:tb]
        dsw_ref[0:1, :] += _colsum(dc * _roll(q_ext, 2)[HALO:])
        dsw_ref[1:2, :] += _colsum(dc * _roll(q_ext, 1)[HALO:])
        dsw_ref[2:3, :] += _colsum(dc * q_ext[HALO:])

    blk = pl.BlockSpec((tb, GW), lambda i: (i, 0))
    return pl.pallas_call(
        body, name="b_ab", grid=(nb,),
        in_specs=[pl.BlockSpec((tb, 1024), lambda i: (i, 0)), prev, nxt, blk, nxt_g, blk, nxt_g,
                  _full((GW, GW)), _row(GW), _full((3, GW))],
        out_specs=[pl.BlockSpec((tb, 1024), lambda i: (i, 0)), _full((GW, GW)), _row(GW), _full((3, GW))],
        out_shape=[jax.ShapeDtypeStruct((t, 1024), F32), jax.ShapeDtypeStruct((GW, GW), F32),
                   jax.ShapeDtypeStruct((1, GW), F32), jax.ShapeDtypeStruct((3, GW), F32)],
        compiler_params=_cparams(1),
    )(proj, proj, proj, dya, dya, dyb, dyb, pool_mat, pool_scale, sconv_w)


CH = 8


def _ssd_conv(x, halo, w, b):
    ext = jnp.concatenate([halo, x], axis=0)
    pre = ext * w[3:4] + _roll(ext, 1) * w[2:3] + _roll(ext, 2) * w[1:2] + _roll(ext, 3) * w[0:1] + b
    return pre[CH:], ext


def _ssd_common(dt_raw, dtb, alog):
    ll = dt_raw.shape[0]
    dtv = _softplus(dt_raw + dtb)
    a_row = -jnp.exp(alog)
    r = lax.broadcasted_iota(jnp.int32, (ll, ll), 0)
    c = lax.broadcasted_iota(jnp.int32, (ll, ll), 1)
    tril = (r >= c).astype(F32)
    cs = _dot(tril, dtv * a_row, prec=HI)
    return dtv, a_row, cs, cs.T, r >= c


def _bd(a, b, ca, cb):
    return lax.dot_general(a, b, (((ca,), (cb,)), ((0,), (0,))), preferred_element_type=F32)


def _head_cols(m):
    return jnp.stack([m[:, h:h + 1] for h in range(NH)])


def _ssd_heads(act, dtv, cs, cs_t, causal):
    xs = jnp.stack([act[:, HP * h:HP * (h + 1)] for h in range(NH)])
    bm = jnp.stack([act[:, 256 + NS * (h // 2):256 + NS * (h // 2 + 1)] for h in range(NH)])
    cm = jnp.stack([act[:, 512 + NS * (h // 2):512 + NS * (h // 2 + 1)] for h in range(NH)])
    cs_c = _head_cols(cs)
    cs_r = jnp.stack([cs_t[h:h + 1, :] for h in range(NH)])
    mdec = jnp.where(causal[None], jnp.exp(jnp.minimum(cs_c - cs_r, 0.0)), 0.0)
    g2 = _bd(jnp.stack([cm[0], cm[2]]), jnp.stack([bm[0], bm[2]]), 2, 2)
    sc = jnp.stack([g2[h // 2] for h in range(NH)]) * mdec
    dt_c = _head_cols(dtv)
    xdt = xs * dt_c
    e = jnp.exp(cs_c)
    cs_last = cs_c[:, SSD_L - 1:SSD_L, :]
    wdec = jnp.exp(cs_last - cs_c)
    return xs, bm, cm, mdec, sc, dt_c, xdt, e, cs_last, wdec


def _head_scalars(row_ref):
    return jnp.stack([row_ref[0:1, h:h + 1] for h in range(NH)])


def _f_ssd(proj, dtp, conv_w, conv_b, dt_bias, a_log, d_skip):
    t = proj.shape[0]
    nc = t // SSD_L
    rows = SSD_SUB * SSD_L
    per = rows // CH

    def body(x_ref, hx_ref, dt_ref, z_ref, cw_ref, cb_ref, dtb_ref, al_ref, dk_ref, y_ref, yp_ref, sp_ref, s_ref):
        i = pl.program_id(0)

        @pl.when(i == 0)
        def _():
            s_ref[...] = jnp.zeros_like(s_ref)

        state = s_ref[...]
        dk = _head_scalars(dk_ref)
        for sub in range(SSD_SUB):
            r0 = sub * SSD_L
            rs = slice(r0, r0 + SSD_L)
            halo = jnp.where(i > 0, hx_ref[...], 0.0) if sub == 0 else x_ref[r0 - CH:r0, :]
            pre, _ = _ssd_conv(x_ref[rs, :], halo, cw_ref[...], cb_ref[...])
            act = _silu(pre)
            dtv, _, cs, cs_t, causal = _ssd_common(dt_ref[rs, :], dtb_ref[...], al_ref[...])
            xs, bm, cm, _, sc, _, xdt, e, cs_last, wdec = _ssd_heads(act, dtv, cs, cs_t, causal)
            sp_ref[sub] = state
            y = _bd(sc, xdt, 2, 1) + e * _bd(cm, state, 2, 2) + xs * dk
            for h in range(NH):
                yp_ref[rs, HP * h:HP * (h + 1)] = y[h]
            state = state * jnp.exp(cs_last) + _bd(xdt * wdec, bm, 1, 1)
            y_ref[rs, :] = yp_ref[rs, :] * _silu(z_ref[rs, :])
        s_ref[...] = state

    blk = pl.BlockSpec((rows, GW), lambda i: (i, 0))
    sds = jax.ShapeDtypeStruct((t, GW), F32)
    return pl.pallas_call(
        body, name="f_ssd", grid=(nc // SSD_SUB,),
        in_specs=[pl.BlockSpec((rows, 768), lambda i: (i, 2)),
                  pl.BlockSpec((CH, 768), lambda i: (jnp.maximum(i * per - 1, 0), 2)),
                  pl.BlockSpec((rows, DTW), lambda i: (i, 0)),
                  pl.BlockSpec((rows, GW), lambda i: (i, 4)),
                  _full((4, 768)), _row(768), _row(DTW), _row(DTW), _row(DTW)],
        out_specs=[blk, blk, pl.BlockSpec((SSD_SUB, NH, HP, NS), lambda i: (i, 0, 0, 0))],
        out_shape=[sds, sds, jax.ShapeDtypeStruct((nc, NH, HP, NS), F32)],
        scratch_shapes=[pltpu.VMEM((NH, HP, NS), F32)], compiler_params=_cparams(1),
    )(proj, proj, dtp, proj, conv_w, conv_b, dt_bias, a_log, d_skip)


def _b_ssd(proj, dtp, ypre, dyc, sprev, conv_w, conv_b, dt_bias, a_log, d_skip):
    t = proj.shape[0]
    nc = t // SSD_L
    steps = nc // SSD_SUB_BWD
    rows = SSD_SUB_BWD * SSD_L
    per = rows // CH
    n_ext = SSD_L + CH

    def chunk(sub, halo, dnext, ds_in, refs):
        (x_ref, dt_ref, z_ref, yp_ref, dy_ref, sp_ref, cw_ref, cb_ref, dtb_ref, al_ref, dk_ref,
         dz_ref, dx_ref, ddt_ref, dact_ref) = refs
        rs = slice(sub * SSD_L, (sub + 1) * SSD_L)
        dact = dact_ref.at[sub]
        w = cw_ref[...]
        pre, ext = _ssd_conv(x_ref[rs, :], halo, w, cb_ref[...])
        act = _silu(pre)
        dt_raw = dt_ref[rs, :]
        dtv, a_row, cs, cs_t, causal = _ssd_common(dt_raw, dtb_ref[...], al_ref[...])
        z = z_ref[rs, :]
        dyc_v = dy_ref[rs, :]
        dz_ref[rs, :] = dyc_v * yp_ref[rs, :] * _dsilu(z)
        dy_all = dyc_v * _silu(z)
        lane = lax.broadcasted_iota(jnp.int32, (SSD_L, DTW), 1)
        rowi = lax.broadcasted_iota(jnp.int32, (1, SSD_L, 1), 1)
        lane1 = lax.broadcasted_iota(jnp.int32, (1, DTW), 1)
        xs, bm, cm, mdec, sc, dt_c, xdt, e, cs_last, wdec = _ssd_heads(act, dtv, cs, cs_t, causal)
        dy = jnp.stack([dy_all[:, HP * h:HP * (h + 1)] for h in range(NH)])
        prev = sp_ref[sub]
        ds = ds_in
        lsum = lambda v: jnp.sum(v, axis=2, keepdims=True)
        dsc = _bd(dy, xdt, 2, 2)
        q = dsc * sc
        dg = dsc * mdec
        dxdt = _bd(sc, dy, 1, 1)
        dcs = lsum(q) - lsum(jnp.swapaxes(q, 1, 2))
        dc = _bd(dg, bm, 2, 1)
        db = _bd(dg, cm, 1, 1)
        cp = _bd(cm, prev, 2, 2)
        dcs += lsum(dy * cp) * e
        ey = e * dy
        dc += _bd(ey, prev, 2, 1)
        dprev = _bd(ey, cm, 1, 1)
        elast = jnp.exp(cs_last)
        dprev += ds * elast
        dcs_last = jnp.sum(lsum(ds * prev), axis=1, keepdims=True) * elast
        bds = _bd(bm, ds, 2, 2)
        dxdt += wdec * bds
        db += wdec * _bd(xdt, ds, 2, 1)
        dw = lsum(xdt * bds) * wdec
        dcs -= dw
        dcs_last += jnp.sum(dw, axis=1, keepdims=True)
        dcs += jnp.where(rowi == SSD_L - 1, dcs_last, 0.0)
        dxs = dxdt * dt_c + dy * _head_scalars(dk_ref)
        ddtx = lsum(dxdt * xs)
        ddk = jnp.sum(lsum(dy * xs), axis=1, keepdims=True)
        dcs_mat = jnp.zeros((SSD_L, DTW), F32)
        ddtx_mat = jnp.zeros((SSD_L, DTW), F32)
        ddk_row = jnp.zeros((1, DTW), F32)
        for h in range(NH):
            dact[:, HP * h:HP * (h + 1)] = dxs[h]
            dcs_mat = jnp.where(lane == h, dcs[h], dcs_mat)
            ddtx_mat = jnp.where(lane == h, ddtx[h], ddtx_mat)
            ddk_row = jnp.where(lane1 == h, ddk[h], ddk_row)
        for g in range(2):
            dact[:, 256 + NS * g:256 + NS * (g + 1)] = db[2 * g] + db[2 * g + 1]
            dact[:, 512 + NS * g:512 + NS * (g + 1)] = dc[2 * g] + dc[2 * g + 1]
        ds_out = dprev
        r2 = lax.broadcasted_iota(jnp.int32, (SSD_L, SSD_L), 0)
        c2 = lax.broadcasted_iota(jnp.int32, (SSD_L, SSD_L), 1)
        dadt = _dot((c2 >= r2).astype(F32), dcs_mat, prec=HI)
        ddt = jnp.where(lane < NH, (dadt * a_row + ddtx_mat) * _sig(dt_raw + dtb_ref[...]), 0.0)
        ddt_ref[rs, :] = ddt
        dpre = dact[...] * _dsilu(pre)
        dcw = jnp.concatenate([_colsum(dpre * _roll(ext, 3 - k)[CH:]) for k in range(4)], axis=0)
        dext = jnp.concatenate([dpre, dnext], axis=0)
        dx_ref[rs, :] = (dext * w[3:4] + _roll(dext, n_ext - 1) * w[2:3] + _roll(dext, n_ext - 2) * w[1:2]
                         + _roll(dext, n_ext - 3) * w[0:1])[:SSD_L]
        acc = (dcw, _colsum(dpre), _colsum(ddt), _colsum(dadt * dtv) * a_row, ddk_row)
        return dpre[0:CH], ds_out, acc

    def body(x_ref, hx_ref, dt_ref, z_ref, yp_ref, dy_ref, sp_ref, cw_ref, cb_ref, dtb_ref, al_ref, dk_ref,
             dz_ref, dx_ref, ddt_ref, dcw_ref, dcb_ref, ddtb_ref, dal_ref, ddk_ref, ds_ref, dnext_ref, dact_ref):
        i = pl.program_id(0)
        acc_refs = (dcw_ref, dcb_ref, ddtb_ref, dal_ref, ddk_ref)

        @pl.when(i == 0)
        def _():
            ds_ref[...] = jnp.zeros_like(ds_ref)
            dnext_ref[...] = jnp.zeros_like(dnext_ref)
            for r in acc_refs:
                r[...] = jnp.zeros_like(r)

        refs = (x_ref, dt_ref, z_ref, yp_ref, dy_ref, sp_ref, cw_ref, cb_ref, dtb_ref, al_ref, dk_ref, dz_ref, dx_ref, ddt_ref,
                dact_ref)
        ds = ds_ref[...]
        dnext = dnext_ref[...]
        total = None
        for sub in reversed(range(SSD_SUB_BWD)):
            if sub == 0:
                halo = jnp.where(i == steps - 1, 0.0, hx_ref[...])
            else:
                halo = x_ref[sub * SSD_L - CH:sub * SSD_L, :]
            dnext, ds, acc = chunk(sub, halo, dnext, ds, refs)
            total = acc if total is None else tuple(a + b for a, b in zip(total, acc))
        ds_ref[...] = ds
        dnext_ref[...] = dnext
        for r, v in zip(acc_refs, total):
            r[...] += v

    rev = lambda i: steps - 1 - i
    blk = lambda n, cb=0: pl.BlockSpec((rows, n), lambda i: (rev(i), cb))
    row = lambda n: jax.ShapeDtypeStruct((1, n), F32)
    return pl.pallas_call(
        body, name="b_ssd", grid=(steps,),
        in_specs=[blk(768, 2), pl.BlockSpec((CH, 768), lambda i: (jnp.maximum(rev(i) * per - 1, 0), 2)),
                  blk(DTW), blk(GW, 4), blk(GW), blk(GW), pl.BlockSpec((SSD_SUB_BWD, NH, HP, NS), lambda i: (rev(i), 0, 0, 0)),
                  _full((4, 768)), _row(768), _row(DTW), _row(DTW), _row(DTW)],
        out_specs=[blk(GW), blk(768), blk(DTW), _full((4, 768)), _row(768), _row(DTW), _row(DTW), _row(DTW)],
        out_shape=[jax.ShapeDtypeStruct((t, GW), F32), jax.ShapeDtypeStruct((t, 768), F32), jax.ShapeDtypeStruct((t, DTW), F32),
                   jax.ShapeDtypeStruct((4, 768), F32), row(768), row(DTW), row(DTW), row(DTW)],
        scratch_shapes=[pltpu.VMEM((NH, HP, NS), F32), pltpu.VMEM((CH, 768), F32), pltpu.VMEM((SSD_SUB_BWD, SSD_L, 768), F32)],
        compiler_params=_cparams(1),
    )(proj, proj, dtp, proj, ypre, dyc, sprev, conv_w, conv_b, dt_bias, a_log, d_skip)


def _s5_block(t):
    return min(t, 256)


def _seg_t():
    r = lax.broadcasted_iota(jnp.int32, (64, 1024), 0)
    c = lax.broadcasted_iota(jnp.int32, (64, 1024), 1)
    return (c // 16 == r).astype(F32)


def _s5_prep_math(a_re, a_im, lstep, b_re, b_im):
    step = jnp.exp(lstep)
    ars = a_re * step
    ais = a_im * step
    mag = jnp.exp(ars)
    lr = mag * jnp.cos(ais)
    li = mag * jnp.sin(ais)
    den = a_re * a_re + a_im * a_im
    nr = lr - 1.0
    f_re = (nr * a_re + li * a_im) / den
    f_im = (li * a_re - nr * a_im) / den
    seg = _seg_t()
    fr = _dot(f_re, seg, prec=HI)
    fi = _dot(f_im, seg, prec=HI)
    return lr, li, fr * b_re - fi * b_im, fr * b_im + fi * b_re, ars, ais


def _s5_prep(a_re, a_im, lstep, b_re, b_im):
    def body(ar, ai, ls, br, bi, lr_o, li_o, bbr_o, bbi_o, ars_o, ais_o):
        outs = _s5_prep_math(ar[...], ai[...], ls[...], br[...], bi[...])
        for o, v in zip((lr_o, li_o, bbr_o, bbi_o, ars_o, ais_o), outs):
            o[...] = v

    s64 = jax.ShapeDtypeStruct((16, 64), F32)
    s1k = jax.ShapeDtypeStruct((16, 1024), F32)
    return pl.pallas_call(body, name="s5_prep", out_shape=[s64, s64, s1k, s1k, s64, s64])(a_re, a_im, lstep, b_re, b_im)


def _s5_prep_bwd(a_re, a_im, lstep, b_re, b_im, dlr, dli, dbbr, dbbi):
    def body(ar, ai, ls, br, bi, g0, g1, g2, g3, o0, o1, o2, o3, o4):
        f = lambda *a: _s5_prep_math(*a)[:4]
        _, vjp = jax.vjp(f, ar[...], ai[...], ls[...], br[...], bi[...])
        for o, v in zip((o0, o1, o2, o3, o4), vjp((g0[...], g1[...], g2[...], g3[...]))):
            o[...] = v

    s64 = jax.ShapeDtypeStruct((16, 64), F32)
    s1k = jax.ShapeDtypeStruct((16, 1024), F32)
    return pl.pallas_call(body, name="s5_prep_bwd", out_shape=[s64, s64, jax.ShapeDtypeStruct((16, 1), F32), s1k, s1k])(
        a_re, a_im, lstep, b_re, b_im, dlr, dli, dbbr, dbbi)


SUB = 8


def _s5_tables(ars, ais):
    def body(ar, ai, tr, ti):
        rr = lax.broadcasted_iota(jnp.int32, (8 * SUB, S5_P), 0)
        seg, r = rr // SUB, rr % SUB
        step = jnp.where((seg == 1) | (seg == 4), 1, jnp.where((seg == 2) | (seg == 5), 2, 4))
        n = jnp.where(seg == 0, r + 1, jnp.where(seg == 7, SUB - r, step))
        fwd_gap = jnp.where(seg <= 3, r - step, SUB - step - 1 - r)
        gap = jnp.where((seg == 0) | (seg == 7), 0, fwd_gap)
        nf = n.astype(F32)
        mag = jnp.where(gap >= 0, jnp.exp(nf * ar[...]), 0.0)
        tr[...] = mag * jnp.cos(nf * ai[...])
        ti[...] = mag * jnp.sin(nf * ai[...])

    sds = jax.ShapeDtypeStruct((8 * SUB, S5_P), F32)
    return pl.pallas_call(body, name="s5_tables", out_shape=[sds] * 2)(ars, ais)


def _s5_table(tb_r, tb_i, k):
    return tb_r[SUB * k:SUB * (k + 1), :], tb_i[SUB * k:SUB * (k + 1), :]


def _s5_scan(bu_r, bu_i, tb_r, tb_i, c_r, c_i, lb):
    nt = lb // SUB
    sr, si = bu_r.reshape(nt, SUB, S5_P), bu_i.reshape(nt, SUB, S5_P)
    for j, k in enumerate((1, 2, 4)):
        mr, mi = _s5_table(tb_r, tb_i, 1 + j)
        tr, ti = pltpu.roll(sr, k, axis=1), pltpu.roll(si, k, axis=1)
        sr, si = sr + mr * tr - mi * ti, si + mr * ti + mi * tr
    pr, pi = _s5_table(tb_r, tb_i, 0)
    out_r, out_i = [], []
    for j in range(nt):
        a_r = sr[j] + pr * c_r - pi * c_i
        a_i = si[j] + pr * c_i + pi * c_r
        out_r.append(a_r)
        out_i.append(a_i)
        c_r, c_i = a_r[SUB - 1:SUB], a_i[SUB - 1:SUB]
    return jnp.concatenate(out_r, axis=0), jnp.concatenate(out_i, axis=0)


def _s5_rscan(g_r, g_i, tb_r, tb_i, n_r, n_i, lb):
    nt = lb // SUB
    gr, gi = g_r.reshape(nt, SUB, S5_P), g_i.reshape(nt, SUB, S5_P)
    for j, k in enumerate((1, 2, 4)):
        mr, mi = _s5_table(tb_r, tb_i, 4 + j)
        tr, ti = pltpu.roll(gr, SUB - k, axis=1), pltpu.roll(gi, SUB - k, axis=1)
        gr, gi = gr + mr * tr + mi * ti, gi + mr * ti - mi * tr
    qr, qi = _s5_table(tb_r, tb_i, 7)
    out_r, out_i = [None] * nt, [None] * nt
    for j in reversed(range(nt)):
        a_r = gr[j] + qr * n_r + qi * n_i
        a_i = gi[j] + qr * n_i - qi * n_r
        out_r[j], out_i[j] = a_r, a_i
        n_r, n_i = a_r[0:1], a_i[0:1]
    return jnp.concatenate(out_r, axis=0), jnp.concatenate(out_i, axis=0)


def _s5_y(u, sr, si, cre, cim, dsk):
    return _bdot(sr, cre) + _bdot(si, cim) + dsk * u


def _f_s5(proj, bmat, cre, cim, p_r, p_i, dsk, glu_w, glu_b):
    t = proj.shape[0]
    lb = _s5_block(t)
    nb = t // lb

    def body(u_ref, bm_ref, cr_ref, ci_ref, pr_ref, pi_ref, dk_ref, gw_ref, gb_ref, y_ref, car_ref, s_ref, st_ref):
        @pl.when(pl.program_id(0) == 0)
        def _():
            st_ref[...] = jnp.zeros_like(st_ref)

        u = u_ref[...]
        bu = _bdot(u, bm_ref[...])
        c_r, c_i = st_ref[0:1, 0:S5_P], st_ref[0:1, S5_P:]
        car_ref[0] = st_ref[0:1, :]
        sr, si = _s5_scan(bu[:, :S5_P], bu[:, S5_P:], pr_ref, pi_ref, c_r, c_i, lb)
        st_ref[0:1, 0:S5_P] = sr[lb - 1:lb]
        st_ref[0:1, S5_P:] = si[lb - 1:lb]
        sr_b, si_b = sr.astype(BF16), si.astype(BF16)
        s_ref[:, 0:S5_P] = sr_b
        s_ref[:, S5_P:] = si_b
        gel = _gelu(_s5_y(u, sr_b, si_b, cr_ref[...], ci_ref[...], dk_ref[...]))
        y_ref[...] = gel * _sig(_bdot(gel, gw_ref[...]) + gb_ref[...])

    return pl.pallas_call(
        body, name="f_s5", grid=(nb,),
        in_specs=[pl.BlockSpec((lb, GW), lambda i: (i, 5)),
                  _full((GW, 2 * S5_P)), _full((S5_P, GW)), _full((S5_P, GW)), _full((8 * SUB, S5_P)), _full((8 * SUB, S5_P)),
                  _row(GW), _full((GW, GW)), _row(GW)],
        out_specs=[pl.BlockSpec((lb, GW), lambda i: (i, 0)), pl.BlockSpec((1, 1, 2 * S5_P), lambda i: (i, 0, 0)),
                   pl.BlockSpec((lb, 2 * S5_P), lambda i: (i, 0))],
        out_shape=[jax.ShapeDtypeStruct((t, GW), F32), jax.ShapeDtypeStruct((nb, 1, 2 * S5_P), F32),
                   jax.ShapeDtypeStruct((t, 2 * S5_P), BF16)],
        scratch_shapes=[pltpu.VMEM((8, 2 * S5_P), F32)], compiler_params=_cparams(1),
    )(proj, bmat, cre, cim, p_r, p_i, dsk, glu_w, glu_b)


def _b_s5(proj, dyd, carries, states, bmat, cre, cim, p_r, p_i, dsk, glu_w, glu_b):
    t = proj.shape[0]
    lb = _s5_block(t)
    nb = t // lb

    def body(u_ref, dy_ref, car_ref, s_ref, bm_ref, cr_ref, ci_ref, pr_ref, pi_ref, dk_ref, gw_ref, gb_ref,
             du_ref, dbm_ref, dcr_ref, dci_ref, dlam_ref, ddk_ref, dgw_ref, dgb_ref, gc_ref):
        @pl.when(pl.program_id(0) == 0)
        def _():
            gc_ref[...] = jnp.zeros_like(gc_ref)
            for r in (dbm_ref, dcr_ref, dci_ref, dlam_ref, ddk_ref, dgw_ref, dgb_ref):
                r[...] = jnp.zeros_like(r)

        u = u_ref[...]
        bm = bm_ref[...]
        u_b = u.astype(BF16)
        c_r, c_i = car_ref[0, 0:1, 0:S5_P], car_ref[0, 0:1, S5_P:]
        cre_v, cim_v, dk, gw = cr_ref[...], ci_ref[...], dk_ref[...], gw_ref[...]
        sr_b, si_b = s_ref[:, 0:S5_P], s_ref[:, S5_P:]
        sr, si = sr_b.astype(F32), si_b.astype(F32)
        y = _dot(sr_b, cre_v) + _dot(si_b, cim_v) + dk * u
        gel = _gelu(y)
        gel_b = gel.astype(BF16)
        gate = _sig(_dot(gel_b, gw) + gb_ref[...])
        dout = dy_ref[...]
        t1 = dout * gel * gate * (1.0 - gate)
        t1_b = t1.astype(BF16)
        dgw_ref[...] += _dot(gel_b, t1_b, TN)
        dgb_ref[...] += _colsum(t1)
        dyv = (dout * gate + _dot(t1_b, gw, NT)) * _dgelu(y)
        dyv_b = dyv.astype(BF16)
        ddk_ref[...] += _colsum(dyv * u)
        dcr_ref[...] += _dot(sr_b, dyv_b, TN)
        dci_ref[...] += _dot(si_b, dyv_b, TN)
        gr = _dot(dyv_b, cre_v, NT)
        gi = _dot(dyv_b, cim_v, NT)
        row = lax.broadcasted_iota(jnp.int32, (lb, S5_P), 0)
        n_r, n_i = gc_ref[0:1, 0:S5_P], gc_ref[0:1, S5_P:]
        gr, gi = _s5_rscan(gr, gi, pr_ref, pi_ref, n_r, n_i, lb)
        gc_ref[0:1, 0:S5_P] = gr[0:1]
        gc_ref[0:1, S5_P:] = gi[0:1]
        gcat = jnp.concatenate([gr, gi], axis=1).astype(BF16)
        dbm_ref[...] += _dot(u_b, gcat, TN)
        du_ref[...] = dyv * dk + _dot(gcat, bm, NT)
        spr = jnp.where(row >= 1, _roll(sr, 1), c_r)
        spi = jnp.where(row >= 1, _roll(si, 1), c_i)
        dlam_ref[0:1, :] += _colsum(gr * spr + gi * spi)
        dlam_ref[1:2, :] += _colsum(gi * spr - gr * spi)

    rev = lambda i: nb - 1 - i
    return pl.pallas_call(
        body, name="b_s5", grid=(nb,),
        in_specs=[pl.BlockSpec((lb, GW), lambda i: (rev(i), 5)), pl.BlockSpec((lb, GW), lambda i: (rev(i), 0)),
                  pl.BlockSpec((1, 1, 2 * S5_P), lambda i: (rev(i), 0, 0)), pl.BlockSpec((lb, 2 * S5_P), lambda i: (rev(i), 0)),
                  _full((GW, 2 * S5_P)), _full((S5_P, GW)), _full((S5_P, GW)), _full((8 * SUB, S5_P)), _full((8 * SUB, S5_P)),
                  _row(GW), _full((GW, GW)), _row(GW)],
        out_specs=[pl.BlockSpec((lb, GW), lambda i: (rev(i), 0)), _full((GW, 2 * S5_P)), _full((S5_P, GW)), _full((S5_P, GW)),
                   _full((2, S5_P)), _row(GW), _full((GW, GW)), _row(GW)],
        out_shape=[jax.ShapeDtypeStruct((t, GW), F32), jax.ShapeDtypeStruct((GW, 2 * S5_P), F32),
                   jax.ShapeDtypeStruct((S5_P, GW), F32), jax.ShapeDtypeStruct((S5_P, GW), F32),
                   jax.ShapeDtypeStruct((2, S5_P), F32), jax.ShapeDtypeStruct((1, GW), F32),
                   jax.ShapeDtypeStruct((GW, GW), F32), jax.ShapeDtypeStruct((1, GW), F32)],
        scratch_shapes=[pltpu.VMEM((8, 2 * S5_P), F32)], compiler_params=_cparams(1),
    )(proj, dyd, carries, states, bmat, cre, cim, p_r, p_i, dsk, glu_w, glu_b)


def _group_norm(ys, bw):
    outs, stats = [], []
    for g, y in enumerate(ys):
        r, n = _rms(y)
        stats.append((r, n))
        outs.append(n * bw[:, GW * g:GW * (g + 1)])
    return jnp.concatenate(outs, axis=1), stats


def _f_out(ya, yb, yc, yd, bw, w_out, h, g1):
    t = h.shape[0]
    tb = _tblock(t)

    def body(a_ref, b_ref, c_ref, d_ref, bw_ref, w_ref, h_ref, g_ref, h2_ref, o_ref, cat_ref):
        cat, _ = _group_norm([a_ref[...], b_ref[...], c_ref[...], d_ref[...]], bw_ref[...])
        catb = cat.astype(BF16)
        cat_ref[...] = catb
        o = _dot(catb, w_ref[...])
        o_ref[...] = o.astype(BF16)
        h2_ref[...] = h_ref[...] + g_ref[...] * o

    yblk = pl.BlockSpec((tb, GW), lambda i: (i, 0))
    blk = pl.BlockSpec((tb, D), lambda i: (i, 0))
    return pl.pallas_call(
        body, name="f_out", grid=(t // tb,), in_specs=[yblk] * 4 + [_row(D), _full((D, D)), blk, _row(D)],
        out_specs=[blk, blk, blk],
        out_shape=[jax.ShapeDtypeStruct((t, D), F32), jax.ShapeDtypeStruct((t, D), BF16), jax.ShapeDtypeStruct((t, D), BF16)],
        compiler_params=_cparams(1),
    )(ya, yb, yc, yd, bw, w_out, h, g1)


def _b_out(dh2, ya, yb, yc, yd, bw, w_out, g1):
    t = dh2.shape[0]
    tb = _tblock(t)

    def body(dh_ref, a_ref, b_ref, c_ref, d_ref, bw_ref, w_ref, g_ref, da_ref, db_ref, dc_ref, dd_ref, do_ref, dbw_ref):
        @pl.when(pl.program_id(0) == 0)
        def _():
            dbw_ref[...] = jnp.zeros_like(dbw_ref)

        do = (dh_ref[...] * g_ref[...]).astype(BF16)
        do_ref[...] = do
        dcat = _dot(do, w_ref[...], NT)
        bw_v = bw_ref[...]
        for g, (y_ref, dy_ref) in enumerate(((a_ref, da_ref), (b_ref, db_ref), (c_ref, dc_ref), (d_ref, dd_ref))):
            r, n = _rms(y_ref[...])
            dc = dcat[:, GW * g:GW * (g + 1)]
            dbw_ref[:, GW * g:GW * (g + 1)] += _colsum(dc * n)
            dy_ref[...] = _rms_bwd(r, n, dc * bw_v[:, GW * g:GW * (g + 1)])

    yblk = pl.BlockSpec((tb, GW), lambda i: (i, 0))
    blk = pl.BlockSpec((tb, D), lambda i: (i, 0))
    ysd = jax.ShapeDtypeStruct((t, GW), F32)
    return pl.pallas_call(
        body, name="b_out", grid=(t // tb,), in_specs=[blk] + [yblk] * 4 + [_row(D), _full((D, D)), _row(D)],
        out_specs=[yblk] * 4 + [blk, _row(D)],
        out_shape=[ysd] * 4 + [jax.ShapeDtypeStruct((t, D), BF16), jax.ShapeDtypeStruct((1, D), F32)],
        compiler_params=_cparams(1),
    )(dh2, ya, yb, yc, yd, bw, w_out, g1)


HB = 512
MLP_ROWS = 1024


def _w1_spec():
    per = HID // 4 // HB
    return pl.BlockSpec((1, D, HB), lambda i, k: (k // per, 0, k % per))


def _f_mlp(h2, nw, sc, sh, g2, w1, w2):
    t = h2.shape[0]
    tb = _tblock(t, MLP_ROWS)
    nk = HID // HB

    def body(h_ref, nw_ref, sc_ref, sh_ref, g_ref, w1_ref, w2_ref, h3_ref, m_ref, a_ref, v_ref, acc_ref):
        k = pl.program_id(1)

        @pl.when(k == 0)
        def _():
            _, n = _rms(h_ref[...])
            v_ref[...] = ((n * nw_ref[...]) * (1.0 + sc_ref[...]) + sh_ref[...]).astype(BF16)
            acc_ref[...] = jnp.zeros_like(acc_ref)

        a = _dot(v_ref[...], w1_ref[0])
        a_ref[...] = a.astype(BF16)
        ra = jnp.maximum(a, 0.0)
        acc_ref[...] += _dot((ra * ra).astype(BF16), w2_ref[...])

        @pl.when(k == nk - 1)
        def _():
            m = acc_ref[...]
            m_ref[...] = m.astype(BF16)
            h3_ref[...] = h_ref[...] + g_ref[...] * m

    blk = pl.BlockSpec((tb, D), lambda i, k: (i, 0))
    return pl.pallas_call(
        body, name="f_mlp", grid=(t // tb, nk),
        in_specs=[blk, _row(D), _row(D), _row(D), _row(D), _w1_spec(),
                  pl.BlockSpec((HB, D), lambda i, k: (k, 0))],
        out_specs=[blk, blk, pl.BlockSpec((tb, HB), lambda i, k: (i, k)), blk],
        out_shape=[jax.ShapeDtypeStruct((t, D), F32), jax.ShapeDtypeStruct((t, D), BF16), jax.ShapeDtypeStruct((t, HID), BF16),
                   jax.ShapeDtypeStruct((t, D), BF16)],
        scratch_shapes=[pltpu.VMEM((tb, D), F32)], compiler_params=_cparams(2),
    )(h2, nw, sc, sh, g2, w1, w2)


def _b_mlp(dh3, a, g2, w1, w2):
    t = dh3.shape[0]
    tb = _tblock(t, MLP_ROWS)
    nk = HID // HB

    def body(dh_ref, a_ref, g_ref, w1_ref, w2_ref, dv_ref, da_ref, act_ref, dm_ref):
        k = pl.program_id(1)
        dm = (dh_ref[...] * g_ref[...]).astype(BF16)

        @pl.when(k == 0)
        def _():
            dm_ref[...] = dm
            dv_ref[...] = jnp.zeros_like(dv_ref)

        ra = jnp.maximum(a_ref[...].astype(F32), 0.0)
        act_ref[...] = (ra * ra).astype(BF16)
        da = (_dot(dm, w2_ref[...], NT) * (2.0 * ra)).astype(BF16)
        da_ref[...] = da
        dv_ref[...] += _dot(da, w1_ref[0], NT)

    blk = pl.BlockSpec((tb, D), lambda i, k: (i, 0))
    hblk = pl.BlockSpec((tb, HB), lambda i, k: (i, k))
    return pl.pallas_call(
        body, name="b_mlp", grid=(t // tb, nk),
        in_specs=[blk, hblk, _row(D), _w1_spec(), pl.BlockSpec((HB, D), lambda i, k: (k, 0))],
        out_specs=[blk, hblk, hblk, blk],
        out_shape=[jax.ShapeDtypeStruct((t, D), F32), jax.ShapeDtypeStruct((t, HID), BF16), jax.ShapeDtypeStruct((t, HID), BF16),
                   jax.ShapeDtypeStruct((t, D), BF16)],
        compiler_params=_cparams(2),
    )(dh3, a, g2, w1, w2)


def _b_final(h, tgt, fw):
    t = h.shape[0]
    tb = _tblock(t)

    def body(h_ref, t_ref, w_ref, dh_ref, loss_ref, dfw_ref):
        @pl.when(pl.program_id(0) == 0)
        def _():
            loss_ref[...] = jnp.zeros_like(loss_ref)
            dfw_ref[...] = jnp.zeros_like(dfw_ref)

        r, n = _rms(h_ref[...])
        wv = w_ref[...]
        err = n * wv - t_ref[...]
        loss_ref[...] += jnp.sum(err * err, keepdims=True) * (0.5 / D)
        dy = err * (1.0 / D)
        dfw_ref[...] += _colsum(dy * n)
        dh_ref[...] = _rms_bwd(r, n, dy * wv)

    blk = pl.BlockSpec((tb, D), lambda i: (i, 0))
    return pl.pallas_call(
        body, name="b_final", grid=(t // tb,), in_specs=[blk, blk, _row(D)], out_specs=[blk, _row(1), _row(D)],
        out_shape=[jax.ShapeDtypeStruct((t, D), F32), jax.ShapeDtypeStruct((1, 1), F32), jax.ShapeDtypeStruct((1, D), F32)],
        compiler_params=_cparams(1),
    )(h, tgt, fw)


def _eye(n):
    return jnp.eye(n, dtype=F32)


def _pool_embed(pool_w):
    return jnp.einsum('gcd,gk->gckd', pool_w, _eye(4)).reshape(GW, GW)


def _pool_extract(m):
    return jnp.einsum('gcgd->gcd', m.reshape(4, 64, 4, 64))


def _bmat_embed(bb):
    return jnp.einsum('gph,gk->ghkp', bb, _eye(16)).reshape(GW, S5_P)


def _bmat_extract(m):
    return jnp.einsum('ghgp->gph', m.reshape(16, 16, 16, 64))


def _cmat_embed(cc):
    return jnp.einsum('ghp,gk->kpgh', cc, _eye(16)).reshape(S5_P, GW)


def _cmat_extract(m):
    return jnp.einsum('gpgh->ghp', m.reshape(16, 64, 16, 16))


def _pad_lanes(v, n=DTW):
    return jnp.pad(v.reshape(1, -1), ((0, 0), (0, n - v.shape[-1])))


def _w_in_layout(w_in_t):
    w_main = jnp.concatenate([w_in_t[:1280], w_in_t[2052:2308], w_in_t[1280:2048]], axis=0)
    return w_main, jnp.pad(w_in_t[2048:2052], ((0, DTW - 4), (0, 0)))


def _layer_params(p, l, mod, w_in, rest):
    q = {'rest': rest}
    q['mod'] = [mod[k:k + 1] for k in range(6)]
    q['nw1'] = p['norm_mix_w'][l:l + 1]
    q['nw2'] = p['norm_mlp_w'][l:l + 1]
    q['w_main'], q['w_dt'] = _w_in_layout(w_in)
    q['pool_mat'] = _pool_embed(p['pool_w'][l]).astype(BF16)
    q['pool_scale'] = p['pool_scale'][l:l + 1]
    q['sconv_w'] = p['sconv_w'][l]
    q['conv_w'] = p['ssd_conv_w'][l]
    q['conv_b'] = p['ssd_conv_b'][l:l + 1]
    q['dt_bias'] = _pad_lanes(p['ssd_dt_bias'][l])
    q['a_log'] = _pad_lanes(p['ssd_a_log'][l])
    q['ssd_d'] = _pad_lanes(p['ssd_d'][l])
    q['s5_raw'] = (p['s5_a_re'][l], p['s5_a_im'][l], p['s5_log_step'][l].reshape(16, 1),
                   p['s5_b_re'][l].reshape(16, 1024), p['s5_b_im'][l].reshape(16, 1024))
    q['cre'] = _cmat_embed(p['s5_c_re'][l]).astype(BF16)
    q['cim'] = (-_cmat_embed(p['s5_c_im'][l])).astype(BF16)
    q['s5_d'] = p['s5_d'][l:l + 1]
    q['glu_w'] = p['s5_glu_w'][l].astype(BF16)
    q['glu_b'] = p['s5_glu_b'][l:l + 1]
    q['bw'] = p['branch_norm_w'][l:l + 1]
    return q


def _layer_fwd(h, q):
    sh1, sc1, g1, sh2, sc2, g2 = q['mod']
    t = h.shape[0]
    s = {'h': h}
    s['proj'], s['dtp'], s['u'] = _f_in(h, q['nw1'], sc1, sh1, q['w_main'], q['w_dt'])
    s['ya'], s['yb'] = _f_ab(s['proj'], q['pool_mat'], q['pool_scale'], q['sconv_w'])
    s['yc'], s['ypre'], s['sprev'] = _f_ssd(s['proj'], s['dtp'], q['conv_w'], q['conv_b'], q['dt_bias'], q['a_log'], q['ssd_d'])
    lr, li, bbr, bbi, ars, ais = _s5_prep(*q['s5_raw'])
    s['bmat'] = jnp.concatenate([_bmat_embed(bbr.reshape(16, 64, 16)), _bmat_embed(bbi.reshape(16, 64, 16))],
                                axis=1).astype(BF16)
    s['tables'] = _s5_tables(ars.reshape(1, S5_P), ais.reshape(1, S5_P))
    s['yd'], s['carries'], s['states'] = _f_s5(s['proj'], s['bmat'], q['cre'], q['cim'], s['tables'][0], s['tables'][1],
                                  q['s5_d'], q['glu_w'], q['glu_b'])
    q['w_out'], q['w1'], q['w2'] = q['rest']((s['ya'], s['yc'], s['yd']))
    s['h2'], s['o'], s['cat'] = _f_out(s['ya'], s['yb'], s['yc'], s['yd'], q['bw'], q['w_out'], h, g1)
    h3, s['m'], s['a'], s['v'] = _f_mlp(s['h2'], q['nw2'], sc2, sh2, g2, q['w1'], q['w2'])
    return h3, s


STACKED = {'mlp_w1': (2, 4, D, HID // 4), 'mlp_w2': (2, HID, D), 'w_out': (2, D, D)}


def _layer_bwd(dh3, q, s, l, stacked, early=None):
    sh1, sc1, g1, sh2, sc2, g2 = q['mod']
    g = {}
    dv, da, act, dm = _b_mlp(dh3, s['a'], g2, q['w1'], q['w2'])
    g['mlp_w1'] = _tn_matmul(s['v'], da, "dw1", col_major=True, into=stacked['mlp_w1'], layer=l)
    g['mlp_w2'] = _tn_matmul(act, dm, "dw2", into=stacked['mlp_w2'], layer=l)
    dh2, dsc2, dsh2, dnw2, dg2 = _b_normmod(dv, s['h2'], dh3, s['m'], q['nw2'], sc2, "b_norm_mlp")
    dya, dyb, dyc, dyd, do, dbw = _b_out(dh2, s['ya'], s['yb'], s['yc'], s['yd'], q['bw'], q['w_out'], g1)
    g['w_out'] = _tn_matmul(s['cat'], do, "dwout", into=stacked['w_out'], layer=l)
    g['branch_norm_w'] = dbw[0]
    if early is not None:
        zero = early(g)[0, 0]
        q = dict(q, pool_scale=q['pool_scale'] + zero, conv_b=q['conv_b'] + zero, s5_d=q['s5_d'] + zero)
    dab, dpm, dps, dsw = _b_ab(s['proj'], dya, dyb, q['pool_mat'], q['pool_scale'], q['sconv_w'])
    g['pool_w'] = _pool_extract(dpm)
    g['pool_scale'] = dps[0]
    g['sconv_w'] = dsw
    dz, dxbc, ddt, dcw, dcb, ddtb, dal, ddk = _b_ssd(s['proj'], s['dtp'], s['ypre'], dyc, s['sprev'], q['conv_w'],
                                                     q['conv_b'], q['dt_bias'], q['a_log'], q['ssd_d'])
    g['ssd_conv_w'] = dcw
    g['ssd_conv_b'] = dcb[0]
    g['ssd_dt_bias'] = ddtb[0, :4]
    g['ssd_a_log'] = dal[0, :4]
    g['ssd_d'] = ddk[0, :4]
    tb = s['tables']
    ds5, dbmat, dcre, dcim, dlam, dd5, dgw, dgb = _b_s5(s['proj'], dyd, s['carries'], s['states'], s['bmat'], q['cre'], q['cim'],
                                                        tb[0], tb[1], q['s5_d'], q['glu_w'], q['glu_b'])
    g['s5_c_re'] = _cmat_extract(dcre)
    g['s5_c_im'] = -_cmat_extract(dcim)
    g['s5_d'] = dd5[0]
    g['s5_glu_w'] = dgw
    g['s5_glu_b'] = dgb[0]
    dbbr = _bmat_extract(dbmat[:, :S5_P]).reshape(16, 1024)
    dbbi = _bmat_extract(dbmat[:, S5_P:]).reshape(16, 1024)
    dar, dai, dls, dbr, dbi = _s5_prep_bwd(*q['s5_raw'], dlam[0].reshape(16, 64), dlam[1].reshape(16, 64), dbbr, dbbi)
    g['s5_a_re'], g['s5_a_im'], g['s5_log_step'] = dar, dai, dls[:, 0]
    g['s5_b_re'], g['s5_b_im'] = dbr, dbi
    dh, dsc1, dsh1, dnw1, dg1 = _b_in(dab, dz, dxbc, ds5, ddt, q['w_main'], q['w_dt'], s['h'], dh2, s['o'], q['nw1'], sc1)
    u = s['u']
    head = jnp.concatenate([_tn_matmul(dab, u, "dwin_ab"), _tn_matmul(dz, u, "dwin_z"), _tn_matmul(dxbc, u, "dwin_xbc"),
                            _tn_matmul(ddt, u, "dwin_dt")[:8]], axis=0)
    full = lax.dynamic_update_slice(jnp.zeros((2308, D), F32), head, (0, 0))
    g['w_in'] = lax.dynamic_update_slice(full, _tn_matmul(ds5, u, "dwin_s5"), (2052, 0))
    g['norm_mix_w'] = dnw1[0]
    g['norm_mlp_w'] = dnw2[0]
    dmod = jnp.concatenate([dsh1, dsc1, dg1, dsh2, dsc2, dg2], axis=1)
    return dh, g, dmod


def _local_step(x, tgt, p, mod, w_in_of, rest_of, early=None):
    h = x
    qs, saved = [], []
    for l in range(2):
        qs.append(_layer_params(p, l, mod[l], w_in_of(l), functools.partial(rest_of, l)))
        h, s = _layer_fwd(h, qs[l])
        saved.append(s)
    dh, loss, dfw = _b_final(h, tgt, p['final_norm_w'].reshape(1, D))
    grads = [None, None]
    dmods = [None, None]
    dh, grads[1], dmods[1] = _layer_bwd(dh, qs[1], saved[1], 1, {k: lax.empty(shp, F32) for k, shp in STACKED.items()})
    dh, grads[0], dmods[0] = _layer_bwd(dh, qs[0], saved[0], 0, grads[1], early)
    out = {k: jnp.stack([grads[0][k], grads[1][k]]) for k in grads[0] if k not in STACKED}
    if early is None:
        out.update({k: grads[0][k] for k in STACKED})
    out['final_norm_w'] = dfw[0]
    return loss, dh, out, jnp.concatenate(dmods, axis=0)


def _shard_of(a, axis, k):
    n = a.shape[axis] // 4
    return lax.dynamic_slice_in_dim(a, k * n, n, axis)


def kernel(x, c, norm_mix_w, norm_mlp_w, ada_w, ada_b, w_in, pool_w, pool_scale, sconv_w, ssd_conv_w, ssd_conv_b, ssd_dt_bias, ssd_a_log, ssd_d, s5_a_re, s5_a_im, s5_log_step, s5_b_re, s5_b_im, s5_c_re, s5_c_im, s5_d, s5_glu_w, s5_glu_b, branch_norm_w, w_out, mlp_w1, mlp_w2, final_norm_w, loss_target, m_norm_mix_w, m_norm_mlp_w, m_ada_w, m_ada_b, m_w_in, m_pool_w, m_pool_scale, m_sconv_w, m_ssd_conv_w, m_ssd_conv_b, m_ssd_dt_bias, m_ssd_a_log, m_ssd_d, m_s5_a_re, m_s5_a_im, m_s5_log_step, m_s5_b_re, m_s5_b_im, m_s5_c_re, m_s5_c_im, m_s5_d, m_s5_glu_w, m_s5_glu_b, m_branch_norm_w, m_w_out, m_mlp_w1, m_mlp_w2, m_final_norm_w, v_norm_mix_w, v_norm_mlp_w, v_ada_w, v_ada_b, v_w_in, v_pool_w, v_pool_scale, v_sconv_w, v_ssd_conv_w, v_ssd_conv_b, v_ssd_dt_bias, v_ssd_a_log, v_ssd_d, v_s5_a_re, v_s5_a_im, v_s5_log_step, v_s5_b_re, v_s5_b_im, v_s5_c_re, v_s5_c_im, v_s5_d, v_s5_glu_w, v_s5_glu_b, v_branch_norm_w, v_w_out, v_mlp_w1, v_mlp_w2, v_final_norm_w):
    loc = locals()
    w = {n: loc[n] for n in WEIGHTS}
    mom = {n: loc['m_' + n] for n in WEIGHTS}
    var = {n: loc['v_' + n] for n in WEIGHTS}
    ix, iy, ic = lax.axis_index("x"), lax.axis_index("y"), lax.axis_index("c")
    chip = 2 * ix + iy
    dev = 4 * ix + 2 * iy + ic

    mine_of = lambda a: lax.dynamic_index_in_dim(a.astype(BF16), ic, axis=0, keepdims=False)
    pad_in = lambda a: jnp.pad(a.T, ((0, WIN_ROWS - 577), (0, 0)))
    shard = jnp.concatenate([pad_in(mine_of(w['w_in'])), mine_of(w['w_out']), mine_of(w['mlp_w1']), mine_of(w['mlp_w2'])], axis=0)

    (c_all,) = _exchange([c], EVERYONE, False, "ag_cond", stage=True)
    c_all = c_all.reshape(8, D)
    small_sh = _exchange([w[n] for n in SMALL_SHARDED], CHIPS, False, "ag_small")
    (w_in0,) = _exchange([pad_in(w['w_in'][0].astype(BF16))], CHIPS, False, "ag_win0")
    p = {n: w[n] for n in WEIGHTS if n not in BIG}
    for n, g in zip(SMALL_SHARDED, small_sh):
        ax = SMALL_SHARDED[n]
        p[n] = jnp.concatenate([g[k] for k in range(4)], axis=ax)

    def w_in_full(sh):
        return sh[:, :577].reshape(4 * 577, D)

    big = {}

    def fetch(after):
        if not big:
            (mine,), (got,) = _split_wait(sems, shard_thru, land, after, False, "ag_big_wait", per_core=True)
            got = lax.dynamic_update_slice(got, mine[None, None], (ic, chip, 0, 0))
            (both,) = _pair_swap([got.reshape(2, -1, D)], False, "swap_big", fill=True)
            big['both'] = both.reshape(got.shape)
        return big['both']

    def w_in_of(l):
        return w_in_full(w_in0) if l == 0 else w_in_full(fetch(None)[1])

    def rest_of(l, after):
        blk = fetch(after)[l]
        r0 = WIN_ROWS
        w_out_l = blk[:, r0:r0 + 256].reshape(D, D)
        w1_l = blk[:, r0 + 256:r0 + 1280]
        w2_l = blk[:, r0 + 1280:r0 + 2304].reshape(HID, D)
        return w_out_l, w1_l, w2_l

    ada_b_sh = _shard_of(w['ada_b'], 1, chip).reshape(2, 1, 6 * D // 4)
    mod_sh = _ada_fwd(c_all, w['ada_w'], ada_b_sh)
    (mod_all,) = _exchange([mod_sh], CHIPS, False, "ag_mod", stage=True)
    mine = lax.dynamic_index_in_dim(mod_all, dev, axis=2, keepdims=False)
    sems, shard_thru, land, token = _split_start([shard], [mod_all, w_in0] + small_sh, False, "ag_big_start", per_core=True)
    mod = jnp.transpose(mine, (1, 0, 2)).reshape(2, 6, D) + token[0, 0]

    layer = ic.astype(jnp.int32).reshape(1)
    flight = {}

    def early(g0):
        gws = [g0['w_out'].reshape(2, 4, 256, D), g0['mlp_w1'], g0['mlp_w2'].reshape(2, 4, 1024, D)]
        got = _pair_swap([a.reshape(2, -1, D) for a in gws], True, "swap_grad", narrow=True)
        pair = [_pair_sum(a, b.reshape(a.shape[1:]), layer, "pair_sum%d" % (k + 1), BF16) for k, (a, b) in enumerate(zip(gws, got))]
        flight['sems'], flight['srcs'], flight['lands'], token = _split_start(pair, [], True, "rs_start")
        return token

    loss, grad_x, g, dmod = _local_step(x[0], loss_target[0], p, mod, w_in_of, rest_of, early)

    (dmod_all,) = _exchange([dmod], EVERYONE, False, "ag_dmod", stage=True)
    dmod_all = jnp.transpose(dmod_all, (1, 0, 2))

    gw_in = jnp.pad(g['w_in'].reshape(2, 4, 577, D), ((0, 0), (0, 0), (0, WIN_ROWS - 577), (0, 0)))
    (got_in,) = _pair_swap([gw_in.reshape(2, -1, D)], True, "swap_grad_in", narrow=True)
    pair_in = _pair_sum(gw_in, got_in.reshape(gw_in.shape[1:]), layer, "pair_sum0", BF16)
    in_sems, in_srcs, in_lands, in_token = _split_start([pair_in], [dmod_all], True, "rs_in_start")

    def chip_sum(land, mine, name):
        own = lax.dynamic_index_in_dim(mine, chip, axis=0, keepdims=True)
        return _sum_lead(lax.dynamic_update_slice(land, own, (chip, 0, 0)), name, F32)

    sent, lands = _split_wait(flight['sems'], flight['srcs'], flight['lands'], [grad_x, in_token], True, "rs_wait")
    quad = [chip_sum(land, mine, "rs_chip_sum%d" % (k + 1)) for k, (land, mine) in enumerate(zip(lands, sent))]
    g_ada_w, g_ada_b = _ada_bwd(c_all, _shard_of(dmod_all, 2, chip), dmod_all)
    adam_ada_w = _adamw(w['ada_w'], g_ada_w, mom['ada_w'], var['ada_w'], "adamw_ada_w")
    (sent_in,), (land_in,) = _split_wait(in_sems, in_srcs, in_lands, quad + [adam_ada_w[0]], True, "rs_in_wait")
    quad = [chip_sum(land_in, sent_in, "rs_chip_sum0")] + quad
    halves = [lax.dynamic_update_slice(lax.empty((2,) + a.shape, F32), a[None], (ic, 0, 0)) for a in quad]
    both = _pair_swap(halves, False, "swap_red", fill=True)
    both[0] = jnp.transpose(both[0][:, :577], (0, 2, 1))
    red = dict(zip(('w_in', 'w_out', 'mlp_w1', 'mlp_w2'), both))
    red['ada_w'] = g_ada_w

    small_names = [n for n in WEIGHTS if n not in BIG and n != 'ada_b']
    pair_parts = _exchange([g[n] for n in small_names] + [loss], SIBLING, False, "ag_smallpair", stage=True)
    chip_parts = _exchange(_sum_many(pair_parts, "smallpair_sum"), CHIPS, False, "ag_smallgrad", stage=True)
    summed = _sum_many(chip_parts, "smallgrad_sum")
    for n, a in zip(small_names, summed[:-1]):
        a = a.reshape(w[n].shape) if n in ('s5_b_re', 's5_b_im') else a
        red[n] = _shard_of(a, SMALL_SHARDED[n], chip) if n in SMALL_SHARDED else a
    red['ada_b'] = g_ada_b
    loss_out = summed[-1].reshape(())

    delta, new_m, new_v = {}, {}, {}
    delta['ada_w'], new_m['ada_w'], new_v['ada_w'] = adam_ada_w
    for n in BIG[1:]:
        delta[n], new_m[n], new_v[n] = _adamw(w[n], red[n], mom[n], var[n], "adamw_" + n)
    rest = [n for n in WEIGHTS if n not in BIG]
    lanes = lambda n, a: a.reshape(2, 16, 1024) if n in ('s5_b_re', 's5_b_im') else a
    outs = _adamw_many(*[[lanes(n, src[n]) for n in rest] for src in (w, red, mom, var)], "adamw_small")
    for k, n in enumerate(rest):
        delta[n], new_m[n], new_v[n] = (outs[3 * k + j].reshape(w[n].shape) for j in range(3))

    return (loss_out, grad_x[None], *[red[n] for n in WEIGHTS], *[delta[n] for n in WEIGHTS],
            *[new_m[n] for n in WEIGHTS], *[new_v[n] for n in WEIGHTS])
```

```python
import functools
import math

import jax
import jax.numpy as jnp
from jax import lax
from jax.experimental import pallas as pl
from jax.experimental.pallas import tpu as pltpu

F32 = jnp.float32
BF16 = jnp.bfloat16
HI = lax.Precision.HIGHEST

D = 1024
GW = 256
HID = 4096
EPS = 1e-6
PW = 2304
DTW = 128
SSD_L = 128
SSD_SUB = 2
SSD_SUB_BWD = 2
NH, HP, NS = 4, 64, 128
S5_P = 1024
MESH = pl.DeviceIdType.MESH

ADAM_LR, ADAM_B1, ADAM_B2, ADAM_EPS, ADAM_WD, ADAM_STEP = 0.001, 0.9, 0.999, 1e-08, 0.01, 10

NT = (((1,), (1,)), ((), ()))
TN = (((0,), (0,)), ((), ()))

WEIGHTS = ['norm_mix_w', 'norm_mlp_w', 'ada_w', 'ada_b', 'w_in', 'pool_w', 'pool_scale', 'sconv_w', 'ssd_conv_w',
           'ssd_conv_b', 'ssd_dt_bias', 'ssd_a_log', 'ssd_d', 's5_a_re', 's5_a_im', 's5_log_step', 's5_b_re', 's5_b_im',
           's5_c_re', 's5_c_im', 's5_d', 's5_glu_w', 's5_glu_b', 'branch_norm_w', 'w_out', 'mlp_w1', 'mlp_w2',
           'final_norm_w']
BIG = ('ada_w', 'w_in', 'w_out', 'mlp_w1', 'mlp_w2')
SMALL_SHARDED = {'sconv_w': 2, 'ssd_conv_w': 2, 's5_glu_w': 1}


def _cparams(n_axes, vmem_mb=48):
    return pltpu.CompilerParams(dimension_semantics=("arbitrary",) * n_axes, vmem_limit_bytes=vmem_mb * 1024 * 1024)


def _row(n):
    return pl.BlockSpec((1, n), lambda *_: (0, 0))


def _full(shape):
    nd = len(shape)
    return pl.BlockSpec(tuple(shape), lambda *_: (0,) * nd)


def _dot(a, b, dims=None, prec=None):
    if dims is None:
        dims = (((a.ndim - 1,), (0,)), ((), ()))
    return lax.dot_general(a, b, dims, preferred_element_type=F32, precision=prec)


def _bdot(a, b, dims=None):
    return _dot(a.astype(BF16), b.astype(BF16), dims)


def _sig(x):
    return jax.nn.sigmoid(x)


def _silu(x):
    return x * _sig(x)


def _dsilu(x):
    s = _sig(x)
    return s * (1.0 + x * (1.0 - s))


def _softplus(x):
    return jnp.maximum(x, 0.0) + jnp.log(1.0 + jnp.exp(-jnp.abs(x)))


_GK = math.sqrt(2.0 / math.pi)


def _gelu(x):
    return 0.5 * x * (1.0 + jnp.tanh(_GK * (x + 0.044715 * x * x * x)))


def _dgelu(x):
    th = jnp.tanh(_GK * (x + 0.044715 * x * x * x))
    return 0.5 * (1.0 + th) + 0.5 * x * (1.0 - th * th) * _GK * (1.0 + 3.0 * 0.044715 * x * x)


def _colsum(x):
    return jnp.sum(x, axis=0, keepdims=True)


def _rms(x):
    r = lax.rsqrt(jnp.mean(x * x, axis=-1, keepdims=True) + EPS)
    return r, x * r


def _rms_bwd(r, n, dn):
    return r * (dn - n * jnp.mean(dn * n, axis=-1, keepdims=True))


def _roll(x, k):
    n = x.shape[0]
    k = k % n
    return x if k == 0 else pltpu.roll(x, k, axis=0)


def _tblock(t, want=512):
    return min(t, want)


def _peer(mask):
    x, y, c = lax.axis_index("x"), lax.axis_index("y"), lax.axis_index("c")
    return (x ^ ((mask >> 2) & 1), y ^ ((mask >> 1) & 1), c ^ (mask & 1))


def _group_index(masks):
    x, y, c = lax.axis_index("x"), lax.axis_index("y"), lax.axis_index("c")
    full = 0
    for m in masks:
        full |= m
    bits = [b for b in (4, 2, 1) if full & b]

    def idx(px, py, pc):
        v = {4: px, 2: py, 1: pc}
        out = 0
        for b in bits:
            out = out * 2 + v[b]
        return out

    return idx(x, y, c), [idx(*_peer(m)) for m in masks]


def _exchange(arrs, masks, scatter, name, stage=False):
    n_arr, n_peer, n_grp = len(arrs), len(masks), len(masks) + 1

    def body(*refs):
        ins, outs = refs[:n_arr], refs[n_arr:2 * n_arr]
        send_sems, recv_sems, local_sems = refs[2 * n_arr:2 * n_arr + 3]
        if stage:
            bufs, load_sems = refs[2 * n_arr + 3:3 * n_arr + 3], refs[3 * n_arr + 3]
            loads = [pltpu.make_async_copy(ins[t], bufs[t], load_sems.at[t]) for t in range(n_arr)]
            for ld in loads:
                ld.start()
            for ld in loads:
                ld.wait()
            ins = bufs
        me, peer_idx = _group_index(masks)
        copies = []
        for t in range(n_arr):
            src_me = ins[t].at[me] if scatter else ins[t]
            loc = pltpu.make_async_copy(src_me, outs[t].at[me], local_sems.at[t])
            loc.start()
            copies.append(loc)
            for j, m in enumerate(masks):
                src = ins[t].at[peer_idx[j]] if scatter else ins[t]
                cp = pltpu.make_async_remote_copy(src_ref=src, dst_ref=outs[t].at[me], send_sem=send_sems.at[t, j],
                                                  recv_sem=recv_sems.at[t, j], device_id=_peer(m), device_id_type=MESH)
                cp.start()
                copies.append(cp)
        for cp in copies:
            cp.wait()

    hbm = pl.BlockSpec(memory_space=pl.ANY)
    out_shape = [jax.ShapeDtypeStruct((n_grp,) + (a.shape[1:] if scatter else a.shape), a.dtype) for a in arrs]
    staging = [pltpu.VMEM(a.shape, a.dtype) for a in arrs] + [pltpu.SemaphoreType.DMA((n_arr,))] if stage else []
    outs = pl.pallas_call(
        body, name=name, in_specs=[hbm] * n_arr, out_specs=[hbm] * n_arr, out_shape=out_shape,
        scratch_shapes=[pltpu.SemaphoreType.DMA((n_arr, n_peer)), pltpu.SemaphoreType.DMA((n_arr, n_peer)),
                        pltpu.SemaphoreType.DMA((n_arr,))] + staging,
        compiler_params=pltpu.CompilerParams(vmem_limit_bytes=48 * 1024 * 1024),
    )(*arrs)
    return list(outs)


def _split_copies(src_refs, land_refs, sems, scatter, per_core):
    me, peer_idx = _group_index(CHIPS)
    n = len(CHIPS) * len(src_refs)
    copies = []
    for t, (src_ref, land_ref) in enumerate(zip(src_refs, land_refs)):
        zone = land_ref.at[lax.axis_index("c")] if per_core else land_ref
        for j, m in enumerate(CHIPS):
            k = len(CHIPS) * t + j
            copies.append(pltpu.make_async_remote_copy(
                src_ref=src_ref.at[peer_idx[j]] if scatter else src_ref, dst_ref=zone.at[me], send_sem=sems[k],
                recv_sem=sems[n + k], device_id=_peer(m), device_id_type=MESH))
    return copies


def _split_start(srcs, after, scatter, name, per_core=False):
    n_arr, n_sem = len(srcs), 2 * len(CHIPS) * len(srcs)

    def body(*refs):
        src_refs, land_refs = refs[:n_arr], refs[n_arr:2 * n_arr]
        outs = refs[2 * n_arr + len(after):]
        for cp in _split_copies(src_refs, land_refs, outs[:n_sem], scatter, per_core):
            cp.start()
        outs[-1][...] = jnp.zeros_like(outs[-1])

    hbm = pl.BlockSpec(memory_space=pltpu.HBM)
    sem = pl.BlockSpec(memory_space=pltpu.SEMAPHORE)
    lands = [lax.empty(((2,) if per_core else ()) + (len(CHIPS) + 1,) + (a.shape[1:] if scatter else a.shape), a.dtype)
             for a in srcs]
    as_hbm = lambda a: pltpu.with_memory_space_constraint(a, pltpu.HBM)
    outs = pl.pallas_call(
        body, name=name,
        out_shape=(pltpu.SemaphoreType.DMA(()),) * n_sem + tuple(pltpu.HBM(a.shape, a.dtype) for a in srcs + lands)
        + (jax.ShapeDtypeStruct((8, 128), F32),),
        in_specs=(hbm,) * (2 * n_arr) + (pl.BlockSpec(memory_space=pl.ANY),) * len(after),
        out_specs=(sem,) * n_sem + (hbm,) * (2 * n_arr) + (pl.BlockSpec(memory_space=pltpu.VMEM),),
        input_output_aliases={t: n_sem + t for t in range(2 * n_arr)},
        compiler_params=pltpu.CompilerParams(has_side_effects=pltpu.SideEffectType.DATAFLOW_SIDE_EFFECTING),
    )(*[as_hbm(a) for a in srcs + lands], *after)
    return outs[:n_sem], list(outs[n_sem:n_sem + n_arr]), list(outs[n_sem + n_arr:n_sem + 2 * n_arr]), outs[-1]


def _split_wait(sems, srcs, lands, after, scatter, name, per_core=False):
    n_arr, n_sem = len(srcs), len(sems)

    def body(*refs):
        src_refs, land_refs = refs[:n_arr], refs[n_arr:2 * n_arr]
        for cp in _split_copies(src_refs, land_refs, refs[2 * n_arr:2 * n_arr + n_sem], scatter, per_core):
            cp.wait_send()
            cp.wait_recv()

    hbm = pl.BlockSpec(memory_space=pltpu.HBM)
    sem = pl.BlockSpec(memory_space=pltpu.SEMAPHORE)
    outs = pl.pallas_call(
        body, name=name, out_shape=tuple(pltpu.HBM(a.shape, a.dtype) for a in srcs + lands),
        in_specs=(hbm,) * (2 * n_arr) + (sem,) * n_sem + (pl.BlockSpec(memory_space=pl.ANY),) * len(after),
        out_specs=(hbm,) * (2 * n_arr), input_output_aliases={t: t for t in range(2 * n_arr)},
        compiler_params=pltpu.CompilerParams(has_side_effects=pltpu.SideEffectType.DATAFLOW_SIDE_EFFECTING),
    )(*srcs, *lands, *sems, *after)
    return list(outs[:n_arr]), list(outs[n_arr:])


CHIPS = (4, 2, 6)
EVERYONE = (1, 2, 3, 4, 5, 6, 7)
SIBLING = (1,)
SWAP_ROWS = 512
WIN_ROWS = 592


def _pair_swap(arrs, other_layer, name, narrow=False, fill=False):
    assert not (fill and (other_layer or narrow))
    n_arr = len(arrs)
    shapes = [a.shape[-2:] for a in arrs]
    out_dtypes = [BF16 if narrow else a.dtype for a in arrs]
    chunks = []
    for t, (rows, _) in enumerate(shapes):
        assert rows % 16 == 0
        for j, r0 in enumerate(range(0, rows, SWAP_ROWS)):
            chunks.append((t, r0, min(SWAP_ROWS, rows - r0), j % 2))

    def body(*refs):
        ins, outs = refs[:n_arr], refs[n_arr:2 * n_arr]
        bufs = refs[2 * n_arr:3 * n_arr]
        out_bufs = refs[3 * n_arr:4 * n_arr] if narrow else bufs
        load_sems, send_sems, recv_sems = refs[-3:]
        sibling = _peer(1)
        c = lax.axis_index("c")

        def load(k):
            t, r0, n, slot = chunks[k]
            src = ins[t].at[1 - c] if other_layer else ins[t].at[c] if fill else ins[t]
            return pltpu.make_async_copy(src.at[pl.ds(r0, n)], bufs[t].at[slot, pl.ds(0, n)], load_sems.at[t, slot])

        def send(k):
            t, r0, n, slot = chunks[k]
            dst = outs[t].at[c] if fill else outs[t]
            return pltpu.make_async_remote_copy(src_ref=out_bufs[t].at[slot, pl.ds(0, n)], dst_ref=dst.at[pl.ds(r0, n)],
                                                send_sem=send_sems.at[t, slot], recv_sem=recv_sems.at[t],
                                                device_id=sibling, device_id_type=MESH)

        in_flight = {}

        def drain(k):
            key = (chunks[k][0], chunks[k][3])
            if key in in_flight:
                send(in_flight.pop(key)).wait_send()

        def start_load(k):
            if not narrow:
                drain(k)
            load(k).start()

        start_load(0)
        for k in range(len(chunks)):
            t, _, n, slot = chunks[k]
            load(k).wait()
            if k + 1 < len(chunks):
                start_load(k + 1)
            if narrow:
                drain(k)
                out_bufs[t][slot, pl.ds(0, n), :] = bufs[t][slot, pl.ds(0, n), :].astype(BF16)
            send(k).start()
            in_flight[(t, slot)] = k
        for k in in_flight.values():
            send(k).wait_send()
        for t in range(n_arr):
            landed = outs[t].at[1 - c] if fill else outs[t]
            pltpu.make_async_remote_copy(src_ref=landed, dst_ref=landed, send_sem=send_sems.at[t, 0],
                                         recv_sem=recv_sems.at[t], device_id=sibling, device_id_type=MESH).wait_recv()

    hbm = pl.BlockSpec(memory_space=pl.ANY)
    outs = pl.pallas_call(
        body, name=name, in_specs=[hbm] * n_arr, out_specs=[hbm] * n_arr,
        out_shape=[jax.ShapeDtypeStruct(a.shape if fill else s, dt) for a, s, dt in zip(arrs, shapes, out_dtypes)],
        input_output_aliases={t: t for t in range(n_arr)} if fill else {},
        scratch_shapes=[pltpu.VMEM((2, min(SWAP_ROWS, s[0]), s[1]), a.dtype) for s, a in zip(shapes, arrs)]
        + ([pltpu.VMEM((2, min(SWAP_ROWS, s[0]), s[1]), BF16) for s in shapes] if narrow else [])
        + [pltpu.SemaphoreType.DMA((n_arr, 2)), pltpu.SemaphoreType.DMA((n_arr, 2)), pltpu.SemaphoreType.DMA((n_arr,))],
        compiler_params=pltpu.CompilerParams(vmem_limit_bytes=48 * 1024 * 1024),
    )(*arrs)
    return list(outs)


def _sum_lead(a, name, out_dtype):
    n = a.shape[0]
    shape = a.shape[1:]

    def body(a_ref, o_ref):
        acc = a_ref[0].astype(F32)
        for k in range(1, n):
            acc = acc + a_ref[k].astype(F32)
        o_ref[...] = acc.astype(out_dtype)

    if len(shape) == 3:
        blk = (1,) + shape[1:]
        return pl.pallas_call(
            body, name=name, grid=(shape[0],), in_specs=[pl.BlockSpec((n,) + blk, lambda i: (0, i, 0, 0))],
            out_specs=pl.BlockSpec(blk, lambda i: (i, 0, 0)), out_shape=jax.ShapeDtypeStruct(shape, out_dtype),
            compiler_params=_cparams(1),
        )(a)
    rows, cols = shape
    rb = rows
    for cand in (512, 256, 128):
        if rows % cand == 0 and rows > cand:
            rb = cand
            break
    return pl.pallas_call(
        body, name=name, grid=(rows // rb,), in_specs=[pl.BlockSpec((n, rb, cols), lambda i: (0, i, 0))],
        out_specs=pl.BlockSpec((rb, cols), lambda i: (i, 0)), out_shape=jax.ShapeDtypeStruct((rows, cols), out_dtype),
        compiler_params=_cparams(1),
    )(a)


def _pair_sum(g, recv, layer, name, out_dtype):
    _, n, r, c = g.shape

    def body(l_ref, g_ref, r_ref, o_ref):
        o_ref[...] = (g_ref[0].astype(F32) + r_ref[...].astype(F32)).astype(out_dtype)

    return pl.pallas_call(
        body, name=name,
        grid_spec=pltpu.PrefetchScalarGridSpec(
            num_scalar_prefetch=1, grid=(n,),
            in_specs=[pl.BlockSpec((1, 1, r, c), lambda i, l: (l[0], i, 0, 0)), pl.BlockSpec((1, r, c), lambda i, l: (i, 0, 0))],
            out_specs=pl.BlockSpec((1, r, c), lambda i, l: (i, 0, 0))),
        out_shape=jax.ShapeDtypeStruct((n, r, c), out_dtype), compiler_params=_cparams(1),
    )(layer, g, recv)


def _tn_matmul(a, b, name, col_major=False, into=None, layer=0):
    t, k = a.shape
    n = b.shape[1]
    tb = _tblock(t, 1024)
    kb = min(k, 1024)
    nb = min(n, 1024)
    grid = (k // kb, n // nb, t // tb)
    lead = (into is not None) + col_major

    def body(a_ref, b_ref, *rest):
        o_ref = rest[-1]
        for _ in range(lead):
            o_ref = o_ref.at[0]

        @pl.when(pl.program_id(2) == 0)
        def _():
            o_ref[...] = jnp.zeros_like(o_ref)

        o_ref[...] += _bdot(a_ref[...], b_ref[...], TN)

    if col_major:
        block, index, shape = (1, kb, nb), (lambda ki, ni: (ni, ki, 0)), (n // nb, k, nb)
    else:
        block, index, shape = (kb, nb), (lambda ki, ni: (ki, ni)), (k, n)
    in_specs = [pl.BlockSpec((tb, kb), lambda ki, ni, ti: (ti, ki)), pl.BlockSpec((tb, nb), lambda ki, ni, ti: (ti, ni))]
    if into is None:
        return pl.pallas_call(
            body, name=name, grid=grid, in_specs=in_specs, out_specs=pl.BlockSpec(block, lambda ki, ni, ti: index(ki, ni)),
            out_shape=jax.ShapeDtypeStruct(shape, F32), compiler_params=_cparams(3),
        )(a, b)
    assert into.shape == (2,) + shape
    return pl.pallas_call(
        body, name=name, grid=grid, in_specs=in_specs + [pl.BlockSpec(memory_space=pl.ANY)],
        out_specs=pl.BlockSpec((1,) + block, lambda ki, ni, ti: (layer,) + index(ki, ni)),
        out_shape=jax.ShapeDtypeStruct(into.shape, F32), input_output_aliases={2: 0}, compiler_params=_cparams(3),
    )(a, b, into)


def _sum_many(arrs, name):
    k = len(arrs)

    def body(*refs):
        for a_ref, o_ref in zip(refs[:k], refs[k:]):
            acc = a_ref[0]
            for j in range(1, a_ref.shape[0]):
                acc = acc + a_ref[j]
            o_ref[...] = acc

    return pl.pallas_call(body, name=name, grid=(1,), in_specs=[_full(a.shape) for a in arrs],
                          out_specs=[_full(a.shape[1:]) for a in arrs],
                          out_shape=[jax.ShapeDtypeStruct(a.shape[1:], F32) for a in arrs], compiler_params=_cparams(1))(*arrs)


def _adamw_math(w, g, m, v):
    m2 = ADAM_B1 * m + (1.0 - ADAM_B1) * g
    v2 = ADAM_B2 * v + (1.0 - ADAM_B2) * (g * g)
    m_hat = m2 / (1.0 - ADAM_B1 ** ADAM_STEP)
    v_hat = v2 / (1.0 - ADAM_B2 ** ADAM_STEP)
    return -ADAM_LR * (m_hat / (jnp.sqrt(v_hat) + ADAM_EPS) + ADAM_WD * w), m2, v2


def _adamw_many(ws, gs, ms, vs, name):
    n = len(ws)

    def body(*refs):
        ins, outs = refs[:4 * n], refs[4 * n:]
        for k in range(n):
            res = _adamw_math(ins[k][...], ins[n + k][...], ins[2 * n + k][...], ins[3 * n + k][...])
            for j in range(3):
                outs[3 * k + j][...] = res[j]

    out_shape = []
    for a in ws:
        out_shape += [jax.ShapeDtypeStruct(a.shape, F32)] * 3
    return pl.pallas_call(body, name=name, grid=(1,), in_specs=[_full(a.shape) for a in ws] * 4,
                          out_specs=[_full(s.shape) for s in out_shape], out_shape=out_shape,
                          compiler_params=_cparams(1))(*ws, *gs, *ms, *vs)


def _adamw(w, g, m, v, name):
    shape = w.shape
    cols = shape[-1]
    rows = int(math.prod(shape[:-1]))
    rb = rows
    for cand in (256, 128, 64, 32, 16, 8):
        if rows % cand == 0 and rows > cand:
            rb = cand
            break
    bc1 = 1.0 - ADAM_B1 ** ADAM_STEP
    bc2 = 1.0 - ADAM_B2 ** ADAM_STEP

    def body(w_ref, g_ref, m_ref, v_ref, d_ref, nm_ref, nv_ref):
        gg = g_ref[...]
        m2 = ADAM_B1 * m_ref[...] + (1.0 - ADAM_B1) * gg
        v2 = ADAM_B2 * v_ref[...] + (1.0 - ADAM_B2) * (gg * gg)
        m_hat = m2 / bc1
        v_hat = v2 / bc2
        d_ref[...] = -ADAM_LR * (m_hat / (jnp.sqrt(v_hat) + ADAM_EPS) + ADAM_WD * w_ref[...])
        nm_ref[...] = m2
        nv_ref[...] = v2

    spec = pl.BlockSpec((rb, cols), lambda i: (i, 0))
    sds = jax.ShapeDtypeStruct((rows, cols), F32)
    outs = pl.pallas_call(
        body, name=name, grid=(rows // rb,), in_specs=[spec] * 4, out_specs=[spec] * 3, out_shape=[sds] * 3,
        compiler_params=_cparams(1),
    )(*(z.reshape(rows, cols) for z in (w, g, m, v)))
    return tuple(o.reshape(shape) for o in outs)


def _ada_fwd(c_all, ada_w_sh, ada_b_sh):
    s = ada_w_sh.shape[2]
    sb = 512

    def body(c_ref, w_ref, b_ref, o_ref):
        cond = _silu(c_ref[...])
        o_ref[0] = _bdot(cond, w_ref[0]) + b_ref[0]

    return pl.pallas_call(
        body, name="ada_fwd", grid=(2, s // sb),
        in_specs=[_full((8, D)), pl.BlockSpec((1, D, sb), lambda l, j: (l, 0, j)), pl.BlockSpec((1, 1, sb), lambda l, j: (l, 0, j))],
        out_specs=pl.BlockSpec((1, 8, sb), lambda l, j: (l, 0, j)), out_shape=jax.ShapeDtypeStruct((2, 8, s), F32),
        compiler_params=_cparams(2),
    )(c_all, ada_w_sh, ada_b_sh)


def _ada_bwd(c_all, dmod_sh, dmod_all):
    s = dmod_sh.shape[2]
    sb = 512

    def body(c_ref, d_ref, o_ref):
        cond = _silu(c_ref[...])
        o_ref[0] = _bdot(cond, d_ref[0], TN)

    gw = pl.pallas_call(
        body, name="ada_bwd_w", grid=(2, s // sb),
        in_specs=[_full((8, D)), pl.BlockSpec((1, 8, sb), lambda l, j: (l, 0, j))],
        out_specs=pl.BlockSpec((1, D, sb), lambda l, j: (l, 0, j)), out_shape=jax.ShapeDtypeStruct((2, D, s), F32),
        compiler_params=_cparams(2),
    )(c_all, dmod_sh)

    def body_b(d_ref, o_ref):
        acc = d_ref[0, 0:1, :]
        for k in range(1, 8):
            acc = acc + d_ref[0, k:k + 1, :]
        o_ref[0] = acc

    gb = pl.pallas_call(
        body_b, name="ada_bwd_b", grid=(2,), in_specs=[pl.BlockSpec((1, 8, 6 * D), lambda l: (l, 0, 0))],
        out_specs=pl.BlockSpec((1, 1, 6 * D), lambda l: (l, 0, 0)), out_shape=jax.ShapeDtypeStruct((2, 1, 6 * D), F32),
        compiler_params=_cparams(1),
    )(dmod_all)
    return gw, gb.reshape(2, 6 * D)


def _f_in(h, nw, sc, sh, w_main, w_dt):
    t = h.shape[0]
    tb = _tblock(t)

    def body(h_ref, nw_ref, sc_ref, sh_ref, w_ref, wd_ref, p_ref, dt_ref, u_ref):
        _, n = _rms(h_ref[...])
        u = ((n * nw_ref[...]) * (1.0 + sc_ref[...]) + sh_ref[...]).astype(BF16)
        u_ref[...] = u
        p_ref[...] = _dot(u, w_ref[...], NT)
        dt_ref[...] = _dot(u, wd_ref[...], NT)

    return pl.pallas_call(
        body, name="f_in", grid=(t // tb,),
        in_specs=[pl.BlockSpec((tb, D), lambda i: (i, 0)), _row(D), _row(D), _row(D), _full((PW, D)), _full((DTW, D))],
        out_specs=[pl.BlockSpec((tb, PW), lambda i: (i, 0)), pl.BlockSpec((tb, DTW), lambda i: (i, 0)),
                   pl.BlockSpec((tb, D), lambda i: (i, 0))],
        out_shape=[jax.ShapeDtypeStruct((t, PW), F32), jax.ShapeDtypeStruct((t, DTW), F32), jax.ShapeDtypeStruct((t, D), BF16)],
        compiler_params=_cparams(1),
    )(h, nw, sc, sh, w_main, w_dt)


def _norm_bwd_step(du_v, x, dres_v, gated, nwv, scv, dx_ref, dsc_ref, dsh_ref, dnw_ref, dg_ref):
    r, n = _rms(x)
    scale = 1.0 + scv
    dsc_ref[...] += _colsum(du_v * (n * nwv))
    dsh_ref[...] += _colsum(du_v)
    dnw_ref[...] += _colsum(du_v * scale * n)
    dg_ref[...] += _colsum(dres_v * gated)
    dx_ref[...] = dres_v + _rms_bwd(r, n, du_v * scale * nwv)


def _b_in(dab, dz, dxbc, ds5, ddt, w_main, w_dt, x, dres, gated, nw, sc):
    t = dab.shape[0]
    tb = _tblock(t)

    def body(a_ref, z_ref, x_ref, s_ref, d_ref, w_ref, wd_ref, h_ref, dr_ref, g_ref, nw_ref, sc_ref,
             dx_ref, dsc_ref, dsh_ref, dnw_ref, dg_ref):
        @pl.when(pl.program_id(0) == 0)
        def _():
            for r in (dsc_ref, dsh_ref, dnw_ref, dg_ref):
                r[...] = jnp.zeros_like(r)

        du = _bdot(a_ref[...], w_ref[0:1024, :])
        du += _bdot(z_ref[...], w_ref[1024:1280, :])
        du += _bdot(s_ref[...], w_ref[1280:1536, :])
        du += _bdot(x_ref[...], w_ref[1536:2304, :])
        du += _bdot(d_ref[...], wd_ref[...])
        _norm_bwd_step(du, h_ref[...], dr_ref[...], g_ref[...], nw_ref[...], sc_ref[...], dx_ref, dsc_ref, dsh_ref, dnw_ref, dg_ref)

    blk = lambda n: pl.BlockSpec((tb, n), lambda i: (i, 0))
    row = jax.ShapeDtypeStruct((1, D), F32)
    return pl.pallas_call(
        body, name="b_in", grid=(t // tb,),
        in_specs=[blk(1024), blk(256), blk(768), blk(256), blk(DTW), _full((PW, D)), _full((DTW, D)),
                  blk(D), blk(D), blk(D), _row(D), _row(D)],
        out_specs=[blk(D), _row(D), _row(D), _row(D), _row(D)],
        out_shape=[jax.ShapeDtypeStruct((t, D), F32), row, row, row, row], compiler_params=_cparams(1),
    )(dab, dz, dxbc, ds5, ddt, w_main, w_dt, x, dres, gated, nw, sc)


def _b_normmod(du, x, dres, gated, nw, sc, name):
    t = x.shape[0]
    tb = _tblock(t)

    def body(du_ref, x_ref, dr_ref, g_ref, nw_ref, sc_ref, dx_ref, dsc_ref, dsh_ref, dnw_ref, dg_ref):
        @pl.when(pl.program_id(0) == 0)
        def _():
            for r in (dsc_ref, dsh_ref, dnw_ref, dg_ref):
                r[...] = jnp.zeros_like(r)

        _norm_bwd_step(du_ref[...], x_ref[...], dr_ref[...], g_ref[...], nw_ref[...], sc_ref[...],
                       dx_ref, dsc_ref, dsh_ref, dnw_ref, dg_ref)

    blk = pl.BlockSpec((tb, D), lambda i: (i, 0))
    row = jax.ShapeDtypeStruct((1, D), F32)
    return pl.pallas_call(
        body, name=name, grid=(t // tb,), in_specs=[blk, blk, blk, blk, _row(D), _row(D)],
        out_specs=[blk, _row(D), _row(D), _row(D), _row(D)], out_shape=[jax.ShapeDtypeStruct((t, D), F32), row, row, row, row],
        compiler_params=_cparams(1),
    )(du, x, dres, gated, nw, sc)


HALO = 16


def _lane_group(shape):
    return lax.broadcasted_iota(jnp.int32, shape, 1) // 64


def _window_select(g, s2, s4, s8, s16):
    return jnp.where(g == 0, s2, jnp.where(g == 1, s4, jnp.where(g == 2, s8, s16)))


def _pool_count(t0, rows):
    g = _lane_group((rows, GW))
    win = _window_select(g, 2, 4, 8, 16)
    tt = t0 + lax.broadcasted_iota(jnp.int32, (rows, GW), 0)
    return jnp.minimum(tt + 1, win).astype(F32)


def _pool_p(v_ext, t0, tb):
    s2 = v_ext + _roll(v_ext, 1)
    s4 = s2 + _roll(s2, 2)
    s8 = s4 + _roll(s4, 4)
    s16 = s8 + _roll(s8, 8)
    ws = _window_select(_lane_group(v_ext.shape), s2, s4, s8, s16)[HALO:]
    return ws / _pool_count(t0, tb) - v_ext[HALO:]


def _sconv(q_ext, w):
    return (_roll(q_ext, 2) * w[0:1] + _roll(q_ext, 1) * w[1:2] + q_ext * w[2:3])[HALO:]


def _halo_specs(t, tb, cols, col_block):
    per = tb // HALO
    last = t // HALO - 1
    prev = pl.BlockSpec((HALO, cols), lambda i: (jnp.maximum(i * per - 1, 0), col_block))
    nxt = pl.BlockSpec((HALO, cols), lambda i: (jnp.minimum((i + 1) * per, last), col_block))
    return prev, nxt


def _f_ab(proj, pool_mat, pool_scale, sconv_w):
    t = proj.shape[0]
    tb = _tblock(t)
    prev, _ = _halo_specs(t, tb, 1024, 0)

    def body(p_ref, h_ref, pm_ref, ps_ref, sw_ref, ya_ref, yb_ref):
        i = pl.program_id(0)
        halo = jnp.where(i > 0, h_ref[...], 0.0)
        ext = jnp.concatenate([halo, p_ref[...]], axis=0)
        p = _pool_p(ext[:, 0:256], i * tb, tb)
        ya_ref[...] = _bdot(p, pm_ref[...]) * ps_ref[...]
        q_ext = ext[:, 512:768] * ext[:, 768:1024]
        yb_ref[...] = p_ref[:, 256:512] * _sconv(q_ext, sw_ref[...])

    blk = pl.BlockSpec((tb, GW), lambda i: (i, 0))
    sds = jax.ShapeDtypeStruct((t, GW), F32)
    return pl.pallas_call(
        body, name="f_ab", grid=(t // tb,),
        in_specs=[pl.BlockSpec((tb, 1024), lambda i: (i, 0)), prev, _full((GW, GW)), _row(GW), _full((3, GW))],
        out_specs=[blk, blk], out_shape=[sds, sds], compiler_params=_cparams(1),
    )(proj, proj, pool_mat, pool_scale, sconv_w)


def _b_ab(proj, dya, dyb, pool_mat, pool_scale, sconv_w):
    t = proj.shape[0]
    tb = _tblock(t)
    nb = t // tb
    prev, nxt = _halo_specs(t, tb, 1024, 0)
    _, nxt_g = _halo_specs(t, tb, GW, 0)
    n_ext = tb + HALO

    def body(p_ref, hp_ref, hn_ref, da_ref, dan_ref, db_ref, dbn_ref, pm_ref, ps_ref, sw_ref,
             o_ref, dpm_ref, dps_ref, dsw_ref):
        i = pl.program_id(0)

        @pl.when(i == 0)
        def _():
            for r in (dpm_ref, dps_ref, dsw_ref):
                r[...] = jnp.zeros_like(r)

        last = i == nb - 1
        halo = jnp.where(i > 0, hp_ref[...], 0.0)
        main = p_ref[...]
        ext = jnp.concatenate([halo, main], axis=0)
        scale = ps_ref[...]
        pm = pm_ref[...]
        p = _pool_p(ext[:, 0:256], i * tb, tb)
        da = da_ref[...]
        dps_ref[...] += _colsum(da * _bdot(p, pm))
        da_ext = jnp.concatenate([da, jnp.where(last, 0.0, dan_ref[...])], axis=0)
        dys = da_ext * scale
        dpm_ref[...] += _bdot(p, dys[:tb], TN)
        dp = _bdot(dys, pm, NT)
        dpc = dp / _pool_count(i * tb, n_ext)
        a2 = dpc + _roll(dpc, n_ext - 1)
        a4 = a2 + _roll(a2, n_ext - 2)
        a8 = a4 + _roll(a4, n_ext - 4)
        a16 = a8 + _roll(a8, n_ext - 8)
        o_ref[:, 0:256] = (_window_select(_lane_group(dpc.shape), a2, a4, a8, a16) - dp)[:tb]
        w = sw_ref[...]
        gb, gc, hh = main[:, 256:512], main[:, 512:768], main[:, 768:1024]
        q_ext = ext[:, 512:768] * ext[:, 768:1024]
        db = db_ref[...]
        o_ref[:, 256:512] = db * _sconv(q_ext, w)
        gb_next = hn_ref[:, 256:512]
        dconv = jnp.concatenate([db * gb, jnp.where(last, 0.0, dbn_ref[...] * gb_next)], axis=0)
        dq = (dconv * w[2:3] + _roll(dconv, n_ext - 1) * w[1:2] + _roll(dconv, n_ext - 2) * w[0:1])[:tb]
        o_ref[:, 512:768] = dq * hh
        o_ref[:, 768:1024] = dq * gc
        dc = dconv[:tb]
        dsw_ref[0:1, :] += _colsum(dc * _roll(q_ext, 2)[HALO:])
        dsw_ref[1:2, :] += _colsum(dc * _roll(q_ext, 1)[HALO:])
        dsw_ref[2:3, :] += _colsum(dc * q_ext[HALO:])

    blk = pl.BlockSpec((tb, GW), lambda i: (i, 0))
    return pl.pallas_call(
        body, name="b_ab", grid=(nb,),
        in_specs=[pl.BlockSpec((tb, 1024), lambda i: (i, 0)), prev, nxt, blk, nxt_g, blk, nxt_g,
                  _full((GW, GW)), _row(GW), _full((3, GW))],
        out_specs=[pl.BlockSpec((tb, 1024), lambda i: (i, 0)), _full((GW, GW)), _row(GW), _full((3, GW))],
        out_shape=[jax.ShapeDtypeStruct((t, 1024), F32), jax.ShapeDtypeStruct((GW, GW), F32),
                   jax.ShapeDtypeStruct((1, GW), F32), jax.ShapeDtypeStruct((3, GW), F32)],
        compiler_params=_cparams(1),
    )(proj, proj, proj, dya, dya, dyb, dyb, pool_mat, pool_scale, sconv_w)


CH = 8


def _ssd_conv(x, halo, w, b):
    ext = jnp.concatenate([halo, x], axis=0)
    pre = ext * w[3:4] + _roll(ext, 1) * w[2:3] + _roll(ext, 2) * w[1:2] + _roll(ext, 3) * w[0:1] + b
    return pre[CH:], ext


def _ssd_common(dt_raw, dtb, alog):
    ll = dt_raw.shape[0]
    dtv = _softplus(dt_raw + dtb)
    a_row = -jnp.exp(alog)
    r = lax.broadcasted_iota(jnp.int32, (ll, ll), 0)
    c = lax.broadcasted_iota(jnp.int32, (ll, ll), 1)
    tril = (r >= c).astype(F32)
    cs = _dot(tril, dtv * a_row, prec=HI)
    return dtv, a_row, cs, cs.T, r >= c


def _bd(a, b, ca, cb):
    return lax.dot_general(a, b, (((ca,), (cb,)), ((0,), (0,))), preferred_element_type=F32)


def _head_cols(m):
    return jnp.stack([m[:, h:h + 1] for h in range(NH)])


def _ssd_heads(act, dtv, cs, cs_t, causal):
    xs = jnp.stack([act[:, HP * h:HP * (h + 1)] for h in range(NH)])
    bm = jnp.stack([act[:, 256 + NS * (h // 2):256 + NS * (h // 2 + 1)] for h in range(NH)])
    cm = jnp.stack([act[:, 512 + NS * (h // 2):512 + NS * (h // 2 + 1)] for h in range(NH)])
    cs_c = _head_cols(cs)
    cs_r = jnp.stack([cs_t[h:h + 1, :] for h in range(NH)])
    mdec = jnp.where(causal[None], jnp.exp(jnp.minimum(cs_c - cs_r, 0.0)), 0.0)
    g2 = _bd(jnp.stack([cm[0], cm[2]]), jnp.stack([bm[0], bm[2]]), 2, 2)
    sc = jnp.stack([g2[h // 2] for h in range(NH)]) * mdec
    dt_c = _head_cols(dtv)
    xdt = xs * dt_c
    e = jnp.exp(cs_c)
    cs_last = cs_c[:, SSD_L - 1:SSD_L, :]
    wdec = jnp.exp(cs_last - cs_c)
    return xs, bm, cm, mdec, sc, dt_c, xdt, e, cs_last, wdec


def _head_scalars(row_ref):
    return jnp.stack([row_ref[0:1, h:h + 1] for h in range(NH)])


def _f_ssd(proj, dtp, conv_w, conv_b, dt_bias, a_log, d_skip):
    t = proj.shape[0]
    nc = t // SSD_L
    rows = SSD_SUB * SSD_L
    per = rows // CH

    def body(x_ref, hx_ref, dt_ref, z_ref, cw_ref, cb_ref, dtb_ref, al_ref, dk_ref, y_ref, yp_ref, sp_ref, s_ref):
        i = pl.program_id(0)

        @pl.when(i == 0)
        def _():
            s_ref[...] = jnp.zeros_like(s_ref)

        state = s_ref[...]
        dk = _head_scalars(dk_ref)
        for sub in range(SSD_SUB):
            r0 = sub * SSD_L
            rs = slice(r0, r0 + SSD_L)
            halo = jnp.where(i > 0, hx_ref[...], 0.0) if sub == 0 else x_ref[r0 - CH:r0, :]
            pre, _ = _ssd_conv(x_ref[rs, :], halo, cw_ref[...], cb_ref[...])
            act = _silu(pre)
            dtv, _, cs, cs_t, causal = _ssd_common(dt_ref[rs, :], dtb_ref[...], al_ref[...])
            xs, bm, cm, _, sc, _, xdt, e, cs_last, wdec = _ssd_heads(act, dtv, cs, cs_t, causal)
            sp_ref[sub] = state
            y = _bd(sc, xdt, 2, 1) + e * _bd(cm, state, 2, 2) + xs * dk
            for h in range(NH):
                yp_ref[rs, HP * h:HP * (h + 1)] = y[h]
            state = state * jnp.exp(cs_last) + _bd(xdt * wdec, bm, 1, 1)
            y_ref[rs, :] = yp_ref[rs, :] * _silu(z_ref[rs, :])
        s_ref[...] = state

    blk = pl.BlockSpec((rows, GW), lambda i: (i, 0))
    sds = jax.ShapeDtypeStruct((t, GW), F32)
    return pl.pallas_call(
        body, name="f_ssd", grid=(nc // SSD_SUB,),
        in_specs=[pl.BlockSpec((rows, 768), lambda i: (i, 2)),
                  pl.BlockSpec((CH, 768), lambda i: (jnp.maximum(i * per - 1, 0), 2)),
                  pl.BlockSpec((rows, DTW), lambda i: (i, 0)),
                  pl.BlockSpec((rows, GW), lambda i: (i, 4)),
                  _full((4, 768)), _row(768), _row(DTW), _row(DTW), _row(DTW)],
        out_specs=[blk, blk, pl.BlockSpec((SSD_SUB, NH, HP, NS), lambda i: (i, 0, 0, 0))],
        out_shape=[sds, sds, jax.ShapeDtypeStruct((nc, NH, HP, NS), F32)],
        scratch_shapes=[pltpu.VMEM((NH, HP, NS), F32)], compiler_params=_cparams(1),
    )(proj, proj, dtp, proj, conv_w, conv_b, dt_bias, a_log, d_skip)


def _b_ssd(proj, dtp, ypre, dyc, sprev, conv_w, conv_b, dt_bias, a_log, d_skip):
    t = proj.shape[0]
    nc = t // SSD_L
    steps = nc // SSD_SUB_BWD
    rows = SSD_SUB_BWD * SSD_L
    per = rows // CH
    n_ext = SSD_L + CH

    def chunk(sub, halo, dnext, ds_in, refs):
        (x_ref, dt_ref, z_ref, yp_ref, dy_ref, sp_ref, cw_ref, cb_ref, dtb_ref, al_ref, dk_ref,
         dz_ref, dx_ref, ddt_ref, dact_ref) = refs
        rs = slice(sub * SSD_L, (sub + 1) * SSD_L)
        dact = dact_ref.at[sub]
        w = cw_ref[...]
        pre, ext = _ssd_conv(x_ref[rs, :], halo, w, cb_ref[...])
        act = _silu(pre)
        dt_raw = dt_ref[rs, :]
        dtv, a_row, cs, cs_t, causal = _ssd_common(dt_raw, dtb_ref[...], al_ref[...])
        z = z_ref[rs, :]
        dyc_v = dy_ref[rs, :]
        dz_ref[rs, :] = dyc_v * yp_ref[rs, :] * _dsilu(z)
        dy_all = dyc_v * _silu(z)
        lane = lax.broadcasted_iota(jnp.int32, (SSD_L, DTW), 1)
        rowi = lax.broadcasted_iota(jnp.int32, (1, SSD_L, 1), 1)
        lane1 = lax.broadcasted_iota(jnp.int32, (1, DTW), 1)
        xs, bm, cm, mdec, sc, dt_c, xdt, e, cs_last, wdec = _ssd_heads(act, dtv, cs, cs_t, causal)
        dy = jnp.stack([dy_all[:, HP * h:HP * (h + 1)] for h in range(NH)])
        prev = sp_ref[sub]
        ds = ds_in
        lsum = lambda v: jnp.sum(v, axis=2, keepdims=True)
        dsc = _bd(dy, xdt, 2, 2)
        q = dsc * sc
        dg = dsc * mdec
        dxdt = _bd(sc, dy, 1, 1)
        dcs = lsum(q) - lsum(jnp.swapaxes(q, 1, 2))
        dc = _bd(dg, bm, 2, 1)
        db = _bd(dg, cm, 1, 1)
        cp = _bd(cm, prev, 2, 2)
        dcs += lsum(dy * cp) * e
        ey = e * dy
        dc += _bd(ey, prev, 2, 1)
        dprev = _bd(ey, cm, 1, 1)
        elast = jnp.exp(cs_last)
        dprev += ds * elast
        dcs_last = jnp.sum(lsum(ds * prev), axis=1, keepdims=True) * elast
        bds = _bd(bm, ds, 2, 2)
        dxdt += wdec * bds
        db += wdec * _bd(xdt, ds, 2, 1)
        dw = lsum(xdt * bds) * wdec
        dcs -= dw
        dcs_last += jnp.sum(dw, axis=1, keepdims=True)
        dcs += jnp.where(rowi == SSD_L - 1, dcs_last, 0.0)
        dxs = dxdt * dt_c + dy * _head_scalars(dk_ref)
        ddtx = lsum(dxdt * xs)
        ddk = jnp.sum(lsum(dy * xs), axis=1, keepdims=True)
        dcs_mat = jnp.zeros((SSD_L, DTW), F32)
        ddtx_mat = jnp.zeros((SSD_L, DTW), F32)
        ddk_row = jnp.zeros((1, DTW), F32)
        for h in range(NH):
            dact[:, HP * h:HP * (h + 1)] = dxs[h]
            dcs_mat = jnp.where(lane == h, dcs[h], dcs_mat)
            ddtx_mat = jnp.where(lane == h, ddtx[h], ddtx_mat)
            ddk_row = jnp.where(lane1 == h, ddk[h], ddk_row)
        for g in range(2):
            dact[:, 256 + NS * g:256 + NS * (g + 1)] = db[2 * g] + db[2 * g + 1]
            dact[:, 512 + NS * g:512 + NS * (g + 1)] = dc[2 * g] + dc[2 * g + 1]
        ds_out = dprev
        r2 = lax.broadcasted_iota(jnp.int32, (SSD_L, SSD_L), 0)
        c2 = lax.broadcasted_iota(jnp.int32, (SSD_L, SSD_L), 1)
        dadt = _dot((c2 >= r2).astype(F32), dcs_mat, prec=HI)
        ddt = jnp.where(lane < NH, (dadt * a_row + ddtx_mat) * _sig(dt_raw + dtb_ref[...]), 0.0)
        ddt_ref[rs, :] = ddt
        dpre = dact[...] * _dsilu(pre)
        dcw = jnp.concatenate([_colsum(dpre * _roll(ext, 3 - k)[CH:]) for k in range(4)], axis=0)
        dext = jnp.concatenate([dpre, dnext], axis=0)
        dx_ref[rs, :] = (dext * w[3:4] + _roll(dext, n_ext - 1) * w[2:3] + _roll(dext, n_ext - 2) * w[1:2]
                         + _roll(dext, n_ext - 3) * w[0:1])[:SSD_L]
        acc = (dcw, _colsum(dpre), _colsum(ddt), _colsum(dadt * dtv) * a_row, ddk_row)
        return dpre[0:CH], ds_out, acc

    def body(x_ref, hx_ref, dt_ref, z_ref, yp_ref, dy_ref, sp_ref, cw_ref, cb_ref, dtb_ref, al_ref, dk_ref,
             dz_ref, dx_ref, ddt_ref, dcw_ref, dcb_ref, ddtb_ref, dal_ref, ddk_ref, ds_ref, dnext_ref, dact_ref):
        i = pl.program_id(0)
        acc_refs = (dcw_ref, dcb_ref, ddtb_ref, dal_ref, ddk_ref)

        @pl.when(i == 0)
        def _():
            ds_ref[...] = jnp.zeros_like(ds_ref)
            dnext_ref[...] = jnp.zeros_like(dnext_ref)
            for r in acc_refs:
                r[...] = jnp.zeros_like(r)

        refs = (x_ref, dt_ref, z_ref, yp_ref, dy_ref, sp_ref, cw_ref, cb_ref, dtb_ref, al_ref, dk_ref, dz_ref, dx_ref, ddt_ref,
                dact_ref)
        ds = ds_ref[...]
        dnext = dnext_ref[...]
        total = None
        for sub in reversed(range(SSD_SUB_BWD)):
            if sub == 0:
                halo = jnp.where(i == steps - 1, 0.0, hx_ref[...])
            else:
                halo = x_ref[sub * SSD_L - CH:sub * SSD_L, :]
            dnext, ds, acc = chunk(sub, halo, dnext, ds, refs)
            total = acc if total is None else tuple(a + b for a, b in zip(total, acc))
        ds_ref[...] = ds
        dnext_ref[...] = dnext
        for r, v in zip(acc_refs, total):
            r[...] += v

    rev = lambda i: steps - 1 - i
    blk = lambda n, cb=0: pl.BlockSpec((rows, n), lambda i: (rev(i), cb))
    row = lambda n: jax.ShapeDtypeStruct((1, n), F32)
    return pl.pallas_call(
        body, name="b_ssd", grid=(steps,),
        in_specs=[blk(768, 2), pl.BlockSpec((CH, 768), lambda i: (jnp.maximum(rev(i) * per - 1, 0), 2)),
                  blk(DTW), blk(GW, 4), blk(GW), blk(GW), pl.BlockSpec((SSD_SUB_BWD, NH, HP, NS), lambda i: (rev(i), 0, 0, 0)),
                  _full((4, 768)), _row(768), _row(DTW), _row(DTW), _row(DTW)],
        out_specs=[blk(GW), blk(768), blk(DTW), _full((4, 768)), _row(768), _row(DTW), _row(DTW), _row(DTW)],
        out_shape=[jax.ShapeDtypeStruct((t, GW), F32), jax.ShapeDtypeStruct((t, 768), F32), jax.ShapeDtypeStruct((t, DTW), F32),
                   jax.ShapeDtypeStruct((4, 768), F32), row(768), row(DTW), row(DTW), row(DTW)],
        scratch_shapes=[pltpu.VMEM((NH, HP, NS), F32), pltpu.VMEM((CH, 768), F32), pltpu.VMEM((SSD_SUB_BWD, SSD_L, 768), F32)],
        compiler_params=_cparams(1),
    )(proj, proj, dtp, proj, ypre, dyc, sprev, conv_w, conv_b, dt_bias, a_log, d_skip)


def _s5_block(t):
    return min(t, 256)


def _seg_t():
    r = lax.broadcasted_iota(jnp.int32, (64, 1024), 0)
    c = lax.broadcasted_iota(jnp.int32, (64, 1024), 1)
    return (c // 16 == r).astype(F32)


def _s5_prep_math(a_re, a_im, lstep, b_re, b_im):
    step = jnp.exp(lstep)
    ars = a_re * step
    ais = a_im * step
    mag = jnp.exp(ars)
    lr = mag * jnp.cos(ais)
    li = mag * jnp.sin(ais)
    den = a_re * a_re + a_im * a_im
    nr = lr - 1.0
    f_re = (nr * a_re + li * a_im) / den
    f_im = (li * a_re - nr * a_im) / den
    seg = _seg_t()
    fr = _dot(f_re, seg, prec=HI)
    fi = _dot(f_im, seg, prec=HI)
    return lr, li, fr * b_re - fi * b_im, fr * b_im + fi * b_re, ars, ais


def _s5_prep(a_re, a_im, lstep, b_re, b_im):
    def body(ar, ai, ls, br, bi, lr_o, li_o, bbr_o, bbi_o, ars_o, ais_o):
        outs = _s5_prep_math(ar[...], ai[...], ls[...], br[...], bi[...])
        for o, v in zip((lr_o, li_o, bbr_o, bbi_o, ars_o, ais_o), outs):
            o[...] = v

    s64 = jax.ShapeDtypeStruct((16, 64), F32)
    s1k = jax.ShapeDtypeStruct((16, 1024), F32)
    return pl.pallas_call(body, name="s5_prep", out_shape=[s64, s64, s1k, s1k, s64, s64])(a_re, a_im, lstep, b_re, b_im)


def _s5_prep_bwd(a_re, a_im, lstep, b_re, b_im, dlr, dli, dbbr, dbbi):
    def body(ar, ai, ls, br, bi, g0, g1, g2, g3, o0, o1, o2, o3, o4):
        f = lambda *a: _s5_prep_math(*a)[:4]
        _, vjp = jax.vjp(f, ar[...], ai[...], ls[...], br[...], bi[...])
        for o, v in zip((o0, o1, o2, o3, o4), vjp((g0[...], g1[...], g2[...], g3[...]))):
            o[...] = v

    s64 = jax.ShapeDtypeStruct((16, 64), F32)
    s1k = jax.ShapeDtypeStruct((16, 1024), F32)
    return pl.pallas_call(body, name="s5_prep_bwd", out_shape=[s64, s64, jax.ShapeDtypeStruct((16, 1), F32), s1k, s1k])(
        a_re, a_im, lstep, b_re, b_im, dlr, dli, dbbr, dbbi)


SUB = 8


def _s5_tables(ars, ais):
    def body(ar, ai, tr, ti):
        rr = lax.broadcasted_iota(jnp.int32, (8 * SUB, S5_P), 0)
        seg, r = rr // SUB, rr % SUB
        step = jnp.where((seg == 1) | (seg == 4), 1, jnp.where((seg == 2) | (seg == 5), 2, 4))
        n = jnp.where(seg == 0, r + 1, jnp.where(seg == 7, SUB - r, step))
        fwd_gap = jnp.where(seg <= 3, r - step, SUB - step - 1 - r)
        gap = jnp.where((seg == 0) | (seg == 7), 0, fwd_gap)
        nf = n.astype(F32)
        mag = jnp.where(gap >= 0, jnp.exp(nf * ar[...]), 0.0)
        tr[...] = mag * jnp.cos(nf * ai[...])
        ti[...] = mag * jnp.sin(nf * ai[...])

    sds = jax.ShapeDtypeStruct((8 * SUB, S5_P), F32)
    return pl.pallas_call(body, name="s5_tables", out_shape=[sds] * 2)(ars, ais)


def _s5_table(tb_r, tb_i, k):
    return tb_r[SUB * k:SUB * (k + 1), :], tb_i[SUB * k:SUB * (k + 1), :]


def _s5_scan(bu_r, bu_i, tb_r, tb_i, c_r, c_i, lb):
    nt = lb // SUB
    sr, si = bu_r.reshape(nt, SUB, S5_P), bu_i.reshape(nt, SUB, S5_P)
    for j, k in enumerate((1, 2, 4)):
        mr, mi = _s5_table(tb_r, tb_i, 1 + j)
        tr, ti = pltpu.roll(sr, k, axis=1), pltpu.roll(si, k, axis=1)
        sr, si = sr + mr * tr - mi * ti, si + mr * ti + mi * tr
    pr, pi = _s5_table(tb_r, tb_i, 0)
    out_r, out_i = [], []
    for j in range(nt):
        a_r = sr[j] + pr * c_r - pi * c_i
        a_i = si[j] + pr * c_i + pi * c_r
        out_r.append(a_r)
        out_i.append(a_i)
        c_r, c_i = a_r[SUB - 1:SUB], a_i[SUB - 1:SUB]
    return jnp.concatenate(out_r, axis=0), jnp.concatenate(out_i, axis=0)


def _s5_rscan(g_r, g_i, tb_r, tb_i, n_r, n_i, lb):
    nt = lb // SUB
    gr, gi = g_r.reshape(nt, SUB, S5_P), g_i.reshape(nt, SUB, S5_P)
    for j, k in enumerate((1, 2, 4)):
        mr, mi = _s5_table(tb_r, tb_i, 4 + j)
        tr, ti = pltpu.roll(gr, SUB - k, axis=1), pltpu.roll(gi, SUB - k, axis=1)
        gr, gi = gr + mr * tr + mi * ti, gi + mr * ti - mi * tr
    qr, qi = _s5_table(tb_r, tb_i, 7)
    out_r, out_i = [None] * nt, [None] * nt
    for j in reversed(range(nt)):
        a_r = gr[j] + qr * n_r + qi * n_i
        a_i = gi[j] + qr * n_i - qi * n_r
        out_r[j], out_i[j] = a_r, a_i
        n_r, n_i = a_r[0:1], a_i[0:1]
    return jnp.concatenate(out_r, axis=0), jnp.concatenate(out_i, axis=0)


def _s5_y(u, sr, si, cre, cim, dsk):
    return _bdot(sr, cre) + _bdot(si, cim) + dsk * u


def _f_s5(proj, bmat, cre, cim, p_r, p_i, dsk, glu_w, glu_b):
    t = proj.shape[0]
    lb = _s5_block(t)
    nb = t // lb

    def body(u_ref, bm_ref, cr_ref, ci_ref, pr_ref, pi_ref, dk_ref, gw_ref, gb_ref, y_ref, car_ref, s_ref, st_ref):
        @pl.when(pl.program_id(0) == 0)
        def _():
            st_ref[...] = jnp.zeros_like(st_ref)

        u = u_ref[...]
        bu = _bdot(u, bm_ref[...])
        c_r, c_i = st_ref[0:1, 0:S5_P], st_ref[0:1, S5_P:]
        car_ref[0] = st_ref[0:1, :]
        sr, si = _s5_scan(bu[:, :S5_P], bu[:, S5_P:], pr_ref, pi_ref, c_r, c_i, lb)
        st_ref[0:1, 0:S5_P] = sr[lb - 1:lb]
        st_ref[0:1, S5_P:] = si[lb - 1:lb]
        sr_b, si_b = sr.astype(BF16), si.astype(BF16)
        s_ref[:, 0:S5_P] = sr_b
        s_ref[:, S5_P:] = si_b
        gel = _gelu(_s5_y(u, sr_b, si_b, cr_ref[...], ci_ref[...], dk_ref[...]))
        y_ref[...] = gel * _sig(_bdot(gel, gw_ref[...]) + gb_ref[...])

    return pl.pallas_call(
        body, name="f_s5", grid=(nb,),
        in_specs=[pl.BlockSpec((lb, GW), lambda i: (i, 5)),
                  _full((GW, 2 * S5_P)), _full((S5_P, GW)), _full((S5_P, GW)), _full((8 * SUB, S5_P)), _full((8 * SUB, S5_P)),
                  _row(GW), _full((GW, GW)), _row(GW)],
        out_specs=[pl.BlockSpec((lb, GW), lambda i: (i, 0)), pl.BlockSpec((1, 1, 2 * S5_P), lambda i: (i, 0, 0)),
                   pl.BlockSpec((lb, 2 * S5_P), lambda i: (i, 0))],
        out_shape=[jax.ShapeDtypeStruct((t, GW), F32), jax.ShapeDtypeStruct((nb, 1, 2 * S5_P), F32),
                   jax.ShapeDtypeStruct((t, 2 * S5_P), BF16)],
        scratch_shapes=[pltpu.VMEM((8, 2 * S5_P), F32)], compiler_params=_cparams(1),
    )(proj, bmat, cre, cim, p_r, p_i, dsk, glu_w, glu_b)


def _b_s5(proj, dyd, carries, states, bmat, cre, cim, p_r, p_i, dsk, glu_w, glu_b):
    t = proj.shape[0]
    lb = _s5_block(t)
    nb = t // lb

    def body(u_ref, dy_ref, car_ref, s_ref, bm_ref, cr_ref, ci_ref, pr_ref, pi_ref, dk_ref, gw_ref, gb_ref,
             du_ref, dbm_ref, dcr_ref, dci_ref, dlam_ref, ddk_ref, dgw_ref, dgb_ref, gc_ref):
        @pl.when(pl.program_id(0) == 0)
        def _():
            gc_ref[...] = jnp.zeros_like(gc_ref)
            for r in (dbm_ref, dcr_ref, dci_ref, dlam_ref, ddk_ref, dgw_ref, dgb_ref):
                r[...] = jnp.zeros_like(r)

        u = u_ref[...]
        bm = bm_ref[...]
        u_b = u.astype(BF16)
        c_r, c_i = car_ref[0, 0:1, 0:S5_P], car_ref[0, 0:1, S5_P:]
        cre_v, cim_v, dk, gw = cr_ref[...], ci_ref[...], dk_ref[...], gw_ref[...]
        sr_b, si_b = s_ref[:, 0:S5_P], s_ref[:, S5_P:]
        sr, si = sr_b.astype(F32), si_b.astype(F32)
        y = _dot(sr_b, cre_v) + _dot(si_b, cim_v) + dk * u
        gel = _gelu(y)
        gel_b = gel.astype(BF16)
        gate = _sig(_dot(gel_b, gw) + gb_ref[...])
        dout = dy_ref[...]
        t1 = dout * gel * gate * (1.0 - gate)
        t1_b = t1.astype(BF16)
        dgw_ref[...] += _dot(gel_b, t1_b, TN)
        dgb_ref[...] += _colsum(t1)
        dyv = (dout * gate + _dot(t1_b, gw, NT)) * _dgelu(y)
        dyv_b = dyv.astype(BF16)
        ddk_ref[...] += _colsum(dyv * u)
        dcr_ref[...] += _dot(sr_b, dyv_b, TN)
        dci_ref[...] += _dot(si_b, dyv_b, TN)
        gr = _dot(dyv_b, cre_v, NT)
        gi = _dot(dyv_b, cim_v, NT)
        row = lax.broadcasted_iota(jnp.int32, (lb, S5_P), 0)
        n_r, n_i = gc_ref[0:1, 0:S5_P], gc_ref[0:1, S5_P:]
        gr, gi = _s5_rscan(gr, gi, pr_ref, pi_ref, n_r, n_i, lb)
        gc_ref[0:1, 0:S5_P] = gr[0:1]
        gc_ref[0:1, S5_P:] = gi[0:1]
        gcat = jnp.concatenate([gr, gi], axis=1).astype(BF16)
        dbm_ref[...] += _dot(u_b, gcat, TN)
        du_ref[...] = dyv * dk + _dot(gcat, bm, NT)
        spr = jnp.where(row >= 1, _roll(sr, 1), c_r)
        spi = jnp.where(row >= 1, _roll(si, 1), c_i)
        dlam_ref[0:1, :] += _colsum(gr * spr + gi * spi)
        dlam_ref[1:2, :] += _colsum(gi * spr - gr * spi)

    rev = lambda i: nb - 1 - i
    return pl.pallas_call(
        body, name="b_s5", grid=(nb,),
        in_specs=[pl.BlockSpec((lb, GW), lambda i: (rev(i), 5)), pl.BlockSpec((lb, GW), lambda i: (rev(i), 0)),
                  pl.BlockSpec((1, 1, 2 * S5_P), lambda i: (rev(i), 0, 0)), pl.BlockSpec((lb, 2 * S5_P), lambda i: (rev(i), 0)),
                  _full((GW, 2 * S5_P)), _full((S5_P, GW)), _full((S5_P, GW)), _full((8 * SUB, S5_P)), _full((8 * SUB, S5_P)),
                  _row(GW), _full((GW, GW)), _row(GW)],
        out_specs=[pl.BlockSpec((lb, GW), lambda i: (rev(i), 0)), _full((GW, 2 * S5_P)), _full((S5_P, GW)), _full((S5_P, GW)),
                   _full((2, S5_P)), _row(GW), _full((GW, GW)), _row(GW)],
        out_shape=[jax.ShapeDtypeStruct((t, GW), F32), jax.ShapeDtypeStruct((GW, 2 * S5_P), F32),
                   jax.ShapeDtypeStruct((S5_P, GW), F32), jax.ShapeDtypeStruct((S5_P, GW), F32),
                   jax.ShapeDtypeStruct((2, S5_P), F32), jax.ShapeDtypeStruct((1, GW), F32),
                   jax.ShapeDtypeStruct((GW, GW), F32), jax.ShapeDtypeStruct((1, GW), F32)],
        scratch_shapes=[pltpu.VMEM((8, 2 * S5_P), F32)], compiler_params=_cparams(1),
    )(proj, dyd, carries, states, bmat, cre, cim, p_r, p_i, dsk, glu_w, glu_b)


def _group_norm(ys, bw):
    outs, stats = [], []
    for g, y in enumerate(ys):
        r, n = _rms(y)
        stats.append((r, n))
        outs.append(n * bw[:, GW * g:GW * (g + 1)])
    return jnp.concatenate(outs, axis=1), stats


def _f_out(ya, yb, yc, yd, bw, wts, l, h, g1):
    t = h.shape[0]
    tb = _tblock(t)

    def body(a_ref, b_ref, c_ref, d_ref, bw_ref, w_ref, h_ref, g_ref, h2_ref, o_ref, cat_ref):
        cat, _ = _group_norm([a_ref[...], b_ref[...], c_ref[...], d_ref[...]], bw_ref[...])
        catb = cat.astype(BF16)
        cat_ref[...] = catb
        o = _dot(catb, w_ref[0].reshape(D, D))
        o_ref[...] = o.astype(BF16)
        h2_ref[...] = h_ref[...] + g_ref[...] * o

    yblk = pl.BlockSpec((tb, GW), lambda i: (i, 0))
    blk = pl.BlockSpec((tb, D), lambda i: (i, 0))
    return pl.pallas_call(
        body, name="f_out", grid=(t // tb,), in_specs=[yblk] * 4 + [_row(D), _wout_spec(l), blk, _row(D)],
        out_specs=[blk, blk, blk],
        out_shape=[jax.ShapeDtypeStruct((t, D), F32), jax.ShapeDtypeStruct((t, D), BF16), jax.ShapeDtypeStruct((t, D), BF16)],
        compiler_params=_cparams(1),
    )(ya, yb, yc, yd, bw, wts, h, g1)


def _b_out(dh2, ya, yb, yc, yd, bw, wts, l, g1):
    t = dh2.shape[0]
    tb = _tblock(t)

    def body(dh_ref, a_ref, b_ref, c_ref, d_ref, bw_ref, w_ref, g_ref, da_ref, db_ref, dc_ref, dd_ref, do_ref, dbw_ref):
        @pl.when(pl.program_id(0) == 0)
        def _():
            dbw_ref[...] = jnp.zeros_like(dbw_ref)

        do = (dh_ref[...] * g_ref[...]).astype(BF16)
        do_ref[...] = do
        dcat = _dot(do, w_ref[0].reshape(D, D), NT)
        bw_v = bw_ref[...]
        for g, (y_ref, dy_ref) in enumerate(((a_ref, da_ref), (b_ref, db_ref), (c_ref, dc_ref), (d_ref, dd_ref))):
            r, n = _rms(y_ref[...])
            dc = dcat[:, GW * g:GW * (g + 1)]
            dbw_ref[:, GW * g:GW * (g + 1)] += _colsum(dc * n)
            dy_ref[...] = _rms_bwd(r, n, dc * bw_v[:, GW * g:GW * (g + 1)])

    yblk = pl.BlockSpec((tb, GW), lambda i: (i, 0))
    blk = pl.BlockSpec((tb, D), lambda i: (i, 0))
    ysd = jax.ShapeDtypeStruct((t, GW), F32)
    return pl.pallas_call(
        body, name="b_out", grid=(t // tb,), in_specs=[blk] + [yblk] * 4 + [_row(D), _wout_spec(l), _row(D)],
        out_specs=[yblk] * 4 + [blk, _row(D)],
        out_shape=[ysd] * 4 + [jax.ShapeDtypeStruct((t, D), BF16), jax.ShapeDtypeStruct((1, D), F32)],
        compiler_params=_cparams(1),
    )(dh2, ya, yb, yc, yd, bw, wts, g1)


HB = 512
MLP_ROWS = 1024


ROW_W1, ROW_W2, ROW_WOUT, ROW_WIN = 0, D, D + HID // 4, D + HID // 4 + D // 4
PACK_ROWS = ROW_WIN + WIN_ROWS


def _w1_spec(l):
    per = HID // 4 // HB
    return pl.BlockSpec((1, 1, D, HB), lambda i, k: (l, k // per, ROW_W1 // D, k % per))


def _w2_spec(l):
    per = HID // 4 // HB
    return pl.BlockSpec((1, 1, HB, D), lambda i, k: (l, k // per, ROW_W2 // HB + k % per, 0))


def _wout_spec(l):
    return pl.BlockSpec((1, 4, D // 4, D), lambda i: (l, 0, ROW_WOUT // (D // 4), 0))


def _f_mlp(h2, nw, sc, sh, g2, wts, l):
    t = h2.shape[0]
    tb = _tblock(t, MLP_ROWS)
    nk = HID // HB

    def body(h_ref, nw_ref, sc_ref, sh_ref, g_ref, w1_ref, w2_ref, h3_ref, m_ref, a_ref, v_ref, acc_ref):
        k = pl.program_id(1)

        @pl.when(k == 0)
        def _():
            _, n = _rms(h_ref[...])
            v_ref[...] = ((n * nw_ref[...]) * (1.0 + sc_ref[...]) + sh_ref[...]).astype(BF16)
            acc_ref[...] = jnp.zeros_like(acc_ref)

        a = _dot(v_ref[...], w1_ref[0, 0])
        a_ref[...] = a.astype(BF16)
        ra = jnp.maximum(a, 0.0)
        acc_ref[...] += _dot((ra * ra).astype(BF16), w2_ref[0, 0])

        @pl.when(k == nk - 1)
        def _():
            m = acc_ref[...]
            m_ref[...] = m.astype(BF16)
            h3_ref[...] = h_ref[...] + g_ref[...] * m

    blk = pl.BlockSpec((tb, D), lambda i, k: (i, 0))
    return pl.pallas_call(
        body, name="f_mlp", grid=(t // tb, nk),
        in_specs=[blk, _row(D), _row(D), _row(D), _row(D), _w1_spec(l), _w2_spec(l)],
        out_specs=[blk, blk, pl.BlockSpec((tb, HB), lambda i, k: (i, k)), blk],
        out_shape=[jax.ShapeDtypeStruct((t, D), F32), jax.ShapeDtypeStruct((t, D), BF16), jax.ShapeDtypeStruct((t, HID), BF16),
                   jax.ShapeDtypeStruct((t, D), BF16)],
        scratch_shapes=[pltpu.VMEM((tb, D), F32)], compiler_params=_cparams(2),
    )(h2, nw, sc, sh, g2, wts, wts)


def _b_mlp(dh3, a, g2, wts, l):
    t = dh3.shape[0]
    tb = _tblock(t, MLP_ROWS)
    nk = HID // HB

    def body(dh_ref, a_ref, g_ref, w1_ref, w2_ref, dv_ref, da_ref, act_ref, dm_ref):
        k = pl.program_id(1)
        dm = (dh_ref[...] * g_ref[...]).astype(BF16)

        @pl.when(k == 0)
        def _():
            dm_ref[...] = dm
            dv_ref[...] = jnp.zeros_like(dv_ref)

        ra = jnp.maximum(a_ref[...].astype(F32), 0.0)
        act_ref[...] = (ra * ra).astype(BF16)
        da = (_dot(dm, w2_ref[0, 0], NT) * (2.0 * ra)).astype(BF16)
        da_ref[...] = da
        dv_ref[...] += _dot(da, w1_ref[0, 0], NT)

    blk = pl.BlockSpec((tb, D), lambda i, k: (i, 0))
    hblk = pl.BlockSpec((tb, HB), lambda i, k: (i, k))
    return pl.pallas_call(
        body, name="b_mlp", grid=(t // tb, nk),
        in_specs=[blk, hblk, _row(D), _w1_spec(l), _w2_spec(l)],
        out_specs=[blk, hblk, hblk, blk],
        out_shape=[jax.ShapeDtypeStruct((t, D), F32), jax.ShapeDtypeStruct((t, HID), BF16), jax.ShapeDtypeStruct((t, HID), BF16),
                   jax.ShapeDtypeStruct((t, D), BF16)],
        compiler_params=_cparams(2),
    )(dh3, a, g2, wts, wts)


def _b_final(h, tgt, fw):
    t = h.shape[0]
    tb = _tblock(t)

    def body(h_ref, t_ref, w_ref, dh_ref, loss_ref, dfw_ref):
        @pl.when(pl.program_id(0) == 0)
        def _():
            loss_ref[...] = jnp.zeros_like(loss_ref)
            dfw_ref[...] = jnp.zeros_like(dfw_ref)

        r, n = _rms(h_ref[...])
        wv = w_ref[...]
        err = n * wv - t_ref[...]
        loss_ref[...] += jnp.sum(err * err, keepdims=True) * (0.5 / D)
        dy = err * (1.0 / D)
        dfw_ref[...] += _colsum(dy * n)
        dh_ref[...] = _rms_bwd(r, n, dy * wv)

    blk = pl.BlockSpec((tb, D), lambda i: (i, 0))
    return pl.pallas_call(
        body, name="b_final", grid=(t // tb,), in_specs=[blk, blk, _row(D)], out_specs=[blk, _row(1), _row(D)],
        out_shape=[jax.ShapeDtypeStruct((t, D), F32), jax.ShapeDtypeStruct((1, 1), F32), jax.ShapeDtypeStruct((1, D), F32)],
        compiler_params=_cparams(1),
    )(h, tgt, fw)


def _eye(n):
    return jnp.eye(n, dtype=F32)


def _pool_embed(pool_w):
    return jnp.einsum('gcd,gk->gckd', pool_w, _eye(4)).reshape(GW, GW)


def _pool_extract(m):
    return jnp.einsum('gcgd->gcd', m.reshape(4, 64, 4, 64))


def _bmat_embed(bb):
    return jnp.einsum('gph,gk->ghkp', bb, _eye(16)).reshape(GW, S5_P)


def _bmat_extract(m):
    return jnp.einsum('ghgp->gph', m.reshape(16, 16, 16, 64))


def _cmat_embed(cc):
    return jnp.einsum('ghp,gk->kpgh', cc, _eye(16)).reshape(S5_P, GW)


def _cmat_extract(m):
    return jnp.einsum('gpgh->ghp', m.reshape(16, 64, 16, 16))


def _pad_lanes(v, n=DTW):
    return jnp.pad(v.reshape(1, -1), ((0, 0), (0, n - v.shape[-1])))


def _w_in_layout(w_in_t):
    w_main = jnp.concatenate([w_in_t[:1280], w_in_t[2052:2308], w_in_t[1280:2048]], axis=0)
    return w_main, jnp.pad(w_in_t[2048:2052], ((0, DTW - 4), (0, 0)))


def _layer_params(p, l, mod, w_in, rest):
    q = {'rest': rest, 'l': l}
    q['mod'] = [mod[k:k + 1] for k in range(6)]
    q['nw1'] = p['norm_mix_w'][l:l + 1]
    q['nw2'] = p['norm_mlp_w'][l:l + 1]
    q['w_main'], q['w_dt'] = _w_in_layout(w_in)
    q['pool_mat'] = _pool_embed(p['pool_w'][l]).astype(BF16)
    q['pool_scale'] = p['pool_scale'][l:l + 1]
    q['sconv_w'] = p['sconv_w'][l]
    q['conv_w'] = p['ssd_conv_w'][l]
    q['conv_b'] = p['ssd_conv_b'][l:l + 1]
    q['dt_bias'] = _pad_lanes(p['ssd_dt_bias'][l])
    q['a_log'] = _pad_lanes(p['ssd_a_log'][l])
    q['ssd_d'] = _pad_lanes(p['ssd_d'][l])
    q['s5_raw'] = (p['s5_a_re'][l], p['s5_a_im'][l], p['s5_log_step'][l].reshape(16, 1),
                   p['s5_b_re'][l].reshape(16, 1024), p['s5_b_im'][l].reshape(16, 1024))
    q['cre'] = _cmat_embed(p['s5_c_re'][l]).astype(BF16)
    q['cim'] = (-_cmat_embed(p['s5_c_im'][l])).astype(BF16)
    q['s5_d'] = p['s5_d'][l:l + 1]
    q['glu_w'] = p['s5_glu_w'][l].astype(BF16)
    q['glu_b'] = p['s5_glu_b'][l:l + 1]
    q['bw'] = p['branch_norm_w'][l:l + 1]
    return q


def _layer_fwd(h, q):
    sh1, sc1, g1, sh2, sc2, g2 = q['mod']
    t = h.shape[0]
    s = {'h': h}
    s['proj'], s['dtp'], s['u'] = _f_in(h, q['nw1'], sc1, sh1, q['w_main'], q['w_dt'])
    s['ya'], s['yb'] = _f_ab(s['proj'], q['pool_mat'], q['pool_scale'], q['sconv_w'])
    s['yc'], s['ypre'], s['sprev'] = _f_ssd(s['proj'], s['dtp'], q['conv_w'], q['conv_b'], q['dt_bias'], q['a_log'], q['ssd_d'])
    lr, li, bbr, bbi, ars, ais = _s5_prep(*q['s5_raw'])
    s['bmat'] = jnp.concatenate([_bmat_embed(bbr.reshape(16, 64, 16)), _bmat_embed(bbi.reshape(16, 64, 16))],
                                axis=1).astype(BF16)
    s['tables'] = _s5_tables(ars.reshape(1, S5_P), ais.reshape(1, S5_P))
    s['yd'], s['carries'], s['states'] = _f_s5(s['proj'], s['bmat'], q['cre'], q['cim'], s['tables'][0], s['tables'][1],
                                  q['s5_d'], q['glu_w'], q['glu_b'])
    q['wts'] = q['rest']((s['ya'], s['yc'], s['yd']))
    s['h2'], s['o'], s['cat'] = _f_out(s['ya'], s['yb'], s['yc'], s['yd'], q['bw'], q['wts'], q['l'], h, g1)
    h3, s['m'], s['a'], s['v'] = _f_mlp(s['h2'], q['nw2'], sc2, sh2, g2, q['wts'], q['l'])
    return h3, s


STACKED = {'mlp_w1': (2, 4, D, HID // 4), 'mlp_w2': (2, HID, D), 'w_out': (2, D, D)}


def _layer_bwd(dh3, q, s, l, stacked, early=None):
    sh1, sc1, g1, sh2, sc2, g2 = q['mod']
    g = {}
    dv, da, act, dm = _b_mlp(dh3, s['a'], g2, q['wts'], l)
    g['mlp_w1'] = _tn_matmul(s['v'], da, "dw1", col_major=True, into=stacked['mlp_w1'], layer=l)
    g['mlp_w2'] = _tn_matmul(act, dm, "dw2", into=stacked['mlp_w2'], layer=l)
    dh2, dsc2, dsh2, dnw2, dg2 = _b_normmod(dv, s['h2'], dh3, s['m'], q['nw2'], sc2, "b_norm_mlp")
    dya, dyb, dyc, dyd, do, dbw = _b_out(dh2, s['ya'], s['yb'], s['yc'], s['yd'], q['bw'], q['wts'], l, g1)
    g['w_out'] = _tn_matmul(s['cat'], do, "dwout", into=stacked['w_out'], layer=l)
    g['branch_norm_w'] = dbw[0]
    if early is not None:
        zero = early(g)[0, 0]
        q = dict(q, pool_scale=q['pool_scale'] + zero, conv_b=q['conv_b'] + zero, s5_d=q['s5_d'] + zero)
    dab, dpm, dps, dsw = _b_ab(s['proj'], dya, dyb, q['pool_mat'], q['pool_scale'], q['sconv_w'])
    g['pool_w'] = _pool_extract(dpm)
    g['pool_scale'] = dps[0]
    g['sconv_w'] = dsw
    dz, dxbc, ddt, dcw, dcb, ddtb, dal, ddk = _b_ssd(s['proj'], s['dtp'], s['ypre'], dyc, s['sprev'], q['conv_w'],
                                                     q['conv_b'], q['dt_bias'], q['a_log'], q['ssd_d'])
    g['ssd_conv_w'] = dcw
    g['ssd_conv_b'] = dcb[0]
    g['ssd_dt_bias'] = ddtb[0, :4]
    g['ssd_a_log'] = dal[0, :4]
    g['ssd_d'] = ddk[0, :4]
    tb = s['tables']
    ds5, dbmat, dcre, dcim, dlam, dd5, dgw, dgb = _b_s5(s['proj'], dyd, s['carries'], s['states'], s['bmat'], q['cre'], q['cim'],
                                                        tb[0], tb[1], q['s5_d'], q['glu_w'], q['glu_b'])
    g['s5_c_re'] = _cmat_extract(dcre)
    g['s5_c_im'] = -_cmat_extract(dcim)
    g['s5_d'] = dd5[0]
    g['s5_glu_w'] = dgw
    g['s5_glu_b'] = dgb[0]
    dbbr = _bmat_extract(dbmat[:, :S5_P]).reshape(16, 1024)
    dbbi = _bmat_extract(dbmat[:, S5_P:]).reshape(16, 1024)
    dar, dai, dls, dbr, dbi = _s5_prep_bwd(*q['s5_raw'], dlam[0].reshape(16, 64), dlam[1].reshape(16, 64), dbbr, dbbi)
    g['s5_a_re'], g['s5_a_im'], g['s5_log_step'] = dar, dai, dls[:, 0]
    g['s5_b_re'], g['s5_b_im'] = dbr, dbi
    dh, dsc1, dsh1, dnw1, dg1 = _b_in(dab, dz, dxbc, ds5, ddt, q['w_main'], q['w_dt'], s['h'], dh2, s['o'], q['nw1'], sc1)
    u = s['u']
    head = jnp.concatenate([_tn_matmul(dab, u, "dwin_ab"), _tn_matmul(dz, u, "dwin_z"), _tn_matmul(dxbc, u, "dwin_xbc"),
                            _tn_matmul(ddt, u, "dwin_dt")[:8]], axis=0)
    full = lax.dynamic_update_slice(jnp.zeros((2308, D), F32), head, (0, 0))
    g['w_in'] = lax.dynamic_update_slice(full, _tn_matmul(ds5, u, "dwin_s5"), (2052, 0))
    g['norm_mix_w'] = dnw1[0]
    g['norm_mlp_w'] = dnw2[0]
    dmod = jnp.concatenate([dsh1, dsc1, dg1, dsh2, dsc2, dg2], axis=1)
    return dh, g, dmod


def _local_step(x, tgt, p, mod, w_in_of, rest_of, early=None):
    h = x
    qs, saved = [], []
    for l in range(2):
        qs.append(_layer_params(p, l, mod[l], w_in_of(l), functools.partial(rest_of, l)))
        h, s = _layer_fwd(h, qs[l])
        saved.append(s)
    dh, loss, dfw = _b_final(h, tgt, p['final_norm_w'].reshape(1, D))
    grads = [None, None]
    dmods = [None, None]
    dh, grads[1], dmods[1] = _layer_bwd(dh, qs[1], saved[1], 1, {k: lax.empty(shp, F32) for k, shp in STACKED.items()})
    dh, grads[0], dmods[0] = _layer_bwd(dh, qs[0], saved[0], 0, grads[1], early)
    out = {k: jnp.stack([grads[0][k], grads[1][k]]) for k in grads[0] if k not in STACKED}
    if early is None:
        out.update({k: grads[0][k] for k in STACKED})
    out['final_norm_w'] = dfw[0]
    return loss, dh, out, jnp.concatenate(dmods, axis=0)


def _shard_of(a, axis, k):
    n = a.shape[axis] // 4
    return lax.dynamic_slice_in_dim(a, k * n, n, axis)


def kernel(x, c, norm_mix_w, norm_mlp_w, ada_w, ada_b, w_in, pool_w, pool_scale, sconv_w, ssd_conv_w, ssd_conv_b, ssd_dt_bias, ssd_a_log, ssd_d, s5_a_re, s5_a_im, s5_log_step, s5_b_re, s5_b_im, s5_c_re, s5_c_im, s5_d, s5_glu_w, s5_glu_b, branch_norm_w, w_out, mlp_w1, mlp_w2, final_norm_w, loss_target, m_norm_mix_w, m_norm_mlp_w, m_ada_w, m_ada_b, m_w_in, m_pool_w, m_pool_scale, m_sconv_w, m_ssd_conv_w, m_ssd_conv_b, m_ssd_dt_bias, m_ssd_a_log, m_ssd_d, m_s5_a_re, m_s5_a_im, m_s5_log_step, m_s5_b_re, m_s5_b_im, m_s5_c_re, m_s5_c_im, m_s5_d, m_s5_glu_w, m_s5_glu_b, m_branch_norm_w, m_w_out, m_mlp_w1, m_mlp_w2, m_final_norm_w, v_norm_mix_w, v_norm_mlp_w, v_ada_w, v_ada_b, v_w_in, v_pool_w, v_pool_scale, v_sconv_w, v_ssd_conv_w, v_ssd_conv_b, v_ssd_dt_bias, v_ssd_a_log, v_ssd_d, v_s5_a_re, v_s5_a_im, v_s5_log_step, v_s5_b_re, v_s5_b_im, v_s5_c_re, v_s5_c_im, v_s5_d, v_s5_glu_w, v_s5_glu_b, v_branch_norm_w, v_w_out, v_mlp_w1, v_mlp_w2, v_final_norm_w):
    loc = locals()
    w = {n: loc[n] for n in WEIGHTS}
    mom = {n: loc['m_' + n] for n in WEIGHTS}
    var = {n: loc['v_' + n] for n in WEIGHTS}
    ix, iy, ic = lax.axis_index("x"), lax.axis_index("y"), lax.axis_index("c")
    chip = 2 * ix + iy
    dev = 4 * ix + 2 * iy + ic

    mine_of = lambda a: lax.dynamic_index_in_dim(a.astype(BF16), ic, axis=0, keepdims=False)
    pad_in = lambda a: jnp.pad(a.T, ((0, WIN_ROWS - 577), (0, 0)))
    shard = jnp.concatenate([mine_of(w['mlp_w1']), mine_of(w['mlp_w2']), mine_of(w['w_out']), pad_in(mine_of(w['w_in']))], axis=0)

    (c_all,) = _exchange([c], EVERYONE, False, "ag_cond", stage=True)
    c_all = c_all.reshape(8, D)
    small_sh = _exchange([w[n] for n in SMALL_SHARDED], CHIPS, False, "ag_small")
    (w_in0,) = _exchange([pad_in(w['w_in'][0].astype(BF16))], CHIPS, False, "ag_win0")
    p = {n: w[n] for n in WEIGHTS if n not in BIG}
    for n, g in zip(SMALL_SHARDED, small_sh):
        ax = SMALL_SHARDED[n]
        p[n] = jnp.concatenate([g[k] for k in range(4)], axis=ax)

    def w_in_full(sh):
        return sh[:, :577].reshape(4 * 577, D)

    big = {}

    def fetch(after):
        if not big:
            (mine,), (got,) = _split_wait(sems, shard_thru, land, after, False, "ag_big_wait", per_core=True)
            got = lax.dynamic_update_slice(got, mine[None, None], (ic, chip, 0, 0))
            (both,) = _pair_swap([got.reshape(2, -1, D)], False, "swap_big", fill=True)
            big['both'] = both.reshape(got.shape)
        return big['both']

    def w_in_of(l):
        return w_in_full(w_in0) if l == 0 else w_in_full(fetch(None)[1, :, ROW_WIN:])

    def rest_of(l, after):
        return fetch(after)

    ada_b_sh = _shard_of(w['ada_b'], 1, chip).reshape(2, 1, 6 * D // 4)
    mod_sh = _ada_fwd(c_all, w['ada_w'], ada_b_sh)
    (mod_all,) = _exchange([mod_sh], CHIPS, False, "ag_mod", stage=True)
    mine = lax.dynamic_index_in_dim(mod_all, dev, axis=2, keepdims=False)
    sems, shard_thru, land, token = _split_start([shard], [mod_all, w_in0] + small_sh, False, "ag_big_start", per_core=True)
    mod = jnp.transpose(mine, (1, 0, 2)).reshape(2, 6, D) + token[0, 0]

    layer = ic.astype(jnp.int32).reshape(1)
    flight = {}

    def early(g0):
        gws = [g0['w_out'].reshape(2, 4, 256, D), g0['mlp_w1'], g0['mlp_w2'].reshape(2, 4, 1024, D)]
        got = _pair_swap([a.reshape(2, -1, D) for a in gws], True, "swap_grad", narrow=True)
        pair = [_pair_sum(a, b.reshape(a.shape[1:]), layer, "pair_sum%d" % (k + 1), BF16) for k, (a, b) in enumerate(zip(gws, got))]
        flight['sems'], flight['srcs'], flight['lands'], token = _split_start(pair, [], True, "rs_start")
        return token

    loss, grad_x, g, dmod = _local_step(x[0], loss_target[0], p, mod, w_in_of, rest_of, early)

    (dmod_all,) = _exchange([dmod], EVERYONE, False, "ag_dmod", stage=True)
    dmod_all = jnp.transpose(dmod_all, (1, 0, 2))

    gw_in = jnp.pad(g['w_in'].reshape(2, 4, 577, D), ((0, 0), (0, 0), (0, WIN_ROWS - 577), (0, 0)))
    (got_in,) = _pair_swap([gw_in.reshape(2, -1, D)], True, "swap_grad_in", narrow=True)
    pair_in = _pair_sum(gw_in, got_in.reshape(gw_in.shape[1:]), layer, "pair_sum0", BF16)
    in_sems, in_srcs, in_lands, in_token = _split_start([pair_in], [dmod_all], True, "rs_in_start")

    def chip_sum(land, mine, name):
        own = lax.dynamic_index_in_dim(mine, chip, axis=0, keepdims=True)
        return _sum_lead(lax.dynamic_update_slice(land, own, (chip, 0, 0)), name, F32)

    sent, lands = _split_wait(flight['sems'], flight['srcs'], flight['lands'], [grad_x, in_token], True, "rs_wait")
    quad = [chip_sum(land, mine, "rs_chip_sum%d" % (k + 1)) for k, (land, mine) in enumerate(zip(lands, sent))]
    g_ada_w, g_ada_b = _ada_bwd(c_all, _shard_of(dmod_all, 2, chip), dmod_all)
    adam_ada_w = _adamw(w['ada_w'], g_ada_w, mom['ada_w'], var['ada_w'], "adamw_ada_w")
    (sent_in,), (land_in,) = _split_wait(in_sems, in_srcs, in_lands, quad + [adam_ada_w[0]], True, "rs_in_wait")
    quad = [chip_sum(land_in, sent_in, "rs_chip_sum0")] + quad
    halves = [lax.dynamic_update_slice(lax.empty((2,) + a.shape, F32), a[None], (ic, 0, 0)) for a in quad]
    both = _pair_swap(halves, False, "swap_red", fill=True)
    both[0] = jnp.transpose(both[0][:, :577], (0, 2, 1))
    red = dict(zip(('w_in', 'w_out', 'mlp_w1', 'mlp_w2'), both))
    red['ada_w'] = g_ada_w

    small_names = [n for n in WEIGHTS if n not in BIG and n != 'ada_b']
    pair_parts = _exchange([g[n] for n in small_names] + [loss], SIBLING, False, "ag_smallpair", stage=True)
    chip_parts = _exchange(_sum_many(pair_parts, "smallpair_sum"), CHIPS, False, "ag_smallgrad", stage=True)
    summed = _sum_many(chip_parts, "smallgrad_sum")
    for n, a in zip(small_names, summed[:-1]):
        a = a.reshape(w[n].shape) if n in ('s5_b_re', 's5_b_im') else a
        red[n] = _shard_of(a, SMALL_SHARDED[n], chip) if n in SMALL_SHARDED else a
    red['ada_b'] = g_ada_b
    loss_out = summed[-1].reshape(())

    delta, new_m, new_v = {}, {}, {}
    delta['ada_w'], new_m['ada_w'], new_v['ada_w'] = adam_ada_w
    for n in BIG[1:]:
        delta[n], new_m[n], new_v[n] = _adamw(w[n], red[n], mom[n], var[n], "adamw_" + n)
    rest = [n for n in WEIGHTS if n not in BIG]
    lanes = lambda n, a: a.reshape(2, 16, 1024) if n in ('s5_b_re', 's5_b_im') else a
    outs = _adamw_many(*[[lanes(n, src[n]) for n in rest] for src in (w, red, mom, var)], "adamw_small")
    for k, n in enumerate(rest):
        delta[n], new_m[n], new_v[n] = (outs[3 * k + j].reshape(w[n].shape) for j in range(3))

    return (loss_out, grad_x[None], *[red[n] for n in WEIGHTS], *[delta[n] for n in WEIGHTS],
            *[new_m[n] for n in WEIGHTS], *[new_v[n] for n in WEIGHTS])
```

```python
import functools
import math

import jax
import jax.numpy as jnp
from jax import lax
from jax.experimental import pallas as pl
from jax.experimental.pallas import tpu as pltpu

F32 = jnp.float32
BF16 = jnp.bfloat16
HI = lax.Precision.HIGHEST

D = 1024
GW = 256
HID = 4096
EPS = 1e-6
PW = 2304
DTW = 128
SSD_L = 128
SSD_SUB = 2
SSD_SUB_BWD = 2
NH, HP, NS = 4, 64, 128
S5_P = 1024
MESH = pl.DeviceIdType.MESH

ADAM_LR, ADAM_B1, ADAM_B2, ADAM_EPS, ADAM_WD, ADAM_STEP = 0.001, 0.9, 0.999, 1e-08, 0.01, 10

NT = (((1,), (1,)), ((), ()))
TN = (((0,), (0,)), ((), ()))

WEIGHTS = ['norm_mix_w', 'norm_mlp_w', 'ada_w', 'ada_b', 'w_in', 'pool_w', 'pool_scale', 'sconv_w', 'ssd_conv_w',
           'ssd_conv_b', 'ssd_dt_bias', 'ssd_a_log', 'ssd_d', 's5_a_re', 's5_a_im', 's5_log_step', 's5_b_re', 's5_b_im',
           's5_c_re', 's5_c_im', 's5_d', 's5_glu_w', 's5_glu_b', 'branch_norm_w', 'w_out', 'mlp_w1', 'mlp_w2',
           'final_norm_w']
BIG = ('ada_w', 'w_in', 'w_out', 'mlp_w1', 'mlp_w2')
SMALL_SHARDED = {'sconv_w': 2, 'ssd_conv_w': 2, 's5_glu_w': 1}


def _cparams(n_axes, vmem_mb=48):
    return pltpu.CompilerParams(dimension_semantics=("arbitrary",) * n_axes, vmem_limit_bytes=vmem_mb * 1024 * 1024)


def _row(n):
    return pl.BlockSpec((1, n), lambda *_: (0, 0))


def _full(shape):
    nd = len(shape)
    return pl.BlockSpec(tuple(shape), lambda *_: (0,) * nd)


def _dot(a, b, dims=None, prec=None):
    if dims is None:
        dims = (((a.ndim - 1,), (0,)), ((), ()))
    return lax.dot_general(a, b, dims, preferred_element_type=F32, precision=prec)


def _bdot(a, b, dims=None):
    return _dot(a.astype(BF16), b.astype(BF16), dims)


def _sig(x):
    return jax.nn.sigmoid(x)


def _silu(x):
    return x * _sig(x)


def _dsilu(x):
    s = _sig(x)
    return s * (1.0 + x * (1.0 - s))


def _softplus(x):
    return jnp.maximum(x, 0.0) + jnp.log(1.0 + jnp.exp(-jnp.abs(x)))


_GK = math.sqrt(2.0 / math.pi)


def _gelu(x):
    return 0.5 * x * (1.0 + jnp.tanh(_GK * (x + 0.044715 * x * x * x)))


def _dgelu(x):
    th = jnp.tanh(_GK * (x + 0.044715 * x * x * x))
    return 0.5 * (1.0 + th) + 0.5 * x * (1.0 - th * th) * _GK * (1.0 + 3.0 * 0.044715 * x * x)


def _colsum(x):
    return jnp.sum(x, axis=0, keepdims=True)


def _rms(x):
    r = lax.rsqrt(jnp.mean(x * x, axis=-1, keepdims=True) + EPS)
    return r, x * r


def _rms_bwd(r, n, dn):
    return r * (dn - n * jnp.mean(dn * n, axis=-1, keepdims=True))


def _roll(x, k):
    n = x.shape[0]
    k = k % n
    return x if k == 0 else pltpu.roll(x, k, axis=0)


def _tblock(t, want=512):
    return min(t, want)


def _peer(mask):
    x, y, c = lax.axis_index("x"), lax.axis_index("y"), lax.axis_index("c")
    return (x ^ ((mask >> 2) & 1), y ^ ((mask >> 1) & 1), c ^ (mask & 1))


def _group_index(masks):
    x, y, c = lax.axis_index("x"), lax.axis_index("y"), lax.axis_index("c")
    full = 0
    for m in masks:
        full |= m
    bits = [b for b in (4, 2, 1) if full & b]

    def idx(px, py, pc):
        v = {4: px, 2: py, 1: pc}
        out = 0
        for b in bits:
            out = out * 2 + v[b]
        return out

    return idx(x, y, c), [idx(*_peer(m)) for m in masks]


def _exchange(arrs, masks, scatter, name, stage=False):
    n_arr, n_peer, n_grp = len(arrs), len(masks), len(masks) + 1

    def body(*refs):
        ins, outs = refs[:n_arr], refs[n_arr:2 * n_arr]
        send_sems, recv_sems, local_sems = refs[2 * n_arr:2 * n_arr + 3]
        if stage:
            bufs, load_sems = refs[2 * n_arr + 3:3 * n_arr + 3], refs[3 * n_arr + 3]
            loads = [pltpu.make_async_copy(ins[t], bufs[t], load_sems.at[t]) for t in range(n_arr)]
            for ld in loads:
                ld.start()
            for ld in loads:
                ld.wait()
            ins = bufs
        me, peer_idx = _group_index(masks)
        copies = []
        for t in range(n_arr):
            src_me = ins[t].at[me] if scatter else ins[t]
            loc = pltpu.make_async_copy(src_me, outs[t].at[me], local_sems.at[t])
            loc.start()
            copies.append(loc)
            for j, m in enumerate(masks):
                src = ins[t].at[peer_idx[j]] if scatter else ins[t]
                cp = pltpu.make_async_remote_copy(src_ref=src, dst_ref=outs[t].at[me], send_sem=send_sems.at[t, j],
                                                  recv_sem=recv_sems.at[t, j], device_id=_peer(m), device_id_type=MESH)
                cp.start()
                copies.append(cp)
        for cp in copies:
            cp.wait()

    hbm = pl.BlockSpec(memory_space=pl.ANY)
    out_shape = [jax.ShapeDtypeStruct((n_grp,) + (a.shape[1:] if scatter else a.shape), a.dtype) for a in arrs]
    staging = [pltpu.VMEM(a.shape, a.dtype) for a in arrs] + [pltpu.SemaphoreType.DMA((n_arr,))] if stage else []
    outs = pl.pallas_call(
        body, name=name, in_specs=[hbm] * n_arr, out_specs=[hbm] * n_arr, out_shape=out_shape,
        scratch_shapes=[pltpu.SemaphoreType.DMA((n_arr, n_peer)), pltpu.SemaphoreType.DMA((n_arr, n_peer)),
                        pltpu.SemaphoreType.DMA((n_arr,))] + staging,
        compiler_params=pltpu.CompilerParams(vmem_limit_bytes=48 * 1024 * 1024),
    )(*arrs)
    return list(outs)


def _split_copies(src_refs, land_refs, sems, scatter, per_core):
    me, peer_idx = _group_index(CHIPS)
    n = len(CHIPS) * len(src_refs)
    copies = []
    for t, (src_ref, land_ref) in enumerate(zip(src_refs, land_refs)):
        zone = land_ref.at[lax.axis_index("c")] if per_core else land_ref
        for j, m in enumerate(CHIPS):
            k = len(CHIPS) * t + j
            copies.append(pltpu.make_async_remote_copy(
                src_ref=src_ref.at[peer_idx[j]] if scatter else src_ref, dst_ref=zone.at[me], send_sem=sems[k],
                recv_sem=sems[n + k], device_id=_peer(m), device_id_type=MESH))
    return copies


def _split_start(srcs, after, scatter, name, per_core=False):
    n_arr, n_sem = len(srcs), 2 * len(CHIPS) * len(srcs)

    def body(*refs):
        src_refs, land_refs = refs[:n_arr], refs[n_arr:2 * n_arr]
        outs = refs[2 * n_arr + len(after):]
        for cp in _split_copies(src_refs, land_refs, outs[:n_sem], scatter, per_core):
            cp.start()
        outs[-1][...] = jnp.zeros_like(outs[-1])

    hbm = pl.BlockSpec(memory_space=pltpu.HBM)
    sem = pl.BlockSpec(memory_space=pltpu.SEMAPHORE)
    lands = [lax.empty(((2,) if per_core else ()) + (len(CHIPS) + 1,) + (a.shape[1:] if scatter else a.shape), a.dtype)
             for a in srcs]
    as_hbm = lambda a: pltpu.with_memory_space_constraint(a, pltpu.HBM)
    outs = pl.pallas_call(
        body, name=name,
        out_shape=(pltpu.SemaphoreType.DMA(()),) * n_sem + tuple(pltpu.HBM(a.shape, a.dtype) for a in srcs + lands)
        + (jax.ShapeDtypeStruct((8, 128), F32),),
        in_specs=(hbm,) * (2 * n_arr) + (pl.BlockSpec(memory_space=pl.ANY),) * len(after),
        out_specs=(sem,) * n_sem + (hbm,) * (2 * n_arr) + (pl.BlockSpec(memory_space=pltpu.VMEM),),
        input_output_aliases={t: n_sem + t for t in range(2 * n_arr)},
        compiler_params=pltpu.CompilerParams(has_side_effects=pltpu.SideEffectType.DATAFLOW_SIDE_EFFECTING),
    )(*[as_hbm(a) for a in srcs + lands], *after)
    return outs[:n_sem], list(outs[n_sem:n_sem + n_arr]), list(outs[n_sem + n_arr:n_sem + 2 * n_arr]), outs[-1]


def _split_wait(sems, srcs, lands, after, scatter, name, per_core=False):
    n_arr, n_sem = len(srcs), len(sems)

    def body(*refs):
        src_refs, land_refs = refs[:n_arr], refs[n_arr:2 * n_arr]
        for cp in _split_copies(src_refs, land_refs, refs[2 * n_arr:2 * n_arr + n_sem], scatter, per_core):
            cp.wait_send()
            cp.wait_recv()

    hbm = pl.BlockSpec(memory_space=pltpu.HBM)
    sem = pl.BlockSpec(memory_space=pltpu.SEMAPHORE)
    outs = pl.pallas_call(
        body, name=name, out_shape=tuple(pltpu.HBM(a.shape, a.dtype) for a in srcs + lands),
        in_specs=(hbm,) * (2 * n_arr) + (sem,) * n_sem + (pl.BlockSpec(memory_space=pl.ANY),) * len(after),
        out_specs=(hbm,) * (2 * n_arr), input_output_aliases={t: t for t in range(2 * n_arr)},
        compiler_params=pltpu.CompilerParams(has_side_effects=pltpu.SideEffectType.DATAFLOW_SIDE_EFFECTING),
    )(*srcs, *lands, *sems, *after)
    return list(outs[:n_arr]), list(outs[n_arr:])


CHIPS = (4, 2, 6)
EVERYONE = (1, 2, 3, 4, 5, 6, 7)
SIBLING = (1,)
SWAP_ROWS = 512
WIN_ROWS = 592


def _pair_swap(arrs, other_layer, name, narrow=False, fill=False):
    assert not (fill and (other_layer or narrow))
    n_arr = len(arrs)
    shapes = [a.shape[-2:] for a in arrs]
    out_dtypes = [BF16 if narrow else a.dtype for a in arrs]
    chunks = []
    for t, (rows, _) in enumerate(shapes):
        assert rows % 16 == 0
        for j, r0 in enumerate(range(0, rows, SWAP_ROWS)):
            chunks.append((t, r0, min(SWAP_ROWS, rows - r0), j % 2))

    def body(*refs):
        ins, outs = refs[:n_arr], refs[n_arr:2 * n_arr]
        bufs = refs[2 * n_arr:3 * n_arr]
        out_bufs = refs[3 * n_arr:4 * n_arr] if narrow else bufs
        load_sems, send_sems, recv_sems = refs[-3:]
        sibling = _peer(1)
        c = lax.axis_index("c")

        def load(k):
            t, r0, n, slot = chunks[k]
            src = ins[t].at[1 - c] if other_layer else ins[t].at[c] if fill else ins[t]
            return pltpu.make_async_copy(src.at[pl.ds(r0, n)], bufs[t].at[slot, pl.ds(0, n)], load_sems.at[t, slot])

        def send(k):
            t, r0, n, slot = chunks[k]
            dst = outs[t].at[c] if fill else outs[t]
            return pltpu.make_async_remote_copy(src_ref=out_bufs[t].at[slot, pl.ds(0, n)], dst_ref=dst.at[pl.ds(r0, n)],
                                                send_sem=send_sems.at[t, slot], recv_sem=recv_sems.at[t],
                                                device_id=sibling, device_id_type=MESH)

        in_flight = {}

        def drain(k):
            key = (chunks[k][0], chunks[k][3])
            if key in in_flight:
                send(in_flight.pop(key)).wait_send()

        def start_load(k):
            if not narrow:
                drain(k)
            load(k).start()

        start_load(0)
        for k in range(len(chunks)):
            t, _, n, slot = chunks[k]
            load(k).wait()
            if k + 1 < len(chunks):
                start_load(k + 1)
            if narrow:
                drain(k)
                out_bufs[t][slot, pl.ds(0, n), :] = bufs[t][slot, pl.ds(0, n), :].astype(BF16)
            send(k).start()
            in_flight[(t, slot)] = k
        for k in in_flight.values():
            send(k).wait_send()
        for t in range(n_arr):
            landed = outs[t].at[1 - c] if fill else outs[t]
            pltpu.make_async_remote_copy(src_ref=landed, dst_ref=landed, send_sem=send_sems.at[t, 0],
                                         recv_sem=recv_sems.at[t], device_id=sibling, device_id_type=MESH).wait_recv()

    hbm = pl.BlockSpec(memory_space=pl.ANY)
    outs = pl.pallas_call(
        body, name=name, in_specs=[hbm] * n_arr, out_specs=[hbm] * n_arr,
        out_shape=[jax.ShapeDtypeStruct(a.shape if fill else s, dt) for a, s, dt in zip(arrs, shapes, out_dtypes)],
        input_output_aliases={t: t for t in range(n_arr)} if fill else {},
        scratch_shapes=[pltpu.VMEM((2, min(SWAP_ROWS, s[0]), s[1]), a.dtype) for s, a in zip(shapes, arrs)]
        + ([pltpu.VMEM((2, min(SWAP_ROWS, s[0]), s[1]), BF16) for s in shapes] if narrow else [])
        + [pltpu.SemaphoreType.DMA((n_arr, 2)), pltpu.SemaphoreType.DMA((n_arr, 2)), pltpu.SemaphoreType.DMA((n_arr,))],
        compiler_params=pltpu.CompilerParams(vmem_limit_bytes=48 * 1024 * 1024),
    )(*arrs)
    return list(outs)


def _sum_lead(a, name, out_dtype):
    n = a.shape[0]
    shape = a.shape[1:]

    def body(a_ref, o_ref):
        acc = a_ref[0].astype(F32)
        for k in range(1, n):
            acc = acc + a_ref[k].astype(F32)
        o_ref[...] = acc.astype(out_dtype)

    if len(shape) == 3:
        blk = (1,) + shape[1:]
        return pl.pallas_call(
            body, name=name, grid=(shape[0],), in_specs=[pl.BlockSpec((n,) + blk, lambda i: (0, i, 0, 0))],
            out_specs=pl.BlockSpec(blk, lambda i: (i, 0, 0)), out_shape=jax.ShapeDtypeStruct(shape, out_dtype),
            compiler_params=_cparams(1),
        )(a)
    rows, cols = shape
    rb = rows
    for cand in (512, 256, 128):
        if rows % cand == 0 and rows > cand:
            rb = cand
            break
    return pl.pallas_call(
        body, name=name, grid=(rows // rb,), in_specs=[pl.BlockSpec((n, rb, cols), lambda i: (0, i, 0))],
        out_specs=pl.BlockSpec((rb, cols), lambda i: (i, 0)), out_shape=jax.ShapeDtypeStruct((rows, cols), out_dtype),
        compiler_params=_cparams(1),
    )(a)


def _pair_sum(g, recv, layer, name, out_dtype):
    _, n, r, c = g.shape

    def body(l_ref, g_ref, r_ref, o_ref):
        o_ref[...] = (g_ref[0].astype(F32) + r_ref[...].astype(F32)).astype(out_dtype)

    return pl.pallas_call(
        body, name=name,
        grid_spec=pltpu.PrefetchScalarGridSpec(
            num_scalar_prefetch=1, grid=(n,),
            in_specs=[pl.BlockSpec((1, 1, r, c), lambda i, l: (l[0], i, 0, 0)), pl.BlockSpec((1, r, c), lambda i, l: (i, 0, 0))],
            out_specs=pl.BlockSpec((1, r, c), lambda i, l: (i, 0, 0))),
        out_shape=jax.ShapeDtypeStruct((n, r, c), out_dtype), compiler_params=_cparams(1),
    )(layer, g, recv)


def _tn_matmul(a, b, name, col_major=False, into=None, layer=0):
    t, k = a.shape
    n = b.shape[1]
    tb = _tblock(t, 1024)
    kb = min(k, 1024)
    nb = min(n, 1024)
    grid = (k // kb, n // nb, t // tb)
    lead = (into is not None) + col_major

    def body(a_ref, b_ref, *rest):
        o_ref = rest[-1]
        for _ in range(lead):
            o_ref = o_ref.at[0]

        @pl.when(pl.program_id(2) == 0)
        def _():
            o_ref[...] = jnp.zeros_like(o_ref)

        o_ref[...] += _bdot(a_ref[...], b_ref[...], TN)

    if col_major:
        block, index, shape = (1, kb, nb), (lambda ki, ni: (ni, ki, 0)), (n // nb, k, nb)
    else:
        block, index, shape = (kb, nb), (lambda ki, ni: (ki, ni)), (k, n)
    in_specs = [pl.BlockSpec((tb, kb), lambda ki, ni, ti: (ti, ki)), pl.BlockSpec((tb, nb), lambda ki, ni, ti: (ti, ni))]
    if into is None:
        return pl.pallas_call(
            body, name=name, grid=grid, in_specs=in_specs, out_specs=pl.BlockSpec(block, lambda ki, ni, ti: index(ki, ni)),
            out_shape=jax.ShapeDtypeStruct(shape, F32), compiler_params=_cparams(3),
        )(a, b)
    assert into.shape == (2,) + shape
    return pl.pallas_call(
        body, name=name, grid=grid, in_specs=in_specs + [pl.BlockSpec(memory_space=pl.ANY)],
        out_specs=pl.BlockSpec((1,) + block, lambda ki, ni, ti: (layer,) + index(ki, ni)),
        out_shape=jax.ShapeDtypeStruct(into.shape, F32), input_output_aliases={2: 0}, compiler_params=_cparams(3),
    )(a, b, into)


def _sum_many(arrs, name):
    k = len(arrs)

    def body(*refs):
        for a_ref, o_ref in zip(refs[:k], refs[k:]):
            acc = a_ref[0]
            for j in range(1, a_ref.shape[0]):
                acc = acc + a_ref[j]
            o_ref[...] = acc

    return pl.pallas_call(body, name=name, grid=(1,), in_specs=[_full(a.shape) for a in arrs],
                          out_specs=[_full(a.shape[1:]) for a in arrs],
                          out_shape=[jax.ShapeDtypeStruct(a.shape[1:], F32) for a in arrs], compiler_params=_cparams(1))(*arrs)


def _adamw_math(w, g, m, v):
    m2 = ADAM_B1 * m + (1.0 - ADAM_B1) * g
    v2 = ADAM_B2 * v + (1.0 - ADAM_B2) * (g * g)
    m_hat = m2 / (1.0 - ADAM_B1 ** ADAM_STEP)
    v_hat = v2 / (1.0 - ADAM_B2 ** ADAM_STEP)
    return -ADAM_LR * (m_hat / (jnp.sqrt(v_hat) + ADAM_EPS) + ADAM_WD * w), m2, v2


def _adamw_many(ws, gs, ms, vs, name):
    n = len(ws)

    def body(*refs):
        ins, outs = refs[:4 * n], refs[4 * n:]
        for k in range(n):
            res = _adamw_math(ins[k][...], ins[n + k][...], ins[2 * n + k][...], ins[3 * n + k][...])
            for j in range(3):
                outs[3 * k + j][...] = res[j]

    out_shape = []
    for a in ws:
        out_shape += [jax.ShapeDtypeStruct(a.shape, F32)] * 3
    return pl.pallas_call(body, name=name, grid=(1,), in_specs=[_full(a.shape) for a in ws] * 4,
                          out_specs=[_full(s.shape) for s in out_shape], out_shape=out_shape,
                          compiler_params=_cparams(1))(*ws, *gs, *ms, *vs)


def _adamw(w, g, m, v, name):
    shape = w.shape
    cols = shape[-1]
    rows = int(math.prod(shape[:-1]))
    rb = rows
    for cand in (256, 128, 64, 32, 16, 8):
        if rows % cand == 0 and rows > cand:
            rb = cand
            break
    bc1 = 1.0 - ADAM_B1 ** ADAM_STEP
    bc2 = 1.0 - ADAM_B2 ** ADAM_STEP

    def body(w_ref, g_ref, m_ref, v_ref, d_ref, nm_ref, nv_ref):
        gg = g_ref[...]
        m2 = ADAM_B1 * m_ref[...] + (1.0 - ADAM_B1) * gg
        v2 = ADAM_B2 * v_ref[...] + (1.0 - ADAM_B2) * (gg * gg)
        m_hat = m2 / bc1
        v_hat = v2 / bc2
        d_ref[...] = -ADAM_LR * (m_hat / (jnp.sqrt(v_hat) + ADAM_EPS) + ADAM_WD * w_ref[...])
        nm_ref[...] = m2
        nv_ref[...] = v2

    spec = pl.BlockSpec((rb, cols), lambda i: (i, 0))
    sds = jax.ShapeDtypeStruct((rows, cols), F32)
    outs = pl.pallas_call(
        body, name=name, grid=(rows // rb,), in_specs=[spec] * 4, out_specs=[spec] * 3, out_shape=[sds] * 3,
        compiler_params=_cparams(1),
    )(*(z.reshape(rows, cols) for z in (w, g, m, v)))
    return tuple(o.reshape(shape) for o in outs)


def _ada_fwd(c_all, ada_w_sh, ada_b_sh):
    s = ada_w_sh.shape[2]
    sb = 512

    def body(c_ref, w_ref, b_ref, o_ref):
        cond = _silu(c_ref[...])
        o_ref[0] = _bdot(cond, w_ref[0]) + b_ref[0]

    return pl.pallas_call(
        body, name="ada_fwd", grid=(2, s // sb),
        in_specs=[_full((8, D)), pl.BlockSpec((1, D, sb), lambda l, j: (l, 0, j)), pl.BlockSpec((1, 1, sb), lambda l, j: (l, 0, j))],
        out_specs=pl.BlockSpec((1, 8, sb), lambda l, j: (l, 0, j)), out_shape=jax.ShapeDtypeStruct((2, 8, s), F32),
        compiler_params=_cparams(2),
    )(c_all, ada_w_sh, ada_b_sh)


def _ada_bwd(c_all, dmod_sh, dmod_all):
    s = dmod_sh.shape[2]
    sb = 512

    def body(c_ref, d_ref, o_ref):
        cond = _silu(c_ref[...])
        o_ref[0] = _bdot(cond, d_ref[0], TN)

    gw = pl.pallas_call(
        body, name="ada_bwd_w", grid=(2, s // sb),
        in_specs=[_full((8, D)), pl.BlockSpec((1, 8, sb), lambda l, j: (l, 0, j))],
        out_specs=pl.BlockSpec((1, D, sb), lambda l, j: (l, 0, j)), out_shape=jax.ShapeDtypeStruct((2, D, s), F32),
        compiler_params=_cparams(2),
    )(c_all, dmod_sh)

    def body_b(d_ref, o_ref):
        acc = d_ref[0, 0:1, :]
        for k in range(1, 8):
            acc = acc + d_ref[0, k:k + 1, :]
        o_ref[0] = acc

    gb = pl.pallas_call(
        body_b, name="ada_bwd_b", grid=(2,), in_specs=[pl.BlockSpec((1, 8, 6 * D), lambda l: (l, 0, 0))],
        out_specs=pl.BlockSpec((1, 1, 6 * D), lambda l: (l, 0, 0)), out_shape=jax.ShapeDtypeStruct((2, 1, 6 * D), F32),
        compiler_params=_cparams(1),
    )(dmod_all)
    return gw, gb.reshape(2, 6 * D)


def _f_in(h, nw, sc, sh, w_main, w_dt):
    t = h.shape[0]
    tb = _tblock(t)

    def body(h_ref, nw_ref, sc_ref, sh_ref, w_ref, wd_ref, p_ref, dt_ref, u_ref):
        _, n = _rms(h_ref[...])
        u = ((n * nw_ref[...]) * (1.0 + sc_ref[...]) + sh_ref[...]).astype(BF16)
        u_ref[...] = u
        p_ref[...] = _dot(u, w_ref[...], NT)
        dt_ref[...] = _dot(u, wd_ref[...], NT)

    return pl.pallas_call(
        body, name="f_in", grid=(t // tb,),
        in_specs=[pl.BlockSpec((tb, D), lambda i: (i, 0)), _row(D), _row(D), _row(D), _full((PW, D)), _full((DTW, D))],
        out_specs=[pl.BlockSpec((tb, PW), lambda i: (i, 0)), pl.BlockSpec((tb, DTW), lambda i: (i, 0)),
                   pl.BlockSpec((tb, D), lambda i: (i, 0))],
        out_shape=[jax.ShapeDtypeStruct((t, PW), F32), jax.ShapeDtypeStruct((t, DTW), F32), jax.ShapeDtypeStruct((t, D), BF16)],
        compiler_params=_cparams(1),
    )(h, nw, sc, sh, w_main, w_dt)


def _norm_bwd_step(du_v, x, dres_v, gated, nwv, scv, dx_ref, dsc_ref, dsh_ref, dnw_ref, dg_ref):
    r, n = _rms(x)
    scale = 1.0 + scv
    dsc_ref[...] += _colsum(du_v * (n * nwv))
    dsh_ref[...] += _colsum(du_v)
    dnw_ref[...] += _colsum(du_v * scale * n)
    dg_ref[...] += _colsum(dres_v * gated)
    dx_ref[...] = dres_v + _rms_bwd(r, n, du_v * scale * nwv)


def _b_in(dab, dz, dxbc, ds5, ddt, w_main, w_dt, x, dres, gated, nw, sc):
    t = dab.shape[0]
    tb = _tblock(t)

    def body(a_ref, z_ref, x_ref, s_ref, d_ref, w_ref, wd_ref, h_ref, dr_ref, g_ref, nw_ref, sc_ref,
             dx_ref, dsc_ref, dsh_ref, dnw_ref, dg_ref):
        @pl.when(pl.program_id(0) == 0)
        def _():
            for r in (dsc_ref, dsh_ref, dnw_ref, dg_ref):
                r[...] = jnp.zeros_like(r)

        du = _bdot(a_ref[...], w_ref[0:1024, :])
        du += _bdot(z_ref[...], w_ref[1024:1280, :])
        du += _bdot(s_ref[...], w_ref[1280:1536, :])
        du += _bdot(x_ref[...], w_ref[1536:2304, :])
        du += _bdot(d_ref[...], wd_ref[...])
        _norm_bwd_step(du, h_ref[...], dr_ref[...], g_ref[...], nw_ref[...], sc_ref[...], dx_ref, dsc_ref, dsh_ref, dnw_ref, dg_ref)

    blk = lambda n: pl.BlockSpec((tb, n), lambda i: (i, 0))
    row = jax.ShapeDtypeStruct((1, D), F32)
    return pl.pallas_call(
        body, name="b_in", grid=(t // tb,),
        in_specs=[blk(1024), blk(256), blk(768), blk(256), blk(DTW), _full((PW, D)), _full((DTW, D)),
                  blk(D), blk(D), blk(D), _row(D), _row(D)],
        out_specs=[blk(D), _row(D), _row(D), _row(D), _row(D)],
        out_shape=[jax.ShapeDtypeStruct((t, D), F32), row, row, row, row], compiler_params=_cparams(1),
    )(dab, dz, dxbc, ds5, ddt, w_main, w_dt, x, dres, gated, nw, sc)


HALO = 16


def _lane_group(shape):
    return lax.broadcasted_iota(jnp.int32, shape, 1) // 64


def _window_select(g, s2, s4, s8, s16):
    return jnp.where(g == 0, s2, jnp.where(g == 1, s4, jnp.where(g == 2, s8, s16)))


def _pool_count(t0, rows):
    g = _lane_group((rows, GW))
    win = _window_select(g, 2, 4, 8, 16)
    tt = t0 + lax.broadcasted_iota(jnp.int32, (rows, GW), 0)
    return jnp.minimum(tt + 1, win).astype(F32)


def _pool_p(v_ext, t0, tb):
    s2 = v_ext + _roll(v_ext, 1)
    s4 = s2 + _roll(s2, 2)
    s8 = s4 + _roll(s4, 4)
    s16 = s8 + _roll(s8, 8)
    ws = _window_select(_lane_group(v_ext.shape), s2, s4, s8, s16)[HALO:]
    return ws / _pool_count(t0, tb) - v_ext[HALO:]


def _sconv(q_ext, w):
    return (_roll(q_ext, 2) * w[0:1] + _roll(q_ext, 1) * w[1:2] + q_ext * w[2:3])[HALO:]


def _halo_specs(t, tb, cols, col_block):
    per = tb // HALO
    last = t // HALO - 1
    prev = pl.BlockSpec((HALO, cols), lambda i: (jnp.maximum(i * per - 1, 0), col_block))
    nxt = pl.BlockSpec((HALO, cols), lambda i: (jnp.minimum((i + 1) * per, last), col_block))
    return prev, nxt


def _f_ab(proj, pool_mat, pool_scale, sconv_w):
    t = proj.shape[0]
    tb = _tblock(t)
    prev, _ = _halo_specs(t, tb, 1024, 0)

    def body(p_ref, h_ref, pm_ref, ps_ref, sw_ref, ya_ref, yb_ref):
        i = pl.program_id(0)
        halo = jnp.where(i > 0, h_ref[...], 0.0)
        ext = jnp.concatenate([halo, p_ref[...]], axis=0)
        p = _pool_p(ext[:, 0:256], i * tb, tb)
        ya_ref[...] = _bdot(p, pm_ref[...]) * ps_ref[...]
        q_ext = ext[:, 512:768] * ext[:, 768:1024]
        yb_ref[...] = p_ref[:, 256:512] * _sconv(q_ext, sw_ref[...])

    blk = pl.BlockSpec((tb, GW), lambda i: (i, 0))
    sds = jax.ShapeDtypeStruct((t, GW), F32)
    return pl.pallas_call(
        body, name="f_ab", grid=(t // tb,),
        in_specs=[pl.BlockSpec((tb, 1024), lambda i: (i, 0)), prev, _full((GW, GW)), _row(GW), _full((3, GW))],
        out_specs=[blk, blk], out_shape=[sds, sds], compiler_params=_cparams(1),
    )(proj, proj, pool_mat, pool_scale, sconv_w)


def _b_ab(proj, dya, dyb, pool_mat, pool_scale, sconv_w):
    t = proj.shape[0]
    tb = _tblock(t)
    nb = t // tb
    prev, nxt = _halo_specs(t, tb, 1024, 0)
    _, nxt_g = _halo_specs(t, tb, GW, 0)
    n_ext = tb + HALO

    def body(p_ref, hp_ref, hn_ref, da_ref, dan_ref, db_ref, dbn_ref, pm_ref, ps_ref, sw_ref,
             o_ref, dpm_ref, dps_ref, dsw_ref):
        i = pl.program_id(0)

        @pl.when(i == 0)
        def _():
            for r in (dpm_ref, dps_ref, dsw_ref):
                r[...] = jnp.zeros_like(r)

        last = i == nb - 1
        halo = jnp.where(i > 0, hp_ref[...], 0.0)
        main = p_ref[...]
        ext = jnp.concatenate([halo, main], axis=0)
        scale = ps_ref[...]
        pm = pm_ref[...]
        p = _pool_p(ext[:, 0:256], i * tb, tb)
        da = da_ref[...]
        dps_ref[...] += _colsum(da * _bdot(p, pm))
        da_ext = jnp.concatenate([da, jnp.where(last, 0.0, dan_ref[...])], axis=0)
        dys = da_ext * scale
        dpm_ref[...] += _bdot(p, dys[:tb], TN)
        dp = _bdot(dys, pm, NT)
        dpc = dp / _pool_count(i * tb, n_ext)
        a2 = dpc + _roll(dpc, n_ext - 1)
        a4 = a2 + _roll(a2, n_ext - 2)
        a8 = a4 + _roll(a4, n_ext - 4)
        a16 = a8 + _roll(a8, n_ext - 8)
        o_ref[:, 0:256] = (_window_select(_lane_group(dpc.shape), a2, a4, a8, a16) - dp)[:tb]
        w = sw_ref[...]
        gb, gc, hh = main[:, 256:512], main[:, 512:768], main[:, 768:1024]
        q_ext = ext[:, 512:768] * ext[:, 768:1024]
        db = db_ref[...]
        o_ref[:, 256:512] = db * _sconv(q_ext, w)
        gb_next = hn_ref[:, 256:512]
        dconv = jnp.concatenate([db * gb, jnp.where(last, 0.0, dbn_ref[...] * gb_next)], axis=0)
        dq = (dconv * w[2:3] + _roll(dconv, n_ext - 1) * w[1:2] + _roll(dconv, n_ext - 2) * w[0:1])[:tb]
        o_ref[:, 512:768] = dq * hh
        o_ref[:, 768:1024] = dq * gc
        dc = dconv[:tb]
        dsw_ref[0:1, :] += _colsum(dc * _roll(q_ext, 2)[HALO:])
        dsw_ref[1:2, :] += _colsum(dc * _roll(q_ext, 1)[HALO:])
        dsw_ref[2:3, :] += _colsum(dc * q_ext[HALO:])

    blk = pl.BlockSpec((tb, GW), lambda i: (i, 0))
    return pl.pallas_call(
        body, name="b_ab", grid=(nb,),
        in_specs=[pl.BlockSpec((tb, 1024), lambda i: (i, 0)), prev, nxt, blk, nxt_g, blk, nxt_g,
                  _full((GW, GW)), _row(GW), _full((3, GW))],
        out_specs=[pl.BlockSpec((tb, 1024), lambda i: (i, 0)), _full((GW, GW)), _row(GW), _full((3, GW))],
        out_shape=[jax.ShapeDtypeStruct((t, 1024), F32), jax.ShapeDtypeStruct((GW, GW), F32),
                   jax.ShapeDtypeStruct((1, GW), F32), jax.ShapeDtypeStruct((3, GW), F32)],
        compiler_params=_cparams(1),
    )(proj, proj, proj, dya, dya, dyb, dyb, pool_mat, pool_scale, sconv_w)


CH = 8


def _ssd_conv(x, halo, w, b):
    ext = jnp.concatenate([halo, x], axis=0)
    pre = ext * w[3:4] + _roll(ext, 1) * w[2:3] + _roll(ext, 2) * w[1:2] + _roll(ext, 3) * w[0:1] + b
    return pre[CH:], ext


def _ssd_common(dt_raw, dtb, alog):
    ll = dt_raw.shape[0]
    dtv = _softplus(dt_raw + dtb)
    a_row = -jnp.exp(alog)
    r = lax.broadcasted_iota(jnp.int32, (ll, ll), 0)
    c = lax.broadcasted_iota(jnp.int32, (ll, ll), 1)
    tril = (r >= c).astype(F32)
    cs = _dot(tril, dtv * a_row, prec=HI)
    return dtv, a_row, cs, cs.T, r >= c


def _bd(a, b, ca, cb):
    return lax.dot_general(a, b, (((ca,), (cb,)), ((0,), (0,))), preferred_element_type=F32)


def _head_cols(m):
    return jnp.stack([m[:, h:h + 1] for h in range(NH)])


def _ssd_heads(act, dtv, cs, cs_t, causal):
    xs = jnp.stack([act[:, HP * h:HP * (h + 1)] for h in range(NH)])
    bm = jnp.stack([act[:, 256 + NS * (h // 2):256 + NS * (h // 2 + 1)] for h in range(NH)])
    cm = jnp.stack([act[:, 512 + NS * (h // 2):512 + NS * (h // 2 + 1)] for h in range(NH)])
    cs_c = _head_cols(cs)
    cs_r = jnp.stack([cs_t[h:h + 1, :] for h in range(NH)])
    mdec = jnp.where(causal[None], jnp.exp(jnp.minimum(cs_c - cs_r, 0.0)), 0.0)
    g2 = _bd(jnp.stack([cm[0], cm[2]]), jnp.stack([bm[0], bm[2]]), 2, 2)
    sc = jnp.stack([g2[h // 2] for h in range(NH)]) * mdec
    dt_c = _head_cols(dtv)
    xdt = xs * dt_c
    e = jnp.exp(cs_c)
    cs_last = cs_c[:, SSD_L - 1:SSD_L, :]
    wdec = jnp.exp(cs_last - cs_c)
    return xs, bm, cm, mdec, sc, dt_c, xdt, e, cs_last, wdec


def _head_scalars(row_ref):
    return jnp.stack([row_ref[0:1, h:h + 1] for h in range(NH)])


def _f_ssd(proj, dtp, conv_w, conv_b, dt_bias, a_log, d_skip):
    t = proj.shape[0]
    nc = t // SSD_L
    rows = SSD_SUB * SSD_L
    per = rows // CH

    def body(x_ref, hx_ref, dt_ref, z_ref, cw_ref, cb_ref, dtb_ref, al_ref, dk_ref, y_ref, yp_ref, sp_ref, s_ref):
        i = pl.program_id(0)

        @pl.when(i == 0)
        def _():
            s_ref[...] = jnp.zeros_like(s_ref)

        state = s_ref[...]
        dk = _head_scalars(dk_ref)
        for sub in range(SSD_SUB):
            r0 = sub * SSD_L
            rs = slice(r0, r0 + SSD_L)
            halo = jnp.where(i > 0, hx_ref[...], 0.0) if sub == 0 else x_ref[r0 - CH:r0, :]
            pre, _ = _ssd_conv(x_ref[rs, :], halo, cw_ref[...], cb_ref[...])
            act = _silu(pre)
            dtv, _, cs, cs_t, causal = _ssd_common(dt_ref[rs, :], dtb_ref[...], al_ref[...])
            xs, bm, cm, _, sc, _, xdt, e, cs_last, wdec = _ssd_heads(act, dtv, cs, cs_t, causal)
            sp_ref[sub] = state
            y = _bd(sc, xdt, 2, 1) + e * _bd(cm, state, 2, 2) + xs * dk
            for h in range(NH):
                yp_ref[rs, HP * h:HP * (h + 1)] = y[h]
            state = state * jnp.exp(cs_last) + _bd(xdt * wdec, bm, 1, 1)
            y_ref[rs, :] = yp_ref[rs, :] * _silu(z_ref[rs, :])
        s_ref[...] = state

    blk = pl.BlockSpec((rows, GW), lambda i: (i, 0))
    sds = jax.ShapeDtypeStruct((t, GW), F32)
    return pl.pallas_call(
        body, name="f_ssd", grid=(nc // SSD_SUB,),
        in_specs=[pl.BlockSpec((rows, 768), lambda i: (i, 2)),
                  pl.BlockSpec((CH, 768), lambda i: (jnp.maximum(i * per - 1, 0), 2)),
                  pl.BlockSpec((rows, DTW), lambda i: (i, 0)),
                  pl.BlockSpec((rows, GW), lambda i: (i, 4)),
                  _full((4, 768)), _row(768), _row(DTW), _row(DTW), _row(DTW)],
        out_specs=[blk, blk, pl.BlockSpec((SSD_SUB, NH, HP, NS), lambda i: (i, 0, 0, 0))],
        out_shape=[sds, sds, jax.ShapeDtypeStruct((nc, NH, HP, NS), F32)],
        scratch_shapes=[pltpu.VMEM((NH, HP, NS), F32)], compiler_params=_cparams(1),
    )(proj, proj, dtp, proj, conv_w, conv_b, dt_bias, a_log, d_skip)


def _b_ssd(proj, dtp, ypre, dyc, sprev, conv_w, conv_b, dt_bias, a_log, d_skip):
    t = proj.shape[0]
    nc = t // SSD_L
    steps = nc // SSD_SUB_BWD
    rows = SSD_SUB_BWD * SSD_L
    per = rows // CH
    n_ext = SSD_L + CH

    def chunk(sub, halo, dnext, ds_in, refs):
        (x_ref, dt_ref, z_ref, yp_ref, dy_ref, sp_ref, cw_ref, cb_ref, dtb_ref, al_ref, dk_ref,
         dz_ref, dx_ref, ddt_ref, dact_ref) = refs
        rs = slice(sub * SSD_L, (sub + 1) * SSD_L)
        dact = dact_ref.at[sub]
        w = cw_ref[...]
        pre, ext = _ssd_conv(x_ref[rs, :], halo, w, cb_ref[...])
        act = _silu(pre)
        dt_raw = dt_ref[rs, :]
        dtv, a_row, cs, cs_t, causal = _ssd_common(dt_raw, dtb_ref[...], al_ref[...])
        z = z_ref[rs, :]
        dyc_v = dy_ref[rs, :]
        dz_ref[rs, :] = dyc_v * yp_ref[rs, :] * _dsilu(z)
        dy_all = dyc_v * _silu(z)
        lane = lax.broadcasted_iota(jnp.int32, (SSD_L, DTW), 1)
        rowi = lax.broadcasted_iota(jnp.int32, (1, SSD_L, 1), 1)
        lane1 = lax.broadcasted_iota(jnp.int32, (1, DTW), 1)
        xs, bm, cm, mdec, sc, dt_c, xdt, e, cs_last, wdec = _ssd_heads(act, dtv, cs, cs_t, causal)
        dy = jnp.stack([dy_all[:, HP * h:HP * (h + 1)] for h in range(NH)])
        prev = sp_ref[sub]
        ds = ds_in
        lsum = lambda v: jnp.sum(v, axis=2, keepdims=True)
        dsc = _bd(dy, xdt, 2, 2)
        q = dsc * sc
        dg = dsc * mdec
        dxdt = _bd(sc, dy, 1, 1)
        dcs = lsum(q) - lsum(jnp.swapaxes(q, 1, 2))
        dc = _bd(dg, bm, 2, 1)
        db = _bd(dg, cm, 1, 1)
        cp = _bd(cm, prev, 2, 2)
        dcs += lsum(dy * cp) * e
        ey = e * dy
        dc += _bd(ey, prev, 2, 1)
        dprev = _bd(ey, cm, 1, 1)
        elast = jnp.exp(cs_last)
        dprev += ds * elast
        dcs_last = jnp.sum(lsum(ds * prev), axis=1, keepdims=True) * elast
        bds = _bd(bm, ds, 2, 2)
        dxdt += wdec * bds
        db += wdec * _bd(xdt, ds, 2, 1)
        dw = lsum(xdt * bds) * wdec
        dcs -= dw
        dcs_last += jnp.sum(dw, axis=1, keepdims=True)
        dcs += jnp.where(rowi == SSD_L - 1, dcs_last, 0.0)
        dxs = dxdt * dt_c + dy * _head_scalars(dk_ref)
        ddtx = lsum(dxdt * xs)
        ddk = jnp.sum(lsum(dy * xs), axis=1, keepdims=True)
        dcs_mat = jnp.zeros((SSD_L, DTW), F32)
        ddtx_mat = jnp.zeros((SSD_L, DTW), F32)
        ddk_row = jnp.zeros((1, DTW), F32)
        for h in range(NH):
            dact[:, HP * h:HP * (h + 1)] = dxs[h]
            dcs_mat = jnp.where(lane == h, dcs[h], dcs_mat)
            ddtx_mat = jnp.where(lane == h, ddtx[h], ddtx_mat)
            ddk_row = jnp.where(lane1 == h, ddk[h], ddk_row)
        for g in range(2):
            dact[:, 256 + NS * g:256 + NS * (g + 1)] = db[2 * g] + db[2 * g + 1]
            dact[:, 512 + NS * g:512 + NS * (g + 1)] = dc[2 * g] + dc[2 * g + 1]
        ds_out = dprev
        r2 = lax.broadcasted_iota(jnp.int32, (SSD_L, SSD_L), 0)
        c2 = lax.broadcasted_iota(jnp.int32, (SSD_L, SSD_L), 1)
        dadt = _dot((c2 >= r2).astype(F32), dcs_mat, prec=HI)
        ddt = jnp.where(lane < NH, (dadt * a_row + ddtx_mat) * _sig(dt_raw + dtb_ref[...]), 0.0)
        ddt_ref[rs, :] = ddt
        dpre = dact[...] * _dsilu(pre)
        dcw = jnp.concatenate([_colsum(dpre * _roll(ext, 3 - k)[CH:]) for k in range(4)], axis=0)
        dext = jnp.concatenate([dpre, dnext], axis=0)
        dx_ref[rs, :] = (dext * w[3:4] + _roll(dext, n_ext - 1) * w[2:3] + _roll(dext, n_ext - 2) * w[1:2]
                         + _roll(dext, n_ext - 3) * w[0:1])[:SSD_L]
        acc = (dcw, _colsum(dpre), _colsum(ddt), _colsum(dadt * dtv) * a_row, ddk_row)
        return dpre[0:CH], ds_out, acc

    def body(x_ref, hx_ref, dt_ref, z_ref, yp_ref, dy_ref, sp_ref, cw_ref, cb_ref, dtb_ref, al_ref, dk_ref,
             dz_ref, dx_ref, ddt_ref, dcw_ref, dcb_ref, ddtb_ref, dal_ref, ddk_ref, ds_ref, dnext_ref, dact_ref):
        i = pl.program_id(0)
        acc_refs = (dcw_ref, dcb_ref, ddtb_ref, dal_ref, ddk_ref)

        @pl.when(i == 0)
        def _():
            ds_ref[...] = jnp.zeros_like(ds_ref)
            dnext_ref[...] = jnp.zeros_like(dnext_ref)
            for r in acc_refs:
                r[...] = jnp.zeros_like(r)

        refs = (x_ref, dt_ref, z_ref, yp_ref, dy_ref, sp_ref, cw_ref, cb_ref, dtb_ref, al_ref, dk_ref, dz_ref, dx_ref, ddt_ref,
                dact_ref)
        ds = ds_ref[...]
        dnext = dnext_ref[...]
        total = None
        for sub in reversed(range(SSD_SUB_BWD)):
            if sub == 0:
                halo = jnp.where(i == steps - 1, 0.0, hx_ref[...])
            else:
                halo = x_ref[sub * SSD_L - CH:sub * SSD_L, :]
            dnext, ds, acc = chunk(sub, halo, dnext, ds, refs)
            total = acc if total is None else tuple(a + b for a, b in zip(total, acc))
        ds_ref[...] = ds
        dnext_ref[...] = dnext
        for r, v in zip(acc_refs, total):
            r[...] += v

    rev = lambda i: steps - 1 - i
    blk = lambda n, cb=0: pl.BlockSpec((rows, n), lambda i: (rev(i), cb))
    row = lambda n: jax.ShapeDtypeStruct((1, n), F32)
    return pl.pallas_call(
        body, name="b_ssd", grid=(steps,),
        in_specs=[blk(768, 2), pl.BlockSpec((CH, 768), lambda i: (jnp.maximum(rev(i) * per - 1, 0), 2)),
                  blk(DTW), blk(GW, 4), blk(GW), blk(GW), pl.BlockSpec((SSD_SUB_BWD, NH, HP, NS), lambda i: (rev(i), 0, 0, 0)),
                  _full((4, 768)), _row(768), _row(DTW), _row(DTW), _row(DTW)],
        out_specs=[blk(GW), blk(768), blk(DTW), _full((4, 768)), _row(768), _row(DTW), _row(DTW), _row(DTW)],
        out_shape=[jax.ShapeDtypeStruct((t, GW), F32), jax.ShapeDtypeStruct((t, 768), F32), jax.ShapeDtypeStruct((t, DTW), F32),
                   jax.ShapeDtypeStruct((4, 768), F32), row(768), row(DTW), row(DTW), row(DTW)],
        scratch_shapes=[pltpu.VMEM((NH, HP, NS), F32), pltpu.VMEM((CH, 768), F32), pltpu.VMEM((SSD_SUB_BWD, SSD_L, 768), F32)],
        compiler_params=_cparams(1),
    )(proj, proj, dtp, proj, ypre, dyc, sprev, conv_w, conv_b, dt_bias, a_log, d_skip)


def _s5_block(t):
    return min(t, 256)


def _seg_t():
    r = lax.broadcasted_iota(jnp.int32, (64, 1024), 0)
    c = lax.broadcasted_iota(jnp.int32, (64, 1024), 1)
    return (c // 16 == r).astype(F32)


def _s5_prep_math(a_re, a_im, lstep, b_re, b_im):
    step = jnp.exp(lstep)
    ars = a_re * step
    ais = a_im * step
    mag = jnp.exp(ars)
    lr = mag * jnp.cos(ais)
    li = mag * jnp.sin(ais)
    den = a_re * a_re + a_im * a_im
    nr = lr - 1.0
    f_re = (nr * a_re + li * a_im) / den
    f_im = (li * a_re - nr * a_im) / den
    seg = _seg_t()
    fr = _dot(f_re, seg, prec=HI)
    fi = _dot(f_im, seg, prec=HI)
    return lr, li, fr * b_re - fi * b_im, fr * b_im + fi * b_re, ars, ais


def _s5_prep(a_re, a_im, lstep, b_re, b_im):
    def body(ar, ai, ls, br, bi, lr_o, li_o, bbr_o, bbi_o, ars_o, ais_o):
        outs = _s5_prep_math(ar[...], ai[...], ls[...], br[...], bi[...])
        for o, v in zip((lr_o, li_o, bbr_o, bbi_o, ars_o, ais_o), outs):
            o[...] = v

    s64 = jax.ShapeDtypeStruct((16, 64), F32)
    s1k = jax.ShapeDtypeStruct((16, 1024), F32)
    return pl.pallas_call(body, name="s5_prep", out_shape=[s64, s64, s1k, s1k, s64, s64])(a_re, a_im, lstep, b_re, b_im)


def _s5_prep_bwd(a_re, a_im, lstep, b_re, b_im, dlr, dli, dbbr, dbbi):
    def body(ar, ai, ls, br, bi, g0, g1, g2, g3, o0, o1, o2, o3, o4):
        f = lambda *a: _s5_prep_math(*a)[:4]
        _, vjp = jax.vjp(f, ar[...], ai[...], ls[...], br[...], bi[...])
        for o, v in zip((o0, o1, o2, o3, o4), vjp((g0[...], g1[...], g2[...], g3[...]))):
            o[...] = v

    s64 = jax.ShapeDtypeStruct((16, 64), F32)
    s1k = jax.ShapeDtypeStruct((16, 1024), F32)
    return pl.pallas_call(body, name="s5_prep_bwd", out_shape=[s64, s64, jax.ShapeDtypeStruct((16, 1), F32), s1k, s1k])(
        a_re, a_im, lstep, b_re, b_im, dlr, dli, dbbr, dbbi)


SUB = 8


def _s5_tables(ars, ais):
    def body(ar, ai, tr, ti):
        rr = lax.broadcasted_iota(jnp.int32, (8 * SUB, S5_P), 0)
        seg, r = rr // SUB, rr % SUB
        step = jnp.where((seg == 1) | (seg == 4), 1, jnp.where((seg == 2) | (seg == 5), 2, 4))
        n = jnp.where(seg == 0, r + 1, jnp.where(seg == 7, SUB - r, step))
        fwd_gap = jnp.where(seg <= 3, r - step, SUB - step - 1 - r)
        gap = jnp.where((seg == 0) | (seg == 7), 0, fwd_gap)
        nf = n.astype(F32)
        mag = jnp.where(gap >= 0, jnp.exp(nf * ar[...]), 0.0)
        tr[...] = mag * jnp.cos(nf * ai[...])
        ti[...] = mag * jnp.sin(nf * ai[...])

    sds = jax.ShapeDtypeStruct((8 * SUB, S5_P), F32)
    return pl.pallas_call(body, name="s5_tables", out_shape=[sds] * 2)(ars, ais)


def _s5_table(tb_r, tb_i, k):
    return tb_r[SUB * k:SUB * (k + 1), :], tb_i[SUB * k:SUB * (k + 1), :]


def _s5_scan(bu_r, bu_i, tb_r, tb_i, c_r, c_i, lb):
    nt = lb // SUB
    sr, si = bu_r.reshape(nt, SUB, S5_P), bu_i.reshape(nt, SUB, S5_P)
    for j, k in enumerate((1, 2, 4)):
        mr, mi = _s5_table(tb_r, tb_i, 1 + j)
        tr, ti = pltpu.roll(sr, k, axis=1), pltpu.roll(si, k, axis=1)
        sr, si = sr + mr * tr - mi * ti, si + mr * ti + mi * tr
    pr, pi = _s5_table(tb_r, tb_i, 0)
    out_r, out_i = [], []
    for j in range(nt):
        a_r = sr[j] + pr * c_r - pi * c_i
        a_i = si[j] + pr * c_i + pi * c_r
        out_r.append(a_r)
        out_i.append(a_i)
        c_r, c_i = a_r[SUB - 1:SUB], a_i[SUB - 1:SUB]
    return jnp.concatenate(out_r, axis=0), jnp.concatenate(out_i, axis=0)


def _s5_rscan(g_r, g_i, tb_r, tb_i, n_r, n_i, lb):
    nt = lb // SUB
    gr, gi = g_r.reshape(nt, SUB, S5_P), g_i.reshape(nt, SUB, S5_P)
    for j, k in enumerate((1, 2, 4)):
        mr, mi = _s5_table(tb_r, tb_i, 4 + j)
        tr, ti = pltpu.roll(gr, SUB - k, axis=1), pltpu.roll(gi, SUB - k, axis=1)
        gr, gi = gr + mr * tr + mi * ti, gi + mr * ti - mi * tr
    qr, qi = _s5_table(tb_r, tb_i, 7)
    out_r, out_i = [None] * nt, [None] * nt
    for j in reversed(range(nt)):
        a_r = gr[j] + qr * n_r + qi * n_i
        a_i = gi[j] + qr * n_i - qi * n_r
        out_r[j], out_i[j] = a_r, a_i
        n_r, n_i = a_r[0:1], a_i[0:1]
    return jnp.concatenate(out_r, axis=0), jnp.concatenate(out_i, axis=0)


def _s5_y(u, sr, si, cre, cim, dsk):
    return _bdot(sr, cre) + _bdot(si, cim) + dsk * u


def _f_s5(proj, bmat, cre, cim, p_r, p_i, dsk, glu_w, glu_b):
    t = proj.shape[0]
    lb = _s5_block(t)
    nb = t // lb

    def body(u_ref, bm_ref, cr_ref, ci_ref, pr_ref, pi_ref, dk_ref, gw_ref, gb_ref, y_ref, car_ref, s_ref, st_ref):
        @pl.when(pl.program_id(0) == 0)
        def _():
            st_ref[...] = jnp.zeros_like(st_ref)

        u = u_ref[...]
        bu = _bdot(u, bm_ref[...])
        c_r, c_i = st_ref[0:1, 0:S5_P], st_ref[0:1, S5_P:]
        car_ref[0] = st_ref[0:1, :]
        sr, si = _s5_scan(bu[:, :S5_P], bu[:, S5_P:], pr_ref, pi_ref, c_r, c_i, lb)
        st_ref[0:1, 0:S5_P] = sr[lb - 1:lb]
        st_ref[0:1, S5_P:] = si[lb - 1:lb]
        sr_b, si_b = sr.astype(BF16), si.astype(BF16)
        s_ref[:, 0:S5_P] = sr_b
        s_ref[:, S5_P:] = si_b
        gel = _gelu(_s5_y(u, sr_b, si_b, cr_ref[...], ci_ref[...], dk_ref[...]))
        y_ref[...] = gel * _sig(_bdot(gel, gw_ref[...]) + gb_ref[...])

    return pl.pallas_call(
        body, name="f_s5", grid=(nb,),
        in_specs=[pl.BlockSpec((lb, GW), lambda i: (i, 5)),
                  _full((GW, 2 * S5_P)), _full((S5_P, GW)), _full((S5_P, GW)), _full((8 * SUB, S5_P)), _full((8 * SUB, S5_P)),
                  _row(GW), _full((GW, GW)), _row(GW)],
        out_specs=[pl.BlockSpec((lb, GW), lambda i: (i, 0)), pl.BlockSpec((1, 1, 2 * S5_P), lambda i: (i, 0, 0)),
                   pl.BlockSpec((lb, 2 * S5_P), lambda i: (i, 0))],
        out_shape=[jax.ShapeDtypeStruct((t, GW), F32), jax.ShapeDtypeStruct((nb, 1, 2 * S5_P), F32),
                   jax.ShapeDtypeStruct((t, 2 * S5_P), BF16)],
        scratch_shapes=[pltpu.VMEM((8, 2 * S5_P), F32)], compiler_params=_cparams(1),
    )(proj, bmat, cre, cim, p_r, p_i, dsk, glu_w, glu_b)


def _b_s5(proj, dyd, carries, states, bmat, cre, cim, p_r, p_i, dsk, glu_w, glu_b):
    t = proj.shape[0]
    lb = _s5_block(t)
    nb = t // lb

    def body(u_ref, dy_ref, car_ref, s_ref, bm_ref, cr_ref, ci_ref, pr_ref, pi_ref, dk_ref, gw_ref, gb_ref,
             du_ref, dbm_ref, dcr_ref, dci_ref, dlam_ref, ddk_ref, dgw_ref, dgb_ref, gc_ref):
        @pl.when(pl.program_id(0) == 0)
        def _():
            gc_ref[...] = jnp.zeros_like(gc_ref)
            for r in (dbm_ref, dcr_ref, dci_ref, dlam_ref, ddk_ref, dgw_ref, dgb_ref):
                r[...] = jnp.zeros_like(r)

        u = u_ref[...]
        bm = bm_ref[...]
        u_b = u.astype(BF16)
        c_r, c_i = car_ref[0, 0:1, 0:S5_P], car_ref[0, 0:1, S5_P:]
        cre_v, cim_v, dk, gw = cr_ref[...], ci_ref[...], dk_ref[...], gw_ref[...]
        sr_b, si_b = s_ref[:, 0:S5_P], s_ref[:, S5_P:]
        sr, si = sr_b.astype(F32), si_b.astype(F32)
        y = _dot(sr_b, cre_v) + _dot(si_b, cim_v) + dk * u
        gel = _gelu(y)
        gel_b = gel.astype(BF16)
        gate = _sig(_dot(gel_b, gw) + gb_ref[...])
        dout = dy_ref[...]
        t1 = dout * gel * gate * (1.0 - gate)
        t1_b = t1.astype(BF16)
        dgw_ref[...] += _dot(gel_b, t1_b, TN)
        dgb_ref[...] += _colsum(t1)
        dyv = (dout * gate + _dot(t1_b, gw, NT)) * _dgelu(y)
        dyv_b = dyv.astype(BF16)
        ddk_ref[...] += _colsum(dyv * u)
        dcr_ref[...] += _dot(sr_b, dyv_b, TN)
        dci_ref[...] += _dot(si_b, dyv_b, TN)
        gr = _dot(dyv_b, cre_v, NT)
        gi = _dot(dyv_b, cim_v, NT)
        row = lax.broadcasted_iota(jnp.int32, (lb, S5_P), 0)
        n_r, n_i = gc_ref[0:1, 0:S5_P], gc_ref[0:1, S5_P:]
        gr, gi = _s5_rscan(gr, gi, pr_ref, pi_ref, n_r, n_i, lb)
        gc_ref[0:1, 0:S5_P] = gr[0:1]
        gc_ref[0:1, S5_P:] = gi[0:1]
        gcat = jnp.concatenate([gr, gi], axis=1).astype(BF16)
        dbm_ref[...] += _dot(u_b, gcat, TN)
        du_ref[...] = dyv * dk + _dot(gcat, bm, NT)
        spr = jnp.where(row >= 1, _roll(sr, 1), c_r)
        spi = jnp.where(row >= 1, _roll(si, 1), c_i)
        dlam_ref[0:1, :] += _colsum(gr * spr + gi * spi)
        dlam_ref[1:2, :] += _colsum(gi * spr - gr * spi)

    rev = lambda i: nb - 1 - i
    return pl.pallas_call(
        body, name="b_s5", grid=(nb,),
        in_specs=[pl.BlockSpec((lb, GW), lambda i: (rev(i), 5)), pl.BlockSpec((lb, GW), lambda i: (rev(i), 0)),
                  pl.BlockSpec((1, 1, 2 * S5_P), lambda i: (rev(i), 0, 0)), pl.BlockSpec((lb, 2 * S5_P), lambda i: (rev(i), 0)),
                  _full((GW, 2 * S5_P)), _full((S5_P, GW)), _full((S5_P, GW)), _full((8 * SUB, S5_P)), _full((8 * SUB, S5_P)),
                  _row(GW), _full((GW, GW)), _row(GW)],
        out_specs=[pl.BlockSpec((lb, GW), lambda i: (rev(i), 0)), _full((GW, 2 * S5_P)), _full((S5_P, GW)), _full((S5_P, GW)),
                   _full((2, S5_P)), _row(GW), _full((GW, GW)), _row(GW)],
        out_shape=[jax.ShapeDtypeStruct((t, GW), F32), jax.ShapeDtypeStruct((GW, 2 * S5_P), F32),
                   jax.ShapeDtypeStruct((S5_P, GW), F32), jax.ShapeDtypeStruct((S5_P, GW), F32),
                   jax.ShapeDtypeStruct((2, S5_P), F32), jax.ShapeDtypeStruct((1, GW), F32),
                   jax.ShapeDtypeStruct((GW, GW), F32), jax.ShapeDtypeStruct((1, GW), F32)],
        scratch_shapes=[pltpu.VMEM((8, 2 * S5_P), F32)], compiler_params=_cparams(1),
    )(proj, dyd, carries, states, bmat, cre, cim, p_r, p_i, dsk, glu_w, glu_b)


def _group_norm(ys, bw):
    outs, stats = [], []
    for g, y in enumerate(ys):
        r, n = _rms(y)
        stats.append((r, n))
        outs.append(n * bw[:, GW * g:GW * (g + 1)])
    return jnp.concatenate(outs, axis=1), stats


def _f_out(ya, yb, yc, yd, bw, wts, l, h, g1):
    t = h.shape[0]
    tb = _tblock(t)

    def body(a_ref, b_ref, c_ref, d_ref, bw_ref, w_ref, h_ref, g_ref, h2_ref, o_ref, cat_ref):
        cat, _ = _group_norm([a_ref[...], b_ref[...], c_ref[...], d_ref[...]], bw_ref[...])
        catb = cat.astype(BF16)
        cat_ref[...] = catb
        o = _dot(catb, w_ref[0].reshape(D, D))
        o_ref[...] = o.astype(BF16)
        h2_ref[...] = h_ref[...] + g_ref[...] * o

    yblk = pl.BlockSpec((tb, GW), lambda i: (i, 0))
    blk = pl.BlockSpec((tb, D), lambda i: (i, 0))
    return pl.pallas_call(
        body, name="f_out", grid=(t // tb,), in_specs=[yblk] * 4 + [_row(D), _wout_spec(l), blk, _row(D)],
        out_specs=[blk, blk, blk],
        out_shape=[jax.ShapeDtypeStruct((t, D), F32), jax.ShapeDtypeStruct((t, D), BF16), jax.ShapeDtypeStruct((t, D), BF16)],
        compiler_params=_cparams(1),
    )(ya, yb, yc, yd, bw, wts, h, g1)


def _b_out(dv, h2, dh3, m, nw2, sc2, ya, yb, yc, yd, bw, wts, l, g1):
    t = dv.shape[0]
    tb = _tblock(t)

    def body(dv_ref, h2_ref, dh3_ref, m_ref, nw_ref, sc_ref, a_ref, b_ref, c_ref, d_ref, bw_ref, w_ref, g_ref,
             dh_ref, dsc_ref, dsh_ref, dnw_ref, dg_ref, da_ref, db_ref, dc_ref, dd_ref, do_ref, dbw_ref):
        @pl.when(pl.program_id(0) == 0)
        def _():
            for r in (dsc_ref, dsh_ref, dnw_ref, dg_ref, dbw_ref):
                r[...] = jnp.zeros_like(r)

        _norm_bwd_step(dv_ref[...], h2_ref[...], dh3_ref[...], m_ref[...], nw_ref[...], sc_ref[...],
                       dh_ref, dsc_ref, dsh_ref, dnw_ref, dg_ref)
        do = (dh_ref[...] * g_ref[...]).astype(BF16)
        do_ref[...] = do
        dcat = _dot(do, w_ref[0].reshape(D, D), NT)
        bw_v = bw_ref[...]
        for g, (y_ref, dy_ref) in enumerate(((a_ref, da_ref), (b_ref, db_ref), (c_ref, dc_ref), (d_ref, dd_ref))):
            r, n = _rms(y_ref[...])
            dc = dcat[:, GW * g:GW * (g + 1)]
            dbw_ref[:, GW * g:GW * (g + 1)] += _colsum(dc * n)
            dy_ref[...] = _rms_bwd(r, n, dc * bw_v[:, GW * g:GW * (g + 1)])

    yblk = pl.BlockSpec((tb, GW), lambda i: (i, 0))
    blk = pl.BlockSpec((tb, D), lambda i: (i, 0))
    ysd = jax.ShapeDtypeStruct((t, GW), F32)
    row = jax.ShapeDtypeStruct((1, D), F32)
    return pl.pallas_call(
        body, name="b_out", grid=(t // tb,),
        in_specs=[blk] * 4 + [_row(D), _row(D)] + [yblk] * 4 + [_row(D), _wout_spec(l), _row(D)],
        out_specs=[blk, _row(D), _row(D), _row(D), _row(D)] + [yblk] * 4 + [blk, _row(D)],
        out_shape=[jax.ShapeDtypeStruct((t, D), F32), row, row, row, row] + [ysd] * 4 + [jax.ShapeDtypeStruct((t, D), BF16), row],
        compiler_params=_cparams(1),
    )(dv, h2, dh3, m, nw2, sc2, ya, yb, yc, yd, bw, wts, g1)


HB = 512
MLP_ROWS = 1024


ROW_W1, ROW_W2, ROW_WOUT, ROW_WIN = 0, D, D + HID // 4, D + HID // 4 + D // 4
PACK_ROWS = ROW_WIN + WIN_ROWS


def _w1_spec(l):
    per = HID // 4 // HB
    return pl.BlockSpec((1, 1, D, HB), lambda i, k: (l, k // per, ROW_W1 // D, k % per))


def _w2_spec(l):
    per = HID // 4 // HB
    return pl.BlockSpec((1, 1, HB, D), lambda i, k: (l, k // per, ROW_W2 // HB + k % per, 0))


def _wout_spec(l):
    return pl.BlockSpec((1, 4, D // 4, D), lambda i: (l, 0, ROW_WOUT // (D // 4), 0))


def _f_mlp(h2, nw, sc, sh, g2, wts, l):
    t = h2.shape[0]
    tb = _tblock(t, MLP_ROWS)
    nk = HID // HB

    def body(h_ref, nw_ref, sc_ref, sh_ref, g_ref, w1_ref, w2_ref, h3_ref, m_ref, a_ref, v_ref, acc_ref):
        k = pl.program_id(1)

        @pl.when(k == 0)
        def _():
            _, n = _rms(h_ref[...])
            v_ref[...] = ((n * nw_ref[...]) * (1.0 + sc_ref[...]) + sh_ref[...]).astype(BF16)
            acc_ref[...] = jnp.zeros_like(acc_ref)

        a = _dot(v_ref[...], w1_ref[0, 0])
        a_ref[...] = a.astype(BF16)
        ra = jnp.maximum(a, 0.0)
        acc_ref[...] += _dot((ra * ra).astype(BF16), w2_ref[0, 0])

        @pl.when(k == nk - 1)
        def _():
            m = acc_ref[...]
            m_ref[...] = m.astype(BF16)
            h3_ref[...] = h_ref[...] + g_ref[...] * m

    blk = pl.BlockSpec((tb, D), lambda i, k: (i, 0))
    return pl.pallas_call(
        body, name="f_mlp", grid=(t // tb, nk),
        in_specs=[blk, _row(D), _row(D), _row(D), _row(D), _w1_spec(l), _w2_spec(l)],
        out_specs=[blk, blk, pl.BlockSpec((tb, HB), lambda i, k: (i, k)), blk],
        out_shape=[jax.ShapeDtypeStruct((t, D), F32), jax.ShapeDtypeStruct((t, D), BF16), jax.ShapeDtypeStruct((t, HID), BF16),
                   jax.ShapeDtypeStruct((t, D), BF16)],
        scratch_shapes=[pltpu.VMEM((tb, D), F32)], compiler_params=_cparams(2),
    )(h2, nw, sc, sh, g2, wts, wts)


def _b_mlp(dh3, a, g2, wts, l):
    t = dh3.shape[0]
    tb = _tblock(t, MLP_ROWS)
    nk = HID // HB

    def body(dh_ref, a_ref, g_ref, w1_ref, w2_ref, dv_ref, da_ref, act_ref, dm_ref):
        k = pl.program_id(1)
        dm = (dh_ref[...] * g_ref[...]).astype(BF16)

        @pl.when(k == 0)
        def _():
            dm_ref[...] = dm
            dv_ref[...] = jnp.zeros_like(dv_ref)

        ra = jnp.maximum(a_ref[...].astype(F32), 0.0)
        act_ref[...] = (ra * ra).astype(BF16)
        da = (_dot(dm, w2_ref[0, 0], NT) * (2.0 * ra)).astype(BF16)
        da_ref[...] = da
        dv_ref[...] += _dot(da, w1_ref[0, 0], NT)

    blk = pl.BlockSpec((tb, D), lambda i, k: (i, 0))
    hblk = pl.BlockSpec((tb, HB), lambda i, k: (i, k))
    return pl.pallas_call(
        body, name="b_mlp", grid=(t // tb, nk),
        in_specs=[blk, hblk, _row(D), _w1_spec(l), _w2_spec(l)],
        out_specs=[blk, hblk, hblk, blk],
        out_shape=[jax.ShapeDtypeStruct((t, D), F32), jax.ShapeDtypeStruct((t, HID), BF16), jax.ShapeDtypeStruct((t, HID), BF16),
                   jax.ShapeDtypeStruct((t, D), BF16)],
        compiler_params=_cparams(2),
    )(dh3, a, g2, wts, wts)


def _b_final(h, tgt, fw):
    t = h.shape[0]
    tb = _tblock(t)

    def body(h_ref, t_ref, w_ref, dh_ref, loss_ref, dfw_ref):
        @pl.when(pl.program_id(0) == 0)
        def _():
            loss_ref[...] = jnp.zeros_like(loss_ref)
            dfw_ref[...] = jnp.zeros_like(dfw_ref)

        r, n = _rms(h_ref[...])
        wv = w_ref[...]
        err = n * wv - t_ref[...]
        loss_ref[...] += jnp.sum(err * err, keepdims=True) * (0.5 / D)
        dy = err * (1.0 / D)
        dfw_ref[...] += _colsum(dy * n)
        dh_ref[...] = _rms_bwd(r, n, dy * wv)

    blk = pl.BlockSpec((tb, D), lambda i: (i, 0))
    return pl.pallas_call(
        body, name="b_final", grid=(t // tb,), in_specs=[blk, blk, _row(D)], out_specs=[blk, _row(1), _row(D)],
        out_shape=[jax.ShapeDtypeStruct((t, D), F32), jax.ShapeDtypeStruct((1, 1), F32), jax.ShapeDtypeStruct((1, D), F32)],
        compiler_params=_cparams(1),
    )(h, tgt, fw)


def _eye(n):
    return jnp.eye(n, dtype=F32)


def _pool_embed(pool_w):
    return jnp.einsum('gcd,gk->gckd', pool_w, _eye(4)).reshape(GW, GW)


def _pool_extract(m):
    return jnp.einsum('gcgd->gcd', m.reshape(4, 64, 4, 64))


def _bmat_embed(bb):
    return jnp.einsum('gph,gk->ghkp', bb, _eye(16)).reshape(GW, S5_P)


def _bmat_extract(m):
    return jnp.einsum('ghgp->gph', m.reshape(16, 16, 16, 64))


def _cmat_embed(cc):
    return jnp.einsum('ghp,gk->kpgh', cc, _eye(16)).reshape(S5_P, GW)


def _cmat_extract(m):
    return jnp.einsum('gpgh->ghp', m.reshape(16, 64, 16, 16))


def _pad_lanes(v, n=DTW):
    return jnp.pad(v.reshape(1, -1), ((0, 0), (0, n - v.shape[-1])))


def _w_in_layout(w_in_t):
    w_main = jnp.concatenate([w_in_t[:1280], w_in_t[2052:2308], w_in_t[1280:2048]], axis=0)
    return w_main, jnp.pad(w_in_t[2048:2052], ((0, DTW - 4), (0, 0)))


def _layer_params(p, l, mod, w_in, rest):
    q = {'rest': rest, 'l': l}
    q['mod'] = [mod[k:k + 1] for k in range(6)]
    q['nw1'] = p['norm_mix_w'][l:l + 1]
    q['nw2'] = p['norm_mlp_w'][l:l + 1]
    q['w_main'], q['w_dt'] = _w_in_layout(w_in)
    q['pool_mat'] = _pool_embed(p['pool_w'][l]).astype(BF16)
    q['pool_scale'] = p['pool_scale'][l:l + 1]
    q['sconv_w'] = p['sconv_w'][l]
    q['conv_w'] = p['ssd_conv_w'][l]
    q['conv_b'] = p['ssd_conv_b'][l:l + 1]
    q['dt_bias'] = _pad_lanes(p['ssd_dt_bias'][l])
    q['a_log'] = _pad_lanes(p['ssd_a_log'][l])
    q['ssd_d'] = _pad_lanes(p['ssd_d'][l])
    q['s5_raw'] = (p['s5_a_re'][l], p['s5_a_im'][l], p['s5_log_step'][l].reshape(16, 1),
                   p['s5_b_re'][l].reshape(16, 1024), p['s5_b_im'][l].reshape(16, 1024))
    q['cre'] = _cmat_embed(p['s5_c_re'][l]).astype(BF16)
    q['cim'] = (-_cmat_embed(p['s5_c_im'][l])).astype(BF16)
    q['s5_d'] = p['s5_d'][l:l + 1]
    q['glu_w'] = p['s5_glu_w'][l].astype(BF16)
    q['glu_b'] = p['s5_glu_b'][l:l + 1]
    q['bw'] = p['branch_norm_w'][l:l + 1]
    return q


def _layer_fwd(h, q):
    sh1, sc1, g1, sh2, sc2, g2 = q['mod']
    t = h.shape[0]
    s = {'h': h}
    s['proj'], s['dtp'], s['u'] = _f_in(h, q['nw1'], sc1, sh1, q['w_main'], q['w_dt'])
    s['ya'], s['yb'] = _f_ab(s['proj'], q['pool_mat'], q['pool_scale'], q['sconv_w'])
    s['yc'], s['ypre'], s['sprev'] = _f_ssd(s['proj'], s['dtp'], q['conv_w'], q['conv_b'], q['dt_bias'], q['a_log'], q['ssd_d'])
    lr, li, bbr, bbi, ars, ais = _s5_prep(*q['s5_raw'])
    s['bmat'] = jnp.concatenate([_bmat_embed(bbr.reshape(16, 64, 16)), _bmat_embed(bbi.reshape(16, 64, 16))],
                                axis=1).astype(BF16)
    s['tables'] = _s5_tables(ars.reshape(1, S5_P), ais.reshape(1, S5_P))
    s['yd'], s['carries'], s['states'] = _f_s5(s['proj'], s['bmat'], q['cre'], q['cim'], s['tables'][0], s['tables'][1],
                                  q['s5_d'], q['glu_w'], q['glu_b'])
    q['wts'] = q['rest']((s['ya'], s['yc'], s['yd']))
    s['h2'], s['o'], s['cat'] = _f_out(s['ya'], s['yb'], s['yc'], s['yd'], q['bw'], q['wts'], q['l'], h, g1)
    h3, s['m'], s['a'], s['v'] = _f_mlp(s['h2'], q['nw2'], sc2, sh2, g2, q['wts'], q['l'])
    return h3, s


STACKED = {'mlp_w1': (2, 4, D, HID // 4), 'mlp_w2': (2, HID, D), 'w_out': (2, D, D)}


def _layer_bwd(dh3, q, s, l, stacked, early=None):
    sh1, sc1, g1, sh2, sc2, g2 = q['mod']
    g = {}
    dv, da, act, dm = _b_mlp(dh3, s['a'], g2, q['wts'], l)
    g['mlp_w1'] = _tn_matmul(s['v'], da, "dw1", col_major=True, into=stacked['mlp_w1'], layer=l)
    g['mlp_w2'] = _tn_matmul(act, dm, "dw2", into=stacked['mlp_w2'], layer=l)
    dh2, dsc2, dsh2, dnw2, dg2, dya, dyb, dyc, dyd, do, dbw = _b_out(
        dv, s['h2'], dh3, s['m'], q['nw2'], sc2, s['ya'], s['yb'], s['yc'], s['yd'], q['bw'], q['wts'], l, g1)
    g['w_out'] = _tn_matmul(s['cat'], do, "dwout", into=stacked['w_out'], layer=l)
    g['branch_norm_w'] = dbw[0]
    if early is not None:
        zero = early(g)[0, 0]
        q = dict(q, pool_scale=q['pool_scale'] + zero, conv_b=q['conv_b'] + zero, s5_d=q['s5_d'] + zero)
    dab, dpm, dps, dsw = _b_ab(s['proj'], dya, dyb, q['pool_mat'], q['pool_scale'], q['sconv_w'])
    g['pool_w'] = _pool_extract(dpm)
    g['pool_scale'] = dps[0]
    g['sconv_w'] = dsw
    dz, dxbc, ddt, dcw, dcb, ddtb, dal, ddk = _b_ssd(s['proj'], s['dtp'], s['ypre'], dyc, s['sprev'], q['conv_w'],
                                                     q['conv_b'], q['dt_bias'], q['a_log'], q['ssd_d'])
    g['ssd_conv_w'] = dcw
    g['ssd_conv_b'] = dcb[0]
    g['ssd_dt_bias'] = ddtb[0, :4]
    g['ssd_a_log'] = dal[0, :4]
    g['ssd_d'] = ddk[0, :4]
    tb = s['tables']
    ds5, dbmat, dcre, dcim, dlam, dd5, dgw, dgb = _b_s5(s['proj'], dyd, s['carries'], s['states'], s['bmat'], q['cre'], q['cim'],
                                                        tb[0], tb[1], q['s5_d'], q['glu_w'], q['glu_b'])
    g['s5_c_re'] = _cmat_extract(dcre)
    g['s5_c_im'] = -_cmat_extract(dcim)
    g['s5_d'] = dd5[0]
    g['s5_glu_w'] = dgw
    g['s5_glu_b'] = dgb[0]
    dbbr = _bmat_extract(dbmat[:, :S5_P]).reshape(16, 1024)
    dbbi = _bmat_extract(dbmat[:, S5_P:]).reshape(16, 1024)
    dar, dai, dls, dbr, dbi = _s5_prep_bwd(*q['s5_raw'], dlam[0].reshape(16, 64), dlam[1].reshape(16, 64), dbbr, dbbi)
    g['s5_a_re'], g['s5_a_im'], g['s5_log_step'] = dar, dai, dls[:, 0]
    g['s5_b_re'], g['s5_b_im'] = dbr, dbi
    dh, dsc1, dsh1, dnw1, dg1 = _b_in(dab, dz, dxbc, ds5, ddt, q['w_main'], q['w_dt'], s['h'], dh2, s['o'], q['nw1'], sc1)
    u = s['u']
    head = jnp.concatenate([_tn_matmul(dab, u, "dwin_ab"), _tn_matmul(dz, u, "dwin_z"), _tn_matmul(dxbc, u, "dwin_xbc"),
                            _tn_matmul(ddt, u, "dwin_dt")[:8]], axis=0)
    full = lax.dynamic_update_slice(jnp.zeros((2308, D), F32), head, (0, 0))
    g['w_in'] = lax.dynamic_update_slice(full, _tn_matmul(ds5, u, "dwin_s5"), (2052, 0))
    g['norm_mix_w'] = dnw1[0]
    g['norm_mlp_w'] = dnw2[0]
    dmod = jnp.concatenate([dsh1, dsc1, dg1, dsh2, dsc2, dg2], axis=1)
    return dh, g, dmod


def _local_step(x, tgt, p, mod, w_in_of, rest_of, early=None):
    h = x
    qs, saved = [], []
    for l in range(2):
        qs.append(_layer_params(p, l, mod[l], w_in_of(l), functools.partial(rest_of, l)))
        h, s = _layer_fwd(h, qs[l])
        saved.append(s)
    dh, loss, dfw = _b_final(h, tgt, p['final_norm_w'].reshape(1, D))
    grads = [None, None]
    dmods = [None, None]
    dh, grads[1], dmods[1] = _layer_bwd(dh, qs[1], saved[1], 1, {k: lax.empty(shp, F32) for k, shp in STACKED.items()})
    dh, grads[0], dmods[0] = _layer_bwd(dh, qs[0], saved[0], 0, grads[1], early)
    out = {k: jnp.stack([grads[0][k], grads[1][k]]) for k in grads[0] if k not in STACKED}
    if early is None:
        out.update({k: grads[0][k] for k in STACKED})
    out['final_norm_w'] = dfw[0]
    return loss, dh, out, jnp.concatenate(dmods, axis=0)


def _shard_of(a, axis, k):
    n = a.shape[axis] // 4
    return lax.dynamic_slice_in_dim(a, k * n, n, axis)


def kernel(x, c, norm_mix_w, norm_mlp_w, ada_w, ada_b, w_in, pool_w, pool_scale, sconv_w, ssd_conv_w, ssd_conv_b, ssd_dt_bias, ssd_a_log, ssd_d, s5_a_re, s5_a_im, s5_log_step, s5_b_re, s5_b_im, s5_c_re, s5_c_im, s5_d, s5_glu_w, s5_glu_b, branch_norm_w, w_out, mlp_w1, mlp_w2, final_norm_w, loss_target, m_norm_mix_w, m_norm_mlp_w, m_ada_w, m_ada_b, m_w_in, m_pool_w, m_pool_scale, m_sconv_w, m_ssd_conv_w, m_ssd_conv_b, m_ssd_dt_bias, m_ssd_a_log, m_ssd_d, m_s5_a_re, m_s5_a_im, m_s5_log_step, m_s5_b_re, m_s5_b_im, m_s5_c_re, m_s5_c_im, m_s5_d, m_s5_glu_w, m_s5_glu_b, m_branch_norm_w, m_w_out, m_mlp_w1, m_mlp_w2, m_final_norm_w, v_norm_mix_w, v_norm_mlp_w, v_ada_w, v_ada_b, v_w_in, v_pool_w, v_pool_scale, v_sconv_w, v_ssd_conv_w, v_ssd_conv_b, v_ssd_dt_bias, v_ssd_a_log, v_ssd_d, v_s5_a_re, v_s5_a_im, v_s5_log_step, v_s5_b_re, v_s5_b_im, v_s5_c_re, v_s5_c_im, v_s5_d, v_s5_glu_w, v_s5_glu_b, v_branch_norm_w, v_w_out, v_mlp_w1, v_mlp_w2, v_final_norm_w):
    loc = locals()
    w = {n: loc[n] for n in WEIGHTS}
    mom = {n: loc['m_' + n] for n in WEIGHTS}
    var = {n: loc['v_' + n] for n in WEIGHTS}
    ix, iy, ic = lax.axis_index("x"), lax.axis_index("y"), lax.axis_index("c")
    chip = 2 * ix + iy
    dev = 4 * ix + 2 * iy + ic

    mine_of = lambda a: lax.dynamic_index_in_dim(a.astype(BF16), ic, axis=0, keepdims=False)
    pad_in = lambda a: jnp.pad(a.T, ((0, WIN_ROWS - 577), (0, 0)))
    shard = jnp.concatenate([mine_of(w['mlp_w1']), mine_of(w['mlp_w2']), mine_of(w['w_out']), pad_in(mine_of(w['w_in']))], axis=0)

    (c_all,) = _exchange([c], EVERYONE, False, "ag_cond", stage=True)
    c_all = c_all.reshape(8, D)
    small_sh = _exchange([w[n] for n in SMALL_SHARDED], CHIPS, False, "ag_small")
    (w_in0,) = _exchange([pad_in(w['w_in'][0].astype(BF16))], CHIPS, False, "ag_win0")
    p = {n: w[n] for n in WEIGHTS if n not in BIG}
    for n, g in zip(SMALL_SHARDED, small_sh):
        ax = SMALL_SHARDED[n]
        p[n] = jnp.concatenate([g[k] for k in range(4)], axis=ax)

    def w_in_full(sh):
        return sh[:, :577].reshape(4 * 577, D)

    big = {}

    def fetch(after):
        if not big:
            (mine,), (got,) = _split_wait(sems, shard_thru, land, after, False, "ag_big_wait", per_core=True)
            got = lax.dynamic_update_slice(got, mine[None, None], (ic, chip, 0, 0))
            (both,) = _pair_swap([got.reshape(2, -1, D)], False, "swap_big", fill=True)
            big['both'] = both.reshape(got.shape)
        return big['both']

    def w_in_of(l):
        return w_in_full(w_in0) if l == 0 else w_in_full(fetch(None)[1, :, ROW_WIN:])

    def rest_of(l, after):
        return fetch(after)

    ada_b_sh = _shard_of(w['ada_b'], 1, chip).reshape(2, 1, 6 * D // 4)
    mod_sh = _ada_fwd(c_all, w['ada_w'], ada_b_sh)
    (mod_all,) = _exchange([mod_sh], CHIPS, False, "ag_mod", stage=True)
    mine = lax.dynamic_index_in_dim(mod_all, dev, axis=2, keepdims=False)
    sems, shard_thru, land, token = _split_start([shard], [mod_all, w_in0] + small_sh, False, "ag_big_start", per_core=True)
    mod = jnp.transpose(mine, (1, 0, 2)).reshape(2, 6, D) + token[0, 0]

    layer = ic.astype(jnp.int32).reshape(1)
    flight = {}

    def early(g0):
        gws = [g0['w_out'].reshape(2, 4, 256, D), g0['mlp_w1'], g0['mlp_w2'].reshape(2, 4, 1024, D)]
        got = _pair_swap([a.reshape(2, -1, D) for a in gws], True, "swap_grad", narrow=True)
        pair = [_pair_sum(a, b.reshape(a.shape[1:]), layer, "pair_sum%d" % (k + 1), BF16) for k, (a, b) in enumerate(zip(gws, got))]
        flight['sems'], flight['srcs'], flight['lands'], token = _split_start(pair, [], True, "rs_start")
        return token

    loss, grad_x, g, dmod = _local_step(x[0], loss_target[0], p, mod, w_in_of, rest_of, early)

    (dmod_all,) = _exchange([dmod], EVERYONE, False, "ag_dmod", stage=True)
    dmod_all = jnp.transpose(dmod_all, (1, 0, 2))

    gw_in = jnp.pad(g['w_in'].reshape(2, 4, 577, D), ((0, 0), (0, 0), (0, WIN_ROWS - 577), (0, 0)))
    (got_in,) = _pair_swap([gw_in.reshape(2, -1, D)], True, "swap_grad_in", narrow=True)
    pair_in = _pair_sum(gw_in, got_in.reshape(gw_in.shape[1:]), layer, "pair_sum0", BF16)
    in_sems, in_srcs, in_lands, in_token = _split_start([pair_in], [dmod_all], True, "rs_in_start")

    def chip_sum(land, mine, name):
        own = lax.dynamic_index_in_dim(mine, chip, axis=0, keepdims=True)
        return _sum_lead(lax.dynamic_update_slice(land, own, (chip, 0, 0)), name, F32)

    sent, lands = _split_wait(flight['sems'], flight['srcs'], flight['lands'], [grad_x, in_token], True, "rs_wait")
    quad = [chip_sum(land, mine, "rs_chip_sum%d" % (k + 1)) for k, (land, mine) in enumerate(zip(lands, sent))]
    g_ada_w, g_ada_b = _ada_bwd(c_all, _shard_of(dmod_all, 2, chip), dmod_all)
    adam_ada_w = _adamw(w['ada_w'], g_ada_w, mom['ada_w'], var['ada_w'], "adamw_ada_w")
    (sent_in,), (land_in,) = _split_wait(in_sems, in_srcs, in_lands, quad + [adam_ada_w[0]], True, "rs_in_wait")
    quad = [chip_sum(land_in, sent_in, "rs_chip_sum0")] + quad
    halves = [lax.dynamic_update_slice(lax.empty((2,) + a.shape, F32), a[None], (ic, 0, 0)) for a in quad]
    both = _pair_swap(halves, False, "swap_red", fill=True)
    both[0] = jnp.transpose(both[0][:, :577], (0, 2, 1))
    red = dict(zip(('w_in', 'w_out', 'mlp_w1', 'mlp_w2'), both))
    red['ada_w'] = g_ada_w

    small_names = [n for n in WEIGHTS if n not in BIG and n != 'ada_b']
    pair_parts = _exchange([g[n] for n in small_names] + [loss], SIBLING, False, "ag_smallpair", stage=True)
    chip_parts = _exchange(_sum_many(pair_parts, "smallpair_sum"), CHIPS, False, "ag_smallgrad", stage=True)
    summed = _sum_many(chip_parts, "smallgrad_sum")
    for n, a in zip(small_names, summed[:-1]):
        a = a.reshape(w[n].shape) if n in ('s5_b_re', 's5_b_im') else a
        red[n] = _shard_of(a, SMALL_SHARDED[n], chip) if n in SMALL_SHARDED else a
    red['ada_b'] = g_ada_b
    loss_out = summed[-1].reshape(())

    delta, new_m, new_v = {}, {}, {}
    delta['ada_w'], new_m['ada_w'], new_v['ada_w'] = adam_ada_w
    for n in BIG[1:]:
        delta[n], new_m[n], new_v[n] = _adamw(w[n], red[n], mom[n], var[n], "adamw_" + n)
    rest = [n for n in WEIGHTS if n not in BIG]
    lanes = lambda n, a: a.reshape(2, 16, 1024) if n in ('s5_b_re', 's5_b_im') else a
    outs = _adamw_many(*[[lanes(n, src[n]) for n in rest] for src in (w, red, mom, var)], "adamw_small")
    for k, n in enumerate(rest):
        delta[n], new_m[n], new_v[n] = (outs[3 * k + j].reshape(w[n].shape) for j in range(3))

    return (loss_out, grad_x[None], *[red[n] for n in WEIGHTS], *[delta[n] for n in WEIGHTS],
            *[new_m[n] for n in WEIGHTS], *[new_v[n] for n in WEIGHTS])
```

```python
import functools
import math

import jax
import jax.numpy as jnp
from jax import lax
from jax.experimental import pallas as pl
from jax.experimental.pallas import tpu as pltpu

F32 = jnp.float32
BF16 = jnp.bfloat16
HI = lax.Precision.HIGHEST

D = 1024
GW = 256
HID = 4096
EPS = 1e-6
PW = 2304
DTW = 128
SSD_L = 128
SSD_SUB = 2
SSD_SUB_BWD = 2
NH, HP, NS = 4, 64, 128
S5_P = 1024
MESH = pl.DeviceIdType.MESH

ADAM_LR, ADAM_B1, ADAM_B2, ADAM_EPS, ADAM_WD, ADAM_STEP = 0.001, 0.9, 0.999, 1e-08, 0.01, 10

NT = (((1,), (1,)), ((), ()))
TN = (((0,), (0,)), ((), ()))

WEIGHTS = ['norm_mix_w', 'norm_mlp_w', 'ada_w', 'ada_b', 'w_in', 'pool_w', 'pool_scale', 'sconv_w', 'ssd_conv_w',
           'ssd_conv_b', 'ssd_dt_bias', 'ssd_a_log', 'ssd_d', 's5_a_re', 's5_a_im', 's5_log_step', 's5_b_re', 's5_b_im',
           's5_c_re', 's5_c_im', 's5_d', 's5_glu_w', 's5_glu_b', 'branch_norm_w', 'w_out', 'mlp_w1', 'mlp_w2',
           'final_norm_w']
BIG = ('ada_w', 'w_in', 'w_out', 'mlp_w1', 'mlp_w2')
SMALL_SHARDED = {'sconv_w': 2, 'ssd_conv_w': 2, 's5_glu_w': 1}


def _cparams(n_axes, vmem_mb=48):
    return pltpu.CompilerParams(dimension_semantics=("arbitrary",) * n_axes, vmem_limit_bytes=vmem_mb * 1024 * 1024)


def _row(n):
    return pl.BlockSpec((1, n), lambda *_: (0, 0))


def _full(shape):
    nd = len(shape)
    return pl.BlockSpec(tuple(shape), lambda *_: (0,) * nd)


def _dot(a, b, dims=None, prec=None):
    if dims is None:
        dims = (((a.ndim - 1,), (0,)), ((), ()))
    return lax.dot_general(a, b, dims, preferred_element_type=F32, precision=prec)


def _bdot(a, b, dims=None):
    return _dot(a.astype(BF16), b.astype(BF16), dims)


def _sig(x):
    return jax.nn.sigmoid(x)


def _silu(x):
    return x * _sig(x)


def _dsilu(x):
    s = _sig(x)
    return s * (1.0 + x * (1.0 - s))


def _softplus(x):
    return jnp.maximum(x, 0.0) + jnp.log(1.0 + jnp.exp(-jnp.abs(x)))


_GK = math.sqrt(2.0 / math.pi)


def _gelu(x):
    return 0.5 * x * (1.0 + jnp.tanh(_GK * (x + 0.044715 * x * x * x)))


def _dgelu(x):
    th = jnp.tanh(_GK * (x + 0.044715 * x * x * x))
    return 0.5 * (1.0 + th) + 0.5 * x * (1.0 - th * th) * _GK * (1.0 + 3.0 * 0.044715 * x * x)


def _colsum(x):
    return jnp.sum(x, axis=0, keepdims=True)


def _rms(x):
    r = lax.rsqrt(jnp.mean(x * x, axis=-1, keepdims=True) + EPS)
    return r, x * r


def _rms_bwd(r, n, dn):
    return r * (dn - n * jnp.mean(dn * n, axis=-1, keepdims=True))


def _roll(x, k):
    n = x.shape[0]
    k = k % n
    return x if k == 0 else pltpu.roll(x, k, axis=0)


def _tblock(t, want=512):
    return min(t, want)


def _peer(mask):
    x, y, c = lax.axis_index("x"), lax.axis_index("y"), lax.axis_index("c")
    return (x ^ ((mask >> 2) & 1), y ^ ((mask >> 1) & 1), c ^ (mask & 1))


def _group_index(masks):
    x, y, c = lax.axis_index("x"), lax.axis_index("y"), lax.axis_index("c")
    full = 0
    for m in masks:
        full |= m
    bits = [b for b in (4, 2, 1) if full & b]

    def idx(px, py, pc):
        v = {4: px, 2: py, 1: pc}
        out = 0
        for b in bits:
            out = out * 2 + v[b]
        return out

    return idx(x, y, c), [idx(*_peer(m)) for m in masks]


def _exchange(arrs, masks, scatter, name, stage=False):
    n_arr, n_peer, n_grp = len(arrs), len(masks), len(masks) + 1

    def body(*refs):
        ins, outs = refs[:n_arr], refs[n_arr:2 * n_arr]
        send_sems, recv_sems, local_sems = refs[2 * n_arr:2 * n_arr + 3]
        if stage:
            bufs, load_sems = refs[2 * n_arr + 3:3 * n_arr + 3], refs[3 * n_arr + 3]
            loads = [pltpu.make_async_copy(ins[t], bufs[t], load_sems.at[t]) for t in range(n_arr)]
            for ld in loads:
                ld.start()
            for ld in loads:
                ld.wait()
            ins = bufs
        me, peer_idx = _group_index(masks)
        copies = []
        for t in range(n_arr):
            src_me = ins[t].at[me] if scatter else ins[t]
            loc = pltpu.make_async_copy(src_me, outs[t].at[me], local_sems.at[t])
            loc.start()
            copies.append(loc)
            for j, m in enumerate(masks):
                src = ins[t].at[peer_idx[j]] if scatter else ins[t]
                cp = pltpu.make_async_remote_copy(src_ref=src, dst_ref=outs[t].at[me], send_sem=send_sems.at[t, j],
                                                  recv_sem=recv_sems.at[t, j], device_id=_peer(m), device_id_type=MESH)
                cp.start()
                copies.append(cp)
        for cp in copies:
            cp.wait()

    hbm = pl.BlockSpec(memory_space=pl.ANY)
    out_shape = [jax.ShapeDtypeStruct((n_grp,) + (a.shape[1:] if scatter else a.shape), a.dtype) for a in arrs]
    staging = [pltpu.VMEM(a.shape, a.dtype) for a in arrs] + [pltpu.SemaphoreType.DMA((n_arr,))] if stage else []
    outs = pl.pallas_call(
        body, name=name, in_specs=[hbm] * n_arr, out_specs=[hbm] * n_arr, out_shape=out_shape,
        scratch_shapes=[pltpu.SemaphoreType.DMA((n_arr, n_peer)), pltpu.SemaphoreType.DMA((n_arr, n_peer)),
                        pltpu.SemaphoreType.DMA((n_arr,))] + staging,
        compiler_params=pltpu.CompilerParams(vmem_limit_bytes=48 * 1024 * 1024),
    )(*arrs)
    return list(outs)


def _split_copies(src_refs, land_refs, sems, scatter, per_core):
    me, peer_idx = _group_index(CHIPS)
    n = len(CHIPS) * len(src_refs)
    copies = []
    for t, (src_ref, land_ref) in enumerate(zip(src_refs, land_refs)):
        zone = land_ref.at[lax.axis_index("c")] if per_core else land_ref
        for j, m in enumerate(CHIPS):
            k = len(CHIPS) * t + j
            copies.append(pltpu.make_async_remote_copy(
                src_ref=src_ref.at[peer_idx[j]] if scatter else src_ref, dst_ref=zone.at[me], send_sem=sems[k],
                recv_sem=sems[n + k], device_id=_peer(m), device_id_type=MESH))
    return copies


def _split_start(srcs, after, scatter, name, per_core=False):
    n_arr, n_sem = len(srcs), 2 * len(CHIPS) * len(srcs)

    def body(*refs):
        src_refs, land_refs = refs[:n_arr], refs[n_arr:2 * n_arr]
        outs = refs[2 * n_arr + len(after):]
        for cp in _split_copies(src_refs, land_refs, outs[:n_sem], scatter, per_core):
            cp.start()
        outs[-1][...] = jnp.zeros_like(outs[-1])

    hbm = pl.BlockSpec(memory_space=pltpu.HBM)
    sem = pl.BlockSpec(memory_space=pltpu.SEMAPHORE)
    lands = [lax.empty(((2,) if per_core else ()) + (len(CHIPS) + 1,) + (a.shape[1:] if scatter else a.shape), a.dtype)
             for a in srcs]
    as_hbm = lambda a: pltpu.with_memory_space_constraint(a, pltpu.HBM)
    outs = pl.pallas_call(
        body, name=name,
        out_shape=(pltpu.SemaphoreType.DMA(()),) * n_sem + tuple(pltpu.HBM(a.shape, a.dtype) for a in srcs + lands)
        + (jax.ShapeDtypeStruct((8, 128), F32),),
        in_specs=(hbm,) * (2 * n_arr) + (pl.BlockSpec(memory_space=pl.ANY),) * len(after),
        out_specs=(sem,) * n_sem + (hbm,) * (2 * n_arr) + (pl.BlockSpec(memory_space=pltpu.VMEM),),
        input_output_aliases={t: n_sem + t for t in range(2 * n_arr)},
        compiler_params=pltpu.CompilerParams(has_side_effects=pltpu.SideEffectType.DATAFLOW_SIDE_EFFECTING),
    )(*[as_hbm(a) for a in srcs + lands], *after)
    return outs[:n_sem], list(outs[n_sem:n_sem + n_arr]), list(outs[n_sem + n_arr:n_sem + 2 * n_arr]), outs[-1]


def _split_wait(sems, srcs, lands, after, scatter, name, per_core=False):
    n_arr, n_sem = len(srcs), len(sems)

    def body(*refs):
        src_refs, land_refs = refs[:n_arr], refs[n_arr:2 * n_arr]
        for cp in _split_copies(src_refs, land_refs, refs[2 * n_arr:2 * n_arr + n_sem], scatter, per_core):
            cp.wait_send()
            cp.wait_recv()

    hbm = pl.BlockSpec(memory_space=pltpu.HBM)
    sem = pl.BlockSpec(memory_space=pltpu.SEMAPHORE)
    outs = pl.pallas_call(
        body, name=name, out_shape=tuple(pltpu.HBM(a.shape, a.dtype) for a in srcs + lands),
        in_specs=(hbm,) * (2 * n_arr) + (sem,) * n_sem + (pl.BlockSpec(memory_space=pl.ANY),) * len(after),
        out_specs=(hbm,) * (2 * n_arr), input_output_aliases={t: t for t in range(2 * n_arr)},
        compiler_params=pltpu.CompilerParams(has_side_effects=pltpu.SideEffectType.DATAFLOW_SIDE_EFFECTING),
    )(*srcs, *lands, *sems, *after)
    return list(outs[:n_arr]), list(outs[n_arr:])


CHIPS = (4, 2, 6)
EVERYONE = (1, 2, 3, 4, 5, 6, 7)
SIBLING = (1,)
SWAP_ROWS = 1024
WIN_ROWS = 592


def _pair_swap(arrs, other_layer, name, narrow=False, fill=False):
    assert not (fill and (other_layer or narrow))
    n_arr = len(arrs)
    shapes = [a.shape[-2:] for a in arrs]
    out_dtypes = [BF16 if narrow else a.dtype for a in arrs]
    chunks = []
    for t, (rows, _) in enumerate(shapes):
        assert rows % 16 == 0
        for j, r0 in enumerate(range(0, rows, SWAP_ROWS)):
            chunks.append((t, r0, min(SWAP_ROWS, rows - r0), j % 2))

    def body(*refs):
        ins, outs = refs[:n_arr], refs[n_arr:2 * n_arr]
        bufs = refs[2 * n_arr:3 * n_arr]
        out_bufs = refs[3 * n_arr:4 * n_arr] if narrow else bufs
        load_sems, send_sems, recv_sems = refs[-3:]
        sibling = _peer(1)
        c = lax.axis_index("c")

        def load(k):
            t, r0, n, slot = chunks[k]
            src = ins[t].at[1 - c] if other_layer else ins[t].at[c] if fill else ins[t]
            return pltpu.make_async_copy(src.at[pl.ds(r0, n)], bufs[t].at[slot, pl.ds(0, n)], load_sems.at[t, slot])

        def send(k):
            t, r0, n, slot = chunks[k]
            dst = outs[t].at[c] if fill else outs[t]
            return pltpu.make_async_remote_copy(src_ref=out_bufs[t].at[slot, pl.ds(0, n)], dst_ref=dst.at[pl.ds(r0, n)],
                                                send_sem=send_sems.at[t, slot], recv_sem=recv_sems.at[t],
                                                device_id=sibling, device_id_type=MESH)

        in_flight = {}

        def drain(k):
            key = (chunks[k][0], chunks[k][3])
            if key in in_flight:
                send(in_flight.pop(key)).wait_send()

        def start_load(k):
            if not narrow:
                drain(k)
            load(k).start()

        start_load(0)
        for k in range(len(chunks)):
            t, _, n, slot = chunks[k]
            load(k).wait()
            if k + 1 < len(chunks):
                start_load(k + 1)
            if narrow:
                drain(k)
                out_bufs[t][slot, pl.ds(0, n), :] = bufs[t][slot, pl.ds(0, n), :].astype(BF16)
            send(k).start()
            in_flight[(t, slot)] = k
        for k in in_flight.values():
            send(k).wait_send()
        for t in range(n_arr):
            landed = outs[t].at[1 - c] if fill else outs[t]
            pltpu.make_async_remote_copy(src_ref=landed, dst_ref=landed, send_sem=send_sems.at[t, 0],
                                         recv_sem=recv_sems.at[t], device_id=sibling, device_id_type=MESH).wait_recv()

    hbm = pl.BlockSpec(memory_space=pl.ANY)
    outs = pl.pallas_call(
        body, name=name, in_specs=[hbm] * n_arr, out_specs=[hbm] * n_arr,
        out_shape=[jax.ShapeDtypeStruct(a.shape if fill else s, dt) for a, s, dt in zip(arrs, shapes, out_dtypes)],
        input_output_aliases={t: t for t in range(n_arr)} if fill else {},
        scratch_shapes=[pltpu.VMEM((2, min(SWAP_ROWS, s[0]), s[1]), a.dtype) for s, a in zip(shapes, arrs)]
        + ([pltpu.VMEM((2, min(SWAP_ROWS, s[0]), s[1]), BF16) for s in shapes] if narrow else [])
        + [pltpu.SemaphoreType.DMA((n_arr, 2)), pltpu.SemaphoreType.DMA((n_arr, 2)), pltpu.SemaphoreType.DMA((n_arr,))],
        compiler_params=pltpu.CompilerParams(vmem_limit_bytes=48 * 1024 * 1024),
    )(*arrs)
    return list(outs)


def _sum_lead(a, name, out_dtype):
    n = a.shape[0]
    shape = a.shape[1:]

    def body(a_ref, o_ref):
        acc = a_ref[0].astype(F32)
        for k in range(1, n):
            acc = acc + a_ref[k].astype(F32)
        o_ref[...] = acc.astype(out_dtype)

    if len(shape) == 3:
        blk = (1,) + shape[1:]
        return pl.pallas_call(
            body, name=name, grid=(shape[0],), in_specs=[pl.BlockSpec((n,) + blk, lambda i: (0, i, 0, 0))],
            out_specs=pl.BlockSpec(blk, lambda i: (i, 0, 0)), out_shape=jax.ShapeDtypeStruct(shape, out_dtype),
            compiler_params=_cparams(1),
        )(a)
    rows, cols = shape
    rb = rows
    for cand in (512, 256, 128):
        if rows % cand == 0 and rows > cand:
            rb = cand
            break
    return pl.pallas_call(
        body, name=name, grid=(rows // rb,), in_specs=[pl.BlockSpec((n, rb, cols), lambda i: (0, i, 0))],
        out_specs=pl.BlockSpec((rb, cols), lambda i: (i, 0)), out_shape=jax.ShapeDtypeStruct((rows, cols), out_dtype),
        compiler_params=_cparams(1),
    )(a)


def _pair_sum(g, recv, layer, name, out_dtype):
    _, n, r, c = g.shape

    def body(l_ref, g_ref, r_ref, o_ref):
        o_ref[...] = (g_ref[0].astype(F32) + r_ref[...].astype(F32)).astype(out_dtype)

    return pl.pallas_call(
        body, name=name,
        grid_spec=pltpu.PrefetchScalarGridSpec(
            num_scalar_prefetch=1, grid=(n,),
            in_specs=[pl.BlockSpec((1, 1, r, c), lambda i, l: (l[0], i, 0, 0)), pl.BlockSpec((1, r, c), lambda i, l: (i, 0, 0))],
            out_specs=pl.BlockSpec((1, r, c), lambda i, l: (i, 0, 0))),
        out_shape=jax.ShapeDtypeStruct((n, r, c), out_dtype), compiler_params=_cparams(1),
    )(layer, g, recv)


def _tn_matmul(a, b, name, col_major=False, into=None, layer=0):
    t, k = a.shape
    n = b.shape[1]
    tb = _tblock(t, 1024)
    kb = min(k, 1024)
    nb = min(n, 1024)
    grid = (k // kb, n // nb, t // tb)
    lead = (into is not None) + col_major

    def body(a_ref, b_ref, *rest):
        o_ref = rest[-1]
        for _ in range(lead):
            o_ref = o_ref.at[0]

        @pl.when(pl.program_id(2) == 0)
        def _():
            o_ref[...] = jnp.zeros_like(o_ref)

        o_ref[...] += _bdot(a_ref[...], b_ref[...], TN)

    if col_major:
        block, index, shape = (1, kb, nb), (lambda ki, ni: (ni, ki, 0)), (n // nb, k, nb)
    else:
        block, index, shape = (kb, nb), (lambda ki, ni: (ki, ni)), (k, n)
    in_specs = [pl.BlockSpec((tb, kb), lambda ki, ni, ti: (ti, ki)), pl.BlockSpec((tb, nb), lambda ki, ni, ti: (ti, ni))]
    if into is None:
        return pl.pallas_call(
            body, name=name, grid=grid, in_specs=in_specs, out_specs=pl.BlockSpec(block, lambda ki, ni, ti: index(ki, ni)),
            out_shape=jax.ShapeDtypeStruct(shape, F32), compiler_params=_cparams(3),
        )(a, b)
    assert into.shape == (2,) + shape
    return pl.pallas_call(
        body, name=name, grid=grid, in_specs=in_specs + [pl.BlockSpec(memory_space=pl.ANY)],
        out_specs=pl.BlockSpec((1,) + block, lambda ki, ni, ti: (layer,) + index(ki, ni)),
        out_shape=jax.ShapeDtypeStruct(into.shape, F32), input_output_aliases={2: 0}, compiler_params=_cparams(3),
    )(a, b, into)


def _sum_many(arrs, name):
    k = len(arrs)

    def body(*refs):
        for a_ref, o_ref in zip(refs[:k], refs[k:]):
            acc = a_ref[0]
            for j in range(1, a_ref.shape[0]):
                acc = acc + a_ref[j]
            o_ref[...] = acc

    return pl.pallas_call(body, name=name, grid=(1,), in_specs=[_full(a.shape) for a in arrs],
                          out_specs=[_full(a.shape[1:]) for a in arrs],
                          out_shape=[jax.ShapeDtypeStruct(a.shape[1:], F32) for a in arrs], compiler_params=_cparams(1))(*arrs)


def _adamw_math(w, g, m, v):
    m2 = ADAM_B1 * m + (1.0 - ADAM_B1) * g
    v2 = ADAM_B2 * v + (1.0 - ADAM_B2) * (g * g)
    m_hat = m2 / (1.0 - ADAM_B1 ** ADAM_STEP)
    v_hat = v2 / (1.0 - ADAM_B2 ** ADAM_STEP)
    return -ADAM_LR * (m_hat / (jnp.sqrt(v_hat) + ADAM_EPS) + ADAM_WD * w), m2, v2


def _adamw_many(ws, gs, ms, vs, name):
    n = len(ws)

    def body(*refs):
        ins, outs = refs[:4 * n], refs[4 * n:]
        for k in range(n):
            res = _adamw_math(ins[k][...], ins[n + k][...], ins[2 * n + k][...], ins[3 * n + k][...])
            for j in range(3):
                outs[3 * k + j][...] = res[j]

    out_shape = []
    for a in ws:
        out_shape += [jax.ShapeDtypeStruct(a.shape, F32)] * 3
    return pl.pallas_call(body, name=name, grid=(1,), in_specs=[_full(a.shape) for a in ws] * 4,
                          out_specs=[_full(s.shape) for s in out_shape], out_shape=out_shape,
                          compiler_params=_cparams(1))(*ws, *gs, *ms, *vs)


def _adamw(w, g, m, v, name):
    shape = w.shape
    cols = shape[-1]
    rows = int(math.prod(shape[:-1]))
    rb = rows
    for cand in (256, 128, 64, 32, 16, 8):
        if rows % cand == 0 and rows > cand:
            rb = cand
            break
    bc1 = 1.0 - ADAM_B1 ** ADAM_STEP
    bc2 = 1.0 - ADAM_B2 ** ADAM_STEP

    def body(w_ref, g_ref, m_ref, v_ref, d_ref, nm_ref, nv_ref):
        gg = g_ref[...]
        m2 = ADAM_B1 * m_ref[...] + (1.0 - ADAM_B1) * gg
        v2 = ADAM_B2 * v_ref[...] + (1.0 - ADAM_B2) * (gg * gg)
        m_hat = m2 / bc1
        v_hat = v2 / bc2
        d_ref[...] = -ADAM_LR * (m_hat / (jnp.sqrt(v_hat) + ADAM_EPS) + ADAM_WD * w_ref[...])
        nm_ref[...] = m2
        nv_ref[...] = v2

    spec = pl.BlockSpec((rb, cols), lambda i: (i, 0))
    sds = jax.ShapeDtypeStruct((rows, cols), F32)
    outs = pl.pallas_call(
        body, name=name, grid=(rows // rb,), in_specs=[spec] * 4, out_specs=[spec] * 3, out_shape=[sds] * 3,
        compiler_params=_cparams(1),
    )(*(z.reshape(rows, cols) for z in (w, g, m, v)))
    return tuple(o.reshape(shape) for o in outs)


def _ada_fwd(c_all, ada_w_sh, ada_b_sh):
    s = ada_w_sh.shape[2]
    sb = 512

    def body(c_ref, w_ref, b_ref, o_ref):
        cond = _silu(c_ref[...])
        o_ref[0] = _bdot(cond, w_ref[0]) + b_ref[0]

    return pl.pallas_call(
        body, name="ada_fwd", grid=(2, s // sb),
        in_specs=[_full((8, D)), pl.BlockSpec((1, D, sb), lambda l, j: (l, 0, j)), pl.BlockSpec((1, 1, sb), lambda l, j: (l, 0, j))],
        out_specs=pl.BlockSpec((1, 8, sb), lambda l, j: (l, 0, j)), out_shape=jax.ShapeDtypeStruct((2, 8, s), F32),
        compiler_params=_cparams(2),
    )(c_all, ada_w_sh, ada_b_sh)


def _ada_bwd(c_all, dmod_sh, dmod_all):
    s = dmod_sh.shape[2]
    sb = 512

    def body(c_ref, d_ref, o_ref):
        cond = _silu(c_ref[...])
        o_ref[0] = _bdot(cond, d_ref[0], TN)

    gw = pl.pallas_call(
        body, name="ada_bwd_w", grid=(2, s // sb),
        in_specs=[_full((8, D)), pl.BlockSpec((1, 8, sb), lambda l, j: (l, 0, j))],
        out_specs=pl.BlockSpec((1, D, sb), lambda l, j: (l, 0, j)), out_shape=jax.ShapeDtypeStruct((2, D, s), F32),
        compiler_params=_cparams(2),
    )(c_all, dmod_sh)

    def body_b(d_ref, o_ref):
        acc = d_ref[0, 0:1, :]
        for k in range(1, 8):
            acc = acc + d_ref[0, k:k + 1, :]
        o_ref[0] = acc

    gb = pl.pallas_call(
        body_b, name="ada_bwd_b", grid=(2,), in_specs=[pl.BlockSpec((1, 8, 6 * D), lambda l: (l, 0, 0))],
        out_specs=pl.BlockSpec((1, 1, 6 * D), lambda l: (l, 0, 0)), out_shape=jax.ShapeDtypeStruct((2, 1, 6 * D), F32),
        compiler_params=_cparams(1),
    )(dmod_all)
    return gw, gb.reshape(2, 6 * D)


def _f_in(h, nw, sc, sh, w_main, w_dt):
    t = h.shape[0]
    tb = _tblock(t)

    def body(h_ref, nw_ref, sc_ref, sh_ref, w_ref, wd_ref, p_ref, dt_ref, u_ref):
        _, n = _rms(h_ref[...])
        u = ((n * nw_ref[...]) * (1.0 + sc_ref[...]) + sh_ref[...]).astype(BF16)
        u_ref[...] = u
        p_ref[...] = _dot(u, w_ref[...], NT)
        dt_ref[...] = _dot(u, wd_ref[...], NT)

    return pl.pallas_call(
        body, name="f_in", grid=(t // tb,),
        in_specs=[pl.BlockSpec((tb, D), lambda i: (i, 0)), _row(D), _row(D), _row(D), _full((PW, D)), _full((DTW, D))],
        out_specs=[pl.BlockSpec((tb, PW), lambda i: (i, 0)), pl.BlockSpec((tb, DTW), lambda i: (i, 0)),
                   pl.BlockSpec((tb, D), lambda i: (i, 0))],
        out_shape=[jax.ShapeDtypeStruct((t, PW), F32), jax.ShapeDtypeStruct((t, DTW), F32), jax.ShapeDtypeStruct((t, D), BF16)],
        compiler_params=_cparams(1),
    )(h, nw, sc, sh, w_main, w_dt)


def _norm_bwd_step(du_v, x, dres_v, gated, nwv, scv, dx_ref, dsc_ref, dsh_ref, dnw_ref, dg_ref):
    r, n = _rms(x)
    scale = 1.0 + scv
    dsc_ref[...] += _colsum(du_v * (n * nwv))
    dsh_ref[...] += _colsum(du_v)
    dnw_ref[...] += _colsum(du_v * scale * n)
    dg_ref[...] += _colsum(dres_v * gated)
    dx_ref[...] = dres_v + _rms_bwd(r, n, du_v * scale * nwv)


def _b_in(dab, dz, dxbc, ds5, ddt, w_main, w_dt, x, dres, gated, nw, sc):
    t = dab.shape[0]
    tb = _tblock(t)

    def body(a_ref, z_ref, x_ref, s_ref, d_ref, w_ref, wd_ref, h_ref, dr_ref, g_ref, nw_ref, sc_ref,
             dx_ref, dsc_ref, dsh_ref, dnw_ref, dg_ref):
        @pl.when(pl.program_id(0) == 0)
        def _():
            for r in (dsc_ref, dsh_ref, dnw_ref, dg_ref):
                r[...] = jnp.zeros_like(r)

        du = _bdot(a_ref[...], w_ref[0:1024, :])
        du += _bdot(z_ref[...], w_ref[1024:1280, :])
        du += _bdot(s_ref[...], w_ref[1280:1536, :])
        du += _bdot(x_ref[...], w_ref[1536:2304, :])
        du += _bdot(d_ref[...], wd_ref[...])
        _norm_bwd_step(du, h_ref[...], dr_ref[...], g_ref[...], nw_ref[...], sc_ref[...], dx_ref, dsc_ref, dsh_ref, dnw_ref, dg_ref)

    blk = lambda n: pl.BlockSpec((tb, n), lambda i: (i, 0))
    row = jax.ShapeDtypeStruct((1, D), F32)
    return pl.pallas_call(
        body, name="b_in", grid=(t // tb,),
        in_specs=[blk(1024), blk(256), blk(768), blk(256), blk(DTW), _full((PW, D)), _full((DTW, D)),
                  blk(D), blk(D), blk(D), _row(D), _row(D)],
        out_specs=[blk(D), _row(D), _row(D), _row(D), _row(D)],
        out_shape=[jax.ShapeDtypeStruct((t, D), F32), row, row, row, row], compiler_params=_cparams(1),
    )(dab, dz, dxbc, ds5, ddt, w_main, w_dt, x, dres, gated, nw, sc)


HALO = 16


def _lane_group(shape):
    return lax.broadcasted_iota(jnp.int32, shape, 1) // 64


def _window_select(g, s2, s4, s8, s16):
    return jnp.where(g == 0, s2, jnp.where(g == 1, s4, jnp.where(g == 2, s8, s16)))


def _pool_count(t0, rows):
    g = _lane_group((rows, GW))
    win = _window_select(g, 2, 4, 8, 16)
    tt = t0 + lax.broadcasted_iota(jnp.int32, (rows, GW), 0)
    return jnp.minimum(tt + 1, win).astype(F32)


def _pool_p(v_ext, t0, tb):
    s2 = v_ext + _roll(v_ext, 1)
    s4 = s2 + _roll(s2, 2)
    s8 = s4 + _roll(s4, 4)
    s16 = s8 + _roll(s8, 8)
    ws = _window_select(_lane_group(v_ext.shape), s2, s4, s8, s16)[HALO:]
    return ws / _pool_count(t0, tb) - v_ext[HALO:]


def _sconv(q_ext, w):
    return (_roll(q_ext, 2) * w[0:1] + _roll(q_ext, 1) * w[1:2] + q_ext * w[2:3])[HALO:]


def _halo_specs(t, tb, cols, col_block):
    per = tb // HALO
    last = t // HALO - 1
    prev = pl.BlockSpec((HALO, cols), lambda i: (jnp.maximum(i * per - 1, 0), col_block))
    nxt = pl.BlockSpec((HALO, cols), lambda i: (jnp.minimum((i + 1) * per, last), col_block))
    return prev, nxt


def _f_ab(proj, pool_mat, pool_scale, sconv_w):
    t = proj.shape[0]
    tb = _tblock(t)
    prev, _ = _halo_specs(t, tb, 1024, 0)

    def body(p_ref, h_ref, pm_ref, ps_ref, sw_ref, ya_ref, yb_ref):
        i = pl.program_id(0)
        halo = jnp.where(i > 0, h_ref[...], 0.0)
        ext = jnp.concatenate([halo, p_ref[...]], axis=0)
        p = _pool_p(ext[:, 0:256], i * tb, tb)
        ya_ref[...] = _bdot(p, pm_ref[...]) * ps_ref[...]
        q_ext = ext[:, 512:768] * ext[:, 768:1024]
        yb_ref[...] = p_ref[:, 256:512] * _sconv(q_ext, sw_ref[...])

    blk = pl.BlockSpec((tb, GW), lambda i: (i, 0))
    sds = jax.ShapeDtypeStruct((t, GW), F32)
    return pl.pallas_call(
        body, name="f_ab", grid=(t // tb,),
        in_specs=[pl.BlockSpec((tb, 1024), lambda i: (i, 0)), prev, _full((GW, GW)), _row(GW), _full((3, GW))],
        out_specs=[blk, blk], out_shape=[sds, sds], compiler_params=_cparams(1),
    )(proj, proj, pool_mat, pool_scale, sconv_w)


def _b_ab(proj, dya, dyb, pool_mat, pool_scale, sconv_w):
    t = proj.shape[0]
    tb = _tblock(t)
    nb = t // tb
    prev, nxt = _halo_specs(t, tb, 1024, 0)
    _, nxt_g = _halo_specs(t, tb, GW, 0)
    n_ext = tb + HALO

    def body(p_ref, hp_ref, hn_ref, da_ref, dan_ref, db_ref, dbn_ref, pm_ref, ps_ref, sw_ref,
             o_ref, dpm_ref, dps_ref, dsw_ref):
        i = pl.program_id(0)

        @pl.when(i == 0)
        def _():
            for r in (dpm_ref, dps_ref, dsw_ref):
                r[...] = jnp.zeros_like(r)

        last = i == nb - 1
        halo = jnp.where(i > 0, hp_ref[...], 0.0)
        main = p_ref[...]
        ext = jnp.concatenate([halo, main], axis=0)
        scale = ps_ref[...]
        pm = pm_ref[...]
        p = _pool_p(ext[:, 0:256], i * tb, tb)
        da = da_ref[...]
        dps_ref[...] += _colsum(da * _bdot(p, pm))
        da_ext = jnp.concatenate([da, jnp.where(last, 0.0, dan_ref[...])], axis=0)
        dys = da_ext * scale
        dpm_ref[...] += _bdot(p, dys[:tb], TN)
        dp = _bdot(dys, pm, NT)
        dpc = dp / _pool_count(i * tb, n_ext)
        a2 = dpc + _roll(dpc, n_ext - 1)
        a4 = a2 + _roll(a2, n_ext - 2)
        a8 = a4 + _roll(a4, n_ext - 4)
        a16 = a8 + _roll(a8, n_ext - 8)
        o_ref[:, 0:256] = (_window_select(_lane_group(dpc.shape), a2, a4, a8, a16) - dp)[:tb]
        w = sw_ref[...]
        gb, gc, hh = main[:, 256:512], main[:, 512:768], main[:, 768:1024]
        q_ext = ext[:, 512:768] * ext[:, 768:1024]
        db = db_ref[...]
        o_ref[:, 256:512] = db * _sconv(q_ext, w)
        gb_next = hn_ref[:, 256:512]
        dconv = jnp.concatenate([db * gb, jnp.where(last, 0.0, dbn_ref[...] * gb_next)], axis=0)
        dq = (dconv * w[2:3] + _roll(dconv, n_ext - 1) * w[1:2] + _roll(dconv, n_ext - 2) * w[0:1])[:tb]
        o_ref[:, 512:768] = dq * hh
        o_ref[:, 768:1024] = dq * gc
        dc = dconv[:tb]
        dsw_ref[0:1, :] += _colsum(dc * _roll(q_ext, 2)[HALO:])
        dsw_ref[1:2, :] += _colsum(dc * _roll(q_ext, 1)[HALO:])
        dsw_ref[2:3, :] += _colsum(dc * q_ext[HALO:])

    blk = pl.BlockSpec((tb, GW), lambda i: (i, 0))
    return pl.pallas_call(
        body, name="b_ab", grid=(nb,),
        in_specs=[pl.BlockSpec((tb, 1024), lambda i: (i, 0)), prev, nxt, blk, nxt_g, blk, nxt_g,
                  _full((GW, GW)), _row(GW), _full((3, GW))],
        out_specs=[pl.BlockSpec((tb, 1024), lambda i: (i, 0)), _full((GW, GW)), _row(GW), _full((3, GW))],
        out_shape=[jax.ShapeDtypeStruct((t, 1024), F32), jax.ShapeDtypeStruct((GW, GW), F32),
                   jax.ShapeDtypeStruct((1, GW), F32), jax.ShapeDtypeStruct((3, GW), F32)],
        compiler_params=_cparams(1),
    )(proj, proj, proj, dya, dya, dyb, dyb, pool_mat, pool_scale, sconv_w)


CH = 8


def _ssd_conv(x, halo, w, b):
    ext = jnp.concatenate([halo, x], axis=0)
    pre = ext * w[3:4] + _roll(ext, 1) * w[2:3] + _roll(ext, 2) * w[1:2] + _roll(ext, 3) * w[0:1] + b
    return pre[CH:], ext


def _ssd_common(dt_raw, dtb, alog):
    ll = dt_raw.shape[0]
    dtv = _softplus(dt_raw + dtb)
    a_row = -jnp.exp(alog)
    r = lax.broadcasted_iota(jnp.int32, (ll, ll), 0)
    c = lax.broadcasted_iota(jnp.int32, (ll, ll), 1)
    tril = (r >= c).astype(F32)
    cs = _dot(tril, dtv * a_row, prec=HI)
    return dtv, a_row, cs, cs.T, r >= c


def _bd(a, b, ca, cb):
    return lax.dot_general(a, b, (((ca,), (cb,)), ((0,), (0,))), preferred_element_type=F32)


def _head_cols(m):
    return jnp.stack([m[:, h:h + 1] for h in range(NH)])


def _ssd_heads(act, dtv, cs, cs_t, causal):
    xs = jnp.stack([act[:, HP * h:HP * (h + 1)] for h in range(NH)])
    bm = jnp.stack([act[:, 256 + NS * (h // 2):256 + NS * (h // 2 + 1)] for h in range(NH)])
    cm = jnp.stack([act[:, 512 + NS * (h // 2):512 + NS * (h // 2 + 1)] for h in range(NH)])
    cs_c = _head_cols(cs)
    cs_r = jnp.stack([cs_t[h:h + 1, :] for h in range(NH)])
    mdec = jnp.where(causal[None], jnp.exp(jnp.minimum(cs_c - cs_r, 0.0)), 0.0)
    g2 = _bd(jnp.stack([cm[0], cm[2]]), jnp.stack([bm[0], bm[2]]), 2, 2)
    sc = jnp.stack([g2[h // 2] for h in range(NH)]) * mdec
    dt_c = _head_cols(dtv)
    xdt = xs * dt_c
    e = jnp.exp(cs_c)
    cs_last = cs_c[:, SSD_L - 1:SSD_L, :]
    wdec = jnp.exp(cs_last - cs_c)
    return xs, bm, cm, mdec, sc, dt_c, xdt, e, cs_last, wdec


def _head_scalars(row_ref):
    return jnp.stack([row_ref[0:1, h:h + 1] for h in range(NH)])


def _f_ssd(proj, dtp, conv_w, conv_b, dt_bias, a_log, d_skip):
    t = proj.shape[0]
    nc = t // SSD_L
    rows = SSD_SUB * SSD_L
    per = rows // CH

    def body(x_ref, hx_ref, dt_ref, z_ref, cw_ref, cb_ref, dtb_ref, al_ref, dk_ref, y_ref, yp_ref, sp_ref, s_ref):
        i = pl.program_id(0)

        @pl.when(i == 0)
        def _():
            s_ref[...] = jnp.zeros_like(s_ref)

        state = s_ref[...]
        dk = _head_scalars(dk_ref)
        for sub in range(SSD_SUB):
            r0 = sub * SSD_L
            rs = slice(r0, r0 + SSD_L)
            halo = jnp.where(i > 0, hx_ref[...], 0.0) if sub == 0 else x_ref[r0 - CH:r0, :]
            pre, _ = _ssd_conv(x_ref[rs, :], halo, cw_ref[...], cb_ref[...])
            act = _silu(pre)
            dtv, _, cs, cs_t, causal = _ssd_common(dt_ref[rs, :], dtb_ref[...], al_ref[...])
            xs, bm, cm, _, sc, _, xdt, e, cs_last, wdec = _ssd_heads(act, dtv, cs, cs_t, causal)
            sp_ref[sub] = state
            y = _bd(sc, xdt, 2, 1) + e * _bd(cm, state, 2, 2) + xs * dk
            for h in range(NH):
                yp_ref[rs, HP * h:HP * (h + 1)] = y[h]
            state = state * jnp.exp(cs_last) + _bd(xdt * wdec, bm, 1, 1)
            y_ref[rs, :] = yp_ref[rs, :] * _silu(z_ref[rs, :])
        s_ref[...] = state

    blk = pl.BlockSpec((rows, GW), lambda i: (i, 0))
    sds = jax.ShapeDtypeStruct((t, GW), F32)
    return pl.pallas_call(
        body, name="f_ssd", grid=(nc // SSD_SUB,),
        in_specs=[pl.BlockSpec((rows, 768), lambda i: (i, 2)),
                  pl.BlockSpec((CH, 768), lambda i: (jnp.maximum(i * per - 1, 0), 2)),
                  pl.BlockSpec((rows, DTW), lambda i: (i, 0)),
                  pl.BlockSpec((rows, GW), lambda i: (i, 4)),
                  _full((4, 768)), _row(768), _row(DTW), _row(DTW), _row(DTW)],
        out_specs=[blk, blk, pl.BlockSpec((SSD_SUB, NH, HP, NS), lambda i: (i, 0, 0, 0))],
        out_shape=[sds, sds, jax.ShapeDtypeStruct((nc, NH, HP, NS), F32)],
        scratch_shapes=[pltpu.VMEM((NH, HP, NS), F32)], compiler_params=_cparams(1),
    )(proj, proj, dtp, proj, conv_w, conv_b, dt_bias, a_log, d_skip)


def _b_ssd(proj, dtp, ypre, dyc, sprev, conv_w, conv_b, dt_bias, a_log, d_skip):
    t = proj.shape[0]
    nc = t // SSD_L
    steps = nc // SSD_SUB_BWD
    rows = SSD_SUB_BWD * SSD_L
    per = rows // CH
    n_ext = SSD_L + CH

    def chunk(sub, halo, dnext, ds_in, refs):
        (x_ref, dt_ref, z_ref, yp_ref, dy_ref, sp_ref, cw_ref, cb_ref, dtb_ref, al_ref, dk_ref,
         dz_ref, dx_ref, ddt_ref, dact_ref) = refs
        rs = slice(sub * SSD_L, (sub + 1) * SSD_L)
        dact = dact_ref.at[sub]
        w = cw_ref[...]
        pre, ext = _ssd_conv(x_ref[rs, :], halo, w, cb_ref[...])
        act = _silu(pre)
        dt_raw = dt_ref[rs, :]
        dtv, a_row, cs, cs_t, causal = _ssd_common(dt_raw, dtb_ref[...], al_ref[...])
        z = z_ref[rs, :]
        dyc_v = dy_ref[rs, :]
        dz_ref[rs, :] = dyc_v * yp_ref[rs, :] * _dsilu(z)
        dy_all = dyc_v * _silu(z)
        lane = lax.broadcasted_iota(jnp.int32, (SSD_L, DTW), 1)
        rowi = lax.broadcasted_iota(jnp.int32, (1, SSD_L, 1), 1)
        lane1 = lax.broadcasted_iota(jnp.int32, (1, DTW), 1)
        xs, bm, cm, mdec, sc, dt_c, xdt, e, cs_last, wdec = _ssd_heads(act, dtv, cs, cs_t, causal)
        dy = jnp.stack([dy_all[:, HP * h:HP * (h + 1)] for h in range(NH)])
        prev = sp_ref[sub]
        ds = ds_in
        lsum = lambda v: jnp.sum(v, axis=2, keepdims=True)
        dsc = _bd(dy, xdt, 2, 2)
        q = dsc * sc
        dg = dsc * mdec
        dxdt = _bd(sc, dy, 1, 1)
        dcs = lsum(q) - lsum(jnp.swapaxes(q, 1, 2))
        dc = _bd(dg, bm, 2, 1)
        db = _bd(dg, cm, 1, 1)
        cp = _bd(cm, prev, 2, 2)
        dcs += lsum(dy * cp) * e
        ey = e * dy
        dc += _bd(ey, prev, 2, 1)
        dprev = _bd(ey, cm, 1, 1)
        elast = jnp.exp(cs_last)
        dprev += ds * elast
        dcs_last = jnp.sum(lsum(ds * prev), axis=1, keepdims=True) * elast
        bds = _bd(bm, ds, 2, 2)
        dxdt += wdec * bds
        db += wdec * _bd(xdt, ds, 2, 1)
        dw = lsum(xdt * bds) * wdec
        dcs -= dw
        dcs_last += jnp.sum(dw, axis=1, keepdims=True)
        dcs += jnp.where(rowi == SSD_L - 1, dcs_last, 0.0)
        dxs = dxdt * dt_c + dy * _head_scalars(dk_ref)
        ddtx = lsum(dxdt * xs)
        ddk = jnp.sum(lsum(dy * xs), axis=1, keepdims=True)
        dcs_mat = jnp.zeros((SSD_L, DTW), F32)
        ddtx_mat = jnp.zeros((SSD_L, DTW), F32)
        ddk_row = jnp.zeros((1, DTW), F32)
        for h in range(NH):
            dact[:, HP * h:HP * (h + 1)] = dxs[h]
            dcs_mat = jnp.where(lane == h, dcs[h], dcs_mat)
            ddtx_mat = jnp.where(lane == h, ddtx[h], ddtx_mat)
            ddk_row = jnp.where(lane1 == h, ddk[h], ddk_row)
        for g in range(2):
            dact[:, 256 + NS * g:256 + NS * (g + 1)] = db[2 * g] + db[2 * g + 1]
            dact[:, 512 + NS * g:512 + NS * (g + 1)] = dc[2 * g] + dc[2 * g + 1]
        ds_out = dprev
        r2 = lax.broadcasted_iota(jnp.int32, (SSD_L, SSD_L), 0)
        c2 = lax.broadcasted_iota(jnp.int32, (SSD_L, SSD_L), 1)
        dadt = _dot((c2 >= r2).astype(F32), dcs_mat, prec=HI)
        ddt = jnp.where(lane < NH, (dadt * a_row + ddtx_mat) * _sig(dt_raw + dtb_ref[...]), 0.0)
        ddt_ref[rs, :] = ddt
        dpre = dact[...] * _dsilu(pre)
        dcw = jnp.concatenate([_colsum(dpre * _roll(ext, 3 - k)[CH:]) for k in range(4)], axis=0)
        dext = jnp.concatenate([dpre, dnext], axis=0)
        dx_ref[rs, :] = (dext * w[3:4] + _roll(dext, n_ext - 1) * w[2:3] + _roll(dext, n_ext - 2) * w[1:2]
                         + _roll(dext, n_ext - 3) * w[0:1])[:SSD_L]
        acc = (dcw, _colsum(dpre), _colsum(ddt), _colsum(dadt * dtv) * a_row, ddk_row)
        return dpre[0:CH], ds_out, acc

    def body(x_ref, hx_ref, dt_ref, z_ref, yp_ref, dy_ref, sp_ref, cw_ref, cb_ref, dtb_ref, al_ref, dk_ref,
             dz_ref, dx_ref, ddt_ref, dcw_ref, dcb_ref, ddtb_ref, dal_ref, ddk_ref, ds_ref, dnext_ref, dact_ref):
        i = pl.program_id(0)
        acc_refs = (dcw_ref, dcb_ref, ddtb_ref, dal_ref, ddk_ref)

        @pl.when(i == 0)
        def _():
            ds_ref[...] = jnp.zeros_like(ds_ref)
            dnext_ref[...] = jnp.zeros_like(dnext_ref)
            for r in acc_refs:
                r[...] = jnp.zeros_like(r)

        refs = (x_ref, dt_ref, z_ref, yp_ref, dy_ref, sp_ref, cw_ref, cb_ref, dtb_ref, al_ref, dk_ref, dz_ref, dx_ref, ddt_ref,
                dact_ref)
        ds = ds_ref[...]
        dnext = dnext_ref[...]
        total = None
        for sub in reversed(range(SSD_SUB_BWD)):
            if sub == 0:
                halo = jnp.where(i == steps - 1, 0.0, hx_ref[...])
            else:
                halo = x_ref[sub * SSD_L - CH:sub * SSD_L, :]
            dnext, ds, acc = chunk(sub, halo, dnext, ds, refs)
            total = acc if total is None else tuple(a + b for a, b in zip(total, acc))
        ds_ref[...] = ds
        dnext_ref[...] = dnext
        for r, v in zip(acc_refs, total):
            r[...] += v

    rev = lambda i: steps - 1 - i
    blk = lambda n, cb=0: pl.BlockSpec((rows, n), lambda i: (rev(i), cb))
    row = lambda n: jax.ShapeDtypeStruct((1, n), F32)
    return pl.pallas_call(
        body, name="b_ssd", grid=(steps,),
        in_specs=[blk(768, 2), pl.BlockSpec((CH, 768), lambda i: (jnp.maximum(rev(i) * per - 1, 0), 2)),
                  blk(DTW), blk(GW, 4), blk(GW), blk(GW), pl.BlockSpec((SSD_SUB_BWD, NH, HP, NS), lambda i: (rev(i), 0, 0, 0)),
                  _full((4, 768)), _row(768), _row(DTW), _row(DTW), _row(DTW)],
        out_specs=[blk(GW), blk(768), blk(DTW), _full((4, 768)), _row(768), _row(DTW), _row(DTW), _row(DTW)],
        out_shape=[jax.ShapeDtypeStruct((t, GW), F32), jax.ShapeDtypeStruct((t, 768), F32), jax.ShapeDtypeStruct((t, DTW), F32),
                   jax.ShapeDtypeStruct((4, 768), F32), row(768), row(DTW), row(DTW), row(DTW)],
        scratch_shapes=[pltpu.VMEM((NH, HP, NS), F32), pltpu.VMEM((CH, 768), F32), pltpu.VMEM((SSD_SUB_BWD, SSD_L, 768), F32)],
        compiler_params=_cparams(1),
    )(proj, proj, dtp, proj, ypre, dyc, sprev, conv_w, conv_b, dt_bias, a_log, d_skip)


def _s5_block(t):
    return min(t, 256)


def _seg_t():
    r = lax.broadcasted_iota(jnp.int32, (64, 1024), 0)
    c = lax.broadcasted_iota(jnp.int32, (64, 1024), 1)
    return (c // 16 == r).astype(F32)


def _s5_prep_math(a_re, a_im, lstep, b_re, b_im):
    step = jnp.exp(lstep)
    ars = a_re * step
    ais = a_im * step
    mag = jnp.exp(ars)
    lr = mag * jnp.cos(ais)
    li = mag * jnp.sin(ais)
    den = a_re * a_re + a_im * a_im
    nr = lr - 1.0
    f_re = (nr * a_re + li * a_im) / den
    f_im = (li * a_re - nr * a_im) / den
    seg = _seg_t()
    fr = _dot(f_re, seg, prec=HI)
    fi = _dot(f_im, seg, prec=HI)
    return lr, li, fr * b_re - fi * b_im, fr * b_im + fi * b_re, ars, ais


def _s5_prep(a_re, a_im, lstep, b_re, b_im):
    def body(ar, ai, ls, br, bi, lr_o, li_o, bbr_o, bbi_o, ars_o, ais_o):
        outs = _s5_prep_math(ar[...], ai[...], ls[...], br[...], bi[...])
        for o, v in zip((lr_o, li_o, bbr_o, bbi_o, ars_o, ais_o), outs):
            o[...] = v

    s64 = jax.ShapeDtypeStruct((16, 64), F32)
    s1k = jax.ShapeDtypeStruct((16, 1024), F32)
    return pl.pallas_call(body, name="s5_prep", out_shape=[s64, s64, s1k, s1k, s64, s64])(a_re, a_im, lstep, b_re, b_im)


def _s5_prep_bwd(a_re, a_im, lstep, b_re, b_im, dlr, dli, dbbr, dbbi):
    def body(ar, ai, ls, br, bi, g0, g1, g2, g3, o0, o1, o2, o3, o4):
        f = lambda *a: _s5_prep_math(*a)[:4]
        _, vjp = jax.vjp(f, ar[...], ai[...], ls[...], br[...], bi[...])
        for o, v in zip((o0, o1, o2, o3, o4), vjp((g0[...], g1[...], g2[...], g3[...]))):
            o[...] = v

    s64 = jax.ShapeDtypeStruct((16, 64), F32)
    s1k = jax.ShapeDtypeStruct((16, 1024), F32)
    return pl.pallas_call(body, name="s5_prep_bwd", out_shape=[s64, s64, jax.ShapeDtypeStruct((16, 1), F32), s1k, s1k])(
        a_re, a_im, lstep, b_re, b_im, dlr, dli, dbbr, dbbi)


SUB = 8


def _s5_tables(ars, ais):
    def body(ar, ai, tr, ti):
        rr = lax.broadcasted_iota(jnp.int32, (8 * SUB, S5_P), 0)
        seg, r = rr // SUB, rr % SUB
        step = jnp.where((seg == 1) | (seg == 4), 1, jnp.where((seg == 2) | (seg == 5), 2, 4))
        n = jnp.where(seg == 0, r + 1, jnp.where(seg == 7, SUB - r, step))
        fwd_gap = jnp.where(seg <= 3, r - step, SUB - step - 1 - r)
        gap = jnp.where((seg == 0) | (seg == 7), 0, fwd_gap)
        nf = n.astype(F32)
        mag = jnp.where(gap >= 0, jnp.exp(nf * ar[...]), 0.0)
        tr[...] = mag * jnp.cos(nf * ai[...])
        ti[...] = mag * jnp.sin(nf * ai[...])

    sds = jax.ShapeDtypeStruct((8 * SUB, S5_P), F32)
    return pl.pallas_call(body, name="s5_tables", out_shape=[sds] * 2)(ars, ais)


def _s5_table(tb_r, tb_i, k):
    return tb_r[SUB * k:SUB * (k + 1), :], tb_i[SUB * k:SUB * (k + 1), :]


def _s5_scan(bu_r, bu_i, tb_r, tb_i, c_r, c_i, lb):
    nt = lb // SUB
    sr, si = bu_r.reshape(nt, SUB, S5_P), bu_i.reshape(nt, SUB, S5_P)
    for j, k in enumerate((1, 2, 4)):
        mr, mi = _s5_table(tb_r, tb_i, 1 + j)
        tr, ti = pltpu.roll(sr, k, axis=1), pltpu.roll(si, k, axis=1)
        sr, si = sr + mr * tr - mi * ti, si + mr * ti + mi * tr
    pr, pi = _s5_table(tb_r, tb_i, 0)
    out_r, out_i = [], []
    for j in range(nt):
        a_r = sr[j] + pr * c_r - pi * c_i
        a_i = si[j] + pr * c_i + pi * c_r
        out_r.append(a_r)
        out_i.append(a_i)
        c_r, c_i = a_r[SUB - 1:SUB], a_i[SUB - 1:SUB]
    return jnp.concatenate(out_r, axis=0), jnp.concatenate(out_i, axis=0)


def _s5_rscan(g_r, g_i, tb_r, tb_i, n_r, n_i, lb):
    nt = lb // SUB
    gr, gi = g_r.reshape(nt, SUB, S5_P), g_i.reshape(nt, SUB, S5_P)
    for j, k in enumerate((1, 2, 4)):
        mr, mi = _s5_table(tb_r, tb_i, 4 + j)
        tr, ti = pltpu.roll(gr, SUB - k, axis=1), pltpu.roll(gi, SUB - k, axis=1)
        gr, gi = gr + mr * tr + mi * ti, gi + mr * ti - mi * tr
    qr, qi = _s5_table(tb_r, tb_i, 7)
    out_r, out_i = [None] * nt, [None] * nt
    for j in reversed(range(nt)):
        a_r = gr[j] + qr * n_r + qi * n_i
        a_i = gi[j] + qr * n_i - qi * n_r
        out_r[j], out_i[j] = a_r, a_i
        n_r, n_i = a_r[0:1], a_i[0:1]
    return jnp.concatenate(out_r, axis=0), jnp.concatenate(out_i, axis=0)


def _s5_y(u, sr, si, cre, cim, dsk):
    return _bdot(sr, cre) + _bdot(si, cim) + dsk * u


def _f_s5(proj, bmat, cre, cim, p_r, p_i, dsk, glu_w, glu_b):
    t = proj.shape[0]
    lb = _s5_block(t)
    nb = t // lb

    def body(u_ref, bm_ref, cr_ref, ci_ref, pr_ref, pi_ref, dk_ref, gw_ref, gb_ref, y_ref, car_ref, s_ref, st_ref):
        @pl.when(pl.program_id(0) == 0)
        def _():
            st_ref[...] = jnp.zeros_like(st_ref)

        u = u_ref[...]
        bu = _bdot(u, bm_ref[...])
        c_r, c_i = st_ref[0:1, 0:S5_P], st_ref[0:1, S5_P:]
        car_ref[0] = st_ref[0:1, :]
        sr, si = _s5_scan(bu[:, :S5_P], bu[:, S5_P:], pr_ref, pi_ref, c_r, c_i, lb)
        st_ref[0:1, 0:S5_P] = sr[lb - 1:lb]
        st_ref[0:1, S5_P:] = si[lb - 1:lb]
        sr_b, si_b = sr.astype(BF16), si.astype(BF16)
        s_ref[:, 0:S5_P] = sr_b
        s_ref[:, S5_P:] = si_b
        gel = _gelu(_s5_y(u, sr_b, si_b, cr_ref[...], ci_ref[...], dk_ref[...]))
        y_ref[...] = gel * _sig(_bdot(gel, gw_ref[...]) + gb_ref[...])

    return pl.pallas_call(
        body, name="f_s5", grid=(nb,),
        in_specs=[pl.BlockSpec((lb, GW), lambda i: (i, 5)),
                  _full((GW, 2 * S5_P)), _full((S5_P, GW)), _full((S5_P, GW)), _full((8 * SUB, S5_P)), _full((8 * SUB, S5_P)),
                  _row(GW), _full((GW, GW)), _row(GW)],
        out_specs=[pl.BlockSpec((lb, GW), lambda i: (i, 0)), pl.BlockSpec((1, 1, 2 * S5_P), lambda i: (i, 0, 0)),
                   pl.BlockSpec((lb, 2 * S5_P), lambda i: (i, 0))],
        out_shape=[jax.ShapeDtypeStruct((t, GW), F32), jax.ShapeDtypeStruct((nb, 1, 2 * S5_P), F32),
                   jax.ShapeDtypeStruct((t, 2 * S5_P), BF16)],
        scratch_shapes=[pltpu.VMEM((8, 2 * S5_P), F32)], compiler_params=_cparams(1),
    )(proj, bmat, cre, cim, p_r, p_i, dsk, glu_w, glu_b)


def _b_s5(proj, dyd, carries, states, bmat, cre, cim, p_r, p_i, dsk, glu_w, glu_b):
    t = proj.shape[0]
    lb = _s5_block(t)
    nb = t // lb

    def body(u_ref, dy_ref, car_ref, s_ref, bm_ref, cr_ref, ci_ref, pr_ref, pi_ref, dk_ref, gw_ref, gb_ref,
             du_ref, dbm_ref, dcr_ref, dci_ref, dlam_ref, ddk_ref, dgw_ref, dgb_ref, gc_ref):
        @pl.when(pl.program_id(0) == 0)
        def _():
            gc_ref[...] = jnp.zeros_like(gc_ref)
            for r in (dbm_ref, dcr_ref, dci_ref, dlam_ref, ddk_ref, dgw_ref, dgb_ref):
                r[...] = jnp.zeros_like(r)

        u = u_ref[...]
        bm = bm_ref[...]
        u_b = u.astype(BF16)
        c_r, c_i = car_ref[0, 0:1, 0:S5_P], car_ref[0, 0:1, S5_P:]
        cre_v, cim_v, dk, gw = cr_ref[...], ci_ref[...], dk_ref[...], gw_ref[...]
        sr_b, si_b = s_ref[:, 0:S5_P], s_ref[:, S5_P:]
        sr, si = sr_b.astype(F32), si_b.astype(F32)
        y = _dot(sr_b, cre_v) + _dot(si_b, cim_v) + dk * u
        gel = _gelu(y)
        gel_b = gel.astype(BF16)
        gate = _sig(_dot(gel_b, gw) + gb_ref[...])
        dout = dy_ref[...]
        t1 = dout * gel * gate * (1.0 - gate)
        t1_b = t1.astype(BF16)
        dgw_ref[...] += _dot(gel_b, t1_b, TN)
        dgb_ref[...] += _colsum(t1)
        dyv = (dout * gate + _dot(t1_b, gw, NT)) * _dgelu(y)
        dyv_b = dyv.astype(BF16)
        ddk_ref[...] += _colsum(dyv * u)
        dcr_ref[...] += _dot(sr_b, dyv_b, TN)
        dci_ref[...] += _dot(si_b, dyv_b, TN)
        gr = _dot(dyv_b, cre_v, NT)
        gi = _dot(dyv_b, cim_v, NT)
        row = lax.broadcasted_iota(jnp.int32, (lb, S5_P), 0)
        n_r, n_i = gc_ref[0:1, 0:S5_P], gc_ref[0:1, S5_P:]
        gr, gi = _s5_rscan(gr, gi, pr_ref, pi_ref, n_r, n_i, lb)
        gc_ref[0:1, 0:S5_P] = gr[0:1]
        gc_ref[0:1, S5_P:] = gi[0:1]
        gcat = jnp.concatenate([gr, gi], axis=1).astype(BF16)
        dbm_ref[...] += _dot(u_b, gcat, TN)
        du_ref[...] = dyv * dk + _dot(gcat, bm, NT)
        spr = jnp.where(row >= 1, _roll(sr, 1), c_r)
        spi = jnp.where(row >= 1, _roll(si, 1), c_i)
        dlam_ref[0:1, :] += _colsum(gr * spr + gi * spi)
        dlam_ref[1:2, :] += _colsum(gi * spr - gr * spi)

    rev = lambda i: nb - 1 - i
    return pl.pallas_call(
        body, name="b_s5", grid=(nb,),
        in_specs=[pl.BlockSpec((lb, GW), lambda i: (rev(i), 5)), pl.BlockSpec((lb, GW), lambda i: (rev(i), 0)),
                  pl.BlockSpec((1, 1, 2 * S5_P), lambda i: (rev(i), 0, 0)), pl.BlockSpec((lb, 2 * S5_P), lambda i: (rev(i), 0)),
                  _full((GW, 2 * S5_P)), _full((S5_P, GW)), _full((S5_P, GW)), _full((8 * SUB, S5_P)), _full((8 * SUB, S5_P)),
                  _row(GW), _full((GW, GW)), _row(GW)],
        out_specs=[pl.BlockSpec((lb, GW), lambda i: (rev(i), 0)), _full((GW, 2 * S5_P)), _full((S5_P, GW)), _full((S5_P, GW)),
                   _full((2, S5_P)), _row(GW), _full((GW, GW)), _row(GW)],
        out_shape=[jax.ShapeDtypeStruct((t, GW), F32), jax.ShapeDtypeStruct((GW, 2 * S5_P), F32),
                   jax.ShapeDtypeStruct((S5_P, GW), F32), jax.ShapeDtypeStruct((S5_P, GW), F32),
                   jax.ShapeDtypeStruct((2, S5_P), F32), jax.ShapeDtypeStruct((1, GW), F32),
                   jax.ShapeDtypeStruct((GW, GW), F32), jax.ShapeDtypeStruct((1, GW), F32)],
        scratch_shapes=[pltpu.VMEM((8, 2 * S5_P), F32)], compiler_params=_cparams(1),
    )(proj, dyd, carries, states, bmat, cre, cim, p_r, p_i, dsk, glu_w, glu_b)


def _group_norm(ys, bw):
    outs, stats = [], []
    for g, y in enumerate(ys):
        r, n = _rms(y)
        stats.append((r, n))
        outs.append(n * bw[:, GW * g:GW * (g + 1)])
    return jnp.concatenate(outs, axis=1), stats


def _f_out(ya, yb, yc, yd, bw, wts, l, h, g1):
    t = h.shape[0]
    tb = _tblock(t)

    def body(a_ref, b_ref, c_ref, d_ref, bw_ref, w_ref, h_ref, g_ref, h2_ref, o_ref, cat_ref):
        cat, _ = _group_norm([a_ref[...], b_ref[...], c_ref[...], d_ref[...]], bw_ref[...])
        catb = cat.astype(BF16)
        cat_ref[...] = catb
        o = _dot(catb, w_ref[0].reshape(D, D))
        o_ref[...] = o.astype(BF16)
        h2_ref[...] = h_ref[...] + g_ref[...] * o

    yblk = pl.BlockSpec((tb, GW), lambda i: (i, 0))
    blk = pl.BlockSpec((tb, D), lambda i: (i, 0))
    return pl.pallas_call(
        body, name="f_out", grid=(t // tb,), in_specs=[yblk] * 4 + [_row(D), _wout_spec(l), blk, _row(D)],
        out_specs=[blk, blk, blk],
        out_shape=[jax.ShapeDtypeStruct((t, D), F32), jax.ShapeDtypeStruct((t, D), BF16), jax.ShapeDtypeStruct((t, D), BF16)],
        compiler_params=_cparams(1),
    )(ya, yb, yc, yd, bw, wts, h, g1)


def _b_out(dv, h2, dh3, m, nw2, sc2, ya, yb, yc, yd, bw, wts, l, g1):
    t = dv.shape[0]
    tb = _tblock(t)

    def body(dv_ref, h2_ref, dh3_ref, m_ref, nw_ref, sc_ref, a_ref, b_ref, c_ref, d_ref, bw_ref, w_ref, g_ref,
             dh_ref, dsc_ref, dsh_ref, dnw_ref, dg_ref, da_ref, db_ref, dc_ref, dd_ref, do_ref, dbw_ref):
        @pl.when(pl.program_id(0) == 0)
        def _():
            for r in (dsc_ref, dsh_ref, dnw_ref, dg_ref, dbw_ref):
                r[...] = jnp.zeros_like(r)

        _norm_bwd_step(dv_ref[...], h2_ref[...], dh3_ref[...], m_ref[...], nw_ref[...], sc_ref[...],
                       dh_ref, dsc_ref, dsh_ref, dnw_ref, dg_ref)
        do = (dh_ref[...] * g_ref[...]).astype(BF16)
        do_ref[...] = do
        dcat = _dot(do, w_ref[0].reshape(D, D), NT)
        bw_v = bw_ref[...]
        for g, (y_ref, dy_ref) in enumerate(((a_ref, da_ref), (b_ref, db_ref), (c_ref, dc_ref), (d_ref, dd_ref))):
            r, n = _rms(y_ref[...])
            dc = dcat[:, GW * g:GW * (g + 1)]
            dbw_ref[:, GW * g:GW * (g + 1)] += _colsum(dc * n)
            dy_ref[...] = _rms_bwd(r, n, dc * bw_v[:, GW * g:GW * (g + 1)])

    yblk = pl.BlockSpec((tb, GW), lambda i: (i, 0))
    blk = pl.BlockSpec((tb, D), lambda i: (i, 0))
    ysd = jax.ShapeDtypeStruct((t, GW), F32)
    row = jax.ShapeDtypeStruct((1, D), F32)
    return pl.pallas_call(
        body, name="b_out", grid=(t // tb,),
        in_specs=[blk] * 4 + [_row(D), _row(D)] + [yblk] * 4 + [_row(D), _wout_spec(l), _row(D)],
        out_specs=[blk, _row(D), _row(D), _row(D), _row(D)] + [yblk] * 4 + [blk, _row(D)],
        out_shape=[jax.ShapeDtypeStruct((t, D), F32), row, row, row, row] + [ysd] * 4 + [jax.ShapeDtypeStruct((t, D), BF16), row],
        compiler_params=_cparams(1),
    )(dv, h2, dh3, m, nw2, sc2, ya, yb, yc, yd, bw, wts, g1)


HB = 512
MLP_ROWS = 1024


ROW_W1, ROW_W2, ROW_WOUT, ROW_WIN = 0, D, D + HID // 4, D + HID // 4 + D // 4
PACK_ROWS = ROW_WIN + WIN_ROWS


def _w1_spec(l):
    per = HID // 4 // HB
    return pl.BlockSpec((1, 1, D, HB), lambda i, k: (l, k // per, ROW_W1 // D, k % per))


def _w2_spec(l):
    per = HID // 4 // HB
    return pl.BlockSpec((1, 1, HB, D), lambda i, k: (l, k // per, ROW_W2 // HB + k % per, 0))


def _wout_spec(l):
    return pl.BlockSpec((1, 4, D // 4, D), lambda i: (l, 0, ROW_WOUT // (D // 4), 0))


def _f_mlp(h2, nw, sc, sh, g2, wts, l):
    t = h2.shape[0]
    tb = _tblock(t, MLP_ROWS)
    nk = HID // HB

    def body(h_ref, nw_ref, sc_ref, sh_ref, g_ref, w1_ref, w2_ref, h3_ref, m_ref, a_ref, v_ref, acc_ref):
        k = pl.program_id(1)

        @pl.when(k == 0)
        def _():
            _, n = _rms(h_ref[...])
            v_ref[...] = ((n * nw_ref[...]) * (1.0 + sc_ref[...]) + sh_ref[...]).astype(BF16)
            acc_ref[...] = jnp.zeros_like(acc_ref)

        a = _dot(v_ref[...], w1_ref[0, 0])
        a_ref[...] = a.astype(BF16)
        ra = jnp.maximum(a, 0.0)
        acc_ref[...] += _dot((ra * ra).astype(BF16), w2_ref[0, 0])

        @pl.when(k == nk - 1)
        def _():
            m = acc_ref[...]
            m_ref[...] = m.astype(BF16)
            h3_ref[...] = h_ref[...] + g_ref[...] * m

    blk = pl.BlockSpec((tb, D), lambda i, k: (i, 0))
    return pl.pallas_call(
        body, name="f_mlp", grid=(t // tb, nk),
        in_specs=[blk, _row(D), _row(D), _row(D), _row(D), _w1_spec(l), _w2_spec(l)],
        out_specs=[blk, blk, pl.BlockSpec((tb, HB), lambda i, k: (i, k)), blk],
        out_shape=[jax.ShapeDtypeStruct((t, D), F32), jax.ShapeDtypeStruct((t, D), BF16), jax.ShapeDtypeStruct((t, HID), BF16),
                   jax.ShapeDtypeStruct((t, D), BF16)],
        scratch_shapes=[pltpu.VMEM((tb, D), F32)], compiler_params=_cparams(2),
    )(h2, nw, sc, sh, g2, wts, wts)


def _b_mlp(dh3, a, g2, wts, l):
    t = dh3.shape[0]
    tb = _tblock(t, MLP_ROWS)
    nk = HID // HB

    def body(dh_ref, a_ref, g_ref, w1_ref, w2_ref, dv_ref, da_ref, act_ref, dm_ref):
        k = pl.program_id(1)
        dm = (dh_ref[...] * g_ref[...]).astype(BF16)

        @pl.when(k == 0)
        def _():
            dm_ref[...] = dm
            dv_ref[...] = jnp.zeros_like(dv_ref)

        ra = jnp.maximum(a_ref[...].astype(F32), 0.0)
        act_ref[...] = (ra * ra).astype(BF16)
        da = (_dot(dm, w2_ref[0, 0], NT) * (2.0 * ra)).astype(BF16)
        da_ref[...] = da
        dv_ref[...] += _dot(da, w1_ref[0, 0], NT)

    blk = pl.BlockSpec((tb, D), lambda i, k: (i, 0))
    hblk = pl.BlockSpec((tb, HB), lambda i, k: (i, k))
    return pl.pallas_call(
        body, name="b_mlp", grid=(t // tb, nk),
        in_specs=[blk, hblk, _row(D), _w1_spec(l), _w2_spec(l)],
        out_specs=[blk, hblk, hblk, blk],
        out_shape=[jax.ShapeDtypeStruct((t, D), F32), jax.ShapeDtypeStruct((t, HID), BF16), jax.ShapeDtypeStruct((t, HID), BF16),
                   jax.ShapeDtypeStruct((t, D), BF16)],
        compiler_params=_cparams(2),
    )(dh3, a, g2, wts, wts)


def _b_final(h, tgt, fw):
    t = h.shape[0]
    tb = _tblock(t)

    def body(h_ref, t_ref, w_ref, dh_ref, loss_ref, dfw_ref):
        @pl.when(pl.program_id(0) == 0)
        def _():
            loss_ref[...] = jnp.zeros_like(loss_ref)
            dfw_ref[...] = jnp.zeros_like(dfw_ref)

        r, n = _rms(h_ref[...])
        wv = w_ref[...]
        err = n * wv - t_ref[...]
        loss_ref[...] += jnp.sum(err * err, keepdims=True) * (0.5 / D)
        dy = err * (1.0 / D)
        dfw_ref[...] += _colsum(dy * n)
        dh_ref[...] = _rms_bwd(r, n, dy * wv)

    blk = pl.BlockSpec((tb, D), lambda i: (i, 0))
    return pl.pallas_call(
        body, name="b_final", grid=(t // tb,), in_specs=[blk, blk, _row(D)], out_specs=[blk, _row(1), _row(D)],
        out_shape=[jax.ShapeDtypeStruct((t, D), F32), jax.ShapeDtypeStruct((1, 1), F32), jax.ShapeDtypeStruct((1, D), F32)],
        compiler_params=_cparams(1),
    )(h, tgt, fw)


def _eye(n):
    return jnp.eye(n, dtype=F32)


def _pool_embed(pool_w):
    return jnp.einsum('gcd,gk->gckd', pool_w, _eye(4)).reshape(GW, GW)


def _pool_extract(m):
    return jnp.einsum('gcgd->gcd', m.reshape(4, 64, 4, 64))


def _bmat_embed(bb):
    return jnp.einsum('gph,gk->ghkp', bb, _eye(16)).reshape(GW, S5_P)


def _bmat_extract(m):
    return jnp.einsum('ghgp->gph', m.reshape(16, 16, 16, 64))


def _cmat_embed(cc):
    return jnp.einsum('ghp,gk->kpgh', cc, _eye(16)).reshape(S5_P, GW)


def _cmat_extract(m):
    return jnp.einsum('gpgh->ghp', m.reshape(16, 64, 16, 16))


def _pad_lanes(v, n=DTW):
    return jnp.pad(v.reshape(1, -1), ((0, 0), (0, n - v.shape[-1])))


def _w_in_layout(w_in_t):
    w_main = jnp.concatenate([w_in_t[:1280], w_in_t[2052:2308], w_in_t[1280:2048]], axis=0)
    return w_main, jnp.pad(w_in_t[2048:2052], ((0, DTW - 4), (0, 0)))


def _layer_params(p, l, mod, w_in, rest):
    q = {'rest': rest, 'l': l}
    q['mod'] = [mod[k:k + 1] for k in range(6)]
    q['nw1'] = p['norm_mix_w'][l:l + 1]
    q['nw2'] = p['norm_mlp_w'][l:l + 1]
    q['w_main'], q['w_dt'] = _w_in_layout(w_in)
    q['pool_mat'] = _pool_embed(p['pool_w'][l]).astype(BF16)
    q['pool_scale'] = p['pool_scale'][l:l + 1]
    q['sconv_w'] = p['sconv_w'][l]
    q['conv_w'] = p['ssd_conv_w'][l]
    q['conv_b'] = p['ssd_conv_b'][l:l + 1]
    q['dt_bias'] = _pad_lanes(p['ssd_dt_bias'][l])
    q['a_log'] = _pad_lanes(p['ssd_a_log'][l])
    q['ssd_d'] = _pad_lanes(p['ssd_d'][l])
    q['s5_raw'] = (p['s5_a_re'][l], p['s5_a_im'][l], p['s5_log_step'][l].reshape(16, 1),
                   p['s5_b_re'][l].reshape(16, 1024), p['s5_b_im'][l].reshape(16, 1024))
    q['cre'] = _cmat_embed(p['s5_c_re'][l]).astype(BF16)
    q['cim'] = (-_cmat_embed(p['s5_c_im'][l])).astype(BF16)
    q['s5_d'] = p['s5_d'][l:l + 1]
    q['glu_w'] = p['s5_glu_w'][l].astype(BF16)
    q['glu_b'] = p['s5_glu_b'][l:l + 1]
    q['bw'] = p['branch_norm_w'][l:l + 1]
    return q


def _layer_fwd(h, q):
    sh1, sc1, g1, sh2, sc2, g2 = q['mod']
    t = h.shape[0]
    s = {'h': h}
    s['proj'], s['dtp'], s['u'] = _f_in(h, q['nw1'], sc1, sh1, q['w_main'], q['w_dt'])
    s['ya'], s['yb'] = _f_ab(s['proj'], q['pool_mat'], q['pool_scale'], q['sconv_w'])
    s['yc'], s['ypre'], s['sprev'] = _f_ssd(s['proj'], s['dtp'], q['conv_w'], q['conv_b'], q['dt_bias'], q['a_log'], q['ssd_d'])
    lr, li, bbr, bbi, ars, ais = _s5_prep(*q['s5_raw'])
    s['bmat'] = jnp.concatenate([_bmat_embed(bbr.reshape(16, 64, 16)), _bmat_embed(bbi.reshape(16, 64, 16))],
                                axis=1).astype(BF16)
    s['tables'] = _s5_tables(ars.reshape(1, S5_P), ais.reshape(1, S5_P))
    s['yd'], s['carries'], s['states'] = _f_s5(s['proj'], s['bmat'], q['cre'], q['cim'], s['tables'][0], s['tables'][1],
                                  q['s5_d'], q['glu_w'], q['glu_b'])
    q['wts'] = q['rest']((s['ya'], s['yc'], s['yd']))
    s['h2'], s['o'], s['cat'] = _f_out(s['ya'], s['yb'], s['yc'], s['yd'], q['bw'], q['wts'], q['l'], h, g1)
    h3, s['m'], s['a'], s['v'] = _f_mlp(s['h2'], q['nw2'], sc2, sh2, g2, q['wts'], q['l'])
    return h3, s


STACKED = {'mlp_w1': (2, 4, D, HID // 4), 'mlp_w2': (2, HID, D), 'w_out': (2, D, D)}


def _layer_bwd(dh3, q, s, l, stacked, early=None):
    sh1, sc1, g1, sh2, sc2, g2 = q['mod']
    g = {}
    dv, da, act, dm = _b_mlp(dh3, s['a'], g2, q['wts'], l)
    g['mlp_w1'] = _tn_matmul(s['v'], da, "dw1", col_major=True, into=stacked['mlp_w1'], layer=l)
    g['mlp_w2'] = _tn_matmul(act, dm, "dw2", into=stacked['mlp_w2'], layer=l)
    dh2, dsc2, dsh2, dnw2, dg2, dya, dyb, dyc, dyd, do, dbw = _b_out(
        dv, s['h2'], dh3, s['m'], q['nw2'], sc2, s['ya'], s['yb'], s['yc'], s['yd'], q['bw'], q['wts'], l, g1)
    g['w_out'] = _tn_matmul(s['cat'], do, "dwout", into=stacked['w_out'], layer=l)
    g['branch_norm_w'] = dbw[0]
    if early is not None:
        zero = early(g)[0, 0]
        q = dict(q, pool_scale=q['pool_scale'] + zero, conv_b=q['conv_b'] + zero, s5_d=q['s5_d'] + zero)
    dab, dpm, dps, dsw = _b_ab(s['proj'], dya, dyb, q['pool_mat'], q['pool_scale'], q['sconv_w'])
    g['pool_w'] = _pool_extract(dpm)
    g['pool_scale'] = dps[0]
    g['sconv_w'] = dsw
    dz, dxbc, ddt, dcw, dcb, ddtb, dal, ddk = _b_ssd(s['proj'], s['dtp'], s['ypre'], dyc, s['sprev'], q['conv_w'],
                                                     q['conv_b'], q['dt_bias'], q['a_log'], q['ssd_d'])
    g['ssd_conv_w'] = dcw
    g['ssd_conv_b'] = dcb[0]
    g['ssd_dt_bias'] = ddtb[0, :4]
    g['ssd_a_log'] = dal[0, :4]
    g['ssd_d'] = ddk[0, :4]
    tb = s['tables']
    ds5, dbmat, dcre, dcim, dlam, dd5, dgw, dgb = _b_s5(s['proj'], dyd, s['carries'], s['states'], s['bmat'], q['cre'], q['cim'],
                                                        tb[0], tb[1], q['s5_d'], q['glu_w'], q['glu_b'])
    g['s5_c_re'] = _cmat_extract(dcre)
    g['s5_c_im'] = -_cmat_extract(dcim)
    g['s5_d'] = dd5[0]
    g['s5_glu_w'] = dgw
    g['s5_glu_b'] = dgb[0]
    dbbr = _bmat_extract(dbmat[:, :S5_P]).reshape(16, 1024)
    dbbi = _bmat_extract(dbmat[:, S5_P:]).reshape(16, 1024)
    dar, dai, dls, dbr, dbi = _s5_prep_bwd(*q['s5_raw'], dlam[0].reshape(16, 64), dlam[1].reshape(16, 64), dbbr, dbbi)
    g['s5_a_re'], g['s5_a_im'], g['s5_log_step'] = dar, dai, dls[:, 0]
    g['s5_b_re'], g['s5_b_im'] = dbr, dbi
    dh, dsc1, dsh1, dnw1, dg1 = _b_in(dab, dz, dxbc, ds5, ddt, q['w_main'], q['w_dt'], s['h'], dh2, s['o'], q['nw1'], sc1)
    u = s['u']
    head = jnp.concatenate([_tn_matmul(dab, u, "dwin_ab"), _tn_matmul(dz, u, "dwin_z"), _tn_matmul(dxbc, u, "dwin_xbc"),
                            _tn_matmul(ddt, u, "dwin_dt")[:8]], axis=0)
    full = lax.dynamic_update_slice(jnp.zeros((2308, D), F32), head, (0, 0))
    g['w_in'] = lax.dynamic_update_slice(full, _tn_matmul(ds5, u, "dwin_s5"), (2052, 0))
    g['norm_mix_w'] = dnw1[0]
    g['norm_mlp_w'] = dnw2[0]
    dmod = jnp.concatenate([dsh1, dsc1, dg1, dsh2, dsc2, dg2], axis=1)
    return dh, g, dmod


def _local_step(x, tgt, p, mod, w_in_of, rest_of, early=None):
    h = x
    qs, saved = [], []
    for l in range(2):
        qs.append(_layer_params(p, l, mod[l], w_in_of(l), functools.partial(rest_of, l)))
        h, s = _layer_fwd(h, qs[l])
        saved.append(s)
    dh, loss, dfw = _b_final(h, tgt, p['final_norm_w'].reshape(1, D))
    grads = [None, None]
    dmods = [None, None]
    dh, grads[1], dmods[1] = _layer_bwd(dh, qs[1], saved[1], 1, {k: lax.empty(shp, F32) for k, shp in STACKED.items()})
    dh, grads[0], dmods[0] = _layer_bwd(dh, qs[0], saved[0], 0, grads[1], early)
    out = {k: jnp.stack([grads[0][k], grads[1][k]]) for k in grads[0] if k not in STACKED}
    if early is None:
        out.update({k: grads[0][k] for k in STACKED})
    out['final_norm_w'] = dfw[0]
    return loss, dh, out, jnp.concatenate(dmods, axis=0)


def _shard_of(a, axis, k):
    n = a.shape[axis] // 4
    return lax.dynamic_slice_in_dim(a, k * n, n, axis)


def kernel(x, c, norm_mix_w, norm_mlp_w, ada_w, ada_b, w_in, pool_w, pool_scale, sconv_w, ssd_conv_w, ssd_conv_b, ssd_dt_bias, ssd_a_log, ssd_d, s5_a_re, s5_a_im, s5_log_step, s5_b_re, s5_b_im, s5_c_re, s5_c_im, s5_d, s5_glu_w, s5_glu_b, branch_norm_w, w_out, mlp_w1, mlp_w2, final_norm_w, loss_target, m_norm_mix_w, m_norm_mlp_w, m_ada_w, m_ada_b, m_w_in, m_pool_w, m_pool_scale, m_sconv_w, m_ssd_conv_w, m_ssd_conv_b, m_ssd_dt_bias, m_ssd_a_log, m_ssd_d, m_s5_a_re, m_s5_a_im, m_s5_log_step, m_s5_b_re, m_s5_b_im, m_s5_c_re, m_s5_c_im, m_s5_d, m_s5_glu_w, m_s5_glu_b, m_branch_norm_w, m_w_out, m_mlp_w1, m_mlp_w2, m_final_norm_w, v_norm_mix_w, v_norm_mlp_w, v_ada_w, v_ada_b, v_w_in, v_pool_w, v_pool_scale, v_sconv_w, v_ssd_conv_w, v_ssd_conv_b, v_ssd_dt_bias, v_ssd_a_log, v_ssd_d, v_s5_a_re, v_s5_a_im, v_s5_log_step, v_s5_b_re, v_s5_b_im, v_s5_c_re, v_s5_c_im, v_s5_d, v_s5_glu_w, v_s5_glu_b, v_branch_norm_w, v_w_out, v_mlp_w1, v_mlp_w2, v_final_norm_w):
    loc = locals()
    w = {n: loc[n] for n in WEIGHTS}
    mom = {n: loc['m_' + n] for n in WEIGHTS}
    var = {n: loc['v_' + n] for n in WEIGHTS}
    ix, iy, ic = lax.axis_index("x"), lax.axis_index("y"), lax.axis_index("c")
    chip = 2 * ix + iy
    dev = 4 * ix + 2 * iy + ic

    mine_of = lambda a: lax.dynamic_index_in_dim(a.astype(BF16), ic, axis=0, keepdims=False)
    pad_in = lambda a: jnp.pad(a.T, ((0, WIN_ROWS - 577), (0, 0)))
    shard = jnp.concatenate([mine_of(w['mlp_w1']), mine_of(w['mlp_w2']), mine_of(w['w_out']), pad_in(mine_of(w['w_in']))], axis=0)

    (c_all,) = _exchange([c], EVERYONE, False, "ag_cond", stage=True)
    c_all = c_all.reshape(8, D)
    small_sh = _exchange([w[n] for n in SMALL_SHARDED], CHIPS, False, "ag_small")
    (w_in0,) = _exchange([pad_in(w['w_in'][0].astype(BF16))], CHIPS, False, "ag_win0")
    p = {n: w[n] for n in WEIGHTS if n not in BIG}
    for n, g in zip(SMALL_SHARDED, small_sh):
        ax = SMALL_SHARDED[n]
        p[n] = jnp.concatenate([g[k] for k in range(4)], axis=ax)

    def w_in_full(sh):
        return sh[:, :577].reshape(4 * 577, D)

    big = {}

    def fetch(after):
        if not big:
            (mine,), (got,) = _split_wait(sems, shard_thru, land, after, False, "ag_big_wait", per_core=True)
            got = lax.dynamic_update_slice(got, mine[None, None], (ic, chip, 0, 0))
            (both,) = _pair_swap([got.reshape(2, -1, D)], False, "swap_big", fill=True)
            big['both'] = both.reshape(got.shape)
        return big['both']

    def w_in_of(l):
        return w_in_full(w_in0) if l == 0 else w_in_full(fetch(None)[1, :, ROW_WIN:])

    def rest_of(l, after):
        return fetch(after)

    ada_b_sh = _shard_of(w['ada_b'], 1, chip).reshape(2, 1, 6 * D // 4)
    mod_sh = _ada_fwd(c_all, w['ada_w'], ada_b_sh)
    (mod_all,) = _exchange([mod_sh], CHIPS, False, "ag_mod", stage=True)
    mine = lax.dynamic_index_in_dim(mod_all, dev, axis=2, keepdims=False)
    sems, shard_thru, land, token = _split_start([shard], [mod_all, w_in0] + small_sh, False, "ag_big_start", per_core=True)
    mod = jnp.transpose(mine, (1, 0, 2)).reshape(2, 6, D) + token[0, 0]

    layer = ic.astype(jnp.int32).reshape(1)
    flight = {}

    def early(g0):
        gws = [g0['w_out'].reshape(2, 4, 256, D), g0['mlp_w1'], g0['mlp_w2'].reshape(2, 4, 1024, D)]
        got = _pair_swap([a.reshape(2, -1, D) for a in gws], True, "swap_grad", narrow=True)
        pair = [_pair_sum(a, b.reshape(a.shape[1:]), layer, "pair_sum%d" % (k + 1), BF16) for k, (a, b) in enumerate(zip(gws, got))]
        flight['sems'], flight['srcs'], flight['lands'], token = _split_start(pair, [], True, "rs_start")
        return token

    loss, grad_x, g, dmod = _local_step(x[0], loss_target[0], p, mod, w_in_of, rest_of, early)

    (dmod_all,) = _exchange([dmod], EVERYONE, False, "ag_dmod", stage=True)
    dmod_all = jnp.transpose(dmod_all, (1, 0, 2))

    gw_in = jnp.pad(g['w_in'].reshape(2, 4, 577, D), ((0, 0), (0, 0), (0, WIN_ROWS - 577), (0, 0)))
    (got_in,) = _pair_swap([gw_in.reshape(2, -1, D)], True, "swap_grad_in", narrow=True)
    pair_in = _pair_sum(gw_in, got_in.reshape(gw_in.shape[1:]), layer, "pair_sum0", BF16)
    in_sems, in_srcs, in_lands, in_token = _split_start([pair_in], [dmod_all], True, "rs_in_start")

    def chip_sum(land, mine, name):
        own = lax.dynamic_index_in_dim(mine, chip, axis=0, keepdims=True)
        return _sum_lead(lax.dynamic_update_slice(land, own, (chip, 0, 0)), name, F32)

    sent, lands = _split_wait(flight['sems'], flight['srcs'], flight['lands'], [grad_x, in_token], True, "rs_wait")
    quad = [chip_sum(land, mine, "rs_chip_sum%d" % (k + 1)) for k, (land, mine) in enumerate(zip(lands, sent))]
    g_ada_w, g_ada_b = _ada_bwd(c_all, _shard_of(dmod_all, 2, chip), dmod_all)
    adam_ada_w = _adamw(w['ada_w'], g_ada_w, mom['ada_w'], var['ada_w'], "adamw_ada_w")
    (sent_in,), (land_in,) = _split_wait(in_sems, in_srcs, in_lands, quad + [adam_ada_w[0]], True, "rs_in_wait")
    quad = [chip_sum(land_in, sent_in, "rs_chip_sum0")] + quad
    halves = [lax.dynamic_update_slice(lax.empty((2,) + a.shape, F32), a[None], (ic, 0, 0)) for a in quad]
    both = _pair_swap(halves, False, "swap_red", fill=True)
    both[0] = jnp.transpose(both[0][:, :577], (0, 2, 1))
    red = dict(zip(('w_in', 'w_out', 'mlp_w1', 'mlp_w2'), both))
    red['ada_w'] = g_ada_w

    small_names = [n for n in WEIGHTS if n not in BIG and n != 'ada_b']
    pair_parts = _exchange([g[n] for n in small_names] + [loss], SIBLING, False, "ag_smallpair", stage=True)
    chip_parts = _exchange(_sum_many(pair_parts, "smallpair_sum"), CHIPS, False, "ag_smallgrad", stage=True)
    summed = _sum_many(chip_parts, "smallgrad_sum")
    for n, a in zip(small_names, summed[:-1]):
        a = a.reshape(w[n].shape) if n in ('s5_b_re', 's5_b_im') else a
        red[n] = _shard_of(a, SMALL_SHARDED[n], chip) if n in SMALL_SHARDED else a
    red['ada_b'] = g_ada_b
    loss_out = summed[-1].reshape(())

    delta, new_m, new_v = {}, {}, {}
    delta['ada_w'], new_m['ada_w'], new_v['ada_w'] = adam_ada_w
    for n in BIG[1:]:
        delta[n], new_m[n], new_v[n] = _adamw(w[n], red[n], mom[n], var[n], "adamw_" + n)
    rest = [n for n in WEIGHTS if n not in BIG]
    lanes = lambda n, a: a.reshape(2, 16, 1024) if n in ('s5_b_re', 's5_b_im') else a
    outs = _adamw_many(*[[lanes(n, src[n]) for n in rest] for src in (w, red, mom, var)], "adamw_small")
    for k, n in enumerate(rest):
        delta[n], new_m[n], new_v[n] = (outs[3 * k + j].reshape(w[n].shape) for j in range(3))

    return (loss_out, grad_x[None], *[red[n] for n in WEIGHTS], *[delta[n] for n in WEIGHTS],
            *[new_m[n] for n in WEIGHTS], *[new_v[n] for n in WEIGHTS])
```

```python
import functools
import math

import jax
import jax.numpy as jnp
from jax import lax
from jax.experimental import pallas as pl
from jax.experimental.pallas import tpu as pltpu

F32 = jnp.float32
BF16 = jnp.bfloat16
HI = lax.Precision.HIGHEST

D = 1024
GW = 256
HID = 4096
EPS = 1e-6
PW = 2304
DTW = 128
SSD_L = 128
SSD_SUB = 2
SSD_SUB_BWD = 2
NH, HP, NS = 4, 64, 128
S5_P = 1024
MESH = pl.DeviceIdType.MESH

ADAM_LR, ADAM_B1, ADAM_B2, ADAM_EPS, ADAM_WD, ADAM_STEP = 0.001, 0.9, 0.999, 1e-08, 0.01, 10

NT = (((1,), (1,)), ((), ()))
TN = (((0,), (0,)), ((), ()))

WEIGHTS = ['norm_mix_w', 'norm_mlp_w', 'ada_w', 'ada_b', 'w_in', 'pool_w', 'pool_scale', 'sconv_w', 'ssd_conv_w',
           'ssd_conv_b', 'ssd_dt_bias', 'ssd_a_log', 'ssd_d', 's5_a_re', 's5_a_im', 's5_log_step', 's5_b_re', 's5_b_im',
           's5_c_re', 's5_c_im', 's5_d', 's5_glu_w', 's5_glu_b', 'branch_norm_w', 'w_out', 'mlp_w1', 'mlp_w2',
           'final_norm_w']
BIG = ('ada_w', 'w_in', 'w_out', 'mlp_w1', 'mlp_w2')
SMALL_SHARDED = {'sconv_w': 2, 'ssd_conv_w': 2, 's5_glu_w': 1}


def _cparams(n_axes, vmem_mb=48):
    return pltpu.CompilerParams(dimension_semantics=("arbitrary",) * n_axes, vmem_limit_bytes=vmem_mb * 1024 * 1024)


def _row(n):
    return pl.BlockSpec((1, n), lambda *_: (0, 0))


def _full(shape):
    nd = len(shape)
    return pl.BlockSpec(tuple(shape), lambda *_: (0,) * nd)


def _dot(a, b, dims=None, prec=None):
    if dims is None:
        dims = (((a.ndim - 1,), (0,)), ((), ()))
    return lax.dot_general(a, b, dims, preferred_element_type=F32, precision=prec)


def _bdot(a, b, dims=None):
    return _dot(a.astype(BF16), b.astype(BF16), dims)


def _sig(x):
    return jax.nn.sigmoid(x)


def _silu(x):
    return x * _sig(x)


def _dsilu(x):
    s = _sig(x)
    return s * (1.0 + x * (1.0 - s))


def _softplus(x):
    return jnp.maximum(x, 0.0) + jnp.log(1.0 + jnp.exp(-jnp.abs(x)))


_GK = math.sqrt(2.0 / math.pi)


def _gelu(x):
    return 0.5 * x * (1.0 + jnp.tanh(_GK * (x + 0.044715 * x * x * x)))


def _dgelu(x):
    th = jnp.tanh(_GK * (x + 0.044715 * x * x * x))
    return 0.5 * (1.0 + th) + 0.5 * x * (1.0 - th * th) * _GK * (1.0 + 3.0 * 0.044715 * x * x)


def _colsum(x):
    return jnp.sum(x, axis=0, keepdims=True)


def _rms(x):
    r = lax.rsqrt(jnp.mean(x * x, axis=-1, keepdims=True) + EPS)
    return r, x * r


def _rms_bwd(r, n, dn):
    return r * (dn - n * jnp.mean(dn * n, axis=-1, keepdims=True))


def _roll(x, k):
    n = x.shape[0]
    k = k % n
    return x if k == 0 else pltpu.roll(x, k, axis=0)


def _tblock(t, want=512):
    return min(t, want)


def _peer(mask):
    x, y, c = lax.axis_index("x"), lax.axis_index("y"), lax.axis_index("c")
    return (x ^ ((mask >> 2) & 1), y ^ ((mask >> 1) & 1), c ^ (mask & 1))


def _group_index(masks):
    x, y, c = lax.axis_index("x"), lax.axis_index("y"), lax.axis_index("c")
    full = 0
    for m in masks:
        full |= m
    bits = [b for b in (4, 2, 1) if full & b]

    def idx(px, py, pc):
        v = {4: px, 2: py, 1: pc}
        out = 0
        for b in bits:
            out = out * 2 + v[b]
        return out

    return idx(x, y, c), [idx(*_peer(m)) for m in masks]


def _exchange(arrs, masks, scatter, name, stage=False):
    n_arr, n_peer, n_grp = len(arrs), len(masks), len(masks) + 1

    def body(*refs):
        ins, outs = refs[:n_arr], refs[n_arr:2 * n_arr]
        send_sems, recv_sems, local_sems = refs[2 * n_arr:2 * n_arr + 3]
        if stage:
            bufs, load_sems = refs[2 * n_arr + 3:3 * n_arr + 3], refs[3 * n_arr + 3]
            loads = [pltpu.make_async_copy(ins[t], bufs[t], load_sems.at[t]) for t in range(n_arr)]
            for ld in loads:
                ld.start()
            for ld in loads:
                ld.wait()
            ins = bufs
        me, peer_idx = _group_index(masks)
        copies = []
        for t in range(n_arr):
            src_me = ins[t].at[me] if scatter else ins[t]
            loc = pltpu.make_async_copy(src_me, outs[t].at[me], local_sems.at[t])
            loc.start()
            copies.append(loc)
            for j, m in enumerate(masks):
                src = ins[t].at[peer_idx[j]] if scatter else ins[t]
                cp = pltpu.make_async_remote_copy(src_ref=src, dst_ref=outs[t].at[me], send_sem=send_sems.at[t, j],
                                                  recv_sem=recv_sems.at[t, j], device_id=_peer(m), device_id_type=MESH)
                cp.start()
                copies.append(cp)
        for cp in copies:
            cp.wait()

    hbm = pl.BlockSpec(memory_space=pl.ANY)
    out_shape = [jax.ShapeDtypeStruct((n_grp,) + (a.shape[1:] if scatter else a.shape), a.dtype) for a in arrs]
    staging = [pltpu.VMEM(a.shape, a.dtype) for a in arrs] + [pltpu.SemaphoreType.DMA((n_arr,))] if stage else []
    outs = pl.pallas_call(
        body, name=name, in_specs=[hbm] * n_arr, out_specs=[hbm] * n_arr, out_shape=out_shape,
        scratch_shapes=[pltpu.SemaphoreType.DMA((n_arr, n_peer)), pltpu.SemaphoreType.DMA((n_arr, n_peer)),
                        pltpu.SemaphoreType.DMA((n_arr,))] + staging,
        compiler_params=pltpu.CompilerParams(vmem_limit_bytes=48 * 1024 * 1024),
    )(*arrs)
    return list(outs)


def _split_copies(src_refs, land_refs, sems, scatter, per_core):
    me, peer_idx = _group_index(CHIPS)
    n = len(CHIPS) * len(src_refs)
    copies = []
    for t, (src_ref, land_ref) in enumerate(zip(src_refs, land_refs)):
        zone = land_ref.at[lax.axis_index("c")] if per_core else land_ref
        for j, m in enumerate(CHIPS):
            k = len(CHIPS) * t + j
            copies.append(pltpu.make_async_remote_copy(
                src_ref=src_ref.at[peer_idx[j]] if scatter else src_ref, dst_ref=zone.at[me], send_sem=sems[k],
                recv_sem=sems[n + k], device_id=_peer(m), device_id_type=MESH))
    return copies


def _split_start(srcs, after, scatter, name, per_core=False):
    n_arr, n_sem = len(srcs), 2 * len(CHIPS) * len(srcs)

    def body(*refs):
        src_refs, land_refs = refs[:n_arr], refs[n_arr:2 * n_arr]
        outs = refs[2 * n_arr + len(after):]
        for cp in _split_copies(src_refs, land_refs, outs[:n_sem], scatter, per_core):
            cp.start()
        outs[-1][...] = jnp.zeros_like(outs[-1])

    hbm = pl.BlockSpec(memory_space=pltpu.HBM)
    sem = pl.BlockSpec(memory_space=pltpu.SEMAPHORE)
    lands = [lax.empty(((2,) if per_core else ()) + (len(CHIPS) + 1,) + (a.shape[1:] if scatter else a.shape), a.dtype)
             for a in srcs]
    as_hbm = lambda a: pltpu.with_memory_space_constraint(a, pltpu.HBM)
    outs = pl.pallas_call(
        body, name=name,
        out_shape=(pltpu.SemaphoreType.DMA(()),) * n_sem + tuple(pltpu.HBM(a.shape, a.dtype) for a in srcs + lands)
        + (jax.ShapeDtypeStruct((8, 128), F32),),
        in_specs=(hbm,) * (2 * n_arr) + (pl.BlockSpec(memory_space=pl.ANY),) * len(after),
        out_specs=(sem,) * n_sem + (hbm,) * (2 * n_arr) + (pl.BlockSpec(memory_space=pltpu.VMEM),),
        input_output_aliases={t: n_sem + t for t in range(2 * n_arr)},
        compiler_params=pltpu.CompilerParams(has_side_effects=pltpu.SideEffectType.DATAFLOW_SIDE_EFFECTING),
    )(*[as_hbm(a) for a in srcs + lands], *after)
    return outs[:n_sem], list(outs[n_sem:n_sem + n_arr]), list(outs[n_sem + n_arr:n_sem + 2 * n_arr]), outs[-1]


def _split_wait(sems, srcs, lands, after, scatter, name, per_core=False):
    n_arr, n_sem = len(srcs), len(sems)

    def body(*refs):
        src_refs, land_refs = refs[:n_arr], refs[n_arr:2 * n_arr]
        for cp in _split_copies(src_refs, land_refs, refs[2 * n_arr:2 * n_arr + n_sem], scatter, per_core):
            cp.wait_send()
            cp.wait_recv()

    hbm = pl.BlockSpec(memory_space=pltpu.HBM)
    sem = pl.BlockSpec(memory_space=pltpu.SEMAPHORE)
    outs = pl.pallas_call(
        body, name=name, out_shape=tuple(pltpu.HBM(a.shape, a.dtype) for a in srcs + lands),
        in_specs=(hbm,) * (2 * n_arr) + (sem,) * n_sem + (pl.BlockSpec(memory_space=pl.ANY),) * len(after),
        out_specs=(hbm,) * (2 * n_arr), input_output_aliases={t: t for t in range(2 * n_arr)},
        compiler_params=pltpu.CompilerParams(has_side_effects=pltpu.SideEffectType.DATAFLOW_SIDE_EFFECTING),
    )(*srcs, *lands, *sems, *after)
    return list(outs[:n_arr]), list(outs[n_arr:])


CHIPS = (4, 2, 6)
EVERYONE = (1, 2, 3, 4, 5, 6, 7)
SIBLING = (1,)
SWAP_ROWS = 1024
WIN_ROWS = 592


def _pair_swap(arrs, other_layer, name, narrow=False, fill=False):
    assert not (fill and (other_layer or narrow))
    n_arr = len(arrs)
    shapes = [a.shape[-2:] for a in arrs]
    out_dtypes = [BF16 if narrow else a.dtype for a in arrs]
    chunks = []
    for t, (rows, _) in enumerate(shapes):
        assert rows % 16 == 0
        for j, r0 in enumerate(range(0, rows, SWAP_ROWS)):
            chunks.append((t, r0, min(SWAP_ROWS, rows - r0), j % 2))

    def body(*refs):
        ins, outs = refs[:n_arr], refs[n_arr:2 * n_arr]
        bufs = refs[2 * n_arr:3 * n_arr]
        out_bufs = refs[3 * n_arr:4 * n_arr] if narrow else bufs
        load_sems, send_sems, recv_sems = refs[-3:]
        sibling = _peer(1)
        c = lax.axis_index("c")

        def load(k):
            t, r0, n, slot = chunks[k]
            src = ins[t].at[1 - c] if other_layer else ins[t].at[c] if fill else ins[t]
            return pltpu.make_async_copy(src.at[pl.ds(r0, n)], bufs[t].at[slot, pl.ds(0, n)], load_sems.at[t, slot])

        def send(k):
            t, r0, n, slot = chunks[k]
            dst = outs[t].at[c] if fill else outs[t]
            return pltpu.make_async_remote_copy(src_ref=out_bufs[t].at[slot, pl.ds(0, n)], dst_ref=dst.at[pl.ds(r0, n)],
                                                send_sem=send_sems.at[t, slot], recv_sem=recv_sems.at[t],
                                                device_id=sibling, device_id_type=MESH)

        in_flight = {}

        def drain(k):
            key = (chunks[k][0], chunks[k][3])
            if key in in_flight:
                send(in_flight.pop(key)).wait_send()

        def start_load(k):
            if not narrow:
                drain(k)
            load(k).start()

        start_load(0)
        for k in range(len(chunks)):
            t, _, n, slot = chunks[k]
            load(k).wait()
            if k + 1 < len(chunks):
                start_load(k + 1)
            if narrow:
                drain(k)
                out_bufs[t][slot, pl.ds(0, n), :] = bufs[t][slot, pl.ds(0, n), :].astype(BF16)
            send(k).start()
            in_flight[(t, slot)] = k
        for k in in_flight.values():
            send(k).wait_send()
        for t in range(n_arr):
            landed = outs[t].at[1 - c] if fill else outs[t]
            pltpu.make_async_remote_copy(src_ref=landed, dst_ref=landed, send_sem=send_sems.at[t, 0],
                                         recv_sem=recv_sems.at[t], device_id=sibling, device_id_type=MESH).wait_recv()

    hbm = pl.BlockSpec(memory_space=pl.ANY)
    outs = pl.pallas_call(
        body, name=name, in_specs=[hbm] * n_arr, out_specs=[hbm] * n_arr,
        out_shape=[jax.ShapeDtypeStruct(a.shape if fill else s, dt) for a, s, dt in zip(arrs, shapes, out_dtypes)],
        input_output_aliases={t: t for t in range(n_arr)} if fill else {},
        scratch_shapes=[pltpu.VMEM((2, min(SWAP_ROWS, s[0]), s[1]), a.dtype) for s, a in zip(shapes, arrs)]
        + ([pltpu.VMEM((2, min(SWAP_ROWS, s[0]), s[1]), BF16) for s in shapes] if narrow else [])
        + [pltpu.SemaphoreType.DMA((n_arr, 2)), pltpu.SemaphoreType.DMA((n_arr, 2)), pltpu.SemaphoreType.DMA((n_arr,))],
        compiler_params=pltpu.CompilerParams(vmem_limit_bytes=48 * 1024 * 1024),
    )(*arrs)
    return list(outs)


def _sum_lead(a, name, out_dtype):
    n = a.shape[0]
    shape = a.shape[1:]

    def body(a_ref, o_ref):
        acc = a_ref[0].astype(F32)
        for k in range(1, n):
            acc = acc + a_ref[k].astype(F32)
        o_ref[...] = acc.astype(out_dtype)

    if len(shape) == 3:
        blk = (1,) + shape[1:]
        return pl.pallas_call(
            body, name=name, grid=(shape[0],), in_specs=[pl.BlockSpec((n,) + blk, lambda i: (0, i, 0, 0))],
            out_specs=pl.BlockSpec(blk, lambda i: (i, 0, 0)), out_shape=jax.ShapeDtypeStruct(shape, out_dtype),
            compiler_params=_cparams(1),
        )(a)
    rows, cols = shape
    rb = rows
    for cand in (512, 256, 128):
        if rows % cand == 0 and rows > cand:
            rb = cand
            break
    return pl.pallas_call(
        body, name=name, grid=(rows // rb,), in_specs=[pl.BlockSpec((n, rb, cols), lambda i: (0, i, 0))],
        out_specs=pl.BlockSpec((rb, cols), lambda i: (i, 0)), out_shape=jax.ShapeDtypeStruct((rows, cols), out_dtype),
        compiler_params=_cparams(1),
    )(a)


def _pair_sum(g, recv, layer, name, out_dtype):
    _, n, r, c = g.shape

    def body(l_ref, g_ref, r_ref, o_ref):
        o_ref[...] = (g_ref[0].astype(F32) + r_ref[...].astype(F32)).astype(out_dtype)

    return pl.pallas_call(
        body, name=name,
        grid_spec=pltpu.PrefetchScalarGridSpec(
            num_scalar_prefetch=1, grid=(n,),
            in_specs=[pl.BlockSpec((1, 1, r, c), lambda i, l: (l[0], i, 0, 0)), pl.BlockSpec((1, r, c), lambda i, l: (i, 0, 0))],
            out_specs=pl.BlockSpec((1, r, c), lambda i, l: (i, 0, 0))),
        out_shape=jax.ShapeDtypeStruct((n, r, c), out_dtype), compiler_params=_cparams(1),
    )(layer, g, recv)


def _tn_matmul(a, b, name, col_major=False, into=None, layer=0):
    t, k = a.shape
    n = b.shape[1]
    tb = _tblock(t, 1024)
    kb = min(k, 1024)
    nb = min(n, 1024)
    grid = (k // kb, n // nb, t // tb)
    lead = (into is not None) + col_major

    def body(a_ref, b_ref, *rest):
        o_ref = rest[-1]
        for _ in range(lead):
            o_ref = o_ref.at[0]

        @pl.when(pl.program_id(2) == 0)
        def _():
            o_ref[...] = jnp.zeros_like(o_ref)

        o_ref[...] += _bdot(a_ref[...], b_ref[...], TN)

    if col_major:
        block, index, shape = (1, kb, nb), (lambda ki, ni: (ni, ki, 0)), (n // nb, k, nb)
    else:
        block, index, shape = (kb, nb), (lambda ki, ni: (ki, ni)), (k, n)
    in_specs = [pl.BlockSpec((tb, kb), lambda ki, ni, ti: (ti, ki)), pl.BlockSpec((tb, nb), lambda ki, ni, ti: (ti, ni))]
    if into is None:
        return pl.pallas_call(
            body, name=name, grid=grid, in_specs=in_specs, out_specs=pl.BlockSpec(block, lambda ki, ni, ti: index(ki, ni)),
            out_shape=jax.ShapeDtypeStruct(shape, F32), compiler_params=_cparams(3),
        )(a, b)
    assert into.shape == (2,) + shape
    return pl.pallas_call(
        body, name=name, grid=grid, in_specs=in_specs + [pl.BlockSpec(memory_space=pl.ANY)],
        out_specs=pl.BlockSpec((1,) + block, lambda ki, ni, ti: (layer,) + index(ki, ni)),
        out_shape=jax.ShapeDtypeStruct(into.shape, F32), input_output_aliases={2: 0}, compiler_params=_cparams(3),
    )(a, b, into)


def _sum_many(arrs, name):
    k = len(arrs)

    def body(*refs):
        for a_ref, o_ref in zip(refs[:k], refs[k:]):
            acc = a_ref[0]
            for j in range(1, a_ref.shape[0]):
                acc = acc + a_ref[j]
            o_ref[...] = acc

    return pl.pallas_call(body, name=name, grid=(1,), in_specs=[_full(a.shape) for a in arrs],
                          out_specs=[_full(a.shape[1:]) for a in arrs],
                          out_shape=[jax.ShapeDtypeStruct(a.shape[1:], F32) for a in arrs], compiler_params=_cparams(1))(*arrs)


def _adamw_math(w, g, m, v):
    m2 = ADAM_B1 * m + (1.0 - ADAM_B1) * g
    v2 = ADAM_B2 * v + (1.0 - ADAM_B2) * (g * g)
    m_hat = m2 / (1.0 - ADAM_B1 ** ADAM_STEP)
    v_hat = v2 / (1.0 - ADAM_B2 ** ADAM_STEP)
    return -ADAM_LR * (m_hat / (jnp.sqrt(v_hat) + ADAM_EPS) + ADAM_WD * w), m2, v2


def _adamw_many(ws, gs, ms, vs, name):
    n = len(ws)

    def body(*refs):
        ins, outs = refs[:4 * n], refs[4 * n:]
        for k in range(n):
            res = _adamw_math(ins[k][...], ins[n + k][...], ins[2 * n + k][...], ins[3 * n + k][...])
            for j in range(3):
                outs[3 * k + j][...] = res[j]

    out_shape = []
    for a in ws:
        out_shape += [jax.ShapeDtypeStruct(a.shape, F32)] * 3
    return pl.pallas_call(body, name=name, grid=(1,), in_specs=[_full(a.shape) for a in ws] * 4,
                          out_specs=[_full(s.shape) for s in out_shape], out_shape=out_shape,
                          compiler_params=_cparams(1))(*ws, *gs, *ms, *vs)


def _adamw(w, g, m, v, name):
    shape = w.shape
    cols = shape[-1]
    rows = int(math.prod(shape[:-1]))
    rb = rows
    for cand in (256, 128, 64, 32, 16, 8):
        if rows % cand == 0 and rows > cand:
            rb = cand
            break
    bc1 = 1.0 - ADAM_B1 ** ADAM_STEP
    bc2 = 1.0 - ADAM_B2 ** ADAM_STEP

    def body(w_ref, g_ref, m_ref, v_ref, d_ref, nm_ref, nv_ref):
        gg = g_ref[...]
        m2 = ADAM_B1 * m_ref[...] + (1.0 - ADAM_B1) * gg
        v2 = ADAM_B2 * v_ref[...] + (1.0 - ADAM_B2) * (gg * gg)
        m_hat = m2 / bc1
        v_hat = v2 / bc2
        d_ref[...] = -ADAM_LR * (m_hat / (jnp.sqrt(v_hat) + ADAM_EPS) + ADAM_WD * w_ref[...])
        nm_ref[...] = m2
        nv_ref[...] = v2

    spec = pl.BlockSpec((rb, cols), lambda i: (i, 0))
    sds = jax.ShapeDtypeStruct((rows, cols), F32)
    outs = pl.pallas_call(
        body, name=name, grid=(rows // rb,), in_specs=[spec] * 4, out_specs=[spec] * 3, out_shape=[sds] * 3,
        compiler_params=_cparams(1),
    )(*(z.reshape(rows, cols) for z in (w, g, m, v)))
    return tuple(o.reshape(shape) for o in outs)


def _ada_fwd(c_all, ada_w_sh, ada_b_sh):
    s = ada_w_sh.shape[2]
    sb = 512

    def body(c_ref, w_ref, b_ref, o_ref):
        cond = _silu(c_ref[...])
        o_ref[0] = _bdot(cond, w_ref[0]) + b_ref[0]

    return pl.pallas_call(
        body, name="ada_fwd", grid=(2, s // sb),
        in_specs=[_full((8, D)), pl.BlockSpec((1, D, sb), lambda l, j: (l, 0, j)), pl.BlockSpec((1, 1, sb), lambda l, j: (l, 0, j))],
        out_specs=pl.BlockSpec((1, 8, sb), lambda l, j: (l, 0, j)), out_shape=jax.ShapeDtypeStruct((2, 8, s), F32),
        compiler_params=_cparams(2),
    )(c_all, ada_w_sh, ada_b_sh)


def _ada_bwd(c_all, dmod_sh, dmod_all):
    s = dmod_sh.shape[2]
    sb = 512

    def body(c_ref, d_ref, o_ref):
        cond = _silu(c_ref[...])
        o_ref[0] = _bdot(cond, d_ref[0], TN)

    gw = pl.pallas_call(
        body, name="ada_bwd_w", grid=(2, s // sb),
        in_specs=[_full((8, D)), pl.BlockSpec((1, 8, sb), lambda l, j: (l, 0, j))],
        out_specs=pl.BlockSpec((1, D, sb), lambda l, j: (l, 0, j)), out_shape=jax.ShapeDtypeStruct((2, D, s), F32),
        compiler_params=_cparams(2),
    )(c_all, dmod_sh)

    def body_b(d_ref, o_ref):
        acc = d_ref[0, 0:1, :]
        for k in range(1, 8):
            acc = acc + d_ref[0, k:k + 1, :]
        o_ref[0] = acc

    gb = pl.pallas_call(
        body_b, name="ada_bwd_b", grid=(2,), in_specs=[pl.BlockSpec((1, 8, 6 * D), lambda l: (l, 0, 0))],
        out_specs=pl.BlockSpec((1, 1, 6 * D), lambda l: (l, 0, 0)), out_shape=jax.ShapeDtypeStruct((2, 1, 6 * D), F32),
        compiler_params=_cparams(1),
    )(dmod_all)
    return gw, gb.reshape(2, 6 * D)


def _f_in(h, nw, sc, sh, w_main, w_dt):
    t = h.shape[0]
    tb = _tblock(t)

    def body(h_ref, nw_ref, sc_ref, sh_ref, w_ref, wd_ref, p_ref, dt_ref, u_ref):
        _, n = _rms(h_ref[...])
        u = ((n * nw_ref[...]) * (1.0 + sc_ref[...]) + sh_ref[...]).astype(BF16)
        u_ref[...] = u
        p_ref[...] = _dot(u, w_ref[...], NT)
        dt_ref[...] = _dot(u, wd_ref[...], NT)

    return pl.pallas_call(
        body, name="f_in", grid=(t // tb,),
        in_specs=[pl.BlockSpec((tb, D), lambda i: (i, 0)), _row(D), _row(D), _row(D), _full((PW, D)), _full((DTW, D))],
        out_specs=[pl.BlockSpec((tb, PW), lambda i: (i, 0)), pl.BlockSpec((tb, DTW), lambda i: (i, 0)),
                   pl.BlockSpec((tb, D), lambda i: (i, 0))],
        out_shape=[jax.ShapeDtypeStruct((t, PW), F32), jax.ShapeDtypeStruct((t, DTW), F32), jax.ShapeDtypeStruct((t, D), BF16)],
        compiler_params=_cparams(1),
    )(h, nw, sc, sh, w_main, w_dt)


def _norm_bwd_step(du_v, x, dres_v, gated, nwv, scv, dx_ref, dsc_ref, dsh_ref, dnw_ref, dg_ref):
    r, n = _rms(x)
    scale = 1.0 + scv
    dsc_ref[...] += _colsum(du_v * (n * nwv))
    dsh_ref[...] += _colsum(du_v)
    dnw_ref[...] += _colsum(du_v * scale * n)
    dg_ref[...] += _colsum(dres_v * gated)
    dx_ref[...] = dres_v + _rms_bwd(r, n, du_v * scale * nwv)


def _b_in(dab, dz, dxbc, ds5, ddt, w_main, w_dt, x, dres, gated, nw, sc):
    t = dab.shape[0]
    tb = _tblock(t)

    def body(a_ref, z_ref, x_ref, s_ref, d_ref, w_ref, wd_ref, h_ref, dr_ref, g_ref, nw_ref, sc_ref,
             dx_ref, dsc_ref, dsh_ref, dnw_ref, dg_ref):
        @pl.when(pl.program_id(0) == 0)
        def _():
            for r in (dsc_ref, dsh_ref, dnw_ref, dg_ref):
                r[...] = jnp.zeros_like(r)

        du = _bdot(a_ref[...], w_ref[0:1024, :])
        du += _bdot(z_ref[...], w_ref[1024:1280, :])
        du += _bdot(s_ref[...], w_ref[1280:1536, :])
        du += _bdot(x_ref[...], w_ref[1536:2304, :])
        du += _bdot(d_ref[...], wd_ref[...])
        _norm_bwd_step(du, h_ref[...], dr_ref[...], g_ref[...], nw_ref[...], sc_ref[...], dx_ref, dsc_ref, dsh_ref, dnw_ref, dg_ref)

    blk = lambda n: pl.BlockSpec((tb, n), lambda i: (i, 0))
    row = jax.ShapeDtypeStruct((1, D), F32)
    return pl.pallas_call(
        body, name="b_in", grid=(t // tb,),
        in_specs=[blk(1024), blk(256), blk(768), blk(256), blk(DTW), _full((PW, D)), _full((DTW, D)),
                  blk(D), blk(D), blk(D), _row(D), _row(D)],
        out_specs=[blk(D), _row(D), _row(D), _row(D), _row(D)],
        out_shape=[jax.ShapeDtypeStruct((t, D), F32), row, row, row, row], compiler_params=_cparams(1),
    )(dab, dz, dxbc, ds5, ddt, w_main, w_dt, x, dres, gated, nw, sc)


HALO = 16


def _lane_group(shape):
    return lax.broadcasted_iota(jnp.int32, shape, 1) // 64


def _window_select(g, s2, s4, s8, s16):
    return jnp.where(g == 0, s2, jnp.where(g == 1, s4, jnp.where(g == 2, s8, s16)))


def _pool_count(t0, rows):
    g = _lane_group((rows, GW))
    win = _window_select(g, 2, 4, 8, 16)
    tt = t0 + lax.broadcasted_iota(jnp.int32, (rows, GW), 0)
    return jnp.minimum(tt + 1, win).astype(F32)


def _pool_p(v_ext, t0, tb):
    s2 = v_ext + _roll(v_ext, 1)
    s4 = s2 + _roll(s2, 2)
    s8 = s4 + _roll(s4, 4)
    s16 = s8 + _roll(s8, 8)
    ws = _window_select(_lane_group(v_ext.shape), s2, s4, s8, s16)[HALO:]
    return ws / _pool_count(t0, tb) - v_ext[HALO:]


def _sconv(q_ext, w):
    return (_roll(q_ext, 2) * w[0:1] + _roll(q_ext, 1) * w[1:2] + q_ext * w[2:3])[HALO:]


def _halo_specs(t, tb, cols, col_block):
    per = tb // HALO
    last = t // HALO - 1
    prev = pl.BlockSpec((HALO, cols), lambda i: (jnp.maximum(i * per - 1, 0), col_block))
    nxt = pl.BlockSpec((HALO, cols), lambda i: (jnp.minimum((i + 1) * per, last), col_block))
    return prev, nxt


def _f_ab(proj, pool_mat, pool_scale, sconv_w):
    t = proj.shape[0]
    tb = _tblock(t)
    prev, _ = _halo_specs(t, tb, 1024, 0)

    def body(p_ref, h_ref, pm_ref, ps_ref, sw_ref, ya_ref, yb_ref):
        i = pl.program_id(0)
        halo = jnp.where(i > 0, h_ref[...], 0.0)
        ext = jnp.concatenate([halo, p_ref[...]], axis=0)
        p = _pool_p(ext[:, 0:256], i * tb, tb)
        ya_ref[...] = _bdot(p, pm_ref[...]) * ps_ref[...]
        q_ext = ext[:, 512:768] * ext[:, 768:1024]
        yb_ref[...] = p_ref[:, 256:512] * _sconv(q_ext, sw_ref[...])

    blk = pl.BlockSpec((tb, GW), lambda i: (i, 0))
    sds = jax.ShapeDtypeStruct((t, GW), F32)
    return pl.pallas_call(
        body, name="f_ab", grid=(t // tb,),
        in_specs=[pl.BlockSpec((tb, 1024), lambda i: (i, 0)), prev, _full((GW, GW)), _row(GW), _full((3, GW))],
        out_specs=[blk, blk], out_shape=[sds, sds], compiler_params=_cparams(1),
    )(proj, proj, pool_mat, pool_scale, sconv_w)


def _b_ab(proj, dya, dyb, pool_mat, pool_scale, sconv_w):
    t = proj.shape[0]
    tb = _tblock(t)
    nb = t // tb
    prev, nxt = _halo_specs(t, tb, 1024, 0)
    _, nxt_g = _halo_specs(t, tb, GW, 0)
    n_ext = tb + HALO

    def body(p_ref, hp_ref, hn_ref, da_ref, dan_ref, db_ref, dbn_ref, pm_ref, ps_ref, sw_ref,
             o_ref, dpm_ref, dps_ref, dsw_ref):
        i = pl.program_id(0)

        @pl.when(i == 0)
        def _():
            for r in (dpm_ref, dps_ref, dsw_ref):
                r[...] = jnp.zeros_like(r)

        last = i == nb - 1
        halo = jnp.where(i > 0, hp_ref[...], 0.0)
        main = p_ref[...]
        ext = jnp.concatenate([halo, main], axis=0)
        scale = ps_ref[...]
        pm = pm_ref[...]
        p = _pool_p(ext[:, 0:256], i * tb, tb)
        da = da_ref[...]
        dps_ref[...] += _colsum(da * _bdot(p, pm))
        da_ext = jnp.concatenate([da, jnp.where(last, 0.0, dan_ref[...])], axis=0)
        dys = da_ext * scale
        dpm_ref[...] += _bdot(p, dys[:tb], TN)
        dp = _bdot(dys, pm, NT)
        dpc = dp / _pool_count(i * tb, n_ext)
        a2 = dpc + _roll(dpc, n_ext - 1)
        a4 = a2 + _roll(a2, n_ext - 2)
        a8 = a4 + _roll(a4, n_ext - 4)
        a16 = a8 + _roll(a8, n_ext - 8)
        o_ref[:, 0:256] = (_window_select(_lane_group(dpc.shape), a2, a4, a8, a16) - dp)[:tb]
        w = sw_ref[...]
        gb, gc, hh = main[:, 256:512], main[:, 512:768], main[:, 768:1024]
        q_ext = ext[:, 512:768] * ext[:, 768:1024]
        db = db_ref[...]
        o_ref[:, 256:512] = db * _sconv(q_ext, w)
        gb_next = hn_ref[:, 256:512]
        dconv = jnp.concatenate([db * gb, jnp.where(last, 0.0, dbn_ref[...] * gb_next)], axis=0)
        dq = (dconv * w[2:3] + _roll(dconv, n_ext - 1) * w[1:2] + _roll(dconv, n_ext - 2) * w[0:1])[:tb]
        o_ref[:, 512:768] = dq * hh
        o_ref[:, 768:1024] = dq * gc
        dc = dconv[:tb]
        dsw_ref[0:1, :] += _colsum(dc * _roll(q_ext, 2)[HALO:])
        dsw_ref[1:2, :] += _colsum(dc * _roll(q_ext, 1)[HALO:])
        dsw_ref[2:3, :] += _colsum(dc * q_ext[HALO:])

    blk = pl.BlockSpec((tb, GW), lambda i: (i, 0))
    return pl.pallas_call(
        body, name="b_ab", grid=(nb,),
        in_specs=[pl.BlockSpec((tb, 1024), lambda i: (i, 0)), prev, nxt, blk, nxt_g, blk, nxt_g,
                  _full((GW, GW)), _row(GW), _full((3, GW))],
        out_specs=[pl.BlockSpec((tb, 1024), lambda i: (i, 0)), _full((GW, GW)), _row(GW), _full((3, GW))],
        out_shape=[jax.ShapeDtypeStruct((t, 1024), F32), jax.ShapeDtypeStruct((GW, GW), F32),
                   jax.ShapeDtypeStruct((1, GW), F32), jax.ShapeDtypeStruct((3, GW), F32)],
        compiler_params=_cparams(1),
    )(proj, proj, proj, dya, dya, dyb, dyb, pool_mat, pool_scale, sconv_w)


CH = 8


def _ssd_conv(x, halo, w, b):
    ext = jnp.concatenate([halo, x], axis=0)
    pre = ext * w[3:4] + _roll(ext, 1) * w[2:3] + _roll(ext, 2) * w[1:2] + _roll(ext, 3) * w[0:1] + b
    return pre[CH:], ext


def _ssd_common(dt_raw, dtb, alog):
    ll = dt_raw.shape[0]
    dtv = _softplus(dt_raw + dtb)
    a_row = -jnp.exp(alog)
    r = lax.broadcasted_iota(jnp.int32, (ll, ll), 0)
    c = lax.broadcasted_iota(jnp.int32, (ll, ll), 1)
    tril = (r >= c).astype(F32)
    cs = _dot(tril, dtv * a_row, prec=HI)
    return dtv, a_row, cs, cs.T, r >= c


def _bd(a, b, ca, cb):
    return lax.dot_general(a, b, (((ca,), (cb,)), ((0,), (0,))), preferred_element_type=F32)


def _head_cols(m):
    return jnp.stack([m[:, h:h + 1] for h in range(NH)])


def _ssd_heads(act, dtv, cs, cs_t, causal):
    xs = jnp.stack([act[:, HP * h:HP * (h + 1)] for h in range(NH)])
    bm = jnp.stack([act[:, 256 + NS * (h // 2):256 + NS * (h // 2 + 1)] for h in range(NH)])
    cm = jnp.stack([act[:, 512 + NS * (h // 2):512 + NS * (h // 2 + 1)] for h in range(NH)])
    cs_c = _head_cols(cs)
    cs_r = jnp.stack([cs_t[h:h + 1, :] for h in range(NH)])
    mdec = jnp.where(causal[None], jnp.exp(jnp.minimum(cs_c - cs_r, 0.0)), 0.0)
    g2 = _bd(jnp.stack([cm[0], cm[2]]), jnp.stack([bm[0], bm[2]]), 2, 2)
    sc = jnp.stack([g2[h // 2] for h in range(NH)]) * mdec
    dt_c = _head_cols(dtv)
    xdt = xs * dt_c
    e = jnp.exp(cs_c)
    cs_last = cs_c[:, SSD_L - 1:SSD_L, :]
    wdec = jnp.exp(cs_last - cs_c)
    return xs, bm, cm, mdec, sc, dt_c, xdt, e, cs_last, wdec


def _head_scalars(row_ref):
    return jnp.stack([row_ref[0:1, h:h + 1] for h in range(NH)])


def _f_ssd(proj, dtp, conv_w, conv_b, dt_bias, a_log, d_skip):
    t = proj.shape[0]
    nc = t // SSD_L
    rows = SSD_SUB * SSD_L
    per = rows // CH

    def body(x_ref, hx_ref, dt_ref, z_ref, cw_ref, cb_ref, dtb_ref, al_ref, dk_ref, y_ref, yp_ref, sp_ref, s_ref):
        i = pl.program_id(0)

        @pl.when(i == 0)
        def _():
            s_ref[...] = jnp.zeros_like(s_ref)

        state = s_ref[...]
        dk = _head_scalars(dk_ref)
        for sub in range(SSD_SUB):
            r0 = sub * SSD_L
            rs = slice(r0, r0 + SSD_L)
            halo = jnp.where(i > 0, hx_ref[...], 0.0) if sub == 0 else x_ref[r0 - CH:r0, :]
            pre, _ = _ssd_conv(x_ref[rs, :], halo, cw_ref[...], cb_ref[...])
            act = _silu(pre)
            dtv, _, cs, cs_t, causal = _ssd_common(dt_ref[rs, :], dtb_ref[...], al_ref[...])
            xs, bm, cm, _, sc, _, xdt, e, cs_last, wdec = _ssd_heads(act, dtv, cs, cs_t, causal)
            sp_ref[sub] = state
            y = _bd(sc, xdt, 2, 1) + e * _bd(cm, state, 2, 2) + xs * dk
            for h in range(NH):
                yp_ref[rs, HP * h:HP * (h + 1)] = y[h]
            state = state * jnp.exp(cs_last) + _bd(xdt * wdec, bm, 1, 1)
            y_ref[rs, :] = yp_ref[rs, :] * _silu(z_ref[rs, :])
        s_ref[...] = state

    blk = pl.BlockSpec((rows, GW), lambda i: (i, 0))
    sds = jax.ShapeDtypeStruct((t, GW), F32)
    return pl.pallas_call(
        body, name="f_ssd", grid=(nc // SSD_SUB,),
        in_specs=[pl.BlockSpec((rows, 768), lambda i: (i, 2)),
                  pl.BlockSpec((CH, 768), lambda i: (jnp.maximum(i * per - 1, 0), 2)),
                  pl.BlockSpec((rows, DTW), lambda i: (i, 0)),
                  pl.BlockSpec((rows, GW), lambda i: (i, 4)),
                  _full((4, 768)), _row(768), _row(DTW), _row(DTW), _row(DTW)],
        out_specs=[blk, blk, pl.BlockSpec((SSD_SUB, NH, HP, NS), lambda i: (i, 0, 0, 0))],
        out_shape=[sds, sds, jax.ShapeDtypeStruct((nc, NH, HP, NS), F32)],
        scratch_shapes=[pltpu.VMEM((NH, HP, NS), F32)], compiler_params=_cparams(1),
    )(proj, proj, dtp, proj, conv_w, conv_b, dt_bias, a_log, d_skip)


def _b_ssd(proj, dtp, ypre, dyc, sprev, conv_w, conv_b, dt_bias, a_log, d_skip):
    t = proj.shape[0]
    nc = t // SSD_L
    steps = nc // SSD_SUB_BWD
    rows = SSD_SUB_BWD * SSD_L
    per = rows // CH
    n_ext = SSD_L + CH

    def chunk(sub, halo, dnext, ds_in, refs):
        (x_ref, dt_ref, z_ref, yp_ref, dy_ref, sp_ref, cw_ref, cb_ref, dtb_ref, al_ref, dk_ref,
         dz_ref, dx_ref, ddt_ref, dact_ref) = refs
        rs = slice(sub * SSD_L, (sub + 1) * SSD_L)
        dact = dact_ref.at[sub]
        w = cw_ref[...]
        pre, ext = _ssd_conv(x_ref[rs, :], halo, w, cb_ref[...])
        act = _silu(pre)
        dt_raw = dt_ref[rs, :]
        dtv, a_row, cs, cs_t, causal = _ssd_common(dt_raw, dtb_ref[...], al_ref[...])
        z = z_ref[rs, :]
        dyc_v = dy_ref[rs, :]
        dz_ref[rs, :] = dyc_v * yp_ref[rs, :] * _dsilu(z)
        dy_all = dyc_v * _silu(z)
        lane = lax.broadcasted_iota(jnp.int32, (SSD_L, DTW), 1)
        rowi = lax.broadcasted_iota(jnp.int32, (1, SSD_L, 1), 1)
        lane1 = lax.broadcasted_iota(jnp.int32, (1, DTW), 1)
        xs, bm, cm, mdec, sc, dt_c, xdt, e, cs_last, wdec = _ssd_heads(act, dtv, cs, cs_t, causal)
        dy = jnp.stack([dy_all[:, HP * h:HP * (h + 1)] for h in range(NH)])
        prev = sp_ref[sub]
        ds = ds_in
        lsum = lambda v: jnp.sum(v, axis=2, keepdims=True)
        dsc = _bd(dy, xdt, 2, 2)
        q = dsc * sc
        dg = dsc * mdec
        dxdt = _bd(sc, dy, 1, 1)
        dcs = lsum(q) - lsum(jnp.swapaxes(q, 1, 2))
        dc = _bd(dg, bm, 2, 1)
        db = _bd(dg, cm, 1, 1)
        cp = _bd(cm, prev, 2, 2)
        dcs += lsum(dy * cp) * e
        ey = e * dy
        dc += _bd(ey, prev, 2, 1)
        dprev = _bd(ey, cm, 1, 1)
        elast = jnp.exp(cs_last)
        dprev += ds * elast
        dcs_last = jnp.sum(lsum(ds * prev), axis=1, keepdims=True) * elast
        bds = _bd(bm, ds, 2, 2)
        dxdt += wdec * bds
        db += wdec * _bd(xdt, ds, 2, 1)
        dw = lsum(xdt * bds) * wdec
        dcs -= dw
        dcs_last += jnp.sum(dw, axis=1, keepdims=True)
        dcs += jnp.where(rowi == SSD_L - 1, dcs_last, 0.0)
        dxs = dxdt * dt_c + dy * _head_scalars(dk_ref)
        ddtx = lsum(dxdt * xs)
        ddk = jnp.sum(lsum(dy * xs), axis=1, keepdims=True)
        dcs_mat = jnp.zeros((SSD_L, DTW), F32)
        ddtx_mat = jnp.zeros((SSD_L, DTW), F32)
        ddk_row = jnp.zeros((1, DTW), F32)
        for h in range(NH):
            dact[:, HP * h:HP * (h + 1)] = dxs[h]
            dcs_mat = jnp.where(lane == h, dcs[h], dcs_mat)
            ddtx_mat = jnp.where(lane == h, ddtx[h], ddtx_mat)
            ddk_row = jnp.where(lane1 == h, ddk[h], ddk_row)
        for g in range(2):
            dact[:, 256 + NS * g:256 + NS * (g + 1)] = db[2 * g] + db[2 * g + 1]
            dact[:, 512 + NS * g:512 + NS * (g + 1)] = dc[2 * g] + dc[2 * g + 1]
        ds_out = dprev
        r2 = lax.broadcasted_iota(jnp.int32, (SSD_L, SSD_L), 0)
        c2 = lax.broadcasted_iota(jnp.int32, (SSD_L, SSD_L), 1)
        dadt = _dot((c2 >= r2).astype(F32), dcs_mat, prec=HI)
        ddt = jnp.where(lane < NH, (dadt * a_row + ddtx_mat) * _sig(dt_raw + dtb_ref[...]), 0.0)
        ddt_ref[rs, :] = ddt
        dpre = dact[...] * _dsilu(pre)
        dcw = jnp.concatenate([_colsum(dpre * _roll(ext, 3 - k)[CH:]) for k in range(4)], axis=0)
        dext = jnp.concatenate([dpre, dnext], axis=0)
        dx_ref[rs, :] = (dext * w[3:4] + _roll(dext, n_ext - 1) * w[2:3] + _roll(dext, n_ext - 2) * w[1:2]
                         + _roll(dext, n_ext - 3) * w[0:1])[:SSD_L]
        acc = (dcw, _colsum(dpre), _colsum(ddt), _colsum(dadt * dtv) * a_row, ddk_row)
        return dpre[0:CH], ds_out, acc

    def body(x_ref, hx_ref, dt_ref, z_ref, yp_ref, dy_ref, sp_ref, cw_ref, cb_ref, dtb_ref, al_ref, dk_ref,
             dz_ref, dx_ref, ddt_ref, dcw_ref, dcb_ref, ddtb_ref, dal_ref, ddk_ref, ds_ref, dnext_ref, dact_ref):
        i = pl.program_id(0)
        acc_refs = (dcw_ref, dcb_ref, ddtb_ref, dal_ref, ddk_ref)

        @pl.when(i == 0)
        def _():
            ds_ref[...] = jnp.zeros_like(ds_ref)
            dnext_ref[...] = jnp.zeros_like(dnext_ref)
            for r in acc_refs:
                r[...] = jnp.zeros_like(r)

        refs = (x_ref, dt_ref, z_ref, yp_ref, dy_ref, sp_ref, cw_ref, cb_ref, dtb_ref, al_ref, dk_ref, dz_ref, dx_ref, ddt_ref,
                dact_ref)
        ds = ds_ref[...]
        dnext = dnext_ref[...]
        total = None
        for sub in reversed(range(SSD_SUB_BWD)):
            if sub == 0:
                halo = jnp.where(i == steps - 1, 0.0, hx_ref[...])
            else:
                halo = x_ref[sub * SSD_L - CH:sub * SSD_L, :]
            dnext, ds, acc = chunk(sub, halo, dnext, ds, refs)
            total = acc if total is None else tuple(a + b for a, b in zip(total, acc))
        ds_ref[...] = ds
        dnext_ref[...] = dnext
        for r, v in zip(acc_refs, total):
            r[...] += v

    rev = lambda i: steps - 1 - i
    blk = lambda n, cb=0: pl.BlockSpec((rows, n), lambda i: (rev(i), cb))
    row = lambda n: jax.ShapeDtypeStruct((1, n), F32)
    return pl.pallas_call(
        body, name="b_ssd", grid=(steps,),
        in_specs=[blk(768, 2), pl.BlockSpec((CH, 768), lambda i: (jnp.maximum(rev(i) * per - 1, 0), 2)),
                  blk(DTW), blk(GW, 4), blk(GW), blk(GW), pl.BlockSpec((SSD_SUB_BWD, NH, HP, NS), lambda i: (rev(i), 0, 0, 0)),
                  _full((4, 768)), _row(768), _row(DTW), _row(DTW), _row(DTW)],
        out_specs=[blk(GW), blk(768), blk(DTW), _full((4, 768)), _row(768), _row(DTW), _row(DTW), _row(DTW)],
        out_shape=[jax.ShapeDtypeStruct((t, GW), F32), jax.ShapeDtypeStruct((t, 768), F32), jax.ShapeDtypeStruct((t, DTW), F32),
                   jax.ShapeDtypeStruct((4, 768), F32), row(768), row(DTW), row(DTW), row(DTW)],
        scratch_shapes=[pltpu.VMEM((NH, HP, NS), F32), pltpu.VMEM((CH, 768), F32), pltpu.VMEM((SSD_SUB_BWD, SSD_L, 768), F32)],
        compiler_params=_cparams(1),
    )(proj, proj, dtp, proj, ypre, dyc, sprev, conv_w, conv_b, dt_bias, a_log, d_skip)


def _s5_block(t):
    return min(t, 256)


def _seg_t():
    r = lax.broadcasted_iota(jnp.int32, (64, 1024), 0)
    c = lax.broadcasted_iota(jnp.int32, (64, 1024), 1)
    return (c // 16 == r).astype(F32)


def _s5_prep_math(a_re, a_im, lstep, b_re, b_im):
    step = jnp.exp(lstep)
    ars = a_re * step
    ais = a_im * step
    mag = jnp.exp(ars)
    lr = mag * jnp.cos(ais)
    li = mag * jnp.sin(ais)
    den = a_re * a_re + a_im * a_im
    nr = lr - 1.0
    f_re = (nr * a_re + li * a_im) / den
    f_im = (li * a_re - nr * a_im) / den
    seg = _seg_t()
    fr = _dot(f_re, seg, prec=HI)
    fi = _dot(f_im, seg, prec=HI)
    return lr, li, fr * b_re - fi * b_im, fr * b_im + fi * b_re, ars, ais


def _s5_prep(a_re, a_im, lstep, b_re, b_im):
    def body(ar, ai, ls, br, bi, lr_o, li_o, bbr_o, bbi_o, ars_o, ais_o):
        outs = _s5_prep_math(ar[...], ai[...], ls[...], br[...], bi[...])
        for o, v in zip((lr_o, li_o, bbr_o, bbi_o, ars_o, ais_o), outs):
            o[...] = v

    s64 = jax.ShapeDtypeStruct((16, 64), F32)
    s1k = jax.ShapeDtypeStruct((16, 1024), F32)
    return pl.pallas_call(body, name="s5_prep", out_shape=[s64, s64, s1k, s1k, s64, s64])(a_re, a_im, lstep, b_re, b_im)


def _s5_prep_bwd(a_re, a_im, lstep, b_re, b_im, dlr, dli, dbbr, dbbi):
    def body(ar, ai, ls, br, bi, g0, g1, g2, g3, o0, o1, o2, o3, o4):
        f = lambda *a: _s5_prep_math(*a)[:4]
        _, vjp = jax.vjp(f, ar[...], ai[...], ls[...], br[...], bi[...])
        for o, v in zip((o0, o1, o2, o3, o4), vjp((g0[...], g1[...], g2[...], g3[...]))):
            o[...] = v

    s64 = jax.ShapeDtypeStruct((16, 64), F32)
    s1k = jax.ShapeDtypeStruct((16, 1024), F32)
    return pl.pallas_call(body, name="s5_prep_bwd", out_shape=[s64, s64, jax.ShapeDtypeStruct((16, 1), F32), s1k, s1k])(
        a_re, a_im, lstep, b_re, b_im, dlr, dli, dbbr, dbbi)


SUB = 8


def _s5_tables(ars, ais):
    def body(ar, ai, tr, ti):
        rr = lax.broadcasted_iota(jnp.int32, (8 * SUB, S5_P), 0)
        seg, r = rr // SUB, rr % SUB
        step = jnp.where((seg == 1) | (seg == 4), 1, jnp.where((seg == 2) | (seg == 5), 2, 4))
        n = jnp.where(seg == 0, r + 1, jnp.where(seg == 7, SUB - r, step))
        fwd_gap = jnp.where(seg <= 3, r - step, SUB - step - 1 - r)
        gap = jnp.where((seg == 0) | (seg == 7), 0, fwd_gap)
        nf = n.astype(F32)
        mag = jnp.where(gap >= 0, jnp.exp(nf * ar[...]), 0.0)
        tr[...] = mag * jnp.cos(nf * ai[...])
        ti[...] = mag * jnp.sin(nf * ai[...])

    sds = jax.ShapeDtypeStruct((8 * SUB, S5_P), F32)
    return pl.pallas_call(body, name="s5_tables", out_shape=[sds] * 2)(ars, ais)


def _s5_table(tb_r, tb_i, k):
    return tb_r[SUB * k:SUB * (k + 1), :], tb_i[SUB * k:SUB * (k + 1), :]


def _s5_scan(bu_r, bu_i, tb_r, tb_i, c_r, c_i, lb):
    nt = lb // SUB
    sr, si = bu_r.reshape(nt, SUB, S5_P), bu_i.reshape(nt, SUB, S5_P)
    for j, k in enumerate((1, 2, 4)):
        mr, mi = _s5_table(tb_r, tb_i, 1 + j)
        tr, ti = pltpu.roll(sr, k, axis=1), pltpu.roll(si, k, axis=1)
        sr, si = sr + mr * tr - mi * ti, si + mr * ti + mi * tr
    pr, pi = _s5_table(tb_r, tb_i, 0)
    out_r, out_i = [], []
    for j in range(nt):
        a_r = sr[j] + pr * c_r - pi * c_i
        a_i = si[j] + pr * c_i + pi * c_r
        out_r.append(a_r)
        out_i.append(a_i)
        c_r, c_i = a_r[SUB - 1:SUB], a_i[SUB - 1:SUB]
    return jnp.concatenate(out_r, axis=0), jnp.concatenate(out_i, axis=0)


def _s5_rscan(g_r, g_i, tb_r, tb_i, n_r, n_i, lb):
    nt = lb // SUB
    gr, gi = g_r.reshape(nt, SUB, S5_P), g_i.reshape(nt, SUB, S5_P)
    for j, k in enumerate((1, 2, 4)):
        mr, mi = _s5_table(tb_r, tb_i, 4 + j)
        tr, ti = pltpu.roll(gr, SUB - k, axis=1), pltpu.roll(gi, SUB - k, axis=1)
        gr, gi = gr + mr * tr + mi * ti, gi + mr * ti - mi * tr
    qr, qi = _s5_table(tb_r, tb_i, 7)
    out_r, out_i = [None] * nt, [None] * nt
    for j in reversed(range(nt)):
        a_r = gr[j] + qr * n_r + qi * n_i
        a_i = gi[j] + qr * n_i - qi * n_r
        out_r[j], out_i[j] = a_r, a_i
        n_r, n_i = a_r[0:1], a_i[0:1]
    return jnp.concatenate(out_r, axis=0), jnp.concatenate(out_i, axis=0)


def _s5_y(u, sr, si, cre, cim, dsk):
    return _bdot(sr, cre) + _bdot(si, cim) + dsk * u


def _f_s5(proj, bmat, cre, cim, p_r, p_i, dsk, glu_w, glu_b):
    t = proj.shape[0]
    lb = _s5_block(t)
    nb = t // lb

    def body(u_ref, bm_ref, cr_ref, ci_ref, pr_ref, pi_ref, dk_ref, gw_ref, gb_ref, y_ref, car_ref, s_ref, st_ref):
        @pl.when(pl.program_id(0) == 0)
        def _():
            st_ref[...] = jnp.zeros_like(st_ref)

        u = u_ref[...]
        bu = _bdot(u, bm_ref[...])
        c_r, c_i = st_ref[0:1, 0:S5_P], st_ref[0:1, S5_P:]
        car_ref[0] = st_ref[0:1, :]
        sr, si = _s5_scan(bu[:, :S5_P], bu[:, S5_P:], pr_ref, pi_ref, c_r, c_i, lb)
        st_ref[0:1, 0:S5_P] = sr[lb - 1:lb]
        st_ref[0:1, S5_P:] = si[lb - 1:lb]
        sr_b, si_b = sr.astype(BF16), si.astype(BF16)
        s_ref[:, 0:S5_P] = sr_b
        s_ref[:, S5_P:] = si_b
        gel = _gelu(_s5_y(u, sr_b, si_b, cr_ref[...], ci_ref[...], dk_ref[...]))
        y_ref[...] = gel * _sig(_bdot(gel, gw_ref[...]) + gb_ref[...])

    return pl.pallas_call(
        body, name="f_s5", grid=(nb,),
        in_specs=[pl.BlockSpec((lb, GW), lambda i: (i, 5)),
                  _full((GW, 2 * S5_P)), _full((S5_P, GW)), _full((S5_P, GW)), _full((8 * SUB, S5_P)), _full((8 * SUB, S5_P)),
                  _row(GW), _full((GW, GW)), _row(GW)],
        out_specs=[pl.BlockSpec((lb, GW), lambda i: (i, 0)), pl.BlockSpec((1, 1, 2 * S5_P), lambda i: (i, 0, 0)),
                   pl.BlockSpec((lb, 2 * S5_P), lambda i: (i, 0))],
        out_shape=[jax.ShapeDtypeStruct((t, GW), F32), jax.ShapeDtypeStruct((nb, 1, 2 * S5_P), F32),
                   jax.ShapeDtypeStruct((t, 2 * S5_P), BF16)],
        scratch_shapes=[pltpu.VMEM((8, 2 * S5_P), F32)], compiler_params=_cparams(1),
    )(proj, bmat, cre, cim, p_r, p_i, dsk, glu_w, glu_b)


def _b_s5(proj, dyd, carries, states, bmat, cre, cim, p_r, p_i, dsk, glu_w, glu_b):
    t = proj.shape[0]
    lb = _s5_block(t)
    nb = t // lb

    def body(u_ref, dy_ref, car_ref, s_ref, bm_ref, cr_ref, ci_ref, pr_ref, pi_ref, dk_ref, gw_ref, gb_ref,
             du_ref, dbm_ref, dcr_ref, dci_ref, dlam_ref, ddk_ref, dgw_ref, dgb_ref, gc_ref):
        @pl.when(pl.program_id(0) == 0)
        def _():
            gc_ref[...] = jnp.zeros_like(gc_ref)
            for r in (dbm_ref, dcr_ref, dci_ref, dlam_ref, ddk_ref, dgw_ref, dgb_ref):
                r[...] = jnp.zeros_like(r)

        u = u_ref[...]
        bm = bm_ref[...]
        u_b = u.astype(BF16)
        c_r, c_i = car_ref[0, 0:1, 0:S5_P], car_ref[0, 0:1, S5_P:]
        cre_v, cim_v, dk, gw = cr_ref[...], ci_ref[...], dk_ref[...], gw_ref[...]
        sr_b, si_b = s_ref[:, 0:S5_P], s_ref[:, S5_P:]
        sr, si = sr_b.astype(F32), si_b.astype(F32)
        y = _dot(sr_b, cre_v) + _dot(si_b, cim_v) + dk * u
        gel = _gelu(y)
        gel_b = gel.astype(BF16)
        gate = _sig(_dot(gel_b, gw) + gb_ref[...])
        dout = dy_ref[...]
        t1 = dout * gel * gate * (1.0 - gate)
        t1_b = t1.astype(BF16)
        dgw_ref[...] += _dot(gel_b, t1_b, TN)
        dgb_ref[...] += _colsum(t1)
        dyv = (dout * gate + _dot(t1_b, gw, NT)) * _dgelu(y)
        dyv_b = dyv.astype(BF16)
        ddk_ref[...] += _colsum(dyv * u)
        dcr_ref[...] += _dot(sr_b, dyv_b, TN)
        dci_ref[...] += _dot(si_b, dyv_b, TN)
        gr = _dot(dyv_b, cre_v, NT)
        gi = _dot(dyv_b, cim_v, NT)
        row = lax.broadcasted_iota(jnp.int32, (lb, S5_P), 0)
        n_r, n_i = gc_ref[0:1, 0:S5_P], gc_ref[0:1, S5_P:]
        gr, gi = _s5_rscan(gr, gi, pr_ref, pi_ref, n_r, n_i, lb)
        gc_ref[0:1, 0:S5_P] = gr[0:1]
        gc_ref[0:1, S5_P:] = gi[0:1]
        gcat = jnp.concatenate([gr, gi], axis=1).astype(BF16)
        dbm_ref[...] += _dot(u_b, gcat, TN)
        du_ref[...] = dyv * dk + _dot(gcat, bm, NT)
        spr = jnp.where(row >= 1, _roll(sr, 1), c_r)
        spi = jnp.where(row >= 1, _roll(si, 1), c_i)
        dlam_ref[0:1, :] += _colsum(gr * spr + gi * spi)
        dlam_ref[1:2, :] += _colsum(gi * spr - gr * spi)

    rev = lambda i: nb - 1 - i
    return pl.pallas_call(
        body, name="b_s5", grid=(nb,),
        in_specs=[pl.BlockSpec((lb, GW), lambda i: (rev(i), 5)), pl.BlockSpec((lb, GW), lambda i: (rev(i), 0)),
                  pl.BlockSpec((1, 1, 2 * S5_P), lambda i: (rev(i), 0, 0)), pl.BlockSpec((lb, 2 * S5_P), lambda i: (rev(i), 0)),
                  _full((GW, 2 * S5_P)), _full((S5_P, GW)), _full((S5_P, GW)), _full((8 * SUB, S5_P)), _full((8 * SUB, S5_P)),
                  _row(GW), _full((GW, GW)), _row(GW)],
        out_specs=[pl.BlockSpec((lb, GW), lambda i: (rev(i), 0)), _full((GW, 2 * S5_P)), _full((S5_P, GW)), _full((S5_P, GW)),
                   _full((2, S5_P)), _row(GW), _full((GW, GW)), _row(GW)],
        out_shape=[jax.ShapeDtypeStruct((t, GW), F32), jax.ShapeDtypeStruct((GW, 2 * S5_P), F32),
                   jax.ShapeDtypeStruct((S5_P, GW), F32), jax.ShapeDtypeStruct((S5_P, GW), F32),
                   jax.ShapeDtypeStruct((2, S5_P), F32), jax.ShapeDtypeStruct((1, GW), F32),
                   jax.ShapeDtypeStruct((GW, GW), F32), jax.ShapeDtypeStruct((1, GW), F32)],
        scratch_shapes=[pltpu.VMEM((8, 2 * S5_P), F32)], compiler_params=_cparams(1),
    )(proj, dyd, carries, states, bmat, cre, cim, p_r, p_i, dsk, glu_w, glu_b)


def _group_norm(ys, bw):
    outs, stats = [], []
    for g, y in enumerate(ys):
        r, n = _rms(y)
        stats.append((r, n))
        outs.append(n * bw[:, GW * g:GW * (g + 1)])
    return jnp.concatenate(outs, axis=1), stats


def _f_out(ya, yb, yc, yd, bw, wts, l, h, g1):
    t = h.shape[0]
    tb = _tblock(t)

    def body(a_ref, b_ref, c_ref, d_ref, bw_ref, w_ref, h_ref, g_ref, h2_ref, o_ref, cat_ref):
        cat, _ = _group_norm([a_ref[...], b_ref[...], c_ref[...], d_ref[...]], bw_ref[...])
        catb = cat.astype(BF16)
        cat_ref[...] = catb
        o = _dot(catb, w_ref[0].reshape(D, D))
        o_ref[...] = o.astype(BF16)
        h2_ref[...] = h_ref[...] + g_ref[...] * o

    yblk = pl.BlockSpec((tb, GW), lambda i: (i, 0))
    blk = pl.BlockSpec((tb, D), lambda i: (i, 0))
    return pl.pallas_call(
        body, name="f_out", grid=(t // tb,), in_specs=[yblk] * 4 + [_row(D), _wout_spec(l), blk, _row(D)],
        out_specs=[blk, blk, blk],
        out_shape=[jax.ShapeDtypeStruct((t, D), F32), jax.ShapeDtypeStruct((t, D), BF16), jax.ShapeDtypeStruct((t, D), BF16)],
        compiler_params=_cparams(1),
    )(ya, yb, yc, yd, bw, wts, h, g1)


def _b_out(dv, h2, dh3, m, nw2, sc2, ya, yb, yc, yd, bw, wts, l, g1):
    t = dv.shape[0]
    tb = _tblock(t)

    def body(dv_ref, h2_ref, dh3_ref, m_ref, nw_ref, sc_ref, a_ref, b_ref, c_ref, d_ref, bw_ref, w_ref, g_ref,
             dh_ref, dsc_ref, dsh_ref, dnw_ref, dg_ref, da_ref, db_ref, dc_ref, dd_ref, do_ref, dbw_ref):
        @pl.when(pl.program_id(0) == 0)
        def _():
            for r in (dsc_ref, dsh_ref, dnw_ref, dg_ref, dbw_ref):
                r[...] = jnp.zeros_like(r)

        _norm_bwd_step(dv_ref[...], h2_ref[...], dh3_ref[...], m_ref[...], nw_ref[...], sc_ref[...],
                       dh_ref, dsc_ref, dsh_ref, dnw_ref, dg_ref)
        do = (dh_ref[...] * g_ref[...]).astype(BF16)
        do_ref[...] = do
        dcat = _dot(do, w_ref[0].reshape(D, D), NT)
        bw_v = bw_ref[...]
        for g, (y_ref, dy_ref) in enumerate(((a_ref, da_ref), (b_ref, db_ref), (c_ref, dc_ref), (d_ref, dd_ref))):
            r, n = _rms(y_ref[...])
            dc = dcat[:, GW * g:GW * (g + 1)]
            dbw_ref[:, GW * g:GW * (g + 1)] += _colsum(dc * n)
            dy_ref[...] = _rms_bwd(r, n, dc * bw_v[:, GW * g:GW * (g + 1)])

    yblk = pl.BlockSpec((tb, GW), lambda i: (i, 0))
    blk = pl.BlockSpec((tb, D), lambda i: (i, 0))
    ysd = jax.ShapeDtypeStruct((t, GW), F32)
    row = jax.ShapeDtypeStruct((1, D), F32)
    return pl.pallas_call(
        body, name="b_out", grid=(t // tb,),
        in_specs=[blk] * 4 + [_row(D), _row(D)] + [yblk] * 4 + [_row(D), _wout_spec(l), _row(D)],
        out_specs=[blk, _row(D), _row(D), _row(D), _row(D)] + [yblk] * 4 + [blk, _row(D)],
        out_shape=[jax.ShapeDtypeStruct((t, D), F32), row, row, row, row] + [ysd] * 4 + [jax.ShapeDtypeStruct((t, D), BF16), row],
        compiler_params=_cparams(1),
    )(dv, h2, dh3, m, nw2, sc2, ya, yb, yc, yd, bw, wts, g1)


HB = 512
MLP_ROWS = 1024


ROW_W1, ROW_W2, ROW_WOUT, ROW_WIN = 0, D, D + HID // 4, D + HID // 4 + D // 4
PACK_ROWS = ROW_WIN + WIN_ROWS


def _w1_spec(l):
    per = HID // 4 // HB
    return pl.BlockSpec((1, 1, D, HB), lambda i, k: (l, k // per, ROW_W1 // D, k % per))


def _w2_spec(l):
    per = HID // 4 // HB
    return pl.BlockSpec((1, 1, HB, D), lambda i, k: (l, k // per, ROW_W2 // HB + k % per, 0))


def _wout_spec(l):
    return pl.BlockSpec((1, 4, D // 4, D), lambda i: (l, 0, ROW_WOUT // (D // 4), 0))


def _f_mlp(h2, nw, sc, sh, g2, wts, l):
    t = h2.shape[0]
    tb = _tblock(t, MLP_ROWS)
    nk = HID // HB

    def body(h_ref, nw_ref, sc_ref, sh_ref, g_ref, w1_ref, w2_ref, h3_ref, m_ref, a_ref, v_ref, acc_ref):
        k = pl.program_id(1)

        @pl.when(k == 0)
        def _():
            _, n = _rms(h_ref[...])
            v_ref[...] = ((n * nw_ref[...]) * (1.0 + sc_ref[...]) + sh_ref[...]).astype(BF16)
            acc_ref[...] = jnp.zeros_like(acc_ref)

        a = _dot(v_ref[...], w1_ref[0, 0])
        a_ref[...] = a.astype(BF16)
        ra = jnp.maximum(a, 0.0)
        acc_ref[...] += _dot((ra * ra).astype(BF16), w2_ref[0, 0])

        @pl.when(k == nk - 1)
        def _():
            m = acc_ref[...]
            m_ref[...] = m.astype(BF16)
            h3_ref[...] = h_ref[...] + g_ref[...] * m

    blk = pl.BlockSpec((tb, D), lambda i, k: (i, 0))
    return pl.pallas_call(
        body, name="f_mlp", grid=(t // tb, nk),
        in_specs=[blk, _row(D), _row(D), _row(D), _row(D), _w1_spec(l), _w2_spec(l)],
        out_specs=[blk, blk, pl.BlockSpec((tb, HB), lambda i, k: (i, k)), blk],
        out_shape=[jax.ShapeDtypeStruct((t, D), F32), jax.ShapeDtypeStruct((t, D), BF16), jax.ShapeDtypeStruct((t, HID), BF16),
                   jax.ShapeDtypeStruct((t, D), BF16)],
        scratch_shapes=[pltpu.VMEM((tb, D), F32)], compiler_params=_cparams(2),
    )(h2, nw, sc, sh, g2, wts, wts)


def _b_mlp(dh3, a, g2, wts, l):
    t = dh3.shape[0]
    tb = _tblock(t, MLP_ROWS)
    nk = HID // HB

    def body(dh_ref, a_ref, g_ref, w1_ref, w2_ref, dv_ref, da_ref, act_ref, dm_ref):
        k = pl.program_id(1)
        dm = (dh_ref[...] * g_ref[...]).astype(BF16)

        @pl.when(k == 0)
        def _():
            dm_ref[...] = dm
            dv_ref[...] = jnp.zeros_like(dv_ref)

        ra = jnp.maximum(a_ref[...].astype(F32), 0.0)
        act_ref[...] = (ra * ra).astype(BF16)
        da = (_dot(dm, w2_ref[0, 0], NT) * (2.0 * ra)).astype(BF16)
        da_ref[...] = da
        dv_ref[...] += _dot(da, w1_ref[0, 0], NT)

    blk = pl.BlockSpec((tb, D), lambda i, k: (i, 0))
    hblk = pl.BlockSpec((tb, HB), lambda i, k: (i, k))
    return pl.pallas_call(
        body, name="b_mlp", grid=(t // tb, nk),
        in_specs=[blk, hblk, _row(D), _w1_spec(l), _w2_spec(l)],
        out_specs=[blk, hblk, hblk, blk],
        out_shape=[jax.ShapeDtypeStruct((t, D), F32), jax.ShapeDtypeStruct((t, HID), BF16), jax.ShapeDtypeStruct((t, HID), BF16),
                   jax.ShapeDtypeStruct((t, D), BF16)],
        compiler_params=_cparams(2),
    )(dh3, a, g2, wts, wts)


def _b_final(h, tgt, fw):
    t = h.shape[0]
    tb = _tblock(t)

    def body(h_ref, t_ref, w_ref, dh_ref, loss_ref, dfw_ref):
        @pl.when(pl.program_id(0) == 0)
        def _():
            loss_ref[...] = jnp.zeros_like(loss_ref)
            dfw_ref[...] = jnp.zeros_like(dfw_ref)

        r, n = _rms(h_ref[...])
        wv = w_ref[...]
        err = n * wv - t_ref[...]
        loss_ref[...] += jnp.sum(err * err, keepdims=True) * (0.5 / D)
        dy = err * (1.0 / D)
        dfw_ref[...] += _colsum(dy * n)
        dh_ref[...] = _rms_bwd(r, n, dy * wv)

    blk = pl.BlockSpec((tb, D), lambda i: (i, 0))
    return pl.pallas_call(
        body, name="b_final", grid=(t // tb,), in_specs=[blk, blk, _row(D)], out_specs=[blk, _row(1), _row(D)],
        out_shape=[jax.ShapeDtypeStruct((t, D), F32), jax.ShapeDtypeStruct((1, 1), F32), jax.ShapeDtypeStruct((1, D), F32)],
        compiler_params=_cparams(1),
    )(h, tgt, fw)


def _eye(n):
    return jnp.eye(n, dtype=F32)


def _pool_embed(pool_w):
    return jnp.einsum('gcd,gk->gckd', pool_w, _eye(4)).reshape(GW, GW)


def _pool_extract(m):
    return jnp.einsum('gcgd->gcd', m.reshape(4, 64, 4, 64))


def _bmat_embed(bb):
    return jnp.einsum('gph,gk->ghkp', bb, _eye(16)).reshape(GW, S5_P)


def _bmat_extract(m):
    return jnp.einsum('ghgp->gph', m.reshape(16, 16, 16, 64))


def _cmat_embed(cc):
    return jnp.einsum('ghp,gk->kpgh', cc, _eye(16)).reshape(S5_P, GW)


def _cmat_extract(m):
    return jnp.einsum('gpgh->ghp', m.reshape(16, 64, 16, 16))


def _pad_lanes(v, n=DTW):
    return jnp.pad(v.reshape(1, -1), ((0, 0), (0, n - v.shape[-1])))


def _w_in_layout(w_in_t):
    w_main = jnp.concatenate([w_in_t[:1280], w_in_t[2052:2308], w_in_t[1280:2048]], axis=0)
    return w_main, jnp.pad(w_in_t[2048:2052], ((0, DTW - 4), (0, 0)))


def _s5_params(p, l):
    raw = (p['s5_a_re'][l], p['s5_a_im'][l], p['s5_log_step'][l].reshape(16, 1),
           p['s5_b_re'][l].reshape(16, 1024), p['s5_b_im'][l].reshape(16, 1024))
    _, _, bbr, bbi, ars, ais = _s5_prep(*raw)
    bmat = jnp.concatenate([_bmat_embed(bbr.reshape(16, 64, 16)), _bmat_embed(bbi.reshape(16, 64, 16))], axis=1).astype(BF16)
    return {'s5_raw': raw, 'bmat': bmat, 'tables': _s5_tables(ars.reshape(1, S5_P), ais.reshape(1, S5_P))}


def _layer_params(p, l, mod, w_in, rest):
    q = {'rest': rest, 'l': l}
    q['mod'] = [mod[k:k + 1] for k in range(6)]
    q['nw1'] = p['norm_mix_w'][l:l + 1]
    q['nw2'] = p['norm_mlp_w'][l:l + 1]
    q['w_main'], q['w_dt'] = _w_in_layout(w_in)
    q['pool_mat'] = _pool_embed(p['pool_w'][l]).astype(BF16)
    q['pool_scale'] = p['pool_scale'][l:l + 1]
    q['sconv_w'] = p['sconv_w'][l]
    q['conv_w'] = p['ssd_conv_w'][l]
    q['conv_b'] = p['ssd_conv_b'][l:l + 1]
    q['dt_bias'] = _pad_lanes(p['ssd_dt_bias'][l])
    q['a_log'] = _pad_lanes(p['ssd_a_log'][l])
    q['ssd_d'] = _pad_lanes(p['ssd_d'][l])
    q.update(p['s5_pre'][l] if 's5_pre' in p else _s5_params(p, l))
    q['cre'] = _cmat_embed(p['s5_c_re'][l]).astype(BF16)
    q['cim'] = (-_cmat_embed(p['s5_c_im'][l])).astype(BF16)
    q['s5_d'] = p['s5_d'][l:l + 1]
    q['glu_w'] = p['s5_glu_w'][l].astype(BF16)
    q['glu_b'] = p['s5_glu_b'][l:l + 1]
    q['bw'] = p['branch_norm_w'][l:l + 1]
    return q


def _layer_fwd(h, q):
    sh1, sc1, g1, sh2, sc2, g2 = q['mod']
    t = h.shape[0]
    s = {'h': h}
    s['proj'], s['dtp'], s['u'] = _f_in(h, q['nw1'], sc1, sh1, q['w_main'], q['w_dt'])
    s['ya'], s['yb'] = _f_ab(s['proj'], q['pool_mat'], q['pool_scale'], q['sconv_w'])
    s['yc'], s['ypre'], s['sprev'] = _f_ssd(s['proj'], s['dtp'], q['conv_w'], q['conv_b'], q['dt_bias'], q['a_log'], q['ssd_d'])
    s['bmat'], s['tables'] = q['bmat'], q['tables']
    s['yd'], s['carries'], s['states'] = _f_s5(s['proj'], s['bmat'], q['cre'], q['cim'], s['tables'][0], s['tables'][1],
                                  q['s5_d'], q['glu_w'], q['glu_b'])
    q['wts'] = q['rest']((s['ya'], s['yc'], s['yd']))
    s['h2'], s['o'], s['cat'] = _f_out(s['ya'], s['yb'], s['yc'], s['yd'], q['bw'], q['wts'], q['l'], h, g1)
    h3, s['m'], s['a'], s['v'] = _f_mlp(s['h2'], q['nw2'], sc2, sh2, g2, q['wts'], q['l'])
    return h3, s


STACKED = {'mlp_w1': (2, 4, D, HID // 4), 'mlp_w2': (2, HID, D), 'w_out': (2, D, D)}


def _layer_bwd(dh3, q, s, l, stacked, early=None):
    sh1, sc1, g1, sh2, sc2, g2 = q['mod']
    g = {}
    dv, da, act, dm = _b_mlp(dh3, s['a'], g2, q['wts'], l)
    g['mlp_w1'] = _tn_matmul(s['v'], da, "dw1", col_major=True, into=stacked['mlp_w1'], layer=l)
    g['mlp_w2'] = _tn_matmul(act, dm, "dw2", into=stacked['mlp_w2'], layer=l)
    dh2, dsc2, dsh2, dnw2, dg2, dya, dyb, dyc, dyd, do, dbw = _b_out(
        dv, s['h2'], dh3, s['m'], q['nw2'], sc2, s['ya'], s['yb'], s['yc'], s['yd'], q['bw'], q['wts'], l, g1)
    g['w_out'] = _tn_matmul(s['cat'], do, "dwout", into=stacked['w_out'], layer=l)
    g['branch_norm_w'] = dbw[0]
    if early is not None:
        zero = early(g)[0, 0]
        q = dict(q, pool_scale=q['pool_scale'] + zero, conv_b=q['conv_b'] + zero, s5_d=q['s5_d'] + zero)
    dab, dpm, dps, dsw = _b_ab(s['proj'], dya, dyb, q['pool_mat'], q['pool_scale'], q['sconv_w'])
    g['pool_w'] = _pool_extract(dpm)
    g['pool_scale'] = dps[0]
    g['sconv_w'] = dsw
    dz, dxbc, ddt, dcw, dcb, ddtb, dal, ddk = _b_ssd(s['proj'], s['dtp'], s['ypre'], dyc, s['sprev'], q['conv_w'],
                                                     q['conv_b'], q['dt_bias'], q['a_log'], q['ssd_d'])
    g['ssd_conv_w'] = dcw
    g['ssd_conv_b'] = dcb[0]
    g['ssd_dt_bias'] = ddtb[0, :4]
    g['ssd_a_log'] = dal[0, :4]
    g['ssd_d'] = ddk[0, :4]
    tb = s['tables']
    ds5, dbmat, dcre, dcim, dlam, dd5, dgw, dgb = _b_s5(s['proj'], dyd, s['carries'], s['states'], s['bmat'], q['cre'], q['cim'],
                                                        tb[0], tb[1], q['s5_d'], q['glu_w'], q['glu_b'])
    g['s5_c_re'] = _cmat_extract(dcre)
    g['s5_c_im'] = -_cmat_extract(dcim)
    g['s5_d'] = dd5[0]
    g['s5_glu_w'] = dgw
    g['s5_glu_b'] = dgb[0]
    dbbr = _bmat_extract(dbmat[:, :S5_P]).reshape(16, 1024)
    dbbi = _bmat_extract(dbmat[:, S5_P:]).reshape(16, 1024)
    dar, dai, dls, dbr, dbi = _s5_prep_bwd(*q['s5_raw'], dlam[0].reshape(16, 64), dlam[1].reshape(16, 64), dbbr, dbbi)
    g['s5_a_re'], g['s5_a_im'], g['s5_log_step'] = dar, dai, dls[:, 0]
    g['s5_b_re'], g['s5_b_im'] = dbr, dbi
    dh, dsc1, dsh1, dnw1, dg1 = _b_in(dab, dz, dxbc, ds5, ddt, q['w_main'], q['w_dt'], s['h'], dh2, s['o'], q['nw1'], sc1)
    u = s['u']
    head = jnp.concatenate([_tn_matmul(dab, u, "dwin_ab"), _tn_matmul(dz, u, "dwin_z"), _tn_matmul(dxbc, u, "dwin_xbc"),
                            _tn_matmul(ddt, u, "dwin_dt")[:8]], axis=0)
    full = lax.dynamic_update_slice(jnp.zeros((2308, D), F32), head, (0, 0))
    g['w_in'] = lax.dynamic_update_slice(full, _tn_matmul(ds5, u, "dwin_s5"), (2052, 0))
    g['norm_mix_w'] = dnw1[0]
    g['norm_mlp_w'] = dnw2[0]
    dmod = jnp.concatenate([dsh1, dsc1, dg1, dsh2, dsc2, dg2], axis=1)
    return dh, g, dmod


def _local_step(x, tgt, p, mod, w_in_of, rest_of, early=None):
    h = x
    qs, saved = [], []
    for l in range(2):
        qs.append(_layer_params(p, l, mod[l], w_in_of(l), functools.partial(rest_of, l)))
        h, s = _layer_fwd(h, qs[l])
        saved.append(s)
    dh, loss, dfw = _b_final(h, tgt, p['final_norm_w'].reshape(1, D))
    grads = [None, None]
    dmods = [None, None]
    dh, grads[1], dmods[1] = _layer_bwd(dh, qs[1], saved[1], 1, {k: lax.empty(shp, F32) for k, shp in STACKED.items()})
    dh, grads[0], dmods[0] = _layer_bwd(dh, qs[0], saved[0], 0, grads[1], early)
    out = {k: jnp.stack([grads[0][k], grads[1][k]]) for k in grads[0] if k not in STACKED}
    if early is None:
        out.update({k: grads[0][k] for k in STACKED})
    out['final_norm_w'] = dfw[0]
    return loss, dh, out, jnp.concatenate(dmods, axis=0)


def _shard_of(a, axis, k):
    n = a.shape[axis] // 4
    return lax.dynamic_slice_in_dim(a, k * n, n, axis)


def kernel(x, c, norm_mix_w, norm_mlp_w, ada_w, ada_b, w_in, pool_w, pool_scale, sconv_w, ssd_conv_w, ssd_conv_b, ssd_dt_bias, ssd_a_log, ssd_d, s5_a_re, s5_a_im, s5_log_step, s5_b_re, s5_b_im, s5_c_re, s5_c_im, s5_d, s5_glu_w, s5_glu_b, branch_norm_w, w_out, mlp_w1, mlp_w2, final_norm_w, loss_target, m_norm_mix_w, m_norm_mlp_w, m_ada_w, m_ada_b, m_w_in, m_pool_w, m_pool_scale, m_sconv_w, m_ssd_conv_w, m_ssd_conv_b, m_ssd_dt_bias, m_ssd_a_log, m_ssd_d, m_s5_a_re, m_s5_a_im, m_s5_log_step, m_s5_b_re, m_s5_b_im, m_s5_c_re, m_s5_c_im, m_s5_d, m_s5_glu_w, m_s5_glu_b, m_branch_norm_w, m_w_out, m_mlp_w1, m_mlp_w2, m_final_norm_w, v_norm_mix_w, v_norm_mlp_w, v_ada_w, v_ada_b, v_w_in, v_pool_w, v_pool_scale, v_sconv_w, v_ssd_conv_w, v_ssd_conv_b, v_ssd_dt_bias, v_ssd_a_log, v_ssd_d, v_s5_a_re, v_s5_a_im, v_s5_log_step, v_s5_b_re, v_s5_b_im, v_s5_c_re, v_s5_c_im, v_s5_d, v_s5_glu_w, v_s5_glu_b, v_branch_norm_w, v_w_out, v_mlp_w1, v_mlp_w2, v_final_norm_w):
    loc = locals()
    w = {n: loc[n] for n in WEIGHTS}
    mom = {n: loc['m_' + n] for n in WEIGHTS}
    var = {n: loc['v_' + n] for n in WEIGHTS}
    ix, iy, ic = lax.axis_index("x"), lax.axis_index("y"), lax.axis_index("c")
    chip = 2 * ix + iy
    dev = 4 * ix + 2 * iy + ic

    mine_of = lambda a: lax.dynamic_index_in_dim(a.astype(BF16), ic, axis=0, keepdims=False)
    pad_in = lambda a: jnp.pad(a.T, ((0, WIN_ROWS - 577), (0, 0)))
    shard = jnp.concatenate([mine_of(w['mlp_w1']), mine_of(w['mlp_w2']), mine_of(w['w_out']), pad_in(mine_of(w['w_in']))], axis=0)

    (c_all,) = _exchange([c], EVERYONE, False, "ag_cond", stage=True)
    c_all = c_all.reshape(8, D)
    small_sh = _exchange([w[n] for n in SMALL_SHARDED], CHIPS, False, "ag_small")
    p = {n: w[n] for n in WEIGHTS if n not in BIG}
    for n, g in zip(SMALL_SHARDED, small_sh):
        ax = SMALL_SHARDED[n]
        p[n] = jnp.concatenate([g[k] for k in range(4)], axis=ax)

    def w_in_full(sh):
        return sh[:, :577].reshape(4 * 577, D)

    big = {}

    def fetch(after):
        if not big:
            (mine,), (got,) = _split_wait(sems, shard_thru, land, after, False, "ag_big_wait", per_core=True)
            got = lax.dynamic_update_slice(got, mine[None, None], (ic, chip, 0, 0))
            (both,) = _pair_swap([got.reshape(2, -1, D)], False, "swap_big", fill=True)
            big['both'] = both.reshape(got.shape)
        return big['both']

    def w_in_of(l):
        return w_in_full(w_in0) if l == 0 else w_in_full(fetch(None)[1, :, ROW_WIN:])

    def rest_of(l, after):
        return fetch(after)

    ada_b_sh = _shard_of(w['ada_b'], 1, chip).reshape(2, 1, 6 * D // 4)
    mod_sh = _ada_fwd(c_all, w['ada_w'], ada_b_sh)
    (mod_all,) = _exchange([mod_sh], CHIPS, False, "ag_mod", stage=True)
    mine = lax.dynamic_index_in_dim(mod_all, dev, axis=2, keepdims=False)
    win_sems, win_src, win_land, win_token = _split_start([pad_in(w['w_in'][0].astype(BF16))], [mod_all] + small_sh, False,
                                                          "ag_win0_start")
    sems, shard_thru, land, token = _split_start([shard], [win_token], False, "ag_big_start", per_core=True)
    p['s5_pre'] = [_s5_params(p, l) for l in range(2)]
    hide = [a for pre in p['s5_pre'] for a in (pre['bmat'],) + tuple(pre['tables'])]
    (mine0,), (w_in0,) = _split_wait(win_sems, win_src, win_land, hide, False, "ag_win0_wait")
    w_in0 = lax.dynamic_update_slice(w_in0, mine0[None], (chip, 0, 0))
    mod = jnp.transpose(mine, (1, 0, 2)).reshape(2, 6, D) + token[0, 0]

    layer = ic.astype(jnp.int32).reshape(1)
    flight = {}

    def early(g0):
        gws = [g0['w_out'].reshape(2, 4, 256, D), g0['mlp_w1'], g0['mlp_w2'].reshape(2, 4, 1024, D)]
        got = _pair_swap([a.reshape(2, -1, D) for a in gws], True, "swap_grad", narrow=True)
        pair = [_pair_sum(a, b.reshape(a.shape[1:]), layer, "pair_sum%d" % (k + 1), BF16) for k, (a, b) in enumerate(zip(gws, got))]
        flight['sems'], flight['srcs'], flight['lands'], token = _split_start(pair, [], True, "rs_start")
        return token

    loss, grad_x, g, dmod = _local_step(x[0], loss_target[0], p, mod, w_in_of, rest_of, early)

    (dmod_all,) = _exchange([dmod], EVERYONE, False, "ag_dmod", stage=True)
    dmod_all = jnp.transpose(dmod_all, (1, 0, 2))

    gw_in = jnp.pad(g['w_in'].reshape(2, 4, 577, D), ((0, 0), (0, 0), (0, WIN_ROWS - 577), (0, 0)))
    (got_in,) = _pair_swap([gw_in.reshape(2, -1, D)], True, "swap_grad_in", narrow=True)
    pair_in = _pair_sum(gw_in, got_in.reshape(gw_in.shape[1:]), layer, "pair_sum0", BF16)
    in_sems, in_srcs, in_lands, in_token = _split_start([pair_in], [dmod_all], True, "rs_in_start")

    def chip_sum(land, mine, name):
        own = lax.dynamic_index_in_dim(mine, chip, axis=0, keepdims=True)
        return _sum_lead(lax.dynamic_update_slice(land, own, (chip, 0, 0)), name, F32)

    sent, lands = _split_wait(flight['sems'], flight['srcs'], flight['lands'], [grad_x, in_token], True, "rs_wait")
    quad = [chip_sum(land, mine, "rs_chip_sum%d" % (k + 1)) for k, (land, mine) in enumerate(zip(lands, sent))]
    g_ada_w, g_ada_b = _ada_bwd(c_all, _shard_of(dmod_all, 2, chip), dmod_all)
    adam_ada_w = _adamw(w['ada_w'], g_ada_w, mom['ada_w'], var['ada_w'], "adamw_ada_w")
    (sent_in,), (land_in,) = _split_wait(in_sems, in_srcs, in_lands, quad + [adam_ada_w[0]], True, "rs_in_wait")
    quad = [chip_sum(land_in, sent_in, "rs_chip_sum0")] + quad
    halves = [lax.dynamic_update_slice(lax.empty((2,) + a.shape, F32), a[None], (ic, 0, 0)) for a in quad]
    both = _pair_swap(halves, False, "swap_red", fill=True)
    both[0] = jnp.transpose(both[0][:, :577], (0, 2, 1))
    red = dict(zip(('w_in', 'w_out', 'mlp_w1', 'mlp_w2'), both))
    red['ada_w'] = g_ada_w

    small_names = [n for n in WEIGHTS if n not in BIG and n != 'ada_b']
    pair_parts = _exchange([g[n] for n in small_names] + [loss], SIBLING, False, "ag_smallpair", stage=True)
    chip_parts = _exchange(_sum_many(pair_parts, "smallpair_sum"), CHIPS, False, "ag_smallgrad", stage=True)
    summed = _sum_many(chip_parts, "smallgrad_sum")
    for n, a in zip(small_names, summed[:-1]):
        a = a.reshape(w[n].shape) if n in ('s5_b_re', 's5_b_im') else a
        red[n] = _shard_of(a, SMALL_SHARDED[n], chip) if n in SMALL_SHARDED else a
    red['ada_b'] = g_ada_b
    loss_out = summed[-1].reshape(())

    delta, new_m, new_v = {}, {}, {}
    delta['ada_w'], new_m['ada_w'], new_v['ada_w'] = adam_ada_w
    for n in BIG[1:]:
        delta[n], new_m[n], new_v[n] = _adamw(w[n], red[n], mom[n], var[n], "adamw_" + n)
    rest = [n for n in WEIGHTS if n not in BIG]
    lanes = lambda n, a: a.reshape(2, 16, 1024) if n in ('s5_b_re', 's5_b_im') else a
    outs = _adamw_many(*[[lanes(n, src[n]) for n in rest] for src in (w, red, mom, var)], "adamw_small")
    for k, n in enumerate(rest):
        delta[n], new_m[n], new_v[n] = (outs[3 * k + j].reshape(w[n].shape) for j in range(3))

    return (loss_out, grad_x[None], *[red[n] for n in WEIGHTS], *[delta[n] for n in WEIGHTS],
            *[new_m[n] for n in WEIGHTS], *[new_v[n] for n in WEIGHTS])
```

```python
import functools
import math

import jax
import jax.numpy as jnp
from jax import lax
from jax.experimental import pallas as pl
from jax.experimental.pallas import tpu as pltpu

F32 = jnp.float32
BF16 = jnp.bfloat16
HI = lax.Precision.HIGHEST

D = 1024
GW = 256
HID = 4096
EPS = 1e-6
PW = 2304
DTW = 128
SSD_L = 128
SSD_SUB = 2
SSD_SUB_BWD = 2
NH, HP, NS = 4, 64, 128
S5_P = 1024
MESH = pl.DeviceIdType.MESH

ADAM_LR, ADAM_B1, ADAM_B2, ADAM_EPS, ADAM_WD, ADAM_STEP = 0.001, 0.9, 0.999, 1e-08, 0.01, 10

NT = (((1,), (1,)), ((), ()))
TN = (((0,), (0,)), ((), ()))

WEIGHTS = ['norm_mix_w', 'norm_mlp_w', 'ada_w', 'ada_b', 'w_in', 'pool_w', 'pool_scale', 'sconv_w', 'ssd_conv_w',
           'ssd_conv_b', 'ssd_dt_bias', 'ssd_a_log', 'ssd_d', 's5_a_re', 's5_a_im', 's5_log_step', 's5_b_re', 's5_b_im',
           's5_c_re', 's5_c_im', 's5_d', 's5_glu_w', 's5_glu_b', 'branch_norm_w', 'w_out', 'mlp_w1', 'mlp_w2',
           'final_norm_w']
BIG = ('ada_w', 'w_in', 'w_out', 'mlp_w1', 'mlp_w2')
SMALL_SHARDED = {'sconv_w': 2, 'ssd_conv_w': 2, 's5_glu_w': 1}


def _cparams(n_axes, vmem_mb=48):
    return pltpu.CompilerParams(dimension_semantics=("arbitrary",) * n_axes, vmem_limit_bytes=vmem_mb * 1024 * 1024)


def _row(n):
    return pl.BlockSpec((1, n), lambda *_: (0, 0))


def _full(shape):
    nd = len(shape)
    return pl.BlockSpec(tuple(shape), lambda *_: (0,) * nd)


def _dot(a, b, dims=None, prec=None):
    if dims is None:
        dims = (((a.ndim - 1,), (0,)), ((), ()))
    return lax.dot_general(a, b, dims, preferred_element_type=F32, precision=prec)


def _bdot(a, b, dims=None):
    return _dot(a.astype(BF16), b.astype(BF16), dims)


def _sig(x):
    return jax.nn.sigmoid(x)


def _silu(x):
    return x * _sig(x)


def _dsilu(x):
    s = _sig(x)
    return s * (1.0 + x * (1.0 - s))


def _softplus(x):
    return jnp.maximum(x, 0.0) + jnp.log(1.0 + jnp.exp(-jnp.abs(x)))


_GK = math.sqrt(2.0 / math.pi)


def _gelu(x):
    return 0.5 * x * (1.0 + jnp.tanh(_GK * (x + 0.044715 * x * x * x)))


def _dgelu(x):
    th = jnp.tanh(_GK * (x + 0.044715 * x * x * x))
    return 0.5 * (1.0 + th) + 0.5 * x * (1.0 - th * th) * _GK * (1.0 + 3.0 * 0.044715 * x * x)


def _colsum(x):
    return jnp.sum(x, axis=0, keepdims=True)


def _rms(x):
    r = lax.rsqrt(jnp.mean(x * x, axis=-1, keepdims=True) + EPS)
    return r, x * r


def _rms_bwd(r, n, dn):
    return r * (dn - n * jnp.mean(dn * n, axis=-1, keepdims=True))


def _roll(x, k):
    n = x.shape[0]
    k = k % n
    return x if k == 0 else pltpu.roll(x, k, axis=0)


def _tblock(t, want=512):
    return min(t, want)


def _peer(mask):
    x, y, c = lax.axis_index("x"), lax.axis_index("y"), lax.axis_index("c")
    return (x ^ ((mask >> 2) & 1), y ^ ((mask >> 1) & 1), c ^ (mask & 1))


def _group_index(masks):
    x, y, c = lax.axis_index("x"), lax.axis_index("y"), lax.axis_index("c")
    full = 0
    for m in masks:
        full |= m
    bits = [b for b in (4, 2, 1) if full & b]

    def idx(px, py, pc):
        v = {4: px, 2: py, 1: pc}
        out = 0
        for b in bits:
            out = out * 2 + v[b]
        return out

    return idx(x, y, c), [idx(*_peer(m)) for m in masks]


def _exchange(arrs, masks, scatter, name, stage=False):
    n_arr, n_peer, n_grp = len(arrs), len(masks), len(masks) + 1

    def body(*refs):
        ins, outs = refs[:n_arr], refs[n_arr:2 * n_arr]
        send_sems, recv_sems, local_sems = refs[2 * n_arr:2 * n_arr + 3]
        if stage:
            bufs, load_sems = refs[2 * n_arr + 3:3 * n_arr + 3], refs[3 * n_arr + 3]
            loads = [pltpu.make_async_copy(ins[t], bufs[t], load_sems.at[t]) for t in range(n_arr)]
            for ld in loads:
                ld.start()
            for ld in loads:
                ld.wait()
            ins = bufs
        me, peer_idx = _group_index(masks)
        copies = []
        for t in range(n_arr):
            src_me = ins[t].at[me] if scatter else ins[t]
            loc = pltpu.make_async_copy(src_me, outs[t].at[me], local_sems.at[t])
            loc.start()
            copies.append(loc)
            for j, m in enumerate(masks):
                src = ins[t].at[peer_idx[j]] if scatter else ins[t]
                cp = pltpu.make_async_remote_copy(src_ref=src, dst_ref=outs[t].at[me], send_sem=send_sems.at[t, j],
                                                  recv_sem=recv_sems.at[t, j], device_id=_peer(m), device_id_type=MESH)
                cp.start()
                copies.append(cp)
        for cp in copies:
            cp.wait()

    hbm = pl.BlockSpec(memory_space=pl.ANY)
    out_shape = [jax.ShapeDtypeStruct((n_grp,) + (a.shape[1:] if scatter else a.shape), a.dtype) for a in arrs]
    staging = [pltpu.VMEM(a.shape, a.dtype) for a in arrs] + [pltpu.SemaphoreType.DMA((n_arr,))] if stage else []
    outs = pl.pallas_call(
        body, name=name, in_specs=[hbm] * n_arr, out_specs=[hbm] * n_arr, out_shape=out_shape,
        scratch_shapes=[pltpu.SemaphoreType.DMA((n_arr, n_peer)), pltpu.SemaphoreType.DMA((n_arr, n_peer)),
                        pltpu.SemaphoreType.DMA((n_arr,))] + staging,
        compiler_params=pltpu.CompilerParams(vmem_limit_bytes=48 * 1024 * 1024),
    )(*arrs)
    return list(outs)


def _split_copies(src_refs, land_refs, sems, scatter, per_core):
    me, peer_idx = _group_index(CHIPS)
    n = len(CHIPS) * len(src_refs)
    copies = []
    for t, (src_ref, land_ref) in enumerate(zip(src_refs, land_refs)):
        zone = land_ref.at[lax.axis_index("c")] if per_core else land_ref
        for j, m in enumerate(CHIPS):
            k = len(CHIPS) * t + j
            copies.append(pltpu.make_async_remote_copy(
                src_ref=src_ref.at[peer_idx[j]] if scatter else src_ref, dst_ref=zone.at[me], send_sem=sems[k],
                recv_sem=sems[n + k], device_id=_peer(m), device_id_type=MESH))
    return copies


def _split_start(srcs, after, scatter, name, per_core=False):
    n_arr, n_sem = len(srcs), 2 * len(CHIPS) * len(srcs)

    def body(*refs):
        src_refs, land_refs = refs[:n_arr], refs[n_arr:2 * n_arr]
        outs = refs[2 * n_arr + len(after):]
        for cp in _split_copies(src_refs, land_refs, outs[:n_sem], scatter, per_core):
            cp.start()
        outs[-1][...] = jnp.zeros_like(outs[-1])

    hbm = pl.BlockSpec(memory_space=pltpu.HBM)
    sem = pl.BlockSpec(memory_space=pltpu.SEMAPHORE)
    lands = [lax.empty(((2,) if per_core else ()) + (len(CHIPS) + 1,) + (a.shape[1:] if scatter else a.shape), a.dtype)
             for a in srcs]
    as_hbm = lambda a: pltpu.with_memory_space_constraint(a, pltpu.HBM)
    outs = pl.pallas_call(
        body, name=name,
        out_shape=(pltpu.SemaphoreType.DMA(()),) * n_sem + tuple(pltpu.HBM(a.shape, a.dtype) for a in srcs + lands)
        + (jax.ShapeDtypeStruct((8, 128), F32),),
        in_specs=(hbm,) * (2 * n_arr) + (pl.BlockSpec(memory_space=pl.ANY),) * len(after),
        out_specs=(sem,) * n_sem + (hbm,) * (2 * n_arr) + (pl.BlockSpec(memory_space=pltpu.VMEM),),
        input_output_aliases={t: n_sem + t for t in range(2 * n_arr)},
        compiler_params=pltpu.CompilerParams(has_side_effects=pltpu.SideEffectType.DATAFLOW_SIDE_EFFECTING),
    )(*[as_hbm(a) for a in srcs + lands], *after)
    return outs[:n_sem], list(outs[n_sem:n_sem + n_arr]), list(outs[n_sem + n_arr:n_sem + 2 * n_arr]), outs[-1]


def _split_wait(sems, srcs, lands, after, scatter, name, per_core=False):
    n_arr, n_sem = len(srcs), len(sems)

    def body(*refs):
        src_refs, land_refs = refs[:n_arr], refs[n_arr:2 * n_arr]
        for cp in _split_copies(src_refs, land_refs, refs[2 * n_arr:2 * n_arr + n_sem], scatter, per_core):
            cp.wait_send()
            cp.wait_recv()

    hbm = pl.BlockSpec(memory_space=pltpu.HBM)
    sem = pl.BlockSpec(memory_space=pltpu.SEMAPHORE)
    outs = pl.pallas_call(
        body, name=name, out_shape=tuple(pltpu.HBM(a.shape, a.dtype) for a in srcs + lands),
        in_specs=(hbm,) * (2 * n_arr) + (sem,) * n_sem + (pl.BlockSpec(memory_space=pl.ANY),) * len(after),
        out_specs=(hbm,) * (2 * n_arr), input_output_aliases={t: t for t in range(2 * n_arr)},
        compiler_params=pltpu.CompilerParams(has_side_effects=pltpu.SideEffectType.DATAFLOW_SIDE_EFFECTING),
    )(*srcs, *lands, *sems, *after)
    return list(outs[:n_arr]), list(outs[n_arr:])


CHIPS = (4, 2, 6)
EVERYONE = (1, 2, 3, 4, 5, 6, 7)
SIBLING = (1,)
SWAP_ROWS = 1024
WIN_ROWS = 592


def _pair_swap(arrs, other_layer, name, narrow=False, fill=False):
    assert not (fill and (other_layer or narrow))
    n_arr = len(arrs)
    shapes = [a.shape[-2:] for a in arrs]
    out_dtypes = [BF16 if narrow else a.dtype for a in arrs]
    chunks = []
    for t, (rows, _) in enumerate(shapes):
        assert rows % 16 == 0
        for j, r0 in enumerate(range(0, rows, SWAP_ROWS)):
            chunks.append((t, r0, min(SWAP_ROWS, rows - r0), j % 2))

    def body(*refs):
        ins, outs = refs[:n_arr], refs[n_arr:2 * n_arr]
        bufs = refs[2 * n_arr:3 * n_arr]
        out_bufs = refs[3 * n_arr:4 * n_arr] if narrow else bufs
        load_sems, send_sems, recv_sems = refs[-3:]
        sibling = _peer(1)
        c = lax.axis_index("c")

        def load(k):
            t, r0, n, slot = chunks[k]
            src = ins[t].at[1 - c] if other_layer else ins[t].at[c] if fill else ins[t]
            return pltpu.make_async_copy(src.at[pl.ds(r0, n)], bufs[t].at[slot, pl.ds(0, n)], load_sems.at[t, slot])

        def send(k):
            t, r0, n, slot = chunks[k]
            dst = outs[t].at[c] if fill else outs[t]
            return pltpu.make_async_remote_copy(src_ref=out_bufs[t].at[slot, pl.ds(0, n)], dst_ref=dst.at[pl.ds(r0, n)],
                                                send_sem=send_sems.at[t, slot], recv_sem=recv_sems.at[t],
                                                device_id=sibling, device_id_type=MESH)

        in_flight = {}

        def drain(k):
            key = (chunks[k][0], chunks[k][3])
            if key in in_flight:
                send(in_flight.pop(key)).wait_send()

        def start_load(k):
            if not narrow:
                drain(k)
            load(k).start()

        start_load(0)
        for k in range(len(chunks)):
            t, _, n, slot = chunks[k]
            load(k).wait()
            if k + 1 < len(chunks):
                start_load(k + 1)
            if narrow:
                drain(k)
                out_bufs[t][slot, pl.ds(0, n), :] = bufs[t][slot, pl.ds(0, n), :].astype(BF16)
            send(k).start()
            in_flight[(t, slot)] = k
        for k in in_flight.values():
            send(k).wait_send()
        for t in range(n_arr):
            landed = outs[t].at[1 - c] if fill else outs[t]
            pltpu.make_async_remote_copy(src_ref=landed, dst_ref=landed, send_sem=send_sems.at[t, 0],
                                         recv_sem=recv_sems.at[t], device_id=sibling, device_id_type=MESH).wait_recv()

    hbm = pl.BlockSpec(memory_space=pl.ANY)
    outs = pl.pallas_call(
        body, name=name, in_specs=[hbm] * n_arr, out_specs=[hbm] * n_arr,
        out_shape=[jax.ShapeDtypeStruct(a.shape if fill else s, dt) for a, s, dt in zip(arrs, shapes, out_dtypes)],
        input_output_aliases={t: t for t in range(n_arr)} if fill else {},
        scratch_shapes=[pltpu.VMEM((2, min(SWAP_ROWS, s[0]), s[1]), a.dtype) for s, a in zip(shapes, arrs)]
        + ([pltpu.VMEM((2, min(SWAP_ROWS, s[0]), s[1]), BF16) for s in shapes] if narrow else [])
        + [pltpu.SemaphoreType.DMA((n_arr, 2)), pltpu.SemaphoreType.DMA((n_arr, 2)), pltpu.SemaphoreType.DMA((n_arr,))],
        compiler_params=pltpu.CompilerParams(vmem_limit_bytes=48 * 1024 * 1024),
    )(*arrs)
    return list(outs)


def _sum_lead(a, name, out_dtype):
    n = a.shape[0]
    shape = a.shape[1:]

    def body(a_ref, o_ref):
        acc = a_ref[0].astype(F32)
        for k in range(1, n):
            acc = acc + a_ref[k].astype(F32)
        o_ref[...] = acc.astype(out_dtype)

    if len(shape) == 3:
        blk = (1,) + shape[1:]
        return pl.pallas_call(
            body, name=name, grid=(shape[0],), in_specs=[pl.BlockSpec((n,) + blk, lambda i: (0, i, 0, 0))],
            out_specs=pl.BlockSpec(blk, lambda i: (i, 0, 0)), out_shape=jax.ShapeDtypeStruct(shape, out_dtype),
            compiler_params=_cparams(1),
        )(a)
    rows, cols = shape
    rb = rows
    for cand in (512, 256, 128):
        if rows % cand == 0 and rows > cand:
            rb = cand
            break
    return pl.pallas_call(
        body, name=name, grid=(rows // rb,), in_specs=[pl.BlockSpec((n, rb, cols), lambda i: (0, i, 0))],
        out_specs=pl.BlockSpec((rb, cols), lambda i: (i, 0)), out_shape=jax.ShapeDtypeStruct((rows, cols), out_dtype),
        compiler_params=_cparams(1),
    )(a)


def _pair_sum(g, recv, layer, name, out_dtype):
    _, n, r, c = g.shape

    def body(l_ref, g_ref, r_ref, o_ref):
        o_ref[...] = (g_ref[0].astype(F32) + r_ref[...].astype(F32)).astype(out_dtype)

    return pl.pallas_call(
        body, name=name,
        grid_spec=pltpu.PrefetchScalarGridSpec(
            num_scalar_prefetch=1, grid=(n,),
            in_specs=[pl.BlockSpec((1, 1, r, c), lambda i, l: (l[0], i, 0, 0)), pl.BlockSpec((1, r, c), lambda i, l: (i, 0, 0))],
            out_specs=pl.BlockSpec((1, r, c), lambda i, l: (i, 0, 0))),
        out_shape=jax.ShapeDtypeStruct((n, r, c), out_dtype), compiler_params=_cparams(1),
    )(layer, g, recv)


def _tn_matmul(a, b, name, col_major=False, into=None, layer=0):
    t, k = a.shape
    n = b.shape[1]
    tb = _tblock(t, 1024)
    kb = min(k, 1024)
    nb = min(n, 1024)
    grid = (k // kb, n // nb, t // tb)
    lead = (into is not None) + col_major

    def body(a_ref, b_ref, *rest):
        o_ref = rest[-1]
        for _ in range(lead):
            o_ref = o_ref.at[0]

        @pl.when(pl.program_id(2) == 0)
        def _():
            o_ref[...] = jnp.zeros_like(o_ref)

        o_ref[...] += _bdot(a_ref[...], b_ref[...], TN)

    if col_major:
        block, index, shape = (1, kb, nb), (lambda ki, ni: (ni, ki, 0)), (n // nb, k, nb)
    else:
        block, index, shape = (kb, nb), (lambda ki, ni: (ki, ni)), (k, n)
    in_specs = [pl.BlockSpec((tb, kb), lambda ki, ni, ti: (ti, ki)), pl.BlockSpec((tb, nb), lambda ki, ni, ti: (ti, ni))]
    if into is None:
        return pl.pallas_call(
            body, name=name, grid=grid, in_specs=in_specs, out_specs=pl.BlockSpec(block, lambda ki, ni, ti: index(ki, ni)),
            out_shape=jax.ShapeDtypeStruct(shape, F32), compiler_params=_cparams(3),
        )(a, b)
    assert into.shape == (2,) + shape
    return pl.pallas_call(
        body, name=name, grid=grid, in_specs=in_specs + [pl.BlockSpec(memory_space=pl.ANY)],
        out_specs=pl.BlockSpec((1,) + block, lambda ki, ni, ti: (layer,) + index(ki, ni)),
        out_shape=jax.ShapeDtypeStruct(into.shape, F32), input_output_aliases={2: 0}, compiler_params=_cparams(3),
    )(a, b, into)


def _sum_many(arrs, name, out_dtypes=None):
    k = len(arrs)
    out_dtypes = out_dtypes or [F32] * k

    def body(*refs):
        for a_ref, o_ref in zip(refs[:k], refs[k:]):
            acc = a_ref[0].astype(F32)
            for j in range(1, a_ref.shape[0]):
                acc = acc + a_ref[j].astype(F32)
            o_ref[...] = acc.astype(o_ref.dtype)

    return pl.pallas_call(body, name=name, grid=(1,), in_specs=[_full(a.shape) for a in arrs],
                          out_specs=[_full(a.shape[1:]) for a in arrs],
                          out_shape=[jax.ShapeDtypeStruct(a.shape[1:], dt) for a, dt in zip(arrs, out_dtypes)],
                          compiler_params=_cparams(1))(*arrs)


def _adamw_math(w, g, m, v):
    m2 = ADAM_B1 * m + (1.0 - ADAM_B1) * g
    v2 = ADAM_B2 * v + (1.0 - ADAM_B2) * (g * g)
    m_hat = m2 / (1.0 - ADAM_B1 ** ADAM_STEP)
    v_hat = v2 / (1.0 - ADAM_B2 ** ADAM_STEP)
    return -ADAM_LR * (m_hat / (jnp.sqrt(v_hat) + ADAM_EPS) + ADAM_WD * w), m2, v2


def _adamw_many(ws, gs, ms, vs, name):
    n = len(ws)

    def body(*refs):
        ins, outs = refs[:4 * n], refs[4 * n:]
        for k in range(n):
            res = _adamw_math(ins[k][...], ins[n + k][...], ins[2 * n + k][...], ins[3 * n + k][...])
            for j in range(3):
                outs[3 * k + j][...] = res[j]

    out_shape = []
    for a in ws:
        out_shape += [jax.ShapeDtypeStruct(a.shape, F32)] * 3
    return pl.pallas_call(body, name=name, grid=(1,), in_specs=[_full(a.shape) for a in ws] * 4,
                          out_specs=[_full(s.shape) for s in out_shape], out_shape=out_shape,
                          compiler_params=_cparams(1))(*ws, *gs, *ms, *vs)


def _adamw(w, g, m, v, name):
    shape = w.shape
    cols = shape[-1]
    rows = int(math.prod(shape[:-1]))
    rb = rows
    for cand in (256, 128, 64, 32, 16, 8):
        if rows % cand == 0 and rows > cand:
            rb = cand
            break
    bc1 = 1.0 - ADAM_B1 ** ADAM_STEP
    bc2 = 1.0 - ADAM_B2 ** ADAM_STEP

    def body(w_ref, g_ref, m_ref, v_ref, d_ref, nm_ref, nv_ref):
        gg = g_ref[...]
        m2 = ADAM_B1 * m_ref[...] + (1.0 - ADAM_B1) * gg
        v2 = ADAM_B2 * v_ref[...] + (1.0 - ADAM_B2) * (gg * gg)
        m_hat = m2 / bc1
        v_hat = v2 / bc2
        d_ref[...] = -ADAM_LR * (m_hat / (jnp.sqrt(v_hat) + ADAM_EPS) + ADAM_WD * w_ref[...])
        nm_ref[...] = m2
        nv_ref[...] = v2

    spec = pl.BlockSpec((rb, cols), lambda i: (i, 0))
    sds = jax.ShapeDtypeStruct((rows, cols), F32)
    outs = pl.pallas_call(
        body, name=name, grid=(rows // rb,), in_specs=[spec] * 4, out_specs=[spec] * 3, out_shape=[sds] * 3,
        compiler_params=_cparams(1),
    )(*(z.reshape(rows, cols) for z in (w, g, m, v)))
    return tuple(o.reshape(shape) for o in outs)


def _ada_fwd(c_all, ada_w_sh, ada_b_sh):
    s = ada_w_sh.shape[2]
    sb = 512

    def body(c_ref, w_ref, b_ref, o_ref):
        cond = _silu(c_ref[...])
        o_ref[0] = _bdot(cond, w_ref[0]) + b_ref[0]

    return pl.pallas_call(
        body, name="ada_fwd", grid=(2, s // sb),
        in_specs=[_full((8, D)), pl.BlockSpec((1, D, sb), lambda l, j: (l, 0, j)), pl.BlockSpec((1, 1, sb), lambda l, j: (l, 0, j))],
        out_specs=pl.BlockSpec((1, 8, sb), lambda l, j: (l, 0, j)), out_shape=jax.ShapeDtypeStruct((2, 8, s), F32),
        compiler_params=_cparams(2),
    )(c_all, ada_w_sh, ada_b_sh)


def _ada_bwd(c_all, dmod_sh, dmod_all):
    s = dmod_sh.shape[2]
    sb = 512

    def body(c_ref, d_ref, o_ref):
        cond = _silu(c_ref[...])
        o_ref[0] = _bdot(cond, d_ref[0], TN)

    gw = pl.pallas_call(
        body, name="ada_bwd_w", grid=(2, s // sb),
        in_specs=[_full((8, D)), pl.BlockSpec((1, 8, sb), lambda l, j: (l, 0, j))],
        out_specs=pl.BlockSpec((1, D, sb), lambda l, j: (l, 0, j)), out_shape=jax.ShapeDtypeStruct((2, D, s), F32),
        compiler_params=_cparams(2),
    )(c_all, dmod_sh)

    def body_b(d_ref, o_ref):
        acc = d_ref[0, 0:1, :]
        for k in range(1, 8):
            acc = acc + d_ref[0, k:k + 1, :]
        o_ref[0] = acc

    gb = pl.pallas_call(
        body_b, name="ada_bwd_b", grid=(2,), in_specs=[pl.BlockSpec((1, 8, 6 * D), lambda l: (l, 0, 0))],
        out_specs=pl.BlockSpec((1, 1, 6 * D), lambda l: (l, 0, 0)), out_shape=jax.ShapeDtypeStruct((2, 1, 6 * D), F32),
        compiler_params=_cparams(1),
    )(dmod_all)
    return gw, gb.reshape(2, 6 * D)


def _f_in(h, nw, sc, sh, w_main, w_dt):
    t = h.shape[0]
    tb = _tblock(t)

    def body(h_ref, nw_ref, sc_ref, sh_ref, w_ref, wd_ref, p_ref, dt_ref, u_ref):
        _, n = _rms(h_ref[...])
        u = ((n * nw_ref[...]) * (1.0 + sc_ref[...]) + sh_ref[...]).astype(BF16)
        u_ref[...] = u
        p_ref[...] = _dot(u, w_ref[...], NT)
        dt_ref[...] = _dot(u, wd_ref[...], NT)

    return pl.pallas_call(
        body, name="f_in", grid=(t // tb,),
        in_specs=[pl.BlockSpec((tb, D), lambda i: (i, 0)), _row(D), _row(D), _row(D), _full((PW, D)), _full((DTW, D))],
        out_specs=[pl.BlockSpec((tb, PW), lambda i: (i, 0)), pl.BlockSpec((tb, DTW), lambda i: (i, 0)),
                   pl.BlockSpec((tb, D), lambda i: (i, 0))],
        out_shape=[jax.ShapeDtypeStruct((t, PW), F32), jax.ShapeDtypeStruct((t, DTW), F32), jax.ShapeDtypeStruct((t, D), BF16)],
        compiler_params=_cparams(1),
    )(h, nw, sc, sh, w_main, w_dt)


def _norm_bwd_step(du_v, x, dres_v, gated, nwv, scv, dx_ref, dsc_ref, dsh_ref, dnw_ref, dg_ref):
    r, n = _rms(x)
    scale = 1.0 + scv
    dsc_ref[...] += _colsum(du_v * (n * nwv))
    dsh_ref[...] += _colsum(du_v)
    dnw_ref[...] += _colsum(du_v * scale * n)
    dg_ref[...] += _colsum(dres_v * gated)
    dx_ref[...] = dres_v + _rms_bwd(r, n, du_v * scale * nwv)


def _b_in(dab, dz, dxbc, ds5, ddt, w_main, w_dt, x, dres, gated, nw, sc):
    t = dab.shape[0]
    tb = _tblock(t)

    def body(a_ref, z_ref, x_ref, s_ref, d_ref, w_ref, wd_ref, h_ref, dr_ref, g_ref, nw_ref, sc_ref,
             dx_ref, dsc_ref, dsh_ref, dnw_ref, dg_ref):
        @pl.when(pl.program_id(0) == 0)
        def _():
            for r in (dsc_ref, dsh_ref, dnw_ref, dg_ref):
                r[...] = jnp.zeros_like(r)

        du = _bdot(a_ref[...], w_ref[0:1024, :])
        du += _bdot(z_ref[...], w_ref[1024:1280, :])
        du += _bdot(s_ref[...], w_ref[1280:1536, :])
        du += _bdot(x_ref[...], w_ref[1536:2304, :])
        du += _bdot(d_ref[...], wd_ref[...])
        _norm_bwd_step(du, h_ref[...], dr_ref[...], g_ref[...], nw_ref[...], sc_ref[...], dx_ref, dsc_ref, dsh_ref, dnw_ref, dg_ref)

    blk = lambda n: pl.BlockSpec((tb, n), lambda i: (i, 0))
    row = jax.ShapeDtypeStruct((1, D), F32)
    return pl.pallas_call(
        body, name="b_in", grid=(t // tb,),
        in_specs=[blk(1024), blk(256), blk(768), blk(256), blk(DTW), _full((PW, D)), _full((DTW, D)),
                  blk(D), blk(D), blk(D), _row(D), _row(D)],
        out_specs=[blk(D), _row(D), _row(D), _row(D), _row(D)],
        out_shape=[jax.ShapeDtypeStruct((t, D), F32), row, row, row, row], compiler_params=_cparams(1),
    )(dab, dz, dxbc, ds5, ddt, w_main, w_dt, x, dres, gated, nw, sc)


HALO = 16


def _lane_group(shape):
    return lax.broadcasted_iota(jnp.int32, shape, 1) // 64


def _window_select(g, s2, s4, s8, s16):
    return jnp.where(g == 0, s2, jnp.where(g == 1, s4, jnp.where(g == 2, s8, s16)))


def _pool_count(t0, rows):
    g = _lane_group((rows, GW))
    win = _window_select(g, 2, 4, 8, 16)
    tt = t0 + lax.broadcasted_iota(jnp.int32, (rows, GW), 0)
    return jnp.minimum(tt + 1, win).astype(F32)


def _pool_p(v_ext, t0, tb):
    s2 = v_ext + _roll(v_ext, 1)
    s4 = s2 + _roll(s2, 2)
    s8 = s4 + _roll(s4, 4)
    s16 = s8 + _roll(s8, 8)
    ws = _window_select(_lane_group(v_ext.shape), s2, s4, s8, s16)[HALO:]
    return ws / _pool_count(t0, tb) - v_ext[HALO:]


def _sconv(q_ext, w):
    return (_roll(q_ext, 2) * w[0:1] + _roll(q_ext, 1) * w[1:2] + q_ext * w[2:3])[HALO:]


def _halo_specs(t, tb, cols, col_block):
    per = tb // HALO
    last = t // HALO - 1
    prev = pl.BlockSpec((HALO, cols), lambda i: (jnp.maximum(i * per - 1, 0), col_block))
    nxt = pl.BlockSpec((HALO, cols), lambda i: (jnp.minimum((i + 1) * per, last), col_block))
    return prev, nxt


def _f_ab(proj, pool_mat, pool_scale, sconv_w):
    t = proj.shape[0]
    tb = _tblock(t)
    prev, _ = _halo_specs(t, tb, 1024, 0)

    def body(p_ref, h_ref, pm_ref, ps_ref, sw_ref, ya_ref, yb_ref):
        i = pl.program_id(0)
        halo = jnp.where(i > 0, h_ref[...], 0.0)
        ext = jnp.concatenate([halo, p_ref[...]], axis=0)
        p = _pool_p(ext[:, 0:256], i * tb, tb)
        ya_ref[...] = _bdot(p, pm_ref[...]) * ps_ref[...]
        q_ext = ext[:, 512:768] * ext[:, 768:1024]
        yb_ref[...] = p_ref[:, 256:512] * _sconv(q_ext, sw_ref[...])

    blk = pl.BlockSpec((tb, GW), lambda i: (i, 0))
    sds = jax.ShapeDtypeStruct((t, GW), F32)
    return pl.pallas_call(
        body, name="f_ab", grid=(t // tb,),
        in_specs=[pl.BlockSpec((tb, 1024), lambda i: (i, 0)), prev, _full((GW, GW)), _row(GW), _full((3, GW))],
        out_specs=[blk, blk], out_shape=[sds, sds], compiler_params=_cparams(1),
    )(proj, proj, pool_mat, pool_scale, sconv_w)


def _b_ab(proj, dya, dyb, pool_mat, pool_scale, sconv_w):
    t = proj.shape[0]
    tb = _tblock(t)
    nb = t // tb
    prev, nxt = _halo_specs(t, tb, 1024, 0)
    _, nxt_g = _halo_specs(t, tb, GW, 0)
    n_ext = tb + HALO

    def body(p_ref, hp_ref, hn_ref, da_ref, dan_ref, db_ref, dbn_ref, pm_ref, ps_ref, sw_ref,
             o_ref, dpm_ref, dps_ref, dsw_ref):
        i = pl.program_id(0)

        @pl.when(i == 0)
        def _():
            for r in (dpm_ref, dps_ref, dsw_ref):
                r[...] = jnp.zeros_like(r)

        last = i == nb - 1
        halo = jnp.where(i > 0, hp_ref[...], 0.0)
        main = p_ref[...]
        ext = jnp.concatenate([halo, main], axis=0)
        scale = ps_ref[...]
        pm = pm_ref[...]
        p = _pool_p(ext[:, 0:256], i * tb, tb)
        da = da_ref[...]
        dps_ref[...] += _colsum(da * _bdot(p, pm))
        da_ext = jnp.concatenate([da, jnp.where(last, 0.0, dan_ref[...])], axis=0)
        dys = da_ext * scale
        dpm_ref[...] += _bdot(p, dys[:tb], TN)
        dp = _bdot(dys, pm, NT)
        dpc = dp / _pool_count(i * tb, n_ext)
        a2 = dpc + _roll(dpc, n_ext - 1)
        a4 = a2 + _roll(a2, n_ext - 2)
        a8 = a4 + _roll(a4, n_ext - 4)
        a16 = a8 + _roll(a8, n_ext - 8)
        o_ref[:, 0:256] = (_window_select(_lane_group(dpc.shape), a2, a4, a8, a16) - dp)[:tb]
        w = sw_ref[...]
        gb, gc, hh = main[:, 256:512], main[:, 512:768], main[:, 768:1024]
        q_ext = ext[:, 512:768] * ext[:, 768:1024]
        db = db_ref[...]
        o_ref[:, 256:512] = db * _sconv(q_ext, w)
        gb_next = hn_ref[:, 256:512]
        dconv = jnp.concatenate([db * gb, jnp.where(last, 0.0, dbn_ref[...] * gb_next)], axis=0)
        dq = (dconv * w[2:3] + _roll(dconv, n_ext - 1) * w[1:2] + _roll(dconv, n_ext - 2) * w[0:1])[:tb]
        o_ref[:, 512:768] = dq * hh
        o_ref[:, 768:1024] = dq * gc
        dc = dconv[:tb]
        dsw_ref[0:1, :] += _colsum(dc * _roll(q_ext, 2)[HALO:])
        dsw_ref[1:2, :] += _colsum(dc * _roll(q_ext, 1)[HALO:])
        dsw_ref[2:3, :] += _colsum(dc * q_ext[HALO:])

    blk = pl.BlockSpec((tb, GW), lambda i: (i, 0))
    return pl.pallas_call(
        body, name="b_ab", grid=(nb,),
        in_specs=[pl.BlockSpec((tb, 1024), lambda i: (i, 0)), prev, nxt, blk, nxt_g, blk, nxt_g,
                  _full((GW, GW)), _row(GW), _full((3, GW))],
        out_specs=[pl.BlockSpec((tb, 1024), lambda i: (i, 0)), _full((GW, GW)), _row(GW), _full((3, GW))],
        out_shape=[jax.ShapeDtypeStruct((t, 1024), F32), jax.ShapeDtypeStruct((GW, GW), F32),
                   jax.ShapeDtypeStruct((1, GW), F32), jax.ShapeDtypeStruct((3, GW), F32)],
        compiler_params=_cparams(1),
    )(proj, proj, proj, dya, dya, dyb, dyb, pool_mat, pool_scale, sconv_w)


CH = 8


def _ssd_conv(x, halo, w, b):
    ext = jnp.concatenate([halo, x], axis=0)
    pre = ext * w[3:4] + _roll(ext, 1) * w[2:3] + _roll(ext, 2) * w[1:2] + _roll(ext, 3) * w[0:1] + b
    return pre[CH:], ext


def _ssd_common(dt_raw, dtb, alog):
    ll = dt_raw.shape[0]
    dtv = _softplus(dt_raw + dtb)
    a_row = -jnp.exp(alog)
    r = lax.broadcasted_iota(jnp.int32, (ll, ll), 0)
    c = lax.broadcasted_iota(jnp.int32, (ll, ll), 1)
    tril = (r >= c).astype(F32)
    cs = _dot(tril, dtv * a_row, prec=HI)
    return dtv, a_row, cs, cs.T, r >= c


def _bd(a, b, ca, cb):
    return lax.dot_general(a, b, (((ca,), (cb,)), ((0,), (0,))), preferred_element_type=F32)


def _head_cols(m):
    return jnp.stack([m[:, h:h + 1] for h in range(NH)])


def _ssd_heads(act, dtv, cs, cs_t, causal):
    xs = jnp.stack([act[:, HP * h:HP * (h + 1)] for h in range(NH)])
    bm = jnp.stack([act[:, 256 + NS * (h // 2):256 + NS * (h // 2 + 1)] for h in range(NH)])
    cm = jnp.stack([act[:, 512 + NS * (h // 2):512 + NS * (h // 2 + 1)] for h in range(NH)])
    cs_c = _head_cols(cs)
    cs_r = jnp.stack([cs_t[h:h + 1, :] for h in range(NH)])
    mdec = jnp.where(causal[None], jnp.exp(jnp.minimum(cs_c - cs_r, 0.0)), 0.0)
    g2 = _bd(jnp.stack([cm[0], cm[2]]), jnp.stack([bm[0], bm[2]]), 2, 2)
    sc = jnp.stack([g2[h // 2] for h in range(NH)]) * mdec
    dt_c = _head_cols(dtv)
    xdt = xs * dt_c
    e = jnp.exp(cs_c)
    cs_last = cs_c[:, SSD_L - 1:SSD_L, :]
    wdec = jnp.exp(cs_last - cs_c)
    return xs, bm, cm, mdec, sc, dt_c, xdt, e, cs_last, wdec


def _head_scalars(row_ref):
    return jnp.stack([row_ref[0:1, h:h + 1] for h in range(NH)])


def _f_ssd(proj, dtp, conv_w, conv_b, dt_bias, a_log, d_skip):
    t = proj.shape[0]
    nc = t // SSD_L
    rows = SSD_SUB * SSD_L
    per = rows // CH

    def body(x_ref, hx_ref, dt_ref, z_ref, cw_ref, cb_ref, dtb_ref, al_ref, dk_ref, y_ref, yp_ref, sp_ref, s_ref):
        i = pl.program_id(0)

        @pl.when(i == 0)
        def _():
            s_ref[...] = jnp.zeros_like(s_ref)

        state = s_ref[...]
        dk = _head_scalars(dk_ref)
        for sub in range(SSD_SUB):
            r0 = sub * SSD_L
            rs = slice(r0, r0 + SSD_L)
            halo = jnp.where(i > 0, hx_ref[...], 0.0) if sub == 0 else x_ref[r0 - CH:r0, :]
            pre, _ = _ssd_conv(x_ref[rs, :], halo, cw_ref[...], cb_ref[...])
            act = _silu(pre)
            dtv, _, cs, cs_t, causal = _ssd_common(dt_ref[rs, :], dtb_ref[...], al_ref[...])
            xs, bm, cm, _, sc, _, xdt, e, cs_last, wdec = _ssd_heads(act, dtv, cs, cs_t, causal)
            sp_ref[sub] = state
            y = _bd(sc, xdt, 2, 1) + e * _bd(cm, state, 2, 2) + xs * dk
            for h in range(NH):
                yp_ref[rs, HP * h:HP * (h + 1)] = y[h]
            state = state * jnp.exp(cs_last) + _bd(xdt * wdec, bm, 1, 1)
            y_ref[rs, :] = yp_ref[rs, :] * _silu(z_ref[rs, :])
        s_ref[...] = state

    blk = pl.BlockSpec((rows, GW), lambda i: (i, 0))
    sds = jax.ShapeDtypeStruct((t, GW), F32)
    return pl.pallas_call(
        body, name="f_ssd", grid=(nc // SSD_SUB,),
        in_specs=[pl.BlockSpec((rows, 768), lambda i: (i, 2)),
                  pl.BlockSpec((CH, 768), lambda i: (jnp.maximum(i * per - 1, 0), 2)),
                  pl.BlockSpec((rows, DTW), lambda i: (i, 0)),
                  pl.BlockSpec((rows, GW), lambda i: (i, 4)),
                  _full((4, 768)), _row(768), _row(DTW), _row(DTW), _row(DTW)],
        out_specs=[blk, blk, pl.BlockSpec((SSD_SUB, NH, HP, NS), lambda i: (i, 0, 0, 0))],
        out_shape=[sds, sds, jax.ShapeDtypeStruct((nc, NH, HP, NS), F32)],
        scratch_shapes=[pltpu.VMEM((NH, HP, NS), F32)], compiler_params=_cparams(1),
    )(proj, proj, dtp, proj, conv_w, conv_b, dt_bias, a_log, d_skip)


def _b_ssd(proj, dtp, ypre, dyc, sprev, conv_w, conv_b, dt_bias, a_log, d_skip):
    t = proj.shape[0]
    nc = t // SSD_L
    steps = nc // SSD_SUB_BWD
    rows = SSD_SUB_BWD * SSD_L
    per = rows // CH
    n_ext = SSD_L + CH

    def chunk(sub, halo, dnext, ds_in, refs):
        (x_ref, dt_ref, z_ref, yp_ref, dy_ref, sp_ref, cw_ref, cb_ref, dtb_ref, al_ref, dk_ref,
         dz_ref, dx_ref, ddt_ref, dact_ref) = refs
        rs = slice(sub * SSD_L, (sub + 1) * SSD_L)
        dact = dact_ref.at[sub]
        w = cw_ref[...]
        pre, ext = _ssd_conv(x_ref[rs, :], halo, w, cb_ref[...])
        act = _silu(pre)
        dt_raw = dt_ref[rs, :]
        dtv, a_row, cs, cs_t, causal = _ssd_common(dt_raw, dtb_ref[...], al_ref[...])
        z = z_ref[rs, :]
        dyc_v = dy_ref[rs, :]
        dz_ref[rs, :] = dyc_v * yp_ref[rs, :] * _dsilu(z)
        dy_all = dyc_v * _silu(z)
        lane = lax.broadcasted_iota(jnp.int32, (SSD_L, DTW), 1)
        rowi = lax.broadcasted_iota(jnp.int32, (1, SSD_L, 1), 1)
        lane1 = lax.broadcasted_iota(jnp.int32, (1, DTW), 1)
        xs, bm, cm, mdec, sc, dt_c, xdt, e, cs_last, wdec = _ssd_heads(act, dtv, cs, cs_t, causal)
        dy = jnp.stack([dy_all[:, HP * h:HP * (h + 1)] for h in range(NH)])
        prev = sp_ref[sub]
        ds = ds_in
        lsum = lambda v: jnp.sum(v, axis=2, keepdims=True)
        dsc = _bd(dy, xdt, 2, 2)
        q = dsc * sc
        dg = dsc * mdec
        dxdt = _bd(sc, dy, 1, 1)
        dcs = lsum(q) - lsum(jnp.swapaxes(q, 1, 2))
        dc = _bd(dg, bm, 2, 1)
        db = _bd(dg, cm, 1, 1)
        cp = _bd(cm, prev, 2, 2)
        dcs += lsum(dy * cp) * e
        ey = e * dy
        dc += _bd(ey, prev, 2, 1)
        dprev = _bd(ey, cm, 1, 1)
        elast = jnp.exp(cs_last)
        dprev += ds * elast
        dcs_last = jnp.sum(lsum(ds * prev), axis=1, keepdims=True) * elast
        bds = _bd(bm, ds, 2, 2)
        dxdt += wdec * bds
        db += wdec * _bd(xdt, ds, 2, 1)
        dw = lsum(xdt * bds) * wdec
        dcs -= dw
        dcs_last += jnp.sum(dw, axis=1, keepdims=True)
        dcs += jnp.where(rowi == SSD_L - 1, dcs_last, 0.0)
        dxs = dxdt * dt_c + dy * _head_scalars(dk_ref)
        ddtx = lsum(dxdt * xs)
        ddk = jnp.sum(lsum(dy * xs), axis=1, keepdims=True)
        dcs_mat = jnp.zeros((SSD_L, DTW), F32)
        ddtx_mat = jnp.zeros((SSD_L, DTW), F32)
        ddk_row = jnp.zeros((1, DTW), F32)
        for h in range(NH):
            dact[:, HP * h:HP * (h + 1)] = dxs[h]
            dcs_mat = jnp.where(lane == h, dcs[h], dcs_mat)
            ddtx_mat = jnp.where(lane == h, ddtx[h], ddtx_mat)
            ddk_row = jnp.where(lane1 == h, ddk[h], ddk_row)
        for g in range(2):
            dact[:, 256 + NS * g:256 + NS * (g + 1)] = db[2 * g] + db[2 * g + 1]
            dact[:, 512 + NS * g:512 + NS * (g + 1)] = dc[2 * g] + dc[2 * g + 1]
        ds_out = dprev
        r2 = lax.broadcasted_iota(jnp.int32, (SSD_L, SSD_L), 0)
        c2 = lax.broadcasted_iota(jnp.int32, (SSD_L, SSD_L), 1)
        dadt = _dot((c2 >= r2).astype(F32), dcs_mat, prec=HI)
        ddt = jnp.where(lane < NH, (dadt * a_row + ddtx_mat) * _sig(dt_raw + dtb_ref[...]), 0.0)
        ddt_ref[rs, :] = ddt
        dpre = dact[...] * _dsilu(pre)
        dcw = jnp.concatenate([_colsum(dpre * _roll(ext, 3 - k)[CH:]) for k in range(4)], axis=0)
        dext = jnp.concatenate([dpre, dnext], axis=0)
        dx_ref[rs, :] = (dext * w[3:4] + _roll(dext, n_ext - 1) * w[2:3] + _roll(dext, n_ext - 2) * w[1:2]
                         + _roll(dext, n_ext - 3) * w[0:1])[:SSD_L]
        acc = (dcw, _colsum(dpre), _colsum(ddt), _colsum(dadt * dtv) * a_row, ddk_row)
        return dpre[0:CH], ds_out, acc

    def body(x_ref, hx_ref, dt_ref, z_ref, yp_ref, dy_ref, sp_ref, cw_ref, cb_ref, dtb_ref, al_ref, dk_ref,
             dz_ref, dx_ref, ddt_ref, dcw_ref, dcb_ref, ddtb_ref, dal_ref, ddk_ref, ds_ref, dnext_ref, dact_ref):
        i = pl.program_id(0)
        acc_refs = (dcw_ref, dcb_ref, ddtb_ref, dal_ref, ddk_ref)

        @pl.when(i == 0)
        def _():
            ds_ref[...] = jnp.zeros_like(ds_ref)
            dnext_ref[...] = jnp.zeros_like(dnext_ref)
            for r in acc_refs:
                r[...] = jnp.zeros_like(r)

        refs = (x_ref, dt_ref, z_ref, yp_ref, dy_ref, sp_ref, cw_ref, cb_ref, dtb_ref, al_ref, dk_ref, dz_ref, dx_ref, ddt_ref,
                dact_ref)
        ds = ds_ref[...]
        dnext = dnext_ref[...]
        total = None
        for sub in reversed(range(SSD_SUB_BWD)):
            if sub == 0:
                halo = jnp.where(i == steps - 1, 0.0, hx_ref[...])
            else:
                halo = x_ref[sub * SSD_L - CH:sub * SSD_L, :]
            dnext, ds, acc = chunk(sub, halo, dnext, ds, refs)
            total = acc if total is None else tuple(a + b for a, b in zip(total, acc))
        ds_ref[...] = ds
        dnext_ref[...] = dnext
        for r, v in zip(acc_refs, total):
            r[...] += v

    rev = lambda i: steps - 1 - i
    blk = lambda n, cb=0: pl.BlockSpec((rows, n), lambda i: (rev(i), cb))
    row = lambda n: jax.ShapeDtypeStruct((1, n), F32)
    return pl.pallas_call(
        body, name="b_ssd", grid=(steps,),
        in_specs=[blk(768, 2), pl.BlockSpec((CH, 768), lambda i: (jnp.maximum(rev(i) * per - 1, 0), 2)),
                  blk(DTW), blk(GW, 4), blk(GW), blk(GW), pl.BlockSpec((SSD_SUB_BWD, NH, HP, NS), lambda i: (rev(i), 0, 0, 0)),
                  _full((4, 768)), _row(768), _row(DTW), _row(DTW), _row(DTW)],
        out_specs=[blk(GW), blk(768), blk(DTW), _full((4, 768)), _row(768), _row(DTW), _row(DTW), _row(DTW)],
        out_shape=[jax.ShapeDtypeStruct((t, GW), F32), jax.ShapeDtypeStruct((t, 768), F32), jax.ShapeDtypeStruct((t, DTW), F32),
                   jax.ShapeDtypeStruct((4, 768), F32), row(768), row(DTW), row(DTW), row(DTW)],
        scratch_shapes=[pltpu.VMEM((NH, HP, NS), F32), pltpu.VMEM((CH, 768), F32), pltpu.VMEM((SSD_SUB_BWD, SSD_L, 768), F32)],
        compiler_params=_cparams(1),
    )(proj, proj, dtp, proj, ypre, dyc, sprev, conv_w, conv_b, dt_bias, a_log, d_skip)


def _s5_block(t):
    return min(t, 256)


def _seg_t():
    r = lax.broadcasted_iota(jnp.int32, (64, 1024), 0)
    c = lax.broadcasted_iota(jnp.int32, (64, 1024), 1)
    return (c // 16 == r).astype(F32)


def _s5_prep_math(a_re, a_im, lstep, b_re, b_im):
    step = jnp.exp(lstep)
    ars = a_re * step
    ais = a_im * step
    mag = jnp.exp(ars)
    lr = mag * jnp.cos(ais)
    li = mag * jnp.sin(ais)
    den = a_re * a_re + a_im * a_im
    nr = lr - 1.0
    f_re = (nr * a_re + li * a_im) / den
    f_im = (li * a_re - nr * a_im) / den
    seg = _seg_t()
    fr = _dot(f_re, seg, prec=HI)
    fi = _dot(f_im, seg, prec=HI)
    return lr, li, fr * b_re - fi * b_im, fr * b_im + fi * b_re, ars, ais


def _s5_prep(a_re, a_im, lstep, b_re, b_im):
    def body(ar, ai, ls, br, bi, lr_o, li_o, bbr_o, bbi_o, ars_o, ais_o):
        outs = _s5_prep_math(ar[...], ai[...], ls[...], br[...], bi[...])
        for o, v in zip((lr_o, li_o, bbr_o, bbi_o, ars_o, ais_o), outs):
            o[...] = v

    s64 = jax.ShapeDtypeStruct((16, 64), F32)
    s1k = jax.ShapeDtypeStruct((16, 1024), F32)
    return pl.pallas_call(body, name="s5_prep", out_shape=[s64, s64, s1k, s1k, s64, s64])(a_re, a_im, lstep, b_re, b_im)


def _s5_prep_bwd(a_re, a_im, lstep, b_re, b_im, dlr, dli, dbbr, dbbi):
    def body(ar, ai, ls, br, bi, g0, g1, g2, g3, o0, o1, o2, o3, o4):
        f = lambda *a: _s5_prep_math(*a)[:4]
        _, vjp = jax.vjp(f, ar[...], ai[...], ls[...], br[...], bi[...])
        for o, v in zip((o0, o1, o2, o3, o4), vjp((g0[...], g1[...], g2[...], g3[...]))):
            o[...] = v

    s64 = jax.ShapeDtypeStruct((16, 64), F32)
    s1k = jax.ShapeDtypeStruct((16, 1024), F32)
    return pl.pallas_call(body, name="s5_prep_bwd", out_shape=[s64, s64, jax.ShapeDtypeStruct((16, 1), F32), s1k, s1k])(
        a_re, a_im, lstep, b_re, b_im, dlr, dli, dbbr, dbbi)


SUB = 8


def _s5_tables(ars, ais):
    def body(ar, ai, tr, ti):
        rr = lax.broadcasted_iota(jnp.int32, (8 * SUB, S5_P), 0)
        seg, r = rr // SUB, rr % SUB
        step = jnp.where((seg == 1) | (seg == 4), 1, jnp.where((seg == 2) | (seg == 5), 2, 4))
        n = jnp.where(seg == 0, r + 1, jnp.where(seg == 7, SUB - r, step))
        fwd_gap = jnp.where(seg <= 3, r - step, SUB - step - 1 - r)
        gap = jnp.where((seg == 0) | (seg == 7), 0, fwd_gap)
        nf = n.astype(F32)
        mag = jnp.where(gap >= 0, jnp.exp(nf * ar[...]), 0.0)
        tr[...] = mag * jnp.cos(nf * ai[...])
        ti[...] = mag * jnp.sin(nf * ai[...])

    sds = jax.ShapeDtypeStruct((8 * SUB, S5_P), F32)
    return pl.pallas_call(body, name="s5_tables", out_shape=[sds] * 2)(ars, ais)


def _s5_table(tb_r, tb_i, k):
    return tb_r[SUB * k:SUB * (k + 1), :], tb_i[SUB * k:SUB * (k + 1), :]


def _s5_scan(bu_r, bu_i, tb_r, tb_i, c_r, c_i, lb):
    nt = lb // SUB
    sr, si = bu_r.reshape(nt, SUB, S5_P), bu_i.reshape(nt, SUB, S5_P)
    for j, k in enumerate((1, 2, 4)):
        mr, mi = _s5_table(tb_r, tb_i, 1 + j)
        tr, ti = pltpu.roll(sr, k, axis=1), pltpu.roll(si, k, axis=1)
        sr, si = sr + mr * tr - mi * ti, si + mr * ti + mi * tr
    pr, pi = _s5_table(tb_r, tb_i, 0)
    out_r, out_i = [], []
    for j in range(nt):
        a_r = sr[j] + pr * c_r - pi * c_i
        a_i = si[j] + pr * c_i + pi * c_r
        out_r.append(a_r)
        out_i.append(a_i)
        c_r, c_i = a_r[SUB - 1:SUB], a_i[SUB - 1:SUB]
    return jnp.concatenate(out_r, axis=0), jnp.concatenate(out_i, axis=0)


def _s5_rscan(g_r, g_i, tb_r, tb_i, n_r, n_i, lb):
    nt = lb // SUB
    gr, gi = g_r.reshape(nt, SUB, S5_P), g_i.reshape(nt, SUB, S5_P)
    for j, k in enumerate((1, 2, 4)):
        mr, mi = _s5_table(tb_r, tb_i, 4 + j)
        tr, ti = pltpu.roll(gr, SUB - k, axis=1), pltpu.roll(gi, SUB - k, axis=1)
        gr, gi = gr + mr * tr + mi * ti, gi + mr * ti - mi * tr
    qr, qi = _s5_table(tb_r, tb_i, 7)
    out_r, out_i = [None] * nt, [None] * nt
    for j in reversed(range(nt)):
        a_r = gr[j] + qr * n_r + qi * n_i
        a_i = gi[j] + qr * n_i - qi * n_r
        out_r[j], out_i[j] = a_r, a_i
        n_r, n_i = a_r[0:1], a_i[0:1]
    return jnp.concatenate(out_r, axis=0), jnp.concatenate(out_i, axis=0)


def _s5_y(u, sr, si, cre, cim, dsk):
    return _bdot(sr, cre) + _bdot(si, cim) + dsk * u


def _f_s5(proj, bmat, cre, cim, p_r, p_i, dsk, glu_w, glu_b):
    t = proj.shape[0]
    lb = _s5_block(t)
    nb = t // lb

    def body(u_ref, bm_ref, cr_ref, ci_ref, pr_ref, pi_ref, dk_ref, gw_ref, gb_ref, y_ref, car_ref, s_ref, st_ref):
        @pl.when(pl.program_id(0) == 0)
        def _():
            st_ref[...] = jnp.zeros_like(st_ref)

        u = u_ref[...]
        bu = _bdot(u, bm_ref[...])
        c_r, c_i = st_ref[0:1, 0:S5_P], st_ref[0:1, S5_P:]
        car_ref[0] = st_ref[0:1, :]
        sr, si = _s5_scan(bu[:, :S5_P], bu[:, S5_P:], pr_ref, pi_ref, c_r, c_i, lb)
        st_ref[0:1, 0:S5_P] = sr[lb - 1:lb]
        st_ref[0:1, S5_P:] = si[lb - 1:lb]
        sr_b, si_b = sr.astype(BF16), si.astype(BF16)
        s_ref[:, 0:S5_P] = sr_b
        s_ref[:, S5_P:] = si_b
        gel = _gelu(_s5_y(u, sr_b, si_b, cr_ref[...], ci_ref[...], dk_ref[...]))
        y_ref[...] = gel * _sig(_bdot(gel, gw_ref[...]) + gb_ref[...])

    return pl.pallas_call(
        body, name="f_s5", grid=(nb,),
        in_specs=[pl.BlockSpec((lb, GW), lambda i: (i, 5)),
                  _full((GW, 2 * S5_P)), _full((S5_P, GW)), _full((S5_P, GW)), _full((8 * SUB, S5_P)), _full((8 * SUB, S5_P)),
                  _row(GW), _full((GW, GW)), _row(GW)],
        out_specs=[pl.BlockSpec((lb, GW), lambda i: (i, 0)), pl.BlockSpec((1, 1, 2 * S5_P), lambda i: (i, 0, 0)),
                   pl.BlockSpec((lb, 2 * S5_P), lambda i: (i, 0))],
        out_shape=[jax.ShapeDtypeStruct((t, GW), F32), jax.ShapeDtypeStruct((nb, 1, 2 * S5_P), F32),
                   jax.ShapeDtypeStruct((t, 2 * S5_P), BF16)],
        scratch_shapes=[pltpu.VMEM((8, 2 * S5_P), F32)], compiler_params=_cparams(1),
    )(proj, bmat, cre, cim, p_r, p_i, dsk, glu_w, glu_b)


def _b_s5(proj, dyd, carries, states, bmat, cre, cim, p_r, p_i, dsk, glu_w, glu_b):
    t = proj.shape[0]
    lb = _s5_block(t)
    nb = t // lb

    def body(u_ref, dy_ref, car_ref, s_ref, bm_ref, cr_ref, ci_ref, pr_ref, pi_ref, dk_ref, gw_ref, gb_ref,
             du_ref, dbm_ref, dcr_ref, dci_ref, dlam_ref, ddk_ref, dgw_ref, dgb_ref, gc_ref):
        @pl.when(pl.program_id(0) == 0)
        def _():
            gc_ref[...] = jnp.zeros_like(gc_ref)
            for r in (dbm_ref, dcr_ref, dci_ref, dlam_ref, ddk_ref, dgw_ref, dgb_ref):
                r[...] = jnp.zeros_like(r)

        u = u_ref[...]
        bm = bm_ref[...]
        u_b = u.astype(BF16)
        c_r, c_i = car_ref[0, 0:1, 0:S5_P], car_ref[0, 0:1, S5_P:]
        cre_v, cim_v, dk, gw = cr_ref[...], ci_ref[...], dk_ref[...], gw_ref[...]
        sr_b, si_b = s_ref[:, 0:S5_P], s_ref[:, S5_P:]
        sr, si = sr_b.astype(F32), si_b.astype(F32)
        y = _dot(sr_b, cre_v) + _dot(si_b, cim_v) + dk * u
        gel = _gelu(y)
        gel_b = gel.astype(BF16)
        gate = _sig(_dot(gel_b, gw) + gb_ref[...])
        dout = dy_ref[...]
        t1 = dout * gel * gate * (1.0 - gate)
        t1_b = t1.astype(BF16)
        dgw_ref[...] += _dot(gel_b, t1_b, TN)
        dgb_ref[...] += _colsum(t1)
        dyv = (dout * gate + _dot(t1_b, gw, NT)) * _dgelu(y)
        dyv_b = dyv.astype(BF16)
        ddk_ref[...] += _colsum(dyv * u)
        dcr_ref[...] += _dot(sr_b, dyv_b, TN)
        dci_ref[...] += _dot(si_b, dyv_b, TN)
        gr = _dot(dyv_b, cre_v, NT)
        gi = _dot(dyv_b, cim_v, NT)
        row = lax.broadcasted_iota(jnp.int32, (lb, S5_P), 0)
        n_r, n_i = gc_ref[0:1, 0:S5_P], gc_ref[0:1, S5_P:]
        gr, gi = _s5_rscan(gr, gi, pr_ref, pi_ref, n_r, n_i, lb)
        gc_ref[0:1, 0:S5_P] = gr[0:1]
        gc_ref[0:1, S5_P:] = gi[0:1]
        gcat = jnp.concatenate([gr, gi], axis=1).astype(BF16)
        dbm_ref[...] += _dot(u_b, gcat, TN)
        du_ref[...] = dyv * dk + _dot(gcat, bm, NT)
        spr = jnp.where(row >= 1, _roll(sr, 1), c_r)
        spi = jnp.where(row >= 1, _roll(si, 1), c_i)
        dlam_ref[0:1, :] += _colsum(gr * spr + gi * spi)
        dlam_ref[1:2, :] += _colsum(gi * spr - gr * spi)

    rev = lambda i: nb - 1 - i
    return pl.pallas_call(
        body, name="b_s5", grid=(nb,),
        in_specs=[pl.BlockSpec((lb, GW), lambda i: (rev(i), 5)), pl.BlockSpec((lb, GW), lambda i: (rev(i), 0)),
                  pl.BlockSpec((1, 1, 2 * S5_P), lambda i: (rev(i), 0, 0)), pl.BlockSpec((lb, 2 * S5_P), lambda i: (rev(i), 0)),
                  _full((GW, 2 * S5_P)), _full((S5_P, GW)), _full((S5_P, GW)), _full((8 * SUB, S5_P)), _full((8 * SUB, S5_P)),
                  _row(GW), _full((GW, GW)), _row(GW)],
        out_specs=[pl.BlockSpec((lb, GW), lambda i: (rev(i), 0)), _full((GW, 2 * S5_P)), _full((S5_P, GW)), _full((S5_P, GW)),
                   _full((2, S5_P)), _row(GW), _full((GW, GW)), _row(GW)],
        out_shape=[jax.ShapeDtypeStruct((t, GW), F32), jax.ShapeDtypeStruct((GW, 2 * S5_P), F32),
                   jax.ShapeDtypeStruct((S5_P, GW), F32), jax.ShapeDtypeStruct((S5_P, GW), F32),
                   jax.ShapeDtypeStruct((2, S5_P), F32), jax.ShapeDtypeStruct((1, GW), F32),
                   jax.ShapeDtypeStruct((GW, GW), F32), jax.ShapeDtypeStruct((1, GW), F32)],
        scratch_shapes=[pltpu.VMEM((8, 2 * S5_P), F32)], compiler_params=_cparams(1),
    )(proj, dyd, carries, states, bmat, cre, cim, p_r, p_i, dsk, glu_w, glu_b)


def _group_norm(ys, bw):
    outs, stats = [], []
    for g, y in enumerate(ys):
        r, n = _rms(y)
        stats.append((r, n))
        outs.append(n * bw[:, GW * g:GW * (g + 1)])
    return jnp.concatenate(outs, axis=1), stats


def _f_out(ya, yb, yc, yd, bw, wts, l, h, g1):
    t = h.shape[0]
    tb = _tblock(t)

    def body(a_ref, b_ref, c_ref, d_ref, bw_ref, w_ref, h_ref, g_ref, h2_ref, o_ref, cat_ref):
        cat, _ = _group_norm([a_ref[...], b_ref[...], c_ref[...], d_ref[...]], bw_ref[...])
        catb = cat.astype(BF16)
        cat_ref[...] = catb
        o = _dot(catb, w_ref[0].reshape(D, D))
        o_ref[...] = o.astype(BF16)
        h2_ref[...] = h_ref[...] + g_ref[...] * o

    yblk = pl.BlockSpec((tb, GW), lambda i: (i, 0))
    blk = pl.BlockSpec((tb, D), lambda i: (i, 0))
    return pl.pallas_call(
        body, name="f_out", grid=(t // tb,), in_specs=[yblk] * 4 + [_row(D), _wout_spec(l), blk, _row(D)],
        out_specs=[blk, blk, blk],
        out_shape=[jax.ShapeDtypeStruct((t, D), F32), jax.ShapeDtypeStruct((t, D), BF16), jax.ShapeDtypeStruct((t, D), BF16)],
        compiler_params=_cparams(1),
    )(ya, yb, yc, yd, bw, wts, h, g1)


def _b_out(dv, h2, dh3, m, nw2, sc2, ya, yb, yc, yd, bw, wts, l, g1):
    t = dv.shape[0]
    tb = _tblock(t)

    def body(dv_ref, h2_ref, dh3_ref, m_ref, nw_ref, sc_ref, a_ref, b_ref, c_ref, d_ref, bw_ref, w_ref, g_ref,
             dh_ref, dsc_ref, dsh_ref, dnw_ref, dg_ref, da_ref, db_ref, dc_ref, dd_ref, do_ref, dbw_ref):
        @pl.when(pl.program_id(0) == 0)
        def _():
            for r in (dsc_ref, dsh_ref, dnw_ref, dg_ref, dbw_ref):
                r[...] = jnp.zeros_like(r)

        _norm_bwd_step(dv_ref[...], h2_ref[...], dh3_ref[...], m_ref[...], nw_ref[...], sc_ref[...],
                       dh_ref, dsc_ref, dsh_ref, dnw_ref, dg_ref)
        do = (dh_ref[...] * g_ref[...]).astype(BF16)
        do_ref[...] = do
        dcat = _dot(do, w_ref[0].reshape(D, D), NT)
        bw_v = bw_ref[...]
        for g, (y_ref, dy_ref) in enumerate(((a_ref, da_ref), (b_ref, db_ref), (c_ref, dc_ref), (d_ref, dd_ref))):
            r, n = _rms(y_ref[...])
            dc = dcat[:, GW * g:GW * (g + 1)]
            dbw_ref[:, GW * g:GW * (g + 1)] += _colsum(dc * n)
            dy_ref[...] = _rms_bwd(r, n, dc * bw_v[:, GW * g:GW * (g + 1)])

    yblk = pl.BlockSpec((tb, GW), lambda i: (i, 0))
    blk = pl.BlockSpec((tb, D), lambda i: (i, 0))
    ysd = jax.ShapeDtypeStruct((t, GW), F32)
    row = jax.ShapeDtypeStruct((1, D), F32)
    return pl.pallas_call(
        body, name="b_out", grid=(t // tb,),
        in_specs=[blk] * 4 + [_row(D), _row(D)] + [yblk] * 4 + [_row(D), _wout_spec(l), _row(D)],
        out_specs=[blk, _row(D), _row(D), _row(D), _row(D)] + [yblk] * 4 + [blk, _row(D)],
        out_shape=[jax.ShapeDtypeStruct((t, D), F32), row, row, row, row] + [ysd] * 4 + [jax.ShapeDtypeStruct((t, D), BF16), row],
        compiler_params=_cparams(1),
    )(dv, h2, dh3, m, nw2, sc2, ya, yb, yc, yd, bw, wts, g1)


HB = 512
MLP_ROWS = 1024


ROW_W1, ROW_W2, ROW_WOUT, ROW_WIN = 0, D, D + HID // 4, D + HID // 4 + D // 4
PACK_ROWS = ROW_WIN + WIN_ROWS


def _w1_spec(l):
    per = HID // 4 // HB
    return pl.BlockSpec((1, 1, D, HB), lambda i, k: (l, k // per, ROW_W1 // D, k % per))


def _w2_spec(l):
    per = HID // 4 // HB
    return pl.BlockSpec((1, 1, HB, D), lambda i, k: (l, k // per, ROW_W2 // HB + k % per, 0))


def _wout_spec(l):
    return pl.BlockSpec((1, 4, D // 4, D), lambda i: (l, 0, ROW_WOUT // (D // 4), 0))


def _f_mlp(h2, nw, sc, sh, g2, wts, l):
    t = h2.shape[0]
    tb = _tblock(t, MLP_ROWS)
    nk = HID // HB

    def body(h_ref, nw_ref, sc_ref, sh_ref, g_ref, w1_ref, w2_ref, h3_ref, m_ref, a_ref, v_ref, acc_ref):
        k = pl.program_id(1)

        @pl.when(k == 0)
        def _():
            _, n = _rms(h_ref[...])
            v_ref[...] = ((n * nw_ref[...]) * (1.0 + sc_ref[...]) + sh_ref[...]).astype(BF16)
            acc_ref[...] = jnp.zeros_like(acc_ref)

        a = _dot(v_ref[...], w1_ref[0, 0])
        a_ref[...] = a.astype(BF16)
        ra = jnp.maximum(a, 0.0)
        acc_ref[...] += _dot((ra * ra).astype(BF16), w2_ref[0, 0])

        @pl.when(k == nk - 1)
        def _():
            m = acc_ref[...]
            m_ref[...] = m.astype(BF16)
            h3_ref[...] = h_ref[...] + g_ref[...] * m

    blk = pl.BlockSpec((tb, D), lambda i, k: (i, 0))
    return pl.pallas_call(
        body, name="f_mlp", grid=(t // tb, nk),
        in_specs=[blk, _row(D), _row(D), _row(D), _row(D), _w1_spec(l), _w2_spec(l)],
        out_specs=[blk, blk, pl.BlockSpec((tb, HB), lambda i, k: (i, k)), blk],
        out_shape=[jax.ShapeDtypeStruct((t, D), F32), jax.ShapeDtypeStruct((t, D), BF16), jax.ShapeDtypeStruct((t, HID), BF16),
                   jax.ShapeDtypeStruct((t, D), BF16)],
        scratch_shapes=[pltpu.VMEM((tb, D), F32)], compiler_params=_cparams(2),
    )(h2, nw, sc, sh, g2, wts, wts)


def _b_mlp(dh3, a, g2, wts, l):
    t = dh3.shape[0]
    tb = _tblock(t, MLP_ROWS)
    nk = HID // HB

    def body(dh_ref, a_ref, g_ref, w1_ref, w2_ref, dv_ref, da_ref, act_ref, dm_ref):
        k = pl.program_id(1)
        dm = (dh_ref[...] * g_ref[...]).astype(BF16)

        @pl.when(k == 0)
        def _():
            dm_ref[...] = dm
            dv_ref[...] = jnp.zeros_like(dv_ref)

        ra = jnp.maximum(a_ref[...].astype(F32), 0.0)
        act_ref[...] = (ra * ra).astype(BF16)
        da = (_dot(dm, w2_ref[0, 0], NT) * (2.0 * ra)).astype(BF16)
        da_ref[...] = da
        dv_ref[...] += _dot(da, w1_ref[0, 0], NT)

    blk = pl.BlockSpec((tb, D), lambda i, k: (i, 0))
    hblk = pl.BlockSpec((tb, HB), lambda i, k: (i, k))
    return pl.pallas_call(
        body, name="b_mlp", grid=(t // tb, nk),
        in_specs=[blk, hblk, _row(D), _w1_spec(l), _w2_spec(l)],
        out_specs=[blk, hblk, hblk, blk],
        out_shape=[jax.ShapeDtypeStruct((t, D), F32), jax.ShapeDtypeStruct((t, HID), BF16), jax.ShapeDtypeStruct((t, HID), BF16),
                   jax.ShapeDtypeStruct((t, D), BF16)],
        compiler_params=_cparams(2),
    )(dh3, a, g2, wts, wts)


def _b_final(h, tgt, fw):
    t = h.shape[0]
    tb = _tblock(t)

    def body(h_ref, t_ref, w_ref, dh_ref, loss_ref, dfw_ref):
        @pl.when(pl.program_id(0) == 0)
        def _():
            loss_ref[...] = jnp.zeros_like(loss_ref)
            dfw_ref[...] = jnp.zeros_like(dfw_ref)

        r, n = _rms(h_ref[...])
        wv = w_ref[...]
        err = n * wv - t_ref[...]
        loss_ref[...] += jnp.sum(err * err, keepdims=True) * (0.5 / D)
        dy = err * (1.0 / D)
        dfw_ref[...] += _colsum(dy * n)
        dh_ref[...] = _rms_bwd(r, n, dy * wv)

    blk = pl.BlockSpec((tb, D), lambda i: (i, 0))
    return pl.pallas_call(
        body, name="b_final", grid=(t // tb,), in_specs=[blk, blk, _row(D)], out_specs=[blk, _row(1), _row(D)],
        out_shape=[jax.ShapeDtypeStruct((t, D), F32), jax.ShapeDtypeStruct((1, 1), F32), jax.ShapeDtypeStruct((1, D), F32)],
        compiler_params=_cparams(1),
    )(h, tgt, fw)


def _eye(n):
    return jnp.eye(n, dtype=F32)


def _pool_embed(pool_w):
    return jnp.einsum('gcd,gk->gckd', pool_w, _eye(4)).reshape(GW, GW)


def _pool_extract(m):
    return jnp.einsum('gcgd->gcd', m.reshape(4, 64, 4, 64))


def _bmat_embed(bb):
    return jnp.einsum('gph,gk->ghkp', bb, _eye(16)).reshape(GW, S5_P)


def _bmat_extract(m):
    return jnp.einsum('ghgp->gph', m.reshape(16, 16, 16, 64))


def _cmat_embed(cc):
    return jnp.einsum('ghp,gk->kpgh', cc, _eye(16)).reshape(S5_P, GW)


def _cmat_extract(m):
    return jnp.einsum('gpgh->ghp', m.reshape(16, 64, 16, 16))


def _pad_lanes(v, n=DTW):
    return jnp.pad(v.reshape(1, -1), ((0, 0), (0, n - v.shape[-1])))


def _w_in_layout(w_in_t):
    w_main = jnp.concatenate([w_in_t[:1280], w_in_t[2052:2308], w_in_t[1280:2048]], axis=0)
    return w_main, jnp.pad(w_in_t[2048:2052], ((0, DTW - 4), (0, 0)))


def _layer_params(p, l, mod, w_in, rest):
    q = {'rest': rest, 'l': l}
    q['mod'] = [mod[k:k + 1] for k in range(6)]
    q['nw1'] = p['norm_mix_w'][l:l + 1]
    q['nw2'] = p['norm_mlp_w'][l:l + 1]
    q['w_main'], q['w_dt'] = _w_in_layout(w_in)
    q['pool_mat'] = _pool_embed(p['pool_w'][l]).astype(BF16)
    q['pool_scale'] = p['pool_scale'][l:l + 1]
    q['sconv_w'] = p['sconv_w'][l]
    q['conv_w'] = p['ssd_conv_w'][l]
    q['conv_b'] = p['ssd_conv_b'][l:l + 1]
    q['dt_bias'] = _pad_lanes(p['ssd_dt_bias'][l])
    q['a_log'] = _pad_lanes(p['ssd_a_log'][l])
    q['ssd_d'] = _pad_lanes(p['ssd_d'][l])
    q['s5_raw'] = (p['s5_a_re'][l], p['s5_a_im'][l], p['s5_log_step'][l].reshape(16, 1),
                   p['s5_b_re'][l].reshape(16, 1024), p['s5_b_im'][l].reshape(16, 1024))
    q['cre'] = _cmat_embed(p['s5_c_re'][l]).astype(BF16)
    q['cim'] = (-_cmat_embed(p['s5_c_im'][l])).astype(BF16)
    q['s5_d'] = p['s5_d'][l:l + 1]
    q['glu_w'] = p['s5_glu_w'][l].astype(BF16)
    q['glu_b'] = p['s5_glu_b'][l:l + 1]
    q['bw'] = p['branch_norm_w'][l:l + 1]
    return q


def _layer_fwd(h, q):
    sh1, sc1, g1, sh2, sc2, g2 = q['mod']
    t = h.shape[0]
    s = {'h': h}
    s['proj'], s['dtp'], s['u'] = _f_in(h, q['nw1'], sc1, sh1, q['w_main'], q['w_dt'])
    s['ya'], s['yb'] = _f_ab(s['proj'], q['pool_mat'], q['pool_scale'], q['sconv_w'])
    s['yc'], s['ypre'], s['sprev'] = _f_ssd(s['proj'], s['dtp'], q['conv_w'], q['conv_b'], q['dt_bias'], q['a_log'], q['ssd_d'])
    lr, li, bbr, bbi, ars, ais = _s5_prep(*q['s5_raw'])
    s['bmat'] = jnp.concatenate([_bmat_embed(bbr.reshape(16, 64, 16)), _bmat_embed(bbi.reshape(16, 64, 16))],
                                axis=1).astype(BF16)
    s['tables'] = _s5_tables(ars.reshape(1, S5_P), ais.reshape(1, S5_P))
    s['yd'], s['carries'], s['states'] = _f_s5(s['proj'], s['bmat'], q['cre'], q['cim'], s['tables'][0], s['tables'][1],
                                  q['s5_d'], q['glu_w'], q['glu_b'])
    q['wts'] = q['rest']((s['ya'], s['yc'], s['yd']))
    s['h2'], s['o'], s['cat'] = _f_out(s['ya'], s['yb'], s['yc'], s['yd'], q['bw'], q['wts'], q['l'], h, g1)
    h3, s['m'], s['a'], s['v'] = _f_mlp(s['h2'], q['nw2'], sc2, sh2, g2, q['wts'], q['l'])
    return h3, s


STACKED = {'mlp_w1': (2, 4, D, HID // 4), 'mlp_w2': (2, HID, D), 'w_out': (2, D, D)}


def _layer_bwd(dh3, q, s, l, stacked, early=None):
    sh1, sc1, g1, sh2, sc2, g2 = q['mod']
    g = {}
    dv, da, act, dm = _b_mlp(dh3, s['a'], g2, q['wts'], l)
    g['mlp_w1'] = _tn_matmul(s['v'], da, "dw1", col_major=True, into=stacked['mlp_w1'], layer=l)
    g['mlp_w2'] = _tn_matmul(act, dm, "dw2", into=stacked['mlp_w2'], layer=l)
    dh2, dsc2, dsh2, dnw2, dg2, dya, dyb, dyc, dyd, do, dbw = _b_out(
        dv, s['h2'], dh3, s['m'], q['nw2'], sc2, s['ya'], s['yb'], s['yc'], s['yd'], q['bw'], q['wts'], l, g1)
    g['w_out'] = _tn_matmul(s['cat'], do, "dwout", into=stacked['w_out'], layer=l)
    g['branch_norm_w'] = dbw[0]
    if early is not None:
        zero = early(g)[0, 0]
        q = dict(q, pool_scale=q['pool_scale'] + zero, conv_b=q['conv_b'] + zero, s5_d=q['s5_d'] + zero)
    dab, dpm, dps, dsw = _b_ab(s['proj'], dya, dyb, q['pool_mat'], q['pool_scale'], q['sconv_w'])
    g['pool_w'] = _pool_extract(dpm)
    g['pool_scale'] = dps[0]
    g['sconv_w'] = dsw
    dz, dxbc, ddt, dcw, dcb, ddtb, dal, ddk = _b_ssd(s['proj'], s['dtp'], s['ypre'], dyc, s['sprev'], q['conv_w'],
                                                     q['conv_b'], q['dt_bias'], q['a_log'], q['ssd_d'])
    g['ssd_conv_w'] = dcw
    g['ssd_conv_b'] = dcb[0]
    g['ssd_dt_bias'] = ddtb[0, :4]
    g['ssd_a_log'] = dal[0, :4]
    g['ssd_d'] = ddk[0, :4]
    tb = s['tables']
    ds5, dbmat, dcre, dcim, dlam, dd5, dgw, dgb = _b_s5(s['proj'], dyd, s['carries'], s['states'], s['bmat'], q['cre'], q['cim'],
                                                        tb[0], tb[1], q['s5_d'], q['glu_w'], q['glu_b'])
    g['s5_c_re'] = _cmat_extract(dcre)
    g['s5_c_im'] = -_cmat_extract(dcim)
    g['s5_d'] = dd5[0]
    g['s5_glu_w'] = dgw
    g['s5_glu_b'] = dgb[0]
    dbbr = _bmat_extract(dbmat[:, :S5_P]).reshape(16, 1024)
    dbbi = _bmat_extract(dbmat[:, S5_P:]).reshape(16, 1024)
    dar, dai, dls, dbr, dbi = _s5_prep_bwd(*q['s5_raw'], dlam[0].reshape(16, 64), dlam[1].reshape(16, 64), dbbr, dbbi)
    g['s5_a_re'], g['s5_a_im'], g['s5_log_step'] = dar, dai, dls[:, 0]
    g['s5_b_re'], g['s5_b_im'] = dbr, dbi
    dh, dsc1, dsh1, dnw1, dg1 = _b_in(dab, dz, dxbc, ds5, ddt, q['w_main'], q['w_dt'], s['h'], dh2, s['o'], q['nw1'], sc1)
    u = s['u']
    head = jnp.concatenate([_tn_matmul(dab, u, "dwin_ab"), _tn_matmul(dz, u, "dwin_z"), _tn_matmul(dxbc, u, "dwin_xbc"),
                            _tn_matmul(ddt, u, "dwin_dt")[:8]], axis=0)
    full = lax.dynamic_update_slice(jnp.zeros((2308, D), F32), head, (0, 0))
    g['w_in'] = lax.dynamic_update_slice(full, _tn_matmul(ds5, u, "dwin_s5"), (2052, 0))
    g['norm_mix_w'] = dnw1[0]
    g['norm_mlp_w'] = dnw2[0]
    dmod = jnp.concatenate([dsh1, dsc1, dg1, dsh2, dsc2, dg2], axis=1)
    return dh, g, dmod


def _local_step(x, tgt, p, mod, w_in_of, rest_of, early=None):
    h = x
    qs, saved = [], []
    for l in range(2):
        qs.append(_layer_params(p, l, mod[l], w_in_of(l), functools.partial(rest_of, l)))
        h, s = _layer_fwd(h, qs[l])
        saved.append(s)
    dh, loss, dfw = _b_final(h, tgt, p['final_norm_w'].reshape(1, D))
    grads = [None, None]
    dmods = [None, None]
    dh, grads[1], dmods[1] = _layer_bwd(dh, qs[1], saved[1], 1, {k: lax.empty(shp, F32) for k, shp in STACKED.items()})
    dh, grads[0], dmods[0] = _layer_bwd(dh, qs[0], saved[0], 0, grads[1], early)
    out = {k: jnp.stack([grads[0][k], grads[1][k]]) for k in grads[0] if k not in STACKED}
    if early is None:
        out.update({k: grads[0][k] for k in STACKED})
    out['final_norm_w'] = dfw[0]
    return loss, dh, out, jnp.concatenate(dmods, axis=0)


def _shard_of(a, axis, k):
    n = a.shape[axis] // 4
    return lax.dynamic_slice_in_dim(a, k * n, n, axis)


def kernel(x, c, norm_mix_w, norm_mlp_w, ada_w, ada_b, w_in, pool_w, pool_scale, sconv_w, ssd_conv_w, ssd_conv_b, ssd_dt_bias, ssd_a_log, ssd_d, s5_a_re, s5_a_im, s5_log_step, s5_b_re, s5_b_im, s5_c_re, s5_c_im, s5_d, s5_glu_w, s5_glu_b, branch_norm_w, w_out, mlp_w1, mlp_w2, final_norm_w, loss_target, m_norm_mix_w, m_norm_mlp_w, m_ada_w, m_ada_b, m_w_in, m_pool_w, m_pool_scale, m_sconv_w, m_ssd_conv_w, m_ssd_conv_b, m_ssd_dt_bias, m_ssd_a_log, m_ssd_d, m_s5_a_re, m_s5_a_im, m_s5_log_step, m_s5_b_re, m_s5_b_im, m_s5_c_re, m_s5_c_im, m_s5_d, m_s5_glu_w, m_s5_glu_b, m_branch_norm_w, m_w_out, m_mlp_w1, m_mlp_w2, m_final_norm_w, v_norm_mix_w, v_norm_mlp_w, v_ada_w, v_ada_b, v_w_in, v_pool_w, v_pool_scale, v_sconv_w, v_ssd_conv_w, v_ssd_conv_b, v_ssd_dt_bias, v_ssd_a_log, v_ssd_d, v_s5_a_re, v_s5_a_im, v_s5_log_step, v_s5_b_re, v_s5_b_im, v_s5_c_re, v_s5_c_im, v_s5_d, v_s5_glu_w, v_s5_glu_b, v_branch_norm_w, v_w_out, v_mlp_w1, v_mlp_w2, v_final_norm_w):
    loc = locals()
    w = {n: loc[n] for n in WEIGHTS}
    mom = {n: loc['m_' + n] for n in WEIGHTS}
    var = {n: loc['v_' + n] for n in WEIGHTS}
    ix, iy, ic = lax.axis_index("x"), lax.axis_index("y"), lax.axis_index("c")
    chip = 2 * ix + iy
    dev = 4 * ix + 2 * iy + ic

    mine_of = lambda a: lax.dynamic_index_in_dim(a.astype(BF16), ic, axis=0, keepdims=False)
    pad_in = lambda a: jnp.pad(a.T, ((0, WIN_ROWS - 577), (0, 0)))
    shard = jnp.concatenate([mine_of(w['mlp_w1']), mine_of(w['mlp_w2']), mine_of(w['w_out']), pad_in(mine_of(w['w_in']))], axis=0)

    (c_all,) = _exchange([c], EVERYONE, False, "ag_cond", stage=True)
    c_all = c_all.reshape(8, D)
    small_sh = _exchange([w[n] for n in SMALL_SHARDED], CHIPS, False, "ag_small")
    (w_in0,) = _exchange([pad_in(w['w_in'][0].astype(BF16))], CHIPS, False, "ag_win0")
    p = {n: w[n] for n in WEIGHTS if n not in BIG}
    for n, g in zip(SMALL_SHARDED, small_sh):
        ax = SMALL_SHARDED[n]
        p[n] = jnp.concatenate([g[k] for k in range(4)], axis=ax)

    def w_in_full(sh):
        return sh[:, :577].reshape(4 * 577, D)

    big = {}

    def fetch(after):
        if not big:
            (mine,), (got,) = _split_wait(sems, shard_thru, land, after, False, "ag_big_wait", per_core=True)
            got = lax.dynamic_update_slice(got, mine[None, None], (ic, chip, 0, 0))
            (both,) = _pair_swap([got.reshape(2, -1, D)], False, "swap_big", fill=True)
            big['both'] = both.reshape(got.shape)
        return big['both']

    def w_in_of(l):
        return w_in_full(w_in0) if l == 0 else w_in_full(fetch(None)[1, :, ROW_WIN:])

    def rest_of(l, after):
        return fetch(after)

    ada_b_sh = _shard_of(w['ada_b'], 1, chip).reshape(2, 1, 6 * D // 4)
    mod_sh = _ada_fwd(c_all, w['ada_w'], ada_b_sh)
    (mod_all,) = _exchange([mod_sh], CHIPS, False, "ag_mod", stage=True)
    mine = lax.dynamic_index_in_dim(mod_all, dev, axis=2, keepdims=False)
    sems, shard_thru, land, token = _split_start([shard], [mod_all, w_in0] + small_sh, False, "ag_big_start", per_core=True)
    mod = jnp.transpose(mine, (1, 0, 2)).reshape(2, 6, D) + token[0, 0]

    layer = ic.astype(jnp.int32).reshape(1)
    flight = {}

    def early(g0):
        gws = [g0['w_out'].reshape(2, 4, 256, D), g0['mlp_w1'], g0['mlp_w2'].reshape(2, 4, 1024, D)]
        got = _pair_swap([a.reshape(2, -1, D) for a in gws], True, "swap_grad", narrow=True)
        pair = [_pair_sum(a, b.reshape(a.shape[1:]), layer, "pair_sum%d" % (k + 1), BF16) for k, (a, b) in enumerate(zip(gws, got))]
        flight['sems'], flight['srcs'], flight['lands'], token = _split_start(pair, [], True, "rs_start")
        return token

    loss, grad_x, g, dmod = _local_step(x[0], loss_target[0], p, mod, w_in_of, rest_of, early)

    (dmod_all,) = _exchange([dmod], EVERYONE, False, "ag_dmod", stage=True)
    dmod_all = jnp.transpose(dmod_all, (1, 0, 2))

    gw_in = jnp.pad(g['w_in'].reshape(2, 4, 577, D), ((0, 0), (0, 0), (0, WIN_ROWS - 577), (0, 0)))
    (got_in,) = _pair_swap([gw_in.reshape(2, -1, D)], True, "swap_grad_in", narrow=True)
    pair_in = _pair_sum(gw_in, got_in.reshape(gw_in.shape[1:]), layer, "pair_sum0", BF16)
    in_sems, in_srcs, in_lands, in_token = _split_start([pair_in], [dmod_all], True, "rs_in_start")

    def chip_sum(land, mine, name):
        own = lax.dynamic_index_in_dim(mine, chip, axis=0, keepdims=True)
        return _sum_lead(lax.dynamic_update_slice(land, own, (chip, 0, 0)), name, F32)

    sent, lands = _split_wait(flight['sems'], flight['srcs'], flight['lands'], [grad_x, in_token], True, "rs_wait")
    quad = [chip_sum(land, mine, "rs_chip_sum%d" % (k + 1)) for k, (land, mine) in enumerate(zip(lands, sent))]
    g_ada_w, g_ada_b = _ada_bwd(c_all, _shard_of(dmod_all, 2, chip), dmod_all)
    adam_ada_w = _adamw(w['ada_w'], g_ada_w, mom['ada_w'], var['ada_w'], "adamw_ada_w")
    (sent_in,), (land_in,) = _split_wait(in_sems, in_srcs, in_lands, quad + [adam_ada_w[0]], True, "rs_in_wait")
    quad = [chip_sum(land_in, sent_in, "rs_chip_sum0")] + quad
    halves = [lax.dynamic_update_slice(lax.empty((2,) + a.shape, F32), a[None], (ic, 0, 0)) for a in quad]
    both = _pair_swap(halves, False, "swap_red", fill=True)
    both[0] = jnp.transpose(both[0][:, :577], (0, 2, 1))
    red = dict(zip(('w_in', 'w_out', 'mlp_w1', 'mlp_w2'), both))
    red['ada_w'] = g_ada_w

    small_names = [n for n in WEIGHTS if n not in BIG and n != 'ada_b']
    pair_parts = _exchange([g[n] for n in small_names] + [loss], SIBLING, False, "ag_smallpair", stage=True)
    narrow = ('pool_w', 's5_b_re', 's5_b_im', 's5_c_re', 's5_c_im', 's5_glu_w')
    pair_dtypes = [BF16 if n in narrow else F32 for n in small_names] + [F32]
    chip_parts = _exchange(_sum_many(pair_parts, "smallpair_sum", pair_dtypes), CHIPS, False, "ag_smallgrad", stage=True)
    summed = _sum_many(chip_parts, "smallgrad_sum")
    for n, a in zip(small_names, summed[:-1]):
        a = a.reshape(w[n].shape) if n in ('s5_b_re', 's5_b_im') else a
        red[n] = _shard_of(a, SMALL_SHARDED[n], chip) if n in SMALL_SHARDED else a
    red['ada_b'] = g_ada_b
    loss_out = summed[-1].reshape(())

    delta, new_m, new_v = {}, {}, {}
    delta['ada_w'], new_m['ada_w'], new_v['ada_w'] = adam_ada_w
    for n in BIG[1:]:
        delta[n], new_m[n], new_v[n] = _adamw(w[n], red[n], mom[n], var[n], "adamw_" + n)
    rest = [n for n in WEIGHTS if n not in BIG]
    lanes = lambda n, a: a.reshape(2, 16, 1024) if n in ('s5_b_re', 's5_b_im') else a
    outs = _adamw_many(*[[lanes(n, src[n]) for n in rest] for src in (w, red, mom, var)], "adamw_small")
    for k, n in enumerate(rest):
        delta[n], new_m[n], new_v[n] = (outs[3 * k + j].reshape(w[n].shape) for j in range(3))

    return (loss_out, grad_x[None], *[red[n] for n in WEIGHTS], *[delta[n] for n in WEIGHTS],
            *[new_m[n] for n in WEIGHTS], *[new_v[n] for n in WEIGHTS])
```

```python
import functools
import math

import jax
import jax.numpy as jnp
from jax import lax
from jax.experimental import pallas as pl
from jax.experimental.pallas import tpu as pltpu

F32 = jnp.float32
BF16 = jnp.bfloat16
HI = lax.Precision.HIGHEST

D = 1024
GW = 256
HID = 4096
EPS = 1e-6
PW = 2304
DTW = 128
SSD_L = 128
SSD_SUB = 2
SSD_SUB_BWD = 2
NH, HP, NS = 4, 64, 128
S5_P = 1024
MESH = pl.DeviceIdType.MESH

ADAM_LR, ADAM_B1, ADAM_B2, ADAM_EPS, ADAM_WD, ADAM_STEP = 0.001, 0.9, 0.999, 1e-08, 0.01, 10

NT = (((1,), (1,)), ((), ()))
TN = (((0,), (0,)), ((), ()))

WEIGHTS = ['norm_mix_w', 'norm_mlp_w', 'ada_w', 'ada_b', 'w_in', 'pool_w', 'pool_scale', 'sconv_w', 'ssd_conv_w',
           'ssd_conv_b', 'ssd_dt_bias', 'ssd_a_log', 'ssd_d', 's5_a_re', 's5_a_im', 's5_log_step', 's5_b_re', 's5_b_im',
           's5_c_re', 's5_c_im', 's5_d', 's5_glu_w', 's5_glu_b', 'branch_norm_w', 'w_out', 'mlp_w1', 'mlp_w2',
           'final_norm_w']
BIG = ('ada_w', 'w_in', 'w_out', 'mlp_w1', 'mlp_w2')
SMALL_SHARDED = {'sconv_w': 2, 'ssd_conv_w': 2, 's5_glu_w': 1}


def _cparams(n_axes, vmem_mb=48):
    return pltpu.CompilerParams(dimension_semantics=("arbitrary",) * n_axes, vmem_limit_bytes=vmem_mb * 1024 * 1024)


def _row(n):
    return pl.BlockSpec((1, n), lambda *_: (0, 0))


def _full(shape):
    nd = len(shape)
    return pl.BlockSpec(tuple(shape), lambda *_: (0,) * nd)


def _dot(a, b, dims=None, prec=None):
    if dims is None:
        dims = (((a.ndim - 1,), (0,)), ((), ()))
    return lax.dot_general(a, b, dims, preferred_element_type=F32, precision=prec)


def _bdot(a, b, dims=None):
    return _dot(a.astype(BF16), b.astype(BF16), dims)


def _sig(x):
    return jax.nn.sigmoid(x)


def _silu(x):
    return x * _sig(x)


def _dsilu(x):
    s = _sig(x)
    return s * (1.0 + x * (1.0 - s))


def _softplus(x):
    return jnp.maximum(x, 0.0) + jnp.log(1.0 + jnp.exp(-jnp.abs(x)))


_GK = math.sqrt(2.0 / math.pi)


def _gelu(x):
    return 0.5 * x * (1.0 + jnp.tanh(_GK * (x + 0.044715 * x * x * x)))


def _dgelu(x):
    th = jnp.tanh(_GK * (x + 0.044715 * x * x * x))
    return 0.5 * (1.0 + th) + 0.5 * x * (1.0 - th * th) * _GK * (1.0 + 3.0 * 0.044715 * x * x)


def _colsum(x):
    return jnp.sum(x, axis=0, keepdims=True)


def _rms(x):
    r = lax.rsqrt(jnp.mean(x * x, axis=-1, keepdims=True) + EPS)
    return r, x * r


def _rms_bwd(r, n, dn):
    return r * (dn - n * jnp.mean(dn * n, axis=-1, keepdims=True))


def _roll(x, k):
    n = x.shape[0]
    k = k % n
    return x if k == 0 else pltpu.roll(x, k, axis=0)


def _tblock(t, want=512):
    return min(t, want)


def _peer(mask):
    x, y, c = lax.axis_index("x"), lax.axis_index("y"), lax.axis_index("c")
    return (x ^ ((mask >> 2) & 1), y ^ ((mask >> 1) & 1), c ^ (mask & 1))


def _group_index(masks):
    x, y, c = lax.axis_index("x"), lax.axis_index("y"), lax.axis_index("c")
    full = 0
    for m in masks:
        full |= m
    bits = [b for b in (4, 2, 1) if full & b]

    def idx(px, py, pc):
        v = {4: px, 2: py, 1: pc}
        out = 0
        for b in bits:
            out = out * 2 + v[b]
        return out

    return idx(x, y, c), [idx(*_peer(m)) for m in masks]


def _exchange(arrs, masks, scatter, name, stage=False):
    n_arr, n_peer, n_grp = len(arrs), len(masks), len(masks) + 1

    def body(*refs):
        ins, outs = refs[:n_arr], refs[n_arr:2 * n_arr]
        send_sems, recv_sems, local_sems = refs[2 * n_arr:2 * n_arr + 3]
        if stage:
            bufs, load_sems = refs[2 * n_arr + 3:3 * n_arr + 3], refs[3 * n_arr + 3]
            loads = [pltpu.make_async_copy(ins[t], bufs[t], load_sems.at[t]) for t in range(n_arr)]
            for ld in loads:
                ld.start()
            for ld in loads:
                ld.wait()
            ins = bufs
        me, peer_idx = _group_index(masks)
        copies = []
        for t in range(n_arr):
            src_me = ins[t].at[me] if scatter else ins[t]
            loc = pltpu.make_async_copy(src_me, outs[t].at[me], local_sems.at[t])
            loc.start()
            copies.append(loc)
            for j, m in enumerate(masks):
                src = ins[t].at[peer_idx[j]] if scatter else ins[t]
                cp = pltpu.make_async_remote_copy(src_ref=src, dst_ref=outs[t].at[me], send_sem=send_sems.at[t, j],
                                                  recv_sem=recv_sems.at[t, j], device_id=_peer(m), device_id_type=MESH)
                cp.start()
                copies.append(cp)
        for cp in copies:
            cp.wait()

    hbm = pl.BlockSpec(memory_space=pl.ANY)
    out_shape = [jax.ShapeDtypeStruct((n_grp,) + (a.shape[1:] if scatter else a.shape), a.dtype) for a in arrs]
    staging = [pltpu.VMEM(a.shape, a.dtype) for a in arrs] + [pltpu.SemaphoreType.DMA((n_arr,))] if stage else []
    outs = pl.pallas_call(
        body, name=name, in_specs=[hbm] * n_arr, out_specs=[hbm] * n_arr, out_shape=out_shape,
        scratch_shapes=[pltpu.SemaphoreType.DMA((n_arr, n_peer)), pltpu.SemaphoreType.DMA((n_arr, n_peer)),
                        pltpu.SemaphoreType.DMA((n_arr,))] + staging,
        compiler_params=pltpu.CompilerParams(vmem_limit_bytes=48 * 1024 * 1024),
    )(*arrs)
    return list(outs)


def _split_copies(src_refs, land_refs, sems, scatter, per_core):
    me, peer_idx = _group_index(CHIPS)
    n = len(CHIPS) * len(src_refs)
    copies = []
    for t, (src_ref, land_ref) in enumerate(zip(src_refs, land_refs)):
        zone = land_ref.at[lax.axis_index("c")] if per_core else land_ref
        for j, m in enumerate(CHIPS):
            k = len(CHIPS) * t + j
            copies.append(pltpu.make_async_remote_copy(
                src_ref=src_ref.at[peer_idx[j]] if scatter else src_ref, dst_ref=zone.at[me], send_sem=sems[k],
                recv_sem=sems[n + k], device_id=_peer(m), device_id_type=MESH))
    return copies


def _split_start(srcs, after, scatter, name, per_core=False):
    n_arr, n_sem = len(srcs), 2 * len(CHIPS) * len(srcs)

    def body(*refs):
        src_refs, land_refs = refs[:n_arr], refs[n_arr:2 * n_arr]
        outs = refs[2 * n_arr + len(after):]
        for cp in _split_copies(src_refs, land_refs, outs[:n_sem], scatter, per_core):
            cp.start()
        outs[-1][...] = jnp.zeros_like(outs[-1])

    hbm = pl.BlockSpec(memory_space=pltpu.HBM)
    sem = pl.BlockSpec(memory_space=pltpu.SEMAPHORE)
    lands = [lax.empty(((2,) if per_core else ()) + (len(CHIPS) + 1,) + (a.shape[1:] if scatter else a.shape), a.dtype)
             for a in srcs]
    as_hbm = lambda a: pltpu.with_memory_space_constraint(a, pltpu.HBM)
    outs = pl.pallas_call(
        body, name=name,
        out_shape=(pltpu.SemaphoreType.DMA(()),) * n_sem + tuple(pltpu.HBM(a.shape, a.dtype) for a in srcs + lands)
        + (jax.ShapeDtypeStruct((8, 128), F32),),
        in_specs=(hbm,) * (2 * n_arr) + (pl.BlockSpec(memory_space=pl.ANY),) * len(after),
        out_specs=(sem,) * n_sem + (hbm,) * (2 * n_arr) + (pl.BlockSpec(memory_space=pltpu.VMEM),),
        input_output_aliases={t: n_sem + t for t in range(2 * n_arr)},
        compiler_params=pltpu.CompilerParams(has_side_effects=pltpu.SideEffectType.DATAFLOW_SIDE_EFFECTING),
    )(*[as_hbm(a) for a in srcs + lands], *after)
    return outs[:n_sem], list(outs[n_sem:n_sem + n_arr]), list(outs[n_sem + n_arr:n_sem + 2 * n_arr]), outs[-1]


def _split_wait(sems, srcs, lands, after, scatter, name, per_core=False):
    n_arr, n_sem = len(srcs), len(sems)

    def body(*refs):
        src_refs, land_refs = refs[:n_arr], refs[n_arr:2 * n_arr]
        for cp in _split_copies(src_refs, land_refs, refs[2 * n_arr:2 * n_arr + n_sem], scatter, per_core):
            cp.wait_send()
            cp.wait_recv()

    hbm = pl.BlockSpec(memory_space=pltpu.HBM)
    sem = pl.BlockSpec(memory_space=pltpu.SEMAPHORE)
    outs = pl.pallas_call(
        body, name=name, out_shape=tuple(pltpu.HBM(a.shape, a.dtype) for a in srcs + lands),
        in_specs=(hbm,) * (2 * n_arr) + (sem,) * n_sem + (pl.BlockSpec(memory_space=pl.ANY),) * len(after),
        out_specs=(hbm,) * (2 * n_arr), input_output_aliases={t: t for t in range(2 * n_arr)},
        compiler_params=pltpu.CompilerParams(has_side_effects=pltpu.SideEffectType.DATAFLOW_SIDE_EFFECTING),
    )(*srcs, *lands, *sems, *after)
    return list(outs[:n_arr]), list(outs[n_arr:])


CHIPS = (4, 2, 6)
EVERYONE = (1, 2, 3, 4, 5, 6, 7)
SIBLING = (1,)
SWAP_ROWS = 1024
WIN_ROWS = 592


def _pair_swap(arrs, other_layer, name, narrow=False, fill=False):
    assert not (fill and (other_layer or narrow))
    n_arr = len(arrs)
    shapes = [a.shape[-2:] for a in arrs]
    out_dtypes = [BF16 if narrow else a.dtype for a in arrs]
    chunks = []
    for t, (rows, _) in enumerate(shapes):
        assert rows % 16 == 0
        for j, r0 in enumerate(range(0, rows, SWAP_ROWS)):
            chunks.append((t, r0, min(SWAP_ROWS, rows - r0), j % 2))

    def body(*refs):
        ins, outs = refs[:n_arr], refs[n_arr:2 * n_arr]
        bufs = refs[2 * n_arr:3 * n_arr]
        out_bufs = refs[3 * n_arr:4 * n_arr] if narrow else bufs
        load_sems, send_sems, recv_sems = refs[-3:]
        sibling = _peer(1)
        c = lax.axis_index("c")

        def load(k):
            t, r0, n, slot = chunks[k]
            src = ins[t].at[1 - c] if other_layer else ins[t].at[c] if fill else ins[t]
            return pltpu.make_async_copy(src.at[pl.ds(r0, n)], bufs[t].at[slot, pl.ds(0, n)], load_sems.at[t, slot])

        def send(k):
            t, r0, n, slot = chunks[k]
            dst = outs[t].at[c] if fill else outs[t]
            return pltpu.make_async_remote_copy(src_ref=out_bufs[t].at[slot, pl.ds(0, n)], dst_ref=dst.at[pl.ds(r0, n)],
                                                send_sem=send_sems.at[t, slot], recv_sem=recv_sems.at[t],
                                                device_id=sibling, device_id_type=MESH)

        in_flight = {}

        def drain(k):
            key = (chunks[k][0], chunks[k][3])
            if key in in_flight:
                send(in_flight.pop(key)).wait_send()

        def start_load(k):
            if not narrow:
                drain(k)
            load(k).start()

        start_load(0)
        for k in range(len(chunks)):
            t, _, n, slot = chunks[k]
            load(k).wait()
            if k + 1 < len(chunks):
                start_load(k + 1)
            if narrow:
                drain(k)
                out_bufs[t][slot, pl.ds(0, n), :] = bufs[t][slot, pl.ds(0, n), :].astype(BF16)
            send(k).start()
            in_flight[(t, slot)] = k
        for k in in_flight.values():
            send(k).wait_send()
        for t in range(n_arr):
            landed = outs[t].at[1 - c] if fill else outs[t]
            pltpu.make_async_remote_copy(src_ref=landed, dst_ref=landed, send_sem=send_sems.at[t, 0],
                                         recv_sem=recv_sems.at[t], device_id=sibling, device_id_type=MESH).wait_recv()

    hbm = pl.BlockSpec(memory_space=pl.ANY)
    outs = pl.pallas_call(
        body, name=name, in_specs=[hbm] * n_arr, out_specs=[hbm] * n_arr,
        out_shape=[jax.ShapeDtypeStruct(a.shape if fill else s, dt) for a, s, dt in zip(arrs, shapes, out_dtypes)],
        input_output_aliases={t: t for t in range(n_arr)} if fill else {},
        scratch_shapes=[pltpu.VMEM((2, min(SWAP_ROWS, s[0]), s[1]), a.dtype) for s, a in zip(shapes, arrs)]
        + ([pltpu.VMEM((2, min(SWAP_ROWS, s[0]), s[1]), BF16) for s in shapes] if narrow else [])
        + [pltpu.SemaphoreType.DMA((n_arr, 2)), pltpu.SemaphoreType.DMA((n_arr, 2)), pltpu.SemaphoreType.DMA((n_arr,))],
        compiler_params=pltpu.CompilerParams(vmem_limit_bytes=48 * 1024 * 1024),
    )(*arrs)
    return list(outs)


def _sum_lead(a, name, out_dtype):
    n = a.shape[0]
    shape = a.shape[1:]

    def body(a_ref, o_ref):
        acc = a_ref[0].astype(F32)
        for k in range(1, n):
            acc = acc + a_ref[k].astype(F32)
        o_ref[...] = acc.astype(out_dtype)

    if len(shape) == 3:
        blk = (1,) + shape[1:]
        return pl.pallas_call(
            body, name=name, grid=(shape[0],), in_specs=[pl.BlockSpec((n,) + blk, lambda i: (0, i, 0, 0))],
            out_specs=pl.BlockSpec(blk, lambda i: (i, 0, 0)), out_shape=jax.ShapeDtypeStruct(shape, out_dtype),
            compiler_params=_cparams(1),
        )(a)
    rows, cols = shape
    rb = rows
    for cand in (512, 256, 128):
        if rows % cand == 0 and rows > cand:
            rb = cand
            break
    return pl.pallas_call(
        body, name=name, grid=(rows // rb,), in_specs=[pl.BlockSpec((n, rb, cols), lambda i: (0, i, 0))],
        out_specs=pl.BlockSpec((rb, cols), lambda i: (i, 0)), out_shape=jax.ShapeDtypeStruct((rows, cols), out_dtype),
        compiler_params=_cparams(1),
    )(a)


def _pair_sum(g, recv, layer, name, out_dtype):
    _, n, r, c = g.shape

    def body(l_ref, g_ref, r_ref, o_ref):
        o_ref[...] = (g_ref[0].astype(F32) + r_ref[...].astype(F32)).astype(out_dtype)

    return pl.pallas_call(
        body, name=name,
        grid_spec=pltpu.PrefetchScalarGridSpec(
            num_scalar_prefetch=1, grid=(n,),
            in_specs=[pl.BlockSpec((1, 1, r, c), lambda i, l: (l[0], i, 0, 0)), pl.BlockSpec((1, r, c), lambda i, l: (i, 0, 0))],
            out_specs=pl.BlockSpec((1, r, c), lambda i, l: (i, 0, 0))),
        out_shape=jax.ShapeDtypeStruct((n, r, c), out_dtype), compiler_params=_cparams(1),
    )(layer, g, recv)


def _tn_matmul(a, b, name, col_major=False, into=None, layer=0):
    t, k = a.shape
    n = b.shape[1]
    tb = _tblock(t, 1024)
    kb = min(k, 1024)
    nb = min(n, 1024)
    grid = (k // kb, n // nb, t // tb)
    lead = (into is not None) + col_major

    def body(a_ref, b_ref, *rest):
        o_ref = rest[-1]
        for _ in range(lead):
            o_ref = o_ref.at[0]

        @pl.when(pl.program_id(2) == 0)
        def _():
            o_ref[...] = jnp.zeros_like(o_ref)

        o_ref[...] += _bdot(a_ref[...], b_ref[...], TN)

    if col_major:
        block, index, shape = (1, kb, nb), (lambda ki, ni: (ni, ki, 0)), (n // nb, k, nb)
    else:
        block, index, shape = (kb, nb), (lambda ki, ni: (ki, ni)), (k, n)
    in_specs = [pl.BlockSpec((tb, kb), lambda ki, ni, ti: (ti, ki)), pl.BlockSpec((tb, nb), lambda ki, ni, ti: (ti, ni))]
    if into is None:
        return pl.pallas_call(
            body, name=name, grid=grid, in_specs=in_specs, out_specs=pl.BlockSpec(block, lambda ki, ni, ti: index(ki, ni)),
            out_shape=jax.ShapeDtypeStruct(shape, F32), compiler_params=_cparams(3),
        )(a, b)
    assert into.shape == (2,) + shape
    return pl.pallas_call(
        body, name=name, grid=grid, in_specs=in_specs + [pl.BlockSpec(memory_space=pl.ANY)],
        out_specs=pl.BlockSpec((1,) + block, lambda ki, ni, ti: (layer,) + index(ki, ni)),
        out_shape=jax.ShapeDtypeStruct(into.shape, F32), input_output_aliases={2: 0}, compiler_params=_cparams(3),
    )(a, b, into)


def _sum_many(arrs, name, out_dtypes=None):
    k = len(arrs)
    out_dtypes = out_dtypes or [F32] * k

    def body(*refs):
        for a_ref, o_ref in zip(refs[:k], refs[k:]):
            acc = a_ref[0].astype(F32)
            for j in range(1, a_ref.shape[0]):
                acc = acc + a_ref[j].astype(F32)
            o_ref[...] = acc.astype(o_ref.dtype)

    return pl.pallas_call(body, name=name, grid=(1,), in_specs=[_full(a.shape) for a in arrs],
                          out_specs=[_full(a.shape[1:]) for a in arrs],
                          out_shape=[jax.ShapeDtypeStruct(a.shape[1:], dt) for a, dt in zip(arrs, out_dtypes)],
                          compiler_params=_cparams(1))(*arrs)


def _adamw_math(w, g, m, v):
    m2 = ADAM_B1 * m + (1.0 - ADAM_B1) * g
    v2 = ADAM_B2 * v + (1.0 - ADAM_B2) * (g * g)
    m_hat = m2 / (1.0 - ADAM_B1 ** ADAM_STEP)
    v_hat = v2 / (1.0 - ADAM_B2 ** ADAM_STEP)
    return -ADAM_LR * (m_hat / (jnp.sqrt(v_hat) + ADAM_EPS) + ADAM_WD * w), m2, v2


def _adamw_many(ws, gs, ms, vs, name):
    n = len(ws)

    def body(*refs):
        ins, outs = refs[:4 * n], refs[4 * n:]
        for k in range(n):
            res = _adamw_math(ins[k][...], ins[n + k][...], ins[2 * n + k][...], ins[3 * n + k][...])
            for j in range(3):
                outs[3 * k + j][...] = res[j]

    out_shape = []
    for a in ws:
        out_shape += [jax.ShapeDtypeStruct(a.shape, F32)] * 3
    return pl.pallas_call(body, name=name, grid=(1,), in_specs=[_full(a.shape) for a in ws] * 4,
                          out_specs=[_full(s.shape) for s in out_shape], out_shape=out_shape,
                          compiler_params=_cparams(1))(*ws, *gs, *ms, *vs)


def _adamw(w, g, m, v, name):
    shape = w.shape
    cols = shape[-1]
    rows = int(math.prod(shape[:-1]))
    rb = rows
    for cand in (256, 128, 64, 32, 16, 8):
        if rows % cand == 0 and rows > cand:
            rb = cand
            break
    bc1 = 1.0 - ADAM_B1 ** ADAM_STEP
    bc2 = 1.0 - ADAM_B2 ** ADAM_STEP

    def body(w_ref, g_ref, m_ref, v_ref, d_ref, nm_ref, nv_ref):
        gg = g_ref[...]
        m2 = ADAM_B1 * m_ref[...] + (1.0 - ADAM_B1) * gg
        v2 = ADAM_B2 * v_ref[...] + (1.0 - ADAM_B2) * (gg * gg)
        m_hat = m2 / bc1
        v_hat = v2 / bc2
        d_ref[...] = -ADAM_LR * (m_hat / (jnp.sqrt(v_hat) + ADAM_EPS) + ADAM_WD * w_ref[...])
        nm_ref[...] = m2
        nv_ref[...] = v2

    spec = pl.BlockSpec((rb, cols), lambda i: (i, 0))
    sds = jax.ShapeDtypeStruct((rows, cols), F32)
    outs = pl.pallas_call(
        body, name=name, grid=(rows // rb,), in_specs=[spec] * 4, out_specs=[spec] * 3, out_shape=[sds] * 3,
        compiler_params=_cparams(1),
    )(*(z.reshape(rows, cols) for z in (w, g, m, v)))
    return tuple(o.reshape(shape) for o in outs)


def _ada_fwd(c_all, ada_w_sh, ada_b_sh):
    s = ada_w_sh.shape[2]
    sb = 512

    def body(c_ref, w_ref, b_ref, o_ref):
        cond = _silu(c_ref[...])
        o_ref[0] = _bdot(cond, w_ref[0]) + b_ref[0]

    return pl.pallas_call(
        body, name="ada_fwd", grid=(2, s // sb),
        in_specs=[_full((8, D)), pl.BlockSpec((1, D, sb), lambda l, j: (l, 0, j)), pl.BlockSpec((1, 1, sb), lambda l, j: (l, 0, j))],
        out_specs=pl.BlockSpec((1, 8, sb), lambda l, j: (l, 0, j)), out_shape=jax.ShapeDtypeStruct((2, 8, s), F32),
        compiler_params=_cparams(2),
    )(c_all, ada_w_sh, ada_b_sh)


def _ada_bwd(c_all, dmod_sh, dmod_all):
    s = dmod_sh.shape[2]
    sb = 512

    def body(c_ref, d_ref, o_ref):
        cond = _silu(c_ref[...])
        o_ref[0] = _bdot(cond, d_ref[0], TN)

    gw = pl.pallas_call(
        body, name="ada_bwd_w", grid=(2, s // sb),
        in_specs=[_full((8, D)), pl.BlockSpec((1, 8, sb), lambda l, j: (l, 0, j))],
        out_specs=pl.BlockSpec((1, D, sb), lambda l, j: (l, 0, j)), out_shape=jax.ShapeDtypeStruct((2, D, s), F32),
        compiler_params=_cparams(2),
    )(c_all, dmod_sh)

    def body_b(d_ref, o_ref):
        acc = d_ref[0, 0:1, :]
        for k in range(1, 8):
            acc = acc + d_ref[0, k:k + 1, :]
        o_ref[0] = acc

    gb = pl.pallas_call(
        body_b, name="ada_bwd_b", grid=(2,), in_specs=[pl.BlockSpec((1, 8, 6 * D), lambda l: (l, 0, 0))],
        out_specs=pl.BlockSpec((1, 1, 6 * D), lambda l: (l, 0, 0)), out_shape=jax.ShapeDtypeStruct((2, 1, 6 * D), F32),
        compiler_params=_cparams(1),
    )(dmod_all)
    return gw, gb.reshape(2, 6 * D)


def _f_in(h, nw, sc, sh, w_main, w_dt):
    t = h.shape[0]
    tb = _tblock(t)

    def body(h_ref, nw_ref, sc_ref, sh_ref, w_ref, wd_ref, p_ref, dt_ref, u_ref):
        _, n = _rms(h_ref[...])
        u = ((n * nw_ref[...]) * (1.0 + sc_ref[...]) + sh_ref[...]).astype(BF16)
        u_ref[...] = u
        p_ref[...] = _dot(u, w_ref[...], NT)
        dt_ref[...] = _dot(u, wd_ref[...], NT)

    return pl.pallas_call(
        body, name="f_in", grid=(t // tb,),
        in_specs=[pl.BlockSpec((tb, D), lambda i: (i, 0)), _row(D), _row(D), _row(D), _full((PW, D)), _full((DTW, D))],
        out_specs=[pl.BlockSpec((tb, PW), lambda i: (i, 0)), pl.BlockSpec((tb, DTW), lambda i: (i, 0)),
                   pl.BlockSpec((tb, D), lambda i: (i, 0))],
        out_shape=[jax.ShapeDtypeStruct((t, PW), F32), jax.ShapeDtypeStruct((t, DTW), F32), jax.ShapeDtypeStruct((t, D), BF16)],
        compiler_params=_cparams(1),
    )(h, nw, sc, sh, w_main, w_dt)


def _norm_bwd_step(du_v, x, dres_v, gated, nwv, scv, dx_ref, dsc_ref, dsh_ref, dnw_ref, dg_ref):
    r, n = _rms(x)
    scale = 1.0 + scv
    dsc_ref[...] += _colsum(du_v * (n * nwv))
    dsh_ref[...] += _colsum(du_v)
    dnw_ref[...] += _colsum(du_v * scale * n)
    dg_ref[...] += _colsum(dres_v * gated)
    dx_ref[...] = dres_v + _rms_bwd(r, n, du_v * scale * nwv)


def _b_in(dab, dz, dxbc, ds5, ddt, w_main, w_dt, x, dres, gated, nw, sc):
    t = dab.shape[0]
    tb = _tblock(t)

    def body(a_ref, z_ref, x_ref, s_ref, d_ref, w_ref, wd_ref, h_ref, dr_ref, g_ref, nw_ref, sc_ref,
             dx_ref, dsc_ref, dsh_ref, dnw_ref, dg_ref):
        @pl.when(pl.program_id(0) == 0)
        def _():
            for r in (dsc_ref, dsh_ref, dnw_ref, dg_ref):
                r[...] = jnp.zeros_like(r)

        du = _bdot(a_ref[...], w_ref[0:1024, :])
        du += _bdot(z_ref[...], w_ref[1024:1280, :])
        du += _bdot(s_ref[...], w_ref[1280:1536, :])
        du += _bdot(x_ref[...], w_ref[1536:2304, :])
        du += _bdot(d_ref[...], wd_ref[...])
        _norm_bwd_step(du, h_ref[...], dr_ref[...], g_ref[...], nw_ref[...], sc_ref[...], dx_ref, dsc_ref, dsh_ref, dnw_ref, dg_ref)

    blk = lambda n: pl.BlockSpec((tb, n), lambda i: (i, 0))
    row = jax.ShapeDtypeStruct((1, D), F32)
    return pl.pallas_call(
        body, name="b_in", grid=(t // tb,),
        in_specs=[blk(1024), blk(256), blk(768), blk(256), blk(DTW), _full((PW, D)), _full((DTW, D)),
                  blk(D), blk(D), blk(D), _row(D), _row(D)],
        out_specs=[blk(D), _row(D), _row(D), _row(D), _row(D)],
        out_shape=[jax.ShapeDtypeStruct((t, D), F32), row, row, row, row], compiler_params=_cparams(1),
    )(dab, dz, dxbc, ds5, ddt, w_main, w_dt, x, dres, gated, nw, sc)


HALO = 16


def _lane_group(shape):
    return lax.broadcasted_iota(jnp.int32, shape, 1) // 64


def _window_select(g, s2, s4, s8, s16):
    return jnp.where(g == 0, s2, jnp.where(g == 1, s4, jnp.where(g == 2, s8, s16)))


def _pool_count(t0, rows):
    g = _lane_group((rows, GW))
    win = _window_select(g, 2, 4, 8, 16)
    tt = t0 + lax.broadcasted_iota(jnp.int32, (rows, GW), 0)
    return jnp.minimum(tt + 1, win).astype(F32)


def _pool_p(v_ext, t0, tb):
    s2 = v_ext + _roll(v_ext, 1)
    s4 = s2 + _roll(s2, 2)
    s8 = s4 + _roll(s4, 4)
    s16 = s8 + _roll(s8, 8)
    ws = _window_select(_lane_group(v_ext.shape), s2, s4, s8, s16)[HALO:]
    return ws / _pool_count(t0, tb) - v_ext[HALO:]


def _sconv(q_ext, w):
    return (_roll(q_ext, 2) * w[0:1] + _roll(q_ext, 1) * w[1:2] + q_ext * w[2:3])[HALO:]


def _halo_specs(t, tb, cols, col_block):
    per = tb // HALO
    last = t // HALO - 1
    prev = pl.BlockSpec((HALO, cols), lambda i: (jnp.maximum(i * per - 1, 0), col_block))
    nxt = pl.BlockSpec((HALO, cols), lambda i: (jnp.minimum((i + 1) * per, last), col_block))
    return prev, nxt


def _f_ab(proj, pool_mat, pool_scale, sconv_w):
    t = proj.shape[0]
    tb = _tblock(t)
    prev, _ = _halo_specs(t, tb, 1024, 0)

    def body(p_ref, h_ref, pm_ref, ps_ref, sw_ref, ya_ref, yb_ref):
        i = pl.program_id(0)
        halo = jnp.where(i > 0, h_ref[...], 0.0)
        ext = jnp.concatenate([halo, p_ref[...]], axis=0)
        p = _pool_p(ext[:, 0:256], i * tb, tb)
        ya_ref[...] = _bdot(p, pm_ref[...]) * ps_ref[...]
        q_ext = ext[:, 512:768] * ext[:, 768:1024]
        yb_ref[...] = p_ref[:, 256:512] * _sconv(q_ext, sw_ref[...])

    blk = pl.BlockSpec((tb, GW), lambda i: (i, 0))
    sds = jax.ShapeDtypeStruct((t, GW), F32)
    return pl.pallas_call(
        body, name="f_ab", grid=(t // tb,),
        in_specs=[pl.BlockSpec((tb, 1024), lambda i: (i, 0)), prev, _full((GW, GW)), _row(GW), _full((3, GW))],
        out_specs=[blk, blk], out_shape=[sds, sds], compiler_params=_cparams(1),
    )(proj, proj, pool_mat, pool_scale, sconv_w)


def _b_ab(proj, dya, dyb, pool_mat, pool_scale, sconv_w):
    t = proj.shape[0]
    tb = _tblock(t)
    nb = t // tb
    prev, nxt = _halo_specs(t, tb, 1024, 0)
    _, nxt_g = _halo_specs(t, tb, GW, 0)
    n_ext = tb + HALO

    def body(p_ref, hp_ref, hn_ref, da_ref, dan_ref, db_ref, dbn_ref, pm_ref, ps_ref, sw_ref,
             o_ref, dpm_ref, dps_ref, dsw_ref):
        i = pl.program_id(0)

        @pl.when(i == 0)
        def _():
            for r in (dpm_ref, dps_ref, dsw_ref):
                r[...] = jnp.zeros_like(r)

        last = i == nb - 1
        halo = jnp.where(i > 0, hp_ref[...], 0.0)
        main = p_ref[...]
        ext = jnp.concatenate([halo, main], axis=0)
        scale = ps_ref[...]
        pm = pm_ref[...]
        p = _pool_p(ext[:, 0:256], i * tb, tb)
        da = da_ref[...]
        dps_ref[...] += _colsum(da * _bdot(p, pm))
        da_ext = jnp.concatenate([da, jnp.where(last, 0.0, dan_ref[...])], axis=0)
        dys = da_ext * scale
        dpm_ref[...] += _bdot(p, dys[:tb], TN)
        dp = _bdot(dys, pm, NT)
        dpc = dp / _pool_count(i * tb, n_ext)
        a2 = dpc + _roll(dpc, n_ext - 1)
        a4 = a2 + _roll(a2, n_ext - 2)
        a8 = a4 + _roll(a4, n_ext - 4)
        a16 = a8 + _roll(a8, n_ext - 8)
        o_ref[:, 0:256] = (_window_select(_lane_group(dpc.shape), a2, a4, a8, a16) - dp)[:tb]
        w = sw_ref[...]
        gb, gc, hh = main[:, 256:512], main[:, 512:768], main[:, 768:1024]
        q_ext = ext[:, 512:768] * ext[:, 768:1024]
        db = db_ref[...]
        o_ref[:, 256:512] = db * _sconv(q_ext, w)
        gb_next = hn_ref[:, 256:512]
        dconv = jnp.concatenate([db * gb, jnp.where(last, 0.0, dbn_ref[...] * gb_next)], axis=0)
        dq = (dconv * w[2:3] + _roll(dconv, n_ext - 1) * w[1:2] + _roll(dconv, n_ext - 2) * w[0:1])[:tb]
        o_ref[:, 512:768] = dq * hh
        o_ref[:, 768:1024] = dq * gc
        dc = dconv[:tb]
        dsw_ref[0:1, :] += _colsum(dc * _roll(q_ext, 2)[HALO:])
        dsw_ref[1:2, :] += _colsum(dc * _roll(q_ext, 1)[HALO:])
        dsw_ref[2:3, :] += _colsum(dc * q_ext[HALO:])

    blk = pl.BlockSpec((tb, GW), lambda i: (i, 0))
    return pl.pallas_call(
        body, name="b_ab", grid=(nb,),
        in_specs=[pl.BlockSpec((tb, 1024), lambda i: (i, 0)), prev, nxt, blk, nxt_g, blk, nxt_g,
                  _full((GW, GW)), _row(GW), _full((3, GW))],
        out_specs=[pl.BlockSpec((tb, 1024), lambda i: (i, 0)), _full((GW, GW)), _row(GW), _full((3, GW))],
        out_shape=[jax.ShapeDtypeStruct((t, 1024), F32), jax.ShapeDtypeStruct((GW, GW), F32),
                   jax.ShapeDtypeStruct((1, GW), F32), jax.ShapeDtypeStruct((3, GW), F32)],
        compiler_params=_cparams(1),
    )(proj, proj, proj, dya, dya, dyb, dyb, pool_mat, pool_scale, sconv_w)


CH = 8


def _ssd_conv(x, halo, w, b):
    ext = jnp.concatenate([halo, x], axis=0)
    pre = ext * w[3:4] + _roll(ext, 1) * w[2:3] + _roll(ext, 2) * w[1:2] + _roll(ext, 3) * w[0:1] + b
    return pre[CH:], ext


def _ssd_common(dt_raw, dtb, alog):
    ll = dt_raw.shape[0]
    dtv = _softplus(dt_raw + dtb)
    a_row = -jnp.exp(alog)
    r = lax.broadcasted_iota(jnp.int32, (ll, ll), 0)
    c = lax.broadcasted_iota(jnp.int32, (ll, ll), 1)
    tril = (r >= c).astype(F32)
    cs = _dot(tril, dtv * a_row, prec=HI)
    return dtv, a_row, cs, cs.T, r >= c


def _bd(a, b, ca, cb):
    return lax.dot_general(a, b, (((ca,), (cb,)), ((0,), (0,))), preferred_element_type=F32)


def _head_cols(m):
    return jnp.stack([m[:, h:h + 1] for h in range(NH)])


def _ssd_heads(act, dtv, cs, cs_t, causal):
    xs = jnp.stack([act[:, HP * h:HP * (h + 1)] for h in range(NH)])
    bm = jnp.stack([act[:, 256 + NS * (h // 2):256 + NS * (h // 2 + 1)] for h in range(NH)])
    cm = jnp.stack([act[:, 512 + NS * (h // 2):512 + NS * (h // 2 + 1)] for h in range(NH)])
    cs_c = _head_cols(cs)
    cs_r = jnp.stack([cs_t[h:h + 1, :] for h in range(NH)])
    mdec = jnp.where(causal[None], jnp.exp(jnp.minimum(cs_c - cs_r, 0.0)), 0.0)
    g2 = _bd(jnp.stack([cm[0], cm[2]]), jnp.stack([bm[0], bm[2]]), 2, 2)
    sc = jnp.stack([g2[h // 2] for h in range(NH)]) * mdec
    dt_c = _head_cols(dtv)
    xdt = xs * dt_c
    e = jnp.exp(cs_c)
    cs_last = cs_c[:, SSD_L - 1:SSD_L, :]
    wdec = jnp.exp(cs_last - cs_c)
    return xs, bm, cm, mdec, sc, dt_c, xdt, e, cs_last, wdec


def _head_scalars(row_ref):
    return jnp.stack([row_ref[0:1, h:h + 1] for h in range(NH)])


def _f_ssd(proj, dtp, conv_w, conv_b, dt_bias, a_log, d_skip):
    t = proj.shape[0]
    nc = t // SSD_L
    rows = SSD_SUB * SSD_L
    per = rows // CH

    def body(x_ref, hx_ref, dt_ref, z_ref, cw_ref, cb_ref, dtb_ref, al_ref, dk_ref, y_ref, yp_ref, sp_ref, s_ref):
        i = pl.program_id(0)

        @pl.when(i == 0)
        def _():
            s_ref[...] = jnp.zeros_like(s_ref)

        state = s_ref[...]
        dk = _head_scalars(dk_ref)
        for sub in range(SSD_SUB):
            r0 = sub * SSD_L
            rs = slice(r0, r0 + SSD_L)
            halo = jnp.where(i > 0, hx_ref[...], 0.0) if sub == 0 else x_ref[r0 - CH:r0, :]
            pre, _ = _ssd_conv(x_ref[rs, :], halo, cw_ref[...], cb_ref[...])
            act = _silu(pre)
            dtv, _, cs, cs_t, causal = _ssd_common(dt_ref[rs, :], dtb_ref[...], al_ref[...])
            xs, bm, cm, _, sc, _, xdt, e, cs_last, wdec = _ssd_heads(act, dtv, cs, cs_t, causal)
            sp_ref[sub] = state
            y = _bd(sc, xdt, 2, 1) + e * _bd(cm, state, 2, 2) + xs * dk
            for h in range(NH):
                yp_ref[rs, HP * h:HP * (h + 1)] = y[h]
            state = state * jnp.exp(cs_last) + _bd(xdt * wdec, bm, 1, 1)
            y_ref[rs, :] = yp_ref[rs, :] * _silu(z_ref[rs, :])
        s_ref[...] = state

    blk = pl.BlockSpec((rows, GW), lambda i: (i, 0))
    sds = jax.ShapeDtypeStruct((t, GW), F32)
    return pl.pallas_call(
        body, name="f_ssd", grid=(nc // SSD_SUB,),
        in_specs=[pl.BlockSpec((rows, 768), lambda i: (i, 2)),
                  pl.BlockSpec((CH, 768), lambda i: (jnp.maximum(i * per - 1, 0), 2)),
                  pl.BlockSpec((rows, DTW), lambda i: (i, 0)),
                  pl.BlockSpec((rows, GW), lambda i: (i, 4)),
                  _full((4, 768)), _row(768), _row(DTW), _row(DTW), _row(DTW)],
        out_specs=[blk, blk, pl.BlockSpec((SSD_SUB, NH, HP, NS), lambda i: (i, 0, 0, 0))],
        out_shape=[sds, sds, jax.ShapeDtypeStruct((nc, NH, HP, NS), F32)],
        scratch_shapes=[pltpu.VMEM((NH, HP, NS), F32)], compiler_params=_cparams(1),
    )(proj, proj, dtp, proj, conv_w, conv_b, dt_bias, a_log, d_skip)


def _b_ssd(proj, dtp, ypre, dyc, sprev, conv_w, conv_b, dt_bias, a_log, d_skip):
    t = proj.shape[0]
    nc = t // SSD_L
    steps = nc // SSD_SUB_BWD
    rows = SSD_SUB_BWD * SSD_L
    per = rows // CH
    n_ext = SSD_L + CH

    def chunk(sub, halo, dnext, ds_in, refs):
        (x_ref, dt_ref, z_ref, yp_ref, dy_ref, sp_ref, cw_ref, cb_ref, dtb_ref, al_ref, dk_ref,
         dz_ref, dx_ref, ddt_ref, dact_ref) = refs
        rs = slice(sub * SSD_L, (sub + 1) * SSD_L)
        dact = dact_ref.at[sub]
        w = cw_ref[...]
        pre, ext = _ssd_conv(x_ref[rs, :], halo, w, cb_ref[...])
        act = _silu(pre)
        dt_raw = dt_ref[rs, :]
        dtv, a_row, cs, cs_t, causal = _ssd_common(dt_raw, dtb_ref[...], al_ref[...])
        z = z_ref[rs, :]
        dyc_v = dy_ref[rs, :]
        dz_ref[rs, :] = dyc_v * yp_ref[rs, :] * _dsilu(z)
        dy_all = dyc_v * _silu(z)
        lane = lax.broadcasted_iota(jnp.int32, (SSD_L, DTW), 1)
        rowi = lax.broadcasted_iota(jnp.int32, (1, SSD_L, 1), 1)
        lane1 = lax.broadcasted_iota(jnp.int32, (1, DTW), 1)
        xs, bm, cm, mdec, sc, dt_c, xdt, e, cs_last, wdec = _ssd_heads(act, dtv, cs, cs_t, causal)
        dy = jnp.stack([dy_all[:, HP * h:HP * (h + 1)] for h in range(NH)])
        prev = sp_ref[sub]
        ds = ds_in
        lsum = lambda v: jnp.sum(v, axis=2, keepdims=True)
        dsc = _bd(dy, xdt, 2, 2)
        q = dsc * sc
        dg = dsc * mdec
        dxdt = _bd(sc, dy, 1, 1)
        dcs = lsum(q) - lsum(jnp.swapaxes(q, 1, 2))
        dc = _bd(dg, bm, 2, 1)
        db = _bd(dg, cm, 1, 1)
        cp = _bd(cm, prev, 2, 2)
        dcs += lsum(dy * cp) * e
        ey = e * dy
        dc += _bd(ey, prev, 2, 1)
        dprev = _bd(ey, cm, 1, 1)
        elast = jnp.exp(cs_last)
        dprev += ds * elast
        dcs_last = jnp.sum(lsum(ds * prev), axis=1, keepdims=True) * elast
        bds = _bd(bm, ds, 2, 2)
        dxdt += wdec * bds
        db += wdec * _bd(xdt, ds, 2, 1)
        dw = lsum(xdt * bds) * wdec
        dcs -= dw
        dcs_last += jnp.sum(dw, axis=1, keepdims=True)
        dcs += jnp.where(rowi == SSD_L - 1, dcs_last, 0.0)
        dxs = dxdt * dt_c + dy * _head_scalars(dk_ref)
        ddtx = lsum(dxdt * xs)
        ddk = jnp.sum(lsum(dy * xs), axis=1, keepdims=True)
        dcs_mat = jnp.zeros((SSD_L, DTW), F32)
        ddtx_mat = jnp.zeros((SSD_L, DTW), F32)
        ddk_row = jnp.zeros((1, DTW), F32)
        for h in range(NH):
            dact[:, HP * h:HP * (h + 1)] = dxs[h]
            dcs_mat = jnp.where(lane == h, dcs[h], dcs_mat)
            ddtx_mat = jnp.where(lane == h, ddtx[h], ddtx_mat)
            ddk_row = jnp.where(lane1 == h, ddk[h], ddk_row)
        for g in range(2):
            dact[:, 256 + NS * g:256 + NS * (g + 1)] = db[2 * g] + db[2 * g + 1]
            dact[:, 512 + NS * g:512 + NS * (g + 1)] = dc[2 * g] + dc[2 * g + 1]
        ds_out = dprev
        r2 = lax.broadcasted_iota(jnp.int32, (SSD_L, SSD_L), 0)
        c2 = lax.broadcasted_iota(jnp.int32, (SSD_L, SSD_L), 1)
        dadt = _dot((c2 >= r2).astype(F32), dcs_mat, prec=HI)
        ddt = jnp.where(lane < NH, (dadt * a_row + ddtx_mat) * _sig(dt_raw + dtb_ref[...]), 0.0)
        ddt_ref[rs, :] = ddt
        dpre = dact[...] * _dsilu(pre)
        dcw = jnp.concatenate([_colsum(dpre * _roll(ext, 3 - k)[CH:]) for k in range(4)], axis=0)
        dext = jnp.concatenate([dpre, dnext], axis=0)
        dx_ref[rs, :] = (dext * w[3:4] + _roll(dext, n_ext - 1) * w[2:3] + _roll(dext, n_ext - 2) * w[1:2]
                         + _roll(dext, n_ext - 3) * w[0:1])[:SSD_L]
        acc = (dcw, _colsum(dpre), _colsum(ddt), _colsum(dadt * dtv) * a_row, ddk_row)
        return dpre[0:CH], ds_out, acc

    def body(x_ref, hx_ref, dt_ref, z_ref, yp_ref, dy_ref, sp_ref, cw_ref, cb_ref, dtb_ref, al_ref, dk_ref,
             dz_ref, dx_ref, ddt_ref, dcw_ref, dcb_ref, ddtb_ref, dal_ref, ddk_ref, ds_ref, dnext_ref, dact_ref):
        i = pl.program_id(0)
        acc_refs = (dcw_ref, dcb_ref, ddtb_ref, dal_ref, ddk_ref)

        @pl.when(i == 0)
        def _():
            ds_ref[...] = jnp.zeros_like(ds_ref)
            dnext_ref[...] = jnp.zeros_like(dnext_ref)
            for r in acc_refs:
                r[...] = jnp.zeros_like(r)

        refs = (x_ref, dt_ref, z_ref, yp_ref, dy_ref, sp_ref, cw_ref, cb_ref, dtb_ref, al_ref, dk_ref, dz_ref, dx_ref, ddt_ref,
                dact_ref)
        ds = ds_ref[...]
        dnext = dnext_ref[...]
        total = None
        for sub in reversed(range(SSD_SUB_BWD)):
            if sub == 0:
                halo = jnp.where(i == steps - 1, 0.0, hx_ref[...])
            else:
                halo = x_ref[sub * SSD_L - CH:sub * SSD_L, :]
            dnext, ds, acc = chunk(sub, halo, dnext, ds, refs)
            total = acc if total is None else tuple(a + b for a, b in zip(total, acc))
        ds_ref[...] = ds
        dnext_ref[...] = dnext
        for r, v in zip(acc_refs, total):
            r[...] += v

    rev = lambda i: steps - 1 - i
    blk = lambda n, cb=0: pl.BlockSpec((rows, n), lambda i: (rev(i), cb))
    row = lambda n: jax.ShapeDtypeStruct((1, n), F32)
    return pl.pallas_call(
        body, name="b_ssd", grid=(steps,),
        in_specs=[blk(768, 2), pl.BlockSpec((CH, 768), lambda i: (jnp.maximum(rev(i) * per - 1, 0), 2)),
                  blk(DTW), blk(GW, 4), blk(GW), blk(GW), pl.BlockSpec((SSD_SUB_BWD, NH, HP, NS), lambda i: (rev(i), 0, 0, 0)),
                  _full((4, 768)), _row(768), _row(DTW), _row(DTW), _row(DTW)],
        out_specs=[blk(GW), blk(768), blk(DTW), _full((4, 768)), _row(768), _row(DTW), _row(DTW), _row(DTW)],
        out_shape=[jax.ShapeDtypeStruct((t, GW), F32), jax.ShapeDtypeStruct((t, 768), F32), jax.ShapeDtypeStruct((t, DTW), F32),
                   jax.ShapeDtypeStruct((4, 768), F32), row(768), row(DTW), row(DTW), row(DTW)],
        scratch_shapes=[pltpu.VMEM((NH, HP, NS), F32), pltpu.VMEM((CH, 768), F32), pltpu.VMEM((SSD_SUB_BWD, SSD_L, 768), F32)],
        compiler_params=_cparams(1),
    )(proj, proj, dtp, proj, ypre, dyc, sprev, conv_w, conv_b, dt_bias, a_log, d_skip)


def _s5_block(t):
    return min(t, 1024)


def _seg_t():
    r = lax.broadcasted_iota(jnp.int32, (64, 1024), 0)
    c = lax.broadcasted_iota(jnp.int32, (64, 1024), 1)
    return (c // 16 == r).astype(F32)


def _s5_prep_math(a_re, a_im, lstep, b_re, b_im):
    step = jnp.exp(lstep)
    ars = a_re * step
    ais = a_im * step
    mag = jnp.exp(ars)
    lr = mag * jnp.cos(ais)
    li = mag * jnp.sin(ais)
    den = a_re * a_re + a_im * a_im
    nr = lr - 1.0
    f_re = (nr * a_re + li * a_im) / den
    f_im = (li * a_re - nr * a_im) / den
    seg = _seg_t()
    fr = _dot(f_re, seg, prec=HI)
    fi = _dot(f_im, seg, prec=HI)
    return lr, li, fr * b_re - fi * b_im, fr * b_im + fi * b_re, ars, ais


def _s5_prep(a_re, a_im, lstep, b_re, b_im):
    def body(ar, ai, ls, br, bi, lr_o, li_o, bbr_o, bbi_o, ars_o, ais_o):
        outs = _s5_prep_math(ar[...], ai[...], ls[...], br[...], bi[...])
        for o, v in zip((lr_o, li_o, bbr_o, bbi_o, ars_o, ais_o), outs):
            o[...] = v

    s64 = jax.ShapeDtypeStruct((16, 64), F32)
    s1k = jax.ShapeDtypeStruct((16, 1024), F32)
    return pl.pallas_call(body, name="s5_prep", out_shape=[s64, s64, s1k, s1k, s64, s64])(a_re, a_im, lstep, b_re, b_im)


def _s5_prep_bwd(a_re, a_im, lstep, b_re, b_im, dlr, dli, dbbr, dbbi):
    def body(ar, ai, ls, br, bi, g0, g1, g2, g3, o0, o1, o2, o3, o4):
        f = lambda *a: _s5_prep_math(*a)[:4]
        _, vjp = jax.vjp(f, ar[...], ai[...], ls[...], br[...], bi[...])
        for o, v in zip((o0, o1, o2, o3, o4), vjp((g0[...], g1[...], g2[...], g3[...]))):
            o[...] = v

    s64 = jax.ShapeDtypeStruct((16, 64), F32)
    s1k = jax.ShapeDtypeStruct((16, 1024), F32)
    return pl.pallas_call(body, name="s5_prep_bwd", out_shape=[s64, s64, jax.ShapeDtypeStruct((16, 1), F32), s1k, s1k])(
        a_re, a_im, lstep, b_re, b_im, dlr, dli, dbbr, dbbi)


SUB = 8


def _s5_tables(ars, ais):
    def body(ar, ai, tr, ti):
        rr = lax.broadcasted_iota(jnp.int32, (8 * SUB, S5_P), 0)
        seg, r = rr // SUB, rr % SUB
        step = jnp.where((seg == 1) | (seg == 4), 1, jnp.where((seg == 2) | (seg == 5), 2, 4))
        n = jnp.where(seg == 0, r + 1, jnp.where(seg == 7, SUB - r, step))
        fwd_gap = jnp.where(seg <= 3, r - step, SUB - step - 1 - r)
        gap = jnp.where((seg == 0) | (seg == 7), 0, fwd_gap)
        nf = n.astype(F32)
        mag = jnp.where(gap >= 0, jnp.exp(nf * ar[...]), 0.0)
        tr[...] = mag * jnp.cos(nf * ai[...])
        ti[...] = mag * jnp.sin(nf * ai[...])

    sds = jax.ShapeDtypeStruct((8 * SUB, S5_P), F32)
    return pl.pallas_call(body, name="s5_tables", out_shape=[sds] * 2)(ars, ais)


def _s5_table(tb_r, tb_i, k):
    return tb_r[SUB * k:SUB * (k + 1), :], tb_i[SUB * k:SUB * (k + 1), :]


def _s5_scan(bu_r, bu_i, tb_r, tb_i, c_r, c_i, lb):
    nt = lb // SUB
    sr, si = bu_r.reshape(nt, SUB, S5_P), bu_i.reshape(nt, SUB, S5_P)
    for j, k in enumerate((1, 2, 4)):
        mr, mi = _s5_table(tb_r, tb_i, 1 + j)
        tr, ti = pltpu.roll(sr, k, axis=1), pltpu.roll(si, k, axis=1)
        sr, si = sr + mr * tr - mi * ti, si + mr * ti + mi * tr
    pr, pi = _s5_table(tb_r, tb_i, 0)
    out_r, out_i = [], []
    for j in range(nt):
        a_r = sr[j] + pr * c_r - pi * c_i
        a_i = si[j] + pr * c_i + pi * c_r
        out_r.append(a_r)
        out_i.append(a_i)
        c_r, c_i = a_r[SUB - 1:SUB], a_i[SUB - 1:SUB]
    return jnp.concatenate(out_r, axis=0), jnp.concatenate(out_i, axis=0)


def _s5_rscan(g_r, g_i, tb_r, tb_i, n_r, n_i, lb):
    nt = lb // SUB
    gr, gi = g_r.reshape(nt, SUB, S5_P), g_i.reshape(nt, SUB, S5_P)
    for j, k in enumerate((1, 2, 4)):
        mr, mi = _s5_table(tb_r, tb_i, 4 + j)
        tr, ti = pltpu.roll(gr, SUB - k, axis=1), pltpu.roll(gi, SUB - k, axis=1)
        gr, gi = gr + mr * tr + mi * ti, gi + mr * ti - mi * tr
    qr, qi = _s5_table(tb_r, tb_i, 7)
    out_r, out_i = [None] * nt, [None] * nt
    for j in reversed(range(nt)):
        a_r = gr[j] + qr * n_r + qi * n_i
        a_i = gi[j] + qr * n_i - qi * n_r
        out_r[j], out_i[j] = a_r, a_i
        n_r, n_i = a_r[0:1], a_i[0:1]
    return jnp.concatenate(out_r, axis=0), jnp.concatenate(out_i, axis=0)


def _s5_y(u, sr, si, cre, cim, dsk):
    return _bdot(sr, cre) + _bdot(si, cim) + dsk * u


def _f_s5(proj, bmat, cre, cim, p_r, p_i, dsk, glu_w, glu_b):
    t = proj.shape[0]
    lb = _s5_block(t)
    nb = t // lb

    def body(u_ref, bm_ref, cr_ref, ci_ref, pr_ref, pi_ref, dk_ref, gw_ref, gb_ref, y_ref, car_ref, s_ref, st_ref):
        @pl.when(pl.program_id(0) == 0)
        def _():
            st_ref[...] = jnp.zeros_like(st_ref)

        u = u_ref[...]
        bu = _bdot(u, bm_ref[...])
        c_r, c_i = st_ref[0:1, 0:S5_P], st_ref[0:1, S5_P:]
        car_ref[0] = st_ref[0:1, :]
        sr, si = _s5_scan(bu[:, :S5_P], bu[:, S5_P:], pr_ref, pi_ref, c_r, c_i, lb)
        st_ref[0:1, 0:S5_P] = sr[lb - 1:lb]
        st_ref[0:1, S5_P:] = si[lb - 1:lb]
        sr_b, si_b = sr.astype(BF16), si.astype(BF16)
        s_ref[:, 0:S5_P] = sr_b
        s_ref[:, S5_P:] = si_b
        gel = _gelu(_s5_y(u, sr_b, si_b, cr_ref[...], ci_ref[...], dk_ref[...]))
        y_ref[...] = gel * _sig(_bdot(gel, gw_ref[...]) + gb_ref[...])

    return pl.pallas_call(
        body, name="f_s5", grid=(nb,),
        in_specs=[pl.BlockSpec((lb, GW), lambda i: (i, 5)),
                  _full((GW, 2 * S5_P)), _full((S5_P, GW)), _full((S5_P, GW)), _full((8 * SUB, S5_P)), _full((8 * SUB, S5_P)),
                  _row(GW), _full((GW, GW)), _row(GW)],
        out_specs=[pl.BlockSpec((lb, GW), lambda i: (i, 0)), pl.BlockSpec((1, 1, 2 * S5_P), lambda i: (i, 0, 0)),
                   pl.BlockSpec((lb, 2 * S5_P), lambda i: (i, 0))],
        out_shape=[jax.ShapeDtypeStruct((t, GW), F32), jax.ShapeDtypeStruct((nb, 1, 2 * S5_P), F32),
                   jax.ShapeDtypeStruct((t, 2 * S5_P), BF16)],
        scratch_shapes=[pltpu.VMEM((8, 2 * S5_P), F32)], compiler_params=_cparams(1),
    )(proj, bmat, cre, cim, p_r, p_i, dsk, glu_w, glu_b)


def _b_s5(proj, dyd, carries, states, bmat, cre, cim, p_r, p_i, dsk, glu_w, glu_b):
    t = proj.shape[0]
    lb = _s5_block(t)
    nb = t // lb

    def body(u_ref, dy_ref, car_ref, s_ref, bm_ref, cr_ref, ci_ref, pr_ref, pi_ref, dk_ref, gw_ref, gb_ref,
             du_ref, dbm_ref, dcr_ref, dci_ref, dlam_ref, ddk_ref, dgw_ref, dgb_ref, gc_ref):
        @pl.when(pl.program_id(0) == 0)
        def _():
            gc_ref[...] = jnp.zeros_like(gc_ref)
            for r in (dbm_ref, dcr_ref, dci_ref, dlam_ref, ddk_ref, dgw_ref, dgb_ref):
                r[...] = jnp.zeros_like(r)

        u = u_ref[...]
        bm = bm_ref[...]
        u_b = u.astype(BF16)
        c_r, c_i = car_ref[0, 0:1, 0:S5_P], car_ref[0, 0:1, S5_P:]
        cre_v, cim_v, dk, gw = cr_ref[...], ci_ref[...], dk_ref[...], gw_ref[...]
        sr_b, si_b = s_ref[:, 0:S5_P], s_ref[:, S5_P:]
        sr, si = sr_b.astype(F32), si_b.astype(F32)
        y = _dot(sr_b, cre_v) + _dot(si_b, cim_v) + dk * u
        gel = _gelu(y)
        gel_b = gel.astype(BF16)
        gate = _sig(_dot(gel_b, gw) + gb_ref[...])
        dout = dy_ref[...]
        t1 = dout * gel * gate * (1.0 - gate)
        t1_b = t1.astype(BF16)
        dgw_ref[...] += _dot(gel_b, t1_b, TN)
        dgb_ref[...] += _colsum(t1)
        dyv = (dout * gate + _dot(t1_b, gw, NT)) * _dgelu(y)
        dyv_b = dyv.astype(BF16)
        ddk_ref[...] += _colsum(dyv * u)
        dcr_ref[...] += _dot(sr_b, dyv_b, TN)
        dci_ref[...] += _dot(si_b, dyv_b, TN)
        gr = _dot(dyv_b, cre_v, NT)
        gi = _dot(dyv_b, cim_v, NT)
        row = lax.broadcasted_iota(jnp.int32, (lb, S5_P), 0)
        n_r, n_i = gc_ref[0:1, 0:S5_P], gc_ref[0:1, S5_P:]
        gr, gi = _s5_rscan(gr, gi, pr_ref, pi_ref, n_r, n_i, lb)
        gc_ref[0:1, 0:S5_P] = gr[0:1]
        gc_ref[0:1, S5_P:] = gi[0:1]
        gcat = jnp.concatenate([gr, gi], axis=1).astype(BF16)
        dbm_ref[...] += _dot(u_b, gcat, TN)
        du_ref[...] = dyv * dk + _dot(gcat, bm, NT)
        spr = jnp.where(row >= 1, _roll(sr, 1), c_r)
        spi = jnp.where(row >= 1, _roll(si, 1), c_i)
        dlam_ref[0:1, :] += _colsum(gr * spr + gi * spi)
        dlam_ref[1:2, :] += _colsum(gi * spr - gr * spi)

    rev = lambda i: nb - 1 - i
    return pl.pallas_call(
        body, name="b_s5", grid=(nb,),
        in_specs=[pl.BlockSpec((lb, GW), lambda i: (rev(i), 5)), pl.BlockSpec((lb, GW), lambda i: (rev(i), 0)),
                  pl.BlockSpec((1, 1, 2 * S5_P), lambda i: (rev(i), 0, 0)), pl.BlockSpec((lb, 2 * S5_P), lambda i: (rev(i), 0)),
                  _full((GW, 2 * S5_P)), _full((S5_P, GW)), _full((S5_P, GW)), _full((8 * SUB, S5_P)), _full((8 * SUB, S5_P)),
                  _row(GW), _full((GW, GW)), _row(GW)],
        out_specs=[pl.BlockSpec((lb, GW), lambda i: (rev(i), 0)), _full((GW, 2 * S5_P)), _full((S5_P, GW)), _full((S5_P, GW)),
                   _full((2, S5_P)), _row(GW), _full((GW, GW)), _row(GW)],
        out_shape=[jax.ShapeDtypeStruct((t, GW), F32), jax.ShapeDtypeStruct((GW, 2 * S5_P), F32),
                   jax.ShapeDtypeStruct((S5_P, GW), F32), jax.ShapeDtypeStruct((S5_P, GW), F32),
                   jax.ShapeDtypeStruct((2, S5_P), F32), jax.ShapeDtypeStruct((1, GW), F32),
                   jax.ShapeDtypeStruct((GW, GW), F32), jax.ShapeDtypeStruct((1, GW), F32)],
        scratch_shapes=[pltpu.VMEM((8, 2 * S5_P), F32)], compiler_params=_cparams(1),
    )(proj, dyd, carries, states, bmat, cre, cim, p_r, p_i, dsk, glu_w, glu_b)


def _group_norm(ys, bw):
    outs, stats = [], []
    for g, y in enumerate(ys):
        r, n = _rms(y)
        stats.append((r, n))
        outs.append(n * bw[:, GW * g:GW * (g + 1)])
    return jnp.concatenate(outs, axis=1), stats


def _f_out(ya, yb, yc, yd, bw, wts, l, h, g1):
    t = h.shape[0]
    tb = _tblock(t)

    def body(a_ref, b_ref, c_ref, d_ref, bw_ref, w_ref, h_ref, g_ref, h2_ref, o_ref, cat_ref):
        cat, _ = _group_norm([a_ref[...], b_ref[...], c_ref[...], d_ref[...]], bw_ref[...])
        catb = cat.astype(BF16)
        cat_ref[...] = catb
        o = _dot(catb, w_ref[0].reshape(D, D))
        o_ref[...] = o.astype(BF16)
        h2_ref[...] = h_ref[...] + g_ref[...] * o

    yblk = pl.BlockSpec((tb, GW), lambda i: (i, 0))
    blk = pl.BlockSpec((tb, D), lambda i: (i, 0))
    return pl.pallas_call(
        body, name="f_out", grid=(t // tb,), in_specs=[yblk] * 4 + [_row(D), _wout_spec(l), blk, _row(D)],
        out_specs=[blk, blk, blk],
        out_shape=[jax.ShapeDtypeStruct((t, D), F32), jax.ShapeDtypeStruct((t, D), BF16), jax.ShapeDtypeStruct((t, D), BF16)],
        compiler_params=_cparams(1),
    )(ya, yb, yc, yd, bw, wts, h, g1)


def _b_out(dv, h2, dh3, m, nw2, sc2, ya, yb, yc, yd, bw, wts, l, g1):
    t = dv.shape[0]
    tb = _tblock(t)

    def body(dv_ref, h2_ref, dh3_ref, m_ref, nw_ref, sc_ref, a_ref, b_ref, c_ref, d_ref, bw_ref, w_ref, g_ref,
             dh_ref, dsc_ref, dsh_ref, dnw_ref, dg_ref, da_ref, db_ref, dc_ref, dd_ref, do_ref, dbw_ref):
        @pl.when(pl.program_id(0) == 0)
        def _():
            for r in (dsc_ref, dsh_ref, dnw_ref, dg_ref, dbw_ref):
                r[...] = jnp.zeros_like(r)

        _norm_bwd_step(dv_ref[...], h2_ref[...], dh3_ref[...], m_ref[...], nw_ref[...], sc_ref[...],
                       dh_ref, dsc_ref, dsh_ref, dnw_ref, dg_ref)
        do = (dh_ref[...] * g_ref[...]).astype(BF16)
        do_ref[...] = do
        dcat = _dot(do, w_ref[0].reshape(D, D), NT)
        bw_v = bw_ref[...]
        for g, (y_ref, dy_ref) in enumerate(((a_ref, da_ref), (b_ref, db_ref), (c_ref, dc_ref), (d_ref, dd_ref))):
            r, n = _rms(y_ref[...])
            dc = dcat[:, GW * g:GW * (g + 1)]
            dbw_ref[:, GW * g:GW * (g + 1)] += _colsum(dc * n)
            dy_ref[...] = _rms_bwd(r, n, dc * bw_v[:, GW * g:GW * (g + 1)])

    yblk = pl.BlockSpec((tb, GW), lambda i: (i, 0))
    blk = pl.BlockSpec((tb, D), lambda i: (i, 0))
    ysd = jax.ShapeDtypeStruct((t, GW), F32)
    row = jax.ShapeDtypeStruct((1, D), F32)
    return pl.pallas_call(
        body, name="b_out", grid=(t // tb,),
        in_specs=[blk] * 4 + [_row(D), _row(D)] + [yblk] * 4 + [_row(D), _wout_spec(l), _row(D)],
        out_specs=[blk, _row(D), _row(D), _row(D), _row(D)] + [yblk] * 4 + [blk, _row(D)],
        out_shape=[jax.ShapeDtypeStruct((t, D), F32), row, row, row, row] + [ysd] * 4 + [jax.ShapeDtypeStruct((t, D), BF16), row],
        compiler_params=_cparams(1),
    )(dv, h2, dh3, m, nw2, sc2, ya, yb, yc, yd, bw, wts, g1)


HB = 512
MLP_ROWS = 1024


ROW_W1, ROW_W2, ROW_WOUT, ROW_WIN = 0, D, D + HID // 4, D + HID // 4 + D // 4
PACK_ROWS = ROW_WIN + WIN_ROWS


def _w1_spec(l):
    per = HID // 4 // HB
    return pl.BlockSpec((1, 1, D, HB), lambda i, k: (l, k // per, ROW_W1 // D, k % per))


def _w2_spec(l):
    per = HID // 4 // HB
    return pl.BlockSpec((1, 1, HB, D), lambda i, k: (l, k // per, ROW_W2 // HB + k % per, 0))


def _wout_spec(l):
    return pl.BlockSpec((1, 4, D // 4, D), lambda i: (l, 0, ROW_WOUT // (D // 4), 0))


def _f_mlp(h2, nw, sc, sh, g2, wts, l):
    t = h2.shape[0]
    tb = _tblock(t, MLP_ROWS)
    nk = HID // HB

    def body(h_ref, nw_ref, sc_ref, sh_ref, g_ref, w1_ref, w2_ref, h3_ref, m_ref, a_ref, v_ref, acc_ref):
        k = pl.program_id(1)

        @pl.when(k == 0)
        def _():
            _, n = _rms(h_ref[...])
            v_ref[...] = ((n * nw_ref[...]) * (1.0 + sc_ref[...]) + sh_ref[...]).astype(BF16)
            acc_ref[...] = jnp.zeros_like(acc_ref)

        a = _dot(v_ref[...], w1_ref[0, 0])
        a_ref[...] = a.astype(BF16)
        ra = jnp.maximum(a, 0.0)
        acc_ref[...] += _dot((ra * ra).astype(BF16), w2_ref[0, 0])

        @pl.when(k == nk - 1)
        def _():
            m = acc_ref[...]
            m_ref[...] = m.astype(BF16)
            h3_ref[...] = h_ref[...] + g_ref[...] * m

    blk = pl.BlockSpec((tb, D), lambda i, k: (i, 0))
    return pl.pallas_call(
        body, name="f_mlp", grid=(t // tb, nk),
        in_specs=[blk, _row(D), _row(D), _row(D), _row(D), _w1_spec(l), _w2_spec(l)],
        out_specs=[blk, blk, pl.BlockSpec((tb, HB), lambda i, k: (i, k)), blk],
        out_shape=[jax.ShapeDtypeStruct((t, D), F32), jax.ShapeDtypeStruct((t, D), BF16), jax.ShapeDtypeStruct((t, HID), BF16),
                   jax.ShapeDtypeStruct((t, D), BF16)],
        scratch_shapes=[pltpu.VMEM((tb, D), F32)], compiler_params=_cparams(2),
    )(h2, nw, sc, sh, g2, wts, wts)


def _b_mlp(dh3, a, g2, wts, l):
    t = dh3.shape[0]
    tb = _tblock(t, MLP_ROWS)
    nk = HID // HB

    def body(dh_ref, a_ref, g_ref, w1_ref, w2_ref, dv_ref, da_ref, act_ref, dm_ref):
        k = pl.program_id(1)
        dm = (dh_ref[...] * g_ref[...]).astype(BF16)

        @pl.when(k == 0)
        def _():
            dm_ref[...] = dm
            dv_ref[...] = jnp.zeros_like(dv_ref)

        ra = jnp.maximum(a_ref[...].astype(F32), 0.0)
        act_ref[...] = (ra * ra).astype(BF16)
        da = (_dot(dm, w2_ref[0, 0], NT) * (2.0 * ra)).astype(BF16)
        da_ref[...] = da
        dv_ref[...] += _dot(da, w1_ref[0, 0], NT)

    blk = pl.BlockSpec((tb, D), lambda i, k: (i, 0))
    hblk = pl.BlockSpec((tb, HB), lambda i, k: (i, k))
    return pl.pallas_call(
        body, name="b_mlp", grid=(t // tb, nk),
        in_specs=[blk, hblk, _row(D), _w1_spec(l), _w2_spec(l)],
        out_specs=[blk, hblk, hblk, blk],
        out_shape=[jax.ShapeDtypeStruct((t, D), F32), jax.ShapeDtypeStruct((t, HID), BF16), jax.ShapeDtypeStruct((t, HID), BF16),
                   jax.ShapeDtypeStruct((t, D), BF16)],
        compiler_params=_cparams(2),
    )(dh3, a, g2, wts, wts)


def _b_final(h, tgt, fw):
    t = h.shape[0]
    tb = _tblock(t)

    def body(h_ref, t_ref, w_ref, dh_ref, loss_ref, dfw_ref):
        @pl.when(pl.program_id(0) == 0)
        def _():
            loss_ref[...] = jnp.zeros_like(loss_ref)
            dfw_ref[...] = jnp.zeros_like(dfw_ref)

        r, n = _rms(h_ref[...])
        wv = w_ref[...]
        err = n * wv - t_ref[...]
        loss_ref[...] += jnp.sum(err * err, keepdims=True) * (0.5 / D)
        dy = err * (1.0 / D)
        dfw_ref[...] += _colsum(dy * n)
        dh_ref[...] = _rms_bwd(r, n, dy * wv)

    blk = pl.BlockSpec((tb, D), lambda i: (i, 0))
    return pl.pallas_call(
        body, name="b_final", grid=(t // tb,), in_specs=[blk, blk, _row(D)], out_specs=[blk, _row(1), _row(D)],
        out_shape=[jax.ShapeDtypeStruct((t, D), F32), jax.ShapeDtypeStruct((1, 1), F32), jax.ShapeDtypeStruct((1, D), F32)],
        compiler_params=_cparams(1),
    )(h, tgt, fw)


def _eye(n):
    return jnp.eye(n, dtype=F32)


def _pool_embed(pool_w):
    return jnp.einsum('gcd,gk->gckd', pool_w, _eye(4)).reshape(GW, GW)


def _pool_extract(m):
    return jnp.einsum('gcgd->gcd', m.reshape(4, 64, 4, 64))


def _bmat_embed(bb):
    return jnp.einsum('gph,gk->ghkp', bb, _eye(16)).reshape(GW, S5_P)


def _bmat_extract(m):
    return jnp.einsum('ghgp->gph', m.reshape(16, 16, 16, 64))


def _cmat_embed(cc):
    return jnp.einsum('ghp,gk->kpgh', cc, _eye(16)).reshape(S5_P, GW)


def _cmat_extract(m):
    return jnp.einsum('gpgh->ghp', m.reshape(16, 64, 16, 16))


def _pad_lanes(v, n=DTW):
    return jnp.pad(v.reshape(1, -1), ((0, 0), (0, n - v.shape[-1])))


def _w_in_layout(w_in_t):
    w_main = jnp.concatenate([w_in_t[:1280], w_in_t[2052:2308], w_in_t[1280:2048]], axis=0)
    return w_main, jnp.pad(w_in_t[2048:2052], ((0, DTW - 4), (0, 0)))


def _layer_params(p, l, mod, w_in, rest):
    q = {'rest': rest, 'l': l}
    q['mod'] = [mod[k:k + 1] for k in range(6)]
    q['nw1'] = p['norm_mix_w'][l:l + 1]
    q['nw2'] = p['norm_mlp_w'][l:l + 1]
    q['w_main'], q['w_dt'] = _w_in_layout(w_in)
    q['pool_mat'] = _pool_embed(p['pool_w'][l]).astype(BF16)
    q['pool_scale'] = p['pool_scale'][l:l + 1]
    q['sconv_w'] = p['sconv_w'][l]
    q['conv_w'] = p['ssd_conv_w'][l]
    q['conv_b'] = p['ssd_conv_b'][l:l + 1]
    q['dt_bias'] = _pad_lanes(p['ssd_dt_bias'][l])
    q['a_log'] = _pad_lanes(p['ssd_a_log'][l])
    q['ssd_d'] = _pad_lanes(p['ssd_d'][l])
    q['s5_raw'] = (p['s5_a_re'][l], p['s5_a_im'][l], p['s5_log_step'][l].reshape(16, 1),
                   p['s5_b_re'][l].reshape(16, 1024), p['s5_b_im'][l].reshape(16, 1024))
    q['cre'] = _cmat_embed(p['s5_c_re'][l]).astype(BF16)
    q['cim'] = (-_cmat_embed(p['s5_c_im'][l])).astype(BF16)
    q['s5_d'] = p['s5_d'][l:l + 1]
    q['glu_w'] = p['s5_glu_w'][l].astype(BF16)
    q['glu_b'] = p['s5_glu_b'][l:l + 1]
    q['bw'] = p['branch_norm_w'][l:l + 1]
    return q


def _layer_fwd(h, q):
    sh1, sc1, g1, sh2, sc2, g2 = q['mod']
    t = h.shape[0]
    s = {'h': h}
    s['proj'], s['dtp'], s['u'] = _f_in(h, q['nw1'], sc1, sh1, q['w_main'], q['w_dt'])
    s['ya'], s['yb'] = _f_ab(s['proj'], q['pool_mat'], q['pool_scale'], q['sconv_w'])
    s['yc'], s['ypre'], s['sprev'] = _f_ssd(s['proj'], s['dtp'], q['conv_w'], q['conv_b'], q['dt_bias'], q['a_log'], q['ssd_d'])
    lr, li, bbr, bbi, ars, ais = _s5_prep(*q['s5_raw'])
    s['bmat'] = jnp.concatenate([_bmat_embed(bbr.reshape(16, 64, 16)), _bmat_embed(bbi.reshape(16, 64, 16))],
                                axis=1).astype(BF16)
    s['tables'] = _s5_tables(ars.reshape(1, S5_P), ais.reshape(1, S5_P))
    s['yd'], s['carries'], s['states'] = _f_s5(s['proj'], s['bmat'], q['cre'], q['cim'], s['tables'][0], s['tables'][1],
                                  q['s5_d'], q['glu_w'], q['glu_b'])
    q['wts'] = q['rest']((s['ya'], s['yc'], s['yd']))
    s['h2'], s['o'], s['cat'] = _f_out(s['ya'], s['yb'], s['yc'], s['yd'], q['bw'], q['wts'], q['l'], h, g1)
    h3, s['m'], s['a'], s['v'] = _f_mlp(s['h2'], q['nw2'], sc2, sh2, g2, q['wts'], q['l'])
    return h3, s


STACKED = {'mlp_w1': (2, 4, D, HID // 4), 'mlp_w2': (2, HID, D), 'w_out': (2, D, D)}


def _layer_bwd(dh3, q, s, l, stacked, early=None):
    sh1, sc1, g1, sh2, sc2, g2 = q['mod']
    g = {}
    dv, da, act, dm = _b_mlp(dh3, s['a'], g2, q['wts'], l)
    g['mlp_w1'] = _tn_matmul(s['v'], da, "dw1", col_major=True, into=stacked['mlp_w1'], layer=l)
    g['mlp_w2'] = _tn_matmul(act, dm, "dw2", into=stacked['mlp_w2'], layer=l)
    dh2, dsc2, dsh2, dnw2, dg2, dya, dyb, dyc, dyd, do, dbw = _b_out(
        dv, s['h2'], dh3, s['m'], q['nw2'], sc2, s['ya'], s['yb'], s['yc'], s['yd'], q['bw'], q['wts'], l, g1)
    g['w_out'] = _tn_matmul(s['cat'], do, "dwout", into=stacked['w_out'], layer=l)
    g['branch_norm_w'] = dbw[0]
    if early is not None:
        zero = early(g)[0, 0]
        q = dict(q, pool_scale=q['pool_scale'] + zero, conv_b=q['conv_b'] + zero, s5_d=q['s5_d'] + zero)
    dab, dpm, dps, dsw = _b_ab(s['proj'], dya, dyb, q['pool_mat'], q['pool_scale'], q['sconv_w'])
    g['pool_w'] = _pool_extract(dpm)
    g['pool_scale'] = dps[0]
    g['sconv_w'] = dsw
    dz, dxbc, ddt, dcw, dcb, ddtb, dal, ddk = _b_ssd(s['proj'], s['dtp'], s['ypre'], dyc, s['sprev'], q['conv_w'],
                                                     q['conv_b'], q['dt_bias'], q['a_log'], q['ssd_d'])
    g['ssd_conv_w'] = dcw
    g['ssd_conv_b'] = dcb[0]
    g['ssd_dt_bias'] = ddtb[0, :4]
    g['ssd_a_log'] = dal[0, :4]
    g['ssd_d'] = ddk[0, :4]
    tb = s['tables']
    ds5, dbmat, dcre, dcim, dlam, dd5, dgw, dgb = _b_s5(s['proj'], dyd, s['carries'], s['states'], s['bmat'], q['cre'], q['cim'],
                                                        tb[0], tb[1], q['s5_d'], q['glu_w'], q['glu_b'])
    g['s5_c_re'] = _cmat_extract(dcre)
    g['s5_c_im'] = -_cmat_extract(dcim)
    g['s5_d'] = dd5[0]
    g['s5_glu_w'] = dgw
    g['s5_glu_b'] = dgb[0]
    dbbr = _bmat_extract(dbmat[:, :S5_P]).reshape(16, 1024)
    dbbi = _bmat_extract(dbmat[:, S5_P:]).reshape(16, 1024)
    dar, dai, dls, dbr, dbi = _s5_prep_bwd(*q['s5_raw'], dlam[0].reshape(16, 64), dlam[1].reshape(16, 64), dbbr, dbbi)
    g['s5_a_re'], g['s5_a_im'], g['s5_log_step'] = dar, dai, dls[:, 0]
    g['s5_b_re'], g['s5_b_im'] = dbr, dbi
    dh, dsc1, dsh1, dnw1, dg1 = _b_in(dab, dz, dxbc, ds5, ddt, q['w_main'], q['w_dt'], s['h'], dh2, s['o'], q['nw1'], sc1)
    u = s['u']
    head = jnp.concatenate([_tn_matmul(dab, u, "dwin_ab"), _tn_matmul(dz, u, "dwin_z"), _tn_matmul(dxbc, u, "dwin_xbc"),
                            _tn_matmul(ddt, u, "dwin_dt")[:8]], axis=0)
    full = lax.dynamic_update_slice(jnp.zeros((2308, D), F32), head, (0, 0))
    g['w_in'] = lax.dynamic_update_slice(full, _tn_matmul(ds5, u, "dwin_s5"), (2052, 0))
    g['norm_mix_w'] = dnw1[0]
    g['norm_mlp_w'] = dnw2[0]
    dmod = jnp.concatenate([dsh1, dsc1, dg1, dsh2, dsc2, dg2], axis=1)
    return dh, g, dmod


def _local_step(x, tgt, p, mod, w_in_of, rest_of, early=None):
    h = x
    qs, saved = [], []
    for l in range(2):
        qs.append(_layer_params(p, l, mod[l], w_in_of(l), functools.partial(rest_of, l)))
        h, s = _layer_fwd(h, qs[l])
        saved.append(s)
    dh, loss, dfw = _b_final(h, tgt, p['final_norm_w'].reshape(1, D))
    grads = [None, None]
    dmods = [None, None]
    dh, grads[1], dmods[1] = _layer_bwd(dh, qs[1], saved[1], 1, {k: lax.empty(shp, F32) for k, shp in STACKED.items()})
    dh, grads[0], dmods[0] = _layer_bwd(dh, qs[0], saved[0], 0, grads[1], early)
    out = {k: jnp.stack([grads[0][k], grads[1][k]]) for k in grads[0] if k not in STACKED}
    if early is None:
        out.update({k: grads[0][k] for k in STACKED})
    out['final_norm_w'] = dfw[0]
    return loss, dh, out, jnp.concatenate(dmods, axis=0)


def _shard_of(a, axis, k):
    n = a.shape[axis] // 4
    return lax.dynamic_slice_in_dim(a, k * n, n, axis)


def kernel(x, c, norm_mix_w, norm_mlp_w, ada_w, ada_b, w_in, pool_w, pool_scale, sconv_w, ssd_conv_w, ssd_conv_b, ssd_dt_bias, ssd_a_log, ssd_d, s5_a_re, s5_a_im, s5_log_step, s5_b_re, s5_b_im, s5_c_re, s5_c_im, s5_d, s5_glu_w, s5_glu_b, branch_norm_w, w_out, mlp_w1, mlp_w2, final_norm_w, loss_target, m_norm_mix_w, m_norm_mlp_w, m_ada_w, m_ada_b, m_w_in, m_pool_w, m_pool_scale, m_sconv_w, m_ssd_conv_w, m_ssd_conv_b, m_ssd_dt_bias, m_ssd_a_log, m_ssd_d, m_s5_a_re, m_s5_a_im, m_s5_log_step, m_s5_b_re, m_s5_b_im, m_s5_c_re, m_s5_c_im, m_s5_d, m_s5_glu_w, m_s5_glu_b, m_branch_norm_w, m_w_out, m_mlp_w1, m_mlp_w2, m_final_norm_w, v_norm_mix_w, v_norm_mlp_w, v_ada_w, v_ada_b, v_w_in, v_pool_w, v_pool_scale, v_sconv_w, v_ssd_conv_w, v_ssd_conv_b, v_ssd_dt_bias, v_ssd_a_log, v_ssd_d, v_s5_a_re, v_s5_a_im, v_s5_log_step, v_s5_b_re, v_s5_b_im, v_s5_c_re, v_s5_c_im, v_s5_d, v_s5_glu_w, v_s5_glu_b, v_branch_norm_w, v_w_out, v_mlp_w1, v_mlp_w2, v_final_norm_w):
    loc = locals()
    w = {n: loc[n] for n in WEIGHTS}
    mom = {n: loc['m_' + n] for n in WEIGHTS}
    var = {n: loc['v_' + n] for n in WEIGHTS}
    ix, iy, ic = lax.axis_index("x"), lax.axis_index("y"), lax.axis_index("c")
    chip = 2 * ix + iy
    dev = 4 * ix + 2 * iy + ic

    mine_of = lambda a: lax.dynamic_index_in_dim(a.astype(BF16), ic, axis=0, keepdims=False)
    pad_in = lambda a: jnp.pad(a.T, ((0, WIN_ROWS - 577), (0, 0)))
    shard = jnp.concatenate([mine_of(w['mlp_w1']), mine_of(w['mlp_w2']), mine_of(w['w_out']), pad_in(mine_of(w['w_in']))], axis=0)

    (c_all,) = _exchange([c], EVERYONE, False, "ag_cond", stage=True)
    c_all = c_all.reshape(8, D)
    small_sh = _exchange([w[n] for n in SMALL_SHARDED], CHIPS, False, "ag_small")
    (w_in0,) = _exchange([pad_in(w['w_in'][0].astype(BF16))], CHIPS, False, "ag_win0")
    p = {n: w[n] for n in WEIGHTS if n not in BIG}
    for n, g in zip(SMALL_SHARDED, small_sh):
        ax = SMALL_SHARDED[n]
        p[n] = jnp.concatenate([g[k] for k in range(4)], axis=ax)

    def w_in_full(sh):
        return sh[:, :577].reshape(4 * 577, D)

    big = {}

    def fetch(after):
        if not big:
            (mine,), (got,) = _split_wait(sems, shard_thru, land, after, False, "ag_big_wait", per_core=True)
            got = lax.dynamic_update_slice(got, mine[None, None], (ic, chip, 0, 0))
            (both,) = _pair_swap([got.reshape(2, -1, D)], False, "swap_big", fill=True)
            big['both'] = both.reshape(got.shape)
        return big['both']

    def w_in_of(l):
        return w_in_full(w_in0) if l == 0 else w_in_full(fetch(None)[1, :, ROW_WIN:])

    def rest_of(l, after):
        return fetch(after)

    ada_b_sh = _shard_of(w['ada_b'], 1, chip).reshape(2, 1, 6 * D // 4)
    mod_sh = _ada_fwd(c_all, w['ada_w'], ada_b_sh)
    (mod_all,) = _exchange([mod_sh], CHIPS, False, "ag_mod", stage=True)
    mine = lax.dynamic_index_in_dim(mod_all, dev, axis=2, keepdims=False)
    sems, shard_thru, land, token = _split_start([shard], [mod_all, w_in0] + small_sh, False, "ag_big_start", per_core=True)
    mod = jnp.transpose(mine, (1, 0, 2)).reshape(2, 6, D) + token[0, 0]

    layer = ic.astype(jnp.int32).reshape(1)
    flight = {}

    def early(g0):
        gws = [g0['w_out'].reshape(2, 4, 256, D), g0['mlp_w1'], g0['mlp_w2'].reshape(2, 4, 1024, D)]
        got = _pair_swap([a.reshape(2, -1, D) for a in gws], True, "swap_grad", narrow=True)
        pair = [_pair_sum(a, b.reshape(a.shape[1:]), layer, "pair_sum%d" % (k + 1), BF16) for k, (a, b) in enumerate(zip(gws, got))]
        flight['sems'], flight['srcs'], flight['lands'], token = _split_start(pair, [], True, "rs_start")
        return token

    loss, grad_x, g, dmod = _local_step(x[0], loss_target[0], p, mod, w_in_of, rest_of, early)

    (dmod_all,) = _exchange([dmod], EVERYONE, False, "ag_dmod", stage=True)
    dmod_all = jnp.transpose(dmod_all, (1, 0, 2))

    gw_in = jnp.pad(g['w_in'].reshape(2, 4, 577, D), ((0, 0), (0, 0), (0, WIN_ROWS - 577), (0, 0)))
    (got_in,) = _pair_swap([gw_in.reshape(2, -1, D)], True, "swap_grad_in", narrow=True)
    pair_in = _pair_sum(gw_in, got_in.reshape(gw_in.shape[1:]), layer, "pair_sum0", BF16)
    in_sems, in_srcs, in_lands, in_token = _split_start([pair_in], [dmod_all], True, "rs_in_start")

    def chip_sum(land, mine, name):
        own = lax.dynamic_index_in_dim(mine, chip, axis=0, keepdims=True)
        return _sum_lead(lax.dynamic_update_slice(land, own, (chip, 0, 0)), name, F32)

    sent, lands = _split_wait(flight['sems'], flight['srcs'], flight['lands'], [grad_x, in_token], True, "rs_wait")
    quad = [chip_sum(land, mine, "rs_chip_sum%d" % (k + 1)) for k, (land, mine) in enumerate(zip(lands, sent))]
    g_ada_w, g_ada_b = _ada_bwd(c_all, _shard_of(dmod_all, 2, chip), dmod_all)
    adam_ada_w = _adamw(w['ada_w'], g_ada_w, mom['ada_w'], var['ada_w'], "adamw_ada_w")
    (sent_in,), (land_in,) = _split_wait(in_sems, in_srcs, in_lands, quad + [adam_ada_w[0]], True, "rs_in_wait")
    quad = [chip_sum(land_in, sent_in, "rs_chip_sum0")] + quad
    halves = [lax.dynamic_update_slice(lax.empty((2,) + a.shape, F32), a[None], (ic, 0, 0)) for a in quad]
    both = _pair_swap(halves, False, "swap_red", fill=True)
    both[0] = jnp.transpose(both[0][:, :577], (0, 2, 1))
    red = dict(zip(('w_in', 'w_out', 'mlp_w1', 'mlp_w2'), both))
    red['ada_w'] = g_ada_w

    small_names = [n for n in WEIGHTS if n not in BIG and n != 'ada_b']
    pair_parts = _exchange([g[n] for n in small_names] + [loss], SIBLING, False, "ag_smallpair", stage=True)
    narrow = ('pool_w', 's5_b_re', 's5_b_im', 's5_c_re', 's5_c_im', 's5_glu_w')
    pair_dtypes = [BF16 if n in narrow else F32 for n in small_names] + [F32]
    chip_parts = _exchange(_sum_many(pair_parts, "smallpair_sum", pair_dtypes), CHIPS, False, "ag_smallgrad", stage=True)
    summed = _sum_many(chip_parts, "smallgrad_sum")
    for n, a in zip(small_names, summed[:-1]):
        a = a.reshape(w[n].shape) if n in ('s5_b_re', 's5_b_im') else a
        red[n] = _shard_of(a, SMALL_SHARDED[n], chip) if n in SMALL_SHARDED else a
    red['ada_b'] = g_ada_b
    loss_out = summed[-1].reshape(())

    delta, new_m, new_v = {}, {}, {}
    delta['ada_w'], new_m['ada_w'], new_v['ada_w'] = adam_ada_w
    for n in BIG[1:]:
        delta[n], new_m[n], new_v[n] = _adamw(w[n], red[n], mom[n], var[n], "adamw_" + n)
    rest = [n for n in WEIGHTS if n not in BIG]
    lanes = lambda n, a: a.reshape(2, 16, 1024) if n in ('s5_b_re', 's5_b_im') else a
    outs = _adamw_many(*[[lanes(n, src[n]) for n in rest] for src in (w, red, mom, var)], "adamw_small")
    for k, n in enumerate(rest):
        delta[n], new_m[n], new_v[n] = (outs[3 * k + j].reshape(w[n].shape) for j in range(3))

    return (loss_out, grad_x[None], *[red[n] for n in WEIGHTS], *[delta[n] for n in WEIGHTS],
            *[new_m[n] for n in WEIGHTS], *[new_v[n] for n in WEIGHTS])
```

```python
import functools
import math

import jax
import jax.numpy as jnp
from jax import lax
from jax.experimental import pallas as pl
from jax.experimental.pallas import tpu as pltpu

F32 = jnp.float32
BF16 = jnp.bfloat16
HI = lax.Precision.HIGHEST

D = 1024
GW = 256
HID = 4096
EPS = 1e-6
PW = 2304
DTW = 128
SSD_L = 128
SSD_SUB = 2
SSD_SUB_BWD = 2
NH, HP, NS = 4, 64, 128
S5_P = 1024
MESH = pl.DeviceIdType.MESH

ADAM_LR, ADAM_B1, ADAM_B2, ADAM_EPS, ADAM_WD, ADAM_STEP = 0.001, 0.9, 0.999, 1e-08, 0.01, 10

NT = (((1,), (1,)), ((), ()))
TN = (((0,), (0,)), ((), ()))

WEIGHTS = ['norm_mix_w', 'norm_mlp_w', 'ada_w', 'ada_b', 'w_in', 'pool_w', 'pool_scale', 'sconv_w', 'ssd_conv_w',
           'ssd_conv_b', 'ssd_dt_bias', 'ssd_a_log', 'ssd_d', 's5_a_re', 's5_a_im', 's5_log_step', 's5_b_re', 's5_b_im',
           's5_c_re', 's5_c_im', 's5_d', 's5_glu_w', 's5_glu_b', 'branch_norm_w', 'w_out', 'mlp_w1', 'mlp_w2',
           'final_norm_w']
BIG = ('ada_w', 'w_in', 'w_out', 'mlp_w1', 'mlp_w2')
SMALL_SHARDED = {'sconv_w': 2, 'ssd_conv_w': 2, 's5_glu_w': 1}


def _cparams(n_axes, vmem_mb=48):
    return pltpu.CompilerParams(dimension_semantics=("arbitrary",) * n_axes, vmem_limit_bytes=vmem_mb * 1024 * 1024)


def _row(n):
    return pl.BlockSpec((1, n), lambda *_: (0, 0))


def _full(shape):
    nd = len(shape)
    return pl.BlockSpec(tuple(shape), lambda *_: (0,) * nd)


def _dot(a, b, dims=None, prec=None):
    if dims is None:
        dims = (((a.ndim - 1,), (0,)), ((), ()))
    return lax.dot_general(a, b, dims, preferred_element_type=F32, precision=prec)


def _bdot(a, b, dims=None):
    return _dot(a.astype(BF16), b.astype(BF16), dims)


def _sig(x):
    return jax.nn.sigmoid(x)


def _silu(x):
    return x * _sig(x)


def _dsilu(x):
    s = _sig(x)
    return s * (1.0 + x * (1.0 - s))


def _softplus(x):
    return jnp.maximum(x, 0.0) + jnp.log(1.0 + jnp.exp(-jnp.abs(x)))


_GK = math.sqrt(2.0 / math.pi)


def _gelu(x):
    return 0.5 * x * (1.0 + jnp.tanh(_GK * (x + 0.044715 * x * x * x)))


def _dgelu(x):
    th = jnp.tanh(_GK * (x + 0.044715 * x * x * x))
    return 0.5 * (1.0 + th) + 0.5 * x * (1.0 - th * th) * _GK * (1.0 + 3.0 * 0.044715 * x * x)


def _colsum(x):
    return jnp.sum(x, axis=0, keepdims=True)


def _rms(x):
    r = lax.rsqrt(jnp.mean(x * x, axis=-1, keepdims=True) + EPS)
    return r, x * r


def _rms_bwd(r, n, dn):
    return r * (dn - n * jnp.mean(dn * n, axis=-1, keepdims=True))


def _roll(x, k):
    n = x.shape[0]
    k = k % n
    return x if k == 0 else pltpu.roll(x, k, axis=0)


def _tblock(t, want=512):
    return min(t, want)


def _peer(mask):
    x, y, c = lax.axis_index("x"), lax.axis_index("y"), lax.axis_index("c")
    return (x ^ ((mask >> 2) & 1), y ^ ((mask >> 1) & 1), c ^ (mask & 1))


def _group_index(masks):
    x, y, c = lax.axis_index("x"), lax.axis_index("y"), lax.axis_index("c")
    full = 0
    for m in masks:
        full |= m
    bits = [b for b in (4, 2, 1) if full & b]

    def idx(px, py, pc):
        v = {4: px, 2: py, 1: pc}
        out = 0
        for b in bits:
            out = out * 2 + v[b]
        return out

    return idx(x, y, c), [idx(*_peer(m)) for m in masks]


def _exchange(arrs, masks, scatter, name, stage=False):
    n_arr, n_peer, n_grp = len(arrs), len(masks), len(masks) + 1

    def body(*refs):
        ins, outs = refs[:n_arr], refs[n_arr:2 * n_arr]
        send_sems, recv_sems, local_sems = refs[2 * n_arr:2 * n_arr + 3]
        if stage:
            bufs, load_sems = refs[2 * n_arr + 3:3 * n_arr + 3], refs[3 * n_arr + 3]
            loads = [pltpu.make_async_copy(ins[t], bufs[t], load_sems.at[t]) for t in range(n_arr)]
            for ld in loads:
                ld.start()
            for ld in loads:
                ld.wait()
            ins = bufs
        me, peer_idx = _group_index(masks)
        copies = []
        for t in range(n_arr):
            src_me = ins[t].at[me] if scatter else ins[t]
            loc = pltpu.make_async_copy(src_me, outs[t].at[me], local_sems.at[t])
            loc.start()
            copies.append(loc)
            for j, m in enumerate(masks):
                src = ins[t].at[peer_idx[j]] if scatter else ins[t]
                cp = pltpu.make_async_remote_copy(src_ref=src, dst_ref=outs[t].at[me], send_sem=send_sems.at[t, j],
                                                  recv_sem=recv_sems.at[t, j], device_id=_peer(m), device_id_type=MESH)
                cp.start()
                copies.append(cp)
        for cp in copies:
            cp.wait()

    hbm = pl.BlockSpec(memory_space=pl.ANY)
    out_shape = [jax.ShapeDtypeStruct((n_grp,) + (a.shape[1:] if scatter else a.shape), a.dtype) for a in arrs]
    staging = [pltpu.VMEM(a.shape, a.dtype) for a in arrs] + [pltpu.SemaphoreType.DMA((n_arr,))] if stage else []
    outs = pl.pallas_call(
        body, name=name, in_specs=[hbm] * n_arr, out_specs=[hbm] * n_arr, out_shape=out_shape,
        scratch_shapes=[pltpu.SemaphoreType.DMA((n_arr, n_peer)), pltpu.SemaphoreType.DMA((n_arr, n_peer)),
                        pltpu.SemaphoreType.DMA((n_arr,))] + staging,
        compiler_params=pltpu.CompilerParams(vmem_limit_bytes=48 * 1024 * 1024),
    )(*arrs)
    return list(outs)


def _split_copies(src_refs, land_refs, sems, scatter, per_core):
    me, peer_idx = _group_index(CHIPS)
    n = len(CHIPS) * len(src_refs)
    copies = []
    for t, (src_ref, land_ref) in enumerate(zip(src_refs, land_refs)):
        zone = land_ref.at[lax.axis_index("c")] if per_core else land_ref
        for j, m in enumerate(CHIPS):
            k = len(CHIPS) * t + j
            copies.append(pltpu.make_async_remote_copy(
                src_ref=src_ref.at[peer_idx[j]] if scatter else src_ref, dst_ref=zone.at[me], send_sem=sems[k],
                recv_sem=sems[n + k], device_id=_peer(m), device_id_type=MESH))
    return copies


def _split_start(srcs, after, scatter, name, per_core=False):
    n_arr, n_sem = len(srcs), 2 * len(CHIPS) * len(srcs)

    def body(*refs):
        src_refs, land_refs = refs[:n_arr], refs[n_arr:2 * n_arr]
        outs = refs[2 * n_arr + len(after):]
        for cp in _split_copies(src_refs, land_refs, outs[:n_sem], scatter, per_core):
            cp.start()
        outs[-1][...] = jnp.zeros_like(outs[-1])

    hbm = pl.BlockSpec(memory_space=pltpu.HBM)
    sem = pl.BlockSpec(memory_space=pltpu.SEMAPHORE)
    lands = [lax.empty(((2,) if per_core else ()) + (len(CHIPS) + 1,) + (a.shape[1:] if scatter else a.shape), a.dtype)
             for a in srcs]
    as_hbm = lambda a: pltpu.with_memory_space_constraint(a, pltpu.HBM)
    outs = pl.pallas_call(
        body, name=name,
        out_shape=(pltpu.SemaphoreType.DMA(()),) * n_sem + tuple(pltpu.HBM(a.shape, a.dtype) for a in srcs + lands)
        + (jax.ShapeDtypeStruct((8, 128), F32),),
        in_specs=(hbm,) * (2 * n_arr) + (pl.BlockSpec(memory_space=pl.ANY),) * len(after),
        out_specs=(sem,) * n_sem + (hbm,) * (2 * n_arr) + (pl.BlockSpec(memory_space=pltpu.VMEM),),
        input_output_aliases={t: n_sem + t for t in range(2 * n_arr)},
        compiler_params=pltpu.CompilerParams(has_side_effects=pltpu.SideEffectType.DATAFLOW_SIDE_EFFECTING),
    )(*[as_hbm(a) for a in srcs + lands], *after)
    return outs[:n_sem], list(outs[n_sem:n_sem + n_arr]), list(outs[n_sem + n_arr:n_sem + 2 * n_arr]), outs[-1]


def _split_wait(sems, srcs, lands, after, scatter, name, per_core=False):
    n_arr, n_sem = len(srcs), len(sems)

    def body(*refs):
        src_refs, land_refs = refs[:n_arr], refs[n_arr:2 * n_arr]
        for cp in _split_copies(src_refs, land_refs, refs[2 * n_arr:2 * n_arr + n_sem], scatter, per_core):
            cp.wait_send()
            cp.wait_recv()

    hbm = pl.BlockSpec(memory_space=pltpu.HBM)
    sem = pl.BlockSpec(memory_space=pltpu.SEMAPHORE)
    outs = pl.pallas_call(
        body, name=name, out_shape=tuple(pltpu.HBM(a.shape, a.dtype) for a in srcs + lands),
        in_specs=(hbm,) * (2 * n_arr) + (sem,) * n_sem + (pl.BlockSpec(memory_space=pl.ANY),) * len(after),
        out_specs=(hbm,) * (2 * n_arr), input_output_aliases={t: t for t in range(2 * n_arr)},
        compiler_params=pltpu.CompilerParams(has_side_effects=pltpu.SideEffectType.DATAFLOW_SIDE_EFFECTING),
    )(*srcs, *lands, *sems, *after)
    return list(outs[:n_arr]), list(outs[n_arr:])


CHIPS = (4, 2, 6)
EVERYONE = (1, 2, 3, 4, 5, 6, 7)
SIBLING = (1,)
SWAP_ROWS = 1024
WIN_ROWS = 592


def _pair_swap(arrs, other_layer, name, narrow=False, fill=False):
    assert not (fill and (other_layer or narrow))
    n_arr = len(arrs)
    shapes = [a.shape[-2:] for a in arrs]
    out_dtypes = [BF16 if narrow else a.dtype for a in arrs]
    chunks = []
    for t, (rows, _) in enumerate(shapes):
        assert rows % 16 == 0
        for j, r0 in enumerate(range(0, rows, SWAP_ROWS)):
            chunks.append((t, r0, min(SWAP_ROWS, rows - r0), j % 2))

    def body(*refs):
        ins, outs = refs[:n_arr], refs[n_arr:2 * n_arr]
        bufs = refs[2 * n_arr:3 * n_arr]
        out_bufs = refs[3 * n_arr:4 * n_arr] if narrow else bufs
        load_sems, send_sems, recv_sems = refs[-3:]
        sibling = _peer(1)
        c = lax.axis_index("c")

        def load(k):
            t, r0, n, slot = chunks[k]
            src = ins[t].at[1 - c] if other_layer else ins[t].at[c] if fill else ins[t]
            return pltpu.make_async_copy(src.at[pl.ds(r0, n)], bufs[t].at[slot, pl.ds(0, n)], load_sems.at[t, slot])

        def send(k):
            t, r0, n, slot = chunks[k]
            dst = outs[t].at[c] if fill else outs[t]
            return pltpu.make_async_remote_copy(src_ref=out_bufs[t].at[slot, pl.ds(0, n)], dst_ref=dst.at[pl.ds(r0, n)],
                                                send_sem=send_sems.at[t, slot], recv_sem=recv_sems.at[t],
                                                device_id=sibling, device_id_type=MESH)

        in_flight = {}

        def drain(k):
            key = (chunks[k][0], chunks[k][3])
            if key in in_flight:
                send(in_flight.pop(key)).wait_send()

        def start_load(k):
            if not narrow:
                drain(k)
            load(k).start()

        start_load(0)
        for k in range(len(chunks)):
            t, _, n, slot = chunks[k]
            load(k).wait()
            if k + 1 < len(chunks):
                start_load(k + 1)
            if narrow:
                drain(k)
                out_bufs[t][slot, pl.ds(0, n), :] = bufs[t][slot, pl.ds(0, n), :].astype(BF16)
            send(k).start()
            in_flight[(t, slot)] = k
        for k in in_flight.values():
            send(k).wait_send()
        for t in range(n_arr):
            landed = outs[t].at[1 - c] if fill else outs[t]
            pltpu.make_async_remote_copy(src_ref=landed, dst_ref=landed, send_sem=send_sems.at[t, 0],
                                         recv_sem=recv_sems.at[t], device_id=sibling, device_id_type=MESH).wait_recv()

    hbm = pl.BlockSpec(memory_space=pl.ANY)
    outs = pl.pallas_call(
        body, name=name, in_specs=[hbm] * n_arr, out_specs=[hbm] * n_arr,
        out_shape=[jax.ShapeDtypeStruct(a.shape if fill else s, dt) for a, s, dt in zip(arrs, shapes, out_dtypes)],
        input_output_aliases={t: t for t in range(n_arr)} if fill else {},
        scratch_shapes=[pltpu.VMEM((2, min(SWAP_ROWS, s[0]), s[1]), a.dtype) for s, a in zip(shapes, arrs)]
        + ([pltpu.VMEM((2, min(SWAP_ROWS, s[0]), s[1]), BF16) for s in shapes] if narrow else [])
        + [pltpu.SemaphoreType.DMA((n_arr, 2)), pltpu.SemaphoreType.DMA((n_arr, 2)), pltpu.SemaphoreType.DMA((n_arr,))],
        compiler_params=pltpu.CompilerParams(vmem_limit_bytes=48 * 1024 * 1024),
    )(*arrs)
    return list(outs)


def _sum_lead(a, name, out_dtype):
    n = a.shape[0]
    shape = a.shape[1:]

    def body(a_ref, o_ref):
        acc = a_ref[0].astype(F32)
        for k in range(1, n):
            acc = acc + a_ref[k].astype(F32)
        o_ref[...] = acc.astype(out_dtype)

    if len(shape) == 3:
        blk = (1,) + shape[1:]
        return pl.pallas_call(
            body, name=name, grid=(shape[0],), in_specs=[pl.BlockSpec((n,) + blk, lambda i: (0, i, 0, 0))],
            out_specs=pl.BlockSpec(blk, lambda i: (i, 0, 0)), out_shape=jax.ShapeDtypeStruct(shape, out_dtype),
            compiler_params=_cparams(1),
        )(a)
    rows, cols = shape
    rb = rows
    for cand in (512, 256, 128):
        if rows % cand == 0 and rows > cand:
            rb = cand
            break
    return pl.pallas_call(
        body, name=name, grid=(rows // rb,), in_specs=[pl.BlockSpec((n, rb, cols), lambda i: (0, i, 0))],
        out_specs=pl.BlockSpec((rb, cols), lambda i: (i, 0)), out_shape=jax.ShapeDtypeStruct((rows, cols), out_dtype),
        compiler_params=_cparams(1),
    )(a)


def _pair_sum(g, recv, layer, name, out_dtype):
    _, n, r, c = g.shape

    def body(l_ref, g_ref, r_ref, o_ref):
        o_ref[...] = (g_ref[0].astype(F32) + r_ref[...].astype(F32)).astype(out_dtype)

    return pl.pallas_call(
        body, name=name,
        grid_spec=pltpu.PrefetchScalarGridSpec(
            num_scalar_prefetch=1, grid=(n,),
            in_specs=[pl.BlockSpec((1, 1, r, c), lambda i, l: (l[0], i, 0, 0)), pl.BlockSpec((1, r, c), lambda i, l: (i, 0, 0))],
            out_specs=pl.BlockSpec((1, r, c), lambda i, l: (i, 0, 0))),
        out_shape=jax.ShapeDtypeStruct((n, r, c), out_dtype), compiler_params=_cparams(1),
    )(layer, g, recv)


def _tn_matmul(a, b, name, col_major=False, into=None, layer=0):
    t, k = a.shape
    n = b.shape[1]
    tb = _tblock(t, 1024)
    kb = min(k, 1024)
    nb = min(n, 1024)
    grid = (k // kb, n // nb, t // tb)
    lead = (into is not None) + col_major

    def body(a_ref, b_ref, *rest):
        o_ref = rest[-1]
        for _ in range(lead):
            o_ref = o_ref.at[0]

        @pl.when(pl.program_id(2) == 0)
        def _():
            o_ref[...] = jnp.zeros_like(o_ref)

        o_ref[...] += _bdot(a_ref[...], b_ref[...], TN)

    if col_major:
        block, index, shape = (1, kb, nb), (lambda ki, ni: (ni, ki, 0)), (n // nb, k, nb)
    else:
        block, index, shape = (kb, nb), (lambda ki, ni: (ki, ni)), (k, n)
    in_specs = [pl.BlockSpec((tb, kb), lambda ki, ni, ti: (ti, ki)), pl.BlockSpec((tb, nb), lambda ki, ni, ti: (ti, ni))]
    if into is None:
        return pl.pallas_call(
            body, name=name, grid=grid, in_specs=in_specs, out_specs=pl.BlockSpec(block, lambda ki, ni, ti: index(ki, ni)),
            out_shape=jax.ShapeDtypeStruct(shape, F32), compiler_params=_cparams(3),
        )(a, b)
    assert into.shape == (2,) + shape
    return pl.pallas_call(
        body, name=name, grid=grid, in_specs=in_specs + [pl.BlockSpec(memory_space=pl.ANY)],
        out_specs=pl.BlockSpec((1,) + block, lambda ki, ni, ti: (layer,) + index(ki, ni)),
        out_shape=jax.ShapeDtypeStruct(into.shape, F32), input_output_aliases={2: 0}, compiler_params=_cparams(3),
    )(a, b, into)


def _sum_many(arrs, name, out_dtypes=None):
    k = len(arrs)
    out_dtypes = out_dtypes or [F32] * k

    def body(*refs):
        for a_ref, o_ref in zip(refs[:k], refs[k:]):
            acc = a_ref[0].astype(F32)
            for j in range(1, a_ref.shape[0]):
                acc = acc + a_ref[j].astype(F32)
            o_ref[...] = acc.astype(o_ref.dtype)

    return pl.pallas_call(body, name=name, grid=(1,), in_specs=[_full(a.shape) for a in arrs],
                          out_specs=[_full(a.shape[1:]) for a in arrs],
                          out_shape=[jax.ShapeDtypeStruct(a.shape[1:], dt) for a, dt in zip(arrs, out_dtypes)],
                          compiler_params=_cparams(1))(*arrs)


def _adamw_math(w, g, m, v):
    m2 = ADAM_B1 * m + (1.0 - ADAM_B1) * g
    v2 = ADAM_B2 * v + (1.0 - ADAM_B2) * (g * g)
    m_hat = m2 / (1.0 - ADAM_B1 ** ADAM_STEP)
    v_hat = v2 / (1.0 - ADAM_B2 ** ADAM_STEP)
    return -ADAM_LR * (m_hat / (jnp.sqrt(v_hat) + ADAM_EPS) + ADAM_WD * w), m2, v2


def _adamw_many(ws, gs, ms, vs, name):
    n = len(ws)

    def body(*refs):
        ins, outs = refs[:4 * n], refs[4 * n:]
        for k in range(n):
            res = _adamw_math(ins[k][...], ins[n + k][...], ins[2 * n + k][...], ins[3 * n + k][...])
            for j in range(3):
                outs[3 * k + j][...] = res[j]

    out_shape = []
    for a in ws:
        out_shape += [jax.ShapeDtypeStruct(a.shape, F32)] * 3
    return pl.pallas_call(body, name=name, grid=(1,), in_specs=[_full(a.shape) for a in ws] * 4,
                          out_specs=[_full(s.shape) for s in out_shape], out_shape=out_shape,
                          compiler_params=_cparams(1))(*ws, *gs, *ms, *vs)


def _adamw(w, g, m, v, name):
    shape = w.shape
    cols = shape[-1]
    rows = int(math.prod(shape[:-1]))
    rb = rows
    for cand in (256, 128, 64, 32, 16, 8):
        if rows % cand == 0 and rows > cand:
            rb = cand
            break
    bc1 = 1.0 - ADAM_B1 ** ADAM_STEP
    bc2 = 1.0 - ADAM_B2 ** ADAM_STEP

    def body(w_ref, g_ref, m_ref, v_ref, d_ref, nm_ref, nv_ref):
        gg = g_ref[...]
        m2 = ADAM_B1 * m_ref[...] + (1.0 - ADAM_B1) * gg
        v2 = ADAM_B2 * v_ref[...] + (1.0 - ADAM_B2) * (gg * gg)
        m_hat = m2 / bc1
        v_hat = v2 / bc2
        d_ref[...] = -ADAM_LR * (m_hat / (jnp.sqrt(v_hat) + ADAM_EPS) + ADAM_WD * w_ref[...])
        nm_ref[...] = m2
        nv_ref[...] = v2

    spec = pl.BlockSpec((rb, cols), lambda i: (i, 0))
    sds = jax.ShapeDtypeStruct((rows, cols), F32)
    outs = pl.pallas_call(
        body, name=name, grid=(rows // rb,), in_specs=[spec] * 4, out_specs=[spec] * 3, out_shape=[sds] * 3,
        compiler_params=_cparams(1),
    )(*(z.reshape(rows, cols) for z in (w, g, m, v)))
    return tuple(o.reshape(shape) for o in outs)


def _ada_fwd(c_all, ada_w_sh, ada_b_sh):
    s = ada_w_sh.shape[2]
    sb = 512

    def body(c_ref, w_ref, b_ref, o_ref):
        cond = _silu(c_ref[...])
        o_ref[0] = _bdot(cond, w_ref[0]) + b_ref[0]

    return pl.pallas_call(
        body, name="ada_fwd", grid=(2, s // sb),
        in_specs=[_full((8, D)), pl.BlockSpec((1, D, sb), lambda l, j: (l, 0, j)), pl.BlockSpec((1, 1, sb), lambda l, j: (l, 0, j))],
        out_specs=pl.BlockSpec((1, 8, sb), lambda l, j: (l, 0, j)), out_shape=jax.ShapeDtypeStruct((2, 8, s), F32),
        compiler_params=_cparams(2),
    )(c_all, ada_w_sh, ada_b_sh)


def _ada_bwd(c_all, dmod_sh, dmod_all):
    s = dmod_sh.shape[2]
    sb = 512

    def body(c_ref, d_ref, o_ref):
        cond = _silu(c_ref[...])
        o_ref[0] = _bdot(cond, d_ref[0], TN)

    gw = pl.pallas_call(
        body, name="ada_bwd_w", grid=(2, s // sb),
        in_specs=[_full((8, D)), pl.BlockSpec((1, 8, sb), lambda l, j: (l, 0, j))],
        out_specs=pl.BlockSpec((1, D, sb), lambda l, j: (l, 0, j)), out_shape=jax.ShapeDtypeStruct((2, D, s), F32),
        compiler_params=_cparams(2),
    )(c_all, dmod_sh)

    def body_b(d_ref, o_ref):
        acc = d_ref[0, 0:1, :]
        for k in range(1, 8):
            acc = acc + d_ref[0, k:k + 1, :]
        o_ref[0] = acc

    gb = pl.pallas_call(
        body_b, name="ada_bwd_b", grid=(2,), in_specs=[pl.BlockSpec((1, 8, 6 * D), lambda l: (l, 0, 0))],
        out_specs=pl.BlockSpec((1, 1, 6 * D), lambda l: (l, 0, 0)), out_shape=jax.ShapeDtypeStruct((2, 1, 6 * D), F32),
        compiler_params=_cparams(1),
    )(dmod_all)
    return gw, gb.reshape(2, 6 * D)


def _f_in(h, nw, sc, sh, w_main, w_dt):
    t = h.shape[0]
    tb = _tblock(t)

    def body(h_ref, nw_ref, sc_ref, sh_ref, w_ref, wd_ref, p_ref, dt_ref, u_ref):
        _, n = _rms(h_ref[...])
        u = ((n * nw_ref[...]) * (1.0 + sc_ref[...]) + sh_ref[...]).astype(BF16)
        u_ref[...] = u
        p_ref[...] = _dot(u, w_ref[...], NT)
        dt_ref[...] = _dot(u, wd_ref[...], NT)

    return pl.pallas_call(
        body, name="f_in", grid=(t // tb,),
        in_specs=[pl.BlockSpec((tb, D), lambda i: (i, 0)), _row(D), _row(D), _row(D), _full((PW, D)), _full((DTW, D))],
        out_specs=[pl.BlockSpec((tb, PW), lambda i: (i, 0)), pl.BlockSpec((tb, DTW), lambda i: (i, 0)),
                   pl.BlockSpec((tb, D), lambda i: (i, 0))],
        out_shape=[jax.ShapeDtypeStruct((t, PW), F32), jax.ShapeDtypeStruct((t, DTW), F32), jax.ShapeDtypeStruct((t, D), BF16)],
        compiler_params=_cparams(1),
    )(h, nw, sc, sh, w_main, w_dt)


def _norm_bwd_step(du_v, x, dres_v, gated, nwv, scv, dx_ref, dsc_ref, dsh_ref, dnw_ref, dg_ref):
    r, n = _rms(x)
    scale = 1.0 + scv
    dsc_ref[...] += _colsum(du_v * (n * nwv))
    dsh_ref[...] += _colsum(du_v)
    dnw_ref[...] += _colsum(du_v * scale * n)
    dg_ref[...] += _colsum(dres_v * gated)
    dx_ref[...] = dres_v + _rms_bwd(r, n, du_v * scale * nwv)


def _b_in(dab, dz, dxbc, ds5, ddt, w_main, w_dt, x, dres, gated, nw, sc):
    t = dab.shape[0]
    tb = _tblock(t)

    def body(a_ref, z_ref, x_ref, s_ref, d_ref, w_ref, wd_ref, h_ref, dr_ref, g_ref, nw_ref, sc_ref,
             dx_ref, dsc_ref, dsh_ref, dnw_ref, dg_ref):
        @pl.when(pl.program_id(0) == 0)
        def _():
            for r in (dsc_ref, dsh_ref, dnw_ref, dg_ref):
                r[...] = jnp.zeros_like(r)

        du = _bdot(a_ref[...], w_ref[0:1024, :])
        du += _bdot(z_ref[...], w_ref[1024:1280, :])
        du += _bdot(s_ref[...], w_ref[1280:1536, :])
        du += _bdot(x_ref[...], w_ref[1536:2304, :])
        du += _bdot(d_ref[...], wd_ref[...])
        _norm_bwd_step(du, h_ref[...], dr_ref[...], g_ref[...], nw_ref[...], sc_ref[...], dx_ref, dsc_ref, dsh_ref, dnw_ref, dg_ref)

    blk = lambda n: pl.BlockSpec((tb, n), lambda i: (i, 0))
    row = jax.ShapeDtypeStruct((1, D), F32)
    return pl.pallas_call(
        body, name="b_in", grid=(t // tb,),
        in_specs=[blk(1024), blk(256), blk(768), blk(256), blk(DTW), _full((PW, D)), _full((DTW, D)),
                  blk(D), blk(D), blk(D), _row(D), _row(D)],
        out_specs=[blk(D), _row(D), _row(D), _row(D), _row(D)],
        out_shape=[jax.ShapeDtypeStruct((t, D), F32), row, row, row, row], compiler_params=_cparams(1),
    )(dab, dz, dxbc, ds5, ddt, w_main, w_dt, x, dres, gated, nw, sc)


HALO = 16
AB_ROWS = 1024


def _lane_group(shape):
    return lax.broadcasted_iota(jnp.int32, shape, 1) // 64


def _window_select(g, s2, s4, s8, s16):
    return jnp.where(g == 0, s2, jnp.where(g == 1, s4, jnp.where(g == 2, s8, s16)))


def _pool_count(t0, rows):
    g = _lane_group((rows, GW))
    win = _window_select(g, 2, 4, 8, 16)
    tt = t0 + lax.broadcasted_iota(jnp.int32, (rows, GW), 0)
    return jnp.minimum(tt + 1, win).astype(F32)


def _pool_p(v_ext, t0, tb):
    s2 = v_ext + _roll(v_ext, 1)
    s4 = s2 + _roll(s2, 2)
    s8 = s4 + _roll(s4, 4)
    s16 = s8 + _roll(s8, 8)
    ws = _window_select(_lane_group(v_ext.shape), s2, s4, s8, s16)[HALO:]
    return ws / _pool_count(t0, tb) - v_ext[HALO:]


def _sconv(q_ext, w):
    return (_roll(q_ext, 2) * w[0:1] + _roll(q_ext, 1) * w[1:2] + q_ext * w[2:3])[HALO:]


def _halo_specs(t, tb, cols, col_block):
    per = tb // HALO
    last = t // HALO - 1
    prev = pl.BlockSpec((HALO, cols), lambda i: (jnp.maximum(i * per - 1, 0), col_block))
    nxt = pl.BlockSpec((HALO, cols), lambda i: (jnp.minimum((i + 1) * per, last), col_block))
    return prev, nxt


def _f_ab(proj, pool_mat, pool_scale, sconv_w):
    t = proj.shape[0]
    tb = _tblock(t, AB_ROWS)
    prev, _ = _halo_specs(t, tb, 1024, 0)

    def body(p_ref, h_ref, pm_ref, ps_ref, sw_ref, ya_ref, yb_ref):
        i = pl.program_id(0)
        halo = jnp.where(i > 0, h_ref[...], 0.0)
        ext = jnp.concatenate([halo, p_ref[...]], axis=0)
        p = _pool_p(ext[:, 0:256], i * tb, tb)
        ya_ref[...] = _bdot(p, pm_ref[...]) * ps_ref[...]
        q_ext = ext[:, 512:768] * ext[:, 768:1024]
        yb_ref[...] = p_ref[:, 256:512] * _sconv(q_ext, sw_ref[...])

    blk = pl.BlockSpec((tb, GW), lambda i: (i, 0))
    sds = jax.ShapeDtypeStruct((t, GW), F32)
    return pl.pallas_call(
        body, name="f_ab", grid=(t // tb,),
        in_specs=[pl.BlockSpec((tb, 1024), lambda i: (i, 0)), prev, _full((GW, GW)), _row(GW), _full((3, GW))],
        out_specs=[blk, blk], out_shape=[sds, sds], compiler_params=_cparams(1),
    )(proj, proj, pool_mat, pool_scale, sconv_w)


def _b_ab(proj, dya, dyb, pool_mat, pool_scale, sconv_w):
    t = proj.shape[0]
    tb = _tblock(t, AB_ROWS)
    nb = t // tb
    prev, nxt = _halo_specs(t, tb, 1024, 0)
    _, nxt_g = _halo_specs(t, tb, GW, 0)
    n_ext = tb + HALO

    def body(p_ref, hp_ref, hn_ref, da_ref, dan_ref, db_ref, dbn_ref, pm_ref, ps_ref, sw_ref,
             o_ref, dpm_ref, dps_ref, dsw_ref):
        i = pl.program_id(0)

        @pl.when(i == 0)
        def _():
            for r in (dpm_ref, dps_ref, dsw_ref):
                r[...] = jnp.zeros_like(r)

        last = i == nb - 1
        halo = jnp.where(i > 0, hp_ref[...], 0.0)
        main = p_ref[...]
        ext = jnp.concatenate([halo, main], axis=0)
        scale = ps_ref[...]
        pm = pm_ref[...]
        p = _pool_p(ext[:, 0:256], i * tb, tb)
        da = da_ref[...]
        dps_ref[...] += _colsum(da * _bdot(p, pm))
        da_ext = jnp.concatenate([da, jnp.where(last, 0.0, dan_ref[...])], axis=0)
        dys = da_ext * scale
        dpm_ref[...] += _bdot(p, dys[:tb], TN)
        dp = _bdot(dys, pm, NT)
        dpc = dp / _pool_count(i * tb, n_ext)
        a2 = dpc + _roll(dpc, n_ext - 1)
        a4 = a2 + _roll(a2, n_ext - 2)
        a8 = a4 + _roll(a4, n_ext - 4)
        a16 = a8 + _roll(a8, n_ext - 8)
        o_ref[:, 0:256] = (_window_select(_lane_group(dpc.shape), a2, a4, a8, a16) - dp)[:tb]
        w = sw_ref[...]
        gb, gc, hh = main[:, 256:512], main[:, 512:768], main[:, 768:1024]
        q_ext = ext[:, 512:768] * ext[:, 768:1024]
        db = db_ref[...]
        o_ref[:, 256:512] = db * _sconv(q_ext, w)
        gb_next = hn_ref[:, 256:512]
        dconv = jnp.concatenate([db * gb, jnp.where(last, 0.0, dbn_ref[...] * gb_next)], axis=0)
        dq = (dconv * w[2:3] + _roll(dconv, n_ext - 1) * w[1:2] + _roll(dconv, n_ext - 2) * w[0:1])[:tb]
        o_ref[:, 512:768] = dq * hh
        o_ref[:, 768:1024] = dq * gc
        dc = dconv[:tb]
        dsw_ref[0:1, :] += _colsum(dc * _roll(q_ext, 2)[HALO:])
        dsw_ref[1:2, :] += _colsum(dc * _roll(q_ext, 1)[HALO:])
        dsw_ref[2:3, :] += _colsum(dc * q_ext[HALO:])

    blk = pl.BlockSpec((tb, GW), lambda i: (i, 0))
    return pl.pallas_call(
        body, name="b_ab", grid=(nb,),
        in_specs=[pl.BlockSpec((tb, 1024), lambda i: (i, 0)), prev, nxt, blk, nxt_g, blk, nxt_g,
                  _full((GW, GW)), _row(GW), _full((3, GW))],
        out_specs=[pl.BlockSpec((tb, 1024), lambda i: (i, 0)), _full((GW, GW)), _row(GW), _full((3, GW))],
        out_shape=[jax.ShapeDtypeStruct((t, 1024), F32), jax.ShapeDtypeStruct((GW, GW), F32),
                   jax.ShapeDtypeStruct((1, GW), F32), jax.ShapeDtypeStruct((3, GW), F32)],
        compiler_params=_cparams(1),
    )(proj, proj, proj, dya, dya, dyb, dyb, pool_mat, pool_scale, sconv_w)


CH = 8


def _ssd_conv(x, halo, w, b):
    ext = jnp.concatenate([halo, x], axis=0)
    pre = ext * w[3:4] + _roll(ext, 1) * w[2:3] + _roll(ext, 2) * w[1:2] + _roll(ext, 3) * w[0:1] + b
    return pre[CH:], ext


def _ssd_common(dt_raw, dtb, alog):
    ll = dt_raw.shape[0]
    dtv = _softplus(dt_raw + dtb)
    a_row = -jnp.exp(alog)
    r = lax.broadcasted_iota(jnp.int32, (ll, ll), 0)
    c = lax.broadcasted_iota(jnp.int32, (ll, ll), 1)
    tril = (r >= c).astype(F32)
    cs = _dot(tril, dtv * a_row, prec=HI)
    return dtv, a_row, cs, cs.T, r >= c


def _bd(a, b, ca, cb):
    return lax.dot_general(a, b, (((ca,), (cb,)), ((0,), (0,))), preferred_element_type=F32)


def _head_cols(m):
    return jnp.stack([m[:, h:h + 1] for h in range(NH)])


def _ssd_heads(act, dtv, cs, cs_t, causal):
    xs = jnp.stack([act[:, HP * h:HP * (h + 1)] for h in range(NH)])
    bm = jnp.stack([act[:, 256 + NS * (h // 2):256 + NS * (h // 2 + 1)] for h in range(NH)])
    cm = jnp.stack([act[:, 512 + NS * (h // 2):512 + NS * (h // 2 + 1)] for h in range(NH)])
    cs_c = _head_cols(cs)
    cs_r = jnp.stack([cs_t[h:h + 1, :] for h in range(NH)])
    mdec = jnp.where(causal[None], jnp.exp(jnp.minimum(cs_c - cs_r, 0.0)), 0.0)
    g2 = _bd(jnp.stack([cm[0], cm[2]]), jnp.stack([bm[0], bm[2]]), 2, 2)
    sc = jnp.stack([g2[h // 2] for h in range(NH)]) * mdec
    dt_c = _head_cols(dtv)
    xdt = xs * dt_c
    e = jnp.exp(cs_c)
    cs_last = cs_c[:, SSD_L - 1:SSD_L, :]
    wdec = jnp.exp(cs_last - cs_c)
    return xs, bm, cm, mdec, sc, dt_c, xdt, e, cs_last, wdec


def _head_scalars(row_ref):
    return jnp.stack([row_ref[0:1, h:h + 1] for h in range(NH)])


def _f_ssd(proj, dtp, conv_w, conv_b, dt_bias, a_log, d_skip):
    t = proj.shape[0]
    nc = t // SSD_L
    rows = SSD_SUB * SSD_L
    per = rows // CH

    def body(x_ref, hx_ref, dt_ref, z_ref, cw_ref, cb_ref, dtb_ref, al_ref, dk_ref, y_ref, yp_ref, sp_ref, s_ref):
        i = pl.program_id(0)

        @pl.when(i == 0)
        def _():
            s_ref[...] = jnp.zeros_like(s_ref)

        state = s_ref[...]
        dk = _head_scalars(dk_ref)
        for sub in range(SSD_SUB):
            r0 = sub * SSD_L
            rs = slice(r0, r0 + SSD_L)
            halo = jnp.where(i > 0, hx_ref[...], 0.0) if sub == 0 else x_ref[r0 - CH:r0, :]
            pre, _ = _ssd_conv(x_ref[rs, :], halo, cw_ref[...], cb_ref[...])
            act = _silu(pre)
            dtv, _, cs, cs_t, causal = _ssd_common(dt_ref[rs, :], dtb_ref[...], al_ref[...])
            xs, bm, cm, _, sc, _, xdt, e, cs_last, wdec = _ssd_heads(act, dtv, cs, cs_t, causal)
            sp_ref[sub] = state
            y = _bd(sc, xdt, 2, 1) + e * _bd(cm, state, 2, 2) + xs * dk
            for h in range(NH):
                yp_ref[rs, HP * h:HP * (h + 1)] = y[h]
            state = state * jnp.exp(cs_last) + _bd(xdt * wdec, bm, 1, 1)
            y_ref[rs, :] = yp_ref[rs, :] * _silu(z_ref[rs, :])
        s_ref[...] = state

    blk = pl.BlockSpec((rows, GW), lambda i: (i, 0))
    sds = jax.ShapeDtypeStruct((t, GW), F32)
    return pl.pallas_call(
        body, name="f_ssd", grid=(nc // SSD_SUB,),
        in_specs=[pl.BlockSpec((rows, 768), lambda i: (i, 2)),
                  pl.BlockSpec((CH, 768), lambda i: (jnp.maximum(i * per - 1, 0), 2)),
                  pl.BlockSpec((rows, DTW), lambda i: (i, 0)),
                  pl.BlockSpec((rows, GW), lambda i: (i, 4)),
                  _full((4, 768)), _row(768), _row(DTW), _row(DTW), _row(DTW)],
        out_specs=[blk, blk, pl.BlockSpec((SSD_SUB, NH, HP, NS), lambda i: (i, 0, 0, 0))],
        out_shape=[sds, sds, jax.ShapeDtypeStruct((nc, NH, HP, NS), F32)],
        scratch_shapes=[pltpu.VMEM((NH, HP, NS), F32)], compiler_params=_cparams(1),
    )(proj, proj, dtp, proj, conv_w, conv_b, dt_bias, a_log, d_skip)


def _b_ssd(proj, dtp, ypre, dyc, sprev, conv_w, conv_b, dt_bias, a_log, d_skip):
    t = proj.shape[0]
    nc = t // SSD_L
    steps = nc // SSD_SUB_BWD
    rows = SSD_SUB_BWD * SSD_L
    per = rows // CH
    n_ext = SSD_L + CH

    def chunk(sub, halo, dnext, ds_in, refs):
        (x_ref, dt_ref, z_ref, yp_ref, dy_ref, sp_ref, cw_ref, cb_ref, dtb_ref, al_ref, dk_ref,
         dz_ref, dx_ref, ddt_ref, dact_ref) = refs
        rs = slice(sub * SSD_L, (sub + 1) * SSD_L)
        dact = dact_ref.at[sub]
        w = cw_ref[...]
        pre, ext = _ssd_conv(x_ref[rs, :], halo, w, cb_ref[...])
        act = _silu(pre)
        dt_raw = dt_ref[rs, :]
        dtv, a_row, cs, cs_t, causal = _ssd_common(dt_raw, dtb_ref[...], al_ref[...])
        z = z_ref[rs, :]
        dyc_v = dy_ref[rs, :]
        dz_ref[rs, :] = dyc_v * yp_ref[rs, :] * _dsilu(z)
        dy_all = dyc_v * _silu(z)
        lane = lax.broadcasted_iota(jnp.int32, (SSD_L, DTW), 1)
        rowi = lax.broadcasted_iota(jnp.int32, (1, SSD_L, 1), 1)
        lane1 = lax.broadcasted_iota(jnp.int32, (1, DTW), 1)
        xs, bm, cm, mdec, sc, dt_c, xdt, e, cs_last, wdec = _ssd_heads(act, dtv, cs, cs_t, causal)
        dy = jnp.stack([dy_all[:, HP * h:HP * (h + 1)] for h in range(NH)])
        prev = sp_ref[sub]
        ds = ds_in
        lsum = lambda v: jnp.sum(v, axis=2, keepdims=True)
        dsc = _bd(dy, xdt, 2, 2)
        q = dsc * sc
        dg = dsc * mdec
        dxdt = _bd(sc, dy, 1, 1)
        dcs = lsum(q) - lsum(jnp.swapaxes(q, 1, 2))
        dc = _bd(dg, bm, 2, 1)
        db = _bd(dg, cm, 1, 1)
        cp = _bd(cm, prev, 2, 2)
        dcs += lsum(dy * cp) * e
        ey = e * dy
        dc += _bd(ey, prev, 2, 1)
        dprev = _bd(ey, cm, 1, 1)
        elast = jnp.exp(cs_last)
        dprev += ds * elast
        dcs_last = jnp.sum(lsum(ds * prev), axis=1, keepdims=True) * elast
        bds = _bd(bm, ds, 2, 2)
        dxdt += wdec * bds
        db += wdec * _bd(xdt, ds, 2, 1)
        dw = lsum(xdt * bds) * wdec
        dcs -= dw
        dcs_last += jnp.sum(dw, axis=1, keepdims=True)
        dcs += jnp.where(rowi == SSD_L - 1, dcs_last, 0.0)
        dxs = dxdt * dt_c + dy * _head_scalars(dk_ref)
        ddtx = lsum(dxdt * xs)
        ddk = jnp.sum(lsum(dy * xs), axis=1, keepdims=True)
        dcs_mat = jnp.zeros((SSD_L, DTW), F32)
        ddtx_mat = jnp.zeros((SSD_L, DTW), F32)
        ddk_row = jnp.zeros((1, DTW), F32)
        for h in range(NH):
            dact[:, HP * h:HP * (h + 1)] = dxs[h]
            dcs_mat = jnp.where(lane == h, dcs[h], dcs_mat)
            ddtx_mat = jnp.where(lane == h, ddtx[h], ddtx_mat)
            ddk_row = jnp.where(lane1 == h, ddk[h], ddk_row)
        for g in range(2):
            dact[:, 256 + NS * g:256 + NS * (g + 1)] = db[2 * g] + db[2 * g + 1]
            dact[:, 512 + NS * g:512 + NS * (g + 1)] = dc[2 * g] + dc[2 * g + 1]
        ds_out = dprev
        r2 = lax.broadcasted_iota(jnp.int32, (SSD_L, SSD_L), 0)
        c2 = lax.broadcasted_iota(jnp.int32, (SSD_L, SSD_L), 1)
        dadt = _dot((c2 >= r2).astype(F32), dcs_mat, prec=HI)
        ddt = jnp.where(lane < NH, (dadt * a_row + ddtx_mat) * _sig(dt_raw + dtb_ref[...]), 0.0)
        ddt_ref[rs, :] = ddt
        dpre = dact[...] * _dsilu(pre)
        dcw = jnp.concatenate([_colsum(dpre * _roll(ext, 3 - k)[CH:]) for k in range(4)], axis=0)
        dext = jnp.concatenate([dpre, dnext], axis=0)
        dx_ref[rs, :] = (dext * w[3:4] + _roll(dext, n_ext - 1) * w[2:3] + _roll(dext, n_ext - 2) * w[1:2]
                         + _roll(dext, n_ext - 3) * w[0:1])[:SSD_L]
        acc = (dcw, _colsum(dpre), _colsum(ddt), _colsum(dadt * dtv) * a_row, ddk_row)
        return dpre[0:CH], ds_out, acc

    def body(x_ref, hx_ref, dt_ref, z_ref, yp_ref, dy_ref, sp_ref, cw_ref, cb_ref, dtb_ref, al_ref, dk_ref,
             dz_ref, dx_ref, ddt_ref, dcw_ref, dcb_ref, ddtb_ref, dal_ref, ddk_ref, ds_ref, dnext_ref, dact_ref):
        i = pl.program_id(0)
        acc_refs = (dcw_ref, dcb_ref, ddtb_ref, dal_ref, ddk_ref)

        @pl.when(i == 0)
        def _():
            ds_ref[...] = jnp.zeros_like(ds_ref)
            dnext_ref[...] = jnp.zeros_like(dnext_ref)
            for r in acc_refs:
                r[...] = jnp.zeros_like(r)

        refs = (x_ref, dt_ref, z_ref, yp_ref, dy_ref, sp_ref, cw_ref, cb_ref, dtb_ref, al_ref, dk_ref, dz_ref, dx_ref, ddt_ref,
                dact_ref)
        ds = ds_ref[...]
        dnext = dnext_ref[...]
        total = None
        for sub in reversed(range(SSD_SUB_BWD)):
            if sub == 0:
                halo = jnp.where(i == steps - 1, 0.0, hx_ref[...])
            else:
                halo = x_ref[sub * SSD_L - CH:sub * SSD_L, :]
            dnext, ds, acc = chunk(sub, halo, dnext, ds, refs)
            total = acc if total is None else tuple(a + b for a, b in zip(total, acc))
        ds_ref[...] = ds
        dnext_ref[...] = dnext
        for r, v in zip(acc_refs, total):
            r[...] += v

    rev = lambda i: steps - 1 - i
    blk = lambda n, cb=0: pl.BlockSpec((rows, n), lambda i: (rev(i), cb))
    row = lambda n: jax.ShapeDtypeStruct((1, n), F32)
    return pl.pallas_call(
        body, name="b_ssd", grid=(steps,),
        in_specs=[blk(768, 2), pl.BlockSpec((CH, 768), lambda i: (jnp.maximum(rev(i) * per - 1, 0), 2)),
                  blk(DTW), blk(GW, 4), blk(GW), blk(GW), pl.BlockSpec((SSD_SUB_BWD, NH, HP, NS), lambda i: (rev(i), 0, 0, 0)),
                  _full((4, 768)), _row(768), _row(DTW), _row(DTW), _row(DTW)],
        out_specs=[blk(GW), blk(768), blk(DTW), _full((4, 768)), _row(768), _row(DTW), _row(DTW), _row(DTW)],
        out_shape=[jax.ShapeDtypeStruct((t, GW), F32), jax.ShapeDtypeStruct((t, 768), F32), jax.ShapeDtypeStruct((t, DTW), F32),
                   jax.ShapeDtypeStruct((4, 768), F32), row(768), row(DTW), row(DTW), row(DTW)],
        scratch_shapes=[pltpu.VMEM((NH, HP, NS), F32), pltpu.VMEM((CH, 768), F32), pltpu.VMEM((SSD_SUB_BWD, SSD_L, 768), F32)],
        compiler_params=_cparams(1),
    )(proj, proj, dtp, proj, ypre, dyc, sprev, conv_w, conv_b, dt_bias, a_log, d_skip)


def _s5_block(t):
    return min(t, 1024)


def _seg_t():
    r = lax.broadcasted_iota(jnp.int32, (64, 1024), 0)
    c = lax.broadcasted_iota(jnp.int32, (64, 1024), 1)
    return (c // 16 == r).astype(F32)


def _s5_prep_math(a_re, a_im, lstep, b_re, b_im):
    step = jnp.exp(lstep)
    ars = a_re * step
    ais = a_im * step
    mag = jnp.exp(ars)
    lr = mag * jnp.cos(ais)
    li = mag * jnp.sin(ais)
    den = a_re * a_re + a_im * a_im
    nr = lr - 1.0
    f_re = (nr * a_re + li * a_im) / den
    f_im = (li * a_re - nr * a_im) / den
    seg = _seg_t()
    fr = _dot(f_re, seg, prec=HI)
    fi = _dot(f_im, seg, prec=HI)
    return lr, li, fr * b_re - fi * b_im, fr * b_im + fi * b_re, ars, ais


def _s5_prep(a_re, a_im, lstep, b_re, b_im):
    def body(ar, ai, ls, br, bi, lr_o, li_o, bbr_o, bbi_o, ars_o, ais_o):
        outs = _s5_prep_math(ar[...], ai[...], ls[...], br[...], bi[...])
        for o, v in zip((lr_o, li_o, bbr_o, bbi_o, ars_o, ais_o), outs):
            o[...] = v

    s64 = jax.ShapeDtypeStruct((16, 64), F32)
    s1k = jax.ShapeDtypeStruct((16, 1024), F32)
    return pl.pallas_call(body, name="s5_prep", out_shape=[s64, s64, s1k, s1k, s64, s64])(a_re, a_im, lstep, b_re, b_im)


def _s5_prep_bwd(a_re, a_im, lstep, b_re, b_im, dlr, dli, dbbr, dbbi):
    def body(ar, ai, ls, br, bi, g0, g1, g2, g3, o0, o1, o2, o3, o4):
        f = lambda *a: _s5_prep_math(*a)[:4]
        _, vjp = jax.vjp(f, ar[...], ai[...], ls[...], br[...], bi[...])
        for o, v in zip((o0, o1, o2, o3, o4), vjp((g0[...], g1[...], g2[...], g3[...]))):
            o[...] = v

    s64 = jax.ShapeDtypeStruct((16, 64), F32)
    s1k = jax.ShapeDtypeStruct((16, 1024), F32)
    return pl.pallas_call(body, name="s5_prep_bwd", out_shape=[s64, s64, jax.ShapeDtypeStruct((16, 1), F32), s1k, s1k])(
        a_re, a_im, lstep, b_re, b_im, dlr, dli, dbbr, dbbi)


SUB = 8


def _s5_tables(ars, ais):
    def body(ar, ai, tr, ti):
        rr = lax.broadcasted_iota(jnp.int32, (8 * SUB, S5_P), 0)
        seg, r = rr // SUB, rr % SUB
        step = jnp.where((seg == 1) | (seg == 4), 1, jnp.where((seg == 2) | (seg == 5), 2, 4))
        n = jnp.where(seg == 0, r + 1, jnp.where(seg == 7, SUB - r, step))
        fwd_gap = jnp.where(seg <= 3, r - step, SUB - step - 1 - r)
        gap = jnp.where((seg == 0) | (seg == 7), 0, fwd_gap)
        nf = n.astype(F32)
        mag = jnp.where(gap >= 0, jnp.exp(nf * ar[...]), 0.0)
        tr[...] = mag * jnp.cos(nf * ai[...])
        ti[...] = mag * jnp.sin(nf * ai[...])

    sds = jax.ShapeDtypeStruct((8 * SUB, S5_P), F32)
    return pl.pallas_call(body, name="s5_tables", out_shape=[sds] * 2)(ars, ais)


def _s5_table(tb_r, tb_i, k):
    return tb_r[SUB * k:SUB * (k + 1), :], tb_i[SUB * k:SUB * (k + 1), :]


def _s5_scan(bu_r, bu_i, tb_r, tb_i, c_r, c_i, lb):
    nt = lb // SUB
    sr, si = bu_r.reshape(nt, SUB, S5_P), bu_i.reshape(nt, SUB, S5_P)
    for j, k in enumerate((1, 2, 4)):
        mr, mi = _s5_table(tb_r, tb_i, 1 + j)
        tr, ti = pltpu.roll(sr, k, axis=1), pltpu.roll(si, k, axis=1)
        sr, si = sr + mr * tr - mi * ti, si + mr * ti + mi * tr
    pr, pi = _s5_table(tb_r, tb_i, 0)
    out_r, out_i = [], []
    for j in range(nt):
        a_r = sr[j] + pr * c_r - pi * c_i
        a_i = si[j] + pr * c_i + pi * c_r
        out_r.append(a_r)
        out_i.append(a_i)
        c_r, c_i = a_r[SUB - 1:SUB], a_i[SUB - 1:SUB]
    return jnp.concatenate(out_r, axis=0), jnp.concatenate(out_i, axis=0)


def _s5_rscan(g_r, g_i, tb_r, tb_i, n_r, n_i, lb):
    nt = lb // SUB
    gr, gi = g_r.reshape(nt, SUB, S5_P), g_i.reshape(nt, SUB, S5_P)
    for j, k in enumerate((1, 2, 4)):
        mr, mi = _s5_table(tb_r, tb_i, 4 + j)
        tr, ti = pltpu.roll(gr, SUB - k, axis=1), pltpu.roll(gi, SUB - k, axis=1)
        gr, gi = gr + mr * tr + mi * ti, gi + mr * ti - mi * tr
    qr, qi = _s5_table(tb_r, tb_i, 7)
    out_r, out_i = [None] * nt, [None] * nt
    for j in reversed(range(nt)):
        a_r = gr[j] + qr * n_r + qi * n_i
        a_i = gi[j] + qr * n_i - qi * n_r
        out_r[j], out_i[j] = a_r, a_i
        n_r, n_i = a_r[0:1], a_i[0:1]
    return jnp.concatenate(out_r, axis=0), jnp.concatenate(out_i, axis=0)


def _s5_y(u, sr, si, cre, cim, dsk):
    return _bdot(sr, cre) + _bdot(si, cim) + dsk * u


def _f_s5(proj, bmat, cre, cim, p_r, p_i, dsk, glu_w, glu_b):
    t = proj.shape[0]
    lb = _s5_block(t)
    nb = t // lb

    def body(u_ref, bm_ref, cr_ref, ci_ref, pr_ref, pi_ref, dk_ref, gw_ref, gb_ref, y_ref, car_ref, s_ref, st_ref):
        @pl.when(pl.program_id(0) == 0)
        def _():
            st_ref[...] = jnp.zeros_like(st_ref)

        u = u_ref[...]
        bu = _bdot(u, bm_ref[...])
        c_r, c_i = st_ref[0:1, 0:S5_P], st_ref[0:1, S5_P:]
        car_ref[0] = st_ref[0:1, :]
        sr, si = _s5_scan(bu[:, :S5_P], bu[:, S5_P:], pr_ref, pi_ref, c_r, c_i, lb)
        st_ref[0:1, 0:S5_P] = sr[lb - 1:lb]
        st_ref[0:1, S5_P:] = si[lb - 1:lb]
        sr_b, si_b = sr.astype(BF16), si.astype(BF16)
        s_ref[:, 0:S5_P] = sr_b
        s_ref[:, S5_P:] = si_b
        gel = _gelu(_s5_y(u, sr_b, si_b, cr_ref[...], ci_ref[...], dk_ref[...]))
        y_ref[...] = gel * _sig(_bdot(gel, gw_ref[...]) + gb_ref[...])

    return pl.pallas_call(
        body, name="f_s5", grid=(nb,),
        in_specs=[pl.BlockSpec((lb, GW), lambda i: (i, 5)),
                  _full((GW, 2 * S5_P)), _full((S5_P, GW)), _full((S5_P, GW)), _full((8 * SUB, S5_P)), _full((8 * SUB, S5_P)),
                  _row(GW), _full((GW, GW)), _row(GW)],
        out_specs=[pl.BlockSpec((lb, GW), lambda i: (i, 0)), pl.BlockSpec((1, 1, 2 * S5_P), lambda i: (i, 0, 0)),
                   pl.BlockSpec((lb, 2 * S5_P), lambda i: (i, 0))],
        out_shape=[jax.ShapeDtypeStruct((t, GW), F32), jax.ShapeDtypeStruct((nb, 1, 2 * S5_P), F32),
                   jax.ShapeDtypeStruct((t, 2 * S5_P), BF16)],
        scratch_shapes=[pltpu.VMEM((8, 2 * S5_P), F32)], compiler_params=_cparams(1),
    )(proj, bmat, cre, cim, p_r, p_i, dsk, glu_w, glu_b)


def _b_s5(proj, dyd, carries, states, bmat, cre, cim, p_r, p_i, dsk, glu_w, glu_b):
    t = proj.shape[0]
    lb = _s5_block(t)
    nb = t // lb

    def body(u_ref, dy_ref, car_ref, s_ref, bm_ref, cr_ref, ci_ref, pr_ref, pi_ref, dk_ref, gw_ref, gb_ref,
             du_ref, dbm_ref, dcr_ref, dci_ref, dlam_ref, ddk_ref, dgw_ref, dgb_ref, gc_ref):
        @pl.when(pl.program_id(0) == 0)
        def _():
            gc_ref[...] = jnp.zeros_like(gc_ref)
            for r in (dbm_ref, dcr_ref, dci_ref, dlam_ref, ddk_ref, dgw_ref, dgb_ref):
                r[...] = jnp.zeros_like(r)

        u = u_ref[...]
        bm = bm_ref[...]
        u_b = u.astype(BF16)
        c_r, c_i = car_ref[0, 0:1, 0:S5_P], car_ref[0, 0:1, S5_P:]
        cre_v, cim_v, dk, gw = cr_ref[...], ci_ref[...], dk_ref[...], gw_ref[...]
        sr_b, si_b = s_ref[:, 0:S5_P], s_ref[:, S5_P:]
        sr, si = sr_b.astype(F32), si_b.astype(F32)
        y = _dot(sr_b, cre_v) + _dot(si_b, cim_v) + dk * u
        gel = _gelu(y)
        gel_b = gel.astype(BF16)
        gate = _sig(_dot(gel_b, gw) + gb_ref[...])
        dout = dy_ref[...]
        t1 = dout * gel * gate * (1.0 - gate)
        t1_b = t1.astype(BF16)
        dgw_ref[...] += _dot(gel_b, t1_b, TN)
        dgb_ref[...] += _colsum(t1)
        dyv = (dout * gate + _dot(t1_b, gw, NT)) * _dgelu(y)
        dyv_b = dyv.astype(BF16)
        ddk_ref[...] += _colsum(dyv * u)
        dcr_ref[...] += _dot(sr_b, dyv_b, TN)
        dci_ref[...] += _dot(si_b, dyv_b, TN)
        gr = _dot(dyv_b, cre_v, NT)
        gi = _dot(dyv_b, cim_v, NT)
        row = lax.broadcasted_iota(jnp.int32, (lb, S5_P), 0)
        n_r, n_i = gc_ref[0:1, 0:S5_P], gc_ref[0:1, S5_P:]
        gr, gi = _s5_rscan(gr, gi, pr_ref, pi_ref, n_r, n_i, lb)
        gc_ref[0:1, 0:S5_P] = gr[0:1]
        gc_ref[0:1, S5_P:] = gi[0:1]
        gcat = jnp.concatenate([gr, gi], axis=1).astype(BF16)
        dbm_ref[...] += _dot(u_b, gcat, TN)
        du_ref[...] = dyv * dk + _dot(gcat, bm, NT)
        spr = jnp.where(row >= 1, _roll(sr, 1), c_r)
        spi = jnp.where(row >= 1, _roll(si, 1), c_i)
        dlam_ref[0:1, :] += _colsum(gr * spr + gi * spi)
        dlam_ref[1:2, :] += _colsum(gi * spr - gr * spi)

    rev = lambda i: nb - 1 - i
    return pl.pallas_call(
        body, name="b_s5", grid=(nb,),
        in_specs=[pl.BlockSpec((lb, GW), lambda i: (rev(i), 5)), pl.BlockSpec((lb, GW), lambda i: (rev(i), 0)),
                  pl.BlockSpec((1, 1, 2 * S5_P), lambda i: (rev(i), 0, 0)), pl.BlockSpec((lb, 2 * S5_P), lambda i: (rev(i), 0)),
                  _full((GW, 2 * S5_P)), _full((S5_P, GW)), _full((S5_P, GW)), _full((8 * SUB, S5_P)), _full((8 * SUB, S5_P)),
                  _row(GW), _full((GW, GW)), _row(GW)],
        out_specs=[pl.BlockSpec((lb, GW), lambda i: (rev(i), 0)), _full((GW, 2 * S5_P)), _full((S5_P, GW)), _full((S5_P, GW)),
                   _full((2, S5_P)), _row(GW), _full((GW, GW)), _row(GW)],
        out_shape=[jax.ShapeDtypeStruct((t, GW), F32), jax.ShapeDtypeStruct((GW, 2 * S5_P), F32),
                   jax.ShapeDtypeStruct((S5_P, GW), F32), jax.ShapeDtypeStruct((S5_P, GW), F32),
                   jax.ShapeDtypeStruct((2, S5_P), F32), jax.ShapeDtypeStruct((1, GW), F32),
                   jax.ShapeDtypeStruct((GW, GW), F32), jax.ShapeDtypeStruct((1, GW), F32)],
        scratch_shapes=[pltpu.VMEM((8, 2 * S5_P), F32)], compiler_params=_cparams(1),
    )(proj, dyd, carries, states, bmat, cre, cim, p_r, p_i, dsk, glu_w, glu_b)


def _group_norm(ys, bw):
    outs, stats = [], []
    for g, y in enumerate(ys):
        r, n = _rms(y)
        stats.append((r, n))
        outs.append(n * bw[:, GW * g:GW * (g + 1)])
    return jnp.concatenate(outs, axis=1), stats


def _f_out(ya, yb, yc, yd, bw, wts, l, h, g1):
    t = h.shape[0]
    tb = _tblock(t)

    def body(a_ref, b_ref, c_ref, d_ref, bw_ref, w_ref, h_ref, g_ref, h2_ref, o_ref, cat_ref):
        cat, _ = _group_norm([a_ref[...], b_ref[...], c_ref[...], d_ref[...]], bw_ref[...])
        catb = cat.astype(BF16)
        cat_ref[...] = catb
        o = _dot(catb, w_ref[0].reshape(D, D))
        o_ref[...] = o.astype(BF16)
        h2_ref[...] = h_ref[...] + g_ref[...] * o

    yblk = pl.BlockSpec((tb, GW), lambda i: (i, 0))
    blk = pl.BlockSpec((tb, D), lambda i: (i, 0))
    return pl.pallas_call(
        body, name="f_out", grid=(t // tb,), in_specs=[yblk] * 4 + [_row(D), _wout_spec(l), blk, _row(D)],
        out_specs=[blk, blk, blk],
        out_shape=[jax.ShapeDtypeStruct((t, D), F32), jax.ShapeDtypeStruct((t, D), BF16), jax.ShapeDtypeStruct((t, D), BF16)],
        compiler_params=_cparams(1),
    )(ya, yb, yc, yd, bw, wts, h, g1)


def _b_out(dv, h2, dh3, m, nw2, sc2, ya, yb, yc, yd, bw, wts, l, g1):
    t = dv.shape[0]
    tb = _tblock(t)

    def body(dv_ref, h2_ref, dh3_ref, m_ref, nw_ref, sc_ref, a_ref, b_ref, c_ref, d_ref, bw_ref, w_ref, g_ref,
             dh_ref, dsc_ref, dsh_ref, dnw_ref, dg_ref, da_ref, db_ref, dc_ref, dd_ref, do_ref, dbw_ref):
        @pl.when(pl.program_id(0) == 0)
        def _():
            for r in (dsc_ref, dsh_ref, dnw_ref, dg_ref, dbw_ref):
                r[...] = jnp.zeros_like(r)

        _norm_bwd_step(dv_ref[...], h2_ref[...], dh3_ref[...], m_ref[...], nw_ref[...], sc_ref[...],
                       dh_ref, dsc_ref, dsh_ref, dnw_ref, dg_ref)
        do = (dh_ref[...] * g_ref[...]).astype(BF16)
        do_ref[...] = do
        dcat = _dot(do, w_ref[0].reshape(D, D), NT)
        bw_v = bw_ref[...]
        for g, (y_ref, dy_ref) in enumerate(((a_ref, da_ref), (b_ref, db_ref), (c_ref, dc_ref), (d_ref, dd_ref))):
            r, n = _rms(y_ref[...])
            dc = dcat[:, GW * g:GW * (g + 1)]
            dbw_ref[:, GW * g:GW * (g + 1)] += _colsum(dc * n)
            dy_ref[...] = _rms_bwd(r, n, dc * bw_v[:, GW * g:GW * (g + 1)])

    yblk = pl.BlockSpec((tb, GW), lambda i: (i, 0))
    blk = pl.BlockSpec((tb, D), lambda i: (i, 0))
    ysd = jax.ShapeDtypeStruct((t, GW), F32)
    row = jax.ShapeDtypeStruct((1, D), F32)
    return pl.pallas_call(
        body, name="b_out", grid=(t // tb,),
        in_specs=[blk] * 4 + [_row(D), _row(D)] + [yblk] * 4 + [_row(D), _wout_spec(l), _row(D)],
        out_specs=[blk, _row(D), _row(D), _row(D), _row(D)] + [yblk] * 4 + [blk, _row(D)],
        out_shape=[jax.ShapeDtypeStruct((t, D), F32), row, row, row, row] + [ysd] * 4 + [jax.ShapeDtypeStruct((t, D), BF16), row],
        compiler_params=_cparams(1),
    )(dv, h2, dh3, m, nw2, sc2, ya, yb, yc, yd, bw, wts, g1)


HB = 512
MLP_ROWS = 1024


ROW_W1, ROW_W2, ROW_WOUT, ROW_WIN = 0, D, D + HID // 4, D + HID // 4 + D // 4
PACK_ROWS = ROW_WIN + WIN_ROWS


def _w1_spec(l):
    per = HID // 4 // HB
    return pl.BlockSpec((1, 1, D, HB), lambda i, k: (l, k // per, ROW_W1 // D, k % per))


def _w2_spec(l):
    per = HID // 4 // HB
    return pl.BlockSpec((1, 1, HB, D), lambda i, k: (l, k // per, ROW_W2 // HB + k % per, 0))


def _wout_spec(l):
    return pl.BlockSpec((1, 4, D // 4, D), lambda i: (l, 0, ROW_WOUT // (D // 4), 0))


def _f_mlp(h2, nw, sc, sh, g2, wts, l):
    t = h2.shape[0]
    tb = _tblock(t, MLP_ROWS)
    nk = HID // HB

    def body(h_ref, nw_ref, sc_ref, sh_ref, g_ref, w1_ref, w2_ref, h3_ref, m_ref, a_ref, v_ref, acc_ref):
        k = pl.program_id(1)

        @pl.when(k == 0)
        def _():
            _, n = _rms(h_ref[...])
            v_ref[...] = ((n * nw_ref[...]) * (1.0 + sc_ref[...]) + sh_ref[...]).astype(BF16)
            acc_ref[...] = jnp.zeros_like(acc_ref)

        a = _dot(v_ref[...], w1_ref[0, 0])
        a_ref[...] = a.astype(BF16)
        ra = jnp.maximum(a, 0.0)
        acc_ref[...] += _dot((ra * ra).astype(BF16), w2_ref[0, 0])

        @pl.when(k == nk - 1)
        def _():
            m = acc_ref[...]
            m_ref[...] = m.astype(BF16)
            h3_ref[...] = h_ref[...] + g_ref[...] * m

    blk = pl.BlockSpec((tb, D), lambda i, k: (i, 0))
    return pl.pallas_call(
        body, name="f_mlp", grid=(t // tb, nk),
        in_specs=[blk, _row(D), _row(D), _row(D), _row(D), _w1_spec(l), _w2_spec(l)],
        out_specs=[blk, blk, pl.BlockSpec((tb, HB), lambda i, k: (i, k)), blk],
        out_shape=[jax.ShapeDtypeStruct((t, D), F32), jax.ShapeDtypeStruct((t, D), BF16), jax.ShapeDtypeStruct((t, HID), BF16),
                   jax.ShapeDtypeStruct((t, D), BF16)],
        scratch_shapes=[pltpu.VMEM((tb, D), F32)], compiler_params=_cparams(2),
    )(h2, nw, sc, sh, g2, wts, wts)


def _b_mlp(dh3, a, g2, wts, l):
    t = dh3.shape[0]
    tb = _tblock(t, MLP_ROWS)
    nk = HID // HB

    def body(dh_ref, a_ref, g_ref, w1_ref, w2_ref, dv_ref, da_ref, act_ref, dm_ref):
        k = pl.program_id(1)
        dm = (dh_ref[...] * g_ref[...]).astype(BF16)

        @pl.when(k == 0)
        def _():
            dm_ref[...] = dm
            dv_ref[...] = jnp.zeros_like(dv_ref)

        ra = jnp.maximum(a_ref[...].astype(F32), 0.0)
        act_ref[...] = (ra * ra).astype(BF16)
        da = (_dot(dm, w2_ref[0, 0], NT) * (2.0 * ra)).astype(BF16)
        da_ref[...] = da
        dv_ref[...] += _dot(da, w1_ref[0, 0], NT)

    blk = pl.BlockSpec((tb, D), lambda i, k: (i, 0))
    hblk = pl.BlockSpec((tb, HB), lambda i, k: (i, k))
    return pl.pallas_call(
        body, name="b_mlp", grid=(t // tb, nk),
        in_specs=[blk, hblk, _row(D), _w1_spec(l), _w2_spec(l)],
        out_specs=[blk, hblk, hblk, blk],
        out_shape=[jax.ShapeDtypeStruct((t, D), F32), jax.ShapeDtypeStruct((t, HID), BF16), jax.ShapeDtypeStruct((t, HID), BF16),
                   jax.ShapeDtypeStruct((t, D), BF16)],
        compiler_params=_cparams(2),
    )(dh3, a, g2, wts, wts)


def _b_final(h, tgt, fw):
    t = h.shape[0]
    tb = _tblock(t)

    def body(h_ref, t_ref, w_ref, dh_ref, loss_ref, dfw_ref):
        @pl.when(pl.program_id(0) == 0)
        def _():
            loss_ref[...] = jnp.zeros_like(loss_ref)
            dfw_ref[...] = jnp.zeros_like(dfw_ref)

        r, n = _rms(h_ref[...])
        wv = w_ref[...]
        err = n * wv - t_ref[...]
        loss_ref[...] += jnp.sum(err * err, keepdims=True) * (0.5 / D)
        dy = err * (1.0 / D)
        dfw_ref[...] += _colsum(dy * n)
        dh_ref[...] = _rms_bwd(r, n, dy * wv)

    blk = pl.BlockSpec((tb, D), lambda i: (i, 0))
    return pl.pallas_call(
        body, name="b_final", grid=(t // tb,), in_specs=[blk, blk, _row(D)], out_specs=[blk, _row(1), _row(D)],
        out_shape=[jax.ShapeDtypeStruct((t, D), F32), jax.ShapeDtypeStruct((1, 1), F32), jax.ShapeDtypeStruct((1, D), F32)],
        compiler_params=_cparams(1),
    )(h, tgt, fw)


def _eye(n):
    return jnp.eye(n, dtype=F32)


def _pool_embed(pool_w):
    return jnp.einsum('gcd,gk->gckd', pool_w, _eye(4)).reshape(GW, GW)


def _pool_extract(m):
    return jnp.einsum('gcgd->gcd', m.reshape(4, 64, 4, 64))


def _bmat_embed(bb):
    return jnp.einsum('gph,gk->ghkp', bb, _eye(16)).reshape(GW, S5_P)


def _bmat_extract(m):
    return jnp.einsum('ghgp->gph', m.reshape(16, 16, 16, 64))


def _cmat_embed(cc):
    return jnp.einsum('ghp,gk->kpgh', cc, _eye(16)).reshape(S5_P, GW)


def _cmat_extract(m):
    return jnp.einsum('gpgh->ghp', m.reshape(16, 64, 16, 16))


def _pad_lanes(v, n=DTW):
    return jnp.pad(v.reshape(1, -1), ((0, 0), (0, n - v.shape[-1])))


def _w_in_layout(w_in_t):
    w_main = jnp.concatenate([w_in_t[:1280], w_in_t[2052:2308], w_in_t[1280:2048]], axis=0)
    return w_main, jnp.pad(w_in_t[2048:2052], ((0, DTW - 4), (0, 0)))


def _layer_params(p, l, mod, w_in, rest):
    q = {'rest': rest, 'l': l}
    q['mod'] = [mod[k:k + 1] for k in range(6)]
    q['nw1'] = p['norm_mix_w'][l:l + 1]
    q['nw2'] = p['norm_mlp_w'][l:l + 1]
    q['w_main'], q['w_dt'] = _w_in_layout(w_in)
    q['pool_mat'] = _pool_embed(p['pool_w'][l]).astype(BF16)
    q['pool_scale'] = p['pool_scale'][l:l + 1]
    q['sconv_w'] = p['sconv_w'][l]
    q['conv_w'] = p['ssd_conv_w'][l]
    q['conv_b'] = p['ssd_conv_b'][l:l + 1]
    q['dt_bias'] = _pad_lanes(p['ssd_dt_bias'][l])
    q['a_log'] = _pad_lanes(p['ssd_a_log'][l])
    q['ssd_d'] = _pad_lanes(p['ssd_d'][l])
    q['s5_raw'] = (p['s5_a_re'][l], p['s5_a_im'][l], p['s5_log_step'][l].reshape(16, 1),
                   p['s5_b_re'][l].reshape(16, 1024), p['s5_b_im'][l].reshape(16, 1024))
    q['cre'] = _cmat_embed(p['s5_c_re'][l]).astype(BF16)
    q['cim'] = (-_cmat_embed(p['s5_c_im'][l])).astype(BF16)
    q['s5_d'] = p['s5_d'][l:l + 1]
    q['glu_w'] = p['s5_glu_w'][l].astype(BF16)
    q['glu_b'] = p['s5_glu_b'][l:l + 1]
    q['bw'] = p['branch_norm_w'][l:l + 1]
    return q


def _layer_fwd(h, q):
    sh1, sc1, g1, sh2, sc2, g2 = q['mod']
    t = h.shape[0]
    s = {'h': h}
    s['proj'], s['dtp'], s['u'] = _f_in(h, q['nw1'], sc1, sh1, q['w_main'], q['w_dt'])
    s['ya'], s['yb'] = _f_ab(s['proj'], q['pool_mat'], q['pool_scale'], q['sconv_w'])
    s['yc'], s['ypre'], s['sprev'] = _f_ssd(s['proj'], s['dtp'], q['conv_w'], q['conv_b'], q['dt_bias'], q['a_log'], q['ssd_d'])
    lr, li, bbr, bbi, ars, ais = _s5_prep(*q['s5_raw'])
    s['bmat'] = jnp.concatenate([_bmat_embed(bbr.reshape(16, 64, 16)), _bmat_embed(bbi.reshape(16, 64, 16))],
                                axis=1).astype(BF16)
    s['tables'] = _s5_tables(ars.reshape(1, S5_P), ais.reshape(1, S5_P))
    s['yd'], s['carries'], s['states'] = _f_s5(s['proj'], s['bmat'], q['cre'], q['cim'], s['tables'][0], s['tables'][1],
                                  q['s5_d'], q['glu_w'], q['glu_b'])
    q['wts'] = q['rest']((s['ya'], s['yc'], s['yd']))
    s['h2'], s['o'], s['cat'] = _f_out(s['ya'], s['yb'], s['yc'], s['yd'], q['bw'], q['wts'], q['l'], h, g1)
    h3, s['m'], s['a'], s['v'] = _f_mlp(s['h2'], q['nw2'], sc2, sh2, g2, q['wts'], q['l'])
    return h3, s


STACKED = {'mlp_w1': (2, 4, D, HID // 4), 'mlp_w2': (2, HID, D), 'w_out': (2, D, D)}


def _layer_bwd(dh3, q, s, l, stacked, early=None):
    sh1, sc1, g1, sh2, sc2, g2 = q['mod']
    g = {}
    dv, da, act, dm = _b_mlp(dh3, s['a'], g2, q['wts'], l)
    g['mlp_w1'] = _tn_matmul(s['v'], da, "dw1", col_major=True, into=stacked['mlp_w1'], layer=l)
    g['mlp_w2'] = _tn_matmul(act, dm, "dw2", into=stacked['mlp_w2'], layer=l)
    dh2, dsc2, dsh2, dnw2, dg2, dya, dyb, dyc, dyd, do, dbw = _b_out(
        dv, s['h2'], dh3, s['m'], q['nw2'], sc2, s['ya'], s['yb'], s['yc'], s['yd'], q['bw'], q['wts'], l, g1)
    g['w_out'] = _tn_matmul(s['cat'], do, "dwout", into=stacked['w_out'], layer=l)
    g['branch_norm_w'] = dbw[0]
    if early is not None:
        zero = early(g)[0, 0]
        q = dict(q, pool_scale=q['pool_scale'] + zero, conv_b=q['conv_b'] + zero, s5_d=q['s5_d'] + zero)
    dab, dpm, dps, dsw = _b_ab(s['proj'], dya, dyb, q['pool_mat'], q['pool_scale'], q['sconv_w'])
    g['pool_w'] = _pool_extract(dpm)
    g['pool_scale'] = dps[0]
    g['sconv_w'] = dsw
    dz, dxbc, ddt, dcw, dcb, ddtb, dal, ddk = _b_ssd(s['proj'], s['dtp'], s['ypre'], dyc, s['sprev'], q['conv_w'],
                                                     q['conv_b'], q['dt_bias'], q['a_log'], q['ssd_d'])
    g['ssd_conv_w'] = dcw
    g['ssd_conv_b'] = dcb[0]
    g['ssd_dt_bias'] = ddtb[0, :4]
    g['ssd_a_log'] = dal[0, :4]
    g['ssd_d'] = ddk[0, :4]
    tb = s['tables']
    ds5, dbmat, dcre, dcim, dlam, dd5, dgw, dgb = _b_s5(s['proj'], dyd, s['carries'], s['states'], s['bmat'], q['cre'], q['cim'],
                                                        tb[0], tb[1], q['s5_d'], q['glu_w'], q['glu_b'])
    g['s5_c_re'] = _cmat_extract(dcre)
    g['s5_c_im'] = -_cmat_extract(dcim)
    g['s5_d'] = dd5[0]
    g['s5_glu_w'] = dgw
    g['s5_glu_b'] = dgb[0]
    dbbr = _bmat_extract(dbmat[:, :S5_P]).reshape(16, 1024)
    dbbi = _bmat_extract(dbmat[:, S5_P:]).reshape(16, 1024)
    dar, dai, dls, dbr, dbi = _s5_prep_bwd(*q['s5_raw'], dlam[0].reshape(16, 64), dlam[1].reshape(16, 64), dbbr, dbbi)
    g['s5_a_re'], g['s5_a_im'], g['s5_log_step'] = dar, dai, dls[:, 0]
    g['s5_b_re'], g['s5_b_im'] = dbr, dbi
    dh, dsc1, dsh1, dnw1, dg1 = _b_in(dab, dz, dxbc, ds5, ddt, q['w_main'], q['w_dt'], s['h'], dh2, s['o'], q['nw1'], sc1)
    u = s['u']
    head = jnp.concatenate([_tn_matmul(dab, u, "dwin_ab"), _tn_matmul(dz, u, "dwin_z"), _tn_matmul(dxbc, u, "dwin_xbc"),
                            _tn_matmul(ddt, u, "dwin_dt")[:8]], axis=0)
    full = lax.dynamic_update_slice(jnp.zeros((2308, D), F32), head, (0, 0))
    g['w_in'] = lax.dynamic_update_slice(full, _tn_matmul(ds5, u, "dwin_s5"), (2052, 0))
    g['norm_mix_w'] = dnw1[0]
    g['norm_mlp_w'] = dnw2[0]
    dmod = jnp.concatenate([dsh1, dsc1, dg1, dsh2, dsc2, dg2], axis=1)
    return dh, g, dmod


def _local_step(x, tgt, p, mod, w_in_of, rest_of, early=None):
    h = x
    qs, saved = [], []
    for l in range(2):
        qs.append(_layer_params(p, l, mod[l], w_in_of(l), functools.partial(rest_of, l)))
        h, s = _layer_fwd(h, qs[l])
        saved.append(s)
    dh, loss, dfw = _b_final(h, tgt, p['final_norm_w'].reshape(1, D))
    grads = [None, None]
    dmods = [None, None]
    dh, grads[1], dmods[1] = _layer_bwd(dh, qs[1], saved[1], 1, {k: lax.empty(shp, F32) for k, shp in STACKED.items()})
    dh, grads[0], dmods[0] = _layer_bwd(dh, qs[0], saved[0], 0, grads[1], early)
    out = {k: jnp.stack([grads[0][k], grads[1][k]]) for k in grads[0] if k not in STACKED}
    if early is None:
        out.update({k: grads[0][k] for k in STACKED})
    out['final_norm_w'] = dfw[0]
    return loss, dh, out, jnp.concatenate(dmods, axis=0)


def _shard_of(a, axis, k):
    n = a.shape[axis] // 4
    return lax.dynamic_slice_in_dim(a, k * n, n, axis)


def kernel(x, c, norm_mix_w, norm_mlp_w, ada_w, ada_b, w_in, pool_w, pool_scale, sconv_w, ssd_conv_w, ssd_conv_b, ssd_dt_bias, ssd_a_log, ssd_d, s5_a_re, s5_a_im, s5_log_step, s5_b_re, s5_b_im, s5_c_re, s5_c_im, s5_d, s5_glu_w, s5_glu_b, branch_norm_w, w_out, mlp_w1, mlp_w2, final_norm_w, loss_target, m_norm_mix_w, m_norm_mlp_w, m_ada_w, m_ada_b, m_w_in, m_pool_w, m_pool_scale, m_sconv_w, m_ssd_conv_w, m_ssd_conv_b, m_ssd_dt_bias, m_ssd_a_log, m_ssd_d, m_s5_a_re, m_s5_a_im, m_s5_log_step, m_s5_b_re, m_s5_b_im, m_s5_c_re, m_s5_c_im, m_s5_d, m_s5_glu_w, m_s5_glu_b, m_branch_norm_w, m_w_out, m_mlp_w1, m_mlp_w2, m_final_norm_w, v_norm_mix_w, v_norm_mlp_w, v_ada_w, v_ada_b, v_w_in, v_pool_w, v_pool_scale, v_sconv_w, v_ssd_conv_w, v_ssd_conv_b, v_ssd_dt_bias, v_ssd_a_log, v_ssd_d, v_s5_a_re, v_s5_a_im, v_s5_log_step, v_s5_b_re, v_s5_b_im, v_s5_c_re, v_s5_c_im, v_s5_d, v_s5_glu_w, v_s5_glu_b, v_branch_norm_w, v_w_out, v_mlp_w1, v_mlp_w2, v_final_norm_w):
    loc = locals()
    w = {n: loc[n] for n in WEIGHTS}
    mom = {n: loc['m_' + n] for n in WEIGHTS}
    var = {n: loc['v_' + n] for n in WEIGHTS}
    ix, iy, ic = lax.axis_index("x"), lax.axis_index("y"), lax.axis_index("c")
    chip = 2 * ix + iy
    dev = 4 * ix + 2 * iy + ic

    mine_of = lambda a: lax.dynamic_index_in_dim(a.astype(BF16), ic, axis=0, keepdims=False)
    pad_in = lambda a: jnp.pad(a.T, ((0, WIN_ROWS - 577), (0, 0)))
    shard = jnp.concatenate([mine_of(w['mlp_w1']), mine_of(w['mlp_w2']), mine_of(w['w_out']), pad_in(mine_of(w['w_in']))], axis=0)

    (c_all,) = _exchange([c], EVERYONE, False, "ag_cond", stage=True)
    c_all = c_all.reshape(8, D)
    small_sh = _exchange([w[n] for n in SMALL_SHARDED], CHIPS, False, "ag_small")
    (w_in0,) = _exchange([pad_in(w['w_in'][0].astype(BF16))], CHIPS, False, "ag_win0")
    p = {n: w[n] for n in WEIGHTS if n not in BIG}
    for n, g in zip(SMALL_SHARDED, small_sh):
        ax = SMALL_SHARDED[n]
        p[n] = jnp.concatenate([g[k] for k in range(4)], axis=ax)

    def w_in_full(sh):
        return sh[:, :577].reshape(4 * 577, D)

    big = {}

    def fetch(after):
        if not big:
            (mine,), (got,) = _split_wait(sems, shard_thru, land, after, False, "ag_big_wait", per_core=True)
            got = lax.dynamic_update_slice(got, mine[None, None], (ic, chip, 0, 0))
            (both,) = _pair_swap([got.reshape(2, -1, D)], False, "swap_big", fill=True)
            big['both'] = both.reshape(got.shape)
        return big['both']

    def w_in_of(l):
        return w_in_full(w_in0) if l == 0 else w_in_full(fetch(None)[1, :, ROW_WIN:])

    def rest_of(l, after):
        return fetch(after)

    ada_b_sh = _shard_of(w['ada_b'], 1, chip).reshape(2, 1, 6 * D // 4)
    mod_sh = _ada_fwd(c_all, w['ada_w'], ada_b_sh)
    (mod_all,) = _exchange([mod_sh], CHIPS, False, "ag_mod", stage=True)
    mine = lax.dynamic_index_in_dim(mod_all, dev, axis=2, keepdims=False)
    sems, shard_thru, land, token = _split_start([shard], [mod_all, w_in0] + small_sh, False, "ag_big_start", per_core=True)
    mod = jnp.transpose(mine, (1, 0, 2)).reshape(2, 6, D) + token[0, 0]

    layer = ic.astype(jnp.int32).reshape(1)
    flight = {}

    def early(g0):
        gws = [g0['w_out'].reshape(2, 4, 256, D), g0['mlp_w1'], g0['mlp_w2'].reshape(2, 4, 1024, D)]
        got = _pair_swap([a.reshape(2, -1, D) for a in gws], True, "swap_grad", narrow=True)
        pair = [_pair_sum(a, b.reshape(a.shape[1:]), layer, "pair_sum%d" % (k + 1), BF16) for k, (a, b) in enumerate(zip(gws, got))]
        flight['sems'], flight['srcs'], flight['lands'], token = _split_start(pair, [], True, "rs_start")
        return token

    loss, grad_x, g, dmod = _local_step(x[0], loss_target[0], p, mod, w_in_of, rest_of, early)

    (dmod_all,) = _exchange([dmod], EVERYONE, False, "ag_dmod", stage=True)
    dmod_all = jnp.transpose(dmod_all, (1, 0, 2))

    gw_in = jnp.pad(g['w_in'].reshape(2, 4, 577, D), ((0, 0), (0, 0), (0, WIN_ROWS - 577), (0, 0)))
    (got_in,) = _pair_swap([gw_in.reshape(2, -1, D)], True, "swap_grad_in", narrow=True)
    pair_in = _pair_sum(gw_in, got_in.reshape(gw_in.shape[1:]), layer, "pair_sum0", BF16)
    in_sems, in_srcs, in_lands, in_token = _split_start([pair_in], [dmod_all], True, "rs_in_start")

    def chip_sum(land, mine, name):
        own = lax.dynamic_index_in_dim(mine, chip, axis=0, keepdims=True)
        return _sum_lead(lax.dynamic_update_slice(land, own, (chip, 0, 0)), name, F32)

    sent, lands = _split_wait(flight['sems'], flight['srcs'], flight['lands'], [grad_x, in_token], True, "rs_wait")
    quad = [chip_sum(land, mine, "rs_chip_sum%d" % (k + 1)) for k, (land, mine) in enumerate(zip(lands, sent))]
    g_ada_w, g_ada_b = _ada_bwd(c_all, _shard_of(dmod_all, 2, chip), dmod_all)
    adam_ada_w = _adamw(w['ada_w'], g_ada_w, mom['ada_w'], var['ada_w'], "adamw_ada_w")
    (sent_in,), (land_in,) = _split_wait(in_sems, in_srcs, in_lands, quad + [adam_ada_w[0]], True, "rs_in_wait")
    quad = [chip_sum(land_in, sent_in, "rs_chip_sum0")] + quad
    halves = [lax.dynamic_update_slice(lax.empty((2,) + a.shape, F32), a[None], (ic, 0, 0)) for a in quad]
    both = _pair_swap(halves, False, "swap_red", fill=True)
    both[0] = jnp.transpose(both[0][:, :577], (0, 2, 1))
    red = dict(zip(('w_in', 'w_out', 'mlp_w1', 'mlp_w2'), both))
    red['ada_w'] = g_ada_w

    small_names = [n for n in WEIGHTS if n not in BIG and n != 'ada_b']
    pair_parts = _exchange([g[n] for n in small_names] + [loss], SIBLING, False, "ag_smallpair", stage=True)
    narrow = ('pool_w', 's5_b_re', 's5_b_im', 's5_c_re', 's5_c_im', 's5_glu_w')
    pair_dtypes = [BF16 if n in narrow else F32 for n in small_names] + [F32]
    chip_parts = _exchange(_sum_many(pair_parts, "smallpair_sum", pair_dtypes), CHIPS, False, "ag_smallgrad", stage=True)
    summed = _sum_many(chip_parts, "smallgrad_sum")
    for n, a in zip(small_names, summed[:-1]):
        a = a.reshape(w[n].shape) if n in ('s5_b_re', 's5_b_im') else a
        red[n] = _shard_of(a, SMALL_SHARDED[n], chip) if n in SMALL_SHARDED else a
    red['ada_b'] = g_ada_b
    loss_out = summed[-1].reshape(())

    delta, new_m, new_v = {}, {}, {}
    delta['ada_w'], new_m['ada_w'], new_v['ada_w'] = adam_ada_w
    for n in BIG[1:]:
        delta[n], new_m[n], new_v[n] = _adamw(w[n], red[n], mom[n], var[n], "adamw_" + n)
    rest = [n for n in WEIGHTS if n not in BIG]
    lanes = lambda n, a: a.reshape(2, 16, 1024) if n in ('s5_b_re', 's5_b_im') else a
    outs = _adamw_many(*[[lanes(n, src[n]) for n in rest] for src in (w, red, mom, var)], "adamw_small")
    for k, n in enumerate(rest):
        delta[n], new_m[n], new_v[n] = (outs[3 * k + j].reshape(w[n].shape) for j in range(3))

    return (loss_out, grad_x[None], *[red[n] for n in WEIGHTS], *[delta[n] for n in WEIGHTS],
            *[new_m[n] for n in WEIGHTS], *[new_v[n] for n in WEIGHTS])
```
